```python
import jax, jax.numpy as jnp
from jax import lax
import numpy as np

D_MODEL = 1024
BATCH = 16
SEQ = 2048
DEPTH = 1

P_DIM = 256
LRU_WIDTH = 1024
LRU_HEADS = 8
LRU_HEAD_DIM = LRU_WIDTH // LRU_HEADS
CONV_WIDTH = 4
LRU_C = 8.0
POOL_WIDTH = D_MODEL // 2
POOL_WINDOWS = (2, 4, 8, 16)
POOL_GROUPS = len(POOL_WINDOWS)
POOL_GROUP_DIM = POOL_WIDTH // POOL_GROUPS
MAX_WIN = max(POOL_WINDOWS)
IN_COLS = 2 * LRU_WIDTH + 2 * POOL_WIDTH + 2 * D_MODEL
EPS = 1e-6

kernel_name = "hybrid_rglru_pool_gated_merge"


def rmsnorm(x, g):
    x32 = x.astype(jnp.float32)
    ms = jnp.mean(x32 * x32, axis=-1, keepdims=True)
    return (x32 * lax.rsqrt(ms + EPS)).astype(x.dtype) * g


def causal_depthwise_conv(x, w, b):
    s = x.shape[1]
    xp = jnp.pad(x, ((0, 0), (CONV_WIDTH - 1, 0), (0, 0)))
    y = b
    for k in range(CONV_WIDTH):
        y = y + xp[:, k:k + s, :] * w[k]
    return y


def block_diag_linear(x, w, b):
    bsz, s, _ = x.shape
    h, dh, _ = w.shape
    xh = x.reshape(bsz, s, h, dh)
    y = jnp.einsum('bshd,hde->bshe', xh, w) + b
    return y.reshape(bsz, s, h * dh)


def rg_lru(x, w_a, b_a, w_x, b_x, lam):
    x32 = x.astype(jnp.float32)
    r = jax.nn.sigmoid(block_diag_linear(x, w_a, b_a).astype(jnp.float32))
    i = jax.nn.sigmoid(block_diag_linear(x, w_x, b_x).astype(jnp.float32))
    log_a = -LRU_C * r * jax.nn.softplus(-lam.astype(jnp.float32))
    a = jnp.exp(log_a)
    mult = jnp.sqrt(-jnp.expm1(2.0 * log_a))
    u = mult * (i * x32)

    def combine(c1, c2):
        a1, b1 = c1
        a2, b2 = c2
        return a2 * a1, a2 * b1 + b2

    _, h = lax.associative_scan(combine, (a, u), axis=1)
    return h.astype(x.dtype)


def multiscale_pool(x, w_pool, scale):
    bsz, s, _ = x.shape
    x32 = x.astype(jnp.float32)
    c = jnp.cumsum(x32, axis=1)
    c_pad = jnp.pad(c, ((0, 0), (MAX_WIN, 0), (0, 0)))
    pos = jnp.arange(s)
    outs = []
    for g, k in enumerate(POOL_WINDOWS):
        cg = c_pad[..., g * POOL_GROUP_DIM:(g + 1) * POOL_GROUP_DIM]
        win_sum = cg[:, MAX_WIN:, :] - cg[:, MAX_WIN - k:MAX_WIN - k + s, :]
        count = jnp.minimum(pos + 1, k).astype(jnp.float32)[None, :, None]
        outs.append(win_sum / count)
    pooled = jnp.concatenate(outs, axis=-1)
    diff = (pooled - x32).astype(x.dtype).reshape(bsz, s, POOL_GROUPS, POOL_GROUP_DIM)
    y = jnp.einsum('bsgd,gde->bsge', diff, w_pool).reshape(bsz, s, POOL_WIDTH)
    return y * scale


def _fwd_setup_inputs(seed: int = 0) -> dict:
    key = jax.random.key(seed)
    ks = jax.random.split(key, 24)
    f32 = jnp.float32

    def nrm(k, shape, fan_in):
        return jax.random.normal(k, shape, f32) * (fan_in ** -0.5)

    x = jax.random.normal(ks[0], (BATCH, SEQ, D_MODEL), f32)
    p = jax.random.normal(ks[1], (DEPTH, BATCH, SEQ, P_DIM), f32)
    norm_g = 1.0 + 0.05 * jax.random.normal(ks[2], (DEPTH, D_MODEL), f32)
    w_in = nrm(ks[3], (DEPTH, D_MODEL, IN_COLS), D_MODEL)
    conv_w = nrm(ks[4], (DEPTH, CONV_WIDTH, LRU_WIDTH), CONV_WIDTH)
    conv_b = 0.02 * jax.random.normal(ks[5], (DEPTH, LRU_WIDTH), f32)
    lru_w_a = nrm(ks[6], (DEPTH, LRU_HEADS, LRU_HEAD_DIM, LRU_HEAD_DIM), LRU_HEAD_DIM)
    lru_b_a = 0.02 * jax.random.normal(ks[7], (DEPTH, LRU_HEADS, LRU_HEAD_DIM), f32)
    lru_w_x = nrm(ks[8], (DEPTH, LRU_HEADS, LRU_HEAD_DIM, LRU_HEAD_DIM), LRU_HEAD_DIM)
    lru_b_x = 0.02 * jax.random.normal(ks[9], (DEPTH, LRU_HEADS, LRU_HEAD_DIM), f32)
    u = jax.random.uniform(ks[10], (DEPTH, LRU_WIDTH), f32, 0.9, 0.999)
    sa = u ** (1.0 / LRU_C)
    lru_lambda = jnp.log(sa) - jnp.log1p(-sa)
    pool_w = nrm(ks[11], (DEPTH, POOL_GROUPS, POOL_GROUP_DIM, POOL_GROUP_DIM), POOL_GROUP_DIM)
    pool_scale = 1.0 + 0.1 * jax.random.normal(ks[12], (DEPTH, POOL_WIDTH), f32)
    w_proj_lru = nrm(ks[13], (DEPTH, LRU_WIDTH, D_MODEL), LRU_WIDTH)
    w_proj_pool = nrm(ks[14], (DEPTH, POOL_WIDTH, D_MODEL), POOL_WIDTH)
    w_out = nrm(ks[15], (DEPTH, D_MODEL, D_MODEL), D_MODEL)
    ple_norm_g = 1.0 + 0.05 * jax.random.normal(ks[16], (DEPTH, D_MODEL), f32)
    w_ple_gate = nrm(ks[17], (DEPTH, D_MODEL, D_MODEL), D_MODEL)
    w_ple_proj = nrm(ks[18], (DEPTH, P_DIM, D_MODEL), P_DIM)
    final_g = 1.0 + 0.05 * jax.random.normal(ks[19], (D_MODEL,), f32)
    return {
        "x": x, "p": p, "norm_g": norm_g, "w_in": w_in,
        "conv_w": conv_w, "conv_b": conv_b,
        "lru_w_a": lru_w_a, "lru_b_a": lru_b_a, "lru_w_x": lru_w_x, "lru_b_x": lru_b_x,
        "lru_lambda": lru_lambda, "pool_w": pool_w, "pool_scale": pool_scale,
        "w_proj_lru": w_proj_lru, "w_proj_pool": w_proj_pool, "w_out": w_out,
        "ple_norm_g": ple_norm_g, "w_ple_gate": w_ple_gate, "w_ple_proj": w_ple_proj,
        "final_g": final_g,
    }


def _fwd_reference(x, p, norm_g, w_in, conv_w, conv_b, lru_w_a, lru_b_a, lru_w_x, lru_b_x,
              lru_lambda, pool_w, pool_scale, w_proj_lru, w_proj_pool, w_out,
              ple_norm_g, w_ple_gate, w_ple_proj, final_g):
    split_points = np.cumsum([LRU_WIDTH, LRU_WIDTH, POOL_WIDTH, POOL_WIDTH, D_MODEL]).tolist()
    for i in range(DEPTH):
        h = rmsnorm(x, norm_g[i])
        z = h @ w_in[i]
        xa, ga, xb, gb, ma, mb = jnp.split(z, split_points, axis=-1)
        xa = causal_depthwise_conv(xa, conv_w[i], conv_b[i])
        ya = rg_lru(xa, lru_w_a[i], lru_b_a[i], lru_w_x[i], lru_b_x[i], lru_lambda[i]) * jax.nn.silu(ga)
        yb = multiscale_pool(xb, pool_w[i], pool_scale[i]) * jax.nn.silu(gb)
        merged = jax.nn.sigmoid(ma) * (ya @ w_proj_lru[i]) + jax.nn.sigmoid(mb) * (yb @ w_proj_pool[i])
        x = x + merged @ w_out[i]
        gate = jax.nn.sigmoid(rmsnorm(x, ple_norm_g[i]) @ w_ple_gate[i])
        x = x + gate * (p[i] @ w_ple_proj[i])
    return rmsnorm(x, final_g)


import jax as _jax
import jax.numpy as _jnp

TWIN_FORMAT = 'train_step'
FWD_PARAMS = ['x', 'p', 'norm_g', 'w_in', 'conv_w', 'conv_b', 'lru_w_a', 'lru_b_a', 'lru_w_x', 'lru_b_x', 'lru_lambda', 'pool_w', 'pool_scale', 'w_proj_lru', 'w_proj_pool', 'w_out', 'ple_norm_g', 'w_ple_gate', 'w_ple_proj', 'final_g']
TWIN_WEIGHTS = ['norm_g', 'w_in', 'conv_w', 'conv_b', 'lru_w_a', 'lru_b_a', 'lru_w_x', 'lru_b_x', 'lru_lambda', 'pool_w', 'pool_scale', 'w_proj_lru', 'w_proj_pool', 'w_out', 'ple_norm_g', 'w_ple_gate', 'w_ple_proj', 'final_g']
TWIN_DIFF_INPUT = 'x'
TWIN_INPUTS = ['x', 'p', 'norm_g', 'w_in', 'conv_w', 'conv_b', 'lru_w_a', 'lru_b_a', 'lru_w_x', 'lru_b_x', 'lru_lambda', 'pool_w', 'pool_scale', 'w_proj_lru', 'w_proj_pool', 'w_out', 'ple_norm_g', 'w_ple_gate', 'w_ple_proj', 'final_g', 'loss_target', 'm_norm_g', 'm_w_in', 'm_conv_w', 'm_conv_b', 'm_lru_w_a', 'm_lru_b_a', 'm_lru_w_x', 'm_lru_b_x', 'm_lru_lambda', 'm_pool_w', 'm_pool_scale', 'm_w_proj_lru', 'm_w_proj_pool', 'm_w_out', 'm_ple_norm_g', 'm_w_ple_gate', 'm_w_ple_proj', 'm_final_g', 'v_norm_g', 'v_w_in', 'v_conv_w', 'v_conv_b', 'v_lru_w_a', 'v_lru_b_a', 'v_lru_w_x', 'v_lru_b_x', 'v_lru_lambda', 'v_pool_w', 'v_pool_scale', 'v_w_proj_lru', 'v_w_proj_pool', 'v_w_out', 'v_ple_norm_g', 'v_w_ple_gate', 'v_w_ple_proj', 'v_final_g']
TWIN_OUTPUTS = ['loss', 'grad_x', 'grad_norm_g', 'grad_w_in', 'grad_conv_w', 'grad_conv_b', 'grad_lru_w_a', 'grad_lru_b_a', 'grad_lru_w_x', 'grad_lru_b_x', 'grad_lru_lambda', 'grad_pool_w', 'grad_pool_scale', 'grad_w_proj_lru', 'grad_w_proj_pool', 'grad_w_out', 'grad_ple_norm_g', 'grad_w_ple_gate', 'grad_w_ple_proj', 'grad_final_g', 'delta_norm_g', 'delta_w_in', 'delta_conv_w', 'delta_conv_b', 'delta_lru_w_a', 'delta_lru_b_a', 'delta_lru_w_x', 'delta_lru_b_x', 'delta_lru_lambda', 'delta_pool_w', 'delta_pool_scale', 'delta_w_proj_lru', 'delta_w_proj_pool', 'delta_w_out', 'delta_ple_norm_g', 'delta_w_ple_gate', 'delta_w_ple_proj', 'delta_final_g', 'new_m_norm_g', 'new_m_w_in', 'new_m_conv_w', 'new_m_conv_b', 'new_m_lru_w_a', 'new_m_lru_b_a', 'new_m_lru_w_x', 'new_m_lru_b_x', 'new_m_lru_lambda', 'new_m_pool_w', 'new_m_pool_scale', 'new_m_w_proj_lru', 'new_m_w_proj_pool', 'new_m_w_out', 'new_m_ple_norm_g', 'new_m_w_ple_gate', 'new_m_w_ple_proj', 'new_m_final_g', 'new_v_norm_g', 'new_v_w_in', 'new_v_conv_w', 'new_v_conv_b', 'new_v_lru_w_a', 'new_v_lru_b_a', 'new_v_lru_w_x', 'new_v_lru_b_x', 'new_v_lru_lambda', 'new_v_pool_w', 'new_v_pool_scale', 'new_v_w_proj_lru', 'new_v_w_proj_pool', 'new_v_w_out', 'new_v_ple_norm_g', 'new_v_w_ple_gate', 'new_v_w_ple_proj', 'new_v_final_g']
TWIN_LEAF_KINDS = {'loss': 'loss', 'grad_x': 'grad_x', 'grad_norm_g': 'grad_w', 'grad_w_in': 'grad_w', 'grad_conv_w': 'grad_w', 'grad_conv_b': 'grad_w', 'grad_lru_w_a': 'grad_w', 'grad_lru_b_a': 'grad_w', 'grad_lru_w_x': 'grad_w', 'grad_lru_b_x': 'grad_w', 'grad_lru_lambda': 'grad_w', 'grad_pool_w': 'grad_w', 'grad_pool_scale': 'grad_w', 'grad_w_proj_lru': 'grad_w', 'grad_w_proj_pool': 'grad_w', 'grad_w_out': 'grad_w', 'grad_ple_norm_g': 'grad_w', 'grad_w_ple_gate': 'grad_w', 'grad_w_ple_proj': 'grad_w', 'grad_final_g': 'grad_w', 'delta_norm_g': 'delta_w', 'delta_w_in': 'delta_w', 'delta_conv_w': 'delta_w', 'delta_conv_b': 'delta_w', 'delta_lru_w_a': 'delta_w', 'delta_lru_b_a': 'delta_w', 'delta_lru_w_x': 'delta_w', 'delta_lru_b_x': 'delta_w', 'delta_lru_lambda': 'delta_w', 'delta_pool_w': 'delta_w', 'delta_pool_scale': 'delta_w', 'delta_w_proj_lru': 'delta_w', 'delta_w_proj_pool': 'delta_w', 'delta_w_out': 'delta_w', 'delta_ple_norm_g': 'delta_w', 'delta_w_ple_gate': 'delta_w', 'delta_w_ple_proj': 'delta_w', 'delta_final_g': 'delta_w', 'new_m_norm_g': 'new_m', 'new_m_w_in': 'new_m', 'new_m_conv_w': 'new_m', 'new_m_conv_b': 'new_m', 'new_m_lru_w_a': 'new_m', 'new_m_lru_b_a': 'new_m', 'new_m_lru_w_x': 'new_m', 'new_m_lru_b_x': 'new_m', 'new_m_lru_lambda': 'new_m', 'new_m_pool_w': 'new_m', 'new_m_pool_scale': 'new_m', 'new_m_w_proj_lru': 'new_m', 'new_m_w_proj_pool': 'new_m', 'new_m_w_out': 'new_m', 'new_m_ple_norm_g': 'new_m', 'new_m_w_ple_gate': 'new_m', 'new_m_w_ple_proj': 'new_m', 'new_m_final_g': 'new_m', 'new_v_norm_g': 'new_v', 'new_v_w_in': 'new_v', 'new_v_conv_w': 'new_v', 'new_v_conv_b': 'new_v', 'new_v_lru_w_a': 'new_v', 'new_v_lru_b_a': 'new_v', 'new_v_lru_w_x': 'new_v', 'new_v_lru_b_x': 'new_v', 'new_v_lru_lambda': 'new_v', 'new_v_pool_w': 'new_v', 'new_v_pool_scale': 'new_v', 'new_v_w_proj_lru': 'new_v', 'new_v_w_proj_pool': 'new_v', 'new_v_w_out': 'new_v', 'new_v_ple_norm_g': 'new_v', 'new_v_w_ple_gate': 'new_v', 'new_v_w_ple_proj': 'new_v', 'new_v_final_g': 'new_v'}


def _forward(args):
    return _fwd_reference(*[args[k] for k in FWD_PARAMS])


def _output_shape():
    out = _jax.eval_shape(lambda: _forward(_fwd_setup_inputs(0)))
    return out.shape, out.dtype

N_MICROBATCH = 1
ADAM_LR = 0.001
ADAM_B1 = 0.9
ADAM_B2 = 0.999
ADAM_EPS = 1e-08
ADAM_WD = 0.01
ADAM_STEP = 10
PER_EXAMPLE_BATCH_AXIS = {'x': 0, 'p': 1, 'loss_target': 0}
SHARED_INPUTS = []
_WEIGHT_DTYPES = {'norm_g': _jnp.float32, 'w_in': _jnp.float32, 'conv_w': _jnp.float32, 'conv_b': _jnp.float32, 'lru_w_a': _jnp.float32, 'lru_b_a': _jnp.float32, 'lru_w_x': _jnp.float32, 'lru_b_x': _jnp.float32, 'lru_lambda': _jnp.float32, 'pool_w': _jnp.float32, 'pool_scale': _jnp.float32, 'w_proj_lru': _jnp.float32, 'w_proj_pool': _jnp.float32, 'w_out': _jnp.float32, 'ple_norm_g': _jnp.float32, 'w_ple_gate': _jnp.float32, 'w_ple_proj': _jnp.float32, 'final_g': _jnp.float32}
MOMENT_SCALE = {'norm_g': 8.464210e-02, 'w_in': 3.648019e-02, 'conv_w': 3.634009e-02, 'conv_b': 4.204682e-01, 'lru_w_a': 1.220354e-02, 'lru_b_a': 8.965704e-03, 'lru_w_x': 2.225255e-02, 'lru_b_x': 1.402539e-02, 'lru_lambda': 1.829057e-02, 'pool_w': 6.203771e-02, 'pool_scale': 6.441726e-02, 'w_proj_lru': 3.528087e-02, 'w_proj_pool': 4.408698e-02, 'w_out': 5.616171e-02, 'ple_norm_g': 3.293819e-02, 'w_ple_gate': 3.188961e-02, 'w_ple_proj': 8.295368e-02, 'final_g': 3.199278e+01}


def _to_microbatches(a, axis):
    t = _jnp.moveaxis(a, axis, 0)
    t = t.reshape((N_MICROBATCH, t.shape[0] // N_MICROBATCH) + t.shape[1:])
    return _jnp.moveaxis(t, 1, axis + 1)


def setup_inputs(seed: int = 0) -> dict:
    inp = _fwd_setup_inputs(seed)
    key = _jax.random.fold_in(_jax.random.key(seed), 7919)
    shape, _ = _output_shape()
    out = dict(inp)
    out["loss_target"] = _jax.random.normal(_jax.random.fold_in(key, 0), shape, _jnp.float32)
    for i, name in enumerate(TWIN_WEIGHTS):
        w = inp[name].astype(_jnp.float32)
        if MOMENT_SCALE is None:
            s = _jnp.sqrt(_jnp.mean(_jnp.square(w)) + 1e-30)
        else:
            s = MOMENT_SCALE[name]
        km, kv = _jax.random.split(_jax.random.fold_in(key, i + 1))
        out[name] = w
        out["m_" + name] = s * _jax.random.normal(km, w.shape, _jnp.float32)
        out["v_" + name] = (s * s) * _jax.random.uniform(kv, w.shape, _jnp.float32, 0.5, 1.5)
    if N_MICROBATCH > 1:
        for name, axis in PER_EXAMPLE_BATCH_AXIS.items():
            out[name] = _to_microbatches(out[name], axis)
    return {'x': out['x'], 'p': out['p'], 'norm_g': out['norm_g'], 'w_in': out['w_in'], 'conv_w': out['conv_w'], 'conv_b': out['conv_b'], 'lru_w_a': out['lru_w_a'], 'lru_b_a': out['lru_b_a'], 'lru_w_x': out['lru_w_x'], 'lru_b_x': out['lru_b_x'], 'lru_lambda': out['lru_lambda'], 'pool_w': out['pool_w'], 'pool_scale': out['pool_scale'], 'w_proj_lru': out['w_proj_lru'], 'w_proj_pool': out['w_proj_pool'], 'w_out': out['w_out'], 'ple_norm_g': out['ple_norm_g'], 'w_ple_gate': out['w_ple_gate'], 'w_ple_proj': out['w_ple_proj'], 'final_g': out['final_g'], 'loss_target': out['loss_target'], 'm_norm_g': out['m_norm_g'], 'm_w_in': out['m_w_in'], 'm_conv_w': out['m_conv_w'], 'm_conv_b': out['m_conv_b'], 'm_lru_w_a': out['m_lru_w_a'], 'm_lru_b_a': out['m_lru_b_a'], 'm_lru_w_x': out['m_lru_w_x'], 'm_lru_b_x': out['m_lru_b_x'], 'm_lru_lambda': out['m_lru_lambda'], 'm_pool_w': out['m_pool_w'], 'm_pool_scale': out['m_pool_scale'], 'm_w_proj_lru': out['m_w_proj_lru'], 'm_w_proj_pool': out['m_w_proj_pool'], 'm_w_out': out['m_w_out'], 'm_ple_norm_g': out['m_ple_norm_g'], 'm_w_ple_gate': out['m_w_ple_gate'], 'm_w_ple_proj': out['m_w_ple_proj'], 'm_final_g': out['m_final_g'], 'v_norm_g': out['v_norm_g'], 'v_w_in': out['v_w_in'], 'v_conv_w': out['v_conv_w'], 'v_conv_b': out['v_conv_b'], 'v_lru_w_a': out['v_lru_w_a'], 'v_lru_b_a': out['v_lru_b_a'], 'v_lru_w_x': out['v_lru_w_x'], 'v_lru_b_x': out['v_lru_b_x'], 'v_lru_lambda': out['v_lru_lambda'], 'v_pool_w': out['v_pool_w'], 'v_pool_scale': out['v_pool_scale'], 'v_w_proj_lru': out['v_w_proj_lru'], 'v_w_proj_pool': out['v_w_proj_pool'], 'v_w_out': out['v_w_out'], 'v_ple_norm_g': out['v_ple_norm_g'], 'v_w_ple_gate': out['v_w_ple_gate'], 'v_w_ple_proj': out['v_w_ple_proj'], 'v_final_g': out['v_final_g']}


def _loss(weights, diff, rest, loss_target):
    with _jax.named_scope("forward"):
        args = {**rest, TWIN_DIFF_INPUT: diff, **{k: w.astype(_WEIGHT_DTYPES[k]) for k, w in weights.items()}}
        y = _forward(args)
    with _jax.named_scope("loss_head"):
        err = _jnp.square(y.astype(_jnp.float32) - loss_target)
        return 0.5 * _jnp.sum(_jnp.mean(err, axis=-1)) if err.ndim else 0.5 * err


def _adamw(w, g, m, v):
    m = ADAM_B1 * m + (1.0 - ADAM_B1) * g
    v = ADAM_B2 * v + (1.0 - ADAM_B2) * _jnp.square(g)
    m_hat = m / (1.0 - ADAM_B1 ** ADAM_STEP)
    v_hat = v / (1.0 - ADAM_B2 ** ADAM_STEP)
    delta = -ADAM_LR * (m_hat / (_jnp.sqrt(v_hat) + ADAM_EPS) + ADAM_WD * w)
    return delta, m, v


def reference(x, p, norm_g, w_in, conv_w, conv_b, lru_w_a, lru_b_a, lru_w_x, lru_b_x, lru_lambda, pool_w, pool_scale, w_proj_lru, w_proj_pool, w_out, ple_norm_g, w_ple_gate, w_ple_proj, final_g, loss_target, m_norm_g, m_w_in, m_conv_w, m_conv_b, m_lru_w_a, m_lru_b_a, m_lru_w_x, m_lru_b_x, m_lru_lambda, m_pool_w, m_pool_scale, m_w_proj_lru, m_w_proj_pool, m_w_out, m_ple_norm_g, m_w_ple_gate, m_w_ple_proj, m_final_g, v_norm_g, v_w_in, v_conv_w, v_conv_b, v_lru_w_a, v_lru_b_a, v_lru_w_x, v_lru_b_x, v_lru_lambda, v_pool_w, v_pool_scale, v_w_proj_lru, v_w_proj_pool, v_w_out, v_ple_norm_g, v_w_ple_gate, v_w_ple_proj, v_final_g):
    given = dict(x=x, p=p, norm_g=norm_g, w_in=w_in, conv_w=conv_w, conv_b=conv_b, lru_w_a=lru_w_a, lru_b_a=lru_b_a, lru_w_x=lru_w_x, lru_b_x=lru_b_x, lru_lambda=lru_lambda, pool_w=pool_w, pool_scale=pool_scale, w_proj_lru=w_proj_lru, w_proj_pool=w_proj_pool, w_out=w_out, ple_norm_g=ple_norm_g, w_ple_gate=w_ple_gate, w_ple_proj=w_ple_proj, final_g=final_g, loss_target=loss_target, m_norm_g=m_norm_g, m_w_in=m_w_in, m_conv_w=m_conv_w, m_conv_b=m_conv_b, m_lru_w_a=m_lru_w_a, m_lru_b_a=m_lru_b_a, m_lru_w_x=m_lru_w_x, m_lru_b_x=m_lru_b_x, m_lru_lambda=m_lru_lambda, m_pool_w=m_pool_w, m_pool_scale=m_pool_scale, m_w_proj_lru=m_w_proj_lru, m_w_proj_pool=m_w_proj_pool, m_w_out=m_w_out, m_ple_norm_g=m_ple_norm_g, m_w_ple_gate=m_w_ple_gate, m_w_ple_proj=m_w_ple_proj, m_final_g=m_final_g, v_norm_g=v_norm_g, v_w_in=v_w_in, v_conv_w=v_conv_w, v_conv_b=v_conv_b, v_lru_w_a=v_lru_w_a, v_lru_b_a=v_lru_b_a, v_lru_w_x=v_lru_w_x, v_lru_b_x=v_lru_b_x, v_lru_lambda=v_lru_lambda, v_pool_w=v_pool_w, v_pool_scale=v_pool_scale, v_w_proj_lru=v_w_proj_lru, v_w_proj_pool=v_w_proj_pool, v_w_out=v_w_out, v_ple_norm_g=v_ple_norm_g, v_w_ple_gate=v_w_ple_gate, v_w_ple_proj=v_w_ple_proj, v_final_g=v_final_g)
    weights = {n: given[n] for n in TWIN_WEIGHTS}
    shared = {n: given[n] for n in SHARED_INPUTS}
    per_example = {n: given[n] for n in ['x', 'p']}
    grad_fn = _jax.value_and_grad(_loss, argnums=(0, 1))

    def one_microbatch(ex, loss_target):
        ex = dict(ex)
        diff = ex.pop(TWIN_DIFF_INPUT)
        return grad_fn(weights, diff, {**shared, **ex}, loss_target)

    if N_MICROBATCH == 1:
        loss, (grad_w, grad_x) = one_microbatch(per_example, given["loss_target"])
    else:
        def body(carry, xs):
            loss_sum, grad_sum = carry
            l_k, (gw_k, gx_k) = one_microbatch(xs[0], xs[1])
            with _jax.named_scope("update"):
                return (loss_sum + l_k, _jax.tree.map(_jnp.add, grad_sum, gw_k)), gx_k

        init = (_jnp.zeros((), _jnp.float32), _jax.tree.map(_jnp.zeros_like, weights))
        (loss, grad_w), grad_x = _jax.lax.scan(body, init, (per_example, given["loss_target"]))
    with _jax.named_scope("update"):
        delta_w, new_m, new_v = {}, {}, {}
        for n in TWIN_WEIGHTS:
            delta_w[n], new_m[n], new_v[n] = _adamw(weights[n], grad_w[n], given["m_" + n], given["v_" + n])
    return (loss, grad_x, *[grad_w[n] for n in TWIN_WEIGHTS], *[delta_w[n] for n in TWIN_WEIGHTS],
            *[new_m[n] for n in TWIN_WEIGHTS], *[new_v[n] for n in TWIN_WEIGHTS])
```

```python
import functools

import jax
import jax.numpy as jnp
from jax import lax
from jax.experimental import pallas as pl
from jax.experimental.pallas import tpu as pltpu

F32 = jnp.float32
BF16 = jnp.bfloat16
MESH = pl.DeviceIdType.MESH
ALL_AXES = ("x", "y", "c")

D_MODEL = 1024
LRU_HEADS = 8
HEAD_DIM = 128
CONV_WIDTH = 4
LRU_C = 8.0
POOL_WIDTH = 512
POOL_WINDOWS = (2, 4, 8, 16)
POOL_GROUP_DIM = 128
IN_COLS = 5120
N_CHIPS = 4
EPS = 1e-6

ADAM_LR = 0.001
ADAM_B1 = 0.9
ADAM_B2 = 0.999
ADAM_EPS = 1e-08
ADAM_WD = 0.01
ADAM_STEP = 10

F32_SUBLANES = 8
CONV_HIST = 8
POOL_HIST = 16
VMEM_LIMIT_BYTES = 58 * 1024 * 1024
BAG_PART_ROWS = (8, 8, 8, 128, 8, 128, 8, 8, 64, 8, 8, 8)
BAG_ROWS = 448


def _dot(a, b):
    return jnp.dot(a, b, preferred_element_type=F32)


def _dot_nt(a, b):
    return lax.dot_general(a, b, (((1,), (1,)), ((), ())), preferred_element_type=F32)


def _dot_tn(a, b):
    return lax.dot_general(a, b, (((0,), (0,)), ((), ())), preferred_element_type=F32)


def _sigmoid(v):
    return jax.nn.sigmoid(v)


def _softplus(v):
    return jnp.maximum(v, 0.0) + jnp.log1p(jnp.exp(-jnp.abs(v)))


def _place():
    return lax.axis_index("x"), lax.axis_index("y"), lax.axis_index("c")


def _gather_shards(shards, name):
    n = len(shards)
    n_sem = 6

    def body(*refs):
        ins, outs = refs[:n], refs[n:2 * n]
        send_sems, recv_sems, local_sems = refs[2 * n:]
        x, y, c = _place()
        chips = [(1 - x, y), (x, 1 - y), (1 - x, 1 - y)]

        def region(k, cx, cy, half):
            (r, cols), axis = shards[k][0].shape, shards[k][1]
            j = 2 * cx + cy
            if axis == 0:
                if half is None:
                    return outs[k].at[pl.ds(j * r, r), :]
                return outs[k].at[pl.ds(j * r + half * (r // 2), r // 2), :]
            if half is None:
                return outs[k].at[:, pl.ds(j * cols, cols)]
            return outs[k].at[pl.ds(half * (r // 2), r // 2), pl.ds(j * cols, cols)]

        def remote(k, sem, block, to, src=None):
            dst = region(k, *block)
            return pltpu.make_async_remote_copy(
                src_ref=dst if src is None else src, dst_ref=dst,
                send_sem=send_sems.at[k * n_sem + sem], recv_sem=recv_sems.at[k * n_sem + sem],
                device_id=to, device_id_type=MESH)

        mine = [pltpu.make_async_copy(ins[k], region(k, x, y, None), local_sems.at[k]) for k in range(n)]
        for cp in mine:
            cp.start()
        sends = []
        for k in range(n):
            r = shards[k][0].shape[0]
            split = shards[k][2]
            src = ins[k].at[pl.ds(c * (r // 2), r // 2), :] if split else ins[k]
            for idx, chip in enumerate(chips):
                sends.append(remote(k, idx, (x, y, c if split else None), (*chip, c), src=src))
        for cp in sends:
            cp.start()
        for k in range(n):
            split = shards[k][2]
            for idx, chip in enumerate(chips):
                remote(k, idx, (*chip, c if split else None), (x, y, c)).wait_recv()
                if split:
                    fwd = remote(k, 3 + idx, (*chip, c), (x, y, 1 - c))
                    fwd.start()
                    sends.append(fwd)
        for k in range(n):
            if shards[k][2]:
                for idx, chip in enumerate(chips):
                    remote(k, 3 + idx, (*chip, 1 - c), (x, y, c)).wait_recv()
        for cp in sends:
            cp.wait_send()
        for cp in mine:
            cp.wait()

    out_shape = []
    for arr, axis, _ in shards:
        r, cols = arr.shape
        full = (N_CHIPS * r, cols) if axis == 0 else (r, N_CHIPS * cols)
        out_shape.append(jax.ShapeDtypeStruct(full, arr.dtype))
    any_spec = pl.BlockSpec(memory_space=pl.ANY)
    return pl.pallas_call(
        body, name=name, out_shape=tuple(out_shape),
        in_specs=[any_spec] * n, out_specs=tuple([any_spec] * n),
        scratch_shapes=[pltpu.SemaphoreType.DMA((n * n_sem,)), pltpu.SemaphoreType.DMA((n * n_sem,)),
                        pltpu.SemaphoreType.DMA((n,))],
    )(*[s[0] for s in shards])


RS_ADD_ROWS = (64, 56, 32, 16, 8)


def _reduce_scatter(parts, name):
    n = len(parts)
    n_sem = 8

    def body(*refs):
        ins, outs = refs[:n], refs[n:2 * n]
        own = refs[2 * n:3 * n]
        sib = refs[3 * n:4 * n]
        got = refs[4 * n:5 * n]
        fin = refs[5 * n:6 * n]
        send_sems, recv_sems, local_sems = refs[6 * n:]
        x, y, c = _place()
        j_me = 2 * x + y
        chips = [(1 - x, y), (x, 1 - y), (1 - x, 1 - y)]

        def remote(a, sem, src, dst, to):
            return pltpu.make_async_remote_copy(
                src_ref=src, dst_ref=dst, send_sem=send_sems.at[a * n_sem + sem],
                recv_sem=recv_sems.at[a * n_sem + sem], device_id=to, device_id_type=MESH)

        def rows_loop(a, fn):
            r = parts[a].shape[1]
            step = max(s for s in RS_ADD_ROWS if r % s == 0)

            def it(i, carry):
                fn(pl.ds(pl.multiple_of(i * step, step), step))
                return carry

            lax.fori_loop(0, r // step, it, 0)

        loads, sends = [], []
        for a in range(n):
            for jj in range(N_CHIPS):
                cp = pltpu.make_async_copy(ins[a].at[2 * jj + c], own[a].at[jj], local_sems.at[a * 5 + jj])
                cp.start()
                loads.append(cp)
                sd = remote(a, jj, ins[a].at[2 * jj + (1 - c)], sib[a].at[jj], (x, y, 1 - c))
                sd.start()
                sends.append(sd)
        for a in range(n):
            for jj in range(N_CHIPS):
                loads[a * N_CHIPS + jj].wait()
                remote(a, jj, sib[a].at[jj], sib[a].at[jj], (x, y, c)).wait_recv()

                def add(sl, a=a, jj=jj):
                    own[a][jj, sl, :] = own[a][jj, sl, :] + sib[a][jj, sl, :]

                rows_loop(a, add)
        for a in range(n):
            for idx, chip in enumerate(chips):
                sd = remote(a, 4 + idx, own[a].at[2 * chip[0] + chip[1]], got[a].at[j_me], (*chip, c))
                sd.start()
                sends.append(sd)
        for a in range(n):
            def keep(sl, a=a):
                got[a][j_me, sl, :] = own[a][j_me, sl, :]

            rows_loop(a, keep)
        for a in range(n):
            for idx, chip in enumerate(chips):
                slot = got[a].at[2 * chip[0] + chip[1]]
                remote(a, 4 + idx, slot, slot, (x, y, c)).wait_recv()

            def total(sl, a=a):
                fin[a][sl, :] = ((got[a][0, sl, :] + got[a][1, sl, :]) + got[a][2, sl, :]) + got[a][3, sl, :]

            rows_loop(a, total)
        stores = []
        for a in range(n):
            st = pltpu.make_async_copy(fin[a], outs[a].at[c], local_sems.at[a * 5 + 4])
            st.start()
            stores.append(st)
            sd = remote(a, 7, fin[a], outs[a].at[c], (x, y, 1 - c))
            sd.start()
            sends.append(sd)
        for a in range(n):
            remote(a, 7, outs[a].at[1 - c], outs[a].at[1 - c], (x, y, c)).wait_recv()
        for cp in sends:
            cp.wait_send()
        for cp in stores:
            cp.wait()

    any_spec = pl.BlockSpec(memory_space=pl.ANY)
    scratch = []
    for lead in (N_CHIPS, N_CHIPS, N_CHIPS, None):
        for p in parts:
            shape = p.shape[1:] if lead is None else (lead,) + p.shape[1:]
            scratch.append(pltpu.VMEM(shape, F32))
    scratch += [pltpu.SemaphoreType.DMA((n * n_sem,)), pltpu.SemaphoreType.DMA((n * n_sem,)),
                pltpu.SemaphoreType.DMA((n * 5,))]
    return pl.pallas_call(
        body, name=name,
        out_shape=tuple(jax.ShapeDtypeStruct((2,) + p.shape[1:], F32) for p in parts),
        in_specs=[any_spec] * n, out_specs=tuple([any_spec] * n), scratch_shapes=scratch,
        compiler_params=pltpu.CompilerParams(vmem_limit_bytes=VMEM_LIMIT_BYTES),
    )(*parts)


def _rms(x):
    r = lax.rsqrt(jnp.mean(x * x, axis=-1, keepdims=True) + EPS)
    return x * r, r


def _rms_bwd(dxn, xn, r):
    return r * (dxn - xn * jnp.mean(dxn * xn, axis=-1, keepdims=True))


def _in_proj(x2d, norm_g, w_in, tb):
    t = x2d.shape[0]
    cols = IN_COLS // N_CHIPS

    def body(x_ref, g_ref, w_ref, z_ref):
        xn, _ = _rms(x_ref[...])
        z_ref[...] = _dot((xn * g_ref[...]).astype(BF16), w_ref[...])

    return pl.pallas_call(
        body, name="in_proj", out_shape=jax.ShapeDtypeStruct((t, IN_COLS), F32),
        grid=(N_CHIPS, t // tb),
        in_specs=[pl.BlockSpec((tb, D_MODEL), lambda j, i: (i, 0)),
                  pl.BlockSpec((1, D_MODEL), lambda j, i: (0, 0)),
                  pl.BlockSpec((D_MODEL, cols), lambda j, i: (0, j))],
        out_specs=pl.BlockSpec((tb, cols), lambda j, i: (i, j)),
        compiler_params=pltpu.CompilerParams(dimension_semantics=("arbitrary", "arbitrary"),
                                             vmem_limit_bytes=VMEM_LIMIT_BYTES),
    )(x2d, norm_g, w_in)


def _in_proj_bwd(dz, w_in, x2d, dx_res, norm_g, tb):
    t = x2d.shape[0]

    def body(dz_ref, w_ref, x_ref, dres_ref, g_ref, dx_ref, h_ref, dg_ref):
        @pl.when(pl.program_id(0) == 0)
        def _():
            dg_ref[...] = jnp.zeros_like(dg_ref)

        xn, r = _rms(x_ref[...])
        g = g_ref[...]
        h_ref[...] = (xn * g).astype(BF16)
        dh = _dot_nt(dz_ref[...], w_ref[...])
        dg_ref[...] += jnp.sum(dh * xn, axis=0, keepdims=True)
        dx_ref[...] = dres_ref[...] + _rms_bwd(dh * g, xn, r)

    row = lambda i: (i, 0)
    fixed = lambda i: (0, 0)
    return pl.pallas_call(
        body, name="in_proj_bwd",
        out_shape=(jax.ShapeDtypeStruct((t, D_MODEL), F32), jax.ShapeDtypeStruct((t, D_MODEL), BF16),
                   jax.ShapeDtypeStruct((1, D_MODEL), F32)),
        grid=(t // tb,),
        in_specs=[pl.BlockSpec((tb, IN_COLS), row),
                  pl.BlockSpec((D_MODEL, IN_COLS), fixed, pipeline_mode=pl.Buffered(1)),
                  pl.BlockSpec((tb, D_MODEL), row), pl.BlockSpec((tb, D_MODEL), row),
                  pl.BlockSpec((1, D_MODEL), fixed)],
        out_specs=(pl.BlockSpec((tb, D_MODEL), row), pl.BlockSpec((tb, D_MODEL), row),
                   pl.BlockSpec((1, D_MODEL), fixed)),
        compiler_params=pltpu.CompilerParams(dimension_semantics=("arbitrary",),
                                             vmem_limit_bytes=VMEM_LIMIT_BYTES),
    )(dz, w_in, x2d, dx_res, norm_g)


def _weight_grad(lhs, rhs, n_chunks, tb, name):
    t, k = lhs.shape
    nc = rhs.shape[1] // n_chunks

    def body(l_ref, r_ref, o_ref):
        @pl.when(pl.program_id(1) == 0)
        def _():
            o_ref[...] = jnp.zeros_like(o_ref)

        o_ref[...] += _dot_tn(l_ref[...], r_ref[...])

    return pl.pallas_call(
        body, name=name, out_shape=jax.ShapeDtypeStruct((n_chunks, k, nc), F32),
        grid=(n_chunks, t // tb),
        in_specs=[pl.BlockSpec((tb, k), lambda j, i: (i, 0)), pl.BlockSpec((tb, nc), lambda j, i: (i, j))],
        out_specs=pl.BlockSpec((None, k, nc), lambda j, i: (j, 0, 0)),
        compiler_params=pltpu.CompilerParams(dimension_semantics=("arbitrary", "arbitrary"),
                                             vmem_limit_bytes=VMEM_LIMIT_BYTES),
    )(lhs, rhs)


def _adamw(w, g, m, v, rows, name):
    r, c = w.shape

    def body(w_ref, g_ref, m_ref, v_ref, d_ref, nm_ref, nv_ref):
        g_ = g_ref[...]
        m_ = ADAM_B1 * m_ref[...] + (1.0 - ADAM_B1) * g_
        v_ = ADAM_B2 * v_ref[...] + (1.0 - ADAM_B2) * jnp.square(g_)
        m_hat = m_ / (1.0 - ADAM_B1 ** ADAM_STEP)
        v_hat = v_ / (1.0 - ADAM_B2 ** ADAM_STEP)
        d_ref[...] = -ADAM_LR * (m_hat / (jnp.sqrt(v_hat) + ADAM_EPS) + ADAM_WD * w_ref[...])
        nm_ref[...] = m_
        nv_ref[...] = v_

    spec = pl.BlockSpec((rows, c), lambda i: (i, 0))
    return pl.pallas_call(
        body, name=name, out_shape=tuple(jax.ShapeDtypeStruct((r, c), F32) for _ in range(3)),
        grid=(r // rows,), in_specs=[spec] * 4, out_specs=(spec,) * 3,
        compiler_params=pltpu.CompilerParams(dimension_semantics=("arbitrary",),
                                             vmem_limit_bytes=VMEM_LIMIT_BYTES),
    )(w, g, m, v)


def _shift_down(ext, s):
    return pltpu.roll(ext, s, 0)


def _shift_up(ext, s):
    return pltpu.roll(ext, ext.shape[0] - s, 0)


def _lru_gates(xc, wa_ref, ba, wx_ref, bx, lam):
    pa, px = [], []
    for h in range(LRU_HEADS):
        xh = xc[:, h * HEAD_DIM:(h + 1) * HEAD_DIM].astype(BF16)
        pa.append(_dot(xh, wa_ref[h]))
        px.append(_dot(xh, wx_ref[h]))
    r = _sigmoid(jnp.concatenate(pa, axis=1) + ba)
    ig = _sigmoid(jnp.concatenate(px, axis=1) + bx)
    sp = _softplus(-lam)
    log_a = (-LRU_C * r) * sp
    a = jnp.exp(log_a)
    mult = jnp.sqrt(jnp.tanh(-log_a) * (1.0 + a * a))
    return r, ig, a, mult, sp


def _conv(ext, w_ref, b):
    y = b + _shift_down(ext, 3) * w_ref[0:1, :]
    y = y + _shift_down(ext, 2) * w_ref[1:2, :]
    y = y + _shift_down(ext, 1) * w_ref[2:3, :]
    y = y + ext * w_ref[3:4, :]
    return y[CONV_HIST:, :]


def _pool_diff(ext, pos):
    out = []
    for g, k in enumerate(POOL_WINDOWS):
        col = ext[:, g * POOL_GROUP_DIM:(g + 1) * POOL_GROUP_DIM]
        s = col
        for step in range(g + 1):
            s = s + _shift_down(s, 2 ** step)
        count = jnp.minimum(pos + 1, k).astype(F32)
        out.append(s[POOL_HIST:, :] / count - col[POOL_HIST:, :])
    return out


def _pool_mix(diff, pw_ref):
    return jnp.concatenate([_dot(diff[g].astype(BF16), pw_ref[g]) for g in range(len(POOL_WINDOWS))], axis=1)


def _branch_specs(tb, row_map, fixed):
    fixed3 = lambda i: (0, 0, 0)
    return [pl.BlockSpec((CONV_WIDTH, D_MODEL), fixed), pl.BlockSpec((1, D_MODEL), fixed),
            pl.BlockSpec((LRU_HEADS, HEAD_DIM, HEAD_DIM), fixed3), pl.BlockSpec((1, D_MODEL), fixed),
            pl.BlockSpec((LRU_HEADS, HEAD_DIM, HEAD_DIM), fixed3), pl.BlockSpec((1, D_MODEL), fixed),
            pl.BlockSpec((1, D_MODEL), fixed),
            pl.BlockSpec((len(POOL_WINDOWS), POOL_GROUP_DIM, POOL_GROUP_DIM), fixed3),
            pl.BlockSpec((1, POOL_WIDTH), fixed)]


def _branches_fwd(z, weights, seq, tb):
    t = z.shape[0]
    nbe = seq // tb
    groups = tb // F32_SUBLANES

    def body(xa_ref, ga_ref, xb_ref, gb_ref, cw_ref, cb_ref, wa_ref, ba_ref, wx_ref, bx_ref, lam_ref,
             pw_ref, ps_ref, ya_ref, yb_ref, hl_ref, xa_ext, xb_ext, carry, a_s, u_s):
        blk = pl.program_id(0) % nbe

        @pl.when(blk == 0)
        def _():
            xa_ext[0:CONV_HIST, :] = jnp.zeros((CONV_HIST, D_MODEL), F32)
            xb_ext[0:POOL_HIST, :] = jnp.zeros((POOL_HIST, POOL_WIDTH), F32)
            carry[...] = jnp.zeros_like(carry)

        xa_ext[CONV_HIST:, :] = xa_ref[...]
        xb_ext[POOL_HIST:, :] = xb_ref[...]
        ea = xa_ext[...]
        eb = xb_ext[...]
        xa_ext[0:CONV_HIST, :] = ea[tb:, :]
        xb_ext[0:POOL_HIST, :] = eb[tb:, :]

        xc = _conv(ea, cw_ref, cb_ref[...])
        _, ig, a, mult, _ = _lru_gates(xc, wa_ref, ba_ref[...], wx_ref, bx_ref[...], lam_ref[...])
        u = mult * (ig * xc)
        row8 = lax.broadcasted_iota(jnp.int32, (tb, D_MODEL), 0) % F32_SUBLANES
        for s in (1, 2, 4):
            m = row8 >= s
            u = jnp.where(m, a * _shift_down(u, s) + u, u)
            a = jnp.where(m, a * _shift_down(a, s), a)
        a_s[...] = a
        u_s[...] = u

        def step(g, cr):
            sl = pl.ds(pl.multiple_of(g * F32_SUBLANES, F32_SUBLANES), F32_SUBLANES)
            hb = a_s[sl, :] * cr + u_s[sl, :]
            hl_ref[sl, :] = hb
            return jnp.broadcast_to(hb[F32_SUBLANES - 1:F32_SUBLANES, :], (F32_SUBLANES, D_MODEL))

        carry[...] = lax.fori_loop(0, groups, step, carry[...], unroll=4)
        ga = ga_ref[...]
        ya_ref[...] = (hl_ref[...] * (ga * _sigmoid(ga))).astype(BF16)

        pos = blk * tb + lax.broadcasted_iota(jnp.int32, (tb, POOL_GROUP_DIM), 0)
        ypre = _pool_mix(_pool_diff(eb, pos), pw_ref)
        gb = gb_ref[...]
        yb_ref[...] = ((ypre * ps_ref[...]) * (gb * _sigmoid(gb))).astype(BF16)

    row = lambda i: (i, 0)
    fixed = lambda i: (0, 0)
    in_specs = [pl.BlockSpec((tb, D_MODEL), lambda i: (i, 0)), pl.BlockSpec((tb, D_MODEL), lambda i: (i, 1)),
                pl.BlockSpec((tb, POOL_WIDTH), lambda i: (i, 4)), pl.BlockSpec((tb, POOL_WIDTH), lambda i: (i, 5)),
                ] + _branch_specs(tb, row, fixed)
    return pl.pallas_call(
        body, name="branches_fwd",
        out_shape=(jax.ShapeDtypeStruct((t, D_MODEL), BF16), jax.ShapeDtypeStruct((t, POOL_WIDTH), BF16),
                   jax.ShapeDtypeStruct((t, D_MODEL), F32)),
        grid=(t // tb,), in_specs=in_specs,
        out_specs=(pl.BlockSpec((tb, D_MODEL), row), pl.BlockSpec((tb, POOL_WIDTH), row),
                   pl.BlockSpec((tb, D_MODEL), row)),
        scratch_shapes=[pltpu.VMEM((tb + CONV_HIST, D_MODEL), F32), pltpu.VMEM((tb + POOL_HIST, POOL_WIDTH), F32),
                        pltpu.VMEM((F32_SUBLANES, D_MODEL), F32),
                        pltpu.VMEM((tb, D_MODEL), F32), pltpu.VMEM((tb, D_MODEL), F32)],
        compiler_params=pltpu.CompilerParams(dimension_semantics=("arbitrary",),
                                             vmem_limit_bytes=VMEM_LIMIT_BYTES),
    )(z, z, z, z, *weights)


def _branches_bwd(z, hl, dya, dyb, dzm, weights, seq, tb):
    t = z.shape[0]
    nb = t // tb
    nbe = seq // tb
    groups = tb // F32_SUBLANES
    n_pool = len(POOL_WINDOWS)

    def body(xa_ref, xap_ref, ga_ref, xb_ref, xbp_ref, gb_ref, hl_ref, hlp_ref, dya_ref, dyb_ref, dzm_ref,
             cw_ref, cb_ref, wa_ref, ba_ref, wx_ref, bx_ref, lam_ref, pw_ref, ps_ref,
             dz_ref, dcw_ref, dcb_ref, dwa_ref, dba_ref, dwx_ref, dbx_ref, dlam_ref, dpw_ref, dps_ref,
             xa_ext, xb_ext, hl_ext, a_ext, dxc_ext, dwin_ext, g_carry, b_s, d_s, g_s):
        i = pl.program_id(0)
        blk = (nb - 1 - i) % nbe

        @pl.when(i == 0)
        def _():
            for ref in (dcw_ref, dcb_ref, dwa_ref, dba_ref, dwx_ref, dbx_ref, dlam_ref, dpw_ref, dps_ref):
                ref[...] = jnp.zeros_like(ref)

        @pl.when(blk == nbe - 1)
        def _():
            a_ext[tb:, :] = jnp.zeros((F32_SUBLANES, D_MODEL), F32)
            dxc_ext[tb:, :] = jnp.zeros((CONV_HIST, D_MODEL), F32)
            dwin_ext[tb:, :] = jnp.zeros((POOL_HIST, POOL_WIDTH), F32)
            g_carry[...] = jnp.zeros_like(g_carry)

        live = (blk > 0).astype(F32)
        xa_ext[0:CONV_HIST, :] = xap_ref[...] * live
        xa_ext[CONV_HIST:, :] = xa_ref[...]
        xb_ext[0:POOL_HIST, :] = xbp_ref[...] * live
        xb_ext[POOL_HIST:, :] = xb_ref[...]
        hl_ext[0:F32_SUBLANES, :] = hlp_ref[...] * live
        hl_ext[F32_SUBLANES:, :] = hl_ref[...]
        ea = xa_ext[...]
        eb = xb_ext[...]

        xc = _conv(ea, cw_ref, cb_ref[...])
        lam = lam_ref[...]
        r, ig, a, mult, sp = _lru_gates(xc, wa_ref, ba_ref[...], wx_ref, bx_ref[...], lam)
        hl = hl_ref[...]
        ga = ga_ref[...]
        sga = _sigmoid(ga)
        dya = dya_ref[...]
        dhl = dya * (ga * sga)
        dz_ref[:, D_MODEL:2 * D_MODEL] = (dya * hl * (sga * (1.0 + ga * (1.0 - sga)))).astype(BF16)

        a_ext[0:tb, :] = a
        b = _shift_up(a_ext[...], 1)[0:tb, :]
        a_ext[tb:, :] = jnp.broadcast_to(a[0:1, :], (F32_SUBLANES, D_MODEL))
        d = dhl
        row8 = lax.broadcasted_iota(jnp.int32, (tb, D_MODEL), 0) % F32_SUBLANES
        for s in (1, 2, 4):
            m = row8 < F32_SUBLANES - s
            d = jnp.where(m, d + b * _shift_up(d, s), d)
            b = jnp.where(m, b * _shift_up(b, s), b)
        b_s[...] = b
        d_s[...] = d

        def step(k, cr):
            sl = pl.ds(pl.multiple_of((groups - 1 - k) * F32_SUBLANES, F32_SUBLANES), F32_SUBLANES)
            gb_ = d_s[sl, :] + b_s[sl, :] * cr
            g_s[sl, :] = gb_
            return jnp.broadcast_to(gb_[0:1, :], (F32_SUBLANES, D_MODEL))

        g_carry[...] = lax.fori_loop(0, groups, step, g_carry[...], unroll=4)
        gsc = g_s[...]
        da = gsc * _shift_down(hl_ext[...], 1)[F32_SUBLANES:, :]
        dmult = gsc * (ig * xc)
        dig = gsc * (mult * xc)
        dxc = gsc * (mult * ig)
        dlog_a = da * a - (a * a) * dmult / mult
        dr = dlog_a * (-LRU_C * sp)
        dlam_ref[...] += jnp.sum(dlog_a * (-LRU_C * r), axis=0, keepdims=True)
        dpa = dr * (r * (1.0 - r))
        dpx = dig * (ig * (1.0 - ig))
        dba_ref[...] += jnp.sum(dpa, axis=0, keepdims=True)
        dbx_ref[...] += jnp.sum(dpx, axis=0, keepdims=True)
        back = []
        for h in range(LRU_HEADS):
            cols = slice(h * HEAD_DIM, (h + 1) * HEAD_DIM)
            xh = xc[:, cols].astype(BF16)
            dpa_h = dpa[:, cols].astype(BF16)
            dpx_h = dpx[:, cols].astype(BF16)
            dwa_ref[h] += _dot_tn(xh, dpa_h)
            dwx_ref[h] += _dot_tn(xh, dpx_h)
            back.append(_dot_nt(dpa_h, wa_ref[h]) + _dot_nt(dpx_h, wx_ref[h]))
        dxc = dxc + jnp.concatenate(back, axis=1)
        dcb_ref[...] += jnp.sum(dxc, axis=0, keepdims=True)
        for k in range(CONV_WIDTH):
            tap = _shift_down(ea, CONV_WIDTH - 1 - k)[CONV_HIST:, :] if k < CONV_WIDTH - 1 else ea[CONV_HIST:, :]
            dcw_ref[k:k + 1, :] += jnp.sum(dxc * tap, axis=0, keepdims=True)
        dxc_ext[0:tb, :] = dxc
        ed = dxc_ext[...]
        dxa = ed * cw_ref[3:4, :]
        dxa = dxa + _shift_up(ed, 1) * cw_ref[2:3, :]
        dxa = dxa + _shift_up(ed, 2) * cw_ref[1:2, :]
        dxa = dxa + _shift_up(ed, 3) * cw_ref[0:1, :]
        dz_ref[:, 0:D_MODEL] = dxa[0:tb, :].astype(BF16)
        dxc_ext[tb:, :] = dxc[0:CONV_HIST, :]

        pos = blk * tb + lax.broadcasted_iota(jnp.int32, (tb, POOL_GROUP_DIM), 0)
        diff = _pool_diff(eb, pos)
        ypre = _pool_mix(diff, pw_ref)
        ps = ps_ref[...]
        gb = gb_ref[...]
        sgb = _sigmoid(gb)
        dyb = dyb_ref[...]
        dyp = dyb * (gb * sgb)
        dz_ref[:, 2 * D_MODEL + POOL_WIDTH:3 * D_MODEL] = (
            dyb * (ypre * ps) * (sgb * (1.0 + gb * (1.0 - sgb)))).astype(BF16)
        dps_ref[...] += jnp.sum(dyp * ypre, axis=0, keepdims=True)
        dypre = dyp * ps
        for g, k in enumerate(POOL_WINDOWS):
            cols = slice(g * POOL_GROUP_DIM, (g + 1) * POOL_GROUP_DIM)
            dyg = dypre[:, cols].astype(BF16)
            dpw_ref[g] += _dot_tn(diff[g].astype(BF16), dyg)
            ddiff = _dot_nt(dyg, pw_ref[g])
            count = jnp.minimum(pos + 1, k).astype(F32)
            dwin = ddiff / count
            dwin_ext[0:tb, cols] = dwin
            s = dwin_ext[:, cols]
            for step_ in range(g + 1):
                s = s + _shift_up(s, 2 ** step_)
            dz_ref[:, 2 * D_MODEL + g * POOL_GROUP_DIM:2 * D_MODEL + (g + 1) * POOL_GROUP_DIM] = (
                s[0:tb, :] - ddiff).astype(BF16)
            dwin_ext[tb:, cols] = dwin[0:POOL_HIST, :]

        dz_ref[:, 3 * D_MODEL:] = dzm_ref[...]

        @pl.when(i == nb - 1)
        def _():
            dlam_ref[...] = dlam_ref[...] * (-_sigmoid(-lam))

    rev = lambda i: (nb - 1 - i, 0)
    fixed = lambda i: (0, 0)
    fixed3 = lambda i: (0, 0, 0)

    def prev(rows, col):
        per = tb // rows
        return lambda i: (jnp.maximum((nb - 1 - i) * per - 1, 0), col)

    in_specs = [pl.BlockSpec((tb, D_MODEL), lambda i: (nb - 1 - i, 0)),
                pl.BlockSpec((CONV_HIST, D_MODEL), prev(CONV_HIST, 0)),
                pl.BlockSpec((tb, D_MODEL), lambda i: (nb - 1 - i, 1)),
                pl.BlockSpec((tb, POOL_WIDTH), lambda i: (nb - 1 - i, 4)),
                pl.BlockSpec((POOL_HIST, POOL_WIDTH), prev(POOL_HIST, 4)),
                pl.BlockSpec((tb, POOL_WIDTH), lambda i: (nb - 1 - i, 5)),
                pl.BlockSpec((tb, D_MODEL), rev),
                pl.BlockSpec((F32_SUBLANES, D_MODEL), prev(F32_SUBLANES, 0)),
                pl.BlockSpec((tb, D_MODEL), rev), pl.BlockSpec((tb, POOL_WIDTH), rev),
                pl.BlockSpec((tb, 2 * D_MODEL), rev)] + _branch_specs(tb, rev, fixed)
    out_shape = (jax.ShapeDtypeStruct((t, IN_COLS), BF16),
                 jax.ShapeDtypeStruct((CONV_WIDTH, D_MODEL), F32), jax.ShapeDtypeStruct((1, D_MODEL), F32),
                 jax.ShapeDtypeStruct((LRU_HEADS, HEAD_DIM, HEAD_DIM), F32), jax.ShapeDtypeStruct((1, D_MODEL), F32),
                 jax.ShapeDtypeStruct((LRU_HEADS, HEAD_DIM, HEAD_DIM), F32), jax.ShapeDtypeStruct((1, D_MODEL), F32),
                 jax.ShapeDtypeStruct((1, D_MODEL), F32),
                 jax.ShapeDtypeStruct((n_pool, POOL_GROUP_DIM, POOL_GROUP_DIM), F32),
                 jax.ShapeDtypeStruct((1, POOL_WIDTH), F32))
    out_specs = (pl.BlockSpec((tb, IN_COLS), rev),
                 pl.BlockSpec((CONV_WIDTH, D_MODEL), fixed), pl.BlockSpec((1, D_MODEL), fixed),
                 pl.BlockSpec((LRU_HEADS, HEAD_DIM, HEAD_DIM), fixed3), pl.BlockSpec((1, D_MODEL), fixed),
                 pl.BlockSpec((LRU_HEADS, HEAD_DIM, HEAD_DIM), fixed3), pl.BlockSpec((1, D_MODEL), fixed),
                 pl.BlockSpec((1, D_MODEL), fixed),
                 pl.BlockSpec((n_pool, POOL_GROUP_DIM, POOL_GROUP_DIM), fixed3),
                 pl.BlockSpec((1, POOL_WIDTH), fixed))
    scratch = [pltpu.VMEM((tb + CONV_HIST, D_MODEL), F32), pltpu.VMEM((tb + POOL_HIST, POOL_WIDTH), F32),
               pltpu.VMEM((tb + F32_SUBLANES, D_MODEL), F32), pltpu.VMEM((tb + F32_SUBLANES, D_MODEL), F32),
               pltpu.VMEM((tb + CONV_HIST, D_MODEL), F32), pltpu.VMEM((tb + POOL_HIST, POOL_WIDTH), F32),
               pltpu.VMEM((F32_SUBLANES, D_MODEL), F32),
               pltpu.VMEM((tb, D_MODEL), F32), pltpu.VMEM((tb, D_MODEL), F32), pltpu.VMEM((tb, D_MODEL), F32)]
    return pl.pallas_call(
        body, name="branches_bwd", out_shape=out_shape, grid=(nb,), in_specs=in_specs, out_specs=out_specs,
        scratch_shapes=scratch,
        compiler_params=pltpu.CompilerParams(dimension_semantics=("arbitrary",),
                                             vmem_limit_bytes=VMEM_LIMIT_BYTES),
    )(z, z, z, z, z, z, hl, hl, dya, dyb, dzm, *weights)


def _merge_head(x2d, ya, yb, z, p2d, tgt, w_pl, w_pp, w_out, w_pg, w_pe, g2, gf, tb):
    t = x2d.shape[0]
    p_dim = p2d.shape[1]

    def body(x_ref, ya_ref, yb_ref, ma_ref, mb_ref, p_ref, t_ref, wpl_ref, wpp_ref, wout_ref, wpg_ref, wpe_ref,
             g2_ref, gf_ref,
             loss_ref, dg2_ref, dgf_ref, dxr_ref, dya_ref, dyb_ref, dzm_ref,
             mg_ref, do_ref, hn_ref, dgp_ref, dpe_ref, da_ref, dbm_ref, pbf_ref):
        @pl.when(pl.program_id(0) == 0)
        def _():
            loss_ref[...] = jnp.zeros_like(loss_ref)
            dg2_ref[...] = jnp.zeros_like(dg2_ref)
            dgf_ref[...] = jnp.zeros_like(dgf_ref)

        a_ = _dot(ya_ref[...], wpl_ref[...])
        bm = _dot(yb_ref[...], wpp_ref[...])
        sa = _sigmoid(ma_ref[...])
        sb = _sigmoid(mb_ref[...])
        mg = (sa * a_ + sb * bm).astype(BF16)
        mg_ref[...] = mg
        x1 = x_ref[...] + _dot(mg, wout_ref[...])
        xn2, r2 = _rms(x1)
        g2 = g2_ref[...]
        hn = (xn2 * g2).astype(BF16)
        hn_ref[...] = hn
        gate = _sigmoid(_dot(hn, wpg_ref[...]))
        pbf = p_ref[...].astype(BF16)
        pbf_ref[...] = pbf
        pe = _dot(pbf, wpe_ref[...])
        x2 = x1 + gate * pe
        xn3, r3 = _rms(x2)
        gf = gf_ref[...]
        err = xn3 * gf - t_ref[...]
        loss_ref[...] += 0.5 * jnp.sum(jnp.mean(err * err, axis=-1))

        dy = err * (1.0 / D_MODEL)
        dgf_ref[...] += jnp.sum(dy * xn3, axis=0, keepdims=True)
        dx2 = _rms_bwd(dy * gf, xn3, r3)
        dpe_ref[...] = (dx2 * gate).astype(BF16)
        dgp = ((dx2 * pe) * (gate * (1.0 - gate))).astype(BF16)
        dgp_ref[...] = dgp
        dhn = _dot_nt(dgp, wpg_ref[...])
        dg2_ref[...] += jnp.sum(dhn * xn2, axis=0, keepdims=True)
        dx1 = dx2 + _rms_bwd(dhn * g2, xn2, r2)
        dxr_ref[...] = dx1
        do = dx1.astype(BF16)
        do_ref[...] = do
        dmg = _dot_nt(do, wout_ref[...])
        da = (dmg * sa).astype(BF16)
        dbm = (dmg * sb).astype(BF16)
        da_ref[...] = da
        dbm_ref[...] = dbm
        dzm_ref[:, 0:D_MODEL] = (dmg * a_ * (sa * (1.0 - sa))).astype(BF16)
        dzm_ref[:, D_MODEL:] = (dmg * bm * (sb * (1.0 - sb))).astype(BF16)
        dya_ref[...] = _dot_nt(da, wpl_ref[...])
        dyb_ref[...] = _dot_nt(dbm, wpp_ref[...])

    row = lambda i: (i, 0)
    fixed = lambda i: (0, 0)

    def resident(shape):
        return pl.BlockSpec(shape, fixed, pipeline_mode=pl.Buffered(1))

    tok = lambda width: pl.BlockSpec((tb, width), row)
    in_specs = [tok(D_MODEL), tok(D_MODEL), tok(POOL_WIDTH),
                pl.BlockSpec((tb, D_MODEL), lambda i: (i, 3)), pl.BlockSpec((tb, D_MODEL), lambda i: (i, 4)),
                tok(p_dim), tok(D_MODEL),
                resident((D_MODEL, D_MODEL)), resident((POOL_WIDTH, D_MODEL)), resident((D_MODEL, D_MODEL)),
                resident((D_MODEL, D_MODEL)), resident((p_dim, D_MODEL)),
                pl.BlockSpec((1, D_MODEL), fixed), pl.BlockSpec((1, D_MODEL), fixed)]
    bf = lambda width: jax.ShapeDtypeStruct((t, width), BF16)
    f32 = lambda width: jax.ShapeDtypeStruct((t, width), F32)
    out_shape = (jax.ShapeDtypeStruct((F32_SUBLANES, 128), F32), jax.ShapeDtypeStruct((1, D_MODEL), F32),
                 jax.ShapeDtypeStruct((1, D_MODEL), F32),
                 f32(D_MODEL), f32(D_MODEL), f32(POOL_WIDTH), bf(2 * D_MODEL),
                 bf(D_MODEL), bf(D_MODEL), bf(D_MODEL), bf(D_MODEL), bf(D_MODEL), bf(D_MODEL), bf(D_MODEL), bf(p_dim))
    out_specs = (pl.BlockSpec((F32_SUBLANES, 128), fixed), pl.BlockSpec((1, D_MODEL), fixed),
                 pl.BlockSpec((1, D_MODEL), fixed),
                 tok(D_MODEL), tok(D_MODEL), tok(POOL_WIDTH), tok(2 * D_MODEL),
                 tok(D_MODEL), tok(D_MODEL), tok(D_MODEL), tok(D_MODEL), tok(D_MODEL), tok(D_MODEL), tok(D_MODEL),
                 tok(p_dim))
    return pl.pallas_call(
        body, name="merge_head", out_shape=out_shape, grid=(t // tb,), in_specs=in_specs, out_specs=out_specs,
        compiler_params=pltpu.CompilerParams(dimension_semantics=("arbitrary",),
                                             vmem_limit_bytes=VMEM_LIMIT_BYTES),
    )(x2d, ya, yb, z, z, p2d, tgt, w_pl, w_pp, w_out, w_pg, w_pe, g2, gf)


def _pad_rows(a, rows):
    return jnp.pad(a, ((0, rows - a.shape[0]), (0, D_MODEL - a.shape[1])))


def _pack_bag(parts):
    rows = [_pad_rows(a.reshape(-1, a.shape[-1]) if a.shape[-1] != HEAD_DIM else a.reshape(-1, D_MODEL), n)
            for a, n in zip(parts, BAG_PART_ROWS)]
    rows.append(jnp.zeros((BAG_ROWS - sum(BAG_PART_ROWS), D_MODEL), F32))
    return jnp.concatenate(rows, axis=0)


def _unpack_bag(bag, shapes):
    out, at = [], 0
    for shape, n in zip(shapes, BAG_PART_ROWS):
        size = 1
        for s in shape:
            size *= s
        if size % D_MODEL == 0:
            piece = bag[at:at + size // D_MODEL, :]
        else:
            piece = bag[at:at + 1, :size]
        out.append(piece.reshape(shape))
        at += n
    return out


def kernel(x, p, norm_g, w_in, conv_w, conv_b, lru_w_a, lru_b_a, lru_w_x, lru_b_x, lru_lambda, pool_w, pool_scale, w_proj_lru, w_proj_pool, w_out, ple_norm_g, w_ple_gate, w_ple_proj, final_g, loss_target, m_norm_g, m_w_in, m_conv_w, m_conv_b, m_lru_w_a, m_lru_b_a, m_lru_w_x, m_lru_b_x, m_lru_lambda, m_pool_w, m_pool_scale, m_w_proj_lru, m_w_proj_pool, m_w_out, m_ple_norm_g, m_w_ple_gate, m_w_ple_proj, m_final_g, v_norm_g, v_w_in, v_conv_w, v_conv_b, v_lru_w_a, v_lru_b_a, v_lru_w_x, v_lru_b_x, v_lru_lambda, v_pool_w, v_pool_scale, v_w_proj_lru, v_w_proj_pool, v_w_out, v_ple_norm_g, v_w_ple_gate, v_w_ple_proj, v_final_g):
    bsz, seq, _ = x.shape
    t = bsz * seq
    tb_mm = min(512, seq)
    tb_seq = min(256, seq // 2) if seq >= 512 else seq
    x2d = x.reshape(t, D_MODEL)
    p2d = p.reshape(t, p.shape[-1])
    tgt = loss_target.reshape(t, D_MODEL)
    chip = 2 * lax.axis_index("x") + lax.axis_index("y")

    big = [(w_in[0], 1), (w_proj_lru[0], 0), (w_proj_pool[0], 1), (w_out[0], 0), (w_ple_gate[0], 0), (w_ple_proj[0], 1)]
    w_in_f, w_pl_f, w_pp_f, w_out_f, w_pg_f, w_pe_f, conv_w_f = _gather_shards(
        [(w.astype(BF16), axis, True) for w, axis in big] + [(conv_w[0], 1, False)], "gather_weights")

    wa_bf = lru_w_a[0].astype(BF16)
    wx_bf = lru_w_x[0].astype(BF16)
    pw_bf = pool_w[0].astype(BF16)
    branch_w = (conv_w_f, conv_b, wa_bf, lru_b_a.reshape(1, D_MODEL), wx_bf, lru_b_x.reshape(1, D_MODEL),
                lru_lambda, pw_bf, pool_scale)

    z = _in_proj(x2d, norm_g, w_in_f, tb_mm)
    ya, yb, hl = _branches_fwd(z, branch_w, seq, tb_seq)
    (loss_acc, d_g2, d_gf, dx_res, dya, dyb, dzm, mg_bf, do_bf, hn_bf, dgp_bf, dpe_bf, da_bf, dbm_bf, p_bf) = _merge_head(
        x2d, ya, yb, z, p2d, tgt, w_pl_f, w_pp_f, w_out_f, w_pg_f, w_pe_f, ple_norm_g, final_g.reshape(1, D_MODEL),
        tb_seq)
    (dz, d_cw, d_cb, d_wa, d_ba, d_wx, d_bx, d_lam, d_pw, d_ps) = _branches_bwd(
        z, hl, dya, dyb, dzm, branch_w, seq, tb_seq)
    dx, h_bf, d_g1 = _in_proj_bwd(dz, w_in_f, x2d, dx_res, norm_g, tb_mm)

    g_in = _weight_grad(h_bf, dz, N_CHIPS, tb_mm, "dw_in").reshape(8, D_MODEL // 2, IN_COLS // N_CHIPS)
    g_pl = _weight_grad(ya, da_bf, 1, tb_mm, "dw_proj_lru").reshape(8, D_MODEL // 8, D_MODEL)
    g_pp = _weight_grad(yb, dbm_bf, N_CHIPS, tb_mm, "dw_proj_pool").reshape(8, POOL_WIDTH // 2, D_MODEL // N_CHIPS)
    g_out = _weight_grad(mg_bf, do_bf, 1, tb_mm, "dw_out").reshape(8, D_MODEL // 8, D_MODEL)
    g_pg = _weight_grad(hn_bf, dgp_bf, 1, tb_mm, "dw_ple_gate").reshape(8, D_MODEL // 8, D_MODEL)
    p_dim = p2d.shape[1]
    g_pe = _weight_grad(p_bf, dpe_bf, N_CHIPS, tb_mm, "dw_ple_proj").reshape(8, p_dim // 2, D_MODEL // N_CHIPS)

    (r_in,) = _reduce_scatter([g_in], "rs_w_in")
    r_pl, r_pp, r_out, r_pg, r_pe = _reduce_scatter([g_pl, g_pp, g_out, g_pg, g_pe], "rs_proj")
    small_shapes = [(1, D_MODEL), (1, CONV_WIDTH, D_MODEL), (1, D_MODEL), lru_w_a.shape, lru_b_a.shape, lru_w_x.shape,
                    lru_b_x.shape, (1, D_MODEL), pool_w.shape, pool_scale.shape, (1, D_MODEL), final_g.shape]
    bag = _pack_bag([d_g1, d_cw, d_cb, d_wa, d_ba.reshape(1, D_MODEL), d_wx, d_bx.reshape(1, D_MODEL), d_lam, d_pw,
                     d_ps, d_g2, d_gf])
    (bag_mine,) = _reduce_scatter([bag.reshape(8, BAG_ROWS // 8, D_MODEL)], "rs_small")
    (bag_sum,) = _gather_shards([(bag_mine.reshape(BAG_ROWS // N_CHIPS, D_MODEL), 0, True)], "gather_small")
    (g_g1, g_cw_full, g_cb, g_wa, g_ba, g_wx, g_bx, g_lam, g_pw, g_ps, g_g2, g_gf) = _unpack_bag(bag_sum, small_shapes)
    cw_cols = D_MODEL // N_CHIPS
    g_cw = lax.dynamic_slice_in_dim(g_cw_full, chip * cw_cols, cw_cols, axis=2)

    def big_update(w, g2d, m, v, rows, name):
        d, nm, nv = _adamw(w[0], g2d, m[0], v[0], rows, name)
        return g2d[None], d[None], nm[None], nv[None]

    u_in = big_update(w_in, r_in.reshape(D_MODEL, IN_COLS // N_CHIPS), m_w_in, v_w_in, 256, "adamw_w_in")
    u_pl = big_update(w_proj_lru, r_pl.reshape(D_MODEL // N_CHIPS, D_MODEL), m_w_proj_lru, v_w_proj_lru, 256, "adamw_w_proj_lru")
    u_pp = big_update(w_proj_pool, r_pp.reshape(POOL_WIDTH, D_MODEL // N_CHIPS), m_w_proj_pool, v_w_proj_pool, 512, "adamw_w_proj_pool")
    u_out = big_update(w_out, r_out.reshape(D_MODEL // N_CHIPS, D_MODEL), m_w_out, v_w_out, 256, "adamw_w_out")
    u_pg = big_update(w_ple_gate, r_pg.reshape(D_MODEL // N_CHIPS, D_MODEL), m_w_ple_gate, v_w_ple_gate, 256, "adamw_w_ple_gate")
    u_pe = big_update(w_ple_proj, r_pe.reshape(p_dim, D_MODEL // N_CHIPS), m_w_ple_proj, v_w_ple_proj, 256, "adamw_w_ple_proj")
    u_cw = big_update(conv_w, g_cw[0], m_conv_w, v_conv_w, CONV_WIDTH, "adamw_conv_w")

    small_w = [norm_g, None, conv_b, lru_w_a, lru_b_a, lru_w_x, lru_b_x, lru_lambda, pool_w, pool_scale, ple_norm_g, final_g]
    small_m = [m_norm_g, None, m_conv_b, m_lru_w_a, m_lru_b_a, m_lru_w_x, m_lru_b_x, m_lru_lambda, m_pool_w, m_pool_scale, m_ple_norm_g, m_final_g]
    small_v = [v_norm_g, None, v_conv_b, v_lru_w_a, v_lru_b_a, v_lru_w_x, v_lru_b_x, v_lru_lambda, v_pool_w, v_pool_scale, v_ple_norm_g, v_final_g]
    fill = jnp.zeros((CONV_WIDTH, D_MODEL), F32)

    def bag_of(arrs):
        return _pack_bag([fill if a is None else (a[0] if a.ndim > 1 else a[None]) for a in arrs])

    d_bag, m_bag, v_bag = _adamw(bag_of(small_w), bag_sum, bag_of(small_m), bag_of(small_v), BAG_ROWS // 8, "adamw_small")
    d_small = _unpack_bag(d_bag, small_shapes)
    m_small = _unpack_bag(m_bag, small_shapes)
    v_small = _unpack_bag(v_bag, small_shapes)

    loss = lax.psum(loss_acc[0, 0], ALL_AXES)
    grad_x = dx.reshape(bsz, seq, D_MODEL)

    def ordered(small, pick):
        s = list(small)
        return [s[0], u_in[pick], u_cw[pick], s[2], s[3], s[4], s[5], s[6], s[7], s[8], s[9],
                u_pl[pick], u_pp[pick], u_out[pick], s[10], u_pg[pick], u_pe[pick], s[11]]

    grads = ordered([g_g1, None, g_cb, g_wa, g_ba, g_wx, g_bx, g_lam, g_pw, g_ps, g_g2, g_gf], 0)
    return (loss, grad_x, *grads, *ordered(d_small, 1), *ordered(m_small, 2), *ordered(v_small, 3))
```

```python
import functools

import jax
import jax.numpy as jnp
from jax import lax
from jax.experimental import pallas as pl
from jax.experimental.pallas import tpu as pltpu

F32 = jnp.float32
BF16 = jnp.bfloat16
MESH = pl.DeviceIdType.MESH
ALL_AXES = ("x", "y", "c")

D_MODEL = 1024
LRU_HEADS = 8
HEAD_DIM = 128
CONV_WIDTH = 4
LRU_C = 8.0
POOL_WIDTH = 512
POOL_WINDOWS = (2, 4, 8, 16)
POOL_GROUP_DIM = 128
IN_COLS = 5120
N_CHIPS = 4
EPS = 1e-6

ADAM_LR = 0.001
ADAM_B1 = 0.9
ADAM_B2 = 0.999
ADAM_EPS = 1e-08
ADAM_WD = 0.01
ADAM_STEP = 10

F32_SUBLANES = 8
CONV_HIST = 8
POOL_HIST = 16
VMEM_LIMIT_BYTES = 58 * 1024 * 1024
BAG_PART_ROWS = (8, 8, 8, 128, 8, 128, 8, 8, 64, 8, 8, 8)
BAG_ROWS = 448


def _dot(a, b):
    return jnp.dot(a, b, preferred_element_type=F32)


def _dot_nt(a, b):
    return lax.dot_general(a, b, (((1,), (1,)), ((), ())), preferred_element_type=F32)


def _dot_tn(a, b):
    return lax.dot_general(a, b, (((0,), (0,)), ((), ())), preferred_element_type=F32)


def _sigmoid(v):
    return jax.nn.sigmoid(v)


def _softplus(v):
    return jnp.maximum(v, 0.0) + jnp.log1p(jnp.exp(-jnp.abs(v)))


def _place():
    return lax.axis_index("x"), lax.axis_index("y"), lax.axis_index("c")


def _gather_shards(shards, name):
    n = len(shards)
    n_sem = 6

    def body(*refs):
        ins, outs = refs[:n], refs[n:2 * n]
        send_sems, recv_sems, local_sems = refs[2 * n:]
        x, y, c = _place()
        chips = [(1 - x, y), (x, 1 - y), (1 - x, 1 - y)]

        def region(k, cx, cy, half):
            (r, cols), axis = shards[k][0].shape, shards[k][1]
            j = 2 * cx + cy
            if axis == 0:
                if half is None:
                    return outs[k].at[pl.ds(j * r, r), :]
                return outs[k].at[pl.ds(j * r + half * (r // 2), r // 2), :]
            if half is None:
                return outs[k].at[:, pl.ds(j * cols, cols)]
            return outs[k].at[pl.ds(half * (r // 2), r // 2), pl.ds(j * cols, cols)]

        def remote(k, sem, block, to, src=None):
            dst = region(k, *block)
            return pltpu.make_async_remote_copy(
                src_ref=dst if src is None else src, dst_ref=dst,
                send_sem=send_sems.at[k * n_sem + sem], recv_sem=recv_sems.at[k * n_sem + sem],
                device_id=to, device_id_type=MESH)

        mine = [pltpu.make_async_copy(ins[k], region(k, x, y, None), local_sems.at[k]) for k in range(n)]
        for cp in mine:
            cp.start()
        sends = []
        for k in range(n):
            r = shards[k][0].shape[0]
            split = shards[k][2]
            src = ins[k].at[pl.ds(c * (r // 2), r // 2), :] if split else ins[k]
            for idx, chip in enumerate(chips):
                sends.append(remote(k, idx, (x, y, c if split else None), (*chip, c), src=src))
        for cp in sends:
            cp.start()
        for k in range(n):
            split = shards[k][2]
            for idx, chip in enumerate(chips):
                remote(k, idx, (*chip, c if split else None), (x, y, c)).wait_recv()
                if split:
                    fwd = remote(k, 3 + idx, (*chip, c), (x, y, 1 - c))
                    fwd.start()
                    sends.append(fwd)
        for k in range(n):
            if shards[k][2]:
                for idx, chip in enumerate(chips):
                    remote(k, 3 + idx, (*chip, 1 - c), (x, y, c)).wait_recv()
        for cp in sends:
            cp.wait_send()
        for cp in mine:
            cp.wait()

    out_shape = []
    for arr, axis, _ in shards:
        r, cols = arr.shape
        full = (N_CHIPS * r, cols) if axis == 0 else (r, N_CHIPS * cols)
        out_shape.append(jax.ShapeDtypeStruct(full, arr.dtype))
    any_spec = pl.BlockSpec(memory_space=pl.ANY)
    return pl.pallas_call(
        body, name=name, out_shape=tuple(out_shape),
        in_specs=[any_spec] * n, out_specs=tuple([any_spec] * n),
        scratch_shapes=[pltpu.SemaphoreType.DMA((n * n_sem,)), pltpu.SemaphoreType.DMA((n * n_sem,)),
                        pltpu.SemaphoreType.DMA((n,))],
    )(*[s[0] for s in shards])


RS_ADD_ROWS = (64, 56, 32, 16, 8)


def _reduce_scatter(parts, name, wire=F32):
    n = len(parts)
    n_sem = 8

    def body(*refs):
        ins, outs = refs[:n], refs[n:2 * n]
        own = refs[2 * n:3 * n]
        sib = refs[3 * n:4 * n]
        got = refs[4 * n:5 * n]
        fin = refs[5 * n:6 * n]
        snd = refs[6 * n:7 * n]
        send_sems, recv_sems, local_sems = refs[7 * n:]
        x, y, c = _place()
        j_me = 2 * x + y
        chips = [(1 - x, y), (x, 1 - y), (1 - x, 1 - y)]

        def remote(a, sem, src, dst, to):
            return pltpu.make_async_remote_copy(
                src_ref=src, dst_ref=dst, send_sem=send_sems.at[a * n_sem + sem],
                recv_sem=recv_sems.at[a * n_sem + sem], device_id=to, device_id_type=MESH)

        def rows_loop(a, fn):
            r = parts[a].shape[1]
            step = max(s for s in RS_ADD_ROWS if r % s == 0)

            def it(i, carry):
                fn(pl.ds(pl.multiple_of(i * step, step), step))
                return carry

            lax.fori_loop(0, r // step, it, 0)

        loads, sends = [], []
        for a in range(n):
            for jj in range(N_CHIPS):
                cp = pltpu.make_async_copy(ins[a].at[2 * jj + c], own[a].at[jj], local_sems.at[a * 5 + jj])
                cp.start()
                loads.append(cp)
                sd = remote(a, jj, ins[a].at[2 * jj + (1 - c)], sib[a].at[jj], (x, y, 1 - c))
                sd.start()
                sends.append(sd)
        for a in range(n):
            for jj in range(N_CHIPS):
                loads[a * N_CHIPS + jj].wait()
                remote(a, jj, sib[a].at[jj], sib[a].at[jj], (x, y, c)).wait_recv()

                def add(sl, a=a, jj=jj):
                    q = own[a][jj, sl, :] + sib[a][jj, sl, :]
                    own[a][jj, sl, :] = q
                    snd[a][jj, sl, :] = q.astype(wire)

                rows_loop(a, add)
        for a in range(n):
            for idx, chip in enumerate(chips):
                sd = remote(a, 4 + idx, snd[a].at[2 * chip[0] + chip[1]], got[a].at[j_me], (*chip, c))
                sd.start()
                sends.append(sd)
        for a in range(n):
            def keep(sl, a=a):
                got[a][j_me, sl, :] = snd[a][j_me, sl, :]

            rows_loop(a, keep)
        for a in range(n):
            for idx, chip in enumerate(chips):
                slot = got[a].at[2 * chip[0] + chip[1]]
                remote(a, 4 + idx, slot, slot, (x, y, c)).wait_recv()

            def total(sl, a=a):
                mine = own[a][j_me, sl, :]
                term = [jnp.where(j_me == jj, mine, got[a][jj, sl, :].astype(F32)) for jj in range(N_CHIPS)]
                fin[a][sl, :] = ((term[0] + term[1]) + term[2]) + term[3]

            rows_loop(a, total)
        stores = []
        for a in range(n):
            st = pltpu.make_async_copy(fin[a], outs[a].at[c], local_sems.at[a * 5 + 4])
            st.start()
            stores.append(st)
            sd = remote(a, 7, fin[a], outs[a].at[c], (x, y, 1 - c))
            sd.start()
            sends.append(sd)
        for a in range(n):
            remote(a, 7, outs[a].at[1 - c], outs[a].at[1 - c], (x, y, c)).wait_recv()
        for cp in sends:
            cp.wait_send()
        for cp in stores:
            cp.wait()

    any_spec = pl.BlockSpec(memory_space=pl.ANY)
    scratch = []
    for lead, dtype in ((N_CHIPS, F32), (N_CHIPS, F32), (N_CHIPS, wire), (None, F32), (N_CHIPS, wire)):
        for p in parts:
            shape = p.shape[1:] if lead is None else (lead,) + p.shape[1:]
            scratch.append(pltpu.VMEM(shape, dtype))
    scratch += [pltpu.SemaphoreType.DMA((n * n_sem,)), pltpu.SemaphoreType.DMA((n * n_sem,)),
                pltpu.SemaphoreType.DMA((n * 5,))]
    return pl.pallas_call(
        body, name=name,
        out_shape=tuple(jax.ShapeDtypeStruct((2,) + p.shape[1:], F32) for p in parts),
        in_specs=[any_spec] * n, out_specs=tuple([any_spec] * n), scratch_shapes=scratch,
        compiler_params=pltpu.CompilerParams(vmem_limit_bytes=VMEM_LIMIT_BYTES),
    )(*parts)


def _rms(x):
    r = lax.rsqrt(jnp.mean(x * x, axis=-1, keepdims=True) + EPS)
    return x * r, r


def _rms_bwd(dxn, xn, r):
    return r * (dxn - xn * jnp.mean(dxn * xn, axis=-1, keepdims=True))


def _in_proj(x2d, norm_g, w_in, tb):
    t = x2d.shape[0]
    cols = IN_COLS // N_CHIPS

    def body(x_ref, g_ref, w_ref, z_ref):
        xn, _ = _rms(x_ref[...])
        z_ref[...] = _dot((xn * g_ref[...]).astype(BF16), w_ref[...])

    return pl.pallas_call(
        body, name="in_proj", out_shape=jax.ShapeDtypeStruct((t, IN_COLS), F32),
        grid=(N_CHIPS, t // tb),
        in_specs=[pl.BlockSpec((tb, D_MODEL), lambda j, i: (i, 0)),
                  pl.BlockSpec((1, D_MODEL), lambda j, i: (0, 0)),
                  pl.BlockSpec((D_MODEL, cols), lambda j, i: (0, j))],
        out_specs=pl.BlockSpec((tb, cols), lambda j, i: (i, j)),
        compiler_params=pltpu.CompilerParams(dimension_semantics=("arbitrary", "arbitrary"),
                                             vmem_limit_bytes=VMEM_LIMIT_BYTES),
    )(x2d, norm_g, w_in)


def _in_proj_bwd(dz, w_in, x2d, dx_res, norm_g, tb):
    t = x2d.shape[0]

    def body(dz_ref, w_ref, x_ref, dres_ref, g_ref, dx_ref, h_ref, dg_ref):
        @pl.when(pl.program_id(0) == 0)
        def _():
            dg_ref[...] = jnp.zeros_like(dg_ref)

        xn, r = _rms(x_ref[...])
        g = g_ref[...]
        h_ref[...] = (xn * g).astype(BF16)
        dh = _dot_nt(dz_ref[...], w_ref[...])
        dg_ref[...] += jnp.sum(dh * xn, axis=0, keepdims=True)
        dx_ref[...] = dres_ref[...] + _rms_bwd(dh * g, xn, r)

    row = lambda i: (i, 0)
    fixed = lambda i: (0, 0)
    return pl.pallas_call(
        body, name="in_proj_bwd",
        out_shape=(jax.ShapeDtypeStruct((t, D_MODEL), F32), jax.ShapeDtypeStruct((t, D_MODEL), BF16),
                   jax.ShapeDtypeStruct((1, D_MODEL), F32)),
        grid=(t // tb,),
        in_specs=[pl.BlockSpec((tb, IN_COLS), row),
                  pl.BlockSpec((D_MODEL, IN_COLS), fixed, pipeline_mode=pl.Buffered(1)),
                  pl.BlockSpec((tb, D_MODEL), row), pl.BlockSpec((tb, D_MODEL), row),
                  pl.BlockSpec((1, D_MODEL), fixed)],
        out_specs=(pl.BlockSpec((tb, D_MODEL), row), pl.BlockSpec((tb, D_MODEL), row),
                   pl.BlockSpec((1, D_MODEL), fixed)),
        compiler_params=pltpu.CompilerParams(dimension_semantics=("arbitrary",),
                                             vmem_limit_bytes=VMEM_LIMIT_BYTES),
    )(dz, w_in, x2d, dx_res, norm_g)


def _weight_grad(lhs, rhs, n_chunks, tb, name):
    t, k = lhs.shape
    nc = rhs.shape[1] // n_chunks

    def body(l_ref, r_ref, o_ref):
        @pl.when(pl.program_id(1) == 0)
        def _():
            o_ref[...] = jnp.zeros_like(o_ref)

        o_ref[...] += _dot_tn(l_ref[...], r_ref[...])

    return pl.pallas_call(
        body, name=name, out_shape=jax.ShapeDtypeStruct((n_chunks, k, nc), F32),
        grid=(n_chunks, t // tb),
        in_specs=[pl.BlockSpec((tb, k), lambda j, i: (i, 0)), pl.BlockSpec((tb, nc), lambda j, i: (i, j))],
        out_specs=pl.BlockSpec((None, k, nc), lambda j, i: (j, 0, 0)),
        compiler_params=pltpu.CompilerParams(dimension_semantics=("arbitrary", "arbitrary"),
                                             vmem_limit_bytes=VMEM_LIMIT_BYTES),
    )(lhs, rhs)


def _adamw(w, g, m, v, rows, name):
    r, c = w.shape

    def body(w_ref, g_ref, m_ref, v_ref, d_ref, nm_ref, nv_ref):
        g_ = g_ref[...]
        m_ = ADAM_B1 * m_ref[...] + (1.0 - ADAM_B1) * g_
        v_ = ADAM_B2 * v_ref[...] + (1.0 - ADAM_B2) * jnp.square(g_)
        m_hat = m_ / (1.0 - ADAM_B1 ** ADAM_STEP)
        v_hat = v_ / (1.0 - ADAM_B2 ** ADAM_STEP)
        d_ref[...] = -ADAM_LR * (m_hat / (jnp.sqrt(v_hat) + ADAM_EPS) + ADAM_WD * w_ref[...])
        nm_ref[...] = m_
        nv_ref[...] = v_

    spec = pl.BlockSpec((rows, c), lambda i: (i, 0))
    return pl.pallas_call(
        body, name=name, out_shape=tuple(jax.ShapeDtypeStruct((r, c), F32) for _ in range(3)),
        grid=(r // rows,), in_specs=[spec] * 4, out_specs=(spec,) * 3,
        compiler_params=pltpu.CompilerParams(dimension_semantics=("arbitrary",),
                                             vmem_limit_bytes=VMEM_LIMIT_BYTES),
    )(w, g, m, v)


def _shift_down(ext, s):
    return pltpu.roll(ext, s, 0)


def _shift_up(ext, s):
    return pltpu.roll(ext, ext.shape[0] - s, 0)


def _lru_gates(xc, wa_ref, ba, wx_ref, bx, lam):
    pa, px = [], []
    for h in range(LRU_HEADS):
        xh = xc[:, h * HEAD_DIM:(h + 1) * HEAD_DIM].astype(BF16)
        pa.append(_dot(xh, wa_ref[h]))
        px.append(_dot(xh, wx_ref[h]))
    r = _sigmoid(jnp.concatenate(pa, axis=1) + ba)
    ig = _sigmoid(jnp.concatenate(px, axis=1) + bx)
    sp = _softplus(-lam)
    log_a = (-LRU_C * r) * sp
    a = jnp.exp(log_a)
    mult = jnp.sqrt(jnp.tanh(-log_a) * (1.0 + a * a))
    return r, ig, a, mult, sp


def _conv(ext, w_ref, b):
    y = b + _shift_down(ext, 3) * w_ref[0:1, :]
    y = y + _shift_down(ext, 2) * w_ref[1:2, :]
    y = y + _shift_down(ext, 1) * w_ref[2:3, :]
    y = y + ext * w_ref[3:4, :]
    return y[CONV_HIST:, :]


def _pool_diff(ext, pos):
    out = []
    for g, k in enumerate(POOL_WINDOWS):
        col = ext[:, g * POOL_GROUP_DIM:(g + 1) * POOL_GROUP_DIM]
        s = col
        for step in range(g + 1):
            s = s + _shift_down(s, 2 ** step)
        count = jnp.minimum(pos + 1, k).astype(F32)
        out.append(s[POOL_HIST:, :] / count - col[POOL_HIST:, :])
    return out


def _pool_mix(diff, pw_ref):
    return jnp.concatenate([_dot(diff[g].astype(BF16), pw_ref[g]) for g in range(len(POOL_WINDOWS))], axis=1)


def _branch_specs(tb, row_map, fixed):
    fixed3 = lambda i: (0, 0, 0)
    return [pl.BlockSpec((CONV_WIDTH, D_MODEL), fixed), pl.BlockSpec((1, D_MODEL), fixed),
            pl.BlockSpec((LRU_HEADS, HEAD_DIM, HEAD_DIM), fixed3), pl.BlockSpec((1, D_MODEL), fixed),
            pl.BlockSpec((LRU_HEADS, HEAD_DIM, HEAD_DIM), fixed3), pl.BlockSpec((1, D_MODEL), fixed),
            pl.BlockSpec((1, D_MODEL), fixed),
            pl.BlockSpec((len(POOL_WINDOWS), POOL_GROUP_DIM, POOL_GROUP_DIM), fixed3),
            pl.BlockSpec((1, POOL_WIDTH), fixed)]


def _branches_fwd(z, weights, seq, tb):
    t = z.shape[0]
    nbe = seq // tb
    groups = tb // F32_SUBLANES

    def body(xa_ref, ga_ref, xb_ref, gb_ref, cw_ref, cb_ref, wa_ref, ba_ref, wx_ref, bx_ref, lam_ref,
             pw_ref, ps_ref, ya_ref, yb_ref, hl_ref, xa_ext, xb_ext, carry, a_s, u_s):
        blk = pl.program_id(0) % nbe

        @pl.when(blk == 0)
        def _():
            xa_ext[0:CONV_HIST, :] = jnp.zeros((CONV_HIST, D_MODEL), F32)
            xb_ext[0:POOL_HIST, :] = jnp.zeros((POOL_HIST, POOL_WIDTH), F32)
            carry[...] = jnp.zeros_like(carry)

        xa_ext[CONV_HIST:, :] = xa_ref[...]
        xb_ext[POOL_HIST:, :] = xb_ref[...]
        ea = xa_ext[...]
        eb = xb_ext[...]
        xa_ext[0:CONV_HIST, :] = ea[tb:, :]
        xb_ext[0:POOL_HIST, :] = eb[tb:, :]

        xc = _conv(ea, cw_ref, cb_ref[...])
        _, ig, a, mult, _ = _lru_gates(xc, wa_ref, ba_ref[...], wx_ref, bx_ref[...], lam_ref[...])
        u = mult * (ig * xc)
        row8 = lax.broadcasted_iota(jnp.int32, (tb, D_MODEL), 0) % F32_SUBLANES
        for s in (1, 2, 4):
            m = row8 >= s
            u = jnp.where(m, a * _shift_down(u, s) + u, u)
            a = jnp.where(m, a * _shift_down(a, s), a)
        a_s[...] = a
        u_s[...] = u

        def step(g, cr):
            sl = pl.ds(pl.multiple_of(g * F32_SUBLANES, F32_SUBLANES), F32_SUBLANES)
            hb = a_s[sl, :] * cr + u_s[sl, :]
            hl_ref[sl, :] = hb
            return jnp.broadcast_to(hb[F32_SUBLANES - 1:F32_SUBLANES, :], (F32_SUBLANES, D_MODEL))

        carry[...] = lax.fori_loop(0, groups, step, carry[...], unroll=4)
        ga = ga_ref[...]
        ya_ref[...] = (hl_ref[...] * (ga * _sigmoid(ga))).astype(BF16)

        pos = blk * tb + lax.broadcasted_iota(jnp.int32, (tb, POOL_GROUP_DIM), 0)
        ypre = _pool_mix(_pool_diff(eb, pos), pw_ref)
        gb = gb_ref[...]
        yb_ref[...] = ((ypre * ps_ref[...]) * (gb * _sigmoid(gb))).astype(BF16)

    row = lambda i: (i, 0)
    fixed = lambda i: (0, 0)
    in_specs = [pl.BlockSpec((tb, D_MODEL), lambda i: (i, 0)), pl.BlockSpec((tb, D_MODEL), lambda i: (i, 1)),
                pl.BlockSpec((tb, POOL_WIDTH), lambda i: (i, 4)), pl.BlockSpec((tb, POOL_WIDTH), lambda i: (i, 5)),
                ] + _branch_specs(tb, row, fixed)
    return pl.pallas_call(
        body, name="branches_fwd",
        out_shape=(jax.ShapeDtypeStruct((t, D_MODEL), BF16), jax.ShapeDtypeStruct((t, POOL_WIDTH), BF16),
                   jax.ShapeDtypeStruct((t, D_MODEL), F32)),
        grid=(t // tb,), in_specs=in_specs,
        out_specs=(pl.BlockSpec((tb, D_MODEL), row), pl.BlockSpec((tb, POOL_WIDTH), row),
                   pl.BlockSpec((tb, D_MODEL), row)),
        scratch_shapes=[pltpu.VMEM((tb + CONV_HIST, D_MODEL), F32), pltpu.VMEM((tb + POOL_HIST, POOL_WIDTH), F32),
                        pltpu.VMEM((F32_SUBLANES, D_MODEL), F32),
                        pltpu.VMEM((tb, D_MODEL), F32), pltpu.VMEM((tb, D_MODEL), F32)],
        compiler_params=pltpu.CompilerParams(dimension_semantics=("arbitrary",),
                                             vmem_limit_bytes=VMEM_LIMIT_BYTES),
    )(z, z, z, z, *weights)


def _branches_bwd(z, hl, dya, dyb, dzm, weights, seq, tb):
    t = z.shape[0]
    nb = t // tb
    nbe = seq // tb
    groups = tb // F32_SUBLANES
    n_pool = len(POOL_WINDOWS)

    def body(xa_ref, xap_ref, ga_ref, xb_ref, xbp_ref, gb_ref, hl_ref, hlp_ref, dya_ref, dyb_ref, dzm_ref,
             cw_ref, cb_ref, wa_ref, ba_ref, wx_ref, bx_ref, lam_ref, pw_ref, ps_ref,
             dz_ref, dcw_ref, dcb_ref, dwa_ref, dba_ref, dwx_ref, dbx_ref, dlam_ref, dpw_ref, dps_ref,
             xa_ext, xb_ext, hl_ext, a_ext, dxc_ext, dwin_ext, g_carry, b_s, d_s, g_s):
        i = pl.program_id(0)
        blk = (nb - 1 - i) % nbe

        @pl.when(i == 0)
        def _():
            for ref in (dcw_ref, dcb_ref, dwa_ref, dba_ref, dwx_ref, dbx_ref, dlam_ref, dpw_ref, dps_ref):
                ref[...] = jnp.zeros_like(ref)

        @pl.when(blk == nbe - 1)
        def _():
            a_ext[tb:, :] = jnp.zeros((F32_SUBLANES, D_MODEL), F32)
            dxc_ext[tb:, :] = jnp.zeros((CONV_HIST, D_MODEL), F32)
            dwin_ext[tb:, :] = jnp.zeros((POOL_HIST, POOL_WIDTH), F32)
            g_carry[...] = jnp.zeros_like(g_carry)

        live = (blk > 0).astype(F32)
        xa_ext[0:CONV_HIST, :] = xap_ref[...] * live
        xa_ext[CONV_HIST:, :] = xa_ref[...]
        xb_ext[0:POOL_HIST, :] = xbp_ref[...] * live
        xb_ext[POOL_HIST:, :] = xb_ref[...]
        hl_ext[0:F32_SUBLANES, :] = hlp_ref[...] * live
        hl_ext[F32_SUBLANES:, :] = hl_ref[...]
        ea = xa_ext[...]
        eb = xb_ext[...]

        xc = _conv(ea, cw_ref, cb_ref[...])
        lam = lam_ref[...]
        r, ig, a, mult, sp = _lru_gates(xc, wa_ref, ba_ref[...], wx_ref, bx_ref[...], lam)
        hl = hl_ref[...]
        ga = ga_ref[...]
        sga = _sigmoid(ga)
        dya = dya_ref[...]
        dhl = dya * (ga * sga)
        dz_ref[:, D_MODEL:2 * D_MODEL] = (dya * hl * (sga * (1.0 + ga * (1.0 - sga)))).astype(BF16)

        a_ext[0:tb, :] = a
        b = _shift_up(a_ext[...], 1)[0:tb, :]
        a_ext[tb:, :] = jnp.broadcast_to(a[0:1, :], (F32_SUBLANES, D_MODEL))
        d = dhl
        row8 = lax.broadcasted_iota(jnp.int32, (tb, D_MODEL), 0) % F32_SUBLANES
        for s in (1, 2, 4):
            m = row8 < F32_SUBLANES - s
            d = jnp.where(m, d + b * _shift_up(d, s), d)
            b = jnp.where(m, b * _shift_up(b, s), b)
        b_s[...] = b
        d_s[...] = d

        def step(k, cr):
            sl = pl.ds(pl.multiple_of((groups - 1 - k) * F32_SUBLANES, F32_SUBLANES), F32_SUBLANES)
            gb_ = d_s[sl, :] + b_s[sl, :] * cr
            g_s[sl, :] = gb_
            return jnp.broadcast_to(gb_[0:1, :], (F32_SUBLANES, D_MODEL))

        g_carry[...] = lax.fori_loop(0, groups, step, g_carry[...], unroll=4)
        gsc = g_s[...]
        da = gsc * _shift_down(hl_ext[...], 1)[F32_SUBLANES:, :]
        dmult = gsc * (ig * xc)
        dig = gsc * (mult * xc)
        dxc = gsc * (mult * ig)
        dlog_a = da * a - (a * a) * dmult / mult
        dr = dlog_a * (-LRU_C * sp)
        dlam_ref[...] += jnp.sum(dlog_a * (-LRU_C * r), axis=0, keepdims=True)
        dpa = dr * (r * (1.0 - r))
        dpx = dig * (ig * (1.0 - ig))
        dba_ref[...] += jnp.sum(dpa, axis=0, keepdims=True)
        dbx_ref[...] += jnp.sum(dpx, axis=0, keepdims=True)
        back = []
        for h in range(LRU_HEADS):
            cols = slice(h * HEAD_DIM, (h + 1) * HEAD_DIM)
            xh = xc[:, cols].astype(BF16)
            dpa_h = dpa[:, cols].astype(BF16)
            dpx_h = dpx[:, cols].astype(BF16)
            dwa_ref[h] += _dot_tn(xh, dpa_h)
            dwx_ref[h] += _dot_tn(xh, dpx_h)
            back.append(_dot_nt(dpa_h, wa_ref[h]) + _dot_nt(dpx_h, wx_ref[h]))
        dxc = dxc + jnp.concatenate(back, axis=1)
        dcb_ref[...] += jnp.sum(dxc, axis=0, keepdims=True)
        for k in range(CONV_WIDTH):
            tap = _shift_down(ea, CONV_WIDTH - 1 - k)[CONV_HIST:, :] if k < CONV_WIDTH - 1 else ea[CONV_HIST:, :]
            dcw_ref[k:k + 1, :] += jnp.sum(dxc * tap, axis=0, keepdims=True)
        dxc_ext[0:tb, :] = dxc
        ed = dxc_ext[...]
        dxa = ed * cw_ref[3:4, :]
        dxa = dxa + _shift_up(ed, 1) * cw_ref[2:3, :]
        dxa = dxa + _shift_up(ed, 2) * cw_ref[1:2, :]
        dxa = dxa + _shift_up(ed, 3) * cw_ref[0:1, :]
        dz_ref[:, 0:D_MODEL] = dxa[0:tb, :].astype(BF16)
        dxc_ext[tb:, :] = dxc[0:CONV_HIST, :]

        pos = blk * tb + lax.broadcasted_iota(jnp.int32, (tb, POOL_GROUP_DIM), 0)
        diff = _pool_diff(eb, pos)
        ypre = _pool_mix(diff, pw_ref)
        ps = ps_ref[...]
        gb = gb_ref[...]
        sgb = _sigmoid(gb)
        dyb = dyb_ref[...]
        dyp = dyb * (gb * sgb)
        dz_ref[:, 2 * D_MODEL + POOL_WIDTH:3 * D_MODEL] = (
            dyb * (ypre * ps) * (sgb * (1.0 + gb * (1.0 - sgb)))).astype(BF16)
        dps_ref[...] += jnp.sum(dyp * ypre, axis=0, keepdims=True)
        dypre = dyp * ps
        for g, k in enumerate(POOL_WINDOWS):
            cols = slice(g * POOL_GROUP_DIM, (g + 1) * POOL_GROUP_DIM)
            dyg = dypre[:, cols].astype(BF16)
            dpw_ref[g] += _dot_tn(diff[g].astype(BF16), dyg)
            ddiff = _dot_nt(dyg, pw_ref[g])
            count = jnp.minimum(pos + 1, k).astype(F32)
            dwin = ddiff / count
            dwin_ext[0:tb, cols] = dwin
            s = dwin_ext[:, cols]
            for step_ in range(g + 1):
                s = s + _shift_up(s, 2 ** step_)
            dz_ref[:, 2 * D_MODEL + g * POOL_GROUP_DIM:2 * D_MODEL + (g + 1) * POOL_GROUP_DIM] = (
                s[0:tb, :] - ddiff).astype(BF16)
            dwin_ext[tb:, cols] = dwin[0:POOL_HIST, :]

        dz_ref[:, 3 * D_MODEL:] = dzm_ref[...]

        @pl.when(i == nb - 1)
        def _():
            dlam_ref[...] = dlam_ref[...] * (-_sigmoid(-lam))

    rev = lambda i: (nb - 1 - i, 0)
    fixed = lambda i: (0, 0)
    fixed3 = lambda i: (0, 0, 0)

    def prev(rows, col):
        per = tb // rows
        return lambda i: (jnp.maximum((nb - 1 - i) * per - 1, 0), col)

    in_specs = [pl.BlockSpec((tb, D_MODEL), lambda i: (nb - 1 - i, 0)),
                pl.BlockSpec((CONV_HIST, D_MODEL), prev(CONV_HIST, 0)),
                pl.BlockSpec((tb, D_MODEL), lambda i: (nb - 1 - i, 1)),
                pl.BlockSpec((tb, POOL_WIDTH), lambda i: (nb - 1 - i, 4)),
                pl.BlockSpec((POOL_HIST, POOL_WIDTH), prev(POOL_HIST, 4)),
                pl.BlockSpec((tb, POOL_WIDTH), lambda i: (nb - 1 - i, 5)),
                pl.BlockSpec((tb, D_MODEL), rev),
                pl.BlockSpec((F32_SUBLANES, D_MODEL), prev(F32_SUBLANES, 0)),
                pl.BlockSpec((tb, D_MODEL), rev), pl.BlockSpec((tb, POOL_WIDTH), rev),
                pl.BlockSpec((tb, 2 * D_MODEL), rev)] + _branch_specs(tb, rev, fixed)
    out_shape = (jax.ShapeDtypeStruct((t, IN_COLS), BF16),
                 jax.ShapeDtypeStruct((CONV_WIDTH, D_MODEL), F32), jax.ShapeDtypeStruct((1, D_MODEL), F32),
                 jax.ShapeDtypeStruct((LRU_HEADS, HEAD_DIM, HEAD_DIM), F32), jax.ShapeDtypeStruct((1, D_MODEL), F32),
                 jax.ShapeDtypeStruct((LRU_HEADS, HEAD_DIM, HEAD_DIM), F32), jax.ShapeDtypeStruct((1, D_MODEL), F32),
                 jax.ShapeDtypeStruct((1, D_MODEL), F32),
                 jax.ShapeDtypeStruct((n_pool, POOL_GROUP_DIM, POOL_GROUP_DIM), F32),
                 jax.ShapeDtypeStruct((1, POOL_WIDTH), F32))
    out_specs = (pl.BlockSpec((tb, IN_COLS), rev),
                 pl.BlockSpec((CONV_WIDTH, D_MODEL), fixed), pl.BlockSpec((1, D_MODEL), fixed),
                 pl.BlockSpec((LRU_HEADS, HEAD_DIM, HEAD_DIM), fixed3), pl.BlockSpec((1, D_MODEL), fixed),
                 pl.BlockSpec((LRU_HEADS, HEAD_DIM, HEAD_DIM), fixed3), pl.BlockSpec((1, D_MODEL), fixed),
                 pl.BlockSpec((1, D_MODEL), fixed),
                 pl.BlockSpec((n_pool, POOL_GROUP_DIM, POOL_GROUP_DIM), fixed3),
                 pl.BlockSpec((1, POOL_WIDTH), fixed))
    scratch = [pltpu.VMEM((tb + CONV_HIST, D_MODEL), F32), pltpu.VMEM((tb + POOL_HIST, POOL_WIDTH), F32),
               pltpu.VMEM((tb + F32_SUBLANES, D_MODEL), F32), pltpu.VMEM((tb + F32_SUBLANES, D_MODEL), F32),
               pltpu.VMEM((tb + CONV_HIST, D_MODEL), F32), pltpu.VMEM((tb + POOL_HIST, POOL_WIDTH), F32),
               pltpu.VMEM((F32_SUBLANES, D_MODEL), F32),
               pltpu.VMEM((tb, D_MODEL), F32), pltpu.VMEM((tb, D_MODEL), F32), pltpu.VMEM((tb, D_MODEL), F32)]
    return pl.pallas_call(
        body, name="branches_bwd", out_shape=out_shape, grid=(nb,), in_specs=in_specs, out_specs=out_specs,
        scratch_shapes=scratch,
        compiler_params=pltpu.CompilerParams(dimension_semantics=("arbitrary",),
                                             vmem_limit_bytes=VMEM_LIMIT_BYTES),
    )(z, z, z, z, z, z, hl, hl, dya, dyb, dzm, *weights)


def _merge_head(x2d, ya, yb, z, p2d, tgt, w_pl, w_pp, w_out, w_pg, w_pe, g2, gf, tb):
    t = x2d.shape[0]
    p_dim = p2d.shape[1]

    def body(x_ref, ya_ref, yb_ref, ma_ref, mb_ref, p_ref, t_ref, wpl_ref, wpp_ref, wout_ref, wpg_ref, wpe_ref,
             g2_ref, gf_ref,
             loss_ref, dg2_ref, dgf_ref, dxr_ref, dya_ref, dyb_ref, dzm_ref,
             mg_ref, do_ref, hn_ref, dgp_ref, dpe_ref, da_ref, dbm_ref, pbf_ref):
        @pl.when(pl.program_id(0) == 0)
        def _():
            loss_ref[...] = jnp.zeros_like(loss_ref)
            dg2_ref[...] = jnp.zeros_like(dg2_ref)
            dgf_ref[...] = jnp.zeros_like(dgf_ref)

        a_ = _dot(ya_ref[...], wpl_ref[...])
        bm = _dot(yb_ref[...], wpp_ref[...])
        sa = _sigmoid(ma_ref[...])
        sb = _sigmoid(mb_ref[...])
        mg = (sa * a_ + sb * bm).astype(BF16)
        mg_ref[...] = mg
        x1 = x_ref[...] + _dot(mg, wout_ref[...])
        xn2, r2 = _rms(x1)
        g2 = g2_ref[...]
        hn = (xn2 * g2).astype(BF16)
        hn_ref[...] = hn
        gate = _sigmoid(_dot(hn, wpg_ref[...]))
        pbf = p_ref[...].astype(BF16)
        pbf_ref[...] = pbf
        pe = _dot(pbf, wpe_ref[...])
        x2 = x1 + gate * pe
        xn3, r3 = _rms(x2)
        gf = gf_ref[...]
        err = xn3 * gf - t_ref[...]
        loss_ref[...] += 0.5 * jnp.sum(jnp.mean(err * err, axis=-1))

        dy = err * (1.0 / D_MODEL)
        dgf_ref[...] += jnp.sum(dy * xn3, axis=0, keepdims=True)
        dx2 = _rms_bwd(dy * gf, xn3, r3)
        dpe_ref[...] = (dx2 * gate).astype(BF16)
        dgp = ((dx2 * pe) * (gate * (1.0 - gate))).astype(BF16)
        dgp_ref[...] = dgp
        dhn = _dot_nt(dgp, wpg_ref[...])
        dg2_ref[...] += jnp.sum(dhn * xn2, axis=0, keepdims=True)
        dx1 = dx2 + _rms_bwd(dhn * g2, xn2, r2)
        dxr_ref[...] = dx1
        do = dx1.astype(BF16)
        do_ref[...] = do
        dmg = _dot_nt(do, wout_ref[...])
        da = (dmg * sa).astype(BF16)
        dbm = (dmg * sb).astype(BF16)
        da_ref[...] = da
        dbm_ref[...] = dbm
        dzm_ref[:, 0:D_MODEL] = (dmg * a_ * (sa * (1.0 - sa))).astype(BF16)
        dzm_ref[:, D_MODEL:] = (dmg * bm * (sb * (1.0 - sb))).astype(BF16)
        dya_ref[...] = _dot_nt(da, wpl_ref[...])
        dyb_ref[...] = _dot_nt(dbm, wpp_ref[...])

    row = lambda i: (i, 0)
    fixed = lambda i: (0, 0)

    def resident(shape):
        return pl.BlockSpec(shape, fixed, pipeline_mode=pl.Buffered(1))

    tok = lambda width: pl.BlockSpec((tb, width), row)
    in_specs = [tok(D_MODEL), tok(D_MODEL), tok(POOL_WIDTH),
                pl.BlockSpec((tb, D_MODEL), lambda i: (i, 3)), pl.BlockSpec((tb, D_MODEL), lambda i: (i, 4)),
                tok(p_dim), tok(D_MODEL),
                resident((D_MODEL, D_MODEL)), resident((POOL_WIDTH, D_MODEL)), resident((D_MODEL, D_MODEL)),
                resident((D_MODEL, D_MODEL)), resident((p_dim, D_MODEL)),
                pl.BlockSpec((1, D_MODEL), fixed), pl.BlockSpec((1, D_MODEL), fixed)]
    bf = lambda width: jax.ShapeDtypeStruct((t, width), BF16)
    f32 = lambda width: jax.ShapeDtypeStruct((t, width), F32)
    out_shape = (jax.ShapeDtypeStruct((F32_SUBLANES, 128), F32), jax.ShapeDtypeStruct((1, D_MODEL), F32),
                 jax.ShapeDtypeStruct((1, D_MODEL), F32),
                 f32(D_MODEL), f32(D_MODEL), f32(POOL_WIDTH), bf(2 * D_MODEL),
                 bf(D_MODEL), bf(D_MODEL), bf(D_MODEL), bf(D_MODEL), bf(D_MODEL), bf(D_MODEL), bf(D_MODEL), bf(p_dim))
    out_specs = (pl.BlockSpec((F32_SUBLANES, 128), fixed), pl.BlockSpec((1, D_MODEL), fixed),
                 pl.BlockSpec((1, D_MODEL), fixed),
                 tok(D_MODEL), tok(D_MODEL), tok(POOL_WIDTH), tok(2 * D_MODEL),
                 tok(D_MODEL), tok(D_MODEL), tok(D_MODEL), tok(D_MODEL), tok(D_MODEL), tok(D_MODEL), tok(D_MODEL),
                 tok(p_dim))
    return pl.pallas_call(
        body, name="merge_head", out_shape=out_shape, grid=(t // tb,), in_specs=in_specs, out_specs=out_specs,
        compiler_params=pltpu.CompilerParams(dimension_semantics=("arbitrary",),
                                             vmem_limit_bytes=VMEM_LIMIT_BYTES),
    )(x2d, ya, yb, z, z, p2d, tgt, w_pl, w_pp, w_out, w_pg, w_pe, g2, gf)


def _pad_rows(a, rows):
    return jnp.pad(a, ((0, rows - a.shape[0]), (0, D_MODEL - a.shape[1])))


def _pack_bag(parts):
    rows = [_pad_rows(a.reshape(-1, a.shape[-1]) if a.shape[-1] != HEAD_DIM else a.reshape(-1, D_MODEL), n)
            for a, n in zip(parts, BAG_PART_ROWS)]
    rows.append(jnp.zeros((BAG_ROWS - sum(BAG_PART_ROWS), D_MODEL), F32))
    return jnp.concatenate(rows, axis=0)


def _unpack_bag(bag, shapes):
    out, at = [], 0
    for shape, n in zip(shapes, BAG_PART_ROWS):
        size = 1
        for s in shape:
            size *= s
        if size % D_MODEL == 0:
            piece = bag[at:at + size // D_MODEL, :]
        else:
            piece = bag[at:at + 1, :size]
        out.append(piece.reshape(shape))
        at += n
    return out


def kernel(x, p, norm_g, w_in, conv_w, conv_b, lru_w_a, lru_b_a, lru_w_x, lru_b_x, lru_lambda, pool_w, pool_scale, w_proj_lru, w_proj_pool, w_out, ple_norm_g, w_ple_gate, w_ple_proj, final_g, loss_target, m_norm_g, m_w_in, m_conv_w, m_conv_b, m_lru_w_a, m_lru_b_a, m_lru_w_x, m_lru_b_x, m_lru_lambda, m_pool_w, m_pool_scale, m_w_proj_lru, m_w_proj_pool, m_w_out, m_ple_norm_g, m_w_ple_gate, m_w_ple_proj, m_final_g, v_norm_g, v_w_in, v_conv_w, v_conv_b, v_lru_w_a, v_lru_b_a, v_lru_w_x, v_lru_b_x, v_lru_lambda, v_pool_w, v_pool_scale, v_w_proj_lru, v_w_proj_pool, v_w_out, v_ple_norm_g, v_w_ple_gate, v_w_ple_proj, v_final_g):
    bsz, seq, _ = x.shape
    t = bsz * seq
    tb_mm = min(512, seq)
    tb_seq = min(256, seq // 2) if seq >= 512 else seq
    x2d = x.reshape(t, D_MODEL)
    p2d = p.reshape(t, p.shape[-1])
    tgt = loss_target.reshape(t, D_MODEL)
    chip = 2 * lax.axis_index("x") + lax.axis_index("y")

    big = [(w_in[0], 1), (w_proj_lru[0], 0), (w_proj_pool[0], 1), (w_out[0], 0), (w_ple_gate[0], 0), (w_ple_proj[0], 1)]
    w_in_f, w_pl_f, w_pp_f, w_out_f, w_pg_f, w_pe_f, conv_w_f = _gather_shards(
        [(w.astype(BF16), axis, True) for w, axis in big] + [(conv_w[0], 1, False)], "gather_weights")

    wa_bf = lru_w_a[0].astype(BF16)
    wx_bf = lru_w_x[0].astype(BF16)
    pw_bf = pool_w[0].astype(BF16)
    branch_w = (conv_w_f, conv_b, wa_bf, lru_b_a.reshape(1, D_MODEL), wx_bf, lru_b_x.reshape(1, D_MODEL),
                lru_lambda, pw_bf, pool_scale)

    z = _in_proj(x2d, norm_g, w_in_f, tb_mm)
    ya, yb, hl = _branches_fwd(z, branch_w, seq, tb_seq)
    (loss_acc, d_g2, d_gf, dx_res, dya, dyb, dzm, mg_bf, do_bf, hn_bf, dgp_bf, dpe_bf, da_bf, dbm_bf, p_bf) = _merge_head(
        x2d, ya, yb, z, p2d, tgt, w_pl_f, w_pp_f, w_out_f, w_pg_f, w_pe_f, ple_norm_g, final_g.reshape(1, D_MODEL),
        tb_seq)
    (dz, d_cw, d_cb, d_wa, d_ba, d_wx, d_bx, d_lam, d_pw, d_ps) = _branches_bwd(
        z, hl, dya, dyb, dzm, branch_w, seq, tb_seq)
    dx, h_bf, d_g1 = _in_proj_bwd(dz, w_in_f, x2d, dx_res, norm_g, tb_mm)

    g_in = _weight_grad(h_bf, dz, N_CHIPS, tb_mm, "dw_in").reshape(8, D_MODEL // 2, IN_COLS // N_CHIPS)
    g_pl = _weight_grad(ya, da_bf, 1, tb_mm, "dw_proj_lru").reshape(8, D_MODEL // 8, D_MODEL)
    g_pp = _weight_grad(yb, dbm_bf, N_CHIPS, tb_mm, "dw_proj_pool").reshape(8, POOL_WIDTH // 2, D_MODEL // N_CHIPS)
    g_out = _weight_grad(mg_bf, do_bf, 1, tb_mm, "dw_out").reshape(8, D_MODEL // 8, D_MODEL)
    g_pg = _weight_grad(hn_bf, dgp_bf, 1, tb_mm, "dw_ple_gate").reshape(8, D_MODEL // 8, D_MODEL)
    p_dim = p2d.shape[1]
    g_pe = _weight_grad(p_bf, dpe_bf, N_CHIPS, tb_mm, "dw_ple_proj").reshape(8, p_dim // 2, D_MODEL // N_CHIPS)

    (r_in,) = _reduce_scatter([g_in], "rs_w_in", BF16)
    r_pl, r_pp, r_out, r_pg, r_pe = _reduce_scatter([g_pl, g_pp, g_out, g_pg, g_pe], "rs_proj", BF16)
    small_shapes = [(1, D_MODEL), (1, CONV_WIDTH, D_MODEL), (1, D_MODEL), lru_w_a.shape, lru_b_a.shape, lru_w_x.shape,
                    lru_b_x.shape, (1, D_MODEL), pool_w.shape, pool_scale.shape, (1, D_MODEL), final_g.shape]
    bag = _pack_bag([d_g1, d_cw, d_cb, d_wa, d_ba.reshape(1, D_MODEL), d_wx, d_bx.reshape(1, D_MODEL), d_lam, d_pw,
                     d_ps, d_g2, d_gf])
    (bag_mine,) = _reduce_scatter([bag.reshape(8, BAG_ROWS // 8, D_MODEL)], "rs_small")
    (bag_sum,) = _gather_shards([(bag_mine.reshape(BAG_ROWS // N_CHIPS, D_MODEL), 0, True)], "gather_small")
    (g_g1, g_cw_full, g_cb, g_wa, g_ba, g_wx, g_bx, g_lam, g_pw, g_ps, g_g2, g_gf) = _unpack_bag(bag_sum, small_shapes)
    cw_cols = D_MODEL // N_CHIPS
    g_cw = lax.dynamic_slice_in_dim(g_cw_full, chip * cw_cols, cw_cols, axis=2)

    def big_update(w, g2d, m, v, rows, name):
        d, nm, nv = _adamw(w[0], g2d, m[0], v[0], rows, name)
        return g2d[None], d[None], nm[None], nv[None]

    u_in = big_update(w_in, r_in.reshape(D_MODEL, IN_COLS // N_CHIPS), m_w_in, v_w_in, 256, "adamw_w_in")
    u_pl = big_update(w_proj_lru, r_pl.reshape(D_MODEL // N_CHIPS, D_MODEL), m_w_proj_lru, v_w_proj_lru, 256, "adamw_w_proj_lru")
    u_pp = big_update(w_proj_pool, r_pp.reshape(POOL_WIDTH, D_MODEL // N_CHIPS), m_w_proj_pool, v_w_proj_pool, 512, "adamw_w_proj_pool")
    u_out = big_update(w_out, r_out.reshape(D_MODEL // N_CHIPS, D_MODEL), m_w_out, v_w_out, 256, "adamw_w_out")
    u_pg = big_update(w_ple_gate, r_pg.reshape(D_MODEL // N_CHIPS, D_MODEL), m_w_ple_gate, v_w_ple_gate, 256, "adamw_w_ple_gate")
    u_pe = big_update(w_ple_proj, r_pe.reshape(p_dim, D_MODEL // N_CHIPS), m_w_ple_proj, v_w_ple_proj, 256, "adamw_w_ple_proj")
    u_cw = big_update(conv_w, g_cw[0], m_conv_w, v_conv_w, CONV_WIDTH, "adamw_conv_w")

    small_w = [norm_g, None, conv_b, lru_w_a, lru_b_a, lru_w_x, lru_b_x, lru_lambda, pool_w, pool_scale, ple_norm_g, final_g]
    small_m = [m_norm_g, None, m_conv_b, m_lru_w_a, m_lru_b_a, m_lru_w_x, m_lru_b_x, m_lru_lambda, m_pool_w, m_pool_scale, m_ple_norm_g, m_final_g]
    small_v = [v_norm_g, None, v_conv_b, v_lru_w_a, v_lru_b_a, v_lru_w_x, v_lru_b_x, v_lru_lambda, v_pool_w, v_pool_scale, v_ple_norm_g, v_final_g]
    fill = jnp.zeros((CONV_WIDTH, D_MODEL), F32)

    def bag_of(arrs):
        return _pack_bag([fill if a is None else (a[0] if a.ndim > 1 else a[None]) for a in arrs])

    d_bag, m_bag, v_bag = _adamw(bag_of(small_w), bag_sum, bag_of(small_m), bag_of(small_v), BAG_ROWS // 8, "adamw_small")
    d_small = _unpack_bag(d_bag, small_shapes)
    m_small = _unpack_bag(m_bag, small_shapes)
    v_small = _unpack_bag(v_bag, small_shapes)

    loss = lax.psum(loss_acc[0, 0], ALL_AXES)
    grad_x = dx.reshape(bsz, seq, D_MODEL)

    def ordered(small, pick):
        s = list(small)
        return [s[0], u_in[pick], u_cw[pick], s[2], s[3], s[4], s[5], s[6], s[7], s[8], s[9],
                u_pl[pick], u_pp[pick], u_out[pick], s[10], u_pg[pick], u_pe[pick], s[11]]

    grads = ordered([g_g1, None, g_cb, g_wa, g_ba, g_wx, g_bx, g_lam, g_pw, g_ps, g_g2, g_gf], 0)
    return (loss, grad_x, *grads, *ordered(d_small, 1), *ordered(m_small, 2), *ordered(v_small, 3))
```

```python
import functools

import jax
import jax.numpy as jnp
from jax import lax
from jax.experimental import pallas as pl
from jax.experimental.pallas import tpu as pltpu

F32 = jnp.float32
BF16 = jnp.bfloat16
MESH = pl.DeviceIdType.MESH
ALL_AXES = ("x", "y", "c")

D_MODEL = 1024
LRU_HEADS = 8
HEAD_DIM = 128
CONV_WIDTH = 4
LRU_C = 8.0
POOL_WIDTH = 512
POOL_WINDOWS = (2, 4, 8, 16)
POOL_GROUP_DIM = 128
IN_COLS = 5120
N_CHIPS = 4
EPS = 1e-6

ADAM_LR = 0.001
ADAM_B1 = 0.9
ADAM_B2 = 0.999
ADAM_EPS = 1e-08
ADAM_WD = 0.01
ADAM_STEP = 10

F32_SUBLANES = 8
CONV_HIST = 8
POOL_HIST = 16
VMEM_LIMIT_BYTES = 58 * 1024 * 1024
BAG_PART_ROWS = (8, 8, 8, 128, 8, 128, 8, 8, 64, 8, 8, 8)
BAG_ROWS = 448


def _dot(a, b):
    return jnp.dot(a, b, preferred_element_type=F32)


def _dot_nt(a, b):
    return lax.dot_general(a, b, (((1,), (1,)), ((), ())), preferred_element_type=F32)


def _dot_tn(a, b):
    return lax.dot_general(a, b, (((0,), (0,)), ((), ())), preferred_element_type=F32)


def _sigmoid(v):
    return jax.nn.sigmoid(v)


def _softplus(v):
    return jnp.maximum(v, 0.0) + jnp.log1p(jnp.exp(-jnp.abs(v)))


def _place():
    return lax.axis_index("x"), lax.axis_index("y"), lax.axis_index("c")


def _gather_shards(shards, name):
    n = len(shards)
    n_sem = 6

    def body(*refs):
        ins, outs = refs[:n], refs[n:2 * n]
        send_sems, recv_sems, local_sems = refs[2 * n:]
        x, y, c = _place()
        chips = [(1 - x, y), (x, 1 - y), (1 - x, 1 - y)]

        def region(k, cx, cy, half):
            (r, cols), axis = shards[k][0].shape, shards[k][1]
            j = 2 * cx + cy
            if axis == 0:
                if half is None:
                    return outs[k].at[pl.ds(j * r, r), :]
                return outs[k].at[pl.ds(j * r + half * (r // 2), r // 2), :]
            if half is None:
                return outs[k].at[:, pl.ds(j * cols, cols)]
            return outs[k].at[pl.ds(half * (r // 2), r // 2), pl.ds(j * cols, cols)]

        def remote(k, sem, block, to, src=None):
            dst = region(k, *block)
            return pltpu.make_async_remote_copy(
                src_ref=dst if src is None else src, dst_ref=dst,
                send_sem=send_sems.at[k * n_sem + sem], recv_sem=recv_sems.at[k * n_sem + sem],
                device_id=to, device_id_type=MESH)

        mine = [pltpu.make_async_copy(ins[k], region(k, x, y, None), local_sems.at[k]) for k in range(n)]
        for cp in mine:
            cp.start()
        sends = []
        for k in range(n):
            r = shards[k][0].shape[0]
            split = shards[k][2]
            src = ins[k].at[pl.ds(c * (r // 2), r // 2), :] if split else ins[k]
            for idx, chip in enumerate(chips):
                sends.append(remote(k, idx, (x, y, c if split else None), (*chip, c), src=src))
        for cp in sends:
            cp.start()
        for k in range(n):
            split = shards[k][2]
            for idx, chip in enumerate(chips):
                remote(k, idx, (*chip, c if split else None), (x, y, c)).wait_recv()
                if split:
                    fwd = remote(k, 3 + idx, (*chip, c), (x, y, 1 - c))
                    fwd.start()
                    sends.append(fwd)
        for k in range(n):
            if shards[k][2]:
                for idx, chip in enumerate(chips):
                    remote(k, 3 + idx, (*chip, 1 - c), (x, y, c)).wait_recv()
        for cp in sends:
            cp.wait_send()
        for cp in mine:
            cp.wait()

    out_shape = []
    for arr, axis, _ in shards:
        r, cols = arr.shape
        full = (N_CHIPS * r, cols) if axis == 0 else (r, N_CHIPS * cols)
        out_shape.append(jax.ShapeDtypeStruct(full, arr.dtype))
    any_spec = pl.BlockSpec(memory_space=pl.ANY)
    return pl.pallas_call(
        body, name=name, out_shape=tuple(out_shape),
        in_specs=[any_spec] * n, out_specs=tuple([any_spec] * n),
        scratch_shapes=[pltpu.SemaphoreType.DMA((n * n_sem,)), pltpu.SemaphoreType.DMA((n * n_sem,)),
                        pltpu.SemaphoreType.DMA((n,))],
    )(*[s[0] for s in shards])


RS_ADD_ROWS = (64, 56, 32, 16, 8)


def _reduce_scatter(parts, name, wire=F32):
    n = len(parts)
    n_sem = 8

    def body(*refs):
        ins, outs = refs[:n], refs[n:2 * n]
        own = refs[2 * n:3 * n]
        sib = refs[3 * n:4 * n]
        got = refs[4 * n:5 * n]
        fin = refs[5 * n:6 * n]
        snd = refs[6 * n:7 * n]
        send_sems, recv_sems, local_sems = refs[7 * n:]
        x, y, c = _place()
        j_me = 2 * x + y
        chips = [(1 - x, y), (x, 1 - y), (1 - x, 1 - y)]

        def remote(a, sem, src, dst, to):
            return pltpu.make_async_remote_copy(
                src_ref=src, dst_ref=dst, send_sem=send_sems.at[a * n_sem + sem],
                recv_sem=recv_sems.at[a * n_sem + sem], device_id=to, device_id_type=MESH)

        def rows_loop(a, fn):
            r = parts[a].shape[1]
            step = max(s for s in RS_ADD_ROWS if r % s == 0)

            def it(i, carry):
                fn(pl.ds(pl.multiple_of(i * step, step), step))
                return carry

            lax.fori_loop(0, r // step, it, 0)

        loads, sends = [], []
        for a in range(n):
            for jj in range(N_CHIPS):
                cp = pltpu.make_async_copy(ins[a].at[2 * jj + c], own[a].at[jj], local_sems.at[a * 5 + jj])
                cp.start()
                loads.append(cp)
                sd = remote(a, jj, ins[a].at[2 * jj + (1 - c)], sib[a].at[jj], (x, y, 1 - c))
                sd.start()
                sends.append(sd)
        for a in range(n):
            for jj in range(N_CHIPS):
                loads[a * N_CHIPS + jj].wait()
                remote(a, jj, sib[a].at[jj], sib[a].at[jj], (x, y, c)).wait_recv()

                def add(sl, a=a, jj=jj):
                    q = own[a][jj, sl, :] + sib[a][jj, sl, :]
                    own[a][jj, sl, :] = q
                    snd[a][jj, sl, :] = q.astype(wire)

                rows_loop(a, add)
        for a in range(n):
            for idx, chip in enumerate(chips):
                sd = remote(a, 4 + idx, snd[a].at[2 * chip[0] + chip[1]], got[a].at[j_me], (*chip, c))
                sd.start()
                sends.append(sd)
        for a in range(n):
            def keep(sl, a=a):
                got[a][j_me, sl, :] = snd[a][j_me, sl, :]

            rows_loop(a, keep)
        for a in range(n):
            for idx, chip in enumerate(chips):
                slot = got[a].at[2 * chip[0] + chip[1]]
                remote(a, 4 + idx, slot, slot, (x, y, c)).wait_recv()

            def total(sl, a=a):
                mine = own[a][j_me, sl, :]
                term = [jnp.where(j_me == jj, mine, got[a][jj, sl, :].astype(F32)) for jj in range(N_CHIPS)]
                fin[a][sl, :] = ((term[0] + term[1]) + term[2]) + term[3]

            rows_loop(a, total)
        stores = []
        for a in range(n):
            st = pltpu.make_async_copy(fin[a], outs[a].at[c], local_sems.at[a * 5 + 4])
            st.start()
            stores.append(st)
            sd = remote(a, 7, fin[a], outs[a].at[c], (x, y, 1 - c))
            sd.start()
            sends.append(sd)
        for a in range(n):
            remote(a, 7, outs[a].at[1 - c], outs[a].at[1 - c], (x, y, c)).wait_recv()
        for cp in sends:
            cp.wait_send()
        for cp in stores:
            cp.wait()

    any_spec = pl.BlockSpec(memory_space=pl.ANY)
    scratch = []
    for lead, dtype in ((N_CHIPS, F32), (N_CHIPS, F32), (N_CHIPS, wire), (None, F32), (N_CHIPS, wire)):
        for p in parts:
            shape = p.shape[1:] if lead is None else (lead,) + p.shape[1:]
            scratch.append(pltpu.VMEM(shape, dtype))
    scratch += [pltpu.SemaphoreType.DMA((n * n_sem,)), pltpu.SemaphoreType.DMA((n * n_sem,)),
                pltpu.SemaphoreType.DMA((n * 5,))]
    return pl.pallas_call(
        body, name=name,
        out_shape=tuple(jax.ShapeDtypeStruct((2,) + p.shape[1:], F32) for p in parts),
        in_specs=[any_spec] * n, out_specs=tuple([any_spec] * n), scratch_shapes=scratch,
        compiler_params=pltpu.CompilerParams(vmem_limit_bytes=VMEM_LIMIT_BYTES),
    )(*parts)


def _rms(x):
    r = lax.rsqrt(jnp.mean(x * x, axis=-1, keepdims=True) + EPS)
    return x * r, r


def _rms_bwd(dxn, xn, r):
    return r * (dxn - xn * jnp.mean(dxn * xn, axis=-1, keepdims=True))


def _in_proj_gather(x2d, norm_g, w_in_sh, shards, tb):
    t = x2d.shape[0]
    nb = t // tb
    cols = IN_COLS // N_CHIPS
    half = D_MODEL // 2
    n = len(shards)
    n_sem = 6

    def body(x_ref, g_ref, win_ref, *refs):
        ins = refs[:n]
        z_ref, wfull_ref = refs[n], refs[n + 1]
        outs = refs[n + 2:2 * n + 2]
        wv, send_sems, recv_sems, local_sems, w_send, w_recv, w_local = refs[2 * n + 2:]
        s, i = pl.program_id(0), pl.program_id(1)
        x, y, c = _place()
        me, sibling = (x, y, c), (x, y, 1 - c)
        chips = [(x, 1 - y), (1 - x, y), (1 - x, 1 - y)]

        def w_half(cx, cy, hc):
            return wv.at[2 * cx + cy, pl.ds(hc * half, half), :]

        def w_remote(sem, block, to, src=None):
            dst = w_half(*block)
            return pltpu.make_async_remote_copy(
                src_ref=dst if src is None else src, dst_ref=dst, send_sem=w_send.at[sem],
                recv_sem=w_recv.at[sem], device_id=to, device_id_type=MESH)

        def w_first(idx):
            return w_remote(idx, (x, y, c), (*chips[idx], c), src=win_ref.at[pl.ds(c * half, half), :])

        def w_pass(idx):
            return w_remote(3 + idx, (*chips[idx], c), sibling)

        def w_store(k, cx, cy):
            jj = 2 * cx + cy
            return pltpu.make_async_copy(wv.at[jj], wfull_ref.at[:, pl.ds(jj * cols, cols)], w_local.at[k])

        def region(k, cx, cy, hc):
            (r, cl), axis = shards[k][0].shape, shards[k][1]
            j = 2 * cx + cy
            if axis == 0:
                if hc is None:
                    return outs[k].at[pl.ds(j * r, r), :]
                return outs[k].at[pl.ds(j * r + hc * (r // 2), r // 2), :]
            if hc is None:
                return outs[k].at[:, pl.ds(j * cl, cl)]
            return outs[k].at[pl.ds(hc * (r // 2), r // 2), pl.ds(j * cl, cl)]

        def remote(k, sem, block, to, src=None):
            dst = region(k, *block)
            return pltpu.make_async_remote_copy(
                src_ref=dst if src is None else src, dst_ref=dst,
                send_sem=send_sems.at[k * n_sem + sem], recv_sem=recv_sems.at[k * n_sem + sem],
                device_id=to, device_id_type=MESH)

        def first(k, idx):
            r, split = shards[k][0].shape[0], shards[k][2]
            src = ins[k].at[pl.ds(c * (r // 2), r // 2), :] if split else ins[k]
            return remote(k, idx, (x, y, c if split else None), (*chips[idx], c), src=src)

        def passed(k, idx):
            return remote(k, 3 + idx, (*chips[idx], c), sibling)

        def mine(k):
            return pltpu.make_async_copy(ins[k], region(k, x, y, None), local_sems.at[k])

        own = pltpu.make_async_copy(win_ref, wv.at[2 * x + y], w_local.at[4])

        @pl.when((s == 0) & (i == 0))
        def _():
            own.start()
            for idx in range(3):
                w_first(idx).start()
            for k in range(n):
                mine(k).start()
                for idx in range(3):
                    first(k, idx).start()
            own.wait()
            w_store(0, x, y).start()

        for idx in range(3):
            @pl.when((s == idx + 1) & (i == 0))
            def _(idx=idx):
                w_remote(idx, (*chips[idx], c), me).wait_recv()
                w_pass(idx).start()
                w_remote(3 + idx, (*chips[idx], 1 - c), me).wait_recv()
                w_store(idx + 1, *chips[idx]).start()

        xn, _ = _rms(x_ref[...])
        z_ref[...] = _dot((xn * g_ref[...]).astype(BF16), wv[jnp.bitwise_xor(2 * x + y, s)])

        @pl.when((s == N_CHIPS - 1) & (i == nb - 1))
        def _():
            for k in range(n):
                split = shards[k][2]
                for idx in range(3):
                    remote(k, idx, (*chips[idx], c if split else None), me).wait_recv()
                    if split:
                        passed(k, idx).start()
            for k in range(n):
                if shards[k][2]:
                    for idx in range(3):
                        remote(k, 3 + idx, (*chips[idx], 1 - c), me).wait_recv()
            for idx in range(3):
                w_first(idx).wait_send()
                w_pass(idx).wait_send()
            for k in range(n):
                for idx in range(3):
                    first(k, idx).wait_send()
                    if shards[k][2]:
                        passed(k, idx).wait_send()
                mine(k).wait()
            w_store(0, x, y).wait()
            for idx in range(3):
                w_store(idx + 1, *chips[idx]).wait()

    out_shape = [jax.ShapeDtypeStruct((t, IN_COLS), F32), jax.ShapeDtypeStruct((D_MODEL, IN_COLS), BF16)]
    for arr, axis, _ in shards:
        r, cl = arr.shape
        out_shape.append(jax.ShapeDtypeStruct((N_CHIPS * r, cl) if axis == 0 else (r, N_CHIPS * cl), arr.dtype))
    any_spec = pl.BlockSpec(memory_space=pl.ANY)

    def z_map(s, i):
        return (i, jnp.bitwise_xor(2 * lax.axis_index("x") + lax.axis_index("y"), s))

    return pl.pallas_call(
        body, name="in_proj", out_shape=tuple(out_shape),
        grid=(N_CHIPS, nb),
        in_specs=[pl.BlockSpec((tb, D_MODEL), lambda s, i: (i, 0)),
                  pl.BlockSpec((1, D_MODEL), lambda s, i: (0, 0)), any_spec] + [any_spec] * n,
        out_specs=tuple([pl.BlockSpec((tb, cols), z_map), any_spec] + [any_spec] * n),
        scratch_shapes=[pltpu.VMEM((N_CHIPS, D_MODEL, cols), BF16),
                        pltpu.SemaphoreType.DMA((n * n_sem,)), pltpu.SemaphoreType.DMA((n * n_sem,)),
                        pltpu.SemaphoreType.DMA((n,)),
                        pltpu.SemaphoreType.DMA((6,)), pltpu.SemaphoreType.DMA((6,)), pltpu.SemaphoreType.DMA((5,))],
        compiler_params=pltpu.CompilerParams(dimension_semantics=("arbitrary", "arbitrary"),
                                             vmem_limit_bytes=VMEM_LIMIT_BYTES),
    )(x2d, norm_g, w_in_sh, *[sh[0] for sh in shards])


def _in_proj_bwd(dz, w_in, x2d, dx_res, norm_g, tb):
    t = x2d.shape[0]

    def body(dz_ref, w_ref, x_ref, dres_ref, g_ref, dx_ref, h_ref, dg_ref):
        @pl.when(pl.program_id(0) == 0)
        def _():
            dg_ref[...] = jnp.zeros_like(dg_ref)

        xn, r = _rms(x_ref[...])
        g = g_ref[...]
        h_ref[...] = (xn * g).astype(BF16)
        dh = _dot_nt(dz_ref[...], w_ref[...])
        dg_ref[...] += jnp.sum(dh * xn, axis=0, keepdims=True)
        dx_ref[...] = dres_ref[...] + _rms_bwd(dh * g, xn, r)

    row = lambda i: (i, 0)
    fixed = lambda i: (0, 0)
    return pl.pallas_call(
        body, name="in_proj_bwd",
        out_shape=(jax.ShapeDtypeStruct((t, D_MODEL), F32), jax.ShapeDtypeStruct((t, D_MODEL), BF16),
                   jax.ShapeDtypeStruct((1, D_MODEL), F32)),
        grid=(t // tb,),
        in_specs=[pl.BlockSpec((tb, IN_COLS), row),
                  pl.BlockSpec((D_MODEL, IN_COLS), fixed, pipeline_mode=pl.Buffered(1)),
                  pl.BlockSpec((tb, D_MODEL), row), pl.BlockSpec((tb, D_MODEL), row),
                  pl.BlockSpec((1, D_MODEL), fixed)],
        out_specs=(pl.BlockSpec((tb, D_MODEL), row), pl.BlockSpec((tb, D_MODEL), row),
                   pl.BlockSpec((1, D_MODEL), fixed)),
        compiler_params=pltpu.CompilerParams(dimension_semantics=("arbitrary",),
                                             vmem_limit_bytes=VMEM_LIMIT_BYTES),
    )(dz, w_in, x2d, dx_res, norm_g)


def _weight_grad(lhs, rhs, n_chunks, tb, name):
    t, k = lhs.shape
    nc = rhs.shape[1] // n_chunks

    def body(l_ref, r_ref, o_ref):
        @pl.when(pl.program_id(1) == 0)
        def _():
            o_ref[...] = jnp.zeros_like(o_ref)

        o_ref[...] += _dot_tn(l_ref[...], r_ref[...])

    return pl.pallas_call(
        body, name=name, out_shape=jax.ShapeDtypeStruct((n_chunks, k, nc), F32),
        grid=(n_chunks, t // tb),
        in_specs=[pl.BlockSpec((tb, k), lambda j, i: (i, 0)), pl.BlockSpec((tb, nc), lambda j, i: (i, j))],
        out_specs=pl.BlockSpec((None, k, nc), lambda j, i: (j, 0, 0)),
        compiler_params=pltpu.CompilerParams(dimension_semantics=("arbitrary", "arbitrary"),
                                             vmem_limit_bytes=VMEM_LIMIT_BYTES),
    )(lhs, rhs)


def _adamw(w, g, m, v, rows, name):
    r, c = w.shape

    def body(w_ref, g_ref, m_ref, v_ref, d_ref, nm_ref, nv_ref):
        g_ = g_ref[...]
        m_ = ADAM_B1 * m_ref[...] + (1.0 - ADAM_B1) * g_
        v_ = ADAM_B2 * v_ref[...] + (1.0 - ADAM_B2) * jnp.square(g_)
        m_hat = m_ / (1.0 - ADAM_B1 ** ADAM_STEP)
        v_hat = v_ / (1.0 - ADAM_B2 ** ADAM_STEP)
        d_ref[...] = -ADAM_LR * (m_hat / (jnp.sqrt(v_hat) + ADAM_EPS) + ADAM_WD * w_ref[...])
        nm_ref[...] = m_
        nv_ref[...] = v_

    spec = pl.BlockSpec((rows, c), lambda i: (i, 0))
    return pl.pallas_call(
        body, name=name, out_shape=tuple(jax.ShapeDtypeStruct((r, c), F32) for _ in range(3)),
        grid=(r // rows,), in_specs=[spec] * 4, out_specs=(spec,) * 3,
        compiler_params=pltpu.CompilerParams(dimension_semantics=("arbitrary",),
                                             vmem_limit_bytes=VMEM_LIMIT_BYTES),
    )(w, g, m, v)


def _shift_down(ext, s):
    return pltpu.roll(ext, s, 0)


def _shift_up(ext, s):
    return pltpu.roll(ext, ext.shape[0] - s, 0)


def _lru_gates(xc, wa_ref, ba, wx_ref, bx, lam):
    pa, px = [], []
    for h in range(LRU_HEADS):
        xh = xc[:, h * HEAD_DIM:(h + 1) * HEAD_DIM].astype(BF16)
        pa.append(_dot(xh, wa_ref[h]))
        px.append(_dot(xh, wx_ref[h]))
    r = _sigmoid(jnp.concatenate(pa, axis=1) + ba)
    ig = _sigmoid(jnp.concatenate(px, axis=1) + bx)
    sp = _softplus(-lam)
    log_a = (-LRU_C * r) * sp
    a = jnp.exp(log_a)
    mult = jnp.sqrt(jnp.tanh(-log_a) * (1.0 + a * a))
    return r, ig, a, mult, sp


def _conv(ext, w_ref, b):
    y = b + _shift_down(ext, 3) * w_ref[0:1, :]
    y = y + _shift_down(ext, 2) * w_ref[1:2, :]
    y = y + _shift_down(ext, 1) * w_ref[2:3, :]
    y = y + ext * w_ref[3:4, :]
    return y[CONV_HIST:, :]


def _pool_diff(ext, pos):
    out = []
    for g, k in enumerate(POOL_WINDOWS):
        col = ext[:, g * POOL_GROUP_DIM:(g + 1) * POOL_GROUP_DIM]
        s = col
        for step in range(g + 1):
            s = s + _shift_down(s, 2 ** step)
        count = jnp.minimum(pos + 1, k).astype(F32)
        out.append(s[POOL_HIST:, :] / count - col[POOL_HIST:, :])
    return out


def _pool_mix(diff, pw_ref):
    return jnp.concatenate([_dot(diff[g].astype(BF16), pw_ref[g]) for g in range(len(POOL_WINDOWS))], axis=1)


def _branch_specs(tb, row_map, fixed):
    fixed3 = lambda i: (0, 0, 0)
    return [pl.BlockSpec((CONV_WIDTH, D_MODEL), fixed), pl.BlockSpec((1, D_MODEL), fixed),
            pl.BlockSpec((LRU_HEADS, HEAD_DIM, HEAD_DIM), fixed3), pl.BlockSpec((1, D_MODEL), fixed),
            pl.BlockSpec((LRU_HEADS, HEAD_DIM, HEAD_DIM), fixed3), pl.BlockSpec((1, D_MODEL), fixed),
            pl.BlockSpec((1, D_MODEL), fixed),
            pl.BlockSpec((len(POOL_WINDOWS), POOL_GROUP_DIM, POOL_GROUP_DIM), fixed3),
            pl.BlockSpec((1, POOL_WIDTH), fixed)]


def _branches_fwd(z, weights, seq, tb):
    t = z.shape[0]
    nbe = seq // tb
    groups = tb // F32_SUBLANES

    def body(xa_ref, ga_ref, xb_ref, gb_ref, cw_ref, cb_ref, wa_ref, ba_ref, wx_ref, bx_ref, lam_ref,
             pw_ref, ps_ref, ya_ref, yb_ref, hl_ref, xa_ext, xb_ext, carry, a_s, u_s):
        blk = pl.program_id(0) % nbe

        @pl.when(blk == 0)
        def _():
            xa_ext[0:CONV_HIST, :] = jnp.zeros((CONV_HIST, D_MODEL), F32)
            xb_ext[0:POOL_HIST, :] = jnp.zeros((POOL_HIST, POOL_WIDTH), F32)
            carry[...] = jnp.zeros_like(carry)

        xa_ext[CONV_HIST:, :] = xa_ref[...]
        xb_ext[POOL_HIST:, :] = xb_ref[...]
        ea = xa_ext[...]
        eb = xb_ext[...]
        xa_ext[0:CONV_HIST, :] = ea[tb:, :]
        xb_ext[0:POOL_HIST, :] = eb[tb:, :]

        xc = _conv(ea, cw_ref, cb_ref[...])
        _, ig, a, mult, _ = _lru_gates(xc, wa_ref, ba_ref[...], wx_ref, bx_ref[...], lam_ref[...])
        u = mult * (ig * xc)
        row8 = lax.broadcasted_iota(jnp.int32, (tb, D_MODEL), 0) % F32_SUBLANES
        for s in (1, 2, 4):
            m = row8 >= s
            u = jnp.where(m, a * _shift_down(u, s) + u, u)
            a = jnp.where(m, a * _shift_down(a, s), a)
        a_s[...] = a
        u_s[...] = u

        def step(g, cr):
            sl = pl.ds(pl.multiple_of(g * F32_SUBLANES, F32_SUBLANES), F32_SUBLANES)
            hb = a_s[sl, :] * cr + u_s[sl, :]
            hl_ref[sl, :] = hb
            return jnp.broadcast_to(hb[F32_SUBLANES - 1:F32_SUBLANES, :], (F32_SUBLANES, D_MODEL))

        carry[...] = lax.fori_loop(0, groups, step, carry[...], unroll=4)
        ga = ga_ref[...]
        ya_ref[...] = (hl_ref[...] * (ga * _sigmoid(ga))).astype(BF16)

        pos = blk * tb + lax.broadcasted_iota(jnp.int32, (tb, POOL_GROUP_DIM), 0)
        ypre = _pool_mix(_pool_diff(eb, pos), pw_ref)
        gb = gb_ref[...]
        yb_ref[...] = ((ypre * ps_ref[...]) * (gb * _sigmoid(gb))).astype(BF16)

    row = lambda i: (i, 0)
    fixed = lambda i: (0, 0)
    in_specs = [pl.BlockSpec((tb, D_MODEL), lambda i: (i, 0)), pl.BlockSpec((tb, D_MODEL), lambda i: (i, 1)),
                pl.BlockSpec((tb, POOL_WIDTH), lambda i: (i, 4)), pl.BlockSpec((tb, POOL_WIDTH), lambda i: (i, 5)),
                ] + _branch_specs(tb, row, fixed)
    return pl.pallas_call(
        body, name="branches_fwd",
        out_shape=(jax.ShapeDtypeStruct((t, D_MODEL), BF16), jax.ShapeDtypeStruct((t, POOL_WIDTH), BF16),
                   jax.ShapeDtypeStruct((t, D_MODEL), F32)),
        grid=(t // tb,), in_specs=in_specs,
        out_specs=(pl.BlockSpec((tb, D_MODEL), row), pl.BlockSpec((tb, POOL_WIDTH), row),
                   pl.BlockSpec((tb, D_MODEL), row)),
        scratch_shapes=[pltpu.VMEM((tb + CONV_HIST, D_MODEL), F32), pltpu.VMEM((tb + POOL_HIST, POOL_WIDTH), F32),
                        pltpu.VMEM((F32_SUBLANES, D_MODEL), F32),
                        pltpu.VMEM((tb, D_MODEL), F32), pltpu.VMEM((tb, D_MODEL), F32)],
        compiler_params=pltpu.CompilerParams(dimension_semantics=("arbitrary",),
                                             vmem_limit_bytes=VMEM_LIMIT_BYTES),
    )(z, z, z, z, *weights)


def _branches_bwd(z, hl, dya, dyb, dzm, weights, seq, tb):
    t = z.shape[0]
    nb = t // tb
    nbe = seq // tb
    groups = tb // F32_SUBLANES
    n_pool = len(POOL_WINDOWS)

    def body(xa_ref, xap_ref, ga_ref, xb_ref, xbp_ref, gb_ref, hl_ref, hlp_ref, dya_ref, dyb_ref, dzm_ref,
             cw_ref, cb_ref, wa_ref, ba_ref, wx_ref, bx_ref, lam_ref, pw_ref, ps_ref,
             dz_ref, dcw_ref, dcb_ref, dwa_ref, dba_ref, dwx_ref, dbx_ref, dlam_ref, dpw_ref, dps_ref,
             xa_ext, xb_ext, hl_ext, a_ext, dxc_ext, dwin_ext, g_carry, b_s, d_s, g_s):
        i = pl.program_id(0)
        blk = (nb - 1 - i) % nbe

        @pl.when(i == 0)
        def _():
            for ref in (dcw_ref, dcb_ref, dwa_ref, dba_ref, dwx_ref, dbx_ref, dlam_ref, dpw_ref, dps_ref):
                ref[...] = jnp.zeros_like(ref)

        @pl.when(blk == nbe - 1)
        def _():
            a_ext[tb:, :] = jnp.zeros((F32_SUBLANES, D_MODEL), F32)
            dxc_ext[tb:, :] = jnp.zeros((CONV_HIST, D_MODEL), F32)
            dwin_ext[tb:, :] = jnp.zeros((POOL_HIST, POOL_WIDTH), F32)
            g_carry[...] = jnp.zeros_like(g_carry)

        live = (blk > 0).astype(F32)
        xa_ext[0:CONV_HIST, :] = xap_ref[...] * live
        xa_ext[CONV_HIST:, :] = xa_ref[...]
        xb_ext[0:POOL_HIST, :] = xbp_ref[...] * live
        xb_ext[POOL_HIST:, :] = xb_ref[...]
        hl_ext[0:F32_SUBLANES, :] = hlp_ref[...] * live
        hl_ext[F32_SUBLANES:, :] = hl_ref[...]
        ea = xa_ext[...]
        eb = xb_ext[...]

        xc = _conv(ea, cw_ref, cb_ref[...])
        lam = lam_ref[...]
        r, ig, a, mult, sp = _lru_gates(xc, wa_ref, ba_ref[...], wx_ref, bx_ref[...], lam)
        hl = hl_ref[...]
        ga = ga_ref[...]
        sga = _sigmoid(ga)
        dya = dya_ref[...]
        dhl = dya * (ga * sga)
        dz_ref[:, D_MODEL:2 * D_MODEL] = (dya * hl * (sga * (1.0 + ga * (1.0 - sga)))).astype(BF16)

        a_ext[0:tb, :] = a
        b = _shift_up(a_ext[...], 1)[0:tb, :]
        a_ext[tb:, :] = jnp.broadcast_to(a[0:1, :], (F32_SUBLANES, D_MODEL))
        d = dhl
        row8 = lax.broadcasted_iota(jnp.int32, (tb, D_MODEL), 0) % F32_SUBLANES
        for s in (1, 2, 4):
            m = row8 < F32_SUBLANES - s
            d = jnp.where(m, d + b * _shift_up(d, s), d)
            b = jnp.where(m, b * _shift_up(b, s), b)
        b_s[...] = b
        d_s[...] = d

        def step(k, cr):
            sl = pl.ds(pl.multiple_of((groups - 1 - k) * F32_SUBLANES, F32_SUBLANES), F32_SUBLANES)
            gb_ = d_s[sl, :] + b_s[sl, :] * cr
            g_s[sl, :] = gb_
            return jnp.broadcast_to(gb_[0:1, :], (F32_SUBLANES, D_MODEL))

        g_carry[...] = lax.fori_loop(0, groups, step, g_carry[...], unroll=4)
        gsc = g_s[...]
        da = gsc * _shift_down(hl_ext[...], 1)[F32_SUBLANES:, :]
        dmult = gsc * (ig * xc)
        dig = gsc * (mult * xc)
        dxc = gsc * (mult * ig)
        dlog_a = da * a - (a * a) * dmult / mult
        dr = dlog_a * (-LRU_C * sp)
        dlam_ref[...] += jnp.sum(dlog_a * (-LRU_C * r), axis=0, keepdims=True)
        dpa = dr * (r * (1.0 - r))
        dpx = dig * (ig * (1.0 - ig))
        dba_ref[...] += jnp.sum(dpa, axis=0, keepdims=True)
        dbx_ref[...] += jnp.sum(dpx, axis=0, keepdims=True)
        back = []
        for h in range(LRU_HEADS):
            cols = slice(h * HEAD_DIM, (h + 1) * HEAD_DIM)
            xh = xc[:, cols].astype(BF16)
            dpa_h = dpa[:, cols].astype(BF16)
            dpx_h = dpx[:, cols].astype(BF16)
            dwa_ref[h] += _dot_tn(xh, dpa_h)
            dwx_ref[h] += _dot_tn(xh, dpx_h)
            back.append(_dot_nt(dpa_h, wa_ref[h]) + _dot_nt(dpx_h, wx_ref[h]))
        dxc = dxc + jnp.concatenate(back, axis=1)
        dcb_ref[...] += jnp.sum(dxc, axis=0, keepdims=True)
        for k in range(CONV_WIDTH):
            tap = _shift_down(ea, CONV_WIDTH - 1 - k)[CONV_HIST:, :] if k < CONV_WIDTH - 1 else ea[CONV_HIST:, :]
            dcw_ref[k:k + 1, :] += jnp.sum(dxc * tap, axis=0, keepdims=True)
        dxc_ext[0:tb, :] = dxc
        ed = dxc_ext[...]
        dxa = ed * cw_ref[3:4, :]
        dxa = dxa + _shift_up(ed, 1) * cw_ref[2:3, :]
        dxa = dxa + _shift_up(ed, 2) * cw_ref[1:2, :]
        dxa = dxa + _shift_up(ed, 3) * cw_ref[0:1, :]
        dz_ref[:, 0:D_MODEL] = dxa[0:tb, :].astype(BF16)
        dxc_ext[tb:, :] = dxc[0:CONV_HIST, :]

        pos = blk * tb + lax.broadcasted_iota(jnp.int32, (tb, POOL_GROUP_DIM), 0)
        diff = _pool_diff(eb, pos)
        ypre = _pool_mix(diff, pw_ref)
        ps = ps_ref[...]
        gb = gb_ref[...]
        sgb = _sigmoid(gb)
        dyb = dyb_ref[...]
        dyp = dyb * (gb * sgb)
        dz_ref[:, 2 * D_MODEL + POOL_WIDTH:3 * D_MODEL] = (
            dyb * (ypre * ps) * (sgb * (1.0 + gb * (1.0 - sgb)))).astype(BF16)
        dps_ref[...] += jnp.sum(dyp * ypre, axis=0, keepdims=True)
        dypre = dyp * ps
        for g, k in enumerate(POOL_WINDOWS):
            cols = slice(g * POOL_GROUP_DIM, (g + 1) * POOL_GROUP_DIM)
            dyg = dypre[:, cols].astype(BF16)
            dpw_ref[g] += _dot_tn(diff[g].astype(BF16), dyg)
            ddiff = _dot_nt(dyg, pw_ref[g])
            count = jnp.minimum(pos + 1, k).astype(F32)
            dwin = ddiff / count
            dwin_ext[0:tb, cols] = dwin
            s = dwin_ext[:, cols]
            for step_ in range(g + 1):
                s = s + _shift_up(s, 2 ** step_)
            dz_ref[:, 2 * D_MODEL + g * POOL_GROUP_DIM:2 * D_MODEL + (g + 1) * POOL_GROUP_DIM] = (
                s[0:tb, :] - ddiff).astype(BF16)
            dwin_ext[tb:, cols] = dwin[0:POOL_HIST, :]

        dz_ref[:, 3 * D_MODEL:] = dzm_ref[...]

        @pl.when(i == nb - 1)
        def _():
            dlam_ref[...] = dlam_ref[...] * (-_sigmoid(-lam))

    rev = lambda i: (nb - 1 - i, 0)
    fixed = lambda i: (0, 0)
    fixed3 = lambda i: (0, 0, 0)

    def prev(rows, col):
        per = tb // rows
        return lambda i: (jnp.maximum((nb - 1 - i) * per - 1, 0), col)

    in_specs = [pl.BlockSpec((tb, D_MODEL), lambda i: (nb - 1 - i, 0)),
                pl.BlockSpec((CONV_HIST, D_MODEL), prev(CONV_HIST, 0)),
                pl.BlockSpec((tb, D_MODEL), lambda i: (nb - 1 - i, 1)),
                pl.BlockSpec((tb, POOL_WIDTH), lambda i: (nb - 1 - i, 4)),
                pl.BlockSpec((POOL_HIST, POOL_WIDTH), prev(POOL_HIST, 4)),
                pl.BlockSpec((tb, POOL_WIDTH), lambda i: (nb - 1 - i, 5)),
                pl.BlockSpec((tb, D_MODEL), rev),
                pl.BlockSpec((F32_SUBLANES, D_MODEL), prev(F32_SUBLANES, 0)),
                pl.BlockSpec((tb, D_MODEL), rev), pl.BlockSpec((tb, POOL_WIDTH), rev),
                pl.BlockSpec((tb, 2 * D_MODEL), rev)] + _branch_specs(tb, rev, fixed)
    out_shape = (jax.ShapeDtypeStruct((t, IN_COLS), BF16),
                 jax.ShapeDtypeStruct((CONV_WIDTH, D_MODEL), F32), jax.ShapeDtypeStruct((1, D_MODEL), F32),
                 jax.ShapeDtypeStruct((LRU_HEADS, HEAD_DIM, HEAD_DIM), F32), jax.ShapeDtypeStruct((1, D_MODEL), F32),
                 jax.ShapeDtypeStruct((LRU_HEADS, HEAD_DIM, HEAD_DIM), F32), jax.ShapeDtypeStruct((1, D_MODEL), F32),
                 jax.ShapeDtypeStruct((1, D_MODEL), F32),
                 jax.ShapeDtypeStruct((n_pool, POOL_GROUP_DIM, POOL_GROUP_DIM), F32),
                 jax.ShapeDtypeStruct((1, POOL_WIDTH), F32))
    out_specs = (pl.BlockSpec((tb, IN_COLS), rev),
                 pl.BlockSpec((CONV_WIDTH, D_MODEL), fixed), pl.BlockSpec((1, D_MODEL), fixed),
                 pl.BlockSpec((LRU_HEADS, HEAD_DIM, HEAD_DIM), fixed3), pl.BlockSpec((1, D_MODEL), fixed),
                 pl.BlockSpec((LRU_HEADS, HEAD_DIM, HEAD_DIM), fixed3), pl.BlockSpec((1, D_MODEL), fixed),
                 pl.BlockSpec((1, D_MODEL), fixed),
                 pl.BlockSpec((n_pool, POOL_GROUP_DIM, POOL_GROUP_DIM), fixed3),
                 pl.BlockSpec((1, POOL_WIDTH), fixed))
    scratch = [pltpu.VMEM((tb + CONV_HIST, D_MODEL), F32), pltpu.VMEM((tb + POOL_HIST, POOL_WIDTH), F32),
               pltpu.VMEM((tb + F32_SUBLANES, D_MODEL), F32), pltpu.VMEM((tb + F32_SUBLANES, D_MODEL), F32),
               pltpu.VMEM((tb + CONV_HIST, D_MODEL), F32), pltpu.VMEM((tb + POOL_HIST, POOL_WIDTH), F32),
               pltpu.VMEM((F32_SUBLANES, D_MODEL), F32),
               pltpu.VMEM((tb, D_MODEL), F32), pltpu.VMEM((tb, D_MODEL), F32), pltpu.VMEM((tb, D_MODEL), F32)]
    return pl.pallas_call(
        body, name="branches_bwd", out_shape=out_shape, grid=(nb,), in_specs=in_specs, out_specs=out_specs,
        scratch_shapes=scratch,
        compiler_params=pltpu.CompilerParams(dimension_semantics=("arbitrary",),
                                             vmem_limit_bytes=VMEM_LIMIT_BYTES),
    )(z, z, z, z, z, z, hl, hl, dya, dyb, dzm, *weights)


def _merge_head(x2d, ya, yb, z, p2d, tgt, w_pl, w_pp, w_out, w_pg, w_pe, g2, gf, tb):
    t = x2d.shape[0]
    p_dim = p2d.shape[1]

    def body(x_ref, ya_ref, yb_ref, ma_ref, mb_ref, p_ref, t_ref, wpl_ref, wpp_ref, wout_ref, wpg_ref, wpe_ref,
             g2_ref, gf_ref,
             loss_ref, dg2_ref, dgf_ref, dxr_ref, dya_ref, dyb_ref, dzm_ref,
             mg_ref, do_ref, hn_ref, dgp_ref, dpe_ref, da_ref, dbm_ref, pbf_ref):
        @pl.when(pl.program_id(0) == 0)
        def _():
            loss_ref[...] = jnp.zeros_like(loss_ref)
            dg2_ref[...] = jnp.zeros_like(dg2_ref)
            dgf_ref[...] = jnp.zeros_like(dgf_ref)

        a_ = _dot(ya_ref[...], wpl_ref[...])
        bm = _dot(yb_ref[...], wpp_ref[...])
        sa = _sigmoid(ma_ref[...])
        sb = _sigmoid(mb_ref[...])
        mg = (sa * a_ + sb * bm).astype(BF16)
        mg_ref[...] = mg
        x1 = x_ref[...] + _dot(mg, wout_ref[...])
        xn2, r2 = _rms(x1)
        g2 = g2_ref[...]
        hn = (xn2 * g2).astype(BF16)
        hn_ref[...] = hn
        gate = _sigmoid(_dot(hn, wpg_ref[...]))
        pbf = p_ref[...].astype(BF16)
        pbf_ref[...] = pbf
        pe = _dot(pbf, wpe_ref[...])
        x2 = x1 + gate * pe
        xn3, r3 = _rms(x2)
        gf = gf_ref[...]
        err = xn3 * gf - t_ref[...]
        loss_ref[...] += 0.5 * jnp.sum(jnp.mean(err * err, axis=-1))

        dy = err * (1.0 / D_MODEL)
        dgf_ref[...] += jnp.sum(dy * xn3, axis=0, keepdims=True)
        dx2 = _rms_bwd(dy * gf, xn3, r3)
        dpe_ref[...] = (dx2 * gate).astype(BF16)
        dgp = ((dx2 * pe) * (gate * (1.0 - gate))).astype(BF16)
        dgp_ref[...] = dgp
        dhn = _dot_nt(dgp, wpg_ref[...])
        dg2_ref[...] += jnp.sum(dhn * xn2, axis=0, keepdims=True)
        dx1 = dx2 + _rms_bwd(dhn * g2, xn2, r2)
        dxr_ref[...] = dx1
        do = dx1.astype(BF16)
        do_ref[...] = do
        dmg = _dot_nt(do, wout_ref[...])
        da = (dmg * sa).astype(BF16)
        dbm = (dmg * sb).astype(BF16)
        da_ref[...] = da
        dbm_ref[...] = dbm
        dzm_ref[:, 0:D_MODEL] = (dmg * a_ * (sa * (1.0 - sa))).astype(BF16)
        dzm_ref[:, D_MODEL:] = (dmg * bm * (sb * (1.0 - sb))).astype(BF16)
        dya_ref[...] = _dot_nt(da, wpl_ref[...])
        dyb_ref[...] = _dot_nt(dbm, wpp_ref[...])

    row = lambda i: (i, 0)
    fixed = lambda i: (0, 0)

    def resident(shape):
        return pl.BlockSpec(shape, fixed, pipeline_mode=pl.Buffered(1))

    tok = lambda width: pl.BlockSpec((tb, width), row)
    in_specs = [tok(D_MODEL), tok(D_MODEL), tok(POOL_WIDTH),
                pl.BlockSpec((tb, D_MODEL), lambda i: (i, 3)), pl.BlockSpec((tb, D_MODEL), lambda i: (i, 4)),
                tok(p_dim), tok(D_MODEL),
                resident((D_MODEL, D_MODEL)), resident((POOL_WIDTH, D_MODEL)), resident((D_MODEL, D_MODEL)),
                resident((D_MODEL, D_MODEL)), resident((p_dim, D_MODEL)),
                pl.BlockSpec((1, D_MODEL), fixed), pl.BlockSpec((1, D_MODEL), fixed)]
    bf = lambda width: jax.ShapeDtypeStruct((t, width), BF16)
    f32 = lambda width: jax.ShapeDtypeStruct((t, width), F32)
    out_shape = (jax.ShapeDtypeStruct((F32_SUBLANES, 128), F32), jax.ShapeDtypeStruct((1, D_MODEL), F32),
                 jax.ShapeDtypeStruct((1, D_MODEL), F32),
                 f32(D_MODEL), f32(D_MODEL), f32(POOL_WIDTH), bf(2 * D_MODEL),
                 bf(D_MODEL), bf(D_MODEL), bf(D_MODEL), bf(D_MODEL), bf(D_MODEL), bf(D_MODEL), bf(D_MODEL), bf(p_dim))
    out_specs = (pl.BlockSpec((F32_SUBLANES, 128), fixed), pl.BlockSpec((1, D_MODEL), fixed),
                 pl.BlockSpec((1, D_MODEL), fixed),
                 tok(D_MODEL), tok(D_MODEL), tok(POOL_WIDTH), tok(2 * D_MODEL),
                 tok(D_MODEL), tok(D_MODEL), tok(D_MODEL), tok(D_MODEL), tok(D_MODEL), tok(D_MODEL), tok(D_MODEL),
                 tok(p_dim))
    return pl.pallas_call(
        body, name="merge_head", out_shape=out_shape, grid=(t // tb,), in_specs=in_specs, out_specs=out_specs,
        compiler_params=pltpu.CompilerParams(dimension_semantics=("arbitrary",),
                                             vmem_limit_bytes=VMEM_LIMIT_BYTES),
    )(x2d, ya, yb, z, z, p2d, tgt, w_pl, w_pp, w_out, w_pg, w_pe, g2, gf)


def _pad_rows(a, rows):
    return jnp.pad(a, ((0, rows - a.shape[0]), (0, D_MODEL - a.shape[1])))


def _pack_bag(parts):
    rows = [_pad_rows(a.reshape(-1, a.shape[-1]) if a.shape[-1] != HEAD_DIM else a.reshape(-1, D_MODEL), n)
            for a, n in zip(parts, BAG_PART_ROWS)]
    rows.append(jnp.zeros((BAG_ROWS - sum(BAG_PART_ROWS), D_MODEL), F32))
    return jnp.concatenate(rows, axis=0)


def _unpack_bag(bag, shapes):
    out, at = [], 0
    for shape, n in zip(shapes, BAG_PART_ROWS):
        size = 1
        for s in shape:
            size *= s
        if size % D_MODEL == 0:
            piece = bag[at:at + size // D_MODEL, :]
        else:
            piece = bag[at:at + 1, :size]
        out.append(piece.reshape(shape))
        at += n
    return out


def kernel(x, p, norm_g, w_in, conv_w, conv_b, lru_w_a, lru_b_a, lru_w_x, lru_b_x, lru_lambda, pool_w, pool_scale, w_proj_lru, w_proj_pool, w_out, ple_norm_g, w_ple_gate, w_ple_proj, final_g, loss_target, m_norm_g, m_w_in, m_conv_w, m_conv_b, m_lru_w_a, m_lru_b_a, m_lru_w_x, m_lru_b_x, m_lru_lambda, m_pool_w, m_pool_scale, m_w_proj_lru, m_w_proj_pool, m_w_out, m_ple_norm_g, m_w_ple_gate, m_w_ple_proj, m_final_g, v_norm_g, v_w_in, v_conv_w, v_conv_b, v_lru_w_a, v_lru_b_a, v_lru_w_x, v_lru_b_x, v_lru_lambda, v_pool_w, v_pool_scale, v_w_proj_lru, v_w_proj_pool, v_w_out, v_ple_norm_g, v_w_ple_gate, v_w_ple_proj, v_final_g):
    bsz, seq, _ = x.shape
    t = bsz * seq
    tb_mm = min(512, seq)
    tb_seq = min(256, seq // 2) if seq >= 512 else seq
    x2d = x.reshape(t, D_MODEL)
    p2d = p.reshape(t, p.shape[-1])
    tgt = loss_target.reshape(t, D_MODEL)
    chip = 2 * lax.axis_index("x") + lax.axis_index("y")

    rest = [(w_proj_lru[0], 0), (w_proj_pool[0], 1), (w_out[0], 0), (w_ple_gate[0], 0), (w_ple_proj[0], 1)]
    z, w_in_f, w_pl_f, w_pp_f, w_out_f, w_pg_f, w_pe_f, conv_w_f = _in_proj_gather(
        x2d, norm_g, w_in[0].astype(BF16),
        [(w.astype(BF16), axis, True) for w, axis in rest] + [(conv_w[0], 1, False)], tb_mm)

    wa_bf = lru_w_a[0].astype(BF16)
    wx_bf = lru_w_x[0].astype(BF16)
    pw_bf = pool_w[0].astype(BF16)
    branch_w = (conv_w_f, conv_b, wa_bf, lru_b_a.reshape(1, D_MODEL), wx_bf, lru_b_x.reshape(1, D_MODEL),
                lru_lambda, pw_bf, pool_scale)

    ya, yb, hl = _branches_fwd(z, branch_w, seq, tb_seq)
    (loss_acc, d_g2, d_gf, dx_res, dya, dyb, dzm, mg_bf, do_bf, hn_bf, dgp_bf, dpe_bf, da_bf, dbm_bf, p_bf) = _merge_head(
        x2d, ya, yb, z, p2d, tgt, w_pl_f, w_pp_f, w_out_f, w_pg_f, w_pe_f, ple_norm_g, final_g.reshape(1, D_MODEL),
        tb_seq)
    (dz, d_cw, d_cb, d_wa, d_ba, d_wx, d_bx, d_lam, d_pw, d_ps) = _branches_bwd(
        z, hl, dya, dyb, dzm, branch_w, seq, tb_seq)
    dx, h_bf, d_g1 = _in_proj_bwd(dz, w_in_f, x2d, dx_res, norm_g, tb_mm)

    g_in = _weight_grad(h_bf, dz, N_CHIPS, tb_mm, "dw_in").reshape(8, D_MODEL // 2, IN_COLS // N_CHIPS)
    g_pl = _weight_grad(ya, da_bf, 1, tb_mm, "dw_proj_lru").reshape(8, D_MODEL // 8, D_MODEL)
    g_pp = _weight_grad(yb, dbm_bf, N_CHIPS, tb_mm, "dw_proj_pool").reshape(8, POOL_WIDTH // 2, D_MODEL // N_CHIPS)
    g_out = _weight_grad(mg_bf, do_bf, 1, tb_mm, "dw_out").reshape(8, D_MODEL // 8, D_MODEL)
    g_pg = _weight_grad(hn_bf, dgp_bf, 1, tb_mm, "dw_ple_gate").reshape(8, D_MODEL // 8, D_MODEL)
    p_dim = p2d.shape[1]
    g_pe = _weight_grad(p_bf, dpe_bf, N_CHIPS, tb_mm, "dw_ple_proj").reshape(8, p_dim // 2, D_MODEL // N_CHIPS)

    (r_in,) = _reduce_scatter([g_in], "rs_w_in", BF16)
    r_pl, r_pp, r_out, r_pg, r_pe = _reduce_scatter([g_pl, g_pp, g_out, g_pg, g_pe], "rs_proj", BF16)
    small_shapes = [(1, D_MODEL), (1, CONV_WIDTH, D_MODEL), (1, D_MODEL), lru_w_a.shape, lru_b_a.shape, lru_w_x.shape,
                    lru_b_x.shape, (1, D_MODEL), pool_w.shape, pool_scale.shape, (1, D_MODEL), final_g.shape]
    bag = _pack_bag([d_g1, d_cw, d_cb, d_wa, d_ba.reshape(1, D_MODEL), d_wx, d_bx.reshape(1, D_MODEL), d_lam, d_pw,
                     d_ps, d_g2, d_gf])
    (bag_mine,) = _reduce_scatter([bag.reshape(8, BAG_ROWS // 8, D_MODEL)], "rs_small")
    (bag_sum,) = _gather_shards([(bag_mine.reshape(BAG_ROWS // N_CHIPS, D_MODEL), 0, True)], "gather_small")
    (g_g1, g_cw_full, g_cb, g_wa, g_ba, g_wx, g_bx, g_lam, g_pw, g_ps, g_g2, g_gf) = _unpack_bag(bag_sum, small_shapes)
    cw_cols = D_MODEL // N_CHIPS
    g_cw = lax.dynamic_slice_in_dim(g_cw_full, chip * cw_cols, cw_cols, axis=2)

    def big_update(w, g2d, m, v, rows, name):
        d, nm, nv = _adamw(w[0], g2d, m[0], v[0], rows, name)
        return g2d[None], d[None], nm[None], nv[None]

    u_in = big_update(w_in, r_in.reshape(D_MODEL, IN_COLS // N_CHIPS), m_w_in, v_w_in, 256, "adamw_w_in")
    u_pl = big_update(w_proj_lru, r_pl.reshape(D_MODEL // N_CHIPS, D_MODEL), m_w_proj_lru, v_w_proj_lru, 256, "adamw_w_proj_lru")
    u_pp = big_update(w_proj_pool, r_pp.reshape(POOL_WIDTH, D_MODEL // N_CHIPS), m_w_proj_pool, v_w_proj_pool, 512, "adamw_w_proj_pool")
    u_out = big_update(w_out, r_out.reshape(D_MODEL // N_CHIPS, D_MODEL), m_w_out, v_w_out, 256, "adamw_w_out")
    u_pg = big_update(w_ple_gate, r_pg.reshape(D_MODEL // N_CHIPS, D_MODEL), m_w_ple_gate, v_w_ple_gate, 256, "adamw_w_ple_gate")
    u_pe = big_update(w_ple_proj, r_pe.reshape(p_dim, D_MODEL // N_CHIPS), m_w_ple_proj, v_w_ple_proj, 256, "adamw_w_ple_proj")
    u_cw = big_update(conv_w, g_cw[0], m_conv_w, v_conv_w, CONV_WIDTH, "adamw_conv_w")

    small_w = [norm_g, None, conv_b, lru_w_a, lru_b_a, lru_w_x, lru_b_x, lru_lambda, pool_w, pool_scale, ple_norm_g, final_g]
    small_m = [m_norm_g, None, m_conv_b, m_lru_w_a, m_lru_b_a, m_lru_w_x, m_lru_b_x, m_lru_lambda, m_pool_w, m_pool_scale, m_ple_norm_g, m_final_g]
    small_v = [v_norm_g, None, v_conv_b, v_lru_w_a, v_lru_b_a, v_lru_w_x, v_lru_b_x, v_lru_lambda, v_pool_w, v_pool_scale, v_ple_norm_g, v_final_g]
    fill = jnp.zeros((CONV_WIDTH, D_MODEL), F32)

    def bag_of(arrs):
        return _pack_bag([fill if a is None else (a[0] if a.ndim > 1 else a[None]) for a in arrs])

    d_bag, m_bag, v_bag = _adamw(bag_of(small_w), bag_sum, bag_of(small_m), bag_of(small_v), BAG_ROWS // 8, "adamw_small")
    d_small = _unpack_bag(d_bag, small_shapes)
    m_small = _unpack_bag(m_bag, small_shapes)
    v_small = _unpack_bag(v_bag, small_shapes)

    loss = lax.psum(loss_acc[0, 0], ALL_AXES)
    grad_x = dx.reshape(bsz, seq, D_MODEL)

    def ordered(small, pick):
        s = list(small)
        return [s[0], u_in[pick], u_cw[pick], s[2], s[3], s[4], s[5], s[6], s[7], s[8], s[9],
                u_pl[pick], u_pp[pick], u_out[pick], s[10], u_pg[pick], u_pe[pick], s[11]]

    grads = ordered([g_g1, None, g_cb, g_wa, g_ba, g_wx, g_bx, g_lam, g_pw, g_ps, g_g2, g_gf], 0)
    return (loss, grad_x, *grads, *ordered(d_small, 1), *ordered(m_small, 2), *ordered(v_small, 3))
```

```python
import functools

import jax
import jax.numpy as jnp
from jax import lax
from jax.experimental import pallas as pl
from jax.experimental.pallas import tpu as pltpu

F32 = jnp.float32
BF16 = jnp.bfloat16
MESH = pl.DeviceIdType.MESH
ALL_AXES = ("x", "y", "c")

D_MODEL = 1024
LRU_HEADS = 8
HEAD_DIM = 128
CONV_WIDTH = 4
LRU_C = 8.0
POOL_WIDTH = 512
POOL_WINDOWS = (2, 4, 8, 16)
POOL_GROUP_DIM = 128
IN_COLS = 5120
N_CHIPS = 4
EPS = 1e-6

ADAM_LR = 0.001
ADAM_B1 = 0.9
ADAM_B2 = 0.999
ADAM_EPS = 1e-08
ADAM_WD = 0.01
ADAM_STEP = 10

F32_SUBLANES = 8
CONV_HIST = 8
POOL_HIST = 16
VMEM_LIMIT_BYTES = 58 * 1024 * 1024
BAG_PART_ROWS = (8, 8, 8, 128, 8, 128, 8, 8, 64, 8, 8, 8)
BAG_ROWS = 448


def _dot(a, b):
    return jnp.dot(a, b, preferred_element_type=F32)


def _dot_nt(a, b):
    return lax.dot_general(a, b, (((1,), (1,)), ((), ())), preferred_element_type=F32)


def _dot_tn(a, b):
    return lax.dot_general(a, b, (((0,), (0,)), ((), ())), preferred_element_type=F32)


def _sigmoid(v):
    return jax.nn.sigmoid(v)


def _softplus(v):
    return jnp.maximum(v, 0.0) + jnp.log1p(jnp.exp(-jnp.abs(v)))


def _place():
    return lax.axis_index("x"), lax.axis_index("y"), lax.axis_index("c")


GATHER_SEMS = 6


def _gather_shapes(shards):
    out_shape = []
    for arr, axis, _ in shards:
        r, cols = arr.shape
        out_shape.append(jax.ShapeDtypeStruct((N_CHIPS * r, cols) if axis == 0 else (r, N_CHIPS * cols), arr.dtype))
    n = len(shards)
    sems = [pltpu.SemaphoreType.DMA((n * GATHER_SEMS,)), pltpu.SemaphoreType.DMA((n * GATHER_SEMS,)),
            pltpu.SemaphoreType.DMA((n,))]
    return out_shape, sems


def _gather_steps(shards, ins, outs, send_sems, recv_sems, local_sems):
    n = len(shards)
    x, y, c = _place()
    me, sibling = (x, y, c), (x, y, 1 - c)
    chips = [(x, 1 - y), (1 - x, y), (1 - x, 1 - y)]

    def region(k, cx, cy, hc):
        (r, cols), axis = shards[k][0].shape, shards[k][1]
        j = 2 * cx + cy
        if axis == 0:
            if hc is None:
                return outs[k].at[pl.ds(j * r, r), :]
            return outs[k].at[pl.ds(j * r + hc * (r // 2), r // 2), :]
        if hc is None:
            return outs[k].at[:, pl.ds(j * cols, cols)]
        return outs[k].at[pl.ds(hc * (r // 2), r // 2), pl.ds(j * cols, cols)]

    def remote(k, sem, block, to, src=None):
        dst = region(k, *block)
        return pltpu.make_async_remote_copy(
            src_ref=dst if src is None else src, dst_ref=dst,
            send_sem=send_sems.at[k * GATHER_SEMS + sem], recv_sem=recv_sems.at[k * GATHER_SEMS + sem],
            device_id=to, device_id_type=MESH)

    def first(k, idx):
        r, split = shards[k][0].shape[0], shards[k][2]
        src = ins[k].at[pl.ds(c * (r // 2), r // 2), :] if split else ins[k]
        return remote(k, idx, (x, y, c if split else None), (*chips[idx], c), src=src)

    def passed(k, idx):
        return remote(k, 3 + idx, (*chips[idx], c), sibling)

    def mine(k):
        return pltpu.make_async_copy(ins[k], region(k, x, y, None), local_sems.at[k])

    def start():
        for k in range(n):
            mine(k).start()
            for idx in range(3):
                first(k, idx).start()

    def finish():
        for k in range(n):
            split = shards[k][2]
            for idx in range(3):
                remote(k, idx, (*chips[idx], c if split else None), me).wait_recv()
                if split:
                    passed(k, idx).start()
        for k in range(n):
            if shards[k][2]:
                for idx in range(3):
                    remote(k, 3 + idx, (*chips[idx], 1 - c), me).wait_recv()
        for k in range(n):
            for idx in range(3):
                first(k, idx).wait_send()
                if shards[k][2]:
                    passed(k, idx).wait_send()
            mine(k).wait()

    return start, finish


def _gather_shards(shards, name):
    n = len(shards)

    def body(*refs):
        start, finish = _gather_steps(shards, refs[:n], refs[n:2 * n], *refs[2 * n:])
        start()
        finish()

    out_shape, sems = _gather_shapes(shards)
    any_spec = pl.BlockSpec(memory_space=pl.ANY)
    return pl.pallas_call(
        body, name=name, out_shape=tuple(out_shape),
        in_specs=[any_spec] * n, out_specs=tuple([any_spec] * n), scratch_shapes=sems,
    )(*[s[0] for s in shards])


RS_ADD_ROWS = (64, 56, 32, 16, 8)


def _reduce_scatter(parts, name, wire=F32):
    n = len(parts)
    n_sem = 8

    def body(*refs):
        ins, outs = refs[:n], refs[n:2 * n]
        own = refs[2 * n:3 * n]
        sib = refs[3 * n:4 * n]
        got = refs[4 * n:5 * n]
        fin = refs[5 * n:6 * n]
        snd = refs[6 * n:7 * n]
        send_sems, recv_sems, local_sems = refs[7 * n:]
        x, y, c = _place()
        j_me = 2 * x + y
        chips = [(1 - x, y), (x, 1 - y), (1 - x, 1 - y)]

        def remote(a, sem, src, dst, to):
            return pltpu.make_async_remote_copy(
                src_ref=src, dst_ref=dst, send_sem=send_sems.at[a * n_sem + sem],
                recv_sem=recv_sems.at[a * n_sem + sem], device_id=to, device_id_type=MESH)

        def rows_loop(a, fn):
            r = parts[a].shape[1]
            step = max(s for s in RS_ADD_ROWS if r % s == 0)

            def it(i, carry):
                fn(pl.ds(pl.multiple_of(i * step, step), step))
                return carry

            lax.fori_loop(0, r // step, it, 0)

        loads, sends = [], []
        for a in range(n):
            for jj in range(N_CHIPS):
                cp = pltpu.make_async_copy(ins[a].at[2 * jj + c], own[a].at[jj], local_sems.at[a * 5 + jj])
                cp.start()
                loads.append(cp)
                sd = remote(a, jj, ins[a].at[2 * jj + (1 - c)], sib[a].at[jj], (x, y, 1 - c))
                sd.start()
                sends.append(sd)
        for a in range(n):
            for jj in range(N_CHIPS):
                loads[a * N_CHIPS + jj].wait()
                remote(a, jj, sib[a].at[jj], sib[a].at[jj], (x, y, c)).wait_recv()

                def add(sl, a=a, jj=jj):
                    q = own[a][jj, sl, :] + sib[a][jj, sl, :]
                    own[a][jj, sl, :] = q
                    snd[a][jj, sl, :] = q.astype(wire)

                rows_loop(a, add)
        for a in range(n):
            for idx, chip in enumerate(chips):
                sd = remote(a, 4 + idx, snd[a].at[2 * chip[0] + chip[1]], got[a].at[j_me], (*chip, c))
                sd.start()
                sends.append(sd)
        for a in range(n):
            def keep(sl, a=a):
                got[a][j_me, sl, :] = snd[a][j_me, sl, :]

            rows_loop(a, keep)
        for a in range(n):
            for idx, chip in enumerate(chips):
                slot = got[a].at[2 * chip[0] + chip[1]]
                remote(a, 4 + idx, slot, slot, (x, y, c)).wait_recv()

            def total(sl, a=a):
                mine = own[a][j_me, sl, :]
                term = [jnp.where(j_me == jj, mine, got[a][jj, sl, :].astype(F32)) for jj in range(N_CHIPS)]
                fin[a][sl, :] = ((term[0] + term[1]) + term[2]) + term[3]

            rows_loop(a, total)
        stores = []
        for a in range(n):
            st = pltpu.make_async_copy(fin[a], outs[a].at[c], local_sems.at[a * 5 + 4])
            st.start()
            stores.append(st)
            sd = remote(a, 7, fin[a], outs[a].at[c], (x, y, 1 - c))
            sd.start()
            sends.append(sd)
        for a in range(n):
            remote(a, 7, outs[a].at[1 - c], outs[a].at[1 - c], (x, y, c)).wait_recv()
        for cp in sends:
            cp.wait_send()
        for cp in stores:
            cp.wait()

    any_spec = pl.BlockSpec(memory_space=pl.ANY)
    scratch = []
    for lead, dtype in ((N_CHIPS, F32), (N_CHIPS, F32), (N_CHIPS, wire), (None, F32), (N_CHIPS, wire)):
        for p in parts:
            shape = p.shape[1:] if lead is None else (lead,) + p.shape[1:]
            scratch.append(pltpu.VMEM(shape, dtype))
    scratch += [pltpu.SemaphoreType.DMA((n * n_sem,)), pltpu.SemaphoreType.DMA((n * n_sem,)),
                pltpu.SemaphoreType.DMA((n * 5,))]
    return pl.pallas_call(
        body, name=name,
        out_shape=tuple(jax.ShapeDtypeStruct((2,) + p.shape[1:], F32) for p in parts),
        in_specs=[any_spec] * n, out_specs=tuple([any_spec] * n), scratch_shapes=scratch,
        compiler_params=pltpu.CompilerParams(vmem_limit_bytes=VMEM_LIMIT_BYTES),
    )(*parts)


def _rms(x):
    r = lax.rsqrt(jnp.mean(x * x, axis=-1, keepdims=True) + EPS)
    return x * r, r


def _rms_bwd(dxn, xn, r):
    return r * (dxn - xn * jnp.mean(dxn * xn, axis=-1, keepdims=True))


def _in_proj_gather(x2d, norm_g, w_in_sh, shards, tb):
    t = x2d.shape[0]
    nb = t // tb
    cols = IN_COLS // N_CHIPS
    half = D_MODEL // 2
    n = len(shards)

    def body(x_ref, g_ref, win_ref, *refs):
        ins = refs[:n]
        z_ref, wfull_ref = refs[n], refs[n + 1]
        outs = refs[n + 2:2 * n + 2]
        wv, send_sems, recv_sems, local_sems, w_send, w_recv, w_local = refs[2 * n + 2:]
        s, i = pl.program_id(0), pl.program_id(1)
        x, y, c = _place()
        me, sibling = (x, y, c), (x, y, 1 - c)
        chips = [(x, 1 - y), (1 - x, y), (1 - x, 1 - y)]

        def w_half(cx, cy, hc):
            return wv.at[2 * cx + cy, pl.ds(hc * half, half), :]

        def w_remote(sem, block, to, src=None):
            dst = w_half(*block)
            return pltpu.make_async_remote_copy(
                src_ref=dst if src is None else src, dst_ref=dst, send_sem=w_send.at[sem],
                recv_sem=w_recv.at[sem], device_id=to, device_id_type=MESH)

        def w_first(idx):
            return w_remote(idx, (x, y, c), (*chips[idx], c), src=win_ref.at[pl.ds(c * half, half), :])

        def w_pass(idx):
            return w_remote(3 + idx, (*chips[idx], c), sibling)

        def w_store(k, cx, cy):
            jj = 2 * cx + cy
            return pltpu.make_async_copy(wv.at[jj], wfull_ref.at[:, pl.ds(jj * cols, cols)], w_local.at[k])

        start_rest, finish_rest = _gather_steps(shards, ins, outs, send_sems, recv_sems, local_sems)
        own = pltpu.make_async_copy(win_ref, wv.at[2 * x + y], w_local.at[4])

        @pl.when((s == 0) & (i == 0))
        def _():
            own.start()
            for idx in range(3):
                w_first(idx).start()
            start_rest()
            own.wait()
            w_store(0, x, y).start()

        for idx in range(3):
            @pl.when((s == idx + 1) & (i == 0))
            def _(idx=idx):
                w_remote(idx, (*chips[idx], c), me).wait_recv()
                w_pass(idx).start()
                w_remote(3 + idx, (*chips[idx], 1 - c), me).wait_recv()
                w_store(idx + 1, *chips[idx]).start()

        xn, _ = _rms(x_ref[...])
        z_ref[...] = _dot((xn * g_ref[...]).astype(BF16), wv[jnp.bitwise_xor(2 * x + y, s)])

        @pl.when((s == N_CHIPS - 1) & (i == nb - 1))
        def _():
            finish_rest()
            for idx in range(3):
                w_first(idx).wait_send()
                w_pass(idx).wait_send()
            w_store(0, x, y).wait()
            for idx in range(3):
                w_store(idx + 1, *chips[idx]).wait()

    rest_shape, rest_sems = _gather_shapes(shards)
    out_shape = [jax.ShapeDtypeStruct((t, IN_COLS), F32), jax.ShapeDtypeStruct((D_MODEL, IN_COLS), BF16)] + rest_shape
    any_spec = pl.BlockSpec(memory_space=pl.ANY)

    def z_map(s, i):
        return (i, jnp.bitwise_xor(2 * lax.axis_index("x") + lax.axis_index("y"), s))

    return pl.pallas_call(
        body, name="in_proj", out_shape=tuple(out_shape),
        grid=(N_CHIPS, nb),
        in_specs=[pl.BlockSpec((tb, D_MODEL), lambda s, i: (i, 0)),
                  pl.BlockSpec((1, D_MODEL), lambda s, i: (0, 0)), any_spec] + [any_spec] * n,
        out_specs=tuple([pl.BlockSpec((tb, cols), z_map), any_spec] + [any_spec] * n),
        scratch_shapes=[pltpu.VMEM((N_CHIPS, D_MODEL, cols), BF16)] + rest_sems + [
            pltpu.SemaphoreType.DMA((GATHER_SEMS,)), pltpu.SemaphoreType.DMA((GATHER_SEMS,)),
            pltpu.SemaphoreType.DMA((N_CHIPS + 1,))],
        compiler_params=pltpu.CompilerParams(dimension_semantics=("arbitrary", "arbitrary"),
                                             vmem_limit_bytes=VMEM_LIMIT_BYTES),
    )(x2d, norm_g, w_in_sh, *[sh[0] for sh in shards])


def _in_proj_bwd(dz, w_in, x2d, dx_res, norm_g, tb):
    t = x2d.shape[0]

    def body(dz_ref, w_ref, x_ref, dres_ref, g_ref, dx_ref, h_ref, dg_ref):
        @pl.when(pl.program_id(0) == 0)
        def _():
            dg_ref[...] = jnp.zeros_like(dg_ref)

        xn, r = _rms(x_ref[...])
        g = g_ref[...]
        h_ref[...] = (xn * g).astype(BF16)
        dh = _dot_nt(dz_ref[...], w_ref[...])
        dg_ref[...] += jnp.sum(dh * xn, axis=0, keepdims=True)
        dx_ref[...] = dres_ref[...] + _rms_bwd(dh * g, xn, r)

    row = lambda i: (i, 0)
    fixed = lambda i: (0, 0)
    return pl.pallas_call(
        body, name="in_proj_bwd",
        out_shape=(jax.ShapeDtypeStruct((t, D_MODEL), F32), jax.ShapeDtypeStruct((t, D_MODEL), BF16),
                   jax.ShapeDtypeStruct((1, D_MODEL), F32)),
        grid=(t // tb,),
        in_specs=[pl.BlockSpec((tb, IN_COLS), row),
                  pl.BlockSpec((D_MODEL, IN_COLS), fixed, pipeline_mode=pl.Buffered(1)),
                  pl.BlockSpec((tb, D_MODEL), row), pl.BlockSpec((tb, D_MODEL), row),
                  pl.BlockSpec((1, D_MODEL), fixed)],
        out_specs=(pl.BlockSpec((tb, D_MODEL), row), pl.BlockSpec((tb, D_MODEL), row),
                   pl.BlockSpec((1, D_MODEL), fixed)),
        compiler_params=pltpu.CompilerParams(dimension_semantics=("arbitrary",),
                                             vmem_limit_bytes=VMEM_LIMIT_BYTES),
    )(dz, w_in, x2d, dx_res, norm_g)


def _weight_grad(lhs, rhs, n_chunks, tb, name):
    t, k = lhs.shape
    nc = rhs.shape[1] // n_chunks

    def body(l_ref, r_ref, o_ref):
        @pl.when(pl.program_id(1) == 0)
        def _():
            o_ref[...] = jnp.zeros_like(o_ref)

        o_ref[...] += _dot_tn(l_ref[...], r_ref[...])

    return pl.pallas_call(
        body, name=name, out_shape=jax.ShapeDtypeStruct((n_chunks, k, nc), F32),
        grid=(n_chunks, t // tb),
        in_specs=[pl.BlockSpec((tb, k), lambda j, i: (i, 0)), pl.BlockSpec((tb, nc), lambda j, i: (i, j))],
        out_specs=pl.BlockSpec((None, k, nc), lambda j, i: (j, 0, 0)),
        compiler_params=pltpu.CompilerParams(dimension_semantics=("arbitrary", "arbitrary"),
                                             vmem_limit_bytes=VMEM_LIMIT_BYTES),
    )(lhs, rhs)


def _adamw(w, g, m, v, rows, name):
    r, c = w.shape

    def body(w_ref, g_ref, m_ref, v_ref, d_ref, nm_ref, nv_ref):
        g_ = g_ref[...]
        m_ = ADAM_B1 * m_ref[...] + (1.0 - ADAM_B1) * g_
        v_ = ADAM_B2 * v_ref[...] + (1.0 - ADAM_B2) * jnp.square(g_)
        m_hat = m_ / (1.0 - ADAM_B1 ** ADAM_STEP)
        v_hat = v_ / (1.0 - ADAM_B2 ** ADAM_STEP)
        d_ref[...] = -ADAM_LR * (m_hat / (jnp.sqrt(v_hat) + ADAM_EPS) + ADAM_WD * w_ref[...])
        nm_ref[...] = m_
        nv_ref[...] = v_

    spec = pl.BlockSpec((rows, c), lambda i: (i, 0))
    return pl.pallas_call(
        body, name=name, out_shape=tuple(jax.ShapeDtypeStruct((r, c), F32) for _ in range(3)),
        grid=(r // rows,), in_specs=[spec] * 4, out_specs=(spec,) * 3,
        compiler_params=pltpu.CompilerParams(dimension_semantics=("arbitrary",),
                                             vmem_limit_bytes=VMEM_LIMIT_BYTES),
    )(w, g, m, v)


def _shift_down(ext, s):
    return pltpu.roll(ext, s, 0)


def _shift_up(ext, s):
    return pltpu.roll(ext, ext.shape[0] - s, 0)


def _lru_gates(xc, wa_ref, ba, wx_ref, bx, lam):
    pa, px = [], []
    for h in range(LRU_HEADS):
        xh = xc[:, h * HEAD_DIM:(h + 1) * HEAD_DIM].astype(BF16)
        pa.append(_dot(xh, wa_ref[h]))
        px.append(_dot(xh, wx_ref[h]))
    r = _sigmoid(jnp.concatenate(pa, axis=1) + ba)
    ig = _sigmoid(jnp.concatenate(px, axis=1) + bx)
    sp = _softplus(-lam)
    log_a = (-LRU_C * r) * sp
    a = jnp.exp(log_a)
    mult = jnp.sqrt(jnp.tanh(-log_a) * (1.0 + a * a))
    return r, ig, a, mult, sp


def _conv(ext, w_ref, b):
    y = b + _shift_down(ext, 3) * w_ref[0:1, :]
    y = y + _shift_down(ext, 2) * w_ref[1:2, :]
    y = y + _shift_down(ext, 1) * w_ref[2:3, :]
    y = y + ext * w_ref[3:4, :]
    return y[CONV_HIST:, :]


def _pool_diff(ext, pos):
    out = []
    for g, k in enumerate(POOL_WINDOWS):
        col = ext[:, g * POOL_GROUP_DIM:(g + 1) * POOL_GROUP_DIM]
        s = col
        for step in range(g + 1):
            s = s + _shift_down(s, 2 ** step)
        count = jnp.minimum(pos + 1, k).astype(F32)
        out.append(s[POOL_HIST:, :] / count - col[POOL_HIST:, :])
    return out


def _pool_mix(diff, pw_ref):
    return jnp.concatenate([_dot(diff[g].astype(BF16), pw_ref[g]) for g in range(len(POOL_WINDOWS))], axis=1)


def _branch_specs(tb, row_map, fixed):
    fixed3 = lambda i: (0, 0, 0)
    return [pl.BlockSpec((CONV_WIDTH, D_MODEL), fixed), pl.BlockSpec((1, D_MODEL), fixed),
            pl.BlockSpec((LRU_HEADS, HEAD_DIM, HEAD_DIM), fixed3), pl.BlockSpec((1, D_MODEL), fixed),
            pl.BlockSpec((LRU_HEADS, HEAD_DIM, HEAD_DIM), fixed3), pl.BlockSpec((1, D_MODEL), fixed),
            pl.BlockSpec((1, D_MODEL), fixed),
            pl.BlockSpec((len(POOL_WINDOWS), POOL_GROUP_DIM, POOL_GROUP_DIM), fixed3),
            pl.BlockSpec((1, POOL_WIDTH), fixed)]


def _branches_fwd(z, weights, seq, tb, shards):
    t = z.shape[0]
    nb = t // tb
    nbe = seq // tb
    groups = tb // F32_SUBLANES
    n = len(shards)

    def body(xa_ref, ga_ref, xb_ref, gb_ref, cw_ref, cb_ref, wa_ref, ba_ref, wx_ref, bx_ref, lam_ref,
             pw_ref, ps_ref, *refs):
        g_ins = refs[:n]
        ya_ref, yb_ref, hl_ref = refs[n:n + 3]
        g_outs = refs[n + 3:2 * n + 3]
        xa_ext, xb_ext, carry, a_s, u_s, send_sems, recv_sems, local_sems = refs[2 * n + 3:]
        blk = pl.program_id(0) % nbe
        start_gather, finish_gather = _gather_steps(shards, g_ins, g_outs, send_sems, recv_sems, local_sems)
        pl.when(pl.program_id(0) == 0)(start_gather)

        @pl.when(blk == 0)
        def _():
            xa_ext[0:CONV_HIST, :] = jnp.zeros((CONV_HIST, D_MODEL), F32)
            xb_ext[0:POOL_HIST, :] = jnp.zeros((POOL_HIST, POOL_WIDTH), F32)
            carry[...] = jnp.zeros_like(carry)

        xa_ext[CONV_HIST:, :] = xa_ref[...]
        xb_ext[POOL_HIST:, :] = xb_ref[...]
        ea = xa_ext[...]
        eb = xb_ext[...]
        xa_ext[0:CONV_HIST, :] = ea[tb:, :]
        xb_ext[0:POOL_HIST, :] = eb[tb:, :]

        xc = _conv(ea, cw_ref, cb_ref[...])
        _, ig, a, mult, _ = _lru_gates(xc, wa_ref, ba_ref[...], wx_ref, bx_ref[...], lam_ref[...])
        u = mult * (ig * xc)
        row8 = lax.broadcasted_iota(jnp.int32, (tb, D_MODEL), 0) % F32_SUBLANES
        for s in (1, 2, 4):
            m = row8 >= s
            u = jnp.where(m, a * _shift_down(u, s) + u, u)
            a = jnp.where(m, a * _shift_down(a, s), a)
        a_s[...] = a
        u_s[...] = u

        def step(g, cr):
            sl = pl.ds(pl.multiple_of(g * F32_SUBLANES, F32_SUBLANES), F32_SUBLANES)
            hb = a_s[sl, :] * cr + u_s[sl, :]
            hl_ref[sl, :] = hb
            return jnp.broadcast_to(hb[F32_SUBLANES - 1:F32_SUBLANES, :], (F32_SUBLANES, D_MODEL))

        carry[...] = lax.fori_loop(0, groups, step, carry[...], unroll=4)
        ga = ga_ref[...]
        ya_ref[...] = (hl_ref[...] * (ga * _sigmoid(ga))).astype(BF16)

        pos = blk * tb + lax.broadcasted_iota(jnp.int32, (tb, POOL_GROUP_DIM), 0)
        ypre = _pool_mix(_pool_diff(eb, pos), pw_ref)
        gb = gb_ref[...]
        yb_ref[...] = ((ypre * ps_ref[...]) * (gb * _sigmoid(gb))).astype(BF16)
        pl.when(pl.program_id(0) == nb - 1)(finish_gather)

    row = lambda i: (i, 0)
    fixed = lambda i: (0, 0)
    any_spec = pl.BlockSpec(memory_space=pl.ANY)
    in_specs = [pl.BlockSpec((tb, D_MODEL), lambda i: (i, 0)), pl.BlockSpec((tb, D_MODEL), lambda i: (i, 1)),
                pl.BlockSpec((tb, POOL_WIDTH), lambda i: (i, 4)), pl.BlockSpec((tb, POOL_WIDTH), lambda i: (i, 5)),
                ] + _branch_specs(tb, row, fixed) + [any_spec] * n
    g_shape, g_sems = _gather_shapes(shards)
    return pl.pallas_call(
        body, name="branches_fwd",
        out_shape=tuple([jax.ShapeDtypeStruct((t, D_MODEL), BF16), jax.ShapeDtypeStruct((t, POOL_WIDTH), BF16),
                         jax.ShapeDtypeStruct((t, D_MODEL), F32)] + g_shape),
        grid=(nb,), in_specs=in_specs,
        out_specs=tuple([pl.BlockSpec((tb, D_MODEL), row), pl.BlockSpec((tb, POOL_WIDTH), row),
                         pl.BlockSpec((tb, D_MODEL), row)] + [any_spec] * n),
        scratch_shapes=[pltpu.VMEM((tb + CONV_HIST, D_MODEL), F32), pltpu.VMEM((tb + POOL_HIST, POOL_WIDTH), F32),
                        pltpu.VMEM((F32_SUBLANES, D_MODEL), F32),
                        pltpu.VMEM((tb, D_MODEL), F32), pltpu.VMEM((tb, D_MODEL), F32)] + g_sems,
        compiler_params=pltpu.CompilerParams(dimension_semantics=("arbitrary",),
                                             vmem_limit_bytes=VMEM_LIMIT_BYTES),
    )(z, z, z, z, *weights, *[sh[0] for sh in shards])


def _branches_bwd(z, hl, dya, dyb, dzm, weights, seq, tb):
    t = z.shape[0]
    nb = t // tb
    nbe = seq // tb
    groups = tb // F32_SUBLANES
    n_pool = len(POOL_WINDOWS)

    def body(xa_ref, xap_ref, ga_ref, xb_ref, xbp_ref, gb_ref, hl_ref, hlp_ref, dya_ref, dyb_ref, dzm_ref,
             cw_ref, cb_ref, wa_ref, ba_ref, wx_ref, bx_ref, lam_ref, pw_ref, ps_ref,
             dz_ref, dcw_ref, dcb_ref, dwa_ref, dba_ref, dwx_ref, dbx_ref, dlam_ref, dpw_ref, dps_ref,
             xa_ext, xb_ext, hl_ext, a_ext, dxc_ext, dwin_ext, g_carry, b_s, d_s, g_s):
        i = pl.program_id(0)
        blk = (nb - 1 - i) % nbe

        @pl.when(i == 0)
        def _():
            for ref in (dcw_ref, dcb_ref, dwa_ref, dba_ref, dwx_ref, dbx_ref, dlam_ref, dpw_ref, dps_ref):
                ref[...] = jnp.zeros_like(ref)

        @pl.when(blk == nbe - 1)
        def _():
            a_ext[tb:, :] = jnp.zeros((F32_SUBLANES, D_MODEL), F32)
            dxc_ext[tb:, :] = jnp.zeros((CONV_HIST, D_MODEL), F32)
            dwin_ext[tb:, :] = jnp.zeros((POOL_HIST, POOL_WIDTH), F32)
            g_carry[...] = jnp.zeros_like(g_carry)

        live = (blk > 0).astype(F32)
        xa_ext[0:CONV_HIST, :] = xap_ref[...] * live
        xa_ext[CONV_HIST:, :] = xa_ref[...]
        xb_ext[0:POOL_HIST, :] = xbp_ref[...] * live
        xb_ext[POOL_HIST:, :] = xb_ref[...]
        hl_ext[0:F32_SUBLANES, :] = hlp_ref[...] * live
        hl_ext[F32_SUBLANES:, :] = hl_ref[...]
        ea = xa_ext[...]
        eb = xb_ext[...]

        xc = _conv(ea, cw_ref, cb_ref[...])
        lam = lam_ref[...]
        r, ig, a, mult, sp = _lru_gates(xc, wa_ref, ba_ref[...], wx_ref, bx_ref[...], lam)
        hl = hl_ref[...]
        ga = ga_ref[...]
        sga = _sigmoid(ga)
        dya = dya_ref[...]
        dhl = dya * (ga * sga)
        dz_ref[:, D_MODEL:2 * D_MODEL] = (dya * hl * (sga * (1.0 + ga * (1.0 - sga)))).astype(BF16)

        a_ext[0:tb, :] = a
        b = _shift_up(a_ext[...], 1)[0:tb, :]
        a_ext[tb:, :] = jnp.broadcast_to(a[0:1, :], (F32_SUBLANES, D_MODEL))
        d = dhl
        row8 = lax.broadcasted_iota(jnp.int32, (tb, D_MODEL), 0) % F32_SUBLANES
        for s in (1, 2, 4):
            m = row8 < F32_SUBLANES - s
            d = jnp.where(m, d + b * _shift_up(d, s), d)
            b = jnp.where(m, b * _shift_up(b, s), b)
        b_s[...] = b
        d_s[...] = d

        def step(k, cr):
            sl = pl.ds(pl.multiple_of((groups - 1 - k) * F32_SUBLANES, F32_SUBLANES), F32_SUBLANES)
            gb_ = d_s[sl, :] + b_s[sl, :] * cr
            g_s[sl, :] = gb_
            return jnp.broadcast_to(gb_[0:1, :], (F32_SUBLANES, D_MODEL))

        g_carry[...] = lax.fori_loop(0, groups, step, g_carry[...], unroll=4)
        gsc = g_s[...]
        da = gsc * _shift_down(hl_ext[...], 1)[F32_SUBLANES:, :]
        dmult = gsc * (ig * xc)
        dig = gsc * (mult * xc)
        dxc = gsc * (mult * ig)
        dlog_a = da * a - (a * a) * dmult / mult
        dr = dlog_a * (-LRU_C * sp)
        dlam_ref[...] += jnp.sum(dlog_a * (-LRU_C * r), axis=0, keepdims=True)
        dpa = dr * (r * (1.0 - r))
        dpx = dig * (ig * (1.0 - ig))
        dba_ref[...] += jnp.sum(dpa, axis=0, keepdims=True)
        dbx_ref[...] += jnp.sum(dpx, axis=0, keepdims=True)
        back = []
        for h in range(LRU_HEADS):
            cols = slice(h * HEAD_DIM, (h + 1) * HEAD_DIM)
            xh = xc[:, cols].astype(BF16)
            dpa_h = dpa[:, cols].astype(BF16)
            dpx_h = dpx[:, cols].astype(BF16)
            dwa_ref[h] += _dot_tn(xh, dpa_h)
            dwx_ref[h] += _dot_tn(xh, dpx_h)
            back.append(_dot_nt(dpa_h, wa_ref[h]) + _dot_nt(dpx_h, wx_ref[h]))
        dxc = dxc + jnp.concatenate(back, axis=1)
        dcb_ref[...] += jnp.sum(dxc, axis=0, keepdims=True)
        for k in range(CONV_WIDTH):
            tap = _shift_down(ea, CONV_WIDTH - 1 - k)[CONV_HIST:, :] if k < CONV_WIDTH - 1 else ea[CONV_HIST:, :]
            dcw_ref[k:k + 1, :] += jnp.sum(dxc * tap, axis=0, keepdims=True)
        dxc_ext[0:tb, :] = dxc
        ed = dxc_ext[...]
        dxa = ed * cw_ref[3:4, :]
        dxa = dxa + _shift_up(ed, 1) * cw_ref[2:3, :]
        dxa = dxa + _shift_up(ed, 2) * cw_ref[1:2, :]
        dxa = dxa + _shift_up(ed, 3) * cw_ref[0:1, :]
        dz_ref[:, 0:D_MODEL] = dxa[0:tb, :].astype(BF16)
        dxc_ext[tb:, :] = dxc[0:CONV_HIST, :]

        pos = blk * tb + lax.broadcasted_iota(jnp.int32, (tb, POOL_GROUP_DIM), 0)
        diff = _pool_diff(eb, pos)
        ypre = _pool_mix(diff, pw_ref)
        ps = ps_ref[...]
        gb = gb_ref[...]
        sgb = _sigmoid(gb)
        dyb = dyb_ref[...]
        dyp = dyb * (gb * sgb)
        dz_ref[:, 2 * D_MODEL + POOL_WIDTH:3 * D_MODEL] = (
            dyb * (ypre * ps) * (sgb * (1.0 + gb * (1.0 - sgb)))).astype(BF16)
        dps_ref[...] += jnp.sum(dyp * ypre, axis=0, keepdims=True)
        dypre = dyp * ps
        for g, k in enumerate(POOL_WINDOWS):
            cols = slice(g * POOL_GROUP_DIM, (g + 1) * POOL_GROUP_DIM)
            dyg = dypre[:, cols].astype(BF16)
            dpw_ref[g] += _dot_tn(diff[g].astype(BF16), dyg)
            ddiff = _dot_nt(dyg, pw_ref[g])
            count = jnp.minimum(pos + 1, k).astype(F32)
            dwin = ddiff / count
            dwin_ext[0:tb, cols] = dwin
            s = dwin_ext[:, cols]
            for step_ in range(g + 1):
                s = s + _shift_up(s, 2 ** step_)
            dz_ref[:, 2 * D_MODEL + g * POOL_GROUP_DIM:2 * D_MODEL + (g + 1) * POOL_GROUP_DIM] = (
                s[0:tb, :] - ddiff).astype(BF16)
            dwin_ext[tb:, cols] = dwin[0:POOL_HIST, :]

        dz_ref[:, 3 * D_MODEL:] = dzm_ref[...]

        @pl.when(i == nb - 1)
        def _():
            dlam_ref[...] = dlam_ref[...] * (-_sigmoid(-lam))

    rev = lambda i: (nb - 1 - i, 0)
    fixed = lambda i: (0, 0)
    fixed3 = lambda i: (0, 0, 0)

    def prev(rows, col):
        per = tb // rows
        return lambda i: (jnp.maximum((nb - 1 - i) * per - 1, 0), col)

    in_specs = [pl.BlockSpec((tb, D_MODEL), lambda i: (nb - 1 - i, 0)),
                pl.BlockSpec((CONV_HIST, D_MODEL), prev(CONV_HIST, 0)),
                pl.BlockSpec((tb, D_MODEL), lambda i: (nb - 1 - i, 1)),
                pl.BlockSpec((tb, POOL_WIDTH), lambda i: (nb - 1 - i, 4)),
                pl.BlockSpec((POOL_HIST, POOL_WIDTH), prev(POOL_HIST, 4)),
                pl.BlockSpec((tb, POOL_WIDTH), lambda i: (nb - 1 - i, 5)),
                pl.BlockSpec((tb, D_MODEL), rev),
                pl.BlockSpec((F32_SUBLANES, D_MODEL), prev(F32_SUBLANES, 0)),
                pl.BlockSpec((tb, D_MODEL), rev), pl.BlockSpec((tb, POOL_WIDTH), rev),
                pl.BlockSpec((tb, 2 * D_MODEL), rev)] + _branch_specs(tb, rev, fixed)
    out_shape = (jax.ShapeDtypeStruct((t, IN_COLS), BF16),
                 jax.ShapeDtypeStruct((CONV_WIDTH, D_MODEL), F32), jax.ShapeDtypeStruct((1, D_MODEL), F32),
                 jax.ShapeDtypeStruct((LRU_HEADS, HEAD_DIM, HEAD_DIM), F32), jax.ShapeDtypeStruct((1, D_MODEL), F32),
                 jax.ShapeDtypeStruct((LRU_HEADS, HEAD_DIM, HEAD_DIM), F32), jax.ShapeDtypeStruct((1, D_MODEL), F32),
                 jax.ShapeDtypeStruct((1, D_MODEL), F32),
                 jax.ShapeDtypeStruct((n_pool, POOL_GROUP_DIM, POOL_GROUP_DIM), F32),
                 jax.ShapeDtypeStruct((1, POOL_WIDTH), F32))
    out_specs = (pl.BlockSpec((tb, IN_COLS), rev),
                 pl.BlockSpec((CONV_WIDTH, D_MODEL), fixed), pl.BlockSpec((1, D_MODEL), fixed),
                 pl.BlockSpec((LRU_HEADS, HEAD_DIM, HEAD_DIM), fixed3), pl.BlockSpec((1, D_MODEL), fixed),
                 pl.BlockSpec((LRU_HEADS, HEAD_DIM, HEAD_DIM), fixed3), pl.BlockSpec((1, D_MODEL), fixed),
                 pl.BlockSpec((1, D_MODEL), fixed),
                 pl.BlockSpec((n_pool, POOL_GROUP_DIM, POOL_GROUP_DIM), fixed3),
                 pl.BlockSpec((1, POOL_WIDTH), fixed))
    scratch = [pltpu.VMEM((tb + CONV_HIST, D_MODEL), F32), pltpu.VMEM((tb + POOL_HIST, POOL_WIDTH), F32),
               pltpu.VMEM((tb + F32_SUBLANES, D_MODEL), F32), pltpu.VMEM((tb + F32_SUBLANES, D_MODEL), F32),
               pltpu.VMEM((tb + CONV_HIST, D_MODEL), F32), pltpu.VMEM((tb + POOL_HIST, POOL_WIDTH), F32),
               pltpu.VMEM((F32_SUBLANES, D_MODEL), F32),
               pltpu.VMEM((tb, D_MODEL), F32), pltpu.VMEM((tb, D_MODEL), F32), pltpu.VMEM((tb, D_MODEL), F32)]
    return pl.pallas_call(
        body, name="branches_bwd", out_shape=out_shape, grid=(nb,), in_specs=in_specs, out_specs=out_specs,
        scratch_shapes=scratch,
        compiler_params=pltpu.CompilerParams(dimension_semantics=("arbitrary",),
                                             vmem_limit_bytes=VMEM_LIMIT_BYTES),
    )(z, z, z, z, z, z, hl, hl, dya, dyb, dzm, *weights)


def _merge_head(x2d, ya, yb, z, p2d, tgt, w_pl, w_pp, w_out, w_pg, w_pe, g2, gf, tb):
    t = x2d.shape[0]
    p_dim = p2d.shape[1]

    def body(x_ref, ya_ref, yb_ref, ma_ref, mb_ref, p_ref, t_ref, wpl_ref, wpp_ref, wout_ref, wpg_ref, wpe_ref,
             g2_ref, gf_ref,
             loss_ref, dg2_ref, dgf_ref, dxr_ref, dya_ref, dyb_ref, dzm_ref,
             mg_ref, do_ref, hn_ref, dgp_ref, dpe_ref, da_ref, dbm_ref, pbf_ref):
        @pl.when(pl.program_id(0) == 0)
        def _():
            loss_ref[...] = jnp.zeros_like(loss_ref)
            dg2_ref[...] = jnp.zeros_like(dg2_ref)
            dgf_ref[...] = jnp.zeros_like(dgf_ref)

        a_ = _dot(ya_ref[...], wpl_ref[...])
        bm = _dot(yb_ref[...], wpp_ref[...])
        sa = _sigmoid(ma_ref[...])
        sb = _sigmoid(mb_ref[...])
        mg = (sa * a_ + sb * bm).astype(BF16)
        mg_ref[...] = mg
        x1 = x_ref[...] + _dot(mg, wout_ref[...])
        xn2, r2 = _rms(x1)
        g2 = g2_ref[...]
        hn = (xn2 * g2).astype(BF16)
        hn_ref[...] = hn
        gate = _sigmoid(_dot(hn, wpg_ref[...]))
        pbf = p_ref[...].astype(BF16)
        pbf_ref[...] = pbf
        pe = _dot(pbf, wpe_ref[...])
        x2 = x1 + gate * pe
        xn3, r3 = _rms(x2)
        gf = gf_ref[...]
        err = xn3 * gf - t_ref[...]
        loss_ref[...] += 0.5 * jnp.sum(jnp.mean(err * err, axis=-1))

        dy = err * (1.0 / D_MODEL)
        dgf_ref[...] += jnp.sum(dy * xn3, axis=0, keepdims=True)
        dx2 = _rms_bwd(dy * gf, xn3, r3)
        dpe_ref[...] = (dx2 * gate).astype(BF16)
        dgp = ((dx2 * pe) * (gate * (1.0 - gate))).astype(BF16)
        dgp_ref[...] = dgp
        dhn = _dot_nt(dgp, wpg_ref[...])
        dg2_ref[...] += jnp.sum(dhn * xn2, axis=0, keepdims=True)
        dx1 = dx2 + _rms_bwd(dhn * g2, xn2, r2)
        dxr_ref[...] = dx1
        do = dx1.astype(BF16)
        do_ref[...] = do
        dmg = _dot_nt(do, wout_ref[...])
        da = (dmg * sa).astype(BF16)
        dbm = (dmg * sb).astype(BF16)
        da_ref[...] = da
        dbm_ref[...] = dbm
        dzm_ref[:, 0:D_MODEL] = (dmg * a_ * (sa * (1.0 - sa))).astype(BF16)
        dzm_ref[:, D_MODEL:] = (dmg * bm * (sb * (1.0 - sb))).astype(BF16)
        dya_ref[...] = _dot_nt(da, wpl_ref[...])
        dyb_ref[...] = _dot_nt(dbm, wpp_ref[...])

    row = lambda i: (i, 0)
    fixed = lambda i: (0, 0)

    def resident(shape):
        return pl.BlockSpec(shape, fixed, pipeline_mode=pl.Buffered(1))

    tok = lambda width: pl.BlockSpec((tb, width), row)
    in_specs = [tok(D_MODEL), tok(D_MODEL), tok(POOL_WIDTH),
                pl.BlockSpec((tb, D_MODEL), lambda i: (i, 3)), pl.BlockSpec((tb, D_MODEL), lambda i: (i, 4)),
                tok(p_dim), tok(D_MODEL),
                resident((D_MODEL, D_MODEL)), resident((POOL_WIDTH, D_MODEL)), resident((D_MODEL, D_MODEL)),
                resident((D_MODEL, D_MODEL)), resident((p_dim, D_MODEL)),
                pl.BlockSpec((1, D_MODEL), fixed), pl.BlockSpec((1, D_MODEL), fixed)]
    bf = lambda width: jax.ShapeDtypeStruct((t, width), BF16)
    f32 = lambda width: jax.ShapeDtypeStruct((t, width), F32)
    out_shape = (jax.ShapeDtypeStruct((F32_SUBLANES, 128), F32), jax.ShapeDtypeStruct((1, D_MODEL), F32),
                 jax.ShapeDtypeStruct((1, D_MODEL), F32),
                 f32(D_MODEL), f32(D_MODEL), f32(POOL_WIDTH), bf(2 * D_MODEL),
                 bf(D_MODEL), bf(D_MODEL), bf(D_MODEL), bf(D_MODEL), bf(D_MODEL), bf(D_MODEL), bf(D_MODEL), bf(p_dim))
    out_specs = (pl.BlockSpec((F32_SUBLANES, 128), fixed), pl.BlockSpec((1, D_MODEL), fixed),
                 pl.BlockSpec((1, D_MODEL), fixed),
                 tok(D_MODEL), tok(D_MODEL), tok(POOL_WIDTH), tok(2 * D_MODEL),
                 tok(D_MODEL), tok(D_MODEL), tok(D_MODEL), tok(D_MODEL), tok(D_MODEL), tok(D_MODEL), tok(D_MODEL),
                 tok(p_dim))
    return pl.pallas_call(
        body, name="merge_head", out_shape=out_shape, grid=(t // tb,), in_specs=in_specs, out_specs=out_specs,
        compiler_params=pltpu.CompilerParams(dimension_semantics=("arbitrary",),
                                             vmem_limit_bytes=VMEM_LIMIT_BYTES),
    )(x2d, ya, yb, z, z, p2d, tgt, w_pl, w_pp, w_out, w_pg, w_pe, g2, gf)


def _pad_rows(a, rows):
    return jnp.pad(a, ((0, rows - a.shape[0]), (0, D_MODEL - a.shape[1])))


def _pack_bag(parts, tail=None):
    rows = [_pad_rows(a.reshape(-1, a.shape[-1]) if a.shape[-1] != HEAD_DIM else a.reshape(-1, D_MODEL), n)
            for a, n in zip(parts, BAG_PART_ROWS)]
    spare = BAG_ROWS - sum(BAG_PART_ROWS)
    if tail is not None:
        rows.append(_pad_rows(tail, F32_SUBLANES))
        spare -= F32_SUBLANES
    rows.append(jnp.zeros((spare, D_MODEL), F32))
    return jnp.concatenate(rows, axis=0)


def _unpack_bag(bag, shapes):
    out, at = [], 0
    for shape, n in zip(shapes, BAG_PART_ROWS):
        size = 1
        for s in shape:
            size *= s
        if size % D_MODEL == 0:
            piece = bag[at:at + size // D_MODEL, :]
        else:
            piece = bag[at:at + 1, :size]
        out.append(piece.reshape(shape))
        at += n
    return out


def kernel(x, p, norm_g, w_in, conv_w, conv_b, lru_w_a, lru_b_a, lru_w_x, lru_b_x, lru_lambda, pool_w, pool_scale, w_proj_lru, w_proj_pool, w_out, ple_norm_g, w_ple_gate, w_ple_proj, final_g, loss_target, m_norm_g, m_w_in, m_conv_w, m_conv_b, m_lru_w_a, m_lru_b_a, m_lru_w_x, m_lru_b_x, m_lru_lambda, m_pool_w, m_pool_scale, m_w_proj_lru, m_w_proj_pool, m_w_out, m_ple_norm_g, m_w_ple_gate, m_w_ple_proj, m_final_g, v_norm_g, v_w_in, v_conv_w, v_conv_b, v_lru_w_a, v_lru_b_a, v_lru_w_x, v_lru_b_x, v_lru_lambda, v_pool_w, v_pool_scale, v_w_proj_lru, v_w_proj_pool, v_w_out, v_ple_norm_g, v_w_ple_gate, v_w_ple_proj, v_final_g):
    bsz, seq, _ = x.shape
    t = bsz * seq
    tb_mm = min(512, seq)
    tb_seq = min(256, seq // 2) if seq >= 512 else seq
    x2d = x.reshape(t, D_MODEL)
    p2d = p.reshape(t, p.shape[-1])
    tgt = loss_target.reshape(t, D_MODEL)
    chip = 2 * lax.axis_index("x") + lax.axis_index("y")

    rest = [(w_proj_lru[0], 0), (w_proj_pool[0], 1), (w_out[0], 0), (w_ple_gate[0], 0), (w_ple_proj[0], 1)]
    z, w_in_f, conv_w_f = _in_proj_gather(x2d, norm_g, w_in[0].astype(BF16), [(conv_w[0], 1, False)], tb_mm)

    wa_bf = lru_w_a[0].astype(BF16)
    wx_bf = lru_w_x[0].astype(BF16)
    pw_bf = pool_w[0].astype(BF16)
    branch_w = (conv_w_f, conv_b, wa_bf, lru_b_a.reshape(1, D_MODEL), wx_bf, lru_b_x.reshape(1, D_MODEL),
                lru_lambda, pw_bf, pool_scale)

    ya, yb, hl, w_pl_f, w_pp_f, w_out_f, w_pg_f, w_pe_f = _branches_fwd(
        z, branch_w, seq, tb_seq, [(w.astype(BF16), axis, True) for w, axis in rest])
    (loss_acc, d_g2, d_gf, dx_res, dya, dyb, dzm, mg_bf, do_bf, hn_bf, dgp_bf, dpe_bf, da_bf, dbm_bf, p_bf) = _merge_head(
        x2d, ya, yb, z, p2d, tgt, w_pl_f, w_pp_f, w_out_f, w_pg_f, w_pe_f, ple_norm_g, final_g.reshape(1, D_MODEL),
        tb_seq)
    (dz, d_cw, d_cb, d_wa, d_ba, d_wx, d_bx, d_lam, d_pw, d_ps) = _branches_bwd(
        z, hl, dya, dyb, dzm, branch_w, seq, tb_seq)
    dx, h_bf, d_g1 = _in_proj_bwd(dz, w_in_f, x2d, dx_res, norm_g, tb_mm)

    g_in = _weight_grad(h_bf, dz, N_CHIPS, tb_mm, "dw_in").reshape(8, D_MODEL // 2, IN_COLS // N_CHIPS)
    g_pl = _weight_grad(ya, da_bf, 1, tb_mm, "dw_proj_lru").reshape(8, D_MODEL // 8, D_MODEL)
    g_pp = _weight_grad(yb, dbm_bf, N_CHIPS, tb_mm, "dw_proj_pool").reshape(8, POOL_WIDTH // 2, D_MODEL // N_CHIPS)
    g_out = _weight_grad(mg_bf, do_bf, 1, tb_mm, "dw_out").reshape(8, D_MODEL // 8, D_MODEL)
    g_pg = _weight_grad(hn_bf, dgp_bf, 1, tb_mm, "dw_ple_gate").reshape(8, D_MODEL // 8, D_MODEL)
    p_dim = p2d.shape[1]
    g_pe = _weight_grad(p_bf, dpe_bf, N_CHIPS, tb_mm, "dw_ple_proj").reshape(8, p_dim // 2, D_MODEL // N_CHIPS)

    (r_in,) = _reduce_scatter([g_in], "rs_w_in", BF16)
    r_pl, r_pp, r_out, r_pg, r_pe = _reduce_scatter([g_pl, g_pp, g_out, g_pg, g_pe], "rs_proj", BF16)
    small_shapes = [(1, D_MODEL), (1, CONV_WIDTH, D_MODEL), (1, D_MODEL), lru_w_a.shape, lru_b_a.shape, lru_w_x.shape,
                    lru_b_x.shape, (1, D_MODEL), pool_w.shape, pool_scale.shape, (1, D_MODEL), final_g.shape]
    bag = _pack_bag([d_g1, d_cw, d_cb, d_wa, d_ba.reshape(1, D_MODEL), d_wx, d_bx.reshape(1, D_MODEL), d_lam, d_pw,
                     d_ps, d_g2, d_gf], tail=loss_acc)
    (bag_mine,) = _reduce_scatter([bag.reshape(8, BAG_ROWS // 8, D_MODEL)], "rs_small")
    (bag_sum,) = _gather_shards([(bag_mine.reshape(BAG_ROWS // N_CHIPS, D_MODEL), 0, True)], "gather_small")
    (g_g1, g_cw_full, g_cb, g_wa, g_ba, g_wx, g_bx, g_lam, g_pw, g_ps, g_g2, g_gf) = _unpack_bag(bag_sum, small_shapes)
    cw_cols = D_MODEL // N_CHIPS
    g_cw = lax.dynamic_slice_in_dim(g_cw_full, chip * cw_cols, cw_cols, axis=2)

    def big_update(w, g2d, m, v, rows, name):
        d, nm, nv = _adamw(w[0], g2d, m[0], v[0], rows, name)
        return g2d[None], d[None], nm[None], nv[None]

    u_in = big_update(w_in, r_in.reshape(D_MODEL, IN_COLS // N_CHIPS), m_w_in, v_w_in, 256, "adamw_w_in")
    u_pl = big_update(w_proj_lru, r_pl.reshape(D_MODEL // N_CHIPS, D_MODEL), m_w_proj_lru, v_w_proj_lru, 256, "adamw_w_proj_lru")
    u_pp = big_update(w_proj_pool, r_pp.reshape(POOL_WIDTH, D_MODEL // N_CHIPS), m_w_proj_pool, v_w_proj_pool, 512, "adamw_w_proj_pool")
    u_out = big_update(w_out, r_out.reshape(D_MODEL // N_CHIPS, D_MODEL), m_w_out, v_w_out, 256, "adamw_w_out")
    u_pg = big_update(w_ple_gate, r_pg.reshape(D_MODEL // N_CHIPS, D_MODEL), m_w_ple_gate, v_w_ple_gate, 256, "adamw_w_ple_gate")
    u_pe = big_update(w_ple_proj, r_pe.reshape(p_dim, D_MODEL // N_CHIPS), m_w_ple_proj, v_w_ple_proj, 256, "adamw_w_ple_proj")
    u_cw = big_update(conv_w, g_cw[0], m_conv_w, v_conv_w, CONV_WIDTH, "adamw_conv_w")

    small_w = [norm_g, None, conv_b, lru_w_a, lru_b_a, lru_w_x, lru_b_x, lru_lambda, pool_w, pool_scale, ple_norm_g, final_g]
    small_m = [m_norm_g, None, m_conv_b, m_lru_w_a, m_lru_b_a, m_lru_w_x, m_lru_b_x, m_lru_lambda, m_pool_w, m_pool_scale, m_ple_norm_g, m_final_g]
    small_v = [v_norm_g, None, v_conv_b, v_lru_w_a, v_lru_b_a, v_lru_w_x, v_lru_b_x, v_lru_lambda, v_pool_w, v_pool_scale, v_ple_norm_g, v_final_g]
    fill = jnp.zeros((CONV_WIDTH, D_MODEL), F32)

    def bag_of(arrs):
        return _pack_bag([fill if a is None else (a[0] if a.ndim > 1 else a[None]) for a in arrs])

    d_bag, m_bag, v_bag = _adamw(bag_of(small_w), bag_sum, bag_of(small_m), bag_of(small_v), BAG_ROWS // 8, "adamw_small")
    d_small = _unpack_bag(d_bag, small_shapes)
    m_small = _unpack_bag(m_bag, small_shapes)
    v_small = _unpack_bag(v_bag, small_shapes)

    loss = bag_sum[sum(BAG_PART_ROWS), 0]
    grad_x = dx.reshape(bsz, seq, D_MODEL)

    def ordered(small, pick):
        s = list(small)
        return [s[0], u_in[pick], u_cw[pick], s[2], s[3], s[4], s[5], s[6], s[7], s[8], s[9],
                u_pl[pick], u_pp[pick], u_out[pick], s[10], u_pg[pick], u_pe[pick], s[11]]

    grads = ordered([g_g1, None, g_cb, g_wa, g_ba, g_wx, g_bx, g_lam, g_pw, g_ps, g_g2, g_gf], 0)
    return (loss, grad_x, *grads, *ordered(d_small, 1), *ordered(m_small, 2), *ordered(v_small, 3))
```

```python
import functools

import jax
import jax.numpy as jnp
from jax import lax
from jax.experimental import pallas as pl
from jax.experimental.pallas import tpu as pltpu

F32 = jnp.float32
BF16 = jnp.bfloat16
MESH = pl.DeviceIdType.MESH
ALL_AXES = ("x", "y", "c")

D_MODEL = 1024
LRU_HEADS = 8
HEAD_DIM = 128
CONV_WIDTH = 4
LRU_C = 8.0
POOL_WIDTH = 512
POOL_WINDOWS = (2, 4, 8, 16)
POOL_GROUP_DIM = 128
IN_COLS = 5120
N_CHIPS = 4
EPS = 1e-6

ADAM_LR = 0.001
ADAM_B1 = 0.9
ADAM_B2 = 0.999
ADAM_EPS = 1e-08
ADAM_WD = 0.01
ADAM_STEP = 10

F32_SUBLANES = 8
CONV_HIST = 8
POOL_HIST = 16
VMEM_LIMIT_BYTES = 58 * 1024 * 1024
BAG_PART_ROWS = (8, 8, 8, 128, 8, 128, 8, 8, 64, 8, 8, 8)
BAG_ROWS = 448


def _dot(a, b):
    return jnp.dot(a, b, preferred_element_type=F32)


def _dot_nt(a, b):
    return lax.dot_general(a, b, (((1,), (1,)), ((), ())), preferred_element_type=F32)


def _dot_tn(a, b):
    return lax.dot_general(a, b, (((0,), (0,)), ((), ())), preferred_element_type=F32)


def _sigmoid(v):
    return jax.nn.sigmoid(v)


def _softplus(v):
    return jnp.maximum(v, 0.0) + jnp.log1p(jnp.exp(-jnp.abs(v)))


def _place():
    return lax.axis_index("x"), lax.axis_index("y"), lax.axis_index("c")


GATHER_SEMS = 6


def _gather_shapes(shards):
    out_shape = []
    for arr, axis, _ in shards:
        r, cols = arr.shape
        out_shape.append(jax.ShapeDtypeStruct((N_CHIPS * r, cols) if axis == 0 else (r, N_CHIPS * cols), arr.dtype))
    n = len(shards)
    sems = [pltpu.SemaphoreType.DMA((n * GATHER_SEMS,)), pltpu.SemaphoreType.DMA((n * GATHER_SEMS,)),
            pltpu.SemaphoreType.DMA((n,))]
    return out_shape, sems


def _gather_steps(shards, ins, outs, send_sems, recv_sems, local_sems):
    n = len(shards)
    x, y, c = _place()
    me, sibling = (x, y, c), (x, y, 1 - c)
    chips = [(x, 1 - y), (1 - x, y), (1 - x, 1 - y)]

    def region(k, cx, cy, hc):
        (r, cols), axis = shards[k][0].shape, shards[k][1]
        j = 2 * cx + cy
        if axis == 0:
            if hc is None:
                return outs[k].at[pl.ds(j * r, r), :]
            return outs[k].at[pl.ds(j * r + hc * (r // 2), r // 2), :]
        if hc is None:
            return outs[k].at[:, pl.ds(j * cols, cols)]
        return outs[k].at[pl.ds(hc * (r // 2), r // 2), pl.ds(j * cols, cols)]

    def remote(k, sem, block, to, src=None):
        dst = region(k, *block)
        return pltpu.make_async_remote_copy(
            src_ref=dst if src is None else src, dst_ref=dst,
            send_sem=send_sems.at[k * GATHER_SEMS + sem], recv_sem=recv_sems.at[k * GATHER_SEMS + sem],
            device_id=to, device_id_type=MESH)

    def first(k, idx):
        r, split = shards[k][0].shape[0], shards[k][2]
        src = ins[k].at[pl.ds(c * (r // 2), r // 2), :] if split else ins[k]
        return remote(k, idx, (x, y, c if split else None), (*chips[idx], c), src=src)

    def relay(k):
        src_chip = (jnp.bitwise_xor(x, 1 - c), jnp.bitwise_xor(y, c))
        dst_chip = (jnp.bitwise_xor(x, c), jnp.bitwise_xor(y, 1 - c))
        return remote(k, 2, (*src_chip, c), (*dst_chip, c))

    def passed(k, idx):
        return remote(k, 3 + idx, (*chips[idx], c), sibling)

    def mine(k):
        return pltpu.make_async_copy(ins[k], region(k, x, y, None), local_sems.at[k])

    def start():
        for k in range(n):
            mine(k).start()
            for idx in range(2 if shards[k][2] else 3):
                first(k, idx).start()

    def finish():
        for k in range(n):
            split = shards[k][2]
            for idx in range(2):
                remote(k, idx, (*chips[idx], c if split else None), me).wait_recv()
            if split:
                relay(k).start()
                passed(k, 0).start()
                passed(k, 1).start()
        for k in range(n):
            split = shards[k][2]
            remote(k, 2, (*chips[2], c if split else None), me).wait_recv()
            if split:
                passed(k, 2).start()
        for k in range(n):
            if shards[k][2]:
                for idx in range(3):
                    remote(k, 3 + idx, (*chips[idx], 1 - c), me).wait_recv()
        for k in range(n):
            if shards[k][2]:
                for cp in (first(k, 0), first(k, 1), relay(k), passed(k, 0), passed(k, 1), passed(k, 2)):
                    cp.wait_send()
            else:
                for idx in range(3):
                    first(k, idx).wait_send()
            mine(k).wait()

    return start, finish


def _gather_shards(shards, name):
    n = len(shards)

    def body(*refs):
        start, finish = _gather_steps(shards, refs[:n], refs[n:2 * n], *refs[2 * n:])
        start()
        finish()

    out_shape, sems = _gather_shapes(shards)
    any_spec = pl.BlockSpec(memory_space=pl.ANY)
    return pl.pallas_call(
        body, name=name, out_shape=tuple(out_shape),
        in_specs=[any_spec] * n, out_specs=tuple([any_spec] * n), scratch_shapes=sems,
    )(*[s[0] for s in shards])


RS_ADD_ROWS = (64, 56, 32, 16, 8)


def _reduce_scatter(parts, name, wire=F32):
    n = len(parts)
    n_sem = 8

    def body(*refs):
        ins, outs = refs[:n], refs[n:2 * n]
        own = refs[2 * n:3 * n]
        sib = refs[3 * n:4 * n]
        got = refs[4 * n:5 * n]
        fin = refs[5 * n:6 * n]
        snd = refs[6 * n:7 * n]
        send_sems, recv_sems, local_sems = refs[7 * n:]
        x, y, c = _place()
        j_me = 2 * x + y
        chips = [(1 - x, y), (x, 1 - y), (1 - x, 1 - y)]

        def remote(a, sem, src, dst, to):
            return pltpu.make_async_remote_copy(
                src_ref=src, dst_ref=dst, send_sem=send_sems.at[a * n_sem + sem],
                recv_sem=recv_sems.at[a * n_sem + sem], device_id=to, device_id_type=MESH)

        def rows_loop(a, fn):
            r = parts[a].shape[1]
            step = max(s for s in RS_ADD_ROWS if r % s == 0)

            def it(i, carry):
                fn(pl.ds(pl.multiple_of(i * step, step), step))
                return carry

            lax.fori_loop(0, r // step, it, 0)

        loads, sends = [], []
        for a in range(n):
            for jj in range(N_CHIPS):
                cp = pltpu.make_async_copy(ins[a].at[2 * jj + c], own[a].at[jj], local_sems.at[a * 5 + jj])
                cp.start()
                loads.append(cp)
                sd = remote(a, jj, ins[a].at[2 * jj + (1 - c)], sib[a].at[jj], (x, y, 1 - c))
                sd.start()
                sends.append(sd)
        for a in range(n):
            for jj in range(N_CHIPS):
                loads[a * N_CHIPS + jj].wait()
                remote(a, jj, sib[a].at[jj], sib[a].at[jj], (x, y, c)).wait_recv()

                def add(sl, a=a, jj=jj):
                    q = own[a][jj, sl, :] + sib[a][jj, sl, :]
                    own[a][jj, sl, :] = q
                    snd[a][jj, sl, :] = q.astype(wire)

                rows_loop(a, add)
        for a in range(n):
            for idx, chip in enumerate(chips):
                sd = remote(a, 4 + idx, snd[a].at[2 * chip[0] + chip[1]], got[a].at[j_me], (*chip, c))
                sd.start()
                sends.append(sd)
        for a in range(n):
            def keep(sl, a=a):
                got[a][j_me, sl, :] = snd[a][j_me, sl, :]

            rows_loop(a, keep)
        for a in range(n):
            for idx, chip in enumerate(chips):
                slot = got[a].at[2 * chip[0] + chip[1]]
                remote(a, 4 + idx, slot, slot, (x, y, c)).wait_recv()

            def total(sl, a=a):
                mine = own[a][j_me, sl, :]
                term = [jnp.where(j_me == jj, mine, got[a][jj, sl, :].astype(F32)) for jj in range(N_CHIPS)]
                fin[a][sl, :] = ((term[0] + term[1]) + term[2]) + term[3]

            rows_loop(a, total)
        stores = []
        for a in range(n):
            st = pltpu.make_async_copy(fin[a], outs[a].at[c], local_sems.at[a * 5 + 4])
            st.start()
            stores.append(st)
            sd = remote(a, 7, fin[a], outs[a].at[c], (x, y, 1 - c))
            sd.start()
            sends.append(sd)
        for a in range(n):
            remote(a, 7, outs[a].at[1 - c], outs[a].at[1 - c], (x, y, c)).wait_recv()
        for cp in sends:
            cp.wait_send()
        for cp in stores:
            cp.wait()

    any_spec = pl.BlockSpec(memory_space=pl.ANY)
    scratch = []
    for lead, dtype in ((N_CHIPS, F32), (N_CHIPS, F32), (N_CHIPS, wire), (None, F32), (N_CHIPS, wire)):
        for p in parts:
            shape = p.shape[1:] if lead is None else (lead,) + p.shape[1:]
            scratch.append(pltpu.VMEM(shape, dtype))
    scratch += [pltpu.SemaphoreType.DMA((n * n_sem,)), pltpu.SemaphoreType.DMA((n * n_sem,)),
                pltpu.SemaphoreType.DMA((n * 5,))]
    return pl.pallas_call(
        body, name=name,
        out_shape=tuple(jax.ShapeDtypeStruct((2,) + p.shape[1:], F32) for p in parts),
        in_specs=[any_spec] * n, out_specs=tuple([any_spec] * n), scratch_shapes=scratch,
        compiler_params=pltpu.CompilerParams(vmem_limit_bytes=VMEM_LIMIT_BYTES),
    )(*parts)


def _rms(x):
    r = lax.rsqrt(jnp.mean(x * x, axis=-1, keepdims=True) + EPS)
    return x * r, r


def _rms_bwd(dxn, xn, r):
    return r * (dxn - xn * jnp.mean(dxn * xn, axis=-1, keepdims=True))


def _in_proj_gather(x2d, norm_g, w_in_sh, shards, tb):
    t = x2d.shape[0]
    nb = t // tb
    cols = IN_COLS // N_CHIPS
    half = D_MODEL // 2
    n = len(shards)

    def body(x_ref, g_ref, win_ref, *refs):
        ins = refs[:n]
        z_ref, wfull_ref = refs[n], refs[n + 1]
        outs = refs[n + 2:2 * n + 2]
        wv, send_sems, recv_sems, local_sems, w_send, w_recv, w_local = refs[2 * n + 2:]
        s, i = pl.program_id(0), pl.program_id(1)
        x, y, c = _place()
        me, sibling = (x, y, c), (x, y, 1 - c)
        chips = [(x, 1 - y), (1 - x, y), (1 - x, 1 - y)]

        def w_half(cx, cy, hc):
            return wv.at[2 * cx + cy, pl.ds(hc * half, half), :]

        def w_remote(sem, block, to, src=None):
            dst = w_half(*block)
            return pltpu.make_async_remote_copy(
                src_ref=dst if src is None else src, dst_ref=dst, send_sem=w_send.at[sem],
                recv_sem=w_recv.at[sem], device_id=to, device_id_type=MESH)

        def w_first(idx):
            return w_remote(idx, (x, y, c), (*chips[idx], c), src=win_ref.at[pl.ds(c * half, half), :])

        def w_relay():
            src_chip = (jnp.bitwise_xor(x, 1 - c), jnp.bitwise_xor(y, c))
            dst_chip = (jnp.bitwise_xor(x, c), jnp.bitwise_xor(y, 1 - c))
            return w_remote(2, (*src_chip, c), (*dst_chip, c))

        def w_pass(idx):
            return w_remote(3 + idx, (*chips[idx], c), sibling)

        def w_store(k, cx, cy):
            jj = 2 * cx + cy
            return pltpu.make_async_copy(wv.at[jj], wfull_ref.at[:, pl.ds(jj * cols, cols)], w_local.at[k])

        start_rest, finish_rest = _gather_steps(shards, ins, outs, send_sems, recv_sems, local_sems)
        own = pltpu.make_async_copy(win_ref, wv.at[2 * x + y], w_local.at[4])

        @pl.when((s == 0) & (i == 0))
        def _():
            own.start()
            w_first(0).start()
            w_first(1).start()
            start_rest()
            own.wait()
            w_store(0, x, y).start()

        @pl.when((s == 1) & (i == 0))
        def _():
            w_remote(0, (*chips[0], c), me).wait_recv()
            w_remote(1, (*chips[1], c), me).wait_recv()
            w_relay().start()
            w_pass(0).start()
            w_pass(1).start()
            w_remote(3, (*chips[0], 1 - c), me).wait_recv()
            w_store(1, *chips[0]).start()

        @pl.when((s == 2) & (i == 0))
        def _():
            w_remote(4, (*chips[1], 1 - c), me).wait_recv()
            w_store(2, *chips[1]).start()

        @pl.when((s == 3) & (i == 0))
        def _():
            w_remote(2, (*chips[2], c), me).wait_recv()
            w_pass(2).start()
            w_remote(5, (*chips[2], 1 - c), me).wait_recv()
            w_store(3, *chips[2]).start()

        xn, _ = _rms(x_ref[...])
        z_ref[...] = _dot((xn * g_ref[...]).astype(BF16), wv[jnp.bitwise_xor(2 * x + y, s)])

        @pl.when((s == N_CHIPS - 1) & (i == nb - 1))
        def _():
            finish_rest()
            for cp in (w_first(0), w_first(1), w_relay(), w_pass(0), w_pass(1), w_pass(2)):
                cp.wait_send()
            w_store(0, x, y).wait()
            for idx in range(3):
                w_store(idx + 1, *chips[idx]).wait()

    rest_shape, rest_sems = _gather_shapes(shards)
    out_shape = [jax.ShapeDtypeStruct((t, IN_COLS), F32), jax.ShapeDtypeStruct((D_MODEL, IN_COLS), BF16)] + rest_shape
    any_spec = pl.BlockSpec(memory_space=pl.ANY)

    def z_map(s, i):
        return (i, jnp.bitwise_xor(2 * lax.axis_index("x") + lax.axis_index("y"), s))

    return pl.pallas_call(
        body, name="in_proj", out_shape=tuple(out_shape),
        grid=(N_CHIPS, nb),
        in_specs=[pl.BlockSpec((tb, D_MODEL), lambda s, i: (i, 0)),
                  pl.BlockSpec((1, D_MODEL), lambda s, i: (0, 0)), any_spec] + [any_spec] * n,
        out_specs=tuple([pl.BlockSpec((tb, cols), z_map), any_spec] + [any_spec] * n),
        scratch_shapes=[pltpu.VMEM((N_CHIPS, D_MODEL, cols), BF16)] + rest_sems + [
            pltpu.SemaphoreType.DMA((GATHER_SEMS,)), pltpu.SemaphoreType.DMA((GATHER_SEMS,)),
            pltpu.SemaphoreType.DMA((N_CHIPS + 1,))],
        compiler_params=pltpu.CompilerParams(dimension_semantics=("arbitrary", "arbitrary"),
                                             vmem_limit_bytes=VMEM_LIMIT_BYTES),
    )(x2d, norm_g, w_in_sh, *[sh[0] for sh in shards])


def _in_proj_bwd(dz, w_in, x2d, dx_res, norm_g, tb):
    t = x2d.shape[0]

    def body(dz_ref, w_ref, x_ref, dres_ref, g_ref, dx_ref, h_ref, dg_ref):
        @pl.when(pl.program_id(0) == 0)
        def _():
            dg_ref[...] = jnp.zeros_like(dg_ref)

        xn, r = _rms(x_ref[...])
        g = g_ref[...]
        h_ref[...] = (xn * g).astype(BF16)
        dh = _dot_nt(dz_ref[...], w_ref[...])
        dg_ref[...] += jnp.sum(dh * xn, axis=0, keepdims=True)
        dx_ref[...] = dres_ref[...] + _rms_bwd(dh * g, xn, r)

    row = lambda i: (i, 0)
    fixed = lambda i: (0, 0)
    return pl.pallas_call(
        body, name="in_proj_bwd",
        out_shape=(jax.ShapeDtypeStruct((t, D_MODEL), F32), jax.ShapeDtypeStruct((t, D_MODEL), BF16),
                   jax.ShapeDtypeStruct((1, D_MODEL), F32)),
        grid=(t // tb,),
        in_specs=[pl.BlockSpec((tb, IN_COLS), row),
                  pl.BlockSpec((D_MODEL, IN_COLS), fixed, pipeline_mode=pl.Buffered(1)),
                  pl.BlockSpec((tb, D_MODEL), row), pl.BlockSpec((tb, D_MODEL), row),
                  pl.BlockSpec((1, D_MODEL), fixed)],
        out_specs=(pl.BlockSpec((tb, D_MODEL), row), pl.BlockSpec((tb, D_MODEL), row),
                   pl.BlockSpec((1, D_MODEL), fixed)),
        compiler_params=pltpu.CompilerParams(dimension_semantics=("arbitrary",),
                                             vmem_limit_bytes=VMEM_LIMIT_BYTES),
    )(dz, w_in, x2d, dx_res, norm_g)


def _weight_grad(lhs, rhs, n_chunks, tb, name):
    t, k = lhs.shape
    nc = rhs.shape[1] // n_chunks

    def body(l_ref, r_ref, o_ref):
        @pl.when(pl.program_id(1) == 0)
        def _():
            o_ref[...] = jnp.zeros_like(o_ref)

        o_ref[...] += _dot_tn(l_ref[...], r_ref[...])

    return pl.pallas_call(
        body, name=name, out_shape=jax.ShapeDtypeStruct((n_chunks, k, nc), F32),
        grid=(n_chunks, t // tb),
        in_specs=[pl.BlockSpec((tb, k), lambda j, i: (i, 0)), pl.BlockSpec((tb, nc), lambda j, i: (i, j))],
        out_specs=pl.BlockSpec((None, k, nc), lambda j, i: (j, 0, 0)),
        compiler_params=pltpu.CompilerParams(dimension_semantics=("arbitrary", "arbitrary"),
                                             vmem_limit_bytes=VMEM_LIMIT_BYTES),
    )(lhs, rhs)


def _adamw(w, g, m, v, rows, name):
    r, c = w.shape

    def body(w_ref, g_ref, m_ref, v_ref, d_ref, nm_ref, nv_ref):
        g_ = g_ref[...]
        m_ = ADAM_B1 * m_ref[...] + (1.0 - ADAM_B1) * g_
        v_ = ADAM_B2 * v_ref[...] + (1.0 - ADAM_B2) * jnp.square(g_)
        m_hat = m_ / (1.0 - ADAM_B1 ** ADAM_STEP)
        v_hat = v_ / (1.0 - ADAM_B2 ** ADAM_STEP)
        d_ref[...] = -ADAM_LR * (m_hat / (jnp.sqrt(v_hat) + ADAM_EPS) + ADAM_WD * w_ref[...])
        nm_ref[...] = m_
        nv_ref[...] = v_

    spec = pl.BlockSpec((rows, c), lambda i: (i, 0))
    return pl.pallas_call(
        body, name=name, out_shape=tuple(jax.ShapeDtypeStruct((r, c), F32) for _ in range(3)),
        grid=(r // rows,), in_specs=[spec] * 4, out_specs=(spec,) * 3,
        compiler_params=pltpu.CompilerParams(dimension_semantics=("arbitrary",),
                                             vmem_limit_bytes=VMEM_LIMIT_BYTES),
    )(w, g, m, v)


def _shift_down(ext, s):
    return pltpu.roll(ext, s, 0)


def _shift_up(ext, s):
    return pltpu.roll(ext, ext.shape[0] - s, 0)


def _lru_gates(xc, wa_ref, ba, wx_ref, bx, lam):
    pa, px = [], []
    for h in range(LRU_HEADS):
        xh = xc[:, h * HEAD_DIM:(h + 1) * HEAD_DIM].astype(BF16)
        pa.append(_dot(xh, wa_ref[h]))
        px.append(_dot(xh, wx_ref[h]))
    r = _sigmoid(jnp.concatenate(pa, axis=1) + ba)
    ig = _sigmoid(jnp.concatenate(px, axis=1) + bx)
    sp = _softplus(-lam)
    log_a = (-LRU_C * r) * sp
    a = jnp.exp(log_a)
    mult = jnp.sqrt(jnp.tanh(-log_a) * (1.0 + a * a))
    return r, ig, a, mult, sp


def _conv(ext, w_ref, b):
    y = b + _shift_down(ext, 3) * w_ref[0:1, :]
    y = y + _shift_down(ext, 2) * w_ref[1:2, :]
    y = y + _shift_down(ext, 1) * w_ref[2:3, :]
    y = y + ext * w_ref[3:4, :]
    return y[CONV_HIST:, :]


def _pool_diff(ext, pos):
    out = []
    for g, k in enumerate(POOL_WINDOWS):
        col = ext[:, g * POOL_GROUP_DIM:(g + 1) * POOL_GROUP_DIM]
        s = col
        for step in range(g + 1):
            s = s + _shift_down(s, 2 ** step)
        count = jnp.minimum(pos + 1, k).astype(F32)
        out.append(s[POOL_HIST:, :] / count - col[POOL_HIST:, :])
    return out


def _pool_mix(diff, pw_ref):
    return jnp.concatenate([_dot(diff[g].astype(BF16), pw_ref[g]) for g in range(len(POOL_WINDOWS))], axis=1)


def _branch_specs(tb, row_map, fixed):
    fixed3 = lambda i: (0, 0, 0)
    return [pl.BlockSpec((CONV_WIDTH, D_MODEL), fixed), pl.BlockSpec((1, D_MODEL), fixed),
            pl.BlockSpec((LRU_HEADS, HEAD_DIM, HEAD_DIM), fixed3), pl.BlockSpec((1, D_MODEL), fixed),
            pl.BlockSpec((LRU_HEADS, HEAD_DIM, HEAD_DIM), fixed3), pl.BlockSpec((1, D_MODEL), fixed),
            pl.BlockSpec((1, D_MODEL), fixed),
            pl.BlockSpec((len(POOL_WINDOWS), POOL_GROUP_DIM, POOL_GROUP_DIM), fixed3),
            pl.BlockSpec((1, POOL_WIDTH), fixed)]


def _branches_fwd(z, weights, seq, tb, shards):
    t = z.shape[0]
    nb = t // tb
    nbe = seq // tb
    groups = tb // F32_SUBLANES
    n = len(shards)

    def body(xa_ref, ga_ref, xb_ref, gb_ref, cw_ref, cb_ref, wa_ref, ba_ref, wx_ref, bx_ref, lam_ref,
             pw_ref, ps_ref, *refs):
        g_ins = refs[:n]
        ya_ref, yb_ref, hl_ref = refs[n:n + 3]
        g_outs = refs[n + 3:2 * n + 3]
        xa_ext, xb_ext, carry, a_s, u_s, send_sems, recv_sems, local_sems = refs[2 * n + 3:]
        blk = pl.program_id(0) % nbe
        start_gather, finish_gather = _gather_steps(shards, g_ins, g_outs, send_sems, recv_sems, local_sems)
        pl.when(pl.program_id(0) == 0)(start_gather)

        @pl.when(blk == 0)
        def _():
            xa_ext[0:CONV_HIST, :] = jnp.zeros((CONV_HIST, D_MODEL), F32)
            xb_ext[0:POOL_HIST, :] = jnp.zeros((POOL_HIST, POOL_WIDTH), F32)
            carry[...] = jnp.zeros_like(carry)

        xa_ext[CONV_HIST:, :] = xa_ref[...]
        xb_ext[POOL_HIST:, :] = xb_ref[...]
        ea = xa_ext[...]
        eb = xb_ext[...]
        xa_ext[0:CONV_HIST, :] = ea[tb:, :]
        xb_ext[0:POOL_HIST, :] = eb[tb:, :]

        xc = _conv(ea, cw_ref, cb_ref[...])
        _, ig, a, mult, _ = _lru_gates(xc, wa_ref, ba_ref[...], wx_ref, bx_ref[...], lam_ref[...])
        u = mult * (ig * xc)
        row8 = lax.broadcasted_iota(jnp.int32, (tb, D_MODEL), 0) % F32_SUBLANES
        for s in (1, 2, 4):
            m = row8 >= s
            u = jnp.where(m, a * _shift_down(u, s) + u, u)
            a = jnp.where(m, a * _shift_down(a, s), a)
        a_s[...] = a
        u_s[...] = u

        def step(g, cr):
            sl = pl.ds(pl.multiple_of(g * F32_SUBLANES, F32_SUBLANES), F32_SUBLANES)
            hb = a_s[sl, :] * cr + u_s[sl, :]
            hl_ref[sl, :] = hb
            return jnp.broadcast_to(hb[F32_SUBLANES - 1:F32_SUBLANES, :], (F32_SUBLANES, D_MODEL))

        carry[...] = lax.fori_loop(0, groups, step, carry[...], unroll=4)
        ga = ga_ref[...]
        ya_ref[...] = (hl_ref[...] * (ga * _sigmoid(ga))).astype(BF16)

        pos = blk * tb + lax.broadcasted_iota(jnp.int32, (tb, POOL_GROUP_DIM), 0)
        ypre = _pool_mix(_pool_diff(eb, pos), pw_ref)
        gb = gb_ref[...]
        yb_ref[...] = ((ypre * ps_ref[...]) * (gb * _sigmoid(gb))).astype(BF16)
        pl.when(pl.program_id(0) == nb - 1)(finish_gather)

    row = lambda i: (i, 0)
    fixed = lambda i: (0, 0)
    any_spec = pl.BlockSpec(memory_space=pl.ANY)
    in_specs = [pl.BlockSpec((tb, D_MODEL), lambda i: (i, 0)), pl.BlockSpec((tb, D_MODEL), lambda i: (i, 1)),
                pl.BlockSpec((tb, POOL_WIDTH), lambda i: (i, 4)), pl.BlockSpec((tb, POOL_WIDTH), lambda i: (i, 5)),
                ] + _branch_specs(tb, row, fixed) + [any_spec] * n
    g_shape, g_sems = _gather_shapes(shards)
    return pl.pallas_call(
        body, name="branches_fwd",
        out_shape=tuple([jax.ShapeDtypeStruct((t, D_MODEL), BF16), jax.ShapeDtypeStruct((t, POOL_WIDTH), BF16),
                         jax.ShapeDtypeStruct((t, D_MODEL), F32)] + g_shape),
        grid=(nb,), in_specs=in_specs,
        out_specs=tuple([pl.BlockSpec((tb, D_MODEL), row), pl.BlockSpec((tb, POOL_WIDTH), row),
                         pl.BlockSpec((tb, D_MODEL), row)] + [any_spec] * n),
        scratch_shapes=[pltpu.VMEM((tb + CONV_HIST, D_MODEL), F32), pltpu.VMEM((tb + POOL_HIST, POOL_WIDTH), F32),
                        pltpu.VMEM((F32_SUBLANES, D_MODEL), F32),
                        pltpu.VMEM((tb, D_MODEL), F32), pltpu.VMEM((tb, D_MODEL), F32)] + g_sems,
        compiler_params=pltpu.CompilerParams(dimension_semantics=("arbitrary",),
                                             vmem_limit_bytes=VMEM_LIMIT_BYTES),
    )(z, z, z, z, *weights, *[sh[0] for sh in shards])


def _branches_bwd(z, hl, dya, dyb, dzm, weights, seq, tb):
    t = z.shape[0]
    nb = t // tb
    nbe = seq // tb
    groups = tb // F32_SUBLANES
    n_pool = len(POOL_WINDOWS)

    def body(xa_ref, xap_ref, ga_ref, xb_ref, xbp_ref, gb_ref, hl_ref, hlp_ref, dya_ref, dyb_ref, dzm_ref,
             cw_ref, cb_ref, wa_ref, ba_ref, wx_ref, bx_ref, lam_ref, pw_ref, ps_ref,
             dz_ref, dcw_ref, dcb_ref, dwa_ref, dba_ref, dwx_ref, dbx_ref, dlam_ref, dpw_ref, dps_ref,
             xa_ext, xb_ext, hl_ext, a_ext, dxc_ext, dwin_ext, g_carry, b_s, d_s, g_s):
        i = pl.program_id(0)
        blk = (nb - 1 - i) % nbe

        @pl.when(i == 0)
        def _():
            for ref in (dcw_ref, dcb_ref, dwa_ref, dba_ref, dwx_ref, dbx_ref, dlam_ref, dpw_ref, dps_ref):
                ref[...] = jnp.zeros_like(ref)

        @pl.when(blk == nbe - 1)
        def _():
            a_ext[tb:, :] = jnp.zeros((F32_SUBLANES, D_MODEL), F32)
            dxc_ext[tb:, :] = jnp.zeros((CONV_HIST, D_MODEL), F32)
            dwin_ext[tb:, :] = jnp.zeros((POOL_HIST, POOL_WIDTH), F32)
            g_carry[...] = jnp.zeros_like(g_carry)

        live = (blk > 0).astype(F32)
        xa_ext[0:CONV_HIST, :] = xap_ref[...] * live
        xa_ext[CONV_HIST:, :] = xa_ref[...]
        xb_ext[0:POOL_HIST, :] = xbp_ref[...] * live
        xb_ext[POOL_HIST:, :] = xb_ref[...]
        hl_ext[0:F32_SUBLANES, :] = hlp_ref[...] * live
        hl_ext[F32_SUBLANES:, :] = hl_ref[...]
        ea = xa_ext[...]
        eb = xb_ext[...]

        xc = _conv(ea, cw_ref, cb_ref[...])
        lam = lam_ref[...]
        r, ig, a, mult, sp = _lru_gates(xc, wa_ref, ba_ref[...], wx_ref, bx_ref[...], lam)
        hl = hl_ref[...]
        ga = ga_ref[...]
        sga = _sigmoid(ga)
        dya = dya_ref[...]
        dhl = dya * (ga * sga)
        dz_ref[:, D_MODEL:2 * D_MODEL] = (dya * hl * (sga * (1.0 + ga * (1.0 - sga)))).astype(BF16)

        a_ext[0:tb, :] = a
        b = _shift_up(a_ext[...], 1)[0:tb, :]
        a_ext[tb:, :] = jnp.broadcast_to(a[0:1, :], (F32_SUBLANES, D_MODEL))
        d = dhl
        row8 = lax.broadcasted_iota(jnp.int32, (tb, D_MODEL), 0) % F32_SUBLANES
        for s in (1, 2, 4):
            m = row8 < F32_SUBLANES - s
            d = jnp.where(m, d + b * _shift_up(d, s), d)
            b = jnp.where(m, b * _shift_up(b, s), b)
        b_s[...] = b
        d_s[...] = d

        def step(k, cr):
            sl = pl.ds(pl.multiple_of((groups - 1 - k) * F32_SUBLANES, F32_SUBLANES), F32_SUBLANES)
            gb_ = d_s[sl, :] + b_s[sl, :] * cr
            g_s[sl, :] = gb_
            return jnp.broadcast_to(gb_[0:1, :], (F32_SUBLANES, D_MODEL))

        g_carry[...] = lax.fori_loop(0, groups, step, g_carry[...], unroll=4)
        gsc = g_s[...]
        da = gsc * _shift_down(hl_ext[...], 1)[F32_SUBLANES:, :]
        dmult = gsc * (ig * xc)
        dig = gsc * (mult * xc)
        dxc = gsc * (mult * ig)
        dlog_a = da * a - (a * a) * dmult / mult
        dr = dlog_a * (-LRU_C * sp)
        dlam_ref[...] += jnp.sum(dlog_a * (-LRU_C * r), axis=0, keepdims=True)
        dpa = dr * (r * (1.0 - r))
        dpx = dig * (ig * (1.0 - ig))
        dba_ref[...] += jnp.sum(dpa, axis=0, keepdims=True)
        dbx_ref[...] += jnp.sum(dpx, axis=0, keepdims=True)
        back = []
        for h in range(LRU_HEADS):
            cols = slice(h * HEAD_DIM, (h + 1) * HEAD_DIM)
            xh = xc[:, cols].astype(BF16)
            dpa_h = dpa[:, cols].astype(BF16)
            dpx_h = dpx[:, cols].astype(BF16)
            dwa_ref[h] += _dot_tn(xh, dpa_h)
            dwx_ref[h] += _dot_tn(xh, dpx_h)
            back.append(_dot_nt(dpa_h, wa_ref[h]) + _dot_nt(dpx_h, wx_ref[h]))
        dxc = dxc + jnp.concatenate(back, axis=1)
        dcb_ref[...] += jnp.sum(dxc, axis=0, keepdims=True)
        for k in range(CONV_WIDTH):
            tap = _shift_down(ea, CONV_WIDTH - 1 - k)[CONV_HIST:, :] if k < CONV_WIDTH - 1 else ea[CONV_HIST:, :]
            dcw_ref[k:k + 1, :] += jnp.sum(dxc * tap, axis=0, keepdims=True)
        dxc_ext[0:tb, :] = dxc
        ed = dxc_ext[...]
        dxa = ed * cw_ref[3:4, :]
        dxa = dxa + _shift_up(ed, 1) * cw_ref[2:3, :]
        dxa = dxa + _shift_up(ed, 2) * cw_ref[1:2, :]
        dxa = dxa + _shift_up(ed, 3) * cw_ref[0:1, :]
        dz_ref[:, 0:D_MODEL] = dxa[0:tb, :].astype(BF16)
        dxc_ext[tb:, :] = dxc[0:CONV_HIST, :]

        pos = blk * tb + lax.broadcasted_iota(jnp.int32, (tb, POOL_GROUP_DIM), 0)
        diff = _pool_diff(eb, pos)
        ypre = _pool_mix(diff, pw_ref)
        ps = ps_ref[...]
        gb = gb_ref[...]
        sgb = _sigmoid(gb)
        dyb = dyb_ref[...]
        dyp = dyb * (gb * sgb)
        dz_ref[:, 2 * D_MODEL + POOL_WIDTH:3 * D_MODEL] = (
            dyb * (ypre * ps) * (sgb * (1.0 + gb * (1.0 - sgb)))).astype(BF16)
        dps_ref[...] += jnp.sum(dyp * ypre, axis=0, keepdims=True)
        dypre = dyp * ps
        for g, k in enumerate(POOL_WINDOWS):
            cols = slice(g * POOL_GROUP_DIM, (g + 1) * POOL_GROUP_DIM)
            dyg = dypre[:, cols].astype(BF16)
            dpw_ref[g] += _dot_tn(diff[g].astype(BF16), dyg)
            ddiff = _dot_nt(dyg, pw_ref[g])
            count = jnp.minimum(pos + 1, k).astype(F32)
            dwin = ddiff / count
            dwin_ext[0:tb, cols] = dwin
            s = dwin_ext[:, cols]
            for step_ in range(g + 1):
                s = s + _shift_up(s, 2 ** step_)
            dz_ref[:, 2 * D_MODEL + g * POOL_GROUP_DIM:2 * D_MODEL + (g + 1) * POOL_GROUP_DIM] = (
                s[0:tb, :] - ddiff).astype(BF16)
            dwin_ext[tb:, cols] = dwin[0:POOL_HIST, :]

        dz_ref[:, 3 * D_MODEL:] = dzm_ref[...]

        @pl.when(i == nb - 1)
        def _():
            dlam_ref[...] = dlam_ref[...] * (-_sigmoid(-lam))

    rev = lambda i: (nb - 1 - i, 0)
    fixed = lambda i: (0, 0)
    fixed3 = lambda i: (0, 0, 0)

    def prev(rows, col):
        per = tb // rows
        return lambda i: (jnp.maximum((nb - 1 - i) * per - 1, 0), col)

    in_specs = [pl.BlockSpec((tb, D_MODEL), lambda i: (nb - 1 - i, 0)),
                pl.BlockSpec((CONV_HIST, D_MODEL), prev(CONV_HIST, 0)),
                pl.BlockSpec((tb, D_MODEL), lambda i: (nb - 1 - i, 1)),
                pl.BlockSpec((tb, POOL_WIDTH), lambda i: (nb - 1 - i, 4)),
                pl.BlockSpec((POOL_HIST, POOL_WIDTH), prev(POOL_HIST, 4)),
                pl.BlockSpec((tb, POOL_WIDTH), lambda i: (nb - 1 - i, 5)),
                pl.BlockSpec((tb, D_MODEL), rev),
                pl.BlockSpec((F32_SUBLANES, D_MODEL), prev(F32_SUBLANES, 0)),
                pl.BlockSpec((tb, D_MODEL), rev), pl.BlockSpec((tb, POOL_WIDTH), rev),
                pl.BlockSpec((tb, 2 * D_MODEL), rev)] + _branch_specs(tb, rev, fixed)
    out_shape = (jax.ShapeDtypeStruct((t, IN_COLS), BF16),
                 jax.ShapeDtypeStruct((CONV_WIDTH, D_MODEL), F32), jax.ShapeDtypeStruct((1, D_MODEL), F32),
                 jax.ShapeDtypeStruct((LRU_HEADS, HEAD_DIM, HEAD_DIM), F32), jax.ShapeDtypeStruct((1, D_MODEL), F32),
                 jax.ShapeDtypeStruct((LRU_HEADS, HEAD_DIM, HEAD_DIM), F32), jax.ShapeDtypeStruct((1, D_MODEL), F32),
                 jax.ShapeDtypeStruct((1, D_MODEL), F32),
                 jax.ShapeDtypeStruct((n_pool, POOL_GROUP_DIM, POOL_GROUP_DIM), F32),
                 jax.ShapeDtypeStruct((1, POOL_WIDTH), F32))
    out_specs = (pl.BlockSpec((tb, IN_COLS), rev),
                 pl.BlockSpec((CONV_WIDTH, D_MODEL), fixed), pl.BlockSpec((1, D_MODEL), fixed),
                 pl.BlockSpec((LRU_HEADS, HEAD_DIM, HEAD_DIM), fixed3), pl.BlockSpec((1, D_MODEL), fixed),
                 pl.BlockSpec((LRU_HEADS, HEAD_DIM, HEAD_DIM), fixed3), pl.BlockSpec((1, D_MODEL), fixed),
                 pl.BlockSpec((1, D_MODEL), fixed),
                 pl.BlockSpec((n_pool, POOL_GROUP_DIM, POOL_GROUP_DIM), fixed3),
                 pl.BlockSpec((1, POOL_WIDTH), fixed))
    scratch = [pltpu.VMEM((tb + CONV_HIST, D_MODEL), F32), pltpu.VMEM((tb + POOL_HIST, POOL_WIDTH), F32),
               pltpu.VMEM((tb + F32_SUBLANES, D_MODEL), F32), pltpu.VMEM((tb + F32_SUBLANES, D_MODEL), F32),
               pltpu.VMEM((tb + CONV_HIST, D_MODEL), F32), pltpu.VMEM((tb + POOL_HIST, POOL_WIDTH), F32),
               pltpu.VMEM((F32_SUBLANES, D_MODEL), F32),
               pltpu.VMEM((tb, D_MODEL), F32), pltpu.VMEM((tb, D_MODEL), F32), pltpu.VMEM((tb, D_MODEL), F32)]
    return pl.pallas_call(
        body, name="branches_bwd", out_shape=out_shape, grid=(nb,), in_specs=in_specs, out_specs=out_specs,
        scratch_shapes=scratch,
        compiler_params=pltpu.CompilerParams(dimension_semantics=("arbitrary",),
                                             vmem_limit_bytes=VMEM_LIMIT_BYTES),
    )(z, z, z, z, z, z, hl, hl, dya, dyb, dzm, *weights)


def _merge_head(x2d, ya, yb, z, p2d, tgt, w_pl, w_pp, w_out, w_pg, w_pe, g2, gf, tb):
    t = x2d.shape[0]
    p_dim = p2d.shape[1]

    def body(x_ref, ya_ref, yb_ref, ma_ref, mb_ref, p_ref, t_ref, wpl_ref, wpp_ref, wout_ref, wpg_ref, wpe_ref,
             g2_ref, gf_ref,
             loss_ref, dg2_ref, dgf_ref, dxr_ref, dya_ref, dyb_ref, dzm_ref,
             mg_ref, do_ref, hn_ref, dgp_ref, dpe_ref, da_ref, dbm_ref, pbf_ref):
        @pl.when(pl.program_id(0) == 0)
        def _():
            loss_ref[...] = jnp.zeros_like(loss_ref)
            dg2_ref[...] = jnp.zeros_like(dg2_ref)
            dgf_ref[...] = jnp.zeros_like(dgf_ref)

        a_ = _dot(ya_ref[...], wpl_ref[...])
        bm = _dot(yb_ref[...], wpp_ref[...])
        sa = _sigmoid(ma_ref[...])
        sb = _sigmoid(mb_ref[...])
        mg = (sa * a_ + sb * bm).astype(BF16)
        mg_ref[...] = mg
        x1 = x_ref[...] + _dot(mg, wout_ref[...])
        xn2, r2 = _rms(x1)
        g2 = g2_ref[...]
        hn = (xn2 * g2).astype(BF16)
        hn_ref[...] = hn
        gate = _sigmoid(_dot(hn, wpg_ref[...]))
        pbf = p_ref[...].astype(BF16)
        pbf_ref[...] = pbf
        pe = _dot(pbf, wpe_ref[...])
        x2 = x1 + gate * pe
        xn3, r3 = _rms(x2)
        gf = gf_ref[...]
        err = xn3 * gf - t_ref[...]
        loss_ref[...] += 0.5 * jnp.sum(jnp.mean(err * err, axis=-1))

        dy = err * (1.0 / D_MODEL)
        dgf_ref[...] += jnp.sum(dy * xn3, axis=0, keepdims=True)
        dx2 = _rms_bwd(dy * gf, xn3, r3)
        dpe_ref[...] = (dx2 * gate).astype(BF16)
        dgp = ((dx2 * pe) * (gate * (1.0 - gate))).astype(BF16)
        dgp_ref[...] = dgp
        dhn = _dot_nt(dgp, wpg_ref[...])
        dg2_ref[...] += jnp.sum(dhn * xn2, axis=0, keepdims=True)
        dx1 = dx2 + _rms_bwd(dhn * g2, xn2, r2)
        dxr_ref[...] = dx1
        do = dx1.astype(BF16)
        do_ref[...] = do
        dmg = _dot_nt(do, wout_ref[...])
        da = (dmg * sa).astype(BF16)
        dbm = (dmg * sb).astype(BF16)
        da_ref[...] = da
        dbm_ref[...] = dbm
        dzm_ref[:, 0:D_MODEL] = (dmg * a_ * (sa * (1.0 - sa))).astype(BF16)
        dzm_ref[:, D_MODEL:] = (dmg * bm * (sb * (1.0 - sb))).astype(BF16)
        dya_ref[...] = _dot_nt(da, wpl_ref[...])
        dyb_ref[...] = _dot_nt(dbm, wpp_ref[...])

    row = lambda i: (i, 0)
    fixed = lambda i: (0, 0)

    def resident(shape):
        return pl.BlockSpec(shape, fixed, pipeline_mode=pl.Buffered(1))

    tok = lambda width: pl.BlockSpec((tb, width), row)
    in_specs = [tok(D_MODEL), tok(D_MODEL), tok(POOL_WIDTH),
                pl.BlockSpec((tb, D_MODEL), lambda i: (i, 3)), pl.BlockSpec((tb, D_MODEL), lambda i: (i, 4)),
                tok(p_dim), tok(D_MODEL),
                resident((D_MODEL, D_MODEL)), resident((POOL_WIDTH, D_MODEL)), resident((D_MODEL, D_MODEL)),
                resident((D_MODEL, D_MODEL)), resident((p_dim, D_MODEL)),
                pl.BlockSpec((1, D_MODEL), fixed), pl.BlockSpec((1, D_MODEL), fixed)]
    bf = lambda width: jax.ShapeDtypeStruct((t, width), BF16)
    f32 = lambda width: jax.ShapeDtypeStruct((t, width), F32)
    out_shape = (jax.ShapeDtypeStruct((F32_SUBLANES, 128), F32), jax.ShapeDtypeStruct((1, D_MODEL), F32),
                 jax.ShapeDtypeStruct((1, D_MODEL), F32),
                 f32(D_MODEL), f32(D_MODEL), f32(POOL_WIDTH), bf(2 * D_MODEL),
                 bf(D_MODEL), bf(D_MODEL), bf(D_MODEL), bf(D_MODEL), bf(D_MODEL), bf(D_MODEL), bf(D_MODEL), bf(p_dim))
    out_specs = (pl.BlockSpec((F32_SUBLANES, 128), fixed), pl.BlockSpec((1, D_MODEL), fixed),
                 pl.BlockSpec((1, D_MODEL), fixed),
                 tok(D_MODEL), tok(D_MODEL), tok(POOL_WIDTH), tok(2 * D_MODEL),
                 tok(D_MODEL), tok(D_MODEL), tok(D_MODEL), tok(D_MODEL), tok(D_MODEL), tok(D_MODEL), tok(D_MODEL),
                 tok(p_dim))
    return pl.pallas_call(
        body, name="merge_head", out_shape=out_shape, grid=(t // tb,), in_specs=in_specs, out_specs=out_specs,
        compiler_params=pltpu.CompilerParams(dimension_semantics=("arbitrary",),
                                             vmem_limit_bytes=VMEM_LIMIT_BYTES),
    )(x2d, ya, yb, z, z, p2d, tgt, w_pl, w_pp, w_out, w_pg, w_pe, g2, gf)


def _pad_rows(a, rows):
    return jnp.pad(a, ((0, rows - a.shape[0]), (0, D_MODEL - a.shape[1])))


def _pack_bag(parts, tail=None):
    rows = [_pad_rows(a.reshape(-1, a.shape[-1]) if a.shape[-1] != HEAD_DIM else a.reshape(-1, D_MODEL), n)
            for a, n in zip(parts, BAG_PART_ROWS)]
    spare = BAG_ROWS - sum(BAG_PART_ROWS)
    if tail is not None:
        rows.append(_pad_rows(tail, F32_SUBLANES))
        spare -= F32_SUBLANES
    rows.append(jnp.zeros((spare, D_MODEL), F32))
    return jnp.concatenate(rows, axis=0)


def _unpack_bag(bag, shapes):
    out, at = [], 0
    for shape, n in zip(shapes, BAG_PART_ROWS):
        size = 1
        for s in shape:
            size *= s
        if size % D_MODEL == 0:
            piece = bag[at:at + size // D_MODEL, :]
        else:
            piece = bag[at:at + 1, :size]
        out.append(piece.reshape(shape))
        at += n
    return out


def kernel(x, p, norm_g, w_in, conv_w, conv_b, lru_w_a, lru_b_a, lru_w_x, lru_b_x, lru_lambda, pool_w, pool_scale, w_proj_lru, w_proj_pool, w_out, ple_norm_g, w_ple_gate, w_ple_proj, final_g, loss_target, m_norm_g, m_w_in, m_conv_w, m_conv_b, m_lru_w_a, m_lru_b_a, m_lru_w_x, m_lru_b_x, m_lru_lambda, m_pool_w, m_pool_scale, m_w_proj_lru, m_w_proj_pool, m_w_out, m_ple_norm_g, m_w_ple_gate, m_w_ple_proj, m_final_g, v_norm_g, v_w_in, v_conv_w, v_conv_b, v_lru_w_a, v_lru_b_a, v_lru_w_x, v_lru_b_x, v_lru_lambda, v_pool_w, v_pool_scale, v_w_proj_lru, v_w_proj_pool, v_w_out, v_ple_norm_g, v_w_ple_gate, v_w_ple_proj, v_final_g):
    bsz, seq, _ = x.shape
    t = bsz * seq
    tb_mm = min(512, seq)
    tb_seq = min(256, seq // 2) if seq >= 512 else seq
    x2d = x.reshape(t, D_MODEL)
    p2d = p.reshape(t, p.shape[-1])
    tgt = loss_target.reshape(t, D_MODEL)
    chip = 2 * lax.axis_index("x") + lax.axis_index("y")

    rest = [(w_proj_lru[0], 0), (w_proj_pool[0], 1), (w_out[0], 0), (w_ple_gate[0], 0), (w_ple_proj[0], 1)]
    z, w_in_f, conv_w_f = _in_proj_gather(x2d, norm_g, w_in[0].astype(BF16), [(conv_w[0], 1, False)], tb_mm)

    wa_bf = lru_w_a[0].astype(BF16)
    wx_bf = lru_w_x[0].astype(BF16)
    pw_bf = pool_w[0].astype(BF16)
    branch_w = (conv_w_f, conv_b, wa_bf, lru_b_a.reshape(1, D_MODEL), wx_bf, lru_b_x.reshape(1, D_MODEL),
                lru_lambda, pw_bf, pool_scale)

    ya, yb, hl, w_pl_f, w_pp_f, w_out_f, w_pg_f, w_pe_f = _branches_fwd(
        z, branch_w, seq, tb_seq, [(w.astype(BF16), axis, True) for w, axis in rest])
    (loss_acc, d_g2, d_gf, dx_res, dya, dyb, dzm, mg_bf, do_bf, hn_bf, dgp_bf, dpe_bf, da_bf, dbm_bf, p_bf) = _merge_head(
        x2d, ya, yb, z, p2d, tgt, w_pl_f, w_pp_f, w_out_f, w_pg_f, w_pe_f, ple_norm_g, final_g.reshape(1, D_MODEL),
        tb_seq)
    (dz, d_cw, d_cb, d_wa, d_ba, d_wx, d_bx, d_lam, d_pw, d_ps) = _branches_bwd(
        z, hl, dya, dyb, dzm, branch_w, seq, tb_seq)
    dx, h_bf, d_g1 = _in_proj_bwd(dz, w_in_f, x2d, dx_res, norm_g, tb_mm)

    g_in = _weight_grad(h_bf, dz, N_CHIPS, tb_mm, "dw_in").reshape(8, D_MODEL // 2, IN_COLS // N_CHIPS)
    g_pl = _weight_grad(ya, da_bf, 1, tb_mm, "dw_proj_lru").reshape(8, D_MODEL // 8, D_MODEL)
    g_pp = _weight_grad(yb, dbm_bf, N_CHIPS, tb_mm, "dw_proj_pool").reshape(8, POOL_WIDTH // 2, D_MODEL // N_CHIPS)
    g_out = _weight_grad(mg_bf, do_bf, 1, tb_mm, "dw_out").reshape(8, D_MODEL // 8, D_MODEL)
    g_pg = _weight_grad(hn_bf, dgp_bf, 1, tb_mm, "dw_ple_gate").reshape(8, D_MODEL // 8, D_MODEL)
    p_dim = p2d.shape[1]
    g_pe = _weight_grad(p_bf, dpe_bf, N_CHIPS, tb_mm, "dw_ple_proj").reshape(8, p_dim // 2, D_MODEL // N_CHIPS)

    (r_in,) = _reduce_scatter([g_in], "rs_w_in", BF16)
    r_pl, r_pp, r_out, r_pg, r_pe = _reduce_scatter([g_pl, g_pp, g_out, g_pg, g_pe], "rs_proj", BF16)
    small_shapes = [(1, D_MODEL), (1, CONV_WIDTH, D_MODEL), (1, D_MODEL), lru_w_a.shape, lru_b_a.shape, lru_w_x.shape,
                    lru_b_x.shape, (1, D_MODEL), pool_w.shape, pool_scale.shape, (1, D_MODEL), final_g.shape]
    bag = _pack_bag([d_g1, d_cw, d_cb, d_wa, d_ba.reshape(1, D_MODEL), d_wx, d_bx.reshape(1, D_MODEL), d_lam, d_pw,
                     d_ps, d_g2, d_gf], tail=loss_acc)
    (bag_mine,) = _reduce_scatter([bag.reshape(8, BAG_ROWS // 8, D_MODEL)], "rs_small")
    (bag_sum,) = _gather_shards([(bag_mine.reshape(BAG_ROWS // N_CHIPS, D_MODEL), 0, True)], "gather_small")
    (g_g1, g_cw_full, g_cb, g_wa, g_ba, g_wx, g_bx, g_lam, g_pw, g_ps, g_g2, g_gf) = _unpack_bag(bag_sum, small_shapes)
    cw_cols = D_MODEL // N_CHIPS
    g_cw = lax.dynamic_slice_in_dim(g_cw_full, chip * cw_cols, cw_cols, axis=2)

    def big_update(w, g2d, m, v, rows, name):
        d, nm, nv = _adamw(w[0], g2d, m[0], v[0], rows, name)
        return g2d[None], d[None], nm[None], nv[None]

    u_in = big_update(w_in, r_in.reshape(D_MODEL, IN_COLS // N_CHIPS), m_w_in, v_w_in, 256, "adamw_w_in")
    u_pl = big_update(w_proj_lru, r_pl.reshape(D_MODEL // N_CHIPS, D_MODEL), m_w_proj_lru, v_w_proj_lru, 256, "adamw_w_proj_lru")
    u_pp = big_update(w_proj_pool, r_pp.reshape(POOL_WIDTH, D_MODEL // N_CHIPS), m_w_proj_pool, v_w_proj_pool, 512, "adamw_w_proj_pool")
    u_out = big_update(w_out, r_out.reshape(D_MODEL // N_CHIPS, D_MODEL), m_w_out, v_w_out, 256, "adamw_w_out")
    u_pg = big_update(w_ple_gate, r_pg.reshape(D_MODEL // N_CHIPS, D_MODEL), m_w_ple_gate, v_w_ple_gate, 256, "adamw_w_ple_gate")
    u_pe = big_update(w_ple_proj, r_pe.reshape(p_dim, D_MODEL // N_CHIPS), m_w_ple_proj, v_w_ple_proj, 256, "adamw_w_ple_proj")
    u_cw = big_update(conv_w, g_cw[0], m_conv_w, v_conv_w, CONV_WIDTH, "adamw_conv_w")

    small_w = [norm_g, None, conv_b, lru_w_a, lru_b_a, lru_w_x, lru_b_x, lru_lambda, pool_w, pool_scale, ple_norm_g, final_g]
    small_m = [m_norm_g, None, m_conv_b, m_lru_w_a, m_lru_b_a, m_lru_w_x, m_lru_b_x, m_lru_lambda, m_pool_w, m_pool_scale, m_ple_norm_g, m_final_g]
    small_v = [v_norm_g, None, v_conv_b, v_lru_w_a, v_lru_b_a, v_lru_w_x, v_lru_b_x, v_lru_lambda, v_pool_w, v_pool_scale, v_ple_norm_g, v_final_g]
    fill = jnp.zeros((CONV_WIDTH, D_MODEL), F32)

    def bag_of(arrs):
        return _pack_bag([fill if a is None else (a[0] if a.ndim > 1 else a[None]) for a in arrs])

    d_bag, m_bag, v_bag = _adamw(bag_of(small_w), bag_sum, bag_of(small_m), bag_of(small_v), BAG_ROWS // 8, "adamw_small")
    d_small = _unpack_bag(d_bag, small_shapes)
    m_small = _unpack_bag(m_bag, small_shapes)
    v_small = _unpack_bag(v_bag, small_shapes)

    loss = bag_sum[sum(BAG_PART_ROWS), 0]
    grad_x = dx.reshape(bsz, seq, D_MODEL)

    def ordered(small, pick):
        s = list(small)
        return [s[0], u_in[pick], u_cw[pick], s[2], s[3], s[4], s[5], s[6], s[7], s[8], s[9],
                u_pl[pick], u_pp[pick], u_out[pick], s[10], u_pg[pick], u_pe[pick], s[11]]

    grads = ordered([g_g1, None, g_cb, g_wa, g_ba, g_wx, g_bx, g_lam, g_pw, g_ps, g_g2, g_gf], 0)
    return (loss, grad_x, *grads, *ordered(d_small, 1), *ordered(m_small, 2), *ordered(v_small, 3))
```

```python
import functools

import jax
import jax.numpy as jnp
from jax import lax
from jax.experimental import pallas as pl
from jax.experimental.pallas import tpu as pltpu

F32 = jnp.float32
BF16 = jnp.bfloat16
MESH = pl.DeviceIdType.MESH
ALL_AXES = ("x", "y", "c")

D_MODEL = 1024
LRU_HEADS = 8
HEAD_DIM = 128
CONV_WIDTH = 4
LRU_C = 8.0
POOL_WIDTH = 512
POOL_WINDOWS = (2, 4, 8, 16)
POOL_GROUP_DIM = 128
IN_COLS = 5120
N_CHIPS = 4
EPS = 1e-6

ADAM_LR = 0.001
ADAM_B1 = 0.9
ADAM_B2 = 0.999
ADAM_EPS = 1e-08
ADAM_WD = 0.01
ADAM_STEP = 10

F32_SUBLANES = 8
CONV_HIST = 8
POOL_HIST = 16
VMEM_LIMIT_BYTES = 58 * 1024 * 1024
BAG_PART_ROWS = (8, 8, 8, 128, 8, 128, 8, 8, 64, 8, 8, 8)
BAG_ROWS = 448


def _dot(a, b):
    return jnp.dot(a, b, preferred_element_type=F32)


def _dot_nt(a, b):
    return lax.dot_general(a, b, (((1,), (1,)), ((), ())), preferred_element_type=F32)


def _dot_tn(a, b):
    return lax.dot_general(a, b, (((0,), (0,)), ((), ())), preferred_element_type=F32)


def _sigmoid(v):
    return jax.nn.sigmoid(v)


def _softplus(v):
    return jnp.maximum(v, 0.0) + jnp.log1p(jnp.exp(-jnp.abs(v)))


def _place():
    return lax.axis_index("x"), lax.axis_index("y"), lax.axis_index("c")


GATHER_SEMS = 6


def _gather_shapes(shards):
    out_shape = []
    for arr, axis, _ in shards:
        r, cols = arr.shape
        out_shape.append(jax.ShapeDtypeStruct((N_CHIPS * r, cols) if axis == 0 else (r, N_CHIPS * cols), arr.dtype))
    n = len(shards)
    sems = [pltpu.SemaphoreType.DMA((n * GATHER_SEMS,)), pltpu.SemaphoreType.DMA((n * GATHER_SEMS,)),
            pltpu.SemaphoreType.DMA((n,))]
    return out_shape, sems


def _gather_steps(shards, ins, outs, send_sems, recv_sems, local_sems):
    n = len(shards)
    x, y, c = _place()
    me, sibling = (x, y, c), (x, y, 1 - c)
    chips = [(x, 1 - y), (1 - x, y), (1 - x, 1 - y)]

    def region(k, cx, cy, hc):
        (r, cols), axis = shards[k][0].shape, shards[k][1]
        j = 2 * cx + cy
        if axis == 0:
            if hc is None:
                return outs[k].at[pl.ds(j * r, r), :]
            return outs[k].at[pl.ds(j * r + hc * (r // 2), r // 2), :]
        if hc is None:
            return outs[k].at[:, pl.ds(j * cols, cols)]
        return outs[k].at[pl.ds(hc * (r // 2), r // 2), pl.ds(j * cols, cols)]

    def remote(k, sem, block, to, src=None):
        dst = region(k, *block)
        return pltpu.make_async_remote_copy(
            src_ref=dst if src is None else src, dst_ref=dst,
            send_sem=send_sems.at[k * GATHER_SEMS + sem], recv_sem=recv_sems.at[k * GATHER_SEMS + sem],
            device_id=to, device_id_type=MESH)

    def first(k, idx):
        r, split = shards[k][0].shape[0], shards[k][2]
        src = ins[k].at[pl.ds(c * (r // 2), r // 2), :] if split else ins[k]
        return remote(k, idx, (x, y, c if split else None), (*chips[idx], c), src=src)

    def relay(k):
        src_chip = (jnp.bitwise_xor(x, 1 - c), jnp.bitwise_xor(y, c))
        dst_chip = (jnp.bitwise_xor(x, c), jnp.bitwise_xor(y, 1 - c))
        return remote(k, 2, (*src_chip, c), (*dst_chip, c))

    def passed(k, idx):
        return remote(k, 3 + idx, (*chips[idx], c), sibling)

    def mine(k):
        return pltpu.make_async_copy(ins[k], region(k, x, y, None), local_sems.at[k])

    def start():
        for k in range(n):
            mine(k).start()
            for idx in range(2 if shards[k][2] else 3):
                first(k, idx).start()

    def relay_on():
        for k in range(n):
            split = shards[k][2]
            for idx in range(2):
                remote(k, idx, (*chips[idx], c if split else None), me).wait_recv()
            if split:
                relay(k).start()
                passed(k, 0).start()
                passed(k, 1).start()

    def finish():
        for k in range(n):
            split = shards[k][2]
            remote(k, 2, (*chips[2], c if split else None), me).wait_recv()
            if split:
                passed(k, 2).start()
        for k in range(n):
            if shards[k][2]:
                for idx in range(3):
                    remote(k, 3 + idx, (*chips[idx], 1 - c), me).wait_recv()
        for k in range(n):
            if shards[k][2]:
                for cp in (first(k, 0), first(k, 1), relay(k), passed(k, 0), passed(k, 1), passed(k, 2)):
                    cp.wait_send()
            else:
                for idx in range(3):
                    first(k, idx).wait_send()
            mine(k).wait()

    return start, relay_on, finish


def _gather_shards(shards, name):
    n = len(shards)

    def body(*refs):
        for step in _gather_steps(shards, refs[:n], refs[n:2 * n], *refs[2 * n:]):
            step()

    out_shape, sems = _gather_shapes(shards)
    any_spec = pl.BlockSpec(memory_space=pl.ANY)
    return pl.pallas_call(
        body, name=name, out_shape=tuple(out_shape),
        in_specs=[any_spec] * n, out_specs=tuple([any_spec] * n), scratch_shapes=sems,
    )(*[s[0] for s in shards])


RS_ADD_ROWS = (64, 56, 32, 16, 8)


RS_SEMS = 8
RS_LOCAL_SEMS = 5


def _rs_piece_shape(part):
    arr, cols = part
    return (arr.shape[0] // 2, arr.shape[1] // N_CHIPS) if cols else tuple(arr.shape[1:])


def _rs_shapes(parts, wire):
    n = len(parts)
    shapes = [_rs_piece_shape(p) for p in parts]
    out_shape = [jax.ShapeDtypeStruct((2,) + s, F32) for s in shapes]
    scratch = []
    for lead, dtype in ((N_CHIPS, F32), (N_CHIPS, F32), (N_CHIPS, wire), (None, F32), (N_CHIPS, wire)):
        for s in shapes:
            scratch.append(pltpu.VMEM(s if lead is None else (lead,) + s, dtype))
    scratch += [pltpu.SemaphoreType.DMA((n * RS_SEMS,)), pltpu.SemaphoreType.DMA((n * RS_SEMS,)),
                pltpu.SemaphoreType.DMA((n * RS_LOCAL_SEMS,))]
    return out_shape, scratch


def _rs_steps(parts, wire, ins, outs, scratch):
    n = len(parts)
    own, sib, got, fin, snd = (scratch[k * n:(k + 1) * n] for k in range(5))
    send_sems, recv_sems, local_sems = scratch[5 * n:]
    shapes = [_rs_piece_shape(p) for p in parts]
    x, y, c = _place()
    j_me = 2 * x + y
    me, sibling = (x, y, c), (x, y, 1 - c)
    chips = [(x, 1 - y), (1 - x, y), (1 - x, 1 - y)]

    def piece(a, jj, core):
        if parts[a][1]:
            r, cl = shapes[a]
            return ins[a].at[pl.ds(core * r, r), pl.ds(jj * cl, cl)]
        return ins[a].at[2 * jj + core]

    def remote(a, sem, src, dst, to):
        return pltpu.make_async_remote_copy(
            src_ref=src, dst_ref=dst, send_sem=send_sems.at[a * RS_SEMS + sem],
            recv_sem=recv_sems.at[a * RS_SEMS + sem], device_id=to, device_id_type=MESH)

    def rows_loop(a, fn):
        r = shapes[a][0]
        step = max(s for s in RS_ADD_ROWS if r % s == 0)

        def it(i, carry):
            fn(pl.ds(pl.multiple_of(i * step, step), step))
            return carry

        lax.fori_loop(0, r // step, it, 0)

    def load(a, jj):
        return pltpu.make_async_copy(piece(a, jj, c), own[a].at[jj], local_sems.at[a * RS_LOCAL_SEMS + jj])

    def to_sibling(a, jj):
        return remote(a, jj, piece(a, jj, 1 - c), sib[a].at[jj], sibling)

    def to_owner(a, idx):
        chip = chips[idx]
        return remote(a, 4 + idx, snd[a].at[2 * chip[0] + chip[1]], got[a].at[j_me], (*chip, c))

    def store(a):
        return pltpu.make_async_copy(fin[a], outs[a].at[c], local_sems.at[a * RS_LOCAL_SEMS + 4])

    def result_to_sibling(a):
        return remote(a, 7, fin[a], outs[a].at[c], sibling)

    def exchange():
        for a in range(n):
            for jj in range(N_CHIPS):
                load(a, jj).start()
                to_sibling(a, jj).start()

    def chip_sums():
        for a in range(n):
            for jj in range(N_CHIPS):
                load(a, jj).wait()
                remote(a, jj, sib[a].at[jj], sib[a].at[jj], me).wait_recv()

                def add(sl, a=a, jj=jj):
                    q = own[a][jj, sl, :] + sib[a][jj, sl, :]
                    own[a][jj, sl, :] = q
                    snd[a][jj, sl, :] = q.astype(wire)

                rows_loop(a, add)
        for a in range(n):
            for idx in range(3):
                to_owner(a, idx).start()
        for a in range(n):
            def keep(sl, a=a):
                got[a][j_me, sl, :] = snd[a][j_me, sl, :]

            rows_loop(a, keep)

    def totals():
        for a in range(n):
            for idx, chip in enumerate(chips):
                slot = got[a].at[2 * chip[0] + chip[1]]
                remote(a, 4 + idx, slot, slot, me).wait_recv()

            def total(sl, a=a):
                mine = own[a][j_me, sl, :]
                term = [jnp.where(j_me == jj, mine, got[a][jj, sl, :].astype(F32)) for jj in range(N_CHIPS)]
                fin[a][sl, :] = ((term[0] + term[1]) + term[2]) + term[3]

            rows_loop(a, total)
            store(a).start()
            result_to_sibling(a).start()

    def finish():
        for a in range(n):
            remote(a, 7, outs[a].at[1 - c], outs[a].at[1 - c], me).wait_recv()
        for a in range(n):
            for jj in range(N_CHIPS):
                to_sibling(a, jj).wait_send()
            for idx in range(3):
                to_owner(a, idx).wait_send()
            result_to_sibling(a).wait_send()
            store(a).wait()

    return exchange, chip_sums, totals, finish


def _reduce_scatter(parts, name, wire=F32):
    n = len(parts)

    def body(*refs):
        for step in _rs_steps(parts, wire, refs[:n], refs[n:2 * n], refs[2 * n:]):
            step()

    out_shape, scratch = _rs_shapes(parts, wire)
    any_spec = pl.BlockSpec(memory_space=pl.ANY)
    return pl.pallas_call(
        body, name=name, out_shape=tuple(out_shape),
        in_specs=[any_spec] * n, out_specs=tuple([any_spec] * n), scratch_shapes=scratch,
        compiler_params=pltpu.CompilerParams(vmem_limit_bytes=VMEM_LIMIT_BYTES),
    )(*[p[0] for p in parts])


def _rms(x):
    r = lax.rsqrt(jnp.mean(x * x, axis=-1, keepdims=True) + EPS)
    return x * r, r


def _rms_bwd(dxn, xn, r):
    return r * (dxn - xn * jnp.mean(dxn * xn, axis=-1, keepdims=True))


def _in_proj_gather(x2d, norm_g, w_in_sh, shards, tb):
    t = x2d.shape[0]
    nb = t // tb
    cols = IN_COLS // N_CHIPS
    half = D_MODEL // 2
    n = len(shards)

    def body(x_ref, g_ref, win_ref, *refs):
        ins = refs[:n]
        z_ref, wfull_ref = refs[n], refs[n + 1]
        outs = refs[n + 2:2 * n + 2]
        wv, send_sems, recv_sems, local_sems, w_send, w_recv, w_local = refs[2 * n + 2:]
        s, i = pl.program_id(0), pl.program_id(1)
        x, y, c = _place()
        me, sibling = (x, y, c), (x, y, 1 - c)
        chips = [(x, 1 - y), (1 - x, y), (1 - x, 1 - y)]

        def w_half(cx, cy, hc):
            return wv.at[2 * cx + cy, pl.ds(hc * half, half), :]

        def w_remote(sem, block, to, src=None):
            dst = w_half(*block)
            return pltpu.make_async_remote_copy(
                src_ref=dst if src is None else src, dst_ref=dst, send_sem=w_send.at[sem],
                recv_sem=w_recv.at[sem], device_id=to, device_id_type=MESH)

        def w_first(idx):
            return w_remote(idx, (x, y, c), (*chips[idx], c), src=win_ref.at[pl.ds(c * half, half), :])

        def w_relay():
            src_chip = (jnp.bitwise_xor(x, 1 - c), jnp.bitwise_xor(y, c))
            dst_chip = (jnp.bitwise_xor(x, c), jnp.bitwise_xor(y, 1 - c))
            return w_remote(2, (*src_chip, c), (*dst_chip, c))

        def w_pass(idx):
            return w_remote(3 + idx, (*chips[idx], c), sibling)

        def w_store(k, cx, cy):
            jj = 2 * cx + cy
            return pltpu.make_async_copy(wv.at[jj], wfull_ref.at[:, pl.ds(jj * cols, cols)], w_local.at[k])

        start_rest, relay_rest, finish_rest = _gather_steps(shards, ins, outs, send_sems, recv_sems, local_sems)
        own = pltpu.make_async_copy(win_ref, wv.at[2 * x + y], w_local.at[4])

        @pl.when((s == 0) & (i == 0))
        def _():
            own.start()
            w_first(0).start()
            w_first(1).start()
            start_rest()
            own.wait()
            w_store(0, x, y).start()

        @pl.when((s == 1) & (i == 0))
        def _():
            w_remote(0, (*chips[0], c), me).wait_recv()
            w_remote(1, (*chips[1], c), me).wait_recv()
            w_relay().start()
            w_pass(0).start()
            w_pass(1).start()
            w_remote(3, (*chips[0], 1 - c), me).wait_recv()
            w_store(1, *chips[0]).start()

        @pl.when((s == 2) & (i == 0))
        def _():
            w_remote(4, (*chips[1], 1 - c), me).wait_recv()
            w_store(2, *chips[1]).start()

        @pl.when((s == 3) & (i == 0))
        def _():
            w_remote(2, (*chips[2], c), me).wait_recv()
            w_pass(2).start()
            w_remote(5, (*chips[2], 1 - c), me).wait_recv()
            w_store(3, *chips[2]).start()

        xn, _ = _rms(x_ref[...])
        z_ref[...] = _dot((xn * g_ref[...]).astype(BF16), wv[jnp.bitwise_xor(2 * x + y, s)])

        @pl.when((s == N_CHIPS - 1) & (i == nb - 1))
        def _():
            relay_rest()
            finish_rest()
            for cp in (w_first(0), w_first(1), w_relay(), w_pass(0), w_pass(1), w_pass(2)):
                cp.wait_send()
            w_store(0, x, y).wait()
            for idx in range(3):
                w_store(idx + 1, *chips[idx]).wait()

    rest_shape, rest_sems = _gather_shapes(shards)
    out_shape = [jax.ShapeDtypeStruct((t, IN_COLS), F32), jax.ShapeDtypeStruct((D_MODEL, IN_COLS), BF16)] + rest_shape
    any_spec = pl.BlockSpec(memory_space=pl.ANY)

    def z_map(s, i):
        return (i, jnp.bitwise_xor(2 * lax.axis_index("x") + lax.axis_index("y"), s))

    return pl.pallas_call(
        body, name="in_proj", out_shape=tuple(out_shape),
        grid=(N_CHIPS, nb),
        in_specs=[pl.BlockSpec((tb, D_MODEL), lambda s, i: (i, 0)),
                  pl.BlockSpec((1, D_MODEL), lambda s, i: (0, 0)), any_spec] + [any_spec] * n,
        out_specs=tuple([pl.BlockSpec((tb, cols), z_map), any_spec] + [any_spec] * n),
        scratch_shapes=[pltpu.VMEM((N_CHIPS, D_MODEL, cols), BF16)] + rest_sems + [
            pltpu.SemaphoreType.DMA((GATHER_SEMS,)), pltpu.SemaphoreType.DMA((GATHER_SEMS,)),
            pltpu.SemaphoreType.DMA((N_CHIPS + 1,))],
        compiler_params=pltpu.CompilerParams(dimension_semantics=("arbitrary", "arbitrary"),
                                             vmem_limit_bytes=VMEM_LIMIT_BYTES),
    )(x2d, norm_g, w_in_sh, *[sh[0] for sh in shards])


def _in_proj_bwd(dz, w_in, x2d, dx_res, norm_g, tb):
    t = x2d.shape[0]

    def body(dz_ref, w_ref, x_ref, dres_ref, g_ref, dx_ref, h_ref, dg_ref):
        @pl.when(pl.program_id(0) == 0)
        def _():
            dg_ref[...] = jnp.zeros_like(dg_ref)

        xn, r = _rms(x_ref[...])
        g = g_ref[...]
        h_ref[...] = (xn * g).astype(BF16)
        dh = _dot_nt(dz_ref[...], w_ref[...])
        dg_ref[...] += jnp.sum(dh * xn, axis=0, keepdims=True)
        dx_ref[...] = dres_ref[...] + _rms_bwd(dh * g, xn, r)

    row = lambda i: (i, 0)
    fixed = lambda i: (0, 0)
    return pl.pallas_call(
        body, name="in_proj_bwd",
        out_shape=(jax.ShapeDtypeStruct((t, D_MODEL), F32), jax.ShapeDtypeStruct((t, D_MODEL), BF16),
                   jax.ShapeDtypeStruct((1, D_MODEL), F32)),
        grid=(t // tb,),
        in_specs=[pl.BlockSpec((tb, IN_COLS), row),
                  pl.BlockSpec((D_MODEL, IN_COLS), fixed, pipeline_mode=pl.Buffered(1)),
                  pl.BlockSpec((tb, D_MODEL), row), pl.BlockSpec((tb, D_MODEL), row),
                  pl.BlockSpec((1, D_MODEL), fixed)],
        out_specs=(pl.BlockSpec((tb, D_MODEL), row), pl.BlockSpec((tb, D_MODEL), row),
                   pl.BlockSpec((1, D_MODEL), fixed)),
        compiler_params=pltpu.CompilerParams(dimension_semantics=("arbitrary",),
                                             vmem_limit_bytes=VMEM_LIMIT_BYTES),
    )(dz, w_in, x2d, dx_res, norm_g)


def _weight_grad(lhs, rhs, n_chunks, tb, name):
    t, k = lhs.shape
    nc = rhs.shape[1] // n_chunks

    def body(l_ref, r_ref, o_ref):
        @pl.when(pl.program_id(1) == 0)
        def _():
            o_ref[...] = jnp.zeros_like(o_ref)

        o_ref[...] += _dot_tn(l_ref[...], r_ref[...])

    return pl.pallas_call(
        body, name=name, out_shape=jax.ShapeDtypeStruct((n_chunks, k, nc), F32),
        grid=(n_chunks, t // tb),
        in_specs=[pl.BlockSpec((tb, k), lambda j, i: (i, 0)), pl.BlockSpec((tb, nc), lambda j, i: (i, j))],
        out_specs=pl.BlockSpec((None, k, nc), lambda j, i: (j, 0, 0)),
        compiler_params=pltpu.CompilerParams(dimension_semantics=("arbitrary", "arbitrary"),
                                             vmem_limit_bytes=VMEM_LIMIT_BYTES),
    )(lhs, rhs)


def _adamw(w, g, m, v, rows, name):
    r, c = w.shape

    def body(w_ref, g_ref, m_ref, v_ref, d_ref, nm_ref, nv_ref):
        g_ = g_ref[...]
        m_ = ADAM_B1 * m_ref[...] + (1.0 - ADAM_B1) * g_
        v_ = ADAM_B2 * v_ref[...] + (1.0 - ADAM_B2) * jnp.square(g_)
        m_hat = m_ / (1.0 - ADAM_B1 ** ADAM_STEP)
        v_hat = v_ / (1.0 - ADAM_B2 ** ADAM_STEP)
        d_ref[...] = -ADAM_LR * (m_hat / (jnp.sqrt(v_hat) + ADAM_EPS) + ADAM_WD * w_ref[...])
        nm_ref[...] = m_
        nv_ref[...] = v_

    spec = pl.BlockSpec((rows, c), lambda i: (i, 0))
    return pl.pallas_call(
        body, name=name, out_shape=tuple(jax.ShapeDtypeStruct((r, c), F32) for _ in range(3)),
        grid=(r // rows,), in_specs=[spec] * 4, out_specs=(spec,) * 3,
        compiler_params=pltpu.CompilerParams(dimension_semantics=("arbitrary",),
                                             vmem_limit_bytes=VMEM_LIMIT_BYTES),
    )(w, g, m, v)


def _shift_down(ext, s):
    return pltpu.roll(ext, s, 0)


def _shift_up(ext, s):
    return pltpu.roll(ext, ext.shape[0] - s, 0)


def _lru_gates(xc, wa_ref, ba, wx_ref, bx, lam):
    pa, px = [], []
    for h in range(LRU_HEADS):
        xh = xc[:, h * HEAD_DIM:(h + 1) * HEAD_DIM].astype(BF16)
        pa.append(_dot(xh, wa_ref[h]))
        px.append(_dot(xh, wx_ref[h]))
    r = _sigmoid(jnp.concatenate(pa, axis=1) + ba)
    ig = _sigmoid(jnp.concatenate(px, axis=1) + bx)
    sp = _softplus(-lam)
    log_a = (-LRU_C * r) * sp
    a = jnp.exp(log_a)
    mult = jnp.sqrt(jnp.tanh(-log_a) * (1.0 + a * a))
    return r, ig, a, mult, sp


def _conv(ext, w_ref, b):
    y = b + _shift_down(ext, 3) * w_ref[0:1, :]
    y = y + _shift_down(ext, 2) * w_ref[1:2, :]
    y = y + _shift_down(ext, 1) * w_ref[2:3, :]
    y = y + ext * w_ref[3:4, :]
    return y[CONV_HIST:, :]


def _pool_diff(ext, pos):
    out = []
    for g, k in enumerate(POOL_WINDOWS):
        col = ext[:, g * POOL_GROUP_DIM:(g + 1) * POOL_GROUP_DIM]
        s = col
        for step in range(g + 1):
            s = s + _shift_down(s, 2 ** step)
        count = jnp.minimum(pos + 1, k).astype(F32)
        out.append(s[POOL_HIST:, :] / count - col[POOL_HIST:, :])
    return out


def _pool_mix(diff, pw_ref):
    return jnp.concatenate([_dot(diff[g].astype(BF16), pw_ref[g]) for g in range(len(POOL_WINDOWS))], axis=1)


def _branch_specs(tb, row_map, fixed):
    fixed3 = lambda i: (0, 0, 0)
    return [pl.BlockSpec((CONV_WIDTH, D_MODEL), fixed), pl.BlockSpec((1, D_MODEL), fixed),
            pl.BlockSpec((LRU_HEADS, HEAD_DIM, HEAD_DIM), fixed3), pl.BlockSpec((1, D_MODEL), fixed),
            pl.BlockSpec((LRU_HEADS, HEAD_DIM, HEAD_DIM), fixed3), pl.BlockSpec((1, D_MODEL), fixed),
            pl.BlockSpec((1, D_MODEL), fixed),
            pl.BlockSpec((len(POOL_WINDOWS), POOL_GROUP_DIM, POOL_GROUP_DIM), fixed3),
            pl.BlockSpec((1, POOL_WIDTH), fixed)]


def _branches_fwd(z, weights, seq, tb, shards):
    t = z.shape[0]
    nb = t // tb
    nbe = seq // tb
    groups = tb // F32_SUBLANES
    n = len(shards)

    def body(xa_ref, ga_ref, xb_ref, gb_ref, cw_ref, cb_ref, wa_ref, ba_ref, wx_ref, bx_ref, lam_ref,
             pw_ref, ps_ref, *refs):
        g_ins = refs[:n]
        ya_ref, yb_ref, hl_ref = refs[n:n + 3]
        g_outs = refs[n + 3:2 * n + 3]
        xa_ext, xb_ext, carry, a_s, u_s, send_sems, recv_sems, local_sems = refs[2 * n + 3:]
        blk = pl.program_id(0) % nbe
        start_gather, relay_gather, finish_gather = _gather_steps(shards, g_ins, g_outs, send_sems, recv_sems,
                                                                  local_sems)
        pl.when(pl.program_id(0) == 0)(start_gather)
        pl.when(pl.program_id(0) == nb // 2)(relay_gather)

        @pl.when(blk == 0)
        def _():
            xa_ext[0:CONV_HIST, :] = jnp.zeros((CONV_HIST, D_MODEL), F32)
            xb_ext[0:POOL_HIST, :] = jnp.zeros((POOL_HIST, POOL_WIDTH), F32)
            carry[...] = jnp.zeros_like(carry)

        xa_ext[CONV_HIST:, :] = xa_ref[...]
        xb_ext[POOL_HIST:, :] = xb_ref[...]
        ea = xa_ext[...]
        eb = xb_ext[...]
        xa_ext[0:CONV_HIST, :] = ea[tb:, :]
        xb_ext[0:POOL_HIST, :] = eb[tb:, :]

        xc = _conv(ea, cw_ref, cb_ref[...])
        _, ig, a, mult, _ = _lru_gates(xc, wa_ref, ba_ref[...], wx_ref, bx_ref[...], lam_ref[...])
        u = mult * (ig * xc)
        row8 = lax.broadcasted_iota(jnp.int32, (tb, D_MODEL), 0) % F32_SUBLANES
        for s in (1, 2, 4):
            m = row8 >= s
            u = jnp.where(m, a * _shift_down(u, s) + u, u)
            a = jnp.where(m, a * _shift_down(a, s), a)
        a_s[...] = a
        u_s[...] = u

        def step(g, cr):
            sl = pl.ds(pl.multiple_of(g * F32_SUBLANES, F32_SUBLANES), F32_SUBLANES)
            hb = a_s[sl, :] * cr + u_s[sl, :]
            hl_ref[sl, :] = hb
            return jnp.broadcast_to(hb[F32_SUBLANES - 1:F32_SUBLANES, :], (F32_SUBLANES, D_MODEL))

        carry[...] = lax.fori_loop(0, groups, step, carry[...], unroll=4)
        ga = ga_ref[...]
        ya_ref[...] = (hl_ref[...] * (ga * _sigmoid(ga))).astype(BF16)

        pos = blk * tb + lax.broadcasted_iota(jnp.int32, (tb, POOL_GROUP_DIM), 0)
        ypre = _pool_mix(_pool_diff(eb, pos), pw_ref)
        gb = gb_ref[...]
        yb_ref[...] = ((ypre * ps_ref[...]) * (gb * _sigmoid(gb))).astype(BF16)
        pl.when(pl.program_id(0) == nb - 1)(finish_gather)

    row = lambda i: (i, 0)
    fixed = lambda i: (0, 0)
    any_spec = pl.BlockSpec(memory_space=pl.ANY)
    in_specs = [pl.BlockSpec((tb, D_MODEL), lambda i: (i, 0)), pl.BlockSpec((tb, D_MODEL), lambda i: (i, 1)),
                pl.BlockSpec((tb, POOL_WIDTH), lambda i: (i, 4)), pl.BlockSpec((tb, POOL_WIDTH), lambda i: (i, 5)),
                ] + _branch_specs(tb, row, fixed) + [any_spec] * n
    g_shape, g_sems = _gather_shapes(shards)
    return pl.pallas_call(
        body, name="branches_fwd",
        out_shape=tuple([jax.ShapeDtypeStruct((t, D_MODEL), BF16), jax.ShapeDtypeStruct((t, POOL_WIDTH), BF16),
                         jax.ShapeDtypeStruct((t, D_MODEL), F32)] + g_shape),
        grid=(nb,), in_specs=in_specs,
        out_specs=tuple([pl.BlockSpec((tb, D_MODEL), row), pl.BlockSpec((tb, POOL_WIDTH), row),
                         pl.BlockSpec((tb, D_MODEL), row)] + [any_spec] * n),
        scratch_shapes=[pltpu.VMEM((tb + CONV_HIST, D_MODEL), F32), pltpu.VMEM((tb + POOL_HIST, POOL_WIDTH), F32),
                        pltpu.VMEM((F32_SUBLANES, D_MODEL), F32),
                        pltpu.VMEM((tb, D_MODEL), F32), pltpu.VMEM((tb, D_MODEL), F32)] + g_sems,
        compiler_params=pltpu.CompilerParams(dimension_semantics=("arbitrary",),
                                             vmem_limit_bytes=VMEM_LIMIT_BYTES),
    )(z, z, z, z, *weights, *[sh[0] for sh in shards])


def _branches_bwd(z, hl, dya, dyb, dzm, weights, seq, tb):
    t = z.shape[0]
    nb = t // tb
    nbe = seq // tb
    groups = tb // F32_SUBLANES
    n_pool = len(POOL_WINDOWS)

    def body(xa_ref, xap_ref, ga_ref, xb_ref, xbp_ref, gb_ref, hl_ref, hlp_ref, dya_ref, dyb_ref, dzm_ref,
             cw_ref, cb_ref, wa_ref, ba_ref, wx_ref, bx_ref, lam_ref, pw_ref, ps_ref,
             dz_ref, dcw_ref, dcb_ref, dwa_ref, dba_ref, dwx_ref, dbx_ref, dlam_ref, dpw_ref, dps_ref,
             xa_ext, xb_ext, hl_ext, a_ext, dxc_ext, dwin_ext, g_carry, b_s, d_s, g_s):
        i = pl.program_id(0)
        blk = (nb - 1 - i) % nbe

        @pl.when(i == 0)
        def _():
            for ref in (dcw_ref, dcb_ref, dwa_ref, dba_ref, dwx_ref, dbx_ref, dlam_ref, dpw_ref, dps_ref):
                ref[...] = jnp.zeros_like(ref)

        @pl.when(blk == nbe - 1)
        def _():
            a_ext[tb:, :] = jnp.zeros((F32_SUBLANES, D_MODEL), F32)
            dxc_ext[tb:, :] = jnp.zeros((CONV_HIST, D_MODEL), F32)
            dwin_ext[tb:, :] = jnp.zeros((POOL_HIST, POOL_WIDTH), F32)
            g_carry[...] = jnp.zeros_like(g_carry)

        live = (blk > 0).astype(F32)
        xa_ext[0:CONV_HIST, :] = xap_ref[...] * live
        xa_ext[CONV_HIST:, :] = xa_ref[...]
        xb_ext[0:POOL_HIST, :] = xbp_ref[...] * live
        xb_ext[POOL_HIST:, :] = xb_ref[...]
        hl_ext[0:F32_SUBLANES, :] = hlp_ref[...] * live
        hl_ext[F32_SUBLANES:, :] = hl_ref[...]
        ea = xa_ext[...]
        eb = xb_ext[...]

        xc = _conv(ea, cw_ref, cb_ref[...])
        lam = lam_ref[...]
        r, ig, a, mult, sp = _lru_gates(xc, wa_ref, ba_ref[...], wx_ref, bx_ref[...], lam)
        hl = hl_ref[...]
        ga = ga_ref[...]
        sga = _sigmoid(ga)
        dya = dya_ref[...]
        dhl = dya * (ga * sga)
        dz_ref[:, D_MODEL:2 * D_MODEL] = (dya * hl * (sga * (1.0 + ga * (1.0 - sga)))).astype(BF16)

        a_ext[0:tb, :] = a
        b = _shift_up(a_ext[...], 1)[0:tb, :]
        a_ext[tb:, :] = jnp.broadcast_to(a[0:1, :], (F32_SUBLANES, D_MODEL))
        d = dhl
        row8 = lax.broadcasted_iota(jnp.int32, (tb, D_MODEL), 0) % F32_SUBLANES
        for s in (1, 2, 4):
            m = row8 < F32_SUBLANES - s
            d = jnp.where(m, d + b * _shift_up(d, s), d)
            b = jnp.where(m, b * _shift_up(b, s), b)
        b_s[...] = b
        d_s[...] = d

        def step(k, cr):
            sl = pl.ds(pl.multiple_of((groups - 1 - k) * F32_SUBLANES, F32_SUBLANES), F32_SUBLANES)
            gb_ = d_s[sl, :] + b_s[sl, :] * cr
            g_s[sl, :] = gb_
            return jnp.broadcast_to(gb_[0:1, :], (F32_SUBLANES, D_MODEL))

        g_carry[...] = lax.fori_loop(0, groups, step, g_carry[...], unroll=4)
        gsc = g_s[...]
        da = gsc * _shift_down(hl_ext[...], 1)[F32_SUBLANES:, :]
        dmult = gsc * (ig * xc)
        dig = gsc * (mult * xc)
        dxc = gsc * (mult * ig)
        dlog_a = da * a - (a * a) * dmult / mult
        dr = dlog_a * (-LRU_C * sp)
        dlam_ref[...] += jnp.sum(dlog_a * (-LRU_C * r), axis=0, keepdims=True)
        dpa = dr * (r * (1.0 - r))
        dpx = dig * (ig * (1.0 - ig))
        dba_ref[...] += jnp.sum(dpa, axis=0, keepdims=True)
        dbx_ref[...] += jnp.sum(dpx, axis=0, keepdims=True)
        back = []
        for h in range(LRU_HEADS):
            cols = slice(h * HEAD_DIM, (h + 1) * HEAD_DIM)
            xh = xc[:, cols].astype(BF16)
            dpa_h = dpa[:, cols].astype(BF16)
            dpx_h = dpx[:, cols].astype(BF16)
            dwa_ref[h] += _dot_tn(xh, dpa_h)
            dwx_ref[h] += _dot_tn(xh, dpx_h)
            back.append(_dot_nt(dpa_h, wa_ref[h]) + _dot_nt(dpx_h, wx_ref[h]))
        dxc = dxc + jnp.concatenate(back, axis=1)
        dcb_ref[...] += jnp.sum(dxc, axis=0, keepdims=True)
        for k in range(CONV_WIDTH):
            tap = _shift_down(ea, CONV_WIDTH - 1 - k)[CONV_HIST:, :] if k < CONV_WIDTH - 1 else ea[CONV_HIST:, :]
            dcw_ref[k:k + 1, :] += jnp.sum(dxc * tap, axis=0, keepdims=True)
        dxc_ext[0:tb, :] = dxc
        ed = dxc_ext[...]
        dxa = ed * cw_ref[3:4, :]
        dxa = dxa + _shift_up(ed, 1) * cw_ref[2:3, :]
        dxa = dxa + _shift_up(ed, 2) * cw_ref[1:2, :]
        dxa = dxa + _shift_up(ed, 3) * cw_ref[0:1, :]
        dz_ref[:, 0:D_MODEL] = dxa[0:tb, :].astype(BF16)
        dxc_ext[tb:, :] = dxc[0:CONV_HIST, :]

        pos = blk * tb + lax.broadcasted_iota(jnp.int32, (tb, POOL_GROUP_DIM), 0)
        diff = _pool_diff(eb, pos)
        ypre = _pool_mix(diff, pw_ref)
        ps = ps_ref[...]
        gb = gb_ref[...]
        sgb = _sigmoid(gb)
        dyb = dyb_ref[...]
        dyp = dyb * (gb * sgb)
        dz_ref[:, 2 * D_MODEL + POOL_WIDTH:3 * D_MODEL] = (
            dyb * (ypre * ps) * (sgb * (1.0 + gb * (1.0 - sgb)))).astype(BF16)
        dps_ref[...] += jnp.sum(dyp * ypre, axis=0, keepdims=True)
        dypre = dyp * ps
        for g, k in enumerate(POOL_WINDOWS):
            cols = slice(g * POOL_GROUP_DIM, (g + 1) * POOL_GROUP_DIM)
            dyg = dypre[:, cols].astype(BF16)
            dpw_ref[g] += _dot_tn(diff[g].astype(BF16), dyg)
            ddiff = _dot_nt(dyg, pw_ref[g])
            count = jnp.minimum(pos + 1, k).astype(F32)
            dwin = ddiff / count
            dwin_ext[0:tb, cols] = dwin
            s = dwin_ext[:, cols]
            for step_ in range(g + 1):
                s = s + _shift_up(s, 2 ** step_)
            dz_ref[:, 2 * D_MODEL + g * POOL_GROUP_DIM:2 * D_MODEL + (g + 1) * POOL_GROUP_DIM] = (
                s[0:tb, :] - ddiff).astype(BF16)
            dwin_ext[tb:, cols] = dwin[0:POOL_HIST, :]

        dz_ref[:, 3 * D_MODEL:] = dzm_ref[...]

        @pl.when(i == nb - 1)
        def _():
            dlam_ref[...] = dlam_ref[...] * (-_sigmoid(-lam))

    rev = lambda i: (nb - 1 - i, 0)
    fixed = lambda i: (0, 0)
    fixed3 = lambda i: (0, 0, 0)

    def prev(rows, col):
        per = tb // rows
        return lambda i: (jnp.maximum((nb - 1 - i) * per - 1, 0), col)

    in_specs = [pl.BlockSpec((tb, D_MODEL), lambda i: (nb - 1 - i, 0)),
                pl.BlockSpec((CONV_HIST, D_MODEL), prev(CONV_HIST, 0)),
                pl.BlockSpec((tb, D_MODEL), lambda i: (nb - 1 - i, 1)),
                pl.BlockSpec((tb, POOL_WIDTH), lambda i: (nb - 1 - i, 4)),
                pl.BlockSpec((POOL_HIST, POOL_WIDTH), prev(POOL_HIST, 4)),
                pl.BlockSpec((tb, POOL_WIDTH), lambda i: (nb - 1 - i, 5)),
                pl.BlockSpec((tb, D_MODEL), rev),
                pl.BlockSpec((F32_SUBLANES, D_MODEL), prev(F32_SUBLANES, 0)),
                pl.BlockSpec((tb, D_MODEL), rev), pl.BlockSpec((tb, POOL_WIDTH), rev),
                pl.BlockSpec((tb, 2 * D_MODEL), rev)] + _branch_specs(tb, rev, fixed)
    out_shape = (jax.ShapeDtypeStruct((t, IN_COLS), BF16),
                 jax.ShapeDtypeStruct((CONV_WIDTH, D_MODEL), F32), jax.ShapeDtypeStruct((1, D_MODEL), F32),
                 jax.ShapeDtypeStruct((LRU_HEADS, HEAD_DIM, HEAD_DIM), F32), jax.ShapeDtypeStruct((1, D_MODEL), F32),
                 jax.ShapeDtypeStruct((LRU_HEADS, HEAD_DIM, HEAD_DIM), F32), jax.ShapeDtypeStruct((1, D_MODEL), F32),
                 jax.ShapeDtypeStruct((1, D_MODEL), F32),
                 jax.ShapeDtypeStruct((n_pool, POOL_GROUP_DIM, POOL_GROUP_DIM), F32),
                 jax.ShapeDtypeStruct((1, POOL_WIDTH), F32))
    out_specs = (pl.BlockSpec((tb, IN_COLS), rev),
                 pl.BlockSpec((CONV_WIDTH, D_MODEL), fixed), pl.BlockSpec((1, D_MODEL), fixed),
                 pl.BlockSpec((LRU_HEADS, HEAD_DIM, HEAD_DIM), fixed3), pl.BlockSpec((1, D_MODEL), fixed),
                 pl.BlockSpec((LRU_HEADS, HEAD_DIM, HEAD_DIM), fixed3), pl.BlockSpec((1, D_MODEL), fixed),
                 pl.BlockSpec((1, D_MODEL), fixed),
                 pl.BlockSpec((n_pool, POOL_GROUP_DIM, POOL_GROUP_DIM), fixed3),
                 pl.BlockSpec((1, POOL_WIDTH), fixed))
    scratch = [pltpu.VMEM((tb + CONV_HIST, D_MODEL), F32), pltpu.VMEM((tb + POOL_HIST, POOL_WIDTH), F32),
               pltpu.VMEM((tb + F32_SUBLANES, D_MODEL), F32), pltpu.VMEM((tb + F32_SUBLANES, D_MODEL), F32),
               pltpu.VMEM((tb + CONV_HIST, D_MODEL), F32), pltpu.VMEM((tb + POOL_HIST, POOL_WIDTH), F32),
               pltpu.VMEM((F32_SUBLANES, D_MODEL), F32),
               pltpu.VMEM((tb, D_MODEL), F32), pltpu.VMEM((tb, D_MODEL), F32), pltpu.VMEM((tb, D_MODEL), F32)]
    return pl.pallas_call(
        body, name="branches_bwd", out_shape=out_shape, grid=(nb,), in_specs=in_specs, out_specs=out_specs,
        scratch_shapes=scratch,
        compiler_params=pltpu.CompilerParams(dimension_semantics=("arbitrary",),
                                             vmem_limit_bytes=VMEM_LIMIT_BYTES),
    )(z, z, z, z, z, z, hl, hl, dya, dyb, dzm, *weights)


def _merge_head(x2d, ya, yb, z, p2d, tgt, w_pl, w_pp, w_out, w_pg, w_pe, g2, gf, tb):
    t = x2d.shape[0]
    p_dim = p2d.shape[1]

    def body(x_ref, ya_ref, yb_ref, ma_ref, mb_ref, p_ref, t_ref, wpl_ref, wpp_ref, wout_ref, wpg_ref, wpe_ref,
             g2_ref, gf_ref,
             loss_ref, dg2_ref, dgf_ref, dxr_ref, dya_ref, dyb_ref, dzm_ref,
             mg_ref, do_ref, hn_ref, dgp_ref, dpe_ref, da_ref, dbm_ref, pbf_ref):
        @pl.when(pl.program_id(0) == 0)
        def _():
            loss_ref[...] = jnp.zeros_like(loss_ref)
            dg2_ref[...] = jnp.zeros_like(dg2_ref)
            dgf_ref[...] = jnp.zeros_like(dgf_ref)

        a_ = _dot(ya_ref[...], wpl_ref[...])
        bm = _dot(yb_ref[...], wpp_ref[...])
        sa = _sigmoid(ma_ref[...])
        sb = _sigmoid(mb_ref[...])
        mg = (sa * a_ + sb * bm).astype(BF16)
        mg_ref[...] = mg
        x1 = x_ref[...] + _dot(mg, wout_ref[...])
        xn2, r2 = _rms(x1)
        g2 = g2_ref[...]
        hn = (xn2 * g2).astype(BF16)
        hn_ref[...] = hn
        gate = _sigmoid(_dot(hn, wpg_ref[...]))
        pbf = p_ref[...].astype(BF16)
        pbf_ref[...] = pbf
        pe = _dot(pbf, wpe_ref[...])
        x2 = x1 + gate * pe
        xn3, r3 = _rms(x2)
        gf = gf_ref[...]
        err = xn3 * gf - t_ref[...]
        loss_ref[...] += 0.5 * jnp.sum(jnp.mean(err * err, axis=-1))

        dy = err * (1.0 / D_MODEL)
        dgf_ref[...] += jnp.sum(dy * xn3, axis=0, keepdims=True)
        dx2 = _rms_bwd(dy * gf, xn3, r3)
        dpe_ref[...] = (dx2 * gate).astype(BF16)
        dgp = ((dx2 * pe) * (gate * (1.0 - gate))).astype(BF16)
        dgp_ref[...] = dgp
        dhn = _dot_nt(dgp, wpg_ref[...])
        dg2_ref[...] += jnp.sum(dhn * xn2, axis=0, keepdims=True)
        dx1 = dx2 + _rms_bwd(dhn * g2, xn2, r2)
        dxr_ref[...] = dx1
        do = dx1.astype(BF16)
        do_ref[...] = do
        dmg = _dot_nt(do, wout_ref[...])
        da = (dmg * sa).astype(BF16)
        dbm = (dmg * sb).astype(BF16)
        da_ref[...] = da
        dbm_ref[...] = dbm
        dzm_ref[:, 0:D_MODEL] = (dmg * a_ * (sa * (1.0 - sa))).astype(BF16)
        dzm_ref[:, D_MODEL:] = (dmg * bm * (sb * (1.0 - sb))).astype(BF16)
        dya_ref[...] = _dot_nt(da, wpl_ref[...])
        dyb_ref[...] = _dot_nt(dbm, wpp_ref[...])

    row = lambda i: (i, 0)
    fixed = lambda i: (0, 0)

    def resident(shape):
        return pl.BlockSpec(shape, fixed, pipeline_mode=pl.Buffered(1))

    tok = lambda width: pl.BlockSpec((tb, width), row)
    in_specs = [tok(D_MODEL), tok(D_MODEL), tok(POOL_WIDTH),
                pl.BlockSpec((tb, D_MODEL), lambda i: (i, 3)), pl.BlockSpec((tb, D_MODEL), lambda i: (i, 4)),
                tok(p_dim), tok(D_MODEL),
                resident((D_MODEL, D_MODEL)), resident((POOL_WIDTH, D_MODEL)), resident((D_MODEL, D_MODEL)),
                resident((D_MODEL, D_MODEL)), resident((p_dim, D_MODEL)),
                pl.BlockSpec((1, D_MODEL), fixed), pl.BlockSpec((1, D_MODEL), fixed)]
    bf = lambda width: jax.ShapeDtypeStruct((t, width), BF16)
    f32 = lambda width: jax.ShapeDtypeStruct((t, width), F32)
    out_shape = (jax.ShapeDtypeStruct((F32_SUBLANES, 128), F32), jax.ShapeDtypeStruct((1, D_MODEL), F32),
                 jax.ShapeDtypeStruct((1, D_MODEL), F32),
                 f32(D_MODEL), f32(D_MODEL), f32(POOL_WIDTH), bf(2 * D_MODEL),
                 bf(D_MODEL), bf(D_MODEL), bf(D_MODEL), bf(D_MODEL), bf(D_MODEL), bf(D_MODEL), bf(D_MODEL), bf(p_dim))
    out_specs = (pl.BlockSpec((F32_SUBLANES, 128), fixed), pl.BlockSpec((1, D_MODEL), fixed),
                 pl.BlockSpec((1, D_MODEL), fixed),
                 tok(D_MODEL), tok(D_MODEL), tok(POOL_WIDTH), tok(2 * D_MODEL),
                 tok(D_MODEL), tok(D_MODEL), tok(D_MODEL), tok(D_MODEL), tok(D_MODEL), tok(D_MODEL), tok(D_MODEL),
                 tok(p_dim))
    return pl.pallas_call(
        body, name="merge_head", out_shape=out_shape, grid=(t // tb,), in_specs=in_specs, out_specs=out_specs,
        compiler_params=pltpu.CompilerParams(dimension_semantics=("arbitrary",),
                                             vmem_limit_bytes=VMEM_LIMIT_BYTES),
    )(x2d, ya, yb, z, z, p2d, tgt, w_pl, w_pp, w_out, w_pg, w_pe, g2, gf)


def _pad_rows(a, rows):
    return jnp.pad(a, ((0, rows - a.shape[0]), (0, D_MODEL - a.shape[1])))


def _pack_bag(parts, tail=None):
    rows = [_pad_rows(a.reshape(-1, a.shape[-1]) if a.shape[-1] != HEAD_DIM else a.reshape(-1, D_MODEL), n)
            for a, n in zip(parts, BAG_PART_ROWS)]
    spare = BAG_ROWS - sum(BAG_PART_ROWS)
    if tail is not None:
        rows.append(_pad_rows(tail, F32_SUBLANES))
        spare -= F32_SUBLANES
    rows.append(jnp.zeros((spare, D_MODEL), F32))
    return jnp.concatenate(rows, axis=0)


def _unpack_bag(bag, shapes):
    out, at = [], 0
    for shape, n in zip(shapes, BAG_PART_ROWS):
        size = 1
        for s in shape:
            size *= s
        if size % D_MODEL == 0:
            piece = bag[at:at + size // D_MODEL, :]
        else:
            piece = bag[at:at + 1, :size]
        out.append(piece.reshape(shape))
        at += n
    return out


def kernel(x, p, norm_g, w_in, conv_w, conv_b, lru_w_a, lru_b_a, lru_w_x, lru_b_x, lru_lambda, pool_w, pool_scale, w_proj_lru, w_proj_pool, w_out, ple_norm_g, w_ple_gate, w_ple_proj, final_g, loss_target, m_norm_g, m_w_in, m_conv_w, m_conv_b, m_lru_w_a, m_lru_b_a, m_lru_w_x, m_lru_b_x, m_lru_lambda, m_pool_w, m_pool_scale, m_w_proj_lru, m_w_proj_pool, m_w_out, m_ple_norm_g, m_w_ple_gate, m_w_ple_proj, m_final_g, v_norm_g, v_w_in, v_conv_w, v_conv_b, v_lru_w_a, v_lru_b_a, v_lru_w_x, v_lru_b_x, v_lru_lambda, v_pool_w, v_pool_scale, v_w_proj_lru, v_w_proj_pool, v_w_out, v_ple_norm_g, v_w_ple_gate, v_w_ple_proj, v_final_g):
    bsz, seq, _ = x.shape
    t = bsz * seq
    tb_mm = min(512, seq)
    tb_seq = min(256, seq // 2) if seq >= 512 else seq
    x2d = x.reshape(t, D_MODEL)
    p2d = p.reshape(t, p.shape[-1])
    tgt = loss_target.reshape(t, D_MODEL)
    chip = 2 * lax.axis_index("x") + lax.axis_index("y")

    rest = [(w_proj_lru[0], 0), (w_proj_pool[0], 1), (w_out[0], 0), (w_ple_gate[0], 0), (w_ple_proj[0], 1)]
    z, w_in_f, conv_w_f = _in_proj_gather(x2d, norm_g, w_in[0].astype(BF16), [(conv_w[0], 1, False)], tb_mm)

    wa_bf = lru_w_a[0].astype(BF16)
    wx_bf = lru_w_x[0].astype(BF16)
    pw_bf = pool_w[0].astype(BF16)
    branch_w = (conv_w_f, conv_b, wa_bf, lru_b_a.reshape(1, D_MODEL), wx_bf, lru_b_x.reshape(1, D_MODEL),
                lru_lambda, pw_bf, pool_scale)

    ya, yb, hl, w_pl_f, w_pp_f, w_out_f, w_pg_f, w_pe_f = _branches_fwd(
        z, branch_w, seq, tb_seq, [(w.astype(BF16), axis, True) for w, axis in rest])
    (loss_acc, d_g2, d_gf, dx_res, dya, dyb, dzm, mg_bf, do_bf, hn_bf, dgp_bf, dpe_bf, da_bf, dbm_bf, p_bf) = _merge_head(
        x2d, ya, yb, z, p2d, tgt, w_pl_f, w_pp_f, w_out_f, w_pg_f, w_pe_f, ple_norm_g, final_g.reshape(1, D_MODEL),
        tb_seq)
    (dz, d_cw, d_cb, d_wa, d_ba, d_wx, d_bx, d_lam, d_pw, d_ps) = _branches_bwd(
        z, hl, dya, dyb, dzm, branch_w, seq, tb_seq)
    dx, h_bf, d_g1 = _in_proj_bwd(dz, w_in_f, x2d, dx_res, norm_g, tb_mm)

    g_in = _weight_grad(h_bf, dz, N_CHIPS, tb_mm, "dw_in").reshape(8, D_MODEL // 2, IN_COLS // N_CHIPS)
    g_pl = _weight_grad(ya, da_bf, 1, tb_mm, "dw_proj_lru").reshape(8, D_MODEL // 8, D_MODEL)
    g_pp = _weight_grad(yb, dbm_bf, 1, tb_mm, "dw_proj_pool")[0]
    g_out = _weight_grad(mg_bf, do_bf, 1, tb_mm, "dw_out").reshape(8, D_MODEL // 8, D_MODEL)
    g_pg = _weight_grad(hn_bf, dgp_bf, 1, tb_mm, "dw_ple_gate").reshape(8, D_MODEL // 8, D_MODEL)
    p_dim = p2d.shape[1]
    g_pe = _weight_grad(p_bf, dpe_bf, 1, tb_mm, "dw_ple_proj")[0]

    (r_in,) = _reduce_scatter([(g_in, False)], "rs_w_in", BF16)
    r_pl, r_pp, r_out, r_pg, r_pe = _reduce_scatter(
        [(g_pl, False), (g_pp, True), (g_out, False), (g_pg, False), (g_pe, True)], "rs_proj", BF16)
    small_shapes = [(1, D_MODEL), (1, CONV_WIDTH, D_MODEL), (1, D_MODEL), lru_w_a.shape, lru_b_a.shape, lru_w_x.shape,
                    lru_b_x.shape, (1, D_MODEL), pool_w.shape, pool_scale.shape, (1, D_MODEL), final_g.shape]
    bag = _pack_bag([d_g1, d_cw, d_cb, d_wa, d_ba.reshape(1, D_MODEL), d_wx, d_bx.reshape(1, D_MODEL), d_lam, d_pw,
                     d_ps, d_g2, d_gf], tail=loss_acc)
    (bag_mine,) = _reduce_scatter([(bag.reshape(8, BAG_ROWS // 8, D_MODEL), False)], "rs_small")
    (bag_sum,) = _gather_shards([(bag_mine.reshape(BAG_ROWS // N_CHIPS, D_MODEL), 0, True)], "gather_small")
    (g_g1, g_cw_full, g_cb, g_wa, g_ba, g_wx, g_bx, g_lam, g_pw, g_ps, g_g2, g_gf) = _unpack_bag(bag_sum, small_shapes)
    cw_cols = D_MODEL // N_CHIPS
    g_cw = lax.dynamic_slice_in_dim(g_cw_full, chip * cw_cols, cw_cols, axis=2)

    def big_update(w, g2d, m, v, rows, name):
        d, nm, nv = _adamw(w[0], g2d, m[0], v[0], rows, name)
        return g2d[None], d[None], nm[None], nv[None]

    u_in = big_update(w_in, r_in.reshape(D_MODEL, IN_COLS // N_CHIPS), m_w_in, v_w_in, 256, "adamw_w_in")
    u_pl = big_update(w_proj_lru, r_pl.reshape(D_MODEL // N_CHIPS, D_MODEL), m_w_proj_lru, v_w_proj_lru, 256, "adamw_w_proj_lru")
    u_pp = big_update(w_proj_pool, r_pp.reshape(POOL_WIDTH, D_MODEL // N_CHIPS), m_w_proj_pool, v_w_proj_pool, 512, "adamw_w_proj_pool")
    u_out = big_update(w_out, r_out.reshape(D_MODEL // N_CHIPS, D_MODEL), m_w_out, v_w_out, 256, "adamw_w_out")
    u_pg = big_update(w_ple_gate, r_pg.reshape(D_MODEL // N_CHIPS, D_MODEL), m_w_ple_gate, v_w_ple_gate, 256, "adamw_w_ple_gate")
    u_pe = big_update(w_ple_proj, r_pe.reshape(p_dim, D_MODEL // N_CHIPS), m_w_ple_proj, v_w_ple_proj, 256, "adamw_w_ple_proj")
    u_cw = big_update(conv_w, g_cw[0], m_conv_w, v_conv_w, CONV_WIDTH, "adamw_conv_w")

    small_w = [norm_g, None, conv_b, lru_w_a, lru_b_a, lru_w_x, lru_b_x, lru_lambda, pool_w, pool_scale, ple_norm_g, final_g]
    small_m = [m_norm_g, None, m_conv_b, m_lru_w_a, m_lru_b_a, m_lru_w_x, m_lru_b_x, m_lru_lambda, m_pool_w, m_pool_scale, m_ple_norm_g, m_final_g]
    small_v = [v_norm_g, None, v_conv_b, v_lru_w_a, v_lru_b_a, v_lru_w_x, v_lru_b_x, v_lru_lambda, v_pool_w, v_pool_scale, v_ple_norm_g, v_final_g]
    fill = jnp.zeros((CONV_WIDTH, D_MODEL), F32)

    def bag_of(arrs):
        return _pack_bag([fill if a is None else (a[0] if a.ndim > 1 else a[None]) for a in arrs])

    d_bag, m_bag, v_bag = _adamw(bag_of(small_w), bag_sum, bag_of(small_m), bag_of(small_v), BAG_ROWS // 8, "adamw_small")
    d_small = _unpack_bag(d_bag, small_shapes)
    m_small = _unpack_bag(m_bag, small_shapes)
    v_small = _unpack_bag(v_bag, small_shapes)

    loss = bag_sum[sum(BAG_PART_ROWS), 0]
    grad_x = dx.reshape(bsz, seq, D_MODEL)

    def ordered(small, pick):
        s = list(small)
        return [s[0], u_in[pick], u_cw[pick], s[2], s[3], s[4], s[5], s[6], s[7], s[8], s[9],
                u_pl[pick], u_pp[pick], u_out[pick], s[10], u_pg[pick], u_pe[pick], s[11]]

    grads = ordered([g_g1, None, g_cb, g_wa, g_ba, g_wx, g_bx, g_lam, g_pw, g_ps, g_g2, g_gf], 0)
    return (loss, grad_x, *grads, *ordered(d_small, 1), *ordered(m_small, 2), *ordered(v_small, 3))
```

```python
import functools

import jax
import jax.numpy as jnp
from jax import lax
from jax.experimental import pallas as pl
from jax.experimental.pallas import tpu as pltpu

F32 = jnp.float32
BF16 = jnp.bfloat16
MESH = pl.DeviceIdType.MESH
ALL_AXES = ("x", "y", "c")

D_MODEL = 1024
LRU_HEADS = 8
HEAD_DIM = 128
CONV_WIDTH = 4
LRU_C = 8.0
POOL_WIDTH = 512
POOL_WINDOWS = (2, 4, 8, 16)
POOL_GROUP_DIM = 128
IN_COLS = 5120
N_CHIPS = 4
EPS = 1e-6

ADAM_LR = 0.001
ADAM_B1 = 0.9
ADAM_B2 = 0.999
ADAM_EPS = 1e-08
ADAM_WD = 0.01
ADAM_STEP = 10

F32_SUBLANES = 8
CONV_HIST = 8
POOL_HIST = 16
VMEM_LIMIT_BYTES = 58 * 1024 * 1024
BAG_PART_ROWS = (8, 8, 8, 128, 8, 128, 8, 8, 64, 8, 8, 8)
BAG_ROWS = 448


def _dot(a, b):
    return jnp.dot(a, b, preferred_element_type=F32)


def _dot_nt(a, b):
    return lax.dot_general(a, b, (((1,), (1,)), ((), ())), preferred_element_type=F32)


def _dot_tn(a, b):
    return lax.dot_general(a, b, (((0,), (0,)), ((), ())), preferred_element_type=F32)


def _sigmoid(v):
    return jax.nn.sigmoid(v)


def _softplus(v):
    return jnp.maximum(v, 0.0) + jnp.log1p(jnp.exp(-jnp.abs(v)))


def _place():
    return lax.axis_index("x"), lax.axis_index("y"), lax.axis_index("c")


GATHER_SEMS = 6


def _gather_shapes(shards):
    out_shape = []
    for arr, axis, _ in shards:
        r, cols = arr.shape
        out_shape.append(jax.ShapeDtypeStruct((N_CHIPS * r, cols) if axis == 0 else (r, N_CHIPS * cols), arr.dtype))
    n = len(shards)
    sems = [pltpu.SemaphoreType.DMA((n * GATHER_SEMS,)), pltpu.SemaphoreType.DMA((n * GATHER_SEMS,)),
            pltpu.SemaphoreType.DMA((n,))]
    return out_shape, sems


def _gather_steps(shards, ins, outs, send_sems, recv_sems, local_sems):
    n = len(shards)
    x, y, c = _place()
    me, sibling = (x, y, c), (x, y, 1 - c)
    chips = [(x, 1 - y), (1 - x, y), (1 - x, 1 - y)]

    def region(k, cx, cy, hc):
        (r, cols), axis = shards[k][0].shape, shards[k][1]
        j = 2 * cx + cy
        if axis == 0:
            if hc is None:
                return outs[k].at[pl.ds(j * r, r), :]
            return outs[k].at[pl.ds(j * r + hc * (r // 2), r // 2), :]
        if hc is None:
            return outs[k].at[:, pl.ds(j * cols, cols)]
        return outs[k].at[pl.ds(hc * (r // 2), r // 2), pl.ds(j * cols, cols)]

    def remote(k, sem, block, to, src=None):
        dst = region(k, *block)
        return pltpu.make_async_remote_copy(
            src_ref=dst if src is None else src, dst_ref=dst,
            send_sem=send_sems.at[k * GATHER_SEMS + sem], recv_sem=recv_sems.at[k * GATHER_SEMS + sem],
            device_id=to, device_id_type=MESH)

    def first(k, idx):
        r, split = shards[k][0].shape[0], shards[k][2]
        src = ins[k].at[pl.ds(c * (r // 2), r // 2), :] if split else ins[k]
        return remote(k, idx, (x, y, c if split else None), (*chips[idx], c), src=src)

    def relay(k):
        src_chip = (jnp.bitwise_xor(x, 1 - c), jnp.bitwise_xor(y, c))
        dst_chip = (jnp.bitwise_xor(x, c), jnp.bitwise_xor(y, 1 - c))
        return remote(k, 2, (*src_chip, c), (*dst_chip, c))

    def passed(k, idx):
        return remote(k, 3 + idx, (*chips[idx], c), sibling)

    def mine(k):
        return pltpu.make_async_copy(ins[k], region(k, x, y, None), local_sems.at[k])

    def start():
        for k in range(n):
            mine(k).start()
            for idx in range(2 if shards[k][2] else 3):
                first(k, idx).start()

    def relay_on():
        for k in range(n):
            split = shards[k][2]
            for idx in range(2):
                remote(k, idx, (*chips[idx], c if split else None), me).wait_recv()
            if split:
                relay(k).start()
                passed(k, 0).start()
                passed(k, 1).start()

    def finish():
        for k in range(n):
            split = shards[k][2]
            remote(k, 2, (*chips[2], c if split else None), me).wait_recv()
            if split:
                passed(k, 2).start()
        for k in range(n):
            if shards[k][2]:
                for idx in range(3):
                    remote(k, 3 + idx, (*chips[idx], 1 - c), me).wait_recv()
        for k in range(n):
            if shards[k][2]:
                for cp in (first(k, 0), first(k, 1), relay(k), passed(k, 0), passed(k, 1), passed(k, 2)):
                    cp.wait_send()
            else:
                for idx in range(3):
                    first(k, idx).wait_send()
            mine(k).wait()

    return start, relay_on, finish


def _gather_shards(shards, name):
    n = len(shards)

    def body(*refs):
        for step in _gather_steps(shards, refs[:n], refs[n:2 * n], *refs[2 * n:]):
            step()

    out_shape, sems = _gather_shapes(shards)
    any_spec = pl.BlockSpec(memory_space=pl.ANY)
    return pl.pallas_call(
        body, name=name, out_shape=tuple(out_shape),
        in_specs=[any_spec] * n, out_specs=tuple([any_spec] * n), scratch_shapes=sems,
    )(*[s[0] for s in shards])


RS_ADD_ROWS = (64, 56, 32, 16, 8)


RS_SEMS = 8
RS_LOCAL_SEMS = 5


def _rs_piece_shape(part):
    arr, cols = part
    return (arr.shape[0] // 2, arr.shape[1] // N_CHIPS) if cols else tuple(arr.shape[1:])


def _rs_shapes(parts, wire):
    n = len(parts)
    shapes = [_rs_piece_shape(p) for p in parts]
    out_shape = [jax.ShapeDtypeStruct((2,) + s, F32) for s in shapes]
    scratch = []
    for lead, dtype in ((N_CHIPS, F32), (N_CHIPS, F32), (N_CHIPS, wire), (None, F32), (N_CHIPS, wire)):
        for s in shapes:
            scratch.append(pltpu.VMEM(s if lead is None else (lead,) + s, dtype))
    scratch += [pltpu.SemaphoreType.DMA((n * RS_SEMS,)), pltpu.SemaphoreType.DMA((n * RS_SEMS,)),
                pltpu.SemaphoreType.DMA((n * RS_LOCAL_SEMS,))]
    return out_shape, scratch


def _rs_steps(parts, wire, ins, outs, scratch):
    n = len(parts)
    own, sib, got, fin, snd = (scratch[k * n:(k + 1) * n] for k in range(5))
    send_sems, recv_sems, local_sems = scratch[5 * n:]
    shapes = [_rs_piece_shape(p) for p in parts]
    x, y, c = _place()
    j_me = 2 * x + y
    me, sibling = (x, y, c), (x, y, 1 - c)
    chips = [(x, 1 - y), (1 - x, y), (1 - x, 1 - y)]

    def piece(a, jj, core):
        if parts[a][1]:
            r, cl = shapes[a]
            return ins[a].at[pl.ds(core * r, r), pl.ds(jj * cl, cl)]
        return ins[a].at[2 * jj + core]

    def remote(a, sem, src, dst, to):
        return pltpu.make_async_remote_copy(
            src_ref=src, dst_ref=dst, send_sem=send_sems.at[a * RS_SEMS + sem],
            recv_sem=recv_sems.at[a * RS_SEMS + sem], device_id=to, device_id_type=MESH)

    def rows_loop(a, fn):
        r = shapes[a][0]
        step = max(s for s in RS_ADD_ROWS if r % s == 0)

        def it(i, carry):
            fn(pl.ds(pl.multiple_of(i * step, step), step))
            return carry

        lax.fori_loop(0, r // step, it, 0)

    def load(a, jj):
        return pltpu.make_async_copy(piece(a, jj, c), own[a].at[jj], local_sems.at[a * RS_LOCAL_SEMS + jj])

    def to_sibling(a, jj):
        return remote(a, jj, piece(a, jj, 1 - c), sib[a].at[jj], sibling)

    def to_owner(a, idx):
        chip = chips[idx]
        return remote(a, 4 + idx, snd[a].at[2 * chip[0] + chip[1]], got[a].at[j_me], (*chip, c))

    def store(a):
        return pltpu.make_async_copy(fin[a], outs[a].at[c], local_sems.at[a * RS_LOCAL_SEMS + 4])

    def result_to_sibling(a):
        return remote(a, 7, fin[a], outs[a].at[c], sibling)

    def exchange():
        for a in range(n):
            for jj in range(N_CHIPS):
                load(a, jj).start()
                to_sibling(a, jj).start()

    def chip_sums():
        for a in range(n):
            for jj in range(N_CHIPS):
                load(a, jj).wait()
                remote(a, jj, sib[a].at[jj], sib[a].at[jj], me).wait_recv()

                def add(sl, a=a, jj=jj):
                    q = own[a][jj, sl, :] + sib[a][jj, sl, :]
                    own[a][jj, sl, :] = q
                    snd[a][jj, sl, :] = q.astype(wire)

                rows_loop(a, add)
        for a in range(n):
            for idx in range(3):
                to_owner(a, idx).start()
        for a in range(n):
            def keep(sl, a=a):
                got[a][j_me, sl, :] = snd[a][j_me, sl, :]

            rows_loop(a, keep)

    def totals():
        for a in range(n):
            for idx, chip in enumerate(chips):
                slot = got[a].at[2 * chip[0] + chip[1]]
                remote(a, 4 + idx, slot, slot, me).wait_recv()

            def total(sl, a=a):
                mine = own[a][j_me, sl, :]
                term = [jnp.where(j_me == jj, mine, got[a][jj, sl, :].astype(F32)) for jj in range(N_CHIPS)]
                fin[a][sl, :] = ((term[0] + term[1]) + term[2]) + term[3]

            rows_loop(a, total)
            store(a).start()
            result_to_sibling(a).start()

    def finish():
        for a in range(n):
            remote(a, 7, outs[a].at[1 - c], outs[a].at[1 - c], me).wait_recv()
        for a in range(n):
            for jj in range(N_CHIPS):
                to_sibling(a, jj).wait_send()
            for idx in range(3):
                to_owner(a, idx).wait_send()
            result_to_sibling(a).wait_send()
            store(a).wait()

    return exchange, chip_sums, totals, finish


def _reduce_scatter(parts, name, wire=F32):
    n = len(parts)

    def body(*refs):
        for step in _rs_steps(parts, wire, refs[:n], refs[n:2 * n], refs[2 * n:]):
            step()

    out_shape, scratch = _rs_shapes(parts, wire)
    any_spec = pl.BlockSpec(memory_space=pl.ANY)
    return pl.pallas_call(
        body, name=name, out_shape=tuple(out_shape),
        in_specs=[any_spec] * n, out_specs=tuple([any_spec] * n), scratch_shapes=scratch,
        compiler_params=pltpu.CompilerParams(vmem_limit_bytes=VMEM_LIMIT_BYTES),
    )(*[p[0] for p in parts])


def _rms(x):
    r = lax.rsqrt(jnp.mean(x * x, axis=-1, keepdims=True) + EPS)
    return x * r, r


def _rms_bwd(dxn, xn, r):
    return r * (dxn - xn * jnp.mean(dxn * xn, axis=-1, keepdims=True))


def _in_proj_gather(x2d, norm_g, w_in_sh, shards, tb):
    t = x2d.shape[0]
    nb = t // tb
    cols = IN_COLS // N_CHIPS
    half = D_MODEL // 2
    n = len(shards)

    def body(x_ref, g_ref, win_ref, *refs):
        ins = refs[:n]
        z_ref, wfull_ref = refs[n], refs[n + 1]
        outs = refs[n + 2:2 * n + 2]
        wv, send_sems, recv_sems, local_sems, w_send, w_recv, w_local = refs[2 * n + 2:]
        s, i = pl.program_id(0), pl.program_id(1)
        x, y, c = _place()
        me, sibling = (x, y, c), (x, y, 1 - c)
        chips = [(x, 1 - y), (1 - x, y), (1 - x, 1 - y)]

        def w_half(cx, cy, hc):
            return wv.at[2 * cx + cy, pl.ds(hc * half, half), :]

        def w_remote(sem, block, to, src=None):
            dst = w_half(*block)
            return pltpu.make_async_remote_copy(
                src_ref=dst if src is None else src, dst_ref=dst, send_sem=w_send.at[sem],
                recv_sem=w_recv.at[sem], device_id=to, device_id_type=MESH)

        def w_first(idx):
            return w_remote(idx, (x, y, c), (*chips[idx], c), src=win_ref.at[pl.ds(c * half, half), :])

        def w_relay():
            src_chip = (jnp.bitwise_xor(x, 1 - c), jnp.bitwise_xor(y, c))
            dst_chip = (jnp.bitwise_xor(x, c), jnp.bitwise_xor(y, 1 - c))
            return w_remote(2, (*src_chip, c), (*dst_chip, c))

        def w_pass(idx):
            return w_remote(3 + idx, (*chips[idx], c), sibling)

        def w_store(k, cx, cy):
            jj = 2 * cx + cy
            return pltpu.make_async_copy(wv.at[jj], wfull_ref.at[:, pl.ds(jj * cols, cols)], w_local.at[k])

        start_rest, relay_rest, finish_rest = _gather_steps(shards, ins, outs, send_sems, recv_sems, local_sems)
        own = pltpu.make_async_copy(win_ref, wv.at[2 * x + y], w_local.at[4])

        @pl.when((s == 0) & (i == 0))
        def _():
            own.start()
            w_first(0).start()
            w_first(1).start()
            start_rest()
            own.wait()
            w_store(0, x, y).start()

        @pl.when((s == 1) & (i == 0))
        def _():
            w_remote(0, (*chips[0], c), me).wait_recv()
            w_remote(1, (*chips[1], c), me).wait_recv()
            w_relay().start()
            w_pass(0).start()
            w_pass(1).start()
            w_remote(3, (*chips[0], 1 - c), me).wait_recv()
            w_store(1, *chips[0]).start()

        @pl.when((s == 2) & (i == 0))
        def _():
            w_remote(4, (*chips[1], 1 - c), me).wait_recv()
            w_store(2, *chips[1]).start()

        @pl.when((s == 3) & (i == 0))
        def _():
            w_remote(2, (*chips[2], c), me).wait_recv()
            w_pass(2).start()
            w_remote(5, (*chips[2], 1 - c), me).wait_recv()
            w_store(3, *chips[2]).start()

        xn, _ = _rms(x_ref[...])
        z_ref[...] = _dot((xn * g_ref[...]).astype(BF16), wv[jnp.bitwise_xor(2 * x + y, s)])

        @pl.when((s == N_CHIPS - 1) & (i == nb - 1))
        def _():
            relay_rest()
            finish_rest()
            for cp in (w_first(0), w_first(1), w_relay(), w_pass(0), w_pass(1), w_pass(2)):
                cp.wait_send()
            w_store(0, x, y).wait()
            for idx in range(3):
                w_store(idx + 1, *chips[idx]).wait()

    rest_shape, rest_sems = _gather_shapes(shards)
    out_shape = [jax.ShapeDtypeStruct((t, IN_COLS), F32), jax.ShapeDtypeStruct((D_MODEL, IN_COLS), BF16)] + rest_shape
    any_spec = pl.BlockSpec(memory_space=pl.ANY)

    def z_map(s, i):
        return (i, jnp.bitwise_xor(2 * lax.axis_index("x") + lax.axis_index("y"), s))

    return pl.pallas_call(
        body, name="in_proj", out_shape=tuple(out_shape),
        grid=(N_CHIPS, nb),
        in_specs=[pl.BlockSpec((tb, D_MODEL), lambda s, i: (i, 0)),
                  pl.BlockSpec((1, D_MODEL), lambda s, i: (0, 0)), any_spec] + [any_spec] * n,
        out_specs=tuple([pl.BlockSpec((tb, cols), z_map), any_spec] + [any_spec] * n),
        scratch_shapes=[pltpu.VMEM((N_CHIPS, D_MODEL, cols), BF16)] + rest_sems + [
            pltpu.SemaphoreType.DMA((GATHER_SEMS,)), pltpu.SemaphoreType.DMA((GATHER_SEMS,)),
            pltpu.SemaphoreType.DMA((N_CHIPS + 1,))],
        compiler_params=pltpu.CompilerParams(dimension_semantics=("arbitrary", "arbitrary"),
                                             vmem_limit_bytes=VMEM_LIMIT_BYTES),
    )(x2d, norm_g, w_in_sh, *[sh[0] for sh in shards])


def _in_proj_bwd(dz, w_in, x2d, dx_res, norm_g, tb):
    t = x2d.shape[0]

    def body(dz_ref, w_ref, x_ref, dres_ref, g_ref, dx_ref, h_ref, dg_ref):
        @pl.when(pl.program_id(0) == 0)
        def _():
            dg_ref[...] = jnp.zeros_like(dg_ref)

        xn, r = _rms(x_ref[...])
        g = g_ref[...]
        h_ref[...] = (xn * g).astype(BF16)
        dh = _dot_nt(dz_ref[...], w_ref[...])
        dg_ref[...] += jnp.sum(dh * xn, axis=0, keepdims=True)
        dx_ref[...] = dres_ref[...] + _rms_bwd(dh * g, xn, r)

    row = lambda i: (i, 0)
    fixed = lambda i: (0, 0)
    return pl.pallas_call(
        body, name="in_proj_bwd",
        out_shape=(jax.ShapeDtypeStruct((t, D_MODEL), F32), jax.ShapeDtypeStruct((t, D_MODEL), BF16),
                   jax.ShapeDtypeStruct((1, D_MODEL), F32)),
        grid=(t // tb,),
        in_specs=[pl.BlockSpec((tb, IN_COLS), row),
                  pl.BlockSpec((D_MODEL, IN_COLS), fixed, pipeline_mode=pl.Buffered(1)),
                  pl.BlockSpec((tb, D_MODEL), row), pl.BlockSpec((tb, D_MODEL), row),
                  pl.BlockSpec((1, D_MODEL), fixed)],
        out_specs=(pl.BlockSpec((tb, D_MODEL), row), pl.BlockSpec((tb, D_MODEL), row),
                   pl.BlockSpec((1, D_MODEL), fixed)),
        compiler_params=pltpu.CompilerParams(dimension_semantics=("arbitrary",),
                                             vmem_limit_bytes=VMEM_LIMIT_BYTES),
    )(dz, w_in, x2d, dx_res, norm_g)


def _weight_grad(lhs, rhs, n_chunks, tb, name, reduce=None):
    t, k = lhs.shape
    nc = rhs.shape[1] // n_chunks
    nb = t // tb
    parts, wire, steps = reduce if reduce is not None else ([], F32, ())
    n = len(parts)

    def body(l_ref, r_ref, *refs):
        o_ref = refs[n]
        if n:
            at = pl.program_id(0) * nb + pl.program_id(1)
            for step, when in zip(_rs_steps(parts, wire, refs[:n], refs[n + 1:2 * n + 1], refs[2 * n + 1:]), steps):
                pl.when(at == when)(step)

        @pl.when(pl.program_id(1) == 0)
        def _():
            o_ref[...] = jnp.zeros_like(o_ref)

        o_ref[...] += _dot_tn(l_ref[...], r_ref[...])

    rs_shape, rs_scratch = _rs_shapes(parts, wire) if n else ([], [])
    any_spec = pl.BlockSpec(memory_space=pl.ANY)
    return pl.pallas_call(
        body, name=name, out_shape=tuple([jax.ShapeDtypeStruct((n_chunks, k, nc), F32)] + rs_shape),
        grid=(n_chunks, nb),
        in_specs=[pl.BlockSpec((tb, k), lambda j, i: (i, 0)), pl.BlockSpec((tb, nc), lambda j, i: (i, j))]
        + [any_spec] * n,
        out_specs=tuple([pl.BlockSpec((None, k, nc), lambda j, i: (j, 0, 0))] + [any_spec] * n),
        scratch_shapes=rs_scratch,
        compiler_params=pltpu.CompilerParams(dimension_semantics=("arbitrary", "arbitrary"),
                                             vmem_limit_bytes=VMEM_LIMIT_BYTES),
    )(lhs, rhs, *[p[0] for p in parts])


def _adamw(w, g, m, v, rows, name):
    r, c = w.shape

    def body(w_ref, g_ref, m_ref, v_ref, d_ref, nm_ref, nv_ref):
        g_ = g_ref[...]
        m_ = ADAM_B1 * m_ref[...] + (1.0 - ADAM_B1) * g_
        v_ = ADAM_B2 * v_ref[...] + (1.0 - ADAM_B2) * jnp.square(g_)
        m_hat = m_ / (1.0 - ADAM_B1 ** ADAM_STEP)
        v_hat = v_ / (1.0 - ADAM_B2 ** ADAM_STEP)
        d_ref[...] = -ADAM_LR * (m_hat / (jnp.sqrt(v_hat) + ADAM_EPS) + ADAM_WD * w_ref[...])
        nm_ref[...] = m_
        nv_ref[...] = v_

    spec = pl.BlockSpec((rows, c), lambda i: (i, 0))
    return pl.pallas_call(
        body, name=name, out_shape=tuple(jax.ShapeDtypeStruct((r, c), F32) for _ in range(3)),
        grid=(r // rows,), in_specs=[spec] * 4, out_specs=(spec,) * 3,
        compiler_params=pltpu.CompilerParams(dimension_semantics=("arbitrary",),
                                             vmem_limit_bytes=VMEM_LIMIT_BYTES),
    )(w, g, m, v)


def _shift_down(ext, s):
    return pltpu.roll(ext, s, 0)


def _shift_up(ext, s):
    return pltpu.roll(ext, ext.shape[0] - s, 0)


def _lru_gates(xc, wa_ref, ba, wx_ref, bx, lam):
    pa, px = [], []
    for h in range(LRU_HEADS):
        xh = xc[:, h * HEAD_DIM:(h + 1) * HEAD_DIM].astype(BF16)
        pa.append(_dot(xh, wa_ref[h]))
        px.append(_dot(xh, wx_ref[h]))
    r = _sigmoid(jnp.concatenate(pa, axis=1) + ba)
    ig = _sigmoid(jnp.concatenate(px, axis=1) + bx)
    sp = _softplus(-lam)
    log_a = (-LRU_C * r) * sp
    a = jnp.exp(log_a)
    mult = jnp.sqrt(jnp.tanh(-log_a) * (1.0 + a * a))
    return r, ig, a, mult, sp


def _conv(ext, w_ref, b):
    y = b + _shift_down(ext, 3) * w_ref[0:1, :]
    y = y + _shift_down(ext, 2) * w_ref[1:2, :]
    y = y + _shift_down(ext, 1) * w_ref[2:3, :]
    y = y + ext * w_ref[3:4, :]
    return y[CONV_HIST:, :]


def _pool_diff(ext, pos):
    out = []
    for g, k in enumerate(POOL_WINDOWS):
        col = ext[:, g * POOL_GROUP_DIM:(g + 1) * POOL_GROUP_DIM]
        s = col
        for step in range(g + 1):
            s = s + _shift_down(s, 2 ** step)
        count = jnp.minimum(pos + 1, k).astype(F32)
        out.append(s[POOL_HIST:, :] / count - col[POOL_HIST:, :])
    return out


def _pool_mix(diff, pw_ref):
    return jnp.concatenate([_dot(diff[g].astype(BF16), pw_ref[g]) for g in range(len(POOL_WINDOWS))], axis=1)


def _branch_specs(tb, row_map, fixed):
    fixed3 = lambda i: (0, 0, 0)
    return [pl.BlockSpec((CONV_WIDTH, D_MODEL), fixed), pl.BlockSpec((1, D_MODEL), fixed),
            pl.BlockSpec((LRU_HEADS, HEAD_DIM, HEAD_DIM), fixed3), pl.BlockSpec((1, D_MODEL), fixed),
            pl.BlockSpec((LRU_HEADS, HEAD_DIM, HEAD_DIM), fixed3), pl.BlockSpec((1, D_MODEL), fixed),
            pl.BlockSpec((1, D_MODEL), fixed),
            pl.BlockSpec((len(POOL_WINDOWS), POOL_GROUP_DIM, POOL_GROUP_DIM), fixed3),
            pl.BlockSpec((1, POOL_WIDTH), fixed)]


def _branches_fwd(z, weights, seq, tb, shards):
    t = z.shape[0]
    nb = t // tb
    nbe = seq // tb
    groups = tb // F32_SUBLANES
    n = len(shards)

    def body(xa_ref, ga_ref, xb_ref, gb_ref, cw_ref, cb_ref, wa_ref, ba_ref, wx_ref, bx_ref, lam_ref,
             pw_ref, ps_ref, *refs):
        g_ins = refs[:n]
        ya_ref, yb_ref, hl_ref = refs[n:n + 3]
        g_outs = refs[n + 3:2 * n + 3]
        xa_ext, xb_ext, carry, a_s, u_s, send_sems, recv_sems, local_sems = refs[2 * n + 3:]
        blk = pl.program_id(0) % nbe
        start_gather, relay_gather, finish_gather = _gather_steps(shards, g_ins, g_outs, send_sems, recv_sems,
                                                                  local_sems)
        pl.when(pl.program_id(0) == 0)(start_gather)
        pl.when(pl.program_id(0) == nb // 2)(relay_gather)

        @pl.when(blk == 0)
        def _():
            xa_ext[0:CONV_HIST, :] = jnp.zeros((CONV_HIST, D_MODEL), F32)
            xb_ext[0:POOL_HIST, :] = jnp.zeros((POOL_HIST, POOL_WIDTH), F32)
            carry[...] = jnp.zeros_like(carry)

        xa_ext[CONV_HIST:, :] = xa_ref[...]
        xb_ext[POOL_HIST:, :] = xb_ref[...]
        ea = xa_ext[...]
        eb = xb_ext[...]
        xa_ext[0:CONV_HIST, :] = ea[tb:, :]
        xb_ext[0:POOL_HIST, :] = eb[tb:, :]

        xc = _conv(ea, cw_ref, cb_ref[...])
        _, ig, a, mult, _ = _lru_gates(xc, wa_ref, ba_ref[...], wx_ref, bx_ref[...], lam_ref[...])
        u = mult * (ig * xc)
        row8 = lax.broadcasted_iota(jnp.int32, (tb, D_MODEL), 0) % F32_SUBLANES
        for s in (1, 2, 4):
            m = row8 >= s
            u = jnp.where(m, a * _shift_down(u, s) + u, u)
            a = jnp.where(m, a * _shift_down(a, s), a)
        a_s[...] = a
        u_s[...] = u

        def step(g, cr):
            sl = pl.ds(pl.multiple_of(g * F32_SUBLANES, F32_SUBLANES), F32_SUBLANES)
            hb = a_s[sl, :] * cr + u_s[sl, :]
            hl_ref[sl, :] = hb
            return jnp.broadcast_to(hb[F32_SUBLANES - 1:F32_SUBLANES, :], (F32_SUBLANES, D_MODEL))

        carry[...] = lax.fori_loop(0, groups, step, carry[...], unroll=4)
        ga = ga_ref[...]
        ya_ref[...] = (hl_ref[...] * (ga * _sigmoid(ga))).astype(BF16)

        pos = blk * tb + lax.broadcasted_iota(jnp.int32, (tb, POOL_GROUP_DIM), 0)
        ypre = _pool_mix(_pool_diff(eb, pos), pw_ref)
        gb = gb_ref[...]
        yb_ref[...] = ((ypre * ps_ref[...]) * (gb * _sigmoid(gb))).astype(BF16)
        pl.when(pl.program_id(0) == nb - 1)(finish_gather)

    row = lambda i: (i, 0)
    fixed = lambda i: (0, 0)
    any_spec = pl.BlockSpec(memory_space=pl.ANY)
    in_specs = [pl.BlockSpec((tb, D_MODEL), lambda i: (i, 0)), pl.BlockSpec((tb, D_MODEL), lambda i: (i, 1)),
                pl.BlockSpec((tb, POOL_WIDTH), lambda i: (i, 4)), pl.BlockSpec((tb, POOL_WIDTH), lambda i: (i, 5)),
                ] + _branch_specs(tb, row, fixed) + [any_spec] * n
    g_shape, g_sems = _gather_shapes(shards)
    return pl.pallas_call(
        body, name="branches_fwd",
        out_shape=tuple([jax.ShapeDtypeStruct((t, D_MODEL), BF16), jax.ShapeDtypeStruct((t, POOL_WIDTH), BF16),
                         jax.ShapeDtypeStruct((t, D_MODEL), F32)] + g_shape),
        grid=(nb,), in_specs=in_specs,
        out_specs=tuple([pl.BlockSpec((tb, D_MODEL), row), pl.BlockSpec((tb, POOL_WIDTH), row),
                         pl.BlockSpec((tb, D_MODEL), row)] + [any_spec] * n),
        scratch_shapes=[pltpu.VMEM((tb + CONV_HIST, D_MODEL), F32), pltpu.VMEM((tb + POOL_HIST, POOL_WIDTH), F32),
                        pltpu.VMEM((F32_SUBLANES, D_MODEL), F32),
                        pltpu.VMEM((tb, D_MODEL), F32), pltpu.VMEM((tb, D_MODEL), F32)] + g_sems,
        compiler_params=pltpu.CompilerParams(dimension_semantics=("arbitrary",),
                                             vmem_limit_bytes=VMEM_LIMIT_BYTES),
    )(z, z, z, z, *weights, *[sh[0] for sh in shards])


def _branches_bwd(z, hl, dya, dyb, dzm, weights, seq, tb):
    t = z.shape[0]
    nb = t // tb
    nbe = seq // tb
    groups = tb // F32_SUBLANES
    n_pool = len(POOL_WINDOWS)

    def body(xa_ref, xap_ref, ga_ref, xb_ref, xbp_ref, gb_ref, hl_ref, hlp_ref, dya_ref, dyb_ref, dzm_ref,
             cw_ref, cb_ref, wa_ref, ba_ref, wx_ref, bx_ref, lam_ref, pw_ref, ps_ref,
             dz_ref, dcw_ref, dcb_ref, dwa_ref, dba_ref, dwx_ref, dbx_ref, dlam_ref, dpw_ref, dps_ref,
             xa_ext, xb_ext, hl_ext, a_ext, dxc_ext, dwin_ext, g_carry, b_s, d_s, g_s):
        i = pl.program_id(0)
        blk = (nb - 1 - i) % nbe

        @pl.when(i == 0)
        def _():
            for ref in (dcw_ref, dcb_ref, dwa_ref, dba_ref, dwx_ref, dbx_ref, dlam_ref, dpw_ref, dps_ref):
                ref[...] = jnp.zeros_like(ref)

        @pl.when(blk == nbe - 1)
        def _():
            a_ext[tb:, :] = jnp.zeros((F32_SUBLANES, D_MODEL), F32)
            dxc_ext[tb:, :] = jnp.zeros((CONV_HIST, D_MODEL), F32)
            dwin_ext[tb:, :] = jnp.zeros((POOL_HIST, POOL_WIDTH), F32)
            g_carry[...] = jnp.zeros_like(g_carry)

        live = (blk > 0).astype(F32)
        xa_ext[0:CONV_HIST, :] = xap_ref[...] * live
        xa_ext[CONV_HIST:, :] = xa_ref[...]
        xb_ext[0:POOL_HIST, :] = xbp_ref[...] * live
        xb_ext[POOL_HIST:, :] = xb_ref[...]
        hl_ext[0:F32_SUBLANES, :] = hlp_ref[...] * live
        hl_ext[F32_SUBLANES:, :] = hl_ref[...]
        ea = xa_ext[...]
        eb = xb_ext[...]

        xc = _conv(ea, cw_ref, cb_ref[...])
        lam = lam_ref[...]
        r, ig, a, mult, sp = _lru_gates(xc, wa_ref, ba_ref[...], wx_ref, bx_ref[...], lam)
        hl = hl_ref[...]
        ga = ga_ref[...]
        sga = _sigmoid(ga)
        dya = dya_ref[...]
        dhl = dya * (ga * sga)
        dz_ref[:, D_MODEL:2 * D_MODEL] = (dya * hl * (sga * (1.0 + ga * (1.0 - sga)))).astype(BF16)

        a_ext[0:tb, :] = a
        b = _shift_up(a_ext[...], 1)[0:tb, :]
        a_ext[tb:, :] = jnp.broadcast_to(a[0:1, :], (F32_SUBLANES, D_MODEL))
        d = dhl
        row8 = lax.broadcasted_iota(jnp.int32, (tb, D_MODEL), 0) % F32_SUBLANES
        for s in (1, 2, 4):
            m = row8 < F32_SUBLANES - s
            d = jnp.where(m, d + b * _shift_up(d, s), d)
            b = jnp.where(m, b * _shift_up(b, s), b)
        b_s[...] = b
        d_s[...] = d

        def step(k, cr):
            sl = pl.ds(pl.multiple_of((groups - 1 - k) * F32_SUBLANES, F32_SUBLANES), F32_SUBLANES)
            gb_ = d_s[sl, :] + b_s[sl, :] * cr
            g_s[sl, :] = gb_
            return jnp.broadcast_to(gb_[0:1, :], (F32_SUBLANES, D_MODEL))

        g_carry[...] = lax.fori_loop(0, groups, step, g_carry[...], unroll=4)
        gsc = g_s[...]
        da = gsc * _shift_down(hl_ext[...], 1)[F32_SUBLANES:, :]
        dmult = gsc * (ig * xc)
        dig = gsc * (mult * xc)
        dxc = gsc * (mult * ig)
        dlog_a = da * a - (a * a) * dmult / mult
        dr = dlog_a * (-LRU_C * sp)
        dlam_ref[...] += jnp.sum(dlog_a * (-LRU_C * r), axis=0, keepdims=True)
        dpa = dr * (r * (1.0 - r))
        dpx = dig * (ig * (1.0 - ig))
        dba_ref[...] += jnp.sum(dpa, axis=0, keepdims=True)
        dbx_ref[...] += jnp.sum(dpx, axis=0, keepdims=True)
        back = []
        for h in range(LRU_HEADS):
            cols = slice(h * HEAD_DIM, (h + 1) * HEAD_DIM)
            xh = xc[:, cols].astype(BF16)
            dpa_h = dpa[:, cols].astype(BF16)
            dpx_h = dpx[:, cols].astype(BF16)
            dwa_ref[h] += _dot_tn(xh, dpa_h)
            dwx_ref[h] += _dot_tn(xh, dpx_h)
            back.append(_dot_nt(dpa_h, wa_ref[h]) + _dot_nt(dpx_h, wx_ref[h]))
        dxc = dxc + jnp.concatenate(back, axis=1)
        dcb_ref[...] += jnp.sum(dxc, axis=0, keepdims=True)
        for k in range(CONV_WIDTH):
            tap = _shift_down(ea, CONV_WIDTH - 1 - k)[CONV_HIST:, :] if k < CONV_WIDTH - 1 else ea[CONV_HIST:, :]
            dcw_ref[k:k + 1, :] += jnp.sum(dxc * tap, axis=0, keepdims=True)
        dxc_ext[0:tb, :] = dxc
        ed = dxc_ext[...]
        dxa = ed * cw_ref[3:4, :]
        dxa = dxa + _shift_up(ed, 1) * cw_ref[2:3, :]
        dxa = dxa + _shift_up(ed, 2) * cw_ref[1:2, :]
        dxa = dxa + _shift_up(ed, 3) * cw_ref[0:1, :]
        dz_ref[:, 0:D_MODEL] = dxa[0:tb, :].astype(BF16)
        dxc_ext[tb:, :] = dxc[0:CONV_HIST, :]

        pos = blk * tb + lax.broadcasted_iota(jnp.int32, (tb, POOL_GROUP_DIM), 0)
        diff = _pool_diff(eb, pos)
        ypre = _pool_mix(diff, pw_ref)
        ps = ps_ref[...]
        gb = gb_ref[...]
        sgb = _sigmoid(gb)
        dyb = dyb_ref[...]
        dyp = dyb * (gb * sgb)
        dz_ref[:, 2 * D_MODEL + POOL_WIDTH:3 * D_MODEL] = (
            dyb * (ypre * ps) * (sgb * (1.0 + gb * (1.0 - sgb)))).astype(BF16)
        dps_ref[...] += jnp.sum(dyp * ypre, axis=0, keepdims=True)
        dypre = dyp * ps
        for g, k in enumerate(POOL_WINDOWS):
            cols = slice(g * POOL_GROUP_DIM, (g + 1) * POOL_GROUP_DIM)
            dyg = dypre[:, cols].astype(BF16)
            dpw_ref[g] += _dot_tn(diff[g].astype(BF16), dyg)
            ddiff = _dot_nt(dyg, pw_ref[g])
            count = jnp.minimum(pos + 1, k).astype(F32)
            dwin = ddiff / count
            dwin_ext[0:tb, cols] = dwin
            s = dwin_ext[:, cols]
            for step_ in range(g + 1):
                s = s + _shift_up(s, 2 ** step_)
            dz_ref[:, 2 * D_MODEL + g * POOL_GROUP_DIM:2 * D_MODEL + (g + 1) * POOL_GROUP_DIM] = (
                s[0:tb, :] - ddiff).astype(BF16)
            dwin_ext[tb:, cols] = dwin[0:POOL_HIST, :]

        dz_ref[:, 3 * D_MODEL:] = dzm_ref[...]

        @pl.when(i == nb - 1)
        def _():
            dlam_ref[...] = dlam_ref[...] * (-_sigmoid(-lam))

    rev = lambda i: (nb - 1 - i, 0)
    fixed = lambda i: (0, 0)
    fixed3 = lambda i: (0, 0, 0)

    def prev(rows, col):
        per = tb // rows
        return lambda i: (jnp.maximum((nb - 1 - i) * per - 1, 0), col)

    in_specs = [pl.BlockSpec((tb, D_MODEL), lambda i: (nb - 1 - i, 0)),
                pl.BlockSpec((CONV_HIST, D_MODEL), prev(CONV_HIST, 0)),
                pl.BlockSpec((tb, D_MODEL), lambda i: (nb - 1 - i, 1)),
                pl.BlockSpec((tb, POOL_WIDTH), lambda i: (nb - 1 - i, 4)),
                pl.BlockSpec((POOL_HIST, POOL_WIDTH), prev(POOL_HIST, 4)),
                pl.BlockSpec((tb, POOL_WIDTH), lambda i: (nb - 1 - i, 5)),
                pl.BlockSpec((tb, D_MODEL), rev),
                pl.BlockSpec((F32_SUBLANES, D_MODEL), prev(F32_SUBLANES, 0)),
                pl.BlockSpec((tb, D_MODEL), rev), pl.BlockSpec((tb, POOL_WIDTH), rev),
                pl.BlockSpec((tb, 2 * D_MODEL), rev)] + _branch_specs(tb, rev, fixed)
    out_shape = (jax.ShapeDtypeStruct((t, IN_COLS), BF16),
                 jax.ShapeDtypeStruct((CONV_WIDTH, D_MODEL), F32), jax.ShapeDtypeStruct((1, D_MODEL), F32),
                 jax.ShapeDtypeStruct((LRU_HEADS, HEAD_DIM, HEAD_DIM), F32), jax.ShapeDtypeStruct((1, D_MODEL), F32),
                 jax.ShapeDtypeStruct((LRU_HEADS, HEAD_DIM, HEAD_DIM), F32), jax.ShapeDtypeStruct((1, D_MODEL), F32),
                 jax.ShapeDtypeStruct((1, D_MODEL), F32),
                 jax.ShapeDtypeStruct((n_pool, POOL_GROUP_DIM, POOL_GROUP_DIM), F32),
                 jax.ShapeDtypeStruct((1, POOL_WIDTH), F32))
    out_specs = (pl.BlockSpec((tb, IN_COLS), rev),
                 pl.BlockSpec((CONV_WIDTH, D_MODEL), fixed), pl.BlockSpec((1, D_MODEL), fixed),
                 pl.BlockSpec((LRU_HEADS, HEAD_DIM, HEAD_DIM), fixed3), pl.BlockSpec((1, D_MODEL), fixed),
                 pl.BlockSpec((LRU_HEADS, HEAD_DIM, HEAD_DIM), fixed3), pl.BlockSpec((1, D_MODEL), fixed),
                 pl.BlockSpec((1, D_MODEL), fixed),
                 pl.BlockSpec((n_pool, POOL_GROUP_DIM, POOL_GROUP_DIM), fixed3),
                 pl.BlockSpec((1, POOL_WIDTH), fixed))
    scratch = [pltpu.VMEM((tb + CONV_HIST, D_MODEL), F32), pltpu.VMEM((tb + POOL_HIST, POOL_WIDTH), F32),
               pltpu.VMEM((tb + F32_SUBLANES, D_MODEL), F32), pltpu.VMEM((tb + F32_SUBLANES, D_MODEL), F32),
               pltpu.VMEM((tb + CONV_HIST, D_MODEL), F32), pltpu.VMEM((tb + POOL_HIST, POOL_WIDTH), F32),
               pltpu.VMEM((F32_SUBLANES, D_MODEL), F32),
               pltpu.VMEM((tb, D_MODEL), F32), pltpu.VMEM((tb, D_MODEL), F32), pltpu.VMEM((tb, D_MODEL), F32)]
    return pl.pallas_call(
        body, name="branches_bwd", out_shape=out_shape, grid=(nb,), in_specs=in_specs, out_specs=out_specs,
        scratch_shapes=scratch,
        compiler_params=pltpu.CompilerParams(dimension_semantics=("arbitrary",),
                                             vmem_limit_bytes=VMEM_LIMIT_BYTES),
    )(z, z, z, z, z, z, hl, hl, dya, dyb, dzm, *weights)


def _merge_head(x2d, ya, yb, z, p2d, tgt, w_pl, w_pp, w_out, w_pg, w_pe, g2, gf, tb):
    t = x2d.shape[0]
    p_dim = p2d.shape[1]

    def body(x_ref, ya_ref, yb_ref, ma_ref, mb_ref, p_ref, t_ref, wpl_ref, wpp_ref, wout_ref, wpg_ref, wpe_ref,
             g2_ref, gf_ref,
             loss_ref, dg2_ref, dgf_ref, dxr_ref, dya_ref, dyb_ref, dzm_ref,
             mg_ref, do_ref, hn_ref, dgp_ref, dpe_ref, da_ref, dbm_ref, pbf_ref):
        @pl.when(pl.program_id(0) == 0)
        def _():
            loss_ref[...] = jnp.zeros_like(loss_ref)
            dg2_ref[...] = jnp.zeros_like(dg2_ref)
            dgf_ref[...] = jnp.zeros_like(dgf_ref)

        a_ = _dot(ya_ref[...], wpl_ref[...])
        bm = _dot(yb_ref[...], wpp_ref[...])
        sa = _sigmoid(ma_ref[...])
        sb = _sigmoid(mb_ref[...])
        mg = (sa * a_ + sb * bm).astype(BF16)
        mg_ref[...] = mg
        x1 = x_ref[...] + _dot(mg, wout_ref[...])
        xn2, r2 = _rms(x1)
        g2 = g2_ref[...]
        hn = (xn2 * g2).astype(BF16)
        hn_ref[...] = hn
        gate = _sigmoid(_dot(hn, wpg_ref[...]))
        pbf = p_ref[...].astype(BF16)
        pbf_ref[...] = pbf
        pe = _dot(pbf, wpe_ref[...])
        x2 = x1 + gate * pe
        xn3, r3 = _rms(x2)
        gf = gf_ref[...]
        err = xn3 * gf - t_ref[...]
        loss_ref[...] += 0.5 * jnp.sum(jnp.mean(err * err, axis=-1))

        dy = err * (1.0 / D_MODEL)
        dgf_ref[...] += jnp.sum(dy * xn3, axis=0, keepdims=True)
        dx2 = _rms_bwd(dy * gf, xn3, r3)
        dpe_ref[...] = (dx2 * gate).astype(BF16)
        dgp = ((dx2 * pe) * (gate * (1.0 - gate))).astype(BF16)
        dgp_ref[...] = dgp
        dhn = _dot_nt(dgp, wpg_ref[...])
        dg2_ref[...] += jnp.sum(dhn * xn2, axis=0, keepdims=True)
        dx1 = dx2 + _rms_bwd(dhn * g2, xn2, r2)
        dxr_ref[...] = dx1
        do = dx1.astype(BF16)
        do_ref[...] = do
        dmg = _dot_nt(do, wout_ref[...])
        da = (dmg * sa).astype(BF16)
        dbm = (dmg * sb).astype(BF16)
        da_ref[...] = da
        dbm_ref[...] = dbm
        dzm_ref[:, 0:D_MODEL] = (dmg * a_ * (sa * (1.0 - sa))).astype(BF16)
        dzm_ref[:, D_MODEL:] = (dmg * bm * (sb * (1.0 - sb))).astype(BF16)
        dya_ref[...] = _dot_nt(da, wpl_ref[...])
        dyb_ref[...] = _dot_nt(dbm, wpp_ref[...])

    row = lambda i: (i, 0)
    fixed = lambda i: (0, 0)

    def resident(shape):
        return pl.BlockSpec(shape, fixed, pipeline_mode=pl.Buffered(1))

    tok = lambda width: pl.BlockSpec((tb, width), row)
    in_specs = [tok(D_MODEL), tok(D_MODEL), tok(POOL_WIDTH),
                pl.BlockSpec((tb, D_MODEL), lambda i: (i, 3)), pl.BlockSpec((tb, D_MODEL), lambda i: (i, 4)),
                tok(p_dim), tok(D_MODEL),
                resident((D_MODEL, D_MODEL)), resident((POOL_WIDTH, D_MODEL)), resident((D_MODEL, D_MODEL)),
                resident((D_MODEL, D_MODEL)), resident((p_dim, D_MODEL)),
                pl.BlockSpec((1, D_MODEL), fixed), pl.BlockSpec((1, D_MODEL), fixed)]
    bf = lambda width: jax.ShapeDtypeStruct((t, width), BF16)
    f32 = lambda width: jax.ShapeDtypeStruct((t, width), F32)
    out_shape = (jax.ShapeDtypeStruct((F32_SUBLANES, 128), F32), jax.ShapeDtypeStruct((1, D_MODEL), F32),
                 jax.ShapeDtypeStruct((1, D_MODEL), F32),
                 f32(D_MODEL), f32(D_MODEL), f32(POOL_WIDTH), bf(2 * D_MODEL),
                 bf(D_MODEL), bf(D_MODEL), bf(D_MODEL), bf(D_MODEL), bf(D_MODEL), bf(D_MODEL), bf(D_MODEL), bf(p_dim))
    out_specs = (pl.BlockSpec((F32_SUBLANES, 128), fixed), pl.BlockSpec((1, D_MODEL), fixed),
                 pl.BlockSpec((1, D_MODEL), fixed),
                 tok(D_MODEL), tok(D_MODEL), tok(POOL_WIDTH), tok(2 * D_MODEL),
                 tok(D_MODEL), tok(D_MODEL), tok(D_MODEL), tok(D_MODEL), tok(D_MODEL), tok(D_MODEL), tok(D_MODEL),
                 tok(p_dim))
    return pl.pallas_call(
        body, name="merge_head", out_shape=out_shape, grid=(t // tb,), in_specs=in_specs, out_specs=out_specs,
        compiler_params=pltpu.CompilerParams(dimension_semantics=("arbitrary",),
                                             vmem_limit_bytes=VMEM_LIMIT_BYTES),
    )(x2d, ya, yb, z, z, p2d, tgt, w_pl, w_pp, w_out, w_pg, w_pe, g2, gf)


def _pad_rows(a, rows):
    return jnp.pad(a, ((0, rows - a.shape[0]), (0, D_MODEL - a.shape[1])))


def _pack_bag(parts, tail=None):
    rows = [_pad_rows(a.reshape(-1, a.shape[-1]) if a.shape[-1] != HEAD_DIM else a.reshape(-1, D_MODEL), n)
            for a, n in zip(parts, BAG_PART_ROWS)]
    spare = BAG_ROWS - sum(BAG_PART_ROWS)
    if tail is not None:
        rows.append(_pad_rows(tail, F32_SUBLANES))
        spare -= F32_SUBLANES
    rows.append(jnp.zeros((spare, D_MODEL), F32))
    return jnp.concatenate(rows, axis=0)


def _unpack_bag(bag, shapes):
    out, at = [], 0
    for shape, n in zip(shapes, BAG_PART_ROWS):
        size = 1
        for s in shape:
            size *= s
        if size % D_MODEL == 0:
            piece = bag[at:at + size // D_MODEL, :]
        else:
            piece = bag[at:at + 1, :size]
        out.append(piece.reshape(shape))
        at += n
    return out


def kernel(x, p, norm_g, w_in, conv_w, conv_b, lru_w_a, lru_b_a, lru_w_x, lru_b_x, lru_lambda, pool_w, pool_scale, w_proj_lru, w_proj_pool, w_out, ple_norm_g, w_ple_gate, w_ple_proj, final_g, loss_target, m_norm_g, m_w_in, m_conv_w, m_conv_b, m_lru_w_a, m_lru_b_a, m_lru_w_x, m_lru_b_x, m_lru_lambda, m_pool_w, m_pool_scale, m_w_proj_lru, m_w_proj_pool, m_w_out, m_ple_norm_g, m_w_ple_gate, m_w_ple_proj, m_final_g, v_norm_g, v_w_in, v_conv_w, v_conv_b, v_lru_w_a, v_lru_b_a, v_lru_w_x, v_lru_b_x, v_lru_lambda, v_pool_w, v_pool_scale, v_w_proj_lru, v_w_proj_pool, v_w_out, v_ple_norm_g, v_w_ple_gate, v_w_ple_proj, v_final_g):
    bsz, seq, _ = x.shape
    t = bsz * seq
    tb_mm = min(512, seq)
    tb_seq = min(256, seq // 2) if seq >= 512 else seq
    x2d = x.reshape(t, D_MODEL)
    p2d = p.reshape(t, p.shape[-1])
    tgt = loss_target.reshape(t, D_MODEL)
    chip = 2 * lax.axis_index("x") + lax.axis_index("y")

    rest = [(w_proj_lru[0], 0), (w_proj_pool[0], 1), (w_out[0], 0), (w_ple_gate[0], 0), (w_ple_proj[0], 1)]
    z, w_in_f, conv_w_f = _in_proj_gather(x2d, norm_g, w_in[0].astype(BF16), [(conv_w[0], 1, False)], tb_mm)

    wa_bf = lru_w_a[0].astype(BF16)
    wx_bf = lru_w_x[0].astype(BF16)
    pw_bf = pool_w[0].astype(BF16)
    branch_w = (conv_w_f, conv_b, wa_bf, lru_b_a.reshape(1, D_MODEL), wx_bf, lru_b_x.reshape(1, D_MODEL),
                lru_lambda, pw_bf, pool_scale)

    ya, yb, hl, w_pl_f, w_pp_f, w_out_f, w_pg_f, w_pe_f = _branches_fwd(
        z, branch_w, seq, tb_seq, [(w.astype(BF16), axis, True) for w, axis in rest])
    (loss_acc, d_g2, d_gf, dx_res, dya, dyb, dzm, mg_bf, do_bf, hn_bf, dgp_bf, dpe_bf, da_bf, dbm_bf, p_bf) = _merge_head(
        x2d, ya, yb, z, p2d, tgt, w_pl_f, w_pp_f, w_out_f, w_pg_f, w_pe_f, ple_norm_g, final_g.reshape(1, D_MODEL),
        tb_seq)
    (dz, d_cw, d_cb, d_wa, d_ba, d_wx, d_bx, d_lam, d_pw, d_ps) = _branches_bwd(
        z, hl, dya, dyb, dzm, branch_w, seq, tb_seq)
    dx, h_bf, d_g1 = _in_proj_bwd(dz, w_in_f, x2d, dx_res, norm_g, tb_mm)

    g_pl = _weight_grad(ya, da_bf, 1, tb_mm, "dw_proj_lru")[0].reshape(8, D_MODEL // 8, D_MODEL)
    g_pp = _weight_grad(yb, dbm_bf, 1, tb_mm, "dw_proj_pool")[0][0]
    g_out = _weight_grad(mg_bf, do_bf, 1, tb_mm, "dw_out")[0].reshape(8, D_MODEL // 8, D_MODEL)
    g_pg = _weight_grad(hn_bf, dgp_bf, 1, tb_mm, "dw_ple_gate")[0].reshape(8, D_MODEL // 8, D_MODEL)
    p_dim = p2d.shape[1]
    g_pe = _weight_grad(p_bf, dpe_bf, 1, tb_mm, "dw_ple_proj")[0][0]
    nb_mm = t // tb_mm
    g_in, r_pl, r_pp, r_out, r_pg, r_pe = _weight_grad(
        h_bf, dz, N_CHIPS, tb_mm, "dw_in",
        reduce=([(g_pl, False), (g_pp, True), (g_out, False), (g_pg, False), (g_pe, True)], BF16,
                (0, nb_mm, 3 * nb_mm + nb_mm // 2, N_CHIPS * nb_mm - 1)))
    g_in = g_in.reshape(8, D_MODEL // 2, IN_COLS // N_CHIPS)

    (r_in,) = _reduce_scatter([(g_in, False)], "rs_w_in", BF16)
    small_shapes = [(1, D_MODEL), (1, CONV_WIDTH, D_MODEL), (1, D_MODEL), lru_w_a.shape, lru_b_a.shape, lru_w_x.shape,
                    lru_b_x.shape, (1, D_MODEL), pool_w.shape, pool_scale.shape, (1, D_MODEL), final_g.shape]
    bag = _pack_bag([d_g1, d_cw, d_cb, d_wa, d_ba.reshape(1, D_MODEL), d_wx, d_bx.reshape(1, D_MODEL), d_lam, d_pw,
                     d_ps, d_g2, d_gf], tail=loss_acc)
    (bag_mine,) = _reduce_scatter([(bag.reshape(8, BAG_ROWS // 8, D_MODEL), False)], "rs_small")
    (bag_sum,) = _gather_shards([(bag_mine.reshape(BAG_ROWS // N_CHIPS, D_MODEL), 0, True)], "gather_small")
    (g_g1, g_cw_full, g_cb, g_wa, g_ba, g_wx, g_bx, g_lam, g_pw, g_ps, g_g2, g_gf) = _unpack_bag(bag_sum, small_shapes)
    cw_cols = D_MODEL // N_CHIPS
    g_cw = lax.dynamic_slice_in_dim(g_cw_full, chip * cw_cols, cw_cols, axis=2)

    def big_update(w, g2d, m, v, rows, name):
        d, nm, nv = _adamw(w[0], g2d, m[0], v[0], rows, name)
        return g2d[None], d[None], nm[None], nv[None]

    u_in = big_update(w_in, r_in.reshape(D_MODEL, IN_COLS // N_CHIPS), m_w_in, v_w_in, 256, "adamw_w_in")
    u_pl = big_update(w_proj_lru, r_pl.reshape(D_MODEL // N_CHIPS, D_MODEL), m_w_proj_lru, v_w_proj_lru, 256, "adamw_w_proj_lru")
    u_pp = big_update(w_proj_pool, r_pp.reshape(POOL_WIDTH, D_MODEL // N_CHIPS), m_w_proj_pool, v_w_proj_pool, 512, "adamw_w_proj_pool")
    u_out = big_update(w_out, r_out.reshape(D_MODEL // N_CHIPS, D_MODEL), m_w_out, v_w_out, 256, "adamw_w_out")
    u_pg = big_update(w_ple_gate, r_pg.reshape(D_MODEL // N_CHIPS, D_MODEL), m_w_ple_gate, v_w_ple_gate, 256, "adamw_w_ple_gate")
    u_pe = big_update(w_ple_proj, r_pe.reshape(p_dim, D_MODEL // N_CHIPS), m_w_ple_proj, v_w_ple_proj, 256, "adamw_w_ple_proj")
    u_cw = big_update(conv_w, g_cw[0], m_conv_w, v_conv_w, CONV_WIDTH, "adamw_conv_w")

    small_w = [norm_g, None, conv_b, lru_w_a, lru_b_a, lru_w_x, lru_b_x, lru_lambda, pool_w, pool_scale, ple_norm_g, final_g]
    small_m = [m_norm_g, None, m_conv_b, m_lru_w_a, m_lru_b_a, m_lru_w_x, m_lru_b_x, m_lru_lambda, m_pool_w, m_pool_scale, m_ple_norm_g, m_final_g]
    small_v = [v_norm_g, None, v_conv_b, v_lru_w_a, v_lru_b_a, v_lru_w_x, v_lru_b_x, v_lru_lambda, v_pool_w, v_pool_scale, v_ple_norm_g, v_final_g]
    fill = jnp.zeros((CONV_WIDTH, D_MODEL), F32)

    def bag_of(arrs):
        return _pack_bag([fill if a is None else (a[0] if a.ndim > 1 else a[None]) for a in arrs])

    d_bag, m_bag, v_bag = _adamw(bag_of(small_w), bag_sum, bag_of(small_m), bag_of(small_v), BAG_ROWS // 8, "adamw_small")
    d_small = _unpack_bag(d_bag, small_shapes)
    m_small = _unpack_bag(m_bag, small_shapes)
    v_small = _unpack_bag(v_bag, small_shapes)

    loss = bag_sum[sum(BAG_PART_ROWS), 0]
    grad_x = dx.reshape(bsz, seq, D_MODEL)

    def ordered(small, pick):
        s = list(small)
        return [s[0], u_in[pick], u_cw[pick], s[2], s[3], s[4], s[5], s[6], s[7], s[8], s[9],
                u_pl[pick], u_pp[pick], u_out[pick], s[10], u_pg[pick], u_pe[pick], s[11]]

    grads = ordered([g_g1, None, g_cb, g_wa, g_ba, g_wx, g_bx, g_lam, g_pw, g_ps, g_g2, g_gf], 0)
    return (loss, grad_x, *grads, *ordered(d_small, 1), *ordered(m_small, 2), *ordered(v_small, 3))
```

```python
import functools

import jax
import jax.numpy as jnp
from jax import lax
from jax.experimental import pallas as pl
from jax.experimental.pallas import tpu as pltpu

F32 = jnp.float32
BF16 = jnp.bfloat16
MESH = pl.DeviceIdType.MESH
ALL_AXES = ("x", "y", "c")

D_MODEL = 1024
LRU_HEADS = 8
HEAD_DIM = 128
CONV_WIDTH = 4
LRU_C = 8.0
POOL_WIDTH = 512
POOL_WINDOWS = (2, 4, 8, 16)
POOL_GROUP_DIM = 128
IN_COLS = 5120
N_CHIPS = 4
EPS = 1e-6

ADAM_LR = 0.001
ADAM_B1 = 0.9
ADAM_B2 = 0.999
ADAM_EPS = 1e-08
ADAM_WD = 0.01
ADAM_STEP = 10

F32_SUBLANES = 8
CONV_HIST = 8
POOL_HIST = 16
VMEM_LIMIT_BYTES = 58 * 1024 * 1024
BAG_PART_ROWS = (8, 8, 8, 128, 8, 128, 8, 8, 64, 8, 8, 8)
BAG_ROWS = 448


def _dot(a, b):
    return jnp.dot(a, b, preferred_element_type=F32)


def _dot_nt(a, b):
    return lax.dot_general(a, b, (((1,), (1,)), ((), ())), preferred_element_type=F32)


def _dot_tn(a, b):
    return lax.dot_general(a, b, (((0,), (0,)), ((), ())), preferred_element_type=F32)


def _sigmoid(v):
    return jax.nn.sigmoid(v)


def _softplus(v):
    return jnp.maximum(v, 0.0) + jnp.log1p(jnp.exp(-jnp.abs(v)))


def _place():
    return lax.axis_index("x"), lax.axis_index("y"), lax.axis_index("c")


GATHER_SEMS = 6


def _gather_shapes(shards):
    out_shape = []
    for arr, axis, _ in shards:
        r, cols = arr.shape
        out_shape.append(jax.ShapeDtypeStruct((N_CHIPS * r, cols) if axis == 0 else (r, N_CHIPS * cols), arr.dtype))
    n = len(shards)
    sems = [pltpu.SemaphoreType.DMA((n * GATHER_SEMS,)), pltpu.SemaphoreType.DMA((n * GATHER_SEMS,)),
            pltpu.SemaphoreType.DMA((n,))]
    return out_shape, sems


def _gather_steps(shards, ins, outs, send_sems, recv_sems, local_sems):
    n = len(shards)
    x, y, c = _place()
    me, sibling = (x, y, c), (x, y, 1 - c)
    chips = [(x, 1 - y), (1 - x, y), (1 - x, 1 - y)]

    def region(k, cx, cy, hc):
        (r, cols), axis = shards[k][0].shape, shards[k][1]
        j = 2 * cx + cy
        if axis == 0:
            if hc is None:
                return outs[k].at[pl.ds(j * r, r), :]
            return outs[k].at[pl.ds(j * r + hc * (r // 2), r // 2), :]
        if hc is None:
            return outs[k].at[:, pl.ds(j * cols, cols)]
        return outs[k].at[pl.ds(hc * (r // 2), r // 2), pl.ds(j * cols, cols)]

    def remote(k, sem, block, to, src=None):
        dst = region(k, *block)
        return pltpu.make_async_remote_copy(
            src_ref=dst if src is None else src, dst_ref=dst,
            send_sem=send_sems.at[k * GATHER_SEMS + sem], recv_sem=recv_sems.at[k * GATHER_SEMS + sem],
            device_id=to, device_id_type=MESH)

    def first(k, idx):
        r, split = shards[k][0].shape[0], shards[k][2]
        src = ins[k].at[pl.ds(c * (r // 2), r // 2), :] if split else ins[k]
        return remote(k, idx, (x, y, c if split else None), (*chips[idx], c), src=src)

    def relay(k):
        src_chip = (jnp.bitwise_xor(x, 1 - c), jnp.bitwise_xor(y, c))
        dst_chip = (jnp.bitwise_xor(x, c), jnp.bitwise_xor(y, 1 - c))
        return remote(k, 2, (*src_chip, c), (*dst_chip, c))

    def passed(k, idx):
        return remote(k, 3 + idx, (*chips[idx], c), sibling)

    def mine(k):
        return pltpu.make_async_copy(ins[k], region(k, x, y, None), local_sems.at[k])

    def start():
        for k in range(n):
            mine(k).start()
            for idx in range(2 if shards[k][2] else 3):
                first(k, idx).start()

    def relay_on():
        for k in range(n):
            split = shards[k][2]
            for idx in range(2):
                remote(k, idx, (*chips[idx], c if split else None), me).wait_recv()
            if split:
                relay(k).start()
                passed(k, 0).start()
                passed(k, 1).start()

    def finish():
        for k in range(n):
            split = shards[k][2]
            remote(k, 2, (*chips[2], c if split else None), me).wait_recv()
            if split:
                passed(k, 2).start()
        for k in range(n):
            if shards[k][2]:
                for idx in range(3):
                    remote(k, 3 + idx, (*chips[idx], 1 - c), me).wait_recv()
        for k in range(n):
            if shards[k][2]:
                for cp in (first(k, 0), first(k, 1), relay(k), passed(k, 0), passed(k, 1), passed(k, 2)):
                    cp.wait_send()
            else:
                for idx in range(3):
                    first(k, idx).wait_send()
            mine(k).wait()

    return start, relay_on, finish


def _gather_shards(shards, name):
    n = len(shards)

    def body(*refs):
        for step in _gather_steps(shards, refs[:n], refs[n:2 * n], *refs[2 * n:]):
            step()

    out_shape, sems = _gather_shapes(shards)
    any_spec = pl.BlockSpec(memory_space=pl.ANY)
    return pl.pallas_call(
        body, name=name, out_shape=tuple(out_shape),
        in_specs=[any_spec] * n, out_specs=tuple([any_spec] * n), scratch_shapes=sems,
    )(*[s[0] for s in shards])


RS_ADD_ROWS = (64, 56, 32, 16, 8)


RS_SEMS = 8
RS_LOCAL_SEMS = 5


def _rs_piece_shape(part):
    arr, cols = part[0], part[1]
    return (arr.shape[0] // 2, arr.shape[1] // N_CHIPS) if cols else tuple(arr.shape[1:])


def _rs_operands(parts):
    return [p[0] for p in parts] + [p[0] if p[2] is None else p[2] for p in parts]


def _rs_shapes(parts, wire):
    n = len(parts)
    shapes = [_rs_piece_shape(p) for p in parts]
    out_shape = [jax.ShapeDtypeStruct((2,) + s, F32) for s in shapes]
    scratch = []
    for lead, dtype in ((N_CHIPS, F32), (N_CHIPS, None), (N_CHIPS, wire), (None, F32), (N_CHIPS, wire)):
        for s, p in zip(shapes, parts):
            narrow = F32 if p[2] is None else p[2].dtype
            scratch.append(pltpu.VMEM(s if lead is None else (lead,) + s, narrow if dtype is None else dtype))
    scratch += [pltpu.SemaphoreType.DMA((n * RS_SEMS,)), pltpu.SemaphoreType.DMA((n * RS_SEMS,)),
                pltpu.SemaphoreType.DMA((n * RS_LOCAL_SEMS,))]
    return out_shape, scratch


def _rs_steps(parts, wire, ins, outs, scratch):
    n = len(parts)
    own, sib, got, fin, snd = (scratch[k * n:(k + 1) * n] for k in range(5))
    send_sems, recv_sems, local_sems = scratch[5 * n:]
    shapes = [_rs_piece_shape(p) for p in parts]
    x, y, c = _place()
    j_me = 2 * x + y
    me, sibling = (x, y, c), (x, y, 1 - c)
    chips = [(x, 1 - y), (1 - x, y), (1 - x, 1 - y)]

    def piece(a, jj, core, narrow=False):
        ref = ins[n + a] if narrow else ins[a]
        if parts[a][1]:
            r, cl = shapes[a]
            return ref.at[pl.ds(core * r, r), pl.ds(jj * cl, cl)]
        return ref.at[2 * jj + core]

    def remote(a, sem, src, dst, to):
        return pltpu.make_async_remote_copy(
            src_ref=src, dst_ref=dst, send_sem=send_sems.at[a * RS_SEMS + sem],
            recv_sem=recv_sems.at[a * RS_SEMS + sem], device_id=to, device_id_type=MESH)

    def rows_loop(a, fn):
        r = shapes[a][0]
        step = max(s for s in RS_ADD_ROWS if r % s == 0)

        def it(i, carry):
            fn(pl.ds(pl.multiple_of(i * step, step), step))
            return carry

        lax.fori_loop(0, r // step, it, 0)

    def load(a, jj):
        return pltpu.make_async_copy(piece(a, jj, c), own[a].at[jj], local_sems.at[a * RS_LOCAL_SEMS + jj])

    def to_sibling(a, jj):
        return remote(a, jj, piece(a, jj, 1 - c, narrow=True), sib[a].at[jj], sibling)

    def to_owner(a, idx):
        chip = chips[idx]
        return remote(a, 4 + idx, snd[a].at[2 * chip[0] + chip[1]], got[a].at[j_me], (*chip, c))

    def store(a):
        return pltpu.make_async_copy(fin[a], outs[a].at[c], local_sems.at[a * RS_LOCAL_SEMS + 4])

    def result_to_sibling(a):
        return remote(a, 7, fin[a], outs[a].at[c], sibling)

    def exchange():
        for a in range(n):
            for jj in range(N_CHIPS):
                load(a, jj).start()
                to_sibling(a, jj).start()

    def chip_sums():
        for a in range(n):
            for jj in range(N_CHIPS):
                load(a, jj).wait()
                remote(a, jj, sib[a].at[jj], sib[a].at[jj], me).wait_recv()

                def add(sl, a=a, jj=jj):
                    q = own[a][jj, sl, :] + sib[a][jj, sl, :].astype(F32)
                    own[a][jj, sl, :] = q
                    snd[a][jj, sl, :] = q.astype(wire)

                rows_loop(a, add)
        for a in range(n):
            for idx in range(3):
                to_owner(a, idx).start()
        for a in range(n):
            def keep(sl, a=a):
                got[a][j_me, sl, :] = snd[a][j_me, sl, :]

            rows_loop(a, keep)

    def totals():
        for a in range(n):
            for idx, chip in enumerate(chips):
                slot = got[a].at[2 * chip[0] + chip[1]]
                remote(a, 4 + idx, slot, slot, me).wait_recv()

            def total(sl, a=a):
                mine = own[a][j_me, sl, :]
                term = [jnp.where(j_me == jj, mine, got[a][jj, sl, :].astype(F32)) for jj in range(N_CHIPS)]
                fin[a][sl, :] = ((term[0] + term[1]) + term[2]) + term[3]

            rows_loop(a, total)
            store(a).start()
            result_to_sibling(a).start()

    def finish():
        for a in range(n):
            remote(a, 7, outs[a].at[1 - c], outs[a].at[1 - c], me).wait_recv()
        for a in range(n):
            for jj in range(N_CHIPS):
                to_sibling(a, jj).wait_send()
            for idx in range(3):
                to_owner(a, idx).wait_send()
            result_to_sibling(a).wait_send()
            store(a).wait()

    return exchange, chip_sums, totals, finish


def _reduce_scatter(parts, name, wire=F32):
    n = len(parts)

    def body(*refs):
        for step in _rs_steps(parts, wire, refs[:2 * n], refs[2 * n:3 * n], refs[3 * n:]):
            step()

    out_shape, scratch = _rs_shapes(parts, wire)
    any_spec = pl.BlockSpec(memory_space=pl.ANY)
    return pl.pallas_call(
        body, name=name, out_shape=tuple(out_shape),
        in_specs=[any_spec] * (2 * n), out_specs=tuple([any_spec] * n), scratch_shapes=scratch,
        compiler_params=pltpu.CompilerParams(vmem_limit_bytes=VMEM_LIMIT_BYTES),
    )(*_rs_operands(parts))


def _rms(x):
    r = lax.rsqrt(jnp.mean(x * x, axis=-1, keepdims=True) + EPS)
    return x * r, r


def _rms_bwd(dxn, xn, r):
    return r * (dxn - xn * jnp.mean(dxn * xn, axis=-1, keepdims=True))


def _in_proj_gather(x2d, norm_g, w_in_sh, shards, tb):
    t = x2d.shape[0]
    nb = t // tb
    cols = IN_COLS // N_CHIPS
    half = D_MODEL // 2
    n = len(shards)

    def body(x_ref, g_ref, win_ref, *refs):
        ins = refs[:n]
        z_ref, h_ref, wfull_ref = refs[n:n + 3]
        outs = refs[n + 3:2 * n + 3]
        wv, h_buf, send_sems, recv_sems, local_sems, w_send, w_recv, w_local = refs[2 * n + 3:]
        s, i = pl.program_id(0), pl.program_id(1)
        x, y, c = _place()
        me, sibling = (x, y, c), (x, y, 1 - c)
        chips = [(x, 1 - y), (1 - x, y), (1 - x, 1 - y)]

        def w_half(cx, cy, hc):
            return wv.at[2 * cx + cy, pl.ds(hc * half, half), :]

        def w_remote(sem, block, to, src=None):
            dst = w_half(*block)
            return pltpu.make_async_remote_copy(
                src_ref=dst if src is None else src, dst_ref=dst, send_sem=w_send.at[sem],
                recv_sem=w_recv.at[sem], device_id=to, device_id_type=MESH)

        def w_first(idx):
            return w_remote(idx, (x, y, c), (*chips[idx], c), src=win_ref.at[pl.ds(c * half, half), :])

        def w_relay():
            src_chip = (jnp.bitwise_xor(x, 1 - c), jnp.bitwise_xor(y, c))
            dst_chip = (jnp.bitwise_xor(x, c), jnp.bitwise_xor(y, 1 - c))
            return w_remote(2, (*src_chip, c), (*dst_chip, c))

        def w_pass(idx):
            return w_remote(3 + idx, (*chips[idx], c), sibling)

        def w_store(k, cx, cy):
            jj = 2 * cx + cy
            return pltpu.make_async_copy(wv.at[jj], wfull_ref.at[:, pl.ds(jj * cols, cols)], w_local.at[k])

        start_rest, relay_rest, finish_rest = _gather_steps(shards, ins, outs, send_sems, recv_sems, local_sems)
        own = pltpu.make_async_copy(win_ref, wv.at[2 * x + y], w_local.at[4])

        @pl.when((s == 0) & (i == 0))
        def _():
            own.start()
            w_first(0).start()
            w_first(1).start()
            start_rest()
            own.wait()
            w_store(0, x, y).start()

        @pl.when((s == 1) & (i == 0))
        def _():
            w_remote(0, (*chips[0], c), me).wait_recv()
            w_remote(1, (*chips[1], c), me).wait_recv()
            w_relay().start()
            w_pass(0).start()
            w_pass(1).start()
            w_remote(3, (*chips[0], 1 - c), me).wait_recv()
            w_store(1, *chips[0]).start()

        @pl.when((s == 2) & (i == 0))
        def _():
            w_remote(4, (*chips[1], 1 - c), me).wait_recv()
            w_store(2, *chips[1]).start()

        @pl.when((s == 3) & (i == 0))
        def _():
            w_remote(2, (*chips[2], c), me).wait_recv()
            w_pass(2).start()
            w_remote(5, (*chips[2], 1 - c), me).wait_recv()
            w_store(3, *chips[2]).start()

        xn, _ = _rms(x_ref[...])
        h = (xn * g_ref[...]).astype(BF16)
        keep_h = pltpu.make_async_copy(h_buf, h_ref.at[pl.ds(pl.multiple_of(i * tb, tb), tb), :], w_local.at[5])

        @pl.when(s == 0)
        def _():
            h_buf[...] = h
            keep_h.start()

        z_ref[...] = _dot(h, wv[jnp.bitwise_xor(2 * x + y, s)])
        pl.when(s == 0)(keep_h.wait)

        @pl.when((s == N_CHIPS - 1) & (i == nb - 1))
        def _():
            relay_rest()
            finish_rest()
            for cp in (w_first(0), w_first(1), w_relay(), w_pass(0), w_pass(1), w_pass(2)):
                cp.wait_send()
            w_store(0, x, y).wait()
            for idx in range(3):
                w_store(idx + 1, *chips[idx]).wait()

    rest_shape, rest_sems = _gather_shapes(shards)
    out_shape = [jax.ShapeDtypeStruct((t, IN_COLS), F32), jax.ShapeDtypeStruct((t, D_MODEL), BF16),
                 jax.ShapeDtypeStruct((D_MODEL, IN_COLS), BF16)] + rest_shape
    any_spec = pl.BlockSpec(memory_space=pl.ANY)

    def z_map(s, i):
        return (i, jnp.bitwise_xor(2 * lax.axis_index("x") + lax.axis_index("y"), s))

    return pl.pallas_call(
        body, name="in_proj", out_shape=tuple(out_shape),
        grid=(N_CHIPS, nb),
        in_specs=[pl.BlockSpec((tb, D_MODEL), lambda s, i: (i, 0)),
                  pl.BlockSpec((1, D_MODEL), lambda s, i: (0, 0)), any_spec] + [any_spec] * n,
        out_specs=tuple([pl.BlockSpec((tb, cols), z_map), any_spec, any_spec] + [any_spec] * n),
        scratch_shapes=[pltpu.VMEM((N_CHIPS, D_MODEL, cols), BF16), pltpu.VMEM((tb, D_MODEL), BF16)] + rest_sems + [
            pltpu.SemaphoreType.DMA((GATHER_SEMS,)), pltpu.SemaphoreType.DMA((GATHER_SEMS,)),
            pltpu.SemaphoreType.DMA((N_CHIPS + 2,))],
        compiler_params=pltpu.CompilerParams(dimension_semantics=("arbitrary", "arbitrary"),
                                             vmem_limit_bytes=VMEM_LIMIT_BYTES),
    )(x2d, norm_g, w_in_sh, *[sh[0] for sh in shards])


def _in_proj_bwd(dz, w_in, x2d, dx_res, norm_g, tb, reduce):
    t = x2d.shape[0]
    parts, wire, steps = reduce
    n = len(parts)

    def body(dz_ref, w_ref, x_ref, dres_ref, g_ref, *refs):
        dx_ref, dg_ref = refs[2 * n:2 * n + 2]
        rs = _rs_steps(parts, wire, refs[:2 * n], refs[2 * n + 2:3 * n + 2], refs[3 * n + 2:])
        for step, when in zip(rs, steps):
            pl.when(pl.program_id(0) == when)(step)

        @pl.when(pl.program_id(0) == 0)
        def _():
            dg_ref[...] = jnp.zeros_like(dg_ref)

        xn, r = _rms(x_ref[...])
        g = g_ref[...]
        dh = _dot_nt(dz_ref[...], w_ref[...])
        dg_ref[...] += jnp.sum(dh * xn, axis=0, keepdims=True)
        dx_ref[...] = dres_ref[...] + _rms_bwd(dh * g, xn, r)

    row = lambda i: (i, 0)
    fixed = lambda i: (0, 0)
    rs_shape, rs_scratch = _rs_shapes(parts, wire)
    any_spec = pl.BlockSpec(memory_space=pl.ANY)
    return pl.pallas_call(
        body, name="in_proj_bwd",
        out_shape=tuple([jax.ShapeDtypeStruct((t, D_MODEL), F32), jax.ShapeDtypeStruct((1, D_MODEL), F32)] + rs_shape),
        grid=(t // tb,),
        in_specs=[pl.BlockSpec((tb, IN_COLS), row),
                  pl.BlockSpec((D_MODEL, IN_COLS), fixed, pipeline_mode=pl.Buffered(1)),
                  pl.BlockSpec((tb, D_MODEL), row), pl.BlockSpec((tb, D_MODEL), row),
                  pl.BlockSpec((1, D_MODEL), fixed)] + [any_spec] * (2 * n),
        out_specs=tuple([pl.BlockSpec((tb, D_MODEL), row), pl.BlockSpec((1, D_MODEL), fixed)] + [any_spec] * n),
        scratch_shapes=rs_scratch,
        compiler_params=pltpu.CompilerParams(dimension_semantics=("arbitrary",),
                                             vmem_limit_bytes=VMEM_LIMIT_BYTES),
    )(dz, w_in, x2d, dx_res, norm_g, *_rs_operands(parts))


def _weight_grad(lhs, rhs, n_chunks, tb, name, reduce=None):
    t, k = lhs.shape
    nc = rhs.shape[1] // n_chunks
    nb = t // tb
    parts, wire, steps = reduce if reduce is not None else ([], F32, ())
    n = len(parts)

    def body(l_ref, r_ref, *refs):
        o_ref, o16_ref = refs[2 * n:2 * n + 2]
        if n:
            at = pl.program_id(0) * nb + pl.program_id(1)
            rs = _rs_steps(parts, wire, refs[:2 * n], refs[2 * n + 2:3 * n + 2], refs[3 * n + 2:])
            for step, when in zip(rs, steps):
                pl.when(at == when)(step)

        @pl.when(pl.program_id(1) == 0)
        def _():
            o_ref[...] = jnp.zeros_like(o_ref)

        o_ref[...] += _dot_tn(l_ref[...], r_ref[...])

        @pl.when(pl.program_id(1) == nb - 1)
        def _():
            o16_ref[...] = o_ref[...].astype(BF16)

    rs_shape, rs_scratch = _rs_shapes(parts, wire) if n else ([], [])
    any_spec = pl.BlockSpec(memory_space=pl.ANY)
    chunk = pl.BlockSpec((None, k, nc), lambda j, i: (j, 0, 0))
    return pl.pallas_call(
        body, name=name,
        out_shape=tuple([jax.ShapeDtypeStruct((n_chunks, k, nc), F32), jax.ShapeDtypeStruct((n_chunks, k, nc), BF16)]
                        + rs_shape),
        grid=(n_chunks, nb),
        in_specs=[pl.BlockSpec((tb, k), lambda j, i: (i, 0)), pl.BlockSpec((tb, nc), lambda j, i: (i, j))]
        + [any_spec] * (2 * n),
        out_specs=tuple([chunk, chunk] + [any_spec] * n),
        scratch_shapes=rs_scratch,
        compiler_params=pltpu.CompilerParams(dimension_semantics=("arbitrary", "arbitrary"),
                                             vmem_limit_bytes=VMEM_LIMIT_BYTES),
    )(lhs, rhs, *_rs_operands(parts))


def _adamw(w, g, m, v, rows, name):
    r, c = w.shape

    def body(w_ref, g_ref, m_ref, v_ref, d_ref, nm_ref, nv_ref):
        g_ = g_ref[...]
        m_ = ADAM_B1 * m_ref[...] + (1.0 - ADAM_B1) * g_
        v_ = ADAM_B2 * v_ref[...] + (1.0 - ADAM_B2) * jnp.square(g_)
        m_hat = m_ / (1.0 - ADAM_B1 ** ADAM_STEP)
        v_hat = v_ / (1.0 - ADAM_B2 ** ADAM_STEP)
        d_ref[...] = -ADAM_LR * (m_hat / (jnp.sqrt(v_hat) + ADAM_EPS) + ADAM_WD * w_ref[...])
        nm_ref[...] = m_
        nv_ref[...] = v_

    spec = pl.BlockSpec((rows, c), lambda i: (i, 0))
    return pl.pallas_call(
        body, name=name, out_shape=tuple(jax.ShapeDtypeStruct((r, c), F32) for _ in range(3)),
        grid=(r // rows,), in_specs=[spec] * 4, out_specs=(spec,) * 3,
        compiler_params=pltpu.CompilerParams(dimension_semantics=("arbitrary",),
                                             vmem_limit_bytes=VMEM_LIMIT_BYTES),
    )(w, g, m, v)


def _shift_down(ext, s):
    return pltpu.roll(ext, s, 0)


def _shift_up(ext, s):
    return pltpu.roll(ext, ext.shape[0] - s, 0)


def _lru_gates(xc, wa_ref, ba, wx_ref, bx, lam):
    pa, px = [], []
    for h in range(LRU_HEADS):
        xh = xc[:, h * HEAD_DIM:(h + 1) * HEAD_DIM].astype(BF16)
        pa.append(_dot(xh, wa_ref[h]))
        px.append(_dot(xh, wx_ref[h]))
    r = _sigmoid(jnp.concatenate(pa, axis=1) + ba)
    ig = _sigmoid(jnp.concatenate(px, axis=1) + bx)
    sp = _softplus(-lam)
    log_a = (-LRU_C * r) * sp
    a = jnp.exp(log_a)
    mult = jnp.sqrt(jnp.tanh(-log_a) * (1.0 + a * a))
    return r, ig, a, mult, sp


def _conv(ext, w_ref, b):
    y = b + _shift_down(ext, 3) * w_ref[0:1, :]
    y = y + _shift_down(ext, 2) * w_ref[1:2, :]
    y = y + _shift_down(ext, 1) * w_ref[2:3, :]
    y = y + ext * w_ref[3:4, :]
    return y[CONV_HIST:, :]


def _pool_diff(ext, pos):
    out = []
    for g, k in enumerate(POOL_WINDOWS):
        col = ext[:, g * POOL_GROUP_DIM:(g + 1) * POOL_GROUP_DIM]
        s = col
        for step in range(g + 1):
            s = s + _shift_down(s, 2 ** step)
        count = jnp.minimum(pos + 1, k).astype(F32)
        out.append(s[POOL_HIST:, :] / count - col[POOL_HIST:, :])
    return out


def _pool_mix(diff, pw_ref):
    return jnp.concatenate([_dot(diff[g].astype(BF16), pw_ref[g]) for g in range(len(POOL_WINDOWS))], axis=1)


def _branch_specs(tb, row_map, fixed):
    fixed3 = lambda i: (0, 0, 0)
    return [pl.BlockSpec((CONV_WIDTH, D_MODEL), fixed), pl.BlockSpec((1, D_MODEL), fixed),
            pl.BlockSpec((LRU_HEADS, HEAD_DIM, HEAD_DIM), fixed3), pl.BlockSpec((1, D_MODEL), fixed),
            pl.BlockSpec((LRU_HEADS, HEAD_DIM, HEAD_DIM), fixed3), pl.BlockSpec((1, D_MODEL), fixed),
            pl.BlockSpec((1, D_MODEL), fixed),
            pl.BlockSpec((len(POOL_WINDOWS), POOL_GROUP_DIM, POOL_GROUP_DIM), fixed3),
            pl.BlockSpec((1, POOL_WIDTH), fixed)]


def _branches_fwd(z, weights, seq, tb, shards):
    t = z.shape[0]
    nb = t // tb
    nbe = seq // tb
    groups = tb // F32_SUBLANES
    n = len(shards)

    def body(xa_ref, ga_ref, xb_ref, gb_ref, cw_ref, cb_ref, wa_ref, ba_ref, wx_ref, bx_ref, lam_ref,
             pw_ref, ps_ref, *refs):
        g_ins = refs[:n]
        ya_ref, yb_ref, hl_ref = refs[n:n + 3]
        g_outs = refs[n + 3:2 * n + 3]
        xa_ext, xb_ext, carry, a_s, u_s, send_sems, recv_sems, local_sems = refs[2 * n + 3:]
        blk = pl.program_id(0) % nbe
        start_gather, relay_gather, finish_gather = _gather_steps(shards, g_ins, g_outs, send_sems, recv_sems,
                                                                  local_sems)
        pl.when(pl.program_id(0) == 0)(start_gather)
        pl.when(pl.program_id(0) == nb // 2)(relay_gather)

        @pl.when(blk == 0)
        def _():
            xa_ext[0:CONV_HIST, :] = jnp.zeros((CONV_HIST, D_MODEL), F32)
            xb_ext[0:POOL_HIST, :] = jnp.zeros((POOL_HIST, POOL_WIDTH), F32)
            carry[...] = jnp.zeros_like(carry)

        xa_ext[CONV_HIST:, :] = xa_ref[...]
        xb_ext[POOL_HIST:, :] = xb_ref[...]
        ea = xa_ext[...]
        eb = xb_ext[...]
        xa_ext[0:CONV_HIST, :] = ea[tb:, :]
        xb_ext[0:POOL_HIST, :] = eb[tb:, :]

        xc = _conv(ea, cw_ref, cb_ref[...])
        _, ig, a, mult, _ = _lru_gates(xc, wa_ref, ba_ref[...], wx_ref, bx_ref[...], lam_ref[...])
        u = mult * (ig * xc)
        row8 = lax.broadcasted_iota(jnp.int32, (tb, D_MODEL), 0) % F32_SUBLANES
        for s in (1, 2, 4):
            m = row8 >= s
            u = jnp.where(m, a * _shift_down(u, s) + u, u)
            a = jnp.where(m, a * _shift_down(a, s), a)
        a_s[...] = a
        u_s[...] = u

        def step(g, cr):
            sl = pl.ds(pl.multiple_of(g * F32_SUBLANES, F32_SUBLANES), F32_SUBLANES)
            hb = a_s[sl, :] * cr + u_s[sl, :]
            hl_ref[sl, :] = hb
            return jnp.broadcast_to(hb[F32_SUBLANES - 1:F32_SUBLANES, :], (F32_SUBLANES, D_MODEL))

        carry[...] = lax.fori_loop(0, groups, step, carry[...], unroll=4)
        ga = ga_ref[...]
        ya_ref[...] = (hl_ref[...] * (ga * _sigmoid(ga))).astype(BF16)

        pos = blk * tb + lax.broadcasted_iota(jnp.int32, (tb, POOL_GROUP_DIM), 0)
        ypre = _pool_mix(_pool_diff(eb, pos), pw_ref)
        gb = gb_ref[...]
        yb_ref[...] = ((ypre * ps_ref[...]) * (gb * _sigmoid(gb))).astype(BF16)
        pl.when(pl.program_id(0) == nb - 1)(finish_gather)

    row = lambda i: (i, 0)
    fixed = lambda i: (0, 0)
    any_spec = pl.BlockSpec(memory_space=pl.ANY)
    in_specs = [pl.BlockSpec((tb, D_MODEL), lambda i: (i, 0)), pl.BlockSpec((tb, D_MODEL), lambda i: (i, 1)),
                pl.BlockSpec((tb, POOL_WIDTH), lambda i: (i, 4)), pl.BlockSpec((tb, POOL_WIDTH), lambda i: (i, 5)),
                ] + _branch_specs(tb, row, fixed) + [any_spec] * n
    g_shape, g_sems = _gather_shapes(shards)
    return pl.pallas_call(
        body, name="branches_fwd",
        out_shape=tuple([jax.ShapeDtypeStruct((t, D_MODEL), BF16), jax.ShapeDtypeStruct((t, POOL_WIDTH), BF16),
                         jax.ShapeDtypeStruct((t, D_MODEL), F32)] + g_shape),
        grid=(nb,), in_specs=in_specs,
        out_specs=tuple([pl.BlockSpec((tb, D_MODEL), row), pl.BlockSpec((tb, POOL_WIDTH), row),
                         pl.BlockSpec((tb, D_MODEL), row)] + [any_spec] * n),
        scratch_shapes=[pltpu.VMEM((tb + CONV_HIST, D_MODEL), F32), pltpu.VMEM((tb + POOL_HIST, POOL_WIDTH), F32),
                        pltpu.VMEM((F32_SUBLANES, D_MODEL), F32),
                        pltpu.VMEM((tb, D_MODEL), F32), pltpu.VMEM((tb, D_MODEL), F32)] + g_sems,
        compiler_params=pltpu.CompilerParams(dimension_semantics=("arbitrary",),
                                             vmem_limit_bytes=VMEM_LIMIT_BYTES),
    )(z, z, z, z, *weights, *[sh[0] for sh in shards])


def _branches_bwd(z, hl, dya, dyb, dzm, weights, seq, tb):
    t = z.shape[0]
    nb = t // tb
    nbe = seq // tb
    groups = tb // F32_SUBLANES
    n_pool = len(POOL_WINDOWS)

    def body(xa_ref, xap_ref, ga_ref, xb_ref, xbp_ref, gb_ref, hl_ref, hlp_ref, dya_ref, dyb_ref, dzm_ref,
             cw_ref, cb_ref, wa_ref, ba_ref, wx_ref, bx_ref, lam_ref, pw_ref, ps_ref,
             dz_ref, dcw_ref, dcb_ref, dwa_ref, dba_ref, dwx_ref, dbx_ref, dlam_ref, dpw_ref, dps_ref,
             xa_ext, xb_ext, hl_ext, a_ext, dxc_ext, dwin_ext, g_carry, b_s, d_s, g_s):
        i = pl.program_id(0)
        blk = (nb - 1 - i) % nbe

        @pl.when(i == 0)
        def _():
            for ref in (dcw_ref, dcb_ref, dwa_ref, dba_ref, dwx_ref, dbx_ref, dlam_ref, dpw_ref, dps_ref):
                ref[...] = jnp.zeros_like(ref)

        @pl.when(blk == nbe - 1)
        def _():
            a_ext[tb:, :] = jnp.zeros((F32_SUBLANES, D_MODEL), F32)
            dxc_ext[tb:, :] = jnp.zeros((CONV_HIST, D_MODEL), F32)
            dwin_ext[tb:, :] = jnp.zeros((POOL_HIST, POOL_WIDTH), F32)
            g_carry[...] = jnp.zeros_like(g_carry)

        live = (blk > 0).astype(F32)
        xa_ext[0:CONV_HIST, :] = xap_ref[...] * live
        xa_ext[CONV_HIST:, :] = xa_ref[...]
        xb_ext[0:POOL_HIST, :] = xbp_ref[...] * live
        xb_ext[POOL_HIST:, :] = xb_ref[...]
        hl_ext[0:F32_SUBLANES, :] = hlp_ref[...] * live
        hl_ext[F32_SUBLANES:, :] = hl_ref[...]
        ea = xa_ext[...]
        eb = xb_ext[...]

        xc = _conv(ea, cw_ref, cb_ref[...])
        lam = lam_ref[...]
        r, ig, a, mult, sp = _lru_gates(xc, wa_ref, ba_ref[...], wx_ref, bx_ref[...], lam)
        hl = hl_ref[...]
        ga = ga_ref[...]
        sga = _sigmoid(ga)
        dya = dya_ref[...]
        dhl = dya * (ga * sga)
        dz_ref[:, D_MODEL:2 * D_MODEL] = (dya * hl * (sga * (1.0 + ga * (1.0 - sga)))).astype(BF16)

        a_ext[0:tb, :] = a
        b = _shift_up(a_ext[...], 1)[0:tb, :]
        a_ext[tb:, :] = jnp.broadcast_to(a[0:1, :], (F32_SUBLANES, D_MODEL))
        d = dhl
        row8 = lax.broadcasted_iota(jnp.int32, (tb, D_MODEL), 0) % F32_SUBLANES
        for s in (1, 2, 4):
            m = row8 < F32_SUBLANES - s
            d = jnp.where(m, d + b * _shift_up(d, s), d)
            b = jnp.where(m, b * _shift_up(b, s), b)
        b_s[...] = b
        d_s[...] = d

        def step(k, cr):
            sl = pl.ds(pl.multiple_of((groups - 1 - k) * F32_SUBLANES, F32_SUBLANES), F32_SUBLANES)
            gb_ = d_s[sl, :] + b_s[sl, :] * cr
            g_s[sl, :] = gb_
            return jnp.broadcast_to(gb_[0:1, :], (F32_SUBLANES, D_MODEL))

        g_carry[...] = lax.fori_loop(0, groups, step, g_carry[...], unroll=4)
        gsc = g_s[...]
        da = gsc * _shift_down(hl_ext[...], 1)[F32_SUBLANES:, :]
        dmult = gsc * (ig * xc)
        dig = gsc * (mult * xc)
        dxc = gsc * (mult * ig)
        dlog_a = da * a - (a * a) * dmult / mult
        dr = dlog_a * (-LRU_C * sp)
        dlam_ref[...] += jnp.sum(dlog_a * (-LRU_C * r), axis=0, keepdims=True)
        dpa = dr * (r * (1.0 - r))
        dpx = dig * (ig * (1.0 - ig))
        dba_ref[...] += jnp.sum(dpa, axis=0, keepdims=True)
        dbx_ref[...] += jnp.sum(dpx, axis=0, keepdims=True)
        back = []
        for h in range(LRU_HEADS):
            cols = slice(h * HEAD_DIM, (h + 1) * HEAD_DIM)
            xh = xc[:, cols].astype(BF16)
            dpa_h = dpa[:, cols].astype(BF16)
            dpx_h = dpx[:, cols].astype(BF16)
            dwa_ref[h] += _dot_tn(xh, dpa_h)
            dwx_ref[h] += _dot_tn(xh, dpx_h)
            back.append(_dot_nt(dpa_h, wa_ref[h]) + _dot_nt(dpx_h, wx_ref[h]))
        dxc = dxc + jnp.concatenate(back, axis=1)
        dcb_ref[...] += jnp.sum(dxc, axis=0, keepdims=True)
        for k in range(CONV_WIDTH):
            tap = _shift_down(ea, CONV_WIDTH - 1 - k)[CONV_HIST:, :] if k < CONV_WIDTH - 1 else ea[CONV_HIST:, :]
            dcw_ref[k:k + 1, :] += jnp.sum(dxc * tap, axis=0, keepdims=True)
        dxc_ext[0:tb, :] = dxc
        ed = dxc_ext[...]
        dxa = ed * cw_ref[3:4, :]
        dxa = dxa + _shift_up(ed, 1) * cw_ref[2:3, :]
        dxa = dxa + _shift_up(ed, 2) * cw_ref[1:2, :]
        dxa = dxa + _shift_up(ed, 3) * cw_ref[0:1, :]
        dz_ref[:, 0:D_MODEL] = dxa[0:tb, :].astype(BF16)
        dxc_ext[tb:, :] = dxc[0:CONV_HIST, :]

        pos = blk * tb + lax.broadcasted_iota(jnp.int32, (tb, POOL_GROUP_DIM), 0)
        diff = _pool_diff(eb, pos)
        ypre = _pool_mix(diff, pw_ref)
        ps = ps_ref[...]
        gb = gb_ref[...]
        sgb = _sigmoid(gb)
        dyb = dyb_ref[...]
        dyp = dyb * (gb * sgb)
        dz_ref[:, 2 * D_MODEL + POOL_WIDTH:3 * D_MODEL] = (
            dyb * (ypre * ps) * (sgb * (1.0 + gb * (1.0 - sgb)))).astype(BF16)
        dps_ref[...] += jnp.sum(dyp * ypre, axis=0, keepdims=True)
        dypre = dyp * ps
        for g, k in enumerate(POOL_WINDOWS):
            cols = slice(g * POOL_GROUP_DIM, (g + 1) * POOL_GROUP_DIM)
            dyg = dypre[:, cols].astype(BF16)
            dpw_ref[g] += _dot_tn(diff[g].astype(BF16), dyg)
            ddiff = _dot_nt(dyg, pw_ref[g])
            count = jnp.minimum(pos + 1, k).astype(F32)
            dwin = ddiff / count
            dwin_ext[0:tb, cols] = dwin
            s = dwin_ext[:, cols]
            for step_ in range(g + 1):
                s = s + _shift_up(s, 2 ** step_)
            dz_ref[:, 2 * D_MODEL + g * POOL_GROUP_DIM:2 * D_MODEL + (g + 1) * POOL_GROUP_DIM] = (
                s[0:tb, :] - ddiff).astype(BF16)
            dwin_ext[tb:, cols] = dwin[0:POOL_HIST, :]

        dz_ref[:, 3 * D_MODEL:] = dzm_ref[...]

        @pl.when(i == nb - 1)
        def _():
            dlam_ref[...] = dlam_ref[...] * (-_sigmoid(-lam))

    rev = lambda i: (nb - 1 - i, 0)
    fixed = lambda i: (0, 0)
    fixed3 = lambda i: (0, 0, 0)

    def prev(rows, col):
        per = tb // rows
        return lambda i: (jnp.maximum((nb - 1 - i) * per - 1, 0), col)

    in_specs = [pl.BlockSpec((tb, D_MODEL), lambda i: (nb - 1 - i, 0)),
                pl.BlockSpec((CONV_HIST, D_MODEL), prev(CONV_HIST, 0)),
                pl.BlockSpec((tb, D_MODEL), lambda i: (nb - 1 - i, 1)),
                pl.BlockSpec((tb, POOL_WIDTH), lambda i: (nb - 1 - i, 4)),
                pl.BlockSpec((POOL_HIST, POOL_WIDTH), prev(POOL_HIST, 4)),
                pl.BlockSpec((tb, POOL_WIDTH), lambda i: (nb - 1 - i, 5)),
                pl.BlockSpec((tb, D_MODEL), rev),
                pl.BlockSpec((F32_SUBLANES, D_MODEL), prev(F32_SUBLANES, 0)),
                pl.BlockSpec((tb, D_MODEL), rev), pl.BlockSpec((tb, POOL_WIDTH), rev),
                pl.BlockSpec((tb, 2 * D_MODEL), rev)] + _branch_specs(tb, rev, fixed)
    out_shape = (jax.ShapeDtypeStruct((t, IN_COLS), BF16),
                 jax.ShapeDtypeStruct((CONV_WIDTH, D_MODEL), F32), jax.ShapeDtypeStruct((1, D_MODEL), F32),
                 jax.ShapeDtypeStruct((LRU_HEADS, HEAD_DIM, HEAD_DIM), F32), jax.ShapeDtypeStruct((1, D_MODEL), F32),
                 jax.ShapeDtypeStruct((LRU_HEADS, HEAD_DIM, HEAD_DIM), F32), jax.ShapeDtypeStruct((1, D_MODEL), F32),
                 jax.ShapeDtypeStruct((1, D_MODEL), F32),
                 jax.ShapeDtypeStruct((n_pool, POOL_GROUP_DIM, POOL_GROUP_DIM), F32),
                 jax.ShapeDtypeStruct((1, POOL_WIDTH), F32))
    out_specs = (pl.BlockSpec((tb, IN_COLS), rev),
                 pl.BlockSpec((CONV_WIDTH, D_MODEL), fixed), pl.BlockSpec((1, D_MODEL), fixed),
                 pl.BlockSpec((LRU_HEADS, HEAD_DIM, HEAD_DIM), fixed3), pl.BlockSpec((1, D_MODEL), fixed),
                 pl.BlockSpec((LRU_HEADS, HEAD_DIM, HEAD_DIM), fixed3), pl.BlockSpec((1, D_MODEL), fixed),
                 pl.BlockSpec((1, D_MODEL), fixed),
                 pl.BlockSpec((n_pool, POOL_GROUP_DIM, POOL_GROUP_DIM), fixed3),
                 pl.BlockSpec((1, POOL_WIDTH), fixed))
    scratch = [pltpu.VMEM((tb + CONV_HIST, D_MODEL), F32), pltpu.VMEM((tb + POOL_HIST, POOL_WIDTH), F32),
               pltpu.VMEM((tb + F32_SUBLANES, D_MODEL), F32), pltpu.VMEM((tb + F32_SUBLANES, D_MODEL), F32),
               pltpu.VMEM((tb + CONV_HIST, D_MODEL), F32), pltpu.VMEM((tb + POOL_HIST, POOL_WIDTH), F32),
               pltpu.VMEM((F32_SUBLANES, D_MODEL), F32),
               pltpu.VMEM((tb, D_MODEL), F32), pltpu.VMEM((tb, D_MODEL), F32), pltpu.VMEM((tb, D_MODEL), F32)]
    return pl.pallas_call(
        body, name="branches_bwd", out_shape=out_shape, grid=(nb,), in_specs=in_specs, out_specs=out_specs,
        scratch_shapes=scratch,
        compiler_params=pltpu.CompilerParams(dimension_semantics=("arbitrary",),
                                             vmem_limit_bytes=VMEM_LIMIT_BYTES),
    )(z, z, z, z, z, z, hl, hl, dya, dyb, dzm, *weights)


def _merge_head(x2d, ya, yb, z, p2d, tgt, w_pl, w_pp, w_out, w_pg, w_pe, g2, gf, tb):
    t = x2d.shape[0]
    p_dim = p2d.shape[1]

    def body(x_ref, ya_ref, yb_ref, ma_ref, mb_ref, p_ref, t_ref, wpl_ref, wpp_ref, wout_ref, wpg_ref, wpe_ref,
             g2_ref, gf_ref,
             loss_ref, dg2_ref, dgf_ref, dxr_ref, dya_ref, dyb_ref, dzm_ref,
             mg_ref, do_ref, hn_ref, dgp_ref, dpe_ref, da_ref, dbm_ref, pbf_ref):
        @pl.when(pl.program_id(0) == 0)
        def _():
            loss_ref[...] = jnp.zeros_like(loss_ref)
            dg2_ref[...] = jnp.zeros_like(dg2_ref)
            dgf_ref[...] = jnp.zeros_like(dgf_ref)

        a_ = _dot(ya_ref[...], wpl_ref[...])
        bm = _dot(yb_ref[...], wpp_ref[...])
        sa = _sigmoid(ma_ref[...])
        sb = _sigmoid(mb_ref[...])
        mg = (sa * a_ + sb * bm).astype(BF16)
        mg_ref[...] = mg
        x1 = x_ref[...] + _dot(mg, wout_ref[...])
        xn2, r2 = _rms(x1)
        g2 = g2_ref[...]
        hn = (xn2 * g2).astype(BF16)
        hn_ref[...] = hn
        gate = _sigmoid(_dot(hn, wpg_ref[...]))
        pbf = p_ref[...].astype(BF16)
        pbf_ref[...] = pbf
        pe = _dot(pbf, wpe_ref[...])
        x2 = x1 + gate * pe
        xn3, r3 = _rms(x2)
        gf = gf_ref[...]
        err = xn3 * gf - t_ref[...]
        loss_ref[...] += 0.5 * jnp.sum(jnp.mean(err * err, axis=-1))

        dy = err * (1.0 / D_MODEL)
        dgf_ref[...] += jnp.sum(dy * xn3, axis=0, keepdims=True)
        dx2 = _rms_bwd(dy * gf, xn3, r3)
        dpe_ref[...] = (dx2 * gate).astype(BF16)
        dgp = ((dx2 * pe) * (gate * (1.0 - gate))).astype(BF16)
        dgp_ref[...] = dgp
        dhn = _dot_nt(dgp, wpg_ref[...])
        dg2_ref[...] += jnp.sum(dhn * xn2, axis=0, keepdims=True)
        dx1 = dx2 + _rms_bwd(dhn * g2, xn2, r2)
        dxr_ref[...] = dx1
        do = dx1.astype(BF16)
        do_ref[...] = do
        dmg = _dot_nt(do, wout_ref[...])
        da = (dmg * sa).astype(BF16)
        dbm = (dmg * sb).astype(BF16)
        da_ref[...] = da
        dbm_ref[...] = dbm
        dzm_ref[:, 0:D_MODEL] = (dmg * a_ * (sa * (1.0 - sa))).astype(BF16)
        dzm_ref[:, D_MODEL:] = (dmg * bm * (sb * (1.0 - sb))).astype(BF16)
        dya_ref[...] = _dot_nt(da, wpl_ref[...])
        dyb_ref[...] = _dot_nt(dbm, wpp_ref[...])

    row = lambda i: (i, 0)
    fixed = lambda i: (0, 0)

    def resident(shape):
        return pl.BlockSpec(shape, fixed, pipeline_mode=pl.Buffered(1))

    tok = lambda width: pl.BlockSpec((tb, width), row)
    in_specs = [tok(D_MODEL), tok(D_MODEL), tok(POOL_WIDTH),
                pl.BlockSpec((tb, D_MODEL), lambda i: (i, 3)), pl.BlockSpec((tb, D_MODEL), lambda i: (i, 4)),
                tok(p_dim), tok(D_MODEL),
                resident((D_MODEL, D_MODEL)), resident((POOL_WIDTH, D_MODEL)), resident((D_MODEL, D_MODEL)),
                resident((D_MODEL, D_MODEL)), resident((p_dim, D_MODEL)),
                pl.BlockSpec((1, D_MODEL), fixed), pl.BlockSpec((1, D_MODEL), fixed)]
    bf = lambda width: jax.ShapeDtypeStruct((t, width), BF16)
    f32 = lambda width: jax.ShapeDtypeStruct((t, width), F32)
    out_shape = (jax.ShapeDtypeStruct((F32_SUBLANES, 128), F32), jax.ShapeDtypeStruct((1, D_MODEL), F32),
                 jax.ShapeDtypeStruct((1, D_MODEL), F32),
                 f32(D_MODEL), f32(D_MODEL), f32(POOL_WIDTH), bf(2 * D_MODEL),
                 bf(D_MODEL), bf(D_MODEL), bf(D_MODEL), bf(D_MODEL), bf(D_MODEL), bf(D_MODEL), bf(D_MODEL), bf(p_dim))
    out_specs = (pl.BlockSpec((F32_SUBLANES, 128), fixed), pl.BlockSpec((1, D_MODEL), fixed),
                 pl.BlockSpec((1, D_MODEL), fixed),
                 tok(D_MODEL), tok(D_MODEL), tok(POOL_WIDTH), tok(2 * D_MODEL),
                 tok(D_MODEL), tok(D_MODEL), tok(D_MODEL), tok(D_MODEL), tok(D_MODEL), tok(D_MODEL), tok(D_MODEL),
                 tok(p_dim))
    return pl.pallas_call(
        body, name="merge_head", out_shape=out_shape, grid=(t // tb,), in_specs=in_specs, out_specs=out_specs,
        compiler_params=pltpu.CompilerParams(dimension_semantics=("arbitrary",),
                                             vmem_limit_bytes=VMEM_LIMIT_BYTES),
    )(x2d, ya, yb, z, z, p2d, tgt, w_pl, w_pp, w_out, w_pg, w_pe, g2, gf)


def _pad_rows(a, rows):
    return jnp.pad(a, ((0, rows - a.shape[0]), (0, D_MODEL - a.shape[1])))


def _pack_bag(parts, tail=None):
    rows = [_pad_rows(a.reshape(-1, a.shape[-1]) if a.shape[-1] != HEAD_DIM else a.reshape(-1, D_MODEL), n)
            for a, n in zip(parts, BAG_PART_ROWS)]
    spare = BAG_ROWS - sum(BAG_PART_ROWS)
    if tail is not None:
        rows.append(_pad_rows(tail, F32_SUBLANES))
        spare -= F32_SUBLANES
    rows.append(jnp.zeros((spare, D_MODEL), F32))
    return jnp.concatenate(rows, axis=0)


def _unpack_bag(bag, shapes):
    out, at = [], 0
    for shape, n in zip(shapes, BAG_PART_ROWS):
        size = 1
        for s in shape:
            size *= s
        if size % D_MODEL == 0:
            piece = bag[at:at + size // D_MODEL, :]
        else:
            piece = bag[at:at + 1, :size]
        out.append(piece.reshape(shape))
        at += n
    return out


def kernel(x, p, norm_g, w_in, conv_w, conv_b, lru_w_a, lru_b_a, lru_w_x, lru_b_x, lru_lambda, pool_w, pool_scale, w_proj_lru, w_proj_pool, w_out, ple_norm_g, w_ple_gate, w_ple_proj, final_g, loss_target, m_norm_g, m_w_in, m_conv_w, m_conv_b, m_lru_w_a, m_lru_b_a, m_lru_w_x, m_lru_b_x, m_lru_lambda, m_pool_w, m_pool_scale, m_w_proj_lru, m_w_proj_pool, m_w_out, m_ple_norm_g, m_w_ple_gate, m_w_ple_proj, m_final_g, v_norm_g, v_w_in, v_conv_w, v_conv_b, v_lru_w_a, v_lru_b_a, v_lru_w_x, v_lru_b_x, v_lru_lambda, v_pool_w, v_pool_scale, v_w_proj_lru, v_w_proj_pool, v_w_out, v_ple_norm_g, v_w_ple_gate, v_w_ple_proj, v_final_g):
    bsz, seq, _ = x.shape
    t = bsz * seq
    tb_mm = min(512, seq)
    tb_seq = min(256, seq // 2) if seq >= 512 else seq
    x2d = x.reshape(t, D_MODEL)
    p2d = p.reshape(t, p.shape[-1])
    tgt = loss_target.reshape(t, D_MODEL)
    chip = 2 * lax.axis_index("x") + lax.axis_index("y")

    rest = [(w_proj_lru[0], 0), (w_proj_pool[0], 1), (w_out[0], 0), (w_ple_gate[0], 0), (w_ple_proj[0], 1)]
    z, h_bf, w_in_f, conv_w_f = _in_proj_gather(x2d, norm_g, w_in[0].astype(BF16), [(conv_w[0], 1, False)], tb_mm)

    wa_bf = lru_w_a[0].astype(BF16)
    wx_bf = lru_w_x[0].astype(BF16)
    pw_bf = pool_w[0].astype(BF16)
    branch_w = (conv_w_f, conv_b, wa_bf, lru_b_a.reshape(1, D_MODEL), wx_bf, lru_b_x.reshape(1, D_MODEL),
                lru_lambda, pw_bf, pool_scale)

    ya, yb, hl, w_pl_f, w_pp_f, w_out_f, w_pg_f, w_pe_f = _branches_fwd(
        z, branch_w, seq, tb_seq, [(w.astype(BF16), axis, True) for w, axis in rest])
    (loss_acc, d_g2, d_gf, dx_res, dya, dyb, dzm, mg_bf, do_bf, hn_bf, dgp_bf, dpe_bf, da_bf, dbm_bf, p_bf) = _merge_head(
        x2d, ya, yb, z, p2d, tgt, w_pl_f, w_pp_f, w_out_f, w_pg_f, w_pe_f, ple_norm_g, final_g.reshape(1, D_MODEL),
        tb_seq)
    (dz, d_cw, d_cb, d_wa, d_ba, d_wx, d_bx, d_lam, d_pw, d_ps) = _branches_bwd(
        z, hl, dya, dyb, dzm, branch_w, seq, tb_seq)

    g_pl = _weight_grad(ya, da_bf, 1, tb_mm, "dw_proj_lru")[0].reshape(8, D_MODEL // 8, D_MODEL)
    g_pp = _weight_grad(yb, dbm_bf, 1, tb_mm, "dw_proj_pool")[0][0]
    g_out = _weight_grad(mg_bf, do_bf, 1, tb_mm, "dw_out")[0].reshape(8, D_MODEL // 8, D_MODEL)
    g_pg = _weight_grad(hn_bf, dgp_bf, 1, tb_mm, "dw_ple_gate")[0].reshape(8, D_MODEL // 8, D_MODEL)
    p_dim = p2d.shape[1]
    g_pe = _weight_grad(p_bf, dpe_bf, 1, tb_mm, "dw_ple_proj")[0][0]
    nb_mm = t // tb_mm
    g_in, g_in16, r_pl, r_pp, r_out, r_pg, r_pe = _weight_grad(
        h_bf, dz, N_CHIPS, tb_mm, "dw_in",
        reduce=([(g_pl, False, None), (g_pp, True, None), (g_out, False, None), (g_pg, False, None),
                 (g_pe, True, None)], BF16, (0, nb_mm, 3 * nb_mm + nb_mm // 2, N_CHIPS * nb_mm - 1)))
    pieces = (8, D_MODEL // 2, IN_COLS // N_CHIPS)
    nb_seq = t // tb_seq
    dx, d_g1, r_in = _in_proj_bwd(
        dz, w_in_f, x2d, dx_res, norm_g, tb_seq,
        reduce=([(g_in.reshape(pieces), False, g_in16.reshape(pieces))], BF16, (0, nb_seq // 4, nb_seq - 1, nb_seq - 1)))
    small_shapes = [(1, D_MODEL), (1, CONV_WIDTH, D_MODEL), (1, D_MODEL), lru_w_a.shape, lru_b_a.shape, lru_w_x.shape,
                    lru_b_x.shape, (1, D_MODEL), pool_w.shape, pool_scale.shape, (1, D_MODEL), final_g.shape]
    bag = _pack_bag([d_g1, d_cw, d_cb, d_wa, d_ba.reshape(1, D_MODEL), d_wx, d_bx.reshape(1, D_MODEL), d_lam, d_pw,
                     d_ps, d_g2, d_gf], tail=loss_acc)
    (bag_mine,) = _reduce_scatter([(bag.reshape(8, BAG_ROWS // 8, D_MODEL), False, None)], "rs_small")
    (bag_sum,) = _gather_shards([(bag_mine.reshape(BAG_ROWS // N_CHIPS, D_MODEL), 0, True)], "gather_small")
    (g_g1, g_cw_full, g_cb, g_wa, g_ba, g_wx, g_bx, g_lam, g_pw, g_ps, g_g2, g_gf) = _unpack_bag(bag_sum, small_shapes)
    cw_cols = D_MODEL // N_CHIPS
    g_cw = lax.dynamic_slice_in_dim(g_cw_full, chip * cw_cols, cw_cols, axis=2)

    def big_update(w, g2d, m, v, rows, name):
        d, nm, nv = _adamw(w[0], g2d, m[0], v[0], rows, name)
        return g2d[None], d[None], nm[None], nv[None]

    u_in = big_update(w_in, r_in.reshape(D_MODEL, IN_COLS // N_CHIPS), m_w_in, v_w_in, 256, "adamw_w_in")
    u_pl = big_update(w_proj_lru, r_pl.reshape(D_MODEL // N_CHIPS, D_MODEL), m_w_proj_lru, v_w_proj_lru, 256, "adamw_w_proj_lru")
    u_pp = big_update(w_proj_pool, r_pp.reshape(POOL_WIDTH, D_MODEL // N_CHIPS), m_w_proj_pool, v_w_proj_pool, 512, "adamw_w_proj_pool")
    u_out = big_update(w_out, r_out.reshape(D_MODEL // N_CHIPS, D_MODEL), m_w_out, v_w_out, 256, "adamw_w_out")
    u_pg = big_update(w_ple_gate, r_pg.reshape(D_MODEL // N_CHIPS, D_MODEL), m_w_ple_gate, v_w_ple_gate, 256, "adamw_w_ple_gate")
    u_pe = big_update(w_ple_proj, r_pe.reshape(p_dim, D_MODEL // N_CHIPS), m_w_ple_proj, v_w_ple_proj, 256, "adamw_w_ple_proj")
    u_cw = big_update(conv_w, g_cw[0], m_conv_w, v_conv_w, CONV_WIDTH, "adamw_conv_w")

    small_w = [norm_g, None, conv_b, lru_w_a, lru_b_a, lru_w_x, lru_b_x, lru_lambda, pool_w, pool_scale, ple_norm_g, final_g]
    small_m = [m_norm_g, None, m_conv_b, m_lru_w_a, m_lru_b_a, m_lru_w_x, m_lru_b_x, m_lru_lambda, m_pool_w, m_pool_scale, m_ple_norm_g, m_final_g]
    small_v = [v_norm_g, None, v_conv_b, v_lru_w_a, v_lru_b_a, v_lru_w_x, v_lru_b_x, v_lru_lambda, v_pool_w, v_pool_scale, v_ple_norm_g, v_final_g]
    fill = jnp.zeros((CONV_WIDTH, D_MODEL), F32)

    def bag_of(arrs):
        return _pack_bag([fill if a is None else (a[0] if a.ndim > 1 else a[None]) for a in arrs])

    d_bag, m_bag, v_bag = _adamw(bag_of(small_w), bag_sum, bag_of(small_m), bag_of(small_v), BAG_ROWS // 8, "adamw_small")
    d_small = _unpack_bag(d_bag, small_shapes)
    m_small = _unpack_bag(m_bag, small_shapes)
    v_small = _unpack_bag(v_bag, small_shapes)

    loss = bag_sum[sum(BAG_PART_ROWS), 0]
    grad_x = dx.reshape(bsz, seq, D_MODEL)

    def ordered(small, pick):
        s = list(small)
        return [s[0], u_in[pick], u_cw[pick], s[2], s[3], s[4], s[5], s[6], s[7], s[8], s[9],
                u_pl[pick], u_pp[pick], u_out[pick], s[10], u_pg[pick], u_pe[pick], s[11]]

    grads = ordered([g_g1, None, g_cb, g_wa, g_ba, g_wx, g_bx, g_lam, g_pw, g_ps, g_g2, g_gf], 0)
    return (loss, grad_x, *grads, *ordered(d_small, 1), *ordered(m_small, 2), *ordered(v_small, 3))
```

```python
import functools

import jax
import jax.numpy as jnp
from jax import lax
from jax.experimental import pallas as pl
from jax.experimental.pallas import tpu as pltpu

F32 = jnp.float32
BF16 = jnp.bfloat16
MESH = pl.DeviceIdType.MESH
ALL_AXES = ("x", "y", "c")

D_MODEL = 1024
LRU_HEADS = 8
HEAD_DIM = 128
CONV_WIDTH = 4
LRU_C = 8.0
POOL_WIDTH = 512
POOL_WINDOWS = (2, 4, 8, 16)
POOL_GROUP_DIM = 128
IN_COLS = 5120
N_CHIPS = 4
EPS = 1e-6

ADAM_LR = 0.001
ADAM_B1 = 0.9
ADAM_B2 = 0.999
ADAM_EPS = 1e-08
ADAM_WD = 0.01
ADAM_STEP = 10

F32_SUBLANES = 8
CONV_HIST = 8
POOL_HIST = 16
VMEM_LIMIT_BYTES = 58 * 1024 * 1024
VEC_BAG_SLOTS = ("norm_g", "conv_w", "conv_b", "lru_b_a", "lru_b_x", "lru_lambda", "pool_scale", "ple_norm_g",
                 "final_g", "loss")
VEC_BAG_ROWS = 128
MAT_BAG_AT = {"lru_w_a": 0, "lru_w_x": LRU_HEADS * HEAD_DIM, "pool_w": 2 * LRU_HEADS * HEAD_DIM}
MAT_BAG_ROWS = 2 * LRU_HEADS * HEAD_DIM + len(POOL_WINDOWS) * POOL_GROUP_DIM


def _bag_row(name, k=0):
    at = F32_SUBLANES * VEC_BAG_SLOTS.index(name) + k
    return slice(at, at + 1)


def _bag_rows(name):
    at = F32_SUBLANES * VEC_BAG_SLOTS.index(name)
    return slice(at, at + F32_SUBLANES)


def _dot(a, b):
    return jnp.dot(a, b, preferred_element_type=F32)


def _dot_nt(a, b):
    return lax.dot_general(a, b, (((1,), (1,)), ((), ())), preferred_element_type=F32)


def _dot_tn(a, b):
    return lax.dot_general(a, b, (((0,), (0,)), ((), ())), preferred_element_type=F32)


def _sigmoid(v):
    return jax.nn.sigmoid(v)


def _softplus(v):
    return jnp.maximum(v, 0.0) + jnp.log1p(jnp.exp(-jnp.abs(v)))


def _place():
    return lax.axis_index("x"), lax.axis_index("y"), lax.axis_index("c")


GATHER_SEMS = 6


def _gather_shapes(shards):
    out_shape = []
    for arr, axis, _ in shards:
        r, cols = arr.shape
        out_shape.append(jax.ShapeDtypeStruct((N_CHIPS * r, cols) if axis == 0 else (r, N_CHIPS * cols), arr.dtype))
    n = len(shards)
    sems = [pltpu.SemaphoreType.DMA((n * GATHER_SEMS,)), pltpu.SemaphoreType.DMA((n * GATHER_SEMS,)),
            pltpu.SemaphoreType.DMA((n,))]
    return out_shape, sems


def _gather_steps(shards, ins, outs, send_sems, recv_sems, local_sems):
    n = len(shards)
    x, y, c = _place()
    me, sibling = (x, y, c), (x, y, 1 - c)
    chips = [(x, 1 - y), (1 - x, y), (1 - x, 1 - y)]

    def region(k, cx, cy, hc):
        (r, cols), axis = shards[k][0].shape, shards[k][1]
        j = 2 * cx + cy
        if axis == 0:
            if hc is None:
                return outs[k].at[pl.ds(j * r, r), :]
            return outs[k].at[pl.ds(j * r + hc * (r // 2), r // 2), :]
        if hc is None:
            return outs[k].at[:, pl.ds(j * cols, cols)]
        return outs[k].at[pl.ds(hc * (r // 2), r // 2), pl.ds(j * cols, cols)]

    def remote(k, sem, block, to, src=None):
        dst = region(k, *block)
        return pltpu.make_async_remote_copy(
            src_ref=dst if src is None else src, dst_ref=dst,
            send_sem=send_sems.at[k * GATHER_SEMS + sem], recv_sem=recv_sems.at[k * GATHER_SEMS + sem],
            device_id=to, device_id_type=MESH)

    def first(k, idx):
        r, split = shards[k][0].shape[0], shards[k][2]
        src = ins[k].at[pl.ds(c * (r // 2), r // 2), :] if split else ins[k]
        return remote(k, idx, (x, y, c if split else None), (*chips[idx], c), src=src)

    def relay(k):
        src_chip = (jnp.bitwise_xor(x, 1 - c), jnp.bitwise_xor(y, c))
        dst_chip = (jnp.bitwise_xor(x, c), jnp.bitwise_xor(y, 1 - c))
        return remote(k, 2, (*src_chip, c), (*dst_chip, c))

    def passed(k, idx):
        return remote(k, 3 + idx, (*chips[idx], c), sibling)

    def mine(k):
        return pltpu.make_async_copy(ins[k], region(k, x, y, None), local_sems.at[k])

    def start():
        for k in range(n):
            mine(k).start()
            for idx in range(2 if shards[k][2] else 3):
                first(k, idx).start()

    def relay_on():
        for k in range(n):
            split = shards[k][2]
            for idx in range(2):
                remote(k, idx, (*chips[idx], c if split else None), me).wait_recv()
            if split:
                relay(k).start()
                passed(k, 0).start()
                passed(k, 1).start()

    def finish():
        for k in range(n):
            split = shards[k][2]
            remote(k, 2, (*chips[2], c if split else None), me).wait_recv()
            if split:
                passed(k, 2).start()
        for k in range(n):
            if shards[k][2]:
                for idx in range(3):
                    remote(k, 3 + idx, (*chips[idx], 1 - c), me).wait_recv()
        for k in range(n):
            if shards[k][2]:
                for cp in (first(k, 0), first(k, 1), relay(k), passed(k, 0), passed(k, 1), passed(k, 2)):
                    cp.wait_send()
            else:
                for idx in range(3):
                    first(k, idx).wait_send()
            mine(k).wait()

    return start, relay_on, finish


def _gather_shards(shards, name):
    n = len(shards)

    def body(*refs):
        for step in _gather_steps(shards, refs[:n], refs[n:2 * n], *refs[2 * n:]):
            step()

    out_shape, sems = _gather_shapes(shards)
    any_spec = pl.BlockSpec(memory_space=pl.ANY)
    return pl.pallas_call(
        body, name=name, out_shape=tuple(out_shape),
        in_specs=[any_spec] * n, out_specs=tuple([any_spec] * n), scratch_shapes=sems,
    )(*[s[0] for s in shards])


RS_ADD_ROWS = (64, 56, 32, 16, 8)


RS_SEMS = 8
RS_LOCAL_SEMS = 5


def _rs_piece_shape(part):
    arr, cols = part[0], part[1]
    return (arr.shape[0] // 2, arr.shape[1] // N_CHIPS) if cols else tuple(arr.shape[1:])


def _rs_operands(parts):
    return [p[0] for p in parts] + [p[0] if p[2] is None else p[2] for p in parts]


def _rs_shapes(parts, wire):
    n = len(parts)
    shapes = [_rs_piece_shape(p) for p in parts]
    out_shape = [jax.ShapeDtypeStruct((2,) + s, F32) for s in shapes]
    scratch = []
    for lead, dtype in ((N_CHIPS, F32), (N_CHIPS, None), (N_CHIPS, wire), (None, F32), (N_CHIPS, wire)):
        for s, p in zip(shapes, parts):
            narrow = F32 if p[2] is None else p[2].dtype
            scratch.append(pltpu.VMEM(s if lead is None else (lead,) + s, narrow if dtype is None else dtype))
    scratch += [pltpu.SemaphoreType.DMA((n * RS_SEMS,)), pltpu.SemaphoreType.DMA((n * RS_SEMS,)),
                pltpu.SemaphoreType.DMA((n * RS_LOCAL_SEMS,))]
    return out_shape, scratch


def _rs_steps(parts, wire, ins, outs, scratch):
    n = len(parts)
    own, sib, got, fin, snd = (scratch[k * n:(k + 1) * n] for k in range(5))
    send_sems, recv_sems, local_sems = scratch[5 * n:]
    shapes = [_rs_piece_shape(p) for p in parts]
    x, y, c = _place()
    j_me = 2 * x + y
    me, sibling = (x, y, c), (x, y, 1 - c)
    chips = [(x, 1 - y), (1 - x, y), (1 - x, 1 - y)]

    def piece(a, jj, core, narrow=False):
        ref = ins[n + a] if narrow else ins[a]
        if parts[a][1]:
            r, cl = shapes[a]
            return ref.at[pl.ds(core * r, r), pl.ds(jj * cl, cl)]
        return ref.at[2 * jj + core]

    def remote(a, sem, src, dst, to):
        return pltpu.make_async_remote_copy(
            src_ref=src, dst_ref=dst, send_sem=send_sems.at[a * RS_SEMS + sem],
            recv_sem=recv_sems.at[a * RS_SEMS + sem], device_id=to, device_id_type=MESH)

    def rows_loop(a, fn):
        r = shapes[a][0]
        step = max(s for s in RS_ADD_ROWS if r % s == 0)

        def it(i, carry):
            fn(pl.ds(pl.multiple_of(i * step, step), step))
            return carry

        lax.fori_loop(0, r // step, it, 0)

    def load(a, jj):
        return pltpu.make_async_copy(piece(a, jj, c), own[a].at[jj], local_sems.at[a * RS_LOCAL_SEMS + jj])

    def to_sibling(a, jj):
        return remote(a, jj, piece(a, jj, 1 - c, narrow=True), sib[a].at[jj], sibling)

    def to_owner(a, idx):
        chip = chips[idx]
        return remote(a, 4 + idx, snd[a].at[2 * chip[0] + chip[1]], got[a].at[j_me], (*chip, c))

    def store(a):
        return pltpu.make_async_copy(fin[a], outs[a].at[c], local_sems.at[a * RS_LOCAL_SEMS + 4])

    def result_to_sibling(a):
        return remote(a, 7, fin[a], outs[a].at[c], sibling)

    def exchange():
        for a in range(n):
            for jj in range(N_CHIPS):
                load(a, jj).start()
                to_sibling(a, jj).start()

    def chip_sums():
        for a in range(n):
            for jj in range(N_CHIPS):
                load(a, jj).wait()
                remote(a, jj, sib[a].at[jj], sib[a].at[jj], me).wait_recv()

                def add(sl, a=a, jj=jj):
                    q = own[a][jj, sl, :] + sib[a][jj, sl, :].astype(F32)
                    own[a][jj, sl, :] = q
                    snd[a][jj, sl, :] = q.astype(wire)

                rows_loop(a, add)
        for a in range(n):
            for idx in range(3):
                to_owner(a, idx).start()
        for a in range(n):
            def keep(sl, a=a):
                got[a][j_me, sl, :] = snd[a][j_me, sl, :]

            rows_loop(a, keep)

    def totals():
        for a in range(n):
            for idx, chip in enumerate(chips):
                slot = got[a].at[2 * chip[0] + chip[1]]
                remote(a, 4 + idx, slot, slot, me).wait_recv()

            def total(sl, a=a):
                mine = own[a][j_me, sl, :]
                term = [jnp.where(j_me == jj, mine, got[a][jj, sl, :].astype(F32)) for jj in range(N_CHIPS)]
                fin[a][sl, :] = ((term[0] + term[1]) + term[2]) + term[3]

            rows_loop(a, total)
            store(a).start()
            result_to_sibling(a).start()

    def finish():
        for a in range(n):
            remote(a, 7, outs[a].at[1 - c], outs[a].at[1 - c], me).wait_recv()
        for a in range(n):
            for jj in range(N_CHIPS):
                to_sibling(a, jj).wait_send()
            for idx in range(3):
                to_owner(a, idx).wait_send()
            result_to_sibling(a).wait_send()
            store(a).wait()

    return exchange, chip_sums, totals, finish


def _reduce_scatter(parts, name, wire=F32):
    n = len(parts)

    def body(*refs):
        for step in _rs_steps(parts, wire, refs[:2 * n], refs[2 * n:3 * n], refs[3 * n:]):
            step()

    out_shape, scratch = _rs_shapes(parts, wire)
    any_spec = pl.BlockSpec(memory_space=pl.ANY)
    return pl.pallas_call(
        body, name=name, out_shape=tuple(out_shape),
        in_specs=[any_spec] * (2 * n), out_specs=tuple([any_spec] * n), scratch_shapes=scratch,
        compiler_params=pltpu.CompilerParams(vmem_limit_bytes=VMEM_LIMIT_BYTES),
    )(*_rs_operands(parts))


def _rms(x):
    r = lax.rsqrt(jnp.mean(x * x, axis=-1, keepdims=True) + EPS)
    return x * r, r


def _rms_bwd(dxn, xn, r):
    return r * (dxn - xn * jnp.mean(dxn * xn, axis=-1, keepdims=True))


def _in_proj_gather(x2d, norm_g, w_in_sh, shards, tb):
    t = x2d.shape[0]
    nb = t // tb
    cols = IN_COLS // N_CHIPS
    half = D_MODEL // 2
    n = len(shards)

    def body(x_ref, g_ref, win_ref, *refs):
        ins = refs[:n]
        z_ref, h_ref, wfull_ref = refs[n:n + 3]
        outs = refs[n + 3:2 * n + 3]
        wv, h_buf, send_sems, recv_sems, local_sems, w_send, w_recv, w_local = refs[2 * n + 3:]
        s, i = pl.program_id(0), pl.program_id(1)
        x, y, c = _place()
        me, sibling = (x, y, c), (x, y, 1 - c)
        chips = [(x, 1 - y), (1 - x, y), (1 - x, 1 - y)]

        def w_half(cx, cy, hc):
            return wv.at[2 * cx + cy, pl.ds(hc * half, half), :]

        def w_remote(sem, block, to, src=None):
            dst = w_half(*block)
            return pltpu.make_async_remote_copy(
                src_ref=dst if src is None else src, dst_ref=dst, send_sem=w_send.at[sem],
                recv_sem=w_recv.at[sem], device_id=to, device_id_type=MESH)

        def w_first(idx):
            return w_remote(idx, (x, y, c), (*chips[idx], c), src=win_ref.at[pl.ds(c * half, half), :])

        def w_relay():
            src_chip = (jnp.bitwise_xor(x, 1 - c), jnp.bitwise_xor(y, c))
            dst_chip = (jnp.bitwise_xor(x, c), jnp.bitwise_xor(y, 1 - c))
            return w_remote(2, (*src_chip, c), (*dst_chip, c))

        def w_pass(idx):
            return w_remote(3 + idx, (*chips[idx], c), sibling)

        def w_store(k, cx, cy):
            jj = 2 * cx + cy
            return pltpu.make_async_copy(wv.at[jj], wfull_ref.at[:, pl.ds(jj * cols, cols)], w_local.at[k])

        start_rest, relay_rest, finish_rest = _gather_steps(shards, ins, outs, send_sems, recv_sems, local_sems)
        own = pltpu.make_async_copy(win_ref, wv.at[2 * x + y], w_local.at[4])

        @pl.when((s == 0) & (i == 0))
        def _():
            own.start()
            w_first(0).start()
            w_first(1).start()
            start_rest()
            own.wait()
            w_store(0, x, y).start()

        @pl.when((s == 1) & (i == 0))
        def _():
            w_remote(0, (*chips[0], c), me).wait_recv()
            w_remote(1, (*chips[1], c), me).wait_recv()
            w_relay().start()
            w_pass(0).start()
            w_pass(1).start()
            w_remote(3, (*chips[0], 1 - c), me).wait_recv()
            w_store(1, *chips[0]).start()

        @pl.when((s == 2) & (i == 0))
        def _():
            w_remote(4, (*chips[1], 1 - c), me).wait_recv()
            w_store(2, *chips[1]).start()

        @pl.when((s == 3) & (i == 0))
        def _():
            w_remote(2, (*chips[2], c), me).wait_recv()
            w_pass(2).start()
            w_remote(5, (*chips[2], 1 - c), me).wait_recv()
            w_store(3, *chips[2]).start()

        xn, _ = _rms(x_ref[...])
        h = (xn * g_ref[...]).astype(BF16)
        keep_h = pltpu.make_async_copy(h_buf, h_ref.at[pl.ds(pl.multiple_of(i * tb, tb), tb), :], w_local.at[5])

        @pl.when(s == 0)
        def _():
            h_buf[...] = h
            keep_h.start()

        z_ref[...] = _dot(h, wv[jnp.bitwise_xor(2 * x + y, s)])
        pl.when(s == 0)(keep_h.wait)

        @pl.when((s == N_CHIPS - 1) & (i == nb - 1))
        def _():
            relay_rest()
            finish_rest()
            for cp in (w_first(0), w_first(1), w_relay(), w_pass(0), w_pass(1), w_pass(2)):
                cp.wait_send()
            w_store(0, x, y).wait()
            for idx in range(3):
                w_store(idx + 1, *chips[idx]).wait()

    rest_shape, rest_sems = _gather_shapes(shards)
    out_shape = [jax.ShapeDtypeStruct((t, IN_COLS), F32), jax.ShapeDtypeStruct((t, D_MODEL), BF16),
                 jax.ShapeDtypeStruct((D_MODEL, IN_COLS), BF16)] + rest_shape
    any_spec = pl.BlockSpec(memory_space=pl.ANY)

    def z_map(s, i):
        return (i, jnp.bitwise_xor(2 * lax.axis_index("x") + lax.axis_index("y"), s))

    return pl.pallas_call(
        body, name="in_proj", out_shape=tuple(out_shape),
        grid=(N_CHIPS, nb),
        in_specs=[pl.BlockSpec((tb, D_MODEL), lambda s, i: (i, 0)),
                  pl.BlockSpec((1, D_MODEL), lambda s, i: (0, 0)), any_spec] + [any_spec] * n,
        out_specs=tuple([pl.BlockSpec((tb, cols), z_map), any_spec, any_spec] + [any_spec] * n),
        scratch_shapes=[pltpu.VMEM((N_CHIPS, D_MODEL, cols), BF16), pltpu.VMEM((tb, D_MODEL), BF16)] + rest_sems + [
            pltpu.SemaphoreType.DMA((GATHER_SEMS,)), pltpu.SemaphoreType.DMA((GATHER_SEMS,)),
            pltpu.SemaphoreType.DMA((N_CHIPS + 2,))],
        compiler_params=pltpu.CompilerParams(dimension_semantics=("arbitrary", "arbitrary"),
                                             vmem_limit_bytes=VMEM_LIMIT_BYTES),
    )(x2d, norm_g, w_in_sh, *[sh[0] for sh in shards])


def _in_proj_bwd(dz, w_in, x2d, dx_res, norm_g, vec_bag, tb, reduce):
    t = x2d.shape[0]
    parts, wire, steps = reduce
    n = len(parts)

    def body(dz_ref, w_ref, x_ref, dres_ref, g_ref, vec_in_ref, *refs):
        dx_ref, vec_ref = refs[2 * n:2 * n + 2]
        rs = _rs_steps(parts, wire, refs[:2 * n], refs[2 * n + 2:3 * n + 2], refs[3 * n + 2:])
        for step, when in zip(rs, steps):
            pl.when(pl.program_id(0) == when)(step)

        @pl.when(pl.program_id(0) == 0)
        def _():
            vec_ref[...] = vec_in_ref[...]

        xn, r = _rms(x_ref[...])
        g = g_ref[...]
        dh = _dot_nt(dz_ref[...], w_ref[...])
        vec_ref[_bag_row("norm_g"), :] += jnp.sum(dh * xn, axis=0, keepdims=True)
        dx_ref[...] = dres_ref[...] + _rms_bwd(dh * g, xn, r)

    row = lambda i: (i, 0)
    fixed = lambda i: (0, 0)
    rs_shape, rs_scratch = _rs_shapes(parts, wire)
    any_spec = pl.BlockSpec(memory_space=pl.ANY)
    bag_spec = pl.BlockSpec((VEC_BAG_ROWS, D_MODEL), fixed)
    return pl.pallas_call(
        body, name="in_proj_bwd",
        out_shape=tuple([jax.ShapeDtypeStruct((t, D_MODEL), F32), jax.ShapeDtypeStruct((VEC_BAG_ROWS, D_MODEL), F32)]
                        + rs_shape),
        grid=(t // tb,),
        in_specs=[pl.BlockSpec((tb, IN_COLS), row),
                  pl.BlockSpec((D_MODEL, IN_COLS), fixed, pipeline_mode=pl.Buffered(1)),
                  pl.BlockSpec((tb, D_MODEL), row), pl.BlockSpec((tb, D_MODEL), row),
                  pl.BlockSpec((1, D_MODEL), fixed), bag_spec] + [any_spec] * (2 * n),
        out_specs=tuple([pl.BlockSpec((tb, D_MODEL), row), bag_spec] + [any_spec] * n),
        scratch_shapes=rs_scratch, input_output_aliases={5: 1},
        compiler_params=pltpu.CompilerParams(dimension_semantics=("arbitrary",),
                                             vmem_limit_bytes=VMEM_LIMIT_BYTES),
    )(dz, w_in, x2d, dx_res, norm_g, vec_bag, *_rs_operands(parts))


def _weight_grad(lhs, rhs, n_chunks, tb, name, reduce=None):
    t, k = lhs.shape
    nc = rhs.shape[1] // n_chunks
    nb = t // tb
    parts, wire, steps = reduce if reduce is not None else ([], F32, ())
    n = len(parts)

    def body(l_ref, r_ref, *refs):
        o_ref, o16_ref = refs[2 * n:2 * n + 2]
        if n:
            at = pl.program_id(0) * nb + pl.program_id(1)
            rs = _rs_steps(parts, wire, refs[:2 * n], refs[2 * n + 2:3 * n + 2], refs[3 * n + 2:])
            for step, when in zip(rs, steps):
                pl.when(at == when)(step)

        @pl.when(pl.program_id(1) == 0)
        def _():
            o_ref[...] = jnp.zeros_like(o_ref)

        o_ref[...] += _dot_tn(l_ref[...], r_ref[...])

        @pl.when(pl.program_id(1) == nb - 1)
        def _():
            o16_ref[...] = o_ref[...].astype(BF16)

    rs_shape, rs_scratch = _rs_shapes(parts, wire) if n else ([], [])
    any_spec = pl.BlockSpec(memory_space=pl.ANY)
    chunk = pl.BlockSpec((None, k, nc), lambda j, i: (j, 0, 0))
    return pl.pallas_call(
        body, name=name,
        out_shape=tuple([jax.ShapeDtypeStruct((n_chunks, k, nc), F32), jax.ShapeDtypeStruct((n_chunks, k, nc), BF16)]
                        + rs_shape),
        grid=(n_chunks, nb),
        in_specs=[pl.BlockSpec((tb, k), lambda j, i: (i, 0)), pl.BlockSpec((tb, nc), lambda j, i: (i, j))]
        + [any_spec] * (2 * n),
        out_specs=tuple([chunk, chunk] + [any_spec] * n),
        scratch_shapes=rs_scratch,
        compiler_params=pltpu.CompilerParams(dimension_semantics=("arbitrary", "arbitrary"),
                                             vmem_limit_bytes=VMEM_LIMIT_BYTES),
    )(lhs, rhs, *_rs_operands(parts))


def _adam_update(w, g, m, v):
    m_ = ADAM_B1 * m + (1.0 - ADAM_B1) * g
    v_ = ADAM_B2 * v + (1.0 - ADAM_B2) * jnp.square(g)
    m_hat = m_ / (1.0 - ADAM_B1 ** ADAM_STEP)
    v_hat = v_ / (1.0 - ADAM_B2 ** ADAM_STEP)
    return -ADAM_LR * (m_hat / (jnp.sqrt(v_hat) + ADAM_EPS) + ADAM_WD * w), m_, v_


def _adamw_replicated(vec_sum, mat_sum, entries, conv):
    n = len(entries)

    def grad_of(name, shape, vec_ref, mat_ref):
        if name in MAT_BAG_AT:
            return mat_ref[MAT_BAG_AT[name]:MAT_BAG_AT[name] + shape[0], :]
        if shape[0] == 1:
            return vec_ref[_bag_row(name), 0:shape[1]]
        return jnp.concatenate([vec_ref[_bag_row(name), h * shape[1]:(h + 1) * shape[1]] for h in range(shape[0])],
                               axis=0)

    def body(vec_ref, mat_ref, *refs):
        ins, outs = refs[:3 * n + 4], refs[3 * n + 4:]
        for k in range(n):
            w_ref, m_ref, v_ref = ins[3 * k:3 * k + 3]
            g = grad_of(entries[k][0], w_ref.shape, vec_ref, mat_ref)
            d, m_, v_ = _adam_update(w_ref[...], g, m_ref[...], v_ref[...])
            for ref, val in zip(outs[4 * k:4 * k + 4], (g, d, m_, v_)):
                ref[...] = val
        w_ref, m_ref, v_ref, g_ref = ins[3 * n:]
        for ref, val in zip(outs[4 * n:], _adam_update(w_ref[...], g_ref[...], m_ref[...], v_ref[...])):
            ref[...] = val

    arrays = [a for e in entries for a in e[1:]] + list(conv)
    out_shape = [jax.ShapeDtypeStruct(e[1].shape, F32) for e in entries for _ in range(4)]
    out_shape += [jax.ShapeDtypeStruct(conv[0].shape, F32)] * 3
    return pl.pallas_call(
        body, name="adamw_replicated", out_shape=tuple(out_shape),
        compiler_params=pltpu.CompilerParams(vmem_limit_bytes=VMEM_LIMIT_BYTES),
    )(vec_sum, mat_sum, *arrays)


def _adamw(w, g, m, v, rows, name):
    r, c = w.shape

    def body(w_ref, g_ref, m_ref, v_ref, d_ref, nm_ref, nv_ref):
        d_ref[...], nm_ref[...], nv_ref[...] = _adam_update(w_ref[...], g_ref[...], m_ref[...], v_ref[...])

    spec = pl.BlockSpec((rows, c), lambda i: (i, 0))
    return pl.pallas_call(
        body, name=name, out_shape=tuple(jax.ShapeDtypeStruct((r, c), F32) for _ in range(3)),
        grid=(r // rows,), in_specs=[spec] * 4, out_specs=(spec,) * 3,
        compiler_params=pltpu.CompilerParams(dimension_semantics=("arbitrary",),
                                             vmem_limit_bytes=VMEM_LIMIT_BYTES),
    )(w, g, m, v)


def _shift_down(ext, s):
    return pltpu.roll(ext, s, 0)


def _shift_up(ext, s):
    return pltpu.roll(ext, ext.shape[0] - s, 0)


def _lru_gates(xc, wa_ref, ba, wx_ref, bx, lam):
    pa, px = [], []
    for h in range(LRU_HEADS):
        xh = xc[:, h * HEAD_DIM:(h + 1) * HEAD_DIM].astype(BF16)
        pa.append(_dot(xh, wa_ref[h]))
        px.append(_dot(xh, wx_ref[h]))
    r = _sigmoid(jnp.concatenate(pa, axis=1) + ba)
    ig = _sigmoid(jnp.concatenate(px, axis=1) + bx)
    sp = _softplus(-lam)
    log_a = (-LRU_C * r) * sp
    a = jnp.exp(log_a)
    mult = jnp.sqrt(jnp.tanh(-log_a) * (1.0 + a * a))
    return r, ig, a, mult, sp


def _conv(ext, w_ref, b):
    y = b + _shift_down(ext, 3) * w_ref[0:1, :]
    y = y + _shift_down(ext, 2) * w_ref[1:2, :]
    y = y + _shift_down(ext, 1) * w_ref[2:3, :]
    y = y + ext * w_ref[3:4, :]
    return y[CONV_HIST:, :]


def _pool_diff(ext, pos):
    out = []
    for g, k in enumerate(POOL_WINDOWS):
        col = ext[:, g * POOL_GROUP_DIM:(g + 1) * POOL_GROUP_DIM]
        s = col
        for step in range(g + 1):
            s = s + _shift_down(s, 2 ** step)
        count = jnp.minimum(pos + 1, k).astype(F32)
        out.append(s[POOL_HIST:, :] / count - col[POOL_HIST:, :])
    return out


def _pool_mix(diff, pw_ref):
    return jnp.concatenate([_dot(diff[g].astype(BF16), pw_ref[g]) for g in range(len(POOL_WINDOWS))], axis=1)


def _branch_specs(tb, row_map, fixed):
    fixed3 = lambda i: (0, 0, 0)
    return [pl.BlockSpec((CONV_WIDTH, D_MODEL), fixed), pl.BlockSpec((1, D_MODEL), fixed),
            pl.BlockSpec((LRU_HEADS, HEAD_DIM, HEAD_DIM), fixed3), pl.BlockSpec((1, D_MODEL), fixed),
            pl.BlockSpec((LRU_HEADS, HEAD_DIM, HEAD_DIM), fixed3), pl.BlockSpec((1, D_MODEL), fixed),
            pl.BlockSpec((1, D_MODEL), fixed),
            pl.BlockSpec((len(POOL_WINDOWS), POOL_GROUP_DIM, POOL_GROUP_DIM), fixed3),
            pl.BlockSpec((1, POOL_WIDTH), fixed)]


def _branches_fwd(z, weights, seq, tb, shards):
    t = z.shape[0]
    nb = t // tb
    nbe = seq // tb
    groups = tb // F32_SUBLANES
    n = len(shards)

    def body(xa_ref, ga_ref, xb_ref, gb_ref, cw_ref, cb_ref, wa_ref, ba_ref, wx_ref, bx_ref, lam_ref,
             pw_ref, ps_ref, *refs):
        g_ins = refs[:n]
        ya_ref, yb_ref, hl_ref = refs[n:n + 3]
        g_outs = refs[n + 3:2 * n + 3]
        xa_ext, xb_ext, carry, a_s, u_s, send_sems, recv_sems, local_sems = refs[2 * n + 3:]
        blk = pl.program_id(0) % nbe
        start_gather, relay_gather, finish_gather = _gather_steps(shards, g_ins, g_outs, send_sems, recv_sems,
                                                                  local_sems)
        pl.when(pl.program_id(0) == 0)(start_gather)
        pl.when(pl.program_id(0) == nb // 2)(relay_gather)

        @pl.when(blk == 0)
        def _():
            xa_ext[0:CONV_HIST, :] = jnp.zeros((CONV_HIST, D_MODEL), F32)
            xb_ext[0:POOL_HIST, :] = jnp.zeros((POOL_HIST, POOL_WIDTH), F32)
            carry[...] = jnp.zeros_like(carry)

        xa_ext[CONV_HIST:, :] = xa_ref[...]
        xb_ext[POOL_HIST:, :] = xb_ref[...]
        ea = xa_ext[...]
        eb = xb_ext[...]
        xa_ext[0:CONV_HIST, :] = ea[tb:, :]
        xb_ext[0:POOL_HIST, :] = eb[tb:, :]

        xc = _conv(ea, cw_ref, cb_ref[...])
        _, ig, a, mult, _ = _lru_gates(xc, wa_ref, ba_ref[...], wx_ref, bx_ref[...], lam_ref[...])
        u = mult * (ig * xc)
        row8 = lax.broadcasted_iota(jnp.int32, (tb, D_MODEL), 0) % F32_SUBLANES
        for s in (1, 2, 4):
            m = row8 >= s
            u = jnp.where(m, a * _shift_down(u, s) + u, u)
            a = jnp.where(m, a * _shift_down(a, s), a)
        a_s[...] = a
        u_s[...] = u

        def step(g, cr):
            sl = pl.ds(pl.multiple_of(g * F32_SUBLANES, F32_SUBLANES), F32_SUBLANES)
            hb = a_s[sl, :] * cr + u_s[sl, :]
            hl_ref[sl, :] = hb
            return jnp.broadcast_to(hb[F32_SUBLANES - 1:F32_SUBLANES, :], (F32_SUBLANES, D_MODEL))

        carry[...] = lax.fori_loop(0, groups, step, carry[...], unroll=4)
        ga = ga_ref[...]
        ya_ref[...] = (hl_ref[...] * (ga * _sigmoid(ga))).astype(BF16)

        pos = blk * tb + lax.broadcasted_iota(jnp.int32, (tb, POOL_GROUP_DIM), 0)
        ypre = _pool_mix(_pool_diff(eb, pos), pw_ref)
        gb = gb_ref[...]
        yb_ref[...] = ((ypre * ps_ref[...]) * (gb * _sigmoid(gb))).astype(BF16)
        pl.when(pl.program_id(0) == nb - 1)(finish_gather)

    row = lambda i: (i, 0)
    fixed = lambda i: (0, 0)
    any_spec = pl.BlockSpec(memory_space=pl.ANY)
    in_specs = [pl.BlockSpec((tb, D_MODEL), lambda i: (i, 0)), pl.BlockSpec((tb, D_MODEL), lambda i: (i, 1)),
                pl.BlockSpec((tb, POOL_WIDTH), lambda i: (i, 4)), pl.BlockSpec((tb, POOL_WIDTH), lambda i: (i, 5)),
                ] + _branch_specs(tb, row, fixed) + [any_spec] * n
    g_shape, g_sems = _gather_shapes(shards)
    return pl.pallas_call(
        body, name="branches_fwd",
        out_shape=tuple([jax.ShapeDtypeStruct((t, D_MODEL), BF16), jax.ShapeDtypeStruct((t, POOL_WIDTH), BF16),
                         jax.ShapeDtypeStruct((t, D_MODEL), F32)] + g_shape),
        grid=(nb,), in_specs=in_specs,
        out_specs=tuple([pl.BlockSpec((tb, D_MODEL), row), pl.BlockSpec((tb, POOL_WIDTH), row),
                         pl.BlockSpec((tb, D_MODEL), row)] + [any_spec] * n),
        scratch_shapes=[pltpu.VMEM((tb + CONV_HIST, D_MODEL), F32), pltpu.VMEM((tb + POOL_HIST, POOL_WIDTH), F32),
                        pltpu.VMEM((F32_SUBLANES, D_MODEL), F32),
                        pltpu.VMEM((tb, D_MODEL), F32), pltpu.VMEM((tb, D_MODEL), F32)] + g_sems,
        compiler_params=pltpu.CompilerParams(dimension_semantics=("arbitrary",),
                                             vmem_limit_bytes=VMEM_LIMIT_BYTES),
    )(z, z, z, z, *weights, *[sh[0] for sh in shards])


def _branches_bwd(z, hl, dya, dyb, dzm, weights, vec_bag, seq, tb):
    t = z.shape[0]
    nb = t // tb
    nbe = seq // tb
    groups = tb // F32_SUBLANES

    def body(xa_ref, xap_ref, ga_ref, xb_ref, xbp_ref, gb_ref, hl_ref, hlp_ref, dya_ref, dyb_ref, dzm_ref,
             cw_ref, cb_ref, wa_ref, ba_ref, wx_ref, bx_ref, lam_ref, pw_ref, ps_ref, vec_in_ref,
             dz_ref, vec_ref, mat_ref,
             xa_ext, xb_ext, hl_ext, a_ext, dxc_ext, dwin_ext, g_carry, b_s, d_s, g_s):
        i = pl.program_id(0)
        blk = (nb - 1 - i) % nbe

        def mat_rows(name, k):
            at = MAT_BAG_AT[name] + k * HEAD_DIM
            return slice(at, at + HEAD_DIM)

        @pl.when(i == 0)
        def _():
            vec_ref[...] = vec_in_ref[...]
            mat_ref[...] = jnp.zeros_like(mat_ref)

        @pl.when(blk == nbe - 1)
        def _():
            a_ext[tb:, :] = jnp.zeros((F32_SUBLANES, D_MODEL), F32)
            dxc_ext[tb:, :] = jnp.zeros((CONV_HIST, D_MODEL), F32)
            dwin_ext[tb:, :] = jnp.zeros((POOL_HIST, POOL_WIDTH), F32)
            g_carry[...] = jnp.zeros_like(g_carry)

        live = (blk > 0).astype(F32)
        xa_ext[0:CONV_HIST, :] = xap_ref[...] * live
        xa_ext[CONV_HIST:, :] = xa_ref[...]
        xb_ext[0:POOL_HIST, :] = xbp_ref[...] * live
        xb_ext[POOL_HIST:, :] = xb_ref[...]
        hl_ext[0:F32_SUBLANES, :] = hlp_ref[...] * live
        hl_ext[F32_SUBLANES:, :] = hl_ref[...]
        ea = xa_ext[...]
        eb = xb_ext[...]

        xc = _conv(ea, cw_ref, cb_ref[...])
        lam = lam_ref[...]
        r, ig, a, mult, sp = _lru_gates(xc, wa_ref, ba_ref[...], wx_ref, bx_ref[...], lam)
        hl = hl_ref[...]
        ga = ga_ref[...]
        sga = _sigmoid(ga)
        dya = dya_ref[...]
        dhl = dya * (ga * sga)
        dz_ref[:, D_MODEL:2 * D_MODEL] = (dya * hl * (sga * (1.0 + ga * (1.0 - sga)))).astype(BF16)

        a_ext[0:tb, :] = a
        b = _shift_up(a_ext[...], 1)[0:tb, :]
        a_ext[tb:, :] = jnp.broadcast_to(a[0:1, :], (F32_SUBLANES, D_MODEL))
        d = dhl
        row8 = lax.broadcasted_iota(jnp.int32, (tb, D_MODEL), 0) % F32_SUBLANES
        for s in (1, 2, 4):
            m = row8 < F32_SUBLANES - s
            d = jnp.where(m, d + b * _shift_up(d, s), d)
            b = jnp.where(m, b * _shift_up(b, s), b)
        b_s[...] = b
        d_s[...] = d

        def step(k, cr):
            sl = pl.ds(pl.multiple_of((groups - 1 - k) * F32_SUBLANES, F32_SUBLANES), F32_SUBLANES)
            gb_ = d_s[sl, :] + b_s[sl, :] * cr
            g_s[sl, :] = gb_
            return jnp.broadcast_to(gb_[0:1, :], (F32_SUBLANES, D_MODEL))

        g_carry[...] = lax.fori_loop(0, groups, step, g_carry[...], unroll=4)
        gsc = g_s[...]
        da = gsc * _shift_down(hl_ext[...], 1)[F32_SUBLANES:, :]
        dmult = gsc * (ig * xc)
        dig = gsc * (mult * xc)
        dxc = gsc * (mult * ig)
        dlog_a = da * a - (a * a) * dmult / mult
        dr = dlog_a * (-LRU_C * sp)
        vec_ref[_bag_row("lru_lambda"), :] += jnp.sum(dlog_a * (-LRU_C * r), axis=0, keepdims=True)
        dpa = dr * (r * (1.0 - r))
        dpx = dig * (ig * (1.0 - ig))
        vec_ref[_bag_row("lru_b_a"), :] += jnp.sum(dpa, axis=0, keepdims=True)
        vec_ref[_bag_row("lru_b_x"), :] += jnp.sum(dpx, axis=0, keepdims=True)
        back = []
        for h in range(LRU_HEADS):
            cols = slice(h * HEAD_DIM, (h + 1) * HEAD_DIM)
            xh = xc[:, cols].astype(BF16)
            dpa_h = dpa[:, cols].astype(BF16)
            dpx_h = dpx[:, cols].astype(BF16)
            mat_ref[mat_rows("lru_w_a", h), :] += _dot_tn(xh, dpa_h)
            mat_ref[mat_rows("lru_w_x", h), :] += _dot_tn(xh, dpx_h)
            back.append(_dot_nt(dpa_h, wa_ref[h]) + _dot_nt(dpx_h, wx_ref[h]))
        dxc = dxc + jnp.concatenate(back, axis=1)
        vec_ref[_bag_row("conv_b"), :] += jnp.sum(dxc, axis=0, keepdims=True)
        for k in range(CONV_WIDTH):
            tap = _shift_down(ea, CONV_WIDTH - 1 - k)[CONV_HIST:, :] if k < CONV_WIDTH - 1 else ea[CONV_HIST:, :]
            vec_ref[_bag_row("conv_w", k), :] += jnp.sum(dxc * tap, axis=0, keepdims=True)
        dxc_ext[0:tb, :] = dxc
        ed = dxc_ext[...]
        dxa = ed * cw_ref[3:4, :]
        dxa = dxa + _shift_up(ed, 1) * cw_ref[2:3, :]
        dxa = dxa + _shift_up(ed, 2) * cw_ref[1:2, :]
        dxa = dxa + _shift_up(ed, 3) * cw_ref[0:1, :]
        dz_ref[:, 0:D_MODEL] = dxa[0:tb, :].astype(BF16)
        dxc_ext[tb:, :] = dxc[0:CONV_HIST, :]

        pos = blk * tb + lax.broadcasted_iota(jnp.int32, (tb, POOL_GROUP_DIM), 0)
        diff = _pool_diff(eb, pos)
        ypre = _pool_mix(diff, pw_ref)
        ps = ps_ref[...]
        gb = gb_ref[...]
        sgb = _sigmoid(gb)
        dyb = dyb_ref[...]
        dyp = dyb * (gb * sgb)
        dz_ref[:, 2 * D_MODEL + POOL_WIDTH:3 * D_MODEL] = (
            dyb * (ypre * ps) * (sgb * (1.0 + gb * (1.0 - sgb)))).astype(BF16)
        vec_ref[_bag_row("pool_scale"), 0:POOL_WIDTH] += jnp.sum(dyp * ypre, axis=0, keepdims=True)
        dypre = dyp * ps
        for g, k in enumerate(POOL_WINDOWS):
            cols = slice(g * POOL_GROUP_DIM, (g + 1) * POOL_GROUP_DIM)
            dyg = dypre[:, cols].astype(BF16)
            mat_ref[mat_rows("pool_w", g), :] += _dot_tn(diff[g].astype(BF16), dyg)
            ddiff = _dot_nt(dyg, pw_ref[g])
            count = jnp.minimum(pos + 1, k).astype(F32)
            dwin = ddiff / count
            dwin_ext[0:tb, cols] = dwin
            s = dwin_ext[:, cols]
            for step_ in range(g + 1):
                s = s + _shift_up(s, 2 ** step_)
            dz_ref[:, 2 * D_MODEL + g * POOL_GROUP_DIM:2 * D_MODEL + (g + 1) * POOL_GROUP_DIM] = (
                s[0:tb, :] - ddiff).astype(BF16)
            dwin_ext[tb:, cols] = dwin[0:POOL_HIST, :]

        dz_ref[:, 3 * D_MODEL:] = dzm_ref[...]

        @pl.when(i == nb - 1)
        def _():
            row = _bag_row("lru_lambda")
            vec_ref[row, :] = vec_ref[row, :] * (-_sigmoid(-lam))

    rev = lambda i: (nb - 1 - i, 0)
    fixed = lambda i: (0, 0)
    fixed3 = lambda i: (0, 0, 0)

    def prev(rows, col):
        per = tb // rows
        return lambda i: (jnp.maximum((nb - 1 - i) * per - 1, 0), col)

    in_specs = [pl.BlockSpec((tb, D_MODEL), lambda i: (nb - 1 - i, 0)),
                pl.BlockSpec((CONV_HIST, D_MODEL), prev(CONV_HIST, 0)),
                pl.BlockSpec((tb, D_MODEL), lambda i: (nb - 1 - i, 1)),
                pl.BlockSpec((tb, POOL_WIDTH), lambda i: (nb - 1 - i, 4)),
                pl.BlockSpec((POOL_HIST, POOL_WIDTH), prev(POOL_HIST, 4)),
                pl.BlockSpec((tb, POOL_WIDTH), lambda i: (nb - 1 - i, 5)),
                pl.BlockSpec((tb, D_MODEL), rev),
                pl.BlockSpec((F32_SUBLANES, D_MODEL), prev(F32_SUBLANES, 0)),
                pl.BlockSpec((tb, D_MODEL), rev), pl.BlockSpec((tb, POOL_WIDTH), rev),
                pl.BlockSpec((tb, 2 * D_MODEL), rev)] + _branch_specs(tb, rev, fixed) + [
                    pl.BlockSpec((VEC_BAG_ROWS, D_MODEL), fixed)]
    out_shape = (jax.ShapeDtypeStruct((t, IN_COLS), BF16), jax.ShapeDtypeStruct((VEC_BAG_ROWS, D_MODEL), F32),
                 jax.ShapeDtypeStruct((MAT_BAG_ROWS, HEAD_DIM), F32))
    out_specs = (pl.BlockSpec((tb, IN_COLS), rev), pl.BlockSpec((VEC_BAG_ROWS, D_MODEL), fixed),
                 pl.BlockSpec((MAT_BAG_ROWS, HEAD_DIM), fixed))
    scratch = [pltpu.VMEM((tb + CONV_HIST, D_MODEL), F32), pltpu.VMEM((tb + POOL_HIST, POOL_WIDTH), F32),
               pltpu.VMEM((tb + F32_SUBLANES, D_MODEL), F32), pltpu.VMEM((tb + F32_SUBLANES, D_MODEL), F32),
               pltpu.VMEM((tb + CONV_HIST, D_MODEL), F32), pltpu.VMEM((tb + POOL_HIST, POOL_WIDTH), F32),
               pltpu.VMEM((F32_SUBLANES, D_MODEL), F32),
               pltpu.VMEM((tb, D_MODEL), F32), pltpu.VMEM((tb, D_MODEL), F32), pltpu.VMEM((tb, D_MODEL), F32)]
    return pl.pallas_call(
        body, name="branches_bwd", out_shape=out_shape, grid=(nb,), in_specs=in_specs, out_specs=out_specs,
        scratch_shapes=scratch, input_output_aliases={len(in_specs) - 1: 1},
        compiler_params=pltpu.CompilerParams(dimension_semantics=("arbitrary",),
                                             vmem_limit_bytes=VMEM_LIMIT_BYTES),
    )(z, z, z, z, z, z, hl, hl, dya, dyb, dzm, *weights, vec_bag)


def _merge_head(x2d, ya, yb, z, p2d, tgt, w_pl, w_pp, w_out, w_pg, w_pe, g2, gf, tb):
    t = x2d.shape[0]
    p_dim = p2d.shape[1]

    def body(x_ref, ya_ref, yb_ref, ma_ref, mb_ref, p_ref, t_ref, wpl_ref, wpp_ref, wout_ref, wpg_ref, wpe_ref,
             g2_ref, gf_ref,
             bag_ref, dxr_ref, dya_ref, dyb_ref, dzm_ref,
             mg_ref, do_ref, hn_ref, dgp_ref, dpe_ref, da_ref, dbm_ref, pbf_ref):
        @pl.when(pl.program_id(0) == 0)
        def _():
            bag_ref[...] = jnp.zeros_like(bag_ref)

        a_ = _dot(ya_ref[...], wpl_ref[...])
        bm = _dot(yb_ref[...], wpp_ref[...])
        sa = _sigmoid(ma_ref[...])
        sb = _sigmoid(mb_ref[...])
        mg = (sa * a_ + sb * bm).astype(BF16)
        mg_ref[...] = mg
        x1 = x_ref[...] + _dot(mg, wout_ref[...])
        xn2, r2 = _rms(x1)
        g2 = g2_ref[...]
        hn = (xn2 * g2).astype(BF16)
        hn_ref[...] = hn
        gate = _sigmoid(_dot(hn, wpg_ref[...]))
        pbf = p_ref[...].astype(BF16)
        pbf_ref[...] = pbf
        pe = _dot(pbf, wpe_ref[...])
        x2 = x1 + gate * pe
        xn3, r3 = _rms(x2)
        gf = gf_ref[...]
        err = xn3 * gf - t_ref[...]
        bag_ref[_bag_rows("loss"), 0:128] += 0.5 * jnp.sum(jnp.mean(err * err, axis=-1))

        dy = err * (1.0 / D_MODEL)
        bag_ref[_bag_row("final_g"), :] += jnp.sum(dy * xn3, axis=0, keepdims=True)
        dx2 = _rms_bwd(dy * gf, xn3, r3)
        dpe_ref[...] = (dx2 * gate).astype(BF16)
        dgp = ((dx2 * pe) * (gate * (1.0 - gate))).astype(BF16)
        dgp_ref[...] = dgp
        dhn = _dot_nt(dgp, wpg_ref[...])
        bag_ref[_bag_row("ple_norm_g"), :] += jnp.sum(dhn * xn2, axis=0, keepdims=True)
        dx1 = dx2 + _rms_bwd(dhn * g2, xn2, r2)
        dxr_ref[...] = dx1
        do = dx1.astype(BF16)
        do_ref[...] = do
        dmg = _dot_nt(do, wout_ref[...])
        da = (dmg * sa).astype(BF16)
        dbm = (dmg * sb).astype(BF16)
        da_ref[...] = da
        dbm_ref[...] = dbm
        dzm_ref[:, 0:D_MODEL] = (dmg * a_ * (sa * (1.0 - sa))).astype(BF16)
        dzm_ref[:, D_MODEL:] = (dmg * bm * (sb * (1.0 - sb))).astype(BF16)
        dya_ref[...] = _dot_nt(da, wpl_ref[...])
        dyb_ref[...] = _dot_nt(dbm, wpp_ref[...])

    row = lambda i: (i, 0)
    fixed = lambda i: (0, 0)

    def resident(shape):
        return pl.BlockSpec(shape, fixed, pipeline_mode=pl.Buffered(1))

    tok = lambda width: pl.BlockSpec((tb, width), row)
    in_specs = [tok(D_MODEL), tok(D_MODEL), tok(POOL_WIDTH),
                pl.BlockSpec((tb, D_MODEL), lambda i: (i, 3)), pl.BlockSpec((tb, D_MODEL), lambda i: (i, 4)),
                tok(p_dim), tok(D_MODEL),
                resident((D_MODEL, D_MODEL)), resident((POOL_WIDTH, D_MODEL)), resident((D_MODEL, D_MODEL)),
                resident((D_MODEL, D_MODEL)), resident((p_dim, D_MODEL)),
                pl.BlockSpec((1, D_MODEL), fixed), pl.BlockSpec((1, D_MODEL), fixed)]
    bf = lambda width: jax.ShapeDtypeStruct((t, width), BF16)
    f32 = lambda width: jax.ShapeDtypeStruct((t, width), F32)
    out_shape = (jax.ShapeDtypeStruct((VEC_BAG_ROWS, D_MODEL), F32),
                 f32(D_MODEL), f32(D_MODEL), f32(POOL_WIDTH), bf(2 * D_MODEL),
                 bf(D_MODEL), bf(D_MODEL), bf(D_MODEL), bf(D_MODEL), bf(D_MODEL), bf(D_MODEL), bf(D_MODEL), bf(p_dim))
    out_specs = (pl.BlockSpec((VEC_BAG_ROWS, D_MODEL), fixed),
                 tok(D_MODEL), tok(D_MODEL), tok(POOL_WIDTH), tok(2 * D_MODEL),
                 tok(D_MODEL), tok(D_MODEL), tok(D_MODEL), tok(D_MODEL), tok(D_MODEL), tok(D_MODEL), tok(D_MODEL),
                 tok(p_dim))
    return pl.pallas_call(
        body, name="merge_head", out_shape=out_shape, grid=(t // tb,), in_specs=in_specs, out_specs=out_specs,
        compiler_params=pltpu.CompilerParams(dimension_semantics=("arbitrary",),
                                             vmem_limit_bytes=VMEM_LIMIT_BYTES),
    )(x2d, ya, yb, z, z, p2d, tgt, w_pl, w_pp, w_out, w_pg, w_pe, g2, gf)


def kernel(x, p, norm_g, w_in, conv_w, conv_b, lru_w_a, lru_b_a, lru_w_x, lru_b_x, lru_lambda, pool_w, pool_scale, w_proj_lru, w_proj_pool, w_out, ple_norm_g, w_ple_gate, w_ple_proj, final_g, loss_target, m_norm_g, m_w_in, m_conv_w, m_conv_b, m_lru_w_a, m_lru_b_a, m_lru_w_x, m_lru_b_x, m_lru_lambda, m_pool_w, m_pool_scale, m_w_proj_lru, m_w_proj_pool, m_w_out, m_ple_norm_g, m_w_ple_gate, m_w_ple_proj, m_final_g, v_norm_g, v_w_in, v_conv_w, v_conv_b, v_lru_w_a, v_lru_b_a, v_lru_w_x, v_lru_b_x, v_lru_lambda, v_pool_w, v_pool_scale, v_w_proj_lru, v_w_proj_pool, v_w_out, v_ple_norm_g, v_w_ple_gate, v_w_ple_proj, v_final_g):
    bsz, seq, _ = x.shape
    t = bsz * seq
    tb_mm = min(512, seq)
    tb_seq = min(256, seq // 2) if seq >= 512 else seq
    x2d = x.reshape(t, D_MODEL)
    p2d = p.reshape(t, p.shape[-1])
    tgt = loss_target.reshape(t, D_MODEL)
    chip = 2 * lax.axis_index("x") + lax.axis_index("y")

    rest = [(w_proj_lru[0], 0), (w_proj_pool[0], 1), (w_out[0], 0), (w_ple_gate[0], 0), (w_ple_proj[0], 1)]
    z, h_bf, w_in_f, conv_w_f = _in_proj_gather(x2d, norm_g, w_in[0].astype(BF16), [(conv_w[0], 1, False)], tb_mm)

    wa_bf = lru_w_a[0].astype(BF16)
    wx_bf = lru_w_x[0].astype(BF16)
    pw_bf = pool_w[0].astype(BF16)
    branch_w = (conv_w_f, conv_b, wa_bf, lru_b_a.reshape(1, D_MODEL), wx_bf, lru_b_x.reshape(1, D_MODEL),
                lru_lambda, pw_bf, pool_scale)

    ya, yb, hl, w_pl_f, w_pp_f, w_out_f, w_pg_f, w_pe_f = _branches_fwd(
        z, branch_w, seq, tb_seq, [(w.astype(BF16), axis, True) for w, axis in rest])
    (vec_bag, dx_res, dya, dyb, dzm, mg_bf, do_bf, hn_bf, dgp_bf, dpe_bf, da_bf, dbm_bf, p_bf) = _merge_head(
        x2d, ya, yb, z, p2d, tgt, w_pl_f, w_pp_f, w_out_f, w_pg_f, w_pe_f, ple_norm_g, final_g.reshape(1, D_MODEL),
        tb_seq)
    dz, vec_bag, mat_bag = _branches_bwd(z, hl, dya, dyb, dzm, branch_w, vec_bag, seq, tb_seq)

    g_pl = _weight_grad(ya, da_bf, 1, tb_mm, "dw_proj_lru")[0].reshape(8, D_MODEL // 8, D_MODEL)
    g_pp = _weight_grad(yb, dbm_bf, 1, tb_mm, "dw_proj_pool")[0][0]
    g_out = _weight_grad(mg_bf, do_bf, 1, tb_mm, "dw_out")[0].reshape(8, D_MODEL // 8, D_MODEL)
    g_pg = _weight_grad(hn_bf, dgp_bf, 1, tb_mm, "dw_ple_gate")[0].reshape(8, D_MODEL // 8, D_MODEL)
    p_dim = p2d.shape[1]
    g_pe = _weight_grad(p_bf, dpe_bf, 1, tb_mm, "dw_ple_proj")[0][0]
    nb_mm = t // tb_mm
    g_in, g_in16, r_pl, r_pp, r_out, r_pg, r_pe = _weight_grad(
        h_bf, dz, N_CHIPS, tb_mm, "dw_in",
        reduce=([(g_pl, False, None), (g_pp, True, None), (g_out, False, None), (g_pg, False, None),
                 (g_pe, True, None)], BF16, (0, nb_mm, 3 * nb_mm + nb_mm // 2, N_CHIPS * nb_mm - 1)))
    pieces = (8, D_MODEL // 2, IN_COLS // N_CHIPS)
    nb_seq = t // tb_seq
    dx, vec_bag, r_in = _in_proj_bwd(
        dz, w_in_f, x2d, dx_res, norm_g, vec_bag, tb_seq,
        reduce=([(g_in.reshape(pieces), False, g_in16.reshape(pieces))], BF16, (0, nb_seq // 4, nb_seq - 1, nb_seq - 1)))
    vec_mine, mat_mine = _reduce_scatter(
        [(vec_bag.reshape(8, VEC_BAG_ROWS // 8, D_MODEL), False, None),
         (mat_bag.reshape(8, MAT_BAG_ROWS // 8, HEAD_DIM), False, None)], "rs_small")
    vec_sum, mat_sum = _gather_shards(
        [(vec_mine.reshape(VEC_BAG_ROWS // N_CHIPS, D_MODEL), 0, True),
         (mat_mine.reshape(MAT_BAG_ROWS // N_CHIPS, HEAD_DIM), 0, True)], "gather_small")

    def big_update(w, g2d, m, v, rows, name):
        d, nm, nv = _adamw(w[0], g2d, m[0], v[0], rows, name)
        return g2d[None], d[None], nm[None], nv[None]

    u_in = big_update(w_in, r_in.reshape(D_MODEL, IN_COLS // N_CHIPS), m_w_in, v_w_in, 256, "adamw_w_in")
    u_pl = big_update(w_proj_lru, r_pl.reshape(D_MODEL // N_CHIPS, D_MODEL), m_w_proj_lru, v_w_proj_lru, 256, "adamw_w_proj_lru")
    u_pp = big_update(w_proj_pool, r_pp.reshape(POOL_WIDTH, D_MODEL // N_CHIPS), m_w_proj_pool, v_w_proj_pool, 512, "adamw_w_proj_pool")
    u_out = big_update(w_out, r_out.reshape(D_MODEL // N_CHIPS, D_MODEL), m_w_out, v_w_out, 256, "adamw_w_out")
    u_pg = big_update(w_ple_gate, r_pg.reshape(D_MODEL // N_CHIPS, D_MODEL), m_w_ple_gate, v_w_ple_gate, 256, "adamw_w_ple_gate")
    u_pe = big_update(w_ple_proj, r_pe.reshape(p_dim, D_MODEL // N_CHIPS), m_w_ple_proj, v_w_ple_proj, 256, "adamw_w_ple_proj")

    small = [("norm_g", norm_g, m_norm_g, v_norm_g), ("conv_b", conv_b, m_conv_b, v_conv_b),
             ("lru_w_a", lru_w_a, m_lru_w_a, v_lru_w_a), ("lru_b_a", lru_b_a, m_lru_b_a, v_lru_b_a),
             ("lru_w_x", lru_w_x, m_lru_w_x, v_lru_w_x), ("lru_b_x", lru_b_x, m_lru_b_x, v_lru_b_x),
             ("lru_lambda", lru_lambda, m_lru_lambda, v_lru_lambda), ("pool_w", pool_w, m_pool_w, v_pool_w),
             ("pool_scale", pool_scale, m_pool_scale, v_pool_scale),
             ("ple_norm_g", ple_norm_g, m_ple_norm_g, v_ple_norm_g), ("final_g", final_g, m_final_g, v_final_g)]

    def view(a):
        return a.reshape(-1, a.shape[-1]) if a.ndim != 3 else a[0]

    cw_at = F32_SUBLANES * VEC_BAG_SLOTS.index("conv_w")
    cw_cols = D_MODEL // N_CHIPS
    g_cw = lax.dynamic_slice(vec_sum, (cw_at, chip * cw_cols), (CONV_WIDTH, cw_cols))
    flat = _adamw_replicated(vec_sum, mat_sum, [(name,) + tuple(view(a) for a in arrs) for name, *arrs in small],
                             (conv_w[0], m_conv_w[0], v_conv_w[0], g_cw))
    u_small = {name: tuple(flat[4 * k + pick].reshape(arrs[0].shape) for pick in range(4))
               for k, (name, *arrs) in enumerate(small)}
    u_cw = tuple(a[None] for a in (g_cw,) + tuple(flat[4 * len(small):]))

    loss = vec_sum[F32_SUBLANES * VEC_BAG_SLOTS.index("loss"), 0]
    grad_x = dx.reshape(bsz, seq, D_MODEL)

    def ordered(pick):
        s = {name: u[pick] for name, u in u_small.items()}
        return [s["norm_g"], u_in[pick], u_cw[pick], s["conv_b"], s["lru_w_a"], s["lru_b_a"], s["lru_w_x"], s["lru_b_x"],
                s["lru_lambda"], s["pool_w"], s["pool_scale"], u_pl[pick], u_pp[pick], u_out[pick], s["ple_norm_g"],
                u_pg[pick], u_pe[pick], s["final_g"]]

    return (loss, grad_x, *ordered(0), *ordered(1), *ordered(2), *ordered(3))
```

```python
import jax
import jax.numpy as jnp
from jax import lax
from jax.experimental import pallas as pl
from jax.experimental.pallas import tpu as pltpu

F32 = jnp.float32
BF16 = jnp.bfloat16
MESH = pl.DeviceIdType.MESH

D_MODEL = 1024
LRU_HEADS = 8
HEAD_DIM = 128
CONV_WIDTH = 4
LRU_C = 8.0
POOL_WIDTH = 512
POOL_WINDOWS = (2, 4, 8, 16)
POOL_GROUP_DIM = 128
IN_COLS = 5120
N_CHIPS = 4
EPS = 1e-6

ADAM_LR = 0.001
ADAM_B1 = 0.9
ADAM_B2 = 0.999
ADAM_EPS = 1e-08
ADAM_WD = 0.01
ADAM_STEP = 10

F32_SUBLANES = 8
CONV_HIST = 8
POOL_HIST = 16
VMEM_LIMIT_BYTES = 58 * 1024 * 1024
VEC_BAG_SLOTS = ("norm_g", "conv_w", "conv_b", "lru_b_a", "lru_b_x", "lru_lambda", "pool_scale", "ple_norm_g",
                 "final_g", "loss")
VEC_BAG_ROWS = 128
MAT_BAG_AT = {"lru_w_a": 0, "lru_w_x": LRU_HEADS * HEAD_DIM, "pool_w": 2 * LRU_HEADS * HEAD_DIM}
MAT_BAG_ROWS = 2 * LRU_HEADS * HEAD_DIM + len(POOL_WINDOWS) * POOL_GROUP_DIM


def _bag_row(name, k=0):
    at = F32_SUBLANES * VEC_BAG_SLOTS.index(name) + k
    return slice(at, at + 1)


def _bag_rows(name):
    at = F32_SUBLANES * VEC_BAG_SLOTS.index(name)
    return slice(at, at + F32_SUBLANES)


def _dot(a, b):
    return jnp.dot(a, b, preferred_element_type=F32)


def _dot_nt(a, b):
    return lax.dot_general(a, b, (((1,), (1,)), ((), ())), preferred_element_type=F32)


def _dot_tn(a, b):
    return lax.dot_general(a, b, (((0,), (0,)), ((), ())), preferred_element_type=F32)


def _sigmoid(v):
    return jax.nn.sigmoid(v)


def _softplus(v):
    return jnp.maximum(v, 0.0) + jnp.log1p(jnp.exp(-jnp.abs(v)))


def _place():
    return lax.axis_index("x"), lax.axis_index("y"), lax.axis_index("c")


GATHER_SEMS = 6


def _gather_shapes(shards):
    out_shape = []
    for arr, axis, _ in shards:
        r, cols = arr.shape
        out_shape.append(jax.ShapeDtypeStruct((N_CHIPS * r, cols) if axis == 0 else (r, N_CHIPS * cols), arr.dtype))
    n = len(shards)
    sems = [pltpu.SemaphoreType.DMA((n * GATHER_SEMS,)), pltpu.SemaphoreType.DMA((n * GATHER_SEMS,)),
            pltpu.SemaphoreType.DMA((n,))]
    return out_shape, sems


def _gather_steps(shards, ins, outs, send_sems, recv_sems, local_sems):
    n = len(shards)
    x, y, c = _place()
    me, sibling = (x, y, c), (x, y, 1 - c)
    chips = [(x, 1 - y), (1 - x, y), (1 - x, 1 - y)]

    def region(k, cx, cy, hc):
        (r, cols), axis = shards[k][0].shape, shards[k][1]
        j = 2 * cx + cy
        if axis == 0:
            if hc is None:
                return outs[k].at[pl.ds(j * r, r), :]
            return outs[k].at[pl.ds(j * r + hc * (r // 2), r // 2), :]
        if hc is None:
            return outs[k].at[:, pl.ds(j * cols, cols)]
        return outs[k].at[pl.ds(hc * (r // 2), r // 2), pl.ds(j * cols, cols)]

    def remote(k, sem, block, to, src=None):
        dst = region(k, *block)
        return pltpu.make_async_remote_copy(
            src_ref=dst if src is None else src, dst_ref=dst,
            send_sem=send_sems.at[k * GATHER_SEMS + sem], recv_sem=recv_sems.at[k * GATHER_SEMS + sem],
            device_id=to, device_id_type=MESH)

    def first(k, idx):
        r, split = shards[k][0].shape[0], shards[k][2]
        src = ins[k].at[pl.ds(c * (r // 2), r // 2), :] if split else ins[k]
        return remote(k, idx, (x, y, c if split else None), (*chips[idx], c), src=src)

    def relay(k):
        src_chip = (jnp.bitwise_xor(x, 1 - c), jnp.bitwise_xor(y, c))
        dst_chip = (jnp.bitwise_xor(x, c), jnp.bitwise_xor(y, 1 - c))
        return remote(k, 2, (*src_chip, c), (*dst_chip, c))

    def passed(k, idx):
        return remote(k, 3 + idx, (*chips[idx], c), sibling)

    def mine(k):
        return pltpu.make_async_copy(ins[k], region(k, x, y, None), local_sems.at[k])

    def start():
        for k in range(n):
            mine(k).start()
            for idx in range(2 if shards[k][2] else 3):
                first(k, idx).start()

    def relay_on():
        for k in range(n):
            split = shards[k][2]
            for idx in range(2):
                remote(k, idx, (*chips[idx], c if split else None), me).wait_recv()
            if split:
                relay(k).start()
                passed(k, 0).start()
                passed(k, 1).start()

    def finish():
        for k in range(n):
            split = shards[k][2]
            remote(k, 2, (*chips[2], c if split else None), me).wait_recv()
            if split:
                passed(k, 2).start()
        for k in range(n):
            if shards[k][2]:
                for idx in range(3):
                    remote(k, 3 + idx, (*chips[idx], 1 - c), me).wait_recv()
        for k in range(n):
            if shards[k][2]:
                for cp in (first(k, 0), first(k, 1), relay(k), passed(k, 0), passed(k, 1), passed(k, 2)):
                    cp.wait_send()
            else:
                for idx in range(3):
                    first(k, idx).wait_send()
            mine(k).wait()

    return start, relay_on, finish


def _gather_shards(shards, name):
    n = len(shards)

    def body(*refs):
        for step in _gather_steps(shards, refs[:n], refs[n:2 * n], *refs[2 * n:]):
            step()

    out_shape, sems = _gather_shapes(shards)
    any_spec = pl.BlockSpec(memory_space=pl.ANY)
    return pl.pallas_call(
        body, name=name, out_shape=tuple(out_shape),
        in_specs=[any_spec] * n, out_specs=tuple([any_spec] * n), scratch_shapes=sems,
    )(*[s[0] for s in shards])


RS_ADD_ROWS = (64, 56, 32, 16, 8)


RS_SEMS = 8
RS_LOCAL_SEMS = 5


def _rs_piece_shape(part):
    arr, cols = part[0], part[1]
    return (arr.shape[0] // 2, arr.shape[1] // N_CHIPS) if cols else tuple(arr.shape[1:])


def _rs_operands(parts):
    return [p[0] for p in parts] + [p[0] if p[2] is None else p[2] for p in parts]


def _rs_shapes(parts, wire):
    n = len(parts)
    shapes = [_rs_piece_shape(p) for p in parts]
    out_shape = [jax.ShapeDtypeStruct((2,) + s, F32) for s in shapes]
    scratch = []
    for lead, dtype in ((N_CHIPS, F32), (N_CHIPS, None), (N_CHIPS, wire), (None, F32), (N_CHIPS, wire)):
        for s, p in zip(shapes, parts):
            narrow = F32 if p[2] is None else p[2].dtype
            scratch.append(pltpu.VMEM(s if lead is None else (lead,) + s, narrow if dtype is None else dtype))
    scratch += [pltpu.SemaphoreType.DMA((n * RS_SEMS,)), pltpu.SemaphoreType.DMA((n * RS_SEMS,)),
                pltpu.SemaphoreType.DMA((n * RS_LOCAL_SEMS,))]
    return out_shape, scratch


def _rs_steps(parts, wire, ins, outs, scratch):
    n = len(parts)
    own, sib, got, fin, snd = (scratch[k * n:(k + 1) * n] for k in range(5))
    send_sems, recv_sems, local_sems = scratch[5 * n:]
    shapes = [_rs_piece_shape(p) for p in parts]
    x, y, c = _place()
    j_me = 2 * x + y
    me, sibling = (x, y, c), (x, y, 1 - c)
    chips = [(x, 1 - y), (1 - x, y), (1 - x, 1 - y)]

    def piece(a, jj, core, narrow=False):
        ref = ins[n + a] if narrow else ins[a]
        if parts[a][1]:
            r, cl = shapes[a]
            return ref.at[pl.ds(core * r, r), pl.ds(jj * cl, cl)]
        return ref.at[2 * jj + core]

    def remote(a, sem, src, dst, to):
        return pltpu.make_async_remote_copy(
            src_ref=src, dst_ref=dst, send_sem=send_sems.at[a * RS_SEMS + sem],
            recv_sem=recv_sems.at[a * RS_SEMS + sem], device_id=to, device_id_type=MESH)

    def rows_loop(a, fn):
        r = shapes[a][0]
        step = max(s for s in RS_ADD_ROWS if r % s == 0)

        def it(i, carry):
            fn(pl.ds(pl.multiple_of(i * step, step), step))
            return carry

        lax.fori_loop(0, r // step, it, 0)

    def load(a, jj):
        return pltpu.make_async_copy(piece(a, jj, c), own[a].at[jj], local_sems.at[a * RS_LOCAL_SEMS + jj])

    def to_sibling(a, jj):
        return remote(a, jj, piece(a, jj, 1 - c, narrow=True), sib[a].at[jj], sibling)

    def to_owner(a, idx):
        chip = chips[idx]
        return remote(a, 4 + idx, snd[a].at[2 * chip[0] + chip[1]], got[a].at[j_me], (*chip, c))

    def store(a):
        return pltpu.make_async_copy(fin[a], outs[a].at[c], local_sems.at[a * RS_LOCAL_SEMS + 4])

    def result_to_sibling(a):
        return remote(a, 7, fin[a], outs[a].at[c], sibling)

    def exchange():
        for a in range(n):
            for jj in range(N_CHIPS):
                load(a, jj).start()
                to_sibling(a, jj).start()

    def chip_sums():
        for a in range(n):
            for jj in range(N_CHIPS):
                load(a, jj).wait()
                remote(a, jj, sib[a].at[jj], sib[a].at[jj], me).wait_recv()

                def add(sl, a=a, jj=jj):
                    q = own[a][jj, sl, :] + sib[a][jj, sl, :].astype(F32)
                    own[a][jj, sl, :] = q
                    snd[a][jj, sl, :] = q.astype(wire)

                rows_loop(a, add)
        for a in range(n):
            for idx in range(3):
                to_owner(a, idx).start()
        for a in range(n):
            def keep(sl, a=a):
                got[a][j_me, sl, :] = snd[a][j_me, sl, :]

            rows_loop(a, keep)

    def totals():
        for a in range(n):
            for idx, chip in enumerate(chips):
                slot = got[a].at[2 * chip[0] + chip[1]]
                remote(a, 4 + idx, slot, slot, me).wait_recv()

            def total(sl, a=a):
                mine = own[a][j_me, sl, :]
                term = [jnp.where(j_me == jj, mine, got[a][jj, sl, :].astype(F32)) for jj in range(N_CHIPS)]
                fin[a][sl, :] = ((term[0] + term[1]) + term[2]) + term[3]

            rows_loop(a, total)
            store(a).start()
            result_to_sibling(a).start()

    def finish():
        for a in range(n):
            remote(a, 7, outs[a].at[1 - c], outs[a].at[1 - c], me).wait_recv()
        for a in range(n):
            for jj in range(N_CHIPS):
                to_sibling(a, jj).wait_send()
            for idx in range(3):
                to_owner(a, idx).wait_send()
            result_to_sibling(a).wait_send()
            store(a).wait()

    return exchange, chip_sums, totals, finish


def _reduce_scatter(parts, name, wire=F32):
    n = len(parts)

    def body(*refs):
        for step in _rs_steps(parts, wire, refs[:2 * n], refs[2 * n:3 * n], refs[3 * n:]):
            step()

    out_shape, scratch = _rs_shapes(parts, wire)
    any_spec = pl.BlockSpec(memory_space=pl.ANY)
    return pl.pallas_call(
        body, name=name, out_shape=tuple(out_shape),
        in_specs=[any_spec] * (2 * n), out_specs=tuple([any_spec] * n), scratch_shapes=scratch,
        compiler_params=pltpu.CompilerParams(vmem_limit_bytes=VMEM_LIMIT_BYTES),
    )(*_rs_operands(parts))


def _rms(x):
    r = lax.rsqrt(jnp.mean(x * x, axis=-1, keepdims=True) + EPS)
    return x * r, r


def _rms_bwd(dxn, xn, r):
    return r * (dxn - xn * jnp.mean(dxn * xn, axis=-1, keepdims=True))


def _in_proj_gather(x2d, norm_g, w_in_sh, shards, tb):
    t = x2d.shape[0]
    nb = t // tb
    cols = IN_COLS // N_CHIPS
    half = D_MODEL // 2
    n = len(shards)

    def body(x_ref, g_ref, win_ref, *refs):
        ins = refs[:n]
        z_ref, h_ref, wfull_ref = refs[n:n + 3]
        outs = refs[n + 3:2 * n + 3]
        wv, h_buf, send_sems, recv_sems, local_sems, w_send, w_recv, w_local = refs[2 * n + 3:]
        s, i = pl.program_id(0), pl.program_id(1)
        x, y, c = _place()
        me, sibling = (x, y, c), (x, y, 1 - c)
        chips = [(x, 1 - y), (1 - x, y), (1 - x, 1 - y)]

        def w_half(cx, cy, hc):
            return wv.at[2 * cx + cy, pl.ds(hc * half, half), :]

        def w_remote(sem, block, to, src=None):
            dst = w_half(*block)
            return pltpu.make_async_remote_copy(
                src_ref=dst if src is None else src, dst_ref=dst, send_sem=w_send.at[sem],
                recv_sem=w_recv.at[sem], device_id=to, device_id_type=MESH)

        def w_first(idx):
            return w_remote(idx, (x, y, c), (*chips[idx], c), src=win_ref.at[pl.ds(c * half, half), :])

        def w_relay():
            src_chip = (jnp.bitwise_xor(x, 1 - c), jnp.bitwise_xor(y, c))
            dst_chip = (jnp.bitwise_xor(x, c), jnp.bitwise_xor(y, 1 - c))
            return w_remote(2, (*src_chip, c), (*dst_chip, c))

        def w_pass(idx):
            return w_remote(3 + idx, (*chips[idx], c), sibling)

        def w_store(k, cx, cy):
            jj = 2 * cx + cy
            return pltpu.make_async_copy(wv.at[jj], wfull_ref.at[:, pl.ds(jj * cols, cols)], w_local.at[k])

        start_rest, relay_rest, finish_rest = _gather_steps(shards, ins, outs, send_sems, recv_sems, local_sems)
        own = pltpu.make_async_copy(win_ref, wv.at[2 * x + y], w_local.at[4])

        @pl.when((s == 0) & (i == 0))
        def _():
            own.start()
            w_first(0).start()
            w_first(1).start()
            start_rest()
            own.wait()
            w_store(0, x, y).start()

        @pl.when((s == 1) & (i == 0))
        def _():
            w_remote(0, (*chips[0], c), me).wait_recv()
            w_remote(1, (*chips[1], c), me).wait_recv()
            w_relay().start()
            w_pass(0).start()
            w_pass(1).start()
            w_remote(3, (*chips[0], 1 - c), me).wait_recv()
            w_store(1, *chips[0]).start()

        @pl.when((s == 2) & (i == 0))
        def _():
            w_remote(4, (*chips[1], 1 - c), me).wait_recv()
            w_store(2, *chips[1]).start()

        @pl.when((s == 3) & (i == 0))
        def _():
            w_remote(2, (*chips[2], c), me).wait_recv()
            w_pass(2).start()
            w_remote(5, (*chips[2], 1 - c), me).wait_recv()
            w_store(3, *chips[2]).start()

        xn, _ = _rms(x_ref[...])
        h = (xn * g_ref[...]).astype(BF16)
        keep_h = pltpu.make_async_copy(h_buf, h_ref.at[pl.ds(pl.multiple_of(i * tb, tb), tb), :], w_local.at[5])

        @pl.when(s == 0)
        def _():
            h_buf[...] = h
            keep_h.start()

        z_ref[...] = _dot(h, wv[jnp.bitwise_xor(2 * x + y, s)])
        pl.when(s == 0)(keep_h.wait)

        @pl.when((s == N_CHIPS - 1) & (i == nb - 1))
        def _():
            relay_rest()
            finish_rest()
            for cp in (w_first(0), w_first(1), w_relay(), w_pass(0), w_pass(1), w_pass(2)):
                cp.wait_send()
            w_store(0, x, y).wait()
            for idx in range(3):
                w_store(idx + 1, *chips[idx]).wait()

    rest_shape, rest_sems = _gather_shapes(shards)
    out_shape = [jax.ShapeDtypeStruct((t, IN_COLS), F32), jax.ShapeDtypeStruct((t, D_MODEL), BF16),
                 jax.ShapeDtypeStruct((D_MODEL, IN_COLS), BF16)] + rest_shape
    any_spec = pl.BlockSpec(memory_space=pl.ANY)

    def z_map(s, i):
        return (i, jnp.bitwise_xor(2 * lax.axis_index("x") + lax.axis_index("y"), s))

    return pl.pallas_call(
        body, name="in_proj", out_shape=tuple(out_shape),
        grid=(N_CHIPS, nb),
        in_specs=[pl.BlockSpec((tb, D_MODEL), lambda s, i: (i, 0)),
                  pl.BlockSpec((1, D_MODEL), lambda s, i: (0, 0)), any_spec] + [any_spec] * n,
        out_specs=tuple([pl.BlockSpec((tb, cols), z_map), any_spec, any_spec] + [any_spec] * n),
        scratch_shapes=[pltpu.VMEM((N_CHIPS, D_MODEL, cols), BF16), pltpu.VMEM((tb, D_MODEL), BF16)] + rest_sems + [
            pltpu.SemaphoreType.DMA((GATHER_SEMS,)), pltpu.SemaphoreType.DMA((GATHER_SEMS,)),
            pltpu.SemaphoreType.DMA((N_CHIPS + 2,))],
        compiler_params=pltpu.CompilerParams(dimension_semantics=("arbitrary", "arbitrary"),
                                             vmem_limit_bytes=VMEM_LIMIT_BYTES),
    )(x2d, norm_g, w_in_sh, *[sh[0] for sh in shards])


def _in_proj_bwd(dz, w_in, x2d, dx_res, norm_g, vec_bag, tb, reduce):
    t = x2d.shape[0]
    parts, wire, steps = reduce
    n = len(parts)

    def body(dz_ref, w_ref, x_ref, dres_ref, g_ref, vec_in_ref, *refs):
        dx_ref, vec_ref = refs[2 * n:2 * n + 2]
        rs = _rs_steps(parts, wire, refs[:2 * n], refs[2 * n + 2:3 * n + 2], refs[3 * n + 2:])
        for step, when in zip(rs, steps):
            pl.when(pl.program_id(0) == when)(step)

        @pl.when(pl.program_id(0) == 0)
        def _():
            vec_ref[...] = vec_in_ref[...]

        xn, r = _rms(x_ref[...])
        g = g_ref[...]
        dh = _dot_nt(dz_ref[...], w_ref[...])
        vec_ref[_bag_row("norm_g"), :] += jnp.sum(dh * xn, axis=0, keepdims=True)
        dx_ref[...] = dres_ref[...] + _rms_bwd(dh * g, xn, r)

    row = lambda i: (i, 0)
    fixed = lambda i: (0, 0)
    rs_shape, rs_scratch = _rs_shapes(parts, wire)
    any_spec = pl.BlockSpec(memory_space=pl.ANY)
    bag_spec = pl.BlockSpec((VEC_BAG_ROWS, D_MODEL), fixed)
    return pl.pallas_call(
        body, name="in_proj_bwd",
        out_shape=tuple([jax.ShapeDtypeStruct((t, D_MODEL), F32), jax.ShapeDtypeStruct((VEC_BAG_ROWS, D_MODEL), F32)]
                        + rs_shape),
        grid=(t // tb,),
        in_specs=[pl.BlockSpec((tb, IN_COLS), row),
                  pl.BlockSpec((D_MODEL, IN_COLS), fixed, pipeline_mode=pl.Buffered(1)),
                  pl.BlockSpec((tb, D_MODEL), row), pl.BlockSpec((tb, D_MODEL), row),
                  pl.BlockSpec((1, D_MODEL), fixed), bag_spec] + [any_spec] * (2 * n),
        out_specs=tuple([pl.BlockSpec((tb, D_MODEL), row), bag_spec] + [any_spec] * n),
        scratch_shapes=rs_scratch, input_output_aliases={5: 1},
        compiler_params=pltpu.CompilerParams(dimension_semantics=("arbitrary",),
                                             vmem_limit_bytes=VMEM_LIMIT_BYTES),
    )(dz, w_in, x2d, dx_res, norm_g, vec_bag, *_rs_operands(parts))


def _weight_grad(lhs, rhs, n_chunks, tb, name, reduce=None):
    t, k = lhs.shape
    nc = rhs.shape[1] // n_chunks
    nb = t // tb
    parts, wire, steps = reduce if reduce is not None else ([], F32, ())
    n = len(parts)

    def body(l_ref, r_ref, *refs):
        o_ref, o16_ref = refs[2 * n:2 * n + 2]
        if n:
            at = pl.program_id(0) * nb + pl.program_id(1)
            rs = _rs_steps(parts, wire, refs[:2 * n], refs[2 * n + 2:3 * n + 2], refs[3 * n + 2:])
            for step, when in zip(rs, steps):
                pl.when(at == when)(step)

        @pl.when(pl.program_id(1) == 0)
        def _():
            o_ref[...] = jnp.zeros_like(o_ref)

        o_ref[...] += _dot_tn(l_ref[...], r_ref[...])

        @pl.when(pl.program_id(1) == nb - 1)
        def _():
            o16_ref[...] = o_ref[...].astype(BF16)

    rs_shape, rs_scratch = _rs_shapes(parts, wire) if n else ([], [])
    any_spec = pl.BlockSpec(memory_space=pl.ANY)
    chunk = pl.BlockSpec((None, k, nc), lambda j, i: (j, 0, 0))
    return pl.pallas_call(
        body, name=name,
        out_shape=tuple([jax.ShapeDtypeStruct((n_chunks, k, nc), F32), jax.ShapeDtypeStruct((n_chunks, k, nc), BF16)]
                        + rs_shape),
        grid=(n_chunks, nb),
        in_specs=[pl.BlockSpec((tb, k), lambda j, i: (i, 0)), pl.BlockSpec((tb, nc), lambda j, i: (i, j))]
        + [any_spec] * (2 * n),
        out_specs=tuple([chunk, chunk] + [any_spec] * n),
        scratch_shapes=rs_scratch,
        compiler_params=pltpu.CompilerParams(dimension_semantics=("arbitrary", "arbitrary"),
                                             vmem_limit_bytes=VMEM_LIMIT_BYTES),
    )(lhs, rhs, *_rs_operands(parts))


def _adam_update(w, g, m, v):
    m_ = ADAM_B1 * m + (1.0 - ADAM_B1) * g
    v_ = ADAM_B2 * v + (1.0 - ADAM_B2) * jnp.square(g)
    m_hat = m_ / (1.0 - ADAM_B1 ** ADAM_STEP)
    v_hat = v_ / (1.0 - ADAM_B2 ** ADAM_STEP)
    return -ADAM_LR * (m_hat / (jnp.sqrt(v_hat) + ADAM_EPS) + ADAM_WD * w), m_, v_


def _adamw_replicated(vec_sum, mat_sum, entries, conv):
    n = len(entries)

    def grad_of(name, shape, vec_ref, mat_ref):
        if name in MAT_BAG_AT:
            return mat_ref[MAT_BAG_AT[name]:MAT_BAG_AT[name] + shape[0], :]
        if shape[0] == 1:
            return vec_ref[_bag_row(name), 0:shape[1]]
        return jnp.concatenate([vec_ref[_bag_row(name), h * shape[1]:(h + 1) * shape[1]] for h in range(shape[0])],
                               axis=0)

    def body(vec_ref, mat_ref, *refs):
        ins, outs = refs[:3 * n + 4], refs[3 * n + 4:]
        for k in range(n):
            w_ref, m_ref, v_ref = ins[3 * k:3 * k + 3]
            g = grad_of(entries[k][0], w_ref.shape, vec_ref, mat_ref)
            d, m_, v_ = _adam_update(w_ref[...], g, m_ref[...], v_ref[...])
            for ref, val in zip(outs[4 * k:4 * k + 4], (g, d, m_, v_)):
                ref[...] = val
        w_ref, m_ref, v_ref, g_ref = ins[3 * n:]
        for ref, val in zip(outs[4 * n:], _adam_update(w_ref[...], g_ref[...], m_ref[...], v_ref[...])):
            ref[...] = val

    arrays = [a for e in entries for a in e[1:]] + list(conv)
    out_shape = [jax.ShapeDtypeStruct(e[1].shape, F32) for e in entries for _ in range(4)]
    out_shape += [jax.ShapeDtypeStruct(conv[0].shape, F32)] * 3
    return pl.pallas_call(
        body, name="adamw_replicated", out_shape=tuple(out_shape),
        compiler_params=pltpu.CompilerParams(vmem_limit_bytes=VMEM_LIMIT_BYTES),
    )(vec_sum, mat_sum, *arrays)


def _adamw(w, g, m, v, rows, name):
    r, c = w.shape

    def body(w_ref, g_ref, m_ref, v_ref, d_ref, nm_ref, nv_ref):
        d_ref[...], nm_ref[...], nv_ref[...] = _adam_update(w_ref[...], g_ref[...], m_ref[...], v_ref[...])

    spec = pl.BlockSpec((rows, c), lambda i: (i, 0))
    return pl.pallas_call(
        body, name=name, out_shape=tuple(jax.ShapeDtypeStruct((r, c), F32) for _ in range(3)),
        grid=(r // rows,), in_specs=[spec] * 4, out_specs=(spec,) * 3,
        compiler_params=pltpu.CompilerParams(dimension_semantics=("arbitrary",),
                                             vmem_limit_bytes=VMEM_LIMIT_BYTES),
    )(w, g, m, v)


def _shift_down(ext, s):
    return pltpu.roll(ext, s, 0)


def _tile_shift(v, s):
    rows, cols = v.shape
    tiles = v.reshape(rows // F32_SUBLANES, F32_SUBLANES, cols)
    return pltpu.roll(tiles, s % F32_SUBLANES, 1).reshape(rows, cols)


def _shift_up(ext, s):
    return pltpu.roll(ext, ext.shape[0] - s, 0)


def _lru_gates(xc, wa_ref, ba, wx_ref, bx, lam):
    pa, px = [], []
    for h in range(LRU_HEADS):
        xh = xc[:, h * HEAD_DIM:(h + 1) * HEAD_DIM].astype(BF16)
        pa.append(_dot(xh, wa_ref[h]))
        px.append(_dot(xh, wx_ref[h]))
    r = _sigmoid(jnp.concatenate(pa, axis=1) + ba)
    ig = _sigmoid(jnp.concatenate(px, axis=1) + bx)
    sp = _softplus(-lam)
    log_a = (-LRU_C * r) * sp
    a = jnp.exp(log_a)
    mult = jnp.sqrt(jnp.tanh(-log_a) * (1.0 + a * a))
    return r, ig, a, mult, sp


def _conv(ext, w_ref, b):
    y = b + _shift_down(ext, 3) * w_ref[0:1, :]
    y = y + _shift_down(ext, 2) * w_ref[1:2, :]
    y = y + _shift_down(ext, 1) * w_ref[2:3, :]
    y = y + ext * w_ref[3:4, :]
    return y[CONV_HIST:, :]


def _pool_diff(ext, pos):
    out = []
    for g, k in enumerate(POOL_WINDOWS):
        col = ext[:, g * POOL_GROUP_DIM:(g + 1) * POOL_GROUP_DIM]
        s = col
        for step in range(g + 1):
            s = s + _shift_down(s, 2 ** step)
        count = jnp.minimum(pos + 1, k).astype(F32)
        out.append(s[POOL_HIST:, :] / count - col[POOL_HIST:, :])
    return out


def _pool_mix(diff, pw_ref):
    return jnp.concatenate([_dot(diff[g].astype(BF16), pw_ref[g]) for g in range(len(POOL_WINDOWS))], axis=1)


def _branch_specs(tb, row_map, fixed):
    fixed3 = lambda i: (0, 0, 0)
    return [pl.BlockSpec((CONV_WIDTH, D_MODEL), fixed), pl.BlockSpec((1, D_MODEL), fixed),
            pl.BlockSpec((LRU_HEADS, HEAD_DIM, HEAD_DIM), fixed3), pl.BlockSpec((1, D_MODEL), fixed),
            pl.BlockSpec((LRU_HEADS, HEAD_DIM, HEAD_DIM), fixed3), pl.BlockSpec((1, D_MODEL), fixed),
            pl.BlockSpec((1, D_MODEL), fixed),
            pl.BlockSpec((len(POOL_WINDOWS), POOL_GROUP_DIM, POOL_GROUP_DIM), fixed3),
            pl.BlockSpec((1, POOL_WIDTH), fixed)]


def _branches_fwd(z, weights, seq, tb, shards):
    t = z.shape[0]
    nb = t // tb
    nbe = seq // tb
    groups = tb // F32_SUBLANES
    n = len(shards)

    def body(xa_ref, ga_ref, xb_ref, gb_ref, cw_ref, cb_ref, wa_ref, ba_ref, wx_ref, bx_ref, lam_ref,
             pw_ref, ps_ref, *refs):
        g_ins = refs[:n]
        ya_ref, yb_ref, hl_ref = refs[n:n + 3]
        g_outs = refs[n + 3:2 * n + 3]
        xa_ext, xb_ext, carry, a_s, u_s, send_sems, recv_sems, local_sems = refs[2 * n + 3:]
        blk = pl.program_id(0) % nbe
        start_gather, relay_gather, finish_gather = _gather_steps(shards, g_ins, g_outs, send_sems, recv_sems,
                                                                  local_sems)
        pl.when(pl.program_id(0) == 0)(start_gather)
        pl.when(pl.program_id(0) == nb // 2)(relay_gather)

        @pl.when(blk == 0)
        def _():
            xa_ext[0:CONV_HIST, :] = jnp.zeros((CONV_HIST, D_MODEL), F32)
            xb_ext[0:POOL_HIST, :] = jnp.zeros((POOL_HIST, POOL_WIDTH), F32)
            carry[...] = jnp.zeros_like(carry)

        xa_ext[CONV_HIST:, :] = xa_ref[...]
        xb_ext[POOL_HIST:, :] = xb_ref[...]
        ea = xa_ext[...]
        eb = xb_ext[...]
        xa_ext[0:CONV_HIST, :] = ea[tb:, :]
        xb_ext[0:POOL_HIST, :] = eb[tb:, :]

        xc = _conv(ea, cw_ref, cb_ref[...])
        _, ig, a, mult, _ = _lru_gates(xc, wa_ref, ba_ref[...], wx_ref, bx_ref[...], lam_ref[...])
        u = mult * (ig * xc)
        row8 = lax.broadcasted_iota(jnp.int32, (tb, D_MODEL), 0) % F32_SUBLANES
        for s in (1, 2, 4):
            m = row8 >= s
            u = jnp.where(m, a * _tile_shift(u, s) + u, u)
            a = jnp.where(m, a * _tile_shift(a, s), a)
        a_s[...] = a
        u_s[...] = u

        def step(g, cr):
            sl = pl.ds(pl.multiple_of(g * F32_SUBLANES, F32_SUBLANES), F32_SUBLANES)
            hb = a_s[sl, :] * cr + u_s[sl, :]
            hl_ref[sl, :] = hb
            return jnp.broadcast_to(hb[F32_SUBLANES - 1:F32_SUBLANES, :], (F32_SUBLANES, D_MODEL))

        carry[...] = lax.fori_loop(0, groups, step, carry[...], unroll=4)
        ga = ga_ref[...]
        ya_ref[...] = (hl_ref[...] * (ga * _sigmoid(ga))).astype(BF16)

        pos = blk * tb + lax.broadcasted_iota(jnp.int32, (tb, POOL_GROUP_DIM), 0)
        ypre = _pool_mix(_pool_diff(eb, pos), pw_ref)
        gb = gb_ref[...]
        yb_ref[...] = ((ypre * ps_ref[...]) * (gb * _sigmoid(gb))).astype(BF16)
        pl.when(pl.program_id(0) == nb - 1)(finish_gather)

    row = lambda i: (i, 0)
    fixed = lambda i: (0, 0)
    any_spec = pl.BlockSpec(memory_space=pl.ANY)
    in_specs = [pl.BlockSpec((tb, D_MODEL), lambda i: (i, 0)), pl.BlockSpec((tb, D_MODEL), lambda i: (i, 1)),
                pl.BlockSpec((tb, POOL_WIDTH), lambda i: (i, 4)), pl.BlockSpec((tb, POOL_WIDTH), lambda i: (i, 5)),
                ] + _branch_specs(tb, row, fixed) + [any_spec] * n
    g_shape, g_sems = _gather_shapes(shards)
    return pl.pallas_call(
        body, name="branches_fwd",
        out_shape=tuple([jax.ShapeDtypeStruct((t, D_MODEL), BF16), jax.ShapeDtypeStruct((t, POOL_WIDTH), BF16),
                         jax.ShapeDtypeStruct((t, D_MODEL), F32)] + g_shape),
        grid=(nb,), in_specs=in_specs,
        out_specs=tuple([pl.BlockSpec((tb, D_MODEL), row), pl.BlockSpec((tb, POOL_WIDTH), row),
                         pl.BlockSpec((tb, D_MODEL), row)] + [any_spec] * n),
        scratch_shapes=[pltpu.VMEM((tb + CONV_HIST, D_MODEL), F32), pltpu.VMEM((tb + POOL_HIST, POOL_WIDTH), F32),
                        pltpu.VMEM((F32_SUBLANES, D_MODEL), F32),
                        pltpu.VMEM((tb, D_MODEL), F32), pltpu.VMEM((tb, D_MODEL), F32)] + g_sems,
        compiler_params=pltpu.CompilerParams(dimension_semantics=("arbitrary",),
                                             vmem_limit_bytes=VMEM_LIMIT_BYTES),
    )(z, z, z, z, *weights, *[sh[0] for sh in shards])


def _branches_bwd(z, hl, dya, dyb, dzm, weights, vec_bag, seq, tb):
    t = z.shape[0]
    nb = t // tb
    nbe = seq // tb
    groups = tb // F32_SUBLANES

    def body(xa_ref, xap_ref, ga_ref, xb_ref, xbp_ref, gb_ref, hl_ref, hlp_ref, dya_ref, dyb_ref, dzm_ref,
             cw_ref, cb_ref, wa_ref, ba_ref, wx_ref, bx_ref, lam_ref, pw_ref, ps_ref, vec_in_ref,
             dz_ref, vec_ref, mat_ref,
             xa_ext, xb_ext, hl_ext, a_ext, dxc_ext, dwin_ext, g_carry, b_s, d_s, g_s):
        i = pl.program_id(0)
        blk = (nb - 1 - i) % nbe

        def mat_rows(name, k):
            at = MAT_BAG_AT[name] + k * HEAD_DIM
            return slice(at, at + HEAD_DIM)

        @pl.when(i == 0)
        def _():
            vec_ref[...] = vec_in_ref[...]
            mat_ref[...] = jnp.zeros_like(mat_ref)

        @pl.when(blk == nbe - 1)
        def _():
            a_ext[tb:, :] = jnp.zeros((F32_SUBLANES, D_MODEL), F32)
            dxc_ext[tb:, :] = jnp.zeros((CONV_HIST, D_MODEL), F32)
            dwin_ext[tb:, :] = jnp.zeros((POOL_HIST, POOL_WIDTH), F32)
            g_carry[...] = jnp.zeros_like(g_carry)

        live = (blk > 0).astype(F32)
        xa_ext[0:CONV_HIST, :] = xap_ref[...] * live
        xa_ext[CONV_HIST:, :] = xa_ref[...]
        xb_ext[0:POOL_HIST, :] = xbp_ref[...] * live
        xb_ext[POOL_HIST:, :] = xb_ref[...]
        hl_ext[0:F32_SUBLANES, :] = hlp_ref[...] * live
        hl_ext[F32_SUBLANES:, :] = hl_ref[...]
        ea = xa_ext[...]
        eb = xb_ext[...]

        xc = _conv(ea, cw_ref, cb_ref[...])
        lam = lam_ref[...]
        r, ig, a, mult, sp = _lru_gates(xc, wa_ref, ba_ref[...], wx_ref, bx_ref[...], lam)
        hl = hl_ref[...]
        ga = ga_ref[...]
        sga = _sigmoid(ga)
        dya = dya_ref[...]
        dhl = dya * (ga * sga)
        dz_ref[:, D_MODEL:2 * D_MODEL] = (dya * hl * (sga * (1.0 + ga * (1.0 - sga)))).astype(BF16)

        a_ext[0:tb, :] = a
        b = _shift_up(a_ext[...], 1)[0:tb, :]
        a_ext[tb:, :] = jnp.broadcast_to(a[0:1, :], (F32_SUBLANES, D_MODEL))
        d = dhl
        row8 = lax.broadcasted_iota(jnp.int32, (tb, D_MODEL), 0) % F32_SUBLANES
        for s in (1, 2, 4):
            m = row8 < F32_SUBLANES - s
            d = jnp.where(m, d + b * _tile_shift(d, -s), d)
            b = jnp.where(m, b * _tile_shift(b, -s), b)
        b_s[...] = b
        d_s[...] = d

        def step(k, cr):
            sl = pl.ds(pl.multiple_of((groups - 1 - k) * F32_SUBLANES, F32_SUBLANES), F32_SUBLANES)
            gb_ = d_s[sl, :] + b_s[sl, :] * cr
            g_s[sl, :] = gb_
            return jnp.broadcast_to(gb_[0:1, :], (F32_SUBLANES, D_MODEL))

        g_carry[...] = lax.fori_loop(0, groups, step, g_carry[...], unroll=4)
        gsc = g_s[...]
        da = gsc * _shift_down(hl_ext[...], 1)[F32_SUBLANES:, :]
        dmult = gsc * (ig * xc)
        dig = gsc * (mult * xc)
        dxc = gsc * (mult * ig)
        dlog_a = da * a - (a * a) * dmult / mult
        dr = dlog_a * (-LRU_C * sp)
        vec_ref[_bag_row("lru_lambda"), :] += jnp.sum(dlog_a * (-LRU_C * r), axis=0, keepdims=True)
        dpa = dr * (r * (1.0 - r))
        dpx = dig * (ig * (1.0 - ig))
        vec_ref[_bag_row("lru_b_a"), :] += jnp.sum(dpa, axis=0, keepdims=True)
        vec_ref[_bag_row("lru_b_x"), :] += jnp.sum(dpx, axis=0, keepdims=True)
        back = []
        for h in range(LRU_HEADS):
            cols = slice(h * HEAD_DIM, (h + 1) * HEAD_DIM)
            xh = xc[:, cols].astype(BF16)
            dpa_h = dpa[:, cols].astype(BF16)
            dpx_h = dpx[:, cols].astype(BF16)
            mat_ref[mat_rows("lru_w_a", h), :] += _dot_tn(xh, dpa_h)
            mat_ref[mat_rows("lru_w_x", h), :] += _dot_tn(xh, dpx_h)
            back.append(_dot_nt(dpa_h, wa_ref[h]) + _dot_nt(dpx_h, wx_ref[h]))
        dxc = dxc + jnp.concatenate(back, axis=1)
        vec_ref[_bag_row("conv_b"), :] += jnp.sum(dxc, axis=0, keepdims=True)
        for k in range(CONV_WIDTH):
            tap = _shift_down(ea, CONV_WIDTH - 1 - k)[CONV_HIST:, :] if k < CONV_WIDTH - 1 else ea[CONV_HIST:, :]
            vec_ref[_bag_row("conv_w", k), :] += jnp.sum(dxc * tap, axis=0, keepdims=True)
        dxc_ext[0:tb, :] = dxc
        ed = dxc_ext[...]
        dxa = ed * cw_ref[3:4, :]
        dxa = dxa + _shift_up(ed, 1) * cw_ref[2:3, :]
        dxa = dxa + _shift_up(ed, 2) * cw_ref[1:2, :]
        dxa = dxa + _shift_up(ed, 3) * cw_ref[0:1, :]
        dz_ref[:, 0:D_MODEL] = dxa[0:tb, :].astype(BF16)
        dxc_ext[tb:, :] = dxc[0:CONV_HIST, :]

        pos = blk * tb + lax.broadcasted_iota(jnp.int32, (tb, POOL_GROUP_DIM), 0)
        diff = _pool_diff(eb, pos)
        ypre = _pool_mix(diff, pw_ref)
        ps = ps_ref[...]
        gb = gb_ref[...]
        sgb = _sigmoid(gb)
        dyb = dyb_ref[...]
        dyp = dyb * (gb * sgb)
        dz_ref[:, 2 * D_MODEL + POOL_WIDTH:3 * D_MODEL] = (
            dyb * (ypre * ps) * (sgb * (1.0 + gb * (1.0 - sgb)))).astype(BF16)
        vec_ref[_bag_row("pool_scale"), 0:POOL_WIDTH] += jnp.sum(dyp * ypre, axis=0, keepdims=True)
        dypre = dyp * ps
        for g, k in enumerate(POOL_WINDOWS):
            cols = slice(g * POOL_GROUP_DIM, (g + 1) * POOL_GROUP_DIM)
            dyg = dypre[:, cols].astype(BF16)
            mat_ref[mat_rows("pool_w", g), :] += _dot_tn(diff[g].astype(BF16), dyg)
            ddiff = _dot_nt(dyg, pw_ref[g])
            count = jnp.minimum(pos + 1, k).astype(F32)
            dwin = ddiff / count
            dwin_ext[0:tb, cols] = dwin
            s = dwin_ext[:, cols]
            for step_ in range(g + 1):
                s = s + _shift_up(s, 2 ** step_)
            dz_ref[:, 2 * D_MODEL + g * POOL_GROUP_DIM:2 * D_MODEL + (g + 1) * POOL_GROUP_DIM] = (
                s[0:tb, :] - ddiff).astype(BF16)
            dwin_ext[tb:, cols] = dwin[0:POOL_HIST, :]

        dz_ref[:, 3 * D_MODEL:] = dzm_ref[...]

        @pl.when(i == nb - 1)
        def _():
            row = _bag_row("lru_lambda")
            vec_ref[row, :] = vec_ref[row, :] * (-_sigmoid(-lam))

    rev = lambda i: (nb - 1 - i, 0)
    fixed = lambda i: (0, 0)

    def prev(rows, col):
        per = tb // rows
        return lambda i: (jnp.maximum((nb - 1 - i) * per - 1, 0), col)

    in_specs = [pl.BlockSpec((tb, D_MODEL), lambda i: (nb - 1 - i, 0)),
                pl.BlockSpec((CONV_HIST, D_MODEL), prev(CONV_HIST, 0)),
                pl.BlockSpec((tb, D_MODEL), lambda i: (nb - 1 - i, 1)),
                pl.BlockSpec((tb, POOL_WIDTH), lambda i: (nb - 1 - i, 4)),
                pl.BlockSpec((POOL_HIST, POOL_WIDTH), prev(POOL_HIST, 4)),
                pl.BlockSpec((tb, POOL_WIDTH), lambda i: (nb - 1 - i, 5)),
                pl.BlockSpec((tb, D_MODEL), rev),
                pl.BlockSpec((F32_SUBLANES, D_MODEL), prev(F32_SUBLANES, 0)),
                pl.BlockSpec((tb, D_MODEL), rev), pl.BlockSpec((tb, POOL_WIDTH), rev),
                pl.BlockSpec((tb, 2 * D_MODEL), rev)] + _branch_specs(tb, rev, fixed) + [
                    pl.BlockSpec((VEC_BAG_ROWS, D_MODEL), fixed)]
    out_shape = (jax.ShapeDtypeStruct((t, IN_COLS), BF16), jax.ShapeDtypeStruct((VEC_BAG_ROWS, D_MODEL), F32),
                 jax.ShapeDtypeStruct((MAT_BAG_ROWS, HEAD_DIM), F32))
    out_specs = (pl.BlockSpec((tb, IN_COLS), rev), pl.BlockSpec((VEC_BAG_ROWS, D_MODEL), fixed),
                 pl.BlockSpec((MAT_BAG_ROWS, HEAD_DIM), fixed))
    scratch = [pltpu.VMEM((tb + CONV_HIST, D_MODEL), F32), pltpu.VMEM((tb + POOL_HIST, POOL_WIDTH), F32),
               pltpu.VMEM((tb + F32_SUBLANES, D_MODEL), F32), pltpu.VMEM((tb + F32_SUBLANES, D_MODEL), F32),
               pltpu.VMEM((tb + CONV_HIST, D_MODEL), F32), pltpu.VMEM((tb + POOL_HIST, POOL_WIDTH), F32),
               pltpu.VMEM((F32_SUBLANES, D_MODEL), F32),
               pltpu.VMEM((tb, D_MODEL), F32), pltpu.VMEM((tb, D_MODEL), F32), pltpu.VMEM((tb, D_MODEL), F32)]
    return pl.pallas_call(
        body, name="branches_bwd", out_shape=out_shape, grid=(nb,), in_specs=in_specs, out_specs=out_specs,
        scratch_shapes=scratch, input_output_aliases={len(in_specs) - 1: 1},
        compiler_params=pltpu.CompilerParams(dimension_semantics=("arbitrary",),
                                             vmem_limit_bytes=VMEM_LIMIT_BYTES),
    )(z, z, z, z, z, z, hl, hl, dya, dyb, dzm, *weights, vec_bag)


def _merge_head(x2d, ya, yb, z, p2d, tgt, w_pl, w_pp, w_out, w_pg, w_pe, g2, gf, tb):
    t = x2d.shape[0]
    p_dim = p2d.shape[1]

    def body(x_ref, ya_ref, yb_ref, ma_ref, mb_ref, p_ref, t_ref, wpl_ref, wpp_ref, wout_ref, wpg_ref, wpe_ref,
             g2_ref, gf_ref,
             bag_ref, dxr_ref, dya_ref, dyb_ref, dzm_ref,
             mg_ref, do_ref, hn_ref, dgp_ref, dpe_ref, da_ref, dbm_ref, pbf_ref):
        @pl.when(pl.program_id(0) == 0)
        def _():
            bag_ref[...] = jnp.zeros_like(bag_ref)

        a_ = _dot(ya_ref[...], wpl_ref[...])
        bm = _dot(yb_ref[...], wpp_ref[...])
        sa = _sigmoid(ma_ref[...])
        sb = _sigmoid(mb_ref[...])
        mg = (sa * a_ + sb * bm).astype(BF16)
        mg_ref[...] = mg
        x1 = x_ref[...] + _dot(mg, wout_ref[...])
        xn2, r2 = _rms(x1)
        g2 = g2_ref[...]
        hn = (xn2 * g2).astype(BF16)
        hn_ref[...] = hn
        gate = _sigmoid(_dot(hn, wpg_ref[...]))
        pbf = p_ref[...].astype(BF16)
        pbf_ref[...] = pbf
        pe = _dot(pbf, wpe_ref[...])
        x2 = x1 + gate * pe
        xn3, r3 = _rms(x2)
        gf = gf_ref[...]
        err = xn3 * gf - t_ref[...]
        bag_ref[_bag_rows("loss"), 0:128] += 0.5 * jnp.sum(jnp.mean(err * err, axis=-1))

        dy = err * (1.0 / D_MODEL)
        bag_ref[_bag_row("final_g"), :] += jnp.sum(dy * xn3, axis=0, keepdims=True)
        dx2 = _rms_bwd(dy * gf, xn3, r3)
        dpe_ref[...] = (dx2 * gate).astype(BF16)
        dgp = ((dx2 * pe) * (gate * (1.0 - gate))).astype(BF16)
        dgp_ref[...] = dgp
        dhn = _dot_nt(dgp, wpg_ref[...])
        bag_ref[_bag_row("ple_norm_g"), :] += jnp.sum(dhn * xn2, axis=0, keepdims=True)
        dx1 = dx2 + _rms_bwd(dhn * g2, xn2, r2)
        dxr_ref[...] = dx1
        do = dx1.astype(BF16)
        do_ref[...] = do
        dmg = _dot_nt(do, wout_ref[...])
        da = (dmg * sa).astype(BF16)
        dbm = (dmg * sb).astype(BF16)
        da_ref[...] = da
        dbm_ref[...] = dbm
        dzm_ref[:, 0:D_MODEL] = (dmg * a_ * (sa * (1.0 - sa))).astype(BF16)
        dzm_ref[:, D_MODEL:] = (dmg * bm * (sb * (1.0 - sb))).astype(BF16)
        dya_ref[...] = _dot_nt(da, wpl_ref[...])
        dyb_ref[...] = _dot_nt(dbm, wpp_ref[...])

    row = lambda i: (i, 0)
    fixed = lambda i: (0, 0)

    def resident(shape):
        return pl.BlockSpec(shape, fixed, pipeline_mode=pl.Buffered(1))

    tok = lambda width: pl.BlockSpec((tb, width), row)
    in_specs = [tok(D_MODEL), tok(D_MODEL), tok(POOL_WIDTH),
                pl.BlockSpec((tb, D_MODEL), lambda i: (i, 3)), pl.BlockSpec((tb, D_MODEL), lambda i: (i, 4)),
                tok(p_dim), tok(D_MODEL),
                resident((D_MODEL, D_MODEL)), resident((POOL_WIDTH, D_MODEL)), resident((D_MODEL, D_MODEL)),
                resident((D_MODEL, D_MODEL)), resident((p_dim, D_MODEL)),
                pl.BlockSpec((1, D_MODEL), fixed), pl.BlockSpec((1, D_MODEL), fixed)]
    bf = lambda width: jax.ShapeDtypeStruct((t, width), BF16)
    f32 = lambda width: jax.ShapeDtypeStruct((t, width), F32)
    out_shape = (jax.ShapeDtypeStruct((VEC_BAG_ROWS, D_MODEL), F32),
                 f32(D_MODEL), f32(D_MODEL), f32(POOL_WIDTH), bf(2 * D_MODEL),
                 bf(D_MODEL), bf(D_MODEL), bf(D_MODEL), bf(D_MODEL), bf(D_MODEL), bf(D_MODEL), bf(D_MODEL), bf(p_dim))
    out_specs = (pl.BlockSpec((VEC_BAG_ROWS, D_MODEL), fixed),
                 tok(D_MODEL), tok(D_MODEL), tok(POOL_WIDTH), tok(2 * D_MODEL),
                 tok(D_MODEL), tok(D_MODEL), tok(D_MODEL), tok(D_MODEL), tok(D_MODEL), tok(D_MODEL), tok(D_MODEL),
                 tok(p_dim))
    return pl.pallas_call(
        body, name="merge_head", out_shape=out_shape, grid=(t // tb,), in_specs=in_specs, out_specs=out_specs,
        compiler_params=pltpu.CompilerParams(dimension_semantics=("arbitrary",),
                                             vmem_limit_bytes=VMEM_LIMIT_BYTES),
    )(x2d, ya, yb, z, z, p2d, tgt, w_pl, w_pp, w_out, w_pg, w_pe, g2, gf)


def kernel(x, p, norm_g, w_in, conv_w, conv_b, lru_w_a, lru_b_a, lru_w_x, lru_b_x, lru_lambda, pool_w, pool_scale, w_proj_lru, w_proj_pool, w_out, ple_norm_g, w_ple_gate, w_ple_proj, final_g, loss_target, m_norm_g, m_w_in, m_conv_w, m_conv_b, m_lru_w_a, m_lru_b_a, m_lru_w_x, m_lru_b_x, m_lru_lambda, m_pool_w, m_pool_scale, m_w_proj_lru, m_w_proj_pool, m_w_out, m_ple_norm_g, m_w_ple_gate, m_w_ple_proj, m_final_g, v_norm_g, v_w_in, v_conv_w, v_conv_b, v_lru_w_a, v_lru_b_a, v_lru_w_x, v_lru_b_x, v_lru_lambda, v_pool_w, v_pool_scale, v_w_proj_lru, v_w_proj_pool, v_w_out, v_ple_norm_g, v_w_ple_gate, v_w_ple_proj, v_final_g):
    bsz, seq, _ = x.shape
    t = bsz * seq
    tb_mm = min(512, seq)
    tb_seq = min(256, seq // 2) if seq >= 512 else seq
    x2d = x.reshape(t, D_MODEL)
    p2d = p.reshape(t, p.shape[-1])
    tgt = loss_target.reshape(t, D_MODEL)
    chip = 2 * lax.axis_index("x") + lax.axis_index("y")

    rest = [(w_proj_lru[0], 0), (w_proj_pool[0], 1), (w_out[0], 0), (w_ple_gate[0], 0), (w_ple_proj[0], 1)]
    z, h_bf, w_in_f, conv_w_f = _in_proj_gather(x2d, norm_g, w_in[0].astype(BF16), [(conv_w[0], 1, False)], tb_mm)

    wa_bf = lru_w_a[0].astype(BF16)
    wx_bf = lru_w_x[0].astype(BF16)
    pw_bf = pool_w[0].astype(BF16)
    branch_w = (conv_w_f, conv_b, wa_bf, lru_b_a.reshape(1, D_MODEL), wx_bf, lru_b_x.reshape(1, D_MODEL),
                lru_lambda, pw_bf, pool_scale)

    ya, yb, hl, w_pl_f, w_pp_f, w_out_f, w_pg_f, w_pe_f = _branches_fwd(
        z, branch_w, seq, tb_seq, [(w.astype(BF16), axis, True) for w, axis in rest])
    (vec_bag, dx_res, dya, dyb, dzm, mg_bf, do_bf, hn_bf, dgp_bf, dpe_bf, da_bf, dbm_bf, p_bf) = _merge_head(
        x2d, ya, yb, z, p2d, tgt, w_pl_f, w_pp_f, w_out_f, w_pg_f, w_pe_f, ple_norm_g, final_g.reshape(1, D_MODEL),
        tb_seq)
    dz, vec_bag, mat_bag = _branches_bwd(z, hl, dya, dyb, dzm, branch_w, vec_bag, seq, tb_seq)

    tb_dw = min(1024, seq)
    g_pl = _weight_grad(ya, da_bf, 1, tb_dw, "dw_proj_lru")[0].reshape(8, D_MODEL // 8, D_MODEL)
    g_pp = _weight_grad(yb, dbm_bf, 1, tb_dw, "dw_proj_pool")[0][0]
    g_out = _weight_grad(mg_bf, do_bf, 1, tb_dw, "dw_out")[0].reshape(8, D_MODEL // 8, D_MODEL)
    g_pg = _weight_grad(hn_bf, dgp_bf, 1, tb_dw, "dw_ple_gate")[0].reshape(8, D_MODEL // 8, D_MODEL)
    p_dim = p2d.shape[1]
    g_pe = _weight_grad(p_bf, dpe_bf, 1, tb_dw, "dw_ple_proj")[0][0]
    nb_dw = t // tb_dw
    g_in, g_in16, r_pl, r_pp, r_out, r_pg, r_pe = _weight_grad(
        h_bf, dz, N_CHIPS, tb_dw, "dw_in",
        reduce=([(g_pl, False, None), (g_pp, True, None), (g_out, False, None), (g_pg, False, None),
                 (g_pe, True, None)], BF16, (0, nb_dw, 3 * nb_dw + nb_dw // 2, N_CHIPS * nb_dw - 1)))
    pieces = (8, D_MODEL // 2, IN_COLS // N_CHIPS)
    nb_seq = t // tb_seq
    dx, vec_bag, r_in = _in_proj_bwd(
        dz, w_in_f, x2d, dx_res, norm_g, vec_bag, tb_seq,
        reduce=([(g_in.reshape(pieces), False, g_in16.reshape(pieces))], BF16, (0, nb_seq // 4, nb_seq - 1, nb_seq - 1)))
    vec_mine, mat_mine = _reduce_scatter(
        [(vec_bag.reshape(8, VEC_BAG_ROWS // 8, D_MODEL), False, None),
         (mat_bag.reshape(8, MAT_BAG_ROWS // 8, HEAD_DIM), False, None)], "rs_small")
    vec_sum, mat_sum = _gather_shards(
        [(vec_mine.reshape(VEC_BAG_ROWS // N_CHIPS, D_MODEL), 0, True),
         (mat_mine.reshape(MAT_BAG_ROWS // N_CHIPS, HEAD_DIM), 0, True)], "gather_small")

    def big_update(w, g2d, m, v, rows, name):
        d, nm, nv = _adamw(w[0], g2d, m[0], v[0], rows, name)
        return g2d[None], d[None], nm[None], nv[None]

    u_in = big_update(w_in, r_in.reshape(D_MODEL, IN_COLS // N_CHIPS), m_w_in, v_w_in, 256, "adamw_w_in")
    u_pl = big_update(w_proj_lru, r_pl.reshape(D_MODEL // N_CHIPS, D_MODEL), m_w_proj_lru, v_w_proj_lru, 256, "adamw_w_proj_lru")
    u_pp = big_update(w_proj_pool, r_pp.reshape(POOL_WIDTH, D_MODEL // N_CHIPS), m_w_proj_pool, v_w_proj_pool, 512, "adamw_w_proj_pool")
    u_out = big_update(w_out, r_out.reshape(D_MODEL // N_CHIPS, D_MODEL), m_w_out, v_w_out, 256, "adamw_w_out")
    u_pg = big_update(w_ple_gate, r_pg.reshape(D_MODEL // N_CHIPS, D_MODEL), m_w_ple_gate, v_w_ple_gate, 256, "adamw_w_ple_gate")
    u_pe = big_update(w_ple_proj, r_pe.reshape(p_dim, D_MODEL // N_CHIPS), m_w_ple_proj, v_w_ple_proj, 256, "adamw_w_ple_proj")

    small = [("norm_g", norm_g, m_norm_g, v_norm_g), ("conv_b", conv_b, m_conv_b, v_conv_b),
             ("lru_w_a", lru_w_a, m_lru_w_a, v_lru_w_a), ("lru_b_a", lru_b_a, m_lru_b_a, v_lru_b_a),
             ("lru_w_x", lru_w_x, m_lru_w_x, v_lru_w_x), ("lru_b_x", lru_b_x, m_lru_b_x, v_lru_b_x),
             ("lru_lambda", lru_lambda, m_lru_lambda, v_lru_lambda), ("pool_w", pool_w, m_pool_w, v_pool_w),
             ("pool_scale", pool_scale, m_pool_scale, v_pool_scale),
             ("ple_norm_g", ple_norm_g, m_ple_norm_g, v_ple_norm_g), ("final_g", final_g, m_final_g, v_final_g)]

    def view(a):
        return a.reshape(-1, a.shape[-1]) if a.ndim != 3 else a[0]

    cw_at = F32_SUBLANES * VEC_BAG_SLOTS.index("conv_w")
    cw_cols = D_MODEL // N_CHIPS
    g_cw = lax.dynamic_slice(vec_sum, (cw_at, chip * cw_cols), (CONV_WIDTH, cw_cols))
    flat = _adamw_replicated(vec_sum, mat_sum, [(name,) + tuple(view(a) for a in arrs) for name, *arrs in small],
                             (conv_w[0], m_conv_w[0], v_conv_w[0], g_cw))
    u_small = {name: tuple(flat[4 * k + pick].reshape(arrs[0].shape) for pick in range(4))
               for k, (name, *arrs) in enumerate(small)}
    u_cw = tuple(a[None] for a in (g_cw,) + tuple(flat[4 * len(small):]))

    loss = vec_sum[F32_SUBLANES * VEC_BAG_SLOTS.index("loss"), 0]
    grad_x = dx.reshape(bsz, seq, D_MODEL)

    def ordered(pick):
        s = {name: u[pick] for name, u in u_small.items()}
        return [s["norm_g"], u_in[pick], u_cw[pick], s["conv_b"], s["lru_w_a"], s["lru_b_a"], s["lru_w_x"], s["lru_b_x"],
                s["lru_lambda"], s["pool_w"], s["pool_scale"], u_pl[pick], u_pp[pick], u_out[pick], s["ple_norm_g"],
                u_pg[pick], u_pe[pick], s["final_g"]]

    return (loss, grad_x, *ordered(0), *ordered(1), *ordered(2), *ordered(3))
```

```python
import jax
import jax.numpy as jnp
from jax import lax
from jax.experimental import pallas as pl
from jax.experimental.pallas import tpu as pltpu

F32 = jnp.float32
BF16 = jnp.bfloat16
MESH = pl.DeviceIdType.MESH

D_MODEL = 1024
LRU_HEADS = 8
HEAD_DIM = 128
CONV_WIDTH = 4
LRU_C = 8.0
POOL_WIDTH = 512
POOL_WINDOWS = (2, 4, 8, 16)
POOL_GROUP_DIM = 128
IN_COLS = 5120
N_CHIPS = 4
EPS = 1e-6

ADAM_LR = 0.001
ADAM_B1 = 0.9
ADAM_B2 = 0.999
ADAM_EPS = 1e-08
ADAM_WD = 0.01
ADAM_STEP = 10

F32_SUBLANES = 8
CONV_HIST = 8
POOL_HIST = 16
VMEM_LIMIT_BYTES = 58 * 1024 * 1024
VEC_BAG_SLOTS = ("norm_g", "conv_w", "conv_b", "lru_b_a", "lru_b_x", "lru_lambda", "pool_scale", "ple_norm_g",
                 "final_g", "loss")
VEC_BAG_ROWS = 128
MAT_BAG_AT = {"lru_w_a": 0, "lru_w_x": LRU_HEADS * HEAD_DIM, "pool_w": 2 * LRU_HEADS * HEAD_DIM}
MAT_BAG_ROWS = 2 * LRU_HEADS * HEAD_DIM + len(POOL_WINDOWS) * POOL_GROUP_DIM


def _bag_row(name, k=0):
    at = F32_SUBLANES * VEC_BAG_SLOTS.index(name) + k
    return slice(at, at + 1)


def _bag_rows(name):
    at = F32_SUBLANES * VEC_BAG_SLOTS.index(name)
    return slice(at, at + F32_SUBLANES)


def _dot(a, b):
    return jnp.dot(a, b, preferred_element_type=F32)


def _dot_nt(a, b):
    return lax.dot_general(a, b, (((1,), (1,)), ((), ())), preferred_element_type=F32)


def _dot_tn(a, b):
    return lax.dot_general(a, b, (((0,), (0,)), ((), ())), preferred_element_type=F32)


def _sigmoid(v):
    return jax.nn.sigmoid(v)


def _softplus(v):
    return jnp.maximum(v, 0.0) + jnp.log1p(jnp.exp(-jnp.abs(v)))


def _place():
    return lax.axis_index("x"), lax.axis_index("y"), lax.axis_index("c")


GATHER_SEMS = 6


def _gather_shapes(shards):
    out_shape = []
    for arr, axis, _ in shards:
        r, cols = arr.shape
        out_shape.append(jax.ShapeDtypeStruct((N_CHIPS * r, cols) if axis == 0 else (r, N_CHIPS * cols), arr.dtype))
    n = len(shards)
    sems = [pltpu.SemaphoreType.DMA((n * GATHER_SEMS,)), pltpu.SemaphoreType.DMA((n * GATHER_SEMS,)),
            pltpu.SemaphoreType.DMA((n,))]
    return out_shape, sems


def _gather_steps(shards, ins, outs, send_sems, recv_sems, local_sems):
    n = len(shards)
    x, y, c = _place()
    me, sibling = (x, y, c), (x, y, 1 - c)
    chips = [(x, 1 - y), (1 - x, y), (1 - x, 1 - y)]

    def region(k, cx, cy, hc):
        (r, cols), axis = shards[k][0].shape, shards[k][1]
        j = 2 * cx + cy
        if axis == 0:
            if hc is None:
                return outs[k].at[pl.ds(j * r, r), :]
            return outs[k].at[pl.ds(j * r + hc * (r // 2), r // 2), :]
        if hc is None:
            return outs[k].at[:, pl.ds(j * cols, cols)]
        return outs[k].at[pl.ds(hc * (r // 2), r // 2), pl.ds(j * cols, cols)]

    def remote(k, sem, block, to, src=None):
        dst = region(k, *block)
        return pltpu.make_async_remote_copy(
            src_ref=dst if src is None else src, dst_ref=dst,
            send_sem=send_sems.at[k * GATHER_SEMS + sem], recv_sem=recv_sems.at[k * GATHER_SEMS + sem],
            device_id=to, device_id_type=MESH)

    def first(k, idx):
        r, split = shards[k][0].shape[0], shards[k][2]
        src = ins[k].at[pl.ds(c * (r // 2), r // 2), :] if split else ins[k]
        return remote(k, idx, (x, y, c if split else None), (*chips[idx], c), src=src)

    def relay(k):
        src_chip = (jnp.bitwise_xor(x, 1 - c), jnp.bitwise_xor(y, c))
        dst_chip = (jnp.bitwise_xor(x, c), jnp.bitwise_xor(y, 1 - c))
        return remote(k, 2, (*src_chip, c), (*dst_chip, c))

    def passed(k, idx):
        return remote(k, 3 + idx, (*chips[idx], c), sibling)

    def mine(k):
        return pltpu.make_async_copy(ins[k], region(k, x, y, None), local_sems.at[k])

    def start():
        for k in range(n):
            mine(k).start()
            for idx in range(2 if shards[k][2] else 3):
                first(k, idx).start()

    def relay_on():
        for k in range(n):
            split = shards[k][2]
            for idx in range(2):
                remote(k, idx, (*chips[idx], c if split else None), me).wait_recv()
            if split:
                relay(k).start()
                passed(k, 0).start()
                passed(k, 1).start()

    def finish():
        for k in range(n):
            split = shards[k][2]
            remote(k, 2, (*chips[2], c if split else None), me).wait_recv()
            if split:
                passed(k, 2).start()
        for k in range(n):
            if shards[k][2]:
                for idx in range(3):
                    remote(k, 3 + idx, (*chips[idx], 1 - c), me).wait_recv()
        for k in range(n):
            if shards[k][2]:
                for cp in (first(k, 0), first(k, 1), relay(k), passed(k, 0), passed(k, 1), passed(k, 2)):
                    cp.wait_send()
            else:
                for idx in range(3):
                    first(k, idx).wait_send()
            mine(k).wait()

    return start, relay_on, finish


def _gather_shards(shards, name):
    n = len(shards)

    def body(*refs):
        for step in _gather_steps(shards, refs[:n], refs[n:2 * n], *refs[2 * n:]):
            step()

    out_shape, sems = _gather_shapes(shards)
    any_spec = pl.BlockSpec(memory_space=pl.ANY)
    return pl.pallas_call(
        body, name=name, out_shape=tuple(out_shape),
        in_specs=[any_spec] * n, out_specs=tuple([any_spec] * n), scratch_shapes=sems,
    )(*[s[0] for s in shards])


RS_ADD_ROWS = (64, 56, 32, 16, 8)


def _all_reduce_tile(v, name):
    n_dev = 2 * N_CHIPS
    flips = [(dx, dy, dc) for dx in (0, 1) for dy in (0, 1) for dc in (0, 1)][1:]

    def body(v_ref, o_ref, slots, send_sems, recv_sems):
        x, y, c = _place()
        mine = 4 * x + 2 * y + c

        def copy(k, to_flip, slot):
            dx, dy, dc = to_flip
            peer = (jnp.bitwise_xor(x, dx), jnp.bitwise_xor(y, dy), jnp.bitwise_xor(c, dc))
            return pltpu.make_async_remote_copy(
                src_ref=v_ref, dst_ref=slots.at[slot], send_sem=send_sems.at[k], recv_sem=recv_sems.at[k],
                device_id=peer, device_id_type=MESH)

        sends = [copy(k, flip, mine) for k, flip in enumerate(flips)]
        for cp in sends:
            cp.start()
        slots[mine] = v_ref[...]
        for k, (dx, dy, dc) in enumerate(flips):
            copy(k, (dx, dy, dc), jnp.bitwise_xor(mine, 4 * dx + 2 * dy + dc)).wait_recv()
        total = slots[0]
        for d in range(1, n_dev):
            total = total + slots[d]
        o_ref[...] = total
        for cp in sends:
            cp.wait_send()

    return pl.pallas_call(
        body, name=name, out_shape=jax.ShapeDtypeStruct(v.shape, F32),
        in_specs=[pl.BlockSpec(memory_space=pltpu.VMEM)], out_specs=pl.BlockSpec(memory_space=pltpu.VMEM),
        scratch_shapes=[pltpu.VMEM((n_dev,) + v.shape, F32), pltpu.SemaphoreType.DMA((n_dev - 1,)),
                        pltpu.SemaphoreType.DMA((n_dev - 1,))],
    )(v)


RS_SEMS = 8
RS_LOCAL_SEMS = 5


def _rs_piece_shape(part):
    arr, cols = part[0], part[1]
    return (arr.shape[0] // 2, arr.shape[1] // N_CHIPS) if cols else tuple(arr.shape[1:])


def _rs_operands(parts):
    return [p[0] for p in parts] + [p[0] if p[2] is None else p[2] for p in parts]


def _rs_wires(parts, wire):
    return list(wire) if isinstance(wire, (list, tuple)) else [wire] * len(parts)


def _rs_shapes(parts, wire):
    n = len(parts)
    shapes = [_rs_piece_shape(p) for p in parts]
    out_shape = [jax.ShapeDtypeStruct((2,) + s, F32) for s in shapes]
    scratch = []
    for lead, kind in ((N_CHIPS, "f32"), (N_CHIPS, "narrow"), (N_CHIPS, "wire"), (None, "f32"), (N_CHIPS, "wire")):
        for s, p, w in zip(shapes, parts, _rs_wires(parts, wire)):
            dtype = {"f32": F32, "narrow": F32 if p[2] is None else p[2].dtype, "wire": w}[kind]
            scratch.append(pltpu.VMEM(s if lead is None else (lead,) + s, dtype))
    scratch += [pltpu.SemaphoreType.DMA((n * RS_SEMS,)), pltpu.SemaphoreType.DMA((n * RS_SEMS,)),
                pltpu.SemaphoreType.DMA((n * RS_LOCAL_SEMS,))]
    return out_shape, scratch


def _rs_steps(parts, wire, ins, outs, scratch):
    n = len(parts)
    own, sib, got, fin, snd = (scratch[k * n:(k + 1) * n] for k in range(5))
    send_sems, recv_sems, local_sems = scratch[5 * n:]
    shapes = [_rs_piece_shape(p) for p in parts]
    x, y, c = _place()
    j_me = 2 * x + y
    me, sibling = (x, y, c), (x, y, 1 - c)
    chips = [(x, 1 - y), (1 - x, y), (1 - x, 1 - y)]

    def piece(a, jj, core, narrow=False):
        ref = ins[n + a] if narrow else ins[a]
        if parts[a][1]:
            r, cl = shapes[a]
            return ref.at[pl.ds(core * r, r), pl.ds(jj * cl, cl)]
        return ref.at[2 * jj + core]

    def remote(a, sem, src, dst, to):
        return pltpu.make_async_remote_copy(
            src_ref=src, dst_ref=dst, send_sem=send_sems.at[a * RS_SEMS + sem],
            recv_sem=recv_sems.at[a * RS_SEMS + sem], device_id=to, device_id_type=MESH)

    def rows_loop(a, fn):
        r = shapes[a][0]
        step = max(s for s in RS_ADD_ROWS if r % s == 0)

        def it(i, carry):
            fn(pl.ds(pl.multiple_of(i * step, step), step))
            return carry

        lax.fori_loop(0, r // step, it, 0)

    def load(a, jj):
        return pltpu.make_async_copy(piece(a, jj, c), own[a].at[jj], local_sems.at[a * RS_LOCAL_SEMS + jj])

    def to_sibling(a, jj):
        return remote(a, jj, piece(a, jj, 1 - c, narrow=True), sib[a].at[jj], sibling)

    def to_owner(a, idx):
        chip = chips[idx]
        return remote(a, 4 + idx, snd[a].at[2 * chip[0] + chip[1]], got[a].at[j_me], (*chip, c))

    def store(a):
        return pltpu.make_async_copy(fin[a], outs[a].at[c], local_sems.at[a * RS_LOCAL_SEMS + 4])

    def result_to_sibling(a):
        return remote(a, 7, fin[a], outs[a].at[c], sibling)

    def exchange():
        for a in range(n):
            for jj in range(N_CHIPS):
                load(a, jj).start()
                to_sibling(a, jj).start()

    def chip_sums():
        for a in range(n):
            for jj in range(N_CHIPS):
                load(a, jj).wait()
                remote(a, jj, sib[a].at[jj], sib[a].at[jj], me).wait_recv()

                def add(sl, a=a, jj=jj):
                    q = own[a][jj, sl, :] + sib[a][jj, sl, :].astype(F32)
                    own[a][jj, sl, :] = q
                    snd[a][jj, sl, :] = q.astype(snd[a].dtype)

                rows_loop(a, add)
        for a in range(n):
            for idx in range(3):
                to_owner(a, idx).start()
        for a in range(n):
            def keep(sl, a=a):
                got[a][j_me, sl, :] = snd[a][j_me, sl, :]

            rows_loop(a, keep)

    def totals():
        for a in range(n):
            for idx, chip in enumerate(chips):
                slot = got[a].at[2 * chip[0] + chip[1]]
                remote(a, 4 + idx, slot, slot, me).wait_recv()

            def total(sl, a=a):
                mine = own[a][j_me, sl, :]
                term = [jnp.where(j_me == jj, mine, got[a][jj, sl, :].astype(F32)) for jj in range(N_CHIPS)]
                fin[a][sl, :] = ((term[0] + term[1]) + term[2]) + term[3]

            rows_loop(a, total)
            store(a).start()
            result_to_sibling(a).start()

    def finish():
        for a in range(n):
            remote(a, 7, outs[a].at[1 - c], outs[a].at[1 - c], me).wait_recv()
        for a in range(n):
            for jj in range(N_CHIPS):
                to_sibling(a, jj).wait_send()
            for idx in range(3):
                to_owner(a, idx).wait_send()
            result_to_sibling(a).wait_send()
            store(a).wait()

    return exchange, chip_sums, totals, finish


def _reduce_scatter(parts, name, wire=F32):
    n = len(parts)

    def body(*refs):
        for step in _rs_steps(parts, wire, refs[:2 * n], refs[2 * n:3 * n], refs[3 * n:]):
            step()

    out_shape, scratch = _rs_shapes(parts, wire)
    any_spec = pl.BlockSpec(memory_space=pl.ANY)
    return pl.pallas_call(
        body, name=name, out_shape=tuple(out_shape),
        in_specs=[any_spec] * (2 * n), out_specs=tuple([any_spec] * n), scratch_shapes=scratch,
        compiler_params=pltpu.CompilerParams(vmem_limit_bytes=VMEM_LIMIT_BYTES),
    )(*_rs_operands(parts))


def _rms(x):
    r = lax.rsqrt(jnp.mean(x * x, axis=-1, keepdims=True) + EPS)
    return x * r, r


def _rms_bwd(dxn, xn, r):
    return r * (dxn - xn * jnp.mean(dxn * xn, axis=-1, keepdims=True))


def _in_proj_gather(x2d, norm_g, w_in_sh, shards, tb):
    t = x2d.shape[0]
    nb = t // tb
    cols = IN_COLS // N_CHIPS
    half = D_MODEL // 2
    n = len(shards)

    def body(x_ref, g_ref, win_ref, *refs):
        ins = refs[:n]
        z_ref, h_ref, wfull_ref = refs[n:n + 3]
        outs = refs[n + 3:2 * n + 3]
        wv, h_buf, send_sems, recv_sems, local_sems, w_send, w_recv, w_local = refs[2 * n + 3:]
        s, i = pl.program_id(0), pl.program_id(1)
        x, y, c = _place()
        me, sibling = (x, y, c), (x, y, 1 - c)
        chips = [(x, 1 - y), (1 - x, y), (1 - x, 1 - y)]

        def w_half(cx, cy, hc):
            return wv.at[2 * cx + cy, pl.ds(hc * half, half), :]

        def w_remote(sem, block, to, src=None):
            dst = w_half(*block)
            return pltpu.make_async_remote_copy(
                src_ref=dst if src is None else src, dst_ref=dst, send_sem=w_send.at[sem],
                recv_sem=w_recv.at[sem], device_id=to, device_id_type=MESH)

        def w_first(idx):
            return w_remote(idx, (x, y, c), (*chips[idx], c), src=win_ref.at[pl.ds(c * half, half), :])

        def w_relay():
            src_chip = (jnp.bitwise_xor(x, 1 - c), jnp.bitwise_xor(y, c))
            dst_chip = (jnp.bitwise_xor(x, c), jnp.bitwise_xor(y, 1 - c))
            return w_remote(2, (*src_chip, c), (*dst_chip, c))

        def w_pass(idx):
            return w_remote(3 + idx, (*chips[idx], c), sibling)

        def w_store(k, cx, cy):
            jj = 2 * cx + cy
            return pltpu.make_async_copy(wv.at[jj], wfull_ref.at[:, pl.ds(jj * cols, cols)], w_local.at[k])

        start_rest, relay_rest, finish_rest = _gather_steps(shards, ins, outs, send_sems, recv_sems, local_sems)
        own = pltpu.make_async_copy(win_ref, wv.at[2 * x + y], w_local.at[4])

        @pl.when((s == 0) & (i == 0))
        def _():
            own.start()
            w_first(0).start()
            w_first(1).start()
            start_rest()
            own.wait()
            w_store(0, x, y).start()

        @pl.when((s == 1) & (i == 0))
        def _():
            w_remote(0, (*chips[0], c), me).wait_recv()
            w_remote(1, (*chips[1], c), me).wait_recv()
            w_relay().start()
            w_pass(0).start()
            w_pass(1).start()
            w_remote(3, (*chips[0], 1 - c), me).wait_recv()
            w_store(1, *chips[0]).start()

        @pl.when((s == 2) & (i == 0))
        def _():
            w_remote(4, (*chips[1], 1 - c), me).wait_recv()
            w_store(2, *chips[1]).start()

        @pl.when((s == 3) & (i == 0))
        def _():
            w_remote(2, (*chips[2], c), me).wait_recv()
            w_pass(2).start()
            w_remote(5, (*chips[2], 1 - c), me).wait_recv()
            w_store(3, *chips[2]).start()

        xn, _ = _rms(x_ref[...])
        h = (xn * g_ref[...]).astype(BF16)
        keep_h = pltpu.make_async_copy(h_buf, h_ref.at[pl.ds(pl.multiple_of(i * tb, tb), tb), :], w_local.at[5])

        @pl.when(s == 0)
        def _():
            h_buf[...] = h
            keep_h.start()

        z_ref[...] = _dot(h, wv[jnp.bitwise_xor(2 * x + y, s)])
        pl.when(s == 0)(keep_h.wait)

        @pl.when((s == N_CHIPS - 1) & (i == nb - 1))
        def _():
            relay_rest()
            finish_rest()
            for cp in (w_first(0), w_first(1), w_relay(), w_pass(0), w_pass(1), w_pass(2)):
                cp.wait_send()
            w_store(0, x, y).wait()
            for idx in range(3):
                w_store(idx + 1, *chips[idx]).wait()

    rest_shape, rest_sems = _gather_shapes(shards)
    out_shape = [jax.ShapeDtypeStruct((t, IN_COLS), F32), jax.ShapeDtypeStruct((t, D_MODEL), BF16),
                 jax.ShapeDtypeStruct((D_MODEL, IN_COLS), BF16)] + rest_shape
    any_spec = pl.BlockSpec(memory_space=pl.ANY)

    def z_map(s, i):
        return (i, jnp.bitwise_xor(2 * lax.axis_index("x") + lax.axis_index("y"), s))

    return pl.pallas_call(
        body, name="in_proj", out_shape=tuple(out_shape),
        grid=(N_CHIPS, nb),
        in_specs=[pl.BlockSpec((tb, D_MODEL), lambda s, i: (i, 0)),
                  pl.BlockSpec((1, D_MODEL), lambda s, i: (0, 0)), any_spec] + [any_spec] * n,
        out_specs=tuple([pl.BlockSpec((tb, cols), z_map), any_spec, any_spec] + [any_spec] * n),
        scratch_shapes=[pltpu.VMEM((N_CHIPS, D_MODEL, cols), BF16), pltpu.VMEM((tb, D_MODEL), BF16)] + rest_sems + [
            pltpu.SemaphoreType.DMA((GATHER_SEMS,)), pltpu.SemaphoreType.DMA((GATHER_SEMS,)),
            pltpu.SemaphoreType.DMA((N_CHIPS + 2,))],
        compiler_params=pltpu.CompilerParams(dimension_semantics=("arbitrary", "arbitrary"),
                                             vmem_limit_bytes=VMEM_LIMIT_BYTES),
    )(x2d, norm_g, w_in_sh, *[sh[0] for sh in shards])


def _in_proj_bwd(dz, w_in, x2d, dx_res, norm_g, tb, reduce, shards):
    t = x2d.shape[0]
    nb = t // tb
    parts, wire, steps = reduce
    n = len(parts)
    k = len(shards)

    def body(dz_ref, w_ref, x_ref, dres_ref, g_ref, *refs):
        at = 2 * n + k
        dx_ref, dg_ref = refs[at:at + 2]
        rs_outs, g_outs = refs[at + 2:at + 2 + n], refs[at + 2 + n:at + 2 + n + k]
        scratch = refs[at + 2 + n + k:]
        rs = _rs_steps(parts, wire, refs[:2 * n], rs_outs, scratch[:len(scratch) - 3])
        for step, when in zip(rs, steps):
            pl.when(pl.program_id(0) == when)(step)
        gather = _gather_steps(shards, refs[2 * n:at], g_outs, *scratch[len(scratch) - 3:])
        for step, when in zip(gather, (0, nb // 2, nb - 1)):
            pl.when(pl.program_id(0) == when)(step)

        @pl.when(pl.program_id(0) == 0)
        def _():
            dg_ref[...] = jnp.zeros_like(dg_ref)

        xn, r = _rms(x_ref[...])
        g = g_ref[...]
        dh = _dot_nt(dz_ref[...], w_ref[...])
        dg_ref[0:1, :] += jnp.sum(dh * xn, axis=0, keepdims=True)
        dx_ref[...] = dres_ref[...] + _rms_bwd(dh * g, xn, r)

    row = lambda i: (i, 0)
    fixed = lambda i: (0, 0)
    rs_shape, rs_scratch = _rs_shapes(parts, wire)
    g_shape, g_sems = _gather_shapes(shards)
    any_spec = pl.BlockSpec(memory_space=pl.ANY)
    return pl.pallas_call(
        body, name="in_proj_bwd",
        out_shape=tuple([jax.ShapeDtypeStruct((t, D_MODEL), F32), jax.ShapeDtypeStruct((F32_SUBLANES, D_MODEL), F32)]
                        + rs_shape + g_shape),
        grid=(nb,),
        in_specs=[pl.BlockSpec((tb, IN_COLS), row),
                  pl.BlockSpec((D_MODEL, IN_COLS), fixed, pipeline_mode=pl.Buffered(1)),
                  pl.BlockSpec((tb, D_MODEL), row), pl.BlockSpec((tb, D_MODEL), row),
                  pl.BlockSpec((1, D_MODEL), fixed)] + [any_spec] * (2 * n + k),
        out_specs=tuple([pl.BlockSpec((tb, D_MODEL), row), pl.BlockSpec((F32_SUBLANES, D_MODEL), fixed)]
                        + [any_spec] * (n + k)),
        scratch_shapes=rs_scratch + g_sems,
        compiler_params=pltpu.CompilerParams(dimension_semantics=("arbitrary",),
                                             vmem_limit_bytes=VMEM_LIMIT_BYTES),
    )(dz, w_in, x2d, dx_res, norm_g, *_rs_operands(parts), *[sh[0] for sh in shards])


def _weight_grad(lhs, rhs, n_chunks, tb, name, reduce=None):
    t, k = lhs.shape
    nc = rhs.shape[1] // n_chunks
    nb = t // tb
    parts, wire, steps = reduce if reduce is not None else ([], F32, ())
    n = len(parts)

    def body(l_ref, r_ref, *refs):
        o_ref, o16_ref = refs[2 * n:2 * n + 2]
        if n:
            at = pl.program_id(0) * nb + pl.program_id(1)
            rs = _rs_steps(parts, wire, refs[:2 * n], refs[2 * n + 2:3 * n + 2], refs[3 * n + 2:])
            for step, when in zip(rs, steps):
                pl.when(at == when)(step)

        @pl.when(pl.program_id(1) == 0)
        def _():
            o_ref[...] = jnp.zeros_like(o_ref)

        o_ref[...] += _dot_tn(l_ref[...], r_ref[...])

        @pl.when(pl.program_id(1) == nb - 1)
        def _():
            o16_ref[...] = o_ref[...].astype(BF16)

    rs_shape, rs_scratch = _rs_shapes(parts, wire) if n else ([], [])
    any_spec = pl.BlockSpec(memory_space=pl.ANY)
    chunk = pl.BlockSpec((None, k, nc), lambda j, i: (j, 0, 0))
    return pl.pallas_call(
        body, name=name,
        out_shape=tuple([jax.ShapeDtypeStruct((n_chunks, k, nc), F32), jax.ShapeDtypeStruct((n_chunks, k, nc), BF16)]
                        + rs_shape),
        grid=(n_chunks, nb),
        in_specs=[pl.BlockSpec((tb, k), lambda j, i: (i, 0)), pl.BlockSpec((tb, nc), lambda j, i: (i, j))]
        + [any_spec] * (2 * n),
        out_specs=tuple([chunk, chunk] + [any_spec] * n),
        scratch_shapes=rs_scratch,
        compiler_params=pltpu.CompilerParams(dimension_semantics=("arbitrary", "arbitrary"),
                                             vmem_limit_bytes=VMEM_LIMIT_BYTES),
    )(lhs, rhs, *_rs_operands(parts))


def _adam_update(w, g, m, v):
    m_ = ADAM_B1 * m + (1.0 - ADAM_B1) * g
    v_ = ADAM_B2 * v + (1.0 - ADAM_B2) * jnp.square(g)
    m_hat = m_ / (1.0 - ADAM_B1 ** ADAM_STEP)
    v_hat = v_ / (1.0 - ADAM_B2 ** ADAM_STEP)
    return -ADAM_LR * (m_hat / (jnp.sqrt(v_hat) + ADAM_EPS) + ADAM_WD * w), m_, v_


def _adamw_replicated(vec_sum, mat_sum, norm_grad, entries, conv):
    n = len(entries)

    def grad_of(name, shape, vec_ref, mat_ref, norm_ref):
        if name == "norm_g":
            return norm_ref[0:1, :]
        if name in MAT_BAG_AT:
            return mat_ref[MAT_BAG_AT[name]:MAT_BAG_AT[name] + shape[0], :]
        if shape[0] == 1:
            return vec_ref[_bag_row(name), 0:shape[1]]
        return jnp.concatenate([vec_ref[_bag_row(name), h * shape[1]:(h + 1) * shape[1]] for h in range(shape[0])],
                               axis=0)

    def body(vec_ref, mat_ref, norm_ref, *refs):
        ins, outs = refs[:3 * n + 4], refs[3 * n + 4:]
        for k in range(n):
            w_ref, m_ref, v_ref = ins[3 * k:3 * k + 3]
            g = grad_of(entries[k][0], w_ref.shape, vec_ref, mat_ref, norm_ref)
            d, m_, v_ = _adam_update(w_ref[...], g, m_ref[...], v_ref[...])
            for ref, val in zip(outs[4 * k:4 * k + 4], (g, d, m_, v_)):
                ref[...] = val
        w_ref, m_ref, v_ref, g_ref = ins[3 * n:]
        for ref, val in zip(outs[4 * n:], _adam_update(w_ref[...], g_ref[...], m_ref[...], v_ref[...])):
            ref[...] = val

    arrays = [a for e in entries for a in e[1:]] + list(conv)
    out_shape = [jax.ShapeDtypeStruct(e[1].shape, F32) for e in entries for _ in range(4)]
    out_shape += [jax.ShapeDtypeStruct(conv[0].shape, F32)] * 3
    return pl.pallas_call(
        body, name="adamw_replicated", out_shape=tuple(out_shape),
        compiler_params=pltpu.CompilerParams(vmem_limit_bytes=VMEM_LIMIT_BYTES),
    )(vec_sum, mat_sum, norm_grad, *arrays)


def _adamw(w, g, m, v, rows, name):
    r, c = w.shape

    def body(w_ref, g_ref, m_ref, v_ref, d_ref, nm_ref, nv_ref):
        d_ref[...], nm_ref[...], nv_ref[...] = _adam_update(w_ref[...], g_ref[...], m_ref[...], v_ref[...])

    spec = pl.BlockSpec((rows, c), lambda i: (i, 0))
    return pl.pallas_call(
        body, name=name, out_shape=tuple(jax.ShapeDtypeStruct((r, c), F32) for _ in range(3)),
        grid=(r // rows,), in_specs=[spec] * 4, out_specs=(spec,) * 3,
        compiler_params=pltpu.CompilerParams(dimension_semantics=("arbitrary",),
                                             vmem_limit_bytes=VMEM_LIMIT_BYTES),
    )(w, g, m, v)


def _shift_down(ext, s):
    return pltpu.roll(ext, s, 0)


def _tile_shift(v, s):
    rows, cols = v.shape
    tiles = v.reshape(rows // F32_SUBLANES, F32_SUBLANES, cols)
    return pltpu.roll(tiles, s % F32_SUBLANES, 1).reshape(rows, cols)


def _shift_up(ext, s):
    return pltpu.roll(ext, ext.shape[0] - s, 0)


def _lru_gates(xc, wa_ref, ba, wx_ref, bx, lam):
    pa, px = [], []
    for h in range(LRU_HEADS):
        xh = xc[:, h * HEAD_DIM:(h + 1) * HEAD_DIM].astype(BF16)
        pa.append(_dot(xh, wa_ref[h]))
        px.append(_dot(xh, wx_ref[h]))
    r = _sigmoid(jnp.concatenate(pa, axis=1) + ba)
    ig = _sigmoid(jnp.concatenate(px, axis=1) + bx)
    sp = _softplus(-lam)
    log_a = (-LRU_C * r) * sp
    a = jnp.exp(log_a)
    mult = jnp.sqrt(jnp.tanh(-log_a) * (1.0 + a * a))
    return r, ig, a, mult, sp


def _conv(ext, w_ref, b):
    y = b + _shift_down(ext, 3) * w_ref[0:1, :]
    y = y + _shift_down(ext, 2) * w_ref[1:2, :]
    y = y + _shift_down(ext, 1) * w_ref[2:3, :]
    y = y + ext * w_ref[3:4, :]
    return y[CONV_HIST:, :]


def _pool_diff(ext, pos):
    out = []
    for g, k in enumerate(POOL_WINDOWS):
        col = ext[:, g * POOL_GROUP_DIM:(g + 1) * POOL_GROUP_DIM]
        s = col
        for step in range(g + 1):
            s = s + _shift_down(s, 2 ** step)
        count = jnp.minimum(pos + 1, k).astype(F32)
        out.append(s[POOL_HIST:, :] / count - col[POOL_HIST:, :])
    return out


def _pool_mix(diff, pw_ref):
    return jnp.concatenate([_dot(diff[g].astype(BF16), pw_ref[g]) for g in range(len(POOL_WINDOWS))], axis=1)


def _branch_specs(tb, row_map, fixed):
    fixed3 = lambda i: (0, 0, 0)
    return [pl.BlockSpec((CONV_WIDTH, D_MODEL), fixed), pl.BlockSpec((1, D_MODEL), fixed),
            pl.BlockSpec((LRU_HEADS, HEAD_DIM, HEAD_DIM), fixed3), pl.BlockSpec((1, D_MODEL), fixed),
            pl.BlockSpec((LRU_HEADS, HEAD_DIM, HEAD_DIM), fixed3), pl.BlockSpec((1, D_MODEL), fixed),
            pl.BlockSpec((1, D_MODEL), fixed),
            pl.BlockSpec((len(POOL_WINDOWS), POOL_GROUP_DIM, POOL_GROUP_DIM), fixed3),
            pl.BlockSpec((1, POOL_WIDTH), fixed)]


def _branches_fwd(z, weights, seq, tb, shards):
    t = z.shape[0]
    nb = t // tb
    nbe = seq // tb
    groups = tb // F32_SUBLANES
    n = len(shards)

    def body(xa_ref, ga_ref, xb_ref, gb_ref, cw_ref, cb_ref, wa_ref, ba_ref, wx_ref, bx_ref, lam_ref,
             pw_ref, ps_ref, *refs):
        g_ins = refs[:n]
        ya_ref, yb_ref, hl_ref = refs[n:n + 3]
        g_outs = refs[n + 3:2 * n + 3]
        xa_ext, xb_ext, carry, a_s, u_s, send_sems, recv_sems, local_sems = refs[2 * n + 3:]
        blk = pl.program_id(0) % nbe
        start_gather, relay_gather, finish_gather = _gather_steps(shards, g_ins, g_outs, send_sems, recv_sems,
                                                                  local_sems)
        pl.when(pl.program_id(0) == 0)(start_gather)
        pl.when(pl.program_id(0) == nb // 2)(relay_gather)

        @pl.when(blk == 0)
        def _():
            xa_ext[0:CONV_HIST, :] = jnp.zeros((CONV_HIST, D_MODEL), F32)
            xb_ext[0:POOL_HIST, :] = jnp.zeros((POOL_HIST, POOL_WIDTH), F32)
            carry[...] = jnp.zeros_like(carry)

        xa_ext[CONV_HIST:, :] = xa_ref[...]
        xb_ext[POOL_HIST:, :] = xb_ref[...]
        ea = xa_ext[...]
        eb = xb_ext[...]
        xa_ext[0:CONV_HIST, :] = ea[tb:, :]
        xb_ext[0:POOL_HIST, :] = eb[tb:, :]

        xc = _conv(ea, cw_ref, cb_ref[...])
        _, ig, a, mult, _ = _lru_gates(xc, wa_ref, ba_ref[...], wx_ref, bx_ref[...], lam_ref[...])
        u = mult * (ig * xc)
        row8 = lax.broadcasted_iota(jnp.int32, (tb, D_MODEL), 0) % F32_SUBLANES
        for s in (1, 2, 4):
            m = row8 >= s
            u = jnp.where(m, a * _tile_shift(u, s) + u, u)
            a = jnp.where(m, a * _tile_shift(a, s), a)
        a_s[...] = a
        u_s[...] = u

        def step(g, cr):
            sl = pl.ds(pl.multiple_of(g * F32_SUBLANES, F32_SUBLANES), F32_SUBLANES)
            hb = a_s[sl, :] * cr + u_s[sl, :]
            hl_ref[sl, :] = hb
            return jnp.broadcast_to(hb[F32_SUBLANES - 1:F32_SUBLANES, :], (F32_SUBLANES, D_MODEL))

        carry[...] = lax.fori_loop(0, groups, step, carry[...], unroll=4)
        ga = ga_ref[...]
        ya_ref[...] = (hl_ref[...] * (ga * _sigmoid(ga))).astype(BF16)

        pos = blk * tb + lax.broadcasted_iota(jnp.int32, (tb, POOL_GROUP_DIM), 0)
        ypre = _pool_mix(_pool_diff(eb, pos), pw_ref)
        gb = gb_ref[...]
        yb_ref[...] = ((ypre * ps_ref[...]) * (gb * _sigmoid(gb))).astype(BF16)
        pl.when(pl.program_id(0) == nb - 1)(finish_gather)

    row = lambda i: (i, 0)
    fixed = lambda i: (0, 0)
    any_spec = pl.BlockSpec(memory_space=pl.ANY)
    in_specs = [pl.BlockSpec((tb, D_MODEL), lambda i: (i, 0)), pl.BlockSpec((tb, D_MODEL), lambda i: (i, 1)),
                pl.BlockSpec((tb, POOL_WIDTH), lambda i: (i, 4)), pl.BlockSpec((tb, POOL_WIDTH), lambda i: (i, 5)),
                ] + _branch_specs(tb, row, fixed) + [any_spec] * n
    g_shape, g_sems = _gather_shapes(shards)
    return pl.pallas_call(
        body, name="branches_fwd",
        out_shape=tuple([jax.ShapeDtypeStruct((t, D_MODEL), BF16), jax.ShapeDtypeStruct((t, POOL_WIDTH), BF16),
                         jax.ShapeDtypeStruct((t, D_MODEL), F32)] + g_shape),
        grid=(nb,), in_specs=in_specs,
        out_specs=tuple([pl.BlockSpec((tb, D_MODEL), row), pl.BlockSpec((tb, POOL_WIDTH), row),
                         pl.BlockSpec((tb, D_MODEL), row)] + [any_spec] * n),
        scratch_shapes=[pltpu.VMEM((tb + CONV_HIST, D_MODEL), F32), pltpu.VMEM((tb + POOL_HIST, POOL_WIDTH), F32),
                        pltpu.VMEM((F32_SUBLANES, D_MODEL), F32),
                        pltpu.VMEM((tb, D_MODEL), F32), pltpu.VMEM((tb, D_MODEL), F32)] + g_sems,
        compiler_params=pltpu.CompilerParams(dimension_semantics=("arbitrary",),
                                             vmem_limit_bytes=VMEM_LIMIT_BYTES),
    )(z, z, z, z, *weights, *[sh[0] for sh in shards])


def _branches_bwd(z, hl, dya, dyb, dzm, weights, vec_bag, seq, tb):
    t = z.shape[0]
    nb = t // tb
    nbe = seq // tb
    groups = tb // F32_SUBLANES

    def body(xa_ref, xap_ref, ga_ref, xb_ref, xbp_ref, gb_ref, hl_ref, hlp_ref, dya_ref, dyb_ref, dzm_ref,
             cw_ref, cb_ref, wa_ref, ba_ref, wx_ref, bx_ref, lam_ref, pw_ref, ps_ref, vec_in_ref,
             dz_ref, vec_ref, mat_ref,
             xa_ext, xb_ext, hl_ext, a_ext, dxc_ext, dwin_ext, g_carry, b_s, d_s, g_s):
        i = pl.program_id(0)
        blk = (nb - 1 - i) % nbe

        def mat_rows(name, k):
            at = MAT_BAG_AT[name] + k * HEAD_DIM
            return slice(at, at + HEAD_DIM)

        @pl.when(i == 0)
        def _():
            vec_ref[...] = vec_in_ref[...]
            mat_ref[...] = jnp.zeros_like(mat_ref)

        @pl.when(blk == nbe - 1)
        def _():
            a_ext[tb:, :] = jnp.zeros((F32_SUBLANES, D_MODEL), F32)
            dxc_ext[tb:, :] = jnp.zeros((CONV_HIST, D_MODEL), F32)
            dwin_ext[tb:, :] = jnp.zeros((POOL_HIST, POOL_WIDTH), F32)
            g_carry[...] = jnp.zeros_like(g_carry)

        live = (blk > 0).astype(F32)
        xa_ext[0:CONV_HIST, :] = xap_ref[...] * live
        xa_ext[CONV_HIST:, :] = xa_ref[...]
        xb_ext[0:POOL_HIST, :] = xbp_ref[...] * live
        xb_ext[POOL_HIST:, :] = xb_ref[...]
        hl_ext[0:F32_SUBLANES, :] = hlp_ref[...] * live
        hl_ext[F32_SUBLANES:, :] = hl_ref[...]
        ea = xa_ext[...]
        eb = xb_ext[...]

        xc = _conv(ea, cw_ref, cb_ref[...])
        lam = lam_ref[...]
        r, ig, a, mult, sp = _lru_gates(xc, wa_ref, ba_ref[...], wx_ref, bx_ref[...], lam)
        hl = hl_ref[...]
        ga = ga_ref[...]
        sga = _sigmoid(ga)
        dya = dya_ref[...]
        dhl = dya * (ga * sga)
        dz_ref[:, D_MODEL:2 * D_MODEL] = (dya * hl * (sga * (1.0 + ga * (1.0 - sga)))).astype(BF16)

        a_ext[0:tb, :] = a
        b = _shift_up(a_ext[...], 1)[0:tb, :]
        a_ext[tb:, :] = jnp.broadcast_to(a[0:1, :], (F32_SUBLANES, D_MODEL))
        d = dhl
        row8 = lax.broadcasted_iota(jnp.int32, (tb, D_MODEL), 0) % F32_SUBLANES
        for s in (1, 2, 4):
            m = row8 < F32_SUBLANES - s
            d = jnp.where(m, d + b * _tile_shift(d, -s), d)
            b = jnp.where(m, b * _tile_shift(b, -s), b)
        b_s[...] = b
        d_s[...] = d

        def step(k, cr):
            sl = pl.ds(pl.multiple_of((groups - 1 - k) * F32_SUBLANES, F32_SUBLANES), F32_SUBLANES)
            gb_ = d_s[sl, :] + b_s[sl, :] * cr
            g_s[sl, :] = gb_
            return jnp.broadcast_to(gb_[0:1, :], (F32_SUBLANES, D_MODEL))

        g_carry[...] = lax.fori_loop(0, groups, step, g_carry[...], unroll=4)
        gsc = g_s[...]
        da = gsc * _shift_down(hl_ext[...], 1)[F32_SUBLANES:, :]
        dmult = gsc * (ig * xc)
        dig = gsc * (mult * xc)
        dxc = gsc * (mult * ig)
        dlog_a = da * a - (a * a) * dmult / mult
        dr = dlog_a * (-LRU_C * sp)
        vec_ref[_bag_row("lru_lambda"), :] += jnp.sum(dlog_a * (-LRU_C * r), axis=0, keepdims=True)
        dpa = dr * (r * (1.0 - r))
        dpx = dig * (ig * (1.0 - ig))
        vec_ref[_bag_row("lru_b_a"), :] += jnp.sum(dpa, axis=0, keepdims=True)
        vec_ref[_bag_row("lru_b_x"), :] += jnp.sum(dpx, axis=0, keepdims=True)
        back = []
        for h in range(LRU_HEADS):
            cols = slice(h * HEAD_DIM, (h + 1) * HEAD_DIM)
            xh = xc[:, cols].astype(BF16)
            dpa_h = dpa[:, cols].astype(BF16)
            dpx_h = dpx[:, cols].astype(BF16)
            mat_ref[mat_rows("lru_w_a", h), :] += _dot_tn(xh, dpa_h)
            mat_ref[mat_rows("lru_w_x", h), :] += _dot_tn(xh, dpx_h)
            back.append(_dot_nt(dpa_h, wa_ref[h]) + _dot_nt(dpx_h, wx_ref[h]))
        dxc = dxc + jnp.concatenate(back, axis=1)
        vec_ref[_bag_row("conv_b"), :] += jnp.sum(dxc, axis=0, keepdims=True)
        for k in range(CONV_WIDTH):
            tap = _shift_down(ea, CONV_WIDTH - 1 - k)[CONV_HIST:, :] if k < CONV_WIDTH - 1 else ea[CONV_HIST:, :]
            vec_ref[_bag_row("conv_w", k), :] += jnp.sum(dxc * tap, axis=0, keepdims=True)
        dxc_ext[0:tb, :] = dxc
        ed = dxc_ext[...]
        dxa = ed * cw_ref[3:4, :]
        dxa = dxa + _shift_up(ed, 1) * cw_ref[2:3, :]
        dxa = dxa + _shift_up(ed, 2) * cw_ref[1:2, :]
        dxa = dxa + _shift_up(ed, 3) * cw_ref[0:1, :]
        dz_ref[:, 0:D_MODEL] = dxa[0:tb, :].astype(BF16)
        dxc_ext[tb:, :] = dxc[0:CONV_HIST, :]

        pos = blk * tb + lax.broadcasted_iota(jnp.int32, (tb, POOL_GROUP_DIM), 0)
        diff = _pool_diff(eb, pos)
        ypre = _pool_mix(diff, pw_ref)
        ps = ps_ref[...]
        gb = gb_ref[...]
        sgb = _sigmoid(gb)
        dyb = dyb_ref[...]
        dyp = dyb * (gb * sgb)
        dz_ref[:, 2 * D_MODEL + POOL_WIDTH:3 * D_MODEL] = (
            dyb * (ypre * ps) * (sgb * (1.0 + gb * (1.0 - sgb)))).astype(BF16)
        vec_ref[_bag_row("pool_scale"), 0:POOL_WIDTH] += jnp.sum(dyp * ypre, axis=0, keepdims=True)
        dypre = dyp * ps
        for g, k in enumerate(POOL_WINDOWS):
            cols = slice(g * POOL_GROUP_DIM, (g + 1) * POOL_GROUP_DIM)
            dyg = dypre[:, cols].astype(BF16)
            mat_ref[mat_rows("pool_w", g), :] += _dot_tn(diff[g].astype(BF16), dyg)
            ddiff = _dot_nt(dyg, pw_ref[g])
            count = jnp.minimum(pos + 1, k).astype(F32)
            dwin = ddiff / count
            dwin_ext[0:tb, cols] = dwin
            s = dwin_ext[:, cols]
            for step_ in range(g + 1):
                s = s + _shift_up(s, 2 ** step_)
            dz_ref[:, 2 * D_MODEL + g * POOL_GROUP_DIM:2 * D_MODEL + (g + 1) * POOL_GROUP_DIM] = (
                s[0:tb, :] - ddiff).astype(BF16)
            dwin_ext[tb:, cols] = dwin[0:POOL_HIST, :]

        dz_ref[:, 3 * D_MODEL:] = dzm_ref[...]

        @pl.when(i == nb - 1)
        def _():
            row = _bag_row("lru_lambda")
            vec_ref[row, :] = vec_ref[row, :] * (-_sigmoid(-lam))

    rev = lambda i: (nb - 1 - i, 0)
    fixed = lambda i: (0, 0)

    def prev(rows, col):
        per = tb // rows
        return lambda i: (jnp.maximum((nb - 1 - i) * per - 1, 0), col)

    in_specs = [pl.BlockSpec((tb, D_MODEL), lambda i: (nb - 1 - i, 0)),
                pl.BlockSpec((CONV_HIST, D_MODEL), prev(CONV_HIST, 0)),
                pl.BlockSpec((tb, D_MODEL), lambda i: (nb - 1 - i, 1)),
                pl.BlockSpec((tb, POOL_WIDTH), lambda i: (nb - 1 - i, 4)),
                pl.BlockSpec((POOL_HIST, POOL_WIDTH), prev(POOL_HIST, 4)),
                pl.BlockSpec((tb, POOL_WIDTH), lambda i: (nb - 1 - i, 5)),
                pl.BlockSpec((tb, D_MODEL), rev),
                pl.BlockSpec((F32_SUBLANES, D_MODEL), prev(F32_SUBLANES, 0)),
                pl.BlockSpec((tb, D_MODEL), rev), pl.BlockSpec((tb, POOL_WIDTH), rev),
                pl.BlockSpec((tb, 2 * D_MODEL), rev)] + _branch_specs(tb, rev, fixed) + [
                    pl.BlockSpec((VEC_BAG_ROWS, D_MODEL), fixed)]
    out_shape = (jax.ShapeDtypeStruct((t, IN_COLS), BF16), jax.ShapeDtypeStruct((VEC_BAG_ROWS, D_MODEL), F32),
                 jax.ShapeDtypeStruct((MAT_BAG_ROWS, HEAD_DIM), F32))
    out_specs = (pl.BlockSpec((tb, IN_COLS), rev), pl.BlockSpec((VEC_BAG_ROWS, D_MODEL), fixed),
                 pl.BlockSpec((MAT_BAG_ROWS, HEAD_DIM), fixed))
    scratch = [pltpu.VMEM((tb + CONV_HIST, D_MODEL), F32), pltpu.VMEM((tb + POOL_HIST, POOL_WIDTH), F32),
               pltpu.VMEM((tb + F32_SUBLANES, D_MODEL), F32), pltpu.VMEM((tb + F32_SUBLANES, D_MODEL), F32),
               pltpu.VMEM((tb + CONV_HIST, D_MODEL), F32), pltpu.VMEM((tb + POOL_HIST, POOL_WIDTH), F32),
               pltpu.VMEM((F32_SUBLANES, D_MODEL), F32),
               pltpu.VMEM((tb, D_MODEL), F32), pltpu.VMEM((tb, D_MODEL), F32), pltpu.VMEM((tb, D_MODEL), F32)]
    return pl.pallas_call(
        body, name="branches_bwd", out_shape=out_shape, grid=(nb,), in_specs=in_specs, out_specs=out_specs,
        scratch_shapes=scratch, input_output_aliases={len(in_specs) - 1: 1},
        compiler_params=pltpu.CompilerParams(dimension_semantics=("arbitrary",),
                                             vmem_limit_bytes=VMEM_LIMIT_BYTES),
    )(z, z, z, z, z, z, hl, hl, dya, dyb, dzm, *weights, vec_bag)


def _merge_head(x2d, ya, yb, z, p2d, tgt, w_pl, w_pp, w_out, w_pg, w_pe, g2, gf, tb):
    t = x2d.shape[0]
    p_dim = p2d.shape[1]

    def body(x_ref, ya_ref, yb_ref, ma_ref, mb_ref, p_ref, t_ref, wpl_ref, wpp_ref, wout_ref, wpg_ref, wpe_ref,
             g2_ref, gf_ref,
             bag_ref, dxr_ref, dya_ref, dyb_ref, dzm_ref,
             mg_ref, do_ref, hn_ref, dgp_ref, dpe_ref, da_ref, dbm_ref, pbf_ref):
        @pl.when(pl.program_id(0) == 0)
        def _():
            bag_ref[...] = jnp.zeros_like(bag_ref)

        a_ = _dot(ya_ref[...], wpl_ref[...])
        bm = _dot(yb_ref[...], wpp_ref[...])
        sa = _sigmoid(ma_ref[...])
        sb = _sigmoid(mb_ref[...])
        mg = (sa * a_ + sb * bm).astype(BF16)
        mg_ref[...] = mg
        x1 = x_ref[...] + _dot(mg, wout_ref[...])
        xn2, r2 = _rms(x1)
        g2 = g2_ref[...]
        hn = (xn2 * g2).astype(BF16)
        hn_ref[...] = hn
        gate = _sigmoid(_dot(hn, wpg_ref[...]))
        pbf = p_ref[...].astype(BF16)
        pbf_ref[...] = pbf
        pe = _dot(pbf, wpe_ref[...])
        x2 = x1 + gate * pe
        xn3, r3 = _rms(x2)
        gf = gf_ref[...]
        err = xn3 * gf - t_ref[...]
        bag_ref[_bag_rows("loss"), 0:128] += 0.5 * jnp.sum(jnp.mean(err * err, axis=-1))

        dy = err * (1.0 / D_MODEL)
        bag_ref[_bag_row("final_g"), :] += jnp.sum(dy * xn3, axis=0, keepdims=True)
        dx2 = _rms_bwd(dy * gf, xn3, r3)
        dpe_ref[...] = (dx2 * gate).astype(BF16)
        dgp = ((dx2 * pe) * (gate * (1.0 - gate))).astype(BF16)
        dgp_ref[...] = dgp
        dhn = _dot_nt(dgp, wpg_ref[...])
        bag_ref[_bag_row("ple_norm_g"), :] += jnp.sum(dhn * xn2, axis=0, keepdims=True)
        dx1 = dx2 + _rms_bwd(dhn * g2, xn2, r2)
        dxr_ref[...] = dx1
        do = dx1.astype(BF16)
        do_ref[...] = do
        dmg = _dot_nt(do, wout_ref[...])
        da = (dmg * sa).astype(BF16)
        dbm = (dmg * sb).astype(BF16)
        da_ref[...] = da
        dbm_ref[...] = dbm
        dzm_ref[:, 0:D_MODEL] = (dmg * a_ * (sa * (1.0 - sa))).astype(BF16)
        dzm_ref[:, D_MODEL:] = (dmg * bm * (sb * (1.0 - sb))).astype(BF16)
        dya_ref[...] = _dot_nt(da, wpl_ref[...])
        dyb_ref[...] = _dot_nt(dbm, wpp_ref[...])

    row = lambda i: (i, 0)
    fixed = lambda i: (0, 0)

    def resident(shape):
        return pl.BlockSpec(shape, fixed, pipeline_mode=pl.Buffered(1))

    tok = lambda width: pl.BlockSpec((tb, width), row)
    in_specs = [tok(D_MODEL), tok(D_MODEL), tok(POOL_WIDTH),
                pl.BlockSpec((tb, D_MODEL), lambda i: (i, 3)), pl.BlockSpec((tb, D_MODEL), lambda i: (i, 4)),
                tok(p_dim), tok(D_MODEL),
                resident((D_MODEL, D_MODEL)), resident((POOL_WIDTH, D_MODEL)), resident((D_MODEL, D_MODEL)),
                resident((D_MODEL, D_MODEL)), resident((p_dim, D_MODEL)),
                pl.BlockSpec((1, D_MODEL), fixed), pl.BlockSpec((1, D_MODEL), fixed)]
    bf = lambda width: jax.ShapeDtypeStruct((t, width), BF16)
    f32 = lambda width: jax.ShapeDtypeStruct((t, width), F32)
    out_shape = (jax.ShapeDtypeStruct((VEC_BAG_ROWS, D_MODEL), F32),
                 f32(D_MODEL), f32(D_MODEL), f32(POOL_WIDTH), bf(2 * D_MODEL),
                 bf(D_MODEL), bf(D_MODEL), bf(D_MODEL), bf(D_MODEL), bf(D_MODEL), bf(D_MODEL), bf(D_MODEL), bf(p_dim))
    out_specs = (pl.BlockSpec((VEC_BAG_ROWS, D_MODEL), fixed),
                 tok(D_MODEL), tok(D_MODEL), tok(POOL_WIDTH), tok(2 * D_MODEL),
                 tok(D_MODEL), tok(D_MODEL), tok(D_MODEL), tok(D_MODEL), tok(D_MODEL), tok(D_MODEL), tok(D_MODEL),
                 tok(p_dim))
    return pl.pallas_call(
        body, name="merge_head", out_shape=out_shape, grid=(t // tb,), in_specs=in_specs, out_specs=out_specs,
        compiler_params=pltpu.CompilerParams(dimension_semantics=("arbitrary",),
                                             vmem_limit_bytes=VMEM_LIMIT_BYTES),
    )(x2d, ya, yb, z, z, p2d, tgt, w_pl, w_pp, w_out, w_pg, w_pe, g2, gf)


def kernel(x, p, norm_g, w_in, conv_w, conv_b, lru_w_a, lru_b_a, lru_w_x, lru_b_x, lru_lambda, pool_w, pool_scale, w_proj_lru, w_proj_pool, w_out, ple_norm_g, w_ple_gate, w_ple_proj, final_g, loss_target, m_norm_g, m_w_in, m_conv_w, m_conv_b, m_lru_w_a, m_lru_b_a, m_lru_w_x, m_lru_b_x, m_lru_lambda, m_pool_w, m_pool_scale, m_w_proj_lru, m_w_proj_pool, m_w_out, m_ple_norm_g, m_w_ple_gate, m_w_ple_proj, m_final_g, v_norm_g, v_w_in, v_conv_w, v_conv_b, v_lru_w_a, v_lru_b_a, v_lru_w_x, v_lru_b_x, v_lru_lambda, v_pool_w, v_pool_scale, v_w_proj_lru, v_w_proj_pool, v_w_out, v_ple_norm_g, v_w_ple_gate, v_w_ple_proj, v_final_g):
    bsz, seq, _ = x.shape
    t = bsz * seq
    tb_mm = min(512, seq)
    tb_seq = min(256, seq // 2) if seq >= 512 else seq
    x2d = x.reshape(t, D_MODEL)
    p2d = p.reshape(t, p.shape[-1])
    tgt = loss_target.reshape(t, D_MODEL)
    chip = 2 * lax.axis_index("x") + lax.axis_index("y")

    rest = [(w_proj_lru[0], 0), (w_proj_pool[0], 1), (w_out[0], 0), (w_ple_gate[0], 0), (w_ple_proj[0], 1)]
    z, h_bf, w_in_f, conv_w_f = _in_proj_gather(x2d, norm_g, w_in[0].astype(BF16), [(conv_w[0], 1, False)], tb_mm)

    wa_bf = lru_w_a[0].astype(BF16)
    wx_bf = lru_w_x[0].astype(BF16)
    pw_bf = pool_w[0].astype(BF16)
    branch_w = (conv_w_f, conv_b, wa_bf, lru_b_a.reshape(1, D_MODEL), wx_bf, lru_b_x.reshape(1, D_MODEL),
                lru_lambda, pw_bf, pool_scale)

    ya, yb, hl, w_pl_f, w_pp_f, w_out_f, w_pg_f, w_pe_f = _branches_fwd(
        z, branch_w, seq, tb_seq, [(w.astype(BF16), axis, True) for w, axis in rest])
    (vec_bag, dx_res, dya, dyb, dzm, mg_bf, do_bf, hn_bf, dgp_bf, dpe_bf, da_bf, dbm_bf, p_bf) = _merge_head(
        x2d, ya, yb, z, p2d, tgt, w_pl_f, w_pp_f, w_out_f, w_pg_f, w_pe_f, ple_norm_g, final_g.reshape(1, D_MODEL),
        tb_seq)
    dz, vec_bag, mat_bag = _branches_bwd(z, hl, dya, dyb, dzm, branch_w, vec_bag, seq, tb_seq)

    tb_dw = min(1024, seq)
    g_pl = _weight_grad(ya, da_bf, 1, tb_dw, "dw_proj_lru")[0].reshape(8, D_MODEL // 8, D_MODEL)
    g_pp = _weight_grad(yb, dbm_bf, 1, tb_dw, "dw_proj_pool")[0][0]
    g_out = _weight_grad(mg_bf, do_bf, 1, tb_dw, "dw_out")[0].reshape(8, D_MODEL // 8, D_MODEL)
    g_pg = _weight_grad(hn_bf, dgp_bf, 1, tb_dw, "dw_ple_gate")[0].reshape(8, D_MODEL // 8, D_MODEL)
    p_dim = p2d.shape[1]
    g_pe = _weight_grad(p_bf, dpe_bf, 1, tb_dw, "dw_ple_proj")[0][0]
    nb_dw = t // tb_dw
    g_in, g_in16, r_pl, r_pp, r_out, r_pg, r_pe, vec_mine, mat_mine = _weight_grad(
        h_bf, dz, N_CHIPS, tb_dw, "dw_in",
        reduce=([(g_pl, False, None), (g_pp, True, None), (g_out, False, None), (g_pg, False, None),
                 (g_pe, True, None), (vec_bag.reshape(8, VEC_BAG_ROWS // 8, D_MODEL), False, None),
                 (mat_bag.reshape(8, MAT_BAG_ROWS // 8, HEAD_DIM), False, None)],
                [BF16] * 5 + [F32] * 2, (0, nb_dw, 3 * nb_dw + nb_dw // 2, N_CHIPS * nb_dw - 1)))
    pieces = (8, D_MODEL // 2, IN_COLS // N_CHIPS)
    nb_seq = t // tb_seq
    dx, d_g1, r_in, vec_sum, mat_sum = _in_proj_bwd(
        dz, w_in_f, x2d, dx_res, norm_g, tb_seq,
        reduce=([(g_in.reshape(pieces), False, g_in16.reshape(pieces))], BF16, (0, nb_seq // 4, nb_seq - 1, nb_seq - 1)),
        shards=[(vec_mine.reshape(VEC_BAG_ROWS // N_CHIPS, D_MODEL), 0, True),
                (mat_mine.reshape(MAT_BAG_ROWS // N_CHIPS, HEAD_DIM), 0, True)])
    g_g1 = _all_reduce_tile(d_g1, "allreduce_norm_g")

    def big_update(w, g2d, m, v, rows, name):
        d, nm, nv = _adamw(w[0], g2d, m[0], v[0], rows, name)
        return g2d[None], d[None], nm[None], nv[None]

    u_in = big_update(w_in, r_in.reshape(D_MODEL, IN_COLS // N_CHIPS), m_w_in, v_w_in, 256, "adamw_w_in")
    u_pl = big_update(w_proj_lru, r_pl.reshape(D_MODEL // N_CHIPS, D_MODEL), m_w_proj_lru, v_w_proj_lru, 256, "adamw_w_proj_lru")
    u_pp = big_update(w_proj_pool, r_pp.reshape(POOL_WIDTH, D_MODEL // N_CHIPS), m_w_proj_pool, v_w_proj_pool, 512, "adamw_w_proj_pool")
    u_out = big_update(w_out, r_out.reshape(D_MODEL // N_CHIPS, D_MODEL), m_w_out, v_w_out, 256, "adamw_w_out")
    u_pg = big_update(w_ple_gate, r_pg.reshape(D_MODEL // N_CHIPS, D_MODEL), m_w_ple_gate, v_w_ple_gate, 256, "adamw_w_ple_gate")
    u_pe = big_update(w_ple_proj, r_pe.reshape(p_dim, D_MODEL // N_CHIPS), m_w_ple_proj, v_w_ple_proj, 256, "adamw_w_ple_proj")

    small = [("norm_g", norm_g, m_norm_g, v_norm_g), ("conv_b", conv_b, m_conv_b, v_conv_b),
             ("lru_w_a", lru_w_a, m_lru_w_a, v_lru_w_a), ("lru_b_a", lru_b_a, m_lru_b_a, v_lru_b_a),
             ("lru_w_x", lru_w_x, m_lru_w_x, v_lru_w_x), ("lru_b_x", lru_b_x, m_lru_b_x, v_lru_b_x),
             ("lru_lambda", lru_lambda, m_lru_lambda, v_lru_lambda), ("pool_w", pool_w, m_pool_w, v_pool_w),
             ("pool_scale", pool_scale, m_pool_scale, v_pool_scale),
             ("ple_norm_g", ple_norm_g, m_ple_norm_g, v_ple_norm_g), ("final_g", final_g, m_final_g, v_final_g)]

    def view(a):
        return a.reshape(-1, a.shape[-1]) if a.ndim != 3 else a[0]

    cw_at = F32_SUBLANES * VEC_BAG_SLOTS.index("conv_w")
    cw_cols = D_MODEL // N_CHIPS
    g_cw = lax.dynamic_slice(vec_sum, (cw_at, chip * cw_cols), (CONV_WIDTH, cw_cols))
    flat = _adamw_replicated(vec_sum, mat_sum, g_g1, [(name,) + tuple(view(a) for a in arrs) for name, *arrs in small],
                             (conv_w[0], m_conv_w[0], v_conv_w[0], g_cw))
    u_small = {name: tuple(flat[4 * k + pick].reshape(arrs[0].shape) for pick in range(4))
               for k, (name, *arrs) in enumerate(small)}
    u_cw = tuple(a[None] for a in (g_cw,) + tuple(flat[4 * len(small):]))

    loss = vec_sum[F32_SUBLANES * VEC_BAG_SLOTS.index("loss"), 0]
    grad_x = dx.reshape(bsz, seq, D_MODEL)

    def ordered(pick):
        s = {name: u[pick] for name, u in u_small.items()}
        return [s["norm_g"], u_in[pick], u_cw[pick], s["conv_b"], s["lru_w_a"], s["lru_b_a"], s["lru_w_x"], s["lru_b_x"],
                s["lru_lambda"], s["pool_w"], s["pool_scale"], u_pl[pick], u_pp[pick], u_out[pick], s["ple_norm_g"],
                u_pg[pick], u_pe[pick], s["final_g"]]

    return (loss, grad_x, *ordered(0), *ordered(1), *ordered(2), *ordered(3))
```

```python
import jax
import jax.numpy as jnp
from jax import lax
from jax.experimental import pallas as pl
from jax.experimental.pallas import tpu as pltpu

F32 = jnp.float32
BF16 = jnp.bfloat16
MESH = pl.DeviceIdType.MESH

D_MODEL = 1024
LRU_HEADS = 8
HEAD_DIM = 128
CONV_WIDTH = 4
LRU_C = 8.0
POOL_WIDTH = 512
POOL_WINDOWS = (2, 4, 8, 16)
POOL_GROUP_DIM = 128
IN_COLS = 5120
N_CHIPS = 4
EPS = 1e-6

ADAM_LR = 0.001
ADAM_B1 = 0.9
ADAM_B2 = 0.999
ADAM_EPS = 1e-08
ADAM_WD = 0.01
ADAM_STEP = 10

F32_SUBLANES = 8
CONV_HIST = 8
POOL_HIST = 16
VMEM_LIMIT_BYTES = 58 * 1024 * 1024
VEC_BAG_SLOTS = ("norm_g", "conv_w", "conv_b", "lru_b_a", "lru_b_x", "lru_lambda", "pool_scale", "ple_norm_g",
                 "final_g", "loss")
VEC_BAG_ROWS = 128
MAT_BAG_AT = {"lru_w_a": 0, "lru_w_x": LRU_HEADS * HEAD_DIM, "pool_w": 2 * LRU_HEADS * HEAD_DIM}
MAT_BAG_ROWS = 2 * LRU_HEADS * HEAD_DIM + len(POOL_WINDOWS) * POOL_GROUP_DIM


def _bag_row(name, k=0):
    at = F32_SUBLANES * VEC_BAG_SLOTS.index(name) + k
    return slice(at, at + 1)


def _bag_rows(name):
    at = F32_SUBLANES * VEC_BAG_SLOTS.index(name)
    return slice(at, at + F32_SUBLANES)


def _dot(a, b):
    return jnp.dot(a, b, preferred_element_type=F32)


def _dot_nt(a, b):
    return lax.dot_general(a, b, (((1,), (1,)), ((), ())), preferred_element_type=F32)


def _dot_tn(a, b):
    return lax.dot_general(a, b, (((0,), (0,)), ((), ())), preferred_element_type=F32)


def _sigmoid(v):
    return jax.nn.sigmoid(v)


def _softplus(v):
    return jnp.maximum(v, 0.0) + jnp.log1p(jnp.exp(-jnp.abs(v)))


def _place():
    return lax.axis_index("x"), lax.axis_index("y"), lax.axis_index("c")


GATHER_SEMS = 6


def _gather_shapes(shards):
    out_shape = []
    for arr, axis, _ in shards:
        r, cols = arr.shape
        out_shape.append(jax.ShapeDtypeStruct((N_CHIPS * r, cols) if axis == 0 else (r, N_CHIPS * cols), arr.dtype))
    n = len(shards)
    sems = [pltpu.SemaphoreType.DMA((n * GATHER_SEMS,)), pltpu.SemaphoreType.DMA((n * GATHER_SEMS,)),
            pltpu.SemaphoreType.DMA((n,))]
    return out_shape, sems


def _gather_steps(shards, ins, outs, send_sems, recv_sems, local_sems):
    n = len(shards)
    x, y, c = _place()
    me, sibling = (x, y, c), (x, y, 1 - c)
    chips = [(x, 1 - y), (1 - x, y), (1 - x, 1 - y)]

    def region(k, cx, cy, hc):
        (r, cols), axis = shards[k][0].shape, shards[k][1]
        j = 2 * cx + cy
        if axis == 0:
            if hc is None:
                return outs[k].at[pl.ds(j * r, r), :]
            return outs[k].at[pl.ds(j * r + hc * (r // 2), r // 2), :]
        if hc is None:
            return outs[k].at[:, pl.ds(j * cols, cols)]
        return outs[k].at[pl.ds(hc * (r // 2), r // 2), pl.ds(j * cols, cols)]

    def remote(k, sem, block, to, src=None):
        dst = region(k, *block)
        return pltpu.make_async_remote_copy(
            src_ref=dst if src is None else src, dst_ref=dst,
            send_sem=send_sems.at[k * GATHER_SEMS + sem], recv_sem=recv_sems.at[k * GATHER_SEMS + sem],
            device_id=to, device_id_type=MESH)

    def first(k, idx):
        r, split = shards[k][0].shape[0], shards[k][2]
        src = ins[k].at[pl.ds(c * (r // 2), r // 2), :] if split else ins[k]
        return remote(k, idx, (x, y, c if split else None), (*chips[idx], c), src=src)

    def relay(k):
        src_chip = (jnp.bitwise_xor(x, 1 - c), jnp.bitwise_xor(y, c))
        dst_chip = (jnp.bitwise_xor(x, c), jnp.bitwise_xor(y, 1 - c))
        return remote(k, 2, (*src_chip, c), (*dst_chip, c))

    def passed(k, idx):
        return remote(k, 3 + idx, (*chips[idx], c), sibling)

    def mine(k):
        return pltpu.make_async_copy(ins[k], region(k, x, y, None), local_sems.at[k])

    def start():
        for k in range(n):
            mine(k).start()
            for idx in range(2 if shards[k][2] else 3):
                first(k, idx).start()

    def relay_on():
        for k in range(n):
            split = shards[k][2]
            for idx in range(2):
                remote(k, idx, (*chips[idx], c if split else None), me).wait_recv()
            if split:
                relay(k).start()
                passed(k, 0).start()
                passed(k, 1).start()

    def finish():
        for k in range(n):
            split = shards[k][2]
            remote(k, 2, (*chips[2], c if split else None), me).wait_recv()
            if split:
                passed(k, 2).start()
        for k in range(n):
            if shards[k][2]:
                for idx in range(3):
                    remote(k, 3 + idx, (*chips[idx], 1 - c), me).wait_recv()
        for k in range(n):
            if shards[k][2]:
                for cp in (first(k, 0), first(k, 1), relay(k), passed(k, 0), passed(k, 1), passed(k, 2)):
                    cp.wait_send()
            else:
                for idx in range(3):
                    first(k, idx).wait_send()
            mine(k).wait()

    return start, relay_on, finish


def _gather_shards(shards, name):
    n = len(shards)

    def body(*refs):
        for step in _gather_steps(shards, refs[:n], refs[n:2 * n], *refs[2 * n:]):
            step()

    out_shape, sems = _gather_shapes(shards)
    any_spec = pl.BlockSpec(memory_space=pl.ANY)
    return pl.pallas_call(
        body, name=name, out_shape=tuple(out_shape),
        in_specs=[any_spec] * n, out_specs=tuple([any_spec] * n), scratch_shapes=sems,
    )(*[s[0] for s in shards])


RS_ADD_ROWS = (64, 56, 32, 16, 8)


def _all_reduce_tile(v, name):
    n_dev = 2 * N_CHIPS
    flips = [(dx, dy, dc) for dx in (0, 1) for dy in (0, 1) for dc in (0, 1)][1:]

    def body(v_ref, o_ref, slots, send_sems, recv_sems):
        x, y, c = _place()
        mine = 4 * x + 2 * y + c

        def copy(k, to_flip, slot):
            dx, dy, dc = to_flip
            peer = (jnp.bitwise_xor(x, dx), jnp.bitwise_xor(y, dy), jnp.bitwise_xor(c, dc))
            return pltpu.make_async_remote_copy(
                src_ref=v_ref, dst_ref=slots.at[slot], send_sem=send_sems.at[k], recv_sem=recv_sems.at[k],
                device_id=peer, device_id_type=MESH)

        sends = [copy(k, flip, mine) for k, flip in enumerate(flips)]
        for cp in sends:
            cp.start()
        slots[mine] = v_ref[...]
        for k, (dx, dy, dc) in enumerate(flips):
            copy(k, (dx, dy, dc), jnp.bitwise_xor(mine, 4 * dx + 2 * dy + dc)).wait_recv()
        total = slots[0]
        for d in range(1, n_dev):
            total = total + slots[d]
        o_ref[...] = total
        for cp in sends:
            cp.wait_send()

    return pl.pallas_call(
        body, name=name, out_shape=jax.ShapeDtypeStruct(v.shape, F32),
        in_specs=[pl.BlockSpec(memory_space=pltpu.VMEM)], out_specs=pl.BlockSpec(memory_space=pltpu.VMEM),
        scratch_shapes=[pltpu.VMEM((n_dev,) + v.shape, F32), pltpu.SemaphoreType.DMA((n_dev - 1,)),
                        pltpu.SemaphoreType.DMA((n_dev - 1,))],
    )(v)


RS_SEMS = 8
RS_LOCAL_SEMS = 5


def _rs_piece_shape(part):
    arr, cols = part[0], part[1]
    return (arr.shape[0] // 2, arr.shape[1] // N_CHIPS) if cols else tuple(arr.shape[1:])


def _rs_operands(parts):
    return [p[0] for p in parts] + [p[0] if p[2] is None else p[2] for p in parts]


def _rs_wires(parts, wire):
    return list(wire) if isinstance(wire, (list, tuple)) else [wire] * len(parts)


def _rs_shapes(parts, wire):
    n = len(parts)
    shapes = [_rs_piece_shape(p) for p in parts]
    out_shape = [jax.ShapeDtypeStruct((2,) + s, F32) for s in shapes]
    scratch = []
    for lead, kind in ((N_CHIPS, "f32"), (N_CHIPS, "narrow"), (N_CHIPS, "wire"), (None, "f32"), (N_CHIPS, "wire")):
        for s, p, w in zip(shapes, parts, _rs_wires(parts, wire)):
            dtype = {"f32": F32, "narrow": F32 if p[2] is None else p[2].dtype, "wire": w}[kind]
            scratch.append(pltpu.VMEM(s if lead is None else (lead,) + s, dtype))
    scratch += [pltpu.SemaphoreType.DMA((n * RS_SEMS,)), pltpu.SemaphoreType.DMA((n * RS_SEMS,)),
                pltpu.SemaphoreType.DMA((n * RS_LOCAL_SEMS,))]
    return out_shape, scratch


def _rs_steps(parts, wire, ins, outs, scratch):
    n = len(parts)
    own, sib, got, fin, snd = (scratch[k * n:(k + 1) * n] for k in range(5))
    send_sems, recv_sems, local_sems = scratch[5 * n:]
    shapes = [_rs_piece_shape(p) for p in parts]
    x, y, c = _place()
    j_me = 2 * x + y
    me, sibling = (x, y, c), (x, y, 1 - c)
    chips = [(x, 1 - y), (1 - x, y), (1 - x, 1 - y)]

    def piece(a, jj, core, narrow=False):
        ref = ins[n + a] if narrow else ins[a]
        if parts[a][1]:
            r, cl = shapes[a]
            return ref.at[pl.ds(core * r, r), pl.ds(jj * cl, cl)]
        return ref.at[2 * jj + core]

    def remote(a, sem, src, dst, to):
        return pltpu.make_async_remote_copy(
            src_ref=src, dst_ref=dst, send_sem=send_sems.at[a * RS_SEMS + sem],
            recv_sem=recv_sems.at[a * RS_SEMS + sem], device_id=to, device_id_type=MESH)

    def rows_loop(a, fn):
        r = shapes[a][0]
        step = max(s for s in RS_ADD_ROWS if r % s == 0)

        def it(i, carry):
            fn(pl.ds(pl.multiple_of(i * step, step), step))
            return carry

        lax.fori_loop(0, r // step, it, 0)

    def load(a, jj):
        return pltpu.make_async_copy(piece(a, jj, c), own[a].at[jj], local_sems.at[a * RS_LOCAL_SEMS + jj])

    def to_sibling(a, jj):
        return remote(a, jj, piece(a, jj, 1 - c, narrow=True), sib[a].at[jj], sibling)

    def to_owner(a, idx):
        chip = chips[idx]
        return remote(a, 4 + idx, snd[a].at[2 * chip[0] + chip[1]], got[a].at[j_me], (*chip, c))

    def store(a):
        return pltpu.make_async_copy(fin[a], outs[a].at[c], local_sems.at[a * RS_LOCAL_SEMS + 4])

    def result_to_sibling(a):
        return remote(a, 7, fin[a], outs[a].at[c], sibling)

    def exchange():
        for a in range(n):
            for jj in range(N_CHIPS):
                load(a, jj).start()
                to_sibling(a, jj).start()

    def chip_sums():
        for a in range(n):
            for jj in range(N_CHIPS):
                load(a, jj).wait()
                remote(a, jj, sib[a].at[jj], sib[a].at[jj], me).wait_recv()

                def add(sl, a=a, jj=jj):
                    q = own[a][jj, sl, :] + sib[a][jj, sl, :].astype(F32)
                    own[a][jj, sl, :] = q
                    snd[a][jj, sl, :] = q.astype(snd[a].dtype)

                rows_loop(a, add)
        for a in range(n):
            for idx in range(3):
                to_owner(a, idx).start()
        for a in range(n):
            def keep(sl, a=a):
                got[a][j_me, sl, :] = snd[a][j_me, sl, :]

            rows_loop(a, keep)

    def totals():
        for a in range(n):
            for idx, chip in enumerate(chips):
                slot = got[a].at[2 * chip[0] + chip[1]]
                remote(a, 4 + idx, slot, slot, me).wait_recv()

            def total(sl, a=a):
                mine = own[a][j_me, sl, :]
                term = [jnp.where(j_me == jj, mine, got[a][jj, sl, :].astype(F32)) for jj in range(N_CHIPS)]
                fin[a][sl, :] = ((term[0] + term[1]) + term[2]) + term[3]

            rows_loop(a, total)
            store(a).start()
            result_to_sibling(a).start()

    def finish():
        for a in range(n):
            remote(a, 7, outs[a].at[1 - c], outs[a].at[1 - c], me).wait_recv()
        for a in range(n):
            for jj in range(N_CHIPS):
                to_sibling(a, jj).wait_send()
            for idx in range(3):
                to_owner(a, idx).wait_send()
            result_to_sibling(a).wait_send()
            store(a).wait()

    return exchange, chip_sums, totals, finish


def _reduce_scatter(parts, name, wire=F32):
    n = len(parts)

    def body(*refs):
        for step in _rs_steps(parts, wire, refs[:2 * n], refs[2 * n:3 * n], refs[3 * n:]):
            step()

    out_shape, scratch = _rs_shapes(parts, wire)
    any_spec = pl.BlockSpec(memory_space=pl.ANY)
    return pl.pallas_call(
        body, name=name, out_shape=tuple(out_shape),
        in_specs=[any_spec] * (2 * n), out_specs=tuple([any_spec] * n), scratch_shapes=scratch,
        compiler_params=pltpu.CompilerParams(vmem_limit_bytes=VMEM_LIMIT_BYTES),
    )(*_rs_operands(parts))


def _rms(x):
    r = lax.rsqrt(jnp.mean(x * x, axis=-1, keepdims=True) + EPS)
    return x * r, r


def _rms_bwd(dxn, xn, r):
    return r * (dxn - xn * jnp.mean(dxn * xn, axis=-1, keepdims=True))


def _in_proj_gather(x2d, norm_g, w_in_sh, shards, tb):
    t = x2d.shape[0]
    nb = t // tb
    cols = IN_COLS // N_CHIPS
    half = D_MODEL // 2
    n = len(shards)

    def body(x_ref, g_ref, win_ref, *refs):
        ins = refs[:n]
        z_ref, h_ref, wfull_ref = refs[n:n + 3]
        outs = refs[n + 3:2 * n + 3]
        wv, h_buf, send_sems, recv_sems, local_sems, w_send, w_recv, w_local = refs[2 * n + 3:]
        s, i = pl.program_id(0), pl.program_id(1)
        x, y, c = _place()
        me, sibling = (x, y, c), (x, y, 1 - c)
        chips = [(x, 1 - y), (1 - x, y), (1 - x, 1 - y)]

        def w_half(cx, cy, hc):
            return wv.at[2 * cx + cy, pl.ds(hc * half, half), :]

        def w_remote(sem, block, to, src=None):
            dst = w_half(*block)
            return pltpu.make_async_remote_copy(
                src_ref=dst if src is None else src, dst_ref=dst, send_sem=w_send.at[sem],
                recv_sem=w_recv.at[sem], device_id=to, device_id_type=MESH)

        def w_first(idx):
            return w_remote(idx, (x, y, c), (*chips[idx], c), src=win_ref.at[pl.ds(c * half, half), :])

        def w_relay():
            src_chip = (jnp.bitwise_xor(x, 1 - c), jnp.bitwise_xor(y, c))
            dst_chip = (jnp.bitwise_xor(x, c), jnp.bitwise_xor(y, 1 - c))
            return w_remote(2, (*src_chip, c), (*dst_chip, c))

        def w_pass(idx):
            return w_remote(3 + idx, (*chips[idx], c), sibling)

        def w_store(k, cx, cy):
            jj = 2 * cx + cy
            return pltpu.make_async_copy(wv.at[jj], wfull_ref.at[:, pl.ds(jj * cols, cols)], w_local.at[k])

        start_rest, relay_rest, finish_rest = _gather_steps(shards, ins, outs, send_sems, recv_sems, local_sems)
        own = pltpu.make_async_copy(win_ref, wv.at[2 * x + y], w_local.at[4])

        @pl.when((s == 0) & (i == 0))
        def _():
            own.start()
            w_first(0).start()
            w_first(1).start()
            start_rest()
            own.wait()
            w_store(0, x, y).start()

        @pl.when((s == 1) & (i == 0))
        def _():
            w_remote(0, (*chips[0], c), me).wait_recv()
            w_remote(1, (*chips[1], c), me).wait_recv()
            w_relay().start()
            w_pass(0).start()
            w_pass(1).start()
            w_remote(3, (*chips[0], 1 - c), me).wait_recv()
            w_store(1, *chips[0]).start()

        @pl.when((s == 2) & (i == 0))
        def _():
            w_remote(4, (*chips[1], 1 - c), me).wait_recv()
            w_store(2, *chips[1]).start()

        @pl.when((s == 3) & (i == 0))
        def _():
            w_remote(2, (*chips[2], c), me).wait_recv()
            w_pass(2).start()
            w_remote(5, (*chips[2], 1 - c), me).wait_recv()
            w_store(3, *chips[2]).start()

        xn, _ = _rms(x_ref[...])
        h = (xn * g_ref[...]).astype(BF16)
        keep_h = pltpu.make_async_copy(h_buf, h_ref.at[pl.ds(pl.multiple_of(i * tb, tb), tb), :], w_local.at[5])

        @pl.when(s == 0)
        def _():
            h_buf[...] = h
            keep_h.start()

        z_ref[...] = _dot(h, wv[jnp.bitwise_xor(2 * x + y, s)])
        pl.when(s == 0)(keep_h.wait)

        @pl.when((s == N_CHIPS - 1) & (i == nb - 1))
        def _():
            relay_rest()
            finish_rest()
            for cp in (w_first(0), w_first(1), w_relay(), w_pass(0), w_pass(1), w_pass(2)):
                cp.wait_send()
            w_store(0, x, y).wait()
            for idx in range(3):
                w_store(idx + 1, *chips[idx]).wait()

    rest_shape, rest_sems = _gather_shapes(shards)
    out_shape = [jax.ShapeDtypeStruct((t, IN_COLS), F32), jax.ShapeDtypeStruct((t, D_MODEL), BF16),
                 jax.ShapeDtypeStruct((D_MODEL, IN_COLS), BF16)] + rest_shape
    any_spec = pl.BlockSpec(memory_space=pl.ANY)

    def z_map(s, i):
        return (i, jnp.bitwise_xor(2 * lax.axis_index("x") + lax.axis_index("y"), s))

    return pl.pallas_call(
        body, name="in_proj", out_shape=tuple(out_shape),
        grid=(N_CHIPS, nb),
        in_specs=[pl.BlockSpec((tb, D_MODEL), lambda s, i: (i, 0)),
                  pl.BlockSpec((1, D_MODEL), lambda s, i: (0, 0)), any_spec] + [any_spec] * n,
        out_specs=tuple([pl.BlockSpec((tb, cols), z_map), any_spec, any_spec] + [any_spec] * n),
        scratch_shapes=[pltpu.VMEM((N_CHIPS, D_MODEL, cols), BF16), pltpu.VMEM((tb, D_MODEL), BF16)] + rest_sems + [
            pltpu.SemaphoreType.DMA((GATHER_SEMS,)), pltpu.SemaphoreType.DMA((GATHER_SEMS,)),
            pltpu.SemaphoreType.DMA((N_CHIPS + 2,))],
        compiler_params=pltpu.CompilerParams(dimension_semantics=("arbitrary", "arbitrary"),
                                             vmem_limit_bytes=VMEM_LIMIT_BYTES),
    )(x2d, norm_g, w_in_sh, *[sh[0] for sh in shards])


def _in_proj_bwd(dz, w_in, x2d, dx_res, norm_g, tb, reduce, shards):
    t = x2d.shape[0]
    nb = t // tb
    parts, wire, steps = reduce
    n = len(parts)
    k = len(shards)

    def body(dz_ref, w_ref, x_ref, dres_ref, g_ref, *refs):
        at = 2 * n + k
        dx_ref, dg_ref = refs[at:at + 2]
        rs_outs, g_outs = refs[at + 2:at + 2 + n], refs[at + 2 + n:at + 2 + n + k]
        scratch = refs[at + 2 + n + k:]
        rs = _rs_steps(parts, wire, refs[:2 * n], rs_outs, scratch[:len(scratch) - 3])
        for step, when in zip(rs, steps):
            pl.when(pl.program_id(0) == when)(step)
        gather = _gather_steps(shards, refs[2 * n:at], g_outs, *scratch[len(scratch) - 3:])
        for step, when in zip(gather, (0, nb // 2, nb - 1)):
            pl.when(pl.program_id(0) == when)(step)

        @pl.when(pl.program_id(0) == 0)
        def _():
            dg_ref[...] = jnp.zeros_like(dg_ref)

        xn, r = _rms(x_ref[...])
        g = g_ref[...]
        dh = _dot_nt(dz_ref[...], w_ref[...])
        dg_ref[0:1, :] += jnp.sum(dh * xn, axis=0, keepdims=True)
        dx_ref[...] = dres_ref[...] + _rms_bwd(dh * g, xn, r)

    row = lambda i: (i, 0)
    fixed = lambda i: (0, 0)
    rs_shape, rs_scratch = _rs_shapes(parts, wire)
    g_shape, g_sems = _gather_shapes(shards)
    any_spec = pl.BlockSpec(memory_space=pl.ANY)
    return pl.pallas_call(
        body, name="in_proj_bwd",
        out_shape=tuple([jax.ShapeDtypeStruct((t, D_MODEL), F32), jax.ShapeDtypeStruct((F32_SUBLANES, D_MODEL), F32)]
                        + rs_shape + g_shape),
        grid=(nb,),
        in_specs=[pl.BlockSpec((tb, IN_COLS), row),
                  pl.BlockSpec((D_MODEL, IN_COLS), fixed, pipeline_mode=pl.Buffered(1)),
                  pl.BlockSpec((tb, D_MODEL), row), pl.BlockSpec((tb, D_MODEL), row),
                  pl.BlockSpec((1, D_MODEL), fixed)] + [any_spec] * (2 * n + k),
        out_specs=tuple([pl.BlockSpec((tb, D_MODEL), row), pl.BlockSpec((F32_SUBLANES, D_MODEL), fixed)]
                        + [any_spec] * (n + k)),
        scratch_shapes=rs_scratch + g_sems,
        compiler_params=pltpu.CompilerParams(dimension_semantics=("arbitrary",),
                                             vmem_limit_bytes=VMEM_LIMIT_BYTES),
    )(dz, w_in, x2d, dx_res, norm_g, *_rs_operands(parts), *[sh[0] for sh in shards])


def _weight_grad(lhs, rhs, n_chunks, tb, name, reduce=None):
    t, k = lhs.shape
    nc = rhs.shape[1] // n_chunks
    nb = t // tb
    parts, wire, steps = reduce if reduce is not None else ([], F32, ())
    n = len(parts)

    def body(l_ref, r_ref, *refs):
        o_ref, o16_ref = refs[2 * n:2 * n + 2]
        if n:
            at = pl.program_id(0) * nb + pl.program_id(1)
            rs = _rs_steps(parts, wire, refs[:2 * n], refs[2 * n + 2:3 * n + 2], refs[3 * n + 2:])
            for step, when in zip(rs, steps):
                pl.when(at == when)(step)

        @pl.when(pl.program_id(1) == 0)
        def _():
            o_ref[...] = jnp.zeros_like(o_ref)

        o_ref[...] += _dot_tn(l_ref[...], r_ref[...])

        @pl.when(pl.program_id(1) == nb - 1)
        def _():
            o16_ref[...] = o_ref[...].astype(BF16)

    rs_shape, rs_scratch = _rs_shapes(parts, wire) if n else ([], [])
    any_spec = pl.BlockSpec(memory_space=pl.ANY)
    chunk = pl.BlockSpec((None, k, nc), lambda j, i: (j, 0, 0))
    return pl.pallas_call(
        body, name=name,
        out_shape=tuple([jax.ShapeDtypeStruct((n_chunks, k, nc), F32), jax.ShapeDtypeStruct((n_chunks, k, nc), BF16)]
                        + rs_shape),
        grid=(n_chunks, nb),
        in_specs=[pl.BlockSpec((tb, k), lambda j, i: (i, 0)), pl.BlockSpec((tb, nc), lambda j, i: (i, j))]
        + [any_spec] * (2 * n),
        out_specs=tuple([chunk, chunk] + [any_spec] * n),
        scratch_shapes=rs_scratch,
        compiler_params=pltpu.CompilerParams(dimension_semantics=("arbitrary", "arbitrary"),
                                             vmem_limit_bytes=VMEM_LIMIT_BYTES),
    )(lhs, rhs, *_rs_operands(parts))


def _adam_update(w, g, m, v):
    m_ = ADAM_B1 * m + (1.0 - ADAM_B1) * g
    v_ = ADAM_B2 * v + (1.0 - ADAM_B2) * jnp.square(g)
    m_hat = m_ / (1.0 - ADAM_B1 ** ADAM_STEP)
    v_hat = v_ / (1.0 - ADAM_B2 ** ADAM_STEP)
    return -ADAM_LR * (m_hat / (jnp.sqrt(v_hat) + ADAM_EPS) + ADAM_WD * w), m_, v_


def _adamw_replicated(vec_sum, mat_sum, norm_grad, entries, conv):
    n = len(entries)

    def grad_of(name, shape, vec_ref, mat_ref, norm_ref):
        if name == "norm_g":
            return norm_ref[0:1, :]
        if name in MAT_BAG_AT:
            return mat_ref[MAT_BAG_AT[name]:MAT_BAG_AT[name] + shape[0], :]
        if shape[0] == 1:
            return vec_ref[_bag_row(name), 0:shape[1]]
        return jnp.concatenate([vec_ref[_bag_row(name), h * shape[1]:(h + 1) * shape[1]] for h in range(shape[0])],
                               axis=0)

    def body(vec_ref, mat_ref, norm_ref, *refs):
        ins, outs = refs[:3 * n + 4], refs[3 * n + 4:]
        for k in range(n):
            w_ref, m_ref, v_ref = ins[3 * k:3 * k + 3]
            g = grad_of(entries[k][0], w_ref.shape, vec_ref, mat_ref, norm_ref)
            d, m_, v_ = _adam_update(w_ref[...], g, m_ref[...], v_ref[...])
            for ref, val in zip(outs[4 * k:4 * k + 4], (g, d, m_, v_)):
                ref[...] = val
        w_ref, m_ref, v_ref, g_ref = ins[3 * n:]
        for ref, val in zip(outs[4 * n:], _adam_update(w_ref[...], g_ref[...], m_ref[...], v_ref[...])):
            ref[...] = val

    arrays = [a for e in entries for a in e[1:]] + list(conv)
    out_shape = [jax.ShapeDtypeStruct(e[1].shape, F32) for e in entries for _ in range(4)]
    out_shape += [jax.ShapeDtypeStruct(conv[0].shape, F32)] * 3
    return pl.pallas_call(
        body, name="adamw_replicated", out_shape=tuple(out_shape),
        compiler_params=pltpu.CompilerParams(vmem_limit_bytes=VMEM_LIMIT_BYTES),
    )(vec_sum, mat_sum, norm_grad, *arrays)


def _adamw(w, g, m, v, rows, name):
    r, c = w.shape

    def body(w_ref, g_ref, m_ref, v_ref, d_ref, nm_ref, nv_ref):
        d_ref[...], nm_ref[...], nv_ref[...] = _adam_update(w_ref[...], g_ref[...], m_ref[...], v_ref[...])

    spec = pl.BlockSpec((rows, c), lambda i: (i, 0))
    return pl.pallas_call(
        body, name=name, out_shape=tuple(jax.ShapeDtypeStruct((r, c), F32) for _ in range(3)),
        grid=(r // rows,), in_specs=[spec] * 4, out_specs=(spec,) * 3,
        compiler_params=pltpu.CompilerParams(dimension_semantics=("arbitrary",),
                                             vmem_limit_bytes=VMEM_LIMIT_BYTES),
    )(w, g, m, v)


def _shift_down(ext, s):
    return pltpu.roll(ext, s, 0)


def _tile_shift(v, s):
    rows, cols = v.shape
    tiles = v.reshape(rows // F32_SUBLANES, F32_SUBLANES, cols)
    return pltpu.roll(tiles, s % F32_SUBLANES, 1).reshape(rows, cols)


def _shift_up(ext, s):
    return pltpu.roll(ext, ext.shape[0] - s, 0)


def _lru_gates(xc, wa_ref, ba, wx_ref, bx, lam):
    pa, px = [], []
    for h in range(LRU_HEADS):
        xh = xc[:, h * HEAD_DIM:(h + 1) * HEAD_DIM].astype(BF16)
        pa.append(_dot(xh, wa_ref[h]))
        px.append(_dot(xh, wx_ref[h]))
    r = _sigmoid(jnp.concatenate(pa, axis=1) + ba)
    ig = _sigmoid(jnp.concatenate(px, axis=1) + bx)
    sp = _softplus(-lam)
    log_a = (-LRU_C * r) * sp
    a = jnp.exp(log_a)
    mult = jnp.sqrt(jnp.tanh(-log_a) * (1.0 + a * a))
    return r, ig, a, mult, sp


def _conv(ext, w_ref, b):
    y = b + _shift_down(ext, 3) * w_ref[0:1, :]
    y = y + _shift_down(ext, 2) * w_ref[1:2, :]
    y = y + _shift_down(ext, 1) * w_ref[2:3, :]
    y = y + ext * w_ref[3:4, :]
    return y[CONV_HIST:, :]


def _pool_diff(ext, pos):
    out = []
    for g, k in enumerate(POOL_WINDOWS):
        col = ext[:, g * POOL_GROUP_DIM:(g + 1) * POOL_GROUP_DIM]
        s = col
        for step in range(g + 1):
            s = s + _shift_down(s, 2 ** step)
        count = jnp.minimum(pos + 1, k).astype(F32)
        out.append(s[POOL_HIST:, :] / count - col[POOL_HIST:, :])
    return out


def _pool_mix(diff, pw_ref):
    return jnp.concatenate([_dot(diff[g].astype(BF16), pw_ref[g]) for g in range(len(POOL_WINDOWS))], axis=1)


def _branch_specs(tb, row_map, fixed):
    fixed3 = lambda i: (0, 0, 0)
    return [pl.BlockSpec((CONV_WIDTH, D_MODEL), fixed), pl.BlockSpec((1, D_MODEL), fixed),
            pl.BlockSpec((LRU_HEADS, HEAD_DIM, HEAD_DIM), fixed3), pl.BlockSpec((1, D_MODEL), fixed),
            pl.BlockSpec((LRU_HEADS, HEAD_DIM, HEAD_DIM), fixed3), pl.BlockSpec((1, D_MODEL), fixed),
            pl.BlockSpec((1, D_MODEL), fixed),
            pl.BlockSpec((len(POOL_WINDOWS), POOL_GROUP_DIM, POOL_GROUP_DIM), fixed3),
            pl.BlockSpec((1, POOL_WIDTH), fixed)]


def _branches_fwd(z, weights, seq, tb, shards):
    t = z.shape[0]
    nb = t // tb
    nbe = seq // tb
    groups = tb // F32_SUBLANES
    n = len(shards)

    def body(xa_ref, ga_ref, xb_ref, gb_ref, cw_ref, cb_ref, wa_ref, ba_ref, wx_ref, bx_ref, lam_ref,
             pw_ref, ps_ref, *refs):
        g_ins = refs[:n]
        ya_ref, yb_ref, hl_ref = refs[n:n + 3]
        g_outs = refs[n + 3:2 * n + 3]
        xa_ext, xb_ext, carry, a_s, u_s, send_sems, recv_sems, local_sems = refs[2 * n + 3:]
        blk = pl.program_id(0) % nbe
        start_gather, relay_gather, finish_gather = _gather_steps(shards, g_ins, g_outs, send_sems, recv_sems,
                                                                  local_sems)
        pl.when(pl.program_id(0) == 0)(start_gather)
        pl.when(pl.program_id(0) == nb // 2)(relay_gather)

        @pl.when(blk == 0)
        def _():
            xa_ext[0:CONV_HIST, :] = jnp.zeros((CONV_HIST, D_MODEL), F32)
            xb_ext[0:POOL_HIST, :] = jnp.zeros((POOL_HIST, POOL_WIDTH), F32)
            carry[...] = jnp.zeros_like(carry)

        xa_ext[CONV_HIST:, :] = xa_ref[...]
        xb_ext[POOL_HIST:, :] = xb_ref[...]
        ea = xa_ext[...]
        eb = xb_ext[...]
        xa_ext[0:CONV_HIST, :] = ea[tb:, :]
        xb_ext[0:POOL_HIST, :] = eb[tb:, :]

        xc = _conv(ea, cw_ref, cb_ref[...])
        _, ig, a, mult, _ = _lru_gates(xc, wa_ref, ba_ref[...], wx_ref, bx_ref[...], lam_ref[...])
        u = mult * (ig * xc)
        row8 = lax.broadcasted_iota(jnp.int32, (tb, D_MODEL), 0) % F32_SUBLANES
        for s in (1, 2, 4):
            m = row8 >= s
            u = jnp.where(m, a * _tile_shift(u, s) + u, u)
            a = jnp.where(m, a * _tile_shift(a, s), a)
        a_s[...] = a
        u_s[...] = u

        def step(g, cr):
            sl = pl.ds(pl.multiple_of(g * F32_SUBLANES, F32_SUBLANES), F32_SUBLANES)
            hb = a_s[sl, :] * cr + u_s[sl, :]
            hl_ref[sl, :] = hb
            return jnp.broadcast_to(hb[F32_SUBLANES - 1:F32_SUBLANES, :], (F32_SUBLANES, D_MODEL))

        carry[...] = lax.fori_loop(0, groups, step, carry[...], unroll=4)
        ga = ga_ref[...]
        ya_ref[...] = (hl_ref[...] * (ga * _sigmoid(ga))).astype(BF16)

        pos = blk * tb + lax.broadcasted_iota(jnp.int32, (tb, POOL_GROUP_DIM), 0)
        ypre = _pool_mix(_pool_diff(eb, pos), pw_ref)
        gb = gb_ref[...]
        yb_ref[...] = ((ypre * ps_ref[...]) * (gb * _sigmoid(gb))).astype(BF16)
        pl.when(pl.program_id(0) == nb - 1)(finish_gather)

    row = lambda i: (i, 0)
    fixed = lambda i: (0, 0)
    any_spec = pl.BlockSpec(memory_space=pl.ANY)
    in_specs = [pl.BlockSpec((tb, D_MODEL), lambda i: (i, 0)), pl.BlockSpec((tb, D_MODEL), lambda i: (i, 1)),
                pl.BlockSpec((tb, POOL_WIDTH), lambda i: (i, 4)), pl.BlockSpec((tb, POOL_WIDTH), lambda i: (i, 5)),
                ] + _branch_specs(tb, row, fixed) + [any_spec] * n
    g_shape, g_sems = _gather_shapes(shards)
    return pl.pallas_call(
        body, name="branches_fwd",
        out_shape=tuple([jax.ShapeDtypeStruct((t, D_MODEL), BF16), jax.ShapeDtypeStruct((t, POOL_WIDTH), BF16),
                         jax.ShapeDtypeStruct((t, D_MODEL), F32)] + g_shape),
        grid=(nb,), in_specs=in_specs,
        out_specs=tuple([pl.BlockSpec((tb, D_MODEL), row), pl.BlockSpec((tb, POOL_WIDTH), row),
                         pl.BlockSpec((tb, D_MODEL), row)] + [any_spec] * n),
        scratch_shapes=[pltpu.VMEM((tb + CONV_HIST, D_MODEL), F32), pltpu.VMEM((tb + POOL_HIST, POOL_WIDTH), F32),
                        pltpu.VMEM((F32_SUBLANES, D_MODEL), F32),
                        pltpu.VMEM((tb, D_MODEL), F32), pltpu.VMEM((tb, D_MODEL), F32)] + g_sems,
        compiler_params=pltpu.CompilerParams(dimension_semantics=("arbitrary",),
                                             vmem_limit_bytes=VMEM_LIMIT_BYTES),
    )(z, z, z, z, *weights, *[sh[0] for sh in shards])


def _branches_bwd(z, hl, dya, dyb, dzm, weights, vec_bag, seq, tb):
    t = z.shape[0]
    nb = t // tb
    nbe = seq // tb
    groups = tb // F32_SUBLANES

    def body(xa_ref, xap_ref, ga_ref, xb_ref, xbp_ref, gb_ref, hl_ref, hlp_ref, dya_ref, dyb_ref, dzm_ref,
             cw_ref, cb_ref, wa_ref, ba_ref, wx_ref, bx_ref, lam_ref, pw_ref, ps_ref, vec_in_ref,
             dz_ref, vec_ref, mat_ref,
             xa_ext, xb_ext, hl_ext, a_ext, dxc_ext, dwin_ext, g_carry, b_s, d_s, g_s):
        i = pl.program_id(0)
        blk = (nb - 1 - i) % nbe

        def mat_rows(name, k):
            at = MAT_BAG_AT[name] + k * HEAD_DIM
            return slice(at, at + HEAD_DIM)

        @pl.when(i == 0)
        def _():
            vec_ref[...] = vec_in_ref[...]
            mat_ref[...] = jnp.zeros_like(mat_ref)

        @pl.when(blk == nbe - 1)
        def _():
            a_ext[tb:, :] = jnp.zeros((F32_SUBLANES, D_MODEL), F32)
            dxc_ext[tb:, :] = jnp.zeros((CONV_HIST, D_MODEL), F32)
            dwin_ext[tb:, :] = jnp.zeros((POOL_HIST, POOL_WIDTH), F32)
            g_carry[...] = jnp.zeros_like(g_carry)

        live = (blk > 0).astype(F32)
        xa_ext[0:CONV_HIST, :] = xap_ref[...] * live
        xa_ext[CONV_HIST:, :] = xa_ref[...]
        xb_ext[0:POOL_HIST, :] = xbp_ref[...] * live
        xb_ext[POOL_HIST:, :] = xb_ref[...]
        hl_ext[0:F32_SUBLANES, :] = hlp_ref[...] * live
        hl_ext[F32_SUBLANES:, :] = hl_ref[...]
        ea = xa_ext[...]
        eb = xb_ext[...]

        xc = _conv(ea, cw_ref, cb_ref[...])
        lam = lam_ref[...]
        r, ig, a, mult, sp = _lru_gates(xc, wa_ref, ba_ref[...], wx_ref, bx_ref[...], lam)
        hl = hl_ref[...]
        ga = ga_ref[...]
        sga = _sigmoid(ga)
        dya = dya_ref[...]
        dhl = dya * (ga * sga)
        dz_ref[:, D_MODEL:2 * D_MODEL] = (dya * hl * (sga * (1.0 + ga * (1.0 - sga)))).astype(BF16)

        a_ext[0:tb, :] = a
        b = _shift_up(a_ext[...], 1)[0:tb, :]
        a_ext[tb:, :] = jnp.broadcast_to(a[0:1, :], (F32_SUBLANES, D_MODEL))
        d = dhl
        row8 = lax.broadcasted_iota(jnp.int32, (tb, D_MODEL), 0) % F32_SUBLANES
        for s in (1, 2, 4):
            m = row8 < F32_SUBLANES - s
            d = jnp.where(m, d + b * _tile_shift(d, -s), d)
            b = jnp.where(m, b * _tile_shift(b, -s), b)
        b_s[...] = b
        d_s[...] = d

        def step(k, cr):
            sl = pl.ds(pl.multiple_of((groups - 1 - k) * F32_SUBLANES, F32_SUBLANES), F32_SUBLANES)
            gb_ = d_s[sl, :] + b_s[sl, :] * cr
            g_s[sl, :] = gb_
            return jnp.broadcast_to(gb_[0:1, :], (F32_SUBLANES, D_MODEL))

        g_carry[...] = lax.fori_loop(0, groups, step, g_carry[...], unroll=4)
        gsc = g_s[...]
        da = gsc * _shift_down(hl_ext[...], 1)[F32_SUBLANES:, :]
        dmult = gsc * (ig * xc)
        dig = gsc * (mult * xc)
        dxc = gsc * (mult * ig)
        dlog_a = da * a - (a * a) * dmult / mult
        dr = dlog_a * (-LRU_C * sp)
        vec_ref[_bag_row("lru_lambda"), :] += jnp.sum(dlog_a * (-LRU_C * r), axis=0, keepdims=True)
        dpa = dr * (r * (1.0 - r))
        dpx = dig * (ig * (1.0 - ig))
        vec_ref[_bag_row("lru_b_a"), :] += jnp.sum(dpa, axis=0, keepdims=True)
        vec_ref[_bag_row("lru_b_x"), :] += jnp.sum(dpx, axis=0, keepdims=True)
        back = []
        for h in range(LRU_HEADS):
            cols = slice(h * HEAD_DIM, (h + 1) * HEAD_DIM)
            xh = xc[:, cols].astype(BF16)
            dpa_h = dpa[:, cols].astype(BF16)
            dpx_h = dpx[:, cols].astype(BF16)
            mat_ref[mat_rows("lru_w_a", h), :] += _dot_tn(xh, dpa_h)
            mat_ref[mat_rows("lru_w_x", h), :] += _dot_tn(xh, dpx_h)
            back.append(_dot_nt(dpa_h, wa_ref[h]) + _dot_nt(dpx_h, wx_ref[h]))
        dxc = dxc + jnp.concatenate(back, axis=1)
        vec_ref[_bag_row("conv_b"), :] += jnp.sum(dxc, axis=0, keepdims=True)
        for k in range(CONV_WIDTH):
            tap = _shift_down(ea, CONV_WIDTH - 1 - k)[CONV_HIST:, :] if k < CONV_WIDTH - 1 else ea[CONV_HIST:, :]
            vec_ref[_bag_row("conv_w", k), :] += jnp.sum(dxc * tap, axis=0, keepdims=True)
        dxc_ext[0:tb, :] = dxc
        ed = dxc_ext[...]
        dxa = ed * cw_ref[3:4, :]
        dxa = dxa + _shift_up(ed, 1) * cw_ref[2:3, :]
        dxa = dxa + _shift_up(ed, 2) * cw_ref[1:2, :]
        dxa = dxa + _shift_up(ed, 3) * cw_ref[0:1, :]
        dz_ref[:, 0:D_MODEL] = dxa[0:tb, :].astype(BF16)
        dxc_ext[tb:, :] = dxc[0:CONV_HIST, :]

        pos = blk * tb + lax.broadcasted_iota(jnp.int32, (tb, POOL_GROUP_DIM), 0)
        diff = _pool_diff(eb, pos)
        ypre = _pool_mix(diff, pw_ref)
        ps = ps_ref[...]
        gb = gb_ref[...]
        sgb = _sigmoid(gb)
        dyb = dyb_ref[...]
        dyp = dyb * (gb * sgb)
        dz_ref[:, 2 * D_MODEL + POOL_WIDTH:3 * D_MODEL] = (
            dyb * (ypre * ps) * (sgb * (1.0 + gb * (1.0 - sgb)))).astype(BF16)
        vec_ref[_bag_row("pool_scale"), 0:POOL_WIDTH] += jnp.sum(dyp * ypre, axis=0, keepdims=True)
        dypre = dyp * ps
        for g, k in enumerate(POOL_WINDOWS):
            cols = slice(g * POOL_GROUP_DIM, (g + 1) * POOL_GROUP_DIM)
            dyg = dypre[:, cols].astype(BF16)
            mat_ref[mat_rows("pool_w", g), :] += _dot_tn(diff[g].astype(BF16), dyg)
            ddiff = _dot_nt(dyg, pw_ref[g])
            count = jnp.minimum(pos + 1, k).astype(F32)
            dwin = ddiff / count
            dwin_ext[0:tb, cols] = dwin
            s = dwin_ext[:, cols]
            for step_ in range(g + 1):
                s = s + _shift_up(s, 2 ** step_)
            dz_ref[:, 2 * D_MODEL + g * POOL_GROUP_DIM:2 * D_MODEL + (g + 1) * POOL_GROUP_DIM] = (
                s[0:tb, :] - ddiff).astype(BF16)
            dwin_ext[tb:, cols] = dwin[0:POOL_HIST, :]

        dz_ref[:, 3 * D_MODEL:] = dzm_ref[...]

        @pl.when(i == nb - 1)
        def _():
            row = _bag_row("lru_lambda")
            vec_ref[row, :] = vec_ref[row, :] * (-_sigmoid(-lam))

    rev = lambda i: (nb - 1 - i, 0)
    fixed = lambda i: (0, 0)

    def prev(rows, col):
        per = tb // rows
        return lambda i: (jnp.maximum((nb - 1 - i) * per - 1, 0), col)

    in_specs = [pl.BlockSpec((tb, D_MODEL), lambda i: (nb - 1 - i, 0)),
                pl.BlockSpec((CONV_HIST, D_MODEL), prev(CONV_HIST, 0)),
                pl.BlockSpec((tb, D_MODEL), lambda i: (nb - 1 - i, 1)),
                pl.BlockSpec((tb, POOL_WIDTH), lambda i: (nb - 1 - i, 4)),
                pl.BlockSpec((POOL_HIST, POOL_WIDTH), prev(POOL_HIST, 4)),
                pl.BlockSpec((tb, POOL_WIDTH), lambda i: (nb - 1 - i, 5)),
                pl.BlockSpec((tb, D_MODEL), rev),
                pl.BlockSpec((F32_SUBLANES, D_MODEL), prev(F32_SUBLANES, 0)),
                pl.BlockSpec((tb, D_MODEL), rev), pl.BlockSpec((tb, POOL_WIDTH), rev),
                pl.BlockSpec((tb, 2 * D_MODEL), rev)] + _branch_specs(tb, rev, fixed) + [
                    pl.BlockSpec((VEC_BAG_ROWS, D_MODEL), fixed)]
    out_shape = (jax.ShapeDtypeStruct((t, IN_COLS), BF16), jax.ShapeDtypeStruct((VEC_BAG_ROWS, D_MODEL), F32),
                 jax.ShapeDtypeStruct((MAT_BAG_ROWS, HEAD_DIM), F32))
    out_specs = (pl.BlockSpec((tb, IN_COLS), rev), pl.BlockSpec((VEC_BAG_ROWS, D_MODEL), fixed),
                 pl.BlockSpec((MAT_BAG_ROWS, HEAD_DIM), fixed))
    scratch = [pltpu.VMEM((tb + CONV_HIST, D_MODEL), F32), pltpu.VMEM((tb + POOL_HIST, POOL_WIDTH), F32),
               pltpu.VMEM((tb + F32_SUBLANES, D_MODEL), F32), pltpu.VMEM((tb + F32_SUBLANES, D_MODEL), F32),
               pltpu.VMEM((tb + CONV_HIST, D_MODEL), F32), pltpu.VMEM((tb + POOL_HIST, POOL_WIDTH), F32),
               pltpu.VMEM((F32_SUBLANES, D_MODEL), F32),
               pltpu.VMEM((tb, D_MODEL), F32), pltpu.VMEM((tb, D_MODEL), F32), pltpu.VMEM((tb, D_MODEL), F32)]
    return pl.pallas_call(
        body, name="branches_bwd", out_shape=out_shape, grid=(nb,), in_specs=in_specs, out_specs=out_specs,
        scratch_shapes=scratch, input_output_aliases={len(in_specs) - 1: 1},
        compiler_params=pltpu.CompilerParams(dimension_semantics=("arbitrary",),
                                             vmem_limit_bytes=VMEM_LIMIT_BYTES),
    )(z, z, z, z, z, z, hl, hl, dya, dyb, dzm, *weights, vec_bag)


def _merge_head(x2d, ya, yb, z, p2d, tgt, w_pl, w_pp, w_out, w_pg, w_pe, g2, gf, tb):
    t = x2d.shape[0]
    p_dim = p2d.shape[1]

    def body(x_ref, ya_ref, yb_ref, ma_ref, mb_ref, p_ref, t_ref, wpl_ref, wpp_ref, wout_ref, wpg_ref, wpe_ref,
             g2_ref, gf_ref,
             bag_ref, dxr_ref, dya_ref, dyb_ref, dzm_ref,
             mg_ref, do_ref, hn_ref, dgp_ref, dpe_ref, da_ref, dbm_ref, pbf_ref):
        @pl.when(pl.program_id(0) == 0)
        def _():
            bag_ref[...] = jnp.zeros_like(bag_ref)

        a_ = _dot(ya_ref[...], wpl_ref[...])
        bm = _dot(yb_ref[...], wpp_ref[...])
        sa = _sigmoid(ma_ref[...])
        sb = _sigmoid(mb_ref[...])
        mg = (sa * a_ + sb * bm).astype(BF16)
        mg_ref[...] = mg
        x1 = x_ref[...] + _dot(mg, wout_ref[...])
        xn2, r2 = _rms(x1)
        g2 = g2_ref[...]
        hn = (xn2 * g2).astype(BF16)
        hn_ref[...] = hn
        gate = _sigmoid(_dot(hn, wpg_ref[...]))
        pbf = p_ref[...].astype(BF16)
        pbf_ref[...] = pbf
        pe = _dot(pbf, wpe_ref[...])
        x2 = x1 + gate * pe
        xn3, r3 = _rms(x2)
        gf = gf_ref[...]
        err = xn3 * gf - t_ref[...]
        bag_ref[_bag_rows("loss"), 0:128] += 0.5 * jnp.sum(jnp.mean(err * err, axis=-1))

        dy = err * (1.0 / D_MODEL)
        bag_ref[_bag_row("final_g"), :] += jnp.sum(dy * xn3, axis=0, keepdims=True)
        dx2 = _rms_bwd(dy * gf, xn3, r3)
        dpe_ref[...] = (dx2 * gate).astype(BF16)
        dgp = ((dx2 * pe) * (gate * (1.0 - gate))).astype(BF16)
        dgp_ref[...] = dgp
        dhn = _dot_nt(dgp, wpg_ref[...])
        bag_ref[_bag_row("ple_norm_g"), :] += jnp.sum(dhn * xn2, axis=0, keepdims=True)
        dx1 = dx2 + _rms_bwd(dhn * g2, xn2, r2)
        dxr_ref[...] = dx1
        do = dx1.astype(BF16)
        do_ref[...] = do
        dmg = _dot_nt(do, wout_ref[...])
        da = (dmg * sa).astype(BF16)
        dbm = (dmg * sb).astype(BF16)
        da_ref[...] = da
        dbm_ref[...] = dbm
        dzm_ref[:, 0:D_MODEL] = (dmg * a_ * (sa * (1.0 - sa))).astype(BF16)
        dzm_ref[:, D_MODEL:] = (dmg * bm * (sb * (1.0 - sb))).astype(BF16)
        dya_ref[...] = _dot_nt(da, wpl_ref[...])
        dyb_ref[...] = _dot_nt(dbm, wpp_ref[...])

    row = lambda i: (i, 0)
    fixed = lambda i: (0, 0)

    def resident(shape):
        return pl.BlockSpec(shape, fixed, pipeline_mode=pl.Buffered(1))

    tok = lambda width: pl.BlockSpec((tb, width), row)
    in_specs = [tok(D_MODEL), tok(D_MODEL), tok(POOL_WIDTH),
                pl.BlockSpec((tb, D_MODEL), lambda i: (i, 3)), pl.BlockSpec((tb, D_MODEL), lambda i: (i, 4)),
                tok(p_dim), tok(D_MODEL),
                resident((D_MODEL, D_MODEL)), resident((POOL_WIDTH, D_MODEL)), resident((D_MODEL, D_MODEL)),
                resident((D_MODEL, D_MODEL)), resident((p_dim, D_MODEL)),
                pl.BlockSpec((1, D_MODEL), fixed), pl.BlockSpec((1, D_MODEL), fixed)]
    bf = lambda width: jax.ShapeDtypeStruct((t, width), BF16)
    f32 = lambda width: jax.ShapeDtypeStruct((t, width), F32)
    out_shape = (jax.ShapeDtypeStruct((VEC_BAG_ROWS, D_MODEL), F32),
                 f32(D_MODEL), f32(D_MODEL), f32(POOL_WIDTH), bf(2 * D_MODEL),
                 bf(D_MODEL), bf(D_MODEL), bf(D_MODEL), bf(D_MODEL), bf(D_MODEL), bf(D_MODEL), bf(D_MODEL), bf(p_dim))
    out_specs = (pl.BlockSpec((VEC_BAG_ROWS, D_MODEL), fixed),
                 tok(D_MODEL), tok(D_MODEL), tok(POOL_WIDTH), tok(2 * D_MODEL),
                 tok(D_MODEL), tok(D_MODEL), tok(D_MODEL), tok(D_MODEL), tok(D_MODEL), tok(D_MODEL), tok(D_MODEL),
                 tok(p_dim))
    return pl.pallas_call(
        body, name="merge_head", out_shape=out_shape, grid=(t // tb,), in_specs=in_specs, out_specs=out_specs,
        compiler_params=pltpu.CompilerParams(dimension_semantics=("arbitrary",),
                                             vmem_limit_bytes=VMEM_LIMIT_BYTES),
    )(x2d, ya, yb, z, z, p2d, tgt, w_pl, w_pp, w_out, w_pg, w_pe, g2, gf)


def kernel(x, p, norm_g, w_in, conv_w, conv_b, lru_w_a, lru_b_a, lru_w_x, lru_b_x, lru_lambda, pool_w, pool_scale, w_proj_lru, w_proj_pool, w_out, ple_norm_g, w_ple_gate, w_ple_proj, final_g, loss_target, m_norm_g, m_w_in, m_conv_w, m_conv_b, m_lru_w_a, m_lru_b_a, m_lru_w_x, m_lru_b_x, m_lru_lambda, m_pool_w, m_pool_scale, m_w_proj_lru, m_w_proj_pool, m_w_out, m_ple_norm_g, m_w_ple_gate, m_w_ple_proj, m_final_g, v_norm_g, v_w_in, v_conv_w, v_conv_b, v_lru_w_a, v_lru_b_a, v_lru_w_x, v_lru_b_x, v_lru_lambda, v_pool_w, v_pool_scale, v_w_proj_lru, v_w_proj_pool, v_w_out, v_ple_norm_g, v_w_ple_gate, v_w_ple_proj, v_final_g):
    bsz, seq, _ = x.shape
    t = bsz * seq
    tb_mm = min(512, seq)
    tb_seq = min(256, seq // 2) if seq >= 512 else seq
    x2d = x.reshape(t, D_MODEL)
    p2d = p.reshape(t, p.shape[-1])
    tgt = loss_target.reshape(t, D_MODEL)
    chip = 2 * lax.axis_index("x") + lax.axis_index("y")

    rest = [(w_proj_lru[0], 0), (w_proj_pool[0], 1), (w_out[0], 0), (w_ple_gate[0], 0), (w_ple_proj[0], 1)]
    z, h_bf, w_in_f, conv_w_f = _in_proj_gather(x2d, norm_g, w_in[0].astype(BF16), [(conv_w[0], 1, False)], tb_mm)

    wa_bf = lru_w_a[0].astype(BF16)
    wx_bf = lru_w_x[0].astype(BF16)
    pw_bf = pool_w[0].astype(BF16)
    branch_w = (conv_w_f, conv_b, wa_bf, lru_b_a.reshape(1, D_MODEL), wx_bf, lru_b_x.reshape(1, D_MODEL),
                lru_lambda, pw_bf, pool_scale)

    ya, yb, hl, w_pl_f, w_pp_f, w_out_f, w_pg_f, w_pe_f = _branches_fwd(
        z, branch_w, seq, tb_seq, [(w.astype(BF16), axis, True) for w, axis in rest])
    (vec_bag, dx_res, dya, dyb, dzm, mg_bf, do_bf, hn_bf, dgp_bf, dpe_bf, da_bf, dbm_bf, p_bf) = _merge_head(
        x2d, ya, yb, z, p2d, tgt, w_pl_f, w_pp_f, w_out_f, w_pg_f, w_pe_f, ple_norm_g, final_g.reshape(1, D_MODEL),
        tb_seq)
    dz, vec_bag, mat_bag = _branches_bwd(z, hl, dya, dyb, dzm, branch_w, vec_bag, seq, tb_seq)

    tb_dw = min(1024, seq)
    def proj_grad(lhs, rhs, name, cols):
        g32, g16 = _weight_grad(lhs, rhs, 1, tb_dw, name)
        if cols:
            return g32[0], True, g16[0]
        rows = g32.shape[1] // 8
        return g32.reshape(8, rows, g32.shape[2]), False, g16.reshape(8, rows, g32.shape[2])

    p_dim = p2d.shape[1]
    proj_parts = [proj_grad(ya, da_bf, "dw_proj_lru", False), proj_grad(yb, dbm_bf, "dw_proj_pool", True),
                  proj_grad(mg_bf, do_bf, "dw_out", False), proj_grad(hn_bf, dgp_bf, "dw_ple_gate", False),
                  proj_grad(p_bf, dpe_bf, "dw_ple_proj", True)]
    nb_dw = t // tb_dw
    g_in, g_in16, r_pl, r_pp, r_out, r_pg, r_pe, vec_mine, mat_mine = _weight_grad(
        h_bf, dz, N_CHIPS, tb_dw, "dw_in",
        reduce=(proj_parts + [(vec_bag.reshape(8, VEC_BAG_ROWS // 8, D_MODEL), False, None),
                              (mat_bag.reshape(8, MAT_BAG_ROWS // 8, HEAD_DIM), False, None)],
                [BF16] * 5 + [F32] * 2, (0, nb_dw // 2, 3 * nb_dw + nb_dw // 2, N_CHIPS * nb_dw - 1)))
    pieces = (8, D_MODEL // 2, IN_COLS // N_CHIPS)
    nb_seq = t // tb_seq
    dx, d_g1, r_in, vec_sum, mat_sum = _in_proj_bwd(
        dz, w_in_f, x2d, dx_res, norm_g, tb_seq,
        reduce=([(g_in.reshape(pieces), False, g_in16.reshape(pieces))], BF16, (0, nb_seq // 8, nb_seq - 1, nb_seq - 1)),
        shards=[(vec_mine.reshape(VEC_BAG_ROWS // N_CHIPS, D_MODEL), 0, True),
                (mat_mine.reshape(MAT_BAG_ROWS // N_CHIPS, HEAD_DIM), 0, True)])
    g_g1 = _all_reduce_tile(d_g1, "allreduce_norm_g")

    def big_update(w, g2d, m, v, rows, name):
        d, nm, nv = _adamw(w[0], g2d, m[0], v[0], rows, name)
        return g2d[None], d[None], nm[None], nv[None]

    u_in = big_update(w_in, r_in.reshape(D_MODEL, IN_COLS // N_CHIPS), m_w_in, v_w_in, 256, "adamw_w_in")
    u_pl = big_update(w_proj_lru, r_pl.reshape(D_MODEL // N_CHIPS, D_MODEL), m_w_proj_lru, v_w_proj_lru, 256, "adamw_w_proj_lru")
    u_pp = big_update(w_proj_pool, r_pp.reshape(POOL_WIDTH, D_MODEL // N_CHIPS), m_w_proj_pool, v_w_proj_pool, 512, "adamw_w_proj_pool")
    u_out = big_update(w_out, r_out.reshape(D_MODEL // N_CHIPS, D_MODEL), m_w_out, v_w_out, 256, "adamw_w_out")
    u_pg = big_update(w_ple_gate, r_pg.reshape(D_MODEL // N_CHIPS, D_MODEL), m_w_ple_gate, v_w_ple_gate, 256, "adamw_w_ple_gate")
    u_pe = big_update(w_ple_proj, r_pe.reshape(p_dim, D_MODEL // N_CHIPS), m_w_ple_proj, v_w_ple_proj, 256, "adamw_w_ple_proj")

    small = [("norm_g", norm_g, m_norm_g, v_norm_g), ("conv_b", conv_b, m_conv_b, v_conv_b),
             ("lru_w_a", lru_w_a, m_lru_w_a, v_lru_w_a), ("lru_b_a", lru_b_a, m_lru_b_a, v_lru_b_a),
             ("lru_w_x", lru_w_x, m_lru_w_x, v_lru_w_x), ("lru_b_x", lru_b_x, m_lru_b_x, v_lru_b_x),
             ("lru_lambda", lru_lambda, m_lru_lambda, v_lru_lambda), ("pool_w", pool_w, m_pool_w, v_pool_w),
             ("pool_scale", pool_scale, m_pool_scale, v_pool_scale),
             ("ple_norm_g", ple_norm_g, m_ple_norm_g, v_ple_norm_g), ("final_g", final_g, m_final_g, v_final_g)]

    def view(a):
        return a.reshape(-1, a.shape[-1]) if a.ndim != 3 else a[0]

    cw_at = F32_SUBLANES * VEC_BAG_SLOTS.index("conv_w")
    cw_cols = D_MODEL // N_CHIPS
    g_cw = lax.dynamic_slice(vec_sum, (cw_at, chip * cw_cols), (CONV_WIDTH, cw_cols))
    flat = _adamw_replicated(vec_sum, mat_sum, g_g1, [(name,) + tuple(view(a) for a in arrs) for name, *arrs in small],
                             (conv_w[0], m_conv_w[0], v_conv_w[0], g_cw))
    u_small = {name: tuple(flat[4 * k + pick].reshape(arrs[0].shape) for pick in range(4))
               for k, (name, *arrs) in enumerate(small)}
    u_cw = tuple(a[None] for a in (g_cw,) + tuple(flat[4 * len(small):]))

    loss = vec_sum[F32_SUBLANES * VEC_BAG_SLOTS.index("loss"), 0]
    grad_x = dx.reshape(bsz, seq, D_MODEL)

    def ordered(pick):
        s = {name: u[pick] for name, u in u_small.items()}
        return [s["norm_g"], u_in[pick], u_cw[pick], s["conv_b"], s["lru_w_a"], s["lru_b_a"], s["lru_w_x"], s["lru_b_x"],
                s["lru_lambda"], s["pool_w"], s["pool_scale"], u_pl[pick], u_pp[pick], u_out[pick], s["ple_norm_g"],
                u_pg[pick], u_pe[pick], s["final_g"]]

    return (loss, grad_x, *ordered(0), *ordered(1), *ordered(2), *ordered(3))
```

```python
import jax
import jax.numpy as jnp
from jax import lax
from jax.experimental import pallas as pl
from jax.experimental.pallas import tpu as pltpu

F32 = jnp.float32
BF16 = jnp.bfloat16
MESH = pl.DeviceIdType.MESH

D_MODEL = 1024
LRU_HEADS = 8
HEAD_DIM = 128
CONV_WIDTH = 4
LRU_C = 8.0
POOL_WIDTH = 512
POOL_WINDOWS = (2, 4, 8, 16)
POOL_GROUP_DIM = 128
IN_COLS = 5120
N_CHIPS = 4
EPS = 1e-6

ADAM_LR = 0.001
ADAM_B1 = 0.9
ADAM_B2 = 0.999
ADAM_EPS = 1e-08
ADAM_WD = 0.01
ADAM_STEP = 10

F32_SUBLANES = 8
CONV_HIST = 8
POOL_HIST = 16
VMEM_LIMIT_BYTES = 58 * 1024 * 1024
VEC_BAG_SLOTS = ("norm_g", "conv_w", "conv_b", "lru_b_a", "lru_b_x", "lru_lambda", "pool_scale", "ple_norm_g",
                 "final_g", "loss")
VEC_BAG_ROWS = 128
MAT_BAG_AT = {"lru_w_a": 0, "lru_w_x": LRU_HEADS * HEAD_DIM, "pool_w": 2 * LRU_HEADS * HEAD_DIM}
MAT_BAG_ROWS = 2 * LRU_HEADS * HEAD_DIM + len(POOL_WINDOWS) * POOL_GROUP_DIM


def _bag_row(name, k=0):
    at = F32_SUBLANES * VEC_BAG_SLOTS.index(name) + k
    return slice(at, at + 1)


def _bag_rows(name):
    at = F32_SUBLANES * VEC_BAG_SLOTS.index(name)
    return slice(at, at + F32_SUBLANES)


def _dot(a, b):
    return jnp.dot(a, b, preferred_element_type=F32)


def _dot_nt(a, b):
    return lax.dot_general(a, b, (((1,), (1,)), ((), ())), preferred_element_type=F32)


def _dot_tn(a, b):
    return lax.dot_general(a, b, (((0,), (0,)), ((), ())), preferred_element_type=F32)


def _sigmoid(v):
    return jax.nn.sigmoid(v)


def _softplus(v):
    return jnp.maximum(v, 0.0) + jnp.log1p(jnp.exp(-jnp.abs(v)))


def _place():
    return lax.axis_index("x"), lax.axis_index("y"), lax.axis_index("c")


GATHER_SEMS = 6


def _gather_shapes(shards):
    out_shape = []
    for arr, axis, _ in shards:
        r, cols = arr.shape
        out_shape.append(jax.ShapeDtypeStruct((N_CHIPS * r, cols) if axis == 0 else (r, N_CHIPS * cols), arr.dtype))
    n = len(shards)
    sems = [pltpu.SemaphoreType.DMA((n * GATHER_SEMS,)), pltpu.SemaphoreType.DMA((n * GATHER_SEMS,)),
            pltpu.SemaphoreType.DMA((n,))]
    return out_shape, sems


def _gather_steps(shards, ins, outs, send_sems, recv_sems, local_sems):
    n = len(shards)
    x, y, c = _place()
    me, sibling = (x, y, c), (x, y, 1 - c)
    chips = [(x, 1 - y), (1 - x, y), (1 - x, 1 - y)]

    def region(k, cx, cy, hc):
        (r, cols), axis = shards[k][0].shape, shards[k][1]
        j = 2 * cx + cy
        if axis == 0:
            if hc is None:
                return outs[k].at[pl.ds(j * r, r), :]
            return outs[k].at[pl.ds(j * r + hc * (r // 2), r // 2), :]
        if hc is None:
            return outs[k].at[:, pl.ds(j * cols, cols)]
        return outs[k].at[pl.ds(hc * (r // 2), r // 2), pl.ds(j * cols, cols)]

    def remote(k, sem, block, to, src=None):
        dst = region(k, *block)
        return pltpu.make_async_remote_copy(
            src_ref=dst if src is None else src, dst_ref=dst,
            send_sem=send_sems.at[k * GATHER_SEMS + sem], recv_sem=recv_sems.at[k * GATHER_SEMS + sem],
            device_id=to, device_id_type=MESH)

    def first(k, idx):
        r, split = shards[k][0].shape[0], shards[k][2]
        src = ins[k].at[pl.ds(c * (r // 2), r // 2), :] if split else ins[k]
        return remote(k, idx, (x, y, c if split else None), (*chips[idx], c), src=src)

    def relay(k):
        src_chip = (jnp.bitwise_xor(x, 1 - c), jnp.bitwise_xor(y, c))
        dst_chip = (jnp.bitwise_xor(x, c), jnp.bitwise_xor(y, 1 - c))
        return remote(k, 2, (*src_chip, c), (*dst_chip, c))

    def passed(k, idx):
        return remote(k, 3 + idx, (*chips[idx], c), sibling)

    def mine(k):
        return pltpu.make_async_copy(ins[k], region(k, x, y, None), local_sems.at[k])

    def start():
        for k in range(n):
            mine(k).start()
            for idx in range(2 if shards[k][2] else 3):
                first(k, idx).start()

    def relay_on():
        for k in range(n):
            split = shards[k][2]
            for idx in range(2):
                remote(k, idx, (*chips[idx], c if split else None), me).wait_recv()
            if split:
                relay(k).start()
                passed(k, 0).start()
                passed(k, 1).start()

    def finish():
        for k in range(n):
            split = shards[k][2]
            remote(k, 2, (*chips[2], c if split else None), me).wait_recv()
            if split:
                passed(k, 2).start()
        for k in range(n):
            if shards[k][2]:
                for idx in range(3):
                    remote(k, 3 + idx, (*chips[idx], 1 - c), me).wait_recv()
        for k in range(n):
            if shards[k][2]:
                for cp in (first(k, 0), first(k, 1), relay(k), passed(k, 0), passed(k, 1), passed(k, 2)):
                    cp.wait_send()
            else:
                for idx in range(3):
                    first(k, idx).wait_send()
            mine(k).wait()

    return start, relay_on, finish


def _gather_shards(shards, name):
    n = len(shards)

    def body(*refs):
        for step in _gather_steps(shards, refs[:n], refs[n:2 * n], *refs[2 * n:]):
            step()

    out_shape, sems = _gather_shapes(shards)
    any_spec = pl.BlockSpec(memory_space=pl.ANY)
    return pl.pallas_call(
        body, name=name, out_shape=tuple(out_shape),
        in_specs=[any_spec] * n, out_specs=tuple([any_spec] * n), scratch_shapes=sems,
    )(*[s[0] for s in shards])


RS_ADD_ROWS = (64, 56, 32, 16, 8)


def _all_reduce_tile(v, name):
    n_dev = 2 * N_CHIPS
    flips = [(dx, dy, dc) for dx in (0, 1) for dy in (0, 1) for dc in (0, 1)][1:]

    def body(v_ref, o_ref, slots, send_sems, recv_sems):
        x, y, c = _place()
        mine = 4 * x + 2 * y + c

        def copy(k, to_flip, slot):
            dx, dy, dc = to_flip
            peer = (jnp.bitwise_xor(x, dx), jnp.bitwise_xor(y, dy), jnp.bitwise_xor(c, dc))
            return pltpu.make_async_remote_copy(
                src_ref=v_ref, dst_ref=slots.at[slot], send_sem=send_sems.at[k], recv_sem=recv_sems.at[k],
                device_id=peer, device_id_type=MESH)

        sends = [copy(k, flip, mine) for k, flip in enumerate(flips)]
        for cp in sends:
            cp.start()
        slots[mine] = v_ref[...]
        for k, (dx, dy, dc) in enumerate(flips):
            copy(k, (dx, dy, dc), jnp.bitwise_xor(mine, 4 * dx + 2 * dy + dc)).wait_recv()
        total = slots[0]
        for d in range(1, n_dev):
            total = total + slots[d]
        o_ref[...] = total
        for cp in sends:
            cp.wait_send()

    return pl.pallas_call(
        body, name=name, out_shape=jax.ShapeDtypeStruct(v.shape, F32),
        in_specs=[pl.BlockSpec(memory_space=pltpu.VMEM)], out_specs=pl.BlockSpec(memory_space=pltpu.VMEM),
        scratch_shapes=[pltpu.VMEM((n_dev,) + v.shape, F32), pltpu.SemaphoreType.DMA((n_dev - 1,)),
                        pltpu.SemaphoreType.DMA((n_dev - 1,))],
    )(v)


RS_SEMS = 8
RS_LOCAL_SEMS = 5


def _rs_piece_shape(part):
    arr, cols = part[0], part[1]
    return (arr.shape[0] // 2, arr.shape[1] // N_CHIPS) if cols else tuple(arr.shape[1:])


def _rs_operands(parts):
    return [p[0] for p in parts] + [p[0] if p[2] is None else p[2] for p in parts]


def _rs_wires(parts, wire):
    return list(wire) if isinstance(wire, (list, tuple)) else [wire] * len(parts)


def _rs_shapes(parts, wire):
    n = len(parts)
    shapes = [_rs_piece_shape(p) for p in parts]
    out_shape = [jax.ShapeDtypeStruct((2,) + s, F32) for s in shapes]
    scratch = []
    for lead, kind in ((N_CHIPS, "f32"), (N_CHIPS, "narrow"), (N_CHIPS, "wire"), (None, "f32"), (N_CHIPS, "wire")):
        for s, p, w in zip(shapes, parts, _rs_wires(parts, wire)):
            dtype = {"f32": F32, "narrow": F32 if p[2] is None else p[2].dtype, "wire": w}[kind]
            scratch.append(pltpu.VMEM(s if lead is None else (lead,) + s, dtype))
    scratch += [pltpu.SemaphoreType.DMA((n * RS_SEMS,)), pltpu.SemaphoreType.DMA((n * RS_SEMS,)),
                pltpu.SemaphoreType.DMA((n * RS_LOCAL_SEMS,))]
    return out_shape, scratch


def _rs_steps(parts, wire, ins, outs, scratch):
    n = len(parts)
    own, sib, got, fin, snd = (scratch[k * n:(k + 1) * n] for k in range(5))
    send_sems, recv_sems, local_sems = scratch[5 * n:]
    shapes = [_rs_piece_shape(p) for p in parts]
    x, y, c = _place()
    j_me = 2 * x + y
    me, sibling = (x, y, c), (x, y, 1 - c)

    def piece(a, jj, core, narrow=False):
        ref = ins[n + a] if narrow else ins[a]
        if parts[a][1]:
            r, cl = shapes[a]
            return ref.at[pl.ds(core * r, r), pl.ds(jj * cl, cl)]
        return ref.at[2 * jj + core]

    def remote(a, sem, src, dst, to):
        return pltpu.make_async_remote_copy(
            src_ref=src, dst_ref=dst, send_sem=send_sems.at[a * RS_SEMS + sem],
            recv_sem=recv_sems.at[a * RS_SEMS + sem], device_id=to, device_id_type=MESH)

    def rows_loop(a, fn):
        r = shapes[a][0]
        step = max(s for s in RS_ADD_ROWS if r % s == 0)

        def it(i, carry):
            fn(pl.ds(pl.multiple_of(i * step, step), step))
            return carry

        lax.fori_loop(0, r // step, it, 0)

    def load(a, jj):
        return pltpu.make_async_copy(piece(a, jj, c), own[a].at[jj], local_sems.at[a * RS_LOCAL_SEMS + jj])

    def to_sibling(a, jj):
        return remote(a, jj, piece(a, jj, 1 - c, narrow=True), sib[a].at[jj], sibling)

    near = (jnp.bitwise_xor(x, 1 - c), jnp.bitwise_xor(y, c))
    far = (jnp.bitwise_xor(x, c), jnp.bitwise_xor(y, 1 - c))
    diag = (1 - x, 1 - y)
    FROM_NEAR, FROM_FAR, FEED = 0, 1, 2

    def chip_of(chip):
        return 2 * chip[0] + chip[1]

    def feed(a):
        return remote(a, 4, snd[a].at[chip_of(diag)], got[a].at[FEED], (*near, c))

    def to_near(a):
        return remote(a, 5, snd[a].at[chip_of(near)], got[a].at[FROM_NEAR], (*near, c))

    def to_far(a):
        return remote(a, 6, snd[a].at[chip_of(far)], got[a].at[FROM_FAR], (*far, c))

    def store(a):
        return pltpu.make_async_copy(fin[a], outs[a].at[c], local_sems.at[a * RS_LOCAL_SEMS + 4])

    def result_to_sibling(a):
        return remote(a, 7, fin[a], outs[a].at[c], sibling)

    def exchange():
        for a in range(n):
            for jj in range(N_CHIPS):
                load(a, jj).start()
                to_sibling(a, jj).start()

    def chip_sums():
        for a in range(n):
            for jj in range(N_CHIPS):
                load(a, jj).wait()
                remote(a, jj, sib[a].at[jj], sib[a].at[jj], me).wait_recv()

                def add(sl, a=a, jj=jj):
                    q = own[a][jj, sl, :] + sib[a][jj, sl, :].astype(F32)
                    own[a][jj, sl, :] = q
                    snd[a][jj, sl, :] = q.astype(snd[a].dtype)

                rows_loop(a, add)
        for a in range(n):
            feed(a).start()
            to_near(a).start()

    def relay():
        for a in range(n):
            remote(a, 4, got[a].at[FEED], got[a].at[FEED], me).wait_recv()

            def add(sl, a=a):
                pair = own[a][chip_of(far), sl, :] + got[a][FEED, sl, :].astype(F32)
                snd[a][chip_of(far), sl, :] = pair.astype(snd[a].dtype)

            rows_loop(a, add)
            to_far(a).start()

    def totals():
        for a in range(n):
            remote(a, 5, got[a].at[FROM_NEAR], got[a].at[FROM_NEAR], me).wait_recv()
            remote(a, 6, got[a].at[FROM_FAR], got[a].at[FROM_FAR], me).wait_recv()

            def total(sl, a=a):
                fin[a][sl, :] = (own[a][j_me, sl, :] + got[a][FROM_NEAR, sl, :].astype(F32)) + (
                    got[a][FROM_FAR, sl, :].astype(F32))

            rows_loop(a, total)
            store(a).start()
            result_to_sibling(a).start()

    def finish():
        for a in range(n):
            remote(a, 7, outs[a].at[1 - c], outs[a].at[1 - c], me).wait_recv()
        for a in range(n):
            for jj in range(N_CHIPS):
                to_sibling(a, jj).wait_send()
            for cp in (feed(a), to_near(a), to_far(a), result_to_sibling(a)):
                cp.wait_send()
            store(a).wait()

    return exchange, chip_sums, relay, totals, finish


def _rms(x):
    r = lax.rsqrt(jnp.mean(x * x, axis=-1, keepdims=True) + EPS)
    return x * r, r


def _rms_bwd(dxn, xn, r):
    return r * (dxn - xn * jnp.mean(dxn * xn, axis=-1, keepdims=True))


def _in_proj_gather(x2d, norm_g, w_in_sh, shards, tb):
    t = x2d.shape[0]
    nb = t // tb
    cols = IN_COLS // N_CHIPS
    half = D_MODEL // 2
    n = len(shards)

    def body(x_ref, g_ref, win_ref, *refs):
        ins = refs[:n]
        z_ref, h_ref, wfull_ref = refs[n:n + 3]
        outs = refs[n + 3:2 * n + 3]
        wv, h_buf, send_sems, recv_sems, local_sems, w_send, w_recv, w_local = refs[2 * n + 3:]
        s, i = pl.program_id(0), pl.program_id(1)
        x, y, c = _place()
        me, sibling = (x, y, c), (x, y, 1 - c)
        chips = [(x, 1 - y), (1 - x, y), (1 - x, 1 - y)]

        def w_half(cx, cy, hc):
            return wv.at[2 * cx + cy, pl.ds(hc * half, half), :]

        def w_remote(sem, block, to, src=None):
            dst = w_half(*block)
            return pltpu.make_async_remote_copy(
                src_ref=dst if src is None else src, dst_ref=dst, send_sem=w_send.at[sem],
                recv_sem=w_recv.at[sem], device_id=to, device_id_type=MESH)

        def w_first(idx):
            return w_remote(idx, (x, y, c), (*chips[idx], c), src=win_ref.at[pl.ds(c * half, half), :])

        def w_relay():
            src_chip = (jnp.bitwise_xor(x, 1 - c), jnp.bitwise_xor(y, c))
            dst_chip = (jnp.bitwise_xor(x, c), jnp.bitwise_xor(y, 1 - c))
            return w_remote(2, (*src_chip, c), (*dst_chip, c))

        def w_pass(idx):
            return w_remote(3 + idx, (*chips[idx], c), sibling)

        def w_store(k, cx, cy):
            jj = 2 * cx + cy
            return pltpu.make_async_copy(wv.at[jj], wfull_ref.at[:, pl.ds(jj * cols, cols)], w_local.at[k])

        start_rest, relay_rest, finish_rest = _gather_steps(shards, ins, outs, send_sems, recv_sems, local_sems)
        own = pltpu.make_async_copy(win_ref, wv.at[2 * x + y], w_local.at[4])

        @pl.when((s == 0) & (i == 0))
        def _():
            own.start()
            w_first(0).start()
            w_first(1).start()
            start_rest()
            own.wait()
            w_store(0, x, y).start()

        @pl.when((s == 1) & (i == 0))
        def _():
            w_remote(0, (*chips[0], c), me).wait_recv()
            w_remote(1, (*chips[1], c), me).wait_recv()
            w_relay().start()
            w_pass(0).start()
            w_pass(1).start()
            w_remote(3, (*chips[0], 1 - c), me).wait_recv()
            w_store(1, *chips[0]).start()

        @pl.when((s == 2) & (i == 0))
        def _():
            w_remote(4, (*chips[1], 1 - c), me).wait_recv()
            w_store(2, *chips[1]).start()

        @pl.when((s == 3) & (i == 0))
        def _():
            w_remote(2, (*chips[2], c), me).wait_recv()
            w_pass(2).start()
            w_remote(5, (*chips[2], 1 - c), me).wait_recv()
            w_store(3, *chips[2]).start()

        xn, _ = _rms(x_ref[...])
        h = (xn * g_ref[...]).astype(BF16)
        keep_h = pltpu.make_async_copy(h_buf, h_ref.at[pl.ds(pl.multiple_of(i * tb, tb), tb), :], w_local.at[5])

        @pl.when(s == 0)
        def _():
            h_buf[...] = h
            keep_h.start()

        z_ref[...] = _dot(h, wv[jnp.bitwise_xor(2 * x + y, s)])
        pl.when(s == 0)(keep_h.wait)

        @pl.when((s == N_CHIPS - 1) & (i == nb - 1))
        def _():
            relay_rest()
            finish_rest()
            for cp in (w_first(0), w_first(1), w_relay(), w_pass(0), w_pass(1), w_pass(2)):
                cp.wait_send()
            w_store(0, x, y).wait()
            for idx in range(3):
                w_store(idx + 1, *chips[idx]).wait()

    rest_shape, rest_sems = _gather_shapes(shards)
    out_shape = [jax.ShapeDtypeStruct((t, IN_COLS), F32), jax.ShapeDtypeStruct((t, D_MODEL), BF16),
                 jax.ShapeDtypeStruct((D_MODEL, IN_COLS), BF16)] + rest_shape
    any_spec = pl.BlockSpec(memory_space=pl.ANY)

    def z_map(s, i):
        return (i, jnp.bitwise_xor(2 * lax.axis_index("x") + lax.axis_index("y"), s))

    return pl.pallas_call(
        body, name="in_proj", out_shape=tuple(out_shape),
        grid=(N_CHIPS, nb),
        in_specs=[pl.BlockSpec((tb, D_MODEL), lambda s, i: (i, 0)),
                  pl.BlockSpec((1, D_MODEL), lambda s, i: (0, 0)), any_spec] + [any_spec] * n,
        out_specs=tuple([pl.BlockSpec((tb, cols), z_map), any_spec, any_spec] + [any_spec] * n),
        scratch_shapes=[pltpu.VMEM((N_CHIPS, D_MODEL, cols), BF16), pltpu.VMEM((tb, D_MODEL), BF16)] + rest_sems + [
            pltpu.SemaphoreType.DMA((GATHER_SEMS,)), pltpu.SemaphoreType.DMA((GATHER_SEMS,)),
            pltpu.SemaphoreType.DMA((N_CHIPS + 2,))],
        compiler_params=pltpu.CompilerParams(dimension_semantics=("arbitrary", "arbitrary"),
                                             vmem_limit_bytes=VMEM_LIMIT_BYTES),
    )(x2d, norm_g, w_in_sh, *[sh[0] for sh in shards])


def _in_proj_bwd(dz, w_in, x2d, dx_res, norm_g, tb, reduce, shards):
    t = x2d.shape[0]
    nb = t // tb
    parts, wire, steps = reduce
    n = len(parts)
    k = len(shards)

    def body(dz_ref, w_ref, x_ref, dres_ref, g_ref, *refs):
        at = 2 * n + k
        dx_ref, dg_ref = refs[at:at + 2]
        rs_outs, g_outs = refs[at + 2:at + 2 + n], refs[at + 2 + n:at + 2 + n + k]
        scratch = refs[at + 2 + n + k:]
        rs = _rs_steps(parts, wire, refs[:2 * n], rs_outs, scratch[:len(scratch) - 3])
        for step, when in zip(rs, steps):
            pl.when(pl.program_id(0) == when)(step)
        gather = _gather_steps(shards, refs[2 * n:at], g_outs, *scratch[len(scratch) - 3:])
        for step, when in zip(gather, (0, nb // 2, nb - 1)):
            pl.when(pl.program_id(0) == when)(step)

        @pl.when(pl.program_id(0) == 0)
        def _():
            dg_ref[...] = jnp.zeros_like(dg_ref)

        xn, r = _rms(x_ref[...])
        g = g_ref[...]
        dh = _dot_nt(dz_ref[...], w_ref[...])
        dg_ref[0:1, :] += jnp.sum(dh * xn, axis=0, keepdims=True)
        dx_ref[...] = dres_ref[...] + _rms_bwd(dh * g, xn, r)

    row = lambda i: (i, 0)
    fixed = lambda i: (0, 0)
    rs_shape, rs_scratch = _rs_shapes(parts, wire)
    g_shape, g_sems = _gather_shapes(shards)
    any_spec = pl.BlockSpec(memory_space=pl.ANY)
    return pl.pallas_call(
        body, name="in_proj_bwd",
        out_shape=tuple([jax.ShapeDtypeStruct((t, D_MODEL), F32), jax.ShapeDtypeStruct((F32_SUBLANES, D_MODEL), F32)]
                        + rs_shape + g_shape),
        grid=(nb,),
        in_specs=[pl.BlockSpec((tb, IN_COLS), row),
                  pl.BlockSpec((D_MODEL, IN_COLS), fixed, pipeline_mode=pl.Buffered(1)),
                  pl.BlockSpec((tb, D_MODEL), row), pl.BlockSpec((tb, D_MODEL), row),
                  pl.BlockSpec((1, D_MODEL), fixed)] + [any_spec] * (2 * n + k),
        out_specs=tuple([pl.BlockSpec((tb, D_MODEL), row), pl.BlockSpec((F32_SUBLANES, D_MODEL), fixed)]
                        + [any_spec] * (n + k)),
        scratch_shapes=rs_scratch + g_sems,
        compiler_params=pltpu.CompilerParams(dimension_semantics=("arbitrary",),
                                             vmem_limit_bytes=VMEM_LIMIT_BYTES),
    )(dz, w_in, x2d, dx_res, norm_g, *_rs_operands(parts), *[sh[0] for sh in shards])


def _weight_grad(lhs, rhs, n_chunks, tb, name, reduce=None):
    t, k = lhs.shape
    nc = rhs.shape[1] // n_chunks
    nb = t // tb
    parts, wire, steps = reduce if reduce is not None else ([], F32, ())
    n = len(parts)

    def body(l_ref, r_ref, *refs):
        o_ref, o16_ref = refs[2 * n:2 * n + 2]
        if n:
            at = pl.program_id(0) * nb + pl.program_id(1)
            rs = _rs_steps(parts, wire, refs[:2 * n], refs[2 * n + 2:3 * n + 2], refs[3 * n + 2:])
            for step, when in zip(rs, steps):
                pl.when(at == when)(step)

        @pl.when(pl.program_id(1) == 0)
        def _():
            o_ref[...] = jnp.zeros_like(o_ref)

        o_ref[...] += _dot_tn(l_ref[...], r_ref[...])

        @pl.when(pl.program_id(1) == nb - 1)
        def _():
            o16_ref[...] = o_ref[...].astype(BF16)

    rs_shape, rs_scratch = _rs_shapes(parts, wire) if n else ([], [])
    any_spec = pl.BlockSpec(memory_space=pl.ANY)
    chunk = pl.BlockSpec((None, k, nc), lambda j, i: (j, 0, 0))
    return pl.pallas_call(
        body, name=name,
        out_shape=tuple([jax.ShapeDtypeStruct((n_chunks, k, nc), F32), jax.ShapeDtypeStruct((n_chunks, k, nc), BF16)]
                        + rs_shape),
        grid=(n_chunks, nb),
        in_specs=[pl.BlockSpec((tb, k), lambda j, i: (i, 0)), pl.BlockSpec((tb, nc), lambda j, i: (i, j))]
        + [any_spec] * (2 * n),
        out_specs=tuple([chunk, chunk] + [any_spec] * n),
        scratch_shapes=rs_scratch,
        compiler_params=pltpu.CompilerParams(dimension_semantics=("arbitrary", "arbitrary"),
                                             vmem_limit_bytes=VMEM_LIMIT_BYTES),
    )(lhs, rhs, *_rs_operands(parts))


def _adam_update(w, g, m, v):
    m_ = ADAM_B1 * m + (1.0 - ADAM_B1) * g
    v_ = ADAM_B2 * v + (1.0 - ADAM_B2) * jnp.square(g)
    m_hat = m_ / (1.0 - ADAM_B1 ** ADAM_STEP)
    v_hat = v_ / (1.0 - ADAM_B2 ** ADAM_STEP)
    return -ADAM_LR * (m_hat / (jnp.sqrt(v_hat) + ADAM_EPS) + ADAM_WD * w), m_, v_


def _adamw_replicated(vec_sum, mat_sum, norm_grad, entries, conv):
    n = len(entries)

    def grad_of(name, shape, vec_ref, mat_ref, norm_ref):
        if name == "norm_g":
            return norm_ref[0:1, :]
        if name in MAT_BAG_AT:
            return mat_ref[MAT_BAG_AT[name]:MAT_BAG_AT[name] + shape[0], :]
        if shape[0] == 1:
            return vec_ref[_bag_row(name), 0:shape[1]]
        return jnp.concatenate([vec_ref[_bag_row(name), h * shape[1]:(h + 1) * shape[1]] for h in range(shape[0])],
                               axis=0)

    def body(vec_ref, mat_ref, norm_ref, *refs):
        ins, outs = refs[:3 * n + 4], refs[3 * n + 4:]
        for k in range(n):
            w_ref, m_ref, v_ref = ins[3 * k:3 * k + 3]
            g = grad_of(entries[k][0], w_ref.shape, vec_ref, mat_ref, norm_ref)
            d, m_, v_ = _adam_update(w_ref[...], g, m_ref[...], v_ref[...])
            for ref, val in zip(outs[4 * k:4 * k + 4], (g, d, m_, v_)):
                ref[...] = val
        w_ref, m_ref, v_ref, g_ref = ins[3 * n:]
        for ref, val in zip(outs[4 * n:], _adam_update(w_ref[...], g_ref[...], m_ref[...], v_ref[...])):
            ref[...] = val

    arrays = [a for e in entries for a in e[1:]] + list(conv)
    out_shape = [jax.ShapeDtypeStruct(e[1].shape, F32) for e in entries for _ in range(4)]
    out_shape += [jax.ShapeDtypeStruct(conv[0].shape, F32)] * 3
    return pl.pallas_call(
        body, name="adamw_replicated", out_shape=tuple(out_shape),
        compiler_params=pltpu.CompilerParams(vmem_limit_bytes=VMEM_LIMIT_BYTES),
    )(vec_sum, mat_sum, norm_grad, *arrays)


def _adamw(w, g, m, v, rows, name):
    r, c = w.shape

    def body(w_ref, g_ref, m_ref, v_ref, d_ref, nm_ref, nv_ref):
        d_ref[...], nm_ref[...], nv_ref[...] = _adam_update(w_ref[...], g_ref[...], m_ref[...], v_ref[...])

    spec = pl.BlockSpec((rows, c), lambda i: (i, 0))
    return pl.pallas_call(
        body, name=name, out_shape=tuple(jax.ShapeDtypeStruct((r, c), F32) for _ in range(3)),
        grid=(r // rows,), in_specs=[spec] * 4, out_specs=(spec,) * 3,
        compiler_params=pltpu.CompilerParams(dimension_semantics=("arbitrary",),
                                             vmem_limit_bytes=VMEM_LIMIT_BYTES),
    )(w, g, m, v)


def _shift_down(ext, s):
    return pltpu.roll(ext, s, 0)


def _tile_shift(v, s):
    rows, cols = v.shape
    tiles = v.reshape(rows // F32_SUBLANES, F32_SUBLANES, cols)
    return pltpu.roll(tiles, s % F32_SUBLANES, 1).reshape(rows, cols)


def _shift_up(ext, s):
    return pltpu.roll(ext, ext.shape[0] - s, 0)


def _lru_gates(xc, wa_ref, ba, wx_ref, bx, lam):
    pa, px = [], []
    for h in range(LRU_HEADS):
        xh = xc[:, h * HEAD_DIM:(h + 1) * HEAD_DIM].astype(BF16)
        pa.append(_dot(xh, wa_ref[h]))
        px.append(_dot(xh, wx_ref[h]))
    r = _sigmoid(jnp.concatenate(pa, axis=1) + ba)
    ig = _sigmoid(jnp.concatenate(px, axis=1) + bx)
    sp = _softplus(-lam)
    log_a = (-LRU_C * r) * sp
    a = jnp.exp(log_a)
    mult = jnp.sqrt(jnp.tanh(-log_a) * (1.0 + a * a))
    return r, ig, a, mult, sp


def _conv(ext, w_ref, b):
    y = b + _shift_down(ext, 3) * w_ref[0:1, :]
    y = y + _shift_down(ext, 2) * w_ref[1:2, :]
    y = y + _shift_down(ext, 1) * w_ref[2:3, :]
    y = y + ext * w_ref[3:4, :]
    return y[CONV_HIST:, :]


def _pool_diff(ext, pos):
    out = []
    for g, k in enumerate(POOL_WINDOWS):
        col = ext[:, g * POOL_GROUP_DIM:(g + 1) * POOL_GROUP_DIM]
        s = col
        for step in range(g + 1):
            s = s + _shift_down(s, 2 ** step)
        count = jnp.minimum(pos + 1, k).astype(F32)
        out.append(s[POOL_HIST:, :] / count - col[POOL_HIST:, :])
    return out


def _pool_mix(diff, pw_ref):
    return jnp.concatenate([_dot(diff[g].astype(BF16), pw_ref[g]) for g in range(len(POOL_WINDOWS))], axis=1)


def _branch_specs(tb, row_map, fixed):
    fixed3 = lambda i: (0, 0, 0)
    return [pl.BlockSpec((CONV_WIDTH, D_MODEL), fixed), pl.BlockSpec((1, D_MODEL), fixed),
            pl.BlockSpec((LRU_HEADS, HEAD_DIM, HEAD_DIM), fixed3), pl.BlockSpec((1, D_MODEL), fixed),
            pl.BlockSpec((LRU_HEADS, HEAD_DIM, HEAD_DIM), fixed3), pl.BlockSpec((1, D_MODEL), fixed),
            pl.BlockSpec((1, D_MODEL), fixed),
            pl.BlockSpec((len(POOL_WINDOWS), POOL_GROUP_DIM, POOL_GROUP_DIM), fixed3),
            pl.BlockSpec((1, POOL_WIDTH), fixed)]


def _branches_fwd(z, weights, seq, tb, shards):
    t = z.shape[0]
    nb = t // tb
    nbe = seq // tb
    groups = tb // F32_SUBLANES
    n = len(shards)

    def body(xa_ref, ga_ref, xb_ref, gb_ref, cw_ref, cb_ref, wa_ref, ba_ref, wx_ref, bx_ref, lam_ref,
             pw_ref, ps_ref, *refs):
        g_ins = refs[:n]
        ya_ref, yb_ref, hl_ref = refs[n:n + 3]
        g_outs = refs[n + 3:2 * n + 3]
        xa_ext, xb_ext, carry, a_s, u_s, send_sems, recv_sems, local_sems = refs[2 * n + 3:]
        blk = pl.program_id(0) % nbe
        start_gather, relay_gather, finish_gather = _gather_steps(shards, g_ins, g_outs, send_sems, recv_sems,
                                                                  local_sems)
        pl.when(pl.program_id(0) == 0)(start_gather)
        pl.when(pl.program_id(0) == nb // 2)(relay_gather)

        @pl.when(blk == 0)
        def _():
            xa_ext[0:CONV_HIST, :] = jnp.zeros((CONV_HIST, D_MODEL), F32)
            xb_ext[0:POOL_HIST, :] = jnp.zeros((POOL_HIST, POOL_WIDTH), F32)
            carry[...] = jnp.zeros_like(carry)

        xa_ext[CONV_HIST:, :] = xa_ref[...]
        xb_ext[POOL_HIST:, :] = xb_ref[...]
        ea = xa_ext[...]
        eb = xb_ext[...]
        xa_ext[0:CONV_HIST, :] = ea[tb:, :]
        xb_ext[0:POOL_HIST, :] = eb[tb:, :]

        xc = _conv(ea, cw_ref, cb_ref[...])
        _, ig, a, mult, _ = _lru_gates(xc, wa_ref, ba_ref[...], wx_ref, bx_ref[...], lam_ref[...])
        u = mult * (ig * xc)
        row8 = lax.broadcasted_iota(jnp.int32, (tb, D_MODEL), 0) % F32_SUBLANES
        for s in (1, 2, 4):
            m = row8 >= s
            u = jnp.where(m, a * _tile_shift(u, s) + u, u)
            a = jnp.where(m, a * _tile_shift(a, s), a)
        a_s[...] = a
        u_s[...] = u

        def step(g, cr):
            sl = pl.ds(pl.multiple_of(g * F32_SUBLANES, F32_SUBLANES), F32_SUBLANES)
            hb = a_s[sl, :] * cr + u_s[sl, :]
            hl_ref[sl, :] = hb
            return jnp.broadcast_to(hb[F32_SUBLANES - 1:F32_SUBLANES, :], (F32_SUBLANES, D_MODEL))

        carry[...] = lax.fori_loop(0, groups, step, carry[...], unroll=4)
        ga = ga_ref[...]
        ya_ref[...] = (hl_ref[...] * (ga * _sigmoid(ga))).astype(BF16)

        pos = blk * tb + lax.broadcasted_iota(jnp.int32, (tb, POOL_GROUP_DIM), 0)
        ypre = _pool_mix(_pool_diff(eb, pos), pw_ref)
        gb = gb_ref[...]
        yb_ref[...] = ((ypre * ps_ref[...]) * (gb * _sigmoid(gb))).astype(BF16)
        pl.when(pl.program_id(0) == nb - 1)(finish_gather)

    row = lambda i: (i, 0)
    fixed = lambda i: (0, 0)
    any_spec = pl.BlockSpec(memory_space=pl.ANY)
    in_specs = [pl.BlockSpec((tb, D_MODEL), lambda i: (i, 0)), pl.BlockSpec((tb, D_MODEL), lambda i: (i, 1)),
                pl.BlockSpec((tb, POOL_WIDTH), lambda i: (i, 4)), pl.BlockSpec((tb, POOL_WIDTH), lambda i: (i, 5)),
                ] + _branch_specs(tb, row, fixed) + [any_spec] * n
    g_shape, g_sems = _gather_shapes(shards)
    return pl.pallas_call(
        body, name="branches_fwd",
        out_shape=tuple([jax.ShapeDtypeStruct((t, D_MODEL), BF16), jax.ShapeDtypeStruct((t, POOL_WIDTH), BF16),
                         jax.ShapeDtypeStruct((t, D_MODEL), F32)] + g_shape),
        grid=(nb,), in_specs=in_specs,
        out_specs=tuple([pl.BlockSpec((tb, D_MODEL), row), pl.BlockSpec((tb, POOL_WIDTH), row),
                         pl.BlockSpec((tb, D_MODEL), row)] + [any_spec] * n),
        scratch_shapes=[pltpu.VMEM((tb + CONV_HIST, D_MODEL), F32), pltpu.VMEM((tb + POOL_HIST, POOL_WIDTH), F32),
                        pltpu.VMEM((F32_SUBLANES, D_MODEL), F32),
                        pltpu.VMEM((tb, D_MODEL), F32), pltpu.VMEM((tb, D_MODEL), F32)] + g_sems,
        compiler_params=pltpu.CompilerParams(dimension_semantics=("arbitrary",),
                                             vmem_limit_bytes=VMEM_LIMIT_BYTES),
    )(z, z, z, z, *weights, *[sh[0] for sh in shards])


def _branches_bwd(z, hl, dya, dyb, dzm, weights, vec_bag, seq, tb):
    t = z.shape[0]
    nb = t // tb
    nbe = seq // tb
    groups = tb // F32_SUBLANES

    def body(xa_ref, xap_ref, ga_ref, xb_ref, xbp_ref, gb_ref, hl_ref, hlp_ref, dya_ref, dyb_ref, dzm_ref,
             cw_ref, cb_ref, wa_ref, ba_ref, wx_ref, bx_ref, lam_ref, pw_ref, ps_ref, vec_in_ref,
             dz_ref, vec_ref, mat_ref,
             xa_ext, xb_ext, hl_ext, a_ext, dxc_ext, dwin_ext, g_carry, b_s, d_s, g_s):
        i = pl.program_id(0)
        blk = (nb - 1 - i) % nbe

        def mat_rows(name, k):
            at = MAT_BAG_AT[name] + k * HEAD_DIM
            return slice(at, at + HEAD_DIM)

        @pl.when(i == 0)
        def _():
            vec_ref[...] = vec_in_ref[...]
            mat_ref[...] = jnp.zeros_like(mat_ref)

        @pl.when(blk == nbe - 1)
        def _():
            a_ext[tb:, :] = jnp.zeros((F32_SUBLANES, D_MODEL), F32)
            dxc_ext[tb:, :] = jnp.zeros((CONV_HIST, D_MODEL), F32)
            dwin_ext[tb:, :] = jnp.zeros((POOL_HIST, POOL_WIDTH), F32)
            g_carry[...] = jnp.zeros_like(g_carry)

        live = (blk > 0).astype(F32)
        xa_ext[0:CONV_HIST, :] = xap_ref[...] * live
        xa_ext[CONV_HIST:, :] = xa_ref[...]
        xb_ext[0:POOL_HIST, :] = xbp_ref[...] * live
        xb_ext[POOL_HIST:, :] = xb_ref[...]
        hl_ext[0:F32_SUBLANES, :] = hlp_ref[...] * live
        hl_ext[F32_SUBLANES:, :] = hl_ref[...]
        ea = xa_ext[...]
        eb = xb_ext[...]

        xc = _conv(ea, cw_ref, cb_ref[...])
        lam = lam_ref[...]
        r, ig, a, mult, sp = _lru_gates(xc, wa_ref, ba_ref[...], wx_ref, bx_ref[...], lam)
        hl = hl_ref[...]
        ga = ga_ref[...]
        sga = _sigmoid(ga)
        dya = dya_ref[...]
        dhl = dya * (ga * sga)
        dz_ref[:, D_MODEL:2 * D_MODEL] = (dya * hl * (sga * (1.0 + ga * (1.0 - sga)))).astype(BF16)

        a_ext[0:tb, :] = a
        b = _shift_up(a_ext[...], 1)[0:tb, :]
        a_ext[tb:, :] = jnp.broadcast_to(a[0:1, :], (F32_SUBLANES, D_MODEL))
        d = dhl
        row8 = lax.broadcasted_iota(jnp.int32, (tb, D_MODEL), 0) % F32_SUBLANES
        for s in (1, 2, 4):
            m = row8 < F32_SUBLANES - s
            d = jnp.where(m, d + b * _tile_shift(d, -s), d)
            b = jnp.where(m, b * _tile_shift(b, -s), b)
        b_s[...] = b
        d_s[...] = d

        def step(k, cr):
            sl = pl.ds(pl.multiple_of((groups - 1 - k) * F32_SUBLANES, F32_SUBLANES), F32_SUBLANES)
            gb_ = d_s[sl, :] + b_s[sl, :] * cr
            g_s[sl, :] = gb_
            return jnp.broadcast_to(gb_[0:1, :], (F32_SUBLANES, D_MODEL))

        g_carry[...] = lax.fori_loop(0, groups, step, g_carry[...], unroll=4)
        gsc = g_s[...]
        da = gsc * _shift_down(hl_ext[...], 1)[F32_SUBLANES:, :]
        dmult = gsc * (ig * xc)
        dig = gsc * (mult * xc)
        dxc = gsc * (mult * ig)
        dlog_a = da * a - (a * a) * dmult / mult
        dr = dlog_a * (-LRU_C * sp)
        vec_ref[_bag_row("lru_lambda"), :] += jnp.sum(dlog_a * (-LRU_C * r), axis=0, keepdims=True)
        dpa = dr * (r * (1.0 - r))
        dpx = dig * (ig * (1.0 - ig))
        vec_ref[_bag_row("lru_b_a"), :] += jnp.sum(dpa, axis=0, keepdims=True)
        vec_ref[_bag_row("lru_b_x"), :] += jnp.sum(dpx, axis=0, keepdims=True)
        back = []
        for h in range(LRU_HEADS):
            cols = slice(h * HEAD_DIM, (h + 1) * HEAD_DIM)
            xh = xc[:, cols].astype(BF16)
            dpa_h = dpa[:, cols].astype(BF16)
            dpx_h = dpx[:, cols].astype(BF16)
            mat_ref[mat_rows("lru_w_a", h), :] += _dot_tn(xh, dpa_h)
            mat_ref[mat_rows("lru_w_x", h), :] += _dot_tn(xh, dpx_h)
            back.append(_dot_nt(dpa_h, wa_ref[h]) + _dot_nt(dpx_h, wx_ref[h]))
        dxc = dxc + jnp.concatenate(back, axis=1)
        vec_ref[_bag_row("conv_b"), :] += jnp.sum(dxc, axis=0, keepdims=True)
        for k in range(CONV_WIDTH):
            tap = _shift_down(ea, CONV_WIDTH - 1 - k)[CONV_HIST:, :] if k < CONV_WIDTH - 1 else ea[CONV_HIST:, :]
            vec_ref[_bag_row("conv_w", k), :] += jnp.sum(dxc * tap, axis=0, keepdims=True)
        dxc_ext[0:tb, :] = dxc
        ed = dxc_ext[...]
        dxa = ed * cw_ref[3:4, :]
        dxa = dxa + _shift_up(ed, 1) * cw_ref[2:3, :]
        dxa = dxa + _shift_up(ed, 2) * cw_ref[1:2, :]
        dxa = dxa + _shift_up(ed, 3) * cw_ref[0:1, :]
        dz_ref[:, 0:D_MODEL] = dxa[0:tb, :].astype(BF16)
        dxc_ext[tb:, :] = dxc[0:CONV_HIST, :]

        pos = blk * tb + lax.broadcasted_iota(jnp.int32, (tb, POOL_GROUP_DIM), 0)
        diff = _pool_diff(eb, pos)
        ypre = _pool_mix(diff, pw_ref)
        ps = ps_ref[...]
        gb = gb_ref[...]
        sgb = _sigmoid(gb)
        dyb = dyb_ref[...]
        dyp = dyb * (gb * sgb)
        dz_ref[:, 2 * D_MODEL + POOL_WIDTH:3 * D_MODEL] = (
            dyb * (ypre * ps) * (sgb * (1.0 + gb * (1.0 - sgb)))).astype(BF16)
        vec_ref[_bag_row("pool_scale"), 0:POOL_WIDTH] += jnp.sum(dyp * ypre, axis=0, keepdims=True)
        dypre = dyp * ps
        for g, k in enumerate(POOL_WINDOWS):
            cols = slice(g * POOL_GROUP_DIM, (g + 1) * POOL_GROUP_DIM)
            dyg = dypre[:, cols].astype(BF16)
            mat_ref[mat_rows("pool_w", g), :] += _dot_tn(diff[g].astype(BF16), dyg)
            ddiff = _dot_nt(dyg, pw_ref[g])
            count = jnp.minimum(pos + 1, k).astype(F32)
            dwin = ddiff / count
            dwin_ext[0:tb, cols] = dwin
            s = dwin_ext[:, cols]
            for step_ in range(g + 1):
                s = s + _shift_up(s, 2 ** step_)
            dz_ref[:, 2 * D_MODEL + g * POOL_GROUP_DIM:2 * D_MODEL + (g + 1) * POOL_GROUP_DIM] = (
                s[0:tb, :] - ddiff).astype(BF16)
            dwin_ext[tb:, cols] = dwin[0:POOL_HIST, :]

        dz_ref[:, 3 * D_MODEL:] = dzm_ref[...]

        @pl.when(i == nb - 1)
        def _():
            row = _bag_row("lru_lambda")
            vec_ref[row, :] = vec_ref[row, :] * (-_sigmoid(-lam))

    rev = lambda i: (nb - 1 - i, 0)
    fixed = lambda i: (0, 0)

    def prev(rows, col):
        per = tb // rows
        return lambda i: (jnp.maximum((nb - 1 - i) * per - 1, 0), col)

    in_specs = [pl.BlockSpec((tb, D_MODEL), lambda i: (nb - 1 - i, 0)),
                pl.BlockSpec((CONV_HIST, D_MODEL), prev(CONV_HIST, 0)),
                pl.BlockSpec((tb, D_MODEL), lambda i: (nb - 1 - i, 1)),
                pl.BlockSpec((tb, POOL_WIDTH), lambda i: (nb - 1 - i, 4)),
                pl.BlockSpec((POOL_HIST, POOL_WIDTH), prev(POOL_HIST, 4)),
                pl.BlockSpec((tb, POOL_WIDTH), lambda i: (nb - 1 - i, 5)),
                pl.BlockSpec((tb, D_MODEL), rev),
                pl.BlockSpec((F32_SUBLANES, D_MODEL), prev(F32_SUBLANES, 0)),
                pl.BlockSpec((tb, D_MODEL), rev), pl.BlockSpec((tb, POOL_WIDTH), rev),
                pl.BlockSpec((tb, 2 * D_MODEL), rev)] + _branch_specs(tb, rev, fixed) + [
                    pl.BlockSpec((VEC_BAG_ROWS, D_MODEL), fixed)]
    out_shape = (jax.ShapeDtypeStruct((t, IN_COLS), BF16), jax.ShapeDtypeStruct((VEC_BAG_ROWS, D_MODEL), F32),
                 jax.ShapeDtypeStruct((MAT_BAG_ROWS, HEAD_DIM), F32))
    out_specs = (pl.BlockSpec((tb, IN_COLS), rev), pl.BlockSpec((VEC_BAG_ROWS, D_MODEL), fixed),
                 pl.BlockSpec((MAT_BAG_ROWS, HEAD_DIM), fixed))
    scratch = [pltpu.VMEM((tb + CONV_HIST, D_MODEL), F32), pltpu.VMEM((tb + POOL_HIST, POOL_WIDTH), F32),
               pltpu.VMEM((tb + F32_SUBLANES, D_MODEL), F32), pltpu.VMEM((tb + F32_SUBLANES, D_MODEL), F32),
               pltpu.VMEM((tb + CONV_HIST, D_MODEL), F32), pltpu.VMEM((tb + POOL_HIST, POOL_WIDTH), F32),
               pltpu.VMEM((F32_SUBLANES, D_MODEL), F32),
               pltpu.VMEM((tb, D_MODEL), F32), pltpu.VMEM((tb, D_MODEL), F32), pltpu.VMEM((tb, D_MODEL), F32)]
    return pl.pallas_call(
        body, name="branches_bwd", out_shape=out_shape, grid=(nb,), in_specs=in_specs, out_specs=out_specs,
        scratch_shapes=scratch, input_output_aliases={len(in_specs) - 1: 1},
        compiler_params=pltpu.CompilerParams(dimension_semantics=("arbitrary",),
                                             vmem_limit_bytes=VMEM_LIMIT_BYTES),
    )(z, z, z, z, z, z, hl, hl, dya, dyb, dzm, *weights, vec_bag)


def _merge_head(x2d, ya, yb, z, p2d, tgt, w_pl, w_pp, w_out, w_pg, w_pe, g2, gf, tb):
    t = x2d.shape[0]
    p_dim = p2d.shape[1]

    def body(x_ref, ya_ref, yb_ref, ma_ref, mb_ref, p_ref, t_ref, wpl_ref, wpp_ref, wout_ref, wpg_ref, wpe_ref,
             g2_ref, gf_ref,
             bag_ref, dxr_ref, dya_ref, dyb_ref, dzm_ref,
             mg_ref, do_ref, hn_ref, dgp_ref, dpe_ref, da_ref, dbm_ref, pbf_ref):
        @pl.when(pl.program_id(0) == 0)
        def _():
            bag_ref[...] = jnp.zeros_like(bag_ref)

        a_ = _dot(ya_ref[...], wpl_ref[...])
        bm = _dot(yb_ref[...], wpp_ref[...])
        sa = _sigmoid(ma_ref[...])
        sb = _sigmoid(mb_ref[...])
        mg = (sa * a_ + sb * bm).astype(BF16)
        mg_ref[...] = mg
        x1 = x_ref[...] + _dot(mg, wout_ref[...])
        xn2, r2 = _rms(x1)
        g2 = g2_ref[...]
        hn = (xn2 * g2).astype(BF16)
        hn_ref[...] = hn
        gate = _sigmoid(_dot(hn, wpg_ref[...]))
        pbf = p_ref[...].astype(BF16)
        pbf_ref[...] = pbf
        pe = _dot(pbf, wpe_ref[...])
        x2 = x1 + gate * pe
        xn3, r3 = _rms(x2)
        gf = gf_ref[...]
        err = xn3 * gf - t_ref[...]
        bag_ref[_bag_rows("loss"), 0:128] += 0.5 * jnp.sum(jnp.mean(err * err, axis=-1))

        dy = err * (1.0 / D_MODEL)
        bag_ref[_bag_row("final_g"), :] += jnp.sum(dy * xn3, axis=0, keepdims=True)
        dx2 = _rms_bwd(dy * gf, xn3, r3)
        dpe_ref[...] = (dx2 * gate).astype(BF16)
        dgp = ((dx2 * pe) * (gate * (1.0 - gate))).astype(BF16)
        dgp_ref[...] = dgp
        dhn = _dot_nt(dgp, wpg_ref[...])
        bag_ref[_bag_row("ple_norm_g"), :] += jnp.sum(dhn * xn2, axis=0, keepdims=True)
        dx1 = dx2 + _rms_bwd(dhn * g2, xn2, r2)
        dxr_ref[...] = dx1
        do = dx1.astype(BF16)
        do_ref[...] = do
        dmg = _dot_nt(do, wout_ref[...])
        da = (dmg * sa).astype(BF16)
        dbm = (dmg * sb).astype(BF16)
        da_ref[...] = da
        dbm_ref[...] = dbm
        dzm_ref[:, 0:D_MODEL] = (dmg * a_ * (sa * (1.0 - sa))).astype(BF16)
        dzm_ref[:, D_MODEL:] = (dmg * bm * (sb * (1.0 - sb))).astype(BF16)
        dya_ref[...] = _dot_nt(da, wpl_ref[...])
        dyb_ref[...] = _dot_nt(dbm, wpp_ref[...])

    row = lambda i: (i, 0)
    fixed = lambda i: (0, 0)

    def resident(shape):
        return pl.BlockSpec(shape, fixed, pipeline_mode=pl.Buffered(1))

    tok = lambda width: pl.BlockSpec((tb, width), row)
    in_specs = [tok(D_MODEL), tok(D_MODEL), tok(POOL_WIDTH),
                pl.BlockSpec((tb, D_MODEL), lambda i: (i, 3)), pl.BlockSpec((tb, D_MODEL), lambda i: (i, 4)),
                tok(p_dim), tok(D_MODEL),
                resident((D_MODEL, D_MODEL)), resident((POOL_WIDTH, D_MODEL)), resident((D_MODEL, D_MODEL)),
                resident((D_MODEL, D_MODEL)), resident((p_dim, D_MODEL)),
                pl.BlockSpec((1, D_MODEL), fixed), pl.BlockSpec((1, D_MODEL), fixed)]
    bf = lambda width: jax.ShapeDtypeStruct((t, width), BF16)
    f32 = lambda width: jax.ShapeDtypeStruct((t, width), F32)
    out_shape = (jax.ShapeDtypeStruct((VEC_BAG_ROWS, D_MODEL), F32),
                 f32(D_MODEL), f32(D_MODEL), f32(POOL_WIDTH), bf(2 * D_MODEL),
                 bf(D_MODEL), bf(D_MODEL), bf(D_MODEL), bf(D_MODEL), bf(D_MODEL), bf(D_MODEL), bf(D_MODEL), bf(p_dim))
    out_specs = (pl.BlockSpec((VEC_BAG_ROWS, D_MODEL), fixed),
                 tok(D_MODEL), tok(D_MODEL), tok(POOL_WIDTH), tok(2 * D_MODEL),
                 tok(D_MODEL), tok(D_MODEL), tok(D_MODEL), tok(D_MODEL), tok(D_MODEL), tok(D_MODEL), tok(D_MODEL),
                 tok(p_dim))
    return pl.pallas_call(
        body, name="merge_head", out_shape=out_shape, grid=(t // tb,), in_specs=in_specs, out_specs=out_specs,
        compiler_params=pltpu.CompilerParams(dimension_semantics=("arbitrary",),
                                             vmem_limit_bytes=VMEM_LIMIT_BYTES),
    )(x2d, ya, yb, z, z, p2d, tgt, w_pl, w_pp, w_out, w_pg, w_pe, g2, gf)


def kernel(x, p, norm_g, w_in, conv_w, conv_b, lru_w_a, lru_b_a, lru_w_x, lru_b_x, lru_lambda, pool_w, pool_scale, w_proj_lru, w_proj_pool, w_out, ple_norm_g, w_ple_gate, w_ple_proj, final_g, loss_target, m_norm_g, m_w_in, m_conv_w, m_conv_b, m_lru_w_a, m_lru_b_a, m_lru_w_x, m_lru_b_x, m_lru_lambda, m_pool_w, m_pool_scale, m_w_proj_lru, m_w_proj_pool, m_w_out, m_ple_norm_g, m_w_ple_gate, m_w_ple_proj, m_final_g, v_norm_g, v_w_in, v_conv_w, v_conv_b, v_lru_w_a, v_lru_b_a, v_lru_w_x, v_lru_b_x, v_lru_lambda, v_pool_w, v_pool_scale, v_w_proj_lru, v_w_proj_pool, v_w_out, v_ple_norm_g, v_w_ple_gate, v_w_ple_proj, v_final_g):
    bsz, seq, _ = x.shape
    t = bsz * seq
    tb_mm = min(512, seq)
    tb_seq = min(256, seq // 2) if seq >= 512 else seq
    x2d = x.reshape(t, D_MODEL)
    p2d = p.reshape(t, p.shape[-1])
    tgt = loss_target.reshape(t, D_MODEL)
    chip = 2 * lax.axis_index("x") + lax.axis_index("y")

    rest = [(w_proj_lru[0], 0), (w_proj_pool[0], 1), (w_out[0], 0), (w_ple_gate[0], 0), (w_ple_proj[0], 1)]
    z, h_bf, w_in_f, conv_w_f = _in_proj_gather(x2d, norm_g, w_in[0].astype(BF16), [(conv_w[0], 1, False)], tb_mm)

    wa_bf = lru_w_a[0].astype(BF16)
    wx_bf = lru_w_x[0].astype(BF16)
    pw_bf = pool_w[0].astype(BF16)
    branch_w = (conv_w_f, conv_b, wa_bf, lru_b_a.reshape(1, D_MODEL), wx_bf, lru_b_x.reshape(1, D_MODEL),
                lru_lambda, pw_bf, pool_scale)

    ya, yb, hl, w_pl_f, w_pp_f, w_out_f, w_pg_f, w_pe_f = _branches_fwd(
        z, branch_w, seq, tb_seq, [(w.astype(BF16), axis, True) for w, axis in rest])
    (vec_bag, dx_res, dya, dyb, dzm, mg_bf, do_bf, hn_bf, dgp_bf, dpe_bf, da_bf, dbm_bf, p_bf) = _merge_head(
        x2d, ya, yb, z, p2d, tgt, w_pl_f, w_pp_f, w_out_f, w_pg_f, w_pe_f, ple_norm_g, final_g.reshape(1, D_MODEL),
        tb_seq)
    dz, vec_bag, mat_bag = _branches_bwd(z, hl, dya, dyb, dzm, branch_w, vec_bag, seq, tb_seq)

    tb_dw = min(1024, seq)
    def proj_grad(lhs, rhs, name, cols):
        g32, g16 = _weight_grad(lhs, rhs, 1, tb_dw, name)
        if cols:
            return g32[0], True, g16[0]
        rows = g32.shape[1] // 8
        return g32.reshape(8, rows, g32.shape[2]), False, g16.reshape(8, rows, g32.shape[2])

    p_dim = p2d.shape[1]
    proj_parts = [proj_grad(ya, da_bf, "dw_proj_lru", False), proj_grad(yb, dbm_bf, "dw_proj_pool", True),
                  proj_grad(mg_bf, do_bf, "dw_out", False), proj_grad(hn_bf, dgp_bf, "dw_ple_gate", False),
                  proj_grad(p_bf, dpe_bf, "dw_ple_proj", True)]
    nb_dw = t // tb_dw
    g_in, g_in16, r_pl, r_pp, r_out, r_pg, r_pe, vec_mine, mat_mine = _weight_grad(
        h_bf, dz, N_CHIPS, tb_dw, "dw_in",
        reduce=(proj_parts + [(vec_bag.reshape(8, VEC_BAG_ROWS // 8, D_MODEL), False, None),
                              (mat_bag.reshape(8, MAT_BAG_ROWS // 8, HEAD_DIM), False, None)],
                [BF16] * 5 + [F32] * 2,
                (0, nb_dw // 2, nb_dw + nb_dw // 2, 3 * nb_dw + nb_dw // 2, N_CHIPS * nb_dw - 1)))
    pieces = (8, D_MODEL // 2, IN_COLS // N_CHIPS)
    nb_seq = t // tb_seq
    dx, d_g1, r_in, vec_sum, mat_sum = _in_proj_bwd(
        dz, w_in_f, x2d, dx_res, norm_g, tb_seq,
        reduce=([(g_in.reshape(pieces), False, g_in16.reshape(pieces))], BF16,
                (0, nb_seq // 8, nb_seq // 2, nb_seq - 1, nb_seq - 1)),
        shards=[(vec_mine.reshape(VEC_BAG_ROWS // N_CHIPS, D_MODEL), 0, True),
                (mat_mine.reshape(MAT_BAG_ROWS // N_CHIPS, HEAD_DIM), 0, True)])
    g_g1 = _all_reduce_tile(d_g1, "allreduce_norm_g")

    def big_update(w, g2d, m, v, rows, name):
        d, nm, nv = _adamw(w[0], g2d, m[0], v[0], rows, name)
        return g2d[None], d[None], nm[None], nv[None]

    u_in = big_update(w_in, r_in.reshape(D_MODEL, IN_COLS // N_CHIPS), m_w_in, v_w_in, 256, "adamw_w_in")
    u_pl = big_update(w_proj_lru, r_pl.reshape(D_MODEL // N_CHIPS, D_MODEL), m_w_proj_lru, v_w_proj_lru, 256, "adamw_w_proj_lru")
    u_pp = big_update(w_proj_pool, r_pp.reshape(POOL_WIDTH, D_MODEL // N_CHIPS), m_w_proj_pool, v_w_proj_pool, 512, "adamw_w_proj_pool")
    u_out = big_update(w_out, r_out.reshape(D_MODEL // N_CHIPS, D_MODEL), m_w_out, v_w_out, 256, "adamw_w_out")
    u_pg = big_update(w_ple_gate, r_pg.reshape(D_MODEL // N_CHIPS, D_MODEL), m_w_ple_gate, v_w_ple_gate, 256, "adamw_w_ple_gate")
    u_pe = big_update(w_ple_proj, r_pe.reshape(p_dim, D_MODEL // N_CHIPS), m_w_ple_proj, v_w_ple_proj, 256, "adamw_w_ple_proj")

    small = [("norm_g", norm_g, m_norm_g, v_norm_g), ("conv_b", conv_b, m_conv_b, v_conv_b),
             ("lru_w_a", lru_w_a, m_lru_w_a, v_lru_w_a), ("lru_b_a", lru_b_a, m_lru_b_a, v_lru_b_a),
             ("lru_w_x", lru_w_x, m_lru_w_x, v_lru_w_x), ("lru_b_x", lru_b_x, m_lru_b_x, v_lru_b_x),
             ("lru_lambda", lru_lambda, m_lru_lambda, v_lru_lambda), ("pool_w", pool_w, m_pool_w, v_pool_w),
             ("pool_scale", pool_scale, m_pool_scale, v_pool_scale),
             ("ple_norm_g", ple_norm_g, m_ple_norm_g, v_ple_norm_g), ("final_g", final_g, m_final_g, v_final_g)]

    def view(a):
        return a.reshape(-1, a.shape[-1]) if a.ndim != 3 else a[0]

    cw_at = F32_SUBLANES * VEC_BAG_SLOTS.index("conv_w")
    cw_cols = D_MODEL // N_CHIPS
    g_cw = lax.dynamic_slice(vec_sum, (cw_at, chip * cw_cols), (CONV_WIDTH, cw_cols))
    flat = _adamw_replicated(vec_sum, mat_sum, g_g1, [(name,) + tuple(view(a) for a in arrs) for name, *arrs in small],
                             (conv_w[0], m_conv_w[0], v_conv_w[0], g_cw))
    u_small = {name: tuple(flat[4 * k + pick].reshape(arrs[0].shape) for pick in range(4))
               for k, (name, *arrs) in enumerate(small)}
    u_cw = tuple(a[None] for a in (g_cw,) + tuple(flat[4 * len(small):]))

    loss = vec_sum[F32_SUBLANES * VEC_BAG_SLOTS.index("loss"), 0]
    grad_x = dx.reshape(bsz, seq, D_MODEL)

    def ordered(pick):
        s = {name: u[pick] for name, u in u_small.items()}
        return [s["norm_g"], u_in[pick], u_cw[pick], s["conv_b"], s["lru_w_a"], s["lru_b_a"], s["lru_w_x"], s["lru_b_x"],
                s["lru_lambda"], s["pool_w"], s["pool_scale"], u_pl[pick], u_pp[pick], u_out[pick], s["ple_norm_g"],
                u_pg[pick], u_pe[pick], s["final_g"]]

    return (loss, grad_x, *ordered(0), *ordered(1), *ordered(2), *ordered(3))
```

```python
import jax
import jax.numpy as jnp
from jax import lax
from jax.experimental import pallas as pl
from jax.experimental.pallas import tpu as pltpu

F32 = jnp.float32
BF16 = jnp.bfloat16
MESH = pl.DeviceIdType.MESH

D_MODEL = 1024
LRU_HEADS = 8
HEAD_DIM = 128
CONV_WIDTH = 4
LRU_C = 8.0
POOL_WIDTH = 512
POOL_WINDOWS = (2, 4, 8, 16)
POOL_GROUP_DIM = 128
IN_COLS = 5120
N_CHIPS = 4
EPS = 1e-6

ADAM_LR = 0.001
ADAM_B1 = 0.9
ADAM_B2 = 0.999
ADAM_EPS = 1e-08
ADAM_WD = 0.01
ADAM_STEP = 10

F32_SUBLANES = 8
CONV_HIST = 8
POOL_HIST = 16
VMEM_LIMIT_BYTES = 58 * 1024 * 1024
VEC_BAG_SLOTS = ("norm_g", "conv_w", "conv_b", "lru_b_a", "lru_b_x", "lru_lambda", "pool_scale", "ple_norm_g",
                 "final_g", "loss")
VEC_BAG_ROWS = 128
MAT_BAG_AT = {"lru_w_a": 0, "lru_w_x": LRU_HEADS * HEAD_DIM, "pool_w": 2 * LRU_HEADS * HEAD_DIM}
MAT_BAG_ROWS = 2 * LRU_HEADS * HEAD_DIM + len(POOL_WINDOWS) * POOL_GROUP_DIM


def _bag_row(name, k=0):
    at = F32_SUBLANES * VEC_BAG_SLOTS.index(name) + k
    return slice(at, at + 1)


def _bag_rows(name):
    at = F32_SUBLANES * VEC_BAG_SLOTS.index(name)
    return slice(at, at + F32_SUBLANES)


def _dot(a, b):
    return jnp.dot(a, b, preferred_element_type=F32)


def _dot_nt(a, b):
    return lax.dot_general(a, b, (((1,), (1,)), ((), ())), preferred_element_type=F32)


def _dot_tn(a, b):
    return lax.dot_general(a, b, (((0,), (0,)), ((), ())), preferred_element_type=F32)


def _sigmoid(v):
    return jax.nn.sigmoid(v)


def _softplus(v):
    return jnp.maximum(v, 0.0) + jnp.log1p(jnp.exp(-jnp.abs(v)))


def _place():
    return lax.axis_index("x"), lax.axis_index("y"), lax.axis_index("c")


GATHER_SEMS = 6


def _gather_shapes(shards):
    out_shape = []
    for arr, axis, _ in shards:
        r, cols = arr.shape
        out_shape.append(jax.ShapeDtypeStruct((N_CHIPS * r, cols) if axis == 0 else (r, N_CHIPS * cols), arr.dtype))
    n = len(shards)
    sems = [pltpu.SemaphoreType.DMA((n * GATHER_SEMS,)), pltpu.SemaphoreType.DMA((n * GATHER_SEMS,)),
            pltpu.SemaphoreType.DMA((n,))]
    return out_shape, sems


def _gather_steps(shards, ins, outs, send_sems, recv_sems, local_sems):
    n = len(shards)
    x, y, c = _place()
    me, sibling = (x, y, c), (x, y, 1 - c)
    chips = [(x, 1 - y), (1 - x, y), (1 - x, 1 - y)]

    def region(k, cx, cy, hc):
        (r, cols), axis = shards[k][0].shape, shards[k][1]
        j = 2 * cx + cy
        if axis == 0:
            if hc is None:
                return outs[k].at[pl.ds(j * r, r), :]
            return outs[k].at[pl.ds(j * r + hc * (r // 2), r // 2), :]
        if hc is None:
            return outs[k].at[:, pl.ds(j * cols, cols)]
        return outs[k].at[pl.ds(hc * (r // 2), r // 2), pl.ds(j * cols, cols)]

    def remote(k, sem, block, to, src=None):
        dst = region(k, *block)
        return pltpu.make_async_remote_copy(
            src_ref=dst if src is None else src, dst_ref=dst,
            send_sem=send_sems.at[k * GATHER_SEMS + sem], recv_sem=recv_sems.at[k * GATHER_SEMS + sem],
            device_id=to, device_id_type=MESH)

    def first(k, idx):
        r, split = shards[k][0].shape[0], shards[k][2]
        src = ins[k].at[pl.ds(c * (r // 2), r // 2), :] if split else ins[k]
        return remote(k, idx, (x, y, c if split else None), (*chips[idx], c), src=src)

    def relay(k):
        src_chip = (jnp.bitwise_xor(x, 1 - c), jnp.bitwise_xor(y, c))
        dst_chip = (jnp.bitwise_xor(x, c), jnp.bitwise_xor(y, 1 - c))
        return remote(k, 2, (*src_chip, c), (*dst_chip, c))

    def passed(k, idx):
        return remote(k, 3 + idx, (*chips[idx], c), sibling)

    def mine(k):
        return pltpu.make_async_copy(ins[k], region(k, x, y, None), local_sems.at[k])

    def start():
        for k in range(n):
            mine(k).start()
            for idx in range(2 if shards[k][2] else 3):
                first(k, idx).start()

    def relay_on():
        for k in range(n):
            split = shards[k][2]
            for idx in range(2):
                remote(k, idx, (*chips[idx], c if split else None), me).wait_recv()
            if split:
                relay(k).start()
                passed(k, 0).start()
                passed(k, 1).start()

    def finish():
        for k in range(n):
            split = shards[k][2]
            remote(k, 2, (*chips[2], c if split else None), me).wait_recv()
            if split:
                passed(k, 2).start()
        for k in range(n):
            if shards[k][2]:
                for idx in range(3):
                    remote(k, 3 + idx, (*chips[idx], 1 - c), me).wait_recv()
        for k in range(n):
            if shards[k][2]:
                for cp in (first(k, 0), first(k, 1), relay(k), passed(k, 0), passed(k, 1), passed(k, 2)):
                    cp.wait_send()
            else:
                for idx in range(3):
                    first(k, idx).wait_send()
            mine(k).wait()

    return start, relay_on, finish


def _gather_shards(shards, name):
    n = len(shards)

    def body(*refs):
        for step in _gather_steps(shards, refs[:n], refs[n:2 * n], *refs[2 * n:]):
            step()

    out_shape, sems = _gather_shapes(shards)
    any_spec = pl.BlockSpec(memory_space=pl.ANY)
    return pl.pallas_call(
        body, name=name, out_shape=tuple(out_shape),
        in_specs=[any_spec] * n, out_specs=tuple([any_spec] * n), scratch_shapes=sems,
    )(*[s[0] for s in shards])


RS_ADD_ROWS = (64, 56, 32, 16, 8)


def _all_reduce_tile(v, name):
    n_dev = 2 * N_CHIPS
    flips = [(dx, dy, dc) for dx in (0, 1) for dy in (0, 1) for dc in (0, 1)][1:]

    def body(v_ref, o_ref, slots, send_sems, recv_sems):
        x, y, c = _place()
        mine = 4 * x + 2 * y + c

        def copy(k, to_flip, slot):
            dx, dy, dc = to_flip
            peer = (jnp.bitwise_xor(x, dx), jnp.bitwise_xor(y, dy), jnp.bitwise_xor(c, dc))
            return pltpu.make_async_remote_copy(
                src_ref=v_ref, dst_ref=slots.at[slot], send_sem=send_sems.at[k], recv_sem=recv_sems.at[k],
                device_id=peer, device_id_type=MESH)

        sends = [copy(k, flip, mine) for k, flip in enumerate(flips)]
        for cp in sends:
            cp.start()
        slots[mine] = v_ref[...]
        for k, (dx, dy, dc) in enumerate(flips):
            copy(k, (dx, dy, dc), jnp.bitwise_xor(mine, 4 * dx + 2 * dy + dc)).wait_recv()
        total = slots[0]
        for d in range(1, n_dev):
            total = total + slots[d]
        o_ref[...] = total
        for cp in sends:
            cp.wait_send()

    return pl.pallas_call(
        body, name=name, out_shape=jax.ShapeDtypeStruct(v.shape, F32),
        in_specs=[pl.BlockSpec(memory_space=pltpu.VMEM)], out_specs=pl.BlockSpec(memory_space=pltpu.VMEM),
        scratch_shapes=[pltpu.VMEM((n_dev,) + v.shape, F32), pltpu.SemaphoreType.DMA((n_dev - 1,)),
                        pltpu.SemaphoreType.DMA((n_dev - 1,))],
    )(v)


RS_SEMS = 8
RS_LOCAL_SEMS = 5


def _rs_piece_shape(part):
    arr, cols = part[0], part[1]
    return (arr.shape[0] // 2, arr.shape[1] // N_CHIPS) if cols else tuple(arr.shape[1:])


def _rs_operands(parts):
    return [p[0] for p in parts] + [p[0] if p[2] is None else p[2] for p in parts]


def _rs_wires(parts, wire):
    return list(wire) if isinstance(wire, (list, tuple)) else [wire] * len(parts)


def _rs_shapes(parts, wire):
    n = len(parts)
    shapes = [_rs_piece_shape(p) for p in parts]
    out_shape = [jax.ShapeDtypeStruct((2,) + s, F32) for s in shapes]
    scratch = []
    for lead, kind in ((N_CHIPS, "f32"), (N_CHIPS, "narrow"), (N_CHIPS, "wire"), (None, "f32"), (N_CHIPS, "wire")):
        for s, p, w in zip(shapes, parts, _rs_wires(parts, wire)):
            dtype = {"f32": F32, "narrow": F32 if p[2] is None else p[2].dtype, "wire": w}[kind]
            scratch.append(pltpu.VMEM(s if lead is None else (lead,) + s, dtype))
    scratch += [pltpu.SemaphoreType.DMA((n * RS_SEMS,)), pltpu.SemaphoreType.DMA((n * RS_SEMS,)),
                pltpu.SemaphoreType.DMA((n * RS_LOCAL_SEMS,))]
    return out_shape, scratch


def _rs_steps(parts, wire, ins, outs, scratch):
    n = len(parts)
    own, sib, got, fin, snd = (scratch[k * n:(k + 1) * n] for k in range(5))
    send_sems, recv_sems, local_sems = scratch[5 * n:]
    shapes = [_rs_piece_shape(p) for p in parts]
    x, y, c = _place()
    j_me = 2 * x + y
    me, sibling = (x, y, c), (x, y, 1 - c)

    def piece(a, jj, core, narrow=False):
        ref = ins[n + a] if narrow else ins[a]
        if parts[a][1]:
            r, cl = shapes[a]
            return ref.at[pl.ds(core * r, r), pl.ds(jj * cl, cl)]
        return ref.at[2 * jj + core]

    def remote(a, sem, src, dst, to):
        return pltpu.make_async_remote_copy(
            src_ref=src, dst_ref=dst, send_sem=send_sems.at[a * RS_SEMS + sem],
            recv_sem=recv_sems.at[a * RS_SEMS + sem], device_id=to, device_id_type=MESH)

    def rows_loop(a, fn):
        r = shapes[a][0]
        step = max(s for s in RS_ADD_ROWS if r % s == 0)

        def it(i, carry):
            fn(pl.ds(pl.multiple_of(i * step, step), step))
            return carry

        lax.fori_loop(0, r // step, it, 0)

    def load(a, jj):
        return pltpu.make_async_copy(piece(a, jj, c), own[a].at[jj], local_sems.at[a * RS_LOCAL_SEMS + jj])

    def to_sibling(a, jj):
        return remote(a, jj, piece(a, jj, 1 - c, narrow=True), sib[a].at[jj], sibling)

    near = (jnp.bitwise_xor(x, 1 - c), jnp.bitwise_xor(y, c))
    far = (jnp.bitwise_xor(x, c), jnp.bitwise_xor(y, 1 - c))
    diag = (1 - x, 1 - y)
    FROM_NEAR, FROM_FAR, FEED = 0, 1, 2

    def chip_of(chip):
        return 2 * chip[0] + chip[1]

    def feed(a):
        return remote(a, 4, snd[a].at[chip_of(diag)], got[a].at[FEED], (*near, c))

    def to_near(a):
        return remote(a, 5, snd[a].at[chip_of(near)], got[a].at[FROM_NEAR], (*near, c))

    def to_far(a):
        return remote(a, 6, snd[a].at[chip_of(far)], got[a].at[FROM_FAR], (*far, c))

    def store(a):
        return pltpu.make_async_copy(fin[a], outs[a].at[c], local_sems.at[a * RS_LOCAL_SEMS + 4])

    def result_to_sibling(a):
        return remote(a, 7, fin[a], outs[a].at[c], sibling)

    def exchange():
        for a in range(n):
            for jj in range(N_CHIPS):
                load(a, jj).start()
                to_sibling(a, jj).start()

    def chip_sums():
        for a in range(n):
            for jj in range(N_CHIPS):
                load(a, jj).wait()
                remote(a, jj, sib[a].at[jj], sib[a].at[jj], me).wait_recv()

                def add(sl, a=a, jj=jj):
                    q = own[a][jj, sl, :] + sib[a][jj, sl, :].astype(F32)
                    own[a][jj, sl, :] = q
                    snd[a][jj, sl, :] = q.astype(snd[a].dtype)

                rows_loop(a, add)
        for a in range(n):
            feed(a).start()
        for a in range(n):
            to_near(a).start()

    def relay():
        for a in range(n):
            remote(a, 4, got[a].at[FEED], got[a].at[FEED], me).wait_recv()

            def add(sl, a=a):
                pair = own[a][chip_of(far), sl, :] + got[a][FEED, sl, :].astype(F32)
                snd[a][chip_of(far), sl, :] = pair.astype(snd[a].dtype)

            rows_loop(a, add)
            to_far(a).start()

    def totals():
        for a in range(n):
            remote(a, 5, got[a].at[FROM_NEAR], got[a].at[FROM_NEAR], me).wait_recv()
            remote(a, 6, got[a].at[FROM_FAR], got[a].at[FROM_FAR], me).wait_recv()

            def total(sl, a=a):
                fin[a][sl, :] = (own[a][j_me, sl, :] + got[a][FROM_NEAR, sl, :].astype(F32)) + (
                    got[a][FROM_FAR, sl, :].astype(F32))

            rows_loop(a, total)
            store(a).start()
            result_to_sibling(a).start()

    def finish():
        for a in range(n):
            remote(a, 7, outs[a].at[1 - c], outs[a].at[1 - c], me).wait_recv()
        for a in range(n):
            for jj in range(N_CHIPS):
                to_sibling(a, jj).wait_send()
            for cp in (feed(a), to_near(a), to_far(a), result_to_sibling(a)):
                cp.wait_send()
            store(a).wait()

    return exchange, chip_sums, relay, totals, finish


def _rms(x):
    r = lax.rsqrt(jnp.mean(x * x, axis=-1, keepdims=True) + EPS)
    return x * r, r


def _rms_bwd(dxn, xn, r):
    return r * (dxn - xn * jnp.mean(dxn * xn, axis=-1, keepdims=True))


def _in_proj_gather(x2d, norm_g, w_in_sh, shards, tb):
    t = x2d.shape[0]
    nb = t // tb
    cols = IN_COLS // N_CHIPS
    half = D_MODEL // 2
    n = len(shards)

    def body(x_ref, g_ref, win_ref, *refs):
        ins = refs[:n]
        z_ref, h_ref, wfull_ref = refs[n:n + 3]
        outs = refs[n + 3:2 * n + 3]
        wv, h_buf, send_sems, recv_sems, local_sems, w_send, w_recv, w_local = refs[2 * n + 3:]
        s, i = pl.program_id(0), pl.program_id(1)
        x, y, c = _place()
        me, sibling = (x, y, c), (x, y, 1 - c)
        chips = [(x, 1 - y), (1 - x, y), (1 - x, 1 - y)]

        def w_half(cx, cy, hc):
            return wv.at[2 * cx + cy, pl.ds(hc * half, half), :]

        def w_remote(sem, block, to, src=None):
            dst = w_half(*block)
            return pltpu.make_async_remote_copy(
                src_ref=dst if src is None else src, dst_ref=dst, send_sem=w_send.at[sem],
                recv_sem=w_recv.at[sem], device_id=to, device_id_type=MESH)

        def w_first(idx):
            return w_remote(idx, (x, y, c), (*chips[idx], c), src=win_ref.at[pl.ds(c * half, half), :])

        def w_relay():
            src_chip = (jnp.bitwise_xor(x, 1 - c), jnp.bitwise_xor(y, c))
            dst_chip = (jnp.bitwise_xor(x, c), jnp.bitwise_xor(y, 1 - c))
            return w_remote(2, (*src_chip, c), (*dst_chip, c))

        def w_pass(idx):
            return w_remote(3 + idx, (*chips[idx], c), sibling)

        def w_store(k, cx, cy):
            jj = 2 * cx + cy
            return pltpu.make_async_copy(wv.at[jj], wfull_ref.at[:, pl.ds(jj * cols, cols)], w_local.at[k])

        start_rest, relay_rest, finish_rest = _gather_steps(shards, ins, outs, send_sems, recv_sems, local_sems)
        own = pltpu.make_async_copy(win_ref, wv.at[2 * x + y], w_local.at[4])

        @pl.when((s == 0) & (i == 0))
        def _():
            own.start()
            w_first(0).start()
            w_first(1).start()
            start_rest()
            own.wait()
            w_store(0, x, y).start()

        @pl.when((s == 1) & (i == 0))
        def _():
            w_remote(0, (*chips[0], c), me).wait_recv()
            w_remote(1, (*chips[1], c), me).wait_recv()
            w_relay().start()
            w_pass(0).start()
            w_pass(1).start()
            w_remote(3, (*chips[0], 1 - c), me).wait_recv()
            w_store(1, *chips[0]).start()

        @pl.when((s == 2) & (i == 0))
        def _():
            w_remote(4, (*chips[1], 1 - c), me).wait_recv()
            w_store(2, *chips[1]).start()

        @pl.when((s == 3) & (i == 0))
        def _():
            w_remote(2, (*chips[2], c), me).wait_recv()
            w_pass(2).start()
            w_remote(5, (*chips[2], 1 - c), me).wait_recv()
            w_store(3, *chips[2]).start()

        xn, _ = _rms(x_ref[...])
        h = (xn * g_ref[...]).astype(BF16)
        keep_h = pltpu.make_async_copy(h_buf, h_ref.at[pl.ds(pl.multiple_of(i * tb, tb), tb), :], w_local.at[5])

        @pl.when(s == 0)
        def _():
            h_buf[...] = h
            keep_h.start()

        z_ref[...] = _dot(h, wv[jnp.bitwise_xor(2 * x + y, s)])
        pl.when(s == 0)(keep_h.wait)

        @pl.when((s == N_CHIPS - 1) & (i == nb - 1))
        def _():
            relay_rest()
            finish_rest()
            for cp in (w_first(0), w_first(1), w_relay(), w_pass(0), w_pass(1), w_pass(2)):
                cp.wait_send()
            w_store(0, x, y).wait()
            for idx in range(3):
                w_store(idx + 1, *chips[idx]).wait()

    rest_shape, rest_sems = _gather_shapes(shards)
    out_shape = [jax.ShapeDtypeStruct((t, IN_COLS), F32), jax.ShapeDtypeStruct((t, D_MODEL), BF16),
                 jax.ShapeDtypeStruct((D_MODEL, IN_COLS), BF16)] + rest_shape
    any_spec = pl.BlockSpec(memory_space=pl.ANY)

    def z_map(s, i):
        return (i, jnp.bitwise_xor(2 * lax.axis_index("x") + lax.axis_index("y"), s))

    return pl.pallas_call(
        body, name="in_proj", out_shape=tuple(out_shape),
        grid=(N_CHIPS, nb),
        in_specs=[pl.BlockSpec((tb, D_MODEL), lambda s, i: (i, 0)),
                  pl.BlockSpec((1, D_MODEL), lambda s, i: (0, 0)), any_spec] + [any_spec] * n,
        out_specs=tuple([pl.BlockSpec((tb, cols), z_map), any_spec, any_spec] + [any_spec] * n),
        scratch_shapes=[pltpu.VMEM((N_CHIPS, D_MODEL, cols), BF16), pltpu.VMEM((tb, D_MODEL), BF16)] + rest_sems + [
            pltpu.SemaphoreType.DMA((GATHER_SEMS,)), pltpu.SemaphoreType.DMA((GATHER_SEMS,)),
            pltpu.SemaphoreType.DMA((N_CHIPS + 2,))],
        compiler_params=pltpu.CompilerParams(dimension_semantics=("arbitrary", "arbitrary"),
                                             vmem_limit_bytes=VMEM_LIMIT_BYTES),
    )(x2d, norm_g, w_in_sh, *[sh[0] for sh in shards])


def _in_proj_bwd(dz, w_in, x2d, dx_res, norm_g, tb, reduce, shards):
    t = x2d.shape[0]
    nb = t // tb
    parts, wire, steps = reduce
    n = len(parts)
    k = len(shards)

    def body(dz_ref, w_ref, x_ref, dres_ref, g_ref, *refs):
        at = 2 * n + k
        dx_ref, dg_ref = refs[at:at + 2]
        rs_outs, g_outs = refs[at + 2:at + 2 + n], refs[at + 2 + n:at + 2 + n + k]
        scratch = refs[at + 2 + n + k:]
        rs = _rs_steps(parts, wire, refs[:2 * n], rs_outs, scratch[:len(scratch) - 3])
        for step, when in zip(rs, steps):
            pl.when(pl.program_id(0) == when)(step)
        gather = _gather_steps(shards, refs[2 * n:at], g_outs, *scratch[len(scratch) - 3:])
        for step, when in zip(gather, (0, nb // 2, nb - 1)):
            pl.when(pl.program_id(0) == when)(step)

        @pl.when(pl.program_id(0) == 0)
        def _():
            dg_ref[...] = jnp.zeros_like(dg_ref)

        xn, r = _rms(x_ref[...])
        g = g_ref[...]
        dh = _dot_nt(dz_ref[...], w_ref[...])
        dg_ref[0:1, :] += jnp.sum(dh * xn, axis=0, keepdims=True)
        dx_ref[...] = dres_ref[...] + _rms_bwd(dh * g, xn, r)

    row = lambda i: (i, 0)
    fixed = lambda i: (0, 0)
    rs_shape, rs_scratch = _rs_shapes(parts, wire)
    g_shape, g_sems = _gather_shapes(shards)
    any_spec = pl.BlockSpec(memory_space=pl.ANY)
    return pl.pallas_call(
        body, name="in_proj_bwd",
        out_shape=tuple([jax.ShapeDtypeStruct((t, D_MODEL), F32), jax.ShapeDtypeStruct((F32_SUBLANES, D_MODEL), F32)]
                        + rs_shape + g_shape),
        grid=(nb,),
        in_specs=[pl.BlockSpec((tb, IN_COLS), row),
                  pl.BlockSpec((D_MODEL, IN_COLS), fixed, pipeline_mode=pl.Buffered(1)),
                  pl.BlockSpec((tb, D_MODEL), row), pl.BlockSpec((tb, D_MODEL), row),
                  pl.BlockSpec((1, D_MODEL), fixed)] + [any_spec] * (2 * n + k),
        out_specs=tuple([pl.BlockSpec((tb, D_MODEL), row), pl.BlockSpec((F32_SUBLANES, D_MODEL), fixed)]
                        + [any_spec] * (n + k)),
        scratch_shapes=rs_scratch + g_sems,
        compiler_params=pltpu.CompilerParams(dimension_semantics=("arbitrary",),
                                             vmem_limit_bytes=VMEM_LIMIT_BYTES),
    )(dz, w_in, x2d, dx_res, norm_g, *_rs_operands(parts), *[sh[0] for sh in shards])


def _weight_grad(lhs, rhs, n_chunks, tb, name, reduce=None):
    t, k = lhs.shape
    nc = rhs.shape[1] // n_chunks
    nb = t // tb
    parts, wire, steps = reduce if reduce is not None else ([], F32, ())
    n = len(parts)

    def body(l_ref, r_ref, *refs):
        o_ref, o16_ref = refs[2 * n:2 * n + 2]
        if n:
            at = pl.program_id(0) * nb + pl.program_id(1)
            rs = _rs_steps(parts, wire, refs[:2 * n], refs[2 * n + 2:3 * n + 2], refs[3 * n + 2:])
            for step, when in zip(rs, steps):
                pl.when(at == when)(step)

        @pl.when(pl.program_id(1) == 0)
        def _():
            o_ref[...] = jnp.zeros_like(o_ref)

        o_ref[...] += _dot_tn(l_ref[...], r_ref[...])

        @pl.when(pl.program_id(1) == nb - 1)
        def _():
            o16_ref[...] = o_ref[...].astype(BF16)

    rs_shape, rs_scratch = _rs_shapes(parts, wire) if n else ([], [])
    any_spec = pl.BlockSpec(memory_space=pl.ANY)
    chunk = pl.BlockSpec((None, k, nc), lambda j, i: (j, 0, 0))
    return pl.pallas_call(
        body, name=name,
        out_shape=tuple([jax.ShapeDtypeStruct((n_chunks, k, nc), F32), jax.ShapeDtypeStruct((n_chunks, k, nc), BF16)]
                        + rs_shape),
        grid=(n_chunks, nb),
        in_specs=[pl.BlockSpec((tb, k), lambda j, i: (i, 0)), pl.BlockSpec((tb, nc), lambda j, i: (i, j))]
        + [any_spec] * (2 * n),
        out_specs=tuple([chunk, chunk] + [any_spec] * n),
        scratch_shapes=rs_scratch,
        compiler_params=pltpu.CompilerParams(dimension_semantics=("arbitrary", "arbitrary"),
                                             vmem_limit_bytes=VMEM_LIMIT_BYTES),
    )(lhs, rhs, *_rs_operands(parts))


def _adam_update(w, g, m, v):
    m_ = ADAM_B1 * m + (1.0 - ADAM_B1) * g
    v_ = ADAM_B2 * v + (1.0 - ADAM_B2) * jnp.square(g)
    m_hat = m_ / (1.0 - ADAM_B1 ** ADAM_STEP)
    v_hat = v_ / (1.0 - ADAM_B2 ** ADAM_STEP)
    return -ADAM_LR * (m_hat / (jnp.sqrt(v_hat) + ADAM_EPS) + ADAM_WD * w), m_, v_


def _adamw_replicated(vec_sum, mat_sum, norm_grad, entries, conv):
    n = len(entries)

    def grad_of(name, shape, vec_ref, mat_ref, norm_ref):
        if name == "norm_g":
            return norm_ref[0:1, :]
        if name in MAT_BAG_AT:
            return mat_ref[MAT_BAG_AT[name]:MAT_BAG_AT[name] + shape[0], :]
        if shape[0] == 1:
            return vec_ref[_bag_row(name), 0:shape[1]]
        return jnp.concatenate([vec_ref[_bag_row(name), h * shape[1]:(h + 1) * shape[1]] for h in range(shape[0])],
                               axis=0)

    def body(vec_ref, mat_ref, norm_ref, *refs):
        ins, outs = refs[:3 * n + 4], refs[3 * n + 4:]
        for k in range(n):
            w_ref, m_ref, v_ref = ins[3 * k:3 * k + 3]
            g = grad_of(entries[k][0], w_ref.shape, vec_ref, mat_ref, norm_ref)
            d, m_, v_ = _adam_update(w_ref[...], g, m_ref[...], v_ref[...])
            for ref, val in zip(outs[4 * k:4 * k + 4], (g, d, m_, v_)):
                ref[...] = val
        w_ref, m_ref, v_ref, g_ref = ins[3 * n:]
        for ref, val in zip(outs[4 * n:], _adam_update(w_ref[...], g_ref[...], m_ref[...], v_ref[...])):
            ref[...] = val

    arrays = [a for e in entries for a in e[1:]] + list(conv)
    out_shape = [jax.ShapeDtypeStruct(e[1].shape, F32) for e in entries for _ in range(4)]
    out_shape += [jax.ShapeDtypeStruct(conv[0].shape, F32)] * 3
    return pl.pallas_call(
        body, name="adamw_replicated", out_shape=tuple(out_shape),
        compiler_params=pltpu.CompilerParams(vmem_limit_bytes=VMEM_LIMIT_BYTES),
    )(vec_sum, mat_sum, norm_grad, *arrays)


def _adamw(w, g, m, v, rows, name):
    r, c = w.shape

    def body(w_ref, g_ref, m_ref, v_ref, d_ref, nm_ref, nv_ref):
        d_ref[...], nm_ref[...], nv_ref[...] = _adam_update(w_ref[...], g_ref[...], m_ref[...], v_ref[...])

    spec = pl.BlockSpec((rows, c), lambda i: (i, 0))
    return pl.pallas_call(
        body, name=name, out_shape=tuple(jax.ShapeDtypeStruct((r, c), F32) for _ in range(3)),
        grid=(r // rows,), in_specs=[spec] * 4, out_specs=(spec,) * 3,
        compiler_params=pltpu.CompilerParams(dimension_semantics=("arbitrary",),
                                             vmem_limit_bytes=VMEM_LIMIT_BYTES),
    )(w, g, m, v)


def _shift_down(ext, s):
    return pltpu.roll(ext, s, 0)


def _tile_shift(v, s):
    rows, cols = v.shape
    tiles = v.reshape(rows // F32_SUBLANES, F32_SUBLANES, cols)
    return pltpu.roll(tiles, s % F32_SUBLANES, 1).reshape(rows, cols)


def _shift_up(ext, s):
    return pltpu.roll(ext, ext.shape[0] - s, 0)


def _lru_gates(xc, wa_ref, ba, wx_ref, bx, lam):
    pa, px = [], []
    for h in range(LRU_HEADS):
        xh = xc[:, h * HEAD_DIM:(h + 1) * HEAD_DIM].astype(BF16)
        pa.append(_dot(xh, wa_ref[h]))
        px.append(_dot(xh, wx_ref[h]))
    r = _sigmoid(jnp.concatenate(pa, axis=1) + ba)
    ig = _sigmoid(jnp.concatenate(px, axis=1) + bx)
    sp = _softplus(-lam)
    log_a = (-LRU_C * r) * sp
    a = jnp.exp(log_a)
    mult = jnp.sqrt(jnp.tanh(-log_a) * (1.0 + a * a))
    return r, ig, a, mult, sp


def _conv(ext, w_ref, b):
    y = b + _shift_down(ext, 3) * w_ref[0:1, :]
    y = y + _shift_down(ext, 2) * w_ref[1:2, :]
    y = y + _shift_down(ext, 1) * w_ref[2:3, :]
    y = y + ext * w_ref[3:4, :]
    return y[CONV_HIST:, :]


def _pool_diff(ext, pos):
    out = []
    for g, k in enumerate(POOL_WINDOWS):
        col = ext[:, g * POOL_GROUP_DIM:(g + 1) * POOL_GROUP_DIM]
        s = col
        for step in range(g + 1):
            s = s + _shift_down(s, 2 ** step)
        count = jnp.minimum(pos + 1, k).astype(F32)
        out.append(s[POOL_HIST:, :] / count - col[POOL_HIST:, :])
    return out


def _pool_mix(diff, pw_ref):
    return jnp.concatenate([_dot(diff[g].astype(BF16), pw_ref[g]) for g in range(len(POOL_WINDOWS))], axis=1)


def _branch_specs(tb, row_map, fixed):
    fixed3 = lambda i: (0, 0, 0)
    return [pl.BlockSpec((CONV_WIDTH, D_MODEL), fixed), pl.BlockSpec((1, D_MODEL), fixed),
            pl.BlockSpec((LRU_HEADS, HEAD_DIM, HEAD_DIM), fixed3), pl.BlockSpec((1, D_MODEL), fixed),
            pl.BlockSpec((LRU_HEADS, HEAD_DIM, HEAD_DIM), fixed3), pl.BlockSpec((1, D_MODEL), fixed),
            pl.BlockSpec((1, D_MODEL), fixed),
            pl.BlockSpec((len(POOL_WINDOWS), POOL_GROUP_DIM, POOL_GROUP_DIM), fixed3),
            pl.BlockSpec((1, POOL_WIDTH), fixed)]


def _branches_fwd(z, weights, seq, tb, shards):
    t = z.shape[0]
    nb = t // tb
    nbe = seq // tb
    groups = tb // F32_SUBLANES
    n = len(shards)

    def body(xa_ref, ga_ref, xb_ref, gb_ref, cw_ref, cb_ref, wa_ref, ba_ref, wx_ref, bx_ref, lam_ref,
             pw_ref, ps_ref, *refs):
        g_ins = refs[:n]
        ya_ref, yb_ref, hl_ref = refs[n:n + 3]
        g_outs = refs[n + 3:2 * n + 3]
        xa_ext, xb_ext, carry, a_s, u_s, send_sems, recv_sems, local_sems = refs[2 * n + 3:]
        blk = pl.program_id(0) % nbe
        start_gather, relay_gather, finish_gather = _gather_steps(shards, g_ins, g_outs, send_sems, recv_sems,
                                                                  local_sems)
        pl.when(pl.program_id(0) == 0)(start_gather)
        pl.when(pl.program_id(0) == nb // 2)(relay_gather)

        @pl.when(blk == 0)
        def _():
            xa_ext[0:CONV_HIST, :] = jnp.zeros((CONV_HIST, D_MODEL), F32)
            xb_ext[0:POOL_HIST, :] = jnp.zeros((POOL_HIST, POOL_WIDTH), F32)
            carry[...] = jnp.zeros_like(carry)

        xa_ext[CONV_HIST:, :] = xa_ref[...]
        xb_ext[POOL_HIST:, :] = xb_ref[...]
        ea = xa_ext[...]
        eb = xb_ext[...]
        xa_ext[0:CONV_HIST, :] = ea[tb:, :]
        xb_ext[0:POOL_HIST, :] = eb[tb:, :]

        xc = _conv(ea, cw_ref, cb_ref[...])
        _, ig, a, mult, _ = _lru_gates(xc, wa_ref, ba_ref[...], wx_ref, bx_ref[...], lam_ref[...])
        u = mult * (ig * xc)
        row8 = lax.broadcasted_iota(jnp.int32, (tb, D_MODEL), 0) % F32_SUBLANES
        for s in (1, 2, 4):
            m = row8 >= s
            u = jnp.where(m, a * _tile_shift(u, s) + u, u)
            a = jnp.where(m, a * _tile_shift(a, s), a)
        a_s[...] = a
        u_s[...] = u

        def step(g, cr):
            sl = pl.ds(pl.multiple_of(g * F32_SUBLANES, F32_SUBLANES), F32_SUBLANES)
            hb = a_s[sl, :] * cr + u_s[sl, :]
            hl_ref[sl, :] = hb
            return jnp.broadcast_to(hb[F32_SUBLANES - 1:F32_SUBLANES, :], (F32_SUBLANES, D_MODEL))

        carry[...] = lax.fori_loop(0, groups, step, carry[...], unroll=4)
        ga = ga_ref[...]
        ya_ref[...] = (hl_ref[...] * (ga * _sigmoid(ga))).astype(BF16)

        pos = blk * tb + lax.broadcasted_iota(jnp.int32, (tb, POOL_GROUP_DIM), 0)
        ypre = _pool_mix(_pool_diff(eb, pos), pw_ref)
        gb = gb_ref[...]
        yb_ref[...] = ((ypre * ps_ref[...]) * (gb * _sigmoid(gb))).astype(BF16)
        pl.when(pl.program_id(0) == nb - 1)(finish_gather)

    row = lambda i: (i, 0)
    fixed = lambda i: (0, 0)
    any_spec = pl.BlockSpec(memory_space=pl.ANY)
    in_specs = [pl.BlockSpec((tb, D_MODEL), lambda i: (i, 0)), pl.BlockSpec((tb, D_MODEL), lambda i: (i, 1)),
                pl.BlockSpec((tb, POOL_WIDTH), lambda i: (i, 4)), pl.BlockSpec((tb, POOL_WIDTH), lambda i: (i, 5)),
                ] + _branch_specs(tb, row, fixed) + [any_spec] * n
    g_shape, g_sems = _gather_shapes(shards)
    return pl.pallas_call(
        body, name="branches_fwd",
        out_shape=tuple([jax.ShapeDtypeStruct((t, D_MODEL), BF16), jax.ShapeDtypeStruct((t, POOL_WIDTH), BF16),
                         jax.ShapeDtypeStruct((t, D_MODEL), F32)] + g_shape),
        grid=(nb,), in_specs=in_specs,
        out_specs=tuple([pl.BlockSpec((tb, D_MODEL), row), pl.BlockSpec((tb, POOL_WIDTH), row),
                         pl.BlockSpec((tb, D_MODEL), row)] + [any_spec] * n),
        scratch_shapes=[pltpu.VMEM((tb + CONV_HIST, D_MODEL), F32), pltpu.VMEM((tb + POOL_HIST, POOL_WIDTH), F32),
                        pltpu.VMEM((F32_SUBLANES, D_MODEL), F32),
                        pltpu.VMEM((tb, D_MODEL), F32), pltpu.VMEM((tb, D_MODEL), F32)] + g_sems,
        compiler_params=pltpu.CompilerParams(dimension_semantics=("arbitrary",),
                                             vmem_limit_bytes=VMEM_LIMIT_BYTES),
    )(z, z, z, z, *weights, *[sh[0] for sh in shards])


def _branches_bwd(z, hl, dya, dyb, dzm, weights, vec_bag, seq, tb):
    t = z.shape[0]
    nb = t // tb
    nbe = seq // tb
    groups = tb // F32_SUBLANES

    def body(xa_ref, xap_ref, ga_ref, xb_ref, xbp_ref, gb_ref, hl_ref, hlp_ref, dya_ref, dyb_ref, dzm_ref,
             cw_ref, cb_ref, wa_ref, ba_ref, wx_ref, bx_ref, lam_ref, pw_ref, ps_ref, vec_in_ref,
             dz_ref, vec_ref, mat_ref,
             xa_ext, xb_ext, hl_ext, a_ext, dxc_ext, dwin_ext, g_carry, b_s, d_s, g_s):
        i = pl.program_id(0)
        blk = (nb - 1 - i) % nbe

        def mat_rows(name, k):
            at = MAT_BAG_AT[name] + k * HEAD_DIM
            return slice(at, at + HEAD_DIM)

        @pl.when(i == 0)
        def _():
            vec_ref[...] = vec_in_ref[...]
            mat_ref[...] = jnp.zeros_like(mat_ref)

        @pl.when(blk == nbe - 1)
        def _():
            a_ext[tb:, :] = jnp.zeros((F32_SUBLANES, D_MODEL), F32)
            dxc_ext[tb:, :] = jnp.zeros((CONV_HIST, D_MODEL), F32)
            dwin_ext[tb:, :] = jnp.zeros((POOL_HIST, POOL_WIDTH), F32)
            g_carry[...] = jnp.zeros_like(g_carry)

        live = (blk > 0).astype(F32)
        xa_ext[0:CONV_HIST, :] = xap_ref[...] * live
        xa_ext[CONV_HIST:, :] = xa_ref[...]
        xb_ext[0:POOL_HIST, :] = xbp_ref[...] * live
        xb_ext[POOL_HIST:, :] = xb_ref[...]
        hl_ext[0:F32_SUBLANES, :] = hlp_ref[...] * live
        hl_ext[F32_SUBLANES:, :] = hl_ref[...]
        ea = xa_ext[...]
        eb = xb_ext[...]

        xc = _conv(ea, cw_ref, cb_ref[...])
        lam = lam_ref[...]
        r, ig, a, mult, sp = _lru_gates(xc, wa_ref, ba_ref[...], wx_ref, bx_ref[...], lam)
        hl = hl_ref[...]
        ga = ga_ref[...]
        sga = _sigmoid(ga)
        dya = dya_ref[...]
        dhl = dya * (ga * sga)
        dz_ref[:, D_MODEL:2 * D_MODEL] = (dya * hl * (sga * (1.0 + ga * (1.0 - sga)))).astype(BF16)

        a_ext[0:tb, :] = a
        b = _shift_up(a_ext[...], 1)[0:tb, :]
        a_ext[tb:, :] = jnp.broadcast_to(a[0:1, :], (F32_SUBLANES, D_MODEL))
        d = dhl
        row8 = lax.broadcasted_iota(jnp.int32, (tb, D_MODEL), 0) % F32_SUBLANES
        for s in (1, 2, 4):
            m = row8 < F32_SUBLANES - s
            d = jnp.where(m, d + b * _tile_shift(d, -s), d)
            b = jnp.where(m, b * _tile_shift(b, -s), b)
        b_s[...] = b
        d_s[...] = d

        def step(k, cr):
            sl = pl.ds(pl.multiple_of((groups - 1 - k) * F32_SUBLANES, F32_SUBLANES), F32_SUBLANES)
            gb_ = d_s[sl, :] + b_s[sl, :] * cr
            g_s[sl, :] = gb_
            return jnp.broadcast_to(gb_[0:1, :], (F32_SUBLANES, D_MODEL))

        g_carry[...] = lax.fori_loop(0, groups, step, g_carry[...], unroll=4)
        gsc = g_s[...]
        da = gsc * _shift_down(hl_ext[...], 1)[F32_SUBLANES:, :]
        dmult = gsc * (ig * xc)
        dig = gsc * (mult * xc)
        dxc = gsc * (mult * ig)
        dlog_a = da * a - (a * a) * dmult / mult
        dr = dlog_a * (-LRU_C * sp)
        vec_ref[_bag_row("lru_lambda"), :] += jnp.sum(dlog_a * (-LRU_C * r), axis=0, keepdims=True)
        dpa = dr * (r * (1.0 - r))
        dpx = dig * (ig * (1.0 - ig))
        vec_ref[_bag_row("lru_b_a"), :] += jnp.sum(dpa, axis=0, keepdims=True)
        vec_ref[_bag_row("lru_b_x"), :] += jnp.sum(dpx, axis=0, keepdims=True)
        back = []
        for h in range(LRU_HEADS):
            cols = slice(h * HEAD_DIM, (h + 1) * HEAD_DIM)
            xh = xc[:, cols].astype(BF16)
            dpa_h = dpa[:, cols].astype(BF16)
            dpx_h = dpx[:, cols].astype(BF16)
            mat_ref[mat_rows("lru_w_a", h), :] += _dot_tn(xh, dpa_h)
            mat_ref[mat_rows("lru_w_x", h), :] += _dot_tn(xh, dpx_h)
            back.append(_dot_nt(dpa_h, wa_ref[h]) + _dot_nt(dpx_h, wx_ref[h]))
        dxc = dxc + jnp.concatenate(back, axis=1)
        vec_ref[_bag_row("conv_b"), :] += jnp.sum(dxc, axis=0, keepdims=True)
        for k in range(CONV_WIDTH):
            tap = _shift_down(ea, CONV_WIDTH - 1 - k)[CONV_HIST:, :] if k < CONV_WIDTH - 1 else ea[CONV_HIST:, :]
            vec_ref[_bag_row("conv_w", k), :] += jnp.sum(dxc * tap, axis=0, keepdims=True)
        dxc_ext[0:tb, :] = dxc
        ed = dxc_ext[...]
        dxa = ed * cw_ref[3:4, :]
        dxa = dxa + _shift_up(ed, 1) * cw_ref[2:3, :]
        dxa = dxa + _shift_up(ed, 2) * cw_ref[1:2, :]
        dxa = dxa + _shift_up(ed, 3) * cw_ref[0:1, :]
        dz_ref[:, 0:D_MODEL] = dxa[0:tb, :].astype(BF16)
        dxc_ext[tb:, :] = dxc[0:CONV_HIST, :]

        pos = blk * tb + lax.broadcasted_iota(jnp.int32, (tb, POOL_GROUP_DIM), 0)
        diff = _pool_diff(eb, pos)
        ypre = _pool_mix(diff, pw_ref)
        ps = ps_ref[...]
        gb = gb_ref[...]
        sgb = _sigmoid(gb)
        dyb = dyb_ref[...]
        dyp = dyb * (gb * sgb)
        dz_ref[:, 2 * D_MODEL + POOL_WIDTH:3 * D_MODEL] = (
            dyb * (ypre * ps) * (sgb * (1.0 + gb * (1.0 - sgb)))).astype(BF16)
        vec_ref[_bag_row("pool_scale"), 0:POOL_WIDTH] += jnp.sum(dyp * ypre, axis=0, keepdims=True)
        dypre = dyp * ps
        for g, k in enumerate(POOL_WINDOWS):
            cols = slice(g * POOL_GROUP_DIM, (g + 1) * POOL_GROUP_DIM)
            dyg = dypre[:, cols].astype(BF16)
            mat_ref[mat_rows("pool_w", g), :] += _dot_tn(diff[g].astype(BF16), dyg)
            ddiff = _dot_nt(dyg, pw_ref[g])
            count = jnp.minimum(pos + 1, k).astype(F32)
            dwin = ddiff / count
            dwin_ext[0:tb, cols] = dwin
            s = dwin_ext[:, cols]
            for step_ in range(g + 1):
                s = s + _shift_up(s, 2 ** step_)
            dz_ref[:, 2 * D_MODEL + g * POOL_GROUP_DIM:2 * D_MODEL + (g + 1) * POOL_GROUP_DIM] = (
                s[0:tb, :] - ddiff).astype(BF16)
            dwin_ext[tb:, cols] = dwin[0:POOL_HIST, :]

        dz_ref[:, 3 * D_MODEL:] = dzm_ref[...]

        @pl.when(i == nb - 1)
        def _():
            row = _bag_row("lru_lambda")
            vec_ref[row, :] = vec_ref[row, :] * (-_sigmoid(-lam))

    rev = lambda i: (nb - 1 - i, 0)
    fixed = lambda i: (0, 0)

    def prev(rows, col):
        per = tb // rows
        return lambda i: (jnp.maximum((nb - 1 - i) * per - 1, 0), col)

    in_specs = [pl.BlockSpec((tb, D_MODEL), lambda i: (nb - 1 - i, 0)),
                pl.BlockSpec((CONV_HIST, D_MODEL), prev(CONV_HIST, 0)),
                pl.BlockSpec((tb, D_MODEL), lambda i: (nb - 1 - i, 1)),
                pl.BlockSpec((tb, POOL_WIDTH), lambda i: (nb - 1 - i, 4)),
                pl.BlockSpec((POOL_HIST, POOL_WIDTH), prev(POOL_HIST, 4)),
                pl.BlockSpec((tb, POOL_WIDTH), lambda i: (nb - 1 - i, 5)),
                pl.BlockSpec((tb, D_MODEL), rev),
                pl.BlockSpec((F32_SUBLANES, D_MODEL), prev(F32_SUBLANES, 0)),
                pl.BlockSpec((tb, D_MODEL), rev), pl.BlockSpec((tb, POOL_WIDTH), rev),
                pl.BlockSpec((tb, 2 * D_MODEL), rev)] + _branch_specs(tb, rev, fixed) + [
                    pl.BlockSpec((VEC_BAG_ROWS, D_MODEL), fixed)]
    out_shape = (jax.ShapeDtypeStruct((t, IN_COLS), BF16), jax.ShapeDtypeStruct((VEC_BAG_ROWS, D_MODEL), F32),
                 jax.ShapeDtypeStruct((MAT_BAG_ROWS, HEAD_DIM), F32))
    out_specs = (pl.BlockSpec((tb, IN_COLS), rev), pl.BlockSpec((VEC_BAG_ROWS, D_MODEL), fixed),
                 pl.BlockSpec((MAT_BAG_ROWS, HEAD_DIM), fixed))
    scratch = [pltpu.VMEM((tb + CONV_HIST, D_MODEL), F32), pltpu.VMEM((tb + POOL_HIST, POOL_WIDTH), F32),
               pltpu.VMEM((tb + F32_SUBLANES, D_MODEL), F32), pltpu.VMEM((tb + F32_SUBLANES, D_MODEL), F32),
               pltpu.VMEM((tb + CONV_HIST, D_MODEL), F32), pltpu.VMEM((tb + POOL_HIST, POOL_WIDTH), F32),
               pltpu.VMEM((F32_SUBLANES, D_MODEL), F32),
               pltpu.VMEM((tb, D_MODEL), F32), pltpu.VMEM((tb, D_MODEL), F32), pltpu.VMEM((tb, D_MODEL), F32)]
    return pl.pallas_call(
        body, name="branches_bwd", out_shape=out_shape, grid=(nb,), in_specs=in_specs, out_specs=out_specs,
        scratch_shapes=scratch, input_output_aliases={len(in_specs) - 1: 1},
        compiler_params=pltpu.CompilerParams(dimension_semantics=("arbitrary",),
                                             vmem_limit_bytes=VMEM_LIMIT_BYTES),
    )(z, z, z, z, z, z, hl, hl, dya, dyb, dzm, *weights, vec_bag)


def _merge_head(x2d, ya, yb, z, p2d, tgt, w_pl, w_pp, w_out, w_pg, w_pe, g2, gf, tb):
    t = x2d.shape[0]
    p_dim = p2d.shape[1]

    def body(x_ref, ya_ref, yb_ref, ma_ref, mb_ref, p_ref, t_ref, wpl_ref, wpp_ref, wout_ref, wpg_ref, wpe_ref,
             g2_ref, gf_ref,
             bag_ref, dxr_ref, dya_ref, dyb_ref, dzm_ref,
             mg_ref, do_ref, hn_ref, dgp_ref, dpe_ref, da_ref, dbm_ref, pbf_ref):
        @pl.when(pl.program_id(0) == 0)
        def _():
            bag_ref[...] = jnp.zeros_like(bag_ref)

        a_ = _dot(ya_ref[...], wpl_ref[...])
        bm = _dot(yb_ref[...], wpp_ref[...])
        sa = _sigmoid(ma_ref[...])
        sb = _sigmoid(mb_ref[...])
        mg = (sa * a_ + sb * bm).astype(BF16)
        mg_ref[...] = mg
        x1 = x_ref[...] + _dot(mg, wout_ref[...])
        xn2, r2 = _rms(x1)
        g2 = g2_ref[...]
        hn = (xn2 * g2).astype(BF16)
        hn_ref[...] = hn
        gate = _sigmoid(_dot(hn, wpg_ref[...]))
        pbf = p_ref[...].astype(BF16)
        pbf_ref[...] = pbf
        pe = _dot(pbf, wpe_ref[...])
        x2 = x1 + gate * pe
        xn3, r3 = _rms(x2)
        gf = gf_ref[...]
        err = xn3 * gf - t_ref[...]
        bag_ref[_bag_rows("loss"), 0:128] += 0.5 * jnp.sum(jnp.mean(err * err, axis=-1))

        dy = err * (1.0 / D_MODEL)
        bag_ref[_bag_row("final_g"), :] += jnp.sum(dy * xn3, axis=0, keepdims=True)
        dx2 = _rms_bwd(dy * gf, xn3, r3)
        dpe_ref[...] = (dx2 * gate).astype(BF16)
        dgp = ((dx2 * pe) * (gate * (1.0 - gate))).astype(BF16)
        dgp_ref[...] = dgp
        dhn = _dot_nt(dgp, wpg_ref[...])
        bag_ref[_bag_row("ple_norm_g"), :] += jnp.sum(dhn * xn2, axis=0, keepdims=True)
        dx1 = dx2 + _rms_bwd(dhn * g2, xn2, r2)
        dxr_ref[...] = dx1
        do = dx1.astype(BF16)
        do_ref[...] = do
        dmg = _dot_nt(do, wout_ref[...])
        da = (dmg * sa).astype(BF16)
        dbm = (dmg * sb).astype(BF16)
        da_ref[...] = da
        dbm_ref[...] = dbm
        dzm_ref[:, 0:D_MODEL] = (dmg * a_ * (sa * (1.0 - sa))).astype(BF16)
        dzm_ref[:, D_MODEL:] = (dmg * bm * (sb * (1.0 - sb))).astype(BF16)
        dya_ref[...] = _dot_nt(da, wpl_ref[...])
        dyb_ref[...] = _dot_nt(dbm, wpp_ref[...])

    row = lambda i: (i, 0)
    fixed = lambda i: (0, 0)

    def resident(shape):
        return pl.BlockSpec(shape, fixed, pipeline_mode=pl.Buffered(1))

    tok = lambda width: pl.BlockSpec((tb, width), row)
    in_specs = [tok(D_MODEL), tok(D_MODEL), tok(POOL_WIDTH),
                pl.BlockSpec((tb, D_MODEL), lambda i: (i, 3)), pl.BlockSpec((tb, D_MODEL), lambda i: (i, 4)),
                tok(p_dim), tok(D_MODEL),
                resident((D_MODEL, D_MODEL)), resident((POOL_WIDTH, D_MODEL)), resident((D_MODEL, D_MODEL)),
                resident((D_MODEL, D_MODEL)), resident((p_dim, D_MODEL)),
                pl.BlockSpec((1, D_MODEL), fixed), pl.BlockSpec((1, D_MODEL), fixed)]
    bf = lambda width: jax.ShapeDtypeStruct((t, width), BF16)
    f32 = lambda width: jax.ShapeDtypeStruct((t, width), F32)
    out_shape = (jax.ShapeDtypeStruct((VEC_BAG_ROWS, D_MODEL), F32),
                 f32(D_MODEL), f32(D_MODEL), f32(POOL_WIDTH), bf(2 * D_MODEL),
                 bf(D_MODEL), bf(D_MODEL), bf(D_MODEL), bf(D_MODEL), bf(D_MODEL), bf(D_MODEL), bf(D_MODEL), bf(p_dim))
    out_specs = (pl.BlockSpec((VEC_BAG_ROWS, D_MODEL), fixed),
                 tok(D_MODEL), tok(D_MODEL), tok(POOL_WIDTH), tok(2 * D_MODEL),
                 tok(D_MODEL), tok(D_MODEL), tok(D_MODEL), tok(D_MODEL), tok(D_MODEL), tok(D_MODEL), tok(D_MODEL),
                 tok(p_dim))
    return pl.pallas_call(
        body, name="merge_head", out_shape=out_shape, grid=(t // tb,), in_specs=in_specs, out_specs=out_specs,
        compiler_params=pltpu.CompilerParams(dimension_semantics=("arbitrary",),
                                             vmem_limit_bytes=VMEM_LIMIT_BYTES),
    )(x2d, ya, yb, z, z, p2d, tgt, w_pl, w_pp, w_out, w_pg, w_pe, g2, gf)


def kernel(x, p, norm_g, w_in, conv_w, conv_b, lru_w_a, lru_b_a, lru_w_x, lru_b_x, lru_lambda, pool_w, pool_scale, w_proj_lru, w_proj_pool, w_out, ple_norm_g, w_ple_gate, w_ple_proj, final_g, loss_target, m_norm_g, m_w_in, m_conv_w, m_conv_b, m_lru_w_a, m_lru_b_a, m_lru_w_x, m_lru_b_x, m_lru_lambda, m_pool_w, m_pool_scale, m_w_proj_lru, m_w_proj_pool, m_w_out, m_ple_norm_g, m_w_ple_gate, m_w_ple_proj, m_final_g, v_norm_g, v_w_in, v_conv_w, v_conv_b, v_lru_w_a, v_lru_b_a, v_lru_w_x, v_lru_b_x, v_lru_lambda, v_pool_w, v_pool_scale, v_w_proj_lru, v_w_proj_pool, v_w_out, v_ple_norm_g, v_w_ple_gate, v_w_ple_proj, v_final_g):
    bsz, seq, _ = x.shape
    t = bsz * seq
    tb_mm = min(512, seq)
    tb_seq = min(256, seq // 2) if seq >= 512 else seq
    x2d = x.reshape(t, D_MODEL)
    p2d = p.reshape(t, p.shape[-1])
    tgt = loss_target.reshape(t, D_MODEL)
    chip = 2 * lax.axis_index("x") + lax.axis_index("y")

    rest = [(w_proj_lru[0], 0), (w_proj_pool[0], 1), (w_out[0], 0), (w_ple_gate[0], 0), (w_ple_proj[0], 1)]
    z, h_bf, w_in_f, conv_w_f = _in_proj_gather(x2d, norm_g, w_in[0].astype(BF16), [(conv_w[0], 1, False)], tb_mm)

    wa_bf = lru_w_a[0].astype(BF16)
    wx_bf = lru_w_x[0].astype(BF16)
    pw_bf = pool_w[0].astype(BF16)
    branch_w = (conv_w_f, conv_b, wa_bf, lru_b_a.reshape(1, D_MODEL), wx_bf, lru_b_x.reshape(1, D_MODEL),
                lru_lambda, pw_bf, pool_scale)

    ya, yb, hl, w_pl_f, w_pp_f, w_out_f, w_pg_f, w_pe_f = _branches_fwd(
        z, branch_w, seq, tb_seq, [(w.astype(BF16), axis, True) for w, axis in rest])
    (vec_bag, dx_res, dya, dyb, dzm, mg_bf, do_bf, hn_bf, dgp_bf, dpe_bf, da_bf, dbm_bf, p_bf) = _merge_head(
        x2d, ya, yb, z, p2d, tgt, w_pl_f, w_pp_f, w_out_f, w_pg_f, w_pe_f, ple_norm_g, final_g.reshape(1, D_MODEL),
        tb_seq)
    dz, vec_bag, mat_bag = _branches_bwd(z, hl, dya, dyb, dzm, branch_w, vec_bag, seq, tb_seq)

    tb_dw = min(1024, seq)
    def proj_grad(lhs, rhs, name, cols):
        g32, g16 = _weight_grad(lhs, rhs, 1, tb_dw, name)
        if cols:
            return g32[0], True, g16[0]
        rows = g32.shape[1] // 8
        return g32.reshape(8, rows, g32.shape[2]), False, g16.reshape(8, rows, g32.shape[2])

    p_dim = p2d.shape[1]
    proj_parts = [proj_grad(ya, da_bf, "dw_proj_lru", False), proj_grad(yb, dbm_bf, "dw_proj_pool", True),
                  proj_grad(mg_bf, do_bf, "dw_out", False), proj_grad(hn_bf, dgp_bf, "dw_ple_gate", False),
                  proj_grad(p_bf, dpe_bf, "dw_ple_proj", True)]
    nb_dw = t // tb_dw
    g_in, g_in16, r_pl, r_pp, r_out, r_pg, r_pe, vec_mine, mat_mine = _weight_grad(
        h_bf, dz, N_CHIPS, tb_dw, "dw_in",
        reduce=(proj_parts + [(vec_bag.reshape(8, VEC_BAG_ROWS // 8, D_MODEL), False, None),
                              (mat_bag.reshape(8, MAT_BAG_ROWS // 8, HEAD_DIM), False, None)],
                [BF16] * 5 + [F32] * 2,
                (0, nb_dw // 2, 2 * nb_dw - 1, 3 * nb_dw + nb_dw // 2, N_CHIPS * nb_dw - 1)))
    pieces = (8, D_MODEL // 2, IN_COLS // N_CHIPS)
    nb_seq = t // tb_seq
    dx, d_g1, r_in, vec_sum, mat_sum = _in_proj_bwd(
        dz, w_in_f, x2d, dx_res, norm_g, tb_seq,
        reduce=([(g_in.reshape(pieces), False, g_in16.reshape(pieces))], BF16,
                (0, nb_seq // 8, nb_seq // 2, nb_seq - 1, nb_seq - 1)),
        shards=[(vec_mine.reshape(VEC_BAG_ROWS // N_CHIPS, D_MODEL), 0, True),
                (mat_mine.reshape(MAT_BAG_ROWS // N_CHIPS, HEAD_DIM), 0, True)])
    g_g1 = _all_reduce_tile(d_g1, "allreduce_norm_g")

    def big_update(w, g2d, m, v, rows, name):
        d, nm, nv = _adamw(w[0], g2d, m[0], v[0], rows, name)
        return g2d[None], d[None], nm[None], nv[None]

    u_in = big_update(w_in, r_in.reshape(D_MODEL, IN_COLS // N_CHIPS), m_w_in, v_w_in, 256, "adamw_w_in")
    u_pl = big_update(w_proj_lru, r_pl.reshape(D_MODEL // N_CHIPS, D_MODEL), m_w_proj_lru, v_w_proj_lru, 256, "adamw_w_proj_lru")
    u_pp = big_update(w_proj_pool, r_pp.reshape(POOL_WIDTH, D_MODEL // N_CHIPS), m_w_proj_pool, v_w_proj_pool, 512, "adamw_w_proj_pool")
    u_out = big_update(w_out, r_out.reshape(D_MODEL // N_CHIPS, D_MODEL), m_w_out, v_w_out, 256, "adamw_w_out")
    u_pg = big_update(w_ple_gate, r_pg.reshape(D_MODEL // N_CHIPS, D_MODEL), m_w_ple_gate, v_w_ple_gate, 256, "adamw_w_ple_gate")
    u_pe = big_update(w_ple_proj, r_pe.reshape(p_dim, D_MODEL // N_CHIPS), m_w_ple_proj, v_w_ple_proj, 256, "adamw_w_ple_proj")

    small = [("norm_g", norm_g, m_norm_g, v_norm_g), ("conv_b", conv_b, m_conv_b, v_conv_b),
             ("lru_w_a", lru_w_a, m_lru_w_a, v_lru_w_a), ("lru_b_a", lru_b_a, m_lru_b_a, v_lru_b_a),
             ("lru_w_x", lru_w_x, m_lru_w_x, v_lru_w_x), ("lru_b_x", lru_b_x, m_lru_b_x, v_lru_b_x),
             ("lru_lambda", lru_lambda, m_lru_lambda, v_lru_lambda), ("pool_w", pool_w, m_pool_w, v_pool_w),
             ("pool_scale", pool_scale, m_pool_scale, v_pool_scale),
             ("ple_norm_g", ple_norm_g, m_ple_norm_g, v_ple_norm_g), ("final_g", final_g, m_final_g, v_final_g)]

    def view(a):
        return a.reshape(-1, a.shape[-1]) if a.ndim != 3 else a[0]

    cw_at = F32_SUBLANES * VEC_BAG_SLOTS.index("conv_w")
    cw_cols = D_MODEL // N_CHIPS
    g_cw = lax.dynamic_slice(vec_sum, (cw_at, chip * cw_cols), (CONV_WIDTH, cw_cols))
    flat = _adamw_replicated(vec_sum, mat_sum, g_g1, [(name,) + tuple(view(a) for a in arrs) for name, *arrs in small],
                             (conv_w[0], m_conv_w[0], v_conv_w[0], g_cw))
    u_small = {name: tuple(flat[4 * k + pick].reshape(arrs[0].shape) for pick in range(4))
               for k, (name, *arrs) in enumerate(small)}
    u_cw = tuple(a[None] for a in (g_cw,) + tuple(flat[4 * len(small):]))

    loss = vec_sum[F32_SUBLANES * VEC_BAG_SLOTS.index("loss"), 0]
    grad_x = dx.reshape(bsz, seq, D_MODEL)

    def ordered(pick):
        s = {name: u[pick] for name, u in u_small.items()}
        return [s["norm_g"], u_in[pick], u_cw[pick], s["conv_b"], s["lru_w_a"], s["lru_b_a"], s["lru_w_x"], s["lru_b_x"],
                s["lru_lambda"], s["pool_w"], s["pool_scale"], u_pl[pick], u_pp[pick], u_out[pick], s["ple_norm_g"],
                u_pg[pick], u_pe[pick], s["final_g"]]

    return (loss, grad_x, *ordered(0), *ordered(1), *ordered(2), *ordered(3))
```

```python
import jax
import jax.numpy as jnp
from jax import lax
from jax.experimental import pallas as pl
from jax.experimental.pallas import tpu as pltpu

F32 = jnp.float32
BF16 = jnp.bfloat16
MESH = pl.DeviceIdType.MESH

D_MODEL = 1024
LRU_HEADS = 8
HEAD_DIM = 128
CONV_WIDTH = 4
LRU_C = 8.0
POOL_WIDTH = 512
POOL_WINDOWS = (2, 4, 8, 16)
POOL_GROUP_DIM = 128
IN_COLS = 5120
N_CHIPS = 4
EPS = 1e-6

ADAM_LR = 0.001
ADAM_B1 = 0.9
ADAM_B2 = 0.999
ADAM_EPS = 1e-08
ADAM_WD = 0.01
ADAM_STEP = 10

F32_SUBLANES = 8
CONV_HIST = 8
POOL_HIST = 16
VMEM_LIMIT_BYTES = 58 * 1024 * 1024
VEC_BAG_SLOTS = ("norm_g", "conv_w", "conv_b", "lru_b_a", "lru_b_x", "lru_lambda", "pool_scale", "ple_norm_g",
                 "final_g", "loss")
VEC_BAG_ROWS = 128
MAT_BAG_AT = {"lru_w_a": 0, "lru_w_x": LRU_HEADS * HEAD_DIM, "pool_w": 2 * LRU_HEADS * HEAD_DIM}
MAT_BAG_ROWS = 2 * LRU_HEADS * HEAD_DIM + len(POOL_WINDOWS) * POOL_GROUP_DIM


def _bag_row(name, k=0):
    at = F32_SUBLANES * VEC_BAG_SLOTS.index(name) + k
    return slice(at, at + 1)


def _bag_rows(name):
    at = F32_SUBLANES * VEC_BAG_SLOTS.index(name)
    return slice(at, at + F32_SUBLANES)


def _dot(a, b):
    return jnp.dot(a, b, preferred_element_type=F32)


def _dot_nt(a, b):
    return lax.dot_general(a, b, (((1,), (1,)), ((), ())), preferred_element_type=F32)


def _dot_tn(a, b):
    return lax.dot_general(a, b, (((0,), (0,)), ((), ())), preferred_element_type=F32)


def _sigmoid(v):
    return jax.nn.sigmoid(v)


def _softplus(v):
    return jnp.maximum(v, 0.0) + jnp.log1p(jnp.exp(-jnp.abs(v)))


def _place():
    return lax.axis_index("x"), lax.axis_index("y"), lax.axis_index("c")


GATHER_SEMS = 6


def _gather_shapes(shards):
    out_shape = []
    for arr, axis, _ in shards:
        r, cols = arr.shape
        out_shape.append(jax.ShapeDtypeStruct((N_CHIPS * r, cols) if axis == 0 else (r, N_CHIPS * cols), arr.dtype))
    n = len(shards)
    sems = [pltpu.SemaphoreType.DMA((n * GATHER_SEMS,)), pltpu.SemaphoreType.DMA((n * GATHER_SEMS,)),
            pltpu.SemaphoreType.DMA((n,))]
    return out_shape, sems


def _gather_steps(shards, ins, outs, send_sems, recv_sems, local_sems):
    n = len(shards)
    x, y, c = _place()
    me, sibling = (x, y, c), (x, y, 1 - c)
    chips = [(x, 1 - y), (1 - x, y), (1 - x, 1 - y)]

    def region(k, cx, cy, hc):
        (r, cols), axis = shards[k][0].shape, shards[k][1]
        j = 2 * cx + cy
        if axis == 0:
            if hc is None:
                return outs[k].at[pl.ds(j * r, r), :]
            return outs[k].at[pl.ds(j * r + hc * (r // 2), r // 2), :]
        if hc is None:
            return outs[k].at[:, pl.ds(j * cols, cols)]
        return outs[k].at[pl.ds(hc * (r // 2), r // 2), pl.ds(j * cols, cols)]

    def remote(k, sem, block, to, src=None):
        dst = region(k, *block)
        return pltpu.make_async_remote_copy(
            src_ref=dst if src is None else src, dst_ref=dst,
            send_sem=send_sems.at[k * GATHER_SEMS + sem], recv_sem=recv_sems.at[k * GATHER_SEMS + sem],
            device_id=to, device_id_type=MESH)

    def first(k, idx):
        r, split = shards[k][0].shape[0], shards[k][2]
        src = ins[k].at[pl.ds(c * (r // 2), r // 2), :] if split else ins[k]
        return remote(k, idx, (x, y, c if split else None), (*chips[idx], c), src=src)

    def relay(k):
        src_chip = (jnp.bitwise_xor(x, 1 - c), jnp.bitwise_xor(y, c))
        dst_chip = (jnp.bitwise_xor(x, c), jnp.bitwise_xor(y, 1 - c))
        return remote(k, 2, (*src_chip, c), (*dst_chip, c))

    def passed(k, idx):
        return remote(k, 3 + idx, (*chips[idx], c), sibling)

    def mine(k):
        return pltpu.make_async_copy(ins[k], region(k, x, y, None), local_sems.at[k])

    def start():
        for k in range(n):
            mine(k).start()
            for idx in range(2 if shards[k][2] else 3):
                first(k, idx).start()

    def relay_on():
        for k in range(n):
            split = shards[k][2]
            for idx in range(2):
                remote(k, idx, (*chips[idx], c if split else None), me).wait_recv()
            if split:
                relay(k).start()
                passed(k, 0).start()
                passed(k, 1).start()

    def finish():
        for k in range(n):
            split = shards[k][2]
            remote(k, 2, (*chips[2], c if split else None), me).wait_recv()
            if split:
                passed(k, 2).start()
        for k in range(n):
            if shards[k][2]:
                for idx in range(3):
                    remote(k, 3 + idx, (*chips[idx], 1 - c), me).wait_recv()
        for k in range(n):
            if shards[k][2]:
                for cp in (first(k, 0), first(k, 1), relay(k), passed(k, 0), passed(k, 1), passed(k, 2)):
                    cp.wait_send()
            else:
                for idx in range(3):
                    first(k, idx).wait_send()
            mine(k).wait()

    return start, relay_on, finish


RS_ADD_ROWS = (64, 32, 16, 8)


def _all_reduce_tile(v, name):
    n_dev = 2 * N_CHIPS
    flips = [(dx, dy, dc) for dx in (0, 1) for dy in (0, 1) for dc in (0, 1)][1:]

    def body(v_ref, o_ref, slots, send_sems, recv_sems):
        x, y, c = _place()
        mine = 4 * x + 2 * y + c

        def copy(k, to_flip, slot):
            dx, dy, dc = to_flip
            peer = (jnp.bitwise_xor(x, dx), jnp.bitwise_xor(y, dy), jnp.bitwise_xor(c, dc))
            return pltpu.make_async_remote_copy(
                src_ref=v_ref, dst_ref=slots.at[slot], send_sem=send_sems.at[k], recv_sem=recv_sems.at[k],
                device_id=peer, device_id_type=MESH)

        sends = [copy(k, flip, mine) for k, flip in enumerate(flips)]
        for cp in sends:
            cp.start()
        slots[mine] = v_ref[...]
        for k, (dx, dy, dc) in enumerate(flips):
            copy(k, (dx, dy, dc), jnp.bitwise_xor(mine, 4 * dx + 2 * dy + dc)).wait_recv()
        total = slots[0]
        for d in range(1, n_dev):
            total = total + slots[d]
        o_ref[...] = total
        for cp in sends:
            cp.wait_send()

    return pl.pallas_call(
        body, name=name, out_shape=jax.ShapeDtypeStruct(v.shape, F32),
        in_specs=[pl.BlockSpec(memory_space=pltpu.VMEM)], out_specs=pl.BlockSpec(memory_space=pltpu.VMEM),
        scratch_shapes=[pltpu.VMEM((n_dev,) + v.shape, F32), pltpu.SemaphoreType.DMA((n_dev - 1,)),
                        pltpu.SemaphoreType.DMA((n_dev - 1,))],
    )(v)


RS_SEMS = 8
RS_LOCAL_SEMS = 5


def _rs_piece_shape(part):
    arr, cols = part[0], part[1]
    return (arr.shape[0] // 2, arr.shape[1] // N_CHIPS) if cols else tuple(arr.shape[1:])


def _rs_operands(parts):
    return [p[0] for p in parts] + [p[0] if p[2] is None else p[2] for p in parts]


def _rs_wires(parts, wire):
    return list(wire) if isinstance(wire, (list, tuple)) else [wire] * len(parts)


def _rs_shapes(parts, wire):
    n = len(parts)
    shapes = [_rs_piece_shape(p) for p in parts]
    out_shape = [jax.ShapeDtypeStruct((2,) + s, F32) for s in shapes]
    scratch = []
    for lead, kind in ((N_CHIPS, "f32"), (N_CHIPS, "narrow"), (N_CHIPS, "wire"), (None, "f32"), (N_CHIPS, "wire")):
        for s, p, w in zip(shapes, parts, _rs_wires(parts, wire)):
            dtype = {"f32": F32, "narrow": F32 if p[2] is None else p[2].dtype, "wire": w}[kind]
            scratch.append(pltpu.VMEM(s if lead is None else (lead,) + s, dtype))
    scratch += [pltpu.SemaphoreType.DMA((n * RS_SEMS,)), pltpu.SemaphoreType.DMA((n * RS_SEMS,)),
                pltpu.SemaphoreType.DMA((n * RS_LOCAL_SEMS,))]
    return out_shape, scratch


def _rs_steps(parts, ins, outs, scratch):
    n = len(parts)
    own, sib, got, fin, snd = (scratch[k * n:(k + 1) * n] for k in range(5))
    send_sems, recv_sems, local_sems = scratch[5 * n:]
    shapes = [_rs_piece_shape(p) for p in parts]
    x, y, c = _place()
    j_me = 2 * x + y
    me, sibling = (x, y, c), (x, y, 1 - c)

    def piece(a, jj, core, narrow=False):
        ref = ins[n + a] if narrow else ins[a]
        if parts[a][1]:
            r, cl = shapes[a]
            return ref.at[pl.ds(core * r, r), pl.ds(jj * cl, cl)]
        return ref.at[2 * jj + core]

    def remote(a, sem, src, dst, to):
        return pltpu.make_async_remote_copy(
            src_ref=src, dst_ref=dst, send_sem=send_sems.at[a * RS_SEMS + sem],
            recv_sem=recv_sems.at[a * RS_SEMS + sem], device_id=to, device_id_type=MESH)

    def rows_loop(a, fn):
        r = shapes[a][0]
        step = max(s for s in RS_ADD_ROWS if r % s == 0)

        def it(i, carry):
            fn(pl.ds(pl.multiple_of(i * step, step), step))
            return carry

        lax.fori_loop(0, r // step, it, 0)

    def load(a, jj):
        return pltpu.make_async_copy(piece(a, jj, c), own[a].at[jj], local_sems.at[a * RS_LOCAL_SEMS + jj])

    def to_sibling(a, jj):
        return remote(a, jj, piece(a, jj, 1 - c, narrow=True), sib[a].at[jj], sibling)

    near = (jnp.bitwise_xor(x, 1 - c), jnp.bitwise_xor(y, c))
    far = (jnp.bitwise_xor(x, c), jnp.bitwise_xor(y, 1 - c))
    diag = (1 - x, 1 - y)
    FROM_NEAR, FROM_FAR, FEED = 0, 1, 2

    def chip_of(chip):
        return 2 * chip[0] + chip[1]

    def feed(a):
        return remote(a, 4, snd[a].at[chip_of(diag)], got[a].at[FEED], (*near, c))

    def to_near(a):
        return remote(a, 5, snd[a].at[chip_of(near)], got[a].at[FROM_NEAR], (*near, c))

    def to_far(a):
        return remote(a, 6, snd[a].at[chip_of(far)], got[a].at[FROM_FAR], (*far, c))

    def store(a):
        return pltpu.make_async_copy(fin[a], outs[a].at[c], local_sems.at[a * RS_LOCAL_SEMS + 4])

    def result_to_sibling(a):
        return remote(a, 7, fin[a], outs[a].at[c], sibling)

    def exchange():
        for a in range(n):
            for jj in range(N_CHIPS):
                load(a, jj).start()
                to_sibling(a, jj).start()

    def chip_sums():
        for a in range(n):
            for jj in range(N_CHIPS):
                load(a, jj).wait()
                remote(a, jj, sib[a].at[jj], sib[a].at[jj], me).wait_recv()

                def add(sl, a=a, jj=jj):
                    q = own[a][jj, sl, :] + sib[a][jj, sl, :].astype(F32)
                    own[a][jj, sl, :] = q
                    snd[a][jj, sl, :] = q.astype(snd[a].dtype)

                rows_loop(a, add)
        for a in range(n):
            feed(a).start()
        for a in range(n):
            to_near(a).start()

    def relay():
        for a in range(n):
            remote(a, 4, got[a].at[FEED], got[a].at[FEED], me).wait_recv()

            def add(sl, a=a):
                pair = own[a][chip_of(far), sl, :] + got[a][FEED, sl, :].astype(F32)
                snd[a][chip_of(far), sl, :] = pair.astype(snd[a].dtype)

            rows_loop(a, add)
            to_far(a).start()

    def totals():
        for a in range(n):
            remote(a, 5, got[a].at[FROM_NEAR], got[a].at[FROM_NEAR], me).wait_recv()
            remote(a, 6, got[a].at[FROM_FAR], got[a].at[FROM_FAR], me).wait_recv()

            def total(sl, a=a):
                fin[a][sl, :] = (own[a][j_me, sl, :] + got[a][FROM_NEAR, sl, :].astype(F32)) + (
                    got[a][FROM_FAR, sl, :].astype(F32))

            rows_loop(a, total)
            store(a).start()
            result_to_sibling(a).start()

    def finish():
        for a in range(n):
            remote(a, 7, outs[a].at[1 - c], outs[a].at[1 - c], me).wait_recv()
        for a in range(n):
            for jj in range(N_CHIPS):
                to_sibling(a, jj).wait_send()
            for cp in (feed(a), to_near(a), to_far(a), result_to_sibling(a)):
                cp.wait_send()
            store(a).wait()

    return exchange, chip_sums, relay, totals, finish


def _rms(x):
    r = lax.rsqrt(jnp.mean(x * x, axis=-1, keepdims=True) + EPS)
    return x * r, r


def _rms_bwd(dxn, xn, r):
    return r * (dxn - xn * jnp.mean(dxn * xn, axis=-1, keepdims=True))


def _in_proj_gather(x2d, norm_g, w_in_sh, shards, tb):
    t = x2d.shape[0]
    nb = t // tb
    cols = IN_COLS // N_CHIPS
    half = D_MODEL // 2
    n = len(shards)

    def body(x_ref, g_ref, win_ref, *refs):
        ins = refs[:n]
        z_ref, h_ref, wfull_ref = refs[n:n + 3]
        outs = refs[n + 3:2 * n + 3]
        wv, h_buf, send_sems, recv_sems, local_sems, w_send, w_recv, w_local = refs[2 * n + 3:]
        s, i = pl.program_id(0), pl.program_id(1)
        x, y, c = _place()
        me, sibling = (x, y, c), (x, y, 1 - c)
        chips = [(x, 1 - y), (1 - x, y), (1 - x, 1 - y)]

        def w_half(cx, cy, hc):
            return wv.at[2 * cx + cy, pl.ds(hc * half, half), :]

        def w_remote(sem, block, to, src=None):
            dst = w_half(*block)
            return pltpu.make_async_remote_copy(
                src_ref=dst if src is None else src, dst_ref=dst, send_sem=w_send.at[sem],
                recv_sem=w_recv.at[sem], device_id=to, device_id_type=MESH)

        def w_first(idx):
            return w_remote(idx, (x, y, c), (*chips[idx], c), src=win_ref.at[pl.ds(c * half, half), :])

        def w_relay():
            src_chip = (jnp.bitwise_xor(x, 1 - c), jnp.bitwise_xor(y, c))
            dst_chip = (jnp.bitwise_xor(x, c), jnp.bitwise_xor(y, 1 - c))
            return w_remote(2, (*src_chip, c), (*dst_chip, c))

        def w_pass(idx):
            return w_remote(3 + idx, (*chips[idx], c), sibling)

        def w_store(k, cx, cy):
            jj = 2 * cx + cy
            return pltpu.make_async_copy(wv.at[jj], wfull_ref.at[:, pl.ds(jj * cols, cols)], w_local.at[k])

        start_rest, relay_rest, finish_rest = _gather_steps(shards, ins, outs, send_sems, recv_sems, local_sems)
        own = pltpu.make_async_copy(win_ref, wv.at[2 * x + y], w_local.at[4])

        @pl.when((s == 0) & (i == 0))
        def _():
            own.start()
            w_first(0).start()
            w_first(1).start()
            start_rest()
            own.wait()
            w_store(0, x, y).start()

        @pl.when((s == 1) & (i == 0))
        def _():
            w_remote(0, (*chips[0], c), me).wait_recv()
            w_remote(1, (*chips[1], c), me).wait_recv()
            w_relay().start()
            w_pass(0).start()
            w_pass(1).start()
            w_remote(3, (*chips[0], 1 - c), me).wait_recv()
            w_store(1, *chips[0]).start()

        @pl.when((s == 2) & (i == 0))
        def _():
            w_remote(4, (*chips[1], 1 - c), me).wait_recv()
            w_store(2, *chips[1]).start()

        @pl.when((s == 3) & (i == 0))
        def _():
            w_remote(2, (*chips[2], c), me).wait_recv()
            w_pass(2).start()
            w_remote(5, (*chips[2], 1 - c), me).wait_recv()
            w_store(3, *chips[2]).start()

        xn, _ = _rms(x_ref[...])
        h = (xn * g_ref[...]).astype(BF16)
        keep_h = pltpu.make_async_copy(h_buf, h_ref.at[pl.ds(pl.multiple_of(i * tb, tb), tb), :], w_local.at[5])

        @pl.when(s == 0)
        def _():
            h_buf[...] = h
            keep_h.start()

        z_ref[...] = _dot(h, wv[jnp.bitwise_xor(2 * x + y, s)])
        pl.when(s == 0)(keep_h.wait)

        @pl.when((s == N_CHIPS - 1) & (i == nb - 1))
        def _():
            relay_rest()
            finish_rest()
            for cp in (w_first(0), w_first(1), w_relay(), w_pass(0), w_pass(1), w_pass(2)):
                cp.wait_send()
            w_store(0, x, y).wait()
            for idx in range(3):
                w_store(idx + 1, *chips[idx]).wait()

    rest_shape, rest_sems = _gather_shapes(shards)
    out_shape = [jax.ShapeDtypeStruct((t, IN_COLS), F32), jax.ShapeDtypeStruct((t, D_MODEL), BF16),
                 jax.ShapeDtypeStruct((D_MODEL, IN_COLS), BF16)] + rest_shape
    any_spec = pl.BlockSpec(memory_space=pl.ANY)

    def z_map(s, i):
        return (i, jnp.bitwise_xor(2 * lax.axis_index("x") + lax.axis_index("y"), s))

    return pl.pallas_call(
        body, name="in_proj", out_shape=tuple(out_shape),
        grid=(N_CHIPS, nb),
        in_specs=[pl.BlockSpec((tb, D_MODEL), lambda s, i: (i, 0)),
                  pl.BlockSpec((1, D_MODEL), lambda s, i: (0, 0)), any_spec] + [any_spec] * n,
        out_specs=tuple([pl.BlockSpec((tb, cols), z_map), any_spec, any_spec] + [any_spec] * n),
        scratch_shapes=[pltpu.VMEM((N_CHIPS, D_MODEL, cols), BF16), pltpu.VMEM((tb, D_MODEL), BF16)] + rest_sems + [
            pltpu.SemaphoreType.DMA((GATHER_SEMS,)), pltpu.SemaphoreType.DMA((GATHER_SEMS,)),
            pltpu.SemaphoreType.DMA((N_CHIPS + 2,))],
        compiler_params=pltpu.CompilerParams(dimension_semantics=("arbitrary", "arbitrary"),
                                             vmem_limit_bytes=VMEM_LIMIT_BYTES),
    )(x2d, norm_g, w_in_sh, *[sh[0] for sh in shards])


def _in_proj_bwd(dz, w_in, x2d, dx_res, norm_g, tb, reduce, shards):
    t = x2d.shape[0]
    nb = t // tb
    parts, wire, steps = reduce
    n = len(parts)
    k = len(shards)

    def body(dz_ref, w_ref, x_ref, dres_ref, g_ref, *refs):
        at = 2 * n + k
        dx_ref, dg_ref = refs[at:at + 2]
        rs_outs, g_outs = refs[at + 2:at + 2 + n], refs[at + 2 + n:at + 2 + n + k]
        scratch = refs[at + 2 + n + k:]
        rs = _rs_steps(parts, refs[:2 * n], rs_outs, scratch[:len(scratch) - 3])
        for step, when in zip(rs, steps):
            pl.when(pl.program_id(0) == when)(step)
        gather = _gather_steps(shards, refs[2 * n:at], g_outs, *scratch[len(scratch) - 3:])
        for step, when in zip(gather, (0, nb // 2, nb - 1)):
            pl.when(pl.program_id(0) == when)(step)

        @pl.when(pl.program_id(0) == 0)
        def _():
            dg_ref[...] = jnp.zeros_like(dg_ref)

        xn, r = _rms(x_ref[...])
        g = g_ref[...]
        dh = _dot_nt(dz_ref[...], w_ref[...])
        dg_ref[0:1, :] += jnp.sum(dh * xn, axis=0, keepdims=True)
        dx_ref[...] = dres_ref[...] + _rms_bwd(dh * g, xn, r)

    row = lambda i: (i, 0)
    fixed = lambda i: (0, 0)
    rs_shape, rs_scratch = _rs_shapes(parts, wire)
    g_shape, g_sems = _gather_shapes(shards)
    any_spec = pl.BlockSpec(memory_space=pl.ANY)
    return pl.pallas_call(
        body, name="in_proj_bwd",
        out_shape=tuple([jax.ShapeDtypeStruct((t, D_MODEL), F32), jax.ShapeDtypeStruct((F32_SUBLANES, D_MODEL), F32)]
                        + rs_shape + g_shape),
        grid=(nb,),
        in_specs=[pl.BlockSpec((tb, IN_COLS), row),
                  pl.BlockSpec((D_MODEL, IN_COLS), fixed, pipeline_mode=pl.Buffered(1)),
                  pl.BlockSpec((tb, D_MODEL), row), pl.BlockSpec((tb, D_MODEL), row),
                  pl.BlockSpec((1, D_MODEL), fixed)] + [any_spec] * (2 * n + k),
        out_specs=tuple([pl.BlockSpec((tb, D_MODEL), row), pl.BlockSpec((F32_SUBLANES, D_MODEL), fixed)]
                        + [any_spec] * (n + k)),
        scratch_shapes=rs_scratch + g_sems,
        compiler_params=pltpu.CompilerParams(dimension_semantics=("arbitrary",),
                                             vmem_limit_bytes=VMEM_LIMIT_BYTES),
    )(dz, w_in, x2d, dx_res, norm_g, *_rs_operands(parts), *[sh[0] for sh in shards])


def _weight_grad(lhs, rhs, n_chunks, tb, name, reduce=None):
    t, k = lhs.shape
    nc = rhs.shape[1] // n_chunks
    nb = t // tb
    parts, wire, steps = reduce if reduce is not None else ([], F32, ())
    n = len(parts)

    def body(l_ref, r_ref, *refs):
        o_ref, o16_ref = refs[2 * n:2 * n + 2]
        if n:
            at = pl.program_id(0) * nb + pl.program_id(1)
            rs = _rs_steps(parts, refs[:2 * n], refs[2 * n + 2:3 * n + 2], refs[3 * n + 2:])
            for step, when in zip(rs, steps):
                pl.when(at == when)(step)

        @pl.when(pl.program_id(1) == 0)
        def _():
            o_ref[...] = jnp.zeros_like(o_ref)

        o_ref[...] += _dot_tn(l_ref[...], r_ref[...])

        @pl.when(pl.program_id(1) == nb - 1)
        def _():
            o16_ref[...] = o_ref[...].astype(BF16)

    rs_shape, rs_scratch = _rs_shapes(parts, wire) if n else ([], [])
    any_spec = pl.BlockSpec(memory_space=pl.ANY)
    chunk = pl.BlockSpec((None, k, nc), lambda j, i: (j, 0, 0))
    return pl.pallas_call(
        body, name=name,
        out_shape=tuple([jax.ShapeDtypeStruct((n_chunks, k, nc), F32), jax.ShapeDtypeStruct((n_chunks, k, nc), BF16)]
                        + rs_shape),
        grid=(n_chunks, nb),
        in_specs=[pl.BlockSpec((tb, k), lambda j, i: (i, 0)), pl.BlockSpec((tb, nc), lambda j, i: (i, j))]
        + [any_spec] * (2 * n),
        out_specs=tuple([chunk, chunk] + [any_spec] * n),
        scratch_shapes=rs_scratch,
        compiler_params=pltpu.CompilerParams(dimension_semantics=("arbitrary", "arbitrary"),
                                             vmem_limit_bytes=VMEM_LIMIT_BYTES),
    )(lhs, rhs, *_rs_operands(parts))


def _adam_update(w, g, m, v):
    m_ = ADAM_B1 * m + (1.0 - ADAM_B1) * g
    v_ = ADAM_B2 * v + (1.0 - ADAM_B2) * jnp.square(g)
    m_hat = m_ / (1.0 - ADAM_B1 ** ADAM_STEP)
    v_hat = v_ / (1.0 - ADAM_B2 ** ADAM_STEP)
    return -ADAM_LR * (m_hat / (jnp.sqrt(v_hat) + ADAM_EPS) + ADAM_WD * w), m_, v_


def _adamw_replicated(vec_sum, mat_sum, norm_grad, entries, conv):
    n = len(entries)

    def grad_of(name, shape, vec_ref, mat_ref, norm_ref):
        if name == "norm_g":
            return norm_ref[0:1, :]
        if name in MAT_BAG_AT:
            return mat_ref[MAT_BAG_AT[name]:MAT_BAG_AT[name] + shape[0], :]
        if shape[0] == 1:
            return vec_ref[_bag_row(name), 0:shape[1]]
        return jnp.concatenate([vec_ref[_bag_row(name), h * shape[1]:(h + 1) * shape[1]] for h in range(shape[0])],
                               axis=0)

    def body(vec_ref, mat_ref, norm_ref, *refs):
        ins, outs = refs[:3 * n + 4], refs[3 * n + 4:]
        for k in range(n):
            w_ref, m_ref, v_ref = ins[3 * k:3 * k + 3]
            g = grad_of(entries[k][0], w_ref.shape, vec_ref, mat_ref, norm_ref)
            d, m_, v_ = _adam_update(w_ref[...], g, m_ref[...], v_ref[...])
            for ref, val in zip(outs[4 * k:4 * k + 4], (g, d, m_, v_)):
                ref[...] = val
        w_ref, m_ref, v_ref, g_ref = ins[3 * n:]
        for ref, val in zip(outs[4 * n:], _adam_update(w_ref[...], g_ref[...], m_ref[...], v_ref[...])):
            ref[...] = val

    arrays = [a for e in entries for a in e[1:]] + list(conv)
    out_shape = [jax.ShapeDtypeStruct(e[1].shape, F32) for e in entries for _ in range(4)]
    out_shape += [jax.ShapeDtypeStruct(conv[0].shape, F32)] * 3
    return pl.pallas_call(
        body, name="adamw_replicated", out_shape=tuple(out_shape),
        compiler_params=pltpu.CompilerParams(vmem_limit_bytes=VMEM_LIMIT_BYTES),
    )(vec_sum, mat_sum, norm_grad, *arrays)


def _adamw(w, g, m, v, rows, name):
    r, c = w.shape

    def body(w_ref, g_ref, m_ref, v_ref, d_ref, nm_ref, nv_ref):
        d_ref[...], nm_ref[...], nv_ref[...] = _adam_update(w_ref[...], g_ref[...], m_ref[...], v_ref[...])

    spec = pl.BlockSpec((rows, c), lambda i: (i, 0))
    return pl.pallas_call(
        body, name=name, out_shape=tuple(jax.ShapeDtypeStruct((r, c), F32) for _ in range(3)),
        grid=(r // rows,), in_specs=[spec] * 4, out_specs=(spec,) * 3,
        compiler_params=pltpu.CompilerParams(dimension_semantics=("arbitrary",),
                                             vmem_limit_bytes=VMEM_LIMIT_BYTES),
    )(w, g, m, v)


def _shift_down(ext, s):
    return pltpu.roll(ext, s, 0)


def _tile_shift(v, s):
    rows, cols = v.shape
    tiles = v.reshape(rows // F32_SUBLANES, F32_SUBLANES, cols)
    return pltpu.roll(tiles, s % F32_SUBLANES, 1).reshape(rows, cols)


def _shift_up(ext, s):
    return pltpu.roll(ext, ext.shape[0] - s, 0)


def _lru_gates(xc, wa_ref, ba, wx_ref, bx, lam):
    pa, px = [], []
    for h in range(LRU_HEADS):
        xh = xc[:, h * HEAD_DIM:(h + 1) * HEAD_DIM].astype(BF16)
        pa.append(_dot(xh, wa_ref[h]))
        px.append(_dot(xh, wx_ref[h]))
    r = _sigmoid(jnp.concatenate(pa, axis=1) + ba)
    ig = _sigmoid(jnp.concatenate(px, axis=1) + bx)
    sp = _softplus(-lam)
    log_a = (-LRU_C * r) * sp
    a = jnp.exp(log_a)
    mult = jnp.sqrt(jnp.tanh(-log_a) * (1.0 + a * a))
    return r, ig, a, mult, sp


def _conv(ext, w_ref, b):
    y = b + _shift_down(ext, 3) * w_ref[0:1, :]
    y = y + _shift_down(ext, 2) * w_ref[1:2, :]
    y = y + _shift_down(ext, 1) * w_ref[2:3, :]
    y = y + ext * w_ref[3:4, :]
    return y[CONV_HIST:, :]


def _pool_diff(ext, pos):
    out = []
    for g, k in enumerate(POOL_WINDOWS):
        col = ext[:, g * POOL_GROUP_DIM:(g + 1) * POOL_GROUP_DIM]
        s = col
        for step in range(g + 1):
            s = s + _shift_down(s, 2 ** step)
        count = jnp.minimum(pos + 1, k).astype(F32)
        out.append(s[POOL_HIST:, :] / count - col[POOL_HIST:, :])
    return out


def _pool_mix(diff, pw_ref):
    return jnp.concatenate([_dot(diff[g].astype(BF16), pw_ref[g]) for g in range(len(POOL_WINDOWS))], axis=1)


def _branch_specs(tb, row_map, fixed):
    fixed3 = lambda i: (0, 0, 0)
    return [pl.BlockSpec((CONV_WIDTH, D_MODEL), fixed), pl.BlockSpec((1, D_MODEL), fixed),
            pl.BlockSpec((LRU_HEADS, HEAD_DIM, HEAD_DIM), fixed3), pl.BlockSpec((1, D_MODEL), fixed),
            pl.BlockSpec((LRU_HEADS, HEAD_DIM, HEAD_DIM), fixed3), pl.BlockSpec((1, D_MODEL), fixed),
            pl.BlockSpec((1, D_MODEL), fixed),
            pl.BlockSpec((len(POOL_WINDOWS), POOL_GROUP_DIM, POOL_GROUP_DIM), fixed3),
            pl.BlockSpec((1, POOL_WIDTH), fixed)]


def _branches_fwd(z, weights, seq, tb, shards):
    t = z.shape[0]
    nb = t // tb
    nbe = seq // tb
    groups = tb // F32_SUBLANES
    n = len(shards)

    def body(xa_ref, ga_ref, xb_ref, gb_ref, cw_ref, cb_ref, wa_ref, ba_ref, wx_ref, bx_ref, lam_ref,
             pw_ref, ps_ref, *refs):
        g_ins = refs[:n]
        ya_ref, yb_ref, hl_ref = refs[n:n + 3]
        g_outs = refs[n + 3:2 * n + 3]
        xa_ext, xb_ext, carry, a_s, u_s, send_sems, recv_sems, local_sems = refs[2 * n + 3:]
        blk = pl.program_id(0) % nbe
        start_gather, relay_gather, finish_gather = _gather_steps(shards, g_ins, g_outs, send_sems, recv_sems,
                                                                  local_sems)
        pl.when(pl.program_id(0) == 0)(start_gather)
        pl.when(pl.program_id(0) == nb // 2)(relay_gather)

        @pl.when(blk == 0)
        def _():
            xa_ext[0:CONV_HIST, :] = jnp.zeros((CONV_HIST, D_MODEL), F32)
            xb_ext[0:POOL_HIST, :] = jnp.zeros((POOL_HIST, POOL_WIDTH), F32)
            carry[...] = jnp.zeros_like(carry)

        xa_ext[CONV_HIST:, :] = xa_ref[...]
        xb_ext[POOL_HIST:, :] = xb_ref[...]
        ea = xa_ext[...]
        eb = xb_ext[...]
        xa_ext[0:CONV_HIST, :] = ea[tb:, :]
        xb_ext[0:POOL_HIST, :] = eb[tb:, :]

        xc = _conv(ea, cw_ref, cb_ref[...])
        _, ig, a, mult, _ = _lru_gates(xc, wa_ref, ba_ref[...], wx_ref, bx_ref[...], lam_ref[...])
        u = mult * (ig * xc)
        row8 = lax.broadcasted_iota(jnp.int32, (tb, D_MODEL), 0) % F32_SUBLANES
        for s in (1, 2, 4):
            m = row8 >= s
            u = jnp.where(m, a * _tile_shift(u, s) + u, u)
            a = jnp.where(m, a * _tile_shift(a, s), a)
        a_s[...] = a
        u_s[...] = u

        def step(g, cr):
            sl = pl.ds(pl.multiple_of(g * F32_SUBLANES, F32_SUBLANES), F32_SUBLANES)
            hb = a_s[sl, :] * cr + u_s[sl, :]
            hl_ref[sl, :] = hb
            return jnp.broadcast_to(hb[F32_SUBLANES - 1:F32_SUBLANES, :], (F32_SUBLANES, D_MODEL))

        carry[...] = lax.fori_loop(0, groups, step, carry[...], unroll=4)
        ga = ga_ref[...]
        ya_ref[...] = (hl_ref[...] * (ga * _sigmoid(ga))).astype(BF16)

        pos = blk * tb + lax.broadcasted_iota(jnp.int32, (tb, POOL_GROUP_DIM), 0)
        ypre = _pool_mix(_pool_diff(eb, pos), pw_ref)
        gb = gb_ref[...]
        yb_ref[...] = ((ypre * ps_ref[...]) * (gb * _sigmoid(gb))).astype(BF16)
        pl.when(pl.program_id(0) == nb - 1)(finish_gather)

    row = lambda i: (i, 0)
    fixed = lambda i: (0, 0)
    any_spec = pl.BlockSpec(memory_space=pl.ANY)
    in_specs = [pl.BlockSpec((tb, D_MODEL), lambda i: (i, 0)), pl.BlockSpec((tb, D_MODEL), lambda i: (i, 1)),
                pl.BlockSpec((tb, POOL_WIDTH), lambda i: (i, 4)), pl.BlockSpec((tb, POOL_WIDTH), lambda i: (i, 5)),
                ] + _branch_specs(tb, row, fixed) + [any_spec] * n
    g_shape, g_sems = _gather_shapes(shards)
    return pl.pallas_call(
        body, name="branches_fwd",
        out_shape=tuple([jax.ShapeDtypeStruct((t, D_MODEL), BF16), jax.ShapeDtypeStruct((t, POOL_WIDTH), BF16),
                         jax.ShapeDtypeStruct((t, D_MODEL), F32)] + g_shape),
        grid=(nb,), in_specs=in_specs,
        out_specs=tuple([pl.BlockSpec((tb, D_MODEL), row), pl.BlockSpec((tb, POOL_WIDTH), row),
                         pl.BlockSpec((tb, D_MODEL), row)] + [any_spec] * n),
        scratch_shapes=[pltpu.VMEM((tb + CONV_HIST, D_MODEL), F32), pltpu.VMEM((tb + POOL_HIST, POOL_WIDTH), F32),
                        pltpu.VMEM((F32_SUBLANES, D_MODEL), F32),
                        pltpu.VMEM((tb, D_MODEL), F32), pltpu.VMEM((tb, D_MODEL), F32)] + g_sems,
        compiler_params=pltpu.CompilerParams(dimension_semantics=("arbitrary",),
                                             vmem_limit_bytes=VMEM_LIMIT_BYTES),
    )(z, z, z, z, *weights, *[sh[0] for sh in shards])


def _branches_bwd(z, hl, dya, dyb, dzm, weights, vec_bag, seq, tb):
    t = z.shape[0]
    nb = t // tb
    nbe = seq // tb
    groups = tb // F32_SUBLANES

    def body(xa_ref, xap_ref, ga_ref, xb_ref, xbp_ref, gb_ref, hl_ref, hlp_ref, dya_ref, dyb_ref, dzm_ref,
             cw_ref, cb_ref, wa_ref, ba_ref, wx_ref, bx_ref, lam_ref, pw_ref, ps_ref, vec_in_ref,
             dz_ref, vec_ref, mat_ref,
             xa_ext, xb_ext, hl_ext, a_ext, dxc_ext, dwin_ext, g_carry, b_s, d_s, g_s):
        i = pl.program_id(0)
        blk = (nb - 1 - i) % nbe

        def mat_rows(name, k):
            at = MAT_BAG_AT[name] + k * HEAD_DIM
            return slice(at, at + HEAD_DIM)

        @pl.when(i == 0)
        def _():
            vec_ref[...] = vec_in_ref[...]
            mat_ref[...] = jnp.zeros_like(mat_ref)

        @pl.when(blk == nbe - 1)
        def _():
            a_ext[tb:, :] = jnp.zeros((F32_SUBLANES, D_MODEL), F32)
            dxc_ext[tb:, :] = jnp.zeros((CONV_HIST, D_MODEL), F32)
            dwin_ext[tb:, :] = jnp.zeros((POOL_HIST, POOL_WIDTH), F32)
            g_carry[...] = jnp.zeros_like(g_carry)

        live = (blk > 0).astype(F32)
        xa_ext[0:CONV_HIST, :] = xap_ref[...] * live
        xa_ext[CONV_HIST:, :] = xa_ref[...]
        xb_ext[0:POOL_HIST, :] = xbp_ref[...] * live
        xb_ext[POOL_HIST:, :] = xb_ref[...]
        hl_ext[0:F32_SUBLANES, :] = hlp_ref[...] * live
        hl_ext[F32_SUBLANES:, :] = hl_ref[...]
        ea = xa_ext[...]
        eb = xb_ext[...]

        xc = _conv(ea, cw_ref, cb_ref[...])
        lam = lam_ref[...]
        r, ig, a, mult, sp = _lru_gates(xc, wa_ref, ba_ref[...], wx_ref, bx_ref[...], lam)
        hl = hl_ref[...]
        ga = ga_ref[...]
        sga = _sigmoid(ga)
        dya = dya_ref[...]
        dhl = dya * (ga * sga)
        dz_ref[:, D_MODEL:2 * D_MODEL] = (dya * hl * (sga * (1.0 + ga * (1.0 - sga)))).astype(BF16)

        a_ext[0:tb, :] = a
        b = _shift_up(a_ext[...], 1)[0:tb, :]
        a_ext[tb:, :] = jnp.broadcast_to(a[0:1, :], (F32_SUBLANES, D_MODEL))
        d = dhl
        row8 = lax.broadcasted_iota(jnp.int32, (tb, D_MODEL), 0) % F32_SUBLANES
        for s in (1, 2, 4):
            m = row8 < F32_SUBLANES - s
            d = jnp.where(m, d + b * _tile_shift(d, -s), d)
            b = jnp.where(m, b * _tile_shift(b, -s), b)
        b_s[...] = b
        d_s[...] = d

        def step(k, cr):
            sl = pl.ds(pl.multiple_of((groups - 1 - k) * F32_SUBLANES, F32_SUBLANES), F32_SUBLANES)
            gb_ = d_s[sl, :] + b_s[sl, :] * cr
            g_s[sl, :] = gb_
            return jnp.broadcast_to(gb_[0:1, :], (F32_SUBLANES, D_MODEL))

        g_carry[...] = lax.fori_loop(0, groups, step, g_carry[...], unroll=4)
        gsc = g_s[...]
        da = gsc * _shift_down(hl_ext[...], 1)[F32_SUBLANES:, :]
        dmult = gsc * (ig * xc)
        dig = gsc * (mult * xc)
        dxc = gsc * (mult * ig)
        dlog_a = da * a - (a * a) * dmult / mult
        dr = dlog_a * (-LRU_C * sp)
        vec_ref[_bag_row("lru_lambda"), :] += jnp.sum(dlog_a * (-LRU_C * r), axis=0, keepdims=True)
        dpa = dr * (r * (1.0 - r))
        dpx = dig * (ig * (1.0 - ig))
        vec_ref[_bag_row("lru_b_a"), :] += jnp.sum(dpa, axis=0, keepdims=True)
        vec_ref[_bag_row("lru_b_x"), :] += jnp.sum(dpx, axis=0, keepdims=True)
        back = []
        for h in range(LRU_HEADS):
            cols = slice(h * HEAD_DIM, (h + 1) * HEAD_DIM)
            xh = xc[:, cols].astype(BF16)
            dpa_h = dpa[:, cols].astype(BF16)
            dpx_h = dpx[:, cols].astype(BF16)
            mat_ref[mat_rows("lru_w_a", h), :] += _dot_tn(xh, dpa_h)
            mat_ref[mat_rows("lru_w_x", h), :] += _dot_tn(xh, dpx_h)
            back.append(_dot_nt(dpa_h, wa_ref[h]) + _dot_nt(dpx_h, wx_ref[h]))
        dxc = dxc + jnp.concatenate(back, axis=1)
        vec_ref[_bag_row("conv_b"), :] += jnp.sum(dxc, axis=0, keepdims=True)
        for k in range(CONV_WIDTH):
            tap = _shift_down(ea, CONV_WIDTH - 1 - k)[CONV_HIST:, :] if k < CONV_WIDTH - 1 else ea[CONV_HIST:, :]
            vec_ref[_bag_row("conv_w", k), :] += jnp.sum(dxc * tap, axis=0, keepdims=True)
        dxc_ext[0:tb, :] = dxc
        ed = dxc_ext[...]
        dxa = ed * cw_ref[3:4, :]
        dxa = dxa + _shift_up(ed, 1) * cw_ref[2:3, :]
        dxa = dxa + _shift_up(ed, 2) * cw_ref[1:2, :]
        dxa = dxa + _shift_up(ed, 3) * cw_ref[0:1, :]
        dz_ref[:, 0:D_MODEL] = dxa[0:tb, :].astype(BF16)
        dxc_ext[tb:, :] = dxc[0:CONV_HIST, :]

        pos = blk * tb + lax.broadcasted_iota(jnp.int32, (tb, POOL_GROUP_DIM), 0)
        diff = _pool_diff(eb, pos)
        ypre = _pool_mix(diff, pw_ref)
        ps = ps_ref[...]
        gb = gb_ref[...]
        sgb = _sigmoid(gb)
        dyb = dyb_ref[...]
        dyp = dyb * (gb * sgb)
        dz_ref[:, 2 * D_MODEL + POOL_WIDTH:3 * D_MODEL] = (
            dyb * (ypre * ps) * (sgb * (1.0 + gb * (1.0 - sgb)))).astype(BF16)
        vec_ref[_bag_row("pool_scale"), 0:POOL_WIDTH] += jnp.sum(dyp * ypre, axis=0, keepdims=True)
        dypre = dyp * ps
        for g, k in enumerate(POOL_WINDOWS):
            cols = slice(g * POOL_GROUP_DIM, (g + 1) * POOL_GROUP_DIM)
            dyg = dypre[:, cols].astype(BF16)
            mat_ref[mat_rows("pool_w", g), :] += _dot_tn(diff[g].astype(BF16), dyg)
            ddiff = _dot_nt(dyg, pw_ref[g])
            count = jnp.minimum(pos + 1, k).astype(F32)
            dwin = ddiff / count
            dwin_ext[0:tb, cols] = dwin
            s = dwin_ext[:, cols]
            for step_ in range(g + 1):
                s = s + _shift_up(s, 2 ** step_)
            dz_ref[:, 2 * D_MODEL + g * POOL_GROUP_DIM:2 * D_MODEL + (g + 1) * POOL_GROUP_DIM] = (
                s[0:tb, :] - ddiff).astype(BF16)
            dwin_ext[tb:, cols] = dwin[0:POOL_HIST, :]

        dz_ref[:, 3 * D_MODEL:] = dzm_ref[...]

        @pl.when(i == nb - 1)
        def _():
            row = _bag_row("lru_lambda")
            vec_ref[row, :] = vec_ref[row, :] * (-_sigmoid(-lam))

    rev = lambda i: (nb - 1 - i, 0)
    fixed = lambda i: (0, 0)

    def prev(rows, col):
        per = tb // rows
        return lambda i: (jnp.maximum((nb - 1 - i) * per - 1, 0), col)

    in_specs = [pl.BlockSpec((tb, D_MODEL), lambda i: (nb - 1 - i, 0)),
                pl.BlockSpec((CONV_HIST, D_MODEL), prev(CONV_HIST, 0)),
                pl.BlockSpec((tb, D_MODEL), lambda i: (nb - 1 - i, 1)),
                pl.BlockSpec((tb, POOL_WIDTH), lambda i: (nb - 1 - i, 4)),
                pl.BlockSpec((POOL_HIST, POOL_WIDTH), prev(POOL_HIST, 4)),
                pl.BlockSpec((tb, POOL_WIDTH), lambda i: (nb - 1 - i, 5)),
                pl.BlockSpec((tb, D_MODEL), rev),
                pl.BlockSpec((F32_SUBLANES, D_MODEL), prev(F32_SUBLANES, 0)),
                pl.BlockSpec((tb, D_MODEL), rev), pl.BlockSpec((tb, POOL_WIDTH), rev),
                pl.BlockSpec((tb, 2 * D_MODEL), rev)] + _branch_specs(tb, rev, fixed) + [
                    pl.BlockSpec((VEC_BAG_ROWS, D_MODEL), fixed)]
    out_shape = (jax.ShapeDtypeStruct((t, IN_COLS), BF16), jax.ShapeDtypeStruct((VEC_BAG_ROWS, D_MODEL), F32),
                 jax.ShapeDtypeStruct((MAT_BAG_ROWS, HEAD_DIM), F32))
    out_specs = (pl.BlockSpec((tb, IN_COLS), rev), pl.BlockSpec((VEC_BAG_ROWS, D_MODEL), fixed),
                 pl.BlockSpec((MAT_BAG_ROWS, HEAD_DIM), fixed))
    scratch = [pltpu.VMEM((tb + CONV_HIST, D_MODEL), F32), pltpu.VMEM((tb + POOL_HIST, POOL_WIDTH), F32),
               pltpu.VMEM((tb + F32_SUBLANES, D_MODEL), F32), pltpu.VMEM((tb + F32_SUBLANES, D_MODEL), F32),
               pltpu.VMEM((tb + CONV_HIST, D_MODEL), F32), pltpu.VMEM((tb + POOL_HIST, POOL_WIDTH), F32),
               pltpu.VMEM((F32_SUBLANES, D_MODEL), F32),
               pltpu.VMEM((tb, D_MODEL), F32), pltpu.VMEM((tb, D_MODEL), F32), pltpu.VMEM((tb, D_MODEL), F32)]
    return pl.pallas_call(
        body, name="branches_bwd", out_shape=out_shape, grid=(nb,), in_specs=in_specs, out_specs=out_specs,
        scratch_shapes=scratch, input_output_aliases={len(in_specs) - 1: 1},
        compiler_params=pltpu.CompilerParams(dimension_semantics=("arbitrary",),
                                             vmem_limit_bytes=VMEM_LIMIT_BYTES),
    )(z, z, z, z, z, z, hl, hl, dya, dyb, dzm, *weights, vec_bag)


def _merge_head(x2d, ya, yb, z, p2d, tgt, w_pl, w_pp, w_out, w_pg, w_pe, g2, gf, tb):
    t = x2d.shape[0]
    p_dim = p2d.shape[1]

    def body(x_ref, ya_ref, yb_ref, ma_ref, mb_ref, p_ref, t_ref, wpl_ref, wpp_ref, wout_ref, wpg_ref, wpe_ref,
             g2_ref, gf_ref,
             bag_ref, dxr_ref, dya_ref, dyb_ref, dzm_ref,
             mg_ref, do_ref, hn_ref, dgp_ref, dpe_ref, da_ref, dbm_ref, pbf_ref):
        @pl.when(pl.program_id(0) == 0)
        def _():
            bag_ref[...] = jnp.zeros_like(bag_ref)

        a_ = _dot(ya_ref[...], wpl_ref[...])
        bm = _dot(yb_ref[...], wpp_ref[...])
        sa = _sigmoid(ma_ref[...])
        sb = _sigmoid(mb_ref[...])
        mg = (sa * a_ + sb * bm).astype(BF16)
        mg_ref[...] = mg
        x1 = x_ref[...] + _dot(mg, wout_ref[...])
        xn2, r2 = _rms(x1)
        g2 = g2_ref[...]
        hn = (xn2 * g2).astype(BF16)
        hn_ref[...] = hn
        gate = _sigmoid(_dot(hn, wpg_ref[...]))
        pbf = p_ref[...].astype(BF16)
        pbf_ref[...] = pbf
        pe = _dot(pbf, wpe_ref[...])
        x2 = x1 + gate * pe
        xn3, r3 = _rms(x2)
        gf = gf_ref[...]
        err = xn3 * gf - t_ref[...]
        bag_ref[_bag_rows("loss"), 0:128] += 0.5 * jnp.sum(jnp.mean(err * err, axis=-1))

        dy = err * (1.0 / D_MODEL)
        bag_ref[_bag_row("final_g"), :] += jnp.sum(dy * xn3, axis=0, keepdims=True)
        dx2 = _rms_bwd(dy * gf, xn3, r3)
        dpe_ref[...] = (dx2 * gate).astype(BF16)
        dgp = ((dx2 * pe) * (gate * (1.0 - gate))).astype(BF16)
        dgp_ref[...] = dgp
        dhn = _dot_nt(dgp, wpg_ref[...])
        bag_ref[_bag_row("ple_norm_g"), :] += jnp.sum(dhn * xn2, axis=0, keepdims=True)
        dx1 = dx2 + _rms_bwd(dhn * g2, xn2, r2)
        dxr_ref[...] = dx1
        do = dx1.astype(BF16)
        do_ref[...] = do
        dmg = _dot_nt(do, wout_ref[...])
        da = (dmg * sa).astype(BF16)
        dbm = (dmg * sb).astype(BF16)
        da_ref[...] = da
        dbm_ref[...] = dbm
        dzm_ref[:, 0:D_MODEL] = (dmg * a_ * (sa * (1.0 - sa))).astype(BF16)
        dzm_ref[:, D_MODEL:] = (dmg * bm * (sb * (1.0 - sb))).astype(BF16)
        dya_ref[...] = _dot_nt(da, wpl_ref[...])
        dyb_ref[...] = _dot_nt(dbm, wpp_ref[...])

    row = lambda i: (i, 0)
    fixed = lambda i: (0, 0)

    def resident(shape):
        return pl.BlockSpec(shape, fixed, pipeline_mode=pl.Buffered(1))

    tok = lambda width: pl.BlockSpec((tb, width), row)
    in_specs = [tok(D_MODEL), tok(D_MODEL), tok(POOL_WIDTH),
                pl.BlockSpec((tb, D_MODEL), lambda i: (i, 3)), pl.BlockSpec((tb, D_MODEL), lambda i: (i, 4)),
                tok(p_dim), tok(D_MODEL),
                resident((D_MODEL, D_MODEL)), resident((POOL_WIDTH, D_MODEL)), resident((D_MODEL, D_MODEL)),
                resident((D_MODEL, D_MODEL)), resident((p_dim, D_MODEL)),
                pl.BlockSpec((1, D_MODEL), fixed), pl.BlockSpec((1, D_MODEL), fixed)]
    bf = lambda width: jax.ShapeDtypeStruct((t, width), BF16)
    f32 = lambda width: jax.ShapeDtypeStruct((t, width), F32)
    out_shape = (jax.ShapeDtypeStruct((VEC_BAG_ROWS, D_MODEL), F32),
                 f32(D_MODEL), f32(D_MODEL), f32(POOL_WIDTH), bf(2 * D_MODEL),
                 bf(D_MODEL), bf(D_MODEL), bf(D_MODEL), bf(D_MODEL), bf(D_MODEL), bf(D_MODEL), bf(D_MODEL), bf(p_dim))
    out_specs = (pl.BlockSpec((VEC_BAG_ROWS, D_MODEL), fixed),
                 tok(D_MODEL), tok(D_MODEL), tok(POOL_WIDTH), tok(2 * D_MODEL),
                 tok(D_MODEL), tok(D_MODEL), tok(D_MODEL), tok(D_MODEL), tok(D_MODEL), tok(D_MODEL), tok(D_MODEL),
                 tok(p_dim))
    return pl.pallas_call(
        body, name="merge_head", out_shape=out_shape, grid=(t // tb,), in_specs=in_specs, out_specs=out_specs,
        compiler_params=pltpu.CompilerParams(dimension_semantics=("arbitrary",),
                                             vmem_limit_bytes=VMEM_LIMIT_BYTES),
    )(x2d, ya, yb, z, z, p2d, tgt, w_pl, w_pp, w_out, w_pg, w_pe, g2, gf)


def kernel(x, p, norm_g, w_in, conv_w, conv_b, lru_w_a, lru_b_a, lru_w_x, lru_b_x, lru_lambda, pool_w, pool_scale, w_proj_lru, w_proj_pool, w_out, ple_norm_g, w_ple_gate, w_ple_proj, final_g, loss_target, m_norm_g, m_w_in, m_conv_w, m_conv_b, m_lru_w_a, m_lru_b_a, m_lru_w_x, m_lru_b_x, m_lru_lambda, m_pool_w, m_pool_scale, m_w_proj_lru, m_w_proj_pool, m_w_out, m_ple_norm_g, m_w_ple_gate, m_w_ple_proj, m_final_g, v_norm_g, v_w_in, v_conv_w, v_conv_b, v_lru_w_a, v_lru_b_a, v_lru_w_x, v_lru_b_x, v_lru_lambda, v_pool_w, v_pool_scale, v_w_proj_lru, v_w_proj_pool, v_w_out, v_ple_norm_g, v_w_ple_gate, v_w_ple_proj, v_final_g):
    bsz, seq, _ = x.shape
    t = bsz * seq
    tb_mm = min(1024, seq)
    tb_seq = min(256, seq // 2) if seq >= 512 else seq
    x2d = x.reshape(t, D_MODEL)
    p2d = p.reshape(t, p.shape[-1])
    tgt = loss_target.reshape(t, D_MODEL)
    chip = 2 * lax.axis_index("x") + lax.axis_index("y")

    rest = [(w_proj_lru[0], 0), (w_proj_pool[0], 1), (w_out[0], 0), (w_ple_gate[0], 0), (w_ple_proj[0], 1)]
    z, h_bf, w_in_f, conv_w_f = _in_proj_gather(x2d, norm_g, w_in[0].astype(BF16), [(conv_w[0], 1, False)], tb_mm)

    wa_bf = lru_w_a[0].astype(BF16)
    wx_bf = lru_w_x[0].astype(BF16)
    pw_bf = pool_w[0].astype(BF16)
    branch_w = (conv_w_f, conv_b, wa_bf, lru_b_a.reshape(1, D_MODEL), wx_bf, lru_b_x.reshape(1, D_MODEL),
                lru_lambda, pw_bf, pool_scale)

    ya, yb, hl, w_pl_f, w_pp_f, w_out_f, w_pg_f, w_pe_f = _branches_fwd(
        z, branch_w, seq, tb_seq, [(w.astype(BF16), axis, True) for w, axis in rest])
    (vec_bag, dx_res, dya, dyb, dzm, mg_bf, do_bf, hn_bf, dgp_bf, dpe_bf, da_bf, dbm_bf, p_bf) = _merge_head(
        x2d, ya, yb, z, p2d, tgt, w_pl_f, w_pp_f, w_out_f, w_pg_f, w_pe_f, ple_norm_g, final_g.reshape(1, D_MODEL),
        tb_seq)
    dz, vec_bag, mat_bag = _branches_bwd(z, hl, dya, dyb, dzm, branch_w, vec_bag, seq, tb_seq)

    tb_dw = min(1024, seq)
    def proj_grad(lhs, rhs, name, cols):
        g32, g16 = _weight_grad(lhs, rhs, 1, tb_dw, name)
        if cols:
            return g32[0], True, g16[0]
        rows = g32.shape[1] // 8
        return g32.reshape(8, rows, g32.shape[2]), False, g16.reshape(8, rows, g32.shape[2])

    p_dim = p2d.shape[1]
    proj_parts = [proj_grad(ya, da_bf, "dw_proj_lru", False), proj_grad(yb, dbm_bf, "dw_proj_pool", True),
                  proj_grad(mg_bf, do_bf, "dw_out", False), proj_grad(hn_bf, dgp_bf, "dw_ple_gate", False),
                  proj_grad(p_bf, dpe_bf, "dw_ple_proj", True)]
    nb_dw = t // tb_dw
    g_in, g_in16, r_pl, r_pp, r_out, r_pg, r_pe, vec_mine, mat_mine = _weight_grad(
        h_bf, dz, N_CHIPS, tb_dw, "dw_in",
        reduce=(proj_parts + [(vec_bag.reshape(8, VEC_BAG_ROWS // 8, D_MODEL), False, None),
                              (mat_bag.reshape(8, MAT_BAG_ROWS // 8, HEAD_DIM), False, None)],
                [BF16] * 5 + [F32] * 2,
                (0, nb_dw // 2, 2 * nb_dw - 1, 3 * nb_dw + nb_dw // 2, N_CHIPS * nb_dw - 1)))
    pieces = (8, D_MODEL // 2, IN_COLS // N_CHIPS)
    nb_seq = t // tb_seq
    dx, d_g1, r_in, vec_sum, mat_sum = _in_proj_bwd(
        dz, w_in_f, x2d, dx_res, norm_g, tb_seq,
        reduce=([(g_in.reshape(pieces), False, g_in16.reshape(pieces))], BF16,
                (0, nb_seq // 8, nb_seq // 2, nb_seq - 1, nb_seq - 1)),
        shards=[(vec_mine.reshape(VEC_BAG_ROWS // N_CHIPS, D_MODEL), 0, True),
                (mat_mine.reshape(MAT_BAG_ROWS // N_CHIPS, HEAD_DIM), 0, True)])
    g_g1 = _all_reduce_tile(d_g1, "allreduce_norm_g")

    def big_update(w, g2d, m, v, rows, name):
        d, nm, nv = _adamw(w[0], g2d, m[0], v[0], rows, name)
        return g2d[None], d[None], nm[None], nv[None]

    u_in = big_update(w_in, r_in.reshape(D_MODEL, IN_COLS // N_CHIPS), m_w_in, v_w_in, 256, "adamw_w_in")
    u_pl = big_update(w_proj_lru, r_pl.reshape(D_MODEL // N_CHIPS, D_MODEL), m_w_proj_lru, v_w_proj_lru, 256, "adamw_w_proj_lru")
    u_pp = big_update(w_proj_pool, r_pp.reshape(POOL_WIDTH, D_MODEL // N_CHIPS), m_w_proj_pool, v_w_proj_pool, 512, "adamw_w_proj_pool")
    u_out = big_update(w_out, r_out.reshape(D_MODEL // N_CHIPS, D_MODEL), m_w_out, v_w_out, 256, "adamw_w_out")
    u_pg = big_update(w_ple_gate, r_pg.reshape(D_MODEL // N_CHIPS, D_MODEL), m_w_ple_gate, v_w_ple_gate, 256, "adamw_w_ple_gate")
    u_pe = big_update(w_ple_proj, r_pe.reshape(p_dim, D_MODEL // N_CHIPS), m_w_ple_proj, v_w_ple_proj, 256, "adamw_w_ple_proj")

    small = [("norm_g", norm_g, m_norm_g, v_norm_g), ("conv_b", conv_b, m_conv_b, v_conv_b),
             ("lru_w_a", lru_w_a, m_lru_w_a, v_lru_w_a), ("lru_b_a", lru_b_a, m_lru_b_a, v_lru_b_a),
             ("lru_w_x", lru_w_x, m_lru_w_x, v_lru_w_x), ("lru_b_x", lru_b_x, m_lru_b_x, v_lru_b_x),
             ("lru_lambda", lru_lambda, m_lru_lambda, v_lru_lambda), ("pool_w", pool_w, m_pool_w, v_pool_w),
             ("pool_scale", pool_scale, m_pool_scale, v_pool_scale),
             ("ple_norm_g", ple_norm_g, m_ple_norm_g, v_ple_norm_g), ("final_g", final_g, m_final_g, v_final_g)]

    def view(a):
        return a.reshape(-1, a.shape[-1]) if a.ndim != 3 else a[0]

    cw_at = F32_SUBLANES * VEC_BAG_SLOTS.index("conv_w")
    cw_cols = D_MODEL // N_CHIPS
    g_cw = lax.dynamic_slice(vec_sum, (cw_at, chip * cw_cols), (CONV_WIDTH, cw_cols))
    flat = _adamw_replicated(vec_sum, mat_sum, g_g1, [(name,) + tuple(view(a) for a in arrs) for name, *arrs in small],
                             (conv_w[0], m_conv_w[0], v_conv_w[0], g_cw))
    u_small = {name: tuple(flat[4 * k + pick].reshape(arrs[0].shape) for pick in range(4))
               for k, (name, *arrs) in enumerate(small)}
    u_cw = tuple(a[None] for a in (g_cw,) + tuple(flat[4 * len(small):]))

    loss = vec_sum[F32_SUBLANES * VEC_BAG_SLOTS.index("loss"), 0]
    grad_x = dx.reshape(bsz, seq, D_MODEL)

    def ordered(pick):
        s = {name: u[pick] for name, u in u_small.items()}
        return [s["norm_g"], u_in[pick], u_cw[pick], s["conv_b"], s["lru_w_a"], s["lru_b_a"], s["lru_w_x"], s["lru_b_x"],
                s["lru_lambda"], s["pool_w"], s["pool_scale"], u_pl[pick], u_pp[pick], u_out[pick], s["ple_norm_g"],
                u_pg[pick], u_pe[pick], s["final_g"]]

    return (loss, grad_x, *ordered(0), *ordered(1), *ordered(2), *ordered(3))
```

```python
import jax
import jax.numpy as jnp
from jax import lax
from jax.experimental import pallas as pl
from jax.experimental.pallas import tpu as pltpu

F32 = jnp.float32
BF16 = jnp.bfloat16
MESH = pl.DeviceIdType.MESH

D_MODEL = 1024
LRU_HEADS = 8
HEAD_DIM = 128
CONV_WIDTH = 4
LRU_C = 8.0
POOL_WIDTH = 512
POOL_WINDOWS = (2, 4, 8, 16)
POOL_GROUP_DIM = 128
IN_COLS = 5120
N_CHIPS = 4
EPS = 1e-6

ADAM_LR = 0.001
ADAM_B1 = 0.9
ADAM_B2 = 0.999
ADAM_EPS = 1e-08
ADAM_WD = 0.01
ADAM_STEP = 10

F32_SUBLANES = 8
CONV_HIST = 8
POOL_HIST = 16
VMEM_LIMIT_BYTES = 58 * 1024 * 1024
VEC_BAG_SLOTS = ("norm_g", "conv_w", "conv_b", "lru_b_a", "lru_b_x", "lru_lambda", "pool_scale", "ple_norm_g",
                 "final_g", "loss")
VEC_BAG_ROWS = 128
MAT_BAG_AT = {"lru_w_a": 0, "lru_w_x": LRU_HEADS * HEAD_DIM, "pool_w": 2 * LRU_HEADS * HEAD_DIM}
MAT_BAG_ROWS = 2 * LRU_HEADS * HEAD_DIM + len(POOL_WINDOWS) * POOL_GROUP_DIM


def _bag_row(name, k=0):
    at = F32_SUBLANES * VEC_BAG_SLOTS.index(name) + k
    return slice(at, at + 1)


def _bag_rows(name):
    at = F32_SUBLANES * VEC_BAG_SLOTS.index(name)
    return slice(at, at + F32_SUBLANES)


def _dot(a, b):
    return jnp.dot(a, b, preferred_element_type=F32)


def _dot_nt(a, b):
    return lax.dot_general(a, b, (((1,), (1,)), ((), ())), preferred_element_type=F32)


def _dot_tn(a, b):
    return lax.dot_general(a, b, (((0,), (0,)), ((), ())), preferred_element_type=F32)


def _sigmoid(v):
    return 0.5 * jnp.tanh(0.5 * v) + 0.5


def _softplus(v):
    return jnp.maximum(v, 0.0) + jnp.log1p(jnp.exp(-jnp.abs(v)))


def _place():
    return lax.axis_index("x"), lax.axis_index("y"), lax.axis_index("c")


GATHER_SEMS = 6


def _gather_shapes(shards):
    out_shape = []
    for arr, axis, _ in shards:
        r, cols = arr.shape
        out_shape.append(jax.ShapeDtypeStruct((N_CHIPS * r, cols) if axis == 0 else (r, N_CHIPS * cols), arr.dtype))
    n = len(shards)
    sems = [pltpu.SemaphoreType.DMA((n * GATHER_SEMS,)), pltpu.SemaphoreType.DMA((n * GATHER_SEMS,)),
            pltpu.SemaphoreType.DMA((n,))]
    return out_shape, sems


def _gather_steps(shards, ins, outs, send_sems, recv_sems, local_sems):
    n = len(shards)
    x, y, c = _place()
    me, sibling = (x, y, c), (x, y, 1 - c)
    chips = [(x, 1 - y), (1 - x, y), (1 - x, 1 - y)]

    def region(k, cx, cy, hc):
        (r, cols), axis = shards[k][0].shape, shards[k][1]
        j = 2 * cx + cy
        if axis == 0:
            if hc is None:
                return outs[k].at[pl.ds(j * r, r), :]
            return outs[k].at[pl.ds(j * r + hc * (r // 2), r // 2), :]
        if hc is None:
            return outs[k].at[:, pl.ds(j * cols, cols)]
        return outs[k].at[pl.ds(hc * (r // 2), r // 2), pl.ds(j * cols, cols)]

    def remote(k, sem, block, to, src=None):
        dst = region(k, *block)
        return pltpu.make_async_remote_copy(
            src_ref=dst if src is None else src, dst_ref=dst,
            send_sem=send_sems.at[k * GATHER_SEMS + sem], recv_sem=recv_sems.at[k * GATHER_SEMS + sem],
            device_id=to, device_id_type=MESH)

    def first(k, idx):
        r, split = shards[k][0].shape[0], shards[k][2]
        src = ins[k].at[pl.ds(c * (r // 2), r // 2), :] if split else ins[k]
        return remote(k, idx, (x, y, c if split else None), (*chips[idx], c), src=src)

    def relay(k):
        src_chip = (jnp.bitwise_xor(x, 1 - c), jnp.bitwise_xor(y, c))
        dst_chip = (jnp.bitwise_xor(x, c), jnp.bitwise_xor(y, 1 - c))
        return remote(k, 2, (*src_chip, c), (*dst_chip, c))

    def passed(k, idx):
        return remote(k, 3 + idx, (*chips[idx], c), sibling)

    def mine(k):
        return pltpu.make_async_copy(ins[k], region(k, x, y, None), local_sems.at[k])

    def start():
        for k in range(n):
            mine(k).start()
            for idx in range(2 if shards[k][2] else 3):
                first(k, idx).start()

    def relay_on():
        for k in range(n):
            split = shards[k][2]
            for idx in range(2):
                remote(k, idx, (*chips[idx], c if split else None), me).wait_recv()
            if split:
                relay(k).start()
                passed(k, 0).start()
                passed(k, 1).start()

    def finish():
        for k in range(n):
            split = shards[k][2]
            remote(k, 2, (*chips[2], c if split else None), me).wait_recv()
            if split:
                passed(k, 2).start()
        for k in range(n):
            if shards[k][2]:
                for idx in range(3):
                    remote(k, 3 + idx, (*chips[idx], 1 - c), me).wait_recv()
        for k in range(n):
            if shards[k][2]:
                for cp in (first(k, 0), first(k, 1), relay(k), passed(k, 0), passed(k, 1), passed(k, 2)):
                    cp.wait_send()
            else:
                for idx in range(3):
                    first(k, idx).wait_send()
            mine(k).wait()

    return start, relay_on, finish


RS_ADD_ROWS = (64, 32, 16, 8)


def _all_reduce_tile(v, name):
    n_dev = 2 * N_CHIPS
    flips = [(dx, dy, dc) for dx in (0, 1) for dy in (0, 1) for dc in (0, 1)][1:]

    def body(v_ref, o_ref, slots, send_sems, recv_sems):
        x, y, c = _place()
        mine = 4 * x + 2 * y + c

        def copy(k, to_flip, slot):
            dx, dy, dc = to_flip
            peer = (jnp.bitwise_xor(x, dx), jnp.bitwise_xor(y, dy), jnp.bitwise_xor(c, dc))
            return pltpu.make_async_remote_copy(
                src_ref=v_ref, dst_ref=slots.at[slot], send_sem=send_sems.at[k], recv_sem=recv_sems.at[k],
                device_id=peer, device_id_type=MESH)

        sends = [copy(k, flip, mine) for k, flip in enumerate(flips)]
        for cp in sends:
            cp.start()
        slots[mine] = v_ref[...]
        for k, (dx, dy, dc) in enumerate(flips):
            copy(k, (dx, dy, dc), jnp.bitwise_xor(mine, 4 * dx + 2 * dy + dc)).wait_recv()
        total = slots[0]
        for d in range(1, n_dev):
            total = total + slots[d]
        o_ref[...] = total
        for cp in sends:
            cp.wait_send()

    return pl.pallas_call(
        body, name=name, out_shape=jax.ShapeDtypeStruct(v.shape, F32),
        in_specs=[pl.BlockSpec(memory_space=pltpu.VMEM)], out_specs=pl.BlockSpec(memory_space=pltpu.VMEM),
        scratch_shapes=[pltpu.VMEM((n_dev,) + v.shape, F32), pltpu.SemaphoreType.DMA((n_dev - 1,)),
                        pltpu.SemaphoreType.DMA((n_dev - 1,))],
    )(v)


RS_SEMS = 8
RS_LOCAL_SEMS = 5


def _rs_piece_shape(part):
    arr, cols = part[0], part[1]
    return (arr.shape[0] // 2, arr.shape[1] // N_CHIPS) if cols else tuple(arr.shape[1:])


def _rs_operands(parts):
    return [p[0] for p in parts] + [p[0] if p[2] is None else p[2] for p in parts]


def _rs_wires(parts, wire):
    return list(wire) if isinstance(wire, (list, tuple)) else [wire] * len(parts)


def _rs_shapes(parts, wire):
    n = len(parts)
    shapes = [_rs_piece_shape(p) for p in parts]
    out_shape = [jax.ShapeDtypeStruct((2,) + s, F32) for s in shapes]
    scratch = []
    for lead, kind in ((N_CHIPS, "f32"), (N_CHIPS, "narrow"), (N_CHIPS, "wire"), (None, "f32"), (N_CHIPS, "wire")):
        for s, p, w in zip(shapes, parts, _rs_wires(parts, wire)):
            dtype = {"f32": F32, "narrow": F32 if p[2] is None else p[2].dtype, "wire": w}[kind]
            scratch.append(pltpu.VMEM(s if lead is None else (lead,) + s, dtype))
    scratch += [pltpu.SemaphoreType.DMA((n * RS_SEMS,)), pltpu.SemaphoreType.DMA((n * RS_SEMS,)),
                pltpu.SemaphoreType.DMA((n * RS_LOCAL_SEMS,))]
    return out_shape, scratch


def _rs_steps(parts, ins, outs, scratch):
    n = len(parts)
    own, sib, got, fin, snd = (scratch[k * n:(k + 1) * n] for k in range(5))
    send_sems, recv_sems, local_sems = scratch[5 * n:]
    shapes = [_rs_piece_shape(p) for p in parts]
    x, y, c = _place()
    j_me = 2 * x + y
    me, sibling = (x, y, c), (x, y, 1 - c)

    def piece(a, jj, core, narrow=False):
        ref = ins[n + a] if narrow else ins[a]
        if parts[a][1]:
            r, cl = shapes[a]
            return ref.at[pl.ds(core * r, r), pl.ds(jj * cl, cl)]
        return ref.at[2 * jj + core]

    def remote(a, sem, src, dst, to):
        return pltpu.make_async_remote_copy(
            src_ref=src, dst_ref=dst, send_sem=send_sems.at[a * RS_SEMS + sem],
            recv_sem=recv_sems.at[a * RS_SEMS + sem], device_id=to, device_id_type=MESH)

    def rows_loop(a, fn):
        r = shapes[a][0]
        step = max(s for s in RS_ADD_ROWS if r % s == 0)

        def it(i, carry):
            fn(pl.ds(pl.multiple_of(i * step, step), step))
            return carry

        lax.fori_loop(0, r // step, it, 0)

    def load(a, jj):
        return pltpu.make_async_copy(piece(a, jj, c), own[a].at[jj], local_sems.at[a * RS_LOCAL_SEMS + jj])

    def to_sibling(a, jj):
        return remote(a, jj, piece(a, jj, 1 - c, narrow=True), sib[a].at[jj], sibling)

    near = (jnp.bitwise_xor(x, 1 - c), jnp.bitwise_xor(y, c))
    far = (jnp.bitwise_xor(x, c), jnp.bitwise_xor(y, 1 - c))
    diag = (1 - x, 1 - y)
    FROM_NEAR, FROM_FAR, FEED = 0, 1, 2

    def chip_of(chip):
        return 2 * chip[0] + chip[1]

    def feed(a):
        return remote(a, 4, snd[a].at[chip_of(diag)], got[a].at[FEED], (*near, c))

    def to_near(a):
        return remote(a, 5, snd[a].at[chip_of(near)], got[a].at[FROM_NEAR], (*near, c))

    def to_far(a):
        return remote(a, 6, snd[a].at[chip_of(far)], got[a].at[FROM_FAR], (*far, c))

    def store(a):
        return pltpu.make_async_copy(fin[a], outs[a].at[c], local_sems.at[a * RS_LOCAL_SEMS + 4])

    def result_to_sibling(a):
        return remote(a, 7, fin[a], outs[a].at[c], sibling)

    def exchange():
        for a in range(n):
            for jj in range(N_CHIPS):
                load(a, jj).start()
                to_sibling(a, jj).start()

    def chip_sums():
        for a in range(n):
            for jj in range(N_CHIPS):
                load(a, jj).wait()
                remote(a, jj, sib[a].at[jj], sib[a].at[jj], me).wait_recv()

                def add(sl, a=a, jj=jj):
                    q = own[a][jj, sl, :] + sib[a][jj, sl, :].astype(F32)
                    own[a][jj, sl, :] = q
                    snd[a][jj, sl, :] = q.astype(snd[a].dtype)

                rows_loop(a, add)
        for a in range(n):
            feed(a).start()
        for a in range(n):
            to_near(a).start()

    def relay():
        for a in range(n):
            remote(a, 4, got[a].at[FEED], got[a].at[FEED], me).wait_recv()

            def add(sl, a=a):
                pair = own[a][chip_of(far), sl, :] + got[a][FEED, sl, :].astype(F32)
                snd[a][chip_of(far), sl, :] = pair.astype(snd[a].dtype)

            rows_loop(a, add)
            to_far(a).start()

    def totals():
        for a in range(n):
            remote(a, 5, got[a].at[FROM_NEAR], got[a].at[FROM_NEAR], me).wait_recv()
            remote(a, 6, got[a].at[FROM_FAR], got[a].at[FROM_FAR], me).wait_recv()

            def total(sl, a=a):
                fin[a][sl, :] = (own[a][j_me, sl, :] + got[a][FROM_NEAR, sl, :].astype(F32)) + (
                    got[a][FROM_FAR, sl, :].astype(F32))

            rows_loop(a, total)
            store(a).start()
            result_to_sibling(a).start()

    def finish():
        for a in range(n):
            remote(a, 7, outs[a].at[1 - c], outs[a].at[1 - c], me).wait_recv()
        for a in range(n):
            for jj in range(N_CHIPS):
                to_sibling(a, jj).wait_send()
            for cp in (feed(a), to_near(a), to_far(a), result_to_sibling(a)):
                cp.wait_send()
            store(a).wait()

    return exchange, chip_sums, relay, totals, finish


def _rms(x):
    r = lax.rsqrt(jnp.mean(x * x, axis=-1, keepdims=True) + EPS)
    return x * r, r


def _rms_bwd(dxn, xn, r):
    return r * (dxn - xn * jnp.mean(dxn * xn, axis=-1, keepdims=True))


def _in_proj_gather(x2d, norm_g, w_in_sh, shards, tb):
    t = x2d.shape[0]
    nb = t // tb
    cols = IN_COLS // N_CHIPS
    half = D_MODEL // 2
    n = len(shards)

    def body(x_ref, g_ref, win_ref, *refs):
        ins = refs[:n]
        z_ref, h_ref, wfull_ref = refs[n:n + 3]
        outs = refs[n + 3:2 * n + 3]
        wv, h_buf, send_sems, recv_sems, local_sems, w_send, w_recv, w_local = refs[2 * n + 3:]
        s, i = pl.program_id(0), pl.program_id(1)
        x, y, c = _place()
        me, sibling = (x, y, c), (x, y, 1 - c)
        chips = [(x, 1 - y), (1 - x, y), (1 - x, 1 - y)]

        def w_half(cx, cy, hc):
            return wv.at[2 * cx + cy, pl.ds(hc * half, half), :]

        def w_remote(sem, block, to, src=None):
            dst = w_half(*block)
            return pltpu.make_async_remote_copy(
                src_ref=dst if src is None else src, dst_ref=dst, send_sem=w_send.at[sem],
                recv_sem=w_recv.at[sem], device_id=to, device_id_type=MESH)

        def w_first(idx):
            return w_remote(idx, (x, y, c), (*chips[idx], c), src=win_ref.at[pl.ds(c * half, half), :])

        def w_relay():
            src_chip = (jnp.bitwise_xor(x, 1 - c), jnp.bitwise_xor(y, c))
            dst_chip = (jnp.bitwise_xor(x, c), jnp.bitwise_xor(y, 1 - c))
            return w_remote(2, (*src_chip, c), (*dst_chip, c))

        def w_pass(idx):
            return w_remote(3 + idx, (*chips[idx], c), sibling)

        def w_store(k, cx, cy):
            jj = 2 * cx + cy
            return pltpu.make_async_copy(wv.at[jj], wfull_ref.at[:, pl.ds(jj * cols, cols)], w_local.at[k])

        start_rest, relay_rest, finish_rest = _gather_steps(shards, ins, outs, send_sems, recv_sems, local_sems)
        own = pltpu.make_async_copy(win_ref, wv.at[2 * x + y], w_local.at[4])

        @pl.when((s == 0) & (i == 0))
        def _():
            own.start()
            w_first(0).start()
            w_first(1).start()
            start_rest()
            own.wait()
            w_store(0, x, y).start()

        @pl.when((s == 1) & (i == 0))
        def _():
            w_remote(0, (*chips[0], c), me).wait_recv()
            w_remote(1, (*chips[1], c), me).wait_recv()
            w_relay().start()
            w_pass(0).start()
            w_pass(1).start()
            w_remote(3, (*chips[0], 1 - c), me).wait_recv()
            w_store(1, *chips[0]).start()

        @pl.when((s == 2) & (i == 0))
        def _():
            w_remote(4, (*chips[1], 1 - c), me).wait_recv()
            w_store(2, *chips[1]).start()

        @pl.when((s == 3) & (i == 0))
        def _():
            w_remote(2, (*chips[2], c), me).wait_recv()
            w_pass(2).start()
            w_remote(5, (*chips[2], 1 - c), me).wait_recv()
            w_store(3, *chips[2]).start()

        xn, _ = _rms(x_ref[...])
        h = (xn * g_ref[...]).astype(BF16)
        keep_h = pltpu.make_async_copy(h_buf, h_ref.at[pl.ds(pl.multiple_of(i * tb, tb), tb), :], w_local.at[5])

        @pl.when(s == 0)
        def _():
            h_buf[...] = h
            keep_h.start()

        z_ref[...] = _dot(h, wv[jnp.bitwise_xor(2 * x + y, s)])
        pl.when(s == 0)(keep_h.wait)

        @pl.when((s == N_CHIPS - 1) & (i == nb - 1))
        def _():
            relay_rest()
            finish_rest()
            for cp in (w_first(0), w_first(1), w_relay(), w_pass(0), w_pass(1), w_pass(2)):
                cp.wait_send()
            w_store(0, x, y).wait()
            for idx in range(3):
                w_store(idx + 1, *chips[idx]).wait()

    rest_shape, rest_sems = _gather_shapes(shards)
    out_shape = [jax.ShapeDtypeStruct((t, IN_COLS), F32), jax.ShapeDtypeStruct((t, D_MODEL), BF16),
                 jax.ShapeDtypeStruct((D_MODEL, IN_COLS), BF16)] + rest_shape
    any_spec = pl.BlockSpec(memory_space=pl.ANY)

    def z_map(s, i):
        return (i, jnp.bitwise_xor(2 * lax.axis_index("x") + lax.axis_index("y"), s))

    return pl.pallas_call(
        body, name="in_proj", out_shape=tuple(out_shape),
        grid=(N_CHIPS, nb),
        in_specs=[pl.BlockSpec((tb, D_MODEL), lambda s, i: (i, 0)),
                  pl.BlockSpec((1, D_MODEL), lambda s, i: (0, 0)), any_spec] + [any_spec] * n,
        out_specs=tuple([pl.BlockSpec((tb, cols), z_map), any_spec, any_spec] + [any_spec] * n),
        scratch_shapes=[pltpu.VMEM((N_CHIPS, D_MODEL, cols), BF16), pltpu.VMEM((tb, D_MODEL), BF16)] + rest_sems + [
            pltpu.SemaphoreType.DMA((GATHER_SEMS,)), pltpu.SemaphoreType.DMA((GATHER_SEMS,)),
            pltpu.SemaphoreType.DMA((N_CHIPS + 2,))],
        compiler_params=pltpu.CompilerParams(dimension_semantics=("arbitrary", "arbitrary"),
                                             vmem_limit_bytes=VMEM_LIMIT_BYTES),
    )(x2d, norm_g, w_in_sh, *[sh[0] for sh in shards])


def _in_proj_bwd(dz, w_in, x2d, dx_res, norm_g, tb, reduce, shards):
    t = x2d.shape[0]
    nb = t // tb
    parts, wire, steps = reduce
    n = len(parts)
    k = len(shards)

    def body(dz_ref, w_ref, x_ref, dres_ref, g_ref, *refs):
        at = 2 * n + k
        dx_ref, dg_ref = refs[at:at + 2]
        rs_outs, g_outs = refs[at + 2:at + 2 + n], refs[at + 2 + n:at + 2 + n + k]
        scratch = refs[at + 2 + n + k:]
        rs = _rs_steps(parts, refs[:2 * n], rs_outs, scratch[:len(scratch) - 3])
        for step, when in zip(rs, steps):
            pl.when(pl.program_id(0) == when)(step)
        gather = _gather_steps(shards, refs[2 * n:at], g_outs, *scratch[len(scratch) - 3:])
        for step, when in zip(gather, (0, nb // 2, nb - 1)):
            pl.when(pl.program_id(0) == when)(step)

        @pl.when(pl.program_id(0) == 0)
        def _():
            dg_ref[...] = jnp.zeros_like(dg_ref)

        xn, r = _rms(x_ref[...])
        g = g_ref[...]
        dh = _dot_nt(dz_ref[...], w_ref[...])
        dg_ref[0:1, :] += jnp.sum(dh * xn, axis=0, keepdims=True)
        dx_ref[...] = dres_ref[...] + _rms_bwd(dh * g, xn, r)

    row = lambda i: (i, 0)
    fixed = lambda i: (0, 0)
    rs_shape, rs_scratch = _rs_shapes(parts, wire)
    g_shape, g_sems = _gather_shapes(shards)
    any_spec = pl.BlockSpec(memory_space=pl.ANY)
    return pl.pallas_call(
        body, name="in_proj_bwd",
        out_shape=tuple([jax.ShapeDtypeStruct((t, D_MODEL), F32), jax.ShapeDtypeStruct((F32_SUBLANES, D_MODEL), F32)]
                        + rs_shape + g_shape),
        grid=(nb,),
        in_specs=[pl.BlockSpec((tb, IN_COLS), row),
                  pl.BlockSpec((D_MODEL, IN_COLS), fixed, pipeline_mode=pl.Buffered(1)),
                  pl.BlockSpec((tb, D_MODEL), row), pl.BlockSpec((tb, D_MODEL), row),
                  pl.BlockSpec((1, D_MODEL), fixed)] + [any_spec] * (2 * n + k),
        out_specs=tuple([pl.BlockSpec((tb, D_MODEL), row), pl.BlockSpec((F32_SUBLANES, D_MODEL), fixed)]
                        + [any_spec] * (n + k)),
        scratch_shapes=rs_scratch + g_sems,
        compiler_params=pltpu.CompilerParams(dimension_semantics=("arbitrary",),
                                             vmem_limit_bytes=VMEM_LIMIT_BYTES),
    )(dz, w_in, x2d, dx_res, norm_g, *_rs_operands(parts), *[sh[0] for sh in shards])


def _weight_grad(lhs, rhs, n_chunks, tb, name, reduce=None):
    t, k = lhs.shape
    nc = rhs.shape[1] // n_chunks
    nb = t // tb
    parts, wire, steps = reduce if reduce is not None else ([], F32, ())
    n = len(parts)

    def body(l_ref, r_ref, *refs):
        o_ref, o16_ref = refs[2 * n:2 * n + 2]
        if n:
            at = pl.program_id(0) * nb + pl.program_id(1)
            rs = _rs_steps(parts, refs[:2 * n], refs[2 * n + 2:3 * n + 2], refs[3 * n + 2:])
            for step, when in zip(rs, steps):
                pl.when(at == when)(step)

        @pl.when(pl.program_id(1) == 0)
        def _():
            o_ref[...] = jnp.zeros_like(o_ref)

        o_ref[...] += _dot_tn(l_ref[...], r_ref[...])

        @pl.when(pl.program_id(1) == nb - 1)
        def _():
            o16_ref[...] = o_ref[...].astype(BF16)

    rs_shape, rs_scratch = _rs_shapes(parts, wire) if n else ([], [])
    any_spec = pl.BlockSpec(memory_space=pl.ANY)
    chunk = pl.BlockSpec((None, k, nc), lambda j, i: (j, 0, 0))
    return pl.pallas_call(
        body, name=name,
        out_shape=tuple([jax.ShapeDtypeStruct((n_chunks, k, nc), F32), jax.ShapeDtypeStruct((n_chunks, k, nc), BF16)]
                        + rs_shape),
        grid=(n_chunks, nb),
        in_specs=[pl.BlockSpec((tb, k), lambda j, i: (i, 0)), pl.BlockSpec((tb, nc), lambda j, i: (i, j))]
        + [any_spec] * (2 * n),
        out_specs=tuple([chunk, chunk] + [any_spec] * n),
        scratch_shapes=rs_scratch,
        compiler_params=pltpu.CompilerParams(dimension_semantics=("arbitrary", "arbitrary"),
                                             vmem_limit_bytes=VMEM_LIMIT_BYTES),
    )(lhs, rhs, *_rs_operands(parts))


def _adam_update(w, g, m, v):
    m_ = ADAM_B1 * m + (1.0 - ADAM_B1) * g
    v_ = ADAM_B2 * v + (1.0 - ADAM_B2) * jnp.square(g)
    m_hat = m_ / (1.0 - ADAM_B1 ** ADAM_STEP)
    v_hat = v_ / (1.0 - ADAM_B2 ** ADAM_STEP)
    return -ADAM_LR * (m_hat / (jnp.sqrt(v_hat) + ADAM_EPS) + ADAM_WD * w), m_, v_


def _adamw_replicated(vec_sum, mat_sum, norm_grad, entries, conv):
    n = len(entries)

    def grad_of(name, shape, vec_ref, mat_ref, norm_ref):
        if name == "norm_g":
            return norm_ref[0:1, :]
        if name in MAT_BAG_AT:
            return mat_ref[MAT_BAG_AT[name]:MAT_BAG_AT[name] + shape[0], :]
        if shape[0] == 1:
            return vec_ref[_bag_row(name), 0:shape[1]]
        return jnp.concatenate([vec_ref[_bag_row(name), h * shape[1]:(h + 1) * shape[1]] for h in range(shape[0])],
                               axis=0)

    def body(vec_ref, mat_ref, norm_ref, *refs):
        ins, outs = refs[:3 * n + 4], refs[3 * n + 4:]
        for k in range(n):
            w_ref, m_ref, v_ref = ins[3 * k:3 * k + 3]
            g = grad_of(entries[k][0], w_ref.shape, vec_ref, mat_ref, norm_ref)
            d, m_, v_ = _adam_update(w_ref[...], g, m_ref[...], v_ref[...])
            for ref, val in zip(outs[4 * k:4 * k + 4], (g, d, m_, v_)):
                ref[...] = val
        w_ref, m_ref, v_ref, g_ref = ins[3 * n:]
        for ref, val in zip(outs[4 * n:], _adam_update(w_ref[...], g_ref[...], m_ref[...], v_ref[...])):
            ref[...] = val

    arrays = [a for e in entries for a in e[1:]] + list(conv)
    out_shape = [jax.ShapeDtypeStruct(e[1].shape, F32) for e in entries for _ in range(4)]
    out_shape += [jax.ShapeDtypeStruct(conv[0].shape, F32)] * 3
    return pl.pallas_call(
        body, name="adamw_replicated", out_shape=tuple(out_shape),
        compiler_params=pltpu.CompilerParams(vmem_limit_bytes=VMEM_LIMIT_BYTES),
    )(vec_sum, mat_sum, norm_grad, *arrays)


def _adamw(w, g, m, v, rows, name):
    r, c = w.shape

    def body(w_ref, g_ref, m_ref, v_ref, d_ref, nm_ref, nv_ref):
        d_ref[...], nm_ref[...], nv_ref[...] = _adam_update(w_ref[...], g_ref[...], m_ref[...], v_ref[...])

    spec = pl.BlockSpec((rows, c), lambda i: (i, 0))
    return pl.pallas_call(
        body, name=name, out_shape=tuple(jax.ShapeDtypeStruct((r, c), F32) for _ in range(3)),
        grid=(r // rows,), in_specs=[spec] * 4, out_specs=(spec,) * 3,
        compiler_params=pltpu.CompilerParams(dimension_semantics=("arbitrary",),
                                             vmem_limit_bytes=VMEM_LIMIT_BYTES),
    )(w, g, m, v)


def _shift_down(ext, s):
    return pltpu.roll(ext, s, 0)


def _tile_shift(v, s):
    rows, cols = v.shape
    tiles = v.reshape(rows // F32_SUBLANES, F32_SUBLANES, cols)
    return pltpu.roll(tiles, s % F32_SUBLANES, 1).reshape(rows, cols)


def _shift_up(ext, s):
    return pltpu.roll(ext, ext.shape[0] - s, 0)


def _lru_gates(xc, wa_ref, ba, wx_ref, bx, lam):
    pa, px = [], []
    for h in range(LRU_HEADS):
        xh = xc[:, h * HEAD_DIM:(h + 1) * HEAD_DIM].astype(BF16)
        pa.append(_dot(xh, wa_ref[h]))
        px.append(_dot(xh, wx_ref[h]))
    r = jax.nn.sigmoid(jnp.concatenate(pa, axis=1) + ba)
    ig = _sigmoid(jnp.concatenate(px, axis=1) + bx)
    sp = _softplus(-lam)
    log_a = (-LRU_C * r) * sp
    a = jnp.exp(log_a)
    mult = jnp.sqrt(jnp.tanh(-log_a) * (1.0 + a * a))
    return r, ig, a, mult, sp


def _conv(ext, w_ref, b):
    y = b + _shift_down(ext, 3) * w_ref[0:1, :]
    y = y + _shift_down(ext, 2) * w_ref[1:2, :]
    y = y + _shift_down(ext, 1) * w_ref[2:3, :]
    y = y + ext * w_ref[3:4, :]
    return y[CONV_HIST:, :]


def _pool_diff(ext, pos):
    out = []
    for g, k in enumerate(POOL_WINDOWS):
        col = ext[:, g * POOL_GROUP_DIM:(g + 1) * POOL_GROUP_DIM]
        s = col
        for step in range(g + 1):
            s = s + _shift_down(s, 2 ** step)
        count = jnp.minimum(pos + 1, k).astype(F32)
        out.append(s[POOL_HIST:, :] / count - col[POOL_HIST:, :])
    return out


def _pool_mix(diff, pw_ref):
    return jnp.concatenate([_dot(diff[g].astype(BF16), pw_ref[g]) for g in range(len(POOL_WINDOWS))], axis=1)


def _branch_specs(tb, row_map, fixed):
    fixed3 = lambda i: (0, 0, 0)
    return [pl.BlockSpec((CONV_WIDTH, D_MODEL), fixed), pl.BlockSpec((1, D_MODEL), fixed),
            pl.BlockSpec((LRU_HEADS, HEAD_DIM, HEAD_DIM), fixed3), pl.BlockSpec((1, D_MODEL), fixed),
            pl.BlockSpec((LRU_HEADS, HEAD_DIM, HEAD_DIM), fixed3), pl.BlockSpec((1, D_MODEL), fixed),
            pl.BlockSpec((1, D_MODEL), fixed),
            pl.BlockSpec((len(POOL_WINDOWS), POOL_GROUP_DIM, POOL_GROUP_DIM), fixed3),
            pl.BlockSpec((1, POOL_WIDTH), fixed)]


def _branches_fwd(z, weights, seq, tb, shards):
    t = z.shape[0]
    nb = t // tb
    nbe = seq // tb
    groups = tb // F32_SUBLANES
    n = len(shards)

    def body(xa_ref, ga_ref, xb_ref, gb_ref, cw_ref, cb_ref, wa_ref, ba_ref, wx_ref, bx_ref, lam_ref,
             pw_ref, ps_ref, *refs):
        g_ins = refs[:n]
        ya_ref, yb_ref, hl_ref = refs[n:n + 3]
        g_outs = refs[n + 3:2 * n + 3]
        xa_ext, xb_ext, carry, a_s, u_s, send_sems, recv_sems, local_sems = refs[2 * n + 3:]
        blk = pl.program_id(0) % nbe
        start_gather, relay_gather, finish_gather = _gather_steps(shards, g_ins, g_outs, send_sems, recv_sems,
                                                                  local_sems)
        pl.when(pl.program_id(0) == 0)(start_gather)
        pl.when(pl.program_id(0) == nb // 2)(relay_gather)

        @pl.when(blk == 0)
        def _():
            xa_ext[0:CONV_HIST, :] = jnp.zeros((CONV_HIST, D_MODEL), F32)
            xb_ext[0:POOL_HIST, :] = jnp.zeros((POOL_HIST, POOL_WIDTH), F32)
            carry[...] = jnp.zeros_like(carry)

        xa_ext[CONV_HIST:, :] = xa_ref[...]
        xb_ext[POOL_HIST:, :] = xb_ref[...]
        ea = xa_ext[...]
        eb = xb_ext[...]
        xa_ext[0:CONV_HIST, :] = ea[tb:, :]
        xb_ext[0:POOL_HIST, :] = eb[tb:, :]

        xc = _conv(ea, cw_ref, cb_ref[...])
        _, ig, a, mult, _ = _lru_gates(xc, wa_ref, ba_ref[...], wx_ref, bx_ref[...], lam_ref[...])
        u = mult * (ig * xc)
        row8 = lax.broadcasted_iota(jnp.int32, (tb, D_MODEL), 0) % F32_SUBLANES
        for s in (1, 2, 4):
            m = row8 >= s
            u = jnp.where(m, a * _tile_shift(u, s) + u, u)
            a = jnp.where(m, a * _tile_shift(a, s), a)
        a_s[...] = a
        u_s[...] = u

        def step(g, cr):
            sl = pl.ds(pl.multiple_of(g * F32_SUBLANES, F32_SUBLANES), F32_SUBLANES)
            hb = a_s[sl, :] * cr + u_s[sl, :]
            hl_ref[sl, :] = hb
            return jnp.broadcast_to(hb[F32_SUBLANES - 1:F32_SUBLANES, :], (F32_SUBLANES, D_MODEL))

        carry[...] = lax.fori_loop(0, groups, step, carry[...], unroll=4)
        ga = ga_ref[...]
        ya_ref[...] = (hl_ref[...] * (ga * _sigmoid(ga))).astype(BF16)

        pos = blk * tb + lax.broadcasted_iota(jnp.int32, (tb, POOL_GROUP_DIM), 0)
        ypre = _pool_mix(_pool_diff(eb, pos), pw_ref)
        gb = gb_ref[...]
        yb_ref[...] = ((ypre * ps_ref[...]) * (gb * _sigmoid(gb))).astype(BF16)
        pl.when(pl.program_id(0) == nb - 1)(finish_gather)

    row = lambda i: (i, 0)
    fixed = lambda i: (0, 0)
    any_spec = pl.BlockSpec(memory_space=pl.ANY)
    in_specs = [pl.BlockSpec((tb, D_MODEL), lambda i: (i, 0)), pl.BlockSpec((tb, D_MODEL), lambda i: (i, 1)),
                pl.BlockSpec((tb, POOL_WIDTH), lambda i: (i, 4)), pl.BlockSpec((tb, POOL_WIDTH), lambda i: (i, 5)),
                ] + _branch_specs(tb, row, fixed) + [any_spec] * n
    g_shape, g_sems = _gather_shapes(shards)
    return pl.pallas_call(
        body, name="branches_fwd",
        out_shape=tuple([jax.ShapeDtypeStruct((t, D_MODEL), BF16), jax.ShapeDtypeStruct((t, POOL_WIDTH), BF16),
                         jax.ShapeDtypeStruct((t, D_MODEL), F32)] + g_shape),
        grid=(nb,), in_specs=in_specs,
        out_specs=tuple([pl.BlockSpec((tb, D_MODEL), row), pl.BlockSpec((tb, POOL_WIDTH), row),
                         pl.BlockSpec((tb, D_MODEL), row)] + [any_spec] * n),
        scratch_shapes=[pltpu.VMEM((tb + CONV_HIST, D_MODEL), F32), pltpu.VMEM((tb + POOL_HIST, POOL_WIDTH), F32),
                        pltpu.VMEM((F32_SUBLANES, D_MODEL), F32),
                        pltpu.VMEM((tb, D_MODEL), F32), pltpu.VMEM((tb, D_MODEL), F32)] + g_sems,
        compiler_params=pltpu.CompilerParams(dimension_semantics=("arbitrary",),
                                             vmem_limit_bytes=VMEM_LIMIT_BYTES),
    )(z, z, z, z, *weights, *[sh[0] for sh in shards])


def _branches_bwd(z, hl, dya, dyb, dzm, weights, vec_bag, seq, tb):
    t = z.shape[0]
    nb = t // tb
    nbe = seq // tb
    groups = tb // F32_SUBLANES

    def body(xa_ref, xap_ref, ga_ref, xb_ref, xbp_ref, gb_ref, hl_ref, hlp_ref, dya_ref, dyb_ref, dzm_ref,
             cw_ref, cb_ref, wa_ref, ba_ref, wx_ref, bx_ref, lam_ref, pw_ref, ps_ref, vec_in_ref,
             dz_ref, vec_ref, mat_ref,
             xa_ext, xb_ext, hl_ext, a_ext, dxc_ext, dwin_ext, g_carry, b_s, d_s, g_s):
        i = pl.program_id(0)
        blk = (nb - 1 - i) % nbe

        def mat_rows(name, k):
            at = MAT_BAG_AT[name] + k * HEAD_DIM
            return slice(at, at + HEAD_DIM)

        @pl.when(i == 0)
        def _():
            vec_ref[...] = vec_in_ref[...]
            mat_ref[...] = jnp.zeros_like(mat_ref)

        @pl.when(blk == nbe - 1)
        def _():
            a_ext[tb:, :] = jnp.zeros((F32_SUBLANES, D_MODEL), F32)
            dxc_ext[tb:, :] = jnp.zeros((CONV_HIST, D_MODEL), F32)
            dwin_ext[tb:, :] = jnp.zeros((POOL_HIST, POOL_WIDTH), F32)
            g_carry[...] = jnp.zeros_like(g_carry)

        live = (blk > 0).astype(F32)
        xa_ext[0:CONV_HIST, :] = xap_ref[...] * live
        xa_ext[CONV_HIST:, :] = xa_ref[...]
        xb_ext[0:POOL_HIST, :] = xbp_ref[...] * live
        xb_ext[POOL_HIST:, :] = xb_ref[...]
        hl_ext[0:F32_SUBLANES, :] = hlp_ref[...] * live
        hl_ext[F32_SUBLANES:, :] = hl_ref[...]
        ea = xa_ext[...]
        eb = xb_ext[...]

        xc = _conv(ea, cw_ref, cb_ref[...])
        lam = lam_ref[...]
        r, ig, a, mult, sp = _lru_gates(xc, wa_ref, ba_ref[...], wx_ref, bx_ref[...], lam)
        hl = hl_ref[...]
        ga = ga_ref[...]
        sga = _sigmoid(ga)
        dya = dya_ref[...]
        dhl = dya * (ga * sga)
        dz_ref[:, D_MODEL:2 * D_MODEL] = (dya * hl * (sga * (1.0 + ga * (1.0 - sga)))).astype(BF16)

        a_ext[0:tb, :] = a
        b = _shift_up(a_ext[...], 1)[0:tb, :]
        a_ext[tb:, :] = jnp.broadcast_to(a[0:1, :], (F32_SUBLANES, D_MODEL))
        d = dhl
        row8 = lax.broadcasted_iota(jnp.int32, (tb, D_MODEL), 0) % F32_SUBLANES
        for s in (1, 2, 4):
            m = row8 < F32_SUBLANES - s
            d = jnp.where(m, d + b * _tile_shift(d, -s), d)
            b = jnp.where(m, b * _tile_shift(b, -s), b)
        b_s[...] = b
        d_s[...] = d

        def step(k, cr):
            sl = pl.ds(pl.multiple_of((groups - 1 - k) * F32_SUBLANES, F32_SUBLANES), F32_SUBLANES)
            gb_ = d_s[sl, :] + b_s[sl, :] * cr
            g_s[sl, :] = gb_
            return jnp.broadcast_to(gb_[0:1, :], (F32_SUBLANES, D_MODEL))

        g_carry[...] = lax.fori_loop(0, groups, step, g_carry[...], unroll=4)
        gsc = g_s[...]
        da = gsc * _shift_down(hl_ext[...], 1)[F32_SUBLANES:, :]
        dmult = gsc * (ig * xc)
        dig = gsc * (mult * xc)
        dxc = gsc * (mult * ig)
        dlog_a = da * a - (a * a) * dmult / mult
        dr = dlog_a * (-LRU_C * sp)
        vec_ref[_bag_row("lru_lambda"), :] += jnp.sum(dlog_a * (-LRU_C * r), axis=0, keepdims=True)
        dpa = dr * (r * (1.0 - r))
        dpx = dig * (ig * (1.0 - ig))
        vec_ref[_bag_row("lru_b_a"), :] += jnp.sum(dpa, axis=0, keepdims=True)
        vec_ref[_bag_row("lru_b_x"), :] += jnp.sum(dpx, axis=0, keepdims=True)
        back = []
        for h in range(LRU_HEADS):
            cols = slice(h * HEAD_DIM, (h + 1) * HEAD_DIM)
            xh = xc[:, cols].astype(BF16)
            dpa_h = dpa[:, cols].astype(BF16)
            dpx_h = dpx[:, cols].astype(BF16)
            mat_ref[mat_rows("lru_w_a", h), :] += _dot_tn(xh, dpa_h)
            mat_ref[mat_rows("lru_w_x", h), :] += _dot_tn(xh, dpx_h)
            back.append(_dot_nt(dpa_h, wa_ref[h]) + _dot_nt(dpx_h, wx_ref[h]))
        dxc = dxc + jnp.concatenate(back, axis=1)
        vec_ref[_bag_row("conv_b"), :] += jnp.sum(dxc, axis=0, keepdims=True)
        for k in range(CONV_WIDTH):
            tap = _shift_down(ea, CONV_WIDTH - 1 - k)[CONV_HIST:, :] if k < CONV_WIDTH - 1 else ea[CONV_HIST:, :]
            vec_ref[_bag_row("conv_w", k), :] += jnp.sum(dxc * tap, axis=0, keepdims=True)
        dxc_ext[0:tb, :] = dxc
        ed = dxc_ext[...]
        dxa = ed * cw_ref[3:4, :]
        dxa = dxa + _shift_up(ed, 1) * cw_ref[2:3, :]
        dxa = dxa + _shift_up(ed, 2) * cw_ref[1:2, :]
        dxa = dxa + _shift_up(ed, 3) * cw_ref[0:1, :]
        dz_ref[:, 0:D_MODEL] = dxa[0:tb, :].astype(BF16)
        dxc_ext[tb:, :] = dxc[0:CONV_HIST, :]

        pos = blk * tb + lax.broadcasted_iota(jnp.int32, (tb, POOL_GROUP_DIM), 0)
        diff = _pool_diff(eb, pos)
        ypre = _pool_mix(diff, pw_ref)
        ps = ps_ref[...]
        gb = gb_ref[...]
        sgb = _sigmoid(gb)
        dyb = dyb_ref[...]
        dyp = dyb * (gb * sgb)
        dz_ref[:, 2 * D_MODEL + POOL_WIDTH:3 * D_MODEL] = (
            dyb * (ypre * ps) * (sgb * (1.0 + gb * (1.0 - sgb)))).astype(BF16)
        vec_ref[_bag_row("pool_scale"), 0:POOL_WIDTH] += jnp.sum(dyp * ypre, axis=0, keepdims=True)
        dypre = dyp * ps
        for g, k in enumerate(POOL_WINDOWS):
            cols = slice(g * POOL_GROUP_DIM, (g + 1) * POOL_GROUP_DIM)
            dyg = dypre[:, cols].astype(BF16)
            mat_ref[mat_rows("pool_w", g), :] += _dot_tn(diff[g].astype(BF16), dyg)
            ddiff = _dot_nt(dyg, pw_ref[g])
            count = jnp.minimum(pos + 1, k).astype(F32)
            dwin = ddiff / count
            dwin_ext[0:tb, cols] = dwin
            s = dwin_ext[:, cols]
            for step_ in range(g + 1):
                s = s + _shift_up(s, 2 ** step_)
            dz_ref[:, 2 * D_MODEL + g * POOL_GROUP_DIM:2 * D_MODEL + (g + 1) * POOL_GROUP_DIM] = (
                s[0:tb, :] - ddiff).astype(BF16)
            dwin_ext[tb:, cols] = dwin[0:POOL_HIST, :]

        dz_ref[:, 3 * D_MODEL:] = dzm_ref[...]

        @pl.when(i == nb - 1)
        def _():
            row = _bag_row("lru_lambda")
            vec_ref[row, :] = vec_ref[row, :] * (-_sigmoid(-lam))

    rev = lambda i: (nb - 1 - i, 0)
    fixed = lambda i: (0, 0)

    def prev(rows, col):
        per = tb // rows
        return lambda i: (jnp.maximum((nb - 1 - i) * per - 1, 0), col)

    in_specs = [pl.BlockSpec((tb, D_MODEL), lambda i: (nb - 1 - i, 0)),
                pl.BlockSpec((CONV_HIST, D_MODEL), prev(CONV_HIST, 0)),
                pl.BlockSpec((tb, D_MODEL), lambda i: (nb - 1 - i, 1)),
                pl.BlockSpec((tb, POOL_WIDTH), lambda i: (nb - 1 - i, 4)),
                pl.BlockSpec((POOL_HIST, POOL_WIDTH), prev(POOL_HIST, 4)),
                pl.BlockSpec((tb, POOL_WIDTH), lambda i: (nb - 1 - i, 5)),
                pl.BlockSpec((tb, D_MODEL), rev),
                pl.BlockSpec((F32_SUBLANES, D_MODEL), prev(F32_SUBLANES, 0)),
                pl.BlockSpec((tb, D_MODEL), rev), pl.BlockSpec((tb, POOL_WIDTH), rev),
                pl.BlockSpec((tb, 2 * D_MODEL), rev)] + _branch_specs(tb, rev, fixed) + [
                    pl.BlockSpec((VEC_BAG_ROWS, D_MODEL), fixed)]
    out_shape = (jax.ShapeDtypeStruct((t, IN_COLS), BF16), jax.ShapeDtypeStruct((VEC_BAG_ROWS, D_MODEL), F32),
                 jax.ShapeDtypeStruct((MAT_BAG_ROWS, HEAD_DIM), F32))
    out_specs = (pl.BlockSpec((tb, IN_COLS), rev), pl.BlockSpec((VEC_BAG_ROWS, D_MODEL), fixed),
                 pl.BlockSpec((MAT_BAG_ROWS, HEAD_DIM), fixed))
    scratch = [pltpu.VMEM((tb + CONV_HIST, D_MODEL), F32), pltpu.VMEM((tb + POOL_HIST, POOL_WIDTH), F32),
               pltpu.VMEM((tb + F32_SUBLANES, D_MODEL), F32), pltpu.VMEM((tb + F32_SUBLANES, D_MODEL), F32),
               pltpu.VMEM((tb + CONV_HIST, D_MODEL), F32), pltpu.VMEM((tb + POOL_HIST, POOL_WIDTH), F32),
               pltpu.VMEM((F32_SUBLANES, D_MODEL), F32),
               pltpu.VMEM((tb, D_MODEL), F32), pltpu.VMEM((tb, D_MODEL), F32), pltpu.VMEM((tb, D_MODEL), F32)]
    return pl.pallas_call(
        body, name="branches_bwd", out_shape=out_shape, grid=(nb,), in_specs=in_specs, out_specs=out_specs,
        scratch_shapes=scratch, input_output_aliases={len(in_specs) - 1: 1},
        compiler_params=pltpu.CompilerParams(dimension_semantics=("arbitrary",),
                                             vmem_limit_bytes=VMEM_LIMIT_BYTES),
    )(z, z, z, z, z, z, hl, hl, dya, dyb, dzm, *weights, vec_bag)


def _merge_head(x2d, ya, yb, z, p2d, tgt, w_pl, w_pp, w_out, w_pg, w_pe, g2, gf, tb):
    t = x2d.shape[0]
    p_dim = p2d.shape[1]

    def body(x_ref, ya_ref, yb_ref, ma_ref, mb_ref, p_ref, t_ref, wpl_ref, wpp_ref, wout_ref, wpg_ref, wpe_ref,
             g2_ref, gf_ref,
             bag_ref, dxr_ref, dya_ref, dyb_ref, dzm_ref,
             mg_ref, do_ref, hn_ref, dgp_ref, dpe_ref, da_ref, dbm_ref, pbf_ref):
        @pl.when(pl.program_id(0) == 0)
        def _():
            bag_ref[...] = jnp.zeros_like(bag_ref)

        a_ = _dot(ya_ref[...], wpl_ref[...])
        bm = _dot(yb_ref[...], wpp_ref[...])
        sa = _sigmoid(ma_ref[...])
        sb = _sigmoid(mb_ref[...])
        mg = (sa * a_ + sb * bm).astype(BF16)
        mg_ref[...] = mg
        x1 = x_ref[...] + _dot(mg, wout_ref[...])
        xn2, r2 = _rms(x1)
        g2 = g2_ref[...]
        hn = (xn2 * g2).astype(BF16)
        hn_ref[...] = hn
        gate = _sigmoid(_dot(hn, wpg_ref[...]))
        pbf = p_ref[...].astype(BF16)
        pbf_ref[...] = pbf
        pe = _dot(pbf, wpe_ref[...])
        x2 = x1 + gate * pe
        xn3, r3 = _rms(x2)
        gf = gf_ref[...]
        err = xn3 * gf - t_ref[...]
        bag_ref[_bag_rows("loss"), 0:128] += 0.5 * jnp.sum(jnp.mean(err * err, axis=-1))

        dy = err * (1.0 / D_MODEL)
        bag_ref[_bag_row("final_g"), :] += jnp.sum(dy * xn3, axis=0, keepdims=True)
        dx2 = _rms_bwd(dy * gf, xn3, r3)
        dpe_ref[...] = (dx2 * gate).astype(BF16)
        dgp = ((dx2 * pe) * (gate * (1.0 - gate))).astype(BF16)
        dgp_ref[...] = dgp
        dhn = _dot_nt(dgp, wpg_ref[...])
        bag_ref[_bag_row("ple_norm_g"), :] += jnp.sum(dhn * xn2, axis=0, keepdims=True)
        dx1 = dx2 + _rms_bwd(dhn * g2, xn2, r2)
        dxr_ref[...] = dx1
        do = dx1.astype(BF16)
        do_ref[...] = do
        dmg = _dot_nt(do, wout_ref[...])
        da = (dmg * sa).astype(BF16)
        dbm = (dmg * sb).astype(BF16)
        da_ref[...] = da
        dbm_ref[...] = dbm
        dzm_ref[:, 0:D_MODEL] = (dmg * a_ * (sa * (1.0 - sa))).astype(BF16)
        dzm_ref[:, D_MODEL:] = (dmg * bm * (sb * (1.0 - sb))).astype(BF16)
        dya_ref[...] = _dot_nt(da, wpl_ref[...])
        dyb_ref[...] = _dot_nt(dbm, wpp_ref[...])

    row = lambda i: (i, 0)
    fixed = lambda i: (0, 0)

    def resident(shape):
        return pl.BlockSpec(shape, fixed, pipeline_mode=pl.Buffered(1))

    tok = lambda width: pl.BlockSpec((tb, width), row)
    in_specs = [tok(D_MODEL), tok(D_MODEL), tok(POOL_WIDTH),
                pl.BlockSpec((tb, D_MODEL), lambda i: (i, 3)), pl.BlockSpec((tb, D_MODEL), lambda i: (i, 4)),
                tok(p_dim), tok(D_MODEL),
                resident((D_MODEL, D_MODEL)), resident((POOL_WIDTH, D_MODEL)), resident((D_MODEL, D_MODEL)),
                resident((D_MODEL, D_MODEL)), resident((p_dim, D_MODEL)),
                pl.BlockSpec((1, D_MODEL), fixed), pl.BlockSpec((1, D_MODEL), fixed)]
    bf = lambda width: jax.ShapeDtypeStruct((t, width), BF16)
    f32 = lambda width: jax.ShapeDtypeStruct((t, width), F32)
    out_shape = (jax.ShapeDtypeStruct((VEC_BAG_ROWS, D_MODEL), F32),
                 f32(D_MODEL), f32(D_MODEL), f32(POOL_WIDTH), bf(2 * D_MODEL),
                 bf(D_MODEL), bf(D_MODEL), bf(D_MODEL), bf(D_MODEL), bf(D_MODEL), bf(D_MODEL), bf(D_MODEL), bf(p_dim))
    out_specs = (pl.BlockSpec((VEC_BAG_ROWS, D_MODEL), fixed),
                 tok(D_MODEL), tok(D_MODEL), tok(POOL_WIDTH), tok(2 * D_MODEL),
                 tok(D_MODEL), tok(D_MODEL), tok(D_MODEL), tok(D_MODEL), tok(D_MODEL), tok(D_MODEL), tok(D_MODEL),
                 tok(p_dim))
    return pl.pallas_call(
        body, name="merge_head", out_shape=out_shape, grid=(t // tb,), in_specs=in_specs, out_specs=out_specs,
        compiler_params=pltpu.CompilerParams(dimension_semantics=("arbitrary",),
                                             vmem_limit_bytes=VMEM_LIMIT_BYTES),
    )(x2d, ya, yb, z, z, p2d, tgt, w_pl, w_pp, w_out, w_pg, w_pe, g2, gf)


def kernel(x, p, norm_g, w_in, conv_w, conv_b, lru_w_a, lru_b_a, lru_w_x, lru_b_x, lru_lambda, pool_w, pool_scale, w_proj_lru, w_proj_pool, w_out, ple_norm_g, w_ple_gate, w_ple_proj, final_g, loss_target, m_norm_g, m_w_in, m_conv_w, m_conv_b, m_lru_w_a, m_lru_b_a, m_lru_w_x, m_lru_b_x, m_lru_lambda, m_pool_w, m_pool_scale, m_w_proj_lru, m_w_proj_pool, m_w_out, m_ple_norm_g, m_w_ple_gate, m_w_ple_proj, m_final_g, v_norm_g, v_w_in, v_conv_w, v_conv_b, v_lru_w_a, v_lru_b_a, v_lru_w_x, v_lru_b_x, v_lru_lambda, v_pool_w, v_pool_scale, v_w_proj_lru, v_w_proj_pool, v_w_out, v_ple_norm_g, v_w_ple_gate, v_w_ple_proj, v_final_g):
    bsz, seq, _ = x.shape
    t = bsz * seq
    tb_mm = min(1024, seq)
    tb_seq = min(256, seq // 2) if seq >= 512 else seq
    x2d = x.reshape(t, D_MODEL)
    p2d = p.reshape(t, p.shape[-1])
    tgt = loss_target.reshape(t, D_MODEL)
    chip = 2 * lax.axis_index("x") + lax.axis_index("y")

    rest = [(w_proj_lru[0], 0), (w_proj_pool[0], 1), (w_out[0], 0), (w_ple_gate[0], 0), (w_ple_proj[0], 1)]
    z, h_bf, w_in_f, conv_w_f = _in_proj_gather(x2d, norm_g, w_in[0].astype(BF16), [(conv_w[0], 1, False)], tb_mm)

    wa_bf = lru_w_a[0].astype(BF16)
    wx_bf = lru_w_x[0].astype(BF16)
    pw_bf = pool_w[0].astype(BF16)
    branch_w = (conv_w_f, conv_b, wa_bf, lru_b_a.reshape(1, D_MODEL), wx_bf, lru_b_x.reshape(1, D_MODEL),
                lru_lambda, pw_bf, pool_scale)

    ya, yb, hl, w_pl_f, w_pp_f, w_out_f, w_pg_f, w_pe_f = _branches_fwd(
        z, branch_w, seq, tb_seq, [(w.astype(BF16), axis, True) for w, axis in rest])
    (vec_bag, dx_res, dya, dyb, dzm, mg_bf, do_bf, hn_bf, dgp_bf, dpe_bf, da_bf, dbm_bf, p_bf) = _merge_head(
        x2d, ya, yb, z, p2d, tgt, w_pl_f, w_pp_f, w_out_f, w_pg_f, w_pe_f, ple_norm_g, final_g.reshape(1, D_MODEL),
        tb_seq)
    dz, vec_bag, mat_bag = _branches_bwd(z, hl, dya, dyb, dzm, branch_w, vec_bag, seq, tb_seq)

    tb_dw = min(1024, seq)
    def proj_grad(lhs, rhs, name, cols):
        g32, g16 = _weight_grad(lhs, rhs, 1, tb_dw, name)
        if cols:
            return g32[0], True, g16[0]
        rows = g32.shape[1] // 8
        return g32.reshape(8, rows, g32.shape[2]), False, g16.reshape(8, rows, g32.shape[2])

    p_dim = p2d.shape[1]
    proj_parts = [proj_grad(ya, da_bf, "dw_proj_lru", False), proj_grad(yb, dbm_bf, "dw_proj_pool", True),
                  proj_grad(mg_bf, do_bf, "dw_out", False), proj_grad(hn_bf, dgp_bf, "dw_ple_gate", False),
                  proj_grad(p_bf, dpe_bf, "dw_ple_proj", True)]
    nb_dw = t // tb_dw
    g_in, g_in16, r_pl, r_pp, r_out, r_pg, r_pe, vec_mine, mat_mine = _weight_grad(
        h_bf, dz, N_CHIPS, tb_dw, "dw_in",
        reduce=(proj_parts + [(vec_bag.reshape(8, VEC_BAG_ROWS // 8, D_MODEL), False, None),
                              (mat_bag.reshape(8, MAT_BAG_ROWS // 8, HEAD_DIM), False, None)],
                [BF16] * 5 + [F32] * 2,
                (0, nb_dw // 2, 2 * nb_dw - 1, 3 * nb_dw + nb_dw // 2, N_CHIPS * nb_dw - 1)))
    pieces = (8, D_MODEL // 2, IN_COLS // N_CHIPS)
    nb_seq = t // tb_seq
    dx, d_g1, r_in, vec_sum, mat_sum = _in_proj_bwd(
        dz, w_in_f, x2d, dx_res, norm_g, tb_seq,
        reduce=([(g_in.reshape(pieces), False, g_in16.reshape(pieces))], BF16,
                (0, nb_seq // 8, nb_seq // 2, nb_seq - 1, nb_seq - 1)),
        shards=[(vec_mine.reshape(VEC_BAG_ROWS // N_CHIPS, D_MODEL), 0, True),
                (mat_mine.reshape(MAT_BAG_ROWS // N_CHIPS, HEAD_DIM), 0, True)])
    g_g1 = _all_reduce_tile(d_g1, "allreduce_norm_g")

    def big_update(w, g2d, m, v, rows, name):
        d, nm, nv = _adamw(w[0], g2d, m[0], v[0], rows, name)
        return g2d[None], d[None], nm[None], nv[None]

    u_in = big_update(w_in, r_in.reshape(D_MODEL, IN_COLS // N_CHIPS), m_w_in, v_w_in, 256, "adamw_w_in")
    u_pl = big_update(w_proj_lru, r_pl.reshape(D_MODEL // N_CHIPS, D_MODEL), m_w_proj_lru, v_w_proj_lru, 256, "adamw_w_proj_lru")
    u_pp = big_update(w_proj_pool, r_pp.reshape(POOL_WIDTH, D_MODEL // N_CHIPS), m_w_proj_pool, v_w_proj_pool, 512, "adamw_w_proj_pool")
    u_out = big_update(w_out, r_out.reshape(D_MODEL // N_CHIPS, D_MODEL), m_w_out, v_w_out, 256, "adamw_w_out")
    u_pg = big_update(w_ple_gate, r_pg.reshape(D_MODEL // N_CHIPS, D_MODEL), m_w_ple_gate, v_w_ple_gate, 256, "adamw_w_ple_gate")
    u_pe = big_update(w_ple_proj, r_pe.reshape(p_dim, D_MODEL // N_CHIPS), m_w_ple_proj, v_w_ple_proj, 256, "adamw_w_ple_proj")

    small = [("norm_g", norm_g, m_norm_g, v_norm_g), ("conv_b", conv_b, m_conv_b, v_conv_b),
             ("lru_w_a", lru_w_a, m_lru_w_a, v_lru_w_a), ("lru_b_a", lru_b_a, m_lru_b_a, v_lru_b_a),
             ("lru_w_x", lru_w_x, m_lru_w_x, v_lru_w_x), ("lru_b_x", lru_b_x, m_lru_b_x, v_lru_b_x),
             ("lru_lambda", lru_lambda, m_lru_lambda, v_lru_lambda), ("pool_w", pool_w, m_pool_w, v_pool_w),
             ("pool_scale", pool_scale, m_pool_scale, v_pool_scale),
             ("ple_norm_g", ple_norm_g, m_ple_norm_g, v_ple_norm_g), ("final_g", final_g, m_final_g, v_final_g)]

    def view(a):
        return a.reshape(-1, a.shape[-1]) if a.ndim != 3 else a[0]

    cw_at = F32_SUBLANES * VEC_BAG_SLOTS.index("conv_w")
    cw_cols = D_MODEL // N_CHIPS
    g_cw = lax.dynamic_slice(vec_sum, (cw_at, chip * cw_cols), (CONV_WIDTH, cw_cols))
    flat = _adamw_replicated(vec_sum, mat_sum, g_g1, [(name,) + tuple(view(a) for a in arrs) for name, *arrs in small],
                             (conv_w[0], m_conv_w[0], v_conv_w[0], g_cw))
    u_small = {name: tuple(flat[4 * k + pick].reshape(arrs[0].shape) for pick in range(4))
               for k, (name, *arrs) in enumerate(small)}
    u_cw = tuple(a[None] for a in (g_cw,) + tuple(flat[4 * len(small):]))

    loss = vec_sum[F32_SUBLANES * VEC_BAG_SLOTS.index("loss"), 0]
    grad_x = dx.reshape(bsz, seq, D_MODEL)

    def ordered(pick):
        s = {name: u[pick] for name, u in u_small.items()}
        return [s["norm_g"], u_in[pick], u_cw[pick], s["conv_b"], s["lru_w_a"], s["lru_b_a"], s["lru_w_x"], s["lru_b_x"],
                s["lru_lambda"], s["pool_w"], s["pool_scale"], u_pl[pick], u_pp[pick], u_out[pick], s["ple_norm_g"],
                u_pg[pick], u_pe[pick], s["final_g"]]

    return (loss, grad_x, *ordered(0), *ordered(1), *ordered(2), *ordered(3))
```

```python
import jax
import jax.numpy as jnp
from jax import lax
from jax.experimental import pallas as pl
from jax.experimental.pallas import tpu as pltpu

F32 = jnp.float32
BF16 = jnp.bfloat16
MESH = pl.DeviceIdType.MESH

D_MODEL = 1024
LRU_HEADS = 8
HEAD_DIM = 128
CONV_WIDTH = 4
LRU_C = 8.0
POOL_WIDTH = 512
POOL_WINDOWS = (2, 4, 8, 16)
POOL_GROUP_DIM = 128
IN_COLS = 5120
N_CHIPS = 4
EPS = 1e-6

ADAM_LR = 0.001
ADAM_B1 = 0.9
ADAM_B2 = 0.999
ADAM_EPS = 1e-08
ADAM_WD = 0.01
ADAM_STEP = 10

F32_SUBLANES = 8
CONV_HIST = 8
POOL_HIST = 16
VMEM_LIMIT_BYTES = 58 * 1024 * 1024
VEC_BAG_SLOTS = ("norm_g", "conv_w", "conv_b", "lru_b_a", "lru_b_x", "lru_lambda", "pool_scale", "ple_norm_g",
                 "final_g", "loss")
VEC_BAG_ROWS = 128
MAT_BAG_AT = {"lru_w_a": 0, "lru_w_x": LRU_HEADS * HEAD_DIM, "pool_w": 2 * LRU_HEADS * HEAD_DIM}
MAT_BAG_ROWS = 2 * LRU_HEADS * HEAD_DIM + len(POOL_WINDOWS) * POOL_GROUP_DIM


def _bag_row(name, k=0):
    at = F32_SUBLANES * VEC_BAG_SLOTS.index(name) + k
    return slice(at, at + 1)


def _bag_rows(name):
    at = F32_SUBLANES * VEC_BAG_SLOTS.index(name)
    return slice(at, at + F32_SUBLANES)


def _dot(a, b):
    return jnp.dot(a, b, preferred_element_type=F32)


def _dot_nt(a, b):
    return lax.dot_general(a, b, (((1,), (1,)), ((), ())), preferred_element_type=F32)


def _dot_tn(a, b):
    return lax.dot_general(a, b, (((0,), (0,)), ((), ())), preferred_element_type=F32)


def _sigmoid(v):
    return jax.nn.sigmoid(v)


def _softplus(v):
    return jnp.maximum(v, 0.0) + jnp.log1p(jnp.exp(-jnp.abs(v)))


def _place():
    return lax.axis_index("x"), lax.axis_index("y"), lax.axis_index("c")


GATHER_SEMS = 6


def _gather_shapes(shards):
    out_shape = []
    for arr, axis, _ in shards:
        r, cols = arr.shape
        out_shape.append(jax.ShapeDtypeStruct((N_CHIPS * r, cols) if axis == 0 else (r, N_CHIPS * cols), arr.dtype))
    n = len(shards)
    sems = [pltpu.SemaphoreType.DMA((n * GATHER_SEMS,)), pltpu.SemaphoreType.DMA((n * GATHER_SEMS,)),
            pltpu.SemaphoreType.DMA((n,))]
    return out_shape, sems


def _gather_steps(shards, ins, outs, send_sems, recv_sems, local_sems):
    n = len(shards)
    x, y, c = _place()
    me, sibling = (x, y, c), (x, y, 1 - c)
    chips = [(x, 1 - y), (1 - x, y), (1 - x, 1 - y)]

    def region(k, cx, cy, hc):
        (r, cols), axis = shards[k][0].shape, shards[k][1]
        j = 2 * cx + cy
        if axis == 0:
            if hc is None:
                return outs[k].at[pl.ds(j * r, r), :]
            return outs[k].at[pl.ds(j * r + hc * (r // 2), r // 2), :]
        if hc is None:
            return outs[k].at[:, pl.ds(j * cols, cols)]
        return outs[k].at[pl.ds(hc * (r // 2), r // 2), pl.ds(j * cols, cols)]

    def remote(k, sem, block, to, src=None):
        dst = region(k, *block)
        return pltpu.make_async_remote_copy(
            src_ref=dst if src is None else src, dst_ref=dst,
            send_sem=send_sems.at[k * GATHER_SEMS + sem], recv_sem=recv_sems.at[k * GATHER_SEMS + sem],
            device_id=to, device_id_type=MESH)

    def first(k, idx):
        r, split = shards[k][0].shape[0], shards[k][2]
        src = ins[k].at[pl.ds(c * (r // 2), r // 2), :] if split else ins[k]
        return remote(k, idx, (x, y, c if split else None), (*chips[idx], c), src=src)

    def relay(k):
        src_chip = (jnp.bitwise_xor(x, 1 - c), jnp.bitwise_xor(y, c))
        dst_chip = (jnp.bitwise_xor(x, c), jnp.bitwise_xor(y, 1 - c))
        return remote(k, 2, (*src_chip, c), (*dst_chip, c))

    def passed(k, idx):
        return remote(k, 3 + idx, (*chips[idx], c), sibling)

    def mine(k):
        return pltpu.make_async_copy(ins[k], region(k, x, y, None), local_sems.at[k])

    def start():
        for k in range(n):
            mine(k).start()
            for idx in range(2 if shards[k][2] else 3):
                first(k, idx).start()

    def relay_on():
        for k in range(n):
            split = shards[k][2]
            for idx in range(2):
                remote(k, idx, (*chips[idx], c if split else None), me).wait_recv()
            if split:
                relay(k).start()
                passed(k, 0).start()
                passed(k, 1).start()

    def finish():
        for k in range(n):
            split = shards[k][2]
            remote(k, 2, (*chips[2], c if split else None), me).wait_recv()
            if split:
                passed(k, 2).start()
        for k in range(n):
            if shards[k][2]:
                for idx in range(3):
                    remote(k, 3 + idx, (*chips[idx], 1 - c), me).wait_recv()
        for k in range(n):
            if shards[k][2]:
                for cp in (first(k, 0), first(k, 1), relay(k), passed(k, 0), passed(k, 1), passed(k, 2)):
                    cp.wait_send()
            else:
                for idx in range(3):
                    first(k, idx).wait_send()
            mine(k).wait()

    return start, relay_on, finish


RS_ADD_ROWS = (64, 32, 16, 8)


def _all_reduce_tile(v, name):
    n_dev = 2 * N_CHIPS
    flips = [(dx, dy, dc) for dx in (0, 1) for dy in (0, 1) for dc in (0, 1)][1:]

    def body(v_ref, o_ref, slots, send_sems, recv_sems):
        x, y, c = _place()
        mine = 4 * x + 2 * y + c

        def copy(k, to_flip, slot):
            dx, dy, dc = to_flip
            peer = (jnp.bitwise_xor(x, dx), jnp.bitwise_xor(y, dy), jnp.bitwise_xor(c, dc))
            return pltpu.make_async_remote_copy(
                src_ref=v_ref, dst_ref=slots.at[slot], send_sem=send_sems.at[k], recv_sem=recv_sems.at[k],
                device_id=peer, device_id_type=MESH)

        sends = [copy(k, flip, mine) for k, flip in enumerate(flips)]
        for cp in sends:
            cp.start()
        slots[mine] = v_ref[...]
        for k, (dx, dy, dc) in enumerate(flips):
            copy(k, (dx, dy, dc), jnp.bitwise_xor(mine, 4 * dx + 2 * dy + dc)).wait_recv()
        total = slots[0]
        for d in range(1, n_dev):
            total = total + slots[d]
        o_ref[...] = total
        for cp in sends:
            cp.wait_send()

    return pl.pallas_call(
        body, name=name, out_shape=jax.ShapeDtypeStruct(v.shape, F32),
        in_specs=[pl.BlockSpec(memory_space=pltpu.VMEM)], out_specs=pl.BlockSpec(memory_space=pltpu.VMEM),
        scratch_shapes=[pltpu.VMEM((n_dev,) + v.shape, F32), pltpu.SemaphoreType.DMA((n_dev - 1,)),
                        pltpu.SemaphoreType.DMA((n_dev - 1,))],
    )(v)


RS_SEMS = 8
RS_LOCAL_SEMS = 5


def _rs_piece_shape(part):
    arr, cols = part[0], part[1]
    return (arr.shape[0] // 2, arr.shape[1] // N_CHIPS) if cols else tuple(arr.shape[1:])


def _rs_operands(parts):
    return [p[0] for p in parts] + [p[0] if p[2] is None else p[2] for p in parts]


def _rs_wires(parts, wire):
    return list(wire) if isinstance(wire, (list, tuple)) else [wire] * len(parts)


def _rs_shapes(parts, wire):
    n = len(parts)
    shapes = [_rs_piece_shape(p) for p in parts]
    out_shape = [jax.ShapeDtypeStruct((2,) + s, F32) for s in shapes]
    scratch = []
    for lead, kind in ((N_CHIPS, "f32"), (N_CHIPS, "narrow"), (N_CHIPS, "wire"), (None, "f32"), (N_CHIPS, "wire")):
        for s, p, w in zip(shapes, parts, _rs_wires(parts, wire)):
            dtype = {"f32": F32, "narrow": F32 if p[2] is None else p[2].dtype, "wire": w}[kind]
            scratch.append(pltpu.VMEM(s if lead is None else (lead,) + s, dtype))
    scratch += [pltpu.SemaphoreType.DMA((n * RS_SEMS,)), pltpu.SemaphoreType.DMA((n * RS_SEMS,)),
                pltpu.SemaphoreType.DMA((n * RS_LOCAL_SEMS,))]
    return out_shape, scratch


def _rs_steps(parts, ins, outs, scratch):
    n = len(parts)
    own, sib, got, fin, snd = (scratch[k * n:(k + 1) * n] for k in range(5))
    send_sems, recv_sems, local_sems = scratch[5 * n:]
    shapes = [_rs_piece_shape(p) for p in parts]
    x, y, c = _place()
    j_me = 2 * x + y
    me, sibling = (x, y, c), (x, y, 1 - c)

    def piece(a, jj, core, narrow=False):
        ref = ins[n + a] if narrow else ins[a]
        if parts[a][1]:
            r, cl = shapes[a]
            return ref.at[pl.ds(core * r, r), pl.ds(jj * cl, cl)]
        return ref.at[2 * jj + core]

    def remote(a, sem, src, dst, to):
        return pltpu.make_async_remote_copy(
            src_ref=src, dst_ref=dst, send_sem=send_sems.at[a * RS_SEMS + sem],
            recv_sem=recv_sems.at[a * RS_SEMS + sem], device_id=to, device_id_type=MESH)

    def rows_loop(a, fn):
        r = shapes[a][0]
        step = max(s for s in RS_ADD_ROWS if r % s == 0)

        def it(i, carry):
            fn(pl.ds(pl.multiple_of(i * step, step), step))
            return carry

        lax.fori_loop(0, r // step, it, 0)

    def load(a, jj):
        return pltpu.make_async_copy(piece(a, jj, c), own[a].at[jj], local_sems.at[a * RS_LOCAL_SEMS + jj])

    def to_sibling(a, jj):
        return remote(a, jj, piece(a, jj, 1 - c, narrow=True), sib[a].at[jj], sibling)

    near = (jnp.bitwise_xor(x, 1 - c), jnp.bitwise_xor(y, c))
    far = (jnp.bitwise_xor(x, c), jnp.bitwise_xor(y, 1 - c))
    diag = (1 - x, 1 - y)
    FROM_NEAR, FROM_FAR, FEED = 0, 1, 2

    def chip_of(chip):
        return 2 * chip[0] + chip[1]

    def feed(a):
        return remote(a, 4, snd[a].at[chip_of(diag)], got[a].at[FEED], (*near, c))

    def to_near(a):
        return remote(a, 5, snd[a].at[chip_of(near)], got[a].at[FROM_NEAR], (*near, c))

    def to_far(a):
        return remote(a, 6, snd[a].at[chip_of(far)], got[a].at[FROM_FAR], (*far, c))

    def store(a):
        return pltpu.make_async_copy(fin[a], outs[a].at[c], local_sems.at[a * RS_LOCAL_SEMS + 4])

    def result_to_sibling(a):
        return remote(a, 7, fin[a], outs[a].at[c], sibling)

    def exchange():
        for a in range(n):
            for jj in range(N_CHIPS):
                load(a, jj).start()
                to_sibling(a, jj).start()

    def chip_sums():
        for a in range(n):
            for jj in range(N_CHIPS):
                load(a, jj).wait()
                remote(a, jj, sib[a].at[jj], sib[a].at[jj], me).wait_recv()

                def add(sl, a=a, jj=jj):
                    q = own[a][jj, sl, :] + sib[a][jj, sl, :].astype(F32)
                    own[a][jj, sl, :] = q
                    snd[a][jj, sl, :] = q.astype(snd[a].dtype)

                rows_loop(a, add)
        for a in range(n):
            feed(a).start()
        for a in range(n):
            to_near(a).start()

    def relay():
        for a in range(n):
            remote(a, 4, got[a].at[FEED], got[a].at[FEED], me).wait_recv()

            def add(sl, a=a):
                pair = own[a][chip_of(far), sl, :] + got[a][FEED, sl, :].astype(F32)
                snd[a][chip_of(far), sl, :] = pair.astype(snd[a].dtype)

            rows_loop(a, add)
            to_far(a).start()

    def totals():
        for a in range(n):
            remote(a, 5, got[a].at[FROM_NEAR], got[a].at[FROM_NEAR], me).wait_recv()
            remote(a, 6, got[a].at[FROM_FAR], got[a].at[FROM_FAR], me).wait_recv()

            def total(sl, a=a):
                fin[a][sl, :] = (own[a][j_me, sl, :] + got[a][FROM_NEAR, sl, :].astype(F32)) + (
                    got[a][FROM_FAR, sl, :].astype(F32))

            rows_loop(a, total)
            store(a).start()
            result_to_sibling(a).start()

    def finish():
        for a in range(n):
            remote(a, 7, outs[a].at[1 - c], outs[a].at[1 - c], me).wait_recv()
        for a in range(n):
            for jj in range(N_CHIPS):
                to_sibling(a, jj).wait_send()
            for cp in (feed(a), to_near(a), to_far(a), result_to_sibling(a)):
                cp.wait_send()
            store(a).wait()

    return exchange, chip_sums, relay, totals, finish


def _rms(x):
    r = lax.rsqrt(jnp.mean(x * x, axis=-1, keepdims=True) + EPS)
    return x * r, r


def _rms_bwd(dxn, xn, r):
    return r * (dxn - xn * jnp.mean(dxn * xn, axis=-1, keepdims=True))


def _in_proj_gather(x2d, norm_g, w_in_sh, shards, tb):
    t = x2d.shape[0]
    nb = t // tb
    cols = IN_COLS // N_CHIPS
    half = D_MODEL // 2
    n = len(shards)

    def body(x_ref, g_ref, win_ref, *refs):
        ins = refs[:n]
        z_ref, h_ref, wfull_ref = refs[n:n + 3]
        outs = refs[n + 3:2 * n + 3]
        wv, h_buf, send_sems, recv_sems, local_sems, w_send, w_recv, w_local = refs[2 * n + 3:]
        s, i = pl.program_id(0), pl.program_id(1)
        x, y, c = _place()
        me, sibling = (x, y, c), (x, y, 1 - c)
        chips = [(x, 1 - y), (1 - x, y), (1 - x, 1 - y)]

        def w_half(cx, cy, hc):
            return wv.at[2 * cx + cy, pl.ds(hc * half, half), :]

        def w_remote(sem, block, to, src=None):
            dst = w_half(*block)
            return pltpu.make_async_remote_copy(
                src_ref=dst if src is None else src, dst_ref=dst, send_sem=w_send.at[sem],
                recv_sem=w_recv.at[sem], device_id=to, device_id_type=MESH)

        def w_first(idx):
            return w_remote(idx, (x, y, c), (*chips[idx], c), src=win_ref.at[pl.ds(c * half, half), :])

        def w_relay():
            src_chip = (jnp.bitwise_xor(x, 1 - c), jnp.bitwise_xor(y, c))
            dst_chip = (jnp.bitwise_xor(x, c), jnp.bitwise_xor(y, 1 - c))
            return w_remote(2, (*src_chip, c), (*dst_chip, c))

        def w_pass(idx):
            return w_remote(3 + idx, (*chips[idx], c), sibling)

        def w_store(k, cx, cy):
            jj = 2 * cx + cy
            return pltpu.make_async_copy(wv.at[jj], wfull_ref.at[:, pl.ds(jj * cols, cols)], w_local.at[k])

        start_rest, relay_rest, finish_rest = _gather_steps(shards, ins, outs, send_sems, recv_sems, local_sems)
        own = pltpu.make_async_copy(win_ref, wv.at[2 * x + y], w_local.at[4])

        @pl.when((s == 0) & (i == 0))
        def _():
            own.start()
            w_first(0).start()
            w_first(1).start()
            start_rest()
            own.wait()
            w_store(0, x, y).start()

        @pl.when((s == 1) & (i == 0))
        def _():
            w_remote(0, (*chips[0], c), me).wait_recv()
            w_remote(1, (*chips[1], c), me).wait_recv()
            w_relay().start()
            w_pass(0).start()
            w_pass(1).start()
            w_remote(3, (*chips[0], 1 - c), me).wait_recv()
            w_store(1, *chips[0]).start()

        @pl.when((s == 2) & (i == 0))
        def _():
            w_remote(4, (*chips[1], 1 - c), me).wait_recv()
            w_store(2, *chips[1]).start()

        @pl.when((s == 3) & (i == 0))
        def _():
            w_remote(2, (*chips[2], c), me).wait_recv()
            w_pass(2).start()
            w_remote(5, (*chips[2], 1 - c), me).wait_recv()
            w_store(3, *chips[2]).start()

        xn, _ = _rms(x_ref[...])
        h = (xn * g_ref[...]).astype(BF16)
        keep_h = pltpu.make_async_copy(h_buf, h_ref.at[pl.ds(pl.multiple_of(i * tb, tb), tb), :], w_local.at[5])

        @pl.when(s == 0)
        def _():
            h_buf[...] = h
            keep_h.start()

        z_ref[...] = _dot(h, wv[jnp.bitwise_xor(2 * x + y, s)])
        pl.when(s == 0)(keep_h.wait)

        @pl.when((s == N_CHIPS - 1) & (i == nb - 1))
        def _():
            relay_rest()
            finish_rest()
            for cp in (w_first(0), w_first(1), w_relay(), w_pass(0), w_pass(1), w_pass(2)):
                cp.wait_send()
            w_store(0, x, y).wait()
            for idx in range(3):
                w_store(idx + 1, *chips[idx]).wait()

    rest_shape, rest_sems = _gather_shapes(shards)
    out_shape = [jax.ShapeDtypeStruct((t, IN_COLS), F32), jax.ShapeDtypeStruct((t, D_MODEL), BF16),
                 jax.ShapeDtypeStruct((D_MODEL, IN_COLS), BF16)] + rest_shape
    any_spec = pl.BlockSpec(memory_space=pl.ANY)

    def z_map(s, i):
        return (i, jnp.bitwise_xor(2 * lax.axis_index("x") + lax.axis_index("y"), s))

    return pl.pallas_call(
        body, name="in_proj", out_shape=tuple(out_shape),
        grid=(N_CHIPS, nb),
        in_specs=[pl.BlockSpec((tb, D_MODEL), lambda s, i: (i, 0)),
                  pl.BlockSpec((1, D_MODEL), lambda s, i: (0, 0)), any_spec] + [any_spec] * n,
        out_specs=tuple([pl.BlockSpec((tb, cols), z_map), any_spec, any_spec] + [any_spec] * n),
        scratch_shapes=[pltpu.VMEM((N_CHIPS, D_MODEL, cols), BF16), pltpu.VMEM((tb, D_MODEL), BF16)] + rest_sems + [
            pltpu.SemaphoreType.DMA((GATHER_SEMS,)), pltpu.SemaphoreType.DMA((GATHER_SEMS,)),
            pltpu.SemaphoreType.DMA((N_CHIPS + 2,))],
        compiler_params=pltpu.CompilerParams(dimension_semantics=("arbitrary", "arbitrary"),
                                             vmem_limit_bytes=VMEM_LIMIT_BYTES),
    )(x2d, norm_g, w_in_sh, *[sh[0] for sh in shards])


def _in_proj_bwd(dz, w_in, x2d, dx_res, norm_g, tb, reduce, shards):
    t = x2d.shape[0]
    nb = t // tb
    parts, wire, steps = reduce
    n = len(parts)
    k = len(shards)

    def body(dz_ref, w_ref, x_ref, dres_ref, g_ref, *refs):
        at = 2 * n + k
        dx_ref, dg_ref = refs[at:at + 2]
        rs_outs, g_outs = refs[at + 2:at + 2 + n], refs[at + 2 + n:at + 2 + n + k]
        scratch = refs[at + 2 + n + k:]
        rs = _rs_steps(parts, refs[:2 * n], rs_outs, scratch[:len(scratch) - 3])
        for step, when in zip(rs, steps):
            pl.when(pl.program_id(0) == when)(step)
        gather = _gather_steps(shards, refs[2 * n:at], g_outs, *scratch[len(scratch) - 3:])
        for step, when in zip(gather, (0, nb // 2, nb - 1)):
            pl.when(pl.program_id(0) == when)(step)

        @pl.when(pl.program_id(0) == 0)
        def _():
            dg_ref[...] = jnp.zeros_like(dg_ref)

        xn, r = _rms(x_ref[...])
        g = g_ref[...]
        dh = _dot_nt(w_ref[...], dz_ref[...]).T
        dg_ref[0:1, :] += jnp.sum(dh * xn, axis=0, keepdims=True)
        dx_ref[...] = dres_ref[...] + _rms_bwd(dh * g, xn, r)

    row = lambda i: (i, 0)
    fixed = lambda i: (0, 0)
    rs_shape, rs_scratch = _rs_shapes(parts, wire)
    g_shape, g_sems = _gather_shapes(shards)
    any_spec = pl.BlockSpec(memory_space=pl.ANY)
    return pl.pallas_call(
        body, name="in_proj_bwd",
        out_shape=tuple([jax.ShapeDtypeStruct((t, D_MODEL), F32), jax.ShapeDtypeStruct((F32_SUBLANES, D_MODEL), F32)]
                        + rs_shape + g_shape),
        grid=(nb,),
        in_specs=[pl.BlockSpec((tb, IN_COLS), row),
                  pl.BlockSpec((D_MODEL, IN_COLS), fixed, pipeline_mode=pl.Buffered(1)),
                  pl.BlockSpec((tb, D_MODEL), row), pl.BlockSpec((tb, D_MODEL), row),
                  pl.BlockSpec((1, D_MODEL), fixed)] + [any_spec] * (2 * n + k),
        out_specs=tuple([pl.BlockSpec((tb, D_MODEL), row), pl.BlockSpec((F32_SUBLANES, D_MODEL), fixed)]
                        + [any_spec] * (n + k)),
        scratch_shapes=rs_scratch + g_sems,
        compiler_params=pltpu.CompilerParams(dimension_semantics=("arbitrary",),
                                             vmem_limit_bytes=VMEM_LIMIT_BYTES),
    )(dz, w_in, x2d, dx_res, norm_g, *_rs_operands(parts), *[sh[0] for sh in shards])


def _weight_grad(lhs, rhs, n_chunks, tb, name, reduce=None):
    t, k = lhs.shape
    nc = rhs.shape[1] // n_chunks
    nb = t // tb
    parts, wire, steps = reduce if reduce is not None else ([], F32, ())
    n = len(parts)

    def body(l_ref, r_ref, *refs):
        o_ref, o16_ref = refs[2 * n:2 * n + 2]
        if n:
            at = pl.program_id(0) * nb + pl.program_id(1)
            rs = _rs_steps(parts, refs[:2 * n], refs[2 * n + 2:3 * n + 2], refs[3 * n + 2:])
            for step, when in zip(rs, steps):
                pl.when(at == when)(step)

        @pl.when(pl.program_id(1) == 0)
        def _():
            o_ref[...] = jnp.zeros_like(o_ref)

        o_ref[...] += _dot_tn(l_ref[...], r_ref[...])

        @pl.when(pl.program_id(1) == nb - 1)
        def _():
            o16_ref[...] = o_ref[...].astype(BF16)

    rs_shape, rs_scratch = _rs_shapes(parts, wire) if n else ([], [])
    any_spec = pl.BlockSpec(memory_space=pl.ANY)
    chunk = pl.BlockSpec((None, k, nc), lambda j, i: (j, 0, 0))
    return pl.pallas_call(
        body, name=name,
        out_shape=tuple([jax.ShapeDtypeStruct((n_chunks, k, nc), F32), jax.ShapeDtypeStruct((n_chunks, k, nc), BF16)]
                        + rs_shape),
        grid=(n_chunks, nb),
        in_specs=[pl.BlockSpec((tb, k), lambda j, i: (i, 0)), pl.BlockSpec((tb, nc), lambda j, i: (i, j))]
        + [any_spec] * (2 * n),
        out_specs=tuple([chunk, chunk] + [any_spec] * n),
        scratch_shapes=rs_scratch,
        compiler_params=pltpu.CompilerParams(dimension_semantics=("arbitrary", "arbitrary"),
                                             vmem_limit_bytes=VMEM_LIMIT_BYTES),
    )(lhs, rhs, *_rs_operands(parts))


def _adam_update(w, g, m, v):
    m_ = ADAM_B1 * m + (1.0 - ADAM_B1) * g
    v_ = ADAM_B2 * v + (1.0 - ADAM_B2) * jnp.square(g)
    m_hat = m_ / (1.0 - ADAM_B1 ** ADAM_STEP)
    v_hat = v_ / (1.0 - ADAM_B2 ** ADAM_STEP)
    return -ADAM_LR * (m_hat / (jnp.sqrt(v_hat) + ADAM_EPS) + ADAM_WD * w), m_, v_


def _adamw_replicated(vec_sum, mat_sum, norm_grad, entries, conv):
    n = len(entries)

    def grad_of(name, shape, vec_ref, mat_ref, norm_ref):
        if name == "norm_g":
            return norm_ref[0:1, :]
        if name in MAT_BAG_AT:
            return mat_ref[MAT_BAG_AT[name]:MAT_BAG_AT[name] + shape[0], :]
        if shape[0] == 1:
            return vec_ref[_bag_row(name), 0:shape[1]]
        return jnp.concatenate([vec_ref[_bag_row(name), h * shape[1]:(h + 1) * shape[1]] for h in range(shape[0])],
                               axis=0)

    def body(vec_ref, mat_ref, norm_ref, *refs):
        ins, outs = refs[:3 * n + 4], refs[3 * n + 4:]
        for k in range(n):
            w_ref, m_ref, v_ref = ins[3 * k:3 * k + 3]
            g = grad_of(entries[k][0], w_ref.shape, vec_ref, mat_ref, norm_ref)
            d, m_, v_ = _adam_update(w_ref[...], g, m_ref[...], v_ref[...])
            for ref, val in zip(outs[4 * k:4 * k + 4], (g, d, m_, v_)):
                ref[...] = val
        w_ref, m_ref, v_ref, g_ref = ins[3 * n:]
        for ref, val in zip(outs[4 * n:], _adam_update(w_ref[...], g_ref[...], m_ref[...], v_ref[...])):
            ref[...] = val

    arrays = [a for e in entries for a in e[1:]] + list(conv)
    out_shape = [jax.ShapeDtypeStruct(e[1].shape, F32) for e in entries for _ in range(4)]
    out_shape += [jax.ShapeDtypeStruct(conv[0].shape, F32)] * 3
    return pl.pallas_call(
        body, name="adamw_replicated", out_shape=tuple(out_shape),
        compiler_params=pltpu.CompilerParams(vmem_limit_bytes=VMEM_LIMIT_BYTES),
    )(vec_sum, mat_sum, norm_grad, *arrays)


def _adamw(w, g, m, v, rows, name):
    r, c = w.shape

    def body(w_ref, g_ref, m_ref, v_ref, d_ref, nm_ref, nv_ref):
        d_ref[...], nm_ref[...], nv_ref[...] = _adam_update(w_ref[...], g_ref[...], m_ref[...], v_ref[...])

    spec = pl.BlockSpec((rows, c), lambda i: (i, 0))
    return pl.pallas_call(
        body, name=name, out_shape=tuple(jax.ShapeDtypeStruct((r, c), F32) for _ in range(3)),
        grid=(r // rows,), in_specs=[spec] * 4, out_specs=(spec,) * 3,
        compiler_params=pltpu.CompilerParams(dimension_semantics=("arbitrary",),
                                             vmem_limit_bytes=VMEM_LIMIT_BYTES),
    )(w, g, m, v)


def _shift_down(ext, s):
    return pltpu.roll(ext, s, 0)


def _tile_shift(v, s):
    rows, cols = v.shape
    tiles = v.reshape(rows // F32_SUBLANES, F32_SUBLANES, cols)
    return pltpu.roll(tiles, s % F32_SUBLANES, 1).reshape(rows, cols)


def _shift_up(ext, s):
    return pltpu.roll(ext, ext.shape[0] - s, 0)


def _lru_gates(xc, wa_ref, ba, wx_ref, bx, lam):
    pa, px = [], []
    for h in range(LRU_HEADS):
        xh = xc[:, h * HEAD_DIM:(h + 1) * HEAD_DIM].astype(BF16)
        pa.append(_dot(xh, wa_ref[h]))
        px.append(_dot(xh, wx_ref[h]))
    r = _sigmoid(jnp.concatenate(pa, axis=1) + ba)
    ig = _sigmoid(jnp.concatenate(px, axis=1) + bx)
    sp = _softplus(-lam)
    log_a = (-LRU_C * r) * sp
    a = jnp.exp(log_a)
    mult = jnp.sqrt(jnp.tanh(-log_a) * (1.0 + a * a))
    return r, ig, a, mult, sp


def _conv(ext, w_ref, b):
    y = b + _shift_down(ext, 3) * w_ref[0:1, :]
    y = y + _shift_down(ext, 2) * w_ref[1:2, :]
    y = y + _shift_down(ext, 1) * w_ref[2:3, :]
    y = y + ext * w_ref[3:4, :]
    return y[CONV_HIST:, :]


def _pool_diff(ext, pos):
    out = []
    for g, k in enumerate(POOL_WINDOWS):
        col = ext[:, g * POOL_GROUP_DIM:(g + 1) * POOL_GROUP_DIM]
        s = col
        for step in range(g + 1):
            s = s + _shift_down(s, 2 ** step)
        count = jnp.minimum(pos + 1, k).astype(F32)
        out.append(s[POOL_HIST:, :] / count - col[POOL_HIST:, :])
    return out


def _pool_mix(diff, pw_ref):
    return jnp.concatenate([_dot(diff[g].astype(BF16), pw_ref[g]) for g in range(len(POOL_WINDOWS))], axis=1)


def _branch_specs(tb, row_map, fixed):
    fixed3 = lambda i: (0, 0, 0)
    return [pl.BlockSpec((CONV_WIDTH, D_MODEL), fixed), pl.BlockSpec((1, D_MODEL), fixed),
            pl.BlockSpec((LRU_HEADS, HEAD_DIM, HEAD_DIM), fixed3), pl.BlockSpec((1, D_MODEL), fixed),
            pl.BlockSpec((LRU_HEADS, HEAD_DIM, HEAD_DIM), fixed3), pl.BlockSpec((1, D_MODEL), fixed),
            pl.BlockSpec((1, D_MODEL), fixed),
            pl.BlockSpec((len(POOL_WINDOWS), POOL_GROUP_DIM, POOL_GROUP_DIM), fixed3),
            pl.BlockSpec((1, POOL_WIDTH), fixed)]


def _branches_fwd(z, weights, seq, tb, shards):
    t = z.shape[0]
    nb = t // tb
    nbe = seq // tb
    groups = tb // F32_SUBLANES
    n = len(shards)

    def body(xa_ref, ga_ref, xb_ref, gb_ref, cw_ref, cb_ref, wa_ref, ba_ref, wx_ref, bx_ref, lam_ref,
             pw_ref, ps_ref, *refs):
        g_ins = refs[:n]
        ya_ref, yb_ref, hl_ref = refs[n:n + 3]
        g_outs = refs[n + 3:2 * n + 3]
        xa_ext, xb_ext, carry, a_s, u_s, send_sems, recv_sems, local_sems = refs[2 * n + 3:]
        blk = pl.program_id(0) % nbe
        start_gather, relay_gather, finish_gather = _gather_steps(shards, g_ins, g_outs, send_sems, recv_sems,
                                                                  local_sems)
        pl.when(pl.program_id(0) == 0)(start_gather)
        pl.when(pl.program_id(0) == nb // 2)(relay_gather)

        @pl.when(blk == 0)
        def _():
            xa_ext[0:CONV_HIST, :] = jnp.zeros((CONV_HIST, D_MODEL), F32)
            xb_ext[0:POOL_HIST, :] = jnp.zeros((POOL_HIST, POOL_WIDTH), F32)
            carry[...] = jnp.zeros_like(carry)

        xa_ext[CONV_HIST:, :] = xa_ref[...]
        xb_ext[POOL_HIST:, :] = xb_ref[...]
        ea = xa_ext[...]
        eb = xb_ext[...]
        xa_ext[0:CONV_HIST, :] = ea[tb:, :]
        xb_ext[0:POOL_HIST, :] = eb[tb:, :]

        xc = _conv(ea, cw_ref, cb_ref[...])
        _, ig, a, mult, _ = _lru_gates(xc, wa_ref, ba_ref[...], wx_ref, bx_ref[...], lam_ref[...])
        u = mult * (ig * xc)
        row8 = lax.broadcasted_iota(jnp.int32, (tb, D_MODEL), 0) % F32_SUBLANES
        for s in (1, 2, 4):
            m = row8 >= s
            u = jnp.where(m, a * _tile_shift(u, s) + u, u)
            a = jnp.where(m, a * _tile_shift(a, s), a)
        a_s[...] = a
        u_s[...] = u

        def step(g, cr):
            sl = pl.ds(pl.multiple_of(g * F32_SUBLANES, F32_SUBLANES), F32_SUBLANES)
            hb = a_s[sl, :] * cr + u_s[sl, :]
            hl_ref[sl, :] = hb
            return jnp.broadcast_to(hb[F32_SUBLANES - 1:F32_SUBLANES, :], (F32_SUBLANES, D_MODEL))

        carry[...] = lax.fori_loop(0, groups, step, carry[...], unroll=4)
        ga = ga_ref[...]
        ya_ref[...] = (hl_ref[...] * (ga * _sigmoid(ga))).astype(BF16)

        pos = blk * tb + lax.broadcasted_iota(jnp.int32, (tb, POOL_GROUP_DIM), 0)
        ypre = _pool_mix(_pool_diff(eb, pos), pw_ref)
        gb = gb_ref[...]
        yb_ref[...] = ((ypre * ps_ref[...]) * (gb * _sigmoid(gb))).astype(BF16)
        pl.when(pl.program_id(0) == nb - 1)(finish_gather)

    row = lambda i: (i, 0)
    fixed = lambda i: (0, 0)
    any_spec = pl.BlockSpec(memory_space=pl.ANY)
    in_specs = [pl.BlockSpec((tb, D_MODEL), lambda i: (i, 0)), pl.BlockSpec((tb, D_MODEL), lambda i: (i, 1)),
                pl.BlockSpec((tb, POOL_WIDTH), lambda i: (i, 4)), pl.BlockSpec((tb, POOL_WIDTH), lambda i: (i, 5)),
                ] + _branch_specs(tb, row, fixed) + [any_spec] * n
    g_shape, g_sems = _gather_shapes(shards)
    return pl.pallas_call(
        body, name="branches_fwd",
        out_shape=tuple([jax.ShapeDtypeStruct((t, D_MODEL), BF16), jax.ShapeDtypeStruct((t, POOL_WIDTH), BF16),
                         jax.ShapeDtypeStruct((t, D_MODEL), F32)] + g_shape),
        grid=(nb,), in_specs=in_specs,
        out_specs=tuple([pl.BlockSpec((tb, D_MODEL), row), pl.BlockSpec((tb, POOL_WIDTH), row),
                         pl.BlockSpec((tb, D_MODEL), row)] + [any_spec] * n),
        scratch_shapes=[pltpu.VMEM((tb + CONV_HIST, D_MODEL), F32), pltpu.VMEM((tb + POOL_HIST, POOL_WIDTH), F32),
                        pltpu.VMEM((F32_SUBLANES, D_MODEL), F32),
                        pltpu.VMEM((tb, D_MODEL), F32), pltpu.VMEM((tb, D_MODEL), F32)] + g_sems,
        compiler_params=pltpu.CompilerParams(dimension_semantics=("arbitrary",),
                                             vmem_limit_bytes=VMEM_LIMIT_BYTES),
    )(z, z, z, z, *weights, *[sh[0] for sh in shards])


def _branches_bwd(z, hl, dya, dyb, dzm, weights, vec_bag, seq, tb):
    t = z.shape[0]
    nb = t // tb
    nbe = seq // tb
    groups = tb // F32_SUBLANES

    def body(xa_ref, xap_ref, ga_ref, xb_ref, xbp_ref, gb_ref, hl_ref, hlp_ref, dya_ref, dyb_ref, dzm_ref,
             cw_ref, cb_ref, wa_ref, ba_ref, wx_ref, bx_ref, lam_ref, pw_ref, ps_ref, vec_in_ref,
             dz_ref, vec_ref, mat_ref,
             xa_ext, xb_ext, hl_ext, a_ext, dxc_ext, dwin_ext, g_carry, b_s, d_s, g_s):
        i = pl.program_id(0)
        blk = (nb - 1 - i) % nbe

        def mat_rows(name, k):
            at = MAT_BAG_AT[name] + k * HEAD_DIM
            return slice(at, at + HEAD_DIM)

        @pl.when(i == 0)
        def _():
            vec_ref[...] = vec_in_ref[...]
            mat_ref[...] = jnp.zeros_like(mat_ref)

        @pl.when(blk == nbe - 1)
        def _():
            a_ext[tb:, :] = jnp.zeros((F32_SUBLANES, D_MODEL), F32)
            dxc_ext[tb:, :] = jnp.zeros((CONV_HIST, D_MODEL), F32)
            dwin_ext[tb:, :] = jnp.zeros((POOL_HIST, POOL_WIDTH), F32)
            g_carry[...] = jnp.zeros_like(g_carry)

        live = (blk > 0).astype(F32)
        xa_ext[0:CONV_HIST, :] = xap_ref[...] * live
        xa_ext[CONV_HIST:, :] = xa_ref[...]
        xb_ext[0:POOL_HIST, :] = xbp_ref[...] * live
        xb_ext[POOL_HIST:, :] = xb_ref[...]
        hl_ext[0:F32_SUBLANES, :] = hlp_ref[...] * live
        hl_ext[F32_SUBLANES:, :] = hl_ref[...]
        ea = xa_ext[...]
        eb = xb_ext[...]

        xc = _conv(ea, cw_ref, cb_ref[...])
        lam = lam_ref[...]
        r, ig, a, mult, sp = _lru_gates(xc, wa_ref, ba_ref[...], wx_ref, bx_ref[...], lam)
        hl = hl_ref[...]
        ga = ga_ref[...]
        sga = _sigmoid(ga)
        dya = dya_ref[...]
        dhl = dya * (ga * sga)
        dz_ref[:, D_MODEL:2 * D_MODEL] = (dya * hl * (sga * (1.0 + ga * (1.0 - sga)))).astype(BF16)

        a_ext[0:tb, :] = a
        b = _shift_up(a_ext[...], 1)[0:tb, :]
        a_ext[tb:, :] = jnp.broadcast_to(a[0:1, :], (F32_SUBLANES, D_MODEL))
        d = dhl
        row8 = lax.broadcasted_iota(jnp.int32, (tb, D_MODEL), 0) % F32_SUBLANES
        for s in (1, 2, 4):
            m = row8 < F32_SUBLANES - s
            d = jnp.where(m, d + b * _tile_shift(d, -s), d)
            b = jnp.where(m, b * _tile_shift(b, -s), b)
        b_s[...] = b
        d_s[...] = d

        def step(k, cr):
            sl = pl.ds(pl.multiple_of((groups - 1 - k) * F32_SUBLANES, F32_SUBLANES), F32_SUBLANES)
            gb_ = d_s[sl, :] + b_s[sl, :] * cr
            g_s[sl, :] = gb_
            return jnp.broadcast_to(gb_[0:1, :], (F32_SUBLANES, D_MODEL))

        g_carry[...] = lax.fori_loop(0, groups, step, g_carry[...], unroll=4)
        gsc = g_s[...]
        da = gsc * _shift_down(hl_ext[...], 1)[F32_SUBLANES:, :]
        dmult = gsc * (ig * xc)
        dig = gsc * (mult * xc)
        dxc = gsc * (mult * ig)
        dlog_a = da * a - (a * a) * dmult / mult
        dr = dlog_a * (-LRU_C * sp)
        vec_ref[_bag_row("lru_lambda"), :] += jnp.sum(dlog_a * (-LRU_C * r), axis=0, keepdims=True)
        dpa = dr * (r * (1.0 - r))
        dpx = dig * (ig * (1.0 - ig))
        vec_ref[_bag_row("lru_b_a"), :] += jnp.sum(dpa, axis=0, keepdims=True)
        vec_ref[_bag_row("lru_b_x"), :] += jnp.sum(dpx, axis=0, keepdims=True)
        back = []
        for h in range(LRU_HEADS):
            cols = slice(h * HEAD_DIM, (h + 1) * HEAD_DIM)
            xh = xc[:, cols].astype(BF16)
            dpa_h = dpa[:, cols].astype(BF16)
            dpx_h = dpx[:, cols].astype(BF16)
            mat_ref[mat_rows("lru_w_a", h), :] += _dot_tn(xh, dpa_h)
            mat_ref[mat_rows("lru_w_x", h), :] += _dot_tn(xh, dpx_h)
            back.append(_dot_nt(dpa_h, wa_ref[h]) + _dot_nt(dpx_h, wx_ref[h]))
        dxc = dxc + jnp.concatenate(back, axis=1)
        vec_ref[_bag_row("conv_b"), :] += jnp.sum(dxc, axis=0, keepdims=True)
        for k in range(CONV_WIDTH):
            tap = _shift_down(ea, CONV_WIDTH - 1 - k)[CONV_HIST:, :] if k < CONV_WIDTH - 1 else ea[CONV_HIST:, :]
            vec_ref[_bag_row("conv_w", k), :] += jnp.sum(dxc * tap, axis=0, keepdims=True)
        dxc_ext[0:tb, :] = dxc
        ed = dxc_ext[...]
        dxa = ed * cw_ref[3:4, :]
        dxa = dxa + _shift_up(ed, 1) * cw_ref[2:3, :]
        dxa = dxa + _shift_up(ed, 2) * cw_ref[1:2, :]
        dxa = dxa + _shift_up(ed, 3) * cw_ref[0:1, :]
        dz_ref[:, 0:D_MODEL] = dxa[0:tb, :].astype(BF16)
        dxc_ext[tb:, :] = dxc[0:CONV_HIST, :]

        pos = blk * tb + lax.broadcasted_iota(jnp.int32, (tb, POOL_GROUP_DIM), 0)
        diff = _pool_diff(eb, pos)
        ypre = _pool_mix(diff, pw_ref)
        ps = ps_ref[...]
        gb = gb_ref[...]
        sgb = _sigmoid(gb)
        dyb = dyb_ref[...]
        dyp = dyb * (gb * sgb)
        dz_ref[:, 2 * D_MODEL + POOL_WIDTH:3 * D_MODEL] = (
            dyb * (ypre * ps) * (sgb * (1.0 + gb * (1.0 - sgb)))).astype(BF16)
        vec_ref[_bag_row("pool_scale"), 0:POOL_WIDTH] += jnp.sum(dyp * ypre, axis=0, keepdims=True)
        dypre = dyp * ps
        for g, k in enumerate(POOL_WINDOWS):
            cols = slice(g * POOL_GROUP_DIM, (g + 1) * POOL_GROUP_DIM)
            dyg = dypre[:, cols].astype(BF16)
            mat_ref[mat_rows("pool_w", g), :] += _dot_tn(diff[g].astype(BF16), dyg)
            ddiff = _dot_nt(dyg, pw_ref[g])
            count = jnp.minimum(pos + 1, k).astype(F32)
            dwin = ddiff / count
            dwin_ext[0:tb, cols] = dwin
            s = dwin_ext[:, cols]
            for step_ in range(g + 1):
                s = s + _shift_up(s, 2 ** step_)
            dz_ref[:, 2 * D_MODEL + g * POOL_GROUP_DIM:2 * D_MODEL + (g + 1) * POOL_GROUP_DIM] = (
                s[0:tb, :] - ddiff).astype(BF16)
            dwin_ext[tb:, cols] = dwin[0:POOL_HIST, :]

        dz_ref[:, 3 * D_MODEL:] = dzm_ref[...]

        @pl.when(i == nb - 1)
        def _():
            row = _bag_row("lru_lambda")
            vec_ref[row, :] = vec_ref[row, :] * (-_sigmoid(-lam))

    rev = lambda i: (nb - 1 - i, 0)
    fixed = lambda i: (0, 0)

    def prev(rows, col):
        per = tb // rows
        return lambda i: (jnp.maximum((nb - 1 - i) * per - 1, 0), col)

    in_specs = [pl.BlockSpec((tb, D_MODEL), lambda i: (nb - 1 - i, 0)),
                pl.BlockSpec((CONV_HIST, D_MODEL), prev(CONV_HIST, 0)),
                pl.BlockSpec((tb, D_MODEL), lambda i: (nb - 1 - i, 1)),
                pl.BlockSpec((tb, POOL_WIDTH), lambda i: (nb - 1 - i, 4)),
                pl.BlockSpec((POOL_HIST, POOL_WIDTH), prev(POOL_HIST, 4)),
                pl.BlockSpec((tb, POOL_WIDTH), lambda i: (nb - 1 - i, 5)),
                pl.BlockSpec((tb, D_MODEL), rev),
                pl.BlockSpec((F32_SUBLANES, D_MODEL), prev(F32_SUBLANES, 0)),
                pl.BlockSpec((tb, D_MODEL), rev), pl.BlockSpec((tb, POOL_WIDTH), rev),
                pl.BlockSpec((tb, 2 * D_MODEL), rev)] + _branch_specs(tb, rev, fixed) + [
                    pl.BlockSpec((VEC_BAG_ROWS, D_MODEL), fixed)]
    out_shape = (jax.ShapeDtypeStruct((t, IN_COLS), BF16), jax.ShapeDtypeStruct((VEC_BAG_ROWS, D_MODEL), F32),
                 jax.ShapeDtypeStruct((MAT_BAG_ROWS, HEAD_DIM), F32))
    out_specs = (pl.BlockSpec((tb, IN_COLS), rev), pl.BlockSpec((VEC_BAG_ROWS, D_MODEL), fixed),
                 pl.BlockSpec((MAT_BAG_ROWS, HEAD_DIM), fixed))
    scratch = [pltpu.VMEM((tb + CONV_HIST, D_MODEL), F32), pltpu.VMEM((tb + POOL_HIST, POOL_WIDTH), F32),
               pltpu.VMEM((tb + F32_SUBLANES, D_MODEL), F32), pltpu.VMEM((tb + F32_SUBLANES, D_MODEL), F32),
               pltpu.VMEM((tb + CONV_HIST, D_MODEL), F32), pltpu.VMEM((tb + POOL_HIST, POOL_WIDTH), F32),
               pltpu.VMEM((F32_SUBLANES, D_MODEL), F32),
               pltpu.VMEM((tb, D_MODEL), F32), pltpu.VMEM((tb, D_MODEL), F32), pltpu.VMEM((tb, D_MODEL), F32)]
    return pl.pallas_call(
        body, name="branches_bwd", out_shape=out_shape, grid=(nb,), in_specs=in_specs, out_specs=out_specs,
        scratch_shapes=scratch, input_output_aliases={len(in_specs) - 1: 1},
        compiler_params=pltpu.CompilerParams(dimension_semantics=("arbitrary",),
                                             vmem_limit_bytes=VMEM_LIMIT_BYTES),
    )(z, z, z, z, z, z, hl, hl, dya, dyb, dzm, *weights, vec_bag)


def _merge_head(x2d, ya, yb, z, p2d, tgt, w_pl, w_pp, w_out, w_pg, w_pe, g2, gf, tb):
    t = x2d.shape[0]
    p_dim = p2d.shape[1]

    def body(x_ref, ya_ref, yb_ref, ma_ref, mb_ref, p_ref, t_ref, wpl_ref, wpp_ref, wout_ref, wpg_ref, wpe_ref,
             g2_ref, gf_ref,
             bag_ref, dxr_ref, dya_ref, dyb_ref, dzm_ref,
             mg_ref, do_ref, hn_ref, dgp_ref, dpe_ref, da_ref, dbm_ref, pbf_ref):
        @pl.when(pl.program_id(0) == 0)
        def _():
            bag_ref[...] = jnp.zeros_like(bag_ref)

        a_ = _dot(ya_ref[...], wpl_ref[...])
        bm = _dot(yb_ref[...], wpp_ref[...])
        sa = _sigmoid(ma_ref[...])
        sb = _sigmoid(mb_ref[...])
        mg = (sa * a_ + sb * bm).astype(BF16)
        mg_ref[...] = mg
        x1 = x_ref[...] + _dot(mg, wout_ref[...])
        xn2, r2 = _rms(x1)
        g2 = g2_ref[...]
        hn = (xn2 * g2).astype(BF16)
        hn_ref[...] = hn
        gate = _sigmoid(_dot(hn, wpg_ref[...]))
        pbf = p_ref[...].astype(BF16)
        pbf_ref[...] = pbf
        pe = _dot(pbf, wpe_ref[...])
        x2 = x1 + gate * pe
        xn3, r3 = _rms(x2)
        gf = gf_ref[...]
        err = xn3 * gf - t_ref[...]
        bag_ref[_bag_rows("loss"), 0:128] += 0.5 * jnp.sum(jnp.mean(err * err, axis=-1))

        dy = err * (1.0 / D_MODEL)
        bag_ref[_bag_row("final_g"), :] += jnp.sum(dy * xn3, axis=0, keepdims=True)
        dx2 = _rms_bwd(dy * gf, xn3, r3)
        dpe_ref[...] = (dx2 * gate).astype(BF16)
        dgp = ((dx2 * pe) * (gate * (1.0 - gate))).astype(BF16)
        dgp_ref[...] = dgp
        dhn = _dot_nt(dgp, wpg_ref[...])
        bag_ref[_bag_row("ple_norm_g"), :] += jnp.sum(dhn * xn2, axis=0, keepdims=True)
        dx1 = dx2 + _rms_bwd(dhn * g2, xn2, r2)
        dxr_ref[...] = dx1
        do = dx1.astype(BF16)
        do_ref[...] = do
        dmg = _dot_nt(do, wout_ref[...])
        da = (dmg * sa).astype(BF16)
        dbm = (dmg * sb).astype(BF16)
        da_ref[...] = da
        dbm_ref[...] = dbm
        dzm_ref[:, 0:D_MODEL] = (dmg * a_ * (sa * (1.0 - sa))).astype(BF16)
        dzm_ref[:, D_MODEL:] = (dmg * bm * (sb * (1.0 - sb))).astype(BF16)
        dya_ref[...] = _dot_nt(da, wpl_ref[...])
        dyb_ref[...] = _dot_nt(dbm, wpp_ref[...])

    row = lambda i: (i, 0)
    fixed = lambda i: (0, 0)

    def resident(shape):
        return pl.BlockSpec(shape, fixed, pipeline_mode=pl.Buffered(1))

    tok = lambda width: pl.BlockSpec((tb, width), row)
    in_specs = [tok(D_MODEL), tok(D_MODEL), tok(POOL_WIDTH),
                pl.BlockSpec((tb, D_MODEL), lambda i: (i, 3)), pl.BlockSpec((tb, D_MODEL), lambda i: (i, 4)),
                tok(p_dim), tok(D_MODEL),
                resident((D_MODEL, D_MODEL)), resident((POOL_WIDTH, D_MODEL)), resident((D_MODEL, D_MODEL)),
                resident((D_MODEL, D_MODEL)), resident((p_dim, D_MODEL)),
                pl.BlockSpec((1, D_MODEL), fixed), pl.BlockSpec((1, D_MODEL), fixed)]
    bf = lambda width: jax.ShapeDtypeStruct((t, width), BF16)
    f32 = lambda width: jax.ShapeDtypeStruct((t, width), F32)
    out_shape = (jax.ShapeDtypeStruct((VEC_BAG_ROWS, D_MODEL), F32),
                 f32(D_MODEL), f32(D_MODEL), f32(POOL_WIDTH), bf(2 * D_MODEL),
                 bf(D_MODEL), bf(D_MODEL), bf(D_MODEL), bf(D_MODEL), bf(D_MODEL), bf(D_MODEL), bf(D_MODEL), bf(p_dim))
    out_specs = (pl.BlockSpec((VEC_BAG_ROWS, D_MODEL), fixed),
                 tok(D_MODEL), tok(D_MODEL), tok(POOL_WIDTH), tok(2 * D_MODEL),
                 tok(D_MODEL), tok(D_MODEL), tok(D_MODEL), tok(D_MODEL), tok(D_MODEL), tok(D_MODEL), tok(D_MODEL),
                 tok(p_dim))
    return pl.pallas_call(
        body, name="merge_head", out_shape=out_shape, grid=(t // tb,), in_specs=in_specs, out_specs=out_specs,
        compiler_params=pltpu.CompilerParams(dimension_semantics=("arbitrary",),
                                             vmem_limit_bytes=VMEM_LIMIT_BYTES),
    )(x2d, ya, yb, z, z, p2d, tgt, w_pl, w_pp, w_out, w_pg, w_pe, g2, gf)


def kernel(x, p, norm_g, w_in, conv_w, conv_b, lru_w_a, lru_b_a, lru_w_x, lru_b_x, lru_lambda, pool_w, pool_scale, w_proj_lru, w_proj_pool, w_out, ple_norm_g, w_ple_gate, w_ple_proj, final_g, loss_target, m_norm_g, m_w_in, m_conv_w, m_conv_b, m_lru_w_a, m_lru_b_a, m_lru_w_x, m_lru_b_x, m_lru_lambda, m_pool_w, m_pool_scale, m_w_proj_lru, m_w_proj_pool, m_w_out, m_ple_norm_g, m_w_ple_gate, m_w_ple_proj, m_final_g, v_norm_g, v_w_in, v_conv_w, v_conv_b, v_lru_w_a, v_lru_b_a, v_lru_w_x, v_lru_b_x, v_lru_lambda, v_pool_w, v_pool_scale, v_w_proj_lru, v_w_proj_pool, v_w_out, v_ple_norm_g, v_w_ple_gate, v_w_ple_proj, v_final_g):
    bsz, seq, _ = x.shape
    t = bsz * seq
    tb_mm = min(1024, seq)
    tb_seq = min(256, seq // 2) if seq >= 512 else seq
    x2d = x.reshape(t, D_MODEL)
    p2d = p.reshape(t, p.shape[-1])
    tgt = loss_target.reshape(t, D_MODEL)
    chip = 2 * lax.axis_index("x") + lax.axis_index("y")

    rest = [(w_proj_lru[0], 0), (w_proj_pool[0], 1), (w_out[0], 0), (w_ple_gate[0], 0), (w_ple_proj[0], 1)]
    z, h_bf, w_in_f, conv_w_f = _in_proj_gather(x2d, norm_g, w_in[0].astype(BF16), [(conv_w[0], 1, False)], tb_mm)

    wa_bf = lru_w_a[0].astype(BF16)
    wx_bf = lru_w_x[0].astype(BF16)
    pw_bf = pool_w[0].astype(BF16)
    branch_w = (conv_w_f, conv_b, wa_bf, lru_b_a.reshape(1, D_MODEL), wx_bf, lru_b_x.reshape(1, D_MODEL),
                lru_lambda, pw_bf, pool_scale)

    ya, yb, hl, w_pl_f, w_pp_f, w_out_f, w_pg_f, w_pe_f = _branches_fwd(
        z, branch_w, seq, tb_seq, [(w.astype(BF16), axis, True) for w, axis in rest])
    (vec_bag, dx_res, dya, dyb, dzm, mg_bf, do_bf, hn_bf, dgp_bf, dpe_bf, da_bf, dbm_bf, p_bf) = _merge_head(
        x2d, ya, yb, z, p2d, tgt, w_pl_f, w_pp_f, w_out_f, w_pg_f, w_pe_f, ple_norm_g, final_g.reshape(1, D_MODEL),
        tb_seq)
    dz, vec_bag, mat_bag = _branches_bwd(z, hl, dya, dyb, dzm, branch_w, vec_bag, seq, tb_seq)

    tb_dw = min(1024, seq)
    def proj_grad(lhs, rhs, name, cols):
        g32, g16 = _weight_grad(lhs, rhs, 1, tb_dw, name)
        if cols:
            return g32[0], True, g16[0]
        rows = g32.shape[1] // 8
        return g32.reshape(8, rows, g32.shape[2]), False, g16.reshape(8, rows, g32.shape[2])

    p_dim = p2d.shape[1]
    proj_parts = [proj_grad(ya, da_bf, "dw_proj_lru", False), proj_grad(yb, dbm_bf, "dw_proj_pool", True),
                  proj_grad(mg_bf, do_bf, "dw_out", False), proj_grad(hn_bf, dgp_bf, "dw_ple_gate", False),
                  proj_grad(p_bf, dpe_bf, "dw_ple_proj", True)]
    nb_dw = t // tb_dw
    g_in, g_in16, r_pl, r_pp, r_out, r_pg, r_pe, vec_mine, mat_mine = _weight_grad(
        h_bf, dz, N_CHIPS, tb_dw, "dw_in",
        reduce=(proj_parts + [(vec_bag.reshape(8, VEC_BAG_ROWS // 8, D_MODEL), False, None),
                              (mat_bag.reshape(8, MAT_BAG_ROWS // 8, HEAD_DIM), False, None)],
                [BF16] * 5 + [F32] * 2,
                (0, nb_dw // 2, 2 * nb_dw - 1, 3 * nb_dw + nb_dw // 2, N_CHIPS * nb_dw - 1)))
    pieces = (8, D_MODEL // 2, IN_COLS // N_CHIPS)
    nb_seq = t // tb_seq
    dx, d_g1, r_in, vec_sum, mat_sum = _in_proj_bwd(
        dz, w_in_f, x2d, dx_res, norm_g, tb_seq,
        reduce=([(g_in.reshape(pieces), False, g_in16.reshape(pieces))], BF16,
                (0, nb_seq // 8, nb_seq // 2, nb_seq - 1, nb_seq - 1)),
        shards=[(vec_mine.reshape(VEC_BAG_ROWS // N_CHIPS, D_MODEL), 0, True),
                (mat_mine.reshape(MAT_BAG_ROWS // N_CHIPS, HEAD_DIM), 0, True)])
    g_g1 = _all_reduce_tile(d_g1, "allreduce_norm_g")

    def big_update(w, g2d, m, v, rows, name):
        d, nm, nv = _adamw(w[0], g2d, m[0], v[0], rows, name)
        return g2d[None], d[None], nm[None], nv[None]

    u_in = big_update(w_in, r_in.reshape(D_MODEL, IN_COLS // N_CHIPS), m_w_in, v_w_in, 256, "adamw_w_in")
    u_pl = big_update(w_proj_lru, r_pl.reshape(D_MODEL // N_CHIPS, D_MODEL), m_w_proj_lru, v_w_proj_lru, 256, "adamw_w_proj_lru")
    u_pp = big_update(w_proj_pool, r_pp.reshape(POOL_WIDTH, D_MODEL // N_CHIPS), m_w_proj_pool, v_w_proj_pool, 512, "adamw_w_proj_pool")
    u_out = big_update(w_out, r_out.reshape(D_MODEL // N_CHIPS, D_MODEL), m_w_out, v_w_out, 256, "adamw_w_out")
    u_pg = big_update(w_ple_gate, r_pg.reshape(D_MODEL // N_CHIPS, D_MODEL), m_w_ple_gate, v_w_ple_gate, 256, "adamw_w_ple_gate")
    u_pe = big_update(w_ple_proj, r_pe.reshape(p_dim, D_MODEL // N_CHIPS), m_w_ple_proj, v_w_ple_proj, 256, "adamw_w_ple_proj")

    small = [("norm_g", norm_g, m_norm_g, v_norm_g), ("conv_b", conv_b, m_conv_b, v_conv_b),
             ("lru_w_a", lru_w_a, m_lru_w_a, v_lru_w_a), ("lru_b_a", lru_b_a, m_lru_b_a, v_lru_b_a),
             ("lru_w_x", lru_w_x, m_lru_w_x, v_lru_w_x), ("lru_b_x", lru_b_x, m_lru_b_x, v_lru_b_x),
             ("lru_lambda", lru_lambda, m_lru_lambda, v_lru_lambda), ("pool_w", pool_w, m_pool_w, v_pool_w),
             ("pool_scale", pool_scale, m_pool_scale, v_pool_scale),
             ("ple_norm_g", ple_norm_g, m_ple_norm_g, v_ple_norm_g), ("final_g", final_g, m_final_g, v_final_g)]

    def view(a):
        return a.reshape(-1, a.shape[-1]) if a.ndim != 3 else a[0]

    cw_at = F32_SUBLANES * VEC_BAG_SLOTS.index("conv_w")
    cw_cols = D_MODEL // N_CHIPS
    g_cw = lax.dynamic_slice(vec_sum, (cw_at, chip * cw_cols), (CONV_WIDTH, cw_cols))
    flat = _adamw_replicated(vec_sum, mat_sum, g_g1, [(name,) + tuple(view(a) for a in arrs) for name, *arrs in small],
                             (conv_w[0], m_conv_w[0], v_conv_w[0], g_cw))
    u_small = {name: tuple(flat[4 * k + pick].reshape(arrs[0].shape) for pick in range(4))
               for k, (name, *arrs) in enumerate(small)}
    u_cw = tuple(a[None] for a in (g_cw,) + tuple(flat[4 * len(small):]))

    loss = vec_sum[F32_SUBLANES * VEC_BAG_SLOTS.index("loss"), 0]
    grad_x = dx.reshape(bsz, seq, D_MODEL)

    def ordered(pick):
        s = {name: u[pick] for name, u in u_small.items()}
        return [s["norm_g"], u_in[pick], u_cw[pick], s["conv_b"], s["lru_w_a"], s["lru_b_a"], s["lru_w_x"], s["lru_b_x"],
                s["lru_lambda"], s["pool_w"], s["pool_scale"], u_pl[pick], u_pp[pick], u_out[pick], s["ple_norm_g"],
                u_pg[pick], u_pe[pick], s["final_g"]]

    return (loss, grad_x, *ordered(0), *ordered(1), *ordered(2), *ordered(3))
```

```python
import jax
import jax.numpy as jnp
from jax import lax
from jax.experimental import pallas as pl
from jax.experimental.pallas import tpu as pltpu

F32 = jnp.float32
BF16 = jnp.bfloat16
MESH = pl.DeviceIdType.MESH

D_MODEL = 1024
LRU_HEADS = 8
HEAD_DIM = 128
CONV_WIDTH = 4
LRU_C = 8.0
POOL_WIDTH = 512
POOL_WINDOWS = (2, 4, 8, 16)
POOL_GROUP_DIM = 128
IN_COLS = 5120
N_CHIPS = 4
EPS = 1e-6

ADAM_LR = 0.001
ADAM_B1 = 0.9
ADAM_B2 = 0.999
ADAM_EPS = 1e-08
ADAM_WD = 0.01
ADAM_STEP = 10

F32_SUBLANES = 8
CONV_HIST = 8
POOL_HIST = 16
VMEM_LIMIT_BYTES = 58 * 1024 * 1024
VEC_BAG_SLOTS = ("norm_g", "conv_w", "conv_b", "lru_b_a", "lru_b_x", "lru_lambda", "pool_scale", "ple_norm_g",
                 "final_g", "loss")
VEC_BAG_ROWS = 128
MAT_BAG_AT = {"lru_w_a": 0, "lru_w_x": LRU_HEADS * HEAD_DIM, "pool_w": 2 * LRU_HEADS * HEAD_DIM}
MAT_BAG_ROWS = 2 * LRU_HEADS * HEAD_DIM + len(POOL_WINDOWS) * POOL_GROUP_DIM


def _bag_row(name, k=0):
    at = F32_SUBLANES * VEC_BAG_SLOTS.index(name) + k
    return slice(at, at + 1)


def _bag_rows(name):
    at = F32_SUBLANES * VEC_BAG_SLOTS.index(name)
    return slice(at, at + F32_SUBLANES)


def _dot(a, b):
    return jnp.dot(a, b, preferred_element_type=F32)


def _dot_nt(a, b):
    return lax.dot_general(a, b, (((1,), (1,)), ((), ())), preferred_element_type=F32)


def _dot_tn(a, b):
    return lax.dot_general(a, b, (((0,), (0,)), ((), ())), preferred_element_type=F32)


def _sigmoid(v):
    return jax.nn.sigmoid(v)


def _softplus(v):
    return jnp.maximum(v, 0.0) + jnp.log1p(jnp.exp(-jnp.abs(v)))


def _place():
    return lax.axis_index("x"), lax.axis_index("y"), lax.axis_index("c")


GATHER_SEMS = 6


def _gather_shapes(shards):
    out_shape = []
    for arr, axis, _ in shards:
        r, cols = arr.shape
        out_shape.append(jax.ShapeDtypeStruct((N_CHIPS * r, cols) if axis == 0 else (r, N_CHIPS * cols), arr.dtype))
    n = len(shards)
    sems = [pltpu.SemaphoreType.DMA((n * GATHER_SEMS,)), pltpu.SemaphoreType.DMA((n * GATHER_SEMS,)),
            pltpu.SemaphoreType.DMA((n,))]
    return out_shape, sems


def _gather_steps(shards, ins, outs, send_sems, recv_sems, local_sems):
    n = len(shards)
    x, y, c = _place()
    me, sibling = (x, y, c), (x, y, 1 - c)
    chips = [(x, 1 - y), (1 - x, y), (1 - x, 1 - y)]

    def region(k, cx, cy, hc):
        (r, cols), axis = shards[k][0].shape, shards[k][1]
        j = 2 * cx + cy
        if axis == 0:
            if hc is None:
                return outs[k].at[pl.ds(j * r, r), :]
            return outs[k].at[pl.ds(j * r + hc * (r // 2), r // 2), :]
        if hc is None:
            return outs[k].at[:, pl.ds(j * cols, cols)]
        return outs[k].at[pl.ds(hc * (r // 2), r // 2), pl.ds(j * cols, cols)]

    def remote(k, sem, block, to, src=None):
        dst = region(k, *block)
        return pltpu.make_async_remote_copy(
            src_ref=dst if src is None else src, dst_ref=dst,
            send_sem=send_sems.at[k * GATHER_SEMS + sem], recv_sem=recv_sems.at[k * GATHER_SEMS + sem],
            device_id=to, device_id_type=MESH)

    def first(k, idx):
        r, split = shards[k][0].shape[0], shards[k][2]
        src = ins[k].at[pl.ds(c * (r // 2), r // 2), :] if split else ins[k]
        return remote(k, idx, (x, y, c if split else None), (*chips[idx], c), src=src)

    def relay(k):
        src_chip = (jnp.bitwise_xor(x, 1 - c), jnp.bitwise_xor(y, c))
        dst_chip = (jnp.bitwise_xor(x, c), jnp.bitwise_xor(y, 1 - c))
        return remote(k, 2, (*src_chip, c), (*dst_chip, c))

    def passed(k, idx):
        return remote(k, 3 + idx, (*chips[idx], c), sibling)

    def mine(k):
        return pltpu.make_async_copy(ins[k], region(k, x, y, None), local_sems.at[k])

    def start():
        for k in range(n):
            mine(k).start()
            for idx in range(2 if shards[k][2] else 3):
                first(k, idx).start()

    def relay_on():
        for k in range(n):
            split = shards[k][2]
            for idx in range(2):
                remote(k, idx, (*chips[idx], c if split else None), me).wait_recv()
            if split:
                relay(k).start()
                passed(k, 0).start()
                passed(k, 1).start()

    def finish():
        for k in range(n):
            split = shards[k][2]
            remote(k, 2, (*chips[2], c if split else None), me).wait_recv()
            if split:
                passed(k, 2).start()
        for k in range(n):
            if shards[k][2]:
                for idx in range(3):
                    remote(k, 3 + idx, (*chips[idx], 1 - c), me).wait_recv()
        for k in range(n):
            if shards[k][2]:
                for cp in (first(k, 0), first(k, 1), relay(k), passed(k, 0), passed(k, 1), passed(k, 2)):
                    cp.wait_send()
            else:
                for idx in range(3):
                    first(k, idx).wait_send()
            mine(k).wait()

    return start, relay_on, finish


RS_ADD_ROWS = (64, 32, 16, 8)


def _all_reduce_tile(v, name):
    n_dev = 2 * N_CHIPS
    flips = [(dx, dy, dc) for dx in (0, 1) for dy in (0, 1) for dc in (0, 1)][1:]

    def body(v_ref, o_ref, slots, send_sems, recv_sems):
        x, y, c = _place()
        mine = 4 * x + 2 * y + c

        def copy(k, to_flip, slot):
            dx, dy, dc = to_flip
            peer = (jnp.bitwise_xor(x, dx), jnp.bitwise_xor(y, dy), jnp.bitwise_xor(c, dc))
            return pltpu.make_async_remote_copy(
                src_ref=v_ref, dst_ref=slots.at[slot], send_sem=send_sems.at[k], recv_sem=recv_sems.at[k],
                device_id=peer, device_id_type=MESH)

        sends = [copy(k, flip, mine) for k, flip in enumerate(flips)]
        for cp in sends:
            cp.start()
        slots[mine] = v_ref[...]
        for k, (dx, dy, dc) in enumerate(flips):
            copy(k, (dx, dy, dc), jnp.bitwise_xor(mine, 4 * dx + 2 * dy + dc)).wait_recv()
        total = slots[0]
        for d in range(1, n_dev):
            total = total + slots[d]
        o_ref[...] = total
        for cp in sends:
            cp.wait_send()

    return pl.pallas_call(
        body, name=name, out_shape=jax.ShapeDtypeStruct(v.shape, F32),
        in_specs=[pl.BlockSpec(memory_space=pltpu.VMEM)], out_specs=pl.BlockSpec(memory_space=pltpu.VMEM),
        scratch_shapes=[pltpu.VMEM((n_dev,) + v.shape, F32), pltpu.SemaphoreType.DMA((n_dev - 1,)),
                        pltpu.SemaphoreType.DMA((n_dev - 1,))],
    )(v)


RS_SEMS = 8
RS_LOCAL_SEMS = 5


def _rs_piece_shape(part):
    arr, cols = part[0], part[1]
    return (arr.shape[0] // 2, arr.shape[1] // N_CHIPS) if cols else tuple(arr.shape[1:])


def _rs_operands(parts):
    return [p[0] for p in parts] + [p[0] if p[2] is None else p[2] for p in parts]


def _rs_wires(parts, wire):
    return list(wire) if isinstance(wire, (list, tuple)) else [wire] * len(parts)


def _rs_shapes(parts, wire):
    n = len(parts)
    shapes = [_rs_piece_shape(p) for p in parts]
    out_shape = [jax.ShapeDtypeStruct((2,) + s, F32) for s in shapes]
    scratch = []
    for lead, kind in ((N_CHIPS, "f32"), (N_CHIPS, "narrow"), (N_CHIPS, "wire"), (None, "f32"), (N_CHIPS, "wire")):
        for s, p, w in zip(shapes, parts, _rs_wires(parts, wire)):
            dtype = {"f32": F32, "narrow": F32 if p[2] is None else p[2].dtype, "wire": w}[kind]
            scratch.append(pltpu.VMEM(s if lead is None else (lead,) + s, dtype))
    scratch += [pltpu.SemaphoreType.DMA((n * RS_SEMS,)), pltpu.SemaphoreType.DMA((n * RS_SEMS,)),
                pltpu.SemaphoreType.DMA((n * RS_LOCAL_SEMS,))]
    return out_shape, scratch


def _rs_steps(parts, ins, outs, scratch):
    n = len(parts)
    own, sib, got, fin, snd = (scratch[k * n:(k + 1) * n] for k in range(5))
    send_sems, recv_sems, local_sems = scratch[5 * n:]
    shapes = [_rs_piece_shape(p) for p in parts]
    x, y, c = _place()
    j_me = 2 * x + y
    me, sibling = (x, y, c), (x, y, 1 - c)

    def piece(a, jj, core, narrow=False):
        ref = ins[n + a] if narrow else ins[a]
        if parts[a][1]:
            r, cl = shapes[a]
            return ref.at[pl.ds(core * r, r), pl.ds(jj * cl, cl)]
        return ref.at[2 * jj + core]

    def remote(a, sem, src, dst, to):
        return pltpu.make_async_remote_copy(
            src_ref=src, dst_ref=dst, send_sem=send_sems.at[a * RS_SEMS + sem],
            recv_sem=recv_sems.at[a * RS_SEMS + sem], device_id=to, device_id_type=MESH)

    def rows_loop(a, fn):
        r = shapes[a][0]
        step = max(s for s in RS_ADD_ROWS if r % s == 0)

        def it(i, carry):
            fn(pl.ds(pl.multiple_of(i * step, step), step))
            return carry

        lax.fori_loop(0, r // step, it, 0)

    def load(a, jj):
        return pltpu.make_async_copy(piece(a, jj, c), own[a].at[jj], local_sems.at[a * RS_LOCAL_SEMS + jj])

    def to_sibling(a, jj):
        return remote(a, jj, piece(a, jj, 1 - c, narrow=True), sib[a].at[jj], sibling)

    near = (jnp.bitwise_xor(x, 1 - c), jnp.bitwise_xor(y, c))
    far = (jnp.bitwise_xor(x, c), jnp.bitwise_xor(y, 1 - c))
    diag = (1 - x, 1 - y)
    FROM_NEAR, FROM_FAR, FEED = 0, 1, 2

    def chip_of(chip):
        return 2 * chip[0] + chip[1]

    def feed(a):
        return remote(a, 4, snd[a].at[chip_of(diag)], got[a].at[FEED], (*near, c))

    def to_near(a):
        return remote(a, 5, snd[a].at[chip_of(near)], got[a].at[FROM_NEAR], (*near, c))

    def to_far(a):
        return remote(a, 6, snd[a].at[chip_of(far)], got[a].at[FROM_FAR], (*far, c))

    def store(a):
        return pltpu.make_async_copy(fin[a], outs[a].at[c], local_sems.at[a * RS_LOCAL_SEMS + 4])

    def result_to_sibling(a):
        return remote(a, 7, fin[a], outs[a].at[c], sibling)

    def exchange():
        for a in range(n):
            for jj in range(N_CHIPS):
                load(a, jj).start()
                to_sibling(a, jj).start()

    def chip_sums():
        for a in range(n):
            for jj in range(N_CHIPS):
                load(a, jj).wait()
                remote(a, jj, sib[a].at[jj], sib[a].at[jj], me).wait_recv()

                def add(sl, a=a, jj=jj):
                    q = own[a][jj, sl, :] + sib[a][jj, sl, :].astype(F32)
                    own[a][jj, sl, :] = q
                    snd[a][jj, sl, :] = q.astype(snd[a].dtype)

                rows_loop(a, add)
        for a in range(n):
            feed(a).start()
        for a in range(n):
            to_near(a).start()

    def relay():
        for a in range(n):
            remote(a, 4, got[a].at[FEED], got[a].at[FEED], me).wait_recv()

            def add(sl, a=a):
                pair = own[a][chip_of(far), sl, :] + got[a][FEED, sl, :].astype(F32)
                snd[a][chip_of(far), sl, :] = pair.astype(snd[a].dtype)

            rows_loop(a, add)
            to_far(a).start()

    def totals():
        for a in range(n):
            remote(a, 5, got[a].at[FROM_NEAR], got[a].at[FROM_NEAR], me).wait_recv()
            remote(a, 6, got[a].at[FROM_FAR], got[a].at[FROM_FAR], me).wait_recv()

            def total(sl, a=a):
                fin[a][sl, :] = (own[a][j_me, sl, :] + got[a][FROM_NEAR, sl, :].astype(F32)) + (
                    got[a][FROM_FAR, sl, :].astype(F32))

            rows_loop(a, total)
            store(a).start()
            result_to_sibling(a).start()

    def finish():
        for a in range(n):
            remote(a, 7, outs[a].at[1 - c], outs[a].at[1 - c], me).wait_recv()
        for a in range(n):
            for jj in range(N_CHIPS):
                to_sibling(a, jj).wait_send()
            for cp in (feed(a), to_near(a), to_far(a), result_to_sibling(a)):
                cp.wait_send()
            store(a).wait()

    return exchange, chip_sums, relay, totals, finish


def _rms(x):
    r = lax.rsqrt(jnp.mean(x * x, axis=-1, keepdims=True) + EPS)
    return x * r, r


def _rms_bwd(dxn, xn, r):
    return r * (dxn - xn * jnp.mean(dxn * xn, axis=-1, keepdims=True))


def _in_proj_gather(x2d, norm_g, w_in_sh, shards, tb):
    t = x2d.shape[0]
    nb = t // tb
    cols = IN_COLS // N_CHIPS
    half = D_MODEL // 2
    n = len(shards)

    def body(x_ref, g_ref, win_ref, *refs):
        ins = refs[:n]
        z_ref, h_ref, wfull_ref = refs[n:n + 3]
        outs = refs[n + 3:2 * n + 3]
        wv, h_buf, send_sems, recv_sems, local_sems, w_send, w_recv, w_local = refs[2 * n + 3:]
        s, i = pl.program_id(0), pl.program_id(1)
        x, y, c = _place()
        me, sibling = (x, y, c), (x, y, 1 - c)
        chips = [(x, 1 - y), (1 - x, y), (1 - x, 1 - y)]

        def w_half(cx, cy, hc):
            return wv.at[2 * cx + cy, pl.ds(hc * half, half), :]

        def w_remote(sem, block, to, src=None):
            dst = w_half(*block)
            return pltpu.make_async_remote_copy(
                src_ref=dst if src is None else src, dst_ref=dst, send_sem=w_send.at[sem],
                recv_sem=w_recv.at[sem], device_id=to, device_id_type=MESH)

        def w_first(idx):
            return w_remote(idx, (x, y, c), (*chips[idx], c), src=win_ref.at[pl.ds(c * half, half), :])

        def w_relay():
            src_chip = (jnp.bitwise_xor(x, 1 - c), jnp.bitwise_xor(y, c))
            dst_chip = (jnp.bitwise_xor(x, c), jnp.bitwise_xor(y, 1 - c))
            return w_remote(2, (*src_chip, c), (*dst_chip, c))

        def w_pass(idx):
            return w_remote(3 + idx, (*chips[idx], c), sibling)

        def w_store(k, cx, cy):
            jj = 2 * cx + cy
            return pltpu.make_async_copy(wv.at[jj], wfull_ref.at[:, pl.ds(jj * cols, cols)], w_local.at[k])

        start_rest, relay_rest, finish_rest = _gather_steps(shards, ins, outs, send_sems, recv_sems, local_sems)
        own = pltpu.make_async_copy(win_ref, wv.at[2 * x + y], w_local.at[4])

        @pl.when((s == 0) & (i == 0))
        def _():
            own.start()
            w_first(0).start()
            w_first(1).start()
            start_rest()
            own.wait()
            w_store(0, x, y).start()

        @pl.when((s == 1) & (i == 0))
        def _():
            w_remote(0, (*chips[0], c), me).wait_recv()
            w_remote(1, (*chips[1], c), me).wait_recv()
            w_relay().start()
            w_pass(0).start()
            w_pass(1).start()
            w_remote(3, (*chips[0], 1 - c), me).wait_recv()
            w_store(1, *chips[0]).start()

        @pl.when((s == 2) & (i == 0))
        def _():
            w_remote(4, (*chips[1], 1 - c), me).wait_recv()
            w_store(2, *chips[1]).start()

        @pl.when((s == 3) & (i == 0))
        def _():
            w_remote(2, (*chips[2], c), me).wait_recv()
            w_pass(2).start()
            w_remote(5, (*chips[2], 1 - c), me).wait_recv()
            w_store(3, *chips[2]).start()

        xn, _ = _rms(x_ref[...])
        h = (xn * g_ref[...]).astype(BF16)
        keep_h = pltpu.make_async_copy(h_buf, h_ref.at[pl.ds(pl.multiple_of(i * tb, tb), tb), :], w_local.at[5])

        @pl.when(s == 0)
        def _():
            h_buf[...] = h
            keep_h.start()

        z_ref[...] = _dot(h, wv[jnp.bitwise_xor(2 * x + y, s)])
        pl.when(s == 0)(keep_h.wait)

        @pl.when((s == N_CHIPS - 1) & (i == nb - 1))
        def _():
            relay_rest()
            finish_rest()
            for cp in (w_first(0), w_first(1), w_relay(), w_pass(0), w_pass(1), w_pass(2)):
                cp.wait_send()
            w_store(0, x, y).wait()
            for idx in range(3):
                w_store(idx + 1, *chips[idx]).wait()

    rest_shape, rest_sems = _gather_shapes(shards)
    out_shape = [jax.ShapeDtypeStruct((t, IN_COLS), F32), jax.ShapeDtypeStruct((t, D_MODEL), BF16),
                 jax.ShapeDtypeStruct((D_MODEL, IN_COLS), BF16)] + rest_shape
    any_spec = pl.BlockSpec(memory_space=pl.ANY)

    def z_map(s, i):
        return (i, jnp.bitwise_xor(2 * lax.axis_index("x") + lax.axis_index("y"), s))

    return pl.pallas_call(
        body, name="in_proj", out_shape=tuple(out_shape),
        grid=(N_CHIPS, nb),
        in_specs=[pl.BlockSpec((tb, D_MODEL), lambda s, i: (i, 0)),
                  pl.BlockSpec((1, D_MODEL), lambda s, i: (0, 0)), any_spec] + [any_spec] * n,
        out_specs=tuple([pl.BlockSpec((tb, cols), z_map), any_spec, any_spec] + [any_spec] * n),
        scratch_shapes=[pltpu.VMEM((N_CHIPS, D_MODEL, cols), BF16), pltpu.VMEM((tb, D_MODEL), BF16)] + rest_sems + [
            pltpu.SemaphoreType.DMA((GATHER_SEMS,)), pltpu.SemaphoreType.DMA((GATHER_SEMS,)),
            pltpu.SemaphoreType.DMA((N_CHIPS + 2,))],
        compiler_params=pltpu.CompilerParams(dimension_semantics=("arbitrary", "arbitrary"),
                                             vmem_limit_bytes=VMEM_LIMIT_BYTES),
    )(x2d, norm_g, w_in_sh, *[sh[0] for sh in shards])


def _in_proj_bwd(dz, w_in, x2d, dx_res, norm_g, tb, reduce, shards):
    t = x2d.shape[0]
    nb = t // tb
    parts, wire, steps = reduce
    n = len(parts)
    k = len(shards)

    def body(dz_ref, w_ref, x_ref, dres_ref, g_ref, *refs):
        at = 2 * n + k
        dx_ref, dg_ref = refs[at:at + 2]
        rs_outs, g_outs = refs[at + 2:at + 2 + n], refs[at + 2 + n:at + 2 + n + k]
        scratch = refs[at + 2 + n + k:]
        rs = _rs_steps(parts, refs[:2 * n], rs_outs, scratch[:len(scratch) - 3])
        for step, when in zip(rs, steps):
            pl.when(pl.program_id(0) == when)(step)
        gather = _gather_steps(shards, refs[2 * n:at], g_outs, *scratch[len(scratch) - 3:])
        for step, when in zip(gather, (0, nb // 2, nb - 1)):
            pl.when(pl.program_id(0) == when)(step)

        @pl.when(pl.program_id(0) == 0)
        def _():
            dg_ref[...] = jnp.zeros_like(dg_ref)

        xn, r = _rms(x_ref[...])
        g = g_ref[...]
        dh = _dot_nt(dz_ref[...], w_ref[...])
        dg_ref[0:1, :] += jnp.sum(dh * xn, axis=0, keepdims=True)
        dx_ref[...] = dres_ref[...] + _rms_bwd(dh * g, xn, r)

    row = lambda i: (i, 0)
    fixed = lambda i: (0, 0)
    rs_shape, rs_scratch = _rs_shapes(parts, wire)
    g_shape, g_sems = _gather_shapes(shards)
    any_spec = pl.BlockSpec(memory_space=pl.ANY)
    return pl.pallas_call(
        body, name="in_proj_bwd",
        out_shape=tuple([jax.ShapeDtypeStruct((t, D_MODEL), F32), jax.ShapeDtypeStruct((F32_SUBLANES, D_MODEL), F32)]
                        + rs_shape + g_shape),
        grid=(nb,),
        in_specs=[pl.BlockSpec((tb, IN_COLS), row),
                  pl.BlockSpec((D_MODEL, IN_COLS), fixed, pipeline_mode=pl.Buffered(1)),
                  pl.BlockSpec((tb, D_MODEL), row), pl.BlockSpec((tb, D_MODEL), row),
                  pl.BlockSpec((1, D_MODEL), fixed)] + [any_spec] * (2 * n + k),
        out_specs=tuple([pl.BlockSpec((tb, D_MODEL), row), pl.BlockSpec((F32_SUBLANES, D_MODEL), fixed)]
                        + [any_spec] * (n + k)),
        scratch_shapes=rs_scratch + g_sems,
        compiler_params=pltpu.CompilerParams(dimension_semantics=("arbitrary",),
                                             vmem_limit_bytes=VMEM_LIMIT_BYTES),
    )(dz, w_in, x2d, dx_res, norm_g, *_rs_operands(parts), *[sh[0] for sh in shards])


def _weight_grad(lhs, rhs, n_chunks, tb, name, reduce=None):
    t, k = lhs.shape
    nc = rhs.shape[1] // n_chunks
    nb = t // tb
    parts, wire, steps = reduce if reduce is not None else ([], F32, ())
    n = len(parts)

    def body(l_ref, r_ref, *refs):
        o_ref, o16_ref = refs[2 * n:2 * n + 2]
        if n:
            at = pl.program_id(0) * nb + pl.program_id(1)
            rs = _rs_steps(parts, refs[:2 * n], refs[2 * n + 2:3 * n + 2], refs[3 * n + 2:])
            for step, when in zip(rs, steps):
                pl.when(at == when)(step)

        @pl.when(pl.program_id(1) == 0)
        def _():
            o_ref[...] = jnp.zeros_like(o_ref)

        o_ref[...] += _dot_tn(l_ref[...], r_ref[...])

        @pl.when(pl.program_id(1) == nb - 1)
        def _():
            o16_ref[...] = o_ref[...].astype(BF16)

    rs_shape, rs_scratch = _rs_shapes(parts, wire) if n else ([], [])
    any_spec = pl.BlockSpec(memory_space=pl.ANY)
    chunk = pl.BlockSpec((None, k, nc), lambda j, i: (j, 0, 0))
    return pl.pallas_call(
        body, name=name,
        out_shape=tuple([jax.ShapeDtypeStruct((n_chunks, k, nc), F32), jax.ShapeDtypeStruct((n_chunks, k, nc), BF16)]
                        + rs_shape),
        grid=(n_chunks, nb),
        in_specs=[pl.BlockSpec((tb, k), lambda j, i: (i, 0)), pl.BlockSpec((tb, nc), lambda j, i: (i, j))]
        + [any_spec] * (2 * n),
        out_specs=tuple([chunk, chunk] + [any_spec] * n),
        scratch_shapes=rs_scratch,
        compiler_params=pltpu.CompilerParams(dimension_semantics=("arbitrary", "arbitrary"),
                                             vmem_limit_bytes=VMEM_LIMIT_BYTES),
    )(lhs, rhs, *_rs_operands(parts))


def _adam_update(w, g, m, v):
    m_ = ADAM_B1 * m + (1.0 - ADAM_B1) * g
    v_ = ADAM_B2 * v + (1.0 - ADAM_B2) * jnp.square(g)
    m_hat = m_ / (1.0 - ADAM_B1 ** ADAM_STEP)
    v_hat = v_ / (1.0 - ADAM_B2 ** ADAM_STEP)
    return -ADAM_LR * (m_hat / (jnp.sqrt(v_hat) + ADAM_EPS) + ADAM_WD * w), m_, v_


def _adamw_replicated(vec_sum, mat_sum, norm_grad, entries, conv):
    n = len(entries)

    def grad_of(name, shape, vec_ref, mat_ref, norm_ref):
        if name == "norm_g":
            return norm_ref[0:1, :]
        if name in MAT_BAG_AT:
            return mat_ref[MAT_BAG_AT[name]:MAT_BAG_AT[name] + shape[0], :]
        if shape[0] == 1:
            return vec_ref[_bag_row(name), 0:shape[1]]
        return jnp.concatenate([vec_ref[_bag_row(name), h * shape[1]:(h + 1) * shape[1]] for h in range(shape[0])],
                               axis=0)

    def body(vec_ref, mat_ref, norm_ref, *refs):
        ins, outs = refs[:3 * n + 4], refs[3 * n + 4:]
        for k in range(n):
            w_ref, m_ref, v_ref = ins[3 * k:3 * k + 3]
            g = grad_of(entries[k][0], w_ref.shape, vec_ref, mat_ref, norm_ref)
            d, m_, v_ = _adam_update(w_ref[...], g, m_ref[...], v_ref[...])
            for ref, val in zip(outs[4 * k:4 * k + 4], (g, d, m_, v_)):
                ref[...] = val
        w_ref, m_ref, v_ref, g_ref = ins[3 * n:]
        for ref, val in zip(outs[4 * n:], _adam_update(w_ref[...], g_ref[...], m_ref[...], v_ref[...])):
            ref[...] = val

    arrays = [a for e in entries for a in e[1:]] + list(conv)
    out_shape = [jax.ShapeDtypeStruct(e[1].shape, F32) for e in entries for _ in range(4)]
    out_shape += [jax.ShapeDtypeStruct(conv[0].shape, F32)] * 3
    return pl.pallas_call(
        body, name="adamw_replicated", out_shape=tuple(out_shape),
        compiler_params=pltpu.CompilerParams(vmem_limit_bytes=VMEM_LIMIT_BYTES),
    )(vec_sum, mat_sum, norm_grad, *arrays)


def _adamw(w, g, m, v, rows, name):
    r, c = w.shape

    def body(w_ref, g_ref, m_ref, v_ref, d_ref, nm_ref, nv_ref):
        d_ref[...], nm_ref[...], nv_ref[...] = _adam_update(w_ref[...], g_ref[...], m_ref[...], v_ref[...])

    spec = pl.BlockSpec((rows, c), lambda i: (i, 0))
    return pl.pallas_call(
        body, name=name, out_shape=tuple(jax.ShapeDtypeStruct((r, c), F32) for _ in range(3)),
        grid=(r // rows,), in_specs=[spec] * 4, out_specs=(spec,) * 3,
        compiler_params=pltpu.CompilerParams(dimension_semantics=("arbitrary",),
                                             vmem_limit_bytes=VMEM_LIMIT_BYTES),
    )(w, g, m, v)


def _shift_down(ext, s):
    return pltpu.roll(ext, s, 0)


def _tile_shift(v, s):
    rows, cols = v.shape
    tiles = v.reshape(rows // F32_SUBLANES, F32_SUBLANES, cols)
    return pltpu.roll(tiles, s % F32_SUBLANES, 1).reshape(rows, cols)


def _shift_up(ext, s):
    return pltpu.roll(ext, ext.shape[0] - s, 0)


def _lru_gates(xc, wa_ref, ba, wx_ref, bx, lam):
    pa, px = [], []
    for h in range(LRU_HEADS):
        xh = xc[:, h * HEAD_DIM:(h + 1) * HEAD_DIM].astype(BF16)
        pa.append(_dot(xh, wa_ref[h]))
        px.append(_dot(xh, wx_ref[h]))
    r = _sigmoid(jnp.concatenate(pa, axis=1) + ba)
    ig = _sigmoid(jnp.concatenate(px, axis=1) + bx)
    sp = _softplus(-lam)
    log_a = (-LRU_C * r) * sp
    a = jnp.exp(log_a)
    mult = jnp.sqrt(jnp.tanh(-log_a) * (1.0 + a * a))
    return r, ig, a, mult, sp


def _conv(ext, w_ref, b):
    y = b + _shift_down(ext, 3) * w_ref[0:1, :]
    y = y + _shift_down(ext, 2) * w_ref[1:2, :]
    y = y + _shift_down(ext, 1) * w_ref[2:3, :]
    y = y + ext * w_ref[3:4, :]
    return y[CONV_HIST:, :]


def _pool_diff(ext, pos):
    out = []
    for g, k in enumerate(POOL_WINDOWS):
        col = ext[:, g * POOL_GROUP_DIM:(g + 1) * POOL_GROUP_DIM]
        s = col
        for step in range(g + 1):
            s = s + _shift_down(s, 2 ** step)
        count = jnp.minimum(pos + 1, k).astype(F32)
        out.append(s[POOL_HIST:, :] / count - col[POOL_HIST:, :])
    return out


def _pool_mix(diff, pw_ref):
    return jnp.concatenate([_dot(diff[g].astype(BF16), pw_ref[g]) for g in range(len(POOL_WINDOWS))], axis=1)


def _branch_specs(tb, row_map, fixed):
    fixed3 = lambda i: (0, 0, 0)
    return [pl.BlockSpec((CONV_WIDTH, D_MODEL), fixed), pl.BlockSpec((1, D_MODEL), fixed),
            pl.BlockSpec((LRU_HEADS, HEAD_DIM, HEAD_DIM), fixed3), pl.BlockSpec((1, D_MODEL), fixed),
            pl.BlockSpec((LRU_HEADS, HEAD_DIM, HEAD_DIM), fixed3), pl.BlockSpec((1, D_MODEL), fixed),
            pl.BlockSpec((1, D_MODEL), fixed),
            pl.BlockSpec((len(POOL_WINDOWS), POOL_GROUP_DIM, POOL_GROUP_DIM), fixed3),
            pl.BlockSpec((1, POOL_WIDTH), fixed)]


def _branches_fwd(z, weights, seq, tb, shards):
    t = z.shape[0]
    nb = t // tb
    nbe = seq // tb
    groups = tb // F32_SUBLANES
    n = len(shards)

    def body(xa_ref, ga_ref, xb_ref, gb_ref, cw_ref, cb_ref, wa_ref, ba_ref, wx_ref, bx_ref, lam_ref,
             pw_ref, ps_ref, *refs):
        g_ins = refs[:n]
        ya_ref, yb_ref, hl_ref, xc_ref, r_ref, ig_ref, a_ref, mult_ref = refs[n:n + 8]
        g_outs = refs[n + 8:2 * n + 8]
        xa_ext, xb_ext, carry, a_s, u_s, send_sems, recv_sems, local_sems = refs[2 * n + 8:]
        blk = pl.program_id(0) % nbe
        start_gather, relay_gather, finish_gather = _gather_steps(shards, g_ins, g_outs, send_sems, recv_sems,
                                                                  local_sems)
        pl.when(pl.program_id(0) == 0)(start_gather)
        pl.when(pl.program_id(0) == nb // 2)(relay_gather)

        @pl.when(blk == 0)
        def _():
            xa_ext[0:CONV_HIST, :] = jnp.zeros((CONV_HIST, D_MODEL), F32)
            xb_ext[0:POOL_HIST, :] = jnp.zeros((POOL_HIST, POOL_WIDTH), F32)
            carry[...] = jnp.zeros_like(carry)

        xa_ext[CONV_HIST:, :] = xa_ref[...]
        xb_ext[POOL_HIST:, :] = xb_ref[...]
        ea = xa_ext[...]
        eb = xb_ext[...]
        xa_ext[0:CONV_HIST, :] = ea[tb:, :]
        xb_ext[0:POOL_HIST, :] = eb[tb:, :]

        xc = _conv(ea, cw_ref, cb_ref[...])
        r, ig, a, mult, _ = _lru_gates(xc, wa_ref, ba_ref[...], wx_ref, bx_ref[...], lam_ref[...])
        for ref, val in ((xc_ref, xc), (r_ref, r), (ig_ref, ig), (a_ref, a), (mult_ref, mult)):
            ref[...] = val
        u = mult * (ig * xc)
        row8 = lax.broadcasted_iota(jnp.int32, (tb, D_MODEL), 0) % F32_SUBLANES
        for s in (1, 2, 4):
            m = row8 >= s
            u = jnp.where(m, a * _tile_shift(u, s) + u, u)
            a = jnp.where(m, a * _tile_shift(a, s), a)
        a_s[...] = a
        u_s[...] = u

        def step(g, cr):
            sl = pl.ds(pl.multiple_of(g * F32_SUBLANES, F32_SUBLANES), F32_SUBLANES)
            hb = a_s[sl, :] * cr + u_s[sl, :]
            hl_ref[sl, :] = hb
            return jnp.broadcast_to(hb[F32_SUBLANES - 1:F32_SUBLANES, :], (F32_SUBLANES, D_MODEL))

        carry[...] = lax.fori_loop(0, groups, step, carry[...], unroll=4)
        ga = ga_ref[...]
        ya_ref[...] = (hl_ref[...] * (ga * _sigmoid(ga))).astype(BF16)

        pos = blk * tb + lax.broadcasted_iota(jnp.int32, (tb, POOL_GROUP_DIM), 0)
        ypre = _pool_mix(_pool_diff(eb, pos), pw_ref)
        gb = gb_ref[...]
        yb_ref[...] = ((ypre * ps_ref[...]) * (gb * _sigmoid(gb))).astype(BF16)
        pl.when(pl.program_id(0) == nb - 1)(finish_gather)

    row = lambda i: (i, 0)
    fixed = lambda i: (0, 0)
    any_spec = pl.BlockSpec(memory_space=pl.ANY)
    in_specs = [pl.BlockSpec((tb, D_MODEL), lambda i: (i, 0)), pl.BlockSpec((tb, D_MODEL), lambda i: (i, 1)),
                pl.BlockSpec((tb, POOL_WIDTH), lambda i: (i, 4)), pl.BlockSpec((tb, POOL_WIDTH), lambda i: (i, 5)),
                ] + _branch_specs(tb, row, fixed) + [any_spec] * n
    g_shape, g_sems = _gather_shapes(shards)
    return pl.pallas_call(
        body, name="branches_fwd",
        out_shape=tuple([jax.ShapeDtypeStruct((t, D_MODEL), BF16), jax.ShapeDtypeStruct((t, POOL_WIDTH), BF16)]
                        + [jax.ShapeDtypeStruct((t, D_MODEL), F32)] * 6 + g_shape),
        grid=(nb,), in_specs=in_specs,
        out_specs=tuple([pl.BlockSpec((tb, D_MODEL), row), pl.BlockSpec((tb, POOL_WIDTH), row)]
                        + [pl.BlockSpec((tb, D_MODEL), row)] * 6 + [any_spec] * n),
        scratch_shapes=[pltpu.VMEM((tb + CONV_HIST, D_MODEL), F32), pltpu.VMEM((tb + POOL_HIST, POOL_WIDTH), F32),
                        pltpu.VMEM((F32_SUBLANES, D_MODEL), F32),
                        pltpu.VMEM((tb, D_MODEL), F32), pltpu.VMEM((tb, D_MODEL), F32)] + g_sems,
        compiler_params=pltpu.CompilerParams(dimension_semantics=("arbitrary",),
                                             vmem_limit_bytes=VMEM_LIMIT_BYTES),
    )(z, z, z, z, *weights, *[sh[0] for sh in shards])


def _branches_bwd(z, hl, dya, dyb, dzm, saved, weights, vec_bag, seq, tb):
    t = z.shape[0]
    nb = t // tb
    nbe = seq // tb
    groups = tb // F32_SUBLANES

    def body(xa_ref, xap_ref, ga_ref, xb_ref, xbp_ref, gb_ref, hl_ref, hlp_ref, dya_ref, dyb_ref, dzm_ref,
             xc_ref, r_ref, ig_ref, a_ref, mult_ref,
             cw_ref, cb_ref, wa_ref, ba_ref, wx_ref, bx_ref, lam_ref, pw_ref, ps_ref, vec_in_ref,
             dz_ref, vec_ref, mat_ref,
             xa_ext, xb_ext, hl_ext, a_ext, dxc_ext, dwin_ext, g_carry, b_s, d_s, g_s):
        i = pl.program_id(0)
        blk = (nb - 1 - i) % nbe

        def mat_rows(name, k):
            at = MAT_BAG_AT[name] + k * HEAD_DIM
            return slice(at, at + HEAD_DIM)

        @pl.when(i == 0)
        def _():
            vec_ref[...] = vec_in_ref[...]
            mat_ref[...] = jnp.zeros_like(mat_ref)

        @pl.when(blk == nbe - 1)
        def _():
            a_ext[tb:, :] = jnp.zeros((F32_SUBLANES, D_MODEL), F32)
            dxc_ext[tb:, :] = jnp.zeros((CONV_HIST, D_MODEL), F32)
            dwin_ext[tb:, :] = jnp.zeros((POOL_HIST, POOL_WIDTH), F32)
            g_carry[...] = jnp.zeros_like(g_carry)

        live = (blk > 0).astype(F32)
        xa_ext[0:CONV_HIST, :] = xap_ref[...] * live
        xa_ext[CONV_HIST:, :] = xa_ref[...]
        xb_ext[0:POOL_HIST, :] = xbp_ref[...] * live
        xb_ext[POOL_HIST:, :] = xb_ref[...]
        hl_ext[0:F32_SUBLANES, :] = hlp_ref[...] * live
        hl_ext[F32_SUBLANES:, :] = hl_ref[...]
        ea = xa_ext[...]
        eb = xb_ext[...]

        xc, r, ig, a, mult = xc_ref[...], r_ref[...], ig_ref[...], a_ref[...], mult_ref[...]
        lam = lam_ref[...]
        sp = _softplus(-lam)
        hl = hl_ref[...]
        ga = ga_ref[...]
        sga = _sigmoid(ga)
        dya = dya_ref[...]
        dhl = dya * (ga * sga)
        dz_ref[:, D_MODEL:2 * D_MODEL] = (dya * hl * (sga * (1.0 + ga * (1.0 - sga)))).astype(BF16)

        a_ext[0:tb, :] = a
        b = _shift_up(a_ext[...], 1)[0:tb, :]
        a_ext[tb:, :] = jnp.broadcast_to(a[0:1, :], (F32_SUBLANES, D_MODEL))
        d = dhl
        row8 = lax.broadcasted_iota(jnp.int32, (tb, D_MODEL), 0) % F32_SUBLANES
        for s in (1, 2, 4):
            m = row8 < F32_SUBLANES - s
            d = jnp.where(m, d + b * _tile_shift(d, -s), d)
            b = jnp.where(m, b * _tile_shift(b, -s), b)
        b_s[...] = b
        d_s[...] = d

        def step(k, cr):
            sl = pl.ds(pl.multiple_of((groups - 1 - k) * F32_SUBLANES, F32_SUBLANES), F32_SUBLANES)
            gb_ = d_s[sl, :] + b_s[sl, :] * cr
            g_s[sl, :] = gb_
            return jnp.broadcast_to(gb_[0:1, :], (F32_SUBLANES, D_MODEL))

        g_carry[...] = lax.fori_loop(0, groups, step, g_carry[...], unroll=4)
        gsc = g_s[...]
        da = gsc * _shift_down(hl_ext[...], 1)[F32_SUBLANES:, :]
        dmult = gsc * (ig * xc)
        dig = gsc * (mult * xc)
        dxc = gsc * (mult * ig)
        dlog_a = da * a - (a * a) * dmult / mult
        dr = dlog_a * (-LRU_C * sp)
        vec_ref[_bag_row("lru_lambda"), :] += jnp.sum(dlog_a * (-LRU_C * r), axis=0, keepdims=True)
        dpa = dr * (r * (1.0 - r))
        dpx = dig * (ig * (1.0 - ig))
        vec_ref[_bag_row("lru_b_a"), :] += jnp.sum(dpa, axis=0, keepdims=True)
        vec_ref[_bag_row("lru_b_x"), :] += jnp.sum(dpx, axis=0, keepdims=True)
        back = []
        for h in range(LRU_HEADS):
            cols = slice(h * HEAD_DIM, (h + 1) * HEAD_DIM)
            xh = xc[:, cols].astype(BF16)
            dpa_h = dpa[:, cols].astype(BF16)
            dpx_h = dpx[:, cols].astype(BF16)
            mat_ref[mat_rows("lru_w_a", h), :] += _dot_tn(xh, dpa_h)
            mat_ref[mat_rows("lru_w_x", h), :] += _dot_tn(xh, dpx_h)
            back.append(_dot_nt(dpa_h, wa_ref[h]) + _dot_nt(dpx_h, wx_ref[h]))
        dxc = dxc + jnp.concatenate(back, axis=1)
        vec_ref[_bag_row("conv_b"), :] += jnp.sum(dxc, axis=0, keepdims=True)
        for k in range(CONV_WIDTH):
            tap = _shift_down(ea, CONV_WIDTH - 1 - k)[CONV_HIST:, :] if k < CONV_WIDTH - 1 else ea[CONV_HIST:, :]
            vec_ref[_bag_row("conv_w", k), :] += jnp.sum(dxc * tap, axis=0, keepdims=True)
        dxc_ext[0:tb, :] = dxc
        ed = dxc_ext[...]
        dxa = ed * cw_ref[3:4, :]
        dxa = dxa + _shift_up(ed, 1) * cw_ref[2:3, :]
        dxa = dxa + _shift_up(ed, 2) * cw_ref[1:2, :]
        dxa = dxa + _shift_up(ed, 3) * cw_ref[0:1, :]
        dz_ref[:, 0:D_MODEL] = dxa[0:tb, :].astype(BF16)
        dxc_ext[tb:, :] = dxc[0:CONV_HIST, :]

        pos = blk * tb + lax.broadcasted_iota(jnp.int32, (tb, POOL_GROUP_DIM), 0)
        diff = _pool_diff(eb, pos)
        ypre = _pool_mix(diff, pw_ref)
        ps = ps_ref[...]
        gb = gb_ref[...]
        sgb = _sigmoid(gb)
        dyb = dyb_ref[...]
        dyp = dyb * (gb * sgb)
        dz_ref[:, 2 * D_MODEL + POOL_WIDTH:3 * D_MODEL] = (
            dyb * (ypre * ps) * (sgb * (1.0 + gb * (1.0 - sgb)))).astype(BF16)
        vec_ref[_bag_row("pool_scale"), 0:POOL_WIDTH] += jnp.sum(dyp * ypre, axis=0, keepdims=True)
        dypre = dyp * ps
        for g, k in enumerate(POOL_WINDOWS):
            cols = slice(g * POOL_GROUP_DIM, (g + 1) * POOL_GROUP_DIM)
            dyg = dypre[:, cols].astype(BF16)
            mat_ref[mat_rows("pool_w", g), :] += _dot_tn(diff[g].astype(BF16), dyg)
            ddiff = _dot_nt(dyg, pw_ref[g])
            count = jnp.minimum(pos + 1, k).astype(F32)
            dwin = ddiff / count
            dwin_ext[0:tb, cols] = dwin
            s = dwin_ext[:, cols]
            for step_ in range(g + 1):
                s = s + _shift_up(s, 2 ** step_)
            dz_ref[:, 2 * D_MODEL + g * POOL_GROUP_DIM:2 * D_MODEL + (g + 1) * POOL_GROUP_DIM] = (
                s[0:tb, :] - ddiff).astype(BF16)
            dwin_ext[tb:, cols] = dwin[0:POOL_HIST, :]

        dz_ref[:, 3 * D_MODEL:] = dzm_ref[...]

        @pl.when(i == nb - 1)
        def _():
            row = _bag_row("lru_lambda")
            vec_ref[row, :] = vec_ref[row, :] * (-_sigmoid(-lam))

    rev = lambda i: (nb - 1 - i, 0)
    fixed = lambda i: (0, 0)

    def prev(rows, col):
        per = tb // rows
        return lambda i: (jnp.maximum((nb - 1 - i) * per - 1, 0), col)

    in_specs = [pl.BlockSpec((tb, D_MODEL), lambda i: (nb - 1 - i, 0)),
                pl.BlockSpec((CONV_HIST, D_MODEL), prev(CONV_HIST, 0)),
                pl.BlockSpec((tb, D_MODEL), lambda i: (nb - 1 - i, 1)),
                pl.BlockSpec((tb, POOL_WIDTH), lambda i: (nb - 1 - i, 4)),
                pl.BlockSpec((POOL_HIST, POOL_WIDTH), prev(POOL_HIST, 4)),
                pl.BlockSpec((tb, POOL_WIDTH), lambda i: (nb - 1 - i, 5)),
                pl.BlockSpec((tb, D_MODEL), rev),
                pl.BlockSpec((F32_SUBLANES, D_MODEL), prev(F32_SUBLANES, 0)),
                pl.BlockSpec((tb, D_MODEL), rev), pl.BlockSpec((tb, POOL_WIDTH), rev),
                pl.BlockSpec((tb, 2 * D_MODEL), rev)] + [pl.BlockSpec((tb, D_MODEL), rev)] * len(saved) + _branch_specs(
                    tb, rev, fixed) + [
                    pl.BlockSpec((VEC_BAG_ROWS, D_MODEL), fixed)]
    out_shape = (jax.ShapeDtypeStruct((t, IN_COLS), BF16), jax.ShapeDtypeStruct((VEC_BAG_ROWS, D_MODEL), F32),
                 jax.ShapeDtypeStruct((MAT_BAG_ROWS, HEAD_DIM), F32))
    out_specs = (pl.BlockSpec((tb, IN_COLS), rev), pl.BlockSpec((VEC_BAG_ROWS, D_MODEL), fixed),
                 pl.BlockSpec((MAT_BAG_ROWS, HEAD_DIM), fixed))
    scratch = [pltpu.VMEM((tb + CONV_HIST, D_MODEL), F32), pltpu.VMEM((tb + POOL_HIST, POOL_WIDTH), F32),
               pltpu.VMEM((tb + F32_SUBLANES, D_MODEL), F32), pltpu.VMEM((tb + F32_SUBLANES, D_MODEL), F32),
               pltpu.VMEM((tb + CONV_HIST, D_MODEL), F32), pltpu.VMEM((tb + POOL_HIST, POOL_WIDTH), F32),
               pltpu.VMEM((F32_SUBLANES, D_MODEL), F32),
               pltpu.VMEM((tb, D_MODEL), F32), pltpu.VMEM((tb, D_MODEL), F32), pltpu.VMEM((tb, D_MODEL), F32)]
    return pl.pallas_call(
        body, name="branches_bwd", out_shape=out_shape, grid=(nb,), in_specs=in_specs, out_specs=out_specs,
        scratch_shapes=scratch, input_output_aliases={len(in_specs) - 1: 1},
        compiler_params=pltpu.CompilerParams(dimension_semantics=("arbitrary",),
                                             vmem_limit_bytes=VMEM_LIMIT_BYTES),
    )(z, z, z, z, z, z, hl, hl, dya, dyb, dzm, *saved, *weights, vec_bag)


def _merge_head(x2d, ya, yb, z, p2d, tgt, w_pl, w_pp, w_out, w_pg, w_pe, g2, gf, tb):
    t = x2d.shape[0]
    p_dim = p2d.shape[1]

    def body(x_ref, ya_ref, yb_ref, ma_ref, mb_ref, p_ref, t_ref, wpl_ref, wpp_ref, wout_ref, wpg_ref, wpe_ref,
             g2_ref, gf_ref,
             bag_ref, dxr_ref, dya_ref, dyb_ref, dzm_ref,
             mg_ref, do_ref, hn_ref, dgp_ref, dpe_ref, da_ref, dbm_ref, pbf_ref):
        @pl.when(pl.program_id(0) == 0)
        def _():
            bag_ref[...] = jnp.zeros_like(bag_ref)

        a_ = _dot(ya_ref[...], wpl_ref[...])
        bm = _dot(yb_ref[...], wpp_ref[...])
        sa = _sigmoid(ma_ref[...])
        sb = _sigmoid(mb_ref[...])
        mg = (sa * a_ + sb * bm).astype(BF16)
        mg_ref[...] = mg
        x1 = x_ref[...] + _dot(mg, wout_ref[...])
        xn2, r2 = _rms(x1)
        g2 = g2_ref[...]
        hn = (xn2 * g2).astype(BF16)
        hn_ref[...] = hn
        gate = _sigmoid(_dot(hn, wpg_ref[...]))
        pbf = p_ref[...].astype(BF16)
        pbf_ref[...] = pbf
        pe = _dot(pbf, wpe_ref[...])
        x2 = x1 + gate * pe
        xn3, r3 = _rms(x2)
        gf = gf_ref[...]
        err = xn3 * gf - t_ref[...]
        bag_ref[_bag_rows("loss"), 0:128] += 0.5 * jnp.sum(jnp.mean(err * err, axis=-1))

        dy = err * (1.0 / D_MODEL)
        bag_ref[_bag_row("final_g"), :] += jnp.sum(dy * xn3, axis=0, keepdims=True)
        dx2 = _rms_bwd(dy * gf, xn3, r3)
        dpe_ref[...] = (dx2 * gate).astype(BF16)
        dgp = ((dx2 * pe) * (gate * (1.0 - gate))).astype(BF16)
        dgp_ref[...] = dgp
        dhn = _dot_nt(dgp, wpg_ref[...])
        bag_ref[_bag_row("ple_norm_g"), :] += jnp.sum(dhn * xn2, axis=0, keepdims=True)
        dx1 = dx2 + _rms_bwd(dhn * g2, xn2, r2)
        dxr_ref[...] = dx1
        do = dx1.astype(BF16)
        do_ref[...] = do
        dmg = _dot_nt(do, wout_ref[...])
        da = (dmg * sa).astype(BF16)
        dbm = (dmg * sb).astype(BF16)
        da_ref[...] = da
        dbm_ref[...] = dbm
        dzm_ref[:, 0:D_MODEL] = (dmg * a_ * (sa * (1.0 - sa))).astype(BF16)
        dzm_ref[:, D_MODEL:] = (dmg * bm * (sb * (1.0 - sb))).astype(BF16)
        dya_ref[...] = _dot_nt(da, wpl_ref[...])
        dyb_ref[...] = _dot_nt(dbm, wpp_ref[...])

    row = lambda i: (i, 0)
    fixed = lambda i: (0, 0)

    def resident(shape):
        return pl.BlockSpec(shape, fixed, pipeline_mode=pl.Buffered(1))

    tok = lambda width: pl.BlockSpec((tb, width), row)
    in_specs = [tok(D_MODEL), tok(D_MODEL), tok(POOL_WIDTH),
                pl.BlockSpec((tb, D_MODEL), lambda i: (i, 3)), pl.BlockSpec((tb, D_MODEL), lambda i: (i, 4)),
                tok(p_dim), tok(D_MODEL),
                resident((D_MODEL, D_MODEL)), resident((POOL_WIDTH, D_MODEL)), resident((D_MODEL, D_MODEL)),
                resident((D_MODEL, D_MODEL)), resident((p_dim, D_MODEL)),
                pl.BlockSpec((1, D_MODEL), fixed), pl.BlockSpec((1, D_MODEL), fixed)]
    bf = lambda width: jax.ShapeDtypeStruct((t, width), BF16)
    f32 = lambda width: jax.ShapeDtypeStruct((t, width), F32)
    out_shape = (jax.ShapeDtypeStruct((VEC_BAG_ROWS, D_MODEL), F32),
                 f32(D_MODEL), f32(D_MODEL), f32(POOL_WIDTH), bf(2 * D_MODEL),
                 bf(D_MODEL), bf(D_MODEL), bf(D_MODEL), bf(D_MODEL), bf(D_MODEL), bf(D_MODEL), bf(D_MODEL), bf(p_dim))
    out_specs = (pl.BlockSpec((VEC_BAG_ROWS, D_MODEL), fixed),
                 tok(D_MODEL), tok(D_MODEL), tok(POOL_WIDTH), tok(2 * D_MODEL),
                 tok(D_MODEL), tok(D_MODEL), tok(D_MODEL), tok(D_MODEL), tok(D_MODEL), tok(D_MODEL), tok(D_MODEL),
                 tok(p_dim))
    return pl.pallas_call(
        body, name="merge_head", out_shape=out_shape, grid=(t // tb,), in_specs=in_specs, out_specs=out_specs,
        compiler_params=pltpu.CompilerParams(dimension_semantics=("arbitrary",),
                                             vmem_limit_bytes=VMEM_LIMIT_BYTES),
    )(x2d, ya, yb, z, z, p2d, tgt, w_pl, w_pp, w_out, w_pg, w_pe, g2, gf)


def kernel(x, p, norm_g, w_in, conv_w, conv_b, lru_w_a, lru_b_a, lru_w_x, lru_b_x, lru_lambda, pool_w, pool_scale, w_proj_lru, w_proj_pool, w_out, ple_norm_g, w_ple_gate, w_ple_proj, final_g, loss_target, m_norm_g, m_w_in, m_conv_w, m_conv_b, m_lru_w_a, m_lru_b_a, m_lru_w_x, m_lru_b_x, m_lru_lambda, m_pool_w, m_pool_scale, m_w_proj_lru, m_w_proj_pool, m_w_out, m_ple_norm_g, m_w_ple_gate, m_w_ple_proj, m_final_g, v_norm_g, v_w_in, v_conv_w, v_conv_b, v_lru_w_a, v_lru_b_a, v_lru_w_x, v_lru_b_x, v_lru_lambda, v_pool_w, v_pool_scale, v_w_proj_lru, v_w_proj_pool, v_w_out, v_ple_norm_g, v_w_ple_gate, v_w_ple_proj, v_final_g):
    bsz, seq, _ = x.shape
    t = bsz * seq
    tb_mm = min(1024, seq)
    tb_seq = min(256, seq // 2) if seq >= 512 else seq
    x2d = x.reshape(t, D_MODEL)
    p2d = p.reshape(t, p.shape[-1])
    tgt = loss_target.reshape(t, D_MODEL)
    chip = 2 * lax.axis_index("x") + lax.axis_index("y")

    rest = [(w_proj_lru[0], 0), (w_proj_pool[0], 1), (w_out[0], 0), (w_ple_gate[0], 0), (w_ple_proj[0], 1)]
    z, h_bf, w_in_f, conv_w_f = _in_proj_gather(x2d, norm_g, w_in[0].astype(BF16), [(conv_w[0], 1, False)], tb_mm)

    wa_bf = lru_w_a[0].astype(BF16)
    wx_bf = lru_w_x[0].astype(BF16)
    pw_bf = pool_w[0].astype(BF16)
    branch_w = (conv_w_f, conv_b, wa_bf, lru_b_a.reshape(1, D_MODEL), wx_bf, lru_b_x.reshape(1, D_MODEL),
                lru_lambda, pw_bf, pool_scale)

    ya, yb, hl, *rest_out = _branches_fwd(
        z, branch_w, seq, tb_seq, [(w.astype(BF16), axis, True) for w, axis in rest])
    saved, (w_pl_f, w_pp_f, w_out_f, w_pg_f, w_pe_f) = rest_out[:5], rest_out[5:]
    (vec_bag, dx_res, dya, dyb, dzm, mg_bf, do_bf, hn_bf, dgp_bf, dpe_bf, da_bf, dbm_bf, p_bf) = _merge_head(
        x2d, ya, yb, z, p2d, tgt, w_pl_f, w_pp_f, w_out_f, w_pg_f, w_pe_f, ple_norm_g, final_g.reshape(1, D_MODEL),
        tb_seq)
    dz, vec_bag, mat_bag = _branches_bwd(z, hl, dya, dyb, dzm, saved, branch_w, vec_bag, seq, tb_seq)

    tb_dw = min(1024, seq)
    def proj_grad(lhs, rhs, name, cols):
        g32, g16 = _weight_grad(lhs, rhs, 1, tb_dw, name)
        if cols:
            return g32[0], True, g16[0]
        rows = g32.shape[1] // 8
        return g32.reshape(8, rows, g32.shape[2]), False, g16.reshape(8, rows, g32.shape[2])

    p_dim = p2d.shape[1]
    proj_parts = [proj_grad(ya, da_bf, "dw_proj_lru", False), proj_grad(yb, dbm_bf, "dw_proj_pool", True),
                  proj_grad(mg_bf, do_bf, "dw_out", False), proj_grad(hn_bf, dgp_bf, "dw_ple_gate", False),
                  proj_grad(p_bf, dpe_bf, "dw_ple_proj", True)]
    nb_dw = t // tb_dw
    g_in, g_in16, r_pl, r_pp, r_out, r_pg, r_pe, vec_mine, mat_mine = _weight_grad(
        h_bf, dz, N_CHIPS, tb_dw, "dw_in",
        reduce=(proj_parts + [(vec_bag.reshape(8, VEC_BAG_ROWS // 8, D_MODEL), False, None),
                              (mat_bag.reshape(8, MAT_BAG_ROWS // 8, HEAD_DIM), False, None)],
                [BF16] * 5 + [F32] * 2,
                (0, nb_dw // 2, 2 * nb_dw - 1, 3 * nb_dw + nb_dw // 2, N_CHIPS * nb_dw - 1)))
    pieces = (8, D_MODEL // 2, IN_COLS // N_CHIPS)
    nb_seq = t // tb_seq
    dx, d_g1, r_in, vec_sum, mat_sum = _in_proj_bwd(
        dz, w_in_f, x2d, dx_res, norm_g, tb_seq,
        reduce=([(g_in.reshape(pieces), False, g_in16.reshape(pieces))], BF16,
                (0, nb_seq // 8, nb_seq // 2, nb_seq - 1, nb_seq - 1)),
        shards=[(vec_mine.reshape(VEC_BAG_ROWS // N_CHIPS, D_MODEL), 0, True),
                (mat_mine.reshape(MAT_BAG_ROWS // N_CHIPS, HEAD_DIM), 0, True)])
    g_g1 = _all_reduce_tile(d_g1, "allreduce_norm_g")

    def big_update(w, g2d, m, v, rows, name):
        d, nm, nv = _adamw(w[0], g2d, m[0], v[0], rows, name)
        return g2d[None], d[None], nm[None], nv[None]

    u_in = big_update(w_in, r_in.reshape(D_MODEL, IN_COLS // N_CHIPS), m_w_in, v_w_in, 256, "adamw_w_in")
    u_pl = big_update(w_proj_lru, r_pl.reshape(D_MODEL // N_CHIPS, D_MODEL), m_w_proj_lru, v_w_proj_lru, 256, "adamw_w_proj_lru")
    u_pp = big_update(w_proj_pool, r_pp.reshape(POOL_WIDTH, D_MODEL // N_CHIPS), m_w_proj_pool, v_w_proj_pool, 512, "adamw_w_proj_pool")
    u_out = big_update(w_out, r_out.reshape(D_MODEL // N_CHIPS, D_MODEL), m_w_out, v_w_out, 256, "adamw_w_out")
    u_pg = big_update(w_ple_gate, r_pg.reshape(D_MODEL // N_CHIPS, D_MODEL), m_w_ple_gate, v_w_ple_gate, 256, "adamw_w_ple_gate")
    u_pe = big_update(w_ple_proj, r_pe.reshape(p_dim, D_MODEL // N_CHIPS), m_w_ple_proj, v_w_ple_proj, 256, "adamw_w_ple_proj")

    small = [("norm_g", norm_g, m_norm_g, v_norm_g), ("conv_b", conv_b, m_conv_b, v_conv_b),
             ("lru_w_a", lru_w_a, m_lru_w_a, v_lru_w_a), ("lru_b_a", lru_b_a, m_lru_b_a, v_lru_b_a),
             ("lru_w_x", lru_w_x, m_lru_w_x, v_lru_w_x), ("lru_b_x", lru_b_x, m_lru_b_x, v_lru_b_x),
             ("lru_lambda", lru_lambda, m_lru_lambda, v_lru_lambda), ("pool_w", pool_w, m_pool_w, v_pool_w),
             ("pool_scale", pool_scale, m_pool_scale, v_pool_scale),
             ("ple_norm_g", ple_norm_g, m_ple_norm_g, v_ple_norm_g), ("final_g", final_g, m_final_g, v_final_g)]

    def view(a):
        return a.reshape(-1, a.shape[-1]) if a.ndim != 3 else a[0]

    cw_at = F32_SUBLANES * VEC_BAG_SLOTS.index("conv_w")
    cw_cols = D_MODEL // N_CHIPS
    g_cw = lax.dynamic_slice(vec_sum, (cw_at, chip * cw_cols), (CONV_WIDTH, cw_cols))
    flat = _adamw_replicated(vec_sum, mat_sum, g_g1, [(name,) + tuple(view(a) for a in arrs) for name, *arrs in small],
                             (conv_w[0], m_conv_w[0], v_conv_w[0], g_cw))
    u_small = {name: tuple(flat[4 * k + pick].reshape(arrs[0].shape) for pick in range(4))
               for k, (name, *arrs) in enumerate(small)}
    u_cw = tuple(a[None] for a in (g_cw,) + tuple(flat[4 * len(small):]))

    loss = vec_sum[F32_SUBLANES * VEC_BAG_SLOTS.index("loss"), 0]
    grad_x = dx.reshape(bsz, seq, D_MODEL)

    def ordered(pick):
        s = {name: u[pick] for name, u in u_small.items()}
        return [s["norm_g"], u_in[pick], u_cw[pick], s["conv_b"], s["lru_w_a"], s["lru_b_a"], s["lru_w_x"], s["lru_b_x"],
                s["lru_lambda"], s["pool_w"], s["pool_scale"], u_pl[pick], u_pp[pick], u_out[pick], s["ple_norm_g"],
                u_pg[pick], u_pe[pick], s["final_g"]]

    return (loss, grad_x, *ordered(0), *ordered(1), *ordered(2), *ordered(3))
```

```python
import jax
import jax.numpy as jnp
from jax import lax
from jax.experimental import pallas as pl
from jax.experimental.pallas import tpu as pltpu

F32 = jnp.float32
BF16 = jnp.bfloat16
MESH = pl.DeviceIdType.MESH

D_MODEL = 1024
LRU_HEADS = 8
HEAD_DIM = 128
CONV_WIDTH = 4
LRU_C = 8.0
POOL_WIDTH = 512
POOL_WINDOWS = (2, 4, 8, 16)
POOL_GROUP_DIM = 128
IN_COLS = 5120
N_CHIPS = 4
EPS = 1e-6

ADAM_LR = 0.001
ADAM_B1 = 0.9
ADAM_B2 = 0.999
ADAM_EPS = 1e-08
ADAM_WD = 0.01
ADAM_STEP = 10

F32_SUBLANES = 8
CONV_HIST = 8
POOL_HIST = 16
VMEM_LIMIT_BYTES = 58 * 1024 * 1024
VEC_BAG_SLOTS = ("norm_g", "conv_w", "conv_b", "lru_b_a", "lru_b_x", "lru_lambda", "pool_scale", "ple_norm_g",
                 "final_g", "loss")
VEC_BAG_ROWS = 128
MAT_BAG_AT = {"lru_w_a": 0, "lru_w_x": LRU_HEADS * HEAD_DIM, "pool_w": 2 * LRU_HEADS * HEAD_DIM}
MAT_BAG_ROWS = 2 * LRU_HEADS * HEAD_DIM + len(POOL_WINDOWS) * POOL_GROUP_DIM


def _bag_row(name, k=0):
    at = F32_SUBLANES * VEC_BAG_SLOTS.index(name) + k
    return slice(at, at + 1)


def _bag_rows(name):
    at = F32_SUBLANES * VEC_BAG_SLOTS.index(name)
    return slice(at, at + F32_SUBLANES)


def _dot(a, b):
    return jnp.dot(a, b, preferred_element_type=F32)


def _dot_nt(a, b):
    return lax.dot_general(a, b, (((1,), (1,)), ((), ())), preferred_element_type=F32)


def _dot_tn(a, b):
    return lax.dot_general(a, b, (((0,), (0,)), ((), ())), preferred_element_type=F32)


def _sigmoid(v):
    return jax.nn.sigmoid(v)


def _softplus(v):
    return jnp.maximum(v, 0.0) + jnp.log1p(jnp.exp(-jnp.abs(v)))


def _place():
    return lax.axis_index("x"), lax.axis_index("y"), lax.axis_index("c")


GATHER_SEMS = 6


def _gather_shapes(shards):
    out_shape = []
    for arr, axis, _ in shards:
        r, cols = arr.shape
        out_shape.append(jax.ShapeDtypeStruct((N_CHIPS * r, cols) if axis == 0 else (r, N_CHIPS * cols), arr.dtype))
    n = len(shards)
    sems = [pltpu.SemaphoreType.DMA((n * GATHER_SEMS,)), pltpu.SemaphoreType.DMA((n * GATHER_SEMS,)),
            pltpu.SemaphoreType.DMA((n,))]
    return out_shape, sems


def _gather_steps(shards, ins, outs, send_sems, recv_sems, local_sems):
    n = len(shards)
    x, y, c = _place()
    me, sibling = (x, y, c), (x, y, 1 - c)
    chips = [(x, 1 - y), (1 - x, y), (1 - x, 1 - y)]

    def region(k, cx, cy, hc):
        (r, cols), axis = shards[k][0].shape, shards[k][1]
        j = 2 * cx + cy
        if axis == 0:
            if hc is None:
                return outs[k].at[pl.ds(j * r, r), :]
            return outs[k].at[pl.ds(j * r + hc * (r // 2), r // 2), :]
        if hc is None:
            return outs[k].at[:, pl.ds(j * cols, cols)]
        return outs[k].at[pl.ds(hc * (r // 2), r // 2), pl.ds(j * cols, cols)]

    def remote(k, sem, block, to, src=None):
        dst = region(k, *block)
        return pltpu.make_async_remote_copy(
            src_ref=dst if src is None else src, dst_ref=dst,
            send_sem=send_sems.at[k * GATHER_SEMS + sem], recv_sem=recv_sems.at[k * GATHER_SEMS + sem],
            device_id=to, device_id_type=MESH)

    def first(k, idx):
        r, split = shards[k][0].shape[0], shards[k][2]
        src = ins[k].at[pl.ds(c * (r // 2), r // 2), :] if split else ins[k]
        return remote(k, idx, (x, y, c if split else None), (*chips[idx], c), src=src)

    def relay(k):
        src_chip = (jnp.bitwise_xor(x, 1 - c), jnp.bitwise_xor(y, c))
        dst_chip = (jnp.bitwise_xor(x, c), jnp.bitwise_xor(y, 1 - c))
        return remote(k, 2, (*src_chip, c), (*dst_chip, c))

    def passed(k, idx):
        return remote(k, 3 + idx, (*chips[idx], c), sibling)

    def mine(k):
        return pltpu.make_async_copy(ins[k], region(k, x, y, None), local_sems.at[k])

    def start():
        for k in range(n):
            mine(k).start()
            for idx in range(2 if shards[k][2] else 3):
                first(k, idx).start()

    def relay_on():
        for k in range(n):
            split = shards[k][2]
            for idx in range(2):
                remote(k, idx, (*chips[idx], c if split else None), me).wait_recv()
            if split:
                relay(k).start()
                passed(k, 0).start()
                passed(k, 1).start()

    def finish():
        for k in range(n):
            split = shards[k][2]
            remote(k, 2, (*chips[2], c if split else None), me).wait_recv()
            if split:
                passed(k, 2).start()
        for k in range(n):
            if shards[k][2]:
                for idx in range(3):
                    remote(k, 3 + idx, (*chips[idx], 1 - c), me).wait_recv()
        for k in range(n):
            if shards[k][2]:
                for cp in (first(k, 0), first(k, 1), relay(k), passed(k, 0), passed(k, 1), passed(k, 2)):
                    cp.wait_send()
            else:
                for idx in range(3):
                    first(k, idx).wait_send()
            mine(k).wait()

    return start, relay_on, finish


RS_ADD_ROWS = (64, 32, 16, 8)


def _all_reduce_tile(v, name):
    n_dev = 2 * N_CHIPS
    flips = [(dx, dy, dc) for dx in (0, 1) for dy in (0, 1) for dc in (0, 1)][1:]

    def body(v_ref, o_ref, slots, send_sems, recv_sems):
        x, y, c = _place()
        mine = 4 * x + 2 * y + c

        def copy(k, to_flip, slot):
            dx, dy, dc = to_flip
            peer = (jnp.bitwise_xor(x, dx), jnp.bitwise_xor(y, dy), jnp.bitwise_xor(c, dc))
            return pltpu.make_async_remote_copy(
                src_ref=v_ref, dst_ref=slots.at[slot], send_sem=send_sems.at[k], recv_sem=recv_sems.at[k],
                device_id=peer, device_id_type=MESH)

        sends = [copy(k, flip, mine) for k, flip in enumerate(flips)]
        for cp in sends:
            cp.start()
        slots[mine] = v_ref[...]
        for k, (dx, dy, dc) in enumerate(flips):
            copy(k, (dx, dy, dc), jnp.bitwise_xor(mine, 4 * dx + 2 * dy + dc)).wait_recv()
        total = slots[0]
        for d in range(1, n_dev):
            total = total + slots[d]
        o_ref[...] = total
        for cp in sends:
            cp.wait_send()

    return pl.pallas_call(
        body, name=name, out_shape=jax.ShapeDtypeStruct(v.shape, F32),
        in_specs=[pl.BlockSpec(memory_space=pltpu.VMEM)], out_specs=pl.BlockSpec(memory_space=pltpu.VMEM),
        scratch_shapes=[pltpu.VMEM((n_dev,) + v.shape, F32), pltpu.SemaphoreType.DMA((n_dev - 1,)),
                        pltpu.SemaphoreType.DMA((n_dev - 1,))],
    )(v)


RS_SEMS = 8
RS_LOCAL_SEMS = 5


def _rs_piece_shape(part):
    arr, cols = part[0], part[1]
    return (arr.shape[0] // 2, arr.shape[1] // N_CHIPS) if cols else tuple(arr.shape[1:])


def _rs_operands(parts):
    return [p[0] for p in parts] + [p[0] if p[2] is None else p[2] for p in parts]


def _rs_wires(parts, wire):
    return list(wire) if isinstance(wire, (list, tuple)) else [wire] * len(parts)


def _rs_shapes(parts, wire):
    n = len(parts)
    shapes = [_rs_piece_shape(p) for p in parts]
    out_shape = [jax.ShapeDtypeStruct((2,) + s, F32) for s in shapes]
    scratch = []
    for lead, kind in ((N_CHIPS, "f32"), (N_CHIPS, "narrow"), (N_CHIPS, "wire"), (None, "f32"), (N_CHIPS, "wire")):
        for s, p, w in zip(shapes, parts, _rs_wires(parts, wire)):
            dtype = {"f32": F32, "narrow": F32 if p[2] is None else p[2].dtype, "wire": w}[kind]
            scratch.append(pltpu.VMEM(s if lead is None else (lead,) + s, dtype))
    scratch += [pltpu.SemaphoreType.DMA((n * RS_SEMS,)), pltpu.SemaphoreType.DMA((n * RS_SEMS,)),
                pltpu.SemaphoreType.DMA((n * RS_LOCAL_SEMS,))]
    return out_shape, scratch


def _rs_steps(parts, ins, outs, scratch):
    n = len(parts)
    own, sib, got, fin, snd = (scratch[k * n:(k + 1) * n] for k in range(5))
    send_sems, recv_sems, local_sems = scratch[5 * n:]
    shapes = [_rs_piece_shape(p) for p in parts]
    x, y, c = _place()
    j_me = 2 * x + y
    me, sibling = (x, y, c), (x, y, 1 - c)

    def piece(a, jj, core, narrow=False):
        ref = ins[n + a] if narrow else ins[a]
        if parts[a][1]:
            r, cl = shapes[a]
            return ref.at[pl.ds(core * r, r), pl.ds(jj * cl, cl)]
        return ref.at[2 * jj + core]

    def remote(a, sem, src, dst, to):
        return pltpu.make_async_remote_copy(
            src_ref=src, dst_ref=dst, send_sem=send_sems.at[a * RS_SEMS + sem],
            recv_sem=recv_sems.at[a * RS_SEMS + sem], device_id=to, device_id_type=MESH)

    def rows_loop(a, fn):
        r = shapes[a][0]
        step = max(s for s in RS_ADD_ROWS if r % s == 0)

        def it(i, carry):
            fn(pl.ds(pl.multiple_of(i * step, step), step))
            return carry

        lax.fori_loop(0, r // step, it, 0)

    def load(a, jj):
        return pltpu.make_async_copy(piece(a, jj, c), own[a].at[jj], local_sems.at[a * RS_LOCAL_SEMS + jj])

    def to_sibling(a, jj):
        return remote(a, jj, piece(a, jj, 1 - c, narrow=True), sib[a].at[jj], sibling)

    near = (jnp.bitwise_xor(x, 1 - c), jnp.bitwise_xor(y, c))
    far = (jnp.bitwise_xor(x, c), jnp.bitwise_xor(y, 1 - c))
    diag = (1 - x, 1 - y)
    FROM_NEAR, FROM_FAR, FEED = 0, 1, 2

    def chip_of(chip):
        return 2 * chip[0] + chip[1]

    def feed(a):
        return remote(a, 4, snd[a].at[chip_of(diag)], got[a].at[FEED], (*near, c))

    def to_near(a):
        return remote(a, 5, snd[a].at[chip_of(near)], got[a].at[FROM_NEAR], (*near, c))

    def to_far(a):
        return remote(a, 6, snd[a].at[chip_of(far)], got[a].at[FROM_FAR], (*far, c))

    def store(a):
        return pltpu.make_async_copy(fin[a], outs[a].at[c], local_sems.at[a * RS_LOCAL_SEMS + 4])

    def result_to_sibling(a):
        return remote(a, 7, fin[a], outs[a].at[c], sibling)

    def exchange():
        for a in range(n):
            for jj in range(N_CHIPS):
                load(a, jj).start()
                to_sibling(a, jj).start()

    def chip_sums():
        for a in range(n):
            for jj in range(N_CHIPS):
                load(a, jj).wait()
                remote(a, jj, sib[a].at[jj], sib[a].at[jj], me).wait_recv()

                def add(sl, a=a, jj=jj):
                    q = own[a][jj, sl, :] + sib[a][jj, sl, :].astype(F32)
                    own[a][jj, sl, :] = q
                    snd[a][jj, sl, :] = q.astype(snd[a].dtype)

                rows_loop(a, add)
        for a in range(n):
            feed(a).start()
        for a in range(n):
            to_near(a).start()

    def relay():
        for a in range(n):
            remote(a, 4, got[a].at[FEED], got[a].at[FEED], me).wait_recv()

            def add(sl, a=a):
                pair = own[a][chip_of(far), sl, :] + got[a][FEED, sl, :].astype(F32)
                snd[a][chip_of(far), sl, :] = pair.astype(snd[a].dtype)

            rows_loop(a, add)
            to_far(a).start()

    def totals():
        for a in range(n):
            remote(a, 5, got[a].at[FROM_NEAR], got[a].at[FROM_NEAR], me).wait_recv()
            remote(a, 6, got[a].at[FROM_FAR], got[a].at[FROM_FAR], me).wait_recv()

            def total(sl, a=a):
                fin[a][sl, :] = (own[a][j_me, sl, :] + got[a][FROM_NEAR, sl, :].astype(F32)) + (
                    got[a][FROM_FAR, sl, :].astype(F32))

            rows_loop(a, total)
            store(a).start()
            result_to_sibling(a).start()

    def finish():
        for a in range(n):
            remote(a, 7, outs[a].at[1 - c], outs[a].at[1 - c], me).wait_recv()
        for a in range(n):
            for jj in range(N_CHIPS):
                to_sibling(a, jj).wait_send()
            for cp in (feed(a), to_near(a), to_far(a), result_to_sibling(a)):
                cp.wait_send()
            store(a).wait()

    return exchange, chip_sums, relay, totals, finish


def _rms(x):
    r = lax.rsqrt(jnp.mean(x * x, axis=-1, keepdims=True) + EPS)
    return x * r, r


def _rms_bwd(dxn, xn, r):
    return r * (dxn - xn * jnp.mean(dxn * xn, axis=-1, keepdims=True))


def _in_proj_gather(x2d, norm_g, w_in_sh, shards, tb):
    t = x2d.shape[0]
    nb = t // tb
    cols = IN_COLS // N_CHIPS
    half = D_MODEL // 2
    n = len(shards)

    def body(x_ref, g_ref, win_ref, *refs):
        ins = refs[:n]
        z_ref, h_ref, wfull_ref = refs[n:n + 3]
        outs = refs[n + 3:2 * n + 3]
        wv, h_buf, send_sems, recv_sems, local_sems, w_send, w_recv, w_local = refs[2 * n + 3:]
        s, i = pl.program_id(0), pl.program_id(1)
        x, y, c = _place()
        me, sibling = (x, y, c), (x, y, 1 - c)
        chips = [(x, 1 - y), (1 - x, y), (1 - x, 1 - y)]

        def w_half(cx, cy, hc):
            return wv.at[2 * cx + cy, pl.ds(hc * half, half), :]

        def w_remote(sem, block, to, src=None):
            dst = w_half(*block)
            return pltpu.make_async_remote_copy(
                src_ref=dst if src is None else src, dst_ref=dst, send_sem=w_send.at[sem],
                recv_sem=w_recv.at[sem], device_id=to, device_id_type=MESH)

        def w_first(idx):
            return w_remote(idx, (x, y, c), (*chips[idx], c), src=win_ref.at[pl.ds(c * half, half), :])

        def w_relay():
            src_chip = (jnp.bitwise_xor(x, 1 - c), jnp.bitwise_xor(y, c))
            dst_chip = (jnp.bitwise_xor(x, c), jnp.bitwise_xor(y, 1 - c))
            return w_remote(2, (*src_chip, c), (*dst_chip, c))

        def w_pass(idx):
            return w_remote(3 + idx, (*chips[idx], c), sibling)

        def w_store(k, cx, cy):
            jj = 2 * cx + cy
            return pltpu.make_async_copy(wv.at[jj], wfull_ref.at[:, pl.ds(jj * cols, cols)], w_local.at[k])

        start_rest, relay_rest, finish_rest = _gather_steps(shards, ins, outs, send_sems, recv_sems, local_sems)
        own = pltpu.make_async_copy(win_ref, wv.at[2 * x + y], w_local.at[4])

        @pl.when((s == 0) & (i == 0))
        def _():
            own.start()
            w_first(0).start()
            w_first(1).start()
            start_rest()
            own.wait()
            w_store(0, x, y).start()

        @pl.when((s == 1) & (i == 0))
        def _():
            w_remote(0, (*chips[0], c), me).wait_recv()
            w_remote(1, (*chips[1], c), me).wait_recv()
            w_relay().start()
            w_pass(0).start()
            w_pass(1).start()
            w_remote(3, (*chips[0], 1 - c), me).wait_recv()
            w_store(1, *chips[0]).start()

        @pl.when((s == 2) & (i == 0))
        def _():
            w_remote(4, (*chips[1], 1 - c), me).wait_recv()
            w_store(2, *chips[1]).start()

        @pl.when((s == 3) & (i == 0))
        def _():
            w_remote(2, (*chips[2], c), me).wait_recv()
            w_pass(2).start()
            w_remote(5, (*chips[2], 1 - c), me).wait_recv()
            w_store(3, *chips[2]).start()

        xn, _ = _rms(x_ref[...])
        h = (xn * g_ref[...]).astype(BF16)
        keep_h = pltpu.make_async_copy(h_buf, h_ref.at[pl.ds(pl.multiple_of(i * tb, tb), tb), :], w_local.at[5])

        @pl.when(s == 0)
        def _():
            h_buf[...] = h
            keep_h.start()

        z_ref[...] = _dot(h, wv[jnp.bitwise_xor(2 * x + y, s)])
        pl.when(s == 0)(keep_h.wait)

        @pl.when((s == N_CHIPS - 1) & (i == nb - 1))
        def _():
            relay_rest()
            finish_rest()
            for cp in (w_first(0), w_first(1), w_relay(), w_pass(0), w_pass(1), w_pass(2)):
                cp.wait_send()
            w_store(0, x, y).wait()
            for idx in range(3):
                w_store(idx + 1, *chips[idx]).wait()

    rest_shape, rest_sems = _gather_shapes(shards)
    out_shape = [jax.ShapeDtypeStruct((t, IN_COLS), F32), jax.ShapeDtypeStruct((t, D_MODEL), BF16),
                 jax.ShapeDtypeStruct((D_MODEL, IN_COLS), BF16)] + rest_shape
    any_spec = pl.BlockSpec(memory_space=pl.ANY)

    def z_map(s, i):
        return (i, jnp.bitwise_xor(2 * lax.axis_index("x") + lax.axis_index("y"), s))

    return pl.pallas_call(
        body, name="in_proj", out_shape=tuple(out_shape),
        grid=(N_CHIPS, nb),
        in_specs=[pl.BlockSpec((tb, D_MODEL), lambda s, i: (i, 0)),
                  pl.BlockSpec((1, D_MODEL), lambda s, i: (0, 0)), any_spec] + [any_spec] * n,
        out_specs=tuple([pl.BlockSpec((tb, cols), z_map), any_spec, any_spec] + [any_spec] * n),
        scratch_shapes=[pltpu.VMEM((N_CHIPS, D_MODEL, cols), BF16), pltpu.VMEM((tb, D_MODEL), BF16)] + rest_sems + [
            pltpu.SemaphoreType.DMA((GATHER_SEMS,)), pltpu.SemaphoreType.DMA((GATHER_SEMS,)),
            pltpu.SemaphoreType.DMA((N_CHIPS + 2,))],
        compiler_params=pltpu.CompilerParams(dimension_semantics=("arbitrary", "arbitrary"),
                                             vmem_limit_bytes=VMEM_LIMIT_BYTES),
    )(x2d, norm_g, w_in_sh, *[sh[0] for sh in shards])


def _in_proj_bwd(dz, w_in, x2d, dx_res, norm_g, tb, reduce, shards):
    t = x2d.shape[0]
    nb = t // tb
    parts, wire, steps = reduce
    n = len(parts)
    k = len(shards)

    def body(dz_ref, w_ref, x_ref, dres_ref, g_ref, *refs):
        at = 2 * n + k
        dx_ref, dg_ref = refs[at:at + 2]
        rs_outs, g_outs = refs[at + 2:at + 2 + n], refs[at + 2 + n:at + 2 + n + k]
        scratch = refs[at + 2 + n + k:]
        rs = _rs_steps(parts, refs[:2 * n], rs_outs, scratch[:len(scratch) - 3])
        for step, when in zip(rs, steps):
            pl.when(pl.program_id(0) == when)(step)
        gather = _gather_steps(shards, refs[2 * n:at], g_outs, *scratch[len(scratch) - 3:])
        for step, when in zip(gather, (0, nb // 2, nb - 1)):
            pl.when(pl.program_id(0) == when)(step)

        @pl.when(pl.program_id(0) == 0)
        def _():
            dg_ref[...] = jnp.zeros_like(dg_ref)

        xn, r = _rms(x_ref[...])
        g = g_ref[...]
        dh = _dot_nt(dz_ref[...], w_ref[...])
        dg_ref[0:1, :] += jnp.sum(dh * xn, axis=0, keepdims=True)
        dx_ref[...] = dres_ref[...] + _rms_bwd(dh * g, xn, r)

    row = lambda i: (i, 0)
    fixed = lambda i: (0, 0)
    rs_shape, rs_scratch = _rs_shapes(parts, wire)
    g_shape, g_sems = _gather_shapes(shards)
    any_spec = pl.BlockSpec(memory_space=pl.ANY)
    return pl.pallas_call(
        body, name="in_proj_bwd",
        out_shape=tuple([jax.ShapeDtypeStruct((t, D_MODEL), F32), jax.ShapeDtypeStruct((F32_SUBLANES, D_MODEL), F32)]
                        + rs_shape + g_shape),
        grid=(nb,),
        in_specs=[pl.BlockSpec((tb, IN_COLS), row),
                  pl.BlockSpec((D_MODEL, IN_COLS), fixed, pipeline_mode=pl.Buffered(1)),
                  pl.BlockSpec((tb, D_MODEL), row), pl.BlockSpec((tb, D_MODEL), row),
                  pl.BlockSpec((1, D_MODEL), fixed)] + [any_spec] * (2 * n + k),
        out_specs=tuple([pl.BlockSpec((tb, D_MODEL), row), pl.BlockSpec((F32_SUBLANES, D_MODEL), fixed)]
                        + [any_spec] * (n + k)),
        scratch_shapes=rs_scratch + g_sems,
        compiler_params=pltpu.CompilerParams(dimension_semantics=("arbitrary",),
                                             vmem_limit_bytes=VMEM_LIMIT_BYTES),
    )(dz, w_in, x2d, dx_res, norm_g, *_rs_operands(parts), *[sh[0] for sh in shards])


def _weight_grad(lhs, rhs, n_chunks, tb, name, reduce=None):
    lhs, lslab, k = lhs if isinstance(lhs, tuple) else (lhs, 0, lhs.shape[1])
    rhs, rslab, n_cols = rhs if isinstance(rhs, tuple) else (rhs, 0, rhs.shape[1])
    t = lhs.shape[0]
    nc = n_cols // n_chunks
    nb = t // tb
    parts, wire, steps = reduce if reduce is not None else ([], F32, ())
    n = len(parts)

    def body(l_ref, r_ref, *refs):
        o_ref, o16_ref = refs[2 * n:2 * n + 2]
        if n:
            at = pl.program_id(0) * nb + pl.program_id(1)
            rs = _rs_steps(parts, refs[:2 * n], refs[2 * n + 2:3 * n + 2], refs[3 * n + 2:])
            for step, when in zip(rs, steps):
                pl.when(at == when)(step)

        @pl.when(pl.program_id(1) == 0)
        def _():
            o_ref[...] = jnp.zeros_like(o_ref)

        o_ref[...] += _dot_tn(l_ref[...], r_ref[...])

        @pl.when(pl.program_id(1) == nb - 1)
        def _():
            o16_ref[...] = o_ref[...].astype(BF16)

    rs_shape, rs_scratch = _rs_shapes(parts, wire) if n else ([], [])
    any_spec = pl.BlockSpec(memory_space=pl.ANY)
    chunk = pl.BlockSpec((None, k, nc), lambda j, i: (j, 0, 0))
    return pl.pallas_call(
        body, name=name,
        out_shape=tuple([jax.ShapeDtypeStruct((n_chunks, k, nc), F32), jax.ShapeDtypeStruct((n_chunks, k, nc), BF16)]
                        + rs_shape),
        grid=(n_chunks, nb),
        in_specs=[pl.BlockSpec((tb, k), lambda j, i: (i, lslab)),
                  pl.BlockSpec((tb, nc), lambda j, i: (i, rslab * n_chunks + j))] + [any_spec] * (2 * n),
        out_specs=tuple([chunk, chunk] + [any_spec] * n),
        scratch_shapes=rs_scratch,
        compiler_params=pltpu.CompilerParams(dimension_semantics=("arbitrary", "arbitrary"),
                                             vmem_limit_bytes=VMEM_LIMIT_BYTES),
    )(lhs, rhs, *_rs_operands(parts))


def _adam_update(w, g, m, v):
    m_ = ADAM_B1 * m + (1.0 - ADAM_B1) * g
    v_ = ADAM_B2 * v + (1.0 - ADAM_B2) * jnp.square(g)
    m_hat = m_ / (1.0 - ADAM_B1 ** ADAM_STEP)
    v_hat = v_ / (1.0 - ADAM_B2 ** ADAM_STEP)
    return -ADAM_LR * (m_hat / (jnp.sqrt(v_hat) + ADAM_EPS) + ADAM_WD * w), m_, v_


def _adamw_replicated(vec_sum, mat_sum, norm_grad, entries, conv):
    n = len(entries)

    def grad_of(name, shape, vec_ref, mat_ref, norm_ref):
        if name == "norm_g":
            return norm_ref[0:1, :]
        if name in MAT_BAG_AT:
            return mat_ref[MAT_BAG_AT[name]:MAT_BAG_AT[name] + shape[0], :]
        if shape[0] == 1:
            return vec_ref[_bag_row(name), 0:shape[1]]
        return jnp.concatenate([vec_ref[_bag_row(name), h * shape[1]:(h + 1) * shape[1]] for h in range(shape[0])],
                               axis=0)

    def body(vec_ref, mat_ref, norm_ref, *refs):
        ins, outs = refs[:3 * n + 4], refs[3 * n + 4:]
        for k in range(n):
            w_ref, m_ref, v_ref = ins[3 * k:3 * k + 3]
            g = grad_of(entries[k][0], w_ref.shape, vec_ref, mat_ref, norm_ref)
            d, m_, v_ = _adam_update(w_ref[...], g, m_ref[...], v_ref[...])
            for ref, val in zip(outs[4 * k:4 * k + 4], (g, d, m_, v_)):
                ref[...] = val
        w_ref, m_ref, v_ref, g_ref = ins[3 * n:]
        for ref, val in zip(outs[4 * n:], _adam_update(w_ref[...], g_ref[...], m_ref[...], v_ref[...])):
            ref[...] = val

    arrays = [a for e in entries for a in e[1:]] + list(conv)
    out_shape = [jax.ShapeDtypeStruct(e[1].shape, F32) for e in entries for _ in range(4)]
    out_shape += [jax.ShapeDtypeStruct(conv[0].shape, F32)] * 3
    return pl.pallas_call(
        body, name="adamw_replicated", out_shape=tuple(out_shape),
        compiler_params=pltpu.CompilerParams(vmem_limit_bytes=VMEM_LIMIT_BYTES),
    )(vec_sum, mat_sum, norm_grad, *arrays)


def _adamw(w, g, m, v, rows, name):
    r, c = w.shape

    def body(w_ref, g_ref, m_ref, v_ref, d_ref, nm_ref, nv_ref):
        d_ref[...], nm_ref[...], nv_ref[...] = _adam_update(w_ref[...], g_ref[...], m_ref[...], v_ref[...])

    spec = pl.BlockSpec((rows, c), lambda i: (i, 0))
    return pl.pallas_call(
        body, name=name, out_shape=tuple(jax.ShapeDtypeStruct((r, c), F32) for _ in range(3)),
        grid=(r // rows,), in_specs=[spec] * 4, out_specs=(spec,) * 3,
        compiler_params=pltpu.CompilerParams(dimension_semantics=("arbitrary",),
                                             vmem_limit_bytes=VMEM_LIMIT_BYTES),
    )(w, g, m, v)


def _shift_down(ext, s):
    return pltpu.roll(ext, s, 0)


def _tile_shift(v, s):
    rows, cols = v.shape
    tiles = v.reshape(rows // F32_SUBLANES, F32_SUBLANES, cols)
    return pltpu.roll(tiles, s % F32_SUBLANES, 1).reshape(rows, cols)


def _shift_up(ext, s):
    return pltpu.roll(ext, ext.shape[0] - s, 0)


def _lru_gates(xc, wa_ref, ba, wx_ref, bx, lam):
    pa, px = [], []
    for h in range(LRU_HEADS):
        xh = xc[:, h * HEAD_DIM:(h + 1) * HEAD_DIM].astype(BF16)
        pa.append(_dot(xh, wa_ref[h]))
        px.append(_dot(xh, wx_ref[h]))
    r = _sigmoid(jnp.concatenate(pa, axis=1) + ba)
    ig = _sigmoid(jnp.concatenate(px, axis=1) + bx)
    sp = _softplus(-lam)
    log_a = (-LRU_C * r) * sp
    a = jnp.exp(log_a)
    mult = jnp.sqrt(jnp.tanh(-log_a) * (1.0 + a * a))
    return r, ig, a, mult, sp


def _conv(ext, w_ref, b):
    y = b + _shift_down(ext, 3) * w_ref[0:1, :]
    y = y + _shift_down(ext, 2) * w_ref[1:2, :]
    y = y + _shift_down(ext, 1) * w_ref[2:3, :]
    y = y + ext * w_ref[3:4, :]
    return y[CONV_HIST:, :]


def _pool_diff(ext, pos):
    out = []
    for g, k in enumerate(POOL_WINDOWS):
        col = ext[:, g * POOL_GROUP_DIM:(g + 1) * POOL_GROUP_DIM]
        s = col
        for step in range(g + 1):
            s = s + _shift_down(s, 2 ** step)
        count = jnp.minimum(pos + 1, k).astype(F32)
        out.append(s[POOL_HIST:, :] / count - col[POOL_HIST:, :])
    return out


def _pool_mix(diff, pw_ref):
    return jnp.concatenate([_dot(diff[g].astype(BF16), pw_ref[g]) for g in range(len(POOL_WINDOWS))], axis=1)


def _branch_specs(tb, row_map, fixed):
    fixed3 = lambda i: (0, 0, 0)
    return [pl.BlockSpec((CONV_WIDTH, D_MODEL), fixed), pl.BlockSpec((1, D_MODEL), fixed),
            pl.BlockSpec((LRU_HEADS, HEAD_DIM, HEAD_DIM), fixed3), pl.BlockSpec((1, D_MODEL), fixed),
            pl.BlockSpec((LRU_HEADS, HEAD_DIM, HEAD_DIM), fixed3), pl.BlockSpec((1, D_MODEL), fixed),
            pl.BlockSpec((1, D_MODEL), fixed),
            pl.BlockSpec((len(POOL_WINDOWS), POOL_GROUP_DIM, POOL_GROUP_DIM), fixed3),
            pl.BlockSpec((1, POOL_WIDTH), fixed)]


def _branches_fwd(z, weights, seq, tb, shards):
    t = z.shape[0]
    nb = t // tb
    nbe = seq // tb
    groups = tb // F32_SUBLANES
    n = len(shards)

    def body(xa_ref, ga_ref, xb_ref, gb_ref, cw_ref, cb_ref, wa_ref, ba_ref, wx_ref, bx_ref, lam_ref,
             pw_ref, ps_ref, *refs):
        g_ins = refs[:n]
        ya_ref, yb_ref, hl_ref = refs[n:n + 3]
        g_outs = refs[n + 3:2 * n + 3]
        xa_ext, xb_ext, carry, a_s, u_s, send_sems, recv_sems, local_sems = refs[2 * n + 3:]
        blk = pl.program_id(0) % nbe
        start_gather, relay_gather, finish_gather = _gather_steps(shards, g_ins, g_outs, send_sems, recv_sems,
                                                                  local_sems)
        pl.when(pl.program_id(0) == 0)(start_gather)
        pl.when(pl.program_id(0) == nb // 2)(relay_gather)

        @pl.when(blk == 0)
        def _():
            xa_ext[0:CONV_HIST, :] = jnp.zeros((CONV_HIST, D_MODEL), F32)
            xb_ext[0:POOL_HIST, :] = jnp.zeros((POOL_HIST, POOL_WIDTH), F32)
            carry[...] = jnp.zeros_like(carry)

        xa_ext[CONV_HIST:, :] = xa_ref[...]
        xb_ext[POOL_HIST:, :] = xb_ref[...]
        ea = xa_ext[...]
        eb = xb_ext[...]
        xa_ext[0:CONV_HIST, :] = ea[tb:, :]
        xb_ext[0:POOL_HIST, :] = eb[tb:, :]

        xc = _conv(ea, cw_ref, cb_ref[...])
        _, ig, a, mult, _ = _lru_gates(xc, wa_ref, ba_ref[...], wx_ref, bx_ref[...], lam_ref[...])
        u = mult * (ig * xc)
        row8 = lax.broadcasted_iota(jnp.int32, (tb, D_MODEL), 0) % F32_SUBLANES
        for s in (1, 2, 4):
            m = row8 >= s
            u = jnp.where(m, a * _tile_shift(u, s) + u, u)
            a = jnp.where(m, a * _tile_shift(a, s), a)
        a_s[...] = a
        u_s[...] = u

        def step(g, cr):
            sl = pl.ds(pl.multiple_of(g * F32_SUBLANES, F32_SUBLANES), F32_SUBLANES)
            hb = a_s[sl, :] * cr + u_s[sl, :]
            hl_ref[sl, :] = hb
            return jnp.broadcast_to(hb[F32_SUBLANES - 1:F32_SUBLANES, :], (F32_SUBLANES, D_MODEL))

        carry[...] = lax.fori_loop(0, groups, step, carry[...], unroll=4)
        ga = ga_ref[...]
        ya_ref[...] = (hl_ref[...] * (ga * _sigmoid(ga))).astype(BF16)

        pos = blk * tb + lax.broadcasted_iota(jnp.int32, (tb, POOL_GROUP_DIM), 0)
        ypre = _pool_mix(_pool_diff(eb, pos), pw_ref)
        gb = gb_ref[...]
        yb_ref[...] = ((ypre * ps_ref[...]) * (gb * _sigmoid(gb))).astype(BF16)
        pl.when(pl.program_id(0) == nb - 1)(finish_gather)

    row = lambda i: (i, 0)
    fixed = lambda i: (0, 0)
    any_spec = pl.BlockSpec(memory_space=pl.ANY)
    in_specs = [pl.BlockSpec((tb, D_MODEL), lambda i: (i, 0)), pl.BlockSpec((tb, D_MODEL), lambda i: (i, 1)),
                pl.BlockSpec((tb, POOL_WIDTH), lambda i: (i, 4)), pl.BlockSpec((tb, POOL_WIDTH), lambda i: (i, 5)),
                ] + _branch_specs(tb, row, fixed) + [any_spec] * n
    g_shape, g_sems = _gather_shapes(shards)
    return pl.pallas_call(
        body, name="branches_fwd",
        out_shape=tuple([jax.ShapeDtypeStruct((t, D_MODEL), BF16), jax.ShapeDtypeStruct((t, POOL_WIDTH), BF16),
                         jax.ShapeDtypeStruct((t, D_MODEL), F32)] + g_shape),
        grid=(nb,), in_specs=in_specs,
        out_specs=tuple([pl.BlockSpec((tb, D_MODEL), row), pl.BlockSpec((tb, POOL_WIDTH), row),
                         pl.BlockSpec((tb, D_MODEL), row)] + [any_spec] * n),
        scratch_shapes=[pltpu.VMEM((tb + CONV_HIST, D_MODEL), F32), pltpu.VMEM((tb + POOL_HIST, POOL_WIDTH), F32),
                        pltpu.VMEM((F32_SUBLANES, D_MODEL), F32),
                        pltpu.VMEM((tb, D_MODEL), F32), pltpu.VMEM((tb, D_MODEL), F32)] + g_sems,
        compiler_params=pltpu.CompilerParams(dimension_semantics=("arbitrary",),
                                             vmem_limit_bytes=VMEM_LIMIT_BYTES),
    )(z, z, z, z, *weights, *[sh[0] for sh in shards])


def _branches_bwd(z, hl, dy, dzm, weights, vec_bag, seq, tb):
    t = z.shape[0]
    nb = t // tb
    nbe = seq // tb
    groups = tb // F32_SUBLANES

    def body(xa_ref, xap_ref, ga_ref, xb_ref, xbp_ref, gb_ref, hl_ref, hlp_ref, dya_ref, dyb_ref, dzm_ref,
             cw_ref, cb_ref, wa_ref, ba_ref, wx_ref, bx_ref, lam_ref, pw_ref, ps_ref, vec_in_ref,
             dz_ref, vec_ref, mat_ref,
             xa_ext, xb_ext, hl_ext, a_ext, dxc_ext, dwin_ext, g_carry, b_s, d_s, g_s):
        i = pl.program_id(0)
        blk = (nb - 1 - i) % nbe

        def mat_rows(name, k):
            at = MAT_BAG_AT[name] + k * HEAD_DIM
            return slice(at, at + HEAD_DIM)

        @pl.when(i == 0)
        def _():
            vec_ref[...] = vec_in_ref[...]
            mat_ref[...] = jnp.zeros_like(mat_ref)

        @pl.when(blk == nbe - 1)
        def _():
            a_ext[tb:, :] = jnp.zeros((F32_SUBLANES, D_MODEL), F32)
            dxc_ext[tb:, :] = jnp.zeros((CONV_HIST, D_MODEL), F32)
            dwin_ext[tb:, :] = jnp.zeros((POOL_HIST, POOL_WIDTH), F32)
            g_carry[...] = jnp.zeros_like(g_carry)

        live = (blk > 0).astype(F32)
        xa_ext[0:CONV_HIST, :] = xap_ref[...] * live
        xa_ext[CONV_HIST:, :] = xa_ref[...]
        xb_ext[0:POOL_HIST, :] = xbp_ref[...] * live
        xb_ext[POOL_HIST:, :] = xb_ref[...]
        hl_ext[0:F32_SUBLANES, :] = hlp_ref[...] * live
        hl_ext[F32_SUBLANES:, :] = hl_ref[...]
        ea = xa_ext[...]
        eb = xb_ext[...]

        xc = _conv(ea, cw_ref, cb_ref[...])
        lam = lam_ref[...]
        r, ig, a, mult, sp = _lru_gates(xc, wa_ref, ba_ref[...], wx_ref, bx_ref[...], lam)
        hl = hl_ref[...]
        ga = ga_ref[...]
        sga = _sigmoid(ga)
        dya = dya_ref[...]
        dhl = dya * (ga * sga)
        dz_ref[:, D_MODEL:2 * D_MODEL] = (dya * hl * (sga * (1.0 + ga * (1.0 - sga)))).astype(BF16)

        a_ext[0:tb, :] = a
        b = _shift_up(a_ext[...], 1)[0:tb, :]
        a_ext[tb:, :] = jnp.broadcast_to(a[0:1, :], (F32_SUBLANES, D_MODEL))
        d = dhl
        row8 = lax.broadcasted_iota(jnp.int32, (tb, D_MODEL), 0) % F32_SUBLANES
        for s in (1, 2, 4):
            m = row8 < F32_SUBLANES - s
            d = jnp.where(m, d + b * _tile_shift(d, -s), d)
            b = jnp.where(m, b * _tile_shift(b, -s), b)
        b_s[...] = b
        d_s[...] = d

        def step(k, cr):
            sl = pl.ds(pl.multiple_of((groups - 1 - k) * F32_SUBLANES, F32_SUBLANES), F32_SUBLANES)
            gb_ = d_s[sl, :] + b_s[sl, :] * cr
            g_s[sl, :] = gb_
            return jnp.broadcast_to(gb_[0:1, :], (F32_SUBLANES, D_MODEL))

        g_carry[...] = lax.fori_loop(0, groups, step, g_carry[...], unroll=4)
        gsc = g_s[...]
        da = gsc * _shift_down(hl_ext[...], 1)[F32_SUBLANES:, :]
        dmult = gsc * (ig * xc)
        dig = gsc * (mult * xc)
        dxc = gsc * (mult * ig)
        dlog_a = da * a - (a * a) * dmult / mult
        dr = dlog_a * (-LRU_C * sp)
        vec_ref[_bag_row("lru_lambda"), :] += jnp.sum(dlog_a * (-LRU_C * r), axis=0, keepdims=True)
        dpa = dr * (r * (1.0 - r))
        dpx = dig * (ig * (1.0 - ig))
        vec_ref[_bag_row("lru_b_a"), :] += jnp.sum(dpa, axis=0, keepdims=True)
        vec_ref[_bag_row("lru_b_x"), :] += jnp.sum(dpx, axis=0, keepdims=True)
        back = []
        for h in range(LRU_HEADS):
            cols = slice(h * HEAD_DIM, (h + 1) * HEAD_DIM)
            xh = xc[:, cols].astype(BF16)
            dpa_h = dpa[:, cols].astype(BF16)
            dpx_h = dpx[:, cols].astype(BF16)
            mat_ref[mat_rows("lru_w_a", h), :] += _dot_tn(xh, dpa_h)
            mat_ref[mat_rows("lru_w_x", h), :] += _dot_tn(xh, dpx_h)
            back.append(_dot_nt(dpa_h, wa_ref[h]) + _dot_nt(dpx_h, wx_ref[h]))
        dxc = dxc + jnp.concatenate(back, axis=1)
        vec_ref[_bag_row("conv_b"), :] += jnp.sum(dxc, axis=0, keepdims=True)
        for k in range(CONV_WIDTH):
            tap = _shift_down(ea, CONV_WIDTH - 1 - k)[CONV_HIST:, :] if k < CONV_WIDTH - 1 else ea[CONV_HIST:, :]
            vec_ref[_bag_row("conv_w", k), :] += jnp.sum(dxc * tap, axis=0, keepdims=True)
        dxc_ext[0:tb, :] = dxc
        ed = dxc_ext[...]
        dxa = ed * cw_ref[3:4, :]
        dxa = dxa + _shift_up(ed, 1) * cw_ref[2:3, :]
        dxa = dxa + _shift_up(ed, 2) * cw_ref[1:2, :]
        dxa = dxa + _shift_up(ed, 3) * cw_ref[0:1, :]
        dz_ref[:, 0:D_MODEL] = dxa[0:tb, :].astype(BF16)
        dxc_ext[tb:, :] = dxc[0:CONV_HIST, :]

        pos = blk * tb + lax.broadcasted_iota(jnp.int32, (tb, POOL_GROUP_DIM), 0)
        diff = _pool_diff(eb, pos)
        ypre = _pool_mix(diff, pw_ref)
        ps = ps_ref[...]
        gb = gb_ref[...]
        sgb = _sigmoid(gb)
        dyb = dyb_ref[...]
        dyp = dyb * (gb * sgb)
        dz_ref[:, 2 * D_MODEL + POOL_WIDTH:3 * D_MODEL] = (
            dyb * (ypre * ps) * (sgb * (1.0 + gb * (1.0 - sgb)))).astype(BF16)
        vec_ref[_bag_row("pool_scale"), 0:POOL_WIDTH] += jnp.sum(dyp * ypre, axis=0, keepdims=True)
        dypre = dyp * ps
        for g, k in enumerate(POOL_WINDOWS):
            cols = slice(g * POOL_GROUP_DIM, (g + 1) * POOL_GROUP_DIM)
            dyg = dypre[:, cols].astype(BF16)
            mat_ref[mat_rows("pool_w", g), :] += _dot_tn(diff[g].astype(BF16), dyg)
            ddiff = _dot_nt(dyg, pw_ref[g])
            count = jnp.minimum(pos + 1, k).astype(F32)
            dwin = ddiff / count
            dwin_ext[0:tb, cols] = dwin
            s = dwin_ext[:, cols]
            for step_ in range(g + 1):
                s = s + _shift_up(s, 2 ** step_)
            dz_ref[:, 2 * D_MODEL + g * POOL_GROUP_DIM:2 * D_MODEL + (g + 1) * POOL_GROUP_DIM] = (
                s[0:tb, :] - ddiff).astype(BF16)
            dwin_ext[tb:, cols] = dwin[0:POOL_HIST, :]

        dz_ref[:, 3 * D_MODEL:] = dzm_ref[...]

        @pl.when(i == nb - 1)
        def _():
            row = _bag_row("lru_lambda")
            vec_ref[row, :] = vec_ref[row, :] * (-_sigmoid(-lam))

    rev = lambda i: (nb - 1 - i, 0)
    fixed = lambda i: (0, 0)

    def prev(rows, col):
        per = tb // rows
        return lambda i: (jnp.maximum((nb - 1 - i) * per - 1, 0), col)

    in_specs = [pl.BlockSpec((tb, D_MODEL), lambda i: (nb - 1 - i, 0)),
                pl.BlockSpec((CONV_HIST, D_MODEL), prev(CONV_HIST, 0)),
                pl.BlockSpec((tb, D_MODEL), lambda i: (nb - 1 - i, 1)),
                pl.BlockSpec((tb, POOL_WIDTH), lambda i: (nb - 1 - i, 4)),
                pl.BlockSpec((POOL_HIST, POOL_WIDTH), prev(POOL_HIST, 4)),
                pl.BlockSpec((tb, POOL_WIDTH), lambda i: (nb - 1 - i, 5)),
                pl.BlockSpec((tb, D_MODEL), rev),
                pl.BlockSpec((F32_SUBLANES, D_MODEL), prev(F32_SUBLANES, 0)),
                pl.BlockSpec((tb, D_MODEL), rev), pl.BlockSpec((tb, POOL_WIDTH), lambda i: (nb - 1 - i, 2)),
                pl.BlockSpec((tb, 2 * D_MODEL), rev)] + _branch_specs(tb, rev, fixed) + [
                    pl.BlockSpec((VEC_BAG_ROWS, D_MODEL), fixed)]
    out_shape = (jax.ShapeDtypeStruct((t, IN_COLS), BF16), jax.ShapeDtypeStruct((VEC_BAG_ROWS, D_MODEL), F32),
                 jax.ShapeDtypeStruct((MAT_BAG_ROWS, HEAD_DIM), F32))
    out_specs = (pl.BlockSpec((tb, IN_COLS), rev), pl.BlockSpec((VEC_BAG_ROWS, D_MODEL), fixed),
                 pl.BlockSpec((MAT_BAG_ROWS, HEAD_DIM), fixed))
    scratch = [pltpu.VMEM((tb + CONV_HIST, D_MODEL), F32), pltpu.VMEM((tb + POOL_HIST, POOL_WIDTH), F32),
               pltpu.VMEM((tb + F32_SUBLANES, D_MODEL), F32), pltpu.VMEM((tb + F32_SUBLANES, D_MODEL), F32),
               pltpu.VMEM((tb + CONV_HIST, D_MODEL), F32), pltpu.VMEM((tb + POOL_HIST, POOL_WIDTH), F32),
               pltpu.VMEM((F32_SUBLANES, D_MODEL), F32),
               pltpu.VMEM((tb, D_MODEL), F32), pltpu.VMEM((tb, D_MODEL), F32), pltpu.VMEM((tb, D_MODEL), F32)]
    return pl.pallas_call(
        body, name="branches_bwd", out_shape=out_shape, grid=(nb,), in_specs=in_specs, out_specs=out_specs,
        scratch_shapes=scratch, input_output_aliases={len(in_specs) - 1: 1},
        compiler_params=pltpu.CompilerParams(dimension_semantics=("arbitrary",),
                                             vmem_limit_bytes=VMEM_LIMIT_BYTES),
    )(z, z, z, z, z, z, hl, hl, dy, dy, dzm, *weights, vec_bag)


MERGE_KEPT = ("merged", "do", "hn", "dgp", "dpe", "da", "dbm")


def _merge_head(x2d, ya, yb, z, p2d, tgt, w_pl, w_pp, w_out, w_pg, w_pe, g2, gf, tb):
    t = x2d.shape[0]
    p_dim = p2d.shape[1]

    def body(x_ref, ya_ref, yb_ref, ma_ref, mb_ref, p_ref, t_ref, wpl_ref, wpp_ref, wout_ref, wpg_ref, wpe_ref,
             g2_ref, gf_ref,
             bag_ref, dxr_ref, dy_ref, dzm_ref, kept_ref, pbf_ref):
        def keep(name, val):
            k = MERGE_KEPT.index(name)
            kept_ref[:, k * D_MODEL:(k + 1) * D_MODEL] = val

        @pl.when(pl.program_id(0) == 0)
        def _():
            bag_ref[...] = jnp.zeros_like(bag_ref)

        a_ = _dot(ya_ref[...], wpl_ref[...])
        bm = _dot(yb_ref[...], wpp_ref[...])
        sa = _sigmoid(ma_ref[...])
        sb = _sigmoid(mb_ref[...])
        mg = (sa * a_ + sb * bm).astype(BF16)
        keep("merged", mg)
        x1 = x_ref[...] + _dot(mg, wout_ref[...])
        xn2, r2 = _rms(x1)
        g2 = g2_ref[...]
        hn = (xn2 * g2).astype(BF16)
        keep("hn", hn)
        gate = _sigmoid(_dot(hn, wpg_ref[...]))
        pbf = p_ref[...].astype(BF16)
        pbf_ref[...] = pbf
        pe = _dot(pbf, wpe_ref[...])
        x2 = x1 + gate * pe
        xn3, r3 = _rms(x2)
        gf = gf_ref[...]
        err = xn3 * gf - t_ref[...]
        bag_ref[_bag_rows("loss"), 0:128] += 0.5 * jnp.sum(jnp.mean(err * err, axis=-1))

        dy = err * (1.0 / D_MODEL)
        bag_ref[_bag_row("final_g"), :] += jnp.sum(dy * xn3, axis=0, keepdims=True)
        dx2 = _rms_bwd(dy * gf, xn3, r3)
        keep("dpe", (dx2 * gate).astype(BF16))
        dgp = ((dx2 * pe) * (gate * (1.0 - gate))).astype(BF16)
        keep("dgp", dgp)
        dhn = _dot_nt(dgp, wpg_ref[...])
        bag_ref[_bag_row("ple_norm_g"), :] += jnp.sum(dhn * xn2, axis=0, keepdims=True)
        dx1 = dx2 + _rms_bwd(dhn * g2, xn2, r2)
        dxr_ref[...] = dx1
        do = dx1.astype(BF16)
        keep("do", do)
        dmg = _dot_nt(do, wout_ref[...])
        da = (dmg * sa).astype(BF16)
        dbm = (dmg * sb).astype(BF16)
        keep("da", da)
        keep("dbm", dbm)
        dzm_ref[:, 0:D_MODEL] = (dmg * a_ * (sa * (1.0 - sa))).astype(BF16)
        dzm_ref[:, D_MODEL:] = (dmg * bm * (sb * (1.0 - sb))).astype(BF16)
        dy_ref[:, 0:D_MODEL] = _dot_nt(da, wpl_ref[...])
        dy_ref[:, D_MODEL:] = _dot_nt(dbm, wpp_ref[...])

    row = lambda i: (i, 0)
    fixed = lambda i: (0, 0)

    def resident(shape):
        return pl.BlockSpec(shape, fixed, pipeline_mode=pl.Buffered(1))

    tok = lambda width: pl.BlockSpec((tb, width), row)
    in_specs = [tok(D_MODEL), tok(D_MODEL), tok(POOL_WIDTH),
                pl.BlockSpec((tb, D_MODEL), lambda i: (i, 3)), pl.BlockSpec((tb, D_MODEL), lambda i: (i, 4)),
                tok(p_dim), tok(D_MODEL),
                resident((D_MODEL, D_MODEL)), resident((POOL_WIDTH, D_MODEL)), resident((D_MODEL, D_MODEL)),
                resident((D_MODEL, D_MODEL)), resident((p_dim, D_MODEL)),
                pl.BlockSpec((1, D_MODEL), fixed), pl.BlockSpec((1, D_MODEL), fixed)]
    bf = lambda width: jax.ShapeDtypeStruct((t, width), BF16)
    f32 = lambda width: jax.ShapeDtypeStruct((t, width), F32)
    kept = len(MERGE_KEPT) * D_MODEL
    out_shape = (jax.ShapeDtypeStruct((VEC_BAG_ROWS, D_MODEL), F32),
                 f32(D_MODEL), f32(D_MODEL + POOL_WIDTH), bf(2 * D_MODEL), bf(kept), bf(p_dim))
    out_specs = (pl.BlockSpec((VEC_BAG_ROWS, D_MODEL), fixed),
                 tok(D_MODEL), tok(D_MODEL + POOL_WIDTH), tok(2 * D_MODEL), tok(kept), tok(p_dim))
    return pl.pallas_call(
        body, name="merge_head", out_shape=out_shape, grid=(t // tb,), in_specs=in_specs, out_specs=out_specs,
        compiler_params=pltpu.CompilerParams(dimension_semantics=("arbitrary",),
                                             vmem_limit_bytes=VMEM_LIMIT_BYTES),
    )(x2d, ya, yb, z, z, p2d, tgt, w_pl, w_pp, w_out, w_pg, w_pe, g2, gf)


def kernel(x, p, norm_g, w_in, conv_w, conv_b, lru_w_a, lru_b_a, lru_w_x, lru_b_x, lru_lambda, pool_w, pool_scale, w_proj_lru, w_proj_pool, w_out, ple_norm_g, w_ple_gate, w_ple_proj, final_g, loss_target, m_norm_g, m_w_in, m_conv_w, m_conv_b, m_lru_w_a, m_lru_b_a, m_lru_w_x, m_lru_b_x, m_lru_lambda, m_pool_w, m_pool_scale, m_w_proj_lru, m_w_proj_pool, m_w_out, m_ple_norm_g, m_w_ple_gate, m_w_ple_proj, m_final_g, v_norm_g, v_w_in, v_conv_w, v_conv_b, v_lru_w_a, v_lru_b_a, v_lru_w_x, v_lru_b_x, v_lru_lambda, v_pool_w, v_pool_scale, v_w_proj_lru, v_w_proj_pool, v_w_out, v_ple_norm_g, v_w_ple_gate, v_w_ple_proj, v_final_g):
    bsz, seq, _ = x.shape
    t = bsz * seq
    tb_mm = min(1024, seq)
    tb_seq = min(256, seq // 2) if seq >= 512 else seq
    x2d = x.reshape(t, D_MODEL)
    p2d = p.reshape(t, p.shape[-1])
    tgt = loss_target.reshape(t, D_MODEL)
    chip = 2 * lax.axis_index("x") + lax.axis_index("y")

    rest = [(w_proj_lru[0], 0), (w_proj_pool[0], 1), (w_out[0], 0), (w_ple_gate[0], 0), (w_ple_proj[0], 1)]
    z, h_bf, w_in_f, conv_w_f = _in_proj_gather(x2d, norm_g, w_in[0].astype(BF16), [(conv_w[0], 1, False)], tb_mm)

    wa_bf = lru_w_a[0].astype(BF16)
    wx_bf = lru_w_x[0].astype(BF16)
    pw_bf = pool_w[0].astype(BF16)
    branch_w = (conv_w_f, conv_b, wa_bf, lru_b_a.reshape(1, D_MODEL), wx_bf, lru_b_x.reshape(1, D_MODEL),
                lru_lambda, pw_bf, pool_scale)

    ya, yb, hl, w_pl_f, w_pp_f, w_out_f, w_pg_f, w_pe_f = _branches_fwd(
        z, branch_w, seq, tb_seq, [(w.astype(BF16), axis, True) for w, axis in rest])
    vec_bag, dx_res, dy, dzm, kept, p_bf = _merge_head(
        x2d, ya, yb, z, p2d, tgt, w_pl_f, w_pp_f, w_out_f, w_pg_f, w_pe_f, ple_norm_g, final_g.reshape(1, D_MODEL),
        tb_seq)
    dz, vec_bag, mat_bag = _branches_bwd(z, hl, dy, dzm, branch_w, vec_bag, seq, tb_seq)

    def slab(name):
        return kept, MERGE_KEPT.index(name), D_MODEL

    tb_dw = min(1024, seq)
    def proj_grad(lhs, rhs, name, cols):
        g32, g16 = _weight_grad(lhs, rhs, 1, tb_dw, name)
        if cols:
            return g32[0], True, g16[0]
        rows = g32.shape[1] // 8
        return g32.reshape(8, rows, g32.shape[2]), False, g16.reshape(8, rows, g32.shape[2])

    p_dim = p2d.shape[1]
    proj_parts = [proj_grad(ya, slab("da"), "dw_proj_lru", False), proj_grad(yb, slab("dbm"), "dw_proj_pool", True),
                  proj_grad(slab("merged"), slab("do"), "dw_out", False),
                  proj_grad(slab("hn"), slab("dgp"), "dw_ple_gate", False),
                  proj_grad(p_bf, slab("dpe"), "dw_ple_proj", True)]
    nb_dw = t // tb_dw
    g_in, g_in16, r_pl, r_pp, r_out, r_pg, r_pe, vec_mine, mat_mine = _weight_grad(
        h_bf, dz, N_CHIPS, tb_dw, "dw_in",
        reduce=(proj_parts + [(vec_bag.reshape(8, VEC_BAG_ROWS // 8, D_MODEL), False, None),
                              (mat_bag.reshape(8, MAT_BAG_ROWS // 8, HEAD_DIM), False, None)],
                [BF16] * 5 + [F32] * 2,
                (0, nb_dw // 2, 2 * nb_dw - 1, 3 * nb_dw + nb_dw // 2, N_CHIPS * nb_dw - 1)))
    pieces = (8, D_MODEL // 2, IN_COLS // N_CHIPS)
    nb_seq = t // tb_seq
    dx, d_g1, r_in, vec_sum, mat_sum = _in_proj_bwd(
        dz, w_in_f, x2d, dx_res, norm_g, tb_seq,
        reduce=([(g_in.reshape(pieces), False, g_in16.reshape(pieces))], BF16,
                (0, nb_seq // 8, nb_seq // 2, nb_seq - 1, nb_seq - 1)),
        shards=[(vec_mine.reshape(VEC_BAG_ROWS // N_CHIPS, D_MODEL), 0, True),
                (mat_mine.reshape(MAT_BAG_ROWS // N_CHIPS, HEAD_DIM), 0, True)])
    g_g1 = _all_reduce_tile(d_g1, "allreduce_norm_g")

    def big_update(w, g2d, m, v, rows, name):
        d, nm, nv = _adamw(w[0], g2d, m[0], v[0], rows, name)
        return g2d[None], d[None], nm[None], nv[None]

    u_in = big_update(w_in, r_in.reshape(D_MODEL, IN_COLS // N_CHIPS), m_w_in, v_w_in, 256, "adamw_w_in")
    u_pl = big_update(w_proj_lru, r_pl.reshape(D_MODEL // N_CHIPS, D_MODEL), m_w_proj_lru, v_w_proj_lru, 256, "adamw_w_proj_lru")
    u_pp = big_update(w_proj_pool, r_pp.reshape(POOL_WIDTH, D_MODEL // N_CHIPS), m_w_proj_pool, v_w_proj_pool, 512, "adamw_w_proj_pool")
    u_out = big_update(w_out, r_out.reshape(D_MODEL // N_CHIPS, D_MODEL), m_w_out, v_w_out, 256, "adamw_w_out")
    u_pg = big_update(w_ple_gate, r_pg.reshape(D_MODEL // N_CHIPS, D_MODEL), m_w_ple_gate, v_w_ple_gate, 256, "adamw_w_ple_gate")
    u_pe = big_update(w_ple_proj, r_pe.reshape(p_dim, D_MODEL // N_CHIPS), m_w_ple_proj, v_w_ple_proj, 256, "adamw_w_ple_proj")

    small = [("norm_g", norm_g, m_norm_g, v_norm_g), ("conv_b", conv_b, m_conv_b, v_conv_b),
             ("lru_w_a", lru_w_a, m_lru_w_a, v_lru_w_a), ("lru_b_a", lru_b_a, m_lru_b_a, v_lru_b_a),
             ("lru_w_x", lru_w_x, m_lru_w_x, v_lru_w_x), ("lru_b_x", lru_b_x, m_lru_b_x, v_lru_b_x),
             ("lru_lambda", lru_lambda, m_lru_lambda, v_lru_lambda), ("pool_w", pool_w, m_pool_w, v_pool_w),
             ("pool_scale", pool_scale, m_pool_scale, v_pool_scale),
             ("ple_norm_g", ple_norm_g, m_ple_norm_g, v_ple_norm_g), ("final_g", final_g, m_final_g, v_final_g)]

    def view(a):
        return a.reshape(-1, a.shape[-1]) if a.ndim != 3 else a[0]

    cw_at = F32_SUBLANES * VEC_BAG_SLOTS.index("conv_w")
    cw_cols = D_MODEL // N_CHIPS
    g_cw = lax.dynamic_slice(vec_sum, (cw_at, chip * cw_cols), (CONV_WIDTH, cw_cols))
    flat = _adamw_replicated(vec_sum, mat_sum, g_g1, [(name,) + tuple(view(a) for a in arrs) for name, *arrs in small],
                             (conv_w[0], m_conv_w[0], v_conv_w[0], g_cw))
    u_small = {name: tuple(flat[4 * k + pick].reshape(arrs[0].shape) for pick in range(4))
               for k, (name, *arrs) in enumerate(small)}
    u_cw = tuple(a[None] for a in (g_cw,) + tuple(flat[4 * len(small):]))

    loss = vec_sum[F32_SUBLANES * VEC_BAG_SLOTS.index("loss"), 0]
    grad_x = dx.reshape(bsz, seq, D_MODEL)

    def ordered(pick):
        s = {name: u[pick] for name, u in u_small.items()}
        return [s["norm_g"], u_in[pick], u_cw[pick], s["conv_b"], s["lru_w_a"], s["lru_b_a"], s["lru_w_x"], s["lru_b_x"],
                s["lru_lambda"], s["pool_w"], s["pool_scale"], u_pl[pick], u_pp[pick], u_out[pick], s["ple_norm_g"],
                u_pg[pick], u_pe[pick], s["final_g"]]

    return (loss, grad_x, *ordered(0), *ordered(1), *ordered(2), *ordered(3))
```

```python
import jax
import jax.numpy as jnp
from jax import lax
from jax.experimental import pallas as pl
from jax.experimental.pallas import tpu as pltpu

F32 = jnp.float32
BF16 = jnp.bfloat16
MESH = pl.DeviceIdType.MESH

D_MODEL = 1024
LRU_HEADS = 8
HEAD_DIM = 128
CONV_WIDTH = 4
LRU_C = 8.0
POOL_WIDTH = 512
POOL_WINDOWS = (2, 4, 8, 16)
POOL_GROUP_DIM = 128
IN_COLS = 5120
N_CHIPS = 4
EPS = 1e-6

ADAM_LR = 0.001
ADAM_B1 = 0.9
ADAM_B2 = 0.999
ADAM_EPS = 1e-08
ADAM_WD = 0.01
ADAM_STEP = 10

F32_SUBLANES = 8
CONV_HIST = 8
POOL_HIST = 16
VMEM_LIMIT_BYTES = 58 * 1024 * 1024
VEC_BAG_SLOTS = ("norm_g", "conv_w", "conv_b", "lru_b_a", "lru_b_x", "lru_lambda", "pool_scale", "ple_norm_g",
                 "final_g", "loss")
VEC_BAG_ROWS = 128
MAT_BAG_AT = {"lru_w_a": 0, "lru_w_x": LRU_HEADS * HEAD_DIM, "pool_w": 2 * LRU_HEADS * HEAD_DIM}
MAT_BAG_ROWS = 2 * LRU_HEADS * HEAD_DIM + len(POOL_WINDOWS) * POOL_GROUP_DIM


def _bag_row(name, k=0):
    at = F32_SUBLANES * VEC_BAG_SLOTS.index(name) + k
    return slice(at, at + 1)


def _bag_rows(name):
    at = F32_SUBLANES * VEC_BAG_SLOTS.index(name)
    return slice(at, at + F32_SUBLANES)


def _dot(a, b):
    return jnp.dot(a, b, preferred_element_type=F32)


def _dot_nt(a, b):
    return lax.dot_general(a, b, (((1,), (1,)), ((), ())), preferred_element_type=F32)


def _dot_tn(a, b):
    return lax.dot_general(a, b, (((0,), (0,)), ((), ())), preferred_element_type=F32)


def _sigmoid(v):
    return jax.nn.sigmoid(v)


def _softplus(v):
    return jnp.maximum(v, 0.0) + jnp.log1p(jnp.exp(-jnp.abs(v)))


def _place():
    return lax.axis_index("x"), lax.axis_index("y"), lax.axis_index("c")


GATHER_SEMS = 6


def _gather_shapes(shards):
    out_shape = []
    for arr, axis, _ in shards:
        r, cols = arr.shape
        out_shape.append(jax.ShapeDtypeStruct((N_CHIPS * r, cols) if axis == 0 else (r, N_CHIPS * cols), arr.dtype))
    n = len(shards)
    sems = [pltpu.SemaphoreType.DMA((n * GATHER_SEMS,)), pltpu.SemaphoreType.DMA((n * GATHER_SEMS,)),
            pltpu.SemaphoreType.DMA((n,))]
    return out_shape, sems


def _gather_steps(shards, ins, outs, send_sems, recv_sems, local_sems):
    n = len(shards)
    x, y, c = _place()
    me, sibling = (x, y, c), (x, y, 1 - c)
    chips = [(x, 1 - y), (1 - x, y), (1 - x, 1 - y)]

    def region(k, cx, cy, hc):
        (r, cols), axis = shards[k][0].shape, shards[k][1]
        j = 2 * cx + cy
        if axis == 0:
            if hc is None:
                return outs[k].at[pl.ds(j * r, r), :]
            return outs[k].at[pl.ds(j * r + hc * (r // 2), r // 2), :]
        if hc is None:
            return outs[k].at[:, pl.ds(j * cols, cols)]
        return outs[k].at[pl.ds(hc * (r // 2), r // 2), pl.ds(j * cols, cols)]

    def remote(k, sem, block, to, src=None):
        dst = region(k, *block)
        return pltpu.make_async_remote_copy(
            src_ref=dst if src is None else src, dst_ref=dst,
            send_sem=send_sems.at[k * GATHER_SEMS + sem], recv_sem=recv_sems.at[k * GATHER_SEMS + sem],
            device_id=to, device_id_type=MESH)

    def first(k, idx):
        r, split = shards[k][0].shape[0], shards[k][2]
        src = ins[k].at[pl.ds(c * (r // 2), r // 2), :] if split else ins[k]
        return remote(k, idx, (x, y, c if split else None), (*chips[idx], c), src=src)

    def relay(k):
        src_chip = (jnp.bitwise_xor(x, 1 - c), jnp.bitwise_xor(y, c))
        dst_chip = (jnp.bitwise_xor(x, c), jnp.bitwise_xor(y, 1 - c))
        return remote(k, 2, (*src_chip, c), (*dst_chip, c))

    def passed(k, idx):
        return remote(k, 3 + idx, (*chips[idx], c), sibling)

    def mine(k):
        return pltpu.make_async_copy(ins[k], region(k, x, y, None), local_sems.at[k])

    def start():
        for k in range(n):
            mine(k).start()
            for idx in range(2 if shards[k][2] else 3):
                first(k, idx).start()

    def relay_on():
        for k in range(n):
            split = shards[k][2]
            for idx in range(2):
                remote(k, idx, (*chips[idx], c if split else None), me).wait_recv()
            if split:
                relay(k).start()
                passed(k, 0).start()
                passed(k, 1).start()

    def finish():
        for k in range(n):
            split = shards[k][2]
            remote(k, 2, (*chips[2], c if split else None), me).wait_recv()
            if split:
                passed(k, 2).start()
        for k in range(n):
            if shards[k][2]:
                for idx in range(3):
                    remote(k, 3 + idx, (*chips[idx], 1 - c), me).wait_recv()
        for k in range(n):
            if shards[k][2]:
                for cp in (first(k, 0), first(k, 1), relay(k), passed(k, 0), passed(k, 1), passed(k, 2)):
                    cp.wait_send()
            else:
                for idx in range(3):
                    first(k, idx).wait_send()
            mine(k).wait()

    return start, relay_on, finish


RS_ADD_ROWS = (64, 32, 16, 8)


N_DEV = 2 * N_CHIPS


def _all_reduce_scratch(shape):
    return [pltpu.VMEM((N_DEV,) + tuple(shape), F32), pltpu.SemaphoreType.DMA((N_DEV - 1,)),
            pltpu.SemaphoreType.DMA((N_DEV - 1,))]


def _all_reduce_tile(v_ref, o_ref, slots, send_sems, recv_sems):
    flips = [(dx, dy, dc) for dx in (0, 1) for dy in (0, 1) for dc in (0, 1)][1:]
    x, y, c = _place()
    mine = 4 * x + 2 * y + c

    def copy(k, to_flip, slot):
        dx, dy, dc = to_flip
        peer = (jnp.bitwise_xor(x, dx), jnp.bitwise_xor(y, dy), jnp.bitwise_xor(c, dc))
        return pltpu.make_async_remote_copy(
            src_ref=v_ref, dst_ref=slots.at[slot], send_sem=send_sems.at[k], recv_sem=recv_sems.at[k],
            device_id=peer, device_id_type=MESH)

    sends = [copy(k, flip, mine) for k, flip in enumerate(flips)]
    for cp in sends:
        cp.start()
    slots[mine] = v_ref[...]
    for k, (dx, dy, dc) in enumerate(flips):
        copy(k, (dx, dy, dc), jnp.bitwise_xor(mine, 4 * dx + 2 * dy + dc)).wait_recv()
    total = slots[0]
    for d in range(1, N_DEV):
        total = total + slots[d]
    o_ref[...] = total
    for cp in sends:
        cp.wait_send()


RS_SEMS = 8
RS_LOCAL_SEMS = 5


def _rs_piece_shape(part):
    arr, cols = part[0], part[1]
    return (arr.shape[0] // 2, arr.shape[1] // N_CHIPS) if cols else tuple(arr.shape[1:])


def _rs_operands(parts):
    return [p[0] for p in parts] + [p[0] if p[2] is None else p[2] for p in parts]


def _rs_wires(parts, wire):
    return list(wire) if isinstance(wire, (list, tuple)) else [wire] * len(parts)


def _rs_shapes(parts, wire):
    n = len(parts)
    shapes = [_rs_piece_shape(p) for p in parts]
    out_shape = [jax.ShapeDtypeStruct((2,) + s, F32) for s in shapes]
    scratch = []
    for lead, kind in ((N_CHIPS, "f32"), (N_CHIPS, "narrow"), (N_CHIPS, "wire"), (None, "f32"), (N_CHIPS, "wire")):
        for s, p, w in zip(shapes, parts, _rs_wires(parts, wire)):
            dtype = {"f32": F32, "narrow": F32 if p[2] is None else p[2].dtype, "wire": w}[kind]
            scratch.append(pltpu.VMEM(s if lead is None else (lead,) + s, dtype))
    scratch += [pltpu.SemaphoreType.DMA((n * RS_SEMS,)), pltpu.SemaphoreType.DMA((n * RS_SEMS,)),
                pltpu.SemaphoreType.DMA((n * RS_LOCAL_SEMS,))]
    return out_shape, scratch


def _rs_steps(parts, ins, outs, scratch):
    n = len(parts)
    own, sib, got, fin, snd = (scratch[k * n:(k + 1) * n] for k in range(5))
    send_sems, recv_sems, local_sems = scratch[5 * n:]
    shapes = [_rs_piece_shape(p) for p in parts]
    x, y, c = _place()
    j_me = 2 * x + y
    me, sibling = (x, y, c), (x, y, 1 - c)

    def piece(a, jj, core, narrow=False):
        ref = ins[n + a] if narrow else ins[a]
        if parts[a][1]:
            r, cl = shapes[a]
            return ref.at[pl.ds(core * r, r), pl.ds(jj * cl, cl)]
        return ref.at[2 * jj + core]

    def remote(a, sem, src, dst, to):
        return pltpu.make_async_remote_copy(
            src_ref=src, dst_ref=dst, send_sem=send_sems.at[a * RS_SEMS + sem],
            recv_sem=recv_sems.at[a * RS_SEMS + sem], device_id=to, device_id_type=MESH)

    def rows_loop(a, fn):
        r = shapes[a][0]
        step = max(s for s in RS_ADD_ROWS if r % s == 0)

        def it(i, carry):
            fn(pl.ds(pl.multiple_of(i * step, step), step))
            return carry

        lax.fori_loop(0, r // step, it, 0)

    def load(a, jj):
        return pltpu.make_async_copy(piece(a, jj, c), own[a].at[jj], local_sems.at[a * RS_LOCAL_SEMS + jj])

    def to_sibling(a, jj):
        return remote(a, jj, piece(a, jj, 1 - c, narrow=True), sib[a].at[jj], sibling)

    near = (jnp.bitwise_xor(x, 1 - c), jnp.bitwise_xor(y, c))
    far = (jnp.bitwise_xor(x, c), jnp.bitwise_xor(y, 1 - c))
    diag = (1 - x, 1 - y)
    FROM_NEAR, FROM_FAR, FEED = 0, 1, 2

    def chip_of(chip):
        return 2 * chip[0] + chip[1]

    def feed(a):
        return remote(a, 4, snd[a].at[chip_of(diag)], got[a].at[FEED], (*near, c))

    def to_near(a):
        return remote(a, 5, snd[a].at[chip_of(near)], got[a].at[FROM_NEAR], (*near, c))

    def to_far(a):
        return remote(a, 6, snd[a].at[chip_of(far)], got[a].at[FROM_FAR], (*far, c))

    def store(a):
        return pltpu.make_async_copy(fin[a], outs[a].at[c], local_sems.at[a * RS_LOCAL_SEMS + 4])

    def result_to_sibling(a):
        return remote(a, 7, fin[a], outs[a].at[c], sibling)

    def exchange():
        for a in range(n):
            for jj in range(N_CHIPS):
                load(a, jj).start()
                to_sibling(a, jj).start()

    def chip_sums():
        for a in range(n):
            for jj in range(N_CHIPS):
                load(a, jj).wait()
                remote(a, jj, sib[a].at[jj], sib[a].at[jj], me).wait_recv()

                def add(sl, a=a, jj=jj):
                    q = own[a][jj, sl, :] + sib[a][jj, sl, :].astype(F32)
                    own[a][jj, sl, :] = q
                    snd[a][jj, sl, :] = q.astype(snd[a].dtype)

                rows_loop(a, add)
        for a in range(n):
            feed(a).start()
        for a in range(n):
            to_near(a).start()

    def relay():
        for a in range(n):
            remote(a, 4, got[a].at[FEED], got[a].at[FEED], me).wait_recv()

            def add(sl, a=a):
                pair = own[a][chip_of(far), sl, :] + got[a][FEED, sl, :].astype(F32)
                snd[a][chip_of(far), sl, :] = pair.astype(snd[a].dtype)

            rows_loop(a, add)
            to_far(a).start()

    def totals():
        for a in range(n):
            remote(a, 5, got[a].at[FROM_NEAR], got[a].at[FROM_NEAR], me).wait_recv()
            remote(a, 6, got[a].at[FROM_FAR], got[a].at[FROM_FAR], me).wait_recv()

            def total(sl, a=a):
                fin[a][sl, :] = (own[a][j_me, sl, :] + got[a][FROM_NEAR, sl, :].astype(F32)) + (
                    got[a][FROM_FAR, sl, :].astype(F32))

            rows_loop(a, total)
            store(a).start()
            result_to_sibling(a).start()

    def finish():
        for a in range(n):
            remote(a, 7, outs[a].at[1 - c], outs[a].at[1 - c], me).wait_recv()
        for a in range(n):
            for jj in range(N_CHIPS):
                to_sibling(a, jj).wait_send()
            for cp in (feed(a), to_near(a), to_far(a), result_to_sibling(a)):
                cp.wait_send()
            store(a).wait()

    return exchange, chip_sums, relay, totals, finish


def _rms(x):
    r = lax.rsqrt(jnp.mean(x * x, axis=-1, keepdims=True) + EPS)
    return x * r, r


def _rms_bwd(dxn, xn, r):
    return r * (dxn - xn * jnp.mean(dxn * xn, axis=-1, keepdims=True))


def _in_proj_gather(x2d, norm_g, w_in_sh, shards, tb):
    t = x2d.shape[0]
    nb = t // tb
    cols = IN_COLS // N_CHIPS
    half = D_MODEL // 2
    n = len(shards)

    def body(x_ref, g_ref, win_ref, *refs):
        ins = refs[:n]
        z_ref, h_ref, wfull_ref = refs[n:n + 3]
        outs = refs[n + 3:2 * n + 3]
        wv, h_buf, send_sems, recv_sems, local_sems, w_send, w_recv, w_local = refs[2 * n + 3:]
        s, i = pl.program_id(0), pl.program_id(1)
        x, y, c = _place()
        me, sibling = (x, y, c), (x, y, 1 - c)
        chips = [(x, 1 - y), (1 - x, y), (1 - x, 1 - y)]

        def w_half(cx, cy, hc):
            return wv.at[2 * cx + cy, pl.ds(hc * half, half), :]

        def w_remote(sem, block, to, src=None):
            dst = w_half(*block)
            return pltpu.make_async_remote_copy(
                src_ref=dst if src is None else src, dst_ref=dst, send_sem=w_send.at[sem],
                recv_sem=w_recv.at[sem], device_id=to, device_id_type=MESH)

        def w_first(idx):
            return w_remote(idx, (x, y, c), (*chips[idx], c), src=win_ref.at[pl.ds(c * half, half), :])

        def w_relay():
            src_chip = (jnp.bitwise_xor(x, 1 - c), jnp.bitwise_xor(y, c))
            dst_chip = (jnp.bitwise_xor(x, c), jnp.bitwise_xor(y, 1 - c))
            return w_remote(2, (*src_chip, c), (*dst_chip, c))

        def w_pass(idx):
            return w_remote(3 + idx, (*chips[idx], c), sibling)

        def w_store(k, cx, cy):
            jj = 2 * cx + cy
            return pltpu.make_async_copy(wv.at[jj], wfull_ref.at[:, pl.ds(jj * cols, cols)], w_local.at[k])

        start_rest, relay_rest, finish_rest = _gather_steps(shards, ins, outs, send_sems, recv_sems, local_sems)
        own = pltpu.make_async_copy(win_ref, wv.at[2 * x + y], w_local.at[4])

        @pl.when((s == 0) & (i == 0))
        def _():
            own.start()
            w_first(0).start()
            w_first(1).start()
            start_rest()
            own.wait()
            w_store(0, x, y).start()

        @pl.when((s == 1) & (i == 0))
        def _():
            w_remote(0, (*chips[0], c), me).wait_recv()
            w_remote(1, (*chips[1], c), me).wait_recv()
            w_relay().start()
            w_pass(0).start()
            w_pass(1).start()
            w_remote(3, (*chips[0], 1 - c), me).wait_recv()
            w_store(1, *chips[0]).start()

        @pl.when((s == 2) & (i == 0))
        def _():
            w_remote(4, (*chips[1], 1 - c), me).wait_recv()
            w_store(2, *chips[1]).start()

        @pl.when((s == 3) & (i == 0))
        def _():
            w_remote(2, (*chips[2], c), me).wait_recv()
            w_pass(2).start()
            w_remote(5, (*chips[2], 1 - c), me).wait_recv()
            w_store(3, *chips[2]).start()

        xn, _ = _rms(x_ref[...])
        h = (xn * g_ref[...]).astype(BF16)
        keep_h = pltpu.make_async_copy(h_buf, h_ref.at[pl.ds(pl.multiple_of(i * tb, tb), tb), :], w_local.at[5])

        @pl.when(s == 0)
        def _():
            h_buf[...] = h
            keep_h.start()

        z_ref[...] = _dot(h, wv[jnp.bitwise_xor(2 * x + y, s)])
        pl.when(s == 0)(keep_h.wait)

        @pl.when((s == N_CHIPS - 1) & (i == nb - 1))
        def _():
            relay_rest()
            finish_rest()
            for cp in (w_first(0), w_first(1), w_relay(), w_pass(0), w_pass(1), w_pass(2)):
                cp.wait_send()
            w_store(0, x, y).wait()
            for idx in range(3):
                w_store(idx + 1, *chips[idx]).wait()

    rest_shape, rest_sems = _gather_shapes(shards)
    out_shape = [jax.ShapeDtypeStruct((t, IN_COLS), F32), jax.ShapeDtypeStruct((t, D_MODEL), BF16),
                 jax.ShapeDtypeStruct((D_MODEL, IN_COLS), BF16)] + rest_shape
    any_spec = pl.BlockSpec(memory_space=pl.ANY)

    def z_map(s, i):
        return (i, jnp.bitwise_xor(2 * lax.axis_index("x") + lax.axis_index("y"), s))

    return pl.pallas_call(
        body, name="in_proj", out_shape=tuple(out_shape),
        grid=(N_CHIPS, nb),
        in_specs=[pl.BlockSpec((tb, D_MODEL), lambda s, i: (i, 0)),
                  pl.BlockSpec((1, D_MODEL), lambda s, i: (0, 0)), any_spec] + [any_spec] * n,
        out_specs=tuple([pl.BlockSpec((tb, cols), z_map), any_spec, any_spec] + [any_spec] * n),
        scratch_shapes=[pltpu.VMEM((N_CHIPS, D_MODEL, cols), BF16), pltpu.VMEM((tb, D_MODEL), BF16)] + rest_sems + [
            pltpu.SemaphoreType.DMA((GATHER_SEMS,)), pltpu.SemaphoreType.DMA((GATHER_SEMS,)),
            pltpu.SemaphoreType.DMA((N_CHIPS + 2,))],
        compiler_params=pltpu.CompilerParams(dimension_semantics=("arbitrary", "arbitrary"),
                                             vmem_limit_bytes=VMEM_LIMIT_BYTES),
    )(x2d, norm_g, w_in_sh, *[sh[0] for sh in shards])


def _in_proj_bwd(dz, w_in, x2d, dx_res, norm_g, tb, reduce, shards):
    t = x2d.shape[0]
    nb = t // tb
    parts, wire, steps = reduce
    n = len(parts)
    k = len(shards)

    def body(dz_ref, w_ref, x_ref, dres_ref, g_ref, *refs):
        at = 2 * n + k
        dx_ref, dg_ref = refs[at:at + 2]
        rs_outs, g_outs = refs[at + 2:at + 2 + n], refs[at + 2 + n:at + 2 + n + k]
        scratch = refs[at + 2 + n + k:]
        rs_scr, g_sems, dg_acc, ar_scr = scratch[:-7], scratch[-7:-4], scratch[-4], scratch[-3:]
        rs = _rs_steps(parts, refs[:2 * n], rs_outs, rs_scr)
        for step, when in zip(rs, steps):
            pl.when(pl.program_id(0) == when)(step)
        gather = _gather_steps(shards, refs[2 * n:at], g_outs, *g_sems)
        for step, when in zip(gather, (0, nb // 2, nb - 1)):
            pl.when(pl.program_id(0) == when)(step)

        @pl.when(pl.program_id(0) == 0)
        def _():
            dg_acc[...] = jnp.zeros_like(dg_acc)

        xn, r = _rms(x_ref[...])
        g = g_ref[...]
        dh = _dot_nt(dz_ref[...], w_ref[...])
        dg_acc[0:1, :] += jnp.sum(dh * xn, axis=0, keepdims=True)
        dx_ref[...] = dres_ref[...] + _rms_bwd(dh * g, xn, r)

        @pl.when(pl.program_id(0) == nb - 1)
        def _():
            _all_reduce_tile(dg_acc, dg_ref, *ar_scr)

    row = lambda i: (i, 0)
    fixed = lambda i: (0, 0)
    rs_shape, rs_scratch = _rs_shapes(parts, wire)
    g_shape, g_sems = _gather_shapes(shards)
    any_spec = pl.BlockSpec(memory_space=pl.ANY)
    return pl.pallas_call(
        body, name="in_proj_bwd",
        out_shape=tuple([jax.ShapeDtypeStruct((t, D_MODEL), F32), jax.ShapeDtypeStruct((F32_SUBLANES, D_MODEL), F32)]
                        + rs_shape + g_shape),
        grid=(nb,),
        in_specs=[pl.BlockSpec((tb, IN_COLS), row),
                  pl.BlockSpec((D_MODEL, IN_COLS), fixed, pipeline_mode=pl.Buffered(1)),
                  pl.BlockSpec((tb, D_MODEL), row), pl.BlockSpec((tb, D_MODEL), row),
                  pl.BlockSpec((1, D_MODEL), fixed)] + [any_spec] * (2 * n + k),
        out_specs=tuple([pl.BlockSpec((tb, D_MODEL), row), pl.BlockSpec((F32_SUBLANES, D_MODEL), fixed)]
                        + [any_spec] * (n + k)),
        scratch_shapes=rs_scratch + g_sems + [pltpu.VMEM((F32_SUBLANES, D_MODEL), F32)] + _all_reduce_scratch(
            (F32_SUBLANES, D_MODEL)),
        compiler_params=pltpu.CompilerParams(dimension_semantics=("arbitrary",),
                                             vmem_limit_bytes=VMEM_LIMIT_BYTES),
    )(dz, w_in, x2d, dx_res, norm_g, *_rs_operands(parts), *[sh[0] for sh in shards])


def _weight_grad(lhs, rhs, n_chunks, tb, name, reduce=None):
    t, k = lhs.shape
    nc = rhs.shape[1] // n_chunks
    nb = t // tb
    parts, wire, steps = reduce if reduce is not None else ([], F32, ())
    n = len(parts)

    def body(l_ref, r_ref, *refs):
        o_ref, o16_ref = refs[2 * n:2 * n + 2]
        if n:
            at = pl.program_id(0) * nb + pl.program_id(1)
            rs = _rs_steps(parts, refs[:2 * n], refs[2 * n + 2:3 * n + 2], refs[3 * n + 2:])
            for step, when in zip(rs, steps):
                pl.when(at == when)(step)

        @pl.when(pl.program_id(1) == 0)
        def _():
            o_ref[...] = jnp.zeros_like(o_ref)

        o_ref[...] += _dot_tn(l_ref[...], r_ref[...])

        @pl.when(pl.program_id(1) == nb - 1)
        def _():
            o16_ref[...] = o_ref[...].astype(BF16)

    rs_shape, rs_scratch = _rs_shapes(parts, wire) if n else ([], [])
    any_spec = pl.BlockSpec(memory_space=pl.ANY)
    chunk = pl.BlockSpec((None, k, nc), lambda j, i: (j, 0, 0))
    return pl.pallas_call(
        body, name=name,
        out_shape=tuple([jax.ShapeDtypeStruct((n_chunks, k, nc), F32), jax.ShapeDtypeStruct((n_chunks, k, nc), BF16)]
                        + rs_shape),
        grid=(n_chunks, nb),
        in_specs=[pl.BlockSpec((tb, k), lambda j, i: (i, 0)), pl.BlockSpec((tb, nc), lambda j, i: (i, j))]
        + [any_spec] * (2 * n),
        out_specs=tuple([chunk, chunk] + [any_spec] * n),
        scratch_shapes=rs_scratch,
        compiler_params=pltpu.CompilerParams(dimension_semantics=("arbitrary", "arbitrary"),
                                             vmem_limit_bytes=VMEM_LIMIT_BYTES),
    )(lhs, rhs, *_rs_operands(parts))


def _adam_update(w, g, m, v):
    m_ = ADAM_B1 * m + (1.0 - ADAM_B1) * g
    v_ = ADAM_B2 * v + (1.0 - ADAM_B2) * jnp.square(g)
    m_hat = m_ / (1.0 - ADAM_B1 ** ADAM_STEP)
    v_hat = v_ / (1.0 - ADAM_B2 ** ADAM_STEP)
    return -ADAM_LR * (m_hat / (jnp.sqrt(v_hat) + ADAM_EPS) + ADAM_WD * w), m_, v_


def _adamw_replicated(vec_sum, mat_sum, norm_grad, entries, conv):
    n = len(entries)

    def grad_of(name, shape, vec_ref, mat_ref, norm_ref):
        if name == "norm_g":
            return norm_ref[0:1, :]
        if name in MAT_BAG_AT:
            return mat_ref[MAT_BAG_AT[name]:MAT_BAG_AT[name] + shape[0], :]
        if shape[0] == 1:
            return vec_ref[_bag_row(name), 0:shape[1]]
        return jnp.concatenate([vec_ref[_bag_row(name), h * shape[1]:(h + 1) * shape[1]] for h in range(shape[0])],
                               axis=0)

    def body(vec_ref, mat_ref, norm_ref, *refs):
        ins, outs = refs[:3 * n + 4], refs[3 * n + 4:]
        for k in range(n):
            w_ref, m_ref, v_ref = ins[3 * k:3 * k + 3]
            g = grad_of(entries[k][0], w_ref.shape, vec_ref, mat_ref, norm_ref)
            d, m_, v_ = _adam_update(w_ref[...], g, m_ref[...], v_ref[...])
            for ref, val in zip(outs[4 * k:4 * k + 4], (g, d, m_, v_)):
                ref[...] = val
        w_ref, m_ref, v_ref, g_ref = ins[3 * n:]
        for ref, val in zip(outs[4 * n:], _adam_update(w_ref[...], g_ref[...], m_ref[...], v_ref[...])):
            ref[...] = val

    arrays = [a for e in entries for a in e[1:]] + list(conv)
    out_shape = [jax.ShapeDtypeStruct(e[1].shape, F32) for e in entries for _ in range(4)]
    out_shape += [jax.ShapeDtypeStruct(conv[0].shape, F32)] * 3
    return pl.pallas_call(
        body, name="adamw_replicated", out_shape=tuple(out_shape),
        compiler_params=pltpu.CompilerParams(vmem_limit_bytes=VMEM_LIMIT_BYTES),
    )(vec_sum, mat_sum, norm_grad, *arrays)


def _adamw(w, g, m, v, rows, name):
    r, c = w.shape

    def body(w_ref, g_ref, m_ref, v_ref, d_ref, nm_ref, nv_ref):
        d_ref[...], nm_ref[...], nv_ref[...] = _adam_update(w_ref[...], g_ref[...], m_ref[...], v_ref[...])

    spec = pl.BlockSpec((rows, c), lambda i: (i, 0))
    return pl.pallas_call(
        body, name=name, out_shape=tuple(jax.ShapeDtypeStruct((r, c), F32) for _ in range(3)),
        grid=(r // rows,), in_specs=[spec] * 4, out_specs=(spec,) * 3,
        compiler_params=pltpu.CompilerParams(dimension_semantics=("arbitrary",),
                                             vmem_limit_bytes=VMEM_LIMIT_BYTES),
    )(w, g, m, v)


def _shift_down(ext, s):
    return pltpu.roll(ext, s, 0)


def _tile_shift(v, s):
    rows, cols = v.shape
    tiles = v.reshape(rows // F32_SUBLANES, F32_SUBLANES, cols)
    return pltpu.roll(tiles, s % F32_SUBLANES, 1).reshape(rows, cols)


def _shift_up(ext, s):
    return pltpu.roll(ext, ext.shape[0] - s, 0)


def _lru_gates(xc, wa_ref, ba, wx_ref, bx, lam):
    pa, px = [], []
    for h in range(LRU_HEADS):
        xh = xc[:, h * HEAD_DIM:(h + 1) * HEAD_DIM].astype(BF16)
        pa.append(_dot(xh, wa_ref[h]))
        px.append(_dot(xh, wx_ref[h]))
    r = _sigmoid(jnp.concatenate(pa, axis=1) + ba)
    ig = _sigmoid(jnp.concatenate(px, axis=1) + bx)
    sp = _softplus(-lam)
    log_a = (-LRU_C * r) * sp
    a = jnp.exp(log_a)
    mult = jnp.sqrt(jnp.tanh(-log_a) * (1.0 + a * a))
    return r, ig, a, mult, sp


def _conv(ext, w_ref, b):
    y = b + _shift_down(ext, 3) * w_ref[0:1, :]
    y = y + _shift_down(ext, 2) * w_ref[1:2, :]
    y = y + _shift_down(ext, 1) * w_ref[2:3, :]
    y = y + ext * w_ref[3:4, :]
    return y[CONV_HIST:, :]


def _pool_diff(ext, pos):
    out = []
    for g, k in enumerate(POOL_WINDOWS):
        col = ext[:, g * POOL_GROUP_DIM:(g + 1) * POOL_GROUP_DIM]
        s = col
        for step in range(g + 1):
            s = s + _shift_down(s, 2 ** step)
        count = jnp.minimum(pos + 1, k).astype(F32)
        out.append(s[POOL_HIST:, :] / count - col[POOL_HIST:, :])
    return out


def _pool_mix(diff, pw_ref):
    return jnp.concatenate([_dot(diff[g].astype(BF16), pw_ref[g]) for g in range(len(POOL_WINDOWS))], axis=1)


def _branch_specs(tb, row_map, fixed):
    fixed3 = lambda i: (0, 0, 0)
    return [pl.BlockSpec((CONV_WIDTH, D_MODEL), fixed), pl.BlockSpec((1, D_MODEL), fixed),
            pl.BlockSpec((LRU_HEADS, HEAD_DIM, HEAD_DIM), fixed3), pl.BlockSpec((1, D_MODEL), fixed),
            pl.BlockSpec((LRU_HEADS, HEAD_DIM, HEAD_DIM), fixed3), pl.BlockSpec((1, D_MODEL), fixed),
            pl.BlockSpec((1, D_MODEL), fixed),
            pl.BlockSpec((len(POOL_WINDOWS), POOL_GROUP_DIM, POOL_GROUP_DIM), fixed3),
            pl.BlockSpec((1, POOL_WIDTH), fixed)]


def _branches_fwd(z, weights, seq, tb, shards):
    t = z.shape[0]
    nb = t // tb
    nbe = seq // tb
    groups = tb // F32_SUBLANES
    n = len(shards)

    def body(xa_ref, ga_ref, xb_ref, gb_ref, cw_ref, cb_ref, wa_ref, ba_ref, wx_ref, bx_ref, lam_ref,
             pw_ref, ps_ref, *refs):
        g_ins = refs[:n]
        ya_ref, yb_ref, hl_ref = refs[n:n + 3]
        g_outs = refs[n + 3:2 * n + 3]
        xa_ext, xb_ext, carry, a_s, u_s, send_sems, recv_sems, local_sems = refs[2 * n + 3:]
        blk = pl.program_id(0) % nbe
        start_gather, relay_gather, finish_gather = _gather_steps(shards, g_ins, g_outs, send_sems, recv_sems,
                                                                  local_sems)
        pl.when(pl.program_id(0) == 0)(start_gather)
        pl.when(pl.program_id(0) == nb // 2)(relay_gather)

        @pl.when(blk == 0)
        def _():
            xa_ext[0:CONV_HIST, :] = jnp.zeros((CONV_HIST, D_MODEL), F32)
            xb_ext[0:POOL_HIST, :] = jnp.zeros((POOL_HIST, POOL_WIDTH), F32)
            carry[...] = jnp.zeros_like(carry)

        xa_ext[CONV_HIST:, :] = xa_ref[...]
        xb_ext[POOL_HIST:, :] = xb_ref[...]
        ea = xa_ext[...]
        eb = xb_ext[...]
        xa_ext[0:CONV_HIST, :] = ea[tb:, :]
        xb_ext[0:POOL_HIST, :] = eb[tb:, :]

        xc = _conv(ea, cw_ref, cb_ref[...])
        _, ig, a, mult, _ = _lru_gates(xc, wa_ref, ba_ref[...], wx_ref, bx_ref[...], lam_ref[...])
        u = mult * (ig * xc)
        row8 = lax.broadcasted_iota(jnp.int32, (tb, D_MODEL), 0) % F32_SUBLANES
        for s in (1, 2, 4):
            m = row8 >= s
            u = jnp.where(m, a * _tile_shift(u, s) + u, u)
            a = jnp.where(m, a * _tile_shift(a, s), a)
        a_s[...] = a
        u_s[...] = u

        def step(g, cr):
            sl = pl.ds(pl.multiple_of(g * F32_SUBLANES, F32_SUBLANES), F32_SUBLANES)
            hb = a_s[sl, :] * cr + u_s[sl, :]
            hl_ref[sl, :] = hb
            return jnp.broadcast_to(hb[F32_SUBLANES - 1:F32_SUBLANES, :], (F32_SUBLANES, D_MODEL))

        carry[...] = lax.fori_loop(0, groups, step, carry[...], unroll=4)
        ga = ga_ref[...]
        ya_ref[...] = (hl_ref[...] * (ga * _sigmoid(ga))).astype(BF16)

        pos = blk * tb + lax.broadcasted_iota(jnp.int32, (tb, POOL_GROUP_DIM), 0)
        ypre = _pool_mix(_pool_diff(eb, pos), pw_ref)
        gb = gb_ref[...]
        yb_ref[...] = ((ypre * ps_ref[...]) * (gb * _sigmoid(gb))).astype(BF16)
        pl.when(pl.program_id(0) == nb - 1)(finish_gather)

    row = lambda i: (i, 0)
    fixed = lambda i: (0, 0)
    any_spec = pl.BlockSpec(memory_space=pl.ANY)
    in_specs = [pl.BlockSpec((tb, D_MODEL), lambda i: (i, 0)), pl.BlockSpec((tb, D_MODEL), lambda i: (i, 1)),
                pl.BlockSpec((tb, POOL_WIDTH), lambda i: (i, 4)), pl.BlockSpec((tb, POOL_WIDTH), lambda i: (i, 5)),
                ] + _branch_specs(tb, row, fixed) + [any_spec] * n
    g_shape, g_sems = _gather_shapes(shards)
    return pl.pallas_call(
        body, name="branches_fwd",
        out_shape=tuple([jax.ShapeDtypeStruct((t, D_MODEL), BF16), jax.ShapeDtypeStruct((t, POOL_WIDTH), BF16),
                         jax.ShapeDtypeStruct((t, D_MODEL), F32)] + g_shape),
        grid=(nb,), in_specs=in_specs,
        out_specs=tuple([pl.BlockSpec((tb, D_MODEL), row), pl.BlockSpec((tb, POOL_WIDTH), row),
                         pl.BlockSpec((tb, D_MODEL), row)] + [any_spec] * n),
        scratch_shapes=[pltpu.VMEM((tb + CONV_HIST, D_MODEL), F32), pltpu.VMEM((tb + POOL_HIST, POOL_WIDTH), F32),
                        pltpu.VMEM((F32_SUBLANES, D_MODEL), F32),
                        pltpu.VMEM((tb, D_MODEL), F32), pltpu.VMEM((tb, D_MODEL), F32)] + g_sems,
        compiler_params=pltpu.CompilerParams(dimension_semantics=("arbitrary",),
                                             vmem_limit_bytes=VMEM_LIMIT_BYTES),
    )(z, z, z, z, *weights, *[sh[0] for sh in shards])


def _branches_bwd(z, hl, dya, dyb, dzm, weights, vec_bag, seq, tb):
    t = z.shape[0]
    nb = t // tb
    nbe = seq // tb
    groups = tb // F32_SUBLANES

    def body(xa_ref, xap_ref, ga_ref, xb_ref, xbp_ref, gb_ref, hl_ref, hlp_ref, dya_ref, dyb_ref, dzm_ref,
             cw_ref, cb_ref, wa_ref, ba_ref, wx_ref, bx_ref, lam_ref, pw_ref, ps_ref, vec_in_ref,
             dz_ref, vec_ref, mat_ref,
             xa_ext, xb_ext, hl_ext, a_ext, dxc_ext, dwin_ext, g_carry, b_s, d_s, g_s):
        i = pl.program_id(0)
        blk = (nb - 1 - i) % nbe

        def mat_rows(name, k):
            at = MAT_BAG_AT[name] + k * HEAD_DIM
            return slice(at, at + HEAD_DIM)

        @pl.when(i == 0)
        def _():
            vec_ref[...] = vec_in_ref[...]
            mat_ref[...] = jnp.zeros_like(mat_ref)

        @pl.when(blk == nbe - 1)
        def _():
            a_ext[tb:, :] = jnp.zeros((F32_SUBLANES, D_MODEL), F32)
            dxc_ext[tb:, :] = jnp.zeros((CONV_HIST, D_MODEL), F32)
            dwin_ext[tb:, :] = jnp.zeros((POOL_HIST, POOL_WIDTH), F32)
            g_carry[...] = jnp.zeros_like(g_carry)

        live = (blk > 0).astype(F32)
        xa_ext[0:CONV_HIST, :] = xap_ref[...] * live
        xa_ext[CONV_HIST:, :] = xa_ref[...]
        xb_ext[0:POOL_HIST, :] = xbp_ref[...] * live
        xb_ext[POOL_HIST:, :] = xb_ref[...]
        hl_ext[0:F32_SUBLANES, :] = hlp_ref[...] * live
        hl_ext[F32_SUBLANES:, :] = hl_ref[...]
        ea = xa_ext[...]
        eb = xb_ext[...]

        xc = _conv(ea, cw_ref, cb_ref[...])
        lam = lam_ref[...]
        r, ig, a, mult, sp = _lru_gates(xc, wa_ref, ba_ref[...], wx_ref, bx_ref[...], lam)
        hl = hl_ref[...]
        ga = ga_ref[...]
        sga = _sigmoid(ga)
        dya = dya_ref[...]
        dhl = dya * (ga * sga)
        dz_ref[:, D_MODEL:2 * D_MODEL] = (dya * hl * (sga * (1.0 + ga * (1.0 - sga)))).astype(BF16)

        a_ext[0:tb, :] = a
        b = _shift_up(a_ext[...], 1)[0:tb, :]
        a_ext[tb:, :] = jnp.broadcast_to(a[0:1, :], (F32_SUBLANES, D_MODEL))
        d = dhl
        row8 = lax.broadcasted_iota(jnp.int32, (tb, D_MODEL), 0) % F32_SUBLANES
        for s in (1, 2, 4):
            m = row8 < F32_SUBLANES - s
            d = jnp.where(m, d + b * _tile_shift(d, -s), d)
            b = jnp.where(m, b * _tile_shift(b, -s), b)
        b_s[...] = b
        d_s[...] = d

        def step(k, cr):
            sl = pl.ds(pl.multiple_of((groups - 1 - k) * F32_SUBLANES, F32_SUBLANES), F32_SUBLANES)
            gb_ = d_s[sl, :] + b_s[sl, :] * cr
            g_s[sl, :] = gb_
            return jnp.broadcast_to(gb_[0:1, :], (F32_SUBLANES, D_MODEL))

        g_carry[...] = lax.fori_loop(0, groups, step, g_carry[...], unroll=4)
        gsc = g_s[...]
        da = gsc * _shift_down(hl_ext[...], 1)[F32_SUBLANES:, :]
        dmult = gsc * (ig * xc)
        dig = gsc * (mult * xc)
        dxc = gsc * (mult * ig)
        dlog_a = da * a - (a * a) * dmult / mult
        dr = dlog_a * (-LRU_C * sp)
        vec_ref[_bag_row("lru_lambda"), :] += jnp.sum(dlog_a * (-LRU_C * r), axis=0, keepdims=True)
        dpa = dr * (r * (1.0 - r))
        dpx = dig * (ig * (1.0 - ig))
        vec_ref[_bag_row("lru_b_a"), :] += jnp.sum(dpa, axis=0, keepdims=True)
        vec_ref[_bag_row("lru_b_x"), :] += jnp.sum(dpx, axis=0, keepdims=True)
        back = []
        for h in range(LRU_HEADS):
            cols = slice(h * HEAD_DIM, (h + 1) * HEAD_DIM)
            xh = xc[:, cols].astype(BF16)
            dpa_h = dpa[:, cols].astype(BF16)
            dpx_h = dpx[:, cols].astype(BF16)
            mat_ref[mat_rows("lru_w_a", h), :] += _dot_tn(xh, dpa_h)
            mat_ref[mat_rows("lru_w_x", h), :] += _dot_tn(xh, dpx_h)
            back.append(_dot_nt(dpa_h, wa_ref[h]) + _dot_nt(dpx_h, wx_ref[h]))
        dxc = dxc + jnp.concatenate(back, axis=1)
        vec_ref[_bag_row("conv_b"), :] += jnp.sum(dxc, axis=0, keepdims=True)
        for k in range(CONV_WIDTH):
            tap = _shift_down(ea, CONV_WIDTH - 1 - k)[CONV_HIST:, :] if k < CONV_WIDTH - 1 else ea[CONV_HIST:, :]
            vec_ref[_bag_row("conv_w", k), :] += jnp.sum(dxc * tap, axis=0, keepdims=True)
        dxc_ext[0:tb, :] = dxc
        ed = dxc_ext[...]
        dxa = ed * cw_ref[3:4, :]
        dxa = dxa + _shift_up(ed, 1) * cw_ref[2:3, :]
        dxa = dxa + _shift_up(ed, 2) * cw_ref[1:2, :]
        dxa = dxa + _shift_up(ed, 3) * cw_ref[0:1, :]
        dz_ref[:, 0:D_MODEL] = dxa[0:tb, :].astype(BF16)
        dxc_ext[tb:, :] = dxc[0:CONV_HIST, :]

        pos = blk * tb + lax.broadcasted_iota(jnp.int32, (tb, POOL_GROUP_DIM), 0)
        diff = _pool_diff(eb, pos)
        ypre = _pool_mix(diff, pw_ref)
        ps = ps_ref[...]
        gb = gb_ref[...]
        sgb = _sigmoid(gb)
        dyb = dyb_ref[...]
        dyp = dyb * (gb * sgb)
        dz_ref[:, 2 * D_MODEL + POOL_WIDTH:3 * D_MODEL] = (
            dyb * (ypre * ps) * (sgb * (1.0 + gb * (1.0 - sgb)))).astype(BF16)
        vec_ref[_bag_row("pool_scale"), 0:POOL_WIDTH] += jnp.sum(dyp * ypre, axis=0, keepdims=True)
        dypre = dyp * ps
        for g, k in enumerate(POOL_WINDOWS):
            cols = slice(g * POOL_GROUP_DIM, (g + 1) * POOL_GROUP_DIM)
            dyg = dypre[:, cols].astype(BF16)
            mat_ref[mat_rows("pool_w", g), :] += _dot_tn(diff[g].astype(BF16), dyg)
            ddiff = _dot_nt(dyg, pw_ref[g])
            count = jnp.minimum(pos + 1, k).astype(F32)
            dwin = ddiff / count
            dwin_ext[0:tb, cols] = dwin
            s = dwin_ext[:, cols]
            for step_ in range(g + 1):
                s = s + _shift_up(s, 2 ** step_)
            dz_ref[:, 2 * D_MODEL + g * POOL_GROUP_DIM:2 * D_MODEL + (g + 1) * POOL_GROUP_DIM] = (
                s[0:tb, :] - ddiff).astype(BF16)
            dwin_ext[tb:, cols] = dwin[0:POOL_HIST, :]

        dz_ref[:, 3 * D_MODEL:] = dzm_ref[...]

        @pl.when(i == nb - 1)
        def _():
            row = _bag_row("lru_lambda")
            vec_ref[row, :] = vec_ref[row, :] * (-_sigmoid(-lam))

    rev = lambda i: (nb - 1 - i, 0)
    fixed = lambda i: (0, 0)

    def prev(rows, col):
        per = tb // rows
        return lambda i: (jnp.maximum((nb - 1 - i) * per - 1, 0), col)

    in_specs = [pl.BlockSpec((tb, D_MODEL), lambda i: (nb - 1 - i, 0)),
                pl.BlockSpec((CONV_HIST, D_MODEL), prev(CONV_HIST, 0)),
                pl.BlockSpec((tb, D_MODEL), lambda i: (nb - 1 - i, 1)),
                pl.BlockSpec((tb, POOL_WIDTH), lambda i: (nb - 1 - i, 4)),
                pl.BlockSpec((POOL_HIST, POOL_WIDTH), prev(POOL_HIST, 4)),
                pl.BlockSpec((tb, POOL_WIDTH), lambda i: (nb - 1 - i, 5)),
                pl.BlockSpec((tb, D_MODEL), rev),
                pl.BlockSpec((F32_SUBLANES, D_MODEL), prev(F32_SUBLANES, 0)),
                pl.BlockSpec((tb, D_MODEL), rev), pl.BlockSpec((tb, POOL_WIDTH), rev),
                pl.BlockSpec((tb, 2 * D_MODEL), rev)] + _branch_specs(tb, rev, fixed) + [
                    pl.BlockSpec((VEC_BAG_ROWS, D_MODEL), fixed)]
    out_shape = (jax.ShapeDtypeStruct((t, IN_COLS), BF16), jax.ShapeDtypeStruct((VEC_BAG_ROWS, D_MODEL), F32),
                 jax.ShapeDtypeStruct((MAT_BAG_ROWS, HEAD_DIM), F32))
    out_specs = (pl.BlockSpec((tb, IN_COLS), rev), pl.BlockSpec((VEC_BAG_ROWS, D_MODEL), fixed),
                 pl.BlockSpec((MAT_BAG_ROWS, HEAD_DIM), fixed))
    scratch = [pltpu.VMEM((tb + CONV_HIST, D_MODEL), F32), pltpu.VMEM((tb + POOL_HIST, POOL_WIDTH), F32),
               pltpu.VMEM((tb + F32_SUBLANES, D_MODEL), F32), pltpu.VMEM((tb + F32_SUBLANES, D_MODEL), F32),
               pltpu.VMEM((tb + CONV_HIST, D_MODEL), F32), pltpu.VMEM((tb + POOL_HIST, POOL_WIDTH), F32),
               pltpu.VMEM((F32_SUBLANES, D_MODEL), F32),
               pltpu.VMEM((tb, D_MODEL), F32), pltpu.VMEM((tb, D_MODEL), F32), pltpu.VMEM((tb, D_MODEL), F32)]
    return pl.pallas_call(
        body, name="branches_bwd", out_shape=out_shape, grid=(nb,), in_specs=in_specs, out_specs=out_specs,
        scratch_shapes=scratch, input_output_aliases={len(in_specs) - 1: 1},
        compiler_params=pltpu.CompilerParams(dimension_semantics=("arbitrary",),
                                             vmem_limit_bytes=VMEM_LIMIT_BYTES),
    )(z, z, z, z, z, z, hl, hl, dya, dyb, dzm, *weights, vec_bag)


def _merge_head(x2d, ya, yb, z, p2d, tgt, w_pl, w_pp, w_out, w_pg, w_pe, g2, gf, tb):
    t = x2d.shape[0]
    p_dim = p2d.shape[1]

    def body(x_ref, ya_ref, yb_ref, ma_ref, mb_ref, p_ref, t_ref, wpl_ref, wpp_ref, wout_ref, wpg_ref, wpe_ref,
             g2_ref, gf_ref,
             bag_ref, dxr_ref, dya_ref, dyb_ref, dzm_ref,
             mg_ref, do_ref, hn_ref, dgp_ref, dpe_ref, da_ref, dbm_ref, pbf_ref):
        @pl.when(pl.program_id(0) == 0)
        def _():
            bag_ref[...] = jnp.zeros_like(bag_ref)

        a_ = _dot(ya_ref[...], wpl_ref[...])
        bm = _dot(yb_ref[...], wpp_ref[...])
        sa = _sigmoid(ma_ref[...])
        sb = _sigmoid(mb_ref[...])
        mg = (sa * a_ + sb * bm).astype(BF16)
        mg_ref[...] = mg
        x1 = x_ref[...] + _dot(mg, wout_ref[...])
        xn2, r2 = _rms(x1)
        g2 = g2_ref[...]
        hn = (xn2 * g2).astype(BF16)
        hn_ref[...] = hn
        gate = _sigmoid(_dot(hn, wpg_ref[...]))
        pbf = p_ref[...].astype(BF16)
        pbf_ref[...] = pbf
        pe = _dot(pbf, wpe_ref[...])
        x2 = x1 + gate * pe
        xn3, r3 = _rms(x2)
        gf = gf_ref[...]
        err = xn3 * gf - t_ref[...]
        bag_ref[_bag_rows("loss"), 0:128] += 0.5 * jnp.sum(jnp.mean(err * err, axis=-1))

        dy = err * (1.0 / D_MODEL)
        bag_ref[_bag_row("final_g"), :] += jnp.sum(dy * xn3, axis=0, keepdims=True)
        dx2 = _rms_bwd(dy * gf, xn3, r3)
        dpe_ref[...] = (dx2 * gate).astype(BF16)
        dgp = ((dx2 * pe) * (gate * (1.0 - gate))).astype(BF16)
        dgp_ref[...] = dgp
        dhn = _dot_nt(dgp, wpg_ref[...])
        bag_ref[_bag_row("ple_norm_g"), :] += jnp.sum(dhn * xn2, axis=0, keepdims=True)
        dx1 = dx2 + _rms_bwd(dhn * g2, xn2, r2)
        dxr_ref[...] = dx1
        do = dx1.astype(BF16)
        do_ref[...] = do
        dmg = _dot_nt(do, wout_ref[...])
        da = (dmg * sa).astype(BF16)
        dbm = (dmg * sb).astype(BF16)
        da_ref[...] = da
        dbm_ref[...] = dbm
        dzm_ref[:, 0:D_MODEL] = (dmg * a_ * (sa * (1.0 - sa))).astype(BF16)
        dzm_ref[:, D_MODEL:] = (dmg * bm * (sb * (1.0 - sb))).astype(BF16)
        dya_ref[...] = _dot_nt(da, wpl_ref[...])
        dyb_ref[...] = _dot_nt(dbm, wpp_ref[...])

    row = lambda i: (i, 0)
    fixed = lambda i: (0, 0)

    def resident(shape):
        return pl.BlockSpec(shape, fixed, pipeline_mode=pl.Buffered(1))

    tok = lambda width: pl.BlockSpec((tb, width), row)
    in_specs = [tok(D_MODEL), tok(D_MODEL), tok(POOL_WIDTH),
                pl.BlockSpec((tb, D_MODEL), lambda i: (i, 3)), pl.BlockSpec((tb, D_MODEL), lambda i: (i, 4)),
                tok(p_dim), tok(D_MODEL),
                resident((D_MODEL, D_MODEL)), resident((POOL_WIDTH, D_MODEL)), resident((D_MODEL, D_MODEL)),
                resident((D_MODEL, D_MODEL)), resident((p_dim, D_MODEL)),
                pl.BlockSpec((1, D_MODEL), fixed), pl.BlockSpec((1, D_MODEL), fixed)]
    bf = lambda width: jax.ShapeDtypeStruct((t, width), BF16)
    f32 = lambda width: jax.ShapeDtypeStruct((t, width), F32)
    out_shape = (jax.ShapeDtypeStruct((VEC_BAG_ROWS, D_MODEL), F32),
                 f32(D_MODEL), f32(D_MODEL), f32(POOL_WIDTH), bf(2 * D_MODEL),
                 bf(D_MODEL), bf(D_MODEL), bf(D_MODEL), bf(D_MODEL), bf(D_MODEL), bf(D_MODEL), bf(D_MODEL), bf(p_dim))
    out_specs = (pl.BlockSpec((VEC_BAG_ROWS, D_MODEL), fixed),
                 tok(D_MODEL), tok(D_MODEL), tok(POOL_WIDTH), tok(2 * D_MODEL),
                 tok(D_MODEL), tok(D_MODEL), tok(D_MODEL), tok(D_MODEL), tok(D_MODEL), tok(D_MODEL), tok(D_MODEL),
                 tok(p_dim))
    return pl.pallas_call(
        body, name="merge_head", out_shape=out_shape, grid=(t // tb,), in_specs=in_specs, out_specs=out_specs,
        compiler_params=pltpu.CompilerParams(dimension_semantics=("arbitrary",),
                                             vmem_limit_bytes=VMEM_LIMIT_BYTES),
    )(x2d, ya, yb, z, z, p2d, tgt, w_pl, w_pp, w_out, w_pg, w_pe, g2, gf)


def kernel(x, p, norm_g, w_in, conv_w, conv_b, lru_w_a, lru_b_a, lru_w_x, lru_b_x, lru_lambda, pool_w, pool_scale, w_proj_lru, w_proj_pool, w_out, ple_norm_g, w_ple_gate, w_ple_proj, final_g, loss_target, m_norm_g, m_w_in, m_conv_w, m_conv_b, m_lru_w_a, m_lru_b_a, m_lru_w_x, m_lru_b_x, m_lru_lambda, m_pool_w, m_pool_scale, m_w_proj_lru, m_w_proj_pool, m_w_out, m_ple_norm_g, m_w_ple_gate, m_w_ple_proj, m_final_g, v_norm_g, v_w_in, v_conv_w, v_conv_b, v_lru_w_a, v_lru_b_a, v_lru_w_x, v_lru_b_x, v_lru_lambda, v_pool_w, v_pool_scale, v_w_proj_lru, v_w_proj_pool, v_w_out, v_ple_norm_g, v_w_ple_gate, v_w_ple_proj, v_final_g):
    bsz, seq, _ = x.shape
    t = bsz * seq
    tb_mm = min(1024, seq)
    tb_seq = min(256, seq // 2) if seq >= 512 else seq
    x2d = x.reshape(t, D_MODEL)
    p2d = p.reshape(t, p.shape[-1])
    tgt = loss_target.reshape(t, D_MODEL)
    chip = 2 * lax.axis_index("x") + lax.axis_index("y")

    rest = [(w_proj_lru[0], 0), (w_proj_pool[0], 1), (w_out[0], 0), (w_ple_gate[0], 0), (w_ple_proj[0], 1)]
    z, h_bf, w_in_f, conv_w_f = _in_proj_gather(x2d, norm_g, w_in[0].astype(BF16), [(conv_w[0], 1, False)], tb_mm)

    wa_bf = lru_w_a[0].astype(BF16)
    wx_bf = lru_w_x[0].astype(BF16)
    pw_bf = pool_w[0].astype(BF16)
    branch_w = (conv_w_f, conv_b, wa_bf, lru_b_a.reshape(1, D_MODEL), wx_bf, lru_b_x.reshape(1, D_MODEL),
                lru_lambda, pw_bf, pool_scale)

    ya, yb, hl, w_pl_f, w_pp_f, w_out_f, w_pg_f, w_pe_f = _branches_fwd(
        z, branch_w, seq, tb_seq, [(w.astype(BF16), axis, True) for w, axis in rest])
    (vec_bag, dx_res, dya, dyb, dzm, mg_bf, do_bf, hn_bf, dgp_bf, dpe_bf, da_bf, dbm_bf, p_bf) = _merge_head(
        x2d, ya, yb, z, p2d, tgt, w_pl_f, w_pp_f, w_out_f, w_pg_f, w_pe_f, ple_norm_g, final_g.reshape(1, D_MODEL),
        tb_seq)
    dz, vec_bag, mat_bag = _branches_bwd(z, hl, dya, dyb, dzm, branch_w, vec_bag, seq, tb_seq)

    tb_dw = min(1024, seq)
    def proj_grad(lhs, rhs, name, cols):
        g32, g16 = _weight_grad(lhs, rhs, 1, tb_dw, name)
        if cols:
            return g32[0], True, g16[0]
        rows = g32.shape[1] // 8
        return g32.reshape(8, rows, g32.shape[2]), False, g16.reshape(8, rows, g32.shape[2])

    p_dim = p2d.shape[1]
    proj_parts = [proj_grad(ya, da_bf, "dw_proj_lru", False), proj_grad(yb, dbm_bf, "dw_proj_pool", True),
                  proj_grad(mg_bf, do_bf, "dw_out", False), proj_grad(hn_bf, dgp_bf, "dw_ple_gate", False),
                  proj_grad(p_bf, dpe_bf, "dw_ple_proj", True)]
    nb_dw = t // tb_dw
    g_in, g_in16, r_pl, r_pp, r_out, r_pg, r_pe, vec_mine, mat_mine = _weight_grad(
        h_bf, dz, N_CHIPS, tb_dw, "dw_in",
        reduce=(proj_parts + [(vec_bag.reshape(8, VEC_BAG_ROWS // 8, D_MODEL), False, None),
                              (mat_bag.reshape(8, MAT_BAG_ROWS // 8, HEAD_DIM), False, None)],
                [BF16] * 5 + [F32] * 2,
                (0, nb_dw // 2, 2 * nb_dw - 1, 3 * nb_dw + nb_dw // 2, N_CHIPS * nb_dw - 1)))
    pieces = (8, D_MODEL // 2, IN_COLS // N_CHIPS)
    nb_seq = t // tb_seq
    dx, g_g1, r_in, vec_sum, mat_sum = _in_proj_bwd(
        dz, w_in_f, x2d, dx_res, norm_g, tb_seq,
        reduce=([(g_in.reshape(pieces), False, g_in16.reshape(pieces))], BF16,
                (0, nb_seq // 8, nb_seq // 2, nb_seq - 1, nb_seq - 1)),
        shards=[(vec_mine.reshape(VEC_BAG_ROWS // N_CHIPS, D_MODEL), 0, True),
                (mat_mine.reshape(MAT_BAG_ROWS // N_CHIPS, HEAD_DIM), 0, True)])

    def big_update(w, g2d, m, v, rows, name):
        d, nm, nv = _adamw(w[0], g2d, m[0], v[0], rows, name)
        return g2d[None], d[None], nm[None], nv[None]

    u_in = big_update(w_in, r_in.reshape(D_MODEL, IN_COLS // N_CHIPS), m_w_in, v_w_in, 256, "adamw_w_in")
    u_pl = big_update(w_proj_lru, r_pl.reshape(D_MODEL // N_CHIPS, D_MODEL), m_w_proj_lru, v_w_proj_lru, 256, "adamw_w_proj_lru")
    u_pp = big_update(w_proj_pool, r_pp.reshape(POOL_WIDTH, D_MODEL // N_CHIPS), m_w_proj_pool, v_w_proj_pool, 512, "adamw_w_proj_pool")
    u_out = big_update(w_out, r_out.reshape(D_MODEL // N_CHIPS, D_MODEL), m_w_out, v_w_out, 256, "adamw_w_out")
    u_pg = big_update(w_ple_gate, r_pg.reshape(D_MODEL // N_CHIPS, D_MODEL), m_w_ple_gate, v_w_ple_gate, 256, "adamw_w_ple_gate")
    u_pe = big_update(w_ple_proj, r_pe.reshape(p_dim, D_MODEL // N_CHIPS), m_w_ple_proj, v_w_ple_proj, 256, "adamw_w_ple_proj")

    small = [("norm_g", norm_g, m_norm_g, v_norm_g), ("conv_b", conv_b, m_conv_b, v_conv_b),
             ("lru_w_a", lru_w_a, m_lru_w_a, v_lru_w_a), ("lru_b_a", lru_b_a, m_lru_b_a, v_lru_b_a),
             ("lru_w_x", lru_w_x, m_lru_w_x, v_lru_w_x), ("lru_b_x", lru_b_x, m_lru_b_x, v_lru_b_x),
             ("lru_lambda", lru_lambda, m_lru_lambda, v_lru_lambda), ("pool_w", pool_w, m_pool_w, v_pool_w),
             ("pool_scale", pool_scale, m_pool_scale, v_pool_scale),
             ("ple_norm_g", ple_norm_g, m_ple_norm_g, v_ple_norm_g), ("final_g", final_g, m_final_g, v_final_g)]

    def view(a):
        return a.reshape(-1, a.shape[-1]) if a.ndim != 3 else a[0]

    cw_at = F32_SUBLANES * VEC_BAG_SLOTS.index("conv_w")
    cw_cols = D_MODEL // N_CHIPS
    g_cw = lax.dynamic_slice(vec_sum, (cw_at, chip * cw_cols), (CONV_WIDTH, cw_cols))
    flat = _adamw_replicated(vec_sum, mat_sum, g_g1, [(name,) + tuple(view(a) for a in arrs) for name, *arrs in small],
                             (conv_w[0], m_conv_w[0], v_conv_w[0], g_cw))
    u_small = {name: tuple(flat[4 * k + pick].reshape(arrs[0].shape) for pick in range(4))
               for k, (name, *arrs) in enumerate(small)}
    u_cw = tuple(a[None] for a in (g_cw,) + tuple(flat[4 * len(small):]))

    loss = vec_sum[F32_SUBLANES * VEC_BAG_SLOTS.index("loss"), 0]
    grad_x = dx.reshape(bsz, seq, D_MODEL)

    def ordered(pick):
        s = {name: u[pick] for name, u in u_small.items()}
        return [s["norm_g"], u_in[pick], u_cw[pick], s["conv_b"], s["lru_w_a"], s["lru_b_a"], s["lru_w_x"], s["lru_b_x"],
                s["lru_lambda"], s["pool_w"], s["pool_scale"], u_pl[pick], u_pp[pick], u_out[pick], s["ple_norm_g"],
                u_pg[pick], u_pe[pick], s["final_g"]]

    return (loss, grad_x, *ordered(0), *ordered(1), *ordered(2), *ordered(3))
```

```python
import jax
import jax.numpy as jnp
from jax import lax
from jax.experimental import pallas as pl
from jax.experimental.pallas import tpu as pltpu

F32 = jnp.float32
BF16 = jnp.bfloat16
MESH = pl.DeviceIdType.MESH

D_MODEL = 1024
LRU_HEADS = 8
HEAD_DIM = 128
CONV_WIDTH = 4
LRU_C = 8.0
POOL_WIDTH = 512
POOL_WINDOWS = (2, 4, 8, 16)
POOL_GROUP_DIM = 128
IN_COLS = 5120
N_CHIPS = 4
EPS = 1e-6

ADAM_LR = 0.001
ADAM_B1 = 0.9
ADAM_B2 = 0.999
ADAM_EPS = 1e-08
ADAM_WD = 0.01
ADAM_STEP = 10

F32_SUBLANES = 8
CONV_HIST = 8
POOL_HIST = 16
VMEM_LIMIT_BYTES = 58 * 1024 * 1024
VEC_BAG_SLOTS = ("norm_g", "conv_w", "conv_b", "lru_b_a", "lru_b_x", "lru_lambda", "pool_scale", "ple_norm_g",
                 "final_g", "loss")
VEC_BAG_ROWS = 128
MAT_BAG_AT = {"lru_w_a": 0, "lru_w_x": LRU_HEADS * HEAD_DIM, "pool_w": 2 * LRU_HEADS * HEAD_DIM}
MAT_BAG_ROWS = 2 * LRU_HEADS * HEAD_DIM + len(POOL_WINDOWS) * POOL_GROUP_DIM


def _bag_row(name, k=0):
    at = F32_SUBLANES * VEC_BAG_SLOTS.index(name) + k
    return slice(at, at + 1)


def _bag_rows(name):
    at = F32_SUBLANES * VEC_BAG_SLOTS.index(name)
    return slice(at, at + F32_SUBLANES)


def _dot(a, b):
    return jnp.dot(a, b, preferred_element_type=F32)


def _dot_nt(a, b):
    return lax.dot_general(a, b, (((1,), (1,)), ((), ())), preferred_element_type=F32)


def _dot_tn(a, b):
    return lax.dot_general(a, b, (((0,), (0,)), ((), ())), preferred_element_type=F32)


def _sigmoid(v):
    return jax.nn.sigmoid(v)


def _softplus(v):
    return jnp.maximum(v, 0.0) + jnp.log1p(jnp.exp(-jnp.abs(v)))


def _place():
    return lax.axis_index("x"), lax.axis_index("y"), lax.axis_index("c")


GATHER_SEMS = 6


def _gather_shapes(shards):
    out_shape = []
    for arr, axis, _ in shards:
        r, cols = arr.shape
        out_shape.append(jax.ShapeDtypeStruct((N_CHIPS * r, cols) if axis == 0 else (r, N_CHIPS * cols), arr.dtype))
    n = len(shards)
    sems = [pltpu.SemaphoreType.DMA((n * GATHER_SEMS,)), pltpu.SemaphoreType.DMA((n * GATHER_SEMS,)),
            pltpu.SemaphoreType.DMA((n,))]
    return out_shape, sems


def _gather_steps(shards, ins, outs, send_sems, recv_sems, local_sems):
    n = len(shards)
    x, y, c = _place()
    me, sibling = (x, y, c), (x, y, 1 - c)
    chips = [(x, 1 - y), (1 - x, y), (1 - x, 1 - y)]

    def region(k, cx, cy, hc):
        (r, cols), axis = shards[k][0].shape, shards[k][1]
        j = 2 * cx + cy
        if axis == 0:
            if hc is None:
                return outs[k].at[pl.ds(j * r, r), :]
            return outs[k].at[pl.ds(j * r + hc * (r // 2), r // 2), :]
        if hc is None:
            return outs[k].at[:, pl.ds(j * cols, cols)]
        return outs[k].at[pl.ds(hc * (r // 2), r // 2), pl.ds(j * cols, cols)]

    def remote(k, sem, block, to, src=None):
        dst = region(k, *block)
        return pltpu.make_async_remote_copy(
            src_ref=dst if src is None else src, dst_ref=dst,
            send_sem=send_sems.at[k * GATHER_SEMS + sem], recv_sem=recv_sems.at[k * GATHER_SEMS + sem],
            device_id=to, device_id_type=MESH)

    def first(k, idx):
        r, split = shards[k][0].shape[0], shards[k][2]
        src = ins[k].at[pl.ds(c * (r // 2), r // 2), :] if split else ins[k]
        return remote(k, idx, (x, y, c if split else None), (*chips[idx], c), src=src)

    def relay(k):
        src_chip = (jnp.bitwise_xor(x, 1 - c), jnp.bitwise_xor(y, c))
        dst_chip = (jnp.bitwise_xor(x, c), jnp.bitwise_xor(y, 1 - c))
        return remote(k, 2, (*src_chip, c), (*dst_chip, c))

    def passed(k, idx):
        return remote(k, 3 + idx, (*chips[idx], c), sibling)

    def mine(k):
        return pltpu.make_async_copy(ins[k], region(k, x, y, None), local_sems.at[k])

    def start():
        for k in range(n):
            mine(k).start()
            for idx in range(2 if shards[k][2] else 3):
                first(k, idx).start()

    def relay_on():
        for k in range(n):
            split = shards[k][2]
            for idx in range(2):
                remote(k, idx, (*chips[idx], c if split else None), me).wait_recv()
            if split:
                relay(k).start()
                passed(k, 0).start()
                passed(k, 1).start()

    def finish():
        for k in range(n):
            split = shards[k][2]
            remote(k, 2, (*chips[2], c if split else None), me).wait_recv()
            if split:
                passed(k, 2).start()
        for k in range(n):
            if shards[k][2]:
                for idx in range(3):
                    remote(k, 3 + idx, (*chips[idx], 1 - c), me).wait_recv()
        for k in range(n):
            if shards[k][2]:
                for cp in (first(k, 0), first(k, 1), relay(k), passed(k, 0), passed(k, 1), passed(k, 2)):
                    cp.wait_send()
            else:
                for idx in range(3):
                    first(k, idx).wait_send()
            mine(k).wait()

    return start, relay_on, finish


RS_ADD_ROWS = (64, 32, 16, 8)


N_DEV = 2 * N_CHIPS


def _all_reduce_scratch(shape):
    return [pltpu.VMEM((N_DEV,) + tuple(shape), F32), pltpu.SemaphoreType.DMA((N_DEV - 1,)),
            pltpu.SemaphoreType.DMA((N_DEV - 1,))]


def _all_reduce_tile(v_ref, o_ref, slots, send_sems, recv_sems):
    flips = [(dx, dy, dc) for dx in (0, 1) for dy in (0, 1) for dc in (0, 1)][1:]
    x, y, c = _place()
    mine = 4 * x + 2 * y + c

    def copy(k, to_flip, slot):
        dx, dy, dc = to_flip
        peer = (jnp.bitwise_xor(x, dx), jnp.bitwise_xor(y, dy), jnp.bitwise_xor(c, dc))
        return pltpu.make_async_remote_copy(
            src_ref=v_ref, dst_ref=slots.at[slot], send_sem=send_sems.at[k], recv_sem=recv_sems.at[k],
            device_id=peer, device_id_type=MESH)

    sends = [copy(k, flip, mine) for k, flip in enumerate(flips)]
    for cp in sends:
        cp.start()
    slots[mine] = v_ref[...]
    for k, (dx, dy, dc) in enumerate(flips):
        copy(k, (dx, dy, dc), jnp.bitwise_xor(mine, 4 * dx + 2 * dy + dc)).wait_recv()
    total = slots[0]
    for d in range(1, N_DEV):
        total = total + slots[d]
    o_ref[...] = total
    for cp in sends:
        cp.wait_send()


RS_SEMS = 8
RS_LOCAL_SEMS = 5


def _rs_piece_shape(part):
    arr, cols = part[0], part[1]
    return (arr.shape[0] // 2, arr.shape[1] // N_CHIPS) if cols else tuple(arr.shape[1:])


def _rs_operands(parts):
    return [p[0] for p in parts] + [p[0] if p[2] is None else p[2] for p in parts]


def _rs_wires(parts, wire):
    return list(wire) if isinstance(wire, (list, tuple)) else [wire] * len(parts)


def _rs_shapes(parts, wire):
    n = len(parts)
    shapes = [_rs_piece_shape(p) for p in parts]
    out_shape = [jax.ShapeDtypeStruct((2,) + s, F32) for s in shapes]
    scratch = []
    for lead, kind in ((N_CHIPS, "f32"), (N_CHIPS, "narrow"), (N_CHIPS, "wire"), (None, "f32"), (N_CHIPS, "wire")):
        for s, p, w in zip(shapes, parts, _rs_wires(parts, wire)):
            dtype = {"f32": F32, "narrow": F32 if p[2] is None else p[2].dtype, "wire": w}[kind]
            scratch.append(pltpu.VMEM(s if lead is None else (lead,) + s, dtype))
    scratch += [pltpu.SemaphoreType.DMA((n * RS_SEMS,)), pltpu.SemaphoreType.DMA((n * RS_SEMS,)),
                pltpu.SemaphoreType.DMA((n * RS_LOCAL_SEMS,))]
    return out_shape, scratch


def _rs_steps(parts, ins, outs, scratch):
    n = len(parts)
    own, sib, got, fin, snd = (scratch[k * n:(k + 1) * n] for k in range(5))
    send_sems, recv_sems, local_sems = scratch[5 * n:]
    shapes = [_rs_piece_shape(p) for p in parts]
    x, y, c = _place()
    j_me = 2 * x + y
    me, sibling = (x, y, c), (x, y, 1 - c)

    def piece(a, jj, core, narrow=False):
        ref = ins[n + a] if narrow else ins[a]
        if parts[a][1]:
            r, cl = shapes[a]
            return ref.at[pl.ds(core * r, r), pl.ds(jj * cl, cl)]
        return ref.at[2 * jj + core]

    def remote(a, sem, src, dst, to):
        return pltpu.make_async_remote_copy(
            src_ref=src, dst_ref=dst, send_sem=send_sems.at[a * RS_SEMS + sem],
            recv_sem=recv_sems.at[a * RS_SEMS + sem], device_id=to, device_id_type=MESH)

    def rows_loop(a, fn):
        r = shapes[a][0]
        step = max(s for s in RS_ADD_ROWS if r % s == 0)

        def it(i, carry):
            fn(pl.ds(pl.multiple_of(i * step, step), step))
            return carry

        lax.fori_loop(0, r // step, it, 0)

    def load(a, jj):
        return pltpu.make_async_copy(piece(a, jj, c), own[a].at[jj], local_sems.at[a * RS_LOCAL_SEMS + jj])

    def to_sibling(a, jj):
        return remote(a, jj, piece(a, jj, 1 - c, narrow=True), sib[a].at[jj], sibling)

    near = (jnp.bitwise_xor(x, 1 - c), jnp.bitwise_xor(y, c))
    far = (jnp.bitwise_xor(x, c), jnp.bitwise_xor(y, 1 - c))
    diag = (1 - x, 1 - y)
    FROM_NEAR, FROM_FAR, FEED = 0, 1, 2

    def chip_of(chip):
        return 2 * chip[0] + chip[1]

    def feed(a):
        return remote(a, 4, snd[a].at[chip_of(diag)], got[a].at[FEED], (*near, c))

    def to_near(a):
        return remote(a, 5, snd[a].at[chip_of(near)], got[a].at[FROM_NEAR], (*near, c))

    def to_far(a):
        return remote(a, 6, snd[a].at[chip_of(far)], got[a].at[FROM_FAR], (*far, c))

    def store(a):
        return pltpu.make_async_copy(fin[a], outs[a].at[c], local_sems.at[a * RS_LOCAL_SEMS + 4])

    def result_to_sibling(a):
        return remote(a, 7, fin[a], outs[a].at[c], sibling)

    def exchange():
        for a in range(n):
            for jj in range(N_CHIPS):
                load(a, jj).start()
                to_sibling(a, jj).start()

    def chip_sums():
        for a in range(n):
            for jj in range(N_CHIPS):
                load(a, jj).wait()
                remote(a, jj, sib[a].at[jj], sib[a].at[jj], me).wait_recv()

                def add(sl, a=a, jj=jj):
                    q = own[a][jj, sl, :] + sib[a][jj, sl, :].astype(F32)
                    own[a][jj, sl, :] = q
                    snd[a][jj, sl, :] = q.astype(snd[a].dtype)

                rows_loop(a, add)
        for a in range(n):
            feed(a).start()
        for a in range(n):
            to_near(a).start()

    def relay():
        for a in range(n):
            remote(a, 4, got[a].at[FEED], got[a].at[FEED], me).wait_recv()

            def add(sl, a=a):
                pair = own[a][chip_of(far), sl, :] + got[a][FEED, sl, :].astype(F32)
                snd[a][chip_of(far), sl, :] = pair.astype(snd[a].dtype)

            rows_loop(a, add)
            to_far(a).start()

    def totals():
        for a in range(n):
            remote(a, 5, got[a].at[FROM_NEAR], got[a].at[FROM_NEAR], me).wait_recv()
            remote(a, 6, got[a].at[FROM_FAR], got[a].at[FROM_FAR], me).wait_recv()

            def total(sl, a=a):
                fin[a][sl, :] = (own[a][j_me, sl, :] + got[a][FROM_NEAR, sl, :].astype(F32)) + (
                    got[a][FROM_FAR, sl, :].astype(F32))

            rows_loop(a, total)
            store(a).start()
            result_to_sibling(a).start()

    def finish():
        for a in range(n):
            remote(a, 7, outs[a].at[1 - c], outs[a].at[1 - c], me).wait_recv()
        for a in range(n):
            for jj in range(N_CHIPS):
                to_sibling(a, jj).wait_send()
            for cp in (feed(a), to_near(a), to_far(a), result_to_sibling(a)):
                cp.wait_send()
            store(a).wait()

    return exchange, chip_sums, relay, totals, finish


def _rms(x):
    r = lax.rsqrt(jnp.mean(x * x, axis=-1, keepdims=True) + EPS)
    return x * r, r


def _rms_bwd(dxn, xn, r):
    return r * (dxn - xn * jnp.mean(dxn * xn, axis=-1, keepdims=True))


def _in_proj_gather(x2d, norm_g, w_in_sh, shards, tb):
    t = x2d.shape[0]
    nb = t // tb
    cols = IN_COLS // N_CHIPS
    half = D_MODEL // 2
    n = len(shards)

    def body(x_ref, g_ref, win_ref, *refs):
        ins = refs[:n]
        z_ref, h_ref, wfull_ref = refs[n:n + 3]
        outs = refs[n + 3:2 * n + 3]
        wv, h_buf, send_sems, recv_sems, local_sems, w_send, w_recv, w_local = refs[2 * n + 3:]
        s, i = pl.program_id(0), pl.program_id(1)
        x, y, c = _place()
        me, sibling = (x, y, c), (x, y, 1 - c)
        chips = [(x, 1 - y), (1 - x, y), (1 - x, 1 - y)]

        def w_half(cx, cy, hc):
            return wv.at[2 * cx + cy, pl.ds(hc * half, half), :]

        def w_remote(sem, block, to, src=None):
            dst = w_half(*block)
            return pltpu.make_async_remote_copy(
                src_ref=dst if src is None else src, dst_ref=dst, send_sem=w_send.at[sem],
                recv_sem=w_recv.at[sem], device_id=to, device_id_type=MESH)

        def w_first(idx):
            return w_remote(idx, (x, y, c), (*chips[idx], c), src=win_ref.at[pl.ds(c * half, half), :])

        def w_relay():
            src_chip = (jnp.bitwise_xor(x, 1 - c), jnp.bitwise_xor(y, c))
            dst_chip = (jnp.bitwise_xor(x, c), jnp.bitwise_xor(y, 1 - c))
            return w_remote(2, (*src_chip, c), (*dst_chip, c))

        def w_pass(idx):
            return w_remote(3 + idx, (*chips[idx], c), sibling)

        def w_store(k, cx, cy):
            jj = 2 * cx + cy
            return pltpu.make_async_copy(wv.at[jj], wfull_ref.at[:, pl.ds(jj * cols, cols)], w_local.at[k])

        start_rest, relay_rest, finish_rest = _gather_steps(shards, ins, outs, send_sems, recv_sems, local_sems)
        own = pltpu.make_async_copy(win_ref, wv.at[2 * x + y], w_local.at[4])

        @pl.when((s == 0) & (i == 0))
        def _():
            own.start()
            w_first(0).start()
            w_first(1).start()
            start_rest()
            own.wait()
            w_store(0, x, y).start()

        @pl.when((s == 1) & (i == 0))
        def _():
            w_remote(0, (*chips[0], c), me).wait_recv()
            w_remote(1, (*chips[1], c), me).wait_recv()
            w_relay().start()
            w_pass(0).start()
            w_pass(1).start()
            w_remote(3, (*chips[0], 1 - c), me).wait_recv()
            w_store(1, *chips[0]).start()

        @pl.when((s == 2) & (i == 0))
        def _():
            w_remote(4, (*chips[1], 1 - c), me).wait_recv()
            w_store(2, *chips[1]).start()

        @pl.when((s == 3) & (i == 0))
        def _():
            w_remote(2, (*chips[2], c), me).wait_recv()
            w_pass(2).start()
            w_remote(5, (*chips[2], 1 - c), me).wait_recv()
            w_store(3, *chips[2]).start()

        xn, _ = _rms(x_ref[...])
        h = (xn * g_ref[...]).astype(BF16)
        keep_h = pltpu.make_async_copy(h_buf, h_ref.at[pl.ds(pl.multiple_of(i * tb, tb), tb), :], w_local.at[5])

        @pl.when(s == 0)
        def _():
            h_buf[...] = h
            keep_h.start()

        z_ref[...] = _dot(h, wv[jnp.bitwise_xor(2 * x + y, s)])
        pl.when(s == 0)(keep_h.wait)

        @pl.when((s == N_CHIPS - 1) & (i == nb - 1))
        def _():
            relay_rest()
            finish_rest()
            for cp in (w_first(0), w_first(1), w_relay(), w_pass(0), w_pass(1), w_pass(2)):
                cp.wait_send()
            w_store(0, x, y).wait()
            for idx in range(3):
                w_store(idx + 1, *chips[idx]).wait()

    rest_shape, rest_sems = _gather_shapes(shards)
    out_shape = [jax.ShapeDtypeStruct((t, IN_COLS), F32), jax.ShapeDtypeStruct((t, D_MODEL), BF16),
                 jax.ShapeDtypeStruct((D_MODEL, IN_COLS), BF16)] + rest_shape
    any_spec = pl.BlockSpec(memory_space=pl.ANY)

    def z_map(s, i):
        return (i, jnp.bitwise_xor(2 * lax.axis_index("x") + lax.axis_index("y"), s))

    return pl.pallas_call(
        body, name="in_proj", out_shape=tuple(out_shape),
        grid=(N_CHIPS, nb),
        in_specs=[pl.BlockSpec((tb, D_MODEL), lambda s, i: (i, 0)),
                  pl.BlockSpec((1, D_MODEL), lambda s, i: (0, 0)), any_spec] + [any_spec] * n,
        out_specs=tuple([pl.BlockSpec((tb, cols), z_map), any_spec, any_spec] + [any_spec] * n),
        scratch_shapes=[pltpu.VMEM((N_CHIPS, D_MODEL, cols), BF16), pltpu.VMEM((tb, D_MODEL), BF16)] + rest_sems + [
            pltpu.SemaphoreType.DMA((GATHER_SEMS,)), pltpu.SemaphoreType.DMA((GATHER_SEMS,)),
            pltpu.SemaphoreType.DMA((N_CHIPS + 2,))],
        compiler_params=pltpu.CompilerParams(dimension_semantics=("arbitrary", "arbitrary"),
                                             vmem_limit_bytes=VMEM_LIMIT_BYTES),
    )(x2d, norm_g, w_in_sh, *[sh[0] for sh in shards])


def _in_proj_bwd(dz, w_in, x2d, dx_res, norm_g, tb, reduce, shards):
    t = x2d.shape[0]
    nb = t // tb
    parts, wire, steps = reduce
    n = len(parts)
    k = len(shards)

    def body(dz_ref, w_ref, x_ref, dres_ref, g_ref, *refs):
        at = 2 * n + k
        dx_ref, dg_ref = refs[at:at + 2]
        rs_outs, g_outs = refs[at + 2:at + 2 + n], refs[at + 2 + n:at + 2 + n + k]
        scratch = refs[at + 2 + n + k:]
        rs_scr, g_sems, dg_acc, ar_scr = scratch[:-7], scratch[-7:-4], scratch[-4], scratch[-3:]
        rs = _rs_steps(parts, refs[:2 * n], rs_outs, rs_scr)
        for step, when in zip(rs, steps):
            pl.when(pl.program_id(0) == when)(step)
        gather = _gather_steps(shards, refs[2 * n:at], g_outs, *g_sems)
        for step, when in zip(gather, (0, nb // 2, nb - 1)):
            pl.when(pl.program_id(0) == when)(step)

        @pl.when(pl.program_id(0) == 0)
        def _():
            dg_acc[...] = jnp.zeros_like(dg_acc)

        xn, r = _rms(x_ref[...])
        g = g_ref[...]
        dh = _dot_nt(dz_ref[...], w_ref[...])
        dg_acc[0:1, :] += jnp.sum(dh * xn, axis=0, keepdims=True)
        dx_ref[...] = dres_ref[...] + _rms_bwd(dh * g, xn, r)

        @pl.when(pl.program_id(0) == nb - 1)
        def _():
            _all_reduce_tile(dg_acc, dg_ref, *ar_scr)

    row = lambda i: (i, 0)
    fixed = lambda i: (0, 0)
    rs_shape, rs_scratch = _rs_shapes(parts, wire)
    g_shape, g_sems = _gather_shapes(shards)
    any_spec = pl.BlockSpec(memory_space=pl.ANY)
    return pl.pallas_call(
        body, name="in_proj_bwd",
        out_shape=tuple([jax.ShapeDtypeStruct((t, D_MODEL), F32), jax.ShapeDtypeStruct((F32_SUBLANES, D_MODEL), F32)]
                        + rs_shape + g_shape),
        grid=(nb,),
        in_specs=[pl.BlockSpec((tb, IN_COLS), row),
                  pl.BlockSpec((D_MODEL, IN_COLS), fixed, pipeline_mode=pl.Buffered(1)),
                  pl.BlockSpec((tb, D_MODEL), row), pl.BlockSpec((tb, D_MODEL), row),
                  pl.BlockSpec((1, D_MODEL), fixed)] + [any_spec] * (2 * n + k),
        out_specs=tuple([pl.BlockSpec((tb, D_MODEL), row), pl.BlockSpec((F32_SUBLANES, D_MODEL), fixed)]
                        + [any_spec] * (n + k)),
        scratch_shapes=rs_scratch + g_sems + [pltpu.VMEM((F32_SUBLANES, D_MODEL), F32)] + _all_reduce_scratch(
            (F32_SUBLANES, D_MODEL)),
        compiler_params=pltpu.CompilerParams(dimension_semantics=("arbitrary",),
                                             vmem_limit_bytes=VMEM_LIMIT_BYTES),
    )(dz, w_in, x2d, dx_res, norm_g, *_rs_operands(parts), *[sh[0] for sh in shards])


def _weight_grad(lhs, rhs, n_chunks, tb, name, reduce=None):
    t, k = lhs.shape
    nc = rhs.shape[1] // n_chunks
    nb = t // tb
    parts, wire, steps = reduce if reduce is not None else ([], F32, ())
    n = len(parts)

    def body(l_ref, r_ref, *refs):
        o_ref, o16_ref = refs[2 * n:2 * n + 2]
        if n:
            at = pl.program_id(0) * nb + pl.program_id(1)
            rs = _rs_steps(parts, refs[:2 * n], refs[2 * n + 2:3 * n + 2], refs[3 * n + 2:])
            for step, when in zip(rs, steps):
                pl.when(at == when)(step)

        @pl.when(pl.program_id(1) == 0)
        def _():
            o_ref[...] = jnp.zeros_like(o_ref)

        o_ref[...] += _dot_tn(l_ref[...], r_ref[...])

        @pl.when(pl.program_id(1) == nb - 1)
        def _():
            o16_ref[...] = o_ref[...].astype(BF16)

    rs_shape, rs_scratch = _rs_shapes(parts, wire) if n else ([], [])
    any_spec = pl.BlockSpec(memory_space=pl.ANY)
    chunk = pl.BlockSpec((None, k, nc), lambda j, i: (j, 0, 0))
    return pl.pallas_call(
        body, name=name,
        out_shape=tuple([jax.ShapeDtypeStruct((n_chunks, k, nc), F32), jax.ShapeDtypeStruct((n_chunks, k, nc), BF16)]
                        + rs_shape),
        grid=(n_chunks, nb),
        in_specs=[pl.BlockSpec((tb, k), lambda j, i: (i, 0)), pl.BlockSpec((tb, nc), lambda j, i: (i, j))]
        + [any_spec] * (2 * n),
        out_specs=tuple([chunk, chunk] + [any_spec] * n),
        scratch_shapes=rs_scratch,
        compiler_params=pltpu.CompilerParams(dimension_semantics=("arbitrary", "arbitrary"),
                                             vmem_limit_bytes=VMEM_LIMIT_BYTES),
    )(lhs, rhs, *_rs_operands(parts))


def _adam_update(w, g, m, v):
    m_ = ADAM_B1 * m + (1.0 - ADAM_B1) * g
    v_ = ADAM_B2 * v + (1.0 - ADAM_B2) * jnp.square(g)
    m_hat = m_ / (1.0 - ADAM_B1 ** ADAM_STEP)
    v_hat = v_ / (1.0 - ADAM_B2 ** ADAM_STEP)
    return -ADAM_LR * (m_hat / (jnp.sqrt(v_hat) + ADAM_EPS) + ADAM_WD * w), m_, v_


def _adamw_replicated(vec_sum, mat_sum, norm_grad, entries, conv):
    n = len(entries)

    def grad_of(name, shape, vec_ref, mat_ref, norm_ref):
        if name == "norm_g":
            return norm_ref[0:1, :]
        if name in MAT_BAG_AT:
            return mat_ref[MAT_BAG_AT[name]:MAT_BAG_AT[name] + shape[0], :]
        if shape[0] == 1:
            return vec_ref[_bag_row(name), 0:shape[1]]
        return jnp.concatenate([vec_ref[_bag_row(name), h * shape[1]:(h + 1) * shape[1]] for h in range(shape[0])],
                               axis=0)

    def body(vec_ref, mat_ref, norm_ref, *refs):
        ins, outs = refs[:3 * n + 4], refs[3 * n + 4:]
        for k in range(n):
            w_ref, m_ref, v_ref = ins[3 * k:3 * k + 3]
            g = grad_of(entries[k][0], w_ref.shape, vec_ref, mat_ref, norm_ref)
            d, m_, v_ = _adam_update(w_ref[...], g, m_ref[...], v_ref[...])
            for ref, val in zip(outs[4 * k:4 * k + 4], (g, d, m_, v_)):
                ref[...] = val
        w_ref, m_ref, v_ref, g_ref = ins[3 * n:]
        for ref, val in zip(outs[4 * n:], _adam_update(w_ref[...], g_ref[...], m_ref[...], v_ref[...])):
            ref[...] = val

    arrays = [a for e in entries for a in e[1:]] + list(conv)
    out_shape = [jax.ShapeDtypeStruct(e[1].shape, F32) for e in entries for _ in range(4)]
    out_shape += [jax.ShapeDtypeStruct(conv[0].shape, F32)] * 3
    return pl.pallas_call(
        body, name="adamw_replicated", out_shape=tuple(out_shape),
        compiler_params=pltpu.CompilerParams(vmem_limit_bytes=VMEM_LIMIT_BYTES),
    )(vec_sum, mat_sum, norm_grad, *arrays)


def _adamw(w, g, m, v, rows, name):
    r, c = w.shape

    def body(w_ref, g_ref, m_ref, v_ref, d_ref, nm_ref, nv_ref):
        d_ref[...], nm_ref[...], nv_ref[...] = _adam_update(w_ref[...], g_ref[...], m_ref[...], v_ref[...])

    spec = pl.BlockSpec((rows, c), lambda i: (i, 0))
    return pl.pallas_call(
        body, name=name, out_shape=tuple(jax.ShapeDtypeStruct((r, c), F32) for _ in range(3)),
        grid=(r // rows,), in_specs=[spec] * 4, out_specs=(spec,) * 3,
        compiler_params=pltpu.CompilerParams(dimension_semantics=("arbitrary",),
                                             vmem_limit_bytes=VMEM_LIMIT_BYTES),
    )(w, g, m, v)


def _shift_down(ext, s):
    return pltpu.roll(ext, s, 0)


def _tile_shift(v, s):
    rows, cols = v.shape
    tiles = v.reshape(rows // F32_SUBLANES, F32_SUBLANES, cols)
    return pltpu.roll(tiles, s % F32_SUBLANES, 1).reshape(rows, cols)


def _shift_up(ext, s):
    return pltpu.roll(ext, ext.shape[0] - s, 0)


def _lru_gates(xc, wa_ref, ba, wx_ref, bx, lam):
    pa, px = [], []
    for h in range(LRU_HEADS):
        xh = xc[:, h * HEAD_DIM:(h + 1) * HEAD_DIM].astype(BF16)
        pa.append(_dot(xh, wa_ref[h]))
        px.append(_dot(xh, wx_ref[h]))
    r = _sigmoid(jnp.concatenate(pa, axis=1) + ba)
    ig = _sigmoid(jnp.concatenate(px, axis=1) + bx)
    sp = _softplus(-lam)
    log_a = (-LRU_C * r) * sp
    a = jnp.exp(log_a)
    mult = jnp.sqrt(jnp.tanh(-log_a) * (1.0 + a * a))
    return r, ig, a, mult, sp


def _conv(ext, w_ref, b):
    y = b + _shift_down(ext, 3) * w_ref[0:1, :]
    y = y + _shift_down(ext, 2) * w_ref[1:2, :]
    y = y + _shift_down(ext, 1) * w_ref[2:3, :]
    y = y + ext * w_ref[3:4, :]
    return y[CONV_HIST:, :]


def _pool_diff(ext, pos):
    out = []
    for g, k in enumerate(POOL_WINDOWS):
        col = ext[:, g * POOL_GROUP_DIM:(g + 1) * POOL_GROUP_DIM]
        s = col
        for step in range(g + 1):
            s = s + _shift_down(s, 2 ** step)
        count = jnp.minimum(pos + 1, k).astype(F32)
        out.append(s[POOL_HIST:, :] / count - col[POOL_HIST:, :])
    return out


def _pool_mix(diff, pw_ref):
    return jnp.concatenate([_dot(diff[g].astype(BF16), pw_ref[g]) for g in range(len(POOL_WINDOWS))], axis=1)


def _branch_specs(tb, row_map, fixed):
    fixed3 = lambda i: (0, 0, 0)
    return [pl.BlockSpec((CONV_WIDTH, D_MODEL), fixed), pl.BlockSpec((1, D_MODEL), fixed),
            pl.BlockSpec((LRU_HEADS, HEAD_DIM, HEAD_DIM), fixed3), pl.BlockSpec((1, D_MODEL), fixed),
            pl.BlockSpec((LRU_HEADS, HEAD_DIM, HEAD_DIM), fixed3), pl.BlockSpec((1, D_MODEL), fixed),
            pl.BlockSpec((1, D_MODEL), fixed),
            pl.BlockSpec((len(POOL_WINDOWS), POOL_GROUP_DIM, POOL_GROUP_DIM), fixed3),
            pl.BlockSpec((1, POOL_WIDTH), fixed)]


def _branches_fwd(z, weights, seq, tb, shards):
    t = z.shape[0]
    nb = t // tb
    nbe = seq // tb
    groups = tb // F32_SUBLANES
    n = len(shards)

    def body(xa_ref, ga_ref, xb_ref, gb_ref, cw_ref, cb_ref, wa_ref, ba_ref, wx_ref, bx_ref, lam_ref,
             pw_ref, ps_ref, *refs):
        g_ins = refs[:n]
        ya_ref, yb_ref, hl_ref = refs[n:n + 3]
        g_outs = refs[n + 3:2 * n + 3]
        xa_ext, xb_ext, carry, a_s, u_s, send_sems, recv_sems, local_sems = refs[2 * n + 3:]
        blk = pl.program_id(0) % nbe
        start_gather, relay_gather, finish_gather = _gather_steps(shards, g_ins, g_outs, send_sems, recv_sems,
                                                                  local_sems)
        pl.when(pl.program_id(0) == 0)(start_gather)
        pl.when(pl.program_id(0) == nb // 2)(relay_gather)

        @pl.when(blk == 0)
        def _():
            xa_ext[0:CONV_HIST, :] = jnp.zeros((CONV_HIST, D_MODEL), F32)
            xb_ext[0:POOL_HIST, :] = jnp.zeros((POOL_HIST, POOL_WIDTH), F32)
            carry[...] = jnp.zeros_like(carry)

        xa_ext[CONV_HIST:, :] = xa_ref[...]
        xb_ext[POOL_HIST:, :] = xb_ref[...]
        ea = xa_ext[...]
        eb = xb_ext[...]
        xa_ext[0:CONV_HIST, :] = ea[tb:, :]
        xb_ext[0:POOL_HIST, :] = eb[tb:, :]

        xc = _conv(ea, cw_ref, cb_ref[...])
        _, ig, a, mult, _ = _lru_gates(xc, wa_ref, ba_ref[...], wx_ref, bx_ref[...], lam_ref[...])
        u = mult * (ig * xc)
        row8 = lax.broadcasted_iota(jnp.int32, (tb, D_MODEL), 0) % F32_SUBLANES
        for s in (1, 2, 4):
            m = row8 >= s
            u = jnp.where(m, a * _tile_shift(u, s) + u, u)
            a = jnp.where(m, a * _tile_shift(a, s), a)
        a_s[...] = a
        u_s[...] = u

        def step(g, cr):
            sl = pl.ds(pl.multiple_of(g * F32_SUBLANES, F32_SUBLANES), F32_SUBLANES)
            hb = a_s[sl, :] * cr + u_s[sl, :]
            hl_ref[sl, :] = hb
            return jnp.broadcast_to(hb[F32_SUBLANES - 1:F32_SUBLANES, :], (F32_SUBLANES, D_MODEL))

        carry[...] = lax.fori_loop(0, groups, step, carry[...], unroll=4)
        ga = ga_ref[...]
        ya_ref[...] = (hl_ref[...] * (ga * _sigmoid(ga))).astype(BF16)

        pos = blk * tb + lax.broadcasted_iota(jnp.int32, (tb, POOL_GROUP_DIM), 0)
        ypre = _pool_mix(_pool_diff(eb, pos), pw_ref)
        gb = gb_ref[...]
        yb_ref[...] = ((ypre * ps_ref[...]) * (gb * _sigmoid(gb))).astype(BF16)
        pl.when(pl.program_id(0) == nb - 1)(finish_gather)

    row = lambda i: (i, 0)
    fixed = lambda i: (0, 0)
    any_spec = pl.BlockSpec(memory_space=pl.ANY)
    in_specs = [pl.BlockSpec((tb, D_MODEL), lambda i: (i, 0)), pl.BlockSpec((tb, D_MODEL), lambda i: (i, 1)),
                pl.BlockSpec((tb, POOL_WIDTH), lambda i: (i, 4)), pl.BlockSpec((tb, POOL_WIDTH), lambda i: (i, 5)),
                ] + _branch_specs(tb, row, fixed) + [any_spec] * n
    g_shape, g_sems = _gather_shapes(shards)
    return pl.pallas_call(
        body, name="branches_fwd",
        out_shape=tuple([jax.ShapeDtypeStruct((t, D_MODEL), BF16), jax.ShapeDtypeStruct((t, POOL_WIDTH), BF16),
                         jax.ShapeDtypeStruct((t, D_MODEL), F32)] + g_shape),
        grid=(nb,), in_specs=in_specs,
        out_specs=tuple([pl.BlockSpec((tb, D_MODEL), row), pl.BlockSpec((tb, POOL_WIDTH), row),
                         pl.BlockSpec((tb, D_MODEL), row)] + [any_spec] * n),
        scratch_shapes=[pltpu.VMEM((tb + CONV_HIST, D_MODEL), F32), pltpu.VMEM((tb + POOL_HIST, POOL_WIDTH), F32),
                        pltpu.VMEM((F32_SUBLANES, D_MODEL), F32),
                        pltpu.VMEM((tb, D_MODEL), F32), pltpu.VMEM((tb, D_MODEL), F32)] + g_sems,
        compiler_params=pltpu.CompilerParams(dimension_semantics=("arbitrary",),
                                             vmem_limit_bytes=VMEM_LIMIT_BYTES),
    )(z, z, z, z, *weights, *[sh[0] for sh in shards])


def _branches_bwd(z, hl, dya, dyb, dzm, weights, vec_bag, seq, tb):
    t = z.shape[0]
    nb = t // tb
    nbe = seq // tb
    groups = tb // F32_SUBLANES

    def body(xa_ref, xap_ref, ga_ref, xb_ref, xbp_ref, gb_ref, hl_ref, hlp_ref, dya_ref, dyb_ref, dzm_ref,
             cw_ref, cb_ref, wa_ref, ba_ref, wx_ref, bx_ref, lam_ref, pw_ref, ps_ref, vec_in_ref,
             dz_ref, vec_ref, mat_ref,
             xa_ext, xb_ext, hl_ext, a_ext, dxc_ext, dwin_ext, g_carry, b_s, d_s, g_s):
        i = pl.program_id(0)
        blk = (nb - 1 - i) % nbe

        def mat_rows(name, k):
            at = MAT_BAG_AT[name] + k * HEAD_DIM
            return slice(at, at + HEAD_DIM)

        @pl.when(i == 0)
        def _():
            vec_ref[...] = vec_in_ref[...]
            mat_ref[...] = jnp.zeros_like(mat_ref)

        @pl.when(blk == nbe - 1)
        def _():
            a_ext[tb:, :] = jnp.zeros((F32_SUBLANES, D_MODEL), F32)
            dxc_ext[tb:, :] = jnp.zeros((CONV_HIST, D_MODEL), F32)
            dwin_ext[tb:, :] = jnp.zeros((POOL_HIST, POOL_WIDTH), F32)
            g_carry[...] = jnp.zeros_like(g_carry)

        live = (blk > 0).astype(F32)
        xa_ext[0:CONV_HIST, :] = xap_ref[...] * live
        xa_ext[CONV_HIST:, :] = xa_ref[...]
        xb_ext[0:POOL_HIST, :] = xbp_ref[...] * live
        xb_ext[POOL_HIST:, :] = xb_ref[...]
        hl_ext[0:F32_SUBLANES, :] = hlp_ref[...] * live
        hl_ext[F32_SUBLANES:, :] = hl_ref[...]
        ea = xa_ext[...]
        eb = xb_ext[...]

        xc = _conv(ea, cw_ref, cb_ref[...])
        lam = lam_ref[...]
        r, ig, a, mult, sp = _lru_gates(xc, wa_ref, ba_ref[...], wx_ref, bx_ref[...], lam)
        hl = hl_ref[...]
        ga = ga_ref[...]
        sga = _sigmoid(ga)
        dya = dya_ref[...]
        dhl = dya * (ga * sga)
        dz_ref[:, D_MODEL:2 * D_MODEL] = (dya * hl * (sga * (1.0 + ga * (1.0 - sga)))).astype(BF16)

        a_ext[0:tb, :] = a
        b = _shift_up(a_ext[...], 1)[0:tb, :]
        a_ext[tb:, :] = jnp.broadcast_to(a[0:1, :], (F32_SUBLANES, D_MODEL))
        d = dhl
        row8 = lax.broadcasted_iota(jnp.int32, (tb, D_MODEL), 0) % F32_SUBLANES
        for s in (1, 2, 4):
            m = row8 < F32_SUBLANES - s
            d = jnp.where(m, d + b * _tile_shift(d, -s), d)
            b = jnp.where(m, b * _tile_shift(b, -s), b)
        b_s[...] = b
        d_s[...] = d

        def step(k, cr):
            sl = pl.ds(pl.multiple_of((groups - 1 - k) * F32_SUBLANES, F32_SUBLANES), F32_SUBLANES)
            gb_ = d_s[sl, :] + b_s[sl, :] * cr
            g_s[sl, :] = gb_
            return jnp.broadcast_to(gb_[0:1, :], (F32_SUBLANES, D_MODEL))

        g_carry[...] = lax.fori_loop(0, groups, step, g_carry[...], unroll=4)
        gsc = g_s[...]
        da = gsc * _shift_down(hl_ext[...], 1)[F32_SUBLANES:, :]
        dmult = gsc * (ig * xc)
        dig = gsc * (mult * xc)
        dxc = gsc * (mult * ig)
        dlog_a = da * a - (a * a) * dmult / mult
        dr = dlog_a * (-LRU_C * sp)
        vec_ref[_bag_row("lru_lambda"), :] += jnp.sum(dlog_a * (-LRU_C * r), axis=0, keepdims=True)
        dpa = dr * (r * (1.0 - r))
        dpx = dig * (ig * (1.0 - ig))
        vec_ref[_bag_row("lru_b_a"), :] += jnp.sum(dpa, axis=0, keepdims=True)
        vec_ref[_bag_row("lru_b_x"), :] += jnp.sum(dpx, axis=0, keepdims=True)
        back = []
        for h in range(LRU_HEADS):
            cols = slice(h * HEAD_DIM, (h + 1) * HEAD_DIM)
            xh = xc[:, cols].astype(BF16)
            dpa_h = dpa[:, cols].astype(BF16)
            dpx_h = dpx[:, cols].astype(BF16)
            mat_ref[mat_rows("lru_w_a", h), :] += _dot_tn(xh, dpa_h)
            mat_ref[mat_rows("lru_w_x", h), :] += _dot_tn(xh, dpx_h)
            back.append(_dot_nt(dpa_h, wa_ref[h]) + _dot_nt(dpx_h, wx_ref[h]))
        dxc = dxc + jnp.concatenate(back, axis=1)
        vec_ref[_bag_row("conv_b"), :] += jnp.sum(dxc, axis=0, keepdims=True)
        for k in range(CONV_WIDTH):
            tap = _shift_down(ea, CONV_WIDTH - 1 - k)[CONV_HIST:, :] if k < CONV_WIDTH - 1 else ea[CONV_HIST:, :]
            vec_ref[_bag_row("conv_w", k), :] += jnp.sum(dxc * tap, axis=0, keepdims=True)
        dxc_ext[0:tb, :] = dxc
        ed = dxc_ext[...]
        dxa = ed * cw_ref[3:4, :]
        dxa = dxa + _shift_up(ed, 1) * cw_ref[2:3, :]
        dxa = dxa + _shift_up(ed, 2) * cw_ref[1:2, :]
        dxa = dxa + _shift_up(ed, 3) * cw_ref[0:1, :]
        dz_ref[:, 0:D_MODEL] = dxa[0:tb, :].astype(BF16)
        dxc_ext[tb:, :] = dxc[0:CONV_HIST, :]

        pos = blk * tb + lax.broadcasted_iota(jnp.int32, (tb, POOL_GROUP_DIM), 0)
        diff = _pool_diff(eb, pos)
        ypre = _pool_mix(diff, pw_ref)
        ps = ps_ref[...]
        gb = gb_ref[...]
        sgb = _sigmoid(gb)
        dyb = dyb_ref[...]
        dyp = dyb * (gb * sgb)
        dz_ref[:, 2 * D_MODEL + POOL_WIDTH:3 * D_MODEL] = (
            dyb * (ypre * ps) * (sgb * (1.0 + gb * (1.0 - sgb)))).astype(BF16)
        vec_ref[_bag_row("pool_scale"), 0:POOL_WIDTH] += jnp.sum(dyp * ypre, axis=0, keepdims=True)
        dypre = dyp * ps
        for g, k in enumerate(POOL_WINDOWS):
            cols = slice(g * POOL_GROUP_DIM, (g + 1) * POOL_GROUP_DIM)
            dyg = dypre[:, cols].astype(BF16)
            mat_ref[mat_rows("pool_w", g), :] += _dot_tn(diff[g].astype(BF16), dyg)
            ddiff = _dot_nt(dyg, pw_ref[g])
            count = jnp.minimum(pos + 1, k).astype(F32)
            dwin = ddiff / count
            dwin_ext[0:tb, cols] = dwin
            s = dwin_ext[:, cols]
            for step_ in range(g + 1):
                s = s + _shift_up(s, 2 ** step_)
            dz_ref[:, 2 * D_MODEL + g * POOL_GROUP_DIM:2 * D_MODEL + (g + 1) * POOL_GROUP_DIM] = (
                s[0:tb, :] - ddiff).astype(BF16)
            dwin_ext[tb:, cols] = dwin[0:POOL_HIST, :]

        dz_ref[:, 3 * D_MODEL:] = dzm_ref[...]

        @pl.when(i == nb - 1)
        def _():
            row = _bag_row("lru_lambda")
            vec_ref[row, :] = vec_ref[row, :] * (-_sigmoid(-lam))

    rev = lambda i: (nb - 1 - i, 0)
    fixed = lambda i: (0, 0)

    def prev(rows, col):
        per = tb // rows
        return lambda i: (jnp.maximum((nb - 1 - i) * per - 1, 0), col)

    in_specs = [pl.BlockSpec((tb, D_MODEL), lambda i: (nb - 1 - i, 0)),
                pl.BlockSpec((CONV_HIST, D_MODEL), prev(CONV_HIST, 0)),
                pl.BlockSpec((tb, D_MODEL), lambda i: (nb - 1 - i, 1)),
                pl.BlockSpec((tb, POOL_WIDTH), lambda i: (nb - 1 - i, 4)),
                pl.BlockSpec((POOL_HIST, POOL_WIDTH), prev(POOL_HIST, 4)),
                pl.BlockSpec((tb, POOL_WIDTH), lambda i: (nb - 1 - i, 5)),
                pl.BlockSpec((tb, D_MODEL), rev),
                pl.BlockSpec((F32_SUBLANES, D_MODEL), prev(F32_SUBLANES, 0)),
                pl.BlockSpec((tb, D_MODEL), rev), pl.BlockSpec((tb, POOL_WIDTH), rev),
                pl.BlockSpec((tb, 2 * D_MODEL), rev)] + _branch_specs(tb, rev, fixed) + [
                    pl.BlockSpec((VEC_BAG_ROWS, D_MODEL), fixed)]
    out_shape = (jax.ShapeDtypeStruct((t, IN_COLS), BF16), jax.ShapeDtypeStruct((VEC_BAG_ROWS, D_MODEL), F32),
                 jax.ShapeDtypeStruct((MAT_BAG_ROWS, HEAD_DIM), F32))
    out_specs = (pl.BlockSpec((tb, IN_COLS), rev), pl.BlockSpec((VEC_BAG_ROWS, D_MODEL), fixed),
                 pl.BlockSpec((MAT_BAG_ROWS, HEAD_DIM), fixed))
    scratch = [pltpu.VMEM((tb + CONV_HIST, D_MODEL), F32), pltpu.VMEM((tb + POOL_HIST, POOL_WIDTH), F32),
               pltpu.VMEM((tb + F32_SUBLANES, D_MODEL), F32), pltpu.VMEM((tb + F32_SUBLANES, D_MODEL), F32),
               pltpu.VMEM((tb + CONV_HIST, D_MODEL), F32), pltpu.VMEM((tb + POOL_HIST, POOL_WIDTH), F32),
               pltpu.VMEM((F32_SUBLANES, D_MODEL), F32),
               pltpu.VMEM((tb, D_MODEL), F32), pltpu.VMEM((tb, D_MODEL), F32), pltpu.VMEM((tb, D_MODEL), F32)]
    return pl.pallas_call(
        body, name="branches_bwd", out_shape=out_shape, grid=(nb,), in_specs=in_specs, out_specs=out_specs,
        scratch_shapes=scratch, input_output_aliases={len(in_specs) - 1: 1},
        compiler_params=pltpu.CompilerParams(dimension_semantics=("arbitrary",),
                                             vmem_limit_bytes=VMEM_LIMIT_BYTES),
    )(z, z, z, z, z, z, hl, hl, dya, dyb, dzm, *weights, vec_bag)


def _merge_head(x2d, ya, yb, z, p2d, tgt, w_pl, w_pp, w_out, w_pg, w_pe, g2, gf, tb):
    t = x2d.shape[0]
    p_dim = p2d.shape[1]

    def body(x_ref, ya_ref, yb_ref, ma_ref, mb_ref, p_ref, t_ref, wpl_ref, wpp_ref, wout_ref, wpg_ref, wpe_ref,
             g2_ref, gf_ref,
             bag_ref, dxr_ref, dya_ref, dyb_ref, dzm_ref,
             mg_ref, do_ref, hn_ref, dgp_ref, dpe_ref, da_ref, dbm_ref, pbf_ref):
        @pl.when(pl.program_id(0) == 0)
        def _():
            bag_ref[...] = jnp.zeros_like(bag_ref)

        a_ = _dot(ya_ref[...], wpl_ref[...])
        bm = _dot(yb_ref[...], wpp_ref[...])
        sa = _sigmoid(ma_ref[...])
        sb = _sigmoid(mb_ref[...])
        mg = (sa * a_ + sb * bm).astype(BF16)
        mg_ref[...] = mg
        x1 = x_ref[...] + _dot(mg, wout_ref[...])
        xn2, r2 = _rms(x1)
        g2 = g2_ref[...]
        hn = (xn2 * g2).astype(BF16)
        hn_ref[...] = hn
        gate = _sigmoid(_dot(hn, wpg_ref[...]))
        pbf = p_ref[...].astype(BF16)
        pbf_ref[...] = pbf
        pe = _dot(pbf, wpe_ref[...])
        x2 = x1 + gate * pe
        xn3, r3 = _rms(x2)
        gf = gf_ref[...]
        err = xn3 * gf - t_ref[...]
        bag_ref[_bag_rows("loss"), 0:128] += 0.5 * jnp.sum(jnp.mean(err * err, axis=-1))

        dy = err * (1.0 / D_MODEL)
        bag_ref[_bag_row("final_g"), :] += jnp.sum(dy * xn3, axis=0, keepdims=True)
        dx2 = _rms_bwd(dy * gf, xn3, r3)
        dpe_ref[...] = (dx2 * gate).astype(BF16)
        dgp = ((dx2 * pe) * (gate * (1.0 - gate))).astype(BF16)
        dgp_ref[...] = dgp
        dhn = _dot_nt(dgp, wpg_ref[...])
        bag_ref[_bag_row("ple_norm_g"), :] += jnp.sum(dhn * xn2, axis=0, keepdims=True)
        dx1 = dx2 + _rms_bwd(dhn * g2, xn2, r2)
        dxr_ref[...] = dx1
        do = dx1.astype(BF16)
        do_ref[...] = do
        dmg = _dot_nt(do, wout_ref[...])
        da = (dmg * sa).astype(BF16)
        dbm = (dmg * sb).astype(BF16)
        da_ref[...] = da
        dbm_ref[...] = dbm
        dzm_ref[:, 0:D_MODEL] = (dmg * a_ * (sa * (1.0 - sa))).astype(BF16)
        dzm_ref[:, D_MODEL:] = (dmg * bm * (sb * (1.0 - sb))).astype(BF16)
        dya_ref[...] = _dot_nt(da, wpl_ref[...])
        dyb_ref[...] = _dot_nt(dbm, wpp_ref[...])

    row = lambda i: (i, 0)
    fixed = lambda i: (0, 0)

    def resident(shape):
        return pl.BlockSpec(shape, fixed, pipeline_mode=pl.Buffered(1))

    tok = lambda width: pl.BlockSpec((tb, width), row)
    in_specs = [tok(D_MODEL), tok(D_MODEL), tok(POOL_WIDTH),
                pl.BlockSpec((tb, D_MODEL), lambda i: (i, 3)), pl.BlockSpec((tb, D_MODEL), lambda i: (i, 4)),
                tok(p_dim), tok(D_MODEL),
                resident((D_MODEL, D_MODEL)), resident((POOL_WIDTH, D_MODEL)), resident((D_MODEL, D_MODEL)),
                resident((D_MODEL, D_MODEL)), resident((p_dim, D_MODEL)),
                pl.BlockSpec((1, D_MODEL), fixed), pl.BlockSpec((1, D_MODEL), fixed)]
    bf = lambda width: jax.ShapeDtypeStruct((t, width), BF16)
    f32 = lambda width: jax.ShapeDtypeStruct((t, width), F32)
    out_shape = (jax.ShapeDtypeStruct((VEC_BAG_ROWS, D_MODEL), F32),
                 f32(D_MODEL), f32(D_MODEL), f32(POOL_WIDTH), bf(2 * D_MODEL),
                 bf(D_MODEL), bf(D_MODEL), bf(D_MODEL), bf(D_MODEL), bf(D_MODEL), bf(D_MODEL), bf(D_MODEL), bf(p_dim))
    out_specs = (pl.BlockSpec((VEC_BAG_ROWS, D_MODEL), fixed),
                 tok(D_MODEL), tok(D_MODEL), tok(POOL_WIDTH), tok(2 * D_MODEL),
                 tok(D_MODEL), tok(D_MODEL), tok(D_MODEL), tok(D_MODEL), tok(D_MODEL), tok(D_MODEL), tok(D_MODEL),
                 tok(p_dim))
    return pl.pallas_call(
        body, name="merge_head", out_shape=out_shape, grid=(t // tb,), in_specs=in_specs, out_specs=out_specs,
        compiler_params=pltpu.CompilerParams(dimension_semantics=("arbitrary",),
                                             vmem_limit_bytes=VMEM_LIMIT_BYTES),
    )(x2d, ya, yb, z, z, p2d, tgt, w_pl, w_pp, w_out, w_pg, w_pe, g2, gf)


def kernel(x, p, norm_g, w_in, conv_w, conv_b, lru_w_a, lru_b_a, lru_w_x, lru_b_x, lru_lambda, pool_w, pool_scale, w_proj_lru, w_proj_pool, w_out, ple_norm_g, w_ple_gate, w_ple_proj, final_g, loss_target, m_norm_g, m_w_in, m_conv_w, m_conv_b, m_lru_w_a, m_lru_b_a, m_lru_w_x, m_lru_b_x, m_lru_lambda, m_pool_w, m_pool_scale, m_w_proj_lru, m_w_proj_pool, m_w_out, m_ple_norm_g, m_w_ple_gate, m_w_ple_proj, m_final_g, v_norm_g, v_w_in, v_conv_w, v_conv_b, v_lru_w_a, v_lru_b_a, v_lru_w_x, v_lru_b_x, v_lru_lambda, v_pool_w, v_pool_scale, v_w_proj_lru, v_w_proj_pool, v_w_out, v_ple_norm_g, v_w_ple_gate, v_w_ple_proj, v_final_g):
    bsz, seq, _ = x.shape
    t = bsz * seq
    tb_mm = min(1024, seq)
    tb_seq = min(256, seq // 2) if seq >= 512 else seq
    x2d = x.reshape(t, D_MODEL)
    p2d = p.reshape(t, p.shape[-1])
    tgt = loss_target.reshape(t, D_MODEL)
    chip = 2 * lax.axis_index("x") + lax.axis_index("y")

    rest = [(w_proj_lru[0], 0), (w_proj_pool[0], 1), (w_out[0], 0), (w_ple_gate[0], 0), (w_ple_proj[0], 1)]
    z, h_bf, w_in_f, conv_w_f = _in_proj_gather(x2d, norm_g, w_in[0].astype(BF16), [(conv_w[0], 1, False)], tb_mm)

    wa_bf = lru_w_a[0].astype(BF16)
    wx_bf = lru_w_x[0].astype(BF16)
    pw_bf = pool_w[0].astype(BF16)
    branch_w = (conv_w_f, conv_b, wa_bf, lru_b_a.reshape(1, D_MODEL), wx_bf, lru_b_x.reshape(1, D_MODEL),
                lru_lambda, pw_bf, pool_scale)

    ya, yb, hl, w_pl_f, w_pp_f, w_out_f, w_pg_f, w_pe_f = _branches_fwd(
        z, branch_w, seq, tb_seq, [(w.astype(BF16), axis, True) for w, axis in rest])
    (vec_bag, dx_res, dya, dyb, dzm, mg_bf, do_bf, hn_bf, dgp_bf, dpe_bf, da_bf, dbm_bf, p_bf) = _merge_head(
        x2d, ya, yb, z, p2d, tgt, w_pl_f, w_pp_f, w_out_f, w_pg_f, w_pe_f, ple_norm_g, final_g.reshape(1, D_MODEL),
        tb_seq)
    dz, vec_bag, mat_bag = _branches_bwd(z, hl, dya, dyb, dzm, branch_w, vec_bag, seq, tb_seq)

    tb_dw = min(1024, seq)
    def proj_grad(lhs, rhs, name, cols):
        g32, g16 = _weight_grad(lhs, rhs, 1, min(2 * tb_dw, t), name)
        if cols:
            return g32[0], True, g16[0]
        rows = g32.shape[1] // 8
        return g32.reshape(8, rows, g32.shape[2]), False, g16.reshape(8, rows, g32.shape[2])

    p_dim = p2d.shape[1]
    proj_parts = [proj_grad(ya, da_bf, "dw_proj_lru", False), proj_grad(yb, dbm_bf, "dw_proj_pool", True),
                  proj_grad(mg_bf, do_bf, "dw_out", False), proj_grad(hn_bf, dgp_bf, "dw_ple_gate", False),
                  proj_grad(p_bf, dpe_bf, "dw_ple_proj", True)]
    nb_dw = t // tb_dw
    g_in, g_in16, r_pl, r_pp, r_out, r_pg, r_pe, vec_mine, mat_mine = _weight_grad(
        h_bf, dz, N_CHIPS, tb_dw, "dw_in",
        reduce=(proj_parts + [(vec_bag.reshape(8, VEC_BAG_ROWS // 8, D_MODEL), False, None),
                              (mat_bag.reshape(8, MAT_BAG_ROWS // 8, HEAD_DIM), False, None)],
                [BF16] * 5 + [F32] * 2,
                (0, nb_dw // 2, 2 * nb_dw - 1, 3 * nb_dw + nb_dw // 2, N_CHIPS * nb_dw - 1)))
    pieces = (8, D_MODEL // 2, IN_COLS // N_CHIPS)
    nb_seq = t // tb_seq
    dx, g_g1, r_in, vec_sum, mat_sum = _in_proj_bwd(
        dz, w_in_f, x2d, dx_res, norm_g, tb_seq,
        reduce=([(g_in.reshape(pieces), False, g_in16.reshape(pieces))], BF16,
                (0, nb_seq // 8, nb_seq // 2, nb_seq - 1, nb_seq - 1)),
        shards=[(vec_mine.reshape(VEC_BAG_ROWS // N_CHIPS, D_MODEL), 0, True),
                (mat_mine.reshape(MAT_BAG_ROWS // N_CHIPS, HEAD_DIM), 0, True)])

    def big_update(w, g2d, m, v, rows, name):
        d, nm, nv = _adamw(w[0], g2d, m[0], v[0], rows, name)
        return g2d[None], d[None], nm[None], nv[None]

    u_in = big_update(w_in, r_in.reshape(D_MODEL, IN_COLS // N_CHIPS), m_w_in, v_w_in, 256, "adamw_w_in")
    u_pl = big_update(w_proj_lru, r_pl.reshape(D_MODEL // N_CHIPS, D_MODEL), m_w_proj_lru, v_w_proj_lru, 256, "adamw_w_proj_lru")
    u_pp = big_update(w_proj_pool, r_pp.reshape(POOL_WIDTH, D_MODEL // N_CHIPS), m_w_proj_pool, v_w_proj_pool, 512, "adamw_w_proj_pool")
    u_out = big_update(w_out, r_out.reshape(D_MODEL // N_CHIPS, D_MODEL), m_w_out, v_w_out, 256, "adamw_w_out")
    u_pg = big_update(w_ple_gate, r_pg.reshape(D_MODEL // N_CHIPS, D_MODEL), m_w_ple_gate, v_w_ple_gate, 256, "adamw_w_ple_gate")
    u_pe = big_update(w_ple_proj, r_pe.reshape(p_dim, D_MODEL // N_CHIPS), m_w_ple_proj, v_w_ple_proj, 256, "adamw_w_ple_proj")

    small = [("norm_g", norm_g, m_norm_g, v_norm_g), ("conv_b", conv_b, m_conv_b, v_conv_b),
             ("lru_w_a", lru_w_a, m_lru_w_a, v_lru_w_a), ("lru_b_a", lru_b_a, m_lru_b_a, v_lru_b_a),
             ("lru_w_x", lru_w_x, m_lru_w_x, v_lru_w_x), ("lru_b_x", lru_b_x, m_lru_b_x, v_lru_b_x),
             ("lru_lambda", lru_lambda, m_lru_lambda, v_lru_lambda), ("pool_w", pool_w, m_pool_w, v_pool_w),
             ("pool_scale", pool_scale, m_pool_scale, v_pool_scale),
             ("ple_norm_g", ple_norm_g, m_ple_norm_g, v_ple_norm_g), ("final_g", final_g, m_final_g, v_final_g)]

    def view(a):
        return a.reshape(-1, a.shape[-1]) if a.ndim != 3 else a[0]

    cw_at = F32_SUBLANES * VEC_BAG_SLOTS.index("conv_w")
    cw_cols = D_MODEL // N_CHIPS
    g_cw = lax.dynamic_slice(vec_sum, (cw_at, chip * cw_cols), (CONV_WIDTH, cw_cols))
    flat = _adamw_replicated(vec_sum, mat_sum, g_g1, [(name,) + tuple(view(a) for a in arrs) for name, *arrs in small],
                             (conv_w[0], m_conv_w[0], v_conv_w[0], g_cw))
    u_small = {name: tuple(flat[4 * k + pick].reshape(arrs[0].shape) for pick in range(4))
               for k, (name, *arrs) in enumerate(small)}
    u_cw = tuple(a[None] for a in (g_cw,) + tuple(flat[4 * len(small):]))

    loss = vec_sum[F32_SUBLANES * VEC_BAG_SLOTS.index("loss"), 0]
    grad_x = dx.reshape(bsz, seq, D_MODEL)

    def ordered(pick):
        s = {name: u[pick] for name, u in u_small.items()}
        return [s["norm_g"], u_in[pick], u_cw[pick], s["conv_b"], s["lru_w_a"], s["lru_b_a"], s["lru_w_x"], s["lru_b_x"],
                s["lru_lambda"], s["pool_w"], s["pool_scale"], u_pl[pick], u_pp[pick], u_out[pick], s["ple_norm_g"],
                u_pg[pick], u_pe[pick], s["final_g"]]

    return (loss, grad_x, *ordered(0), *ordered(1), *ordered(2), *ordered(3))
```

```python
import jax
import jax.numpy as jnp
from jax import lax
from jax.experimental import pallas as pl
from jax.experimental.pallas import tpu as pltpu

F32 = jnp.float32
BF16 = jnp.bfloat16
MESH = pl.DeviceIdType.MESH

D_MODEL = 1024
LRU_HEADS = 8
HEAD_DIM = 128
CONV_WIDTH = 4
LRU_C = 8.0
POOL_WIDTH = 512
POOL_WINDOWS = (2, 4, 8, 16)
POOL_GROUP_DIM = 128
IN_COLS = 5120
N_CHIPS = 4
EPS = 1e-6

ADAM_LR = 0.001
ADAM_B1 = 0.9
ADAM_B2 = 0.999
ADAM_EPS = 1e-08
ADAM_WD = 0.01
ADAM_STEP = 10

F32_SUBLANES = 8
CONV_HIST = 8
POOL_HIST = 16
VMEM_LIMIT_BYTES = 58 * 1024 * 1024
VEC_BAG_SLOTS = ("norm_g", "conv_w", "conv_b", "lru_b_a", "lru_b_x", "lru_lambda", "pool_scale", "ple_norm_g",
                 "final_g", "loss")
VEC_BAG_ROWS = 128
MAT_BAG_AT = {"lru_w_a": 0, "lru_w_x": LRU_HEADS * HEAD_DIM, "pool_w": 2 * LRU_HEADS * HEAD_DIM}
MAT_BAG_ROWS = 2 * LRU_HEADS * HEAD_DIM + len(POOL_WINDOWS) * POOL_GROUP_DIM


def _bag_row(name, k=0):
    at = F32_SUBLANES * VEC_BAG_SLOTS.index(name) + k
    return slice(at, at + 1)


def _bag_rows(name):
    at = F32_SUBLANES * VEC_BAG_SLOTS.index(name)
    return slice(at, at + F32_SUBLANES)


def _dot(a, b):
    return jnp.dot(a, b, preferred_element_type=F32)


def _dot_nt(a, b):
    return lax.dot_general(a, b, (((1,), (1,)), ((), ())), preferred_element_type=F32)


def _dot_tn(a, b):
    return lax.dot_general(a, b, (((0,), (0,)), ((), ())), preferred_element_type=F32)


def _sigmoid(v):
    return jax.nn.sigmoid(v)


def _softplus(v):
    return jnp.maximum(v, 0.0) + jnp.log1p(jnp.exp(-jnp.abs(v)))


def _place():
    return lax.axis_index("x"), lax.axis_index("y"), lax.axis_index("c")


GATHER_SEMS = 6


def _gather_shapes(shards):
    out_shape = []
    for arr, axis, _ in shards:
        r, cols = arr.shape
        out_shape.append(jax.ShapeDtypeStruct((N_CHIPS * r, cols) if axis == 0 else (r, N_CHIPS * cols), arr.dtype))
    n = len(shards)
    sems = [pltpu.SemaphoreType.DMA((n * GATHER_SEMS,)), pltpu.SemaphoreType.DMA((n * GATHER_SEMS,)),
            pltpu.SemaphoreType.DMA((n,))]
    return out_shape, sems


def _gather_steps(shards, ins, outs, send_sems, recv_sems, local_sems):
    n = len(shards)
    x, y, c = _place()
    me, sibling = (x, y, c), (x, y, 1 - c)
    chips = [(x, 1 - y), (1 - x, y), (1 - x, 1 - y)]

    def region(k, cx, cy, hc):
        (r, cols), axis = shards[k][0].shape, shards[k][1]
        j = 2 * cx + cy
        if axis == 0:
            if hc is None:
                return outs[k].at[pl.ds(j * r, r), :]
            return outs[k].at[pl.ds(j * r + hc * (r // 2), r // 2), :]
        if hc is None:
            return outs[k].at[:, pl.ds(j * cols, cols)]
        return outs[k].at[pl.ds(hc * (r // 2), r // 2), pl.ds(j * cols, cols)]

    def remote(k, sem, block, to, src=None):
        dst = region(k, *block)
        return pltpu.make_async_remote_copy(
            src_ref=dst if src is None else src, dst_ref=dst,
            send_sem=send_sems.at[k * GATHER_SEMS + sem], recv_sem=recv_sems.at[k * GATHER_SEMS + sem],
            device_id=to, device_id_type=MESH)

    def first(k, idx):
        r, split = shards[k][0].shape[0], shards[k][2]
        src = ins[k].at[pl.ds(c * (r // 2), r // 2), :] if split else ins[k]
        return remote(k, idx, (x, y, c if split else None), (*chips[idx], c), src=src)

    def relay(k):
        src_chip = (jnp.bitwise_xor(x, 1 - c), jnp.bitwise_xor(y, c))
        dst_chip = (jnp.bitwise_xor(x, c), jnp.bitwise_xor(y, 1 - c))
        return remote(k, 2, (*src_chip, c), (*dst_chip, c))

    def passed(k, idx):
        return remote(k, 3 + idx, (*chips[idx], c), sibling)

    def mine(k):
        return pltpu.make_async_copy(ins[k], region(k, x, y, None), local_sems.at[k])

    def start():
        for k in range(n):
            mine(k).start()
            for idx in range(2 if shards[k][2] else 3):
                first(k, idx).start()

    def relay_on():
        for k in range(n):
            split = shards[k][2]
            for idx in range(2):
                remote(k, idx, (*chips[idx], c if split else None), me).wait_recv()
            if split:
                relay(k).start()
                passed(k, 0).start()
                passed(k, 1).start()

    def finish():
        for k in range(n):
            split = shards[k][2]
            remote(k, 2, (*chips[2], c if split else None), me).wait_recv()
            if split:
                passed(k, 2).start()
        for k in range(n):
            if shards[k][2]:
                for idx in range(3):
                    remote(k, 3 + idx, (*chips[idx], 1 - c), me).wait_recv()
        for k in range(n):
            if shards[k][2]:
                for cp in (first(k, 0), first(k, 1), relay(k), passed(k, 0), passed(k, 1), passed(k, 2)):
                    cp.wait_send()
            else:
                for idx in range(3):
                    first(k, idx).wait_send()
            mine(k).wait()

    return start, relay_on, finish


RS_ADD_ROWS = (64, 32, 16, 8)


N_DEV = 2 * N_CHIPS


def _all_reduce_scratch(shape):
    return [pltpu.VMEM((N_DEV,) + tuple(shape), F32), pltpu.SemaphoreType.DMA((N_DEV - 1,)),
            pltpu.SemaphoreType.DMA((N_DEV - 1,))]


def _all_reduce_tile(v_ref, o_ref, slots, send_sems, recv_sems):
    flips = [(dx, dy, dc) for dx in (0, 1) for dy in (0, 1) for dc in (0, 1)][1:]
    x, y, c = _place()
    mine = 4 * x + 2 * y + c

    def copy(k, to_flip, slot):
        dx, dy, dc = to_flip
        peer = (jnp.bitwise_xor(x, dx), jnp.bitwise_xor(y, dy), jnp.bitwise_xor(c, dc))
        return pltpu.make_async_remote_copy(
            src_ref=v_ref, dst_ref=slots.at[slot], send_sem=send_sems.at[k], recv_sem=recv_sems.at[k],
            device_id=peer, device_id_type=MESH)

    sends = [copy(k, flip, mine) for k, flip in enumerate(flips)]
    for cp in sends:
        cp.start()
    slots[mine] = v_ref[...]
    for k, (dx, dy, dc) in enumerate(flips):
        copy(k, (dx, dy, dc), jnp.bitwise_xor(mine, 4 * dx + 2 * dy + dc)).wait_recv()
    total = slots[0]
    for d in range(1, N_DEV):
        total = total + slots[d]
    o_ref[...] = total
    for cp in sends:
        cp.wait_send()


RS_SEMS = 8
RS_LOCAL_SEMS = 5


def _rs_piece_shape(part):
    arr, cols = part[0], part[1]
    return (arr.shape[0] // 2, arr.shape[1] // N_CHIPS) if cols else tuple(arr.shape[1:])


def _rs_operands(parts):
    return [p[0] for p in parts] + [p[0] if p[2] is None else p[2] for p in parts]


def _rs_wires(parts, wire):
    return list(wire) if isinstance(wire, (list, tuple)) else [wire] * len(parts)


def _rs_shapes(parts, wire):
    n = len(parts)
    shapes = [_rs_piece_shape(p) for p in parts]
    out_shape = [jax.ShapeDtypeStruct((2,) + s, F32) for s in shapes]
    scratch = []
    for lead, kind in ((N_CHIPS, "f32"), (N_CHIPS, "narrow"), (N_CHIPS, "wire"), (None, "f32"), (N_CHIPS, "wire")):
        for s, p, w in zip(shapes, parts, _rs_wires(parts, wire)):
            dtype = {"f32": F32, "narrow": F32 if p[2] is None else p[2].dtype, "wire": w}[kind]
            scratch.append(pltpu.VMEM(s if lead is None else (lead,) + s, dtype))
    scratch += [pltpu.SemaphoreType.DMA((n * RS_SEMS,)), pltpu.SemaphoreType.DMA((n * RS_SEMS,)),
                pltpu.SemaphoreType.DMA((n * RS_LOCAL_SEMS,))]
    return out_shape, scratch


def _rs_steps(parts, ins, outs, scratch):
    n = len(parts)
    own, sib, got, fin, snd = (scratch[k * n:(k + 1) * n] for k in range(5))
    send_sems, recv_sems, local_sems = scratch[5 * n:]
    shapes = [_rs_piece_shape(p) for p in parts]
    x, y, c = _place()
    j_me = 2 * x + y
    me, sibling = (x, y, c), (x, y, 1 - c)

    def piece(a, jj, core, narrow=False):
        ref = ins[n + a] if narrow else ins[a]
        if parts[a][1]:
            r, cl = shapes[a]
            return ref.at[pl.ds(core * r, r), pl.ds(jj * cl, cl)]
        return ref.at[2 * jj + core]

    def remote(a, sem, src, dst, to):
        return pltpu.make_async_remote_copy(
            src_ref=src, dst_ref=dst, send_sem=send_sems.at[a * RS_SEMS + sem],
            recv_sem=recv_sems.at[a * RS_SEMS + sem], device_id=to, device_id_type=MESH)

    def rows_loop(a, fn):
        r = shapes[a][0]
        step = max(s for s in RS_ADD_ROWS if r % s == 0)

        def it(i, carry):
            fn(pl.ds(pl.multiple_of(i * step, step), step))
            return carry

        lax.fori_loop(0, r // step, it, 0)

    def load(a, jj):
        return pltpu.make_async_copy(piece(a, jj, c), own[a].at[jj], local_sems.at[a * RS_LOCAL_SEMS + jj])

    def to_sibling(a, jj):
        return remote(a, jj, piece(a, jj, 1 - c, narrow=True), sib[a].at[jj], sibling)

    near = (jnp.bitwise_xor(x, 1 - c), jnp.bitwise_xor(y, c))
    far = (jnp.bitwise_xor(x, c), jnp.bitwise_xor(y, 1 - c))
    diag = (1 - x, 1 - y)
    FROM_NEAR, FROM_FAR, FEED = 0, 1, 2

    def chip_of(chip):
        return 2 * chip[0] + chip[1]

    def feed(a):
        return remote(a, 4, snd[a].at[chip_of(diag)], got[a].at[FEED], (*near, c))

    def to_near(a):
        return remote(a, 5, snd[a].at[chip_of(near)], got[a].at[FROM_NEAR], (*near, c))

    def to_far(a):
        return remote(a, 6, snd[a].at[chip_of(far)], got[a].at[FROM_FAR], (*far, c))

    def store(a):
        return pltpu.make_async_copy(fin[a], outs[a].at[c], local_sems.at[a * RS_LOCAL_SEMS + 4])

    def result_to_sibling(a):
        return remote(a, 7, fin[a], outs[a].at[c], sibling)

    def exchange():
        for a in range(n):
            for jj in range(N_CHIPS):
                load(a, jj).start()
                to_sibling(a, jj).start()

    def chip_sums():
        for a in range(n):
            for jj in range(N_CHIPS):
                load(a, jj).wait()
                remote(a, jj, sib[a].at[jj], sib[a].at[jj], me).wait_recv()

                def add(sl, a=a, jj=jj):
                    q = own[a][jj, sl, :] + sib[a][jj, sl, :].astype(F32)
                    own[a][jj, sl, :] = q
                    snd[a][jj, sl, :] = q.astype(snd[a].dtype)

                rows_loop(a, add)
        for a in range(n):
            feed(a).start()
        for a in range(n):
            to_near(a).start()

    def relay():
        for a in range(n):
            remote(a, 4, got[a].at[FEED], got[a].at[FEED], me).wait_recv()

            def add(sl, a=a):
                pair = own[a][chip_of(far), sl, :] + got[a][FEED, sl, :].astype(F32)
                snd[a][chip_of(far), sl, :] = pair.astype(snd[a].dtype)

            rows_loop(a, add)
            to_far(a).start()

    def totals():
        for a in range(n):
            remote(a, 5, got[a].at[FROM_NEAR], got[a].at[FROM_NEAR], me).wait_recv()
            remote(a, 6, got[a].at[FROM_FAR], got[a].at[FROM_FAR], me).wait_recv()

            def total(sl, a=a):
                fin[a][sl, :] = (own[a][j_me, sl, :] + got[a][FROM_NEAR, sl, :].astype(F32)) + (
                    got[a][FROM_FAR, sl, :].astype(F32))

            rows_loop(a, total)
            store(a).start()
            result_to_sibling(a).start()

    def finish():
        for a in range(n):
            remote(a, 7, outs[a].at[1 - c], outs[a].at[1 - c], me).wait_recv()
        for a in range(n):
            for jj in range(N_CHIPS):
                to_sibling(a, jj).wait_send()
            for cp in (feed(a), to_near(a), to_far(a), result_to_sibling(a)):
                cp.wait_send()
            store(a).wait()

    return exchange, chip_sums, relay, totals, finish


def _rms(x):
    r = lax.rsqrt(jnp.mean(x * x, axis=-1, keepdims=True) + EPS)
    return x * r, r


def _rms_bwd(dxn, xn, r):
    return r * (dxn - xn * jnp.mean(dxn * xn, axis=-1, keepdims=True))


def _in_proj_gather(x2d, norm_g, w_in_sh, shards, tb):
    t = x2d.shape[0]
    nb = t // tb
    cols = IN_COLS // N_CHIPS
    half = D_MODEL // 2
    n = len(shards)

    def body(x_ref, g_ref, win_ref, *refs):
        ins = refs[:n]
        z_ref, h_ref, wfull_ref = refs[n:n + 3]
        outs = refs[n + 3:2 * n + 3]
        wv, h_buf, send_sems, recv_sems, local_sems, w_send, w_recv, w_local = refs[2 * n + 3:]
        s, i = pl.program_id(0), pl.program_id(1)
        x, y, c = _place()
        me, sibling = (x, y, c), (x, y, 1 - c)
        chips = [(x, 1 - y), (1 - x, y), (1 - x, 1 - y)]

        def w_half(cx, cy, hc):
            return wv.at[2 * cx + cy, pl.ds(hc * half, half), :]

        def w_remote(sem, block, to, src=None):
            dst = w_half(*block)
            return pltpu.make_async_remote_copy(
                src_ref=dst if src is None else src, dst_ref=dst, send_sem=w_send.at[sem],
                recv_sem=w_recv.at[sem], device_id=to, device_id_type=MESH)

        def w_first(idx):
            return w_remote(idx, (x, y, c), (*chips[idx], c), src=win_ref.at[pl.ds(c * half, half), :])

        def w_relay():
            src_chip = (jnp.bitwise_xor(x, 1 - c), jnp.bitwise_xor(y, c))
            dst_chip = (jnp.bitwise_xor(x, c), jnp.bitwise_xor(y, 1 - c))
            return w_remote(2, (*src_chip, c), (*dst_chip, c))

        def w_pass(idx):
            return w_remote(3 + idx, (*chips[idx], c), sibling)

        def w_store(k, cx, cy):
            jj = 2 * cx + cy
            return pltpu.make_async_copy(wv.at[jj], wfull_ref.at[:, pl.ds(jj * cols, cols)], w_local.at[k])

        start_rest, relay_rest, finish_rest = _gather_steps(shards, ins, outs, send_sems, recv_sems, local_sems)
        own = pltpu.make_async_copy(win_ref, wv.at[2 * x + y], w_local.at[4])

        @pl.when((s == 0) & (i == 0))
        def _():
            own.start()
            w_first(0).start()
            w_first(1).start()
            start_rest()
            own.wait()
            w_store(0, x, y).start()

        @pl.when((s == 1) & (i == 0))
        def _():
            w_remote(0, (*chips[0], c), me).wait_recv()
            w_remote(1, (*chips[1], c), me).wait_recv()
            w_relay().start()
            w_pass(0).start()
            w_pass(1).start()
            w_remote(3, (*chips[0], 1 - c), me).wait_recv()
            w_store(1, *chips[0]).start()

        @pl.when((s == 2) & (i == 0))
        def _():
            w_remote(4, (*chips[1], 1 - c), me).wait_recv()
            w_store(2, *chips[1]).start()

        @pl.when((s == 3) & (i == 0))
        def _():
            w_remote(2, (*chips[2], c), me).wait_recv()
            w_pass(2).start()
            w_remote(5, (*chips[2], 1 - c), me).wait_recv()
            w_store(3, *chips[2]).start()

        xn, _ = _rms(x_ref[...])
        h = (xn * g_ref[...]).astype(BF16)
        keep_h = pltpu.make_async_copy(h_buf, h_ref.at[pl.ds(pl.multiple_of(i * tb, tb), tb), :], w_local.at[5])

        @pl.when(s == 0)
        def _():
            h_buf[...] = h
            keep_h.start()

        z_ref[...] = _dot(h, wv[jnp.bitwise_xor(2 * x + y, s)])
        pl.when(s == 0)(keep_h.wait)

        @pl.when((s == N_CHIPS - 1) & (i == nb - 1))
        def _():
            relay_rest()
            finish_rest()
            for cp in (w_first(0), w_first(1), w_relay(), w_pass(0), w_pass(1), w_pass(2)):
                cp.wait_send()
            w_store(0, x, y).wait()
            for idx in range(3):
                w_store(idx + 1, *chips[idx]).wait()

    rest_shape, rest_sems = _gather_shapes(shards)
    out_shape = [jax.ShapeDtypeStruct((t, IN_COLS), F32), jax.ShapeDtypeStruct((t, D_MODEL), BF16),
                 jax.ShapeDtypeStruct((D_MODEL, IN_COLS), BF16)] + rest_shape
    any_spec = pl.BlockSpec(memory_space=pl.ANY)

    def z_map(s, i):
        return (i, jnp.bitwise_xor(2 * lax.axis_index("x") + lax.axis_index("y"), s))

    return pl.pallas_call(
        body, name="in_proj", out_shape=tuple(out_shape),
        grid=(N_CHIPS, nb),
        in_specs=[pl.BlockSpec((tb, D_MODEL), lambda s, i: (i, 0)),
                  pl.BlockSpec((1, D_MODEL), lambda s, i: (0, 0)), any_spec] + [any_spec] * n,
        out_specs=tuple([pl.BlockSpec((tb, cols), z_map), any_spec, any_spec] + [any_spec] * n),
        scratch_shapes=[pltpu.VMEM((N_CHIPS, D_MODEL, cols), BF16), pltpu.VMEM((tb, D_MODEL), BF16)] + rest_sems + [
            pltpu.SemaphoreType.DMA((GATHER_SEMS,)), pltpu.SemaphoreType.DMA((GATHER_SEMS,)),
            pltpu.SemaphoreType.DMA((N_CHIPS + 2,))],
        compiler_params=pltpu.CompilerParams(dimension_semantics=("arbitrary", "arbitrary"),
                                             vmem_limit_bytes=VMEM_LIMIT_BYTES),
    )(x2d, norm_g, w_in_sh, *[sh[0] for sh in shards])


def _in_proj_bwd(dz, w_in, x2d, dx_res, norm_g, tb, reduce, shards):
    t = x2d.shape[0]
    nb = t // tb
    parts, wire, steps = reduce
    n = len(parts)
    k = len(shards)

    def body(dz_ref, w_ref, x_ref, dres_ref, g_ref, *refs):
        at = 2 * n + k
        dx_ref, dg_ref = refs[at:at + 2]
        rs_outs, g_outs = refs[at + 2:at + 2 + n], refs[at + 2 + n:at + 2 + n + k]
        scratch = refs[at + 2 + n + k:]
        rs_scr, g_sems, dg_acc, ar_scr = scratch[:-7], scratch[-7:-4], scratch[-4], scratch[-3:]
        rs = _rs_steps(parts, refs[:2 * n], rs_outs, rs_scr)
        for step, when in zip(rs, steps):
            pl.when(pl.program_id(0) == when)(step)
        gather = _gather_steps(shards, refs[2 * n:at], g_outs, *g_sems)
        for step, when in zip(gather, (0, nb // 2, nb - 1)):
            pl.when(pl.program_id(0) == when)(step)

        @pl.when(pl.program_id(0) == 0)
        def _():
            dg_acc[...] = jnp.zeros_like(dg_acc)

        xn, r = _rms(x_ref[...])
        g = g_ref[...]
        dh = _dot_nt(dz_ref[...], w_ref[...])
        dg_acc[0:1, :] += jnp.sum(dh * xn, axis=0, keepdims=True)
        dx_ref[...] = dres_ref[...] + _rms_bwd(dh * g, xn, r)

        @pl.when(pl.program_id(0) == nb - 1)
        def _():
            _all_reduce_tile(dg_acc, dg_ref, *ar_scr)

    row = lambda i: (i, 0)
    fixed = lambda i: (0, 0)
    rs_shape, rs_scratch = _rs_shapes(parts, wire)
    g_shape, g_sems = _gather_shapes(shards)
    any_spec = pl.BlockSpec(memory_space=pl.ANY)
    return pl.pallas_call(
        body, name="in_proj_bwd",
        out_shape=tuple([jax.ShapeDtypeStruct((t, D_MODEL), F32), jax.ShapeDtypeStruct((F32_SUBLANES, D_MODEL), F32)]
                        + rs_shape + g_shape),
        grid=(nb,),
        in_specs=[pl.BlockSpec((tb, IN_COLS), row),
                  pl.BlockSpec((D_MODEL, IN_COLS), fixed, pipeline_mode=pl.Buffered(1)),
                  pl.BlockSpec((tb, D_MODEL), row), pl.BlockSpec((tb, D_MODEL), row),
                  pl.BlockSpec((1, D_MODEL), fixed)] + [any_spec] * (2 * n + k),
        out_specs=tuple([pl.BlockSpec((tb, D_MODEL), row), pl.BlockSpec((F32_SUBLANES, D_MODEL), fixed)]
                        + [any_spec] * (n + k)),
        scratch_shapes=rs_scratch + g_sems + [pltpu.VMEM((F32_SUBLANES, D_MODEL), F32)] + _all_reduce_scratch(
            (F32_SUBLANES, D_MODEL)),
        compiler_params=pltpu.CompilerParams(dimension_semantics=("arbitrary",),
                                             vmem_limit_bytes=VMEM_LIMIT_BYTES),
    )(dz, w_in, x2d, dx_res, norm_g, *_rs_operands(parts), *[sh[0] for sh in shards])


def _weight_grad(lhs, rhs, n_chunks, tb, name, reduce=None):
    t, k = lhs.shape
    nc = rhs.shape[1] // n_chunks
    nb = t // tb
    parts, wire, steps = reduce if reduce is not None else ([], F32, ())
    n = len(parts)

    def body(l_ref, r_ref, *refs):
        o_ref, o16_ref = refs[2 * n:2 * n + 2]
        if n:
            at = pl.program_id(0) * nb + pl.program_id(1)
            rs = _rs_steps(parts, refs[:2 * n], refs[2 * n + 2:3 * n + 2], refs[3 * n + 2:])
            for step, when in zip(rs, steps):
                pl.when(at == when)(step)

        @pl.when(pl.program_id(1) == 0)
        def _():
            o_ref[...] = jnp.zeros_like(o_ref)

        o_ref[...] += _dot_tn(l_ref[...], r_ref[...])

        @pl.when(pl.program_id(1) == nb - 1)
        def _():
            o16_ref[...] = o_ref[...].astype(BF16)

    rs_shape, rs_scratch = _rs_shapes(parts, wire) if n else ([], [])
    any_spec = pl.BlockSpec(memory_space=pl.ANY)
    chunk = pl.BlockSpec((None, k, nc), lambda j, i: (j, 0, 0))
    return pl.pallas_call(
        body, name=name,
        out_shape=tuple([jax.ShapeDtypeStruct((n_chunks, k, nc), F32), jax.ShapeDtypeStruct((n_chunks, k, nc), BF16)]
                        + rs_shape),
        grid=(n_chunks, nb),
        in_specs=[pl.BlockSpec((tb, k), lambda j, i: (i, 0)), pl.BlockSpec((tb, nc), lambda j, i: (i, j))]
        + [any_spec] * (2 * n),
        out_specs=tuple([chunk, chunk] + [any_spec] * n),
        scratch_shapes=rs_scratch,
        compiler_params=pltpu.CompilerParams(dimension_semantics=("arbitrary", "arbitrary"),
                                             vmem_limit_bytes=VMEM_LIMIT_BYTES),
    )(lhs, rhs, *_rs_operands(parts))


def _proj_grads(pairs, tb):
    n = len(pairs)
    t = pairs[0][0].shape[0]
    nb = t // tb
    shapes = [(lhs.shape[1], rhs.shape[1]) for lhs, rhs in pairs]

    def body(*refs):
        ins, outs32, outs16 = refs[:2 * n], refs[2 * n:3 * n], refs[3 * n:4 * n]
        accs, halves, sems = refs[4 * n:5 * n], refs[5 * n:6 * n], refs[6 * n]
        i = pl.program_id(0)
        for k in range(n):
            part = _dot_tn(ins[2 * k][...], ins[2 * k + 1][...])

            @pl.when(i == 0)
            def _(k=k, part=part):
                accs[k][...] = part

            @pl.when(i > 0)
            def _(k=k, part=part):
                accs[k][...] += part

        @pl.when(i == nb - 1)
        def _():
            copies = []
            for k in range(n):
                halves[k][...] = accs[k][...].astype(BF16)
                copies.append(pltpu.make_async_copy(accs[k], outs32[k], sems.at[2 * k]))
                copies.append(pltpu.make_async_copy(halves[k], outs16[k], sems.at[2 * k + 1]))
            for cp in copies:
                cp.start()
            for cp in copies:
                cp.wait()

    any_spec = pl.BlockSpec(memory_space=pl.ANY)
    in_specs = []
    for lhs, rhs in pairs:
        in_specs += [pl.BlockSpec((tb, lhs.shape[1]), lambda i: (i, 0)), pl.BlockSpec((tb, rhs.shape[1]), lambda i: (i, 0))]
    return pl.pallas_call(
        body, name="dw_proj",
        out_shape=tuple([jax.ShapeDtypeStruct(s, F32) for s in shapes] + [jax.ShapeDtypeStruct(s, BF16) for s in shapes]),
        grid=(nb,), in_specs=in_specs, out_specs=tuple([any_spec] * (2 * n)),
        scratch_shapes=[pltpu.VMEM(s, F32) for s in shapes] + [pltpu.VMEM(s, BF16) for s in shapes]
        + [pltpu.SemaphoreType.DMA((2 * n,))],
        compiler_params=pltpu.CompilerParams(dimension_semantics=("arbitrary",),
                                             vmem_limit_bytes=VMEM_LIMIT_BYTES),
    )(*[a for pair in pairs for a in pair])


def _adam_update(w, g, m, v):
    m_ = ADAM_B1 * m + (1.0 - ADAM_B1) * g
    v_ = ADAM_B2 * v + (1.0 - ADAM_B2) * jnp.square(g)
    m_hat = m_ / (1.0 - ADAM_B1 ** ADAM_STEP)
    v_hat = v_ / (1.0 - ADAM_B2 ** ADAM_STEP)
    return -ADAM_LR * (m_hat / (jnp.sqrt(v_hat) + ADAM_EPS) + ADAM_WD * w), m_, v_


def _adamw_replicated(vec_sum, mat_sum, norm_grad, entries, conv):
    n = len(entries)

    def grad_of(name, shape, vec_ref, mat_ref, norm_ref):
        if name == "norm_g":
            return norm_ref[0:1, :]
        if name in MAT_BAG_AT:
            return mat_ref[MAT_BAG_AT[name]:MAT_BAG_AT[name] + shape[0], :]
        if shape[0] == 1:
            return vec_ref[_bag_row(name), 0:shape[1]]
        return jnp.concatenate([vec_ref[_bag_row(name), h * shape[1]:(h + 1) * shape[1]] for h in range(shape[0])],
                               axis=0)

    def body(vec_ref, mat_ref, norm_ref, *refs):
        ins, outs = refs[:3 * n + 4], refs[3 * n + 4:]
        for k in range(n):
            w_ref, m_ref, v_ref = ins[3 * k:3 * k + 3]
            g = grad_of(entries[k][0], w_ref.shape, vec_ref, mat_ref, norm_ref)
            d, m_, v_ = _adam_update(w_ref[...], g, m_ref[...], v_ref[...])
            for ref, val in zip(outs[4 * k:4 * k + 4], (g, d, m_, v_)):
                ref[...] = val
        w_ref, m_ref, v_ref, g_ref = ins[3 * n:]
        for ref, val in zip(outs[4 * n:], _adam_update(w_ref[...], g_ref[...], m_ref[...], v_ref[...])):
            ref[...] = val

    arrays = [a for e in entries for a in e[1:]] + list(conv)
    out_shape = [jax.ShapeDtypeStruct(e[1].shape, F32) for e in entries for _ in range(4)]
    out_shape += [jax.ShapeDtypeStruct(conv[0].shape, F32)] * 3
    return pl.pallas_call(
        body, name="adamw_replicated", out_shape=tuple(out_shape),
        compiler_params=pltpu.CompilerParams(vmem_limit_bytes=VMEM_LIMIT_BYTES),
    )(vec_sum, mat_sum, norm_grad, *arrays)


def _adamw(w, g, m, v, rows, name):
    r, c = w.shape

    def body(w_ref, g_ref, m_ref, v_ref, d_ref, nm_ref, nv_ref):
        d_ref[...], nm_ref[...], nv_ref[...] = _adam_update(w_ref[...], g_ref[...], m_ref[...], v_ref[...])

    spec = pl.BlockSpec((rows, c), lambda i: (i, 0))
    return pl.pallas_call(
        body, name=name, out_shape=tuple(jax.ShapeDtypeStruct((r, c), F32) for _ in range(3)),
        grid=(r // rows,), in_specs=[spec] * 4, out_specs=(spec,) * 3,
        compiler_params=pltpu.CompilerParams(dimension_semantics=("arbitrary",),
                                             vmem_limit_bytes=VMEM_LIMIT_BYTES),
    )(w, g, m, v)


def _shift_down(ext, s):
    return pltpu.roll(ext, s, 0)


def _tile_shift(v, s):
    rows, cols = v.shape
    tiles = v.reshape(rows // F32_SUBLANES, F32_SUBLANES, cols)
    return pltpu.roll(tiles, s % F32_SUBLANES, 1).reshape(rows, cols)


def _shift_up(ext, s):
    return pltpu.roll(ext, ext.shape[0] - s, 0)


def _lru_gates(xc, wa_ref, ba, wx_ref, bx, lam):
    pa, px = [], []
    for h in range(LRU_HEADS):
        xh = xc[:, h * HEAD_DIM:(h + 1) * HEAD_DIM].astype(BF16)
        pa.append(_dot(xh, wa_ref[h]))
        px.append(_dot(xh, wx_ref[h]))
    r = _sigmoid(jnp.concatenate(pa, axis=1) + ba)
    ig = _sigmoid(jnp.concatenate(px, axis=1) + bx)
    sp = _softplus(-lam)
    log_a = (-LRU_C * r) * sp
    a = jnp.exp(log_a)
    mult = jnp.sqrt(jnp.tanh(-log_a) * (1.0 + a * a))
    return r, ig, a, mult, sp


def _conv(ext, w_ref, b):
    y = b + _shift_down(ext, 3) * w_ref[0:1, :]
    y = y + _shift_down(ext, 2) * w_ref[1:2, :]
    y = y + _shift_down(ext, 1) * w_ref[2:3, :]
    y = y + ext * w_ref[3:4, :]
    return y[CONV_HIST:, :]


def _pool_diff(ext, pos):
    out = []
    for g, k in enumerate(POOL_WINDOWS):
        col = ext[:, g * POOL_GROUP_DIM:(g + 1) * POOL_GROUP_DIM]
        s = col
        for step in range(g + 1):
            s = s + _shift_down(s, 2 ** step)
        count = jnp.minimum(pos + 1, k).astype(F32)
        out.append(s[POOL_HIST:, :] / count - col[POOL_HIST:, :])
    return out


def _pool_mix(diff, pw_ref):
    return jnp.concatenate([_dot(diff[g].astype(BF16), pw_ref[g]) for g in range(len(POOL_WINDOWS))], axis=1)


def _branch_specs(tb, row_map, fixed):
    fixed3 = lambda i: (0, 0, 0)
    return [pl.BlockSpec((CONV_WIDTH, D_MODEL), fixed), pl.BlockSpec((1, D_MODEL), fixed),
            pl.BlockSpec((LRU_HEADS, HEAD_DIM, HEAD_DIM), fixed3), pl.BlockSpec((1, D_MODEL), fixed),
            pl.BlockSpec((LRU_HEADS, HEAD_DIM, HEAD_DIM), fixed3), pl.BlockSpec((1, D_MODEL), fixed),
            pl.BlockSpec((1, D_MODEL), fixed),
            pl.BlockSpec((len(POOL_WINDOWS), POOL_GROUP_DIM, POOL_GROUP_DIM), fixed3),
            pl.BlockSpec((1, POOL_WIDTH), fixed)]


def _branches_fwd(z, weights, seq, tb, shards):
    t = z.shape[0]
    nb = t // tb
    nbe = seq // tb
    groups = tb // F32_SUBLANES
    n = len(shards)

    def body(xa_ref, ga_ref, xb_ref, gb_ref, cw_ref, cb_ref, wa_ref, ba_ref, wx_ref, bx_ref, lam_ref,
             pw_ref, ps_ref, *refs):
        g_ins = refs[:n]
        ya_ref, yb_ref, hl_ref = refs[n:n + 3]
        g_outs = refs[n + 3:2 * n + 3]
        xa_ext, xb_ext, carry, a_s, u_s, send_sems, recv_sems, local_sems = refs[2 * n + 3:]
        blk = pl.program_id(0) % nbe
        start_gather, relay_gather, finish_gather = _gather_steps(shards, g_ins, g_outs, send_sems, recv_sems,
                                                                  local_sems)
        pl.when(pl.program_id(0) == 0)(start_gather)
        pl.when(pl.program_id(0) == nb // 2)(relay_gather)

        @pl.when(blk == 0)
        def _():
            xa_ext[0:CONV_HIST, :] = jnp.zeros((CONV_HIST, D_MODEL), F32)
            xb_ext[0:POOL_HIST, :] = jnp.zeros((POOL_HIST, POOL_WIDTH), F32)
            carry[...] = jnp.zeros_like(carry)

        xa_ext[CONV_HIST:, :] = xa_ref[...]
        xb_ext[POOL_HIST:, :] = xb_ref[...]
        ea = xa_ext[...]
        eb = xb_ext[...]
        xa_ext[0:CONV_HIST, :] = ea[tb:, :]
        xb_ext[0:POOL_HIST, :] = eb[tb:, :]

        xc = _conv(ea, cw_ref, cb_ref[...])
        _, ig, a, mult, _ = _lru_gates(xc, wa_ref, ba_ref[...], wx_ref, bx_ref[...], lam_ref[...])
        u = mult * (ig * xc)
        row8 = lax.broadcasted_iota(jnp.int32, (tb, D_MODEL), 0) % F32_SUBLANES
        for s in (1, 2, 4):
            m = row8 >= s
            u = jnp.where(m, a * _tile_shift(u, s) + u, u)
            a = jnp.where(m, a * _tile_shift(a, s), a)
        a_s[...] = a
        u_s[...] = u

        def step(g, cr):
            sl = pl.ds(pl.multiple_of(g * F32_SUBLANES, F32_SUBLANES), F32_SUBLANES)
            hb = a_s[sl, :] * cr + u_s[sl, :]
            hl_ref[sl, :] = hb
            return jnp.broadcast_to(hb[F32_SUBLANES - 1:F32_SUBLANES, :], (F32_SUBLANES, D_MODEL))

        carry[...] = lax.fori_loop(0, groups, step, carry[...], unroll=4)
        ga = ga_ref[...]
        ya_ref[...] = (hl_ref[...] * (ga * _sigmoid(ga))).astype(BF16)

        pos = blk * tb + lax.broadcasted_iota(jnp.int32, (tb, POOL_GROUP_DIM), 0)
        ypre = _pool_mix(_pool_diff(eb, pos), pw_ref)
        gb = gb_ref[...]
        yb_ref[...] = ((ypre * ps_ref[...]) * (gb * _sigmoid(gb))).astype(BF16)
        pl.when(pl.program_id(0) == nb - 1)(finish_gather)

    row = lambda i: (i, 0)
    fixed = lambda i: (0, 0)
    any_spec = pl.BlockSpec(memory_space=pl.ANY)
    in_specs = [pl.BlockSpec((tb, D_MODEL), lambda i: (i, 0)), pl.BlockSpec((tb, D_MODEL), lambda i: (i, 1)),
                pl.BlockSpec((tb, POOL_WIDTH), lambda i: (i, 4)), pl.BlockSpec((tb, POOL_WIDTH), lambda i: (i, 5)),
                ] + _branch_specs(tb, row, fixed) + [any_spec] * n
    g_shape, g_sems = _gather_shapes(shards)
    return pl.pallas_call(
        body, name="branches_fwd",
        out_shape=tuple([jax.ShapeDtypeStruct((t, D_MODEL), BF16), jax.ShapeDtypeStruct((t, POOL_WIDTH), BF16),
                         jax.ShapeDtypeStruct((t, D_MODEL), F32)] + g_shape),
        grid=(nb,), in_specs=in_specs,
        out_specs=tuple([pl.BlockSpec((tb, D_MODEL), row), pl.BlockSpec((tb, POOL_WIDTH), row),
                         pl.BlockSpec((tb, D_MODEL), row)] + [any_spec] * n),
        scratch_shapes=[pltpu.VMEM((tb + CONV_HIST, D_MODEL), F32), pltpu.VMEM((tb + POOL_HIST, POOL_WIDTH), F32),
                        pltpu.VMEM((F32_SUBLANES, D_MODEL), F32),
                        pltpu.VMEM((tb, D_MODEL), F32), pltpu.VMEM((tb, D_MODEL), F32)] + g_sems,
        compiler_params=pltpu.CompilerParams(dimension_semantics=("arbitrary",),
                                             vmem_limit_bytes=VMEM_LIMIT_BYTES),
    )(z, z, z, z, *weights, *[sh[0] for sh in shards])


def _branches_bwd(z, hl, dya, dyb, dzm, weights, vec_bag, seq, tb):
    t = z.shape[0]
    nb = t // tb
    nbe = seq // tb
    groups = tb // F32_SUBLANES

    def body(xa_ref, xap_ref, ga_ref, xb_ref, xbp_ref, gb_ref, hl_ref, hlp_ref, dya_ref, dyb_ref, dzm_ref,
             cw_ref, cb_ref, wa_ref, ba_ref, wx_ref, bx_ref, lam_ref, pw_ref, ps_ref, vec_in_ref,
             dz_ref, vec_ref, mat_ref,
             xa_ext, xb_ext, hl_ext, a_ext, dxc_ext, dwin_ext, g_carry, b_s, d_s, g_s):
        i = pl.program_id(0)
        blk = (nb - 1 - i) % nbe

        def mat_rows(name, k):
            at = MAT_BAG_AT[name] + k * HEAD_DIM
            return slice(at, at + HEAD_DIM)

        @pl.when(i == 0)
        def _():
            vec_ref[...] = vec_in_ref[...]
            mat_ref[...] = jnp.zeros_like(mat_ref)

        @pl.when(blk == nbe - 1)
        def _():
            a_ext[tb:, :] = jnp.zeros((F32_SUBLANES, D_MODEL), F32)
            dxc_ext[tb:, :] = jnp.zeros((CONV_HIST, D_MODEL), F32)
            dwin_ext[tb:, :] = jnp.zeros((POOL_HIST, POOL_WIDTH), F32)
            g_carry[...] = jnp.zeros_like(g_carry)

        live = (blk > 0).astype(F32)
        xa_ext[0:CONV_HIST, :] = xap_ref[...] * live
        xa_ext[CONV_HIST:, :] = xa_ref[...]
        xb_ext[0:POOL_HIST, :] = xbp_ref[...] * live
        xb_ext[POOL_HIST:, :] = xb_ref[...]
        hl_ext[0:F32_SUBLANES, :] = hlp_ref[...] * live
        hl_ext[F32_SUBLANES:, :] = hl_ref[...]
        ea = xa_ext[...]
        eb = xb_ext[...]

        xc = _conv(ea, cw_ref, cb_ref[...])
        lam = lam_ref[...]
        r, ig, a, mult, sp = _lru_gates(xc, wa_ref, ba_ref[...], wx_ref, bx_ref[...], lam)
        hl = hl_ref[...]
        ga = ga_ref[...]
        sga = _sigmoid(ga)
        dya = dya_ref[...]
        dhl = dya * (ga * sga)
        dz_ref[:, D_MODEL:2 * D_MODEL] = (dya * hl * (sga * (1.0 + ga * (1.0 - sga)))).astype(BF16)

        a_ext[0:tb, :] = a
        b = _shift_up(a_ext[...], 1)[0:tb, :]
        a_ext[tb:, :] = jnp.broadcast_to(a[0:1, :], (F32_SUBLANES, D_MODEL))
        d = dhl
        row8 = lax.broadcasted_iota(jnp.int32, (tb, D_MODEL), 0) % F32_SUBLANES
        for s in (1, 2, 4):
            m = row8 < F32_SUBLANES - s
            d = jnp.where(m, d + b * _tile_shift(d, -s), d)
            b = jnp.where(m, b * _tile_shift(b, -s), b)
        b_s[...] = b
        d_s[...] = d

        def step(k, cr):
            sl = pl.ds(pl.multiple_of((groups - 1 - k) * F32_SUBLANES, F32_SUBLANES), F32_SUBLANES)
            gb_ = d_s[sl, :] + b_s[sl, :] * cr
            g_s[sl, :] = gb_
            return jnp.broadcast_to(gb_[0:1, :], (F32_SUBLANES, D_MODEL))

        g_carry[...] = lax.fori_loop(0, groups, step, g_carry[...], unroll=4)
        gsc = g_s[...]
        da = gsc * _shift_down(hl_ext[...], 1)[F32_SUBLANES:, :]
        dmult = gsc * (ig * xc)
        dig = gsc * (mult * xc)
        dxc = gsc * (mult * ig)
        dlog_a = da * a - (a * a) * dmult / mult
        dr = dlog_a * (-LRU_C * sp)
        vec_ref[_bag_row("lru_lambda"), :] += jnp.sum(dlog_a * (-LRU_C * r), axis=0, keepdims=True)
        dpa = dr * (r * (1.0 - r))
        dpx = dig * (ig * (1.0 - ig))
        vec_ref[_bag_row("lru_b_a"), :] += jnp.sum(dpa, axis=0, keepdims=True)
        vec_ref[_bag_row("lru_b_x"), :] += jnp.sum(dpx, axis=0, keepdims=True)
        back = []
        for h in range(LRU_HEADS):
            cols = slice(h * HEAD_DIM, (h + 1) * HEAD_DIM)
            xh = xc[:, cols].astype(BF16)
            dpa_h = dpa[:, cols].astype(BF16)
            dpx_h = dpx[:, cols].astype(BF16)
            mat_ref[mat_rows("lru_w_a", h), :] += _dot_tn(xh, dpa_h)
            mat_ref[mat_rows("lru_w_x", h), :] += _dot_tn(xh, dpx_h)
            back.append(_dot_nt(dpa_h, wa_ref[h]) + _dot_nt(dpx_h, wx_ref[h]))
        dxc = dxc + jnp.concatenate(back, axis=1)
        vec_ref[_bag_row("conv_b"), :] += jnp.sum(dxc, axis=0, keepdims=True)
        for k in range(CONV_WIDTH):
            tap = _shift_down(ea, CONV_WIDTH - 1 - k)[CONV_HIST:, :] if k < CONV_WIDTH - 1 else ea[CONV_HIST:, :]
            vec_ref[_bag_row("conv_w", k), :] += jnp.sum(dxc * tap, axis=0, keepdims=True)
        dxc_ext[0:tb, :] = dxc
        ed = dxc_ext[...]
        dxa = ed * cw_ref[3:4, :]
        dxa = dxa + _shift_up(ed, 1) * cw_ref[2:3, :]
        dxa = dxa + _shift_up(ed, 2) * cw_ref[1:2, :]
        dxa = dxa + _shift_up(ed, 3) * cw_ref[0:1, :]
        dz_ref[:, 0:D_MODEL] = dxa[0:tb, :].astype(BF16)
        dxc_ext[tb:, :] = dxc[0:CONV_HIST, :]

        pos = blk * tb + lax.broadcasted_iota(jnp.int32, (tb, POOL_GROUP_DIM), 0)
        diff = _pool_diff(eb, pos)
        ypre = _pool_mix(diff, pw_ref)
        ps = ps_ref[...]
        gb = gb_ref[...]
        sgb = _sigmoid(gb)
        dyb = dyb_ref[...]
        dyp = dyb * (gb * sgb)
        dz_ref[:, 2 * D_MODEL + POOL_WIDTH:3 * D_MODEL] = (
            dyb * (ypre * ps) * (sgb * (1.0 + gb * (1.0 - sgb)))).astype(BF16)
        vec_ref[_bag_row("pool_scale"), 0:POOL_WIDTH] += jnp.sum(dyp * ypre, axis=0, keepdims=True)
        dypre = dyp * ps
        for g, k in enumerate(POOL_WINDOWS):
            cols = slice(g * POOL_GROUP_DIM, (g + 1) * POOL_GROUP_DIM)
            dyg = dypre[:, cols].astype(BF16)
            mat_ref[mat_rows("pool_w", g), :] += _dot_tn(diff[g].astype(BF16), dyg)
            ddiff = _dot_nt(dyg, pw_ref[g])
            count = jnp.minimum(pos + 1, k).astype(F32)
            dwin = ddiff / count
            dwin_ext[0:tb, cols] = dwin
            s = dwin_ext[:, cols]
            for step_ in range(g + 1):
                s = s + _shift_up(s, 2 ** step_)
            dz_ref[:, 2 * D_MODEL + g * POOL_GROUP_DIM:2 * D_MODEL + (g + 1) * POOL_GROUP_DIM] = (
                s[0:tb, :] - ddiff).astype(BF16)
            dwin_ext[tb:, cols] = dwin[0:POOL_HIST, :]

        dz_ref[:, 3 * D_MODEL:] = dzm_ref[...]

        @pl.when(i == nb - 1)
        def _():
            row = _bag_row("lru_lambda")
            vec_ref[row, :] = vec_ref[row, :] * (-_sigmoid(-lam))

    rev = lambda i: (nb - 1 - i, 0)
    fixed = lambda i: (0, 0)

    def prev(rows, col):
        per = tb // rows
        return lambda i: (jnp.maximum((nb - 1 - i) * per - 1, 0), col)

    in_specs = [pl.BlockSpec((tb, D_MODEL), lambda i: (nb - 1 - i, 0)),
                pl.BlockSpec((CONV_HIST, D_MODEL), prev(CONV_HIST, 0)),
                pl.BlockSpec((tb, D_MODEL), lambda i: (nb - 1 - i, 1)),
                pl.BlockSpec((tb, POOL_WIDTH), lambda i: (nb - 1 - i, 4)),
                pl.BlockSpec((POOL_HIST, POOL_WIDTH), prev(POOL_HIST, 4)),
                pl.BlockSpec((tb, POOL_WIDTH), lambda i: (nb - 1 - i, 5)),
                pl.BlockSpec((tb, D_MODEL), rev),
                pl.BlockSpec((F32_SUBLANES, D_MODEL), prev(F32_SUBLANES, 0)),
                pl.BlockSpec((tb, D_MODEL), rev), pl.BlockSpec((tb, POOL_WIDTH), rev),
                pl.BlockSpec((tb, 2 * D_MODEL), rev)] + _branch_specs(tb, rev, fixed) + [
                    pl.BlockSpec((VEC_BAG_ROWS, D_MODEL), fixed)]
    out_shape = (jax.ShapeDtypeStruct((t, IN_COLS), BF16), jax.ShapeDtypeStruct((VEC_BAG_ROWS, D_MODEL), F32),
                 jax.ShapeDtypeStruct((MAT_BAG_ROWS, HEAD_DIM), F32))
    out_specs = (pl.BlockSpec((tb, IN_COLS), rev), pl.BlockSpec((VEC_BAG_ROWS, D_MODEL), fixed),
                 pl.BlockSpec((MAT_BAG_ROWS, HEAD_DIM), fixed))
    scratch = [pltpu.VMEM((tb + CONV_HIST, D_MODEL), F32), pltpu.VMEM((tb + POOL_HIST, POOL_WIDTH), F32),
               pltpu.VMEM((tb + F32_SUBLANES, D_MODEL), F32), pltpu.VMEM((tb + F32_SUBLANES, D_MODEL), F32),
               pltpu.VMEM((tb + CONV_HIST, D_MODEL), F32), pltpu.VMEM((tb + POOL_HIST, POOL_WIDTH), F32),
               pltpu.VMEM((F32_SUBLANES, D_MODEL), F32),
               pltpu.VMEM((tb, D_MODEL), F32), pltpu.VMEM((tb, D_MODEL), F32), pltpu.VMEM((tb, D_MODEL), F32)]
    return pl.pallas_call(
        body, name="branches_bwd", out_shape=out_shape, grid=(nb,), in_specs=in_specs, out_specs=out_specs,
        scratch_shapes=scratch, input_output_aliases={len(in_specs) - 1: 1},
        compiler_params=pltpu.CompilerParams(dimension_semantics=("arbitrary",),
                                             vmem_limit_bytes=VMEM_LIMIT_BYTES),
    )(z, z, z, z, z, z, hl, hl, dya, dyb, dzm, *weights, vec_bag)


def _merge_head(x2d, ya, yb, z, p2d, tgt, w_pl, w_pp, w_out, w_pg, w_pe, g2, gf, tb):
    t = x2d.shape[0]
    p_dim = p2d.shape[1]

    def body(x_ref, ya_ref, yb_ref, ma_ref, mb_ref, p_ref, t_ref, wpl_ref, wpp_ref, wout_ref, wpg_ref, wpe_ref,
             g2_ref, gf_ref,
             bag_ref, dxr_ref, dya_ref, dyb_ref, dzm_ref,
             mg_ref, do_ref, hn_ref, dgp_ref, dpe_ref, da_ref, dbm_ref, pbf_ref):
        @pl.when(pl.program_id(0) == 0)
        def _():
            bag_ref[...] = jnp.zeros_like(bag_ref)

        a_ = _dot(ya_ref[...], wpl_ref[...])
        bm = _dot(yb_ref[...], wpp_ref[...])
        sa = _sigmoid(ma_ref[...])
        sb = _sigmoid(mb_ref[...])
        mg = (sa * a_ + sb * bm).astype(BF16)
        mg_ref[...] = mg
        x1 = x_ref[...] + _dot(mg, wout_ref[...])
        xn2, r2 = _rms(x1)
        g2 = g2_ref[...]
        hn = (xn2 * g2).astype(BF16)
        hn_ref[...] = hn
        gate = _sigmoid(_dot(hn, wpg_ref[...]))
        pbf = p_ref[...].astype(BF16)
        pbf_ref[...] = pbf
        pe = _dot(pbf, wpe_ref[...])
        x2 = x1 + gate * pe
        xn3, r3 = _rms(x2)
        gf = gf_ref[...]
        err = xn3 * gf - t_ref[...]
        bag_ref[_bag_rows("loss"), 0:128] += 0.5 * jnp.sum(jnp.mean(err * err, axis=-1))

        dy = err * (1.0 / D_MODEL)
        bag_ref[_bag_row("final_g"), :] += jnp.sum(dy * xn3, axis=0, keepdims=True)
        dx2 = _rms_bwd(dy * gf, xn3, r3)
        dpe_ref[...] = (dx2 * gate).astype(BF16)
        dgp = ((dx2 * pe) * (gate * (1.0 - gate))).astype(BF16)
        dgp_ref[...] = dgp
        dhn = _dot_nt(dgp, wpg_ref[...])
        bag_ref[_bag_row("ple_norm_g"), :] += jnp.sum(dhn * xn2, axis=0, keepdims=True)
        dx1 = dx2 + _rms_bwd(dhn * g2, xn2, r2)
        dxr_ref[...] = dx1
        do = dx1.astype(BF16)
        do_ref[...] = do
        dmg = _dot_nt(do, wout_ref[...])
        da = (dmg * sa).astype(BF16)
        dbm = (dmg * sb).astype(BF16)
        da_ref[...] = da
        dbm_ref[...] = dbm
        dzm_ref[:, 0:D_MODEL] = (dmg * a_ * (sa * (1.0 - sa))).astype(BF16)
        dzm_ref[:, D_MODEL:] = (dmg * bm * (sb * (1.0 - sb))).astype(BF16)
        dya_ref[...] = _dot_nt(da, wpl_ref[...])
        dyb_ref[...] = _dot_nt(dbm, wpp_ref[...])

    row = lambda i: (i, 0)
    fixed = lambda i: (0, 0)

    def resident(shape):
        return pl.BlockSpec(shape, fixed, pipeline_mode=pl.Buffered(1))

    tok = lambda width: pl.BlockSpec((tb, width), row)
    in_specs = [tok(D_MODEL), tok(D_MODEL), tok(POOL_WIDTH),
                pl.BlockSpec((tb, D_MODEL), lambda i: (i, 3)), pl.BlockSpec((tb, D_MODEL), lambda i: (i, 4)),
                tok(p_dim), tok(D_MODEL),
                resident((D_MODEL, D_MODEL)), resident((POOL_WIDTH, D_MODEL)), resident((D_MODEL, D_MODEL)),
                resident((D_MODEL, D_MODEL)), resident((p_dim, D_MODEL)),
                pl.BlockSpec((1, D_MODEL), fixed), pl.BlockSpec((1, D_MODEL), fixed)]
    bf = lambda width: jax.ShapeDtypeStruct((t, width), BF16)
    f32 = lambda width: jax.ShapeDtypeStruct((t, width), F32)
    out_shape = (jax.ShapeDtypeStruct((VEC_BAG_ROWS, D_MODEL), F32),
                 f32(D_MODEL), f32(D_MODEL), f32(POOL_WIDTH), bf(2 * D_MODEL),
                 bf(D_MODEL), bf(D_MODEL), bf(D_MODEL), bf(D_MODEL), bf(D_MODEL), bf(D_MODEL), bf(D_MODEL), bf(p_dim))
    out_specs = (pl.BlockSpec((VEC_BAG_ROWS, D_MODEL), fixed),
                 tok(D_MODEL), tok(D_MODEL), tok(POOL_WIDTH), tok(2 * D_MODEL),
                 tok(D_MODEL), tok(D_MODEL), tok(D_MODEL), tok(D_MODEL), tok(D_MODEL), tok(D_MODEL), tok(D_MODEL),
                 tok(p_dim))
    return pl.pallas_call(
        body, name="merge_head", out_shape=out_shape, grid=(t // tb,), in_specs=in_specs, out_specs=out_specs,
        compiler_params=pltpu.CompilerParams(dimension_semantics=("arbitrary",),
                                             vmem_limit_bytes=VMEM_LIMIT_BYTES),
    )(x2d, ya, yb, z, z, p2d, tgt, w_pl, w_pp, w_out, w_pg, w_pe, g2, gf)


def kernel(x, p, norm_g, w_in, conv_w, conv_b, lru_w_a, lru_b_a, lru_w_x, lru_b_x, lru_lambda, pool_w, pool_scale, w_proj_lru, w_proj_pool, w_out, ple_norm_g, w_ple_gate, w_ple_proj, final_g, loss_target, m_norm_g, m_w_in, m_conv_w, m_conv_b, m_lru_w_a, m_lru_b_a, m_lru_w_x, m_lru_b_x, m_lru_lambda, m_pool_w, m_pool_scale, m_w_proj_lru, m_w_proj_pool, m_w_out, m_ple_norm_g, m_w_ple_gate, m_w_ple_proj, m_final_g, v_norm_g, v_w_in, v_conv_w, v_conv_b, v_lru_w_a, v_lru_b_a, v_lru_w_x, v_lru_b_x, v_lru_lambda, v_pool_w, v_pool_scale, v_w_proj_lru, v_w_proj_pool, v_w_out, v_ple_norm_g, v_w_ple_gate, v_w_ple_proj, v_final_g):
    bsz, seq, _ = x.shape
    t = bsz * seq
    tb_mm = min(1024, seq)
    tb_seq = min(256, seq // 2) if seq >= 512 else seq
    x2d = x.reshape(t, D_MODEL)
    p2d = p.reshape(t, p.shape[-1])
    tgt = loss_target.reshape(t, D_MODEL)
    chip = 2 * lax.axis_index("x") + lax.axis_index("y")

    rest = [(w_proj_lru[0], 0), (w_proj_pool[0], 1), (w_out[0], 0), (w_ple_gate[0], 0), (w_ple_proj[0], 1)]
    z, h_bf, w_in_f, conv_w_f = _in_proj_gather(x2d, norm_g, w_in[0].astype(BF16), [(conv_w[0], 1, False)], tb_mm)

    wa_bf = lru_w_a[0].astype(BF16)
    wx_bf = lru_w_x[0].astype(BF16)
    pw_bf = pool_w[0].astype(BF16)
    branch_w = (conv_w_f, conv_b, wa_bf, lru_b_a.reshape(1, D_MODEL), wx_bf, lru_b_x.reshape(1, D_MODEL),
                lru_lambda, pw_bf, pool_scale)

    ya, yb, hl, w_pl_f, w_pp_f, w_out_f, w_pg_f, w_pe_f = _branches_fwd(
        z, branch_w, seq, tb_seq, [(w.astype(BF16), axis, True) for w, axis in rest])
    (vec_bag, dx_res, dya, dyb, dzm, mg_bf, do_bf, hn_bf, dgp_bf, dpe_bf, da_bf, dbm_bf, p_bf) = _merge_head(
        x2d, ya, yb, z, p2d, tgt, w_pl_f, w_pp_f, w_out_f, w_pg_f, w_pe_f, ple_norm_g, final_g.reshape(1, D_MODEL),
        tb_seq)
    dz, vec_bag, mat_bag = _branches_bwd(z, hl, dya, dyb, dzm, branch_w, vec_bag, seq, tb_seq)

    tb_dw = min(1024, seq)
    proj = _proj_grads([(ya, da_bf), (yb, dbm_bf), (mg_bf, do_bf), (hn_bf, dgp_bf), (p_bf, dpe_bf)], min(512, seq))
    p_dim = p2d.shape[1]
    proj_parts = []
    for g32, g16, cols in zip(proj[:5], proj[5:], (False, True, False, False, True)):
        if not cols:
            g32, g16 = (g.reshape(8, g.shape[0] // 8, g.shape[1]) for g in (g32, g16))
        proj_parts.append((g32, cols, g16))
    nb_dw = t // tb_dw
    g_in, g_in16, r_pl, r_pp, r_out, r_pg, r_pe, vec_mine, mat_mine = _weight_grad(
        h_bf, dz, N_CHIPS, tb_dw, "dw_in",
        reduce=(proj_parts + [(vec_bag.reshape(8, VEC_BAG_ROWS // 8, D_MODEL), False, None),
                              (mat_bag.reshape(8, MAT_BAG_ROWS // 8, HEAD_DIM), False, None)],
                [BF16] * 5 + [F32] * 2,
                (0, nb_dw // 2, 2 * nb_dw - 1, 3 * nb_dw + nb_dw // 2, N_CHIPS * nb_dw - 1)))
    pieces = (8, D_MODEL // 2, IN_COLS // N_CHIPS)
    nb_seq = t // tb_seq
    dx, g_g1, r_in, vec_sum, mat_sum = _in_proj_bwd(
        dz, w_in_f, x2d, dx_res, norm_g, tb_seq,
        reduce=([(g_in.reshape(pieces), False, g_in16.reshape(pieces))], BF16,
                (0, nb_seq // 8, nb_seq // 2, nb_seq - 1, nb_seq - 1)),
        shards=[(vec_mine.reshape(VEC_BAG_ROWS // N_CHIPS, D_MODEL), 0, True),
                (mat_mine.reshape(MAT_BAG_ROWS // N_CHIPS, HEAD_DIM), 0, True)])

    def big_update(w, g2d, m, v, rows, name):
        d, nm, nv = _adamw(w[0], g2d, m[0], v[0], rows, name)
        return g2d[None], d[None], nm[None], nv[None]

    u_in = big_update(w_in, r_in.reshape(D_MODEL, IN_COLS // N_CHIPS), m_w_in, v_w_in, 256, "adamw_w_in")
    u_pl = big_update(w_proj_lru, r_pl.reshape(D_MODEL // N_CHIPS, D_MODEL), m_w_proj_lru, v_w_proj_lru, 256, "adamw_w_proj_lru")
    u_pp = big_update(w_proj_pool, r_pp.reshape(POOL_WIDTH, D_MODEL // N_CHIPS), m_w_proj_pool, v_w_proj_pool, 512, "adamw_w_proj_pool")
    u_out = big_update(w_out, r_out.reshape(D_MODEL // N_CHIPS, D_MODEL), m_w_out, v_w_out, 256, "adamw_w_out")
    u_pg = big_update(w_ple_gate, r_pg.reshape(D_MODEL // N_CHIPS, D_MODEL), m_w_ple_gate, v_w_ple_gate, 256, "adamw_w_ple_gate")
    u_pe = big_update(w_ple_proj, r_pe.reshape(p_dim, D_MODEL // N_CHIPS), m_w_ple_proj, v_w_ple_proj, 256, "adamw_w_ple_proj")

    small = [("norm_g", norm_g, m_norm_g, v_norm_g), ("conv_b", conv_b, m_conv_b, v_conv_b),
             ("lru_w_a", lru_w_a, m_lru_w_a, v_lru_w_a), ("lru_b_a", lru_b_a, m_lru_b_a, v_lru_b_a),
             ("lru_w_x", lru_w_x, m_lru_w_x, v_lru_w_x), ("lru_b_x", lru_b_x, m_lru_b_x, v_lru_b_x),
             ("lru_lambda", lru_lambda, m_lru_lambda, v_lru_lambda), ("pool_w", pool_w, m_pool_w, v_pool_w),
             ("pool_scale", pool_scale, m_pool_scale, v_pool_scale),
             ("ple_norm_g", ple_norm_g, m_ple_norm_g, v_ple_norm_g), ("final_g", final_g, m_final_g, v_final_g)]

    def view(a):
        return a.reshape(-1, a.shape[-1]) if a.ndim != 3 else a[0]

    cw_at = F32_SUBLANES * VEC_BAG_SLOTS.index("conv_w")
    cw_cols = D_MODEL // N_CHIPS
    g_cw = lax.dynamic_slice(vec_sum, (cw_at, chip * cw_cols), (CONV_WIDTH, cw_cols))
    flat = _adamw_replicated(vec_sum, mat_sum, g_g1, [(name,) + tuple(view(a) for a in arrs) for name, *arrs in small],
                             (conv_w[0], m_conv_w[0], v_conv_w[0], g_cw))
    u_small = {name: tuple(flat[4 * k + pick].reshape(arrs[0].shape) for pick in range(4))
               for k, (name, *arrs) in enumerate(small)}
    u_cw = tuple(a[None] for a in (g_cw,) + tuple(flat[4 * len(small):]))

    loss = vec_sum[F32_SUBLANES * VEC_BAG_SLOTS.index("loss"), 0]
    grad_x = dx.reshape(bsz, seq, D_MODEL)

    def ordered(pick):
        s = {name: u[pick] for name, u in u_small.items()}
        return [s["norm_g"], u_in[pick], u_cw[pick], s["conv_b"], s["lru_w_a"], s["lru_b_a"], s["lru_w_x"], s["lru_b_x"],
                s["lru_lambda"], s["pool_w"], s["pool_scale"], u_pl[pick], u_pp[pick], u_out[pick], s["ple_norm_g"],
                u_pg[pick], u_pe[pick], s["final_g"]]

    return (loss, grad_x, *ordered(0), *ordered(1), *ordered(2), *ordered(3))
```

```python
import jax
import jax.numpy as jnp
from jax import lax
from jax.experimental import pallas as pl
from jax.experimental.pallas import tpu as pltpu

F32 = jnp.float32
BF16 = jnp.bfloat16
MESH = pl.DeviceIdType.MESH

D_MODEL = 1024
LRU_HEADS = 8
HEAD_DIM = 128
CONV_WIDTH = 4
LRU_C = 8.0
POOL_WIDTH = 512
POOL_WINDOWS = (2, 4, 8, 16)
POOL_GROUP_DIM = 128
IN_COLS = 5120
N_CHIPS = 4
EPS = 1e-6

ADAM_LR = 0.001
ADAM_B1 = 0.9
ADAM_B2 = 0.999
ADAM_EPS = 1e-08
ADAM_WD = 0.01
ADAM_STEP = 10

F32_SUBLANES = 8
CONV_HIST = 8
POOL_HIST = 16
VMEM_LIMIT_BYTES = 58 * 1024 * 1024
VEC_BAG_SLOTS = ("norm_g", "conv_w", "conv_b", "lru_b_a", "lru_b_x", "lru_lambda", "pool_scale", "ple_norm_g",
                 "final_g", "loss")
VEC_BAG_ROWS = 128
MAT_BAG_AT = {"lru_w_a": 0, "lru_w_x": LRU_HEADS * HEAD_DIM, "pool_w": 2 * LRU_HEADS * HEAD_DIM}
MAT_BAG_ROWS = 2 * LRU_HEADS * HEAD_DIM + len(POOL_WINDOWS) * POOL_GROUP_DIM


def _bag_row(name, k=0):
    at = F32_SUBLANES * VEC_BAG_SLOTS.index(name) + k
    return slice(at, at + 1)


def _bag_rows(name):
    at = F32_SUBLANES * VEC_BAG_SLOTS.index(name)
    return slice(at, at + F32_SUBLANES)


def _dot(a, b):
    return jnp.dot(a, b, preferred_element_type=F32)


def _dot_nt(a, b):
    return lax.dot_general(a, b, (((1,), (1,)), ((), ())), preferred_element_type=F32)


def _dot_tn(a, b):
    return lax.dot_general(a, b, (((0,), (0,)), ((), ())), preferred_element_type=F32)


def _sigmoid(v):
    return jax.nn.sigmoid(v)


def _softplus(v):
    return jnp.maximum(v, 0.0) + jnp.log1p(jnp.exp(-jnp.abs(v)))


def _place():
    return lax.axis_index("x"), lax.axis_index("y"), lax.axis_index("c")


GATHER_SEMS = 6


def _gather_shapes(shards):
    out_shape = []
    for arr, axis, _ in shards:
        r, cols = arr.shape
        out_shape.append(jax.ShapeDtypeStruct((N_CHIPS * r, cols) if axis == 0 else (r, N_CHIPS * cols), arr.dtype))
    n = len(shards)
    sems = [pltpu.SemaphoreType.DMA((n * GATHER_SEMS,)), pltpu.SemaphoreType.DMA((n * GATHER_SEMS,)),
            pltpu.SemaphoreType.DMA((n,))]
    return out_shape, sems


def _gather_steps(shards, ins, outs, send_sems, recv_sems, local_sems):
    n = len(shards)
    x, y, c = _place()
    me, sibling = (x, y, c), (x, y, 1 - c)
    chips = [(x, 1 - y), (1 - x, y), (1 - x, 1 - y)]

    def region(k, cx, cy, hc):
        (r, cols), axis = shards[k][0].shape, shards[k][1]
        j = 2 * cx + cy
        if axis == 0:
            if hc is None:
                return outs[k].at[pl.ds(j * r, r), :]
            return outs[k].at[pl.ds(j * r + hc * (r // 2), r // 2), :]
        if hc is None:
            return outs[k].at[:, pl.ds(j * cols, cols)]
        return outs[k].at[pl.ds(hc * (r // 2), r // 2), pl.ds(j * cols, cols)]

    def remote(k, sem, block, to, src=None):
        dst = region(k, *block)
        return pltpu.make_async_remote_copy(
            src_ref=dst if src is None else src, dst_ref=dst,
            send_sem=send_sems.at[k * GATHER_SEMS + sem], recv_sem=recv_sems.at[k * GATHER_SEMS + sem],
            device_id=to, device_id_type=MESH)

    def first(k, idx):
        r, split = shards[k][0].shape[0], shards[k][2]
        src = ins[k].at[pl.ds(c * (r // 2), r // 2), :] if split else ins[k]
        return remote(k, idx, (x, y, c if split else None), (*chips[idx], c), src=src)

    def relay(k):
        src_chip = (jnp.bitwise_xor(x, 1 - c), jnp.bitwise_xor(y, c))
        dst_chip = (jnp.bitwise_xor(x, c), jnp.bitwise_xor(y, 1 - c))
        return remote(k, 2, (*src_chip, c), (*dst_chip, c))

    def passed(k, idx):
        return remote(k, 3 + idx, (*chips[idx], c), sibling)

    def mine(k):
        return pltpu.make_async_copy(ins[k], region(k, x, y, None), local_sems.at[k])

    def start():
        for k in range(n):
            mine(k).start()
            for idx in range(2 if shards[k][2] else 3):
                first(k, idx).start()

    def relay_on():
        for k in range(n):
            split = shards[k][2]
            for idx in range(2):
                remote(k, idx, (*chips[idx], c if split else None), me).wait_recv()
            if split:
                relay(k).start()
                passed(k, 0).start()
                passed(k, 1).start()

    def finish():
        for k in range(n):
            split = shards[k][2]
            remote(k, 2, (*chips[2], c if split else None), me).wait_recv()
            if split:
                passed(k, 2).start()
        for k in range(n):
            if shards[k][2]:
                for idx in range(3):
                    remote(k, 3 + idx, (*chips[idx], 1 - c), me).wait_recv()
        for k in range(n):
            if shards[k][2]:
                for cp in (first(k, 0), first(k, 1), relay(k), passed(k, 0), passed(k, 1), passed(k, 2)):
                    cp.wait_send()
            else:
                for idx in range(3):
                    first(k, idx).wait_send()
            mine(k).wait()

    return start, relay_on, finish


RS_ADD_ROWS = (64, 32, 16, 8)


N_DEV = 2 * N_CHIPS


def _all_reduce_scratch(shape):
    return [pltpu.VMEM((N_DEV,) + tuple(shape), F32), pltpu.SemaphoreType.DMA((N_DEV - 1,)),
            pltpu.SemaphoreType.DMA((N_DEV - 1,))]


def _all_reduce_tile(v_ref, o_ref, slots, send_sems, recv_sems):
    flips = [(dx, dy, dc) for dx in (0, 1) for dy in (0, 1) for dc in (0, 1)][1:]
    x, y, c = _place()
    mine = 4 * x + 2 * y + c

    def copy(k, to_flip, slot):
        dx, dy, dc = to_flip
        peer = (jnp.bitwise_xor(x, dx), jnp.bitwise_xor(y, dy), jnp.bitwise_xor(c, dc))
        return pltpu.make_async_remote_copy(
            src_ref=v_ref, dst_ref=slots.at[slot], send_sem=send_sems.at[k], recv_sem=recv_sems.at[k],
            device_id=peer, device_id_type=MESH)

    sends = [copy(k, flip, mine) for k, flip in enumerate(flips)]
    for cp in sends:
        cp.start()
    slots[mine] = v_ref[...]
    for k, (dx, dy, dc) in enumerate(flips):
        copy(k, (dx, dy, dc), jnp.bitwise_xor(mine, 4 * dx + 2 * dy + dc)).wait_recv()
    total = slots[0]
    for d in range(1, N_DEV):
        total = total + slots[d]
    o_ref[...] = total
    for cp in sends:
        cp.wait_send()


RS_SEMS = 8
RS_LOCAL_SEMS = 5


def _rs_piece_shape(part):
    arr, cols = part[0], part[1]
    return (arr.shape[0] // 2, arr.shape[1] // N_CHIPS) if cols else tuple(arr.shape[1:])


def _rs_operands(parts):
    return [p[0] for p in parts] + [p[0] if p[2] is None else p[2] for p in parts]


def _rs_wires(parts, wire):
    return list(wire) if isinstance(wire, (list, tuple)) else [wire] * len(parts)


def _rs_shapes(parts, wire):
    n = len(parts)
    shapes = [_rs_piece_shape(p) for p in parts]
    out_shape = [jax.ShapeDtypeStruct((2,) + s, F32) for s in shapes]
    scratch = []
    for lead, kind in ((N_CHIPS, "f32"), (N_CHIPS, "narrow"), (N_CHIPS, "wire"), (None, "f32"), (N_CHIPS, "wire")):
        for s, p, w in zip(shapes, parts, _rs_wires(parts, wire)):
            dtype = {"f32": F32, "narrow": F32 if p[2] is None else p[2].dtype, "wire": w}[kind]
            scratch.append(pltpu.VMEM(s if lead is None else (lead,) + s, dtype))
    scratch += [pltpu.SemaphoreType.DMA((n * RS_SEMS,)), pltpu.SemaphoreType.DMA((n * RS_SEMS,)),
                pltpu.SemaphoreType.DMA((n * RS_LOCAL_SEMS,))]
    return out_shape, scratch


def _rs_steps(parts, ins, outs, scratch):
    n = len(parts)
    own, sib, got, fin, snd = (scratch[k * n:(k + 1) * n] for k in range(5))
    send_sems, recv_sems, local_sems = scratch[5 * n:]
    shapes = [_rs_piece_shape(p) for p in parts]
    x, y, c = _place()
    j_me = 2 * x + y
    me, sibling = (x, y, c), (x, y, 1 - c)

    def piece(a, jj, core, narrow=False):
        ref = ins[n + a] if narrow else ins[a]
        if parts[a][1]:
            r, cl = shapes[a]
            return ref.at[pl.ds(core * r, r), pl.ds(jj * cl, cl)]
        return ref.at[2 * jj + core]

    def remote(a, sem, src, dst, to):
        return pltpu.make_async_remote_copy(
            src_ref=src, dst_ref=dst, send_sem=send_sems.at[a * RS_SEMS + sem],
            recv_sem=recv_sems.at[a * RS_SEMS + sem], device_id=to, device_id_type=MESH)

    def rows_loop(a, fn):
        r = shapes[a][0]
        step = max(s for s in RS_ADD_ROWS if r % s == 0)

        def it(i, carry):
            fn(pl.ds(pl.multiple_of(i * step, step), step))
            return carry

        lax.fori_loop(0, r // step, it, 0)

    def load(a, jj):
        return pltpu.make_async_copy(piece(a, jj, c), own[a].at[jj], local_sems.at[a * RS_LOCAL_SEMS + jj])

    def to_sibling(a, jj):
        return remote(a, jj, piece(a, jj, 1 - c, narrow=True), sib[a].at[jj], sibling)

    near = (jnp.bitwise_xor(x, 1 - c), jnp.bitwise_xor(y, c))
    far = (jnp.bitwise_xor(x, c), jnp.bitwise_xor(y, 1 - c))
    diag = (1 - x, 1 - y)
    FROM_NEAR, FROM_FAR, FEED = 0, 1, 2

    def chip_of(chip):
        return 2 * chip[0] + chip[1]

    def feed(a):
        return remote(a, 4, snd[a].at[chip_of(diag)], got[a].at[FEED], (*near, c))

    def to_near(a):
        return remote(a, 5, snd[a].at[chip_of(near)], got[a].at[FROM_NEAR], (*near, c))

    def to_far(a):
        return remote(a, 6, snd[a].at[chip_of(far)], got[a].at[FROM_FAR], (*far, c))

    def store(a):
        return pltpu.make_async_copy(fin[a], outs[a].at[c], local_sems.at[a * RS_LOCAL_SEMS + 4])

    def result_to_sibling(a):
        return remote(a, 7, fin[a], outs[a].at[c], sibling)

    def exchange():
        for a in range(n):
            for jj in range(N_CHIPS):
                load(a, jj).start()
                to_sibling(a, jj).start()

    def chip_sums():
        for a in range(n):
            for jj in range(N_CHIPS):
                load(a, jj).wait()
                remote(a, jj, sib[a].at[jj], sib[a].at[jj], me).wait_recv()

                def add(sl, a=a, jj=jj):
                    q = own[a][jj, sl, :] + sib[a][jj, sl, :].astype(F32)
                    own[a][jj, sl, :] = q
                    snd[a][jj, sl, :] = q.astype(snd[a].dtype)

                rows_loop(a, add)
        for a in range(n):
            feed(a).start()
        for a in range(n):
            to_near(a).start()

    def relay():
        for a in range(n):
            remote(a, 4, got[a].at[FEED], got[a].at[FEED], me).wait_recv()

            def add(sl, a=a):
                pair = own[a][chip_of(far), sl, :] + got[a][FEED, sl, :].astype(F32)
                snd[a][chip_of(far), sl, :] = pair.astype(snd[a].dtype)

            rows_loop(a, add)
            to_far(a).start()

    def totals():
        for a in range(n):
            remote(a, 5, got[a].at[FROM_NEAR], got[a].at[FROM_NEAR], me).wait_recv()
            remote(a, 6, got[a].at[FROM_FAR], got[a].at[FROM_FAR], me).wait_recv()

            def total(sl, a=a):
                fin[a][sl, :] = (own[a][j_me, sl, :] + got[a][FROM_NEAR, sl, :].astype(F32)) + (
                    got[a][FROM_FAR, sl, :].astype(F32))

            rows_loop(a, total)
            store(a).start()
            result_to_sibling(a).start()

    def finish():
        for a in range(n):
            remote(a, 7, outs[a].at[1 - c], outs[a].at[1 - c], me).wait_recv()
        for a in range(n):
            for jj in range(N_CHIPS):
                to_sibling(a, jj).wait_send()
            for cp in (feed(a), to_near(a), to_far(a), result_to_sibling(a)):
                cp.wait_send()
            store(a).wait()

    return exchange, chip_sums, relay, totals, finish


def _rms(x):
    r = lax.rsqrt(jnp.mean(x * x, axis=-1, keepdims=True) + EPS)
    return x * r, r


def _rms_bwd(dxn, xn, r):
    return r * (dxn - xn * jnp.mean(dxn * xn, axis=-1, keepdims=True))


def _in_proj_gather(x2d, norm_g, w_in_sh, shards, tb):
    t = x2d.shape[0]
    nb = t // tb
    cols = IN_COLS // N_CHIPS
    half = D_MODEL // 2
    n = len(shards)

    def body(x_ref, g_ref, win_ref, *refs):
        ins = refs[:n]
        z_ref, h_ref, wfull_ref = refs[n:n + 3]
        outs = refs[n + 3:2 * n + 3]
        wv, h_all, send_sems, recv_sems, local_sems, w_send, w_recv, w_local = refs[2 * n + 3:]
        s, i = pl.program_id(0), pl.program_id(1)
        x, y, c = _place()
        me, sibling = (x, y, c), (x, y, 1 - c)
        chips = [(x, 1 - y), (1 - x, y), (1 - x, 1 - y)]

        def w_half(cx, cy, hc):
            return wv.at[2 * cx + cy, pl.ds(hc * half, half), :]

        def w_remote(sem, block, to, src=None):
            dst = w_half(*block)
            return pltpu.make_async_remote_copy(
                src_ref=dst if src is None else src, dst_ref=dst, send_sem=w_send.at[sem],
                recv_sem=w_recv.at[sem], device_id=to, device_id_type=MESH)

        def w_first(idx):
            return w_remote(idx, (x, y, c), (*chips[idx], c), src=win_ref.at[pl.ds(c * half, half), :])

        def w_relay():
            src_chip = (jnp.bitwise_xor(x, 1 - c), jnp.bitwise_xor(y, c))
            dst_chip = (jnp.bitwise_xor(x, c), jnp.bitwise_xor(y, 1 - c))
            return w_remote(2, (*src_chip, c), (*dst_chip, c))

        def w_pass(idx):
            return w_remote(3 + idx, (*chips[idx], c), sibling)

        def w_store(k, cx, cy):
            jj = 2 * cx + cy
            return pltpu.make_async_copy(wv.at[jj], wfull_ref.at[:, pl.ds(jj * cols, cols)], w_local.at[k])

        start_rest, relay_rest, finish_rest = _gather_steps(shards, ins, outs, send_sems, recv_sems, local_sems)
        own = pltpu.make_async_copy(win_ref, wv.at[2 * x + y], w_local.at[4])

        @pl.when((s == 0) & (i == 0))
        def _():
            own.start()
            w_first(0).start()
            w_first(1).start()
            start_rest()
            own.wait()
            w_store(0, x, y).start()

        @pl.when((s == 1) & (i == 0))
        def _():
            w_remote(0, (*chips[0], c), me).wait_recv()
            w_remote(1, (*chips[1], c), me).wait_recv()
            w_relay().start()
            w_pass(0).start()
            w_pass(1).start()
            w_remote(3, (*chips[0], 1 - c), me).wait_recv()
            w_store(1, *chips[0]).start()

        @pl.when((s == 2) & (i == 0))
        def _():
            w_remote(4, (*chips[1], 1 - c), me).wait_recv()
            w_store(2, *chips[1]).start()

        @pl.when((s == 3) & (i == 0))
        def _():
            w_remote(2, (*chips[2], c), me).wait_recv()
            w_pass(2).start()
            w_remote(5, (*chips[2], 1 - c), me).wait_recv()
            w_store(3, *chips[2]).start()

        keep_h = pltpu.make_async_copy(h_all.at[i], h_ref.at[pl.ds(pl.multiple_of(i * tb, tb), tb), :], w_local.at[5])

        @pl.when(s == 0)
        def _():
            xn, _ = _rms(x_ref[...])
            h_all[i] = (xn * g_ref[...]).astype(BF16)
            keep_h.start()

        z_ref[...] = _dot(h_all[i], wv[jnp.bitwise_xor(2 * x + y, s)])
        pl.when(s == 0)(keep_h.wait)

        @pl.when((s == N_CHIPS - 1) & (i == nb - 1))
        def _():
            relay_rest()
            finish_rest()
            for cp in (w_first(0), w_first(1), w_relay(), w_pass(0), w_pass(1), w_pass(2)):
                cp.wait_send()
            w_store(0, x, y).wait()
            for idx in range(3):
                w_store(idx + 1, *chips[idx]).wait()

    rest_shape, rest_sems = _gather_shapes(shards)
    out_shape = [jax.ShapeDtypeStruct((t, IN_COLS), F32), jax.ShapeDtypeStruct((t, D_MODEL), BF16),
                 jax.ShapeDtypeStruct((D_MODEL, IN_COLS), BF16)] + rest_shape
    any_spec = pl.BlockSpec(memory_space=pl.ANY)

    def z_map(s, i):
        return (i, jnp.bitwise_xor(2 * lax.axis_index("x") + lax.axis_index("y"), s))

    return pl.pallas_call(
        body, name="in_proj", out_shape=tuple(out_shape),
        grid=(N_CHIPS, nb),
        in_specs=[pl.BlockSpec((tb, D_MODEL), lambda s, i: (jnp.where(s == 0, i, nb - 1), 0)),
                  pl.BlockSpec((1, D_MODEL), lambda s, i: (0, 0)), any_spec] + [any_spec] * n,
        out_specs=tuple([pl.BlockSpec((tb, cols), z_map), any_spec, any_spec] + [any_spec] * n),
        scratch_shapes=[pltpu.VMEM((N_CHIPS, D_MODEL, cols), BF16), pltpu.VMEM((nb, tb, D_MODEL), BF16)] + rest_sems + [
            pltpu.SemaphoreType.DMA((GATHER_SEMS,)), pltpu.SemaphoreType.DMA((GATHER_SEMS,)),
            pltpu.SemaphoreType.DMA((N_CHIPS + 2,))],
        compiler_params=pltpu.CompilerParams(dimension_semantics=("arbitrary", "arbitrary"),
                                             vmem_limit_bytes=VMEM_LIMIT_BYTES),
    )(x2d, norm_g, w_in_sh, *[sh[0] for sh in shards])


def _in_proj_bwd(dz, w_in, x2d, dx_res, norm_g, tb, reduce, shards):
    t = x2d.shape[0]
    nb = t // tb
    parts, wire, steps = reduce
    n = len(parts)
    k = len(shards)

    def body(dz_ref, w_ref, x_ref, dres_ref, g_ref, *refs):
        at = 2 * n + k
        dx_ref, dg_ref = refs[at:at + 2]
        rs_outs, g_outs = refs[at + 2:at + 2 + n], refs[at + 2 + n:at + 2 + n + k]
        scratch = refs[at + 2 + n + k:]
        rs_scr, g_sems, dg_acc, ar_scr = scratch[:-7], scratch[-7:-4], scratch[-4], scratch[-3:]
        rs = _rs_steps(parts, refs[:2 * n], rs_outs, rs_scr)
        for step, when in zip(rs, steps):
            pl.when(pl.program_id(0) == when)(step)
        gather = _gather_steps(shards, refs[2 * n:at], g_outs, *g_sems)
        for step, when in zip(gather, (0, nb // 2, nb - 1)):
            pl.when(pl.program_id(0) == when)(step)

        @pl.when(pl.program_id(0) == 0)
        def _():
            dg_acc[...] = jnp.zeros_like(dg_acc)

        xn, r = _rms(x_ref[...])
        g = g_ref[...]
        dh = _dot_nt(dz_ref[...], w_ref[...])
        dg_acc[0:1, :] += jnp.sum(dh * xn, axis=0, keepdims=True)
        dx_ref[...] = dres_ref[...] + _rms_bwd(dh * g, xn, r)

        @pl.when(pl.program_id(0) == nb - 1)
        def _():
            _all_reduce_tile(dg_acc, dg_ref, *ar_scr)

    row = lambda i: (i, 0)
    fixed = lambda i: (0, 0)
    rs_shape, rs_scratch = _rs_shapes(parts, wire)
    g_shape, g_sems = _gather_shapes(shards)
    any_spec = pl.BlockSpec(memory_space=pl.ANY)
    return pl.pallas_call(
        body, name="in_proj_bwd",
        out_shape=tuple([jax.ShapeDtypeStruct((t, D_MODEL), F32), jax.ShapeDtypeStruct((F32_SUBLANES, D_MODEL), F32)]
                        + rs_shape + g_shape),
        grid=(nb,),
        in_specs=[pl.BlockSpec((tb, IN_COLS), row),
                  pl.BlockSpec((D_MODEL, IN_COLS), fixed, pipeline_mode=pl.Buffered(1)),
                  pl.BlockSpec((tb, D_MODEL), row), pl.BlockSpec((tb, D_MODEL), row),
                  pl.BlockSpec((1, D_MODEL), fixed)] + [any_spec] * (2 * n + k),
        out_specs=tuple([pl.BlockSpec((tb, D_MODEL), row), pl.BlockSpec((F32_SUBLANES, D_MODEL), fixed)]
                        + [any_spec] * (n + k)),
        scratch_shapes=rs_scratch + g_sems + [pltpu.VMEM((F32_SUBLANES, D_MODEL), F32)] + _all_reduce_scratch(
            (F32_SUBLANES, D_MODEL)),
        compiler_params=pltpu.CompilerParams(dimension_semantics=("arbitrary",),
                                             vmem_limit_bytes=VMEM_LIMIT_BYTES),
    )(dz, w_in, x2d, dx_res, norm_g, *_rs_operands(parts), *[sh[0] for sh in shards])


def _weight_grad(lhs, rhs, n_chunks, tb, name, reduce=None):
    t, k = lhs.shape
    nc = rhs.shape[1] // n_chunks
    nb = t // tb
    parts, wire, steps = reduce if reduce is not None else ([], F32, ())
    n = len(parts)

    def body(l_ref, r_ref, *refs):
        o_ref, o16_ref = refs[2 * n:2 * n + 2]
        if n:
            at = pl.program_id(0) * nb + pl.program_id(1)
            rs = _rs_steps(parts, refs[:2 * n], refs[2 * n + 2:3 * n + 2], refs[3 * n + 2:])
            for step, when in zip(rs, steps):
                pl.when(at == when)(step)

        @pl.when(pl.program_id(1) == 0)
        def _():
            o_ref[...] = jnp.zeros_like(o_ref)

        o_ref[...] += _dot_tn(l_ref[...], r_ref[...])

        @pl.when(pl.program_id(1) == nb - 1)
        def _():
            o16_ref[...] = o_ref[...].astype(BF16)

    rs_shape, rs_scratch = _rs_shapes(parts, wire) if n else ([], [])
    any_spec = pl.BlockSpec(memory_space=pl.ANY)
    chunk = pl.BlockSpec((None, k, nc), lambda j, i: (j, 0, 0))
    return pl.pallas_call(
        body, name=name,
        out_shape=tuple([jax.ShapeDtypeStruct((n_chunks, k, nc), F32), jax.ShapeDtypeStruct((n_chunks, k, nc), BF16)]
                        + rs_shape),
        grid=(n_chunks, nb),
        in_specs=[pl.BlockSpec((tb, k), lambda j, i: (i, 0)), pl.BlockSpec((tb, nc), lambda j, i: (i, j))]
        + [any_spec] * (2 * n),
        out_specs=tuple([chunk, chunk] + [any_spec] * n),
        scratch_shapes=rs_scratch,
        compiler_params=pltpu.CompilerParams(dimension_semantics=("arbitrary", "arbitrary"),
                                             vmem_limit_bytes=VMEM_LIMIT_BYTES),
    )(lhs, rhs, *_rs_operands(parts))


def _adam_update(w, g, m, v):
    m_ = ADAM_B1 * m + (1.0 - ADAM_B1) * g
    v_ = ADAM_B2 * v + (1.0 - ADAM_B2) * jnp.square(g)
    m_hat = m_ / (1.0 - ADAM_B1 ** ADAM_STEP)
    v_hat = v_ / (1.0 - ADAM_B2 ** ADAM_STEP)
    return -ADAM_LR * (m_hat / (jnp.sqrt(v_hat) + ADAM_EPS) + ADAM_WD * w), m_, v_


def _adamw_replicated(vec_sum, mat_sum, norm_grad, entries, conv):
    n = len(entries)

    def grad_of(name, shape, vec_ref, mat_ref, norm_ref):
        if name == "norm_g":
            return norm_ref[0:1, :]
        if name in MAT_BAG_AT:
            return mat_ref[MAT_BAG_AT[name]:MAT_BAG_AT[name] + shape[0], :]
        if shape[0] == 1:
            return vec_ref[_bag_row(name), 0:shape[1]]
        return jnp.concatenate([vec_ref[_bag_row(name), h * shape[1]:(h + 1) * shape[1]] for h in range(shape[0])],
                               axis=0)

    def body(vec_ref, mat_ref, norm_ref, *refs):
        ins, outs = refs[:3 * n + 4], refs[3 * n + 4:]
        for k in range(n):
            w_ref, m_ref, v_ref = ins[3 * k:3 * k + 3]
            g = grad_of(entries[k][0], w_ref.shape, vec_ref, mat_ref, norm_ref)
            d, m_, v_ = _adam_update(w_ref[...], g, m_ref[...], v_ref[...])
            for ref, val in zip(outs[4 * k:4 * k + 4], (g, d, m_, v_)):
                ref[...] = val
        w_ref, m_ref, v_ref, g_ref = ins[3 * n:]
        for ref, val in zip(outs[4 * n:], _adam_update(w_ref[...], g_ref[...], m_ref[...], v_ref[...])):
            ref[...] = val

    arrays = [a for e in entries for a in e[1:]] + list(conv)
    out_shape = [jax.ShapeDtypeStruct(e[1].shape, F32) for e in entries for _ in range(4)]
    out_shape += [jax.ShapeDtypeStruct(conv[0].shape, F32)] * 3
    return pl.pallas_call(
        body, name="adamw_replicated", out_shape=tuple(out_shape),
        compiler_params=pltpu.CompilerParams(vmem_limit_bytes=VMEM_LIMIT_BYTES),
    )(vec_sum, mat_sum, norm_grad, *arrays)


def _adamw(w, g, m, v, rows, name):
    r, c = w.shape

    def body(w_ref, g_ref, m_ref, v_ref, d_ref, nm_ref, nv_ref):
        d_ref[...], nm_ref[...], nv_ref[...] = _adam_update(w_ref[...], g_ref[...], m_ref[...], v_ref[...])

    spec = pl.BlockSpec((rows, c), lambda i: (i, 0))
    return pl.pallas_call(
        body, name=name, out_shape=tuple(jax.ShapeDtypeStruct((r, c), F32) for _ in range(3)),
        grid=(r // rows,), in_specs=[spec] * 4, out_specs=(spec,) * 3,
        compiler_params=pltpu.CompilerParams(dimension_semantics=("arbitrary",),
                                             vmem_limit_bytes=VMEM_LIMIT_BYTES),
    )(w, g, m, v)


def _shift_down(ext, s):
    return pltpu.roll(ext, s, 0)


def _tile_shift(v, s):
    rows, cols = v.shape
    tiles = v.reshape(rows // F32_SUBLANES, F32_SUBLANES, cols)
    return pltpu.roll(tiles, s % F32_SUBLANES, 1).reshape(rows, cols)


def _shift_up(ext, s):
    return pltpu.roll(ext, ext.shape[0] - s, 0)


def _lru_gates(xc, wa_ref, ba, wx_ref, bx, lam):
    pa, px = [], []
    for h in range(LRU_HEADS):
        xh = xc[:, h * HEAD_DIM:(h + 1) * HEAD_DIM].astype(BF16)
        pa.append(_dot(xh, wa_ref[h]))
        px.append(_dot(xh, wx_ref[h]))
    r = _sigmoid(jnp.concatenate(pa, axis=1) + ba)
    ig = _sigmoid(jnp.concatenate(px, axis=1) + bx)
    sp = _softplus(-lam)
    log_a = (-LRU_C * r) * sp
    a = jnp.exp(log_a)
    mult = jnp.sqrt(jnp.tanh(-log_a) * (1.0 + a * a))
    return r, ig, a, mult, sp


def _conv(ext, w_ref, b):
    y = b + _shift_down(ext, 3) * w_ref[0:1, :]
    y = y + _shift_down(ext, 2) * w_ref[1:2, :]
    y = y + _shift_down(ext, 1) * w_ref[2:3, :]
    y = y + ext * w_ref[3:4, :]
    return y[CONV_HIST:, :]


def _pool_diff(ext, pos):
    out = []
    for g, k in enumerate(POOL_WINDOWS):
        col = ext[:, g * POOL_GROUP_DIM:(g + 1) * POOL_GROUP_DIM]
        s = col
        for step in range(g + 1):
            s = s + _shift_down(s, 2 ** step)
        count = jnp.minimum(pos + 1, k).astype(F32)
        out.append(s[POOL_HIST:, :] / count - col[POOL_HIST:, :])
    return out


def _pool_mix(diff, pw_ref):
    return jnp.concatenate([_dot(diff[g].astype(BF16), pw_ref[g]) for g in range(len(POOL_WINDOWS))], axis=1)


def _branch_specs(tb, row_map, fixed):
    fixed3 = lambda i: (0, 0, 0)
    return [pl.BlockSpec((CONV_WIDTH, D_MODEL), fixed), pl.BlockSpec((1, D_MODEL), fixed),
            pl.BlockSpec((LRU_HEADS, HEAD_DIM, HEAD_DIM), fixed3), pl.BlockSpec((1, D_MODEL), fixed),
            pl.BlockSpec((LRU_HEADS, HEAD_DIM, HEAD_DIM), fixed3), pl.BlockSpec((1, D_MODEL), fixed),
            pl.BlockSpec((1, D_MODEL), fixed),
            pl.BlockSpec((len(POOL_WINDOWS), POOL_GROUP_DIM, POOL_GROUP_DIM), fixed3),
            pl.BlockSpec((1, POOL_WIDTH), fixed)]


def _branches_fwd(z, weights, seq, tb, shards):
    t = z.shape[0]
    nb = t // tb
    nbe = seq // tb
    groups = tb // F32_SUBLANES
    n = len(shards)

    def body(xa_ref, ga_ref, xb_ref, gb_ref, cw_ref, cb_ref, wa_ref, ba_ref, wx_ref, bx_ref, lam_ref,
             pw_ref, ps_ref, *refs):
        g_ins = refs[:n]
        ya_ref, yb_ref, hl_ref = refs[n:n + 3]
        g_outs = refs[n + 3:2 * n + 3]
        xa_ext, xb_ext, carry, a_s, u_s, send_sems, recv_sems, local_sems = refs[2 * n + 3:]
        blk = pl.program_id(0) % nbe
        start_gather, relay_gather, finish_gather = _gather_steps(shards, g_ins, g_outs, send_sems, recv_sems,
                                                                  local_sems)
        pl.when(pl.program_id(0) == 0)(start_gather)
        pl.when(pl.program_id(0) == nb // 2)(relay_gather)

        @pl.when(blk == 0)
        def _():
            xa_ext[0:CONV_HIST, :] = jnp.zeros((CONV_HIST, D_MODEL), F32)
            xb_ext[0:POOL_HIST, :] = jnp.zeros((POOL_HIST, POOL_WIDTH), F32)
            carry[...] = jnp.zeros_like(carry)

        xa_ext[CONV_HIST:, :] = xa_ref[...]
        xb_ext[POOL_HIST:, :] = xb_ref[...]
        ea = xa_ext[...]
        eb = xb_ext[...]
        xa_ext[0:CONV_HIST, :] = ea[tb:, :]
        xb_ext[0:POOL_HIST, :] = eb[tb:, :]

        xc = _conv(ea, cw_ref, cb_ref[...])
        _, ig, a, mult, _ = _lru_gates(xc, wa_ref, ba_ref[...], wx_ref, bx_ref[...], lam_ref[...])
        u = mult * (ig * xc)
        row8 = lax.broadcasted_iota(jnp.int32, (tb, D_MODEL), 0) % F32_SUBLANES
        for s in (1, 2, 4):
            m = row8 >= s
            u = jnp.where(m, a * _tile_shift(u, s) + u, u)
            a = jnp.where(m, a * _tile_shift(a, s), a)
        a_s[...] = a
        u_s[...] = u

        def step(g, cr):
            sl = pl.ds(pl.multiple_of(g * F32_SUBLANES, F32_SUBLANES), F32_SUBLANES)
            hb = a_s[sl, :] * cr + u_s[sl, :]
            hl_ref[sl, :] = hb
            return jnp.broadcast_to(hb[F32_SUBLANES - 1:F32_SUBLANES, :], (F32_SUBLANES, D_MODEL))

        carry[...] = lax.fori_loop(0, groups, step, carry[...], unroll=4)
        ga = ga_ref[...]
        ya_ref[...] = (hl_ref[...] * (ga * _sigmoid(ga))).astype(BF16)

        pos = blk * tb + lax.broadcasted_iota(jnp.int32, (tb, POOL_GROUP_DIM), 0)
        ypre = _pool_mix(_pool_diff(eb, pos), pw_ref)
        gb = gb_ref[...]
        yb_ref[...] = ((ypre * ps_ref[...]) * (gb * _sigmoid(gb))).astype(BF16)
        pl.when(pl.program_id(0) == nb - 1)(finish_gather)

    row = lambda i: (i, 0)
    fixed = lambda i: (0, 0)
    any_spec = pl.BlockSpec(memory_space=pl.ANY)
    in_specs = [pl.BlockSpec((tb, D_MODEL), lambda i: (i, 0)), pl.BlockSpec((tb, D_MODEL), lambda i: (i, 1)),
                pl.BlockSpec((tb, POOL_WIDTH), lambda i: (i, 4)), pl.BlockSpec((tb, POOL_WIDTH), lambda i: (i, 5)),
                ] + _branch_specs(tb, row, fixed) + [any_spec] * n
    g_shape, g_sems = _gather_shapes(shards)
    return pl.pallas_call(
        body, name="branches_fwd",
        out_shape=tuple([jax.ShapeDtypeStruct((t, D_MODEL), BF16), jax.ShapeDtypeStruct((t, POOL_WIDTH), BF16),
                         jax.ShapeDtypeStruct((t, D_MODEL), F32)] + g_shape),
        grid=(nb,), in_specs=in_specs,
        out_specs=tuple([pl.BlockSpec((tb, D_MODEL), row), pl.BlockSpec((tb, POOL_WIDTH), row),
                         pl.BlockSpec((tb, D_MODEL), row)] + [any_spec] * n),
        scratch_shapes=[pltpu.VMEM((tb + CONV_HIST, D_MODEL), F32), pltpu.VMEM((tb + POOL_HIST, POOL_WIDTH), F32),
                        pltpu.VMEM((F32_SUBLANES, D_MODEL), F32),
                        pltpu.VMEM((tb, D_MODEL), F32), pltpu.VMEM((tb, D_MODEL), F32)] + g_sems,
        compiler_params=pltpu.CompilerParams(dimension_semantics=("arbitrary",),
                                             vmem_limit_bytes=VMEM_LIMIT_BYTES),
    )(z, z, z, z, *weights, *[sh[0] for sh in shards])


def _branches_bwd(z, hl, dya, dyb, dzm, weights, vec_bag, seq, tb):
    t = z.shape[0]
    nb = t // tb
    nbe = seq // tb
    groups = tb // F32_SUBLANES

    def body(xa_ref, xap_ref, ga_ref, xb_ref, xbp_ref, gb_ref, hl_ref, hlp_ref, dya_ref, dyb_ref, dzm_ref,
             cw_ref, cb_ref, wa_ref, ba_ref, wx_ref, bx_ref, lam_ref, pw_ref, ps_ref, vec_in_ref,
             dz_ref, vec_ref, mat_ref,
             xa_ext, xb_ext, hl_ext, a_ext, dxc_ext, dwin_ext, g_carry, b_s, d_s, g_s):
        i = pl.program_id(0)
        blk = (nb - 1 - i) % nbe

        def mat_rows(name, k):
            at = MAT_BAG_AT[name] + k * HEAD_DIM
            return slice(at, at + HEAD_DIM)

        @pl.when(i == 0)
        def _():
            vec_ref[...] = vec_in_ref[...]
            mat_ref[...] = jnp.zeros_like(mat_ref)

        @pl.when(blk == nbe - 1)
        def _():
            a_ext[tb:, :] = jnp.zeros((F32_SUBLANES, D_MODEL), F32)
            dxc_ext[tb:, :] = jnp.zeros((CONV_HIST, D_MODEL), F32)
            dwin_ext[tb:, :] = jnp.zeros((POOL_HIST, POOL_WIDTH), F32)
            g_carry[...] = jnp.zeros_like(g_carry)

        live = (blk > 0).astype(F32)
        xa_ext[0:CONV_HIST, :] = xap_ref[...] * live
        xa_ext[CONV_HIST:, :] = xa_ref[...]
        xb_ext[0:POOL_HIST, :] = xbp_ref[...] * live
        xb_ext[POOL_HIST:, :] = xb_ref[...]
        hl_ext[0:F32_SUBLANES, :] = hlp_ref[...] * live
        hl_ext[F32_SUBLANES:, :] = hl_ref[...]
        ea = xa_ext[...]
        eb = xb_ext[...]

        xc = _conv(ea, cw_ref, cb_ref[...])
        lam = lam_ref[...]
        r, ig, a, mult, sp = _lru_gates(xc, wa_ref, ba_ref[...], wx_ref, bx_ref[...], lam)
        hl = hl_ref[...]
        ga = ga_ref[...]
        sga = _sigmoid(ga)
        dya = dya_ref[...]
        dhl = dya * (ga * sga)
        dz_ref[:, D_MODEL:2 * D_MODEL] = (dya * hl * (sga * (1.0 + ga * (1.0 - sga)))).astype(BF16)

        a_ext[0:tb, :] = a
        b = _shift_up(a_ext[...], 1)[0:tb, :]
        a_ext[tb:, :] = jnp.broadcast_to(a[0:1, :], (F32_SUBLANES, D_MODEL))
        d = dhl
        row8 = lax.broadcasted_iota(jnp.int32, (tb, D_MODEL), 0) % F32_SUBLANES
        for s in (1, 2, 4):
            m = row8 < F32_SUBLANES - s
            d = jnp.where(m, d + b * _tile_shift(d, -s), d)
            b = jnp.where(m, b * _tile_shift(b, -s), b)
        b_s[...] = b
        d_s[...] = d

        def step(k, cr):
            sl = pl.ds(pl.multiple_of((groups - 1 - k) * F32_SUBLANES, F32_SUBLANES), F32_SUBLANES)
            gb_ = d_s[sl, :] + b_s[sl, :] * cr
            g_s[sl, :] = gb_
            return jnp.broadcast_to(gb_[0:1, :], (F32_SUBLANES, D_MODEL))

        g_carry[...] = lax.fori_loop(0, groups, step, g_carry[...], unroll=4)
        gsc = g_s[...]
        da = gsc * _shift_down(hl_ext[...], 1)[F32_SUBLANES:, :]
        dmult = gsc * (ig * xc)
        dig = gsc * (mult * xc)
        dxc = gsc * (mult * ig)
        dlog_a = da * a - (a * a) * dmult / mult
        dr = dlog_a * (-LRU_C * sp)
        vec_ref[_bag_row("lru_lambda"), :] += jnp.sum(dlog_a * (-LRU_C * r), axis=0, keepdims=True)
        dpa = dr * (r * (1.0 - r))
        dpx = dig * (ig * (1.0 - ig))
        vec_ref[_bag_row("lru_b_a"), :] += jnp.sum(dpa, axis=0, keepdims=True)
        vec_ref[_bag_row("lru_b_x"), :] += jnp.sum(dpx, axis=0, keepdims=True)
        back = []
        for h in range(LRU_HEADS):
            cols = slice(h * HEAD_DIM, (h + 1) * HEAD_DIM)
            xh = xc[:, cols].astype(BF16)
            dpa_h = dpa[:, cols].astype(BF16)
            dpx_h = dpx[:, cols].astype(BF16)
            mat_ref[mat_rows("lru_w_a", h), :] += _dot_tn(xh, dpa_h)
            mat_ref[mat_rows("lru_w_x", h), :] += _dot_tn(xh, dpx_h)
            back.append(_dot_nt(dpa_h, wa_ref[h]) + _dot_nt(dpx_h, wx_ref[h]))
        dxc = dxc + jnp.concatenate(back, axis=1)
        vec_ref[_bag_row("conv_b"), :] += jnp.sum(dxc, axis=0, keepdims=True)
        for k in range(CONV_WIDTH):
            tap = _shift_down(ea, CONV_WIDTH - 1 - k)[CONV_HIST:, :] if k < CONV_WIDTH - 1 else ea[CONV_HIST:, :]
            vec_ref[_bag_row("conv_w", k), :] += jnp.sum(dxc * tap, axis=0, keepdims=True)
        dxc_ext[0:tb, :] = dxc
        ed = dxc_ext[...]
        dxa = ed * cw_ref[3:4, :]
        dxa = dxa + _shift_up(ed, 1) * cw_ref[2:3, :]
        dxa = dxa + _shift_up(ed, 2) * cw_ref[1:2, :]
        dxa = dxa + _shift_up(ed, 3) * cw_ref[0:1, :]
        dz_ref[:, 0:D_MODEL] = dxa[0:tb, :].astype(BF16)
        dxc_ext[tb:, :] = dxc[0:CONV_HIST, :]

        pos = blk * tb + lax.broadcasted_iota(jnp.int32, (tb, POOL_GROUP_DIM), 0)
        diff = _pool_diff(eb, pos)
        ypre = _pool_mix(diff, pw_ref)
        ps = ps_ref[...]
        gb = gb_ref[...]
        sgb = _sigmoid(gb)
        dyb = dyb_ref[...]
        dyp = dyb * (gb * sgb)
        dz_ref[:, 2 * D_MODEL + POOL_WIDTH:3 * D_MODEL] = (
            dyb * (ypre * ps) * (sgb * (1.0 + gb * (1.0 - sgb)))).astype(BF16)
        vec_ref[_bag_row("pool_scale"), 0:POOL_WIDTH] += jnp.sum(dyp * ypre, axis=0, keepdims=True)
        dypre = dyp * ps
        for g, k in enumerate(POOL_WINDOWS):
            cols = slice(g * POOL_GROUP_DIM, (g + 1) * POOL_GROUP_DIM)
            dyg = dypre[:, cols].astype(BF16)
            mat_ref[mat_rows("pool_w", g), :] += _dot_tn(diff[g].astype(BF16), dyg)
            ddiff = _dot_nt(dyg, pw_ref[g])
            count = jnp.minimum(pos + 1, k).astype(F32)
            dwin = ddiff / count
            dwin_ext[0:tb, cols] = dwin
            s = dwin_ext[:, cols]
            for step_ in range(g + 1):
                s = s + _shift_up(s, 2 ** step_)
            dz_ref[:, 2 * D_MODEL + g * POOL_GROUP_DIM:2 * D_MODEL + (g + 1) * POOL_GROUP_DIM] = (
                s[0:tb, :] - ddiff).astype(BF16)
            dwin_ext[tb:, cols] = dwin[0:POOL_HIST, :]

        dz_ref[:, 3 * D_MODEL:] = dzm_ref[...]

        @pl.when(i == nb - 1)
        def _():
            row = _bag_row("lru_lambda")
            vec_ref[row, :] = vec_ref[row, :] * (-_sigmoid(-lam))

    rev = lambda i: (nb - 1 - i, 0)
    fixed = lambda i: (0, 0)

    def prev(rows, col):
        per = tb // rows
        return lambda i: (jnp.maximum((nb - 1 - i) * per - 1, 0), col)

    in_specs = [pl.BlockSpec((tb, D_MODEL), lambda i: (nb - 1 - i, 0)),
                pl.BlockSpec((CONV_HIST, D_MODEL), prev(CONV_HIST, 0)),
                pl.BlockSpec((tb, D_MODEL), lambda i: (nb - 1 - i, 1)),
                pl.BlockSpec((tb, POOL_WIDTH), lambda i: (nb - 1 - i, 4)),
                pl.BlockSpec((POOL_HIST, POOL_WIDTH), prev(POOL_HIST, 4)),
                pl.BlockSpec((tb, POOL_WIDTH), lambda i: (nb - 1 - i, 5)),
                pl.BlockSpec((tb, D_MODEL), rev),
                pl.BlockSpec((F32_SUBLANES, D_MODEL), prev(F32_SUBLANES, 0)),
                pl.BlockSpec((tb, D_MODEL), rev), pl.BlockSpec((tb, POOL_WIDTH), rev),
                pl.BlockSpec((tb, 2 * D_MODEL), rev)] + _branch_specs(tb, rev, fixed) + [
                    pl.BlockSpec((VEC_BAG_ROWS, D_MODEL), fixed)]
    out_shape = (jax.ShapeDtypeStruct((t, IN_COLS), BF16), jax.ShapeDtypeStruct((VEC_BAG_ROWS, D_MODEL), F32),
                 jax.ShapeDtypeStruct((MAT_BAG_ROWS, HEAD_DIM), F32))
    out_specs = (pl.BlockSpec((tb, IN_COLS), rev), pl.BlockSpec((VEC_BAG_ROWS, D_MODEL), fixed),
                 pl.BlockSpec((MAT_BAG_ROWS, HEAD_DIM), fixed))
    scratch = [pltpu.VMEM((tb + CONV_HIST, D_MODEL), F32), pltpu.VMEM((tb + POOL_HIST, POOL_WIDTH), F32),
               pltpu.VMEM((tb + F32_SUBLANES, D_MODEL), F32), pltpu.VMEM((tb + F32_SUBLANES, D_MODEL), F32),
               pltpu.VMEM((tb + CONV_HIST, D_MODEL), F32), pltpu.VMEM((tb + POOL_HIST, POOL_WIDTH), F32),
               pltpu.VMEM((F32_SUBLANES, D_MODEL), F32),
               pltpu.VMEM((tb, D_MODEL), F32), pltpu.VMEM((tb, D_MODEL), F32), pltpu.VMEM((tb, D_MODEL), F32)]
    return pl.pallas_call(
        body, name="branches_bwd", out_shape=out_shape, grid=(nb,), in_specs=in_specs, out_specs=out_specs,
        scratch_shapes=scratch, input_output_aliases={len(in_specs) - 1: 1},
        compiler_params=pltpu.CompilerParams(dimension_semantics=("arbitrary",),
                                             vmem_limit_bytes=VMEM_LIMIT_BYTES),
    )(z, z, z, z, z, z, hl, hl, dya, dyb, dzm, *weights, vec_bag)


def _merge_head(x2d, ya, yb, z, p2d, tgt, w_pl, w_pp, w_out, w_pg, w_pe, g2, gf, tb):
    t = x2d.shape[0]
    p_dim = p2d.shape[1]

    def body(x_ref, ya_ref, yb_ref, ma_ref, mb_ref, p_ref, t_ref, wpl_ref, wpp_ref, wout_ref, wpg_ref, wpe_ref,
             g2_ref, gf_ref,
             bag_ref, dxr_ref, dya_ref, dyb_ref, dzm_ref,
             mg_ref, do_ref, hn_ref, dgp_ref, dpe_ref, da_ref, dbm_ref, pbf_ref):
        @pl.when(pl.program_id(0) == 0)
        def _():
            bag_ref[...] = jnp.zeros_like(bag_ref)

        a_ = _dot(ya_ref[...], wpl_ref[...])
        bm = _dot(yb_ref[...], wpp_ref[...])
        sa = _sigmoid(ma_ref[...])
        sb = _sigmoid(mb_ref[...])
        mg = (sa * a_ + sb * bm).astype(BF16)
        mg_ref[...] = mg
        x1 = x_ref[...] + _dot(mg, wout_ref[...])
        xn2, r2 = _rms(x1)
        g2 = g2_ref[...]
        hn = (xn2 * g2).astype(BF16)
        hn_ref[...] = hn
        gate = _sigmoid(_dot(hn, wpg_ref[...]))
        pbf = p_ref[...].astype(BF16)
        pbf_ref[...] = pbf
        pe = _dot(pbf, wpe_ref[...])
        x2 = x1 + gate * pe
        xn3, r3 = _rms(x2)
        gf = gf_ref[...]
        err = xn3 * gf - t_ref[...]
        bag_ref[_bag_rows("loss"), 0:128] += 0.5 * jnp.sum(jnp.mean(err * err, axis=-1))

        dy = err * (1.0 / D_MODEL)
        bag_ref[_bag_row("final_g"), :] += jnp.sum(dy * xn3, axis=0, keepdims=True)
        dx2 = _rms_bwd(dy * gf, xn3, r3)
        dpe_ref[...] = (dx2 * gate).astype(BF16)
        dgp = ((dx2 * pe) * (gate * (1.0 - gate))).astype(BF16)
        dgp_ref[...] = dgp
        dhn = _dot_nt(dgp, wpg_ref[...])
        bag_ref[_bag_row("ple_norm_g"), :] += jnp.sum(dhn * xn2, axis=0, keepdims=True)
        dx1 = dx2 + _rms_bwd(dhn * g2, xn2, r2)
        dxr_ref[...] = dx1
        do = dx1.astype(BF16)
        do_ref[...] = do
        dmg = _dot_nt(do, wout_ref[...])
        da = (dmg * sa).astype(BF16)
        dbm = (dmg * sb).astype(BF16)
        da_ref[...] = da
        dbm_ref[...] = dbm
        dzm_ref[:, 0:D_MODEL] = (dmg * a_ * (sa * (1.0 - sa))).astype(BF16)
        dzm_ref[:, D_MODEL:] = (dmg * bm * (sb * (1.0 - sb))).astype(BF16)
        dya_ref[...] = _dot_nt(da, wpl_ref[...])
        dyb_ref[...] = _dot_nt(dbm, wpp_ref[...])

    row = lambda i: (i, 0)
    fixed = lambda i: (0, 0)

    def resident(shape):
        return pl.BlockSpec(shape, fixed, pipeline_mode=pl.Buffered(1))

    tok = lambda width: pl.BlockSpec((tb, width), row)
    in_specs = [tok(D_MODEL), tok(D_MODEL), tok(POOL_WIDTH),
                pl.BlockSpec((tb, D_MODEL), lambda i: (i, 3)), pl.BlockSpec((tb, D_MODEL), lambda i: (i, 4)),
                tok(p_dim), tok(D_MODEL),
                resident((D_MODEL, D_MODEL)), resident((POOL_WIDTH, D_MODEL)), resident((D_MODEL, D_MODEL)),
                resident((D_MODEL, D_MODEL)), resident((p_dim, D_MODEL)),
                pl.BlockSpec((1, D_MODEL), fixed), pl.BlockSpec((1, D_MODEL), fixed)]
    bf = lambda width: jax.ShapeDtypeStruct((t, width), BF16)
    f32 = lambda width: jax.ShapeDtypeStruct((t, width), F32)
    out_shape = (jax.ShapeDtypeStruct((VEC_BAG_ROWS, D_MODEL), F32),
                 f32(D_MODEL), f32(D_MODEL), f32(POOL_WIDTH), bf(2 * D_MODEL),
                 bf(D_MODEL), bf(D_MODEL), bf(D_MODEL), bf(D_MODEL), bf(D_MODEL), bf(D_MODEL), bf(D_MODEL), bf(p_dim))
    out_specs = (pl.BlockSpec((VEC_BAG_ROWS, D_MODEL), fixed),
                 tok(D_MODEL), tok(D_MODEL), tok(POOL_WIDTH), tok(2 * D_MODEL),
                 tok(D_MODEL), tok(D_MODEL), tok(D_MODEL), tok(D_MODEL), tok(D_MODEL), tok(D_MODEL), tok(D_MODEL),
                 tok(p_dim))
    return pl.pallas_call(
        body, name="merge_head", out_shape=out_shape, grid=(t // tb,), in_specs=in_specs, out_specs=out_specs,
        compiler_params=pltpu.CompilerParams(dimension_semantics=("arbitrary",),
                                             vmem_limit_bytes=VMEM_LIMIT_BYTES),
    )(x2d, ya, yb, z, z, p2d, tgt, w_pl, w_pp, w_out, w_pg, w_pe, g2, gf)


def kernel(x, p, norm_g, w_in, conv_w, conv_b, lru_w_a, lru_b_a, lru_w_x, lru_b_x, lru_lambda, pool_w, pool_scale, w_proj_lru, w_proj_pool, w_out, ple_norm_g, w_ple_gate, w_ple_proj, final_g, loss_target, m_norm_g, m_w_in, m_conv_w, m_conv_b, m_lru_w_a, m_lru_b_a, m_lru_w_x, m_lru_b_x, m_lru_lambda, m_pool_w, m_pool_scale, m_w_proj_lru, m_w_proj_pool, m_w_out, m_ple_norm_g, m_w_ple_gate, m_w_ple_proj, m_final_g, v_norm_g, v_w_in, v_conv_w, v_conv_b, v_lru_w_a, v_lru_b_a, v_lru_w_x, v_lru_b_x, v_lru_lambda, v_pool_w, v_pool_scale, v_w_proj_lru, v_w_proj_pool, v_w_out, v_ple_norm_g, v_w_ple_gate, v_w_ple_proj, v_final_g):
    bsz, seq, _ = x.shape
    t = bsz * seq
    tb_mm = min(1024, seq)
    tb_seq = min(256, seq // 2) if seq >= 512 else seq
    x2d = x.reshape(t, D_MODEL)
    p2d = p.reshape(t, p.shape[-1])
    tgt = loss_target.reshape(t, D_MODEL)
    chip = 2 * lax.axis_index("x") + lax.axis_index("y")

    rest = [(w_proj_lru[0], 0), (w_proj_pool[0], 1), (w_out[0], 0), (w_ple_gate[0], 0), (w_ple_proj[0], 1)]
    z, h_bf, w_in_f, conv_w_f = _in_proj_gather(x2d, norm_g, w_in[0].astype(BF16), [(conv_w[0], 1, False)], tb_mm)

    wa_bf = lru_w_a[0].astype(BF16)
    wx_bf = lru_w_x[0].astype(BF16)
    pw_bf = pool_w[0].astype(BF16)
    branch_w = (conv_w_f, conv_b, wa_bf, lru_b_a.reshape(1, D_MODEL), wx_bf, lru_b_x.reshape(1, D_MODEL),
                lru_lambda, pw_bf, pool_scale)

    ya, yb, hl, w_pl_f, w_pp_f, w_out_f, w_pg_f, w_pe_f = _branches_fwd(
        z, branch_w, seq, tb_seq, [(w.astype(BF16), axis, True) for w, axis in rest])
    (vec_bag, dx_res, dya, dyb, dzm, mg_bf, do_bf, hn_bf, dgp_bf, dpe_bf, da_bf, dbm_bf, p_bf) = _merge_head(
        x2d, ya, yb, z, p2d, tgt, w_pl_f, w_pp_f, w_out_f, w_pg_f, w_pe_f, ple_norm_g, final_g.reshape(1, D_MODEL),
        tb_seq)
    dz, vec_bag, mat_bag = _branches_bwd(z, hl, dya, dyb, dzm, branch_w, vec_bag, seq, tb_seq)

    tb_dw = min(1024, seq)
    def proj_grad(lhs, rhs, name, cols):
        g32, g16 = _weight_grad(lhs, rhs, 1, tb_dw, name)
        if cols:
            return g32[0], True, g16[0]
        rows = g32.shape[1] // 8
        return g32.reshape(8, rows, g32.shape[2]), False, g16.reshape(8, rows, g32.shape[2])

    p_dim = p2d.shape[1]
    proj_parts = [proj_grad(ya, da_bf, "dw_proj_lru", False), proj_grad(yb, dbm_bf, "dw_proj_pool", True),
                  proj_grad(mg_bf, do_bf, "dw_out", False), proj_grad(hn_bf, dgp_bf, "dw_ple_gate", False),
                  proj_grad(p_bf, dpe_bf, "dw_ple_proj", True)]
    nb_dw = t // tb_dw
    g_in, g_in16, r_pl, r_pp, r_out, r_pg, r_pe, vec_mine, mat_mine = _weight_grad(
        h_bf, dz, N_CHIPS, tb_dw, "dw_in",
        reduce=(proj_parts + [(vec_bag.reshape(8, VEC_BAG_ROWS // 8, D_MODEL), False, None),
                              (mat_bag.reshape(8, MAT_BAG_ROWS // 8, HEAD_DIM), False, None)],
                [BF16] * 5 + [F32] * 2,
                (0, nb_dw // 2, 2 * nb_dw - 1, 3 * nb_dw + nb_dw // 2, N_CHIPS * nb_dw - 1)))
    pieces = (8, D_MODEL // 2, IN_COLS // N_CHIPS)
    nb_seq = t // tb_seq
    dx, g_g1, r_in, vec_sum, mat_sum = _in_proj_bwd(
        dz, w_in_f, x2d, dx_res, norm_g, tb_seq,
        reduce=([(g_in.reshape(pieces), False, g_in16.reshape(pieces))], BF16,
                (0, nb_seq // 8, nb_seq // 2, nb_seq - 1, nb_seq - 1)),
        shards=[(vec_mine.reshape(VEC_BAG_ROWS // N_CHIPS, D_MODEL), 0, True),
                (mat_mine.reshape(MAT_BAG_ROWS // N_CHIPS, HEAD_DIM), 0, True)])

    def big_update(w, g2d, m, v, rows, name):
        d, nm, nv = _adamw(w[0], g2d, m[0], v[0], rows, name)
        return g2d[None], d[None], nm[None], nv[None]

    u_in = big_update(w_in, r_in.reshape(D_MODEL, IN_COLS // N_CHIPS), m_w_in, v_w_in, 256, "adamw_w_in")
    u_pl = big_update(w_proj_lru, r_pl.reshape(D_MODEL // N_CHIPS, D_MODEL), m_w_proj_lru, v_w_proj_lru, 256, "adamw_w_proj_lru")
    u_pp = big_update(w_proj_pool, r_pp.reshape(POOL_WIDTH, D_MODEL // N_CHIPS), m_w_proj_pool, v_w_proj_pool, 512, "adamw_w_proj_pool")
    u_out = big_update(w_out, r_out.reshape(D_MODEL // N_CHIPS, D_MODEL), m_w_out, v_w_out, 256, "adamw_w_out")
    u_pg = big_update(w_ple_gate, r_pg.reshape(D_MODEL // N_CHIPS, D_MODEL), m_w_ple_gate, v_w_ple_gate, 256, "adamw_w_ple_gate")
    u_pe = big_update(w_ple_proj, r_pe.reshape(p_dim, D_MODEL // N_CHIPS), m_w_ple_proj, v_w_ple_proj, 256, "adamw_w_ple_proj")

    small = [("norm_g", norm_g, m_norm_g, v_norm_g), ("conv_b", conv_b, m_conv_b, v_conv_b),
             ("lru_w_a", lru_w_a, m_lru_w_a, v_lru_w_a), ("lru_b_a", lru_b_a, m_lru_b_a, v_lru_b_a),
             ("lru_w_x", lru_w_x, m_lru_w_x, v_lru_w_x), ("lru_b_x", lru_b_x, m_lru_b_x, v_lru_b_x),
             ("lru_lambda", lru_lambda, m_lru_lambda, v_lru_lambda), ("pool_w", pool_w, m_pool_w, v_pool_w),
             ("pool_scale", pool_scale, m_pool_scale, v_pool_scale),
             ("ple_norm_g", ple_norm_g, m_ple_norm_g, v_ple_norm_g), ("final_g", final_g, m_final_g, v_final_g)]

    def view(a):
        return a.reshape(-1, a.shape[-1]) if a.ndim != 3 else a[0]

    cw_at = F32_SUBLANES * VEC_BAG_SLOTS.index("conv_w")
    cw_cols = D_MODEL // N_CHIPS
    g_cw = lax.dynamic_slice(vec_sum, (cw_at, chip * cw_cols), (CONV_WIDTH, cw_cols))
    flat = _adamw_replicated(vec_sum, mat_sum, g_g1, [(name,) + tuple(view(a) for a in arrs) for name, *arrs in small],
                             (conv_w[0], m_conv_w[0], v_conv_w[0], g_cw))
    u_small = {name: tuple(flat[4 * k + pick].reshape(arrs[0].shape) for pick in range(4))
               for k, (name, *arrs) in enumerate(small)}
    u_cw = tuple(a[None] for a in (g_cw,) + tuple(flat[4 * len(small):]))

    loss = vec_sum[F32_SUBLANES * VEC_BAG_SLOTS.index("loss"), 0]
    grad_x = dx.reshape(bsz, seq, D_MODEL)

    def ordered(pick):
        s = {name: u[pick] for name, u in u_small.items()}
        return [s["norm_g"], u_in[pick], u_cw[pick], s["conv_b"], s["lru_w_a"], s["lru_b_a"], s["lru_w_x"], s["lru_b_x"],
                s["lru_lambda"], s["pool_w"], s["pool_scale"], u_pl[pick], u_pp[pick], u_out[pick], s["ple_norm_g"],
                u_pg[pick], u_pe[pick], s["final_g"]]

    return (loss, grad_x, *ordered(0), *ordered(1), *ordered(2), *ordered(3))
```

```python
import jax
import jax.numpy as jnp
from jax import lax
from jax.experimental import pallas as pl
from jax.experimental.pallas import tpu as pltpu

F32 = jnp.float32
BF16 = jnp.bfloat16
MESH = pl.DeviceIdType.MESH

D_MODEL = 1024
LRU_HEADS = 8
HEAD_DIM = 128
CONV_WIDTH = 4
LRU_C = 8.0
POOL_WIDTH = 512
POOL_WINDOWS = (2, 4, 8, 16)
POOL_GROUP_DIM = 128
IN_COLS = 5120
N_CHIPS = 4
EPS = 1e-6

ADAM_LR = 0.001
ADAM_B1 = 0.9
ADAM_B2 = 0.999
ADAM_EPS = 1e-08
ADAM_WD = 0.01
ADAM_STEP = 10

F32_SUBLANES = 8
CONV_HIST = 8
POOL_HIST = 16
VMEM_LIMIT_BYTES = 58 * 1024 * 1024
VEC_BAG_SLOTS = ("norm_g", "conv_w", "conv_b", "lru_b_a", "lru_b_x", "lru_lambda", "pool_scale", "ple_norm_g",
                 "final_g", "loss")
VEC_BAG_ROWS = 128
MAT_BAG_AT = {"lru_w_a": 0, "lru_w_x": LRU_HEADS * HEAD_DIM, "pool_w": 2 * LRU_HEADS * HEAD_DIM}
MAT_BAG_ROWS = 2 * LRU_HEADS * HEAD_DIM + len(POOL_WINDOWS) * POOL_GROUP_DIM


def _bag_row(name, k=0):
    at = F32_SUBLANES * VEC_BAG_SLOTS.index(name) + k
    return slice(at, at + 1)


def _bag_rows(name):
    at = F32_SUBLANES * VEC_BAG_SLOTS.index(name)
    return slice(at, at + F32_SUBLANES)


def _dot(a, b):
    return jnp.dot(a, b, preferred_element_type=F32)


def _dot_nt(a, b):
    return lax.dot_general(a, b, (((1,), (1,)), ((), ())), preferred_element_type=F32)


def _dot_tn(a, b):
    return lax.dot_general(a, b, (((0,), (0,)), ((), ())), preferred_element_type=F32)


def _sigmoid(v):
    return jax.nn.sigmoid(v)


def _softplus(v):
    return jnp.maximum(v, 0.0) + jnp.log1p(jnp.exp(-jnp.abs(v)))


def _place():
    return lax.axis_index("x"), lax.axis_index("y"), lax.axis_index("c")


GATHER_SEMS = 6


def _gather_shapes(shards):
    out_shape = []
    for arr, axis, _ in shards:
        r, cols = arr.shape
        out_shape.append(jax.ShapeDtypeStruct((N_CHIPS * r, cols) if axis == 0 else (r, N_CHIPS * cols), arr.dtype))
    n = len(shards)
    sems = [pltpu.SemaphoreType.DMA((n * GATHER_SEMS,)), pltpu.SemaphoreType.DMA((n * GATHER_SEMS,)),
            pltpu.SemaphoreType.DMA((n,))]
    return out_shape, sems


def _gather_steps(shards, ins, outs, send_sems, recv_sems, local_sems):
    n = len(shards)
    x, y, c = _place()
    me, sibling = (x, y, c), (x, y, 1 - c)
    chips = [(x, 1 - y), (1 - x, y), (1 - x, 1 - y)]

    def region(k, cx, cy, hc):
        (r, cols), axis = shards[k][0].shape, shards[k][1]
        j = 2 * cx + cy
        if axis == 0:
            if hc is None:
                return outs[k].at[pl.ds(j * r, r), :]
            return outs[k].at[pl.ds(j * r + hc * (r // 2), r // 2), :]
        if hc is None:
            return outs[k].at[:, pl.ds(j * cols, cols)]
        return outs[k].at[pl.ds(hc * (r // 2), r // 2), pl.ds(j * cols, cols)]

    def remote(k, sem, block, to, src=None):
        dst = region(k, *block)
        return pltpu.make_async_remote_copy(
            src_ref=dst if src is None else src, dst_ref=dst,
            send_sem=send_sems.at[k * GATHER_SEMS + sem], recv_sem=recv_sems.at[k * GATHER_SEMS + sem],
            device_id=to, device_id_type=MESH)

    def first(k, idx):
        r, split = shards[k][0].shape[0], shards[k][2]
        src = ins[k].at[pl.ds(c * (r // 2), r // 2), :] if split else ins[k]
        return remote(k, idx, (x, y, c if split else None), (*chips[idx], c), src=src)

    def relay(k):
        src_chip = (jnp.bitwise_xor(x, 1 - c), jnp.bitwise_xor(y, c))
        dst_chip = (jnp.bitwise_xor(x, c), jnp.bitwise_xor(y, 1 - c))
        return remote(k, 2, (*src_chip, c), (*dst_chip, c))

    def passed(k, idx):
        return remote(k, 3 + idx, (*chips[idx], c), sibling)

    def mine(k):
        return pltpu.make_async_copy(ins[k], region(k, x, y, None), local_sems.at[k])

    def start():
        for k in range(n):
            mine(k).start()
            for idx in range(2 if shards[k][2] else 3):
                first(k, idx).start()

    def relay_on():
        for k in range(n):
            split = shards[k][2]
            for idx in range(2):
                remote(k, idx, (*chips[idx], c if split else None), me).wait_recv()
            if split:
                relay(k).start()
                passed(k, 0).start()
                passed(k, 1).start()

    def finish():
        for k in range(n):
            split = shards[k][2]
            remote(k, 2, (*chips[2], c if split else None), me).wait_recv()
            if split:
                passed(k, 2).start()
        for k in range(n):
            if shards[k][2]:
                for idx in range(3):
                    remote(k, 3 + idx, (*chips[idx], 1 - c), me).wait_recv()
        for k in range(n):
            if shards[k][2]:
                for cp in (first(k, 0), first(k, 1), relay(k), passed(k, 0), passed(k, 1), passed(k, 2)):
                    cp.wait_send()
            else:
                for idx in range(3):
                    first(k, idx).wait_send()
            mine(k).wait()

    return start, relay_on, finish


RS_ADD_ROWS = (64, 32, 16, 8)


N_DEV = 2 * N_CHIPS


def _all_reduce_scratch(shape):
    return [pltpu.VMEM((N_DEV,) + tuple(shape), F32), pltpu.SemaphoreType.DMA((N_DEV - 1,)),
            pltpu.SemaphoreType.DMA((N_DEV - 1,))]


def _all_reduce_tile(v_ref, o_ref, slots, send_sems, recv_sems):
    flips = [(dx, dy, dc) for dx in (0, 1) for dy in (0, 1) for dc in (0, 1)][1:]
    x, y, c = _place()
    mine = 4 * x + 2 * y + c

    def copy(k, to_flip, slot):
        dx, dy, dc = to_flip
        peer = (jnp.bitwise_xor(x, dx), jnp.bitwise_xor(y, dy), jnp.bitwise_xor(c, dc))
        return pltpu.make_async_remote_copy(
            src_ref=v_ref, dst_ref=slots.at[slot], send_sem=send_sems.at[k], recv_sem=recv_sems.at[k],
            device_id=peer, device_id_type=MESH)

    sends = [copy(k, flip, mine) for k, flip in enumerate(flips)]
    for cp in sends:
        cp.start()
    slots[mine] = v_ref[...]
    for k, (dx, dy, dc) in enumerate(flips):
        copy(k, (dx, dy, dc), jnp.bitwise_xor(mine, 4 * dx + 2 * dy + dc)).wait_recv()
    total = slots[0]
    for d in range(1, N_DEV):
        total = total + slots[d]
    o_ref[...] = total
    for cp in sends:
        cp.wait_send()


RS_SEMS = 8
RS_LOCAL_SEMS = 5


def _rs_piece_shape(part):
    arr, cols = part[0], part[1]
    return (arr.shape[0] // 2, arr.shape[1] // N_CHIPS) if cols else tuple(arr.shape[1:])


def _rs_operands(parts):
    return [p[0] for p in parts] + [p[0] if p[2] is None else p[2] for p in parts]


def _rs_wires(parts, wire):
    return list(wire) if isinstance(wire, (list, tuple)) else [wire] * len(parts)


def _rs_shapes(parts, wire):
    n = len(parts)
    shapes = [_rs_piece_shape(p) for p in parts]
    out_shape = [jax.ShapeDtypeStruct((2,) + s, F32) for s in shapes]
    scratch = []
    for lead, kind in ((N_CHIPS, "f32"), (N_CHIPS, "narrow"), (N_CHIPS, "wire"), (None, "f32"), (N_CHIPS, "wire")):
        for s, p, w in zip(shapes, parts, _rs_wires(parts, wire)):
            dtype = {"f32": F32, "narrow": F32 if p[2] is None else p[2].dtype, "wire": w}[kind]
            scratch.append(pltpu.VMEM(s if lead is None else (lead,) + s, dtype))
    scratch += [pltpu.SemaphoreType.DMA((n * RS_SEMS,)), pltpu.SemaphoreType.DMA((n * RS_SEMS,)),
                pltpu.SemaphoreType.DMA((n * RS_LOCAL_SEMS,))]
    return out_shape, scratch


def _rs_steps(parts, ins, outs, scratch):
    n = len(parts)
    own, sib, got, fin, snd = (scratch[k * n:(k + 1) * n] for k in range(5))
    send_sems, recv_sems, local_sems = scratch[5 * n:]
    shapes = [_rs_piece_shape(p) for p in parts]
    x, y, c = _place()
    j_me = 2 * x + y
    me, sibling = (x, y, c), (x, y, 1 - c)

    def piece(a, jj, core, narrow=False):
        ref = ins[n + a] if narrow else ins[a]
        if parts[a][1]:
            r, cl = shapes[a]
            return ref.at[pl.ds(core * r, r), pl.ds(jj * cl, cl)]
        return ref.at[2 * jj + core]

    def remote(a, sem, src, dst, to):
        return pltpu.make_async_remote_copy(
            src_ref=src, dst_ref=dst, send_sem=send_sems.at[a * RS_SEMS + sem],
            recv_sem=recv_sems.at[a * RS_SEMS + sem], device_id=to, device_id_type=MESH)

    def rows_loop(a, fn):
        r = shapes[a][0]
        step = max(s for s in RS_ADD_ROWS if r % s == 0)

        def it(i, carry):
            fn(pl.ds(pl.multiple_of(i * step, step), step))
            return carry

        lax.fori_loop(0, r // step, it, 0)

    def load(a, jj):
        return pltpu.make_async_copy(piece(a, jj, c), own[a].at[jj], local_sems.at[a * RS_LOCAL_SEMS + jj])

    def to_sibling(a, jj):
        return remote(a, jj, piece(a, jj, 1 - c, narrow=True), sib[a].at[jj], sibling)

    near = (jnp.bitwise_xor(x, 1 - c), jnp.bitwise_xor(y, c))
    far = (jnp.bitwise_xor(x, c), jnp.bitwise_xor(y, 1 - c))
    diag = (1 - x, 1 - y)
    FROM_NEAR, FROM_FAR, FEED = 0, 1, 2

    def chip_of(chip):
        return 2 * chip[0] + chip[1]

    def feed(a):
        return remote(a, 4, snd[a].at[chip_of(diag)], got[a].at[FEED], (*near, c))

    def to_near(a):
        return remote(a, 5, snd[a].at[chip_of(near)], got[a].at[FROM_NEAR], (*near, c))

    def to_far(a):
        return remote(a, 6, snd[a].at[chip_of(far)], got[a].at[FROM_FAR], (*far, c))

    def store(a):
        return pltpu.make_async_copy(fin[a], outs[a].at[c], local_sems.at[a * RS_LOCAL_SEMS + 4])

    def result_to_sibling(a):
        return remote(a, 7, fin[a], outs[a].at[c], sibling)

    def exchange():
        for a in range(n):
            for jj in range(N_CHIPS):
                load(a, jj).start()
                to_sibling(a, jj).start()

    def chip_sums():
        for a in range(n):
            for jj in range(N_CHIPS):
                load(a, jj).wait()
                remote(a, jj, sib[a].at[jj], sib[a].at[jj], me).wait_recv()

                def add(sl, a=a, jj=jj):
                    q = own[a][jj, sl, :] + sib[a][jj, sl, :].astype(F32)
                    own[a][jj, sl, :] = q
                    snd[a][jj, sl, :] = q.astype(snd[a].dtype)

                rows_loop(a, add)
        for a in range(n):
            feed(a).start()
        for a in range(n):
            to_near(a).start()

    def relay():
        for a in range(n):
            remote(a, 4, got[a].at[FEED], got[a].at[FEED], me).wait_recv()

            def add(sl, a=a):
                pair = own[a][chip_of(far), sl, :] + got[a][FEED, sl, :].astype(F32)
                snd[a][chip_of(far), sl, :] = pair.astype(snd[a].dtype)

            rows_loop(a, add)
            to_far(a).start()

    def totals():
        for a in range(n):
            remote(a, 5, got[a].at[FROM_NEAR], got[a].at[FROM_NEAR], me).wait_recv()
            remote(a, 6, got[a].at[FROM_FAR], got[a].at[FROM_FAR], me).wait_recv()

            def total(sl, a=a):
                fin[a][sl, :] = (own[a][j_me, sl, :] + got[a][FROM_NEAR, sl, :].astype(F32)) + (
                    got[a][FROM_FAR, sl, :].astype(F32))

            rows_loop(a, total)
            store(a).start()
            result_to_sibling(a).start()

    def finish():
        for a in range(n):
            remote(a, 7, outs[a].at[1 - c], outs[a].at[1 - c], me).wait_recv()
        for a in range(n):
            for jj in range(N_CHIPS):
                to_sibling(a, jj).wait_send()
            for cp in (feed(a), to_near(a), to_far(a), result_to_sibling(a)):
                cp.wait_send()
            store(a).wait()

    return exchange, chip_sums, relay, totals, finish


def _rms(x):
    r = lax.rsqrt(jnp.mean(x * x, axis=-1, keepdims=True) + EPS)
    return x * r, r


def _rms_bwd(dxn, xn, r):
    return r * (dxn - xn * jnp.mean(dxn * xn, axis=-1, keepdims=True))


def _in_proj_gather(x2d, norm_g, w_in_sh, shards, tb):
    t = x2d.shape[0]
    nb = t // tb
    cols = IN_COLS // N_CHIPS
    half = D_MODEL // 2
    n = len(shards)

    def body(x_ref, g_ref, win_ref, *refs):
        ins = refs[:n]
        z_ref, h_ref, wfull_ref = refs[n:n + 3]
        outs = refs[n + 3:2 * n + 3]
        wv, h_all, send_sems, recv_sems, local_sems, w_send, w_recv, w_local = refs[2 * n + 3:]
        s, i = pl.program_id(0), pl.program_id(1)
        x, y, c = _place()
        me, sibling = (x, y, c), (x, y, 1 - c)
        chips = [(x, 1 - y), (1 - x, y), (1 - x, 1 - y)]

        def w_half(cx, cy, hc):
            return wv.at[2 * cx + cy, pl.ds(hc * half, half), :]

        def w_remote(sem, block, to, src=None):
            dst = w_half(*block)
            return pltpu.make_async_remote_copy(
                src_ref=dst if src is None else src, dst_ref=dst, send_sem=w_send.at[sem],
                recv_sem=w_recv.at[sem], device_id=to, device_id_type=MESH)

        def w_first(idx):
            return w_remote(idx, (x, y, c), (*chips[idx], c), src=win_ref.at[pl.ds(c * half, half), :])

        def w_relay():
            src_chip = (jnp.bitwise_xor(x, 1 - c), jnp.bitwise_xor(y, c))
            dst_chip = (jnp.bitwise_xor(x, c), jnp.bitwise_xor(y, 1 - c))
            return w_remote(2, (*src_chip, c), (*dst_chip, c))

        def w_pass(idx):
            return w_remote(3 + idx, (*chips[idx], c), sibling)

        def w_store(k, cx, cy):
            jj = 2 * cx + cy
            return pltpu.make_async_copy(wv.at[jj], wfull_ref.at[:, pl.ds(jj * cols, cols)], w_local.at[k])

        start_rest, relay_rest, finish_rest = _gather_steps(shards, ins, outs, send_sems, recv_sems, local_sems)
        own = pltpu.make_async_copy(win_ref, wv.at[2 * x + y], w_local.at[4])

        @pl.when((s == 0) & (i == 0))
        def _():
            own.start()
            w_first(0).start()
            w_first(1).start()
            start_rest()
            own.wait()
            w_store(0, x, y).start()

        @pl.when((s == 1) & (i == 0))
        def _():
            w_remote(0, (*chips[0], c), me).wait_recv()
            w_remote(1, (*chips[1], c), me).wait_recv()
            w_relay().start()
            w_pass(0).start()
            w_pass(1).start()
            w_remote(3, (*chips[0], 1 - c), me).wait_recv()
            w_store(1, *chips[0]).start()

        @pl.when((s == 2) & (i == 0))
        def _():
            w_remote(4, (*chips[1], 1 - c), me).wait_recv()
            w_store(2, *chips[1]).start()

        @pl.when((s == 3) & (i == 0))
        def _():
            w_remote(2, (*chips[2], c), me).wait_recv()
            w_pass(2).start()
            w_remote(5, (*chips[2], 1 - c), me).wait_recv()
            w_store(3, *chips[2]).start()

        keep_h = pltpu.make_async_copy(h_all.at[i], h_ref.at[pl.ds(pl.multiple_of(i * tb, tb), tb), :], w_local.at[5])

        @pl.when(s == 0)
        def _():
            xn, _ = _rms(x_ref[...])
            h_all[i] = (xn * g_ref[...]).astype(BF16)
            keep_h.start()

        z_ref[...] = _dot(h_all[i], wv[jnp.bitwise_xor(2 * x + y, s)])
        pl.when(s == 0)(keep_h.wait)

        @pl.when((s == N_CHIPS - 1) & (i == nb - 1))
        def _():
            relay_rest()
            finish_rest()
            for cp in (w_first(0), w_first(1), w_relay(), w_pass(0), w_pass(1), w_pass(2)):
                cp.wait_send()
            w_store(0, x, y).wait()
            for idx in range(3):
                w_store(idx + 1, *chips[idx]).wait()

    rest_shape, rest_sems = _gather_shapes(shards)
    out_shape = [jax.ShapeDtypeStruct((t, IN_COLS), F32), jax.ShapeDtypeStruct((t, D_MODEL), BF16),
                 jax.ShapeDtypeStruct((D_MODEL, IN_COLS), BF16)] + rest_shape
    any_spec = pl.BlockSpec(memory_space=pl.ANY)

    def z_map(s, i):
        return (i, jnp.bitwise_xor(2 * lax.axis_index("x") + lax.axis_index("y"), s))

    return pl.pallas_call(
        body, name="in_proj", out_shape=tuple(out_shape),
        grid=(N_CHIPS, nb),
        in_specs=[pl.BlockSpec((tb, D_MODEL), lambda s, i: (jnp.where(s == 0, i, nb - 1), 0)),
                  pl.BlockSpec((1, D_MODEL), lambda s, i: (0, 0)), any_spec] + [any_spec] * n,
        out_specs=tuple([pl.BlockSpec((tb, cols), z_map), any_spec, any_spec] + [any_spec] * n),
        scratch_shapes=[pltpu.VMEM((N_CHIPS, D_MODEL, cols), BF16), pltpu.VMEM((nb, tb, D_MODEL), BF16)] + rest_sems + [
            pltpu.SemaphoreType.DMA((GATHER_SEMS,)), pltpu.SemaphoreType.DMA((GATHER_SEMS,)),
            pltpu.SemaphoreType.DMA((N_CHIPS + 2,))],
        compiler_params=pltpu.CompilerParams(dimension_semantics=("arbitrary", "arbitrary"),
                                             vmem_limit_bytes=VMEM_LIMIT_BYTES),
    )(x2d, norm_g, w_in_sh, *[sh[0] for sh in shards])


def _in_proj_bwd(dz, w_in, x2d, dx_res, norm_g, tb, reduce, shards):
    t = x2d.shape[0]
    nb = t // tb
    parts, wire, steps = reduce
    n = len(parts)
    k = len(shards)

    def body(dz_ref, w_ref, x_ref, dres_ref, g_ref, *refs):
        at = 2 * n + k
        dx_ref, dg_ref = refs[at:at + 2]
        rs_outs, g_outs = refs[at + 2:at + 2 + n], refs[at + 2 + n:at + 2 + n + k]
        scratch = refs[at + 2 + n + k:]
        rs_scr, g_sems, dg_acc, ar_scr = scratch[:-7], scratch[-7:-4], scratch[-4], scratch[-3:]
        rs = _rs_steps(parts, refs[:2 * n], rs_outs, rs_scr)
        for step, when in zip(rs, steps):
            pl.when(pl.program_id(0) == when)(step)
        gather = _gather_steps(shards, refs[2 * n:at], g_outs, *g_sems)
        for step, when in zip(gather, (0, nb // 2, nb - 1)):
            pl.when(pl.program_id(0) == when)(step)

        @pl.when(pl.program_id(0) == 0)
        def _():
            dg_acc[...] = jnp.zeros_like(dg_acc)

        xn, r = _rms(x_ref[...])
        g = g_ref[...]
        dh = _dot_nt(dz_ref[...], w_ref[...])
        dg_acc[0:1, :] += jnp.sum(dh * xn, axis=0, keepdims=True)
        dx_ref[...] = dres_ref[...] + _rms_bwd(dh * g, xn, r)

        @pl.when(pl.program_id(0) == nb - 1)
        def _():
            _all_reduce_tile(dg_acc, dg_ref, *ar_scr)

    row = lambda i: (i, 0)
    fixed = lambda i: (0, 0)
    rs_shape, rs_scratch = _rs_shapes(parts, wire)
    g_shape, g_sems = _gather_shapes(shards)
    any_spec = pl.BlockSpec(memory_space=pl.ANY)
    return pl.pallas_call(
        body, name="in_proj_bwd",
        out_shape=tuple([jax.ShapeDtypeStruct((t, D_MODEL), F32), jax.ShapeDtypeStruct((F32_SUBLANES, D_MODEL), F32)]
                        + rs_shape + g_shape),
        grid=(nb,),
        in_specs=[pl.BlockSpec((tb, IN_COLS), row),
                  pl.BlockSpec((D_MODEL, IN_COLS), fixed, pipeline_mode=pl.Buffered(1)),
                  pl.BlockSpec((tb, D_MODEL), row), pl.BlockSpec((tb, D_MODEL), row),
                  pl.BlockSpec((1, D_MODEL), fixed)] + [any_spec] * (2 * n + k),
        out_specs=tuple([pl.BlockSpec((tb, D_MODEL), row), pl.BlockSpec((F32_SUBLANES, D_MODEL), fixed)]
                        + [any_spec] * (n + k)),
        scratch_shapes=rs_scratch + g_sems + [pltpu.VMEM((F32_SUBLANES, D_MODEL), F32)] + _all_reduce_scratch(
            (F32_SUBLANES, D_MODEL)),
        compiler_params=pltpu.CompilerParams(dimension_semantics=("arbitrary",),
                                             vmem_limit_bytes=VMEM_LIMIT_BYTES),
    )(dz, w_in, x2d, dx_res, norm_g, *_rs_operands(parts), *[sh[0] for sh in shards])


def _weight_grad(lhs, rhs, n_chunks, tb, name, reduce=None):
    t, k = lhs.shape
    nc = rhs.shape[1] // n_chunks
    nb = t // tb
    parts, wire, steps = reduce if reduce is not None else ([], F32, ())
    n = len(parts)

    def body(l_ref, r_ref, *refs):
        o_ref, o16_ref = refs[2 * n:2 * n + 2]
        if n:
            at = pl.program_id(0) * nb + pl.program_id(1)
            rs = _rs_steps(parts, refs[:2 * n], refs[2 * n + 2:3 * n + 2], refs[3 * n + 2:])
            for step, when in zip(rs, steps):
                pl.when(at == when)(step)

        @pl.when(pl.program_id(1) == 0)
        def _():
            o_ref[...] = jnp.zeros_like(o_ref)

        o_ref[...] += _dot_tn(l_ref[...], r_ref[...])

        @pl.when(pl.program_id(1) == nb - 1)
        def _():
            o16_ref[...] = o_ref[...].astype(BF16)

    rs_shape, rs_scratch = _rs_shapes(parts, wire) if n else ([], [])
    any_spec = pl.BlockSpec(memory_space=pl.ANY)
    chunk = pl.BlockSpec((None, k, nc), lambda j, i: (j, 0, 0))
    return pl.pallas_call(
        body, name=name,
        out_shape=tuple([jax.ShapeDtypeStruct((n_chunks, k, nc), F32), jax.ShapeDtypeStruct((n_chunks, k, nc), BF16)]
                        + rs_shape),
        grid=(n_chunks, nb),
        in_specs=[pl.BlockSpec((tb, k), lambda j, i: (i, 0)), pl.BlockSpec((tb, nc), lambda j, i: (i, j))]
        + [any_spec] * (2 * n),
        out_specs=tuple([chunk, chunk] + [any_spec] * n),
        scratch_shapes=rs_scratch,
        compiler_params=pltpu.CompilerParams(dimension_semantics=("arbitrary", "arbitrary"),
                                             vmem_limit_bytes=VMEM_LIMIT_BYTES),
    )(lhs, rhs, *_rs_operands(parts))


def _adam_update(w, g, m, v):
    m_ = ADAM_B1 * m + (1.0 - ADAM_B1) * g
    v_ = ADAM_B2 * v + (1.0 - ADAM_B2) * jnp.square(g)
    m_hat = m_ / (1.0 - ADAM_B1 ** ADAM_STEP)
    v_hat = v_ / (1.0 - ADAM_B2 ** ADAM_STEP)
    return -ADAM_LR * (m_hat / (jnp.sqrt(v_hat) + ADAM_EPS) + ADAM_WD * w), m_, v_


def _adamw_replicated(vec_sum, mat_sum, norm_grad, entries, conv):
    n = len(entries)

    def grad_of(name, shape, vec_ref, mat_ref, norm_ref):
        if name == "norm_g":
            return norm_ref[0:1, :]
        if name in MAT_BAG_AT:
            return mat_ref[MAT_BAG_AT[name]:MAT_BAG_AT[name] + shape[0], :]
        if shape[0] == 1:
            return vec_ref[_bag_row(name), 0:shape[1]]
        return jnp.concatenate([vec_ref[_bag_row(name), h * shape[1]:(h + 1) * shape[1]] for h in range(shape[0])],
                               axis=0)

    def body(vec_ref, mat_ref, norm_ref, *refs):
        ins, outs = refs[:3 * n + 4], refs[3 * n + 4:]
        for k in range(n):
            w_ref, m_ref, v_ref = ins[3 * k:3 * k + 3]
            g = grad_of(entries[k][0], w_ref.shape, vec_ref, mat_ref, norm_ref)
            d, m_, v_ = _adam_update(w_ref[...], g, m_ref[...], v_ref[...])
            for ref, val in zip(outs[4 * k:4 * k + 4], (g, d, m_, v_)):
                ref[...] = val
        w_ref, m_ref, v_ref, g_ref = ins[3 * n:]
        g = g_ref[...]
        for ref, val in zip(outs[4 * n:4 * n + 4], (g,) + _adam_update(w_ref[...], g, m_ref[...], v_ref[...])):
            ref[...] = val
        outs[4 * n + 4][...] = vec_ref[_bag_row("loss"), 0:1]

    arrays = [a for e in entries for a in e[1:]] + list(conv)
    out_shape = [jax.ShapeDtypeStruct(e[1].shape, F32) for e in entries for _ in range(4)]
    out_shape += [jax.ShapeDtypeStruct(conv[0].shape, F32)] * 4 + [jax.ShapeDtypeStruct((1, 1), F32)]
    return pl.pallas_call(
        body, name="adamw_replicated", out_shape=tuple(out_shape),
        compiler_params=pltpu.CompilerParams(vmem_limit_bytes=VMEM_LIMIT_BYTES),
    )(vec_sum, mat_sum, norm_grad, *arrays)


def _adamw(w, g, m, v, rows, name):
    r, c = w.shape

    def body(w_ref, g_ref, m_ref, v_ref, go_ref, d_ref, nm_ref, nv_ref):
        g = g_ref[...]
        go_ref[...] = g
        d_ref[...], nm_ref[...], nv_ref[...] = _adam_update(w_ref[...], g, m_ref[...], v_ref[...])

    spec = pl.BlockSpec((rows, c), lambda i: (i, 0))
    return pl.pallas_call(
        body, name=name, out_shape=tuple(jax.ShapeDtypeStruct((r, c), F32) for _ in range(4)),
        grid=(r // rows,), in_specs=[spec] * 4, out_specs=(spec,) * 4,
        compiler_params=pltpu.CompilerParams(dimension_semantics=("arbitrary",),
                                             vmem_limit_bytes=VMEM_LIMIT_BYTES),
    )(w, g, m, v)


def _shift_down(ext, s):
    return pltpu.roll(ext, s, 0)


def _tile_shift(v, s):
    rows, cols = v.shape
    tiles = v.reshape(rows // F32_SUBLANES, F32_SUBLANES, cols)
    return pltpu.roll(tiles, s % F32_SUBLANES, 1).reshape(rows, cols)


def _shift_up(ext, s):
    return pltpu.roll(ext, ext.shape[0] - s, 0)


def _lru_gates(xc, wa_ref, ba, wx_ref, bx, lam):
    pa, px = [], []
    for h in range(LRU_HEADS):
        xh = xc[:, h * HEAD_DIM:(h + 1) * HEAD_DIM].astype(BF16)
        pa.append(_dot(xh, wa_ref[h]))
        px.append(_dot(xh, wx_ref[h]))
    r = _sigmoid(jnp.concatenate(pa, axis=1) + ba)
    ig = _sigmoid(jnp.concatenate(px, axis=1) + bx)
    sp = _softplus(-lam)
    log_a = (-LRU_C * r) * sp
    a = jnp.exp(log_a)
    mult = jnp.sqrt(jnp.tanh(-log_a) * (1.0 + a * a))
    return r, ig, a, mult, sp


def _conv(ext, w_ref, b):
    y = b + _shift_down(ext, 3) * w_ref[0:1, :]
    y = y + _shift_down(ext, 2) * w_ref[1:2, :]
    y = y + _shift_down(ext, 1) * w_ref[2:3, :]
    y = y + ext * w_ref[3:4, :]
    return y[CONV_HIST:, :]


def _pool_diff(ext, pos):
    out = []
    for g, k in enumerate(POOL_WINDOWS):
        col = ext[:, g * POOL_GROUP_DIM:(g + 1) * POOL_GROUP_DIM]
        s = col
        for step in range(g + 1):
            s = s + _shift_down(s, 2 ** step)
        count = jnp.minimum(pos + 1, k).astype(F32)
        out.append(s[POOL_HIST:, :] / count - col[POOL_HIST:, :])
    return out


def _pool_mix(diff, pw_ref):
    return jnp.concatenate([_dot(diff[g].astype(BF16), pw_ref[g]) for g in range(len(POOL_WINDOWS))], axis=1)


def _branch_specs(tb, row_map, fixed):
    fixed3 = lambda i: (0, 0, 0)
    return [pl.BlockSpec((CONV_WIDTH, D_MODEL), fixed), pl.BlockSpec((1, D_MODEL), fixed),
            pl.BlockSpec((LRU_HEADS, HEAD_DIM, HEAD_DIM), fixed3), pl.BlockSpec((1, D_MODEL), fixed),
            pl.BlockSpec((LRU_HEADS, HEAD_DIM, HEAD_DIM), fixed3), pl.BlockSpec((1, D_MODEL), fixed),
            pl.BlockSpec((1, D_MODEL), fixed),
            pl.BlockSpec((len(POOL_WINDOWS), POOL_GROUP_DIM, POOL_GROUP_DIM), fixed3),
            pl.BlockSpec((1, POOL_WIDTH), fixed)]


def _branches_fwd(z, weights, seq, tb, shards):
    t = z.shape[0]
    nb = t // tb
    nbe = seq // tb
    groups = tb // F32_SUBLANES
    n = len(shards)

    def body(xa_ref, ga_ref, xb_ref, gb_ref, cw_ref, cb_ref, wa_ref, ba_ref, wx_ref, bx_ref, lam_ref,
             pw_ref, ps_ref, *refs):
        g_ins = refs[:n]
        ya_ref, yb_ref, hl_ref = refs[n:n + 3]
        g_outs = refs[n + 3:2 * n + 3]
        xa_ext, xb_ext, carry, a_s, u_s, send_sems, recv_sems, local_sems = refs[2 * n + 3:]
        blk = pl.program_id(0) % nbe
        start_gather, relay_gather, finish_gather = _gather_steps(shards, g_ins, g_outs, send_sems, recv_sems,
                                                                  local_sems)
        pl.when(pl.program_id(0) == 0)(start_gather)
        pl.when(pl.program_id(0) == nb // 2)(relay_gather)

        @pl.when(blk == 0)
        def _():
            xa_ext[0:CONV_HIST, :] = jnp.zeros((CONV_HIST, D_MODEL), F32)
            xb_ext[0:POOL_HIST, :] = jnp.zeros((POOL_HIST, POOL_WIDTH), F32)
            carry[...] = jnp.zeros_like(carry)

        xa_ext[CONV_HIST:, :] = xa_ref[...]
        xb_ext[POOL_HIST:, :] = xb_ref[...]
        ea = xa_ext[...]
        eb = xb_ext[...]
        xa_ext[0:CONV_HIST, :] = ea[tb:, :]
        xb_ext[0:POOL_HIST, :] = eb[tb:, :]

        xc = _conv(ea, cw_ref, cb_ref[...])
        _, ig, a, mult, _ = _lru_gates(xc, wa_ref, ba_ref[...], wx_ref, bx_ref[...], lam_ref[...])
        u = mult * (ig * xc)
        row8 = lax.broadcasted_iota(jnp.int32, (tb, D_MODEL), 0) % F32_SUBLANES
        for s in (1, 2, 4):
            m = row8 >= s
            u = jnp.where(m, a * _tile_shift(u, s) + u, u)
            a = jnp.where(m, a * _tile_shift(a, s), a)
        a_s[...] = a
        u_s[...] = u

        def step(g, cr):
            sl = pl.ds(pl.multiple_of(g * F32_SUBLANES, F32_SUBLANES), F32_SUBLANES)
            hb = a_s[sl, :] * cr + u_s[sl, :]
            hl_ref[sl, :] = hb
            return jnp.broadcast_to(hb[F32_SUBLANES - 1:F32_SUBLANES, :], (F32_SUBLANES, D_MODEL))

        carry[...] = lax.fori_loop(0, groups, step, carry[...], unroll=4)
        ga = ga_ref[...]
        ya_ref[...] = (hl_ref[...] * (ga * _sigmoid(ga))).astype(BF16)

        pos = blk * tb + lax.broadcasted_iota(jnp.int32, (tb, POOL_GROUP_DIM), 0)
        ypre = _pool_mix(_pool_diff(eb, pos), pw_ref)
        gb = gb_ref[...]
        yb_ref[...] = ((ypre * ps_ref[...]) * (gb * _sigmoid(gb))).astype(BF16)
        pl.when(pl.program_id(0) == nb - 1)(finish_gather)

    row = lambda i: (i, 0)
    fixed = lambda i: (0, 0)
    any_spec = pl.BlockSpec(memory_space=pl.ANY)
    in_specs = [pl.BlockSpec((tb, D_MODEL), lambda i: (i, 0)), pl.BlockSpec((tb, D_MODEL), lambda i: (i, 1)),
                pl.BlockSpec((tb, POOL_WIDTH), lambda i: (i, 4)), pl.BlockSpec((tb, POOL_WIDTH), lambda i: (i, 5)),
                ] + _branch_specs(tb, row, fixed) + [any_spec] * n
    g_shape, g_sems = _gather_shapes(shards)
    return pl.pallas_call(
        body, name="branches_fwd",
        out_shape=tuple([jax.ShapeDtypeStruct((t, D_MODEL), BF16), jax.ShapeDtypeStruct((t, POOL_WIDTH), BF16),
                         jax.ShapeDtypeStruct((t, D_MODEL), F32)] + g_shape),
        grid=(nb,), in_specs=in_specs,
        out_specs=tuple([pl.BlockSpec((tb, D_MODEL), row), pl.BlockSpec((tb, POOL_WIDTH), row),
                         pl.BlockSpec((tb, D_MODEL), row)] + [any_spec] * n),
        scratch_shapes=[pltpu.VMEM((tb + CONV_HIST, D_MODEL), F32), pltpu.VMEM((tb + POOL_HIST, POOL_WIDTH), F32),
                        pltpu.VMEM((F32_SUBLANES, D_MODEL), F32),
                        pltpu.VMEM((tb, D_MODEL), F32), pltpu.VMEM((tb, D_MODEL), F32)] + g_sems,
        compiler_params=pltpu.CompilerParams(dimension_semantics=("arbitrary",),
                                             vmem_limit_bytes=VMEM_LIMIT_BYTES),
    )(z, z, z, z, *weights, *[sh[0] for sh in shards])


def _branches_bwd(z, hl, dya, dyb, dzm, weights, vec_bag, seq, tb):
    t = z.shape[0]
    nb = t // tb
    nbe = seq // tb
    groups = tb // F32_SUBLANES

    def body(xa_ref, xap_ref, ga_ref, xb_ref, xbp_ref, gb_ref, hl_ref, hlp_ref, dya_ref, dyb_ref, dzm_ref,
             cw_ref, cb_ref, wa_ref, ba_ref, wx_ref, bx_ref, lam_ref, pw_ref, ps_ref, vec_in_ref,
             dz_ref, vec_ref, mat_ref,
             xa_ext, xb_ext, hl_ext, a_ext, dxc_ext, dwin_ext, g_carry, b_s, d_s, g_s):
        i = pl.program_id(0)
        blk = (nb - 1 - i) % nbe

        def mat_rows(name, k):
            at = MAT_BAG_AT[name] + k * HEAD_DIM
            return slice(at, at + HEAD_DIM)

        @pl.when(i == 0)
        def _():
            vec_ref[...] = vec_in_ref[...]
            mat_ref[...] = jnp.zeros_like(mat_ref)

        @pl.when(blk == nbe - 1)
        def _():
            a_ext[tb:, :] = jnp.zeros((F32_SUBLANES, D_MODEL), F32)
            dxc_ext[tb:, :] = jnp.zeros((CONV_HIST, D_MODEL), F32)
            dwin_ext[tb:, :] = jnp.zeros((POOL_HIST, POOL_WIDTH), F32)
            g_carry[...] = jnp.zeros_like(g_carry)

        live = (blk > 0).astype(F32)
        xa_ext[0:CONV_HIST, :] = xap_ref[...] * live
        xa_ext[CONV_HIST:, :] = xa_ref[...]
        xb_ext[0:POOL_HIST, :] = xbp_ref[...] * live
        xb_ext[POOL_HIST:, :] = xb_ref[...]
        hl_ext[0:F32_SUBLANES, :] = hlp_ref[...] * live
        hl_ext[F32_SUBLANES:, :] = hl_ref[...]
        ea = xa_ext[...]
        eb = xb_ext[...]

        xc = _conv(ea, cw_ref, cb_ref[...])
        lam = lam_ref[...]
        r, ig, a, mult, sp = _lru_gates(xc, wa_ref, ba_ref[...], wx_ref, bx_ref[...], lam)
        hl = hl_ref[...]
        ga = ga_ref[...]
        sga = _sigmoid(ga)
        dya = dya_ref[...]
        dhl = dya * (ga * sga)
        dz_ref[:, D_MODEL:2 * D_MODEL] = (dya * hl * (sga * (1.0 + ga * (1.0 - sga)))).astype(BF16)

        a_ext[0:tb, :] = a
        b = _shift_up(a_ext[...], 1)[0:tb, :]
        a_ext[tb:, :] = jnp.broadcast_to(a[0:1, :], (F32_SUBLANES, D_MODEL))
        d = dhl
        row8 = lax.broadcasted_iota(jnp.int32, (tb, D_MODEL), 0) % F32_SUBLANES
        for s in (1, 2, 4):
            m = row8 < F32_SUBLANES - s
            d = jnp.where(m, d + b * _tile_shift(d, -s), d)
            b = jnp.where(m, b * _tile_shift(b, -s), b)
        b_s[...] = b
        d_s[...] = d

        def step(k, cr):
            sl = pl.ds(pl.multiple_of((groups - 1 - k) * F32_SUBLANES, F32_SUBLANES), F32_SUBLANES)
            gb_ = d_s[sl, :] + b_s[sl, :] * cr
            g_s[sl, :] = gb_
            return jnp.broadcast_to(gb_[0:1, :], (F32_SUBLANES, D_MODEL))

        g_carry[...] = lax.fori_loop(0, groups, step, g_carry[...], unroll=4)
        gsc = g_s[...]
        da = gsc * _shift_down(hl_ext[...], 1)[F32_SUBLANES:, :]
        dmult = gsc * (ig * xc)
        dig = gsc * (mult * xc)
        dxc = gsc * (mult * ig)
        dlog_a = da * a - (a * a) * dmult / mult
        dr = dlog_a * (-LRU_C * sp)
        vec_ref[_bag_row("lru_lambda"), :] += jnp.sum(dlog_a * (-LRU_C * r), axis=0, keepdims=True)
        dpa = dr * (r * (1.0 - r))
        dpx = dig * (ig * (1.0 - ig))
        vec_ref[_bag_row("lru_b_a"), :] += jnp.sum(dpa, axis=0, keepdims=True)
        vec_ref[_bag_row("lru_b_x"), :] += jnp.sum(dpx, axis=0, keepdims=True)
        back = []
        for h in range(LRU_HEADS):
            cols = slice(h * HEAD_DIM, (h + 1) * HEAD_DIM)
            xh = xc[:, cols].astype(BF16)
            dpa_h = dpa[:, cols].astype(BF16)
            dpx_h = dpx[:, cols].astype(BF16)
            mat_ref[mat_rows("lru_w_a", h), :] += _dot_tn(xh, dpa_h)
            mat_ref[mat_rows("lru_w_x", h), :] += _dot_tn(xh, dpx_h)
            back.append(_dot_nt(dpa_h, wa_ref[h]) + _dot_nt(dpx_h, wx_ref[h]))
        dxc = dxc + jnp.concatenate(back, axis=1)
        vec_ref[_bag_row("conv_b"), :] += jnp.sum(dxc, axis=0, keepdims=True)
        for k in range(CONV_WIDTH):
            tap = _shift_down(ea, CONV_WIDTH - 1 - k)[CONV_HIST:, :] if k < CONV_WIDTH - 1 else ea[CONV_HIST:, :]
            vec_ref[_bag_row("conv_w", k), :] += jnp.sum(dxc * tap, axis=0, keepdims=True)
        dxc_ext[0:tb, :] = dxc
        ed = dxc_ext[...]
        dxa = ed * cw_ref[3:4, :]
        dxa = dxa + _shift_up(ed, 1) * cw_ref[2:3, :]
        dxa = dxa + _shift_up(ed, 2) * cw_ref[1:2, :]
        dxa = dxa + _shift_up(ed, 3) * cw_ref[0:1, :]
        dz_ref[:, 0:D_MODEL] = dxa[0:tb, :].astype(BF16)
        dxc_ext[tb:, :] = dxc[0:CONV_HIST, :]

        pos = blk * tb + lax.broadcasted_iota(jnp.int32, (tb, POOL_GROUP_DIM), 0)
        diff = _pool_diff(eb, pos)
        ypre = _pool_mix(diff, pw_ref)
        ps = ps_ref[...]
        gb = gb_ref[...]
        sgb = _sigmoid(gb)
        dyb = dyb_ref[...]
        dyp = dyb * (gb * sgb)
        dz_ref[:, 2 * D_MODEL + POOL_WIDTH:3 * D_MODEL] = (
            dyb * (ypre * ps) * (sgb * (1.0 + gb * (1.0 - sgb)))).astype(BF16)
        vec_ref[_bag_row("pool_scale"), 0:POOL_WIDTH] += jnp.sum(dyp * ypre, axis=0, keepdims=True)
        dypre = dyp * ps
        for g, k in enumerate(POOL_WINDOWS):
            cols = slice(g * POOL_GROUP_DIM, (g + 1) * POOL_GROUP_DIM)
            dyg = dypre[:, cols].astype(BF16)
            mat_ref[mat_rows("pool_w", g), :] += _dot_tn(diff[g].astype(BF16), dyg)
            ddiff = _dot_nt(dyg, pw_ref[g])
            count = jnp.minimum(pos + 1, k).astype(F32)
            dwin = ddiff / count
            dwin_ext[0:tb, cols] = dwin
            s = dwin_ext[:, cols]
            for step_ in range(g + 1):
                s = s + _shift_up(s, 2 ** step_)
            dz_ref[:, 2 * D_MODEL + g * POOL_GROUP_DIM:2 * D_MODEL + (g + 1) * POOL_GROUP_DIM] = (
                s[0:tb, :] - ddiff).astype(BF16)
            dwin_ext[tb:, cols] = dwin[0:POOL_HIST, :]

        dz_ref[:, 3 * D_MODEL:] = dzm_ref[...]

        @pl.when(i == nb - 1)
        def _():
            row = _bag_row("lru_lambda")
            vec_ref[row, :] = vec_ref[row, :] * (-_sigmoid(-lam))

    rev = lambda i: (nb - 1 - i, 0)
    fixed = lambda i: (0, 0)

    def prev(rows, col):
        per = tb // rows
        return lambda i: (jnp.maximum((nb - 1 - i) * per - 1, 0), col)

    in_specs = [pl.BlockSpec((tb, D_MODEL), lambda i: (nb - 1 - i, 0)),
                pl.BlockSpec((CONV_HIST, D_MODEL), prev(CONV_HIST, 0)),
                pl.BlockSpec((tb, D_MODEL), lambda i: (nb - 1 - i, 1)),
                pl.BlockSpec((tb, POOL_WIDTH), lambda i: (nb - 1 - i, 4)),
                pl.BlockSpec((POOL_HIST, POOL_WIDTH), prev(POOL_HIST, 4)),
                pl.BlockSpec((tb, POOL_WIDTH), lambda i: (nb - 1 - i, 5)),
                pl.BlockSpec((tb, D_MODEL), rev),
                pl.BlockSpec((F32_SUBLANES, D_MODEL), prev(F32_SUBLANES, 0)),
                pl.BlockSpec((tb, D_MODEL), rev), pl.BlockSpec((tb, POOL_WIDTH), rev),
                pl.BlockSpec((tb, 2 * D_MODEL), rev)] + _branch_specs(tb, rev, fixed) + [
                    pl.BlockSpec((VEC_BAG_ROWS, D_MODEL), fixed)]
    out_shape = (jax.ShapeDtypeStruct((t, IN_COLS), BF16), jax.ShapeDtypeStruct((VEC_BAG_ROWS, D_MODEL), F32),
                 jax.ShapeDtypeStruct((MAT_BAG_ROWS, HEAD_DIM), F32))
    out_specs = (pl.BlockSpec((tb, IN_COLS), rev), pl.BlockSpec((VEC_BAG_ROWS, D_MODEL), fixed),
                 pl.BlockSpec((MAT_BAG_ROWS, HEAD_DIM), fixed))
    scratch = [pltpu.VMEM((tb + CONV_HIST, D_MODEL), F32), pltpu.VMEM((tb + POOL_HIST, POOL_WIDTH), F32),
               pltpu.VMEM((tb + F32_SUBLANES, D_MODEL), F32), pltpu.VMEM((tb + F32_SUBLANES, D_MODEL), F32),
               pltpu.VMEM((tb + CONV_HIST, D_MODEL), F32), pltpu.VMEM((tb + POOL_HIST, POOL_WIDTH), F32),
               pltpu.VMEM((F32_SUBLANES, D_MODEL), F32),
               pltpu.VMEM((tb, D_MODEL), F32), pltpu.VMEM((tb, D_MODEL), F32), pltpu.VMEM((tb, D_MODEL), F32)]
    return pl.pallas_call(
        body, name="branches_bwd", out_shape=out_shape, grid=(nb,), in_specs=in_specs, out_specs=out_specs,
        scratch_shapes=scratch, input_output_aliases={len(in_specs) - 1: 1},
        compiler_params=pltpu.CompilerParams(dimension_semantics=("arbitrary",),
                                             vmem_limit_bytes=VMEM_LIMIT_BYTES),
    )(z, z, z, z, z, z, hl, hl, dya, dyb, dzm, *weights, vec_bag)


def _merge_head(x2d, ya, yb, z, p2d, tgt, w_pl, w_pp, w_out, w_pg, w_pe, g2, gf, tb):
    t = x2d.shape[0]
    p_dim = p2d.shape[1]

    def body(x_ref, ya_ref, yb_ref, ma_ref, mb_ref, p_ref, t_ref, wpl_ref, wpp_ref, wout_ref, wpg_ref, wpe_ref,
             g2_ref, gf_ref,
             bag_ref, dxr_ref, dya_ref, dyb_ref, dzm_ref,
             mg_ref, do_ref, hn_ref, dgp_ref, dpe_ref, da_ref, dbm_ref, pbf_ref):
        @pl.when(pl.program_id(0) == 0)
        def _():
            bag_ref[...] = jnp.zeros_like(bag_ref)

        a_ = _dot(ya_ref[...], wpl_ref[...])
        bm = _dot(yb_ref[...], wpp_ref[...])
        sa = _sigmoid(ma_ref[...])
        sb = _sigmoid(mb_ref[...])
        mg = (sa * a_ + sb * bm).astype(BF16)
        mg_ref[...] = mg
        x1 = x_ref[...] + _dot(mg, wout_ref[...])
        xn2, r2 = _rms(x1)
        g2 = g2_ref[...]
        hn = (xn2 * g2).astype(BF16)
        hn_ref[...] = hn
        gate = _sigmoid(_dot(hn, wpg_ref[...]))
        pbf = p_ref[...].astype(BF16)
        pbf_ref[...] = pbf
        pe = _dot(pbf, wpe_ref[...])
        x2 = x1 + gate * pe
        xn3, r3 = _rms(x2)
        gf = gf_ref[...]
        err = xn3 * gf - t_ref[...]
        bag_ref[_bag_rows("loss"), 0:128] += 0.5 * jnp.sum(jnp.mean(err * err, axis=-1))

        dy = err * (1.0 / D_MODEL)
        bag_ref[_bag_row("final_g"), :] += jnp.sum(dy * xn3, axis=0, keepdims=True)
        dx2 = _rms_bwd(dy * gf, xn3, r3)
        dpe_ref[...] = (dx2 * gate).astype(BF16)
        dgp = ((dx2 * pe) * (gate * (1.0 - gate))).astype(BF16)
        dgp_ref[...] = dgp
        dhn = _dot_nt(dgp, wpg_ref[...])
        bag_ref[_bag_row("ple_norm_g"), :] += jnp.sum(dhn * xn2, axis=0, keepdims=True)
        dx1 = dx2 + _rms_bwd(dhn * g2, xn2, r2)
        dxr_ref[...] = dx1
        do = dx1.astype(BF16)
        do_ref[...] = do
        dmg = _dot_nt(do, wout_ref[...])
        da = (dmg * sa).astype(BF16)
        dbm = (dmg * sb).astype(BF16)
        da_ref[...] = da
        dbm_ref[...] = dbm
        dzm_ref[:, 0:D_MODEL] = (dmg * a_ * (sa * (1.0 - sa))).astype(BF16)
        dzm_ref[:, D_MODEL:] = (dmg * bm * (sb * (1.0 - sb))).astype(BF16)
        dya_ref[...] = _dot_nt(da, wpl_ref[...])
        dyb_ref[...] = _dot_nt(dbm, wpp_ref[...])

    row = lambda i: (i, 0)
    fixed = lambda i: (0, 0)

    def resident(shape):
        return pl.BlockSpec(shape, fixed, pipeline_mode=pl.Buffered(1))

    tok = lambda width: pl.BlockSpec((tb, width), row)
    in_specs = [tok(D_MODEL), tok(D_MODEL), tok(POOL_WIDTH),
                pl.BlockSpec((tb, D_MODEL), lambda i: (i, 3)), pl.BlockSpec((tb, D_MODEL), lambda i: (i, 4)),
                tok(p_dim), tok(D_MODEL),
                resident((D_MODEL, D_MODEL)), resident((POOL_WIDTH, D_MODEL)), resident((D_MODEL, D_MODEL)),
                resident((D_MODEL, D_MODEL)), resident((p_dim, D_MODEL)),
                pl.BlockSpec((1, D_MODEL), fixed), pl.BlockSpec((1, D_MODEL), fixed)]
    bf = lambda width: jax.ShapeDtypeStruct((t, width), BF16)
    f32 = lambda width: jax.ShapeDtypeStruct((t, width), F32)
    out_shape = (jax.ShapeDtypeStruct((VEC_BAG_ROWS, D_MODEL), F32),
                 f32(D_MODEL), f32(D_MODEL), f32(POOL_WIDTH), bf(2 * D_MODEL),
                 bf(D_MODEL), bf(D_MODEL), bf(D_MODEL), bf(D_MODEL), bf(D_MODEL), bf(D_MODEL), bf(D_MODEL), bf(p_dim))
    out_specs = (pl.BlockSpec((VEC_BAG_ROWS, D_MODEL), fixed),
                 tok(D_MODEL), tok(D_MODEL), tok(POOL_WIDTH), tok(2 * D_MODEL),
                 tok(D_MODEL), tok(D_MODEL), tok(D_MODEL), tok(D_MODEL), tok(D_MODEL), tok(D_MODEL), tok(D_MODEL),
                 tok(p_dim))
    return pl.pallas_call(
        body, name="merge_head", out_shape=out_shape, grid=(t // tb,), in_specs=in_specs, out_specs=out_specs,
        compiler_params=pltpu.CompilerParams(dimension_semantics=("arbitrary",),
                                             vmem_limit_bytes=VMEM_LIMIT_BYTES),
    )(x2d, ya, yb, z, z, p2d, tgt, w_pl, w_pp, w_out, w_pg, w_pe, g2, gf)


def kernel(x, p, norm_g, w_in, conv_w, conv_b, lru_w_a, lru_b_a, lru_w_x, lru_b_x, lru_lambda, pool_w, pool_scale, w_proj_lru, w_proj_pool, w_out, ple_norm_g, w_ple_gate, w_ple_proj, final_g, loss_target, m_norm_g, m_w_in, m_conv_w, m_conv_b, m_lru_w_a, m_lru_b_a, m_lru_w_x, m_lru_b_x, m_lru_lambda, m_pool_w, m_pool_scale, m_w_proj_lru, m_w_proj_pool, m_w_out, m_ple_norm_g, m_w_ple_gate, m_w_ple_proj, m_final_g, v_norm_g, v_w_in, v_conv_w, v_conv_b, v_lru_w_a, v_lru_b_a, v_lru_w_x, v_lru_b_x, v_lru_lambda, v_pool_w, v_pool_scale, v_w_proj_lru, v_w_proj_pool, v_w_out, v_ple_norm_g, v_w_ple_gate, v_w_ple_proj, v_final_g):
    bsz, seq, _ = x.shape
    t = bsz * seq
    tb_mm = min(1024, seq)
    tb_seq = min(256, seq // 2) if seq >= 512 else seq
    x2d = x.reshape(t, D_MODEL)
    p2d = p.reshape(t, p.shape[-1])
    tgt = loss_target.reshape(t, D_MODEL)
    chip = 2 * lax.axis_index("x") + lax.axis_index("y")

    rest = [(w_proj_lru[0], 0), (w_proj_pool[0], 1), (w_out[0], 0), (w_ple_gate[0], 0), (w_ple_proj[0], 1)]
    z, h_bf, w_in_f, conv_w_f = _in_proj_gather(x2d, norm_g, w_in[0].astype(BF16), [(conv_w[0], 1, False)], tb_mm)

    wa_bf = lru_w_a[0].astype(BF16)
    wx_bf = lru_w_x[0].astype(BF16)
    pw_bf = pool_w[0].astype(BF16)
    branch_w = (conv_w_f, conv_b, wa_bf, lru_b_a.reshape(1, D_MODEL), wx_bf, lru_b_x.reshape(1, D_MODEL),
                lru_lambda, pw_bf, pool_scale)

    ya, yb, hl, w_pl_f, w_pp_f, w_out_f, w_pg_f, w_pe_f = _branches_fwd(
        z, branch_w, seq, tb_seq, [(w.astype(BF16), axis, True) for w, axis in rest])
    (vec_bag, dx_res, dya, dyb, dzm, mg_bf, do_bf, hn_bf, dgp_bf, dpe_bf, da_bf, dbm_bf, p_bf) = _merge_head(
        x2d, ya, yb, z, p2d, tgt, w_pl_f, w_pp_f, w_out_f, w_pg_f, w_pe_f, ple_norm_g, final_g.reshape(1, D_MODEL),
        tb_seq)
    dz, vec_bag, mat_bag = _branches_bwd(z, hl, dya, dyb, dzm, branch_w, vec_bag, seq, tb_seq)

    tb_dw = min(1024, seq)
    def proj_grad(lhs, rhs, name, cols):
        g32, g16 = _weight_grad(lhs, rhs, 1, tb_dw, name)
        if cols:
            return g32[0], True, g16[0]
        rows = g32.shape[1] // 8
        return g32.reshape(8, rows, g32.shape[2]), False, g16.reshape(8, rows, g32.shape[2])

    p_dim = p2d.shape[1]
    proj_parts = [proj_grad(ya, da_bf, "dw_proj_lru", False), proj_grad(yb, dbm_bf, "dw_proj_pool", True),
                  proj_grad(mg_bf, do_bf, "dw_out", False), proj_grad(hn_bf, dgp_bf, "dw_ple_gate", False),
                  proj_grad(p_bf, dpe_bf, "dw_ple_proj", True)]
    nb_dw = t // tb_dw
    g_in, g_in16, r_pl, r_pp, r_out, r_pg, r_pe, vec_mine, mat_mine = _weight_grad(
        h_bf, dz, N_CHIPS, tb_dw, "dw_in",
        reduce=(proj_parts + [(vec_bag.reshape(8, VEC_BAG_ROWS // 8, D_MODEL), False, None),
                              (mat_bag.reshape(8, MAT_BAG_ROWS // 8, HEAD_DIM), False, None)],
                [BF16] * 5 + [F32] * 2,
                (0, nb_dw // 2, 2 * nb_dw - 1, 3 * nb_dw + nb_dw // 2, N_CHIPS * nb_dw - 1)))
    pieces = (8, D_MODEL // 2, IN_COLS // N_CHIPS)
    nb_seq = t // tb_seq
    dx, g_g1, r_in, vec_sum, mat_sum = _in_proj_bwd(
        dz, w_in_f, x2d, dx_res, norm_g, tb_seq,
        reduce=([(g_in.reshape(pieces), False, g_in16.reshape(pieces))], BF16,
                (0, nb_seq // 8, nb_seq // 2, nb_seq - 1, nb_seq - 1)),
        shards=[(vec_mine.reshape(VEC_BAG_ROWS // N_CHIPS, D_MODEL), 0, True),
                (mat_mine.reshape(MAT_BAG_ROWS // N_CHIPS, HEAD_DIM), 0, True)])

    def big_update(w, g2d, m, v, name):
        return tuple(a[None] for a in _adamw(w[0], g2d, m[0], v[0], g2d.shape[0] // 4, name))

    u_in = big_update(w_in, r_in.reshape(D_MODEL, IN_COLS // N_CHIPS), m_w_in, v_w_in, "adamw_w_in")
    u_pl = big_update(w_proj_lru, r_pl.reshape(D_MODEL // N_CHIPS, D_MODEL), m_w_proj_lru, v_w_proj_lru, "adamw_w_proj_lru")
    u_pp = big_update(w_proj_pool, r_pp.reshape(POOL_WIDTH, D_MODEL // N_CHIPS), m_w_proj_pool, v_w_proj_pool, "adamw_w_proj_pool")
    u_out = big_update(w_out, r_out.reshape(D_MODEL // N_CHIPS, D_MODEL), m_w_out, v_w_out, "adamw_w_out")
    u_pg = big_update(w_ple_gate, r_pg.reshape(D_MODEL // N_CHIPS, D_MODEL), m_w_ple_gate, v_w_ple_gate, "adamw_w_ple_gate")
    u_pe = big_update(w_ple_proj, r_pe.reshape(p_dim, D_MODEL // N_CHIPS), m_w_ple_proj, v_w_ple_proj, "adamw_w_ple_proj")

    small = [("norm_g", norm_g, m_norm_g, v_norm_g), ("conv_b", conv_b, m_conv_b, v_conv_b),
             ("lru_w_a", lru_w_a, m_lru_w_a, v_lru_w_a), ("lru_b_a", lru_b_a, m_lru_b_a, v_lru_b_a),
             ("lru_w_x", lru_w_x, m_lru_w_x, v_lru_w_x), ("lru_b_x", lru_b_x, m_lru_b_x, v_lru_b_x),
             ("lru_lambda", lru_lambda, m_lru_lambda, v_lru_lambda), ("pool_w", pool_w, m_pool_w, v_pool_w),
             ("pool_scale", pool_scale, m_pool_scale, v_pool_scale),
             ("ple_norm_g", ple_norm_g, m_ple_norm_g, v_ple_norm_g), ("final_g", final_g, m_final_g, v_final_g)]

    def view(a):
        return a.reshape(-1, a.shape[-1]) if a.ndim != 3 else a[0]

    cw_at = F32_SUBLANES * VEC_BAG_SLOTS.index("conv_w")
    cw_cols = D_MODEL // N_CHIPS
    g_cw = lax.dynamic_slice(vec_sum, (cw_at, chip * cw_cols), (CONV_WIDTH, cw_cols))
    flat = _adamw_replicated(vec_sum, mat_sum, g_g1, [(name,) + tuple(view(a) for a in arrs) for name, *arrs in small],
                             (conv_w[0], m_conv_w[0], v_conv_w[0], g_cw))
    u_small = {name: tuple(flat[4 * k + pick].reshape(arrs[0].shape) for pick in range(4))
               for k, (name, *arrs) in enumerate(small)}
    u_cw = tuple(a[None] for a in flat[4 * len(small):4 * len(small) + 4])

    loss = flat[-1].reshape(())
    grad_x = dx.reshape(bsz, seq, D_MODEL)

    def ordered(pick):
        s = {name: u[pick] for name, u in u_small.items()}
        return [s["norm_g"], u_in[pick], u_cw[pick], s["conv_b"], s["lru_w_a"], s["lru_b_a"], s["lru_w_x"], s["lru_b_x"],
                s["lru_lambda"], s["pool_w"], s["pool_scale"], u_pl[pick], u_pp[pick], u_out[pick], s["ple_norm_g"],
                u_pg[pick], u_pe[pick], s["final_g"]]

    return (loss, grad_x, *ordered(0), *ordered(1), *ordered(2), *ordered(3))
```

```python
import jax
import jax.numpy as jnp
from jax import lax
from jax.experimental import pallas as pl
from jax.experimental.pallas import tpu as pltpu

F32 = jnp.float32
BF16 = jnp.bfloat16
MESH = pl.DeviceIdType.MESH

D_MODEL = 1024
LRU_HEADS = 8
HEAD_DIM = 128
CONV_WIDTH = 4
LRU_C = 8.0
POOL_WIDTH = 512
POOL_WINDOWS = (2, 4, 8, 16)
POOL_GROUP_DIM = 128
IN_COLS = 5120
N_CHIPS = 4
EPS = 1e-6

ADAM_LR = 0.001
ADAM_B1 = 0.9
ADAM_B2 = 0.999
ADAM_EPS = 1e-08
ADAM_WD = 0.01
ADAM_STEP = 10

F32_SUBLANES = 8
CONV_HIST = 8
POOL_HIST = 16
VMEM_LIMIT_BYTES = 58 * 1024 * 1024
VEC_BAG_SLOTS = ("norm_g", "conv_w", "conv_b", "lru_b_a", "lru_b_x", "lru_lambda", "pool_scale", "ple_norm_g",
                 "final_g", "loss")
VEC_BAG_ROWS = 128
MAT_BAG_AT = {"lru_w_a": 0, "lru_w_x": LRU_HEADS * HEAD_DIM, "pool_w": 2 * LRU_HEADS * HEAD_DIM}
MAT_BAG_ROWS = 2 * LRU_HEADS * HEAD_DIM + len(POOL_WINDOWS) * POOL_GROUP_DIM


def _bag_row(name, k=0):
    at = F32_SUBLANES * VEC_BAG_SLOTS.index(name) + k
    return slice(at, at + 1)


def _bag_rows(name):
    at = F32_SUBLANES * VEC_BAG_SLOTS.index(name)
    return slice(at, at + F32_SUBLANES)


def _dot(a, b):
    return jnp.dot(a, b, preferred_element_type=F32)


def _dot_nt(a, b):
    return lax.dot_general(a, b, (((1,), (1,)), ((), ())), preferred_element_type=F32)


def _dot_tn(a, b):
    return lax.dot_general(a, b, (((0,), (0,)), ((), ())), preferred_element_type=F32)


def _sigmoid(v):
    return jax.nn.sigmoid(v)


def _softplus(v):
    return jnp.maximum(v, 0.0) + jnp.log1p(jnp.exp(-jnp.abs(v)))


def _place():
    return lax.axis_index("x"), lax.axis_index("y"), lax.axis_index("c")


GATHER_SEMS = 6


def _gather_shapes(shards):
    out_shape = []
    for arr, axis, _ in shards:
        r, cols = arr.shape
        out_shape.append(jax.ShapeDtypeStruct((N_CHIPS * r, cols) if axis == 0 else (r, N_CHIPS * cols), arr.dtype))
    n = len(shards)
    sems = [pltpu.SemaphoreType.DMA((n * GATHER_SEMS,)), pltpu.SemaphoreType.DMA((n * GATHER_SEMS,)),
            pltpu.SemaphoreType.DMA((n,))]
    return out_shape, sems


def _gather_steps(shards, ins, outs, send_sems, recv_sems, local_sems):
    n = len(shards)
    x, y, c = _place()
    me, sibling = (x, y, c), (x, y, 1 - c)
    chips = [(x, 1 - y), (1 - x, y), (1 - x, 1 - y)]

    def region(k, cx, cy, hc):
        (r, cols), axis = shards[k][0].shape, shards[k][1]
        j = 2 * cx + cy
        if axis == 0:
            if hc is None:
                return outs[k].at[pl.ds(j * r, r), :]
            return outs[k].at[pl.ds(j * r + hc * (r // 2), r // 2), :]
        if hc is None:
            return outs[k].at[:, pl.ds(j * cols, cols)]
        return outs[k].at[pl.ds(hc * (r // 2), r // 2), pl.ds(j * cols, cols)]

    def remote(k, sem, block, to, src=None):
        dst = region(k, *block)
        return pltpu.make_async_remote_copy(
            src_ref=dst if src is None else src, dst_ref=dst,
            send_sem=send_sems.at[k * GATHER_SEMS + sem], recv_sem=recv_sems.at[k * GATHER_SEMS + sem],
            device_id=to, device_id_type=MESH)

    def first(k, idx):
        r, split = shards[k][0].shape[0], shards[k][2]
        src = ins[k].at[pl.ds(c * (r // 2), r // 2), :] if split else ins[k]
        return remote(k, idx, (x, y, c if split else None), (*chips[idx], c), src=src)

    def relay(k):
        src_chip = (jnp.bitwise_xor(x, 1 - c), jnp.bitwise_xor(y, c))
        dst_chip = (jnp.bitwise_xor(x, c), jnp.bitwise_xor(y, 1 - c))
        return remote(k, 2, (*src_chip, c), (*dst_chip, c))

    def passed(k, idx):
        return remote(k, 3 + idx, (*chips[idx], c), sibling)

    def mine(k):
        return pltpu.make_async_copy(ins[k], region(k, x, y, None), local_sems.at[k])

    def start():
        for k in range(n):
            mine(k).start()
            for idx in range(2 if shards[k][2] else 3):
                first(k, idx).start()

    def relay_on():
        for k in range(n):
            split = shards[k][2]
            for idx in range(2):
                remote(k, idx, (*chips[idx], c if split else None), me).wait_recv()
            if split:
                relay(k).start()
                passed(k, 0).start()
                passed(k, 1).start()

    def finish():
        for k in range(n):
            split = shards[k][2]
            remote(k, 2, (*chips[2], c if split else None), me).wait_recv()
            if split:
                passed(k, 2).start()
        for k in range(n):
            if shards[k][2]:
                for idx in range(3):
                    remote(k, 3 + idx, (*chips[idx], 1 - c), me).wait_recv()
        for k in range(n):
            if shards[k][2]:
                for cp in (first(k, 0), first(k, 1), relay(k), passed(k, 0), passed(k, 1), passed(k, 2)):
                    cp.wait_send()
            else:
                for idx in range(3):
                    first(k, idx).wait_send()
            mine(k).wait()

    return start, relay_on, finish


RS_ADD_ROWS = (64, 32, 16, 8)


N_DEV = 2 * N_CHIPS


def _all_reduce_scratch(shape):
    return [pltpu.VMEM((N_DEV,) + tuple(shape), F32), pltpu.SemaphoreType.DMA((N_DEV - 1,)),
            pltpu.SemaphoreType.DMA((N_DEV - 1,))]


def _all_reduce_tile(v_ref, o_ref, slots, send_sems, recv_sems):
    flips = [(dx, dy, dc) for dx in (0, 1) for dy in (0, 1) for dc in (0, 1)][1:]
    x, y, c = _place()
    mine = 4 * x + 2 * y + c

    def copy(k, to_flip, slot):
        dx, dy, dc = to_flip
        peer = (jnp.bitwise_xor(x, dx), jnp.bitwise_xor(y, dy), jnp.bitwise_xor(c, dc))
        return pltpu.make_async_remote_copy(
            src_ref=v_ref, dst_ref=slots.at[slot], send_sem=send_sems.at[k], recv_sem=recv_sems.at[k],
            device_id=peer, device_id_type=MESH)

    sends = [copy(k, flip, mine) for k, flip in enumerate(flips)]
    for cp in sends:
        cp.start()
    slots[mine] = v_ref[...]
    for k, (dx, dy, dc) in enumerate(flips):
        copy(k, (dx, dy, dc), jnp.bitwise_xor(mine, 4 * dx + 2 * dy + dc)).wait_recv()
    total = slots[0]
    for d in range(1, N_DEV):
        total = total + slots[d]
    o_ref[...] = total
    for cp in sends:
        cp.wait_send()


RS_SEMS = 8
RS_LOCAL_SEMS = 5


def _rs_piece_shape(part):
    arr, cols = part[0], part[1]
    return (arr.shape[0] // 2, arr.shape[1] // N_CHIPS) if cols else tuple(arr.shape[1:])


def _rs_operands(parts):
    return [p[0] for p in parts] + [p[0] if p[2] is None else p[2] for p in parts]


def _rs_wires(parts, wire):
    return list(wire) if isinstance(wire, (list, tuple)) else [wire] * len(parts)


def _rs_shapes(parts, wire):
    n = len(parts)
    shapes = [_rs_piece_shape(p) for p in parts]
    out_shape = [jax.ShapeDtypeStruct((2,) + s, F32) for s in shapes]
    scratch = []
    for lead, kind in ((N_CHIPS, "f32"), (N_CHIPS, "narrow"), (N_CHIPS, "wire"), (None, "f32"), (N_CHIPS, "wire")):
        for s, p, w in zip(shapes, parts, _rs_wires(parts, wire)):
            dtype = {"f32": F32, "narrow": F32 if p[2] is None else p[2].dtype, "wire": w}[kind]
            scratch.append(pltpu.VMEM(s if lead is None else (lead,) + s, dtype))
    scratch += [pltpu.SemaphoreType.DMA((n * RS_SEMS,)), pltpu.SemaphoreType.DMA((n * RS_SEMS,)),
                pltpu.SemaphoreType.DMA((n * RS_LOCAL_SEMS,))]
    return out_shape, scratch


def _rs_steps(parts, ins, outs, scratch):
    n = len(parts)
    own, sib, got, fin, snd = (scratch[k * n:(k + 1) * n] for k in range(5))
    send_sems, recv_sems, local_sems = scratch[5 * n:]
    shapes = [_rs_piece_shape(p) for p in parts]
    x, y, c = _place()
    j_me = 2 * x + y
    me, sibling = (x, y, c), (x, y, 1 - c)

    def piece(a, jj, core, narrow=False):
        ref = ins[n + a] if narrow else ins[a]
        if parts[a][1]:
            r, cl = shapes[a]
            return ref.at[pl.ds(core * r, r), pl.ds(jj * cl, cl)]
        return ref.at[2 * jj + core]

    def remote(a, sem, src, dst, to):
        return pltpu.make_async_remote_copy(
            src_ref=src, dst_ref=dst, send_sem=send_sems.at[a * RS_SEMS + sem],
            recv_sem=recv_sems.at[a * RS_SEMS + sem], device_id=to, device_id_type=MESH)

    def rows_loop(a, fn):
        r = shapes[a][0]
        step = max(s for s in RS_ADD_ROWS if r % s == 0)

        def it(i, carry):
            fn(pl.ds(pl.multiple_of(i * step, step), step))
            return carry

        lax.fori_loop(0, r // step, it, 0)

    def load(a, jj):
        return pltpu.make_async_copy(piece(a, jj, c), own[a].at[jj], local_sems.at[a * RS_LOCAL_SEMS + jj])

    def to_sibling(a, jj):
        return remote(a, jj, piece(a, jj, 1 - c, narrow=True), sib[a].at[jj], sibling)

    near = (jnp.bitwise_xor(x, 1 - c), jnp.bitwise_xor(y, c))
    far = (jnp.bitwise_xor(x, c), jnp.bitwise_xor(y, 1 - c))
    diag = (1 - x, 1 - y)
    FROM_NEAR, FROM_FAR, FEED = 0, 1, 2

    def chip_of(chip):
        return 2 * chip[0] + chip[1]

    def feed(a):
        return remote(a, 4, snd[a].at[chip_of(diag)], got[a].at[FEED], (*near, c))

    def to_near(a):
        return remote(a, 5, snd[a].at[chip_of(near)], got[a].at[FROM_NEAR], (*near, c))

    def to_far(a):
        return remote(a, 6, snd[a].at[chip_of(far)], got[a].at[FROM_FAR], (*far, c))

    def store(a):
        return pltpu.make_async_copy(fin[a], outs[a].at[c], local_sems.at[a * RS_LOCAL_SEMS + 4])

    def result_to_sibling(a):
        return remote(a, 7, fin[a], outs[a].at[c], sibling)

    def exchange():
        for a in range(n):
            for jj in range(N_CHIPS):
                load(a, jj).start()
                to_sibling(a, jj).start()

    def chip_sums():
        for a in range(n):
            for jj in range(N_CHIPS):
                load(a, jj).wait()
                remote(a, jj, sib[a].at[jj], sib[a].at[jj], me).wait_recv()

                def add(sl, a=a, jj=jj):
                    q = own[a][jj, sl, :] + sib[a][jj, sl, :].astype(F32)
                    own[a][jj, sl, :] = q
                    snd[a][jj, sl, :] = q.astype(snd[a].dtype)

                rows_loop(a, add)
        for a in range(n):
            feed(a).start()
        for a in range(n):
            to_near(a).start()

    def relay():
        for a in range(n):
            remote(a, 4, got[a].at[FEED], got[a].at[FEED], me).wait_recv()

            def add(sl, a=a):
                pair = own[a][chip_of(far), sl, :] + got[a][FEED, sl, :].astype(F32)
                snd[a][chip_of(far), sl, :] = pair.astype(snd[a].dtype)

            rows_loop(a, add)
            to_far(a).start()

    def totals():
        for a in range(n):
            remote(a, 5, got[a].at[FROM_NEAR], got[a].at[FROM_NEAR], me).wait_recv()
            remote(a, 6, got[a].at[FROM_FAR], got[a].at[FROM_FAR], me).wait_recv()

            def total(sl, a=a):
                fin[a][sl, :] = (own[a][j_me, sl, :] + got[a][FROM_NEAR, sl, :].astype(F32)) + (
                    got[a][FROM_FAR, sl, :].astype(F32))

            rows_loop(a, total)
            store(a).start()
            result_to_sibling(a).start()

    def finish():
        for a in range(n):
            remote(a, 7, outs[a].at[1 - c], outs[a].at[1 - c], me).wait_recv()
        for a in range(n):
            for jj in range(N_CHIPS):
                to_sibling(a, jj).wait_send()
            for cp in (feed(a), to_near(a), to_far(a), result_to_sibling(a)):
                cp.wait_send()
            store(a).wait()

    return exchange, chip_sums, relay, totals, finish


def _rms(x):
    r = lax.rsqrt(jnp.mean(x * x, axis=-1, keepdims=True) + EPS)
    return x * r, r


def _rms_bwd(dxn, xn, r):
    return r * (dxn - xn * jnp.mean(dxn * xn, axis=-1, keepdims=True))


def _in_proj_gather(x2d, norm_g, w_in_sh, shards, tb):
    t = x2d.shape[0]
    nb = t // tb
    cols = IN_COLS // N_CHIPS
    half = D_MODEL // 2
    n = len(shards)

    def body(x_ref, g_ref, win_ref, *refs):
        ins = refs[:n]
        z_ref, h_ref, wfull_ref = refs[n:n + 3]
        outs = refs[n + 3:2 * n + 3]
        wv, h_all, send_sems, recv_sems, local_sems, w_send, w_recv, w_local = refs[2 * n + 3:]
        s, i = pl.program_id(0), pl.program_id(1)
        x, y, c = _place()
        me, sibling = (x, y, c), (x, y, 1 - c)
        chips = [(x, 1 - y), (1 - x, y), (1 - x, 1 - y)]

        def w_half(cx, cy, hc):
            return wv.at[2 * cx + cy, pl.ds(hc * half, half), :]

        def w_remote(sem, block, to, src=None):
            dst = w_half(*block)
            return pltpu.make_async_remote_copy(
                src_ref=dst if src is None else src, dst_ref=dst, send_sem=w_send.at[sem],
                recv_sem=w_recv.at[sem], device_id=to, device_id_type=MESH)

        def w_first(idx):
            return w_remote(idx, (x, y, c), (*chips[idx], c), src=win_ref.at[pl.ds(c * half, half), :])

        def w_relay():
            src_chip = (jnp.bitwise_xor(x, 1 - c), jnp.bitwise_xor(y, c))
            dst_chip = (jnp.bitwise_xor(x, c), jnp.bitwise_xor(y, 1 - c))
            return w_remote(2, (*src_chip, c), (*dst_chip, c))

        def w_pass(idx):
            return w_remote(3 + idx, (*chips[idx], c), sibling)

        def w_store(k, cx, cy):
            jj = 2 * cx + cy
            return pltpu.make_async_copy(wv.at[jj], wfull_ref.at[:, pl.ds(jj * cols, cols)], w_local.at[k])

        start_rest, relay_rest, finish_rest = _gather_steps(shards, ins, outs, send_sems, recv_sems, local_sems)
        own = pltpu.make_async_copy(win_ref, wv.at[2 * x + y], w_local.at[4])

        @pl.when((s == 0) & (i == 0))
        def _():
            own.start()
            w_first(0).start()
            w_first(1).start()
            start_rest()
            own.wait()
            w_store(0, x, y).start()

        @pl.when((s == 1) & (i == 0))
        def _():
            w_remote(0, (*chips[0], c), me).wait_recv()
            w_remote(1, (*chips[1], c), me).wait_recv()
            w_relay().start()
            w_pass(0).start()
            w_pass(1).start()
            w_remote(3, (*chips[0], 1 - c), me).wait_recv()
            w_store(1, *chips[0]).start()

        @pl.when((s == 2) & (i == 0))
        def _():
            w_remote(4, (*chips[1], 1 - c), me).wait_recv()
            w_store(2, *chips[1]).start()

        @pl.when((s == 3) & (i == 0))
        def _():
            w_remote(2, (*chips[2], c), me).wait_recv()
            w_pass(2).start()
            w_remote(5, (*chips[2], 1 - c), me).wait_recv()
            w_store(3, *chips[2]).start()

        keep_h = pltpu.make_async_copy(h_all.at[i], h_ref.at[pl.ds(pl.multiple_of(i * tb, tb), tb), :], w_local.at[5])

        @pl.when(s == 0)
        def _():
            xn, _ = _rms(x_ref[...])
            h_all[i] = (xn * g_ref[...]).astype(BF16)
            keep_h.start()

        z_ref[...] = _dot(h_all[i], wv[jnp.bitwise_xor(2 * x + y, s)])
        pl.when(s == 0)(keep_h.wait)

        @pl.when((s == N_CHIPS - 1) & (i == nb - 1))
        def _():
            relay_rest()
            finish_rest()
            for cp in (w_first(0), w_first(1), w_relay(), w_pass(0), w_pass(1), w_pass(2)):
                cp.wait_send()
            w_store(0, x, y).wait()
            for idx in range(3):
                w_store(idx + 1, *chips[idx]).wait()

    rest_shape, rest_sems = _gather_shapes(shards)
    out_shape = [jax.ShapeDtypeStruct((t, IN_COLS), F32), jax.ShapeDtypeStruct((t, D_MODEL), BF16),
                 jax.ShapeDtypeStruct((D_MODEL, IN_COLS), BF16)] + rest_shape
    any_spec = pl.BlockSpec(memory_space=pl.ANY)

    def z_map(s, i):
        return (i, jnp.bitwise_xor(2 * lax.axis_index("x") + lax.axis_index("y"), s))

    return pl.pallas_call(
        body, name="in_proj", out_shape=tuple(out_shape),
        grid=(N_CHIPS, nb),
        in_specs=[pl.BlockSpec((tb, D_MODEL), lambda s, i: (jnp.where(s == 0, i, nb - 1), 0)),
                  pl.BlockSpec((1, D_MODEL), lambda s, i: (0, 0)), any_spec] + [any_spec] * n,
        out_specs=tuple([pl.BlockSpec((tb, cols), z_map), any_spec, any_spec] + [any_spec] * n),
        scratch_shapes=[pltpu.VMEM((N_CHIPS, D_MODEL, cols), BF16), pltpu.VMEM((nb, tb, D_MODEL), BF16)] + rest_sems + [
            pltpu.SemaphoreType.DMA((GATHER_SEMS,)), pltpu.SemaphoreType.DMA((GATHER_SEMS,)),
            pltpu.SemaphoreType.DMA((N_CHIPS + 2,))],
        compiler_params=pltpu.CompilerParams(dimension_semantics=("arbitrary", "arbitrary"),
                                             vmem_limit_bytes=VMEM_LIMIT_BYTES),
    )(x2d, norm_g, w_in_sh, *[sh[0] for sh in shards])


def _in_proj_bwd(dz, w_in, x2d, dx_res, norm_g, tb, reduce, shards):
    t = x2d.shape[0]
    nb = t // tb
    parts, wire, steps = reduce
    n = len(parts)
    k = len(shards)

    def body(dz_ref, w_ref, x_ref, dres_ref, g_ref, *refs):
        at = 2 * n + k
        dx_ref, dg_ref = refs[at:at + 2]
        rs_outs, g_outs = refs[at + 2:at + 2 + n], refs[at + 2 + n:at + 2 + n + k]
        scratch = refs[at + 2 + n + k:]
        rs_scr, g_sems, dg_acc, ar_scr = scratch[:-7], scratch[-7:-4], scratch[-4], scratch[-3:]
        rs = _rs_steps(parts, refs[:2 * n], rs_outs, rs_scr)
        for step, when in zip(rs, steps):
            pl.when(pl.program_id(0) == when)(step)
        gather = _gather_steps(shards, refs[2 * n:at], g_outs, *g_sems)
        for step, when in zip(gather, (0, nb // 2, nb - 1)):
            pl.when(pl.program_id(0) == when)(step)

        @pl.when(pl.program_id(0) == 0)
        def _():
            dg_acc[...] = jnp.zeros_like(dg_acc)

        xn, r = _rms(x_ref[...])
        g = g_ref[...]
        dh = _dot_nt(dz_ref[...], w_ref[...])
        dg_acc[0:1, :] += jnp.sum(dh * xn, axis=0, keepdims=True)
        dx_ref[...] = dres_ref[...] + _rms_bwd(dh * g, xn, r)

        @pl.when(pl.program_id(0) == nb - 1)
        def _():
            _all_reduce_tile(dg_acc, dg_ref, *ar_scr)

    row = lambda i: (i, 0)
    fixed = lambda i: (0, 0)
    rs_shape, rs_scratch = _rs_shapes(parts, wire)
    g_shape, g_sems = _gather_shapes(shards)
    any_spec = pl.BlockSpec(memory_space=pl.ANY)
    return pl.pallas_call(
        body, name="in_proj_bwd",
        out_shape=tuple([jax.ShapeDtypeStruct((t, D_MODEL), F32), jax.ShapeDtypeStruct((F32_SUBLANES, D_MODEL), F32)]
                        + rs_shape + g_shape),
        grid=(nb,),
        in_specs=[pl.BlockSpec((tb, IN_COLS), row),
                  pl.BlockSpec((D_MODEL, IN_COLS), fixed, pipeline_mode=pl.Buffered(1)),
                  pl.BlockSpec((tb, D_MODEL), row), pl.BlockSpec((tb, D_MODEL), row),
                  pl.BlockSpec((1, D_MODEL), fixed)] + [any_spec] * (2 * n + k),
        out_specs=tuple([pl.BlockSpec((tb, D_MODEL), row), pl.BlockSpec((F32_SUBLANES, D_MODEL), fixed)]
                        + [any_spec] * (n + k)),
        scratch_shapes=rs_scratch + g_sems + [pltpu.VMEM((F32_SUBLANES, D_MODEL), F32)] + _all_reduce_scratch(
            (F32_SUBLANES, D_MODEL)),
        compiler_params=pltpu.CompilerParams(dimension_semantics=("arbitrary",),
                                             vmem_limit_bytes=VMEM_LIMIT_BYTES),
    )(dz, w_in, x2d, dx_res, norm_g, *_rs_operands(parts), *[sh[0] for sh in shards])


def _weight_grad(lhs, rhs, n_chunks, tb, name, reduce=None):
    t, k = lhs.shape
    nc = rhs.shape[1] // n_chunks
    nb = t // tb
    parts, wire, steps = reduce if reduce is not None else ([], F32, ())
    n = len(parts)

    def body(l_ref, r_ref, *refs):
        o_ref, o16_ref = refs[2 * n:2 * n + 2]
        if n:
            at = pl.program_id(0) * nb + pl.program_id(1)
            rs = _rs_steps(parts, refs[:2 * n], refs[2 * n + 2:3 * n + 2], refs[3 * n + 2:])
            for step, when in zip(rs, steps):
                pl.when(at == when)(step)

        @pl.when(pl.program_id(1) == 0)
        def _():
            o_ref[...] = jnp.zeros_like(o_ref)

        o_ref[...] += _dot_tn(l_ref[...], r_ref[...])

        @pl.when(pl.program_id(1) == nb - 1)
        def _():
            o16_ref[...] = o_ref[...].astype(BF16)

    rs_shape, rs_scratch = _rs_shapes(parts, wire) if n else ([], [])
    any_spec = pl.BlockSpec(memory_space=pl.ANY)
    chunk = pl.BlockSpec((None, k, nc), lambda j, i: (j, 0, 0))
    return pl.pallas_call(
        body, name=name,
        out_shape=tuple([jax.ShapeDtypeStruct((n_chunks, k, nc), F32), jax.ShapeDtypeStruct((n_chunks, k, nc), BF16)]
                        + rs_shape),
        grid=(n_chunks, nb),
        in_specs=[pl.BlockSpec((tb, k), lambda j, i: (i, 0)), pl.BlockSpec((tb, nc), lambda j, i: (i, j))]
        + [any_spec] * (2 * n),
        out_specs=tuple([chunk, chunk] + [any_spec] * n),
        scratch_shapes=rs_scratch,
        compiler_params=pltpu.CompilerParams(dimension_semantics=("arbitrary", "arbitrary"),
                                             vmem_limit_bytes=VMEM_LIMIT_BYTES),
    )(lhs, rhs, *_rs_operands(parts))


def _adam_update(w, g, m, v):
    m_ = ADAM_B1 * m + (1.0 - ADAM_B1) * g
    v_ = ADAM_B2 * v + (1.0 - ADAM_B2) * jnp.square(g)
    m_hat = m_ / (1.0 - ADAM_B1 ** ADAM_STEP)
    v_hat = v_ / (1.0 - ADAM_B2 ** ADAM_STEP)
    return -ADAM_LR * (m_hat / (jnp.sqrt(v_hat) + ADAM_EPS) + ADAM_WD * w), m_, v_


def _adamw_replicated(vec_sum, mat_sum, norm_grad, entries, conv):
    n = len(entries)

    def grad_of(name, shape, vec_ref, mat_ref, norm_ref):
        if name == "norm_g":
            return norm_ref[0:1, :]
        if name in MAT_BAG_AT:
            return mat_ref[MAT_BAG_AT[name]:MAT_BAG_AT[name] + shape[0], :]
        if shape[0] == 1:
            return vec_ref[_bag_row(name), 0:shape[1]]
        return jnp.concatenate([vec_ref[_bag_row(name), h * shape[1]:(h + 1) * shape[1]] for h in range(shape[0])],
                               axis=0)

    def body(vec_ref, mat_ref, norm_ref, *refs):
        ins, outs = refs[:3 * n + 4], refs[3 * n + 4:]
        for k in range(n):
            w_ref, m_ref, v_ref = ins[3 * k:3 * k + 3]
            g = grad_of(entries[k][0], w_ref.shape, vec_ref, mat_ref, norm_ref)
            d, m_, v_ = _adam_update(w_ref[...], g, m_ref[...], v_ref[...])
            for ref, val in zip(outs[4 * k:4 * k + 4], (g, d, m_, v_)):
                ref[...] = val
        w_ref, m_ref, v_ref, g_ref = ins[3 * n:]
        g = g_ref[...]
        for ref, val in zip(outs[4 * n:4 * n + 4], (g,) + _adam_update(w_ref[...], g, m_ref[...], v_ref[...])):
            ref[...] = val
        outs[4 * n + 4][...] = vec_ref[_bag_row("loss"), 0:1]

    arrays = [a for e in entries for a in e[1:]] + list(conv)
    out_shape = [jax.ShapeDtypeStruct(e[1].shape, F32) for e in entries for _ in range(4)]
    out_shape += [jax.ShapeDtypeStruct(conv[0].shape, F32)] * 4 + [jax.ShapeDtypeStruct((1, 1), F32)]
    return pl.pallas_call(
        body, name="adamw_replicated", out_shape=tuple(out_shape),
        compiler_params=pltpu.CompilerParams(vmem_limit_bytes=VMEM_LIMIT_BYTES),
    )(vec_sum, mat_sum, norm_grad, *arrays)


def _adamw(w, g, m, v, rows, name):
    r, c = w.shape

    def body(w_ref, g_ref, m_ref, v_ref, go_ref, d_ref, nm_ref, nv_ref):
        g = g_ref[...]
        go_ref[...] = g
        d_ref[...], nm_ref[...], nv_ref[...] = _adam_update(w_ref[...], g, m_ref[...], v_ref[...])

    spec = pl.BlockSpec((rows, c), lambda i: (i, 0))
    return pl.pallas_call(
        body, name=name, out_shape=tuple(jax.ShapeDtypeStruct((r, c), F32) for _ in range(4)),
        grid=(r // rows,), in_specs=[spec] * 4, out_specs=(spec,) * 4,
        compiler_params=pltpu.CompilerParams(dimension_semantics=("arbitrary",),
                                             vmem_limit_bytes=VMEM_LIMIT_BYTES),
    )(w, g, m, v)


def _adamw_group(items, name):
    n = 4 * len(items)

    def body(*refs):
        ins, outs, bufs = refs[:n], refs[n:2 * n], refs[2 * n:3 * n]
        load_sems, store_sems = refs[3 * n:]
        loads = [pltpu.make_async_copy(ins[j], bufs[j], load_sems.at[j]) for j in range(n)]
        stores = [pltpu.make_async_copy(bufs[j], outs[j], store_sems.at[j]) for j in range(n)]
        for cp in loads:
            cp.start()
        for k in range(len(items)):
            for cp in loads[4 * k:4 * k + 4]:
                cp.wait()
            w_buf, g_buf, m_buf, v_buf = bufs[4 * k:4 * k + 4]
            w_buf[...], m_buf[...], v_buf[...] = _adam_update(w_buf[...], g_buf[...], m_buf[...], v_buf[...])
            for cp in stores[4 * k:4 * k + 4]:
                cp.start()
        for cp in stores:
            cp.wait()

    arrays = [a for item in items for a in item]
    any_spec = pl.BlockSpec(memory_space=pl.ANY)
    flat = pl.pallas_call(
        body, name=name, out_shape=tuple(jax.ShapeDtypeStruct(a.shape, F32) for a in arrays),
        in_specs=[any_spec] * n, out_specs=(any_spec,) * n,
        scratch_shapes=[pltpu.VMEM(a.shape, F32) for a in arrays] + [pltpu.SemaphoreType.DMA((n,))] * 2,
        compiler_params=pltpu.CompilerParams(vmem_limit_bytes=VMEM_LIMIT_BYTES),
    )(*arrays)
    return [(flat[4 * k + 1], flat[4 * k], flat[4 * k + 2], flat[4 * k + 3]) for k in range(len(items))]


def _shift_down(ext, s):
    return pltpu.roll(ext, s, 0)


def _tile_shift(v, s):
    rows, cols = v.shape
    tiles = v.reshape(rows // F32_SUBLANES, F32_SUBLANES, cols)
    return pltpu.roll(tiles, s % F32_SUBLANES, 1).reshape(rows, cols)


def _shift_up(ext, s):
    return pltpu.roll(ext, ext.shape[0] - s, 0)


def _lru_gates(xc, wa_ref, ba, wx_ref, bx, lam):
    pa, px = [], []
    for h in range(LRU_HEADS):
        xh = xc[:, h * HEAD_DIM:(h + 1) * HEAD_DIM].astype(BF16)
        pa.append(_dot(xh, wa_ref[h]))
        px.append(_dot(xh, wx_ref[h]))
    r = _sigmoid(jnp.concatenate(pa, axis=1) + ba)
    ig = _sigmoid(jnp.concatenate(px, axis=1) + bx)
    sp = _softplus(-lam)
    log_a = (-LRU_C * r) * sp
    a = jnp.exp(log_a)
    mult = jnp.sqrt(jnp.tanh(-log_a) * (1.0 + a * a))
    return r, ig, a, mult, sp


def _conv(ext, w_ref, b):
    y = b + _shift_down(ext, 3) * w_ref[0:1, :]
    y = y + _shift_down(ext, 2) * w_ref[1:2, :]
    y = y + _shift_down(ext, 1) * w_ref[2:3, :]
    y = y + ext * w_ref[3:4, :]
    return y[CONV_HIST:, :]


def _pool_diff(ext, pos):
    out = []
    for g, k in enumerate(POOL_WINDOWS):
        col = ext[:, g * POOL_GROUP_DIM:(g + 1) * POOL_GROUP_DIM]
        s = col
        for step in range(g + 1):
            s = s + _shift_down(s, 2 ** step)
        count = jnp.minimum(pos + 1, k).astype(F32)
        out.append(s[POOL_HIST:, :] / count - col[POOL_HIST:, :])
    return out


def _pool_mix(diff, pw_ref):
    return jnp.concatenate([_dot(diff[g].astype(BF16), pw_ref[g]) for g in range(len(POOL_WINDOWS))], axis=1)


def _branch_specs(tb, row_map, fixed):
    fixed3 = lambda i: (0, 0, 0)
    return [pl.BlockSpec((CONV_WIDTH, D_MODEL), fixed), pl.BlockSpec((1, D_MODEL), fixed),
            pl.BlockSpec((LRU_HEADS, HEAD_DIM, HEAD_DIM), fixed3), pl.BlockSpec((1, D_MODEL), fixed),
            pl.BlockSpec((LRU_HEADS, HEAD_DIM, HEAD_DIM), fixed3), pl.BlockSpec((1, D_MODEL), fixed),
            pl.BlockSpec((1, D_MODEL), fixed),
            pl.BlockSpec((len(POOL_WINDOWS), POOL_GROUP_DIM, POOL_GROUP_DIM), fixed3),
            pl.BlockSpec((1, POOL_WIDTH), fixed)]


def _branches_fwd(z, weights, seq, tb, shards):
    t = z.shape[0]
    nb = t // tb
    nbe = seq // tb
    groups = tb // F32_SUBLANES
    n = len(shards)

    def body(xa_ref, ga_ref, xb_ref, gb_ref, cw_ref, cb_ref, wa_ref, ba_ref, wx_ref, bx_ref, lam_ref,
             pw_ref, ps_ref, *refs):
        g_ins = refs[:n]
        ya_ref, yb_ref, hl_ref = refs[n:n + 3]
        g_outs = refs[n + 3:2 * n + 3]
        xa_ext, xb_ext, carry, a_s, u_s, send_sems, recv_sems, local_sems = refs[2 * n + 3:]
        blk = pl.program_id(0) % nbe
        start_gather, relay_gather, finish_gather = _gather_steps(shards, g_ins, g_outs, send_sems, recv_sems,
                                                                  local_sems)
        pl.when(pl.program_id(0) == 0)(start_gather)
        pl.when(pl.program_id(0) == nb // 2)(relay_gather)

        @pl.when(blk == 0)
        def _():
            xa_ext[0:CONV_HIST, :] = jnp.zeros((CONV_HIST, D_MODEL), F32)
            xb_ext[0:POOL_HIST, :] = jnp.zeros((POOL_HIST, POOL_WIDTH), F32)
            carry[...] = jnp.zeros_like(carry)

        xa_ext[CONV_HIST:, :] = xa_ref[...]
        xb_ext[POOL_HIST:, :] = xb_ref[...]
        ea = xa_ext[...]
        eb = xb_ext[...]
        xa_ext[0:CONV_HIST, :] = ea[tb:, :]
        xb_ext[0:POOL_HIST, :] = eb[tb:, :]

        xc = _conv(ea, cw_ref, cb_ref[...])
        _, ig, a, mult, _ = _lru_gates(xc, wa_ref, ba_ref[...], wx_ref, bx_ref[...], lam_ref[...])
        u = mult * (ig * xc)
        row8 = lax.broadcasted_iota(jnp.int32, (tb, D_MODEL), 0) % F32_SUBLANES
        for s in (1, 2, 4):
            m = row8 >= s
            u = jnp.where(m, a * _tile_shift(u, s) + u, u)
            a = jnp.where(m, a * _tile_shift(a, s), a)
        a_s[...] = a
        u_s[...] = u

        def step(g, cr):
            sl = pl.ds(pl.multiple_of(g * F32_SUBLANES, F32_SUBLANES), F32_SUBLANES)
            hb = a_s[sl, :] * cr + u_s[sl, :]
            hl_ref[sl, :] = hb
            return jnp.broadcast_to(hb[F32_SUBLANES - 1:F32_SUBLANES, :], (F32_SUBLANES, D_MODEL))

        carry[...] = lax.fori_loop(0, groups, step, carry[...], unroll=4)
        ga = ga_ref[...]
        ya_ref[...] = (hl_ref[...] * (ga * _sigmoid(ga))).astype(BF16)

        pos = blk * tb + lax.broadcasted_iota(jnp.int32, (tb, POOL_GROUP_DIM), 0)
        ypre = _pool_mix(_pool_diff(eb, pos), pw_ref)
        gb = gb_ref[...]
        yb_ref[...] = ((ypre * ps_ref[...]) * (gb * _sigmoid(gb))).astype(BF16)
        pl.when(pl.program_id(0) == nb - 1)(finish_gather)

    row = lambda i: (i, 0)
    fixed = lambda i: (0, 0)
    any_spec = pl.BlockSpec(memory_space=pl.ANY)
    in_specs = [pl.BlockSpec((tb, D_MODEL), lambda i: (i, 0)), pl.BlockSpec((tb, D_MODEL), lambda i: (i, 1)),
                pl.BlockSpec((tb, POOL_WIDTH), lambda i: (i, 4)), pl.BlockSpec((tb, POOL_WIDTH), lambda i: (i, 5)),
                ] + _branch_specs(tb, row, fixed) + [any_spec] * n
    g_shape, g_sems = _gather_shapes(shards)
    return pl.pallas_call(
        body, name="branches_fwd",
        out_shape=tuple([jax.ShapeDtypeStruct((t, D_MODEL), BF16), jax.ShapeDtypeStruct((t, POOL_WIDTH), BF16),
                         jax.ShapeDtypeStruct((t, D_MODEL), F32)] + g_shape),
        grid=(nb,), in_specs=in_specs,
        out_specs=tuple([pl.BlockSpec((tb, D_MODEL), row), pl.BlockSpec((tb, POOL_WIDTH), row),
                         pl.BlockSpec((tb, D_MODEL), row)] + [any_spec] * n),
        scratch_shapes=[pltpu.VMEM((tb + CONV_HIST, D_MODEL), F32), pltpu.VMEM((tb + POOL_HIST, POOL_WIDTH), F32),
                        pltpu.VMEM((F32_SUBLANES, D_MODEL), F32),
                        pltpu.VMEM((tb, D_MODEL), F32), pltpu.VMEM((tb, D_MODEL), F32)] + g_sems,
        compiler_params=pltpu.CompilerParams(dimension_semantics=("arbitrary",),
                                             vmem_limit_bytes=VMEM_LIMIT_BYTES),
    )(z, z, z, z, *weights, *[sh[0] for sh in shards])


def _branches_bwd(z, hl, dya, dyb, dzm, weights, vec_bag, seq, tb):
    t = z.shape[0]
    nb = t // tb
    nbe = seq // tb
    groups = tb // F32_SUBLANES

    def body(xa_ref, xap_ref, ga_ref, xb_ref, xbp_ref, gb_ref, hl_ref, hlp_ref, dya_ref, dyb_ref, dzm_ref,
             cw_ref, cb_ref, wa_ref, ba_ref, wx_ref, bx_ref, lam_ref, pw_ref, ps_ref, vec_in_ref,
             dz_ref, vec_ref, mat_ref,
             xa_ext, xb_ext, hl_ext, a_ext, dxc_ext, dwin_ext, g_carry, b_s, d_s, g_s):
        i = pl.program_id(0)
        blk = (nb - 1 - i) % nbe

        def mat_rows(name, k):
            at = MAT_BAG_AT[name] + k * HEAD_DIM
            return slice(at, at + HEAD_DIM)

        @pl.when(i == 0)
        def _():
            vec_ref[...] = vec_in_ref[...]
            mat_ref[...] = jnp.zeros_like(mat_ref)

        @pl.when(blk == nbe - 1)
        def _():
            a_ext[tb:, :] = jnp.zeros((F32_SUBLANES, D_MODEL), F32)
            dxc_ext[tb:, :] = jnp.zeros((CONV_HIST, D_MODEL), F32)
            dwin_ext[tb:, :] = jnp.zeros((POOL_HIST, POOL_WIDTH), F32)
            g_carry[...] = jnp.zeros_like(g_carry)

        live = (blk > 0).astype(F32)
        xa_ext[0:CONV_HIST, :] = xap_ref[...] * live
        xa_ext[CONV_HIST:, :] = xa_ref[...]
        xb_ext[0:POOL_HIST, :] = xbp_ref[...] * live
        xb_ext[POOL_HIST:, :] = xb_ref[...]
        hl_ext[0:F32_SUBLANES, :] = hlp_ref[...] * live
        hl_ext[F32_SUBLANES:, :] = hl_ref[...]
        ea = xa_ext[...]
        eb = xb_ext[...]

        xc = _conv(ea, cw_ref, cb_ref[...])
        lam = lam_ref[...]
        r, ig, a, mult, sp = _lru_gates(xc, wa_ref, ba_ref[...], wx_ref, bx_ref[...], lam)
        hl = hl_ref[...]
        ga = ga_ref[...]
        sga = _sigmoid(ga)
        dya = dya_ref[...]
        dhl = dya * (ga * sga)
        dz_ref[:, D_MODEL:2 * D_MODEL] = (dya * hl * (sga * (1.0 + ga * (1.0 - sga)))).astype(BF16)

        a_ext[0:tb, :] = a
        b = _shift_up(a_ext[...], 1)[0:tb, :]
        a_ext[tb:, :] = jnp.broadcast_to(a[0:1, :], (F32_SUBLANES, D_MODEL))
        d = dhl
        row8 = lax.broadcasted_iota(jnp.int32, (tb, D_MODEL), 0) % F32_SUBLANES
        for s in (1, 2, 4):
            m = row8 < F32_SUBLANES - s
            d = jnp.where(m, d + b * _tile_shift(d, -s), d)
            b = jnp.where(m, b * _tile_shift(b, -s), b)
        b_s[...] = b
        d_s[...] = d

        def step(k, cr):
            sl = pl.ds(pl.multiple_of((groups - 1 - k) * F32_SUBLANES, F32_SUBLANES), F32_SUBLANES)
            gb_ = d_s[sl, :] + b_s[sl, :] * cr
            g_s[sl, :] = gb_
            return jnp.broadcast_to(gb_[0:1, :], (F32_SUBLANES, D_MODEL))

        g_carry[...] = lax.fori_loop(0, groups, step, g_carry[...], unroll=4)
        gsc = g_s[...]
        da = gsc * _shift_down(hl_ext[...], 1)[F32_SUBLANES:, :]
        dmult = gsc * (ig * xc)
        dig = gsc * (mult * xc)
        dxc = gsc * (mult * ig)
        dlog_a = da * a - (a * a) * dmult / mult
        dr = dlog_a * (-LRU_C * sp)
        vec_ref[_bag_row("lru_lambda"), :] += jnp.sum(dlog_a * (-LRU_C * r), axis=0, keepdims=True)
        dpa = dr * (r * (1.0 - r))
        dpx = dig * (ig * (1.0 - ig))
        vec_ref[_bag_row("lru_b_a"), :] += jnp.sum(dpa, axis=0, keepdims=True)
        vec_ref[_bag_row("lru_b_x"), :] += jnp.sum(dpx, axis=0, keepdims=True)
        back = []
        for h in range(LRU_HEADS):
            cols = slice(h * HEAD_DIM, (h + 1) * HEAD_DIM)
            xh = xc[:, cols].astype(BF16)
            dpa_h = dpa[:, cols].astype(BF16)
            dpx_h = dpx[:, cols].astype(BF16)
            mat_ref[mat_rows("lru_w_a", h), :] += _dot_tn(xh, dpa_h)
            mat_ref[mat_rows("lru_w_x", h), :] += _dot_tn(xh, dpx_h)
            back.append(_dot_nt(dpa_h, wa_ref[h]) + _dot_nt(dpx_h, wx_ref[h]))
        dxc = dxc + jnp.concatenate(back, axis=1)
        vec_ref[_bag_row("conv_b"), :] += jnp.sum(dxc, axis=0, keepdims=True)
        for k in range(CONV_WIDTH):
            tap = _shift_down(ea, CONV_WIDTH - 1 - k)[CONV_HIST:, :] if k < CONV_WIDTH - 1 else ea[CONV_HIST:, :]
            vec_ref[_bag_row("conv_w", k), :] += jnp.sum(dxc * tap, axis=0, keepdims=True)
        dxc_ext[0:tb, :] = dxc
        ed = dxc_ext[...]
        dxa = ed * cw_ref[3:4, :]
        dxa = dxa + _shift_up(ed, 1) * cw_ref[2:3, :]
        dxa = dxa + _shift_up(ed, 2) * cw_ref[1:2, :]
        dxa = dxa + _shift_up(ed, 3) * cw_ref[0:1, :]
        dz_ref[:, 0:D_MODEL] = dxa[0:tb, :].astype(BF16)
        dxc_ext[tb:, :] = dxc[0:CONV_HIST, :]

        pos = blk * tb + lax.broadcasted_iota(jnp.int32, (tb, POOL_GROUP_DIM), 0)
        diff = _pool_diff(eb, pos)
        ypre = _pool_mix(diff, pw_ref)
        ps = ps_ref[...]
        gb = gb_ref[...]
        sgb = _sigmoid(gb)
        dyb = dyb_ref[...]
        dyp = dyb * (gb * sgb)
        dz_ref[:, 2 * D_MODEL + POOL_WIDTH:3 * D_MODEL] = (
            dyb * (ypre * ps) * (sgb * (1.0 + gb * (1.0 - sgb)))).astype(BF16)
        vec_ref[_bag_row("pool_scale"), 0:POOL_WIDTH] += jnp.sum(dyp * ypre, axis=0, keepdims=True)
        dypre = dyp * ps
        for g, k in enumerate(POOL_WINDOWS):
            cols = slice(g * POOL_GROUP_DIM, (g + 1) * POOL_GROUP_DIM)
            dyg = dypre[:, cols].astype(BF16)
            mat_ref[mat_rows("pool_w", g), :] += _dot_tn(diff[g].astype(BF16), dyg)
            ddiff = _dot_nt(dyg, pw_ref[g])
            count = jnp.minimum(pos + 1, k).astype(F32)
            dwin = ddiff / count
            dwin_ext[0:tb, cols] = dwin
            s = dwin_ext[:, cols]
            for step_ in range(g + 1):
                s = s + _shift_up(s, 2 ** step_)
            dz_ref[:, 2 * D_MODEL + g * POOL_GROUP_DIM:2 * D_MODEL + (g + 1) * POOL_GROUP_DIM] = (
                s[0:tb, :] - ddiff).astype(BF16)
            dwin_ext[tb:, cols] = dwin[0:POOL_HIST, :]

        dz_ref[:, 3 * D_MODEL:] = dzm_ref[...]

        @pl.when(i == nb - 1)
        def _():
            row = _bag_row("lru_lambda")
            vec_ref[row, :] = vec_ref[row, :] * (-_sigmoid(-lam))

    rev = lambda i: (nb - 1 - i, 0)
    fixed = lambda i: (0, 0)

    def prev(rows, col):
        per = tb // rows
        return lambda i: (jnp.maximum((nb - 1 - i) * per - 1, 0), col)

    in_specs = [pl.BlockSpec((tb, D_MODEL), lambda i: (nb - 1 - i, 0)),
                pl.BlockSpec((CONV_HIST, D_MODEL), prev(CONV_HIST, 0)),
                pl.BlockSpec((tb, D_MODEL), lambda i: (nb - 1 - i, 1)),
                pl.BlockSpec((tb, POOL_WIDTH), lambda i: (nb - 1 - i, 4)),
                pl.BlockSpec((POOL_HIST, POOL_WIDTH), prev(POOL_HIST, 4)),
                pl.BlockSpec((tb, POOL_WIDTH), lambda i: (nb - 1 - i, 5)),
                pl.BlockSpec((tb, D_MODEL), rev),
                pl.BlockSpec((F32_SUBLANES, D_MODEL), prev(F32_SUBLANES, 0)),
                pl.BlockSpec((tb, D_MODEL), rev), pl.BlockSpec((tb, POOL_WIDTH), rev),
                pl.BlockSpec((tb, 2 * D_MODEL), rev)] + _branch_specs(tb, rev, fixed) + [
                    pl.BlockSpec((VEC_BAG_ROWS, D_MODEL), fixed)]
    out_shape = (jax.ShapeDtypeStruct((t, IN_COLS), BF16), jax.ShapeDtypeStruct((VEC_BAG_ROWS, D_MODEL), F32),
                 jax.ShapeDtypeStruct((MAT_BAG_ROWS, HEAD_DIM), F32))
    out_specs = (pl.BlockSpec((tb, IN_COLS), rev), pl.BlockSpec((VEC_BAG_ROWS, D_MODEL), fixed),
                 pl.BlockSpec((MAT_BAG_ROWS, HEAD_DIM), fixed))
    scratch = [pltpu.VMEM((tb + CONV_HIST, D_MODEL), F32), pltpu.VMEM((tb + POOL_HIST, POOL_WIDTH), F32),
               pltpu.VMEM((tb + F32_SUBLANES, D_MODEL), F32), pltpu.VMEM((tb + F32_SUBLANES, D_MODEL), F32),
               pltpu.VMEM((tb + CONV_HIST, D_MODEL), F32), pltpu.VMEM((tb + POOL_HIST, POOL_WIDTH), F32),
               pltpu.VMEM((F32_SUBLANES, D_MODEL), F32),
               pltpu.VMEM((tb, D_MODEL), F32), pltpu.VMEM((tb, D_MODEL), F32), pltpu.VMEM((tb, D_MODEL), F32)]
    return pl.pallas_call(
        body, name="branches_bwd", out_shape=out_shape, grid=(nb,), in_specs=in_specs, out_specs=out_specs,
        scratch_shapes=scratch, input_output_aliases={len(in_specs) - 1: 1},
        compiler_params=pltpu.CompilerParams(dimension_semantics=("arbitrary",),
                                             vmem_limit_bytes=VMEM_LIMIT_BYTES),
    )(z, z, z, z, z, z, hl, hl, dya, dyb, dzm, *weights, vec_bag)


def _merge_head(x2d, ya, yb, z, p2d, tgt, w_pl, w_pp, w_out, w_pg, w_pe, g2, gf, tb):
    t = x2d.shape[0]
    p_dim = p2d.shape[1]

    def body(x_ref, ya_ref, yb_ref, ma_ref, mb_ref, p_ref, t_ref, wpl_ref, wpp_ref, wout_ref, wpg_ref, wpe_ref,
             g2_ref, gf_ref,
             bag_ref, dxr_ref, dya_ref, dyb_ref, dzm_ref,
             mg_ref, do_ref, hn_ref, dgp_ref, dpe_ref, da_ref, dbm_ref, pbf_ref):
        @pl.when(pl.program_id(0) == 0)
        def _():
            bag_ref[...] = jnp.zeros_like(bag_ref)

        a_ = _dot(ya_ref[...], wpl_ref[...])
        bm = _dot(yb_ref[...], wpp_ref[...])
        sa = _sigmoid(ma_ref[...])
        sb = _sigmoid(mb_ref[...])
        mg = (sa * a_ + sb * bm).astype(BF16)
        mg_ref[...] = mg
        x1 = x_ref[...] + _dot(mg, wout_ref[...])
        xn2, r2 = _rms(x1)
        g2 = g2_ref[...]
        hn = (xn2 * g2).astype(BF16)
        hn_ref[...] = hn
        gate = _sigmoid(_dot(hn, wpg_ref[...]))
        pbf = p_ref[...].astype(BF16)
        pbf_ref[...] = pbf
        pe = _dot(pbf, wpe_ref[...])
        x2 = x1 + gate * pe
        xn3, r3 = _rms(x2)
        gf = gf_ref[...]
        err = xn3 * gf - t_ref[...]
        bag_ref[_bag_rows("loss"), 0:128] += 0.5 * jnp.sum(jnp.mean(err * err, axis=-1))

        dy = err * (1.0 / D_MODEL)
        bag_ref[_bag_row("final_g"), :] += jnp.sum(dy * xn3, axis=0, keepdims=True)
        dx2 = _rms_bwd(dy * gf, xn3, r3)
        dpe_ref[...] = (dx2 * gate).astype(BF16)
        dgp = ((dx2 * pe) * (gate * (1.0 - gate))).astype(BF16)
        dgp_ref[...] = dgp
        dhn = _dot_nt(dgp, wpg_ref[...])
        bag_ref[_bag_row("ple_norm_g"), :] += jnp.sum(dhn * xn2, axis=0, keepdims=True)
        dx1 = dx2 + _rms_bwd(dhn * g2, xn2, r2)
        dxr_ref[...] = dx1
        do = dx1.astype(BF16)
        do_ref[...] = do
        dmg = _dot_nt(do, wout_ref[...])
        da = (dmg * sa).astype(BF16)
        dbm = (dmg * sb).astype(BF16)
        da_ref[...] = da
        dbm_ref[...] = dbm
        dzm_ref[:, 0:D_MODEL] = (dmg * a_ * (sa * (1.0 - sa))).astype(BF16)
        dzm_ref[:, D_MODEL:] = (dmg * bm * (sb * (1.0 - sb))).astype(BF16)
        dya_ref[...] = _dot_nt(da, wpl_ref[...])
        dyb_ref[...] = _dot_nt(dbm, wpp_ref[...])

    row = lambda i: (i, 0)
    fixed = lambda i: (0, 0)

    def resident(shape):
        return pl.BlockSpec(shape, fixed, pipeline_mode=pl.Buffered(1))

    tok = lambda width: pl.BlockSpec((tb, width), row)
    in_specs = [tok(D_MODEL), tok(D_MODEL), tok(POOL_WIDTH),
                pl.BlockSpec((tb, D_MODEL), lambda i: (i, 3)), pl.BlockSpec((tb, D_MODEL), lambda i: (i, 4)),
                tok(p_dim), tok(D_MODEL),
                resident((D_MODEL, D_MODEL)), resident((POOL_WIDTH, D_MODEL)), resident((D_MODEL, D_MODEL)),
                resident((D_MODEL, D_MODEL)), resident((p_dim, D_MODEL)),
                pl.BlockSpec((1, D_MODEL), fixed), pl.BlockSpec((1, D_MODEL), fixed)]
    bf = lambda width: jax.ShapeDtypeStruct((t, width), BF16)
    f32 = lambda width: jax.ShapeDtypeStruct((t, width), F32)
    out_shape = (jax.ShapeDtypeStruct((VEC_BAG_ROWS, D_MODEL), F32),
                 f32(D_MODEL), f32(D_MODEL), f32(POOL_WIDTH), bf(2 * D_MODEL),
                 bf(D_MODEL), bf(D_MODEL), bf(D_MODEL), bf(D_MODEL), bf(D_MODEL), bf(D_MODEL), bf(D_MODEL), bf(p_dim))
    out_specs = (pl.BlockSpec((VEC_BAG_ROWS, D_MODEL), fixed),
                 tok(D_MODEL), tok(D_MODEL), tok(POOL_WIDTH), tok(2 * D_MODEL),
                 tok(D_MODEL), tok(D_MODEL), tok(D_MODEL), tok(D_MODEL), tok(D_MODEL), tok(D_MODEL), tok(D_MODEL),
                 tok(p_dim))
    return pl.pallas_call(
        body, name="merge_head", out_shape=out_shape, grid=(t // tb,), in_specs=in_specs, out_specs=out_specs,
        compiler_params=pltpu.CompilerParams(dimension_semantics=("arbitrary",),
                                             vmem_limit_bytes=VMEM_LIMIT_BYTES),
    )(x2d, ya, yb, z, z, p2d, tgt, w_pl, w_pp, w_out, w_pg, w_pe, g2, gf)


def kernel(x, p, norm_g, w_in, conv_w, conv_b, lru_w_a, lru_b_a, lru_w_x, lru_b_x, lru_lambda, pool_w, pool_scale, w_proj_lru, w_proj_pool, w_out, ple_norm_g, w_ple_gate, w_ple_proj, final_g, loss_target, m_norm_g, m_w_in, m_conv_w, m_conv_b, m_lru_w_a, m_lru_b_a, m_lru_w_x, m_lru_b_x, m_lru_lambda, m_pool_w, m_pool_scale, m_w_proj_lru, m_w_proj_pool, m_w_out, m_ple_norm_g, m_w_ple_gate, m_w_ple_proj, m_final_g, v_norm_g, v_w_in, v_conv_w, v_conv_b, v_lru_w_a, v_lru_b_a, v_lru_w_x, v_lru_b_x, v_lru_lambda, v_pool_w, v_pool_scale, v_w_proj_lru, v_w_proj_pool, v_w_out, v_ple_norm_g, v_w_ple_gate, v_w_ple_proj, v_final_g):
    bsz, seq, _ = x.shape
    t = bsz * seq
    tb_mm = min(1024, seq)
    tb_seq = min(256, seq // 2) if seq >= 512 else seq
    x2d = x.reshape(t, D_MODEL)
    p2d = p.reshape(t, p.shape[-1])
    tgt = loss_target.reshape(t, D_MODEL)
    chip = 2 * lax.axis_index("x") + lax.axis_index("y")

    rest = [(w_proj_lru[0], 0), (w_proj_pool[0], 1), (w_out[0], 0), (w_ple_gate[0], 0), (w_ple_proj[0], 1)]
    z, h_bf, w_in_f, conv_w_f = _in_proj_gather(x2d, norm_g, w_in[0].astype(BF16), [(conv_w[0], 1, False)], tb_mm)

    wa_bf = lru_w_a[0].astype(BF16)
    wx_bf = lru_w_x[0].astype(BF16)
    pw_bf = pool_w[0].astype(BF16)
    branch_w = (conv_w_f, conv_b, wa_bf, lru_b_a.reshape(1, D_MODEL), wx_bf, lru_b_x.reshape(1, D_MODEL),
                lru_lambda, pw_bf, pool_scale)

    ya, yb, hl, w_pl_f, w_pp_f, w_out_f, w_pg_f, w_pe_f = _branches_fwd(
        z, branch_w, seq, tb_seq, [(w.astype(BF16), axis, True) for w, axis in rest])
    (vec_bag, dx_res, dya, dyb, dzm, mg_bf, do_bf, hn_bf, dgp_bf, dpe_bf, da_bf, dbm_bf, p_bf) = _merge_head(
        x2d, ya, yb, z, p2d, tgt, w_pl_f, w_pp_f, w_out_f, w_pg_f, w_pe_f, ple_norm_g, final_g.reshape(1, D_MODEL),
        tb_seq)
    dz, vec_bag, mat_bag = _branches_bwd(z, hl, dya, dyb, dzm, branch_w, vec_bag, seq, tb_seq)

    tb_dw = min(1024, seq)
    def proj_grad(lhs, rhs, name, cols):
        g32, g16 = _weight_grad(lhs, rhs, 1, tb_dw, name)
        if cols:
            return g32[0], True, g16[0]
        rows = g32.shape[1] // 8
        return g32.reshape(8, rows, g32.shape[2]), False, g16.reshape(8, rows, g32.shape[2])

    p_dim = p2d.shape[1]
    proj_parts = [proj_grad(ya, da_bf, "dw_proj_lru", False), proj_grad(yb, dbm_bf, "dw_proj_pool", True),
                  proj_grad(mg_bf, do_bf, "dw_out", False), proj_grad(hn_bf, dgp_bf, "dw_ple_gate", False),
                  proj_grad(p_bf, dpe_bf, "dw_ple_proj", True)]
    nb_dw = t // tb_dw
    g_in, g_in16, r_pl, r_pp, r_out, r_pg, r_pe, vec_mine, mat_mine = _weight_grad(
        h_bf, dz, N_CHIPS, tb_dw, "dw_in",
        reduce=(proj_parts + [(vec_bag.reshape(8, VEC_BAG_ROWS // 8, D_MODEL), False, None),
                              (mat_bag.reshape(8, MAT_BAG_ROWS // 8, HEAD_DIM), False, None)],
                [BF16] * 5 + [F32] * 2,
                (0, nb_dw // 2, 2 * nb_dw - 1, 3 * nb_dw + nb_dw // 2, N_CHIPS * nb_dw - 1)))
    pieces = (8, D_MODEL // 2, IN_COLS // N_CHIPS)
    nb_seq = t // tb_seq
    dx, g_g1, r_in, vec_sum, mat_sum = _in_proj_bwd(
        dz, w_in_f, x2d, dx_res, norm_g, tb_seq,
        reduce=([(g_in.reshape(pieces), False, g_in16.reshape(pieces))], BF16,
                (0, nb_seq // 8, nb_seq // 2, nb_seq - 1, nb_seq - 1)),
        shards=[(vec_mine.reshape(VEC_BAG_ROWS // N_CHIPS, D_MODEL), 0, True),
                (mat_mine.reshape(MAT_BAG_ROWS // N_CHIPS, HEAD_DIM), 0, True)])

    u_in = tuple(a[None] for a in _adamw(w_in[0], r_in.reshape(D_MODEL, IN_COLS // N_CHIPS), m_w_in[0], v_w_in[0],
                                         D_MODEL // 4, "adamw_w_in"))
    proj = [(w_proj_lru, r_pl, m_w_proj_lru, v_w_proj_lru), (w_proj_pool, r_pp, m_w_proj_pool, v_w_proj_pool),
            (w_out, r_out, m_w_out, v_w_out), (w_ple_gate, r_pg, m_w_ple_gate, v_w_ple_gate),
            (w_ple_proj, r_pe, m_w_ple_proj, v_w_ple_proj)]
    u_pl, u_pp, u_out, u_pg, u_pe = [tuple(a[None] for a in u) for u in _adamw_group(
        [(w[0], g.reshape(w.shape[1:]), m[0], v[0]) for w, g, m, v in proj], "adamw_proj")]

    small = [("norm_g", norm_g, m_norm_g, v_norm_g), ("conv_b", conv_b, m_conv_b, v_conv_b),
             ("lru_w_a", lru_w_a, m_lru_w_a, v_lru_w_a), ("lru_b_a", lru_b_a, m_lru_b_a, v_lru_b_a),
             ("lru_w_x", lru_w_x, m_lru_w_x, v_lru_w_x), ("lru_b_x", lru_b_x, m_lru_b_x, v_lru_b_x),
             ("lru_lambda", lru_lambda, m_lru_lambda, v_lru_lambda), ("pool_w", pool_w, m_pool_w, v_pool_w),
             ("pool_scale", pool_scale, m_pool_scale, v_pool_scale),
             ("ple_norm_g", ple_norm_g, m_ple_norm_g, v_ple_norm_g), ("final_g", final_g, m_final_g, v_final_g)]

    def view(a):
        return a.reshape(-1, a.shape[-1]) if a.ndim != 3 else a[0]

    cw_at = F32_SUBLANES * VEC_BAG_SLOTS.index("conv_w")
    cw_cols = D_MODEL // N_CHIPS
    g_cw = lax.dynamic_slice(vec_sum, (cw_at, chip * cw_cols), (CONV_WIDTH, cw_cols))
    flat = _adamw_replicated(vec_sum, mat_sum, g_g1, [(name,) + tuple(view(a) for a in arrs) for name, *arrs in small],
                             (conv_w[0], m_conv_w[0], v_conv_w[0], g_cw))
    u_small = {name: tuple(flat[4 * k + pick].reshape(arrs[0].shape) for pick in range(4))
               for k, (name, *arrs) in enumerate(small)}
    u_cw = tuple(a[None] for a in flat[4 * len(small):4 * len(small) + 4])

    loss = flat[-1].reshape(())
    grad_x = dx.reshape(bsz, seq, D_MODEL)

    def ordered(pick):
        s = {name: u[pick] for name, u in u_small.items()}
        return [s["norm_g"], u_in[pick], u_cw[pick], s["conv_b"], s["lru_w_a"], s["lru_b_a"], s["lru_w_x"], s["lru_b_x"],
                s["lru_lambda"], s["pool_w"], s["pool_scale"], u_pl[pick], u_pp[pick], u_out[pick], s["ple_norm_g"],
                u_pg[pick], u_pe[pick], s["final_g"]]

    return (loss, grad_x, *ordered(0), *ordered(1), *ordered(2), *ordered(3))
```

```python
import jax
import jax.numpy as jnp
from jax import lax
from jax.experimental import pallas as pl
from jax.experimental.pallas import tpu as pltpu

F32 = jnp.float32
BF16 = jnp.bfloat16
MESH = pl.DeviceIdType.MESH

D_MODEL = 1024
LRU_HEADS = 8
HEAD_DIM = 128
CONV_WIDTH = 4
LRU_C = 8.0
POOL_WIDTH = 512
POOL_WINDOWS = (2, 4, 8, 16)
POOL_GROUP_DIM = 128
IN_COLS = 5120
N_CHIPS = 4
EPS = 1e-6

ADAM_LR = 0.001
ADAM_B1 = 0.9
ADAM_B2 = 0.999
ADAM_EPS = 1e-08
ADAM_WD = 0.01
ADAM_STEP = 10

F32_SUBLANES = 8
CONV_HIST = 8
POOL_HIST = 16
VMEM_LIMIT_BYTES = 58 * 1024 * 1024
VEC_BAG_SLOTS = ("norm_g", "conv_w", "conv_b", "lru_b_a", "lru_b_x", "lru_lambda", "pool_scale", "ple_norm_g",
                 "final_g", "loss")
VEC_BAG_ROWS = 128
MAT_BAG_AT = {"lru_w_a": 0, "lru_w_x": LRU_HEADS * HEAD_DIM, "pool_w": 2 * LRU_HEADS * HEAD_DIM}
MAT_BAG_ROWS = 2 * LRU_HEADS * HEAD_DIM + len(POOL_WINDOWS) * POOL_GROUP_DIM


def _bag_row(name, k=0):
    at = F32_SUBLANES * VEC_BAG_SLOTS.index(name) + k
    return slice(at, at + 1)


def _bag_rows(name):
    at = F32_SUBLANES * VEC_BAG_SLOTS.index(name)
    return slice(at, at + F32_SUBLANES)


def _dot(a, b):
    return jnp.dot(a, b, preferred_element_type=F32)


def _dot_nt(a, b):
    return lax.dot_general(a, b, (((1,), (1,)), ((), ())), preferred_element_type=F32)


def _dot_tn(a, b):
    return lax.dot_general(a, b, (((0,), (0,)), ((), ())), preferred_element_type=F32)


def _sigmoid(v):
    return jax.nn.sigmoid(v)


def _softplus(v):
    return jnp.maximum(v, 0.0) + jnp.log1p(jnp.exp(-jnp.abs(v)))


def _place():
    return lax.axis_index("x"), lax.axis_index("y"), lax.axis_index("c")


GATHER_SEMS = 6


def _gather_shapes(shards):
    out_shape = []
    for arr, axis, _ in shards:
        r, cols = arr.shape
        out_shape.append(jax.ShapeDtypeStruct((N_CHIPS * r, cols) if axis == 0 else (r, N_CHIPS * cols), arr.dtype))
    n = len(shards)
    sems = [pltpu.SemaphoreType.DMA((n * GATHER_SEMS,)), pltpu.SemaphoreType.DMA((n * GATHER_SEMS,)),
            pltpu.SemaphoreType.DMA((n,))]
    return out_shape, sems


def _gather_steps(shards, ins, outs, send_sems, recv_sems, local_sems):
    n = len(shards)
    x, y, c = _place()
    me, sibling = (x, y, c), (x, y, 1 - c)
    chips = [(x, 1 - y), (1 - x, y), (1 - x, 1 - y)]

    def region(k, cx, cy, hc):
        (r, cols), axis = shards[k][0].shape, shards[k][1]
        j = 2 * cx + cy
        if axis == 0:
            if hc is None:
                return outs[k].at[pl.ds(j * r, r), :]
            return outs[k].at[pl.ds(j * r + hc * (r // 2), r // 2), :]
        if hc is None:
            return outs[k].at[:, pl.ds(j * cols, cols)]
        return outs[k].at[pl.ds(hc * (r // 2), r // 2), pl.ds(j * cols, cols)]

    def remote(k, sem, block, to, src=None):
        dst = region(k, *block)
        return pltpu.make_async_remote_copy(
            src_ref=dst if src is None else src, dst_ref=dst,
            send_sem=send_sems.at[k * GATHER_SEMS + sem], recv_sem=recv_sems.at[k * GATHER_SEMS + sem],
            device_id=to, device_id_type=MESH)

    def first(k, idx):
        r, split = shards[k][0].shape[0], shards[k][2]
        src = ins[k].at[pl.ds(c * (r // 2), r // 2), :] if split else ins[k]
        return remote(k, idx, (x, y, c if split else None), (*chips[idx], c), src=src)

    def relay(k):
        src_chip = (jnp.bitwise_xor(x, 1 - c), jnp.bitwise_xor(y, c))
        dst_chip = (jnp.bitwise_xor(x, c), jnp.bitwise_xor(y, 1 - c))
        return remote(k, 2, (*src_chip, c), (*dst_chip, c))

    def passed(k, idx):
        return remote(k, 3 + idx, (*chips[idx], c), sibling)

    def mine(k):
        return pltpu.make_async_copy(ins[k], region(k, x, y, None), local_sems.at[k])

    def start():
        for k in range(n):
            mine(k).start()
            for idx in range(2 if shards[k][2] else 3):
                first(k, idx).start()

    def relay_on():
        for k in range(n):
            split = shards[k][2]
            for idx in range(2):
                remote(k, idx, (*chips[idx], c if split else None), me).wait_recv()
            if split:
                relay(k).start()
                passed(k, 0).start()
                passed(k, 1).start()

    def finish():
        for k in range(n):
            split = shards[k][2]
            remote(k, 2, (*chips[2], c if split else None), me).wait_recv()
            if split:
                passed(k, 2).start()
        for k in range(n):
            if shards[k][2]:
                for idx in range(3):
                    remote(k, 3 + idx, (*chips[idx], 1 - c), me).wait_recv()
        for k in range(n):
            if shards[k][2]:
                for cp in (first(k, 0), first(k, 1), relay(k), passed(k, 0), passed(k, 1), passed(k, 2)):
                    cp.wait_send()
            else:
                for idx in range(3):
                    first(k, idx).wait_send()
            mine(k).wait()

    return start, relay_on, finish


RS_ADD_ROWS = (64, 32, 16, 8)


N_DEV = 2 * N_CHIPS


def _all_reduce_scratch(shape):
    return [pltpu.VMEM((N_DEV,) + tuple(shape), F32), pltpu.SemaphoreType.DMA((N_DEV - 1,)),
            pltpu.SemaphoreType.DMA((N_DEV - 1,))]


def _all_reduce_tile(v_ref, o_ref, slots, send_sems, recv_sems):
    flips = [(dx, dy, dc) for dx in (0, 1) for dy in (0, 1) for dc in (0, 1)][1:]
    x, y, c = _place()
    mine = 4 * x + 2 * y + c

    def copy(k, to_flip, slot):
        dx, dy, dc = to_flip
        peer = (jnp.bitwise_xor(x, dx), jnp.bitwise_xor(y, dy), jnp.bitwise_xor(c, dc))
        return pltpu.make_async_remote_copy(
            src_ref=v_ref, dst_ref=slots.at[slot], send_sem=send_sems.at[k], recv_sem=recv_sems.at[k],
            device_id=peer, device_id_type=MESH)

    sends = [copy(k, flip, mine) for k, flip in enumerate(flips)]
    for cp in sends:
        cp.start()
    slots[mine] = v_ref[...]
    for k, (dx, dy, dc) in enumerate(flips):
        copy(k, (dx, dy, dc), jnp.bitwise_xor(mine, 4 * dx + 2 * dy + dc)).wait_recv()
    total = slots[0]
    for d in range(1, N_DEV):
        total = total + slots[d]
    o_ref[...] = total
    for cp in sends:
        cp.wait_send()


RS_SEMS = 8
RS_LOCAL_SEMS = 5


def _rs_piece_shape(part):
    arr, cols = part[0], part[1]
    return (arr.shape[0] // 2, arr.shape[1] // N_CHIPS) if cols else tuple(arr.shape[1:])


def _rs_operands(parts):
    return [p[0] for p in parts] + [p[0] if p[2] is None else p[2] for p in parts]


def _rs_wires(parts, wire):
    return list(wire) if isinstance(wire, (list, tuple)) else [wire] * len(parts)


def _rs_shapes(parts, wire):
    n = len(parts)
    shapes = [_rs_piece_shape(p) for p in parts]
    out_shape = [jax.ShapeDtypeStruct((2,) + s, F32) for s in shapes]
    scratch = []
    for lead, kind in ((N_CHIPS, "f32"), (N_CHIPS, "narrow"), (N_CHIPS, "wire"), (None, "f32"), (N_CHIPS, "wire")):
        for s, p, w in zip(shapes, parts, _rs_wires(parts, wire)):
            dtype = {"f32": F32, "narrow": F32 if p[2] is None else p[2].dtype, "wire": w}[kind]
            scratch.append(pltpu.VMEM(s if lead is None else (lead,) + s, dtype))
    scratch += [pltpu.SemaphoreType.DMA((n * RS_SEMS,)), pltpu.SemaphoreType.DMA((n * RS_SEMS,)),
                pltpu.SemaphoreType.DMA((n * RS_LOCAL_SEMS,))]
    return out_shape, scratch


def _rs_steps(parts, ins, outs, scratch):
    n = len(parts)
    own, sib, got, fin, snd = (scratch[k * n:(k + 1) * n] for k in range(5))
    send_sems, recv_sems, local_sems = scratch[5 * n:]
    shapes = [_rs_piece_shape(p) for p in parts]
    x, y, c = _place()
    j_me = 2 * x + y
    me, sibling = (x, y, c), (x, y, 1 - c)

    def piece(a, jj, core, narrow=False):
        ref = ins[n + a] if narrow else ins[a]
        if parts[a][1]:
            r, cl = shapes[a]
            return ref.at[pl.ds(core * r, r), pl.ds(jj * cl, cl)]
        return ref.at[2 * jj + core]

    def remote(a, sem, src, dst, to):
        return pltpu.make_async_remote_copy(
            src_ref=src, dst_ref=dst, send_sem=send_sems.at[a * RS_SEMS + sem],
            recv_sem=recv_sems.at[a * RS_SEMS + sem], device_id=to, device_id_type=MESH)

    def rows_loop(a, fn):
        r = shapes[a][0]
        step = max(s for s in RS_ADD_ROWS if r % s == 0)

        def it(i, carry):
            fn(pl.ds(pl.multiple_of(i * step, step), step))
            return carry

        lax.fori_loop(0, r // step, it, 0)

    def load(a, jj):
        return pltpu.make_async_copy(piece(a, jj, c), own[a].at[jj], local_sems.at[a * RS_LOCAL_SEMS + jj])

    def to_sibling(a, jj):
        return remote(a, jj, piece(a, jj, 1 - c, narrow=True), sib[a].at[jj], sibling)

    near = (jnp.bitwise_xor(x, 1 - c), jnp.bitwise_xor(y, c))
    far = (jnp.bitwise_xor(x, c), jnp.bitwise_xor(y, 1 - c))
    diag = (1 - x, 1 - y)
    FROM_NEAR, FROM_FAR, FEED = 0, 1, 2

    def chip_of(chip):
        return 2 * chip[0] + chip[1]

    def feed(a):
        return remote(a, 4, snd[a].at[chip_of(diag)], got[a].at[FEED], (*near, c))

    def to_near(a):
        return remote(a, 5, snd[a].at[chip_of(near)], got[a].at[FROM_NEAR], (*near, c))

    def to_far(a):
        return remote(a, 6, snd[a].at[chip_of(far)], got[a].at[FROM_FAR], (*far, c))

    def store(a):
        return pltpu.make_async_copy(fin[a], outs[a].at[c], local_sems.at[a * RS_LOCAL_SEMS + 4])

    def result_to_sibling(a):
        return remote(a, 7, fin[a], outs[a].at[c], sibling)

    def exchange():
        for a in range(n):
            for jj in range(N_CHIPS):
                load(a, jj).start()
                to_sibling(a, jj).start()

    def chip_sums():
        for a in range(n):
            for jj in range(N_CHIPS):
                load(a, jj).wait()
                remote(a, jj, sib[a].at[jj], sib[a].at[jj], me).wait_recv()

                def add(sl, a=a, jj=jj):
                    q = own[a][jj, sl, :] + sib[a][jj, sl, :].astype(F32)
                    own[a][jj, sl, :] = q
                    snd[a][jj, sl, :] = q.astype(snd[a].dtype)

                rows_loop(a, add)
        for a in range(n):
            feed(a).start()
        for a in range(n):
            to_near(a).start()

    def relay():
        for a in range(n):
            remote(a, 4, got[a].at[FEED], got[a].at[FEED], me).wait_recv()

            def add(sl, a=a):
                pair = own[a][chip_of(far), sl, :] + got[a][FEED, sl, :].astype(F32)
                snd[a][chip_of(far), sl, :] = pair.astype(snd[a].dtype)

            rows_loop(a, add)
            to_far(a).start()

    def totals():
        for a in range(n):
            remote(a, 5, got[a].at[FROM_NEAR], got[a].at[FROM_NEAR], me).wait_recv()
            remote(a, 6, got[a].at[FROM_FAR], got[a].at[FROM_FAR], me).wait_recv()

            def total(sl, a=a):
                fin[a][sl, :] = (own[a][j_me, sl, :] + got[a][FROM_NEAR, sl, :].astype(F32)) + (
                    got[a][FROM_FAR, sl, :].astype(F32))

            rows_loop(a, total)
            store(a).start()
            result_to_sibling(a).start()

    def finish():
        for a in range(n):
            remote(a, 7, outs[a].at[1 - c], outs[a].at[1 - c], me).wait_recv()
        for a in range(n):
            for jj in range(N_CHIPS):
                to_sibling(a, jj).wait_send()
            for cp in (feed(a), to_near(a), to_far(a), result_to_sibling(a)):
                cp.wait_send()
            store(a).wait()

    return exchange, chip_sums, relay, totals, finish


def _rms(x):
    r = lax.rsqrt(jnp.mean(x * x, axis=-1, keepdims=True) + EPS)
    return x * r, r


def _rms_bwd(dxn, xn, r):
    return r * (dxn - xn * jnp.mean(dxn * xn, axis=-1, keepdims=True))


def _in_proj_gather(x2d, norm_g, w_in_sh, shards, tb):
    t = x2d.shape[0]
    nb = t // tb
    cols = IN_COLS // N_CHIPS
    half = D_MODEL // 2
    n = len(shards)

    def body(x_ref, g_ref, win_ref, *refs):
        ins = refs[:n]
        z_ref, h_ref, wfull_ref = refs[n:n + 3]
        outs = refs[n + 3:2 * n + 3]
        wv, h_all, send_sems, recv_sems, local_sems, w_send, w_recv, w_local = refs[2 * n + 3:]
        s, i = pl.program_id(0), pl.program_id(1)
        x, y, c = _place()
        me, sibling = (x, y, c), (x, y, 1 - c)
        chips = [(x, 1 - y), (1 - x, y), (1 - x, 1 - y)]

        def w_half(cx, cy, hc):
            return wv.at[2 * cx + cy, pl.ds(hc * half, half), :]

        def w_remote(sem, block, to, src=None):
            dst = w_half(*block)
            return pltpu.make_async_remote_copy(
                src_ref=dst if src is None else src, dst_ref=dst, send_sem=w_send.at[sem],
                recv_sem=w_recv.at[sem], device_id=to, device_id_type=MESH)

        def w_first(idx):
            return w_remote(idx, (x, y, c), (*chips[idx], c), src=win_ref.at[pl.ds(c * half, half), :])

        def w_relay():
            src_chip = (jnp.bitwise_xor(x, 1 - c), jnp.bitwise_xor(y, c))
            dst_chip = (jnp.bitwise_xor(x, c), jnp.bitwise_xor(y, 1 - c))
            return w_remote(2, (*src_chip, c), (*dst_chip, c))

        def w_pass(idx):
            return w_remote(3 + idx, (*chips[idx], c), sibling)

        def w_store(k, cx, cy):
            jj = 2 * cx + cy
            return pltpu.make_async_copy(wv.at[jj], wfull_ref.at[:, pl.ds(jj * cols, cols)], w_local.at[k])

        start_rest, relay_rest, finish_rest = _gather_steps(shards, ins, outs, send_sems, recv_sems, local_sems)
        own = pltpu.make_async_copy(win_ref, wv.at[2 * x + y], w_local.at[4])

        @pl.when((s == 0) & (i == 0))
        def _():
            own.start()
            w_first(0).start()
            w_first(1).start()
            start_rest()
            own.wait()
            w_store(0, x, y).start()

        @pl.when((s == 1) & (i == 0))
        def _():
            w_remote(0, (*chips[0], c), me).wait_recv()
            w_remote(1, (*chips[1], c), me).wait_recv()
            w_relay().start()
            w_pass(0).start()
            w_pass(1).start()
            w_remote(3, (*chips[0], 1 - c), me).wait_recv()
            w_store(1, *chips[0]).start()

        @pl.when((s == 2) & (i == 0))
        def _():
            w_remote(4, (*chips[1], 1 - c), me).wait_recv()
            w_store(2, *chips[1]).start()

        @pl.when((s == 3) & (i == 0))
        def _():
            w_remote(2, (*chips[2], c), me).wait_recv()
            w_pass(2).start()
            w_remote(5, (*chips[2], 1 - c), me).wait_recv()
            w_store(3, *chips[2]).start()

        keep_h = pltpu.make_async_copy(h_all.at[i], h_ref.at[pl.ds(pl.multiple_of(i * tb, tb), tb), :], w_local.at[5])

        @pl.when(s == 0)
        def _():
            xn, _ = _rms(x_ref[...])
            h_all[i] = (xn * g_ref[...]).astype(BF16)
            keep_h.start()

        z_ref[...] = _dot(h_all[i], wv[jnp.bitwise_xor(2 * x + y, s)])
        pl.when(s == 0)(keep_h.wait)

        @pl.when((s == N_CHIPS - 1) & (i == nb - 1))
        def _():
            relay_rest()
            finish_rest()
            for cp in (w_first(0), w_first(1), w_relay(), w_pass(0), w_pass(1), w_pass(2)):
                cp.wait_send()
            w_store(0, x, y).wait()
            for idx in range(3):
                w_store(idx + 1, *chips[idx]).wait()

    rest_shape, rest_sems = _gather_shapes(shards)
    out_shape = [jax.ShapeDtypeStruct((t, IN_COLS), F32), jax.ShapeDtypeStruct((t, D_MODEL), BF16),
                 jax.ShapeDtypeStruct((D_MODEL, IN_COLS), BF16)] + rest_shape
    any_spec = pl.BlockSpec(memory_space=pl.ANY)

    def z_map(s, i):
        return (i, jnp.bitwise_xor(2 * lax.axis_index("x") + lax.axis_index("y"), s))

    return pl.pallas_call(
        body, name="in_proj", out_shape=tuple(out_shape),
        grid=(N_CHIPS, nb),
        in_specs=[pl.BlockSpec((tb, D_MODEL), lambda s, i: (jnp.where(s == 0, i, nb - 1), 0)),
                  pl.BlockSpec((1, D_MODEL), lambda s, i: (0, 0)), any_spec] + [any_spec] * n,
        out_specs=tuple([pl.BlockSpec((tb, cols), z_map), any_spec, any_spec] + [any_spec] * n),
        scratch_shapes=[pltpu.VMEM((N_CHIPS, D_MODEL, cols), BF16), pltpu.VMEM((nb, tb, D_MODEL), BF16)] + rest_sems + [
            pltpu.SemaphoreType.DMA((GATHER_SEMS,)), pltpu.SemaphoreType.DMA((GATHER_SEMS,)),
            pltpu.SemaphoreType.DMA((N_CHIPS + 2,))],
        compiler_params=pltpu.CompilerParams(dimension_semantics=("arbitrary", "arbitrary"),
                                             vmem_limit_bytes=VMEM_LIMIT_BYTES),
    )(x2d, norm_g, w_in_sh, *[sh[0] for sh in shards])


def _in_proj_bwd(dz, w_in, x2d, dx_res, norm_g, tb, reduce, shards):
    t = x2d.shape[0]
    nb = t // tb
    parts, wire, steps = reduce
    n = len(parts)
    k = len(shards)

    def body(dz_ref, w_ref, x_ref, dres_ref, g_ref, *refs):
        at = 2 * n + k
        dx_ref, dg_ref = refs[at:at + 2]
        rs_outs, g_outs = refs[at + 2:at + 2 + n], refs[at + 2 + n:at + 2 + n + k]
        scratch = refs[at + 2 + n + k:]
        rs_scr, g_sems, dg_acc, ar_scr = scratch[:-7], scratch[-7:-4], scratch[-4], scratch[-3:]
        rs = _rs_steps(parts, refs[:2 * n], rs_outs, rs_scr)
        for step, when in zip(rs, steps):
            pl.when(pl.program_id(0) == when)(step)
        gather = _gather_steps(shards, refs[2 * n:at], g_outs, *g_sems)
        for step, when in zip(gather, (0, nb // 2, nb - 1)):
            pl.when(pl.program_id(0) == when)(step)

        @pl.when(pl.program_id(0) == 0)
        def _():
            dg_acc[...] = jnp.zeros_like(dg_acc)

        xn, r = _rms(x_ref[...])
        g = g_ref[...]
        dh = _dot_nt(dz_ref[...], w_ref[...])
        dg_acc[0:1, :] += jnp.sum(dh * xn, axis=0, keepdims=True)
        dx_ref[...] = dres_ref[...] + _rms_bwd(dh * g, xn, r)

        @pl.when(pl.program_id(0) == nb - 1)
        def _():
            _all_reduce_tile(dg_acc, dg_ref, *ar_scr)

    row = lambda i: (i, 0)
    fixed = lambda i: (0, 0)
    rs_shape, rs_scratch = _rs_shapes(parts, wire)
    g_shape, g_sems = _gather_shapes(shards)
    any_spec = pl.BlockSpec(memory_space=pl.ANY)
    return pl.pallas_call(
        body, name="in_proj_bwd",
        out_shape=tuple([jax.ShapeDtypeStruct((t, D_MODEL), F32), jax.ShapeDtypeStruct((F32_SUBLANES, D_MODEL), F32)]
                        + rs_shape + g_shape),
        grid=(nb,),
        in_specs=[pl.BlockSpec((tb, IN_COLS), row),
                  pl.BlockSpec((D_MODEL, IN_COLS), fixed, pipeline_mode=pl.Buffered(1)),
                  pl.BlockSpec((tb, D_MODEL), row), pl.BlockSpec((tb, D_MODEL), row),
                  pl.BlockSpec((1, D_MODEL), fixed)] + [any_spec] * (2 * n + k),
        out_specs=tuple([pl.BlockSpec((tb, D_MODEL), row), pl.BlockSpec((F32_SUBLANES, D_MODEL), fixed)]
                        + [any_spec] * (n + k)),
        scratch_shapes=rs_scratch + g_sems + [pltpu.VMEM((F32_SUBLANES, D_MODEL), F32)] + _all_reduce_scratch(
            (F32_SUBLANES, D_MODEL)),
        compiler_params=pltpu.CompilerParams(dimension_semantics=("arbitrary",),
                                             vmem_limit_bytes=VMEM_LIMIT_BYTES),
    )(dz, w_in, x2d, dx_res, norm_g, *_rs_operands(parts), *[sh[0] for sh in shards])


def _weight_grad(lhs, rhs, n_chunks, tb, name, reduce=None):
    t, k = lhs.shape
    nc = rhs.shape[1] // n_chunks
    nb = t // tb
    parts, wire, steps = reduce if reduce is not None else ([], F32, ())
    n = len(parts)

    def body(l_ref, r_ref, *refs):
        o_ref, o16_ref = refs[2 * n:2 * n + 2]
        if n:
            at = pl.program_id(0) * nb + pl.program_id(1)
            rs = _rs_steps(parts, refs[:2 * n], refs[2 * n + 2:3 * n + 2], refs[3 * n + 2:])
            for step, when in zip(rs, steps):
                pl.when(at == when)(step)

        @pl.when(pl.program_id(1) == 0)
        def _():
            o_ref[...] = jnp.zeros_like(o_ref)

        o_ref[...] += _dot_tn(l_ref[...], r_ref[...])

        @pl.when(pl.program_id(1) == nb - 1)
        def _():
            o16_ref[...] = o_ref[...].astype(BF16)

    rs_shape, rs_scratch = _rs_shapes(parts, wire) if n else ([], [])
    any_spec = pl.BlockSpec(memory_space=pl.ANY)
    chunk = pl.BlockSpec((None, k, nc), lambda j, i: (j, 0, 0))
    return pl.pallas_call(
        body, name=name,
        out_shape=tuple([jax.ShapeDtypeStruct((n_chunks, k, nc), F32), jax.ShapeDtypeStruct((n_chunks, k, nc), BF16)]
                        + rs_shape),
        grid=(n_chunks, nb),
        in_specs=[pl.BlockSpec((tb, k), lambda j, i: (i, 0)), pl.BlockSpec((tb, nc), lambda j, i: (i, j))]
        + [any_spec] * (2 * n),
        out_specs=tuple([chunk, chunk] + [any_spec] * n),
        scratch_shapes=rs_scratch,
        compiler_params=pltpu.CompilerParams(dimension_semantics=("arbitrary", "arbitrary"),
                                             vmem_limit_bytes=VMEM_LIMIT_BYTES),
    )(lhs, rhs, *_rs_operands(parts))


def _adam_update(w, g, m, v):
    m_ = ADAM_B1 * m + (1.0 - ADAM_B1) * g
    v_ = ADAM_B2 * v + (1.0 - ADAM_B2) * jnp.square(g)
    m_hat = m_ / (1.0 - ADAM_B1 ** ADAM_STEP)
    v_hat = v_ / (1.0 - ADAM_B2 ** ADAM_STEP)
    return -ADAM_LR * (m_hat / (jnp.sqrt(v_hat) + ADAM_EPS) + ADAM_WD * w), m_, v_


def _adamw_replicated(vec_sum, mat_sum, norm_grad, entries, conv):
    n = len(entries)

    def grad_of(name, shape, vec_ref, mat_ref, norm_ref):
        if name == "norm_g":
            return norm_ref[0:1, :]
        if name in MAT_BAG_AT:
            return mat_ref[MAT_BAG_AT[name]:MAT_BAG_AT[name] + shape[0], :]
        if shape[0] == 1:
            return vec_ref[_bag_row(name), 0:shape[1]]
        return jnp.concatenate([vec_ref[_bag_row(name), h * shape[1]:(h + 1) * shape[1]] for h in range(shape[0])],
                               axis=0)

    def body(vec_ref, mat_ref, norm_ref, *refs):
        ins, outs = refs[:3 * n + 4], refs[3 * n + 4:]
        for k in range(n):
            w_ref, m_ref, v_ref = ins[3 * k:3 * k + 3]
            g = grad_of(entries[k][0], w_ref.shape, vec_ref, mat_ref, norm_ref)
            d, m_, v_ = _adam_update(w_ref[...], g, m_ref[...], v_ref[...])
            for ref, val in zip(outs[4 * k:4 * k + 4], (g, d, m_, v_)):
                ref[...] = val
        w_ref, m_ref, v_ref, g_ref = ins[3 * n:]
        g = g_ref[...]
        for ref, val in zip(outs[4 * n:4 * n + 4], (g,) + _adam_update(w_ref[...], g, m_ref[...], v_ref[...])):
            ref[...] = val
        outs[4 * n + 4][...] = vec_ref[_bag_row("loss"), 0:1]

    arrays = [a for e in entries for a in e[1:]] + list(conv)
    out_shape = [jax.ShapeDtypeStruct(e[1].shape, F32) for e in entries for _ in range(4)]
    out_shape += [jax.ShapeDtypeStruct(conv[0].shape, F32)] * 4 + [jax.ShapeDtypeStruct((1, 1), F32)]
    return pl.pallas_call(
        body, name="adamw_replicated", out_shape=tuple(out_shape),
        compiler_params=pltpu.CompilerParams(vmem_limit_bytes=VMEM_LIMIT_BYTES),
    )(vec_sum, mat_sum, norm_grad, *arrays)


def _adamw(w, g, m, v, rows, name):
    r, c = w.shape

    def body(w_ref, g_ref, m_ref, v_ref, go_ref, d_ref, nm_ref, nv_ref):
        g = g_ref[...]
        go_ref[...] = g
        d_ref[...], nm_ref[...], nv_ref[...] = _adam_update(w_ref[...], g, m_ref[...], v_ref[...])

    spec = pl.BlockSpec((rows, c), lambda i: (i, 0))
    return pl.pallas_call(
        body, name=name, out_shape=tuple(jax.ShapeDtypeStruct((r, c), F32) for _ in range(4)),
        grid=(r // rows,), in_specs=[spec] * 4, out_specs=(spec,) * 4,
        compiler_params=pltpu.CompilerParams(dimension_semantics=("arbitrary",),
                                             vmem_limit_bytes=VMEM_LIMIT_BYTES),
    )(w, g, m, v)


def _adamw_group(items, name):
    n = 4 * len(items)

    def body(*refs):
        ins, outs, bufs = refs[:n], refs[n:2 * n], refs[2 * n:3 * n]
        load_sems, store_sems = refs[3 * n:]
        loads = [pltpu.make_async_copy(ins[j], bufs[j], load_sems.at[j]) for j in range(n)]
        stores = [pltpu.make_async_copy(bufs[j], outs[j], store_sems.at[j]) for j in range(n)]
        for cp in loads:
            cp.start()
        for k in range(len(items)):
            for cp in loads[4 * k:4 * k + 4]:
                cp.wait()
            w_buf, g_buf, m_buf, v_buf = bufs[4 * k:4 * k + 4]
            w_buf[...], m_buf[...], v_buf[...] = _adam_update(w_buf[...], g_buf[...], m_buf[...], v_buf[...])
            for cp in stores[4 * k:4 * k + 4]:
                cp.start()
        for cp in stores:
            cp.wait()

    arrays = [a for item in items for a in item]
    any_spec = pl.BlockSpec(memory_space=pl.ANY)
    flat = pl.pallas_call(
        body, name=name, out_shape=tuple(jax.ShapeDtypeStruct(a.shape, F32) for a in arrays),
        in_specs=[any_spec] * n, out_specs=(any_spec,) * n,
        scratch_shapes=[pltpu.VMEM(a.shape, F32) for a in arrays] + [pltpu.SemaphoreType.DMA((n,))] * 2,
        compiler_params=pltpu.CompilerParams(vmem_limit_bytes=VMEM_LIMIT_BYTES),
    )(*arrays)
    return [(flat[4 * k + 1], flat[4 * k], flat[4 * k + 2], flat[4 * k + 3]) for k in range(len(items))]


def _shift_down(ext, s):
    return pltpu.roll(ext, s, 0)


def _tile_shift(v, s):
    rows, cols = v.shape
    tiles = v.reshape(rows // F32_SUBLANES, F32_SUBLANES, cols)
    return pltpu.roll(tiles, s % F32_SUBLANES, 1).reshape(rows, cols)


def _shift_up(ext, s):
    return pltpu.roll(ext, ext.shape[0] - s, 0)


def _lru_gates(xc, wa_ref, ba, wx_ref, bx, lam):
    pa, px = [], []
    for h in range(LRU_HEADS):
        xh = xc[:, h * HEAD_DIM:(h + 1) * HEAD_DIM].astype(BF16)
        pa.append(_dot(xh, wa_ref[h]))
        px.append(_dot(xh, wx_ref[h]))
    r = _sigmoid(jnp.concatenate(pa, axis=1) + ba)
    ig = _sigmoid(jnp.concatenate(px, axis=1) + bx)
    sp = _softplus(-lam)
    log_a = (-LRU_C * r) * sp
    a = jnp.exp(log_a)
    mult = jnp.sqrt(jnp.tanh(-log_a) * (1.0 + a * a))
    return r, ig, a, mult, sp


def _conv(ext, w_ref, b):
    y = b + _shift_down(ext, 3) * w_ref[0:1, :]
    y = y + _shift_down(ext, 2) * w_ref[1:2, :]
    y = y + _shift_down(ext, 1) * w_ref[2:3, :]
    y = y + ext * w_ref[3:4, :]
    return y[CONV_HIST:, :]


def _pool_diff(ext, pos):
    out = []
    for g, k in enumerate(POOL_WINDOWS):
        col = ext[:, g * POOL_GROUP_DIM:(g + 1) * POOL_GROUP_DIM]
        s = col
        for step in range(g + 1):
            s = s + _shift_down(s, 2 ** step)
        count = jnp.minimum(pos + 1, k).astype(F32)
        out.append(s[POOL_HIST:, :] / count - col[POOL_HIST:, :])
    return out


def _pool_mix(diff, pw_ref):
    return jnp.concatenate([_dot(diff[g].astype(BF16), pw_ref[g]) for g in range(len(POOL_WINDOWS))], axis=1)


def _branch_specs(tb, row_map, fixed):
    fixed3 = lambda i: (0, 0, 0)
    return [pl.BlockSpec((CONV_WIDTH, D_MODEL), fixed), pl.BlockSpec((1, D_MODEL), fixed),
            pl.BlockSpec((LRU_HEADS, HEAD_DIM, HEAD_DIM), fixed3), pl.BlockSpec((1, D_MODEL), fixed),
            pl.BlockSpec((LRU_HEADS, HEAD_DIM, HEAD_DIM), fixed3), pl.BlockSpec((1, D_MODEL), fixed),
            pl.BlockSpec((1, D_MODEL), fixed),
            pl.BlockSpec((len(POOL_WINDOWS), POOL_GROUP_DIM, POOL_GROUP_DIM), fixed3),
            pl.BlockSpec((1, POOL_WIDTH), fixed)]


def _branches_fwd(z, weights, seq, tb, shards):
    t = z.shape[0]
    nb = t // tb
    nbe = seq // tb
    groups = tb // F32_SUBLANES
    n = len(shards)

    def body(xa_ref, ga_ref, xb_ref, gb_ref, cw_ref, cb_ref, wa_ref, ba_ref, wx_ref, bx_ref, lam_ref,
             pw_ref, ps_ref, *refs):
        g_ins = refs[:n]
        ya_ref, yb_ref, hl_ref = refs[n:n + 3]
        g_outs = refs[n + 3:2 * n + 3]
        xa_ext, xb_ext, carry, a_s, u_s, send_sems, recv_sems, local_sems = refs[2 * n + 3:]
        blk = pl.program_id(0) % nbe
        start_gather, relay_gather, finish_gather = _gather_steps(shards, g_ins, g_outs, send_sems, recv_sems,
                                                                  local_sems)
        pl.when(pl.program_id(0) == 0)(start_gather)
        pl.when(pl.program_id(0) == nb // 2)(relay_gather)

        @pl.when(blk == 0)
        def _():
            xa_ext[0:CONV_HIST, :] = jnp.zeros((CONV_HIST, D_MODEL), F32)
            xb_ext[0:POOL_HIST, :] = jnp.zeros((POOL_HIST, POOL_WIDTH), F32)
            carry[...] = jnp.zeros_like(carry)

        xa_ext[CONV_HIST:, :] = xa_ref[...]
        xb_ext[POOL_HIST:, :] = xb_ref[...]
        ea = xa_ext[...]
        eb = xb_ext[...]
        xa_ext[0:CONV_HIST, :] = ea[tb:, :]
        xb_ext[0:POOL_HIST, :] = eb[tb:, :]

        xc = _conv(ea, cw_ref, cb_ref[...])
        _, ig, a, mult, _ = _lru_gates(xc, wa_ref, ba_ref[...], wx_ref, bx_ref[...], lam_ref[...])
        u = mult * (ig * xc)
        row8 = lax.broadcasted_iota(jnp.int32, (tb, D_MODEL), 0) % F32_SUBLANES
        for s in (1, 2, 4):
            m = row8 >= s
            u = jnp.where(m, a * _tile_shift(u, s) + u, u)
            a = jnp.where(m, a * _tile_shift(a, s), a)
        a_s[...] = a
        u_s[...] = u

        def step(g, cr):
            sl = pl.ds(pl.multiple_of(g * F32_SUBLANES, F32_SUBLANES), F32_SUBLANES)
            hb = a_s[sl, :] * cr + u_s[sl, :]
            hl_ref[sl, :] = hb
            return jnp.broadcast_to(hb[F32_SUBLANES - 1:F32_SUBLANES, :], (F32_SUBLANES, D_MODEL))

        carry[...] = lax.fori_loop(0, groups, step, carry[...], unroll=4)
        ga = ga_ref[...]
        ya_ref[...] = (hl_ref[...] * (ga * _sigmoid(ga))).astype(BF16)

        pos = blk * tb + lax.broadcasted_iota(jnp.int32, (tb, POOL_GROUP_DIM), 0)
        ypre = _pool_mix(_pool_diff(eb, pos), pw_ref)
        gb = gb_ref[...]
        yb_ref[...] = ((ypre * ps_ref[...]) * (gb * _sigmoid(gb))).astype(BF16)
        pl.when(pl.program_id(0) == nb - 1)(finish_gather)

    row = lambda i: (i, 0)
    fixed = lambda i: (0, 0)
    any_spec = pl.BlockSpec(memory_space=pl.ANY)
    in_specs = [pl.BlockSpec((tb, D_MODEL), lambda i: (i, 0)), pl.BlockSpec((tb, D_MODEL), lambda i: (i, 1)),
                pl.BlockSpec((tb, POOL_WIDTH), lambda i: (i, 4)), pl.BlockSpec((tb, POOL_WIDTH), lambda i: (i, 5)),
                ] + _branch_specs(tb, row, fixed) + [any_spec] * n
    g_shape, g_sems = _gather_shapes(shards)
    return pl.pallas_call(
        body, name="branches_fwd",
        out_shape=tuple([jax.ShapeDtypeStruct((t, D_MODEL), BF16), jax.ShapeDtypeStruct((t, POOL_WIDTH), BF16),
                         jax.ShapeDtypeStruct((t, D_MODEL), F32)] + g_shape),
        grid=(nb,), in_specs=in_specs,
        out_specs=tuple([pl.BlockSpec((tb, D_MODEL), row), pl.BlockSpec((tb, POOL_WIDTH), row),
                         pl.BlockSpec((tb, D_MODEL), row)] + [any_spec] * n),
        scratch_shapes=[pltpu.VMEM((tb + CONV_HIST, D_MODEL), F32), pltpu.VMEM((tb + POOL_HIST, POOL_WIDTH), F32),
                        pltpu.VMEM((F32_SUBLANES, D_MODEL), F32),
                        pltpu.VMEM((tb, D_MODEL), F32), pltpu.VMEM((tb, D_MODEL), F32)] + g_sems,
        compiler_params=pltpu.CompilerParams(dimension_semantics=("arbitrary",),
                                             vmem_limit_bytes=VMEM_LIMIT_BYTES),
    )(z, z, z, z, *weights, *[sh[0] for sh in shards])


def _branches_bwd(z, hl, dya, dyb, dzm, weights, vec_bag, seq, tb, riders):
    t = z.shape[0]
    nb = t // tb
    nbe = seq // tb
    groups = tb // F32_SUBLANES
    nr = len(riders)

    def body(xa_ref, xap_ref, ga_ref, xb_ref, xbp_ref, gb_ref, hl_ref, hlp_ref, dya_ref, dyb_ref, dzm_ref,
             cw_ref, cb_ref, wa_ref, ba_ref, wx_ref, bx_ref, lam_ref, pw_ref, ps_ref, vec_in_ref, *rest):
        pairs, (dz_ref, vec_ref, mat_ref), grads = rest[:2 * nr], rest[2 * nr:2 * nr + 3], rest[2 * nr + 3:4 * nr + 3]
        xa_ext, xb_ext, hl_ext, a_ext, dxc_ext, dwin_ext, g_carry, b_s, d_s, g_s = rest[4 * nr + 3:]
        i = pl.program_id(0)
        blk = (nb - 1 - i) % nbe

        def mat_rows(name, k):
            at = MAT_BAG_AT[name] + k * HEAD_DIM
            return slice(at, at + HEAD_DIM)

        def rider(k):
            grads[2 * k][...] += _dot_tn(pairs[2 * k][...], pairs[2 * k + 1][...])

        @pl.when(i == 0)
        def _():
            vec_ref[...] = vec_in_ref[...]
            mat_ref[...] = jnp.zeros_like(mat_ref)
            for k in range(nr):
                grads[2 * k][...] = jnp.zeros_like(grads[2 * k])

        @pl.when(blk == nbe - 1)
        def _():
            a_ext[tb:, :] = jnp.zeros((F32_SUBLANES, D_MODEL), F32)
            dxc_ext[tb:, :] = jnp.zeros((CONV_HIST, D_MODEL), F32)
            dwin_ext[tb:, :] = jnp.zeros((POOL_HIST, POOL_WIDTH), F32)
            g_carry[...] = jnp.zeros_like(g_carry)

        live = (blk > 0).astype(F32)
        xa_ext[0:CONV_HIST, :] = xap_ref[...] * live
        xa_ext[CONV_HIST:, :] = xa_ref[...]
        xb_ext[0:POOL_HIST, :] = xbp_ref[...] * live
        xb_ext[POOL_HIST:, :] = xb_ref[...]
        hl_ext[0:F32_SUBLANES, :] = hlp_ref[...] * live
        hl_ext[F32_SUBLANES:, :] = hl_ref[...]
        ea = xa_ext[...]
        eb = xb_ext[...]
        rider(0)

        xc = _conv(ea, cw_ref, cb_ref[...])
        lam = lam_ref[...]
        r, ig, a, mult, sp = _lru_gates(xc, wa_ref, ba_ref[...], wx_ref, bx_ref[...], lam)
        hl = hl_ref[...]
        ga = ga_ref[...]
        sga = _sigmoid(ga)
        dya = dya_ref[...]
        dhl = dya * (ga * sga)
        dz_ref[:, D_MODEL:2 * D_MODEL] = (dya * hl * (sga * (1.0 + ga * (1.0 - sga)))).astype(BF16)

        a_ext[0:tb, :] = a
        b = _shift_up(a_ext[...], 1)[0:tb, :]
        a_ext[tb:, :] = jnp.broadcast_to(a[0:1, :], (F32_SUBLANES, D_MODEL))
        d = dhl
        row8 = lax.broadcasted_iota(jnp.int32, (tb, D_MODEL), 0) % F32_SUBLANES
        for s in (1, 2, 4):
            m = row8 < F32_SUBLANES - s
            d = jnp.where(m, d + b * _tile_shift(d, -s), d)
            b = jnp.where(m, b * _tile_shift(b, -s), b)
        b_s[...] = b
        d_s[...] = d

        def step(k, cr):
            sl = pl.ds(pl.multiple_of((groups - 1 - k) * F32_SUBLANES, F32_SUBLANES), F32_SUBLANES)
            gb_ = d_s[sl, :] + b_s[sl, :] * cr
            g_s[sl, :] = gb_
            return jnp.broadcast_to(gb_[0:1, :], (F32_SUBLANES, D_MODEL))

        g_carry[...] = lax.fori_loop(0, groups, step, g_carry[...], unroll=4)
        rider(1)
        gsc = g_s[...]
        da = gsc * _shift_down(hl_ext[...], 1)[F32_SUBLANES:, :]
        dmult = gsc * (ig * xc)
        dig = gsc * (mult * xc)
        dxc = gsc * (mult * ig)
        dlog_a = da * a - (a * a) * dmult / mult
        dr = dlog_a * (-LRU_C * sp)
        vec_ref[_bag_row("lru_lambda"), :] += jnp.sum(dlog_a * (-LRU_C * r), axis=0, keepdims=True)
        dpa = dr * (r * (1.0 - r))
        dpx = dig * (ig * (1.0 - ig))
        vec_ref[_bag_row("lru_b_a"), :] += jnp.sum(dpa, axis=0, keepdims=True)
        vec_ref[_bag_row("lru_b_x"), :] += jnp.sum(dpx, axis=0, keepdims=True)
        back = []
        for h in range(LRU_HEADS):
            cols = slice(h * HEAD_DIM, (h + 1) * HEAD_DIM)
            xh = xc[:, cols].astype(BF16)
            dpa_h = dpa[:, cols].astype(BF16)
            dpx_h = dpx[:, cols].astype(BF16)
            mat_ref[mat_rows("lru_w_a", h), :] += _dot_tn(xh, dpa_h)
            mat_ref[mat_rows("lru_w_x", h), :] += _dot_tn(xh, dpx_h)
            back.append(_dot_nt(dpa_h, wa_ref[h]) + _dot_nt(dpx_h, wx_ref[h]))
        dxc = dxc + jnp.concatenate(back, axis=1)
        vec_ref[_bag_row("conv_b"), :] += jnp.sum(dxc, axis=0, keepdims=True)
        for k in range(CONV_WIDTH):
            tap = _shift_down(ea, CONV_WIDTH - 1 - k)[CONV_HIST:, :] if k < CONV_WIDTH - 1 else ea[CONV_HIST:, :]
            vec_ref[_bag_row("conv_w", k), :] += jnp.sum(dxc * tap, axis=0, keepdims=True)
        dxc_ext[0:tb, :] = dxc
        ed = dxc_ext[...]
        dxa = ed * cw_ref[3:4, :]
        dxa = dxa + _shift_up(ed, 1) * cw_ref[2:3, :]
        dxa = dxa + _shift_up(ed, 2) * cw_ref[1:2, :]
        dxa = dxa + _shift_up(ed, 3) * cw_ref[0:1, :]
        dz_ref[:, 0:D_MODEL] = dxa[0:tb, :].astype(BF16)
        dxc_ext[tb:, :] = dxc[0:CONV_HIST, :]

        pos = blk * tb + lax.broadcasted_iota(jnp.int32, (tb, POOL_GROUP_DIM), 0)
        diff = _pool_diff(eb, pos)
        rider(2)
        ypre = _pool_mix(diff, pw_ref)
        ps = ps_ref[...]
        gb = gb_ref[...]
        sgb = _sigmoid(gb)
        dyb = dyb_ref[...]
        dyp = dyb * (gb * sgb)
        dz_ref[:, 2 * D_MODEL + POOL_WIDTH:3 * D_MODEL] = (
            dyb * (ypre * ps) * (sgb * (1.0 + gb * (1.0 - sgb)))).astype(BF16)
        vec_ref[_bag_row("pool_scale"), 0:POOL_WIDTH] += jnp.sum(dyp * ypre, axis=0, keepdims=True)
        dypre = dyp * ps
        for g, k in enumerate(POOL_WINDOWS):
            cols = slice(g * POOL_GROUP_DIM, (g + 1) * POOL_GROUP_DIM)
            dyg = dypre[:, cols].astype(BF16)
            mat_ref[mat_rows("pool_w", g), :] += _dot_tn(diff[g].astype(BF16), dyg)
            ddiff = _dot_nt(dyg, pw_ref[g])
            count = jnp.minimum(pos + 1, k).astype(F32)
            dwin = ddiff / count
            dwin_ext[0:tb, cols] = dwin
            s = dwin_ext[:, cols]
            for step_ in range(g + 1):
                s = s + _shift_up(s, 2 ** step_)
            dz_ref[:, 2 * D_MODEL + g * POOL_GROUP_DIM:2 * D_MODEL + (g + 1) * POOL_GROUP_DIM] = (
                s[0:tb, :] - ddiff).astype(BF16)
            dwin_ext[tb:, cols] = dwin[0:POOL_HIST, :]

        dz_ref[:, 3 * D_MODEL:] = dzm_ref[...]

        @pl.when(i == nb - 1)
        def _():
            row = _bag_row("lru_lambda")
            vec_ref[row, :] = vec_ref[row, :] * (-_sigmoid(-lam))
            for k in range(nr):
                grads[2 * k + 1][...] = grads[2 * k][...].astype(BF16)

    rev = lambda i: (nb - 1 - i, 0)
    fixed = lambda i: (0, 0)

    def prev(rows, col):
        per = tb // rows
        return lambda i: (jnp.maximum((nb - 1 - i) * per - 1, 0), col)

    in_specs = [pl.BlockSpec((tb, D_MODEL), lambda i: (nb - 1 - i, 0)),
                pl.BlockSpec((CONV_HIST, D_MODEL), prev(CONV_HIST, 0)),
                pl.BlockSpec((tb, D_MODEL), lambda i: (nb - 1 - i, 1)),
                pl.BlockSpec((tb, POOL_WIDTH), lambda i: (nb - 1 - i, 4)),
                pl.BlockSpec((POOL_HIST, POOL_WIDTH), prev(POOL_HIST, 4)),
                pl.BlockSpec((tb, POOL_WIDTH), lambda i: (nb - 1 - i, 5)),
                pl.BlockSpec((tb, D_MODEL), rev),
                pl.BlockSpec((F32_SUBLANES, D_MODEL), prev(F32_SUBLANES, 0)),
                pl.BlockSpec((tb, D_MODEL), rev), pl.BlockSpec((tb, POOL_WIDTH), rev),
                pl.BlockSpec((tb, 2 * D_MODEL), rev)] + _branch_specs(tb, rev, fixed) + [
                    pl.BlockSpec((VEC_BAG_ROWS, D_MODEL), fixed)]
    vec_at = len(in_specs) - 1
    out_shape = [jax.ShapeDtypeStruct((t, IN_COLS), BF16), jax.ShapeDtypeStruct((VEC_BAG_ROWS, D_MODEL), F32),
                 jax.ShapeDtypeStruct((MAT_BAG_ROWS, HEAD_DIM), F32)]
    out_specs = [pl.BlockSpec((tb, IN_COLS), rev), pl.BlockSpec((VEC_BAG_ROWS, D_MODEL), fixed),
                 pl.BlockSpec((MAT_BAG_ROWS, HEAD_DIM), fixed)]
    for lhs, rhs in riders:
        in_specs += [pl.BlockSpec((tb, lhs.shape[1]), rev), pl.BlockSpec((tb, rhs.shape[1]), rev)]
        grad = (lhs.shape[1], rhs.shape[1])
        out_shape += [jax.ShapeDtypeStruct(grad, F32), jax.ShapeDtypeStruct(grad, BF16)]
        out_specs += [pl.BlockSpec(grad, fixed)] * 2
    scratch = [pltpu.VMEM((tb + CONV_HIST, D_MODEL), F32), pltpu.VMEM((tb + POOL_HIST, POOL_WIDTH), F32),
               pltpu.VMEM((tb + F32_SUBLANES, D_MODEL), F32), pltpu.VMEM((tb + F32_SUBLANES, D_MODEL), F32),
               pltpu.VMEM((tb + CONV_HIST, D_MODEL), F32), pltpu.VMEM((tb + POOL_HIST, POOL_WIDTH), F32),
               pltpu.VMEM((F32_SUBLANES, D_MODEL), F32),
               pltpu.VMEM((tb, D_MODEL), F32), pltpu.VMEM((tb, D_MODEL), F32), pltpu.VMEM((tb, D_MODEL), F32)]
    return pl.pallas_call(
        body, name="branches_bwd", out_shape=tuple(out_shape), grid=(nb,), in_specs=in_specs,
        out_specs=tuple(out_specs), scratch_shapes=scratch, input_output_aliases={vec_at: 1},
        compiler_params=pltpu.CompilerParams(dimension_semantics=("arbitrary",),
                                             vmem_limit_bytes=VMEM_LIMIT_BYTES),
    )(z, z, z, z, z, z, hl, hl, dya, dyb, dzm, *weights, vec_bag, *[a for pair in riders for a in pair])


def _merge_head(x2d, ya, yb, z, p2d, tgt, w_pl, w_pp, w_out, w_pg, w_pe, g2, gf, tb):
    t = x2d.shape[0]
    p_dim = p2d.shape[1]

    def body(x_ref, ya_ref, yb_ref, ma_ref, mb_ref, p_ref, t_ref, wpl_ref, wpp_ref, wout_ref, wpg_ref, wpe_ref,
             g2_ref, gf_ref,
             bag_ref, dxr_ref, dya_ref, dyb_ref, dzm_ref,
             mg_ref, do_ref, hn_ref, dgp_ref, dpe_ref, da_ref, dbm_ref, pbf_ref):
        @pl.when(pl.program_id(0) == 0)
        def _():
            bag_ref[...] = jnp.zeros_like(bag_ref)

        a_ = _dot(ya_ref[...], wpl_ref[...])
        bm = _dot(yb_ref[...], wpp_ref[...])
        sa = _sigmoid(ma_ref[...])
        sb = _sigmoid(mb_ref[...])
        mg = (sa * a_ + sb * bm).astype(BF16)
        mg_ref[...] = mg
        x1 = x_ref[...] + _dot(mg, wout_ref[...])
        xn2, r2 = _rms(x1)
        g2 = g2_ref[...]
        hn = (xn2 * g2).astype(BF16)
        hn_ref[...] = hn
        gate = _sigmoid(_dot(hn, wpg_ref[...]))
        pbf = p_ref[...].astype(BF16)
        pbf_ref[...] = pbf
        pe = _dot(pbf, wpe_ref[...])
        x2 = x1 + gate * pe
        xn3, r3 = _rms(x2)
        gf = gf_ref[...]
        err = xn3 * gf - t_ref[...]
        bag_ref[_bag_rows("loss"), 0:128] += 0.5 * jnp.sum(jnp.mean(err * err, axis=-1))

        dy = err * (1.0 / D_MODEL)
        bag_ref[_bag_row("final_g"), :] += jnp.sum(dy * xn3, axis=0, keepdims=True)
        dx2 = _rms_bwd(dy * gf, xn3, r3)
        dpe_ref[...] = (dx2 * gate).astype(BF16)
        dgp = ((dx2 * pe) * (gate * (1.0 - gate))).astype(BF16)
        dgp_ref[...] = dgp
        dhn = _dot_nt(dgp, wpg_ref[...])
        bag_ref[_bag_row("ple_norm_g"), :] += jnp.sum(dhn * xn2, axis=0, keepdims=True)
        dx1 = dx2 + _rms_bwd(dhn * g2, xn2, r2)
        dxr_ref[...] = dx1
        do = dx1.astype(BF16)
        do_ref[...] = do
        dmg = _dot_nt(do, wout_ref[...])
        da = (dmg * sa).astype(BF16)
        dbm = (dmg * sb).astype(BF16)
        da_ref[...] = da
        dbm_ref[...] = dbm
        dzm_ref[:, 0:D_MODEL] = (dmg * a_ * (sa * (1.0 - sa))).astype(BF16)
        dzm_ref[:, D_MODEL:] = (dmg * bm * (sb * (1.0 - sb))).astype(BF16)
        dya_ref[...] = _dot_nt(da, wpl_ref[...])
        dyb_ref[...] = _dot_nt(dbm, wpp_ref[...])

    row = lambda i: (i, 0)
    fixed = lambda i: (0, 0)

    def resident(shape):
        return pl.BlockSpec(shape, fixed, pipeline_mode=pl.Buffered(1))

    tok = lambda width: pl.BlockSpec((tb, width), row)
    in_specs = [tok(D_MODEL), tok(D_MODEL), tok(POOL_WIDTH),
                pl.BlockSpec((tb, D_MODEL), lambda i: (i, 3)), pl.BlockSpec((tb, D_MODEL), lambda i: (i, 4)),
                tok(p_dim), tok(D_MODEL),
                resident((D_MODEL, D_MODEL)), resident((POOL_WIDTH, D_MODEL)), resident((D_MODEL, D_MODEL)),
                resident((D_MODEL, D_MODEL)), resident((p_dim, D_MODEL)),
                pl.BlockSpec((1, D_MODEL), fixed), pl.BlockSpec((1, D_MODEL), fixed)]
    bf = lambda width: jax.ShapeDtypeStruct((t, width), BF16)
    f32 = lambda width: jax.ShapeDtypeStruct((t, width), F32)
    out_shape = (jax.ShapeDtypeStruct((VEC_BAG_ROWS, D_MODEL), F32),
                 f32(D_MODEL), f32(D_MODEL), f32(POOL_WIDTH), bf(2 * D_MODEL),
                 bf(D_MODEL), bf(D_MODEL), bf(D_MODEL), bf(D_MODEL), bf(D_MODEL), bf(D_MODEL), bf(D_MODEL), bf(p_dim))
    out_specs = (pl.BlockSpec((VEC_BAG_ROWS, D_MODEL), fixed),
                 tok(D_MODEL), tok(D_MODEL), tok(POOL_WIDTH), tok(2 * D_MODEL),
                 tok(D_MODEL), tok(D_MODEL), tok(D_MODEL), tok(D_MODEL), tok(D_MODEL), tok(D_MODEL), tok(D_MODEL),
                 tok(p_dim))
    return pl.pallas_call(
        body, name="merge_head", out_shape=out_shape, grid=(t // tb,), in_specs=in_specs, out_specs=out_specs,
        compiler_params=pltpu.CompilerParams(dimension_semantics=("arbitrary",),
                                             vmem_limit_bytes=VMEM_LIMIT_BYTES),
    )(x2d, ya, yb, z, z, p2d, tgt, w_pl, w_pp, w_out, w_pg, w_pe, g2, gf)


def kernel(x, p, norm_g, w_in, conv_w, conv_b, lru_w_a, lru_b_a, lru_w_x, lru_b_x, lru_lambda, pool_w, pool_scale, w_proj_lru, w_proj_pool, w_out, ple_norm_g, w_ple_gate, w_ple_proj, final_g, loss_target, m_norm_g, m_w_in, m_conv_w, m_conv_b, m_lru_w_a, m_lru_b_a, m_lru_w_x, m_lru_b_x, m_lru_lambda, m_pool_w, m_pool_scale, m_w_proj_lru, m_w_proj_pool, m_w_out, m_ple_norm_g, m_w_ple_gate, m_w_ple_proj, m_final_g, v_norm_g, v_w_in, v_conv_w, v_conv_b, v_lru_w_a, v_lru_b_a, v_lru_w_x, v_lru_b_x, v_lru_lambda, v_pool_w, v_pool_scale, v_w_proj_lru, v_w_proj_pool, v_w_out, v_ple_norm_g, v_w_ple_gate, v_w_ple_proj, v_final_g):
    bsz, seq, _ = x.shape
    t = bsz * seq
    tb_mm = min(1024, seq)
    tb_seq = min(256, seq // 2) if seq >= 512 else seq
    x2d = x.reshape(t, D_MODEL)
    p2d = p.reshape(t, p.shape[-1])
    tgt = loss_target.reshape(t, D_MODEL)
    chip = 2 * lax.axis_index("x") + lax.axis_index("y")

    rest = [(w_proj_lru[0], 0), (w_proj_pool[0], 1), (w_out[0], 0), (w_ple_gate[0], 0), (w_ple_proj[0], 1)]
    z, h_bf, w_in_f, conv_w_f = _in_proj_gather(x2d, norm_g, w_in[0].astype(BF16), [(conv_w[0], 1, False)], tb_mm)

    wa_bf = lru_w_a[0].astype(BF16)
    wx_bf = lru_w_x[0].astype(BF16)
    pw_bf = pool_w[0].astype(BF16)
    branch_w = (conv_w_f, conv_b, wa_bf, lru_b_a.reshape(1, D_MODEL), wx_bf, lru_b_x.reshape(1, D_MODEL),
                lru_lambda, pw_bf, pool_scale)

    ya, yb, hl, w_pl_f, w_pp_f, w_out_f, w_pg_f, w_pe_f = _branches_fwd(
        z, branch_w, seq, tb_seq, [(w.astype(BF16), axis, True) for w, axis in rest])
    (vec_bag, dx_res, dya, dyb, dzm, mg_bf, do_bf, hn_bf, dgp_bf, dpe_bf, da_bf, dbm_bf, p_bf) = _merge_head(
        x2d, ya, yb, z, p2d, tgt, w_pl_f, w_pp_f, w_out_f, w_pg_f, w_pe_f, ple_norm_g, final_g.reshape(1, D_MODEL),
        tb_seq)
    dz, vec_bag, mat_bag, g_out, g_out16, g_pp, g_pp16, g_pe, g_pe16 = _branches_bwd(
        z, hl, dya, dyb, dzm, branch_w, vec_bag, seq, tb_seq, [(mg_bf, do_bf), (yb, dbm_bf), (p_bf, dpe_bf)])

    tb_dw = min(1024, seq)
    def row_pieces(g32, g16):
        pieces = (8, g32.shape[0] // 8, g32.shape[1])
        return g32.reshape(pieces), False, g16.reshape(pieces)

    def proj_grad(lhs, rhs, name):
        g32, g16 = _weight_grad(lhs, rhs, 1, tb_dw, name)
        return row_pieces(g32[0], g16[0])

    p_dim = p2d.shape[1]
    proj_parts = [proj_grad(ya, da_bf, "dw_proj_lru"), (g_pp, True, g_pp16), row_pieces(g_out, g_out16),
                  proj_grad(hn_bf, dgp_bf, "dw_ple_gate"), (g_pe, True, g_pe16)]
    nb_dw = t // tb_dw
    g_in, g_in16, r_pl, r_pp, r_out, r_pg, r_pe, vec_mine, mat_mine = _weight_grad(
        h_bf, dz, N_CHIPS, tb_dw, "dw_in",
        reduce=(proj_parts + [(vec_bag.reshape(8, VEC_BAG_ROWS // 8, D_MODEL), False, None),
                              (mat_bag.reshape(8, MAT_BAG_ROWS // 8, HEAD_DIM), False, None)],
                [BF16] * 5 + [F32] * 2,
                (0, nb_dw // 2, 2 * nb_dw - 1, 3 * nb_dw + nb_dw // 2, N_CHIPS * nb_dw - 1)))
    pieces = (8, D_MODEL // 2, IN_COLS // N_CHIPS)
    nb_seq = t // tb_seq
    dx, g_g1, r_in, vec_sum, mat_sum = _in_proj_bwd(
        dz, w_in_f, x2d, dx_res, norm_g, tb_seq,
        reduce=([(g_in.reshape(pieces), False, g_in16.reshape(pieces))], BF16,
                (0, nb_seq // 8, nb_seq // 2, nb_seq - 1, nb_seq - 1)),
        shards=[(vec_mine.reshape(VEC_BAG_ROWS // N_CHIPS, D_MODEL), 0, True),
                (mat_mine.reshape(MAT_BAG_ROWS // N_CHIPS, HEAD_DIM), 0, True)])

    u_in = tuple(a[None] for a in _adamw(w_in[0], r_in.reshape(D_MODEL, IN_COLS // N_CHIPS), m_w_in[0], v_w_in[0],
                                         D_MODEL // 4, "adamw_w_in"))
    proj = [(w_proj_lru, r_pl, m_w_proj_lru, v_w_proj_lru), (w_proj_pool, r_pp, m_w_proj_pool, v_w_proj_pool),
            (w_out, r_out, m_w_out, v_w_out), (w_ple_gate, r_pg, m_w_ple_gate, v_w_ple_gate),
            (w_ple_proj, r_pe, m_w_ple_proj, v_w_ple_proj)]
    u_pl, u_pp, u_out, u_pg, u_pe = [tuple(a[None] for a in u) for u in _adamw_group(
        [(w[0], g.reshape(w.shape[1:]), m[0], v[0]) for w, g, m, v in proj], "adamw_proj")]

    small = [("norm_g", norm_g, m_norm_g, v_norm_g), ("conv_b", conv_b, m_conv_b, v_conv_b),
             ("lru_w_a", lru_w_a, m_lru_w_a, v_lru_w_a), ("lru_b_a", lru_b_a, m_lru_b_a, v_lru_b_a),
             ("lru_w_x", lru_w_x, m_lru_w_x, v_lru_w_x), ("lru_b_x", lru_b_x, m_lru_b_x, v_lru_b_x),
             ("lru_lambda", lru_lambda, m_lru_lambda, v_lru_lambda), ("pool_w", pool_w, m_pool_w, v_pool_w),
             ("pool_scale", pool_scale, m_pool_scale, v_pool_scale),
             ("ple_norm_g", ple_norm_g, m_ple_norm_g, v_ple_norm_g), ("final_g", final_g, m_final_g, v_final_g)]

    def view(a):
        return a.reshape(-1, a.shape[-1]) if a.ndim != 3 else a[0]

    cw_at = F32_SUBLANES * VEC_BAG_SLOTS.index("conv_w")
    cw_cols = D_MODEL // N_CHIPS
    g_cw = lax.dynamic_slice(vec_sum, (cw_at, chip * cw_cols), (CONV_WIDTH, cw_cols))
    flat = _adamw_replicated(vec_sum, mat_sum, g_g1, [(name,) + tuple(view(a) for a in arrs) for name, *arrs in small],
                             (conv_w[0], m_conv_w[0], v_conv_w[0], g_cw))
    u_small = {name: tuple(flat[4 * k + pick].reshape(arrs[0].shape) for pick in range(4))
               for k, (name, *arrs) in enumerate(small)}
    u_cw = tuple(a[None] for a in flat[4 * len(small):4 * len(small) + 4])

    loss = flat[-1].reshape(())
    grad_x = dx.reshape(bsz, seq, D_MODEL)

    def ordered(pick):
        s = {name: u[pick] for name, u in u_small.items()}
        return [s["norm_g"], u_in[pick], u_cw[pick], s["conv_b"], s["lru_w_a"], s["lru_b_a"], s["lru_w_x"], s["lru_b_x"],
                s["lru_lambda"], s["pool_w"], s["pool_scale"], u_pl[pick], u_pp[pick], u_out[pick], s["ple_norm_g"],
                u_pg[pick], u_pe[pick], s["final_g"]]

    return (loss, grad_x, *ordered(0), *ordered(1), *ordered(2), *ordered(3))
```

```python
import jax
import jax.numpy as jnp
from jax import lax
from jax.experimental import pallas as pl
from jax.experimental.pallas import tpu as pltpu

F32 = jnp.float32
BF16 = jnp.bfloat16
MESH = pl.DeviceIdType.MESH

D_MODEL = 1024
LRU_HEADS = 8
HEAD_DIM = 128
CONV_WIDTH = 4
LRU_C = 8.0
POOL_WIDTH = 512
POOL_WINDOWS = (2, 4, 8, 16)
POOL_GROUP_DIM = 128
IN_COLS = 5120
N_CHIPS = 4
EPS = 1e-6

ADAM_LR = 0.001
ADAM_B1 = 0.9
ADAM_B2 = 0.999
ADAM_EPS = 1e-08
ADAM_WD = 0.01
ADAM_STEP = 10

F32_SUBLANES = 8
CONV_HIST = 8
POOL_HIST = 16
VMEM_LIMIT_BYTES = 58 * 1024 * 1024
VEC_BAG_SLOTS = ("norm_g", "conv_w", "conv_b", "lru_b_a", "lru_b_x", "lru_lambda", "pool_scale", "ple_norm_g",
                 "final_g", "loss")
VEC_BAG_ROWS = 128
MAT_BAG_AT = {"lru_w_a": 0, "lru_w_x": LRU_HEADS * HEAD_DIM, "pool_w": 2 * LRU_HEADS * HEAD_DIM}
MAT_BAG_ROWS = 2 * LRU_HEADS * HEAD_DIM + len(POOL_WINDOWS) * POOL_GROUP_DIM


def _bag_row(name, k=0):
    at = F32_SUBLANES * VEC_BAG_SLOTS.index(name) + k
    return slice(at, at + 1)


def _bag_rows(name):
    at = F32_SUBLANES * VEC_BAG_SLOTS.index(name)
    return slice(at, at + F32_SUBLANES)


def _dot(a, b):
    return jnp.dot(a, b, preferred_element_type=F32)


def _dot_nt(a, b):
    return lax.dot_general(a, b, (((1,), (1,)), ((), ())), preferred_element_type=F32)


def _dot_tn(a, b):
    return lax.dot_general(a, b, (((0,), (0,)), ((), ())), preferred_element_type=F32)


def _sigmoid(v):
    return jax.nn.sigmoid(v)


def _softplus(v):
    return jnp.maximum(v, 0.0) + jnp.log1p(jnp.exp(-jnp.abs(v)))


def _place():
    return lax.axis_index("x"), lax.axis_index("y"), lax.axis_index("c")


GATHER_SEMS = 6


def _gather_shapes(shards):
    out_shape = []
    for arr, axis, _ in shards:
        r, cols = arr.shape
        out_shape.append(jax.ShapeDtypeStruct((N_CHIPS * r, cols) if axis == 0 else (r, N_CHIPS * cols), arr.dtype))
    n = len(shards)
    sems = [pltpu.SemaphoreType.DMA((n * GATHER_SEMS,)), pltpu.SemaphoreType.DMA((n * GATHER_SEMS,)),
            pltpu.SemaphoreType.DMA((n,))]
    return out_shape, sems


def _gather_steps(shards, ins, outs, send_sems, recv_sems, local_sems):
    n = len(shards)
    x, y, c = _place()
    me, sibling = (x, y, c), (x, y, 1 - c)
    chips = [(x, 1 - y), (1 - x, y), (1 - x, 1 - y)]

    def region(k, cx, cy, hc):
        (r, cols), axis = shards[k][0].shape, shards[k][1]
        j = 2 * cx + cy
        if axis == 0:
            if hc is None:
                return outs[k].at[pl.ds(j * r, r), :]
            return outs[k].at[pl.ds(j * r + hc * (r // 2), r // 2), :]
        if hc is None:
            return outs[k].at[:, pl.ds(j * cols, cols)]
        return outs[k].at[pl.ds(hc * (r // 2), r // 2), pl.ds(j * cols, cols)]

    def remote(k, sem, block, to, src=None):
        dst = region(k, *block)
        return pltpu.make_async_remote_copy(
            src_ref=dst if src is None else src, dst_ref=dst,
            send_sem=send_sems.at[k * GATHER_SEMS + sem], recv_sem=recv_sems.at[k * GATHER_SEMS + sem],
            device_id=to, device_id_type=MESH)

    def first(k, idx):
        r, split = shards[k][0].shape[0], shards[k][2]
        src = ins[k].at[pl.ds(c * (r // 2), r // 2), :] if split else ins[k]
        return remote(k, idx, (x, y, c if split else None), (*chips[idx], c), src=src)

    def relay(k):
        src_chip = (jnp.bitwise_xor(x, 1 - c), jnp.bitwise_xor(y, c))
        dst_chip = (jnp.bitwise_xor(x, c), jnp.bitwise_xor(y, 1 - c))
        return remote(k, 2, (*src_chip, c), (*dst_chip, c))

    def passed(k, idx):
        return remote(k, 3 + idx, (*chips[idx], c), sibling)

    def mine(k):
        return pltpu.make_async_copy(ins[k], region(k, x, y, None), local_sems.at[k])

    def start():
        for k in range(n):
            mine(k).start()
            for idx in range(2 if shards[k][2] else 3):
                first(k, idx).start()

    def relay_on():
        for k in range(n):
            split = shards[k][2]
            for idx in range(2):
                remote(k, idx, (*chips[idx], c if split else None), me).wait_recv()
            if split:
                relay(k).start()
                passed(k, 0).start()
                passed(k, 1).start()

    def finish():
        for k in range(n):
            split = shards[k][2]
            remote(k, 2, (*chips[2], c if split else None), me).wait_recv()
            if split:
                passed(k, 2).start()
        for k in range(n):
            if shards[k][2]:
                for idx in range(3):
                    remote(k, 3 + idx, (*chips[idx], 1 - c), me).wait_recv()
        for k in range(n):
            if shards[k][2]:
                for cp in (first(k, 0), first(k, 1), relay(k), passed(k, 0), passed(k, 1), passed(k, 2)):
                    cp.wait_send()
            else:
                for idx in range(3):
                    first(k, idx).wait_send()
            mine(k).wait()

    return start, relay_on, finish


RS_ADD_ROWS = (64, 32, 16, 8)


N_DEV = 2 * N_CHIPS


def _all_reduce_scratch(shape):
    return [pltpu.VMEM((N_DEV,) + tuple(shape), F32), pltpu.SemaphoreType.DMA((N_DEV - 1,)),
            pltpu.SemaphoreType.DMA((N_DEV - 1,))]


def _all_reduce_tile(v_ref, o_ref, slots, send_sems, recv_sems):
    flips = [(dx, dy, dc) for dx in (0, 1) for dy in (0, 1) for dc in (0, 1)][1:]
    x, y, c = _place()
    mine = 4 * x + 2 * y + c

    def copy(k, to_flip, slot):
        dx, dy, dc = to_flip
        peer = (jnp.bitwise_xor(x, dx), jnp.bitwise_xor(y, dy), jnp.bitwise_xor(c, dc))
        return pltpu.make_async_remote_copy(
            src_ref=v_ref, dst_ref=slots.at[slot], send_sem=send_sems.at[k], recv_sem=recv_sems.at[k],
            device_id=peer, device_id_type=MESH)

    sends = [copy(k, flip, mine) for k, flip in enumerate(flips)]
    for cp in sends:
        cp.start()
    slots[mine] = v_ref[...]
    for k, (dx, dy, dc) in enumerate(flips):
        copy(k, (dx, dy, dc), jnp.bitwise_xor(mine, 4 * dx + 2 * dy + dc)).wait_recv()
    total = slots[0]
    for d in range(1, N_DEV):
        total = total + slots[d]
    o_ref[...] = total
    for cp in sends:
        cp.wait_send()


RS_SEMS = 8
RS_LOCAL_SEMS = 5


def _rs_piece_shape(part):
    arr, cols = part[0], part[1]
    return (arr.shape[0] // 2, arr.shape[1] // N_CHIPS) if cols else tuple(arr.shape[1:])


def _rs_operands(parts):
    return [p[0] for p in parts] + [p[0] if p[2] is None else p[2] for p in parts]


def _rs_wires(parts, wire):
    return list(wire) if isinstance(wire, (list, tuple)) else [wire] * len(parts)


def _rs_shapes(parts, wire):
    n = len(parts)
    shapes = [_rs_piece_shape(p) for p in parts]
    out_shape = [jax.ShapeDtypeStruct((2,) + s, F32) for s in shapes]
    scratch = []
    for lead, kind in ((N_CHIPS, "f32"), (N_CHIPS, "narrow"), (N_CHIPS, "wire"), (None, "f32"), (N_CHIPS, "wire")):
        for s, p, w in zip(shapes, parts, _rs_wires(parts, wire)):
            dtype = {"f32": F32, "narrow": F32 if p[2] is None else p[2].dtype, "wire": w}[kind]
            scratch.append(pltpu.VMEM(s if lead is None else (lead,) + s, dtype))
    scratch += [pltpu.SemaphoreType.DMA((n * RS_SEMS,)), pltpu.SemaphoreType.DMA((n * RS_SEMS,)),
                pltpu.SemaphoreType.DMA((n * RS_LOCAL_SEMS,))]
    return out_shape, scratch


def _rs_steps(parts, ins, outs, scratch):
    n = len(parts)
    own, sib, got, fin, snd = (scratch[k * n:(k + 1) * n] for k in range(5))
    send_sems, recv_sems, local_sems = scratch[5 * n:]
    shapes = [_rs_piece_shape(p) for p in parts]
    x, y, c = _place()
    j_me = 2 * x + y
    me, sibling = (x, y, c), (x, y, 1 - c)

    def piece(a, jj, core, narrow=False):
        ref = ins[n + a] if narrow else ins[a]
        if parts[a][1]:
            r, cl = shapes[a]
            return ref.at[pl.ds(core * r, r), pl.ds(jj * cl, cl)]
        return ref.at[2 * jj + core]

    def remote(a, sem, src, dst, to):
        return pltpu.make_async_remote_copy(
            src_ref=src, dst_ref=dst, send_sem=send_sems.at[a * RS_SEMS + sem],
            recv_sem=recv_sems.at[a * RS_SEMS + sem], device_id=to, device_id_type=MESH)

    def rows_loop(a, fn):
        r = shapes[a][0]
        step = max(s for s in RS_ADD_ROWS if r % s == 0)

        def it(i, carry):
            fn(pl.ds(pl.multiple_of(i * step, step), step))
            return carry

        lax.fori_loop(0, r // step, it, 0)

    def load(a, jj):
        return pltpu.make_async_copy(piece(a, jj, c), own[a].at[jj], local_sems.at[a * RS_LOCAL_SEMS + jj])

    def to_sibling(a, jj):
        return remote(a, jj, piece(a, jj, 1 - c, narrow=True), sib[a].at[jj], sibling)

    near = (jnp.bitwise_xor(x, 1 - c), jnp.bitwise_xor(y, c))
    far = (jnp.bitwise_xor(x, c), jnp.bitwise_xor(y, 1 - c))
    diag = (1 - x, 1 - y)
    FROM_NEAR, FROM_FAR, FEED = 0, 1, 2

    def chip_of(chip):
        return 2 * chip[0] + chip[1]

    def feed(a):
        return remote(a, 4, snd[a].at[chip_of(diag)], got[a].at[FEED], (*near, c))

    def to_near(a):
        return remote(a, 5, snd[a].at[chip_of(near)], got[a].at[FROM_NEAR], (*near, c))

    def to_far(a):
        return remote(a, 6, snd[a].at[chip_of(far)], got[a].at[FROM_FAR], (*far, c))

    def store(a):
        return pltpu.make_async_copy(fin[a], outs[a].at[c], local_sems.at[a * RS_LOCAL_SEMS + 4])

    def result_to_sibling(a):
        return remote(a, 7, fin[a], outs[a].at[c], sibling)

    def exchange():
        for a in range(n):
            for jj in range(N_CHIPS):
                load(a, jj).start()
                to_sibling(a, jj).start()

    def chip_sums():
        for a in range(n):
            for jj in range(N_CHIPS):
                load(a, jj).wait()
                remote(a, jj, sib[a].at[jj], sib[a].at[jj], me).wait_recv()

                def add(sl, a=a, jj=jj):
                    q = own[a][jj, sl, :] + sib[a][jj, sl, :].astype(F32)
                    own[a][jj, sl, :] = q
                    snd[a][jj, sl, :] = q.astype(snd[a].dtype)

                rows_loop(a, add)
        for a in range(n):
            feed(a).start()
        for a in range(n):
            to_near(a).start()

    def relay():
        for a in range(n):
            remote(a, 4, got[a].at[FEED], got[a].at[FEED], me).wait_recv()

            def add(sl, a=a):
                pair = own[a][chip_of(far), sl, :] + got[a][FEED, sl, :].astype(F32)
                snd[a][chip_of(far), sl, :] = pair.astype(snd[a].dtype)

            rows_loop(a, add)
            to_far(a).start()

    def totals():
        for a in range(n):
            remote(a, 5, got[a].at[FROM_NEAR], got[a].at[FROM_NEAR], me).wait_recv()
            remote(a, 6, got[a].at[FROM_FAR], got[a].at[FROM_FAR], me).wait_recv()

            def total(sl, a=a):
                fin[a][sl, :] = (own[a][j_me, sl, :] + got[a][FROM_NEAR, sl, :].astype(F32)) + (
                    got[a][FROM_FAR, sl, :].astype(F32))

            rows_loop(a, total)
            store(a).start()
            result_to_sibling(a).start()

    def finish():
        for a in range(n):
            remote(a, 7, outs[a].at[1 - c], outs[a].at[1 - c], me).wait_recv()
        for a in range(n):
            for jj in range(N_CHIPS):
                to_sibling(a, jj).wait_send()
            for cp in (feed(a), to_near(a), to_far(a), result_to_sibling(a)):
                cp.wait_send()
            store(a).wait()

    return exchange, chip_sums, relay, totals, finish


def _rms(x):
    r = lax.rsqrt(jnp.mean(x * x, axis=-1, keepdims=True) + EPS)
    return x * r, r


def _rms_bwd(dxn, xn, r):
    return r * (dxn - xn * jnp.mean(dxn * xn, axis=-1, keepdims=True))


def _in_proj_gather(x2d, norm_g, w_in_sh, shards, tb, casts):
    t = x2d.shape[0]
    nb = t // tb
    cols = IN_COLS // N_CHIPS
    half = D_MODEL // 2
    n = len(shards)
    nc = len(casts)

    def body(x_ref, g_ref, win_ref, *refs):
        ins, cast_ins = refs[:n], refs[n:n + nc]
        z_ref, h_ref, wfull_ref = refs[n + nc:n + nc + 3]
        outs, cast_outs = refs[n + nc + 3:2 * n + nc + 3], refs[2 * n + nc + 3:2 * (n + nc) + 3]
        scratch = refs[2 * (n + nc) + 3:]
        wv, h_all, send_sems, recv_sems, local_sems, w_send, w_recv, w_local, stage = scratch[:9]
        wide, narrow, cast_sems = scratch[9:9 + nc], scratch[9 + nc:9 + 2 * nc], scratch[9 + 2 * nc]
        s, i = pl.program_id(0), pl.program_id(1)
        x, y, c = _place()
        me, sibling = (x, y, c), (x, y, 1 - c)
        chips = [(x, 1 - y), (1 - x, y), (1 - x, 1 - y)]

        def w_half(cx, cy, hc):
            return wv.at[2 * cx + cy, pl.ds(hc * half, half), :]

        def w_remote(sem, block, to, src=None):
            dst = w_half(*block)
            return pltpu.make_async_remote_copy(
                src_ref=dst if src is None else src, dst_ref=dst, send_sem=w_send.at[sem],
                recv_sem=w_recv.at[sem], device_id=to, device_id_type=MESH)

        def w_first(idx):
            return w_remote(idx, (x, y, c), (*chips[idx], c))

        def w_relay():
            src_chip = (jnp.bitwise_xor(x, 1 - c), jnp.bitwise_xor(y, c))
            dst_chip = (jnp.bitwise_xor(x, c), jnp.bitwise_xor(y, 1 - c))
            return w_remote(2, (*src_chip, c), (*dst_chip, c))

        def w_pass(idx):
            return w_remote(3 + idx, (*chips[idx], c), sibling)

        def w_store(k, cx, cy):
            jj = 2 * cx + cy
            return pltpu.make_async_copy(wv.at[jj], wfull_ref.at[:, pl.ds(jj * cols, cols)], w_local.at[k])

        start_rest, relay_rest, finish_rest = _gather_steps(shards, ins, outs, send_sems, recv_sems, local_sems)

        def own(k, hc):
            return pltpu.make_async_copy(win_ref.at[pl.ds(pl.multiple_of(hc * half, half), half), :], stage.at[k],
                                         w_local.at[4 + 2 * k])

        def round_own(k, hc):
            own(k, hc).wait()
            wv[2 * x + y, pl.ds(pl.multiple_of(hc * half, half), half), :] = stage[k].astype(BF16)

        wide_in = [pltpu.make_async_copy(cast_ins[k], wide[k], cast_sems.at[k]) for k in range(nc)]
        narrow_out = [pltpu.make_async_copy(narrow[k], cast_outs[k], cast_sems.at[nc + k]) for k in range(nc)]

        @pl.when((s == 0) & (i == 0))
        def _():
            own(0, c).start()
            own(1, 1 - c).start()
            for cp in wide_in:
                cp.start()
            round_own(0, c)
            w_first(0).start()
            w_first(1).start()
            start_rest()
            round_own(1, 1 - c)
            w_store(0, x, y).start()

        @pl.when((s == 1) & (i == 0))
        def _():
            for k in range(nc):
                wide_in[k].wait()
                narrow[k][...] = wide[k][...].astype(BF16)
                narrow_out[k].start()
            w_remote(0, (*chips[0], c), me).wait_recv()
            w_remote(1, (*chips[1], c), me).wait_recv()
            w_relay().start()
            w_pass(0).start()
            w_pass(1).start()
            w_remote(3, (*chips[0], 1 - c), me).wait_recv()
            w_store(1, *chips[0]).start()

        @pl.when((s == 2) & (i == 0))
        def _():
            w_remote(4, (*chips[1], 1 - c), me).wait_recv()
            w_store(2, *chips[1]).start()

        @pl.when((s == 3) & (i == 0))
        def _():
            w_remote(2, (*chips[2], c), me).wait_recv()
            w_pass(2).start()
            w_remote(5, (*chips[2], 1 - c), me).wait_recv()
            w_store(3, *chips[2]).start()

        keep_h = pltpu.make_async_copy(h_all.at[i], h_ref.at[pl.ds(pl.multiple_of(i * tb, tb), tb), :], w_local.at[5])

        @pl.when(s == 0)
        def _():
            xn, _ = _rms(x_ref[...])
            h_all[i] = (xn * g_ref[...]).astype(BF16)
            keep_h.start()

        z_ref[...] = _dot(h_all[i], wv[jnp.bitwise_xor(2 * x + y, s)])
        pl.when(s == 0)(keep_h.wait)

        @pl.when((s == N_CHIPS - 1) & (i == nb - 1))
        def _():
            relay_rest()
            finish_rest()
            for cp in (w_first(0), w_first(1), w_relay(), w_pass(0), w_pass(1), w_pass(2)):
                cp.wait_send()
            w_store(0, x, y).wait()
            for idx in range(3):
                w_store(idx + 1, *chips[idx]).wait()
            for cp in narrow_out:
                cp.wait()

    rest_shape, rest_sems = _gather_shapes(shards)
    out_shape = [jax.ShapeDtypeStruct((t, IN_COLS), F32), jax.ShapeDtypeStruct((t, D_MODEL), BF16),
                 jax.ShapeDtypeStruct((D_MODEL, IN_COLS), BF16)] + rest_shape
    out_shape += [jax.ShapeDtypeStruct(a.shape, BF16) for a in casts]
    any_spec = pl.BlockSpec(memory_space=pl.ANY)

    def z_map(s, i):
        return (i, jnp.bitwise_xor(2 * lax.axis_index("x") + lax.axis_index("y"), s))

    return pl.pallas_call(
        body, name="in_proj", out_shape=tuple(out_shape),
        grid=(N_CHIPS, nb),
        in_specs=[pl.BlockSpec((tb, D_MODEL), lambda s, i: (jnp.where(s == 0, i, nb - 1), 0)),
                  pl.BlockSpec((1, D_MODEL), lambda s, i: (0, 0)), any_spec] + [any_spec] * (n + nc),
        out_specs=tuple([pl.BlockSpec((tb, cols), z_map), any_spec, any_spec] + [any_spec] * (n + nc)),
        scratch_shapes=[pltpu.VMEM((N_CHIPS, D_MODEL, cols), BF16), pltpu.VMEM((nb, tb, D_MODEL), BF16)] + rest_sems + [
            pltpu.SemaphoreType.DMA((GATHER_SEMS,)), pltpu.SemaphoreType.DMA((GATHER_SEMS,)),
            pltpu.SemaphoreType.DMA((N_CHIPS + 3,)), pltpu.VMEM((2, half, cols), F32)]
        + [pltpu.VMEM(a.shape, F32) for a in casts] + [pltpu.VMEM(a.shape, BF16) for a in casts]
        + [pltpu.SemaphoreType.DMA((2 * nc,))],
        compiler_params=pltpu.CompilerParams(dimension_semantics=("arbitrary", "arbitrary"),
                                             vmem_limit_bytes=VMEM_LIMIT_BYTES),
    )(x2d, norm_g, w_in_sh, *[sh[0] for sh in shards], *casts)


def _in_proj_bwd(dz, w_in, x2d, dx_res, norm_g, tb, reduce, shards):
    t = x2d.shape[0]
    nb = t // tb
    parts, wire, steps = reduce
    n = len(parts)
    k = len(shards)

    def body(dz_ref, w_ref, x_ref, dres_ref, g_ref, *refs):
        at = 2 * n + k
        dx_ref, dg_ref = refs[at:at + 2]
        rs_outs, g_outs = refs[at + 2:at + 2 + n], refs[at + 2 + n:at + 2 + n + k]
        scratch = refs[at + 2 + n + k:]
        rs_scr, g_sems, dg_acc, ar_scr = scratch[:-7], scratch[-7:-4], scratch[-4], scratch[-3:]
        rs = _rs_steps(parts, refs[:2 * n], rs_outs, rs_scr)
        for step, when in zip(rs, steps):
            pl.when(pl.program_id(0) == when)(step)
        gather = _gather_steps(shards, refs[2 * n:at], g_outs, *g_sems)
        for step, when in zip(gather, (0, nb // 2, nb - 1)):
            pl.when(pl.program_id(0) == when)(step)

        @pl.when(pl.program_id(0) == 0)
        def _():
            dg_acc[...] = jnp.zeros_like(dg_acc)

        xn, r = _rms(x_ref[...])
        g = g_ref[...]
        dh = _dot_nt(dz_ref[...], w_ref[...])
        dg_acc[0:1, :] += jnp.sum(dh * xn, axis=0, keepdims=True)
        dx_ref[...] = dres_ref[...] + _rms_bwd(dh * g, xn, r)

        @pl.when(pl.program_id(0) == nb - 1)
        def _():
            _all_reduce_tile(dg_acc, dg_ref, *ar_scr)

    row = lambda i: (i, 0)
    fixed = lambda i: (0, 0)
    rs_shape, rs_scratch = _rs_shapes(parts, wire)
    g_shape, g_sems = _gather_shapes(shards)
    any_spec = pl.BlockSpec(memory_space=pl.ANY)
    return pl.pallas_call(
        body, name="in_proj_bwd",
        out_shape=tuple([jax.ShapeDtypeStruct((t, D_MODEL), F32), jax.ShapeDtypeStruct((F32_SUBLANES, D_MODEL), F32)]
                        + rs_shape + g_shape),
        grid=(nb,),
        in_specs=[pl.BlockSpec((tb, IN_COLS), row),
                  pl.BlockSpec((D_MODEL, IN_COLS), fixed, pipeline_mode=pl.Buffered(1)),
                  pl.BlockSpec((tb, D_MODEL), row), pl.BlockSpec((tb, D_MODEL), row),
                  pl.BlockSpec((1, D_MODEL), fixed)] + [any_spec] * (2 * n + k),
        out_specs=tuple([pl.BlockSpec((tb, D_MODEL), row), pl.BlockSpec((F32_SUBLANES, D_MODEL), fixed)]
                        + [any_spec] * (n + k)),
        scratch_shapes=rs_scratch + g_sems + [pltpu.VMEM((F32_SUBLANES, D_MODEL), F32)] + _all_reduce_scratch(
            (F32_SUBLANES, D_MODEL)),
        compiler_params=pltpu.CompilerParams(dimension_semantics=("arbitrary",),
                                             vmem_limit_bytes=VMEM_LIMIT_BYTES),
    )(dz, w_in, x2d, dx_res, norm_g, *_rs_operands(parts), *[sh[0] for sh in shards])


def _weight_grad(lhs, rhs, n_chunks, tb, name, reduce=None):
    t, k = lhs.shape
    nc = rhs.shape[1] // n_chunks
    nb = t // tb
    parts, wire, steps = reduce if reduce is not None else ([], F32, ())
    n = len(parts)

    def body(l_ref, r_ref, *refs):
        o_ref, o16_ref = refs[2 * n:2 * n + 2]
        if n:
            at = pl.program_id(0) * nb + pl.program_id(1)
            rs = _rs_steps(parts, refs[:2 * n], refs[2 * n + 2:3 * n + 2], refs[3 * n + 2:])
            for step, when in zip(rs, steps):
                pl.when(at == when)(step)

        @pl.when(pl.program_id(1) == 0)
        def _():
            o_ref[...] = jnp.zeros_like(o_ref)

        o_ref[...] += _dot_tn(l_ref[...], r_ref[...])

        @pl.when(pl.program_id(1) == nb - 1)
        def _():
            o16_ref[...] = o_ref[...].astype(BF16)

    rs_shape, rs_scratch = _rs_shapes(parts, wire) if n else ([], [])
    any_spec = pl.BlockSpec(memory_space=pl.ANY)
    chunk = pl.BlockSpec((None, k, nc), lambda j, i: (j, 0, 0))
    return pl.pallas_call(
        body, name=name,
        out_shape=tuple([jax.ShapeDtypeStruct((n_chunks, k, nc), F32), jax.ShapeDtypeStruct((n_chunks, k, nc), BF16)]
                        + rs_shape),
        grid=(n_chunks, nb),
        in_specs=[pl.BlockSpec((tb, k), lambda j, i: (i, 0)), pl.BlockSpec((tb, nc), lambda j, i: (i, j))]
        + [any_spec] * (2 * n),
        out_specs=tuple([chunk, chunk] + [any_spec] * n),
        scratch_shapes=rs_scratch,
        compiler_params=pltpu.CompilerParams(dimension_semantics=("arbitrary", "arbitrary"),
                                             vmem_limit_bytes=VMEM_LIMIT_BYTES),
    )(lhs, rhs, *_rs_operands(parts))


def _adam_update(w, g, m, v):
    m_ = ADAM_B1 * m + (1.0 - ADAM_B1) * g
    v_ = ADAM_B2 * v + (1.0 - ADAM_B2) * jnp.square(g)
    m_hat = m_ / (1.0 - ADAM_B1 ** ADAM_STEP)
    v_hat = v_ / (1.0 - ADAM_B2 ** ADAM_STEP)
    return -ADAM_LR * (m_hat / (jnp.sqrt(v_hat) + ADAM_EPS) + ADAM_WD * w), m_, v_


def _adamw_replicated(vec_sum, mat_sum, norm_grad, entries, conv):
    n = len(entries)

    def grad_of(name, shape, vec_ref, mat_ref, norm_ref):
        if name == "norm_g":
            return norm_ref[0:1, :]
        if name in MAT_BAG_AT:
            return mat_ref[MAT_BAG_AT[name]:MAT_BAG_AT[name] + shape[0], :]
        if shape[0] == 1:
            return vec_ref[_bag_row(name), 0:shape[1]]
        return jnp.concatenate([vec_ref[_bag_row(name), h * shape[1]:(h + 1) * shape[1]] for h in range(shape[0])],
                               axis=0)

    def body(vec_ref, mat_ref, norm_ref, *refs):
        ins, outs = refs[:3 * n + 4], refs[3 * n + 4:]
        for k in range(n):
            w_ref, m_ref, v_ref = ins[3 * k:3 * k + 3]
            g = grad_of(entries[k][0], w_ref.shape, vec_ref, mat_ref, norm_ref)
            d, m_, v_ = _adam_update(w_ref[...], g, m_ref[...], v_ref[...])
            for ref, val in zip(outs[4 * k:4 * k + 4], (g, d, m_, v_)):
                ref[...] = val
        w_ref, m_ref, v_ref, g_ref = ins[3 * n:]
        g = g_ref[...]
        for ref, val in zip(outs[4 * n:4 * n + 4], (g,) + _adam_update(w_ref[...], g, m_ref[...], v_ref[...])):
            ref[...] = val
        outs[4 * n + 4][...] = vec_ref[_bag_row("loss"), 0:1]

    arrays = [a for e in entries for a in e[1:]] + list(conv)
    out_shape = [jax.ShapeDtypeStruct(e[1].shape, F32) for e in entries for _ in range(4)]
    out_shape += [jax.ShapeDtypeStruct(conv[0].shape, F32)] * 4 + [jax.ShapeDtypeStruct((1, 1), F32)]
    return pl.pallas_call(
        body, name="adamw_replicated", out_shape=tuple(out_shape),
        compiler_params=pltpu.CompilerParams(vmem_limit_bytes=VMEM_LIMIT_BYTES),
    )(vec_sum, mat_sum, norm_grad, *arrays)


def _adamw(w, g, m, v, rows, name):
    r, c = w.shape

    def body(w_ref, g_ref, m_ref, v_ref, go_ref, d_ref, nm_ref, nv_ref):
        g = g_ref[...]
        go_ref[...] = g
        d_ref[...], nm_ref[...], nv_ref[...] = _adam_update(w_ref[...], g, m_ref[...], v_ref[...])

    spec = pl.BlockSpec((rows, c), lambda i: (i, 0))
    return pl.pallas_call(
        body, name=name, out_shape=tuple(jax.ShapeDtypeStruct((r, c), F32) for _ in range(4)),
        grid=(r // rows,), in_specs=[spec] * 4, out_specs=(spec,) * 4,
        compiler_params=pltpu.CompilerParams(dimension_semantics=("arbitrary",),
                                             vmem_limit_bytes=VMEM_LIMIT_BYTES),
    )(w, g, m, v)


def _adamw_group(items, name):
    n = 4 * len(items)

    def body(*refs):
        ins, outs, bufs = refs[:n], refs[n:2 * n], refs[2 * n:3 * n]
        load_sems, store_sems = refs[3 * n:]
        loads = [pltpu.make_async_copy(ins[j], bufs[j], load_sems.at[j]) for j in range(n)]
        stores = [pltpu.make_async_copy(bufs[j], outs[j], store_sems.at[j]) for j in range(n)]
        for cp in loads:
            cp.start()
        for k in range(len(items)):
            for cp in loads[4 * k:4 * k + 4]:
                cp.wait()
            w_buf, g_buf, m_buf, v_buf = bufs[4 * k:4 * k + 4]
            w_buf[...], m_buf[...], v_buf[...] = _adam_update(w_buf[...], g_buf[...], m_buf[...], v_buf[...])
            for cp in stores[4 * k:4 * k + 4]:
                cp.start()
        for cp in stores:
            cp.wait()

    arrays = [a for item in items for a in item]
    any_spec = pl.BlockSpec(memory_space=pl.ANY)
    flat = pl.pallas_call(
        body, name=name, out_shape=tuple(jax.ShapeDtypeStruct(a.shape, F32) for a in arrays),
        in_specs=[any_spec] * n, out_specs=(any_spec,) * n,
        scratch_shapes=[pltpu.VMEM(a.shape, F32) for a in arrays] + [pltpu.SemaphoreType.DMA((n,))] * 2,
        compiler_params=pltpu.CompilerParams(vmem_limit_bytes=VMEM_LIMIT_BYTES),
    )(*arrays)
    return [(flat[4 * k + 1], flat[4 * k], flat[4 * k + 2], flat[4 * k + 3]) for k in range(len(items))]


def _shift_down(ext, s):
    return pltpu.roll(ext, s, 0)


def _tile_shift(v, s):
    rows, cols = v.shape
    tiles = v.reshape(rows // F32_SUBLANES, F32_SUBLANES, cols)
    return pltpu.roll(tiles, s % F32_SUBLANES, 1).reshape(rows, cols)


def _shift_up(ext, s):
    return pltpu.roll(ext, ext.shape[0] - s, 0)


def _lru_gates(xc, wa_ref, ba, wx_ref, bx, lam):
    pa, px = [], []
    for h in range(LRU_HEADS):
        xh = xc[:, h * HEAD_DIM:(h + 1) * HEAD_DIM].astype(BF16)
        pa.append(_dot(xh, wa_ref[h]))
        px.append(_dot(xh, wx_ref[h]))
    r = _sigmoid(jnp.concatenate(pa, axis=1) + ba)
    ig = _sigmoid(jnp.concatenate(px, axis=1) + bx)
    sp = _softplus(-lam)
    log_a = (-LRU_C * r) * sp
    a = jnp.exp(log_a)
    mult = jnp.sqrt(jnp.tanh(-log_a) * (1.0 + a * a))
    return r, ig, a, mult, sp


def _conv(ext, w_ref, b):
    y = b + _shift_down(ext, 3) * w_ref[0:1, :]
    y = y + _shift_down(ext, 2) * w_ref[1:2, :]
    y = y + _shift_down(ext, 1) * w_ref[2:3, :]
    y = y + ext * w_ref[3:4, :]
    return y[CONV_HIST:, :]


def _pool_diff(ext, pos):
    out = []
    for g, k in enumerate(POOL_WINDOWS):
        col = ext[:, g * POOL_GROUP_DIM:(g + 1) * POOL_GROUP_DIM]
        s = col
        for step in range(g + 1):
            s = s + _shift_down(s, 2 ** step)
        count = jnp.minimum(pos + 1, k).astype(F32)
        out.append(s[POOL_HIST:, :] / count - col[POOL_HIST:, :])
    return out


def _pool_mix(diff, pw_ref):
    return jnp.concatenate([_dot(diff[g].astype(BF16), pw_ref[g]) for g in range(len(POOL_WINDOWS))], axis=1)


def _branch_specs(tb, row_map, fixed):
    fixed3 = lambda i: (0, 0, 0)
    return [pl.BlockSpec((CONV_WIDTH, D_MODEL), fixed), pl.BlockSpec((1, D_MODEL), fixed),
            pl.BlockSpec((LRU_HEADS, HEAD_DIM, HEAD_DIM), fixed3), pl.BlockSpec((1, D_MODEL), fixed),
            pl.BlockSpec((LRU_HEADS, HEAD_DIM, HEAD_DIM), fixed3), pl.BlockSpec((1, D_MODEL), fixed),
            pl.BlockSpec((1, D_MODEL), fixed),
            pl.BlockSpec((len(POOL_WINDOWS), POOL_GROUP_DIM, POOL_GROUP_DIM), fixed3),
            pl.BlockSpec((1, POOL_WIDTH), fixed)]


def _branches_fwd(z, weights, seq, tb, shards):
    t = z.shape[0]
    nb = t // tb
    nbe = seq // tb
    groups = tb // F32_SUBLANES
    n = len(shards)

    def body(xa_ref, ga_ref, xb_ref, gb_ref, cw_ref, cb_ref, wa_ref, ba_ref, wx_ref, bx_ref, lam_ref,
             pw_ref, ps_ref, *refs):
        g_ins = refs[:n]
        ya_ref, yb_ref, hl_ref = refs[n:n + 3]
        g_outs = refs[n + 3:2 * n + 3]
        xa_ext, xb_ext, carry, a_s, u_s, send_sems, recv_sems, local_sems = refs[2 * n + 3:]
        blk = pl.program_id(0) % nbe
        start_gather, relay_gather, finish_gather = _gather_steps(shards, g_ins, g_outs, send_sems, recv_sems,
                                                                  local_sems)
        pl.when(pl.program_id(0) == 0)(start_gather)
        pl.when(pl.program_id(0) == nb // 2)(relay_gather)

        @pl.when(blk == 0)
        def _():
            xa_ext[0:CONV_HIST, :] = jnp.zeros((CONV_HIST, D_MODEL), F32)
            xb_ext[0:POOL_HIST, :] = jnp.zeros((POOL_HIST, POOL_WIDTH), F32)
            carry[...] = jnp.zeros_like(carry)

        xa_ext[CONV_HIST:, :] = xa_ref[...]
        xb_ext[POOL_HIST:, :] = xb_ref[...]
        ea = xa_ext[...]
        eb = xb_ext[...]
        xa_ext[0:CONV_HIST, :] = ea[tb:, :]
        xb_ext[0:POOL_HIST, :] = eb[tb:, :]

        xc = _conv(ea, cw_ref, cb_ref[...])
        _, ig, a, mult, _ = _lru_gates(xc, wa_ref, ba_ref[...], wx_ref, bx_ref[...], lam_ref[...])
        u = mult * (ig * xc)
        row8 = lax.broadcasted_iota(jnp.int32, (tb, D_MODEL), 0) % F32_SUBLANES
        for s in (1, 2, 4):
            m = row8 >= s
            u = jnp.where(m, a * _tile_shift(u, s) + u, u)
            a = jnp.where(m, a * _tile_shift(a, s), a)
        a_s[...] = a
        u_s[...] = u

        def step(g, cr):
            sl = pl.ds(pl.multiple_of(g * F32_SUBLANES, F32_SUBLANES), F32_SUBLANES)
            hb = a_s[sl, :] * cr + u_s[sl, :]
            hl_ref[sl, :] = hb
            return jnp.broadcast_to(hb[F32_SUBLANES - 1:F32_SUBLANES, :], (F32_SUBLANES, D_MODEL))

        carry[...] = lax.fori_loop(0, groups, step, carry[...], unroll=4)
        ga = ga_ref[...]
        ya_ref[...] = (hl_ref[...] * (ga * _sigmoid(ga))).astype(BF16)

        pos = blk * tb + lax.broadcasted_iota(jnp.int32, (tb, POOL_GROUP_DIM), 0)
        ypre = _pool_mix(_pool_diff(eb, pos), pw_ref)
        gb = gb_ref[...]
        yb_ref[...] = ((ypre * ps_ref[...]) * (gb * _sigmoid(gb))).astype(BF16)
        pl.when(pl.program_id(0) == nb - 1)(finish_gather)

    row = lambda i: (i, 0)
    fixed = lambda i: (0, 0)
    any_spec = pl.BlockSpec(memory_space=pl.ANY)
    in_specs = [pl.BlockSpec((tb, D_MODEL), lambda i: (i, 0)), pl.BlockSpec((tb, D_MODEL), lambda i: (i, 1)),
                pl.BlockSpec((tb, POOL_WIDTH), lambda i: (i, 4)), pl.BlockSpec((tb, POOL_WIDTH), lambda i: (i, 5)),
                ] + _branch_specs(tb, row, fixed) + [any_spec] * n
    g_shape, g_sems = _gather_shapes(shards)
    return pl.pallas_call(
        body, name="branches_fwd",
        out_shape=tuple([jax.ShapeDtypeStruct((t, D_MODEL), BF16), jax.ShapeDtypeStruct((t, POOL_WIDTH), BF16),
                         jax.ShapeDtypeStruct((t, D_MODEL), F32)] + g_shape),
        grid=(nb,), in_specs=in_specs,
        out_specs=tuple([pl.BlockSpec((tb, D_MODEL), row), pl.BlockSpec((tb, POOL_WIDTH), row),
                         pl.BlockSpec((tb, D_MODEL), row)] + [any_spec] * n),
        scratch_shapes=[pltpu.VMEM((tb + CONV_HIST, D_MODEL), F32), pltpu.VMEM((tb + POOL_HIST, POOL_WIDTH), F32),
                        pltpu.VMEM((F32_SUBLANES, D_MODEL), F32),
                        pltpu.VMEM((tb, D_MODEL), F32), pltpu.VMEM((tb, D_MODEL), F32)] + g_sems,
        compiler_params=pltpu.CompilerParams(dimension_semantics=("arbitrary",),
                                             vmem_limit_bytes=VMEM_LIMIT_BYTES),
    )(z, z, z, z, *weights, *[sh[0] for sh in shards])


def _branches_bwd(z, hl, dya, dyb, dzm, weights, vec_bag, seq, tb, riders):
    t = z.shape[0]
    nb = t // tb
    nbe = seq // tb
    groups = tb // F32_SUBLANES
    nr = len(riders)

    def body(xa_ref, xap_ref, ga_ref, xb_ref, xbp_ref, gb_ref, hl_ref, hlp_ref, dya_ref, dyb_ref, dzm_ref,
             cw_ref, cb_ref, wa_ref, ba_ref, wx_ref, bx_ref, lam_ref, pw_ref, ps_ref, vec_in_ref, *rest):
        pairs, (dz_ref, vec_ref, mat_ref), grads = rest[:2 * nr], rest[2 * nr:2 * nr + 3], rest[2 * nr + 3:4 * nr + 3]
        xa_ext, xb_ext, hl_ext, a_ext, dxc_ext, dwin_ext, g_carry, b_s, d_s, g_s = rest[4 * nr + 3:]
        i = pl.program_id(0)
        blk = (nb - 1 - i) % nbe

        def mat_rows(name, k):
            at = MAT_BAG_AT[name] + k * HEAD_DIM
            return slice(at, at + HEAD_DIM)

        def rider(k):
            grads[2 * k][...] += _dot_tn(pairs[2 * k][...], pairs[2 * k + 1][...])

        @pl.when(i == 0)
        def _():
            vec_ref[...] = vec_in_ref[...]
            mat_ref[...] = jnp.zeros_like(mat_ref)
            for k in range(nr):
                grads[2 * k][...] = jnp.zeros_like(grads[2 * k])

        @pl.when(blk == nbe - 1)
        def _():
            a_ext[tb:, :] = jnp.zeros((F32_SUBLANES, D_MODEL), F32)
            dxc_ext[tb:, :] = jnp.zeros((CONV_HIST, D_MODEL), F32)
            dwin_ext[tb:, :] = jnp.zeros((POOL_HIST, POOL_WIDTH), F32)
            g_carry[...] = jnp.zeros_like(g_carry)

        live = (blk > 0).astype(F32)
        xa_ext[0:CONV_HIST, :] = xap_ref[...] * live
        xa_ext[CONV_HIST:, :] = xa_ref[...]
        xb_ext[0:POOL_HIST, :] = xbp_ref[...] * live
        xb_ext[POOL_HIST:, :] = xb_ref[...]
        hl_ext[0:F32_SUBLANES, :] = hlp_ref[...] * live
        hl_ext[F32_SUBLANES:, :] = hl_ref[...]
        ea = xa_ext[...]
        eb = xb_ext[...]
        rider(0)

        xc = _conv(ea, cw_ref, cb_ref[...])
        lam = lam_ref[...]
        r, ig, a, mult, sp = _lru_gates(xc, wa_ref, ba_ref[...], wx_ref, bx_ref[...], lam)
        hl = hl_ref[...]
        ga = ga_ref[...]
        sga = _sigmoid(ga)
        dya = dya_ref[...]
        dhl = dya * (ga * sga)
        dz_ref[:, D_MODEL:2 * D_MODEL] = (dya * hl * (sga * (1.0 + ga * (1.0 - sga)))).astype(BF16)

        a_ext[0:tb, :] = a
        b = _shift_up(a_ext[...], 1)[0:tb, :]
        a_ext[tb:, :] = jnp.broadcast_to(a[0:1, :], (F32_SUBLANES, D_MODEL))
        d = dhl
        row8 = lax.broadcasted_iota(jnp.int32, (tb, D_MODEL), 0) % F32_SUBLANES
        for s in (1, 2, 4):
            m = row8 < F32_SUBLANES - s
            d = jnp.where(m, d + b * _tile_shift(d, -s), d)
            b = jnp.where(m, b * _tile_shift(b, -s), b)
        b_s[...] = b
        d_s[...] = d

        def step(k, cr):
            sl = pl.ds(pl.multiple_of((groups - 1 - k) * F32_SUBLANES, F32_SUBLANES), F32_SUBLANES)
            gb_ = d_s[sl, :] + b_s[sl, :] * cr
            g_s[sl, :] = gb_
            return jnp.broadcast_to(gb_[0:1, :], (F32_SUBLANES, D_MODEL))

        g_carry[...] = lax.fori_loop(0, groups, step, g_carry[...], unroll=4)
        rider(1)
        gsc = g_s[...]
        da = gsc * _shift_down(hl_ext[...], 1)[F32_SUBLANES:, :]
        dmult = gsc * (ig * xc)
        dig = gsc * (mult * xc)
        dxc = gsc * (mult * ig)
        dlog_a = da * a - (a * a) * dmult / mult
        dr = dlog_a * (-LRU_C * sp)
        vec_ref[_bag_row("lru_lambda"), :] += jnp.sum(dlog_a * (-LRU_C * r), axis=0, keepdims=True)
        dpa = dr * (r * (1.0 - r))
        dpx = dig * (ig * (1.0 - ig))
        vec_ref[_bag_row("lru_b_a"), :] += jnp.sum(dpa, axis=0, keepdims=True)
        vec_ref[_bag_row("lru_b_x"), :] += jnp.sum(dpx, axis=0, keepdims=True)
        back = []
        for h in range(LRU_HEADS):
            cols = slice(h * HEAD_DIM, (h + 1) * HEAD_DIM)
            xh = xc[:, cols].astype(BF16)
            dpa_h = dpa[:, cols].astype(BF16)
            dpx_h = dpx[:, cols].astype(BF16)
            mat_ref[mat_rows("lru_w_a", h), :] += _dot_tn(xh, dpa_h)
            mat_ref[mat_rows("lru_w_x", h), :] += _dot_tn(xh, dpx_h)
            back.append(_dot_nt(dpa_h, wa_ref[h]) + _dot_nt(dpx_h, wx_ref[h]))
        dxc = dxc + jnp.concatenate(back, axis=1)
        vec_ref[_bag_row("conv_b"), :] += jnp.sum(dxc, axis=0, keepdims=True)
        for k in range(CONV_WIDTH):
            tap = _shift_down(ea, CONV_WIDTH - 1 - k)[CONV_HIST:, :] if k < CONV_WIDTH - 1 else ea[CONV_HIST:, :]
            vec_ref[_bag_row("conv_w", k), :] += jnp.sum(dxc * tap, axis=0, keepdims=True)
        dxc_ext[0:tb, :] = dxc
        ed = dxc_ext[...]
        dxa = ed * cw_ref[3:4, :]
        dxa = dxa + _shift_up(ed, 1) * cw_ref[2:3, :]
        dxa = dxa + _shift_up(ed, 2) * cw_ref[1:2, :]
        dxa = dxa + _shift_up(ed, 3) * cw_ref[0:1, :]
        dz_ref[:, 0:D_MODEL] = dxa[0:tb, :].astype(BF16)
        dxc_ext[tb:, :] = dxc[0:CONV_HIST, :]

        pos = blk * tb + lax.broadcasted_iota(jnp.int32, (tb, POOL_GROUP_DIM), 0)
        diff = _pool_diff(eb, pos)
        rider(2)
        ypre = _pool_mix(diff, pw_ref)
        ps = ps_ref[...]
        gb = gb_ref[...]
        sgb = _sigmoid(gb)
        dyb = dyb_ref[...]
        dyp = dyb * (gb * sgb)
        dz_ref[:, 2 * D_MODEL + POOL_WIDTH:3 * D_MODEL] = (
            dyb * (ypre * ps) * (sgb * (1.0 + gb * (1.0 - sgb)))).astype(BF16)
        vec_ref[_bag_row("pool_scale"), 0:POOL_WIDTH] += jnp.sum(dyp * ypre, axis=0, keepdims=True)
        dypre = dyp * ps
        for g, k in enumerate(POOL_WINDOWS):
            cols = slice(g * POOL_GROUP_DIM, (g + 1) * POOL_GROUP_DIM)
            dyg = dypre[:, cols].astype(BF16)
            mat_ref[mat_rows("pool_w", g), :] += _dot_tn(diff[g].astype(BF16), dyg)
            ddiff = _dot_nt(dyg, pw_ref[g])
            count = jnp.minimum(pos + 1, k).astype(F32)
            dwin = ddiff / count
            dwin_ext[0:tb, cols] = dwin
            s = dwin_ext[:, cols]
            for step_ in range(g + 1):
                s = s + _shift_up(s, 2 ** step_)
            dz_ref[:, 2 * D_MODEL + g * POOL_GROUP_DIM:2 * D_MODEL + (g + 1) * POOL_GROUP_DIM] = (
                s[0:tb, :] - ddiff).astype(BF16)
            dwin_ext[tb:, cols] = dwin[0:POOL_HIST, :]

        dz_ref[:, 3 * D_MODEL:] = dzm_ref[...]

        @pl.when(i == nb - 1)
        def _():
            row = _bag_row("lru_lambda")
            vec_ref[row, :] = vec_ref[row, :] * (-_sigmoid(-lam))
            for k in range(nr):
                grads[2 * k + 1][...] = grads[2 * k][...].astype(BF16)

    rev = lambda i: (nb - 1 - i, 0)
    fixed = lambda i: (0, 0)

    def prev(rows, col):
        per = tb // rows
        return lambda i: (jnp.maximum((nb - 1 - i) * per - 1, 0), col)

    in_specs = [pl.BlockSpec((tb, D_MODEL), lambda i: (nb - 1 - i, 0)),
                pl.BlockSpec((CONV_HIST, D_MODEL), prev(CONV_HIST, 0)),
                pl.BlockSpec((tb, D_MODEL), lambda i: (nb - 1 - i, 1)),
                pl.BlockSpec((tb, POOL_WIDTH), lambda i: (nb - 1 - i, 4)),
                pl.BlockSpec((POOL_HIST, POOL_WIDTH), prev(POOL_HIST, 4)),
                pl.BlockSpec((tb, POOL_WIDTH), lambda i: (nb - 1 - i, 5)),
                pl.BlockSpec((tb, D_MODEL), rev),
                pl.BlockSpec((F32_SUBLANES, D_MODEL), prev(F32_SUBLANES, 0)),
                pl.BlockSpec((tb, D_MODEL), rev), pl.BlockSpec((tb, POOL_WIDTH), rev),
                pl.BlockSpec((tb, 2 * D_MODEL), rev)] + _branch_specs(tb, rev, fixed) + [
                    pl.BlockSpec((VEC_BAG_ROWS, D_MODEL), fixed)]
    vec_at = len(in_specs) - 1
    out_shape = [jax.ShapeDtypeStruct((t, IN_COLS), BF16), jax.ShapeDtypeStruct((VEC_BAG_ROWS, D_MODEL), F32),
                 jax.ShapeDtypeStruct((MAT_BAG_ROWS, HEAD_DIM), F32)]
    out_specs = [pl.BlockSpec((tb, IN_COLS), rev), pl.BlockSpec((VEC_BAG_ROWS, D_MODEL), fixed),
                 pl.BlockSpec((MAT_BAG_ROWS, HEAD_DIM), fixed)]
    for lhs, rhs in riders:
        in_specs += [pl.BlockSpec((tb, lhs.shape[1]), rev), pl.BlockSpec((tb, rhs.shape[1]), rev)]
        grad = (lhs.shape[1], rhs.shape[1])
        out_shape += [jax.ShapeDtypeStruct(grad, F32), jax.ShapeDtypeStruct(grad, BF16)]
        out_specs += [pl.BlockSpec(grad, fixed)] * 2
    scratch = [pltpu.VMEM((tb + CONV_HIST, D_MODEL), F32), pltpu.VMEM((tb + POOL_HIST, POOL_WIDTH), F32),
               pltpu.VMEM((tb + F32_SUBLANES, D_MODEL), F32), pltpu.VMEM((tb + F32_SUBLANES, D_MODEL), F32),
               pltpu.VMEM((tb + CONV_HIST, D_MODEL), F32), pltpu.VMEM((tb + POOL_HIST, POOL_WIDTH), F32),
               pltpu.VMEM((F32_SUBLANES, D_MODEL), F32),
               pltpu.VMEM((tb, D_MODEL), F32), pltpu.VMEM((tb, D_MODEL), F32), pltpu.VMEM((tb, D_MODEL), F32)]
    return pl.pallas_call(
        body, name="branches_bwd", out_shape=tuple(out_shape), grid=(nb,), in_specs=in_specs,
        out_specs=tuple(out_specs), scratch_shapes=scratch, input_output_aliases={vec_at: 1},
        compiler_params=pltpu.CompilerParams(dimension_semantics=("arbitrary",),
                                             vmem_limit_bytes=VMEM_LIMIT_BYTES),
    )(z, z, z, z, z, z, hl, hl, dya, dyb, dzm, *weights, vec_bag, *[a for pair in riders for a in pair])


def _merge_head(x2d, ya, yb, z, p2d, tgt, w_pl, w_pp, w_out, w_pg, w_pe, g2, gf, tb):
    t = x2d.shape[0]
    p_dim = p2d.shape[1]

    def body(x_ref, ya_ref, yb_ref, ma_ref, mb_ref, p_ref, t_ref, wpl_ref, wpp_ref, wout_ref, wpg_ref, wpe_ref,
             g2_ref, gf_ref,
             bag_ref, dxr_ref, dya_ref, dyb_ref, dzm_ref,
             mg_ref, do_ref, hn_ref, dgp_ref, dpe_ref, da_ref, dbm_ref, pbf_ref):
        @pl.when(pl.program_id(0) == 0)
        def _():
            bag_ref[...] = jnp.zeros_like(bag_ref)

        a_ = _dot(ya_ref[...], wpl_ref[...])
        bm = _dot(yb_ref[...], wpp_ref[...])
        sa = _sigmoid(ma_ref[...])
        sb = _sigmoid(mb_ref[...])
        mg = (sa * a_ + sb * bm).astype(BF16)
        mg_ref[...] = mg
        x1 = x_ref[...] + _dot(mg, wout_ref[...])
        xn2, r2 = _rms(x1)
        g2 = g2_ref[...]
        hn = (xn2 * g2).astype(BF16)
        hn_ref[...] = hn
        gate = _sigmoid(_dot(hn, wpg_ref[...]))
        pbf = p_ref[...].astype(BF16)
        pbf_ref[...] = pbf
        pe = _dot(pbf, wpe_ref[...])
        x2 = x1 + gate * pe
        xn3, r3 = _rms(x2)
        gf = gf_ref[...]
        err = xn3 * gf - t_ref[...]
        bag_ref[_bag_rows("loss"), 0:128] += 0.5 * jnp.sum(jnp.mean(err * err, axis=-1))

        dy = err * (1.0 / D_MODEL)
        bag_ref[_bag_row("final_g"), :] += jnp.sum(dy * xn3, axis=0, keepdims=True)
        dx2 = _rms_bwd(dy * gf, xn3, r3)
        dpe_ref[...] = (dx2 * gate).astype(BF16)
        dgp = ((dx2 * pe) * (gate * (1.0 - gate))).astype(BF16)
        dgp_ref[...] = dgp
        dhn = _dot_nt(dgp, wpg_ref[...])
        bag_ref[_bag_row("ple_norm_g"), :] += jnp.sum(dhn * xn2, axis=0, keepdims=True)
        dx1 = dx2 + _rms_bwd(dhn * g2, xn2, r2)
        dxr_ref[...] = dx1
        do = dx1.astype(BF16)
        do_ref[...] = do
        dmg = _dot_nt(do, wout_ref[...])
        da = (dmg * sa).astype(BF16)
        dbm = (dmg * sb).astype(BF16)
        da_ref[...] = da
        dbm_ref[...] = dbm
        dzm_ref[:, 0:D_MODEL] = (dmg * a_ * (sa * (1.0 - sa))).astype(BF16)
        dzm_ref[:, D_MODEL:] = (dmg * bm * (sb * (1.0 - sb))).astype(BF16)
        dya_ref[...] = _dot_nt(da, wpl_ref[...])
        dyb_ref[...] = _dot_nt(dbm, wpp_ref[...])

    row = lambda i: (i, 0)
    fixed = lambda i: (0, 0)

    def resident(shape):
        return pl.BlockSpec(shape, fixed, pipeline_mode=pl.Buffered(1))

    tok = lambda width: pl.BlockSpec((tb, width), row)
    in_specs = [tok(D_MODEL), tok(D_MODEL), tok(POOL_WIDTH),
                pl.BlockSpec((tb, D_MODEL), lambda i: (i, 3)), pl.BlockSpec((tb, D_MODEL), lambda i: (i, 4)),
                tok(p_dim), tok(D_MODEL),
                resident((D_MODEL, D_MODEL)), resident((POOL_WIDTH, D_MODEL)), resident((D_MODEL, D_MODEL)),
                resident((D_MODEL, D_MODEL)), resident((p_dim, D_MODEL)),
                pl.BlockSpec((1, D_MODEL), fixed), pl.BlockSpec((1, D_MODEL), fixed)]
    bf = lambda width: jax.ShapeDtypeStruct((t, width), BF16)
    f32 = lambda width: jax.ShapeDtypeStruct((t, width), F32)
    out_shape = (jax.ShapeDtypeStruct((VEC_BAG_ROWS, D_MODEL), F32),
                 f32(D_MODEL), f32(D_MODEL), f32(POOL_WIDTH), bf(2 * D_MODEL),
                 bf(D_MODEL), bf(D_MODEL), bf(D_MODEL), bf(D_MODEL), bf(D_MODEL), bf(D_MODEL), bf(D_MODEL), bf(p_dim))
    out_specs = (pl.BlockSpec((VEC_BAG_ROWS, D_MODEL), fixed),
                 tok(D_MODEL), tok(D_MODEL), tok(POOL_WIDTH), tok(2 * D_MODEL),
                 tok(D_MODEL), tok(D_MODEL), tok(D_MODEL), tok(D_MODEL), tok(D_MODEL), tok(D_MODEL), tok(D_MODEL),
                 tok(p_dim))
    return pl.pallas_call(
        body, name="merge_head", out_shape=out_shape, grid=(t // tb,), in_specs=in_specs, out_specs=out_specs,
        compiler_params=pltpu.CompilerParams(dimension_semantics=("arbitrary",),
                                             vmem_limit_bytes=VMEM_LIMIT_BYTES),
    )(x2d, ya, yb, z, z, p2d, tgt, w_pl, w_pp, w_out, w_pg, w_pe, g2, gf)


def kernel(x, p, norm_g, w_in, conv_w, conv_b, lru_w_a, lru_b_a, lru_w_x, lru_b_x, lru_lambda, pool_w, pool_scale, w_proj_lru, w_proj_pool, w_out, ple_norm_g, w_ple_gate, w_ple_proj, final_g, loss_target, m_norm_g, m_w_in, m_conv_w, m_conv_b, m_lru_w_a, m_lru_b_a, m_lru_w_x, m_lru_b_x, m_lru_lambda, m_pool_w, m_pool_scale, m_w_proj_lru, m_w_proj_pool, m_w_out, m_ple_norm_g, m_w_ple_gate, m_w_ple_proj, m_final_g, v_norm_g, v_w_in, v_conv_w, v_conv_b, v_lru_w_a, v_lru_b_a, v_lru_w_x, v_lru_b_x, v_lru_lambda, v_pool_w, v_pool_scale, v_w_proj_lru, v_w_proj_pool, v_w_out, v_ple_norm_g, v_w_ple_gate, v_w_ple_proj, v_final_g):
    bsz, seq, _ = x.shape
    t = bsz * seq
    tb_mm = min(1024, seq)
    tb_seq = min(256, seq // 2) if seq >= 512 else seq
    x2d = x.reshape(t, D_MODEL)
    p2d = p.reshape(t, p.shape[-1])
    tgt = loss_target.reshape(t, D_MODEL)
    chip = 2 * lax.axis_index("x") + lax.axis_index("y")

    rest = [(w_proj_lru[0], 0), (w_proj_pool[0], 1), (w_out[0], 0), (w_ple_gate[0], 0), (w_ple_proj[0], 1)]
    z, h_bf, w_in_f, conv_w_f, *narrow = _in_proj_gather(
        x2d, norm_g, w_in[0], [(conv_w[0], 1, False)], tb_mm,
        [w for w, _ in rest] + [lru_w_a[0], lru_w_x[0], pool_w[0]])
    wa_bf, wx_bf, pw_bf = narrow[len(rest):]
    branch_w = (conv_w_f, conv_b, wa_bf, lru_b_a.reshape(1, D_MODEL), wx_bf, lru_b_x.reshape(1, D_MODEL),
                lru_lambda, pw_bf, pool_scale)

    ya, yb, hl, w_pl_f, w_pp_f, w_out_f, w_pg_f, w_pe_f = _branches_fwd(
        z, branch_w, seq, tb_seq, [(w16, axis, True) for w16, (_, axis) in zip(narrow, rest)])
    (vec_bag, dx_res, dya, dyb, dzm, mg_bf, do_bf, hn_bf, dgp_bf, dpe_bf, da_bf, dbm_bf, p_bf) = _merge_head(
        x2d, ya, yb, z, p2d, tgt, w_pl_f, w_pp_f, w_out_f, w_pg_f, w_pe_f, ple_norm_g, final_g.reshape(1, D_MODEL),
        tb_seq)
    dz, vec_bag, mat_bag, g_out, g_out16, g_pp, g_pp16, g_pe, g_pe16 = _branches_bwd(
        z, hl, dya, dyb, dzm, branch_w, vec_bag, seq, tb_seq, [(mg_bf, do_bf), (yb, dbm_bf), (p_bf, dpe_bf)])

    tb_dw = min(1024, seq)
    def row_pieces(g32, g16):
        pieces = (8, g32.shape[0] // 8, g32.shape[1])
        return g32.reshape(pieces), False, g16.reshape(pieces)

    def proj_grad(lhs, rhs, name):
        g32, g16 = _weight_grad(lhs, rhs, 1, tb_dw, name)
        return row_pieces(g32[0], g16[0])

    p_dim = p2d.shape[1]
    proj_parts = [proj_grad(ya, da_bf, "dw_proj_lru"), (g_pp, True, g_pp16), row_pieces(g_out, g_out16),
                  proj_grad(hn_bf, dgp_bf, "dw_ple_gate"), (g_pe, True, g_pe16)]
    nb_dw = t // tb_dw
    g_in, g_in16, r_pl, r_pp, r_out, r_pg, r_pe, vec_mine, mat_mine = _weight_grad(
        h_bf, dz, N_CHIPS, tb_dw, "dw_in",
        reduce=(proj_parts + [(vec_bag.reshape(8, VEC_BAG_ROWS // 8, D_MODEL), False, None),
                              (mat_bag.reshape(8, MAT_BAG_ROWS // 8, HEAD_DIM), False, None)],
                [BF16] * 5 + [F32] * 2,
                (0, nb_dw // 2, 2 * nb_dw - 1, 3 * nb_dw + nb_dw // 2, N_CHIPS * nb_dw - 1)))
    pieces = (8, D_MODEL // 2, IN_COLS // N_CHIPS)
    nb_seq = t // tb_seq
    dx, g_g1, r_in, vec_sum, mat_sum = _in_proj_bwd(
        dz, w_in_f, x2d, dx_res, norm_g, tb_seq,
        reduce=([(g_in.reshape(pieces), False, g_in16.reshape(pieces))], BF16,
                (0, nb_seq // 8, nb_seq // 2, nb_seq - 1, nb_seq - 1)),
        shards=[(vec_mine.reshape(VEC_BAG_ROWS // N_CHIPS, D_MODEL), 0, True),
                (mat_mine.reshape(MAT_BAG_ROWS // N_CHIPS, HEAD_DIM), 0, True)])

    u_in = tuple(a[None] for a in _adamw(w_in[0], r_in.reshape(D_MODEL, IN_COLS // N_CHIPS), m_w_in[0], v_w_in[0],
                                         D_MODEL // 4, "adamw_w_in"))
    proj = [(w_proj_lru, r_pl, m_w_proj_lru, v_w_proj_lru), (w_proj_pool, r_pp, m_w_proj_pool, v_w_proj_pool),
            (w_out, r_out, m_w_out, v_w_out), (w_ple_gate, r_pg, m_w_ple_gate, v_w_ple_gate),
            (w_ple_proj, r_pe, m_w_ple_proj, v_w_ple_proj)]
    u_pl, u_pp, u_out, u_pg, u_pe = [tuple(a[None] for a in u) for u in _adamw_group(
        [(w[0], g.reshape(w.shape[1:]), m[0], v[0]) for w, g, m, v in proj], "adamw_proj")]

    small = [("norm_g", norm_g, m_norm_g, v_norm_g), ("conv_b", conv_b, m_conv_b, v_conv_b),
             ("lru_w_a", lru_w_a, m_lru_w_a, v_lru_w_a), ("lru_b_a", lru_b_a, m_lru_b_a, v_lru_b_a),
             ("lru_w_x", lru_w_x, m_lru_w_x, v_lru_w_x), ("lru_b_x", lru_b_x, m_lru_b_x, v_lru_b_x),
             ("lru_lambda", lru_lambda, m_lru_lambda, v_lru_lambda), ("pool_w", pool_w, m_pool_w, v_pool_w),
             ("pool_scale", pool_scale, m_pool_scale, v_pool_scale),
             ("ple_norm_g", ple_norm_g, m_ple_norm_g, v_ple_norm_g), ("final_g", final_g, m_final_g, v_final_g)]

    def view(a):
        return a.reshape(-1, a.shape[-1]) if a.ndim != 3 else a[0]

    cw_at = F32_SUBLANES * VEC_BAG_SLOTS.index("conv_w")
    cw_cols = D_MODEL // N_CHIPS
    g_cw = lax.dynamic_slice(vec_sum, (cw_at, chip * cw_cols), (CONV_WIDTH, cw_cols))
    flat = _adamw_replicated(vec_sum, mat_sum, g_g1, [(name,) + tuple(view(a) for a in arrs) for name, *arrs in small],
                             (conv_w[0], m_conv_w[0], v_conv_w[0], g_cw))
    u_small = {name: tuple(flat[4 * k + pick].reshape(arrs[0].shape) for pick in range(4))
               for k, (name, *arrs) in enumerate(small)}
    u_cw = tuple(a[None] for a in flat[4 * len(small):4 * len(small) + 4])

    loss = flat[-1].reshape(())
    grad_x = dx.reshape(bsz, seq, D_MODEL)

    def ordered(pick):
        s = {name: u[pick] for name, u in u_small.items()}
        return [s["norm_g"], u_in[pick], u_cw[pick], s["conv_b"], s["lru_w_a"], s["lru_b_a"], s["lru_w_x"], s["lru_b_x"],
                s["lru_lambda"], s["pool_w"], s["pool_scale"], u_pl[pick], u_pp[pick], u_out[pick], s["ple_norm_g"],
                u_pg[pick], u_pe[pick], s["final_g"]]

    return (loss, grad_x, *ordered(0), *ordered(1), *ordered(2), *ordered(3))
```

```python
import jax
import jax.numpy as jnp
from jax import lax
from jax.experimental import pallas as pl
from jax.experimental.pallas import tpu as pltpu

F32 = jnp.float32
BF16 = jnp.bfloat16
MESH = pl.DeviceIdType.MESH

D_MODEL = 1024
LRU_HEADS = 8
HEAD_DIM = 128
CONV_WIDTH = 4
LRU_C = 8.0
POOL_WIDTH = 512
POOL_WINDOWS = (2, 4, 8, 16)
POOL_GROUP_DIM = 128
IN_COLS = 5120
N_CHIPS = 4
EPS = 1e-6

ADAM_LR = 0.001
ADAM_B1 = 0.9
ADAM_B2 = 0.999
ADAM_EPS = 1e-08
ADAM_WD = 0.01
ADAM_STEP = 10

F32_SUBLANES = 8
CONV_HIST = 8
POOL_HIST = 16
VMEM_LIMIT_BYTES = 58 * 1024 * 1024
VEC_BAG_SLOTS = ("norm_g", "conv_w", "conv_b", "lru_b_a", "lru_b_x", "lru_lambda", "pool_scale", "ple_norm_g",
                 "final_g", "loss")
VEC_BAG_ROWS = 128
MAT_BAG_AT = {"lru_w_a": 0, "lru_w_x": LRU_HEADS * HEAD_DIM, "pool_w": 2 * LRU_HEADS * HEAD_DIM}
MAT_BAG_ROWS = 2 * LRU_HEADS * HEAD_DIM + len(POOL_WINDOWS) * POOL_GROUP_DIM


def _bag_row(name, k=0):
    at = F32_SUBLANES * VEC_BAG_SLOTS.index(name) + k
    return slice(at, at + 1)


def _bag_rows(name):
    at = F32_SUBLANES * VEC_BAG_SLOTS.index(name)
    return slice(at, at + F32_SUBLANES)


def _dot(a, b):
    return jnp.dot(a, b, preferred_element_type=F32)


def _dot_nt(a, b):
    return lax.dot_general(a, b, (((1,), (1,)), ((), ())), preferred_element_type=F32)


def _dot_tn(a, b):
    return lax.dot_general(a, b, (((0,), (0,)), ((), ())), preferred_element_type=F32)


def _sigmoid(v):
    return jax.nn.sigmoid(v)


def _softplus(v):
    return jnp.maximum(v, 0.0) + jnp.log1p(jnp.exp(-jnp.abs(v)))


def _place():
    return lax.axis_index("x"), lax.axis_index("y"), lax.axis_index("c")


GATHER_SEMS = 6


def _gather_shapes(shards):
    out_shape = []
    for arr, axis, _ in shards:
        r, cols = arr.shape
        out_shape.append(jax.ShapeDtypeStruct((N_CHIPS * r, cols) if axis == 0 else (r, N_CHIPS * cols), arr.dtype))
    n = len(shards)
    sems = [pltpu.SemaphoreType.DMA((n * GATHER_SEMS,)), pltpu.SemaphoreType.DMA((n * GATHER_SEMS,)),
            pltpu.SemaphoreType.DMA((n,))]
    return out_shape, sems


def _gather_steps(shards, ins, outs, send_sems, recv_sems, local_sems):
    n = len(shards)
    x, y, c = _place()
    me, sibling = (x, y, c), (x, y, 1 - c)
    chips = [(x, 1 - y), (1 - x, y), (1 - x, 1 - y)]

    def region(k, cx, cy, hc):
        (r, cols), axis = shards[k][0].shape, shards[k][1]
        j = 2 * cx + cy
        if axis == 0:
            if hc is None:
                return outs[k].at[pl.ds(j * r, r), :]
            return outs[k].at[pl.ds(j * r + hc * (r // 2), r // 2), :]
        if hc is None:
            return outs[k].at[:, pl.ds(j * cols, cols)]
        return outs[k].at[pl.ds(hc * (r // 2), r // 2), pl.ds(j * cols, cols)]

    def remote(k, sem, block, to, src=None):
        dst = region(k, *block)
        return pltpu.make_async_remote_copy(
            src_ref=dst if src is None else src, dst_ref=dst,
            send_sem=send_sems.at[k * GATHER_SEMS + sem], recv_sem=recv_sems.at[k * GATHER_SEMS + sem],
            device_id=to, device_id_type=MESH)

    def first(k, idx):
        r, split = shards[k][0].shape[0], shards[k][2]
        src = ins[k].at[pl.ds(c * (r // 2), r // 2), :] if split else ins[k]
        return remote(k, idx, (x, y, c if split else None), (*chips[idx], c), src=src)

    def relay(k):
        src_chip = (jnp.bitwise_xor(x, 1 - c), jnp.bitwise_xor(y, c))
        dst_chip = (jnp.bitwise_xor(x, c), jnp.bitwise_xor(y, 1 - c))
        return remote(k, 2, (*src_chip, c), (*dst_chip, c))

    def passed(k, idx):
        return remote(k, 3 + idx, (*chips[idx], c), sibling)

    def mine(k):
        return pltpu.make_async_copy(ins[k], region(k, x, y, None), local_sems.at[k])

    def start():
        for k in range(n):
            mine(k).start()
            for idx in range(2 if shards[k][2] else 3):
                first(k, idx).start()

    def relay_on():
        for k in range(n):
            split = shards[k][2]
            for idx in range(2):
                remote(k, idx, (*chips[idx], c if split else None), me).wait_recv()
            if split:
                relay(k).start()
                passed(k, 0).start()
                passed(k, 1).start()

    def finish():
        for k in range(n):
            split = shards[k][2]
            remote(k, 2, (*chips[2], c if split else None), me).wait_recv()
            if split:
                passed(k, 2).start()
        for k in range(n):
            if shards[k][2]:
                for idx in range(3):
                    remote(k, 3 + idx, (*chips[idx], 1 - c), me).wait_recv()
        for k in range(n):
            if shards[k][2]:
                for cp in (first(k, 0), first(k, 1), relay(k), passed(k, 0), passed(k, 1), passed(k, 2)):
                    cp.wait_send()
            else:
                for idx in range(3):
                    first(k, idx).wait_send()
            mine(k).wait()

    return start, relay_on, finish


RS_ADD_ROWS = (64, 32, 16, 8)


N_DEV = 2 * N_CHIPS


def _all_reduce_scratch(shape):
    return [pltpu.VMEM((N_DEV,) + tuple(shape), F32), pltpu.SemaphoreType.DMA((N_DEV - 1,)),
            pltpu.SemaphoreType.DMA((N_DEV - 1,))]


def _all_reduce_tile(v_ref, o_ref, slots, send_sems, recv_sems):
    flips = [(dx, dy, dc) for dx in (0, 1) for dy in (0, 1) for dc in (0, 1)][1:]
    x, y, c = _place()
    mine = 4 * x + 2 * y + c

    def copy(k, to_flip, slot):
        dx, dy, dc = to_flip
        peer = (jnp.bitwise_xor(x, dx), jnp.bitwise_xor(y, dy), jnp.bitwise_xor(c, dc))
        return pltpu.make_async_remote_copy(
            src_ref=v_ref, dst_ref=slots.at[slot], send_sem=send_sems.at[k], recv_sem=recv_sems.at[k],
            device_id=peer, device_id_type=MESH)

    sends = [copy(k, flip, mine) for k, flip in enumerate(flips)]
    for cp in sends:
        cp.start()
    slots[mine] = v_ref[...]
    for k, (dx, dy, dc) in enumerate(flips):
        copy(k, (dx, dy, dc), jnp.bitwise_xor(mine, 4 * dx + 2 * dy + dc)).wait_recv()
    total = slots[0]
    for d in range(1, N_DEV):
        total = total + slots[d]
    o_ref[...] = total
    for cp in sends:
        cp.wait_send()


RS_SEMS = 8
RS_LOCAL_SEMS = 5


def _rs_piece_shape(part):
    arr, cols = part[0], part[1]
    return (arr.shape[0] // 2, arr.shape[1] // N_CHIPS) if cols else tuple(arr.shape[1:])


def _rs_operands(parts):
    return [p[0] for p in parts] + [p[0] if p[2] is None else p[2] for p in parts]


def _rs_wires(parts, wire):
    return list(wire) if isinstance(wire, (list, tuple)) else [wire] * len(parts)


def _rs_shapes(parts, wire):
    n = len(parts)
    shapes = [_rs_piece_shape(p) for p in parts]
    out_shape = [jax.ShapeDtypeStruct((2,) + s, F32) for s in shapes]
    scratch = []
    for lead, kind in ((N_CHIPS, "f32"), (N_CHIPS, "narrow"), (N_CHIPS, "wire"), (None, "f32"), (N_CHIPS, "wire")):
        for s, p, w in zip(shapes, parts, _rs_wires(parts, wire)):
            dtype = {"f32": F32, "narrow": F32 if p[2] is None else p[2].dtype, "wire": w}[kind]
            scratch.append(pltpu.VMEM(s if lead is None else (lead,) + s, dtype))
    scratch += [pltpu.SemaphoreType.DMA((n * RS_SEMS,)), pltpu.SemaphoreType.DMA((n * RS_SEMS,)),
                pltpu.SemaphoreType.DMA((n * RS_LOCAL_SEMS,))]
    return out_shape, scratch


def _rs_steps(parts, ins, outs, scratch):
    n = len(parts)
    own, sib, got, fin, snd = (scratch[k * n:(k + 1) * n] for k in range(5))
    send_sems, recv_sems, local_sems = scratch[5 * n:]
    shapes = [_rs_piece_shape(p) for p in parts]
    x, y, c = _place()
    j_me = 2 * x + y
    me, sibling = (x, y, c), (x, y, 1 - c)

    def piece(a, jj, core, narrow=False):
        ref = ins[n + a] if narrow else ins[a]
        if parts[a][1]:
            r, cl = shapes[a]
            return ref.at[pl.ds(core * r, r), pl.ds(jj * cl, cl)]
        return ref.at[2 * jj + core]

    def remote(a, sem, src, dst, to):
        return pltpu.make_async_remote_copy(
            src_ref=src, dst_ref=dst, send_sem=send_sems.at[a * RS_SEMS + sem],
            recv_sem=recv_sems.at[a * RS_SEMS + sem], device_id=to, device_id_type=MESH)

    def rows_loop(a, fn):
        r = shapes[a][0]
        step = max(s for s in RS_ADD_ROWS if r % s == 0)

        def it(i, carry):
            fn(pl.ds(pl.multiple_of(i * step, step), step))
            return carry

        lax.fori_loop(0, r // step, it, 0)

    def load(a, jj):
        return pltpu.make_async_copy(piece(a, jj, c), own[a].at[jj], local_sems.at[a * RS_LOCAL_SEMS + jj])

    def to_sibling(a, jj):
        return remote(a, jj, piece(a, jj, 1 - c, narrow=True), sib[a].at[jj], sibling)

    near = (jnp.bitwise_xor(x, 1 - c), jnp.bitwise_xor(y, c))
    far = (jnp.bitwise_xor(x, c), jnp.bitwise_xor(y, 1 - c))
    diag = (1 - x, 1 - y)
    FROM_NEAR, FROM_FAR, FEED = 0, 1, 2

    def chip_of(chip):
        return 2 * chip[0] + chip[1]

    def feed(a):
        return remote(a, 4, snd[a].at[chip_of(diag)], got[a].at[FEED], (*near, c))

    def to_near(a):
        return remote(a, 5, snd[a].at[chip_of(near)], got[a].at[FROM_NEAR], (*near, c))

    def to_far(a):
        return remote(a, 6, snd[a].at[chip_of(far)], got[a].at[FROM_FAR], (*far, c))

    def store(a):
        return pltpu.make_async_copy(fin[a], outs[a].at[c], local_sems.at[a * RS_LOCAL_SEMS + 4])

    def result_to_sibling(a):
        return remote(a, 7, fin[a], outs[a].at[c], sibling)

    def exchange():
        for a in range(n):
            for jj in range(N_CHIPS):
                load(a, jj).start()
                to_sibling(a, jj).start()

    def chip_sums():
        for a in range(n):
            for jj in range(N_CHIPS):
                load(a, jj).wait()
                remote(a, jj, sib[a].at[jj], sib[a].at[jj], me).wait_recv()

                def add(sl, a=a, jj=jj):
                    q = own[a][jj, sl, :] + sib[a][jj, sl, :].astype(F32)
                    own[a][jj, sl, :] = q
                    snd[a][jj, sl, :] = q.astype(snd[a].dtype)

                rows_loop(a, add)
        for a in range(n):
            feed(a).start()
        for a in range(n):
            to_near(a).start()

    def relay():
        for a in range(n):
            remote(a, 4, got[a].at[FEED], got[a].at[FEED], me).wait_recv()

            def add(sl, a=a):
                pair = own[a][chip_of(far), sl, :] + got[a][FEED, sl, :].astype(F32)
                snd[a][chip_of(far), sl, :] = pair.astype(snd[a].dtype)

            rows_loop(a, add)
            to_far(a).start()

    def totals():
        for a in range(n):
            remote(a, 5, got[a].at[FROM_NEAR], got[a].at[FROM_NEAR], me).wait_recv()
            remote(a, 6, got[a].at[FROM_FAR], got[a].at[FROM_FAR], me).wait_recv()

            def total(sl, a=a):
                fin[a][sl, :] = (own[a][j_me, sl, :] + got[a][FROM_NEAR, sl, :].astype(F32)) + (
                    got[a][FROM_FAR, sl, :].astype(F32))

            rows_loop(a, total)
            store(a).start()
            result_to_sibling(a).start()

    def finish():
        for a in range(n):
            remote(a, 7, outs[a].at[1 - c], outs[a].at[1 - c], me).wait_recv()
        for a in range(n):
            for jj in range(N_CHIPS):
                to_sibling(a, jj).wait_send()
            for cp in (feed(a), to_near(a), to_far(a), result_to_sibling(a)):
                cp.wait_send()
            store(a).wait()

    return exchange, chip_sums, relay, totals, finish


def _rms(x):
    r = lax.rsqrt(jnp.mean(x * x, axis=-1, keepdims=True) + EPS)
    return x * r, r


def _rms_bwd(dxn, xn, r):
    return r * (dxn - xn * jnp.mean(dxn * xn, axis=-1, keepdims=True))


def _in_proj_gather(x2d, norm_g, w_in_sh, shards, tb, casts):
    t = x2d.shape[0]
    nb = t // tb
    cols = IN_COLS // N_CHIPS
    half = D_MODEL // 2
    n = len(shards)
    nc = len(casts)

    def body(x_ref, g_ref, win_ref, *refs):
        ins, cast_ins = refs[:n], refs[n:n + nc]
        z_ref, h_ref, wfull_ref = refs[n + nc:n + nc + 3]
        outs, cast_outs = refs[n + nc + 3:2 * n + nc + 3], refs[2 * n + nc + 3:2 * (n + nc) + 3]
        scratch = refs[2 * (n + nc) + 3:]
        wv, h_all, send_sems, recv_sems, local_sems, w_send, w_recv, w_local, stage = scratch[:9]
        wide, narrow, cast_sems = scratch[9:9 + nc], scratch[9 + nc:9 + 2 * nc], scratch[9 + 2 * nc]
        s, i = pl.program_id(0), pl.program_id(1)
        x, y, c = _place()
        me, sibling = (x, y, c), (x, y, 1 - c)
        chips = [(x, 1 - y), (1 - x, y), (1 - x, 1 - y)]

        def w_half(cx, cy, hc):
            return wv.at[2 * cx + cy, pl.ds(hc * half, half), :]

        def w_remote(sem, block, to, src=None):
            dst = w_half(*block)
            return pltpu.make_async_remote_copy(
                src_ref=dst if src is None else src, dst_ref=dst, send_sem=w_send.at[sem],
                recv_sem=w_recv.at[sem], device_id=to, device_id_type=MESH)

        def w_first(idx):
            return w_remote(idx, (x, y, c), (*chips[idx], c))

        def w_relay():
            src_chip = (jnp.bitwise_xor(x, 1 - c), jnp.bitwise_xor(y, c))
            dst_chip = (jnp.bitwise_xor(x, c), jnp.bitwise_xor(y, 1 - c))
            return w_remote(2, (*src_chip, c), (*dst_chip, c))

        def w_pass(idx):
            return w_remote(3 + idx, (*chips[idx], c), sibling)

        def w_store(k, cx, cy):
            jj = 2 * cx + cy
            return pltpu.make_async_copy(wv.at[jj], wfull_ref.at[:, pl.ds(jj * cols, cols)], w_local.at[k])

        start_rest, relay_rest, finish_rest = _gather_steps(shards, ins, outs, send_sems, recv_sems, local_sems)

        def own(k, hc):
            return pltpu.make_async_copy(win_ref.at[pl.ds(pl.multiple_of(hc * half, half), half), :], stage.at[k],
                                         w_local.at[4 + 2 * k])

        def round_own(k, hc):
            own(k, hc).wait()
            wv[2 * x + y, pl.ds(pl.multiple_of(hc * half, half), half), :] = stage[k].astype(BF16)

        wide_in = [pltpu.make_async_copy(cast_ins[k], wide[k], cast_sems.at[k]) for k in range(nc)]
        narrow_out = [pltpu.make_async_copy(narrow[k], cast_outs[k], cast_sems.at[nc + k]) for k in range(nc)]

        @pl.when((s == 0) & (i == 0))
        def _():
            own(0, c).start()
            own(1, 1 - c).start()
            for cp in wide_in:
                cp.start()
            round_own(0, c)
            w_first(0).start()
            w_first(1).start()
            start_rest()
            round_own(1, 1 - c)
            w_store(0, x, y).start()

        @pl.when((s == 1) & (i == 0))
        def _():
            for k in range(nc):
                wide_in[k].wait()
                narrow[k][...] = wide[k][...].astype(BF16)
                narrow_out[k].start()
            w_remote(0, (*chips[0], c), me).wait_recv()
            w_remote(1, (*chips[1], c), me).wait_recv()
            w_relay().start()
            w_pass(0).start()
            w_pass(1).start()
            w_remote(3, (*chips[0], 1 - c), me).wait_recv()
            w_store(1, *chips[0]).start()

        @pl.when((s == 2) & (i == 0))
        def _():
            w_remote(4, (*chips[1], 1 - c), me).wait_recv()
            w_store(2, *chips[1]).start()

        @pl.when((s == 3) & (i == 0))
        def _():
            w_remote(2, (*chips[2], c), me).wait_recv()
            w_pass(2).start()
            w_remote(5, (*chips[2], 1 - c), me).wait_recv()
            w_store(3, *chips[2]).start()

        keep_h = pltpu.make_async_copy(h_all.at[i], h_ref.at[pl.ds(pl.multiple_of(i * tb, tb), tb), :], w_local.at[5])

        @pl.when(s == 0)
        def _():
            xn, _ = _rms(x_ref[...])
            h_all[i] = (xn * g_ref[...]).astype(BF16)
            keep_h.start()

        z_ref[...] = _dot(h_all[i], wv[jnp.bitwise_xor(2 * x + y, s)])
        pl.when(s == 0)(keep_h.wait)

        @pl.when((s == N_CHIPS - 1) & (i == nb - 1))
        def _():
            relay_rest()
            finish_rest()
            for cp in (w_first(0), w_first(1), w_relay(), w_pass(0), w_pass(1), w_pass(2)):
                cp.wait_send()
            w_store(0, x, y).wait()
            for idx in range(3):
                w_store(idx + 1, *chips[idx]).wait()
            for cp in narrow_out:
                cp.wait()

    rest_shape, rest_sems = _gather_shapes(shards)
    out_shape = [jax.ShapeDtypeStruct((t, IN_COLS), F32), jax.ShapeDtypeStruct((t, D_MODEL), BF16),
                 jax.ShapeDtypeStruct((D_MODEL, IN_COLS), BF16)] + rest_shape
    out_shape += [jax.ShapeDtypeStruct(a.shape, BF16) for a in casts]
    any_spec = pl.BlockSpec(memory_space=pl.ANY)

    def z_map(s, i):
        return (i, jnp.bitwise_xor(2 * lax.axis_index("x") + lax.axis_index("y"), s))

    return pl.pallas_call(
        body, name="in_proj", out_shape=tuple(out_shape),
        grid=(N_CHIPS, nb),
        in_specs=[pl.BlockSpec((tb, D_MODEL), lambda s, i: (jnp.where(s == 0, i, nb - 1), 0)),
                  pl.BlockSpec((1, D_MODEL), lambda s, i: (0, 0)), any_spec] + [any_spec] * (n + nc),
        out_specs=tuple([pl.BlockSpec((tb, cols), z_map), any_spec, any_spec] + [any_spec] * (n + nc)),
        scratch_shapes=[pltpu.VMEM((N_CHIPS, D_MODEL, cols), BF16), pltpu.VMEM((nb, tb, D_MODEL), BF16)] + rest_sems + [
            pltpu.SemaphoreType.DMA((GATHER_SEMS,)), pltpu.SemaphoreType.DMA((GATHER_SEMS,)),
            pltpu.SemaphoreType.DMA((N_CHIPS + 3,)), pltpu.VMEM((2, half, cols), F32)]
        + [pltpu.VMEM(a.shape, F32) for a in casts] + [pltpu.VMEM(a.shape, BF16) for a in casts]
        + [pltpu.SemaphoreType.DMA((2 * nc,))],
        compiler_params=pltpu.CompilerParams(dimension_semantics=("arbitrary", "arbitrary"),
                                             vmem_limit_bytes=VMEM_LIMIT_BYTES),
    )(x2d, norm_g, w_in_sh, *[sh[0] for sh in shards], *casts)


def _in_proj_bwd(dz, w_in, x2d, dx_res, norm_g, tb, reduce, shards, take):
    t = x2d.shape[0]
    nb = t // tb
    parts, wire, steps = reduce
    n = len(parts)
    k = len(shards)
    take_rows, take_width = take

    def body(dz_ref, w_ref, x_ref, dres_ref, g_ref, *refs):
        at = 2 * n + k
        dx_ref, dg_ref = refs[at:at + 2]
        rs_outs, g_outs = refs[at + 2:at + 2 + n], refs[at + 2 + n:at + 2 + n + k]
        cut_ref = refs[at + 2 + n + k]
        scratch = refs[at + 3 + n + k:]
        rs_scr, g_sems, dg_acc, ar_scr, cut_sem = scratch[:-8], scratch[-8:-5], scratch[-5], scratch[-4:-1], scratch[-1]
        rs = _rs_steps(parts, refs[:2 * n], rs_outs, rs_scr)
        for step, when in zip(rs[:-1], steps):
            pl.when(pl.program_id(0) == when)(step)
        gather = _gather_steps(shards, refs[2 * n:at], g_outs, *g_sems)
        for step, when in zip(gather, (0, nb // 2, nb - 1)):
            pl.when(pl.program_id(0) == when)(step)

        @pl.when(pl.program_id(0) == 0)
        def _():
            dg_acc[...] = jnp.zeros_like(dg_acc)

        xn, r = _rms(x_ref[...])
        g = g_ref[...]
        dh = _dot_nt(dz_ref[...], w_ref[...])
        dg_acc[0:1, :] += jnp.sum(dh * xn, axis=0, keepdims=True)
        dx_ref[...] = dres_ref[...] + _rms_bwd(dh * g, xn, r)

        @pl.when(pl.program_id(0) == nb - 1)
        def _():
            x, y, _ = _place()
            mine = pl.ds(pl.multiple_of((2 * x + y) * take_width, take_width), take_width)
            cut = pltpu.make_async_copy(g_outs[0].at[take_rows, mine], cut_ref, cut_sem)
            cut.start()
            _all_reduce_tile(dg_acc, dg_ref, *ar_scr)
            rs[-1]()
            cut.wait()

    row = lambda i: (i, 0)
    fixed = lambda i: (0, 0)
    rs_shape, rs_scratch = _rs_shapes(parts, wire)
    g_shape, g_sems = _gather_shapes(shards)
    any_spec = pl.BlockSpec(memory_space=pl.ANY)
    cut_shape = jax.ShapeDtypeStruct((take_rows.stop - take_rows.start, take_width), F32)
    return pl.pallas_call(
        body, name="in_proj_bwd",
        out_shape=tuple([jax.ShapeDtypeStruct((t, D_MODEL), F32), jax.ShapeDtypeStruct((F32_SUBLANES, D_MODEL), F32)]
                        + rs_shape + g_shape + [cut_shape]),
        grid=(nb,),
        in_specs=[pl.BlockSpec((tb, IN_COLS), row),
                  pl.BlockSpec((D_MODEL, IN_COLS), fixed, pipeline_mode=pl.Buffered(1)),
                  pl.BlockSpec((tb, D_MODEL), row), pl.BlockSpec((tb, D_MODEL), row),
                  pl.BlockSpec((1, D_MODEL), fixed)] + [any_spec] * (2 * n + k),
        out_specs=tuple([pl.BlockSpec((tb, D_MODEL), row), pl.BlockSpec((F32_SUBLANES, D_MODEL), fixed)]
                        + [any_spec] * (n + k + 1)),
        scratch_shapes=rs_scratch + g_sems + [pltpu.VMEM((F32_SUBLANES, D_MODEL), F32)] + _all_reduce_scratch(
            (F32_SUBLANES, D_MODEL)) + [pltpu.SemaphoreType.DMA(())],
        compiler_params=pltpu.CompilerParams(dimension_semantics=("arbitrary",),
                                             vmem_limit_bytes=VMEM_LIMIT_BYTES),
    )(dz, w_in, x2d, dx_res, norm_g, *_rs_operands(parts), *[sh[0] for sh in shards])


def _weight_grad(lhs, rhs, n_chunks, tb, name, reduce=None):
    t, k = lhs.shape
    nc = rhs.shape[1] // n_chunks
    nb = t // tb
    parts, wire, steps = reduce if reduce is not None else ([], F32, ())
    n = len(parts)

    def body(l_ref, r_ref, *refs):
        o_ref, o16_ref = refs[2 * n:2 * n + 2]
        if n:
            at = pl.program_id(0) * nb + pl.program_id(1)
            rs = _rs_steps(parts, refs[:2 * n], refs[2 * n + 2:3 * n + 2], refs[3 * n + 2:])
            for step, when in zip(rs, steps):
                pl.when(at == when)(step)

        @pl.when(pl.program_id(1) == 0)
        def _():
            o_ref[...] = jnp.zeros_like(o_ref)

        o_ref[...] += _dot_tn(l_ref[...], r_ref[...])

        @pl.when(pl.program_id(1) == nb - 1)
        def _():
            o16_ref[...] = o_ref[...].astype(BF16)

    rs_shape, rs_scratch = _rs_shapes(parts, wire) if n else ([], [])
    any_spec = pl.BlockSpec(memory_space=pl.ANY)
    chunk = pl.BlockSpec((None, k, nc), lambda j, i: (j, 0, 0))
    return pl.pallas_call(
        body, name=name,
        out_shape=tuple([jax.ShapeDtypeStruct((n_chunks, k, nc), F32), jax.ShapeDtypeStruct((n_chunks, k, nc), BF16)]
                        + rs_shape),
        grid=(n_chunks, nb),
        in_specs=[pl.BlockSpec((tb, k), lambda j, i: (i, 0)), pl.BlockSpec((tb, nc), lambda j, i: (i, j))]
        + [any_spec] * (2 * n),
        out_specs=tuple([chunk, chunk] + [any_spec] * n),
        scratch_shapes=rs_scratch,
        compiler_params=pltpu.CompilerParams(dimension_semantics=("arbitrary", "arbitrary"),
                                             vmem_limit_bytes=VMEM_LIMIT_BYTES),
    )(lhs, rhs, *_rs_operands(parts))


def _adam_update(w, g, m, v):
    m_ = ADAM_B1 * m + (1.0 - ADAM_B1) * g
    v_ = ADAM_B2 * v + (1.0 - ADAM_B2) * jnp.square(g)
    m_hat = m_ / (1.0 - ADAM_B1 ** ADAM_STEP)
    v_hat = v_ / (1.0 - ADAM_B2 ** ADAM_STEP)
    return -ADAM_LR * (m_hat / (jnp.sqrt(v_hat) + ADAM_EPS) + ADAM_WD * w), m_, v_


def _adamw_replicated(vec_sum, mat_sum, norm_grad, entries, conv):
    n = len(entries)

    def grad_of(name, shape, vec_ref, mat_ref, norm_ref):
        if name == "norm_g":
            return norm_ref[0:1, :]
        if name in MAT_BAG_AT:
            return mat_ref[MAT_BAG_AT[name]:MAT_BAG_AT[name] + shape[0], :]
        if shape[0] == 1:
            return vec_ref[_bag_row(name), 0:shape[1]]
        return jnp.concatenate([vec_ref[_bag_row(name), h * shape[1]:(h + 1) * shape[1]] for h in range(shape[0])],
                               axis=0)

    def body(vec_ref, mat_ref, norm_ref, *refs):
        ins, outs = refs[:3 * n + 4], refs[3 * n + 4:]
        for k in range(n):
            w_ref, m_ref, v_ref = ins[3 * k:3 * k + 3]
            g = grad_of(entries[k][0], w_ref.shape, vec_ref, mat_ref, norm_ref)
            d, m_, v_ = _adam_update(w_ref[...], g, m_ref[...], v_ref[...])
            for ref, val in zip(outs[4 * k:4 * k + 4], (g, d, m_, v_)):
                ref[...] = val
        w_ref, m_ref, v_ref, g_ref = ins[3 * n:]
        g = g_ref[0:w_ref.shape[0], :]
        for ref, val in zip(outs[4 * n:4 * n + 4], (g,) + _adam_update(w_ref[...], g, m_ref[...], v_ref[...])):
            ref[...] = val
        outs[4 * n + 4][...] = vec_ref[_bag_row("loss"), 0:1]

    arrays = [a for e in entries for a in e[1:]] + list(conv)
    out_shape = [jax.ShapeDtypeStruct(e[1].shape, F32) for e in entries for _ in range(4)]
    out_shape += [jax.ShapeDtypeStruct(conv[0].shape, F32)] * 4 + [jax.ShapeDtypeStruct((1, 1), F32)]
    return pl.pallas_call(
        body, name="adamw_replicated", out_shape=tuple(out_shape),
        compiler_params=pltpu.CompilerParams(vmem_limit_bytes=VMEM_LIMIT_BYTES),
    )(vec_sum, mat_sum, norm_grad, *arrays)


def _adamw(w, g, m, v, rows, name):
    r, c = w.shape

    def body(w_ref, g_ref, m_ref, v_ref, go_ref, d_ref, nm_ref, nv_ref):
        g = g_ref[...]
        go_ref[...] = g
        d_ref[...], nm_ref[...], nv_ref[...] = _adam_update(w_ref[...], g, m_ref[...], v_ref[...])

    spec = pl.BlockSpec((rows, c), lambda i: (i, 0))
    return pl.pallas_call(
        body, name=name, out_shape=tuple(jax.ShapeDtypeStruct((r, c), F32) for _ in range(4)),
        grid=(r // rows,), in_specs=[spec] * 4, out_specs=(spec,) * 4,
        compiler_params=pltpu.CompilerParams(dimension_semantics=("arbitrary",),
                                             vmem_limit_bytes=VMEM_LIMIT_BYTES),
    )(w, g, m, v)


def _adamw_group(items, name):
    n = 4 * len(items)

    def body(*refs):
        ins, outs, bufs = refs[:n], refs[n:2 * n], refs[2 * n:3 * n]
        load_sems, store_sems = refs[3 * n:]
        loads = [pltpu.make_async_copy(ins[j], bufs[j], load_sems.at[j]) for j in range(n)]
        stores = [pltpu.make_async_copy(bufs[j], outs[j], store_sems.at[j]) for j in range(n)]
        for cp in loads:
            cp.start()
        for k in range(len(items)):
            for cp in loads[4 * k:4 * k + 4]:
                cp.wait()
            w_buf, g_buf, m_buf, v_buf = bufs[4 * k:4 * k + 4]
            w_buf[...], m_buf[...], v_buf[...] = _adam_update(w_buf[...], g_buf[...], m_buf[...], v_buf[...])
            for cp in stores[4 * k:4 * k + 4]:
                cp.start()
        for cp in stores:
            cp.wait()

    arrays = [a for item in items for a in item]
    any_spec = pl.BlockSpec(memory_space=pl.ANY)
    flat = pl.pallas_call(
        body, name=name, out_shape=tuple(jax.ShapeDtypeStruct(a.shape, F32) for a in arrays),
        in_specs=[any_spec] * n, out_specs=(any_spec,) * n,
        scratch_shapes=[pltpu.VMEM(a.shape, F32) for a in arrays] + [pltpu.SemaphoreType.DMA((n,))] * 2,
        compiler_params=pltpu.CompilerParams(vmem_limit_bytes=VMEM_LIMIT_BYTES),
    )(*arrays)
    return [(flat[4 * k + 1], flat[4 * k], flat[4 * k + 2], flat[4 * k + 3]) for k in range(len(items))]


def _shift_down(ext, s):
    return pltpu.roll(ext, s, 0)


def _tile_shift(v, s):
    rows, cols = v.shape
    tiles = v.reshape(rows // F32_SUBLANES, F32_SUBLANES, cols)
    return pltpu.roll(tiles, s % F32_SUBLANES, 1).reshape(rows, cols)


def _shift_up(ext, s):
    return pltpu.roll(ext, ext.shape[0] - s, 0)


def _lru_gates(xc, wa_ref, ba, wx_ref, bx, lam):
    pa, px = [], []
    for h in range(LRU_HEADS):
        xh = xc[:, h * HEAD_DIM:(h + 1) * HEAD_DIM].astype(BF16)
        pa.append(_dot(xh, wa_ref[h]))
        px.append(_dot(xh, wx_ref[h]))
    r = _sigmoid(jnp.concatenate(pa, axis=1) + ba)
    ig = _sigmoid(jnp.concatenate(px, axis=1) + bx)
    sp = _softplus(-lam)
    log_a = (-LRU_C * r) * sp
    a = jnp.exp(log_a)
    mult = jnp.sqrt(jnp.tanh(-log_a) * (1.0 + a * a))
    return r, ig, a, mult, sp


def _conv(ext, w_ref, b):
    y = b + _shift_down(ext, 3) * w_ref[0:1, :]
    y = y + _shift_down(ext, 2) * w_ref[1:2, :]
    y = y + _shift_down(ext, 1) * w_ref[2:3, :]
    y = y + ext * w_ref[3:4, :]
    return y[CONV_HIST:, :]


def _pool_diff(ext, pos):
    out = []
    for g, k in enumerate(POOL_WINDOWS):
        col = ext[:, g * POOL_GROUP_DIM:(g + 1) * POOL_GROUP_DIM]
        s = col
        for step in range(g + 1):
            s = s + _shift_down(s, 2 ** step)
        count = jnp.minimum(pos + 1, k).astype(F32)
        out.append(s[POOL_HIST:, :] / count - col[POOL_HIST:, :])
    return out


def _pool_mix(diff, pw_ref):
    return jnp.concatenate([_dot(diff[g].astype(BF16), pw_ref[g]) for g in range(len(POOL_WINDOWS))], axis=1)


def _branch_specs(tb, row_map, fixed):
    fixed3 = lambda i: (0, 0, 0)
    return [pl.BlockSpec((CONV_WIDTH, D_MODEL), fixed), pl.BlockSpec((1, D_MODEL), fixed),
            pl.BlockSpec((LRU_HEADS, HEAD_DIM, HEAD_DIM), fixed3), pl.BlockSpec((1, D_MODEL), fixed),
            pl.BlockSpec((LRU_HEADS, HEAD_DIM, HEAD_DIM), fixed3), pl.BlockSpec((1, D_MODEL), fixed),
            pl.BlockSpec((1, D_MODEL), fixed),
            pl.BlockSpec((len(POOL_WINDOWS), POOL_GROUP_DIM, POOL_GROUP_DIM), fixed3),
            pl.BlockSpec((1, POOL_WIDTH), fixed)]


def _branches_fwd(z, weights, seq, tb, shards):
    t = z.shape[0]
    nb = t // tb
    nbe = seq // tb
    groups = tb // F32_SUBLANES
    n = len(shards)

    def body(xa_ref, ga_ref, xb_ref, gb_ref, cw_ref, cb_ref, wa_ref, ba_ref, wx_ref, bx_ref, lam_ref,
             pw_ref, ps_ref, *refs):
        g_ins = refs[:n]
        ya_ref, yb_ref, hl_ref = refs[n:n + 3]
        g_outs = refs[n + 3:2 * n + 3]
        xa_ext, xb_ext, carry, a_s, u_s, send_sems, recv_sems, local_sems = refs[2 * n + 3:]
        blk = pl.program_id(0) % nbe
        start_gather, relay_gather, finish_gather = _gather_steps(shards, g_ins, g_outs, send_sems, recv_sems,
                                                                  local_sems)
        pl.when(pl.program_id(0) == 0)(start_gather)
        pl.when(pl.program_id(0) == nb // 2)(relay_gather)

        @pl.when(blk == 0)
        def _():
            xa_ext[0:CONV_HIST, :] = jnp.zeros((CONV_HIST, D_MODEL), F32)
            xb_ext[0:POOL_HIST, :] = jnp.zeros((POOL_HIST, POOL_WIDTH), F32)
            carry[...] = jnp.zeros_like(carry)

        xa_ext[CONV_HIST:, :] = xa_ref[...]
        xb_ext[POOL_HIST:, :] = xb_ref[...]
        ea = xa_ext[...]
        eb = xb_ext[...]
        xa_ext[0:CONV_HIST, :] = ea[tb:, :]
        xb_ext[0:POOL_HIST, :] = eb[tb:, :]

        xc = _conv(ea, cw_ref, cb_ref[...])
        _, ig, a, mult, _ = _lru_gates(xc, wa_ref, ba_ref[...], wx_ref, bx_ref[...], lam_ref[...])
        u = mult * (ig * xc)
        row8 = lax.broadcasted_iota(jnp.int32, (tb, D_MODEL), 0) % F32_SUBLANES
        for s in (1, 2, 4):
            m = row8 >= s
            u = jnp.where(m, a * _tile_shift(u, s) + u, u)
            a = jnp.where(m, a * _tile_shift(a, s), a)
        a_s[...] = a
        u_s[...] = u

        def step(g, cr):
            sl = pl.ds(pl.multiple_of(g * F32_SUBLANES, F32_SUBLANES), F32_SUBLANES)
            hb = a_s[sl, :] * cr + u_s[sl, :]
            hl_ref[sl, :] = hb
            return jnp.broadcast_to(hb[F32_SUBLANES - 1:F32_SUBLANES, :], (F32_SUBLANES, D_MODEL))

        carry[...] = lax.fori_loop(0, groups, step, carry[...], unroll=4)
        ga = ga_ref[...]
        ya_ref[...] = (hl_ref[...] * (ga * _sigmoid(ga))).astype(BF16)

        pos = blk * tb + lax.broadcasted_iota(jnp.int32, (tb, POOL_GROUP_DIM), 0)
        ypre = _pool_mix(_pool_diff(eb, pos), pw_ref)
        gb = gb_ref[...]
        yb_ref[...] = ((ypre * ps_ref[...]) * (gb * _sigmoid(gb))).astype(BF16)
        pl.when(pl.program_id(0) == nb - 1)(finish_gather)

    row = lambda i: (i, 0)
    fixed = lambda i: (0, 0)
    any_spec = pl.BlockSpec(memory_space=pl.ANY)
    in_specs = [pl.BlockSpec((tb, D_MODEL), lambda i: (i, 0)), pl.BlockSpec((tb, D_MODEL), lambda i: (i, 1)),
                pl.BlockSpec((tb, POOL_WIDTH), lambda i: (i, 4)), pl.BlockSpec((tb, POOL_WIDTH), lambda i: (i, 5)),
                ] + _branch_specs(tb, row, fixed) + [any_spec] * n
    g_shape, g_sems = _gather_shapes(shards)
    return pl.pallas_call(
        body, name="branches_fwd",
        out_shape=tuple([jax.ShapeDtypeStruct((t, D_MODEL), BF16), jax.ShapeDtypeStruct((t, POOL_WIDTH), BF16),
                         jax.ShapeDtypeStruct((t, D_MODEL), F32)] + g_shape),
        grid=(nb,), in_specs=in_specs,
        out_specs=tuple([pl.BlockSpec((tb, D_MODEL), row), pl.BlockSpec((tb, POOL_WIDTH), row),
                         pl.BlockSpec((tb, D_MODEL), row)] + [any_spec] * n),
        scratch_shapes=[pltpu.VMEM((tb + CONV_HIST, D_MODEL), F32), pltpu.VMEM((tb + POOL_HIST, POOL_WIDTH), F32),
                        pltpu.VMEM((F32_SUBLANES, D_MODEL), F32),
                        pltpu.VMEM((tb, D_MODEL), F32), pltpu.VMEM((tb, D_MODEL), F32)] + g_sems,
        compiler_params=pltpu.CompilerParams(dimension_semantics=("arbitrary",),
                                             vmem_limit_bytes=VMEM_LIMIT_BYTES),
    )(z, z, z, z, *weights, *[sh[0] for sh in shards])


def _branches_bwd(z, hl, dya, dyb, dzm, weights, vec_bag, seq, tb, riders):
    t = z.shape[0]
    nb = t // tb
    nbe = seq // tb
    groups = tb // F32_SUBLANES
    nr = len(riders)

    def body(xa_ref, xap_ref, ga_ref, xb_ref, xbp_ref, gb_ref, hl_ref, hlp_ref, dya_ref, dyb_ref, dzm_ref,
             cw_ref, cb_ref, wa_ref, ba_ref, wx_ref, bx_ref, lam_ref, pw_ref, ps_ref, vec_in_ref, *rest):
        pairs, (dz_ref, vec_ref, mat_ref), grads = rest[:2 * nr], rest[2 * nr:2 * nr + 3], rest[2 * nr + 3:4 * nr + 3]
        xa_ext, xb_ext, hl_ext, a_ext, dxc_ext, dwin_ext, g_carry, b_s, d_s, g_s = rest[4 * nr + 3:]
        i = pl.program_id(0)
        blk = (nb - 1 - i) % nbe

        def mat_rows(name, k):
            at = MAT_BAG_AT[name] + k * HEAD_DIM
            return slice(at, at + HEAD_DIM)

        def rider(k):
            grads[2 * k][...] += _dot_tn(pairs[2 * k][...], pairs[2 * k + 1][...])

        @pl.when(i == 0)
        def _():
            vec_ref[...] = vec_in_ref[...]
            mat_ref[...] = jnp.zeros_like(mat_ref)
            for k in range(nr):
                grads[2 * k][...] = jnp.zeros_like(grads[2 * k])

        @pl.when(blk == nbe - 1)
        def _():
            a_ext[tb:, :] = jnp.zeros((F32_SUBLANES, D_MODEL), F32)
            dxc_ext[tb:, :] = jnp.zeros((CONV_HIST, D_MODEL), F32)
            dwin_ext[tb:, :] = jnp.zeros((POOL_HIST, POOL_WIDTH), F32)
            g_carry[...] = jnp.zeros_like(g_carry)

        live = (blk > 0).astype(F32)
        xa_ext[0:CONV_HIST, :] = xap_ref[...] * live
        xa_ext[CONV_HIST:, :] = xa_ref[...]
        xb_ext[0:POOL_HIST, :] = xbp_ref[...] * live
        xb_ext[POOL_HIST:, :] = xb_ref[...]
        hl_ext[0:F32_SUBLANES, :] = hlp_ref[...] * live
        hl_ext[F32_SUBLANES:, :] = hl_ref[...]
        ea = xa_ext[...]
        eb = xb_ext[...]
        rider(0)

        xc = _conv(ea, cw_ref, cb_ref[...])
        lam = lam_ref[...]
        r, ig, a, mult, sp = _lru_gates(xc, wa_ref, ba_ref[...], wx_ref, bx_ref[...], lam)
        hl = hl_ref[...]
        ga = ga_ref[...]
        sga = _sigmoid(ga)
        dya = dya_ref[...]
        dhl = dya * (ga * sga)
        dz_ref[:, D_MODEL:2 * D_MODEL] = (dya * hl * (sga * (1.0 + ga * (1.0 - sga)))).astype(BF16)

        a_ext[0:tb, :] = a
        b = _shift_up(a_ext[...], 1)[0:tb, :]
        a_ext[tb:, :] = jnp.broadcast_to(a[0:1, :], (F32_SUBLANES, D_MODEL))
        d = dhl
        row8 = lax.broadcasted_iota(jnp.int32, (tb, D_MODEL), 0) % F32_SUBLANES
        for s in (1, 2, 4):
            m = row8 < F32_SUBLANES - s
            d = jnp.where(m, d + b * _tile_shift(d, -s), d)
            b = jnp.where(m, b * _tile_shift(b, -s), b)
        b_s[...] = b
        d_s[...] = d

        def step(k, cr):
            sl = pl.ds(pl.multiple_of((groups - 1 - k) * F32_SUBLANES, F32_SUBLANES), F32_SUBLANES)
            gb_ = d_s[sl, :] + b_s[sl, :] * cr
            g_s[sl, :] = gb_
            return jnp.broadcast_to(gb_[0:1, :], (F32_SUBLANES, D_MODEL))

        g_carry[...] = lax.fori_loop(0, groups, step, g_carry[...], unroll=4)
        rider(1)
        gsc = g_s[...]
        da = gsc * _shift_down(hl_ext[...], 1)[F32_SUBLANES:, :]
        dmult = gsc * (ig * xc)
        dig = gsc * (mult * xc)
        dxc = gsc * (mult * ig)
        dlog_a = da * a - (a * a) * dmult / mult
        dr = dlog_a * (-LRU_C * sp)
        vec_ref[_bag_row("lru_lambda"), :] += jnp.sum(dlog_a * (-LRU_C * r), axis=0, keepdims=True)
        dpa = dr * (r * (1.0 - r))
        dpx = dig * (ig * (1.0 - ig))
        vec_ref[_bag_row("lru_b_a"), :] += jnp.sum(dpa, axis=0, keepdims=True)
        vec_ref[_bag_row("lru_b_x"), :] += jnp.sum(dpx, axis=0, keepdims=True)
        back = []
        for h in range(LRU_HEADS):
            cols = slice(h * HEAD_DIM, (h + 1) * HEAD_DIM)
            xh = xc[:, cols].astype(BF16)
            dpa_h = dpa[:, cols].astype(BF16)
            dpx_h = dpx[:, cols].astype(BF16)
            mat_ref[mat_rows("lru_w_a", h), :] += _dot_tn(xh, dpa_h)
            mat_ref[mat_rows("lru_w_x", h), :] += _dot_tn(xh, dpx_h)
            back.append(_dot_nt(dpa_h, wa_ref[h]) + _dot_nt(dpx_h, wx_ref[h]))
        dxc = dxc + jnp.concatenate(back, axis=1)
        vec_ref[_bag_row("conv_b"), :] += jnp.sum(dxc, axis=0, keepdims=True)
        for k in range(CONV_WIDTH):
            tap = _shift_down(ea, CONV_WIDTH - 1 - k)[CONV_HIST:, :] if k < CONV_WIDTH - 1 else ea[CONV_HIST:, :]
            vec_ref[_bag_row("conv_w", k), :] += jnp.sum(dxc * tap, axis=0, keepdims=True)
        dxc_ext[0:tb, :] = dxc
        ed = dxc_ext[...]
        dxa = ed * cw_ref[3:4, :]
        dxa = dxa + _shift_up(ed, 1) * cw_ref[2:3, :]
        dxa = dxa + _shift_up(ed, 2) * cw_ref[1:2, :]
        dxa = dxa + _shift_up(ed, 3) * cw_ref[0:1, :]
        dz_ref[:, 0:D_MODEL] = dxa[0:tb, :].astype(BF16)
        dxc_ext[tb:, :] = dxc[0:CONV_HIST, :]

        pos = blk * tb + lax.broadcasted_iota(jnp.int32, (tb, POOL_GROUP_DIM), 0)
        diff = _pool_diff(eb, pos)
        rider(2)
        ypre = _pool_mix(diff, pw_ref)
        ps = ps_ref[...]
        gb = gb_ref[...]
        sgb = _sigmoid(gb)
        dyb = dyb_ref[...]
        dyp = dyb * (gb * sgb)
        dz_ref[:, 2 * D_MODEL + POOL_WIDTH:3 * D_MODEL] = (
            dyb * (ypre * ps) * (sgb * (1.0 + gb * (1.0 - sgb)))).astype(BF16)
        vec_ref[_bag_row("pool_scale"), 0:POOL_WIDTH] += jnp.sum(dyp * ypre, axis=0, keepdims=True)
        dypre = dyp * ps
        for g, k in enumerate(POOL_WINDOWS):
            cols = slice(g * POOL_GROUP_DIM, (g + 1) * POOL_GROUP_DIM)
            dyg = dypre[:, cols].astype(BF16)
            mat_ref[mat_rows("pool_w", g), :] += _dot_tn(diff[g].astype(BF16), dyg)
            ddiff = _dot_nt(dyg, pw_ref[g])
            count = jnp.minimum(pos + 1, k).astype(F32)
            dwin = ddiff / count
            dwin_ext[0:tb, cols] = dwin
            s = dwin_ext[:, cols]
            for step_ in range(g + 1):
                s = s + _shift_up(s, 2 ** step_)
            dz_ref[:, 2 * D_MODEL + g * POOL_GROUP_DIM:2 * D_MODEL + (g + 1) * POOL_GROUP_DIM] = (
                s[0:tb, :] - ddiff).astype(BF16)
            dwin_ext[tb:, cols] = dwin[0:POOL_HIST, :]

        dz_ref[:, 3 * D_MODEL:] = dzm_ref[...]

        @pl.when(i == nb - 1)
        def _():
            row = _bag_row("lru_lambda")
            vec_ref[row, :] = vec_ref[row, :] * (-_sigmoid(-lam))
            for k in range(nr):
                grads[2 * k + 1][...] = grads[2 * k][...].astype(BF16)

    rev = lambda i: (nb - 1 - i, 0)
    fixed = lambda i: (0, 0)

    def prev(rows, col):
        per = tb // rows
        return lambda i: (jnp.maximum((nb - 1 - i) * per - 1, 0), col)

    in_specs = [pl.BlockSpec((tb, D_MODEL), lambda i: (nb - 1 - i, 0)),
                pl.BlockSpec((CONV_HIST, D_MODEL), prev(CONV_HIST, 0)),
                pl.BlockSpec((tb, D_MODEL), lambda i: (nb - 1 - i, 1)),
                pl.BlockSpec((tb, POOL_WIDTH), lambda i: (nb - 1 - i, 4)),
                pl.BlockSpec((POOL_HIST, POOL_WIDTH), prev(POOL_HIST, 4)),
                pl.BlockSpec((tb, POOL_WIDTH), lambda i: (nb - 1 - i, 5)),
                pl.BlockSpec((tb, D_MODEL), rev),
                pl.BlockSpec((F32_SUBLANES, D_MODEL), prev(F32_SUBLANES, 0)),
                pl.BlockSpec((tb, D_MODEL), rev), pl.BlockSpec((tb, POOL_WIDTH), rev),
                pl.BlockSpec((tb, 2 * D_MODEL), rev)] + _branch_specs(tb, rev, fixed) + [
                    pl.BlockSpec((VEC_BAG_ROWS, D_MODEL), fixed)]
    vec_at = len(in_specs) - 1
    out_shape = [jax.ShapeDtypeStruct((t, IN_COLS), BF16), jax.ShapeDtypeStruct((VEC_BAG_ROWS, D_MODEL), F32),
                 jax.ShapeDtypeStruct((MAT_BAG_ROWS, HEAD_DIM), F32)]
    out_specs = [pl.BlockSpec((tb, IN_COLS), rev), pl.BlockSpec((VEC_BAG_ROWS, D_MODEL), fixed),
                 pl.BlockSpec((MAT_BAG_ROWS, HEAD_DIM), fixed)]
    for lhs, rhs in riders:
        in_specs += [pl.BlockSpec((tb, lhs.shape[1]), rev), pl.BlockSpec((tb, rhs.shape[1]), rev)]
        grad = (lhs.shape[1], rhs.shape[1])
        out_shape += [jax.ShapeDtypeStruct(grad, F32), jax.ShapeDtypeStruct(grad, BF16)]
        out_specs += [pl.BlockSpec(grad, fixed)] * 2
    scratch = [pltpu.VMEM((tb + CONV_HIST, D_MODEL), F32), pltpu.VMEM((tb + POOL_HIST, POOL_WIDTH), F32),
               pltpu.VMEM((tb + F32_SUBLANES, D_MODEL), F32), pltpu.VMEM((tb + F32_SUBLANES, D_MODEL), F32),
               pltpu.VMEM((tb + CONV_HIST, D_MODEL), F32), pltpu.VMEM((tb + POOL_HIST, POOL_WIDTH), F32),
               pltpu.VMEM((F32_SUBLANES, D_MODEL), F32),
               pltpu.VMEM((tb, D_MODEL), F32), pltpu.VMEM((tb, D_MODEL), F32), pltpu.VMEM((tb, D_MODEL), F32)]
    return pl.pallas_call(
        body, name="branches_bwd", out_shape=tuple(out_shape), grid=(nb,), in_specs=in_specs,
        out_specs=tuple(out_specs), scratch_shapes=scratch, input_output_aliases={vec_at: 1},
        compiler_params=pltpu.CompilerParams(dimension_semantics=("arbitrary",),
                                             vmem_limit_bytes=VMEM_LIMIT_BYTES),
    )(z, z, z, z, z, z, hl, hl, dya, dyb, dzm, *weights, vec_bag, *[a for pair in riders for a in pair])


def _merge_head(x2d, ya, yb, z, p2d, tgt, w_pl, w_pp, w_out, w_pg, w_pe, g2, gf, tb):
    t = x2d.shape[0]
    p_dim = p2d.shape[1]

    def body(x_ref, ya_ref, yb_ref, ma_ref, mb_ref, p_ref, t_ref, wpl_ref, wpp_ref, wout_ref, wpg_ref, wpe_ref,
             g2_ref, gf_ref,
             bag_ref, dxr_ref, dya_ref, dyb_ref, dzm_ref,
             mg_ref, do_ref, hn_ref, dgp_ref, dpe_ref, da_ref, dbm_ref, pbf_ref):
        @pl.when(pl.program_id(0) == 0)
        def _():
            bag_ref[...] = jnp.zeros_like(bag_ref)

        a_ = _dot(ya_ref[...], wpl_ref[...])
        bm = _dot(yb_ref[...], wpp_ref[...])
        sa = _sigmoid(ma_ref[...])
        sb = _sigmoid(mb_ref[...])
        mg = (sa * a_ + sb * bm).astype(BF16)
        mg_ref[...] = mg
        x1 = x_ref[...] + _dot(mg, wout_ref[...])
        xn2, r2 = _rms(x1)
        g2 = g2_ref[...]
        hn = (xn2 * g2).astype(BF16)
        hn_ref[...] = hn
        gate = _sigmoid(_dot(hn, wpg_ref[...]))
        pbf = p_ref[...].astype(BF16)
        pbf_ref[...] = pbf
        pe = _dot(pbf, wpe_ref[...])
        x2 = x1 + gate * pe
        xn3, r3 = _rms(x2)
        gf = gf_ref[...]
        err = xn3 * gf - t_ref[...]
        bag_ref[_bag_rows("loss"), 0:128] += 0.5 * jnp.sum(jnp.mean(err * err, axis=-1))

        dy = err * (1.0 / D_MODEL)
        bag_ref[_bag_row("final_g"), :] += jnp.sum(dy * xn3, axis=0, keepdims=True)
        dx2 = _rms_bwd(dy * gf, xn3, r3)
        dpe_ref[...] = (dx2 * gate).astype(BF16)
        dgp = ((dx2 * pe) * (gate * (1.0 - gate))).astype(BF16)
        dgp_ref[...] = dgp
        dhn = _dot_nt(dgp, wpg_ref[...])
        bag_ref[_bag_row("ple_norm_g"), :] += jnp.sum(dhn * xn2, axis=0, keepdims=True)
        dx1 = dx2 + _rms_bwd(dhn * g2, xn2, r2)
        dxr_ref[...] = dx1
        do = dx1.astype(BF16)
        do_ref[...] = do
        dmg = _dot_nt(do, wout_ref[...])
        da = (dmg * sa).astype(BF16)
        dbm = (dmg * sb).astype(BF16)
        da_ref[...] = da
        dbm_ref[...] = dbm
        dzm_ref[:, 0:D_MODEL] = (dmg * a_ * (sa * (1.0 - sa))).astype(BF16)
        dzm_ref[:, D_MODEL:] = (dmg * bm * (sb * (1.0 - sb))).astype(BF16)
        dya_ref[...] = _dot_nt(da, wpl_ref[...])
        dyb_ref[...] = _dot_nt(dbm, wpp_ref[...])

    row = lambda i: (i, 0)
    fixed = lambda i: (0, 0)

    def resident(shape):
        return pl.BlockSpec(shape, fixed, pipeline_mode=pl.Buffered(1))

    tok = lambda width: pl.BlockSpec((tb, width), row)
    in_specs = [tok(D_MODEL), tok(D_MODEL), tok(POOL_WIDTH),
                pl.BlockSpec((tb, D_MODEL), lambda i: (i, 3)), pl.BlockSpec((tb, D_MODEL), lambda i: (i, 4)),
                tok(p_dim), tok(D_MODEL),
                resident((D_MODEL, D_MODEL)), resident((POOL_WIDTH, D_MODEL)), resident((D_MODEL, D_MODEL)),
                resident((D_MODEL, D_MODEL)), resident((p_dim, D_MODEL)),
                pl.BlockSpec((1, D_MODEL), fixed), pl.BlockSpec((1, D_MODEL), fixed)]
    bf = lambda width: jax.ShapeDtypeStruct((t, width), BF16)
    f32 = lambda width: jax.ShapeDtypeStruct((t, width), F32)
    out_shape = (jax.ShapeDtypeStruct((VEC_BAG_ROWS, D_MODEL), F32),
                 f32(D_MODEL), f32(D_MODEL), f32(POOL_WIDTH), bf(2 * D_MODEL),
                 bf(D_MODEL), bf(D_MODEL), bf(D_MODEL), bf(D_MODEL), bf(D_MODEL), bf(D_MODEL), bf(D_MODEL), bf(p_dim))
    out_specs = (pl.BlockSpec((VEC_BAG_ROWS, D_MODEL), fixed),
                 tok(D_MODEL), tok(D_MODEL), tok(POOL_WIDTH), tok(2 * D_MODEL),
                 tok(D_MODEL), tok(D_MODEL), tok(D_MODEL), tok(D_MODEL), tok(D_MODEL), tok(D_MODEL), tok(D_MODEL),
                 tok(p_dim))
    return pl.pallas_call(
        body, name="merge_head", out_shape=out_shape, grid=(t // tb,), in_specs=in_specs, out_specs=out_specs,
        compiler_params=pltpu.CompilerParams(dimension_semantics=("arbitrary",),
                                             vmem_limit_bytes=VMEM_LIMIT_BYTES),
    )(x2d, ya, yb, z, z, p2d, tgt, w_pl, w_pp, w_out, w_pg, w_pe, g2, gf)


def kernel(x, p, norm_g, w_in, conv_w, conv_b, lru_w_a, lru_b_a, lru_w_x, lru_b_x, lru_lambda, pool_w, pool_scale, w_proj_lru, w_proj_pool, w_out, ple_norm_g, w_ple_gate, w_ple_proj, final_g, loss_target, m_norm_g, m_w_in, m_conv_w, m_conv_b, m_lru_w_a, m_lru_b_a, m_lru_w_x, m_lru_b_x, m_lru_lambda, m_pool_w, m_pool_scale, m_w_proj_lru, m_w_proj_pool, m_w_out, m_ple_norm_g, m_w_ple_gate, m_w_ple_proj, m_final_g, v_norm_g, v_w_in, v_conv_w, v_conv_b, v_lru_w_a, v_lru_b_a, v_lru_w_x, v_lru_b_x, v_lru_lambda, v_pool_w, v_pool_scale, v_w_proj_lru, v_w_proj_pool, v_w_out, v_ple_norm_g, v_w_ple_gate, v_w_ple_proj, v_final_g):
    bsz, seq, _ = x.shape
    t = bsz * seq
    tb_mm = min(1024, seq)
    tb_seq = min(256, seq // 2) if seq >= 512 else seq
    x2d = x.reshape(t, D_MODEL)
    p2d = p.reshape(t, p.shape[-1])
    tgt = loss_target.reshape(t, D_MODEL)

    rest = [(w_proj_lru[0], 0), (w_proj_pool[0], 1), (w_out[0], 0), (w_ple_gate[0], 0), (w_ple_proj[0], 1)]
    z, h_bf, w_in_f, conv_w_f, *narrow = _in_proj_gather(
        x2d, norm_g, w_in[0], [(conv_w[0], 1, False)], tb_mm,
        [w for w, _ in rest] + [lru_w_a[0], lru_w_x[0], pool_w[0]])
    wa_bf, wx_bf, pw_bf = narrow[len(rest):]
    branch_w = (conv_w_f, conv_b, wa_bf, lru_b_a.reshape(1, D_MODEL), wx_bf, lru_b_x.reshape(1, D_MODEL),
                lru_lambda, pw_bf, pool_scale)

    ya, yb, hl, w_pl_f, w_pp_f, w_out_f, w_pg_f, w_pe_f = _branches_fwd(
        z, branch_w, seq, tb_seq, [(w16, axis, True) for w16, (_, axis) in zip(narrow, rest)])
    (vec_bag, dx_res, dya, dyb, dzm, mg_bf, do_bf, hn_bf, dgp_bf, dpe_bf, da_bf, dbm_bf, p_bf) = _merge_head(
        x2d, ya, yb, z, p2d, tgt, w_pl_f, w_pp_f, w_out_f, w_pg_f, w_pe_f, ple_norm_g, final_g.reshape(1, D_MODEL),
        tb_seq)
    dz, vec_bag, mat_bag, g_out, g_out16, g_pp, g_pp16, g_pe, g_pe16 = _branches_bwd(
        z, hl, dya, dyb, dzm, branch_w, vec_bag, seq, tb_seq, [(mg_bf, do_bf), (yb, dbm_bf), (p_bf, dpe_bf)])

    tb_dw = min(1024, seq)
    def row_pieces(g32, g16):
        pieces = (8, g32.shape[0] // 8, g32.shape[1])
        return g32.reshape(pieces), False, g16.reshape(pieces)

    def proj_grad(lhs, rhs, name):
        g32, g16 = _weight_grad(lhs, rhs, 1, tb_dw, name)
        return row_pieces(g32[0], g16[0])

    p_dim = p2d.shape[1]
    proj_parts = [proj_grad(ya, da_bf, "dw_proj_lru"), (g_pp, True, g_pp16), row_pieces(g_out, g_out16),
                  proj_grad(hn_bf, dgp_bf, "dw_ple_gate"), (g_pe, True, g_pe16)]
    nb_dw = t // tb_dw
    g_in, g_in16, r_pl, r_pp, r_out, r_pg, r_pe, vec_mine, mat_mine = _weight_grad(
        h_bf, dz, N_CHIPS, tb_dw, "dw_in",
        reduce=(proj_parts + [(vec_bag.reshape(8, VEC_BAG_ROWS // 8, D_MODEL), False, None),
                              (mat_bag.reshape(8, MAT_BAG_ROWS // 8, HEAD_DIM), False, None)],
                [BF16] * 5 + [F32] * 2,
                (0, nb_dw // 2, 2 * nb_dw - 1, 3 * nb_dw + nb_dw // 2, N_CHIPS * nb_dw - 1)))
    pieces = (8, D_MODEL // 2, IN_COLS // N_CHIPS)
    nb_seq = t // tb_seq
    dx, g_g1, r_in, vec_sum, mat_sum, g_cw = _in_proj_bwd(
        dz, w_in_f, x2d, dx_res, norm_g, tb_seq,
        reduce=([(g_in.reshape(pieces), False, g_in16.reshape(pieces))], BF16,
                (0, nb_seq // 8, nb_seq // 2, nb_seq - 1)),
        shards=[(vec_mine.reshape(VEC_BAG_ROWS // N_CHIPS, D_MODEL), 0, True),
                (mat_mine.reshape(MAT_BAG_ROWS // N_CHIPS, HEAD_DIM), 0, True)],
        take=(_bag_rows("conv_w"), D_MODEL // N_CHIPS))

    u_in = tuple(a[None] for a in _adamw(w_in[0], r_in.reshape(D_MODEL, IN_COLS // N_CHIPS), m_w_in[0], v_w_in[0],
                                         D_MODEL // 4, "adamw_w_in"))
    proj = [(w_proj_lru, r_pl, m_w_proj_lru, v_w_proj_lru), (w_proj_pool, r_pp, m_w_proj_pool, v_w_proj_pool),
            (w_out, r_out, m_w_out, v_w_out), (w_ple_gate, r_pg, m_w_ple_gate, v_w_ple_gate),
            (w_ple_proj, r_pe, m_w_ple_proj, v_w_ple_proj)]
    u_pl, u_pp, u_out, u_pg, u_pe = [tuple(a[None] for a in u) for u in _adamw_group(
        [(w[0], g.reshape(w.shape[1:]), m[0], v[0]) for w, g, m, v in proj], "adamw_proj")]

    small = [("norm_g", norm_g, m_norm_g, v_norm_g), ("conv_b", conv_b, m_conv_b, v_conv_b),
             ("lru_w_a", lru_w_a, m_lru_w_a, v_lru_w_a), ("lru_b_a", lru_b_a, m_lru_b_a, v_lru_b_a),
             ("lru_w_x", lru_w_x, m_lru_w_x, v_lru_w_x), ("lru_b_x", lru_b_x, m_lru_b_x, v_lru_b_x),
             ("lru_lambda", lru_lambda, m_lru_lambda, v_lru_lambda), ("pool_w", pool_w, m_pool_w, v_pool_w),
             ("pool_scale", pool_scale, m_pool_scale, v_pool_scale),
             ("ple_norm_g", ple_norm_g, m_ple_norm_g, v_ple_norm_g), ("final_g", final_g, m_final_g, v_final_g)]

    def view(a):
        return a.reshape(-1, a.shape[-1]) if a.ndim != 3 else a[0]

    flat = _adamw_replicated(vec_sum, mat_sum, g_g1, [(name,) + tuple(view(a) for a in arrs) for name, *arrs in small],
                             (conv_w[0], m_conv_w[0], v_conv_w[0], g_cw))
    u_small = {name: tuple(flat[4 * k + pick].reshape(arrs[0].shape) for pick in range(4))
               for k, (name, *arrs) in enumerate(small)}
    u_cw = tuple(a[None] for a in flat[4 * len(small):4 * len(small) + 4])

    loss = flat[-1].reshape(())
    grad_x = dx.reshape(bsz, seq, D_MODEL)

    def ordered(pick):
        s = {name: u[pick] for name, u in u_small.items()}
        return [s["norm_g"], u_in[pick], u_cw[pick], s["conv_b"], s["lru_w_a"], s["lru_b_a"], s["lru_w_x"], s["lru_b_x"],
                s["lru_lambda"], s["pool_w"], s["pool_scale"], u_pl[pick], u_pp[pick], u_out[pick], s["ple_norm_g"],
                u_pg[pick], u_pe[pick], s["final_g"]]

    return (loss, grad_x, *ordered(0), *ordered(1), *ordered(2), *ordered(3))
```

```python
import jax
import jax.numpy as jnp
from jax import lax
from jax.experimental import pallas as pl
from jax.experimental.pallas import tpu as pltpu

F32 = jnp.float32
BF16 = jnp.bfloat16
MESH = pl.DeviceIdType.MESH

D_MODEL = 1024
LRU_HEADS = 8
HEAD_DIM = 128
CONV_WIDTH = 4
LRU_C = 8.0
POOL_WIDTH = 512
POOL_WINDOWS = (2, 4, 8, 16)
POOL_GROUP_DIM = 128
IN_COLS = 5120
N_CHIPS = 4
EPS = 1e-6

ADAM_LR = 0.001
ADAM_B1 = 0.9
ADAM_B2 = 0.999
ADAM_EPS = 1e-08
ADAM_WD = 0.01
ADAM_STEP = 10

F32_SUBLANES = 8
CONV_HIST = 8
POOL_HIST = 16
VMEM_LIMIT_BYTES = 58 * 1024 * 1024
VEC_BAG_SLOTS = ("norm_g", "conv_w", "conv_b", "lru_b_a", "lru_b_x", "lru_lambda", "pool_scale", "ple_norm_g",
                 "final_g", "loss")
VEC_BAG_ROWS = 128
MAT_BAG_AT = {"lru_w_a": 0, "lru_w_x": LRU_HEADS * HEAD_DIM, "pool_w": 2 * LRU_HEADS * HEAD_DIM}
MAT_BAG_ROWS = 2 * LRU_HEADS * HEAD_DIM + len(POOL_WINDOWS) * POOL_GROUP_DIM


def _bag_row(name, k=0):
    at = F32_SUBLANES * VEC_BAG_SLOTS.index(name) + k
    return slice(at, at + 1)


def _bag_rows(name):
    at = F32_SUBLANES * VEC_BAG_SLOTS.index(name)
    return slice(at, at + F32_SUBLANES)


def _dot(a, b):
    return jnp.dot(a, b, preferred_element_type=F32)


def _dot_nt(a, b):
    return lax.dot_general(a, b, (((1,), (1,)), ((), ())), preferred_element_type=F32)


def _dot_tn(a, b):
    return lax.dot_general(a, b, (((0,), (0,)), ((), ())), preferred_element_type=F32)


def _sigmoid(v):
    return jax.nn.sigmoid(v)


def _softplus(v):
    return jnp.maximum(v, 0.0) + jnp.log1p(jnp.exp(-jnp.abs(v)))


def _place():
    return lax.axis_index("x"), lax.axis_index("y"), lax.axis_index("c")


GATHER_SEMS = 6


def _gather_shapes(shards):
    out_shape = []
    for arr, axis, _ in shards:
        r, cols = arr.shape
        out_shape.append(jax.ShapeDtypeStruct((N_CHIPS * r, cols) if axis == 0 else (r, N_CHIPS * cols), arr.dtype))
    n = len(shards)
    sems = [pltpu.SemaphoreType.DMA((n * GATHER_SEMS,)), pltpu.SemaphoreType.DMA((n * GATHER_SEMS,)),
            pltpu.SemaphoreType.DMA((n,))]
    return out_shape, sems


def _gather_steps(shards, ins, outs, send_sems, recv_sems, local_sems):
    n = len(shards)
    x, y, c = _place()
    me, sibling = (x, y, c), (x, y, 1 - c)
    chips = [(x, 1 - y), (1 - x, y), (1 - x, 1 - y)]

    def region(k, cx, cy, hc):
        (r, cols), axis = shards[k][0].shape, shards[k][1]
        j = 2 * cx + cy
        if axis == 0:
            if hc is None:
                return outs[k].at[pl.ds(j * r, r), :]
            return outs[k].at[pl.ds(j * r + hc * (r // 2), r // 2), :]
        if hc is None:
            return outs[k].at[:, pl.ds(j * cols, cols)]
        return outs[k].at[pl.ds(hc * (r // 2), r // 2), pl.ds(j * cols, cols)]

    def remote(k, sem, block, to, src=None):
        dst = region(k, *block)
        return pltpu.make_async_remote_copy(
            src_ref=dst if src is None else src, dst_ref=dst,
            send_sem=send_sems.at[k * GATHER_SEMS + sem], recv_sem=recv_sems.at[k * GATHER_SEMS + sem],
            device_id=to, device_id_type=MESH)

    def first(k, idx):
        r, split = shards[k][0].shape[0], shards[k][2]
        src = ins[k].at[pl.ds(c * (r // 2), r // 2), :] if split else ins[k]
        return remote(k, idx, (x, y, c if split else None), (*chips[idx], c), src=src)

    def relay(k):
        src_chip = (jnp.bitwise_xor(x, 1 - c), jnp.bitwise_xor(y, c))
        dst_chip = (jnp.bitwise_xor(x, c), jnp.bitwise_xor(y, 1 - c))
        return remote(k, 2, (*src_chip, c), (*dst_chip, c))

    def passed(k, idx):
        return remote(k, 3 + idx, (*chips[idx], c), sibling)

    def mine(k):
        return pltpu.make_async_copy(ins[k], region(k, x, y, None), local_sems.at[k])

    def start():
        for k in range(n):
            mine(k).start()
            for idx in range(2 if shards[k][2] else 3):
                first(k, idx).start()

    def relay_on():
        for k in range(n):
            split = shards[k][2]
            for idx in range(2):
                remote(k, idx, (*chips[idx], c if split else None), me).wait_recv()
            if split:
                relay(k).start()
                passed(k, 0).start()
                passed(k, 1).start()

    def finish():
        for k in range(n):
            split = shards[k][2]
            remote(k, 2, (*chips[2], c if split else None), me).wait_recv()
            if split:
                passed(k, 2).start()
        for k in range(n):
            if shards[k][2]:
                for idx in range(3):
                    remote(k, 3 + idx, (*chips[idx], 1 - c), me).wait_recv()
        for k in range(n):
            if shards[k][2]:
                for cp in (first(k, 0), first(k, 1), relay(k), passed(k, 0), passed(k, 1), passed(k, 2)):
                    cp.wait_send()
            else:
                for idx in range(3):
                    first(k, idx).wait_send()
            mine(k).wait()

    return start, relay_on, finish


RS_ADD_ROWS = (64, 32, 16, 8)


N_DEV = 2 * N_CHIPS


def _all_reduce_scratch(shape):
    return [pltpu.VMEM((N_DEV,) + tuple(shape), F32), pltpu.SemaphoreType.DMA((N_DEV - 1,)),
            pltpu.SemaphoreType.DMA((N_DEV - 1,))]


def _all_reduce_tile(v_ref, o_ref, slots, send_sems, recv_sems):
    flips = [(dx, dy, dc) for dx in (0, 1) for dy in (0, 1) for dc in (0, 1)][1:]
    x, y, c = _place()
    mine = 4 * x + 2 * y + c

    def copy(k, to_flip, slot):
        dx, dy, dc = to_flip
        peer = (jnp.bitwise_xor(x, dx), jnp.bitwise_xor(y, dy), jnp.bitwise_xor(c, dc))
        return pltpu.make_async_remote_copy(
            src_ref=v_ref, dst_ref=slots.at[slot], send_sem=send_sems.at[k], recv_sem=recv_sems.at[k],
            device_id=peer, device_id_type=MESH)

    sends = [copy(k, flip, mine) for k, flip in enumerate(flips)]
    for cp in sends:
        cp.start()
    slots[mine] = v_ref[...]
    for k, (dx, dy, dc) in enumerate(flips):
        copy(k, (dx, dy, dc), jnp.bitwise_xor(mine, 4 * dx + 2 * dy + dc)).wait_recv()
    total = slots[0]
    for d in range(1, N_DEV):
        total = total + slots[d]
    o_ref[...] = total
    for cp in sends:
        cp.wait_send()


RS_SEMS = 8
RS_LOCAL_SEMS = 5


def _rs_piece_shape(part):
    arr, cols = part[0], part[1]
    return (arr.shape[0] // 2, arr.shape[1] // N_CHIPS) if cols else tuple(arr.shape[1:])


def _rs_operands(parts):
    return [p[0] for p in parts] + [p[0] if p[2] is None else p[2] for p in parts]


def _rs_wires(parts, wire):
    return list(wire) if isinstance(wire, (list, tuple)) else [wire] * len(parts)


def _rs_shapes(parts, wire):
    n = len(parts)
    shapes = [_rs_piece_shape(p) for p in parts]
    out_shape = [jax.ShapeDtypeStruct((2,) + s, F32) for s in shapes]
    scratch = []
    for lead, kind in ((N_CHIPS, "f32"), (N_CHIPS, "narrow"), (N_CHIPS, "wire"), (None, "f32"), (N_CHIPS, "wire")):
        for s, p, w in zip(shapes, parts, _rs_wires(parts, wire)):
            dtype = {"f32": F32, "narrow": F32 if p[2] is None else p[2].dtype, "wire": w}[kind]
            scratch.append(pltpu.VMEM(s if lead is None else (lead,) + s, dtype))
    scratch += [pltpu.SemaphoreType.DMA((n * RS_SEMS,)), pltpu.SemaphoreType.DMA((n * RS_SEMS,)),
                pltpu.SemaphoreType.DMA((n * RS_LOCAL_SEMS,))]
    return out_shape, scratch


def _rs_steps(parts, ins, outs, scratch):
    n = len(parts)
    own, sib, got, fin, snd = (scratch[k * n:(k + 1) * n] for k in range(5))
    send_sems, recv_sems, local_sems = scratch[5 * n:]
    shapes = [_rs_piece_shape(p) for p in parts]
    x, y, c = _place()
    j_me = 2 * x + y
    me, sibling = (x, y, c), (x, y, 1 - c)

    def piece(a, jj, core, narrow=False):
        ref = ins[n + a] if narrow else ins[a]
        if parts[a][1]:
            r, cl = shapes[a]
            return ref.at[pl.ds(core * r, r), pl.ds(jj * cl, cl)]
        return ref.at[2 * jj + core]

    def remote(a, sem, src, dst, to):
        return pltpu.make_async_remote_copy(
            src_ref=src, dst_ref=dst, send_sem=send_sems.at[a * RS_SEMS + sem],
            recv_sem=recv_sems.at[a * RS_SEMS + sem], device_id=to, device_id_type=MESH)

    def rows_loop(a, fn):
        r = shapes[a][0]
        step = max(s for s in RS_ADD_ROWS if r % s == 0)

        def it(i, carry):
            fn(pl.ds(pl.multiple_of(i * step, step), step))
            return carry

        lax.fori_loop(0, r // step, it, 0)

    def load(a, jj):
        return pltpu.make_async_copy(piece(a, jj, c), own[a].at[jj], local_sems.at[a * RS_LOCAL_SEMS + jj])

    def to_sibling(a, jj):
        return remote(a, jj, piece(a, jj, 1 - c, narrow=True), sib[a].at[jj], sibling)

    near = (jnp.bitwise_xor(x, 1 - c), jnp.bitwise_xor(y, c))
    far = (jnp.bitwise_xor(x, c), jnp.bitwise_xor(y, 1 - c))
    diag = (1 - x, 1 - y)
    FROM_NEAR, FROM_FAR, FEED = 0, 1, 2

    def chip_of(chip):
        return 2 * chip[0] + chip[1]

    def feed(a):
        return remote(a, 4, snd[a].at[chip_of(diag)], got[a].at[FEED], (*near, c))

    def to_near(a):
        return remote(a, 5, snd[a].at[chip_of(near)], got[a].at[FROM_NEAR], (*near, c))

    def to_far(a):
        return remote(a, 6, snd[a].at[chip_of(far)], got[a].at[FROM_FAR], (*far, c))

    def store(a):
        return pltpu.make_async_copy(fin[a], outs[a].at[c], local_sems.at[a * RS_LOCAL_SEMS + 4])

    def result_to_sibling(a):
        return remote(a, 7, fin[a], outs[a].at[c], sibling)

    def exchange():
        for a in range(n):
            for jj in range(N_CHIPS):
                load(a, jj).start()
                to_sibling(a, jj).start()

    def chip_sums():
        for a in range(n):
            for jj in range(N_CHIPS):
                load(a, jj).wait()
                remote(a, jj, sib[a].at[jj], sib[a].at[jj], me).wait_recv()

                def add(sl, a=a, jj=jj):
                    q = own[a][jj, sl, :] + sib[a][jj, sl, :].astype(F32)
                    own[a][jj, sl, :] = q
                    snd[a][jj, sl, :] = q.astype(snd[a].dtype)

                rows_loop(a, add)
        for a in range(n):
            feed(a).start()
        for a in range(n):
            to_near(a).start()

    def relay():
        for a in range(n):
            remote(a, 4, got[a].at[FEED], got[a].at[FEED], me).wait_recv()

            def add(sl, a=a):
                pair = own[a][chip_of(far), sl, :] + got[a][FEED, sl, :].astype(F32)
                snd[a][chip_of(far), sl, :] = pair.astype(snd[a].dtype)

            rows_loop(a, add)
            to_far(a).start()

    def totals():
        for a in range(n):
            remote(a, 5, got[a].at[FROM_NEAR], got[a].at[FROM_NEAR], me).wait_recv()
            remote(a, 6, got[a].at[FROM_FAR], got[a].at[FROM_FAR], me).wait_recv()

            def total(sl, a=a):
                fin[a][sl, :] = (own[a][j_me, sl, :] + got[a][FROM_NEAR, sl, :].astype(F32)) + (
                    got[a][FROM_FAR, sl, :].astype(F32))

            rows_loop(a, total)
            store(a).start()
            result_to_sibling(a).start()

    def finish():
        for a in range(n):
            remote(a, 7, outs[a].at[1 - c], outs[a].at[1 - c], me).wait_recv()
        for a in range(n):
            for jj in range(N_CHIPS):
                to_sibling(a, jj).wait_send()
            for cp in (feed(a), to_near(a), to_far(a), result_to_sibling(a)):
                cp.wait_send()
            store(a).wait()

    return exchange, chip_sums, relay, totals, finish


def _rms(x):
    r = lax.rsqrt(jnp.mean(x * x, axis=-1, keepdims=True) + EPS)
    return x * r, r


def _rms_bwd(dxn, xn, r):
    return r * (dxn - xn * jnp.mean(dxn * xn, axis=-1, keepdims=True))


def _in_proj_order(s):
    shard = jnp.where(s < 2, 0, jnp.where(s < 6, 1 + jnp.bitwise_and(s, 1), 3))
    piece = jnp.where(s < 2, s, jnp.where(s < 6, (s - 2) // 2, s - 6))
    return shard, piece


def _in_proj_gather(x2d, norm_g, w_in_sh, shards, tb, casts):
    t = x2d.shape[0]
    nb = t // tb
    cols = IN_COLS // N_CHIPS
    pc = cols // 2
    half = D_MODEL // 2
    n = len(shards)
    nc = len(casts)

    def body(x_ref, g_ref, win_ref, *refs):
        ins, cast_ins = refs[:n], refs[n:n + nc]
        z_ref, h_ref, wfull_ref = refs[n + nc:n + nc + 3]
        outs, cast_outs = refs[n + nc + 3:2 * n + nc + 3], refs[2 * n + nc + 3:2 * (n + nc) + 3]
        scratch = refs[2 * (n + nc) + 3:]
        wv, h_all, send_sems, recv_sems, local_sems, w_send, w_recv, w_local, stage = scratch[:9]
        wide, narrow, cast_sems = scratch[9:9 + nc], scratch[9 + nc:9 + 2 * nc], scratch[9 + 2 * nc]
        s, i = pl.program_id(0), pl.program_id(1)
        x, y, c = _place()
        me, sibling = (x, y, c), (x, y, 1 - c)
        chips = [(x, 1 - y), (1 - x, y), (1 - x, 1 - y)]

        def w_part(cx, cy, hc, p):
            return wv.at[2 * cx + cy, p, pl.ds(hc * half, half), :]

        def w_remote(sem, block, to):
            part = w_part(*block)
            return pltpu.make_async_remote_copy(
                src_ref=part, dst_ref=part, send_sem=w_send.at[sem], recv_sem=w_recv.at[sem], device_id=to,
                device_id_type=MESH)

        def w_first(idx, p):
            return w_remote(GATHER_SEMS * p + idx, (x, y, c, p), (*chips[idx], c))

        def w_relay(p):
            src_chip = (jnp.bitwise_xor(x, 1 - c), jnp.bitwise_xor(y, c))
            dst_chip = (jnp.bitwise_xor(x, c), jnp.bitwise_xor(y, 1 - c))
            return w_remote(GATHER_SEMS * p + 2, (*src_chip, c, p), (*dst_chip, c))

        def w_pass(idx, p):
            return w_remote(GATHER_SEMS * p + 3 + idx, (*chips[idx], c, p), sibling)

        def landed(sem, chip, hc, p):
            w_remote(GATHER_SEMS * p + sem, (*chip, hc, p), me).wait_recv()

        def w_store(k, cx, cy, p):
            jj = 2 * cx + cy
            return pltpu.make_async_copy(wv.at[jj, p], wfull_ref.at[:, pl.ds(jj * cols + p * pc, pc)],
                                         w_local.at[2 * k + p])

        start_rest, relay_rest, finish_rest = _gather_steps(shards, ins, outs, send_sems, recv_sems, local_sems)

        def own(k, hc):
            return pltpu.make_async_copy(win_ref.at[pl.ds(pl.multiple_of(hc * half, half), half), :], stage.at[k],
                                         w_local.at[2 * N_CHIPS + k])

        def round_own(k, hc):
            own(k, hc).wait()
            for p in range(2):
                wv[2 * x + y, p, pl.ds(pl.multiple_of(hc * half, half), half), :] = (
                    stage[k, :, p * pc:(p + 1) * pc].astype(BF16))

        wide_in = [pltpu.make_async_copy(cast_ins[k], wide[k], cast_sems.at[k]) for k in range(nc)]
        narrow_out = [pltpu.make_async_copy(narrow[k], cast_outs[k], cast_sems.at[nc + k]) for k in range(nc)]

        def first_step(k):
            return (s == k) & (i == 0)

        @pl.when(first_step(0))
        def _():
            own(0, c).start()
            own(1, 1 - c).start()
            for cp in wide_in:
                cp.start()
            round_own(0, c)
            for p in range(2):
                w_first(0, p).start()
                w_first(1, p).start()
            start_rest()
            round_own(1, 1 - c)
            for p in range(2):
                w_store(0, x, y, p).start()

        def arrive(idx, p, relayed=False):
            landed(2 if relayed else idx, chips[idx], c, p)
            w_pass(idx, p).start()
            if idx == 1:
                w_relay(p).start()
            landed(3 + idx, chips[idx], 1 - c, p)
            w_store(idx + 1, *chips[idx], p).start()

        @pl.when(first_step(2))
        def _():
            for k in range(nc):
                wide_in[k].wait()
                narrow[k][...] = wide[k][...].astype(BF16)
                narrow_out[k].start()
            arrive(0, 0)

        pl.when(first_step(3))(lambda: arrive(1, 0))
        pl.when(first_step(4))(lambda: arrive(0, 1))
        pl.when(first_step(5))(lambda: arrive(1, 1))
        pl.when(first_step(6))(lambda: arrive(2, 0, relayed=True))
        pl.when(first_step(7))(lambda: arrive(2, 1, relayed=True))

        keep_h = pltpu.make_async_copy(h_all.at[i], h_ref.at[pl.ds(pl.multiple_of(i * tb, tb), tb), :],
                                       w_local.at[2 * N_CHIPS + 2])

        @pl.when(s == 0)
        def _():
            xn, _ = _rms(x_ref[...])
            h_all[i] = (xn * g_ref[...]).astype(BF16)
            keep_h.start()

        shard, piece = _in_proj_order(s)
        z_ref[...] = _dot(h_all[i], wv[jnp.bitwise_xor(2 * x + y, shard), piece])
        pl.when(s == 0)(keep_h.wait)

        @pl.when((s == 2 * N_CHIPS - 1) & (i == nb - 1))
        def _():
            relay_rest()
            finish_rest()
            for p in range(2):
                for cp in (w_first(0, p), w_first(1, p), w_relay(p), w_pass(0, p), w_pass(1, p), w_pass(2, p)):
                    cp.wait_send()
                w_store(0, x, y, p).wait()
                for idx in range(3):
                    w_store(idx + 1, *chips[idx], p).wait()
            for cp in narrow_out:
                cp.wait()

    rest_shape, rest_sems = _gather_shapes(shards)
    out_shape = [jax.ShapeDtypeStruct((t, IN_COLS), F32), jax.ShapeDtypeStruct((t, D_MODEL), BF16),
                 jax.ShapeDtypeStruct((D_MODEL, IN_COLS), BF16)] + rest_shape
    out_shape += [jax.ShapeDtypeStruct(a.shape, BF16) for a in casts]
    any_spec = pl.BlockSpec(memory_space=pl.ANY)

    def z_map(s, i):
        shard, piece = _in_proj_order(s)
        return (i, 2 * jnp.bitwise_xor(2 * lax.axis_index("x") + lax.axis_index("y"), shard) + piece)

    return pl.pallas_call(
        body, name="in_proj", out_shape=tuple(out_shape),
        grid=(2 * N_CHIPS, nb),
        in_specs=[pl.BlockSpec((tb, D_MODEL), lambda s, i: (jnp.where(s == 0, i, nb - 1), 0)),
                  pl.BlockSpec((1, D_MODEL), lambda s, i: (0, 0)), any_spec] + [any_spec] * (n + nc),
        out_specs=tuple([pl.BlockSpec((tb, pc), z_map), any_spec, any_spec] + [any_spec] * (n + nc)),
        scratch_shapes=[pltpu.VMEM((N_CHIPS, 2, D_MODEL, pc), BF16), pltpu.VMEM((nb, tb, D_MODEL), BF16)] + rest_sems + [
            pltpu.SemaphoreType.DMA((2 * GATHER_SEMS,)), pltpu.SemaphoreType.DMA((2 * GATHER_SEMS,)),
            pltpu.SemaphoreType.DMA((2 * N_CHIPS + 3,)), pltpu.VMEM((2, half, cols), F32)]
        + [pltpu.VMEM(a.shape, F32) for a in casts] + [pltpu.VMEM(a.shape, BF16) for a in casts]
        + [pltpu.SemaphoreType.DMA((2 * nc,))],
        compiler_params=pltpu.CompilerParams(dimension_semantics=("arbitrary", "arbitrary"),
                                             vmem_limit_bytes=VMEM_LIMIT_BYTES),
    )(x2d, norm_g, w_in_sh, *[sh[0] for sh in shards], *casts)


def _in_proj_bwd(dz, w_in, x2d, dx_res, norm_g, tb, reduce, shards, take):
    t = x2d.shape[0]
    nb = t // tb
    parts, wire, steps = reduce
    n = len(parts)
    k = len(shards)
    take_rows, take_width = take

    def body(dz_ref, w_ref, x_ref, dres_ref, g_ref, *refs):
        at = 2 * n + k
        dx_ref, dg_ref = refs[at:at + 2]
        rs_outs, g_outs = refs[at + 2:at + 2 + n], refs[at + 2 + n:at + 2 + n + k]
        cut_ref = refs[at + 2 + n + k]
        scratch = refs[at + 3 + n + k:]
        rs_scr, g_sems, dg_acc, ar_scr, cut_sem = scratch[:-8], scratch[-8:-5], scratch[-5], scratch[-4:-1], scratch[-1]
        rs = _rs_steps(parts, refs[:2 * n], rs_outs, rs_scr)
        for step, when in zip(rs[:-1], steps):
            pl.when(pl.program_id(0) == when)(step)
        gather = _gather_steps(shards, refs[2 * n:at], g_outs, *g_sems)
        for step, when in zip(gather, (0, nb // 2, nb - 1)):
            pl.when(pl.program_id(0) == when)(step)

        @pl.when(pl.program_id(0) == 0)
        def _():
            dg_acc[...] = jnp.zeros_like(dg_acc)

        xn, r = _rms(x_ref[...])
        g = g_ref[...]
        dh = _dot_nt(dz_ref[...], w_ref[...])
        dg_acc[0:1, :] += jnp.sum(dh * xn, axis=0, keepdims=True)
        dx_ref[...] = dres_ref[...] + _rms_bwd(dh * g, xn, r)

        @pl.when(pl.program_id(0) == nb - 1)
        def _():
            x, y, _ = _place()
            mine = pl.ds(pl.multiple_of((2 * x + y) * take_width, take_width), take_width)
            cut = pltpu.make_async_copy(g_outs[0].at[take_rows, mine], cut_ref, cut_sem)
            cut.start()
            _all_reduce_tile(dg_acc, dg_ref, *ar_scr)
            rs[-1]()
            cut.wait()

    row = lambda i: (i, 0)
    fixed = lambda i: (0, 0)
    rs_shape, rs_scratch = _rs_shapes(parts, wire)
    g_shape, g_sems = _gather_shapes(shards)
    any_spec = pl.BlockSpec(memory_space=pl.ANY)
    cut_shape = jax.ShapeDtypeStruct((take_rows.stop - take_rows.start, take_width), F32)
    return pl.pallas_call(
        body, name="in_proj_bwd",
        out_shape=tuple([jax.ShapeDtypeStruct((t, D_MODEL), F32), jax.ShapeDtypeStruct((F32_SUBLANES, D_MODEL), F32)]
                        + rs_shape + g_shape + [cut_shape]),
        grid=(nb,),
        in_specs=[pl.BlockSpec((tb, IN_COLS), row),
                  pl.BlockSpec((D_MODEL, IN_COLS), fixed, pipeline_mode=pl.Buffered(1)),
                  pl.BlockSpec((tb, D_MODEL), row), pl.BlockSpec((tb, D_MODEL), row),
                  pl.BlockSpec((1, D_MODEL), fixed)] + [any_spec] * (2 * n + k),
        out_specs=tuple([pl.BlockSpec((tb, D_MODEL), row), pl.BlockSpec((F32_SUBLANES, D_MODEL), fixed)]
                        + [any_spec] * (n + k + 1)),
        scratch_shapes=rs_scratch + g_sems + [pltpu.VMEM((F32_SUBLANES, D_MODEL), F32)] + _all_reduce_scratch(
            (F32_SUBLANES, D_MODEL)) + [pltpu.SemaphoreType.DMA(())],
        compiler_params=pltpu.CompilerParams(dimension_semantics=("arbitrary",),
                                             vmem_limit_bytes=VMEM_LIMIT_BYTES),
    )(dz, w_in, x2d, dx_res, norm_g, *_rs_operands(parts), *[sh[0] for sh in shards])


def _weight_grad(lhs, rhs, n_chunks, tb, name, reduce=None):
    t, k = lhs.shape
    nc = rhs.shape[1] // n_chunks
    nb = t // tb
    parts, wire, steps = reduce if reduce is not None else ([], F32, ())
    n = len(parts)

    def body(l_ref, r_ref, *refs):
        o_ref, o16_ref = refs[2 * n:2 * n + 2]
        if n:
            at = pl.program_id(0) * nb + pl.program_id(1)
            rs = _rs_steps(parts, refs[:2 * n], refs[2 * n + 2:3 * n + 2], refs[3 * n + 2:])
            for step, when in zip(rs, steps):
                pl.when(at == when)(step)

        @pl.when(pl.program_id(1) == 0)
        def _():
            o_ref[...] = jnp.zeros_like(o_ref)

        o_ref[...] += _dot_tn(l_ref[...], r_ref[...])

        @pl.when(pl.program_id(1) == nb - 1)
        def _():
            o16_ref[...] = o_ref[...].astype(BF16)

    rs_shape, rs_scratch = _rs_shapes(parts, wire) if n else ([], [])
    any_spec = pl.BlockSpec(memory_space=pl.ANY)
    chunk = pl.BlockSpec((None, k, nc), lambda j, i: (j, 0, 0))
    return pl.pallas_call(
        body, name=name,
        out_shape=tuple([jax.ShapeDtypeStruct((n_chunks, k, nc), F32), jax.ShapeDtypeStruct((n_chunks, k, nc), BF16)]
                        + rs_shape),
        grid=(n_chunks, nb),
        in_specs=[pl.BlockSpec((tb, k), lambda j, i: (i, 0)), pl.BlockSpec((tb, nc), lambda j, i: (i, j))]
        + [any_spec] * (2 * n),
        out_specs=tuple([chunk, chunk] + [any_spec] * n),
        scratch_shapes=rs_scratch,
        compiler_params=pltpu.CompilerParams(dimension_semantics=("arbitrary", "arbitrary"),
                                             vmem_limit_bytes=VMEM_LIMIT_BYTES),
    )(lhs, rhs, *_rs_operands(parts))


def _adam_update(w, g, m, v):
    m_ = ADAM_B1 * m + (1.0 - ADAM_B1) * g
    v_ = ADAM_B2 * v + (1.0 - ADAM_B2) * jnp.square(g)
    m_hat = m_ / (1.0 - ADAM_B1 ** ADAM_STEP)
    v_hat = v_ / (1.0 - ADAM_B2 ** ADAM_STEP)
    return -ADAM_LR * (m_hat / (jnp.sqrt(v_hat) + ADAM_EPS) + ADAM_WD * w), m_, v_


def _adamw_replicated(vec_sum, mat_sum, norm_grad, entries, conv):
    n = len(entries)

    def grad_of(name, shape, vec_ref, mat_ref, norm_ref):
        if name == "norm_g":
            return norm_ref[0:1, :]
        if name in MAT_BAG_AT:
            return mat_ref[MAT_BAG_AT[name]:MAT_BAG_AT[name] + shape[0], :]
        if shape[0] == 1:
            return vec_ref[_bag_row(name), 0:shape[1]]
        return jnp.concatenate([vec_ref[_bag_row(name), h * shape[1]:(h + 1) * shape[1]] for h in range(shape[0])],
                               axis=0)

    def body(vec_ref, mat_ref, norm_ref, *refs):
        ins, outs = refs[:3 * n + 4], refs[3 * n + 4:]
        for k in range(n):
            w_ref, m_ref, v_ref = ins[3 * k:3 * k + 3]
            g = grad_of(entries[k][0], w_ref.shape, vec_ref, mat_ref, norm_ref)
            d, m_, v_ = _adam_update(w_ref[...], g, m_ref[...], v_ref[...])
            for ref, val in zip(outs[4 * k:4 * k + 4], (g, d, m_, v_)):
                ref[...] = val
        w_ref, m_ref, v_ref, g_ref = ins[3 * n:]
        g = g_ref[0:w_ref.shape[0], :]
        for ref, val in zip(outs[4 * n:4 * n + 4], (g,) + _adam_update(w_ref[...], g, m_ref[...], v_ref[...])):
            ref[...] = val
        outs[4 * n + 4][...] = vec_ref[_bag_row("loss"), 0:1]

    arrays = [a for e in entries for a in e[1:]] + list(conv)
    out_shape = [jax.ShapeDtypeStruct(e[1].shape, F32) for e in entries for _ in range(4)]
    out_shape += [jax.ShapeDtypeStruct(conv[0].shape, F32)] * 4 + [jax.ShapeDtypeStruct((1, 1), F32)]
    return pl.pallas_call(
        body, name="adamw_replicated", out_shape=tuple(out_shape),
        compiler_params=pltpu.CompilerParams(vmem_limit_bytes=VMEM_LIMIT_BYTES),
    )(vec_sum, mat_sum, norm_grad, *arrays)


def _adamw(w, g, m, v, rows, name):
    r, c = w.shape

    def body(w_ref, g_ref, m_ref, v_ref, go_ref, d_ref, nm_ref, nv_ref):
        g = g_ref[...]
        go_ref[...] = g
        d_ref[...], nm_ref[...], nv_ref[...] = _adam_update(w_ref[...], g, m_ref[...], v_ref[...])

    spec = pl.BlockSpec((rows, c), lambda i: (i, 0))
    return pl.pallas_call(
        body, name=name, out_shape=tuple(jax.ShapeDtypeStruct((r, c), F32) for _ in range(4)),
        grid=(r // rows,), in_specs=[spec] * 4, out_specs=(spec,) * 4,
        compiler_params=pltpu.CompilerParams(dimension_semantics=("arbitrary",),
                                             vmem_limit_bytes=VMEM_LIMIT_BYTES),
    )(w, g, m, v)


def _adamw_group(items, name):
    n = 4 * len(items)

    def body(*refs):
        ins, outs, bufs = refs[:n], refs[n:2 * n], refs[2 * n:3 * n]
        load_sems, store_sems = refs[3 * n:]
        loads = [pltpu.make_async_copy(ins[j], bufs[j], load_sems.at[j]) for j in range(n)]
        stores = [pltpu.make_async_copy(bufs[j], outs[j], store_sems.at[j]) for j in range(n)]
        for cp in loads:
            cp.start()
        for k in range(len(items)):
            for cp in loads[4 * k:4 * k + 4]:
                cp.wait()
            w_buf, g_buf, m_buf, v_buf = bufs[4 * k:4 * k + 4]
            w_buf[...], m_buf[...], v_buf[...] = _adam_update(w_buf[...], g_buf[...], m_buf[...], v_buf[...])
            for cp in stores[4 * k:4 * k + 4]:
                cp.start()
        for cp in stores:
            cp.wait()

    arrays = [a for item in items for a in item]
    any_spec = pl.BlockSpec(memory_space=pl.ANY)
    flat = pl.pallas_call(
        body, name=name, out_shape=tuple(jax.ShapeDtypeStruct(a.shape, F32) for a in arrays),
        in_specs=[any_spec] * n, out_specs=(any_spec,) * n,
        scratch_shapes=[pltpu.VMEM(a.shape, F32) for a in arrays] + [pltpu.SemaphoreType.DMA((n,))] * 2,
        compiler_params=pltpu.CompilerParams(vmem_limit_bytes=VMEM_LIMIT_BYTES),
    )(*arrays)
    return [(flat[4 * k + 1], flat[4 * k], flat[4 * k + 2], flat[4 * k + 3]) for k in range(len(items))]


def _shift_down(ext, s):
    return pltpu.roll(ext, s, 0)


def _tile_shift(v, s):
    rows, cols = v.shape
    tiles = v.reshape(rows // F32_SUBLANES, F32_SUBLANES, cols)
    return pltpu.roll(tiles, s % F32_SUBLANES, 1).reshape(rows, cols)


def _shift_up(ext, s):
    return pltpu.roll(ext, ext.shape[0] - s, 0)


def _lru_gates(xc, wa_ref, ba, wx_ref, bx, lam):
    pa, px = [], []
    for h in range(LRU_HEADS):
        xh = xc[:, h * HEAD_DIM:(h + 1) * HEAD_DIM].astype(BF16)
        pa.append(_dot(xh, wa_ref[h]))
        px.append(_dot(xh, wx_ref[h]))
    r = _sigmoid(jnp.concatenate(pa, axis=1) + ba)
    ig = _sigmoid(jnp.concatenate(px, axis=1) + bx)
    sp = _softplus(-lam)
    log_a = (-LRU_C * r) * sp
    a = jnp.exp(log_a)
    mult = jnp.sqrt(jnp.tanh(-log_a) * (1.0 + a * a))
    return r, ig, a, mult, sp


def _conv(ext, w_ref, b):
    y = b + _shift_down(ext, 3) * w_ref[0:1, :]
    y = y + _shift_down(ext, 2) * w_ref[1:2, :]
    y = y + _shift_down(ext, 1) * w_ref[2:3, :]
    y = y + ext * w_ref[3:4, :]
    return y[CONV_HIST:, :]


def _pool_diff(ext, pos):
    out = []
    for g, k in enumerate(POOL_WINDOWS):
        col = ext[:, g * POOL_GROUP_DIM:(g + 1) * POOL_GROUP_DIM]
        s = col
        for step in range(g + 1):
            s = s + _shift_down(s, 2 ** step)
        count = jnp.minimum(pos + 1, k).astype(F32)
        out.append(s[POOL_HIST:, :] / count - col[POOL_HIST:, :])
    return out


def _pool_mix(diff, pw_ref):
    return jnp.concatenate([_dot(diff[g].astype(BF16), pw_ref[g]) for g in range(len(POOL_WINDOWS))], axis=1)


def _branch_specs(tb, row_map, fixed):
    fixed3 = lambda i: (0, 0, 0)
    return [pl.BlockSpec((CONV_WIDTH, D_MODEL), fixed), pl.BlockSpec((1, D_MODEL), fixed),
            pl.BlockSpec((LRU_HEADS, HEAD_DIM, HEAD_DIM), fixed3), pl.BlockSpec((1, D_MODEL), fixed),
            pl.BlockSpec((LRU_HEADS, HEAD_DIM, HEAD_DIM), fixed3), pl.BlockSpec((1, D_MODEL), fixed),
            pl.BlockSpec((1, D_MODEL), fixed),
            pl.BlockSpec((len(POOL_WINDOWS), POOL_GROUP_DIM, POOL_GROUP_DIM), fixed3),
            pl.BlockSpec((1, POOL_WIDTH), fixed)]


def _branches_fwd(z, weights, seq, tb, shards):
    t = z.shape[0]
    nb = t // tb
    nbe = seq // tb
    groups = tb // F32_SUBLANES
    n = len(shards)

    def body(xa_ref, ga_ref, xb_ref, gb_ref, cw_ref, cb_ref, wa_ref, ba_ref, wx_ref, bx_ref, lam_ref,
             pw_ref, ps_ref, *refs):
        g_ins = refs[:n]
        ya_ref, yb_ref, hl_ref = refs[n:n + 3]
        g_outs = refs[n + 3:2 * n + 3]
        xa_ext, xb_ext, carry, a_s, u_s, send_sems, recv_sems, local_sems = refs[2 * n + 3:]
        blk = pl.program_id(0) % nbe
        start_gather, relay_gather, finish_gather = _gather_steps(shards, g_ins, g_outs, send_sems, recv_sems,
                                                                  local_sems)
        pl.when(pl.program_id(0) == 0)(start_gather)
        pl.when(pl.program_id(0) == nb // 2)(relay_gather)

        @pl.when(blk == 0)
        def _():
            xa_ext[0:CONV_HIST, :] = jnp.zeros((CONV_HIST, D_MODEL), F32)
            xb_ext[0:POOL_HIST, :] = jnp.zeros((POOL_HIST, POOL_WIDTH), F32)
            carry[...] = jnp.zeros_like(carry)

        xa_ext[CONV_HIST:, :] = xa_ref[...]
        xb_ext[POOL_HIST:, :] = xb_ref[...]
        ea = xa_ext[...]
        eb = xb_ext[...]
        xa_ext[0:CONV_HIST, :] = ea[tb:, :]
        xb_ext[0:POOL_HIST, :] = eb[tb:, :]

        xc = _conv(ea, cw_ref, cb_ref[...])
        _, ig, a, mult, _ = _lru_gates(xc, wa_ref, ba_ref[...], wx_ref, bx_ref[...], lam_ref[...])
        u = mult * (ig * xc)
        row8 = lax.broadcasted_iota(jnp.int32, (tb, D_MODEL), 0) % F32_SUBLANES
        for s in (1, 2, 4):
            m = row8 >= s
            u = jnp.where(m, a * _tile_shift(u, s) + u, u)
            a = jnp.where(m, a * _tile_shift(a, s), a)
        a_s[...] = a
        u_s[...] = u

        def step(g, cr):
            sl = pl.ds(pl.multiple_of(g * F32_SUBLANES, F32_SUBLANES), F32_SUBLANES)
            hb = a_s[sl, :] * cr + u_s[sl, :]
            hl_ref[sl, :] = hb
            return jnp.broadcast_to(hb[F32_SUBLANES - 1:F32_SUBLANES, :], (F32_SUBLANES, D_MODEL))

        carry[...] = lax.fori_loop(0, groups, step, carry[...], unroll=4)
        ga = ga_ref[...]
        ya_ref[...] = (hl_ref[...] * (ga * _sigmoid(ga))).astype(BF16)

        pos = blk * tb + lax.broadcasted_iota(jnp.int32, (tb, POOL_GROUP_DIM), 0)
        ypre = _pool_mix(_pool_diff(eb, pos), pw_ref)
        gb = gb_ref[...]
        yb_ref[...] = ((ypre * ps_ref[...]) * (gb * _sigmoid(gb))).astype(BF16)
        pl.when(pl.program_id(0) == nb - 1)(finish_gather)

    row = lambda i: (i, 0)
    fixed = lambda i: (0, 0)
    any_spec = pl.BlockSpec(memory_space=pl.ANY)
    in_specs = [pl.BlockSpec((tb, D_MODEL), lambda i: (i, 0)), pl.BlockSpec((tb, D_MODEL), lambda i: (i, 1)),
                pl.BlockSpec((tb, POOL_WIDTH), lambda i: (i, 4)), pl.BlockSpec((tb, POOL_WIDTH), lambda i: (i, 5)),
                ] + _branch_specs(tb, row, fixed) + [any_spec] * n
    g_shape, g_sems = _gather_shapes(shards)
    return pl.pallas_call(
        body, name="branches_fwd",
        out_shape=tuple([jax.ShapeDtypeStruct((t, D_MODEL), BF16), jax.ShapeDtypeStruct((t, POOL_WIDTH), BF16),
                         jax.ShapeDtypeStruct((t, D_MODEL), F32)] + g_shape),
        grid=(nb,), in_specs=in_specs,
        out_specs=tuple([pl.BlockSpec((tb, D_MODEL), row), pl.BlockSpec((tb, POOL_WIDTH), row),
                         pl.BlockSpec((tb, D_MODEL), row)] + [any_spec] * n),
        scratch_shapes=[pltpu.VMEM((tb + CONV_HIST, D_MODEL), F32), pltpu.VMEM((tb + POOL_HIST, POOL_WIDTH), F32),
                        pltpu.VMEM((F32_SUBLANES, D_MODEL), F32),
                        pltpu.VMEM((tb, D_MODEL), F32), pltpu.VMEM((tb, D_MODEL), F32)] + g_sems,
        compiler_params=pltpu.CompilerParams(dimension_semantics=("arbitrary",),
                                             vmem_limit_bytes=VMEM_LIMIT_BYTES),
    )(z, z, z, z, *weights, *[sh[0] for sh in shards])


def _branches_bwd(z, hl, dya, dyb, dzm, weights, vec_bag, seq, tb, riders):
    t = z.shape[0]
    nb = t // tb
    nbe = seq // tb
    groups = tb // F32_SUBLANES
    nr = len(riders)

    def body(xa_ref, xap_ref, ga_ref, xb_ref, xbp_ref, gb_ref, hl_ref, hlp_ref, dya_ref, dyb_ref, dzm_ref,
             cw_ref, cb_ref, wa_ref, ba_ref, wx_ref, bx_ref, lam_ref, pw_ref, ps_ref, vec_in_ref, *rest):
        pairs, (dz_ref, vec_ref, mat_ref), grads = rest[:2 * nr], rest[2 * nr:2 * nr + 3], rest[2 * nr + 3:4 * nr + 3]
        xa_ext, xb_ext, hl_ext, a_ext, dxc_ext, dwin_ext, g_carry, b_s, d_s, g_s = rest[4 * nr + 3:]
        i = pl.program_id(0)
        blk = (nb - 1 - i) % nbe

        def mat_rows(name, k):
            at = MAT_BAG_AT[name] + k * HEAD_DIM
            return slice(at, at + HEAD_DIM)

        def rider(k):
            grads[2 * k][...] += _dot_tn(pairs[2 * k][...], pairs[2 * k + 1][...])

        @pl.when(i == 0)
        def _():
            vec_ref[...] = vec_in_ref[...]
            mat_ref[...] = jnp.zeros_like(mat_ref)
            for k in range(nr):
                grads[2 * k][...] = jnp.zeros_like(grads[2 * k])

        @pl.when(blk == nbe - 1)
        def _():
            a_ext[tb:, :] = jnp.zeros((F32_SUBLANES, D_MODEL), F32)
            dxc_ext[tb:, :] = jnp.zeros((CONV_HIST, D_MODEL), F32)
            dwin_ext[tb:, :] = jnp.zeros((POOL_HIST, POOL_WIDTH), F32)
            g_carry[...] = jnp.zeros_like(g_carry)

        live = (blk > 0).astype(F32)
        xa_ext[0:CONV_HIST, :] = xap_ref[...] * live
        xa_ext[CONV_HIST:, :] = xa_ref[...]
        xb_ext[0:POOL_HIST, :] = xbp_ref[...] * live
        xb_ext[POOL_HIST:, :] = xb_ref[...]
        hl_ext[0:F32_SUBLANES, :] = hlp_ref[...] * live
        hl_ext[F32_SUBLANES:, :] = hl_ref[...]
        ea = xa_ext[...]
        eb = xb_ext[...]
        rider(0)

        xc = _conv(ea, cw_ref, cb_ref[...])
        lam = lam_ref[...]
        r, ig, a, mult, sp = _lru_gates(xc, wa_ref, ba_ref[...], wx_ref, bx_ref[...], lam)
        hl = hl_ref[...]
        ga = ga_ref[...]
        sga = _sigmoid(ga)
        dya = dya_ref[...]
        dhl = dya * (ga * sga)
        dz_ref[:, D_MODEL:2 * D_MODEL] = (dya * hl * (sga * (1.0 + ga * (1.0 - sga)))).astype(BF16)

        a_ext[0:tb, :] = a
        b = _shift_up(a_ext[...], 1)[0:tb, :]
        a_ext[tb:, :] = jnp.broadcast_to(a[0:1, :], (F32_SUBLANES, D_MODEL))
        d = dhl
        row8 = lax.broadcasted_iota(jnp.int32, (tb, D_MODEL), 0) % F32_SUBLANES
        for s in (1, 2, 4):
            m = row8 < F32_SUBLANES - s
            d = jnp.where(m, d + b * _tile_shift(d, -s), d)
            b = jnp.where(m, b * _tile_shift(b, -s), b)
        b_s[...] = b
        d_s[...] = d

        def step(k, cr):
            sl = pl.ds(pl.multiple_of((groups - 1 - k) * F32_SUBLANES, F32_SUBLANES), F32_SUBLANES)
            gb_ = d_s[sl, :] + b_s[sl, :] * cr
            g_s[sl, :] = gb_
            return jnp.broadcast_to(gb_[0:1, :], (F32_SUBLANES, D_MODEL))

        g_carry[...] = lax.fori_loop(0, groups, step, g_carry[...], unroll=4)
        rider(1)
        gsc = g_s[...]
        da = gsc * _shift_down(hl_ext[...], 1)[F32_SUBLANES:, :]
        dmult = gsc * (ig * xc)
        dig = gsc * (mult * xc)
        dxc = gsc * (mult * ig)
        dlog_a = da * a - (a * a) * dmult / mult
        dr = dlog_a * (-LRU_C * sp)
        vec_ref[_bag_row("lru_lambda"), :] += jnp.sum(dlog_a * (-LRU_C * r), axis=0, keepdims=True)
        dpa = dr * (r * (1.0 - r))
        dpx = dig * (ig * (1.0 - ig))
        vec_ref[_bag_row("lru_b_a"), :] += jnp.sum(dpa, axis=0, keepdims=True)
        vec_ref[_bag_row("lru_b_x"), :] += jnp.sum(dpx, axis=0, keepdims=True)
        back = []
        for h in range(LRU_HEADS):
            cols = slice(h * HEAD_DIM, (h + 1) * HEAD_DIM)
            xh = xc[:, cols].astype(BF16)
            dpa_h = dpa[:, cols].astype(BF16)
            dpx_h = dpx[:, cols].astype(BF16)
            mat_ref[mat_rows("lru_w_a", h), :] += _dot_tn(xh, dpa_h)
            mat_ref[mat_rows("lru_w_x", h), :] += _dot_tn(xh, dpx_h)
            back.append(_dot_nt(dpa_h, wa_ref[h]) + _dot_nt(dpx_h, wx_ref[h]))
        dxc = dxc + jnp.concatenate(back, axis=1)
        vec_ref[_bag_row("conv_b"), :] += jnp.sum(dxc, axis=0, keepdims=True)
        for k in range(CONV_WIDTH):
            tap = _shift_down(ea, CONV_WIDTH - 1 - k)[CONV_HIST:, :] if k < CONV_WIDTH - 1 else ea[CONV_HIST:, :]
            vec_ref[_bag_row("conv_w", k), :] += jnp.sum(dxc * tap, axis=0, keepdims=True)
        dxc_ext[0:tb, :] = dxc
        ed = dxc_ext[...]
        dxa = ed * cw_ref[3:4, :]
        dxa = dxa + _shift_up(ed, 1) * cw_ref[2:3, :]
        dxa = dxa + _shift_up(ed, 2) * cw_ref[1:2, :]
        dxa = dxa + _shift_up(ed, 3) * cw_ref[0:1, :]
        dz_ref[:, 0:D_MODEL] = dxa[0:tb, :].astype(BF16)
        dxc_ext[tb:, :] = dxc[0:CONV_HIST, :]

        pos = blk * tb + lax.broadcasted_iota(jnp.int32, (tb, POOL_GROUP_DIM), 0)
        diff = _pool_diff(eb, pos)
        rider(2)
        ypre = _pool_mix(diff, pw_ref)
        ps = ps_ref[...]
        gb = gb_ref[...]
        sgb = _sigmoid(gb)
        dyb = dyb_ref[...]
        dyp = dyb * (gb * sgb)
        dz_ref[:, 2 * D_MODEL + POOL_WIDTH:3 * D_MODEL] = (
            dyb * (ypre * ps) * (sgb * (1.0 + gb * (1.0 - sgb)))).astype(BF16)
        vec_ref[_bag_row("pool_scale"), 0:POOL_WIDTH] += jnp.sum(dyp * ypre, axis=0, keepdims=True)
        dypre = dyp * ps
        for g, k in enumerate(POOL_WINDOWS):
            cols = slice(g * POOL_GROUP_DIM, (g + 1) * POOL_GROUP_DIM)
            dyg = dypre[:, cols].astype(BF16)
            mat_ref[mat_rows("pool_w", g), :] += _dot_tn(diff[g].astype(BF16), dyg)
            ddiff = _dot_nt(dyg, pw_ref[g])
            count = jnp.minimum(pos + 1, k).astype(F32)
            dwin = ddiff / count
            dwin_ext[0:tb, cols] = dwin
            s = dwin_ext[:, cols]
            for step_ in range(g + 1):
                s = s + _shift_up(s, 2 ** step_)
            dz_ref[:, 2 * D_MODEL + g * POOL_GROUP_DIM:2 * D_MODEL + (g + 1) * POOL_GROUP_DIM] = (
                s[0:tb, :] - ddiff).astype(BF16)
            dwin_ext[tb:, cols] = dwin[0:POOL_HIST, :]

        dz_ref[:, 3 * D_MODEL:] = dzm_ref[...]

        @pl.when(i == nb - 1)
        def _():
            row = _bag_row("lru_lambda")
            vec_ref[row, :] = vec_ref[row, :] * (-_sigmoid(-lam))
            for k in range(nr):
                grads[2 * k + 1][...] = grads[2 * k][...].astype(BF16)

    rev = lambda i: (nb - 1 - i, 0)
    fixed = lambda i: (0, 0)

    def prev(rows, col):
        per = tb // rows
        return lambda i: (jnp.maximum((nb - 1 - i) * per - 1, 0), col)

    in_specs = [pl.BlockSpec((tb, D_MODEL), lambda i: (nb - 1 - i, 0)),
                pl.BlockSpec((CONV_HIST, D_MODEL), prev(CONV_HIST, 0)),
                pl.BlockSpec((tb, D_MODEL), lambda i: (nb - 1 - i, 1)),
                pl.BlockSpec((tb, POOL_WIDTH), lambda i: (nb - 1 - i, 4)),
                pl.BlockSpec((POOL_HIST, POOL_WIDTH), prev(POOL_HIST, 4)),
                pl.BlockSpec((tb, POOL_WIDTH), lambda i: (nb - 1 - i, 5)),
                pl.BlockSpec((tb, D_MODEL), rev),
                pl.BlockSpec((F32_SUBLANES, D_MODEL), prev(F32_SUBLANES, 0)),
                pl.BlockSpec((tb, D_MODEL), rev), pl.BlockSpec((tb, POOL_WIDTH), rev),
                pl.BlockSpec((tb, 2 * D_MODEL), rev)] + _branch_specs(tb, rev, fixed) + [
                    pl.BlockSpec((VEC_BAG_ROWS, D_MODEL), fixed)]
    vec_at = len(in_specs) - 1
    out_shape = [jax.ShapeDtypeStruct((t, IN_COLS), BF16), jax.ShapeDtypeStruct((VEC_BAG_ROWS, D_MODEL), F32),
                 jax.ShapeDtypeStruct((MAT_BAG_ROWS, HEAD_DIM), F32)]
    out_specs = [pl.BlockSpec((tb, IN_COLS), rev), pl.BlockSpec((VEC_BAG_ROWS, D_MODEL), fixed),
                 pl.BlockSpec((MAT_BAG_ROWS, HEAD_DIM), fixed)]
    for lhs, rhs in riders:
        in_specs += [pl.BlockSpec((tb, lhs.shape[1]), rev), pl.BlockSpec((tb, rhs.shape[1]), rev)]
        grad = (lhs.shape[1], rhs.shape[1])
        out_shape += [jax.ShapeDtypeStruct(grad, F32), jax.ShapeDtypeStruct(grad, BF16)]
        out_specs += [pl.BlockSpec(grad, fixed)] * 2
    scratch = [pltpu.VMEM((tb + CONV_HIST, D_MODEL), F32), pltpu.VMEM((tb + POOL_HIST, POOL_WIDTH), F32),
               pltpu.VMEM((tb + F32_SUBLANES, D_MODEL), F32), pltpu.VMEM((tb + F32_SUBLANES, D_MODEL), F32),
               pltpu.VMEM((tb + CONV_HIST, D_MODEL), F32), pltpu.VMEM((tb + POOL_HIST, POOL_WIDTH), F32),
               pltpu.VMEM((F32_SUBLANES, D_MODEL), F32),
               pltpu.VMEM((tb, D_MODEL), F32), pltpu.VMEM((tb, D_MODEL), F32), pltpu.VMEM((tb, D_MODEL), F32)]
    return pl.pallas_call(
        body, name="branches_bwd", out_shape=tuple(out_shape), grid=(nb,), in_specs=in_specs,
        out_specs=tuple(out_specs), scratch_shapes=scratch, input_output_aliases={vec_at: 1},
        compiler_params=pltpu.CompilerParams(dimension_semantics=("arbitrary",),
                                             vmem_limit_bytes=VMEM_LIMIT_BYTES),
    )(z, z, z, z, z, z, hl, hl, dya, dyb, dzm, *weights, vec_bag, *[a for pair in riders for a in pair])


def _merge_head(x2d, ya, yb, z, p2d, tgt, w_pl, w_pp, w_out, w_pg, w_pe, g2, gf, tb):
    t = x2d.shape[0]
    p_dim = p2d.shape[1]

    def body(x_ref, ya_ref, yb_ref, ma_ref, mb_ref, p_ref, t_ref, wpl_ref, wpp_ref, wout_ref, wpg_ref, wpe_ref,
             g2_ref, gf_ref,
             bag_ref, dxr_ref, dya_ref, dyb_ref, dzm_ref,
             mg_ref, do_ref, hn_ref, dgp_ref, dpe_ref, da_ref, dbm_ref, pbf_ref):
        @pl.when(pl.program_id(0) == 0)
        def _():
            bag_ref[...] = jnp.zeros_like(bag_ref)

        a_ = _dot(ya_ref[...], wpl_ref[...])
        bm = _dot(yb_ref[...], wpp_ref[...])
        sa = _sigmoid(ma_ref[...])
        sb = _sigmoid(mb_ref[...])
        mg = (sa * a_ + sb * bm).astype(BF16)
        mg_ref[...] = mg
        x1 = x_ref[...] + _dot(mg, wout_ref[...])
        xn2, r2 = _rms(x1)
        g2 = g2_ref[...]
        hn = (xn2 * g2).astype(BF16)
        hn_ref[...] = hn
        gate = _sigmoid(_dot(hn, wpg_ref[...]))
        pbf = p_ref[...].astype(BF16)
        pbf_ref[...] = pbf
        pe = _dot(pbf, wpe_ref[...])
        x2 = x1 + gate * pe
        xn3, r3 = _rms(x2)
        gf = gf_ref[...]
        err = xn3 * gf - t_ref[...]
        bag_ref[_bag_rows("loss"), 0:128] += 0.5 * jnp.sum(jnp.mean(err * err, axis=-1))

        dy = err * (1.0 / D_MODEL)
        bag_ref[_bag_row("final_g"), :] += jnp.sum(dy * xn3, axis=0, keepdims=True)
        dx2 = _rms_bwd(dy * gf, xn3, r3)
        dpe_ref[...] = (dx2 * gate).astype(BF16)
        dgp = ((dx2 * pe) * (gate * (1.0 - gate))).astype(BF16)
        dgp_ref[...] = dgp
        dhn = _dot_nt(dgp, wpg_ref[...])
        bag_ref[_bag_row("ple_norm_g"), :] += jnp.sum(dhn * xn2, axis=0, keepdims=True)
        dx1 = dx2 + _rms_bwd(dhn * g2, xn2, r2)
        dxr_ref[...] = dx1
        do = dx1.astype(BF16)
        do_ref[...] = do
        dmg = _dot_nt(do, wout_ref[...])
        da = (dmg * sa).astype(BF16)
        dbm = (dmg * sb).astype(BF16)
        da_ref[...] = da
        dbm_ref[...] = dbm
        dzm_ref[:, 0:D_MODEL] = (dmg * a_ * (sa * (1.0 - sa))).astype(BF16)
        dzm_ref[:, D_MODEL:] = (dmg * bm * (sb * (1.0 - sb))).astype(BF16)
        dya_ref[...] = _dot_nt(da, wpl_ref[...])
        dyb_ref[...] = _dot_nt(dbm, wpp_ref[...])

    row = lambda i: (i, 0)
    fixed = lambda i: (0, 0)

    def resident(shape):
        return pl.BlockSpec(shape, fixed, pipeline_mode=pl.Buffered(1))

    tok = lambda width: pl.BlockSpec((tb, width), row)
    in_specs = [tok(D_MODEL), tok(D_MODEL), tok(POOL_WIDTH),
                pl.BlockSpec((tb, D_MODEL), lambda i: (i, 3)), pl.BlockSpec((tb, D_MODEL), lambda i: (i, 4)),
                tok(p_dim), tok(D_MODEL),
                resident((D_MODEL, D_MODEL)), resident((POOL_WIDTH, D_MODEL)), resident((D_MODEL, D_MODEL)),
                resident((D_MODEL, D_MODEL)), resident((p_dim, D_MODEL)),
                pl.BlockSpec((1, D_MODEL), fixed), pl.BlockSpec((1, D_MODEL), fixed)]
    bf = lambda width: jax.ShapeDtypeStruct((t, width), BF16)
    f32 = lambda width: jax.ShapeDtypeStruct((t, width), F32)
    out_shape = (jax.ShapeDtypeStruct((VEC_BAG_ROWS, D_MODEL), F32),
                 f32(D_MODEL), f32(D_MODEL), f32(POOL_WIDTH), bf(2 * D_MODEL),
                 bf(D_MODEL), bf(D_MODEL), bf(D_MODEL), bf(D_MODEL), bf(D_MODEL), bf(D_MODEL), bf(D_MODEL), bf(p_dim))
    out_specs = (pl.BlockSpec((VEC_BAG_ROWS, D_MODEL), fixed),
                 tok(D_MODEL), tok(D_MODEL), tok(POOL_WIDTH), tok(2 * D_MODEL),
                 tok(D_MODEL), tok(D_MODEL), tok(D_MODEL), tok(D_MODEL), tok(D_MODEL), tok(D_MODEL), tok(D_MODEL),
                 tok(p_dim))
    return pl.pallas_call(
        body, name="merge_head", out_shape=out_shape, grid=(t // tb,), in_specs=in_specs, out_specs=out_specs,
        compiler_params=pltpu.CompilerParams(dimension_semantics=("arbitrary",),
                                             vmem_limit_bytes=VMEM_LIMIT_BYTES),
    )(x2d, ya, yb, z, z, p2d, tgt, w_pl, w_pp, w_out, w_pg, w_pe, g2, gf)


def kernel(x, p, norm_g, w_in, conv_w, conv_b, lru_w_a, lru_b_a, lru_w_x, lru_b_x, lru_lambda, pool_w, pool_scale, w_proj_lru, w_proj_pool, w_out, ple_norm_g, w_ple_gate, w_ple_proj, final_g, loss_target, m_norm_g, m_w_in, m_conv_w, m_conv_b, m_lru_w_a, m_lru_b_a, m_lru_w_x, m_lru_b_x, m_lru_lambda, m_pool_w, m_pool_scale, m_w_proj_lru, m_w_proj_pool, m_w_out, m_ple_norm_g, m_w_ple_gate, m_w_ple_proj, m_final_g, v_norm_g, v_w_in, v_conv_w, v_conv_b, v_lru_w_a, v_lru_b_a, v_lru_w_x, v_lru_b_x, v_lru_lambda, v_pool_w, v_pool_scale, v_w_proj_lru, v_w_proj_pool, v_w_out, v_ple_norm_g, v_w_ple_gate, v_w_ple_proj, v_final_g):
    bsz, seq, _ = x.shape
    t = bsz * seq
    tb_mm = min(1024, seq)
    tb_seq = min(256, seq // 2) if seq >= 512 else seq
    x2d = x.reshape(t, D_MODEL)
    p2d = p.reshape(t, p.shape[-1])
    tgt = loss_target.reshape(t, D_MODEL)

    rest = [(w_proj_lru[0], 0), (w_proj_pool[0], 1), (w_out[0], 0), (w_ple_gate[0], 0), (w_ple_proj[0], 1)]
    z, h_bf, w_in_f, conv_w_f, *narrow = _in_proj_gather(
        x2d, norm_g, w_in[0], [(conv_w[0], 1, False)], tb_mm,
        [w for w, _ in rest] + [lru_w_a[0], lru_w_x[0], pool_w[0]])
    wa_bf, wx_bf, pw_bf = narrow[len(rest):]
    branch_w = (conv_w_f, conv_b, wa_bf, lru_b_a.reshape(1, D_MODEL), wx_bf, lru_b_x.reshape(1, D_MODEL),
                lru_lambda, pw_bf, pool_scale)

    ya, yb, hl, w_pl_f, w_pp_f, w_out_f, w_pg_f, w_pe_f = _branches_fwd(
        z, branch_w, seq, tb_seq, [(w16, axis, True) for w16, (_, axis) in zip(narrow, rest)])
    (vec_bag, dx_res, dya, dyb, dzm, mg_bf, do_bf, hn_bf, dgp_bf, dpe_bf, da_bf, dbm_bf, p_bf) = _merge_head(
        x2d, ya, yb, z, p2d, tgt, w_pl_f, w_pp_f, w_out_f, w_pg_f, w_pe_f, ple_norm_g, final_g.reshape(1, D_MODEL),
        tb_seq)
    dz, vec_bag, mat_bag, g_out, g_out16, g_pp, g_pp16, g_pe, g_pe16 = _branches_bwd(
        z, hl, dya, dyb, dzm, branch_w, vec_bag, seq, tb_seq, [(mg_bf, do_bf), (yb, dbm_bf), (p_bf, dpe_bf)])

    tb_dw = min(1024, seq)
    def row_pieces(g32, g16):
        pieces = (8, g32.shape[0] // 8, g32.shape[1])
        return g32.reshape(pieces), False, g16.reshape(pieces)

    def proj_grad(lhs, rhs, name):
        g32, g16 = _weight_grad(lhs, rhs, 1, tb_dw, name)
        return row_pieces(g32[0], g16[0])

    p_dim = p2d.shape[1]
    proj_parts = [proj_grad(ya, da_bf, "dw_proj_lru"), (g_pp, True, g_pp16), row_pieces(g_out, g_out16),
                  proj_grad(hn_bf, dgp_bf, "dw_ple_gate"), (g_pe, True, g_pe16)]
    nb_dw = t // tb_dw
    g_in, g_in16, r_pl, r_pp, r_out, r_pg, r_pe, vec_mine, mat_mine = _weight_grad(
        h_bf, dz, N_CHIPS, tb_dw, "dw_in",
        reduce=(proj_parts + [(vec_bag.reshape(8, VEC_BAG_ROWS // 8, D_MODEL), False, None),
                              (mat_bag.reshape(8, MAT_BAG_ROWS // 8, HEAD_DIM), False, None)],
                [BF16] * 5 + [F32] * 2,
                (0, nb_dw // 2, 2 * nb_dw - 1, 3 * nb_dw + nb_dw // 2, N_CHIPS * nb_dw - 1)))
    pieces = (8, D_MODEL // 2, IN_COLS // N_CHIPS)
    nb_seq = t // tb_seq
    dx, g_g1, r_in, vec_sum, mat_sum, g_cw = _in_proj_bwd(
        dz, w_in_f, x2d, dx_res, norm_g, tb_seq,
        reduce=([(g_in.reshape(pieces), False, g_in16.reshape(pieces))], BF16,
                (0, nb_seq // 8, nb_seq // 2, nb_seq - 1)),
        shards=[(vec_mine.reshape(VEC_BAG_ROWS // N_CHIPS, D_MODEL), 0, True),
                (mat_mine.reshape(MAT_BAG_ROWS // N_CHIPS, HEAD_DIM), 0, True)],
        take=(_bag_rows("conv_w"), D_MODEL // N_CHIPS))

    u_in = tuple(a[None] for a in _adamw(w_in[0], r_in.reshape(D_MODEL, IN_COLS // N_CHIPS), m_w_in[0], v_w_in[0],
                                         D_MODEL // 4, "adamw_w_in"))
    proj = [(w_proj_lru, r_pl, m_w_proj_lru, v_w_proj_lru), (w_proj_pool, r_pp, m_w_proj_pool, v_w_proj_pool),
            (w_out, r_out, m_w_out, v_w_out), (w_ple_gate, r_pg, m_w_ple_gate, v_w_ple_gate),
            (w_ple_proj, r_pe, m_w_ple_proj, v_w_ple_proj)]
    u_pl, u_pp, u_out, u_pg, u_pe = [tuple(a[None] for a in u) for u in _adamw_group(
        [(w[0], g.reshape(w.shape[1:]), m[0], v[0]) for w, g, m, v in proj], "adamw_proj")]

    small = [("norm_g", norm_g, m_norm_g, v_norm_g), ("conv_b", conv_b, m_conv_b, v_conv_b),
             ("lru_w_a", lru_w_a, m_lru_w_a, v_lru_w_a), ("lru_b_a", lru_b_a, m_lru_b_a, v_lru_b_a),
             ("lru_w_x", lru_w_x, m_lru_w_x, v_lru_w_x), ("lru_b_x", lru_b_x, m_lru_b_x, v_lru_b_x),
             ("lru_lambda", lru_lambda, m_lru_lambda, v_lru_lambda), ("pool_w", pool_w, m_pool_w, v_pool_w),
             ("pool_scale", pool_scale, m_pool_scale, v_pool_scale),
             ("ple_norm_g", ple_norm_g, m_ple_norm_g, v_ple_norm_g), ("final_g", final_g, m_final_g, v_final_g)]

    def view(a):
        return a.reshape(-1, a.shape[-1]) if a.ndim != 3 else a[0]

    flat = _adamw_replicated(vec_sum, mat_sum, g_g1, [(name,) + tuple(view(a) for a in arrs) for name, *arrs in small],
                             (conv_w[0], m_conv_w[0], v_conv_w[0], g_cw))
    u_small = {name: tuple(flat[4 * k + pick].reshape(arrs[0].shape) for pick in range(4))
               for k, (name, *arrs) in enumerate(small)}
    u_cw = tuple(a[None] for a in flat[4 * len(small):4 * len(small) + 4])

    loss = flat[-1].reshape(())
    grad_x = dx.reshape(bsz, seq, D_MODEL)

    def ordered(pick):
        s = {name: u[pick] for name, u in u_small.items()}
        return [s["norm_g"], u_in[pick], u_cw[pick], s["conv_b"], s["lru_w_a"], s["lru_b_a"], s["lru_w_x"], s["lru_b_x"],
                s["lru_lambda"], s["pool_w"], s["pool_scale"], u_pl[pick], u_pp[pick], u_out[pick], s["ple_norm_g"],
                u_pg[pick], u_pe[pick], s["final_g"]]

    return (loss, grad_x, *ordered(0), *ordered(1), *ordered(2), *ordered(3))
```

```python
import jax
import jax.numpy as jnp
from jax import lax
from jax.experimental import pallas as pl
from jax.experimental.pallas import tpu as pltpu

F32 = jnp.float32
BF16 = jnp.bfloat16
MESH = pl.DeviceIdType.MESH

D_MODEL = 1024
LRU_HEADS = 8
HEAD_DIM = 128
CONV_WIDTH = 4
LRU_C = 8.0
POOL_WIDTH = 512
POOL_WINDOWS = (2, 4, 8, 16)
POOL_GROUP_DIM = 128
IN_COLS = 5120
N_CHIPS = 4
EPS = 1e-6

ADAM_LR = 0.001
ADAM_B1 = 0.9
ADAM_B2 = 0.999
ADAM_EPS = 1e-08
ADAM_WD = 0.01
ADAM_STEP = 10

F32_SUBLANES = 8
CONV_HIST = 8
POOL_HIST = 16
VMEM_LIMIT_BYTES = 58 * 1024 * 1024
VEC_BAG_SLOTS = ("norm_g", "conv_w", "conv_b", "lru_b_a", "lru_b_x", "lru_lambda", "pool_scale", "ple_norm_g",
                 "final_g", "loss")
VEC_BAG_ROWS = 128
MAT_BAG_AT = {"lru_w_a": 0, "lru_w_x": LRU_HEADS * HEAD_DIM, "pool_w": 2 * LRU_HEADS * HEAD_DIM}
MAT_BAG_ROWS = 2 * LRU_HEADS * HEAD_DIM + len(POOL_WINDOWS) * POOL_GROUP_DIM


def _bag_row(name, k=0):
    at = F32_SUBLANES * VEC_BAG_SLOTS.index(name) + k
    return slice(at, at + 1)


def _bag_rows(name):
    at = F32_SUBLANES * VEC_BAG_SLOTS.index(name)
    return slice(at, at + F32_SUBLANES)


def _dot(a, b):
    return jnp.dot(a, b, preferred_element_type=F32)


def _dot_nt(a, b):
    return lax.dot_general(a, b, (((1,), (1,)), ((), ())), preferred_element_type=F32)


def _dot_tn(a, b):
    return lax.dot_general(a, b, (((0,), (0,)), ((), ())), preferred_element_type=F32)


def _sigmoid(v):
    return jax.nn.sigmoid(v)


def _softplus(v):
    return jnp.maximum(v, 0.0) + jnp.log1p(jnp.exp(-jnp.abs(v)))


def _place():
    return lax.axis_index("x"), lax.axis_index("y"), lax.axis_index("c")


GATHER_SEMS = 6


def _gather_shapes(shards):
    out_shape = []
    for arr, axis, _ in shards:
        r, cols = arr.shape
        out_shape.append(jax.ShapeDtypeStruct((N_CHIPS * r, cols) if axis == 0 else (r, N_CHIPS * cols), arr.dtype))
    n = len(shards)
    sems = [pltpu.SemaphoreType.DMA((n * GATHER_SEMS,)), pltpu.SemaphoreType.DMA((n * GATHER_SEMS,)),
            pltpu.SemaphoreType.DMA((n,))]
    return out_shape, sems


def _gather_steps(shards, ins, outs, send_sems, recv_sems, local_sems):
    n = len(shards)
    x, y, c = _place()
    me, sibling = (x, y, c), (x, y, 1 - c)
    chips = [(x, 1 - y), (1 - x, y), (1 - x, 1 - y)]

    def region(k, cx, cy, hc):
        (r, cols), axis = shards[k][0].shape, shards[k][1]
        j = 2 * cx + cy
        if axis == 0:
            if hc is None:
                return outs[k].at[pl.ds(j * r, r), :]
            return outs[k].at[pl.ds(j * r + hc * (r // 2), r // 2), :]
        if hc is None:
            return outs[k].at[:, pl.ds(j * cols, cols)]
        return outs[k].at[pl.ds(hc * (r // 2), r // 2), pl.ds(j * cols, cols)]

    def remote(k, sem, block, to, src=None):
        dst = region(k, *block)
        return pltpu.make_async_remote_copy(
            src_ref=dst if src is None else src, dst_ref=dst,
            send_sem=send_sems.at[k * GATHER_SEMS + sem], recv_sem=recv_sems.at[k * GATHER_SEMS + sem],
            device_id=to, device_id_type=MESH)

    def first(k, idx):
        r, split = shards[k][0].shape[0], shards[k][2]
        src = ins[k].at[pl.ds(c * (r // 2), r // 2), :] if split else ins[k]
        return remote(k, idx, (x, y, c if split else None), (*chips[idx], c), src=src)

    def relay(k):
        src_chip = (jnp.bitwise_xor(x, 1 - c), jnp.bitwise_xor(y, c))
        dst_chip = (jnp.bitwise_xor(x, c), jnp.bitwise_xor(y, 1 - c))
        return remote(k, 2, (*src_chip, c), (*dst_chip, c))

    def passed(k, idx):
        return remote(k, 3 + idx, (*chips[idx], c), sibling)

    def mine(k):
        return pltpu.make_async_copy(ins[k], region(k, x, y, None), local_sems.at[k])

    def start():
        for k in range(n):
            mine(k).start()
            for idx in range(2 if shards[k][2] else 3):
                first(k, idx).start()

    def relay_on():
        for k in range(n):
            split = shards[k][2]
            for idx in range(2):
                remote(k, idx, (*chips[idx], c if split else None), me).wait_recv()
            if split:
                relay(k).start()
                passed(k, 0).start()
                passed(k, 1).start()

    def finish():
        for k in range(n):
            split = shards[k][2]
            remote(k, 2, (*chips[2], c if split else None), me).wait_recv()
            if split:
                passed(k, 2).start()
        for k in range(n):
            if shards[k][2]:
                for idx in range(3):
                    remote(k, 3 + idx, (*chips[idx], 1 - c), me).wait_recv()
        for k in range(n):
            if shards[k][2]:
                for cp in (first(k, 0), first(k, 1), relay(k), passed(k, 0), passed(k, 1), passed(k, 2)):
                    cp.wait_send()
            else:
                for idx in range(3):
                    first(k, idx).wait_send()
            mine(k).wait()

    return start, relay_on, finish


RS_ADD_ROWS = (64, 32, 16, 8)


N_DEV = 2 * N_CHIPS


def _all_reduce_scratch(shape):
    return [pltpu.VMEM((N_DEV,) + tuple(shape), F32), pltpu.SemaphoreType.DMA((N_DEV - 1,)),
            pltpu.SemaphoreType.DMA((N_DEV - 1,))]


def _all_reduce_tile(v_ref, o_ref, slots, send_sems, recv_sems):
    flips = [(dx, dy, dc) for dx in (0, 1) for dy in (0, 1) for dc in (0, 1)][1:]
    x, y, c = _place()
    mine = 4 * x + 2 * y + c

    def copy(k, to_flip, slot):
        dx, dy, dc = to_flip
        peer = (jnp.bitwise_xor(x, dx), jnp.bitwise_xor(y, dy), jnp.bitwise_xor(c, dc))
        return pltpu.make_async_remote_copy(
            src_ref=v_ref, dst_ref=slots.at[slot], send_sem=send_sems.at[k], recv_sem=recv_sems.at[k],
            device_id=peer, device_id_type=MESH)

    sends = [copy(k, flip, mine) for k, flip in enumerate(flips)]
    for cp in sends:
        cp.start()
    slots[mine] = v_ref[...]
    for k, (dx, dy, dc) in enumerate(flips):
        copy(k, (dx, dy, dc), jnp.bitwise_xor(mine, 4 * dx + 2 * dy + dc)).wait_recv()
    total = slots[0]
    for d in range(1, N_DEV):
        total = total + slots[d]
    o_ref[...] = total
    for cp in sends:
        cp.wait_send()


RS_SEMS = 8
RS_LOCAL_SEMS = 5


def _rs_piece_shape(part):
    arr, cols = part[0], part[1]
    return (arr.shape[0] // 2, arr.shape[1] // N_CHIPS) if cols else tuple(arr.shape[1:])


def _rs_operands(parts):
    return [p[0] for p in parts] + [p[0] if p[2] is None else p[2] for p in parts]


def _rs_wires(parts, wire):
    return list(wire) if isinstance(wire, (list, tuple)) else [wire] * len(parts)


def _rs_shapes(parts, wire):
    n = len(parts)
    shapes = [_rs_piece_shape(p) for p in parts]
    out_shape = [jax.ShapeDtypeStruct((2,) + s, F32) for s in shapes]
    scratch = []
    for lead, kind in ((N_CHIPS, "f32"), (N_CHIPS, "narrow"), (N_CHIPS, "wire"), (None, "f32"), (N_CHIPS, "wire")):
        for s, p, w in zip(shapes, parts, _rs_wires(parts, wire)):
            dtype = {"f32": F32, "narrow": F32 if p[2] is None else p[2].dtype, "wire": w}[kind]
            scratch.append(pltpu.VMEM(s if lead is None else (lead,) + s, dtype))
    scratch += [pltpu.SemaphoreType.DMA((n * RS_SEMS,)), pltpu.SemaphoreType.DMA((n * RS_SEMS,)),
                pltpu.SemaphoreType.DMA((n * RS_LOCAL_SEMS,))]
    return out_shape, scratch


def _rs_steps(parts, ins, outs, scratch):
    n = len(parts)
    own, sib, got, fin, snd = (scratch[k * n:(k + 1) * n] for k in range(5))
    send_sems, recv_sems, local_sems = scratch[5 * n:]
    shapes = [_rs_piece_shape(p) for p in parts]
    x, y, c = _place()
    j_me = 2 * x + y
    me, sibling = (x, y, c), (x, y, 1 - c)

    def piece(a, jj, core, narrow=False):
        ref = ins[n + a] if narrow else ins[a]
        if parts[a][1]:
            r, cl = shapes[a]
            return ref.at[pl.ds(core * r, r), pl.ds(jj * cl, cl)]
        return ref.at[2 * jj + core]

    def remote(a, sem, src, dst, to):
        return pltpu.make_async_remote_copy(
            src_ref=src, dst_ref=dst, send_sem=send_sems.at[a * RS_SEMS + sem],
            recv_sem=recv_sems.at[a * RS_SEMS + sem], device_id=to, device_id_type=MESH)

    def rows_loop(a, fn):
        r = shapes[a][0]
        step = max(s for s in RS_ADD_ROWS if r % s == 0)

        def it(i, carry):
            fn(pl.ds(pl.multiple_of(i * step, step), step))
            return carry

        lax.fori_loop(0, r // step, it, 0)

    def load(a, jj):
        return pltpu.make_async_copy(piece(a, jj, c), own[a].at[jj], local_sems.at[a * RS_LOCAL_SEMS + jj])

    def to_sibling(a, jj):
        return remote(a, jj, piece(a, jj, 1 - c, narrow=True), sib[a].at[jj], sibling)

    near = (jnp.bitwise_xor(x, 1 - c), jnp.bitwise_xor(y, c))
    far = (jnp.bitwise_xor(x, c), jnp.bitwise_xor(y, 1 - c))
    diag = (1 - x, 1 - y)
    FROM_NEAR, FROM_FAR, FEED = 0, 1, 2

    def chip_of(chip):
        return 2 * chip[0] + chip[1]

    def feed(a):
        return remote(a, 4, snd[a].at[chip_of(diag)], got[a].at[FEED], (*near, c))

    def to_near(a):
        return remote(a, 5, snd[a].at[chip_of(near)], got[a].at[FROM_NEAR], (*near, c))

    def to_far(a):
        return remote(a, 6, snd[a].at[chip_of(far)], got[a].at[FROM_FAR], (*far, c))

    def store(a):
        return pltpu.make_async_copy(fin[a], outs[a].at[c], local_sems.at[a * RS_LOCAL_SEMS + 4])

    def result_to_sibling(a):
        return remote(a, 7, fin[a], outs[a].at[c], sibling)

    order = [chip_of(diag), chip_of(near), chip_of(far), j_me]
    sibling_order = [chip_of(diag), chip_of(far), chip_of(near), j_me]

    def exchange():
        for k in range(N_CHIPS):
            for a in range(n):
                load(a, order[k]).start()
                to_sibling(a, sibling_order[k]).start()

    def pair_sum(k):
        jj = order[k]
        for a in range(n):
            load(a, jj).wait()
            remote(a, jj, sib[a].at[jj], sib[a].at[jj], me).wait_recv()

            def add(sl, a=a):
                q = own[a][jj, sl, :] + sib[a][jj, sl, :].astype(F32)
                own[a][jj, sl, :] = q
                snd[a][jj, sl, :] = q.astype(snd[a].dtype)

            rows_loop(a, add)

    def feed_sums():
        pair_sum(0)
        for a in range(n):
            feed(a).start()

    def chip_sums():
        pair_sum(1)
        for a in range(n):
            to_near(a).start()
        pair_sum(2)
        pair_sum(3)

    def relay():
        for a in range(n):
            remote(a, 4, got[a].at[FEED], got[a].at[FEED], me).wait_recv()

            def add(sl, a=a):
                pair = own[a][chip_of(far), sl, :] + got[a][FEED, sl, :].astype(F32)
                snd[a][chip_of(far), sl, :] = pair.astype(snd[a].dtype)

            rows_loop(a, add)
            to_far(a).start()

    def totals():
        for a in range(n):
            remote(a, 5, got[a].at[FROM_NEAR], got[a].at[FROM_NEAR], me).wait_recv()
            remote(a, 6, got[a].at[FROM_FAR], got[a].at[FROM_FAR], me).wait_recv()

            def total(sl, a=a):
                fin[a][sl, :] = (own[a][j_me, sl, :] + got[a][FROM_NEAR, sl, :].astype(F32)) + (
                    got[a][FROM_FAR, sl, :].astype(F32))

            rows_loop(a, total)
            store(a).start()
            result_to_sibling(a).start()

    def finish():
        for a in range(n):
            remote(a, 7, outs[a].at[1 - c], outs[a].at[1 - c], me).wait_recv()
        for a in range(n):
            for jj in range(N_CHIPS):
                to_sibling(a, jj).wait_send()
            for cp in (feed(a), to_near(a), to_far(a), result_to_sibling(a)):
                cp.wait_send()
            store(a).wait()

    return exchange, feed_sums, chip_sums, relay, totals, finish


def _rms(x):
    r = lax.rsqrt(jnp.mean(x * x, axis=-1, keepdims=True) + EPS)
    return x * r, r


def _rms_bwd(dxn, xn, r):
    return r * (dxn - xn * jnp.mean(dxn * xn, axis=-1, keepdims=True))


def _in_proj_gather(x2d, norm_g, w_in_sh, shards, tb, casts):
    t = x2d.shape[0]
    nb = t // tb
    cols = IN_COLS // N_CHIPS
    half = D_MODEL // 2
    n = len(shards)
    nc = len(casts)

    def body(x_ref, g_ref, win_ref, *refs):
        ins, cast_ins = refs[:n], refs[n:n + nc]
        z_ref, h_ref, wfull_ref = refs[n + nc:n + nc + 3]
        outs, cast_outs = refs[n + nc + 3:2 * n + nc + 3], refs[2 * n + nc + 3:2 * (n + nc) + 3]
        scratch = refs[2 * (n + nc) + 3:]
        wv, h_all, send_sems, recv_sems, local_sems, w_send, w_recv, w_local, stage = scratch[:9]
        wide, narrow, cast_sems = scratch[9:9 + nc], scratch[9 + nc:9 + 2 * nc], scratch[9 + 2 * nc]
        s, i = pl.program_id(0), pl.program_id(1)
        x, y, c = _place()
        me, sibling = (x, y, c), (x, y, 1 - c)
        chips = [(x, 1 - y), (1 - x, y), (1 - x, 1 - y)]

        def w_half(cx, cy, hc):
            return wv.at[2 * cx + cy, pl.ds(hc * half, half), :]

        def w_remote(sem, block, to, src=None):
            dst = w_half(*block)
            return pltpu.make_async_remote_copy(
                src_ref=dst if src is None else src, dst_ref=dst, send_sem=w_send.at[sem],
                recv_sem=w_recv.at[sem], device_id=to, device_id_type=MESH)

        def w_first(idx):
            return w_remote(idx, (x, y, c), (*chips[idx], c))

        def w_relay():
            src_chip = (jnp.bitwise_xor(x, 1 - c), jnp.bitwise_xor(y, c))
            dst_chip = (jnp.bitwise_xor(x, c), jnp.bitwise_xor(y, 1 - c))
            return w_remote(2, (*src_chip, c), (*dst_chip, c))

        def w_pass(idx):
            return w_remote(3 + idx, (*chips[idx], c), sibling)

        def w_store(k, cx, cy):
            jj = 2 * cx + cy
            return pltpu.make_async_copy(wv.at[jj], wfull_ref.at[:, pl.ds(jj * cols, cols)], w_local.at[k])

        start_rest, relay_rest, finish_rest = _gather_steps(shards, ins, outs, send_sems, recv_sems, local_sems)

        def own(k, hc):
            return pltpu.make_async_copy(win_ref.at[pl.ds(pl.multiple_of(hc * half, half), half), :], stage.at[k],
                                         w_local.at[4 + 2 * k])

        def round_own(k, hc):
            own(k, hc).wait()
            wv[2 * x + y, pl.ds(pl.multiple_of(hc * half, half), half), :] = stage[k].astype(BF16)

        wide_in = [pltpu.make_async_copy(cast_ins[k], wide[k], cast_sems.at[k]) for k in range(nc)]
        narrow_out = [pltpu.make_async_copy(narrow[k], cast_outs[k], cast_sems.at[nc + k]) for k in range(nc)]

        @pl.when((s == 0) & (i == 0))
        def _():
            own(0, c).start()
            own(1, 1 - c).start()
            for cp in wide_in:
                cp.start()
            round_own(0, c)
            w_first(0).start()
            w_first(1).start()
            start_rest()
            round_own(1, 1 - c)
            w_store(0, x, y).start()

        @pl.when((s == 1) & (i == 0))
        def _():
            for k in range(nc):
                wide_in[k].wait()
                narrow[k][...] = wide[k][...].astype(BF16)
                narrow_out[k].start()
            w_remote(0, (*chips[0], c), me).wait_recv()
            w_remote(1, (*chips[1], c), me).wait_recv()
            w_relay().start()
            w_pass(0).start()
            w_pass(1).start()
            w_remote(3, (*chips[0], 1 - c), me).wait_recv()
            w_store(1, *chips[0]).start()

        @pl.when((s == 2) & (i == 0))
        def _():
            w_remote(4, (*chips[1], 1 - c), me).wait_recv()
            w_store(2, *chips[1]).start()

        @pl.when((s == 3) & (i == 0))
        def _():
            w_remote(2, (*chips[2], c), me).wait_recv()
            w_pass(2).start()
            w_remote(5, (*chips[2], 1 - c), me).wait_recv()
            w_store(3, *chips[2]).start()

        keep_h = pltpu.make_async_copy(h_all.at[i], h_ref.at[pl.ds(pl.multiple_of(i * tb, tb), tb), :], w_local.at[5])

        @pl.when(s == 0)
        def _():
            xn, _ = _rms(x_ref[...])
            h_all[i] = (xn * g_ref[...]).astype(BF16)
            keep_h.start()

        z_ref[...] = _dot(h_all[i], wv[jnp.bitwise_xor(2 * x + y, s)])
        pl.when(s == 0)(keep_h.wait)

        @pl.when((s == N_CHIPS - 1) & (i == nb - 1))
        def _():
            relay_rest()
            finish_rest()
            for cp in (w_first(0), w_first(1), w_relay(), w_pass(0), w_pass(1), w_pass(2)):
                cp.wait_send()
            w_store(0, x, y).wait()
            for idx in range(3):
                w_store(idx + 1, *chips[idx]).wait()
            for cp in narrow_out:
                cp.wait()

    rest_shape, rest_sems = _gather_shapes(shards)
    out_shape = [jax.ShapeDtypeStruct((t, IN_COLS), F32), jax.ShapeDtypeStruct((t, D_MODEL), BF16),
                 jax.ShapeDtypeStruct((D_MODEL, IN_COLS), BF16)] + rest_shape
    out_shape += [jax.ShapeDtypeStruct(a.shape, BF16) for a in casts]
    any_spec = pl.BlockSpec(memory_space=pl.ANY)

    def z_map(s, i):
        return (i, jnp.bitwise_xor(2 * lax.axis_index("x") + lax.axis_index("y"), s))

    return pl.pallas_call(
        body, name="in_proj", out_shape=tuple(out_shape),
        grid=(N_CHIPS, nb),
        in_specs=[pl.BlockSpec((tb, D_MODEL), lambda s, i: (jnp.where(s == 0, i, nb - 1), 0)),
                  pl.BlockSpec((1, D_MODEL), lambda s, i: (0, 0)), any_spec] + [any_spec] * (n + nc),
        out_specs=tuple([pl.BlockSpec((tb, cols), z_map), any_spec, any_spec] + [any_spec] * (n + nc)),
        scratch_shapes=[pltpu.VMEM((N_CHIPS, D_MODEL, cols), BF16), pltpu.VMEM((nb, tb, D_MODEL), BF16)] + rest_sems + [
            pltpu.SemaphoreType.DMA((GATHER_SEMS,)), pltpu.SemaphoreType.DMA((GATHER_SEMS,)),
            pltpu.SemaphoreType.DMA((N_CHIPS + 3,)), pltpu.VMEM((2, half, cols), F32)]
        + [pltpu.VMEM(a.shape, F32) for a in casts] + [pltpu.VMEM(a.shape, BF16) for a in casts]
        + [pltpu.SemaphoreType.DMA((2 * nc,))],
        compiler_params=pltpu.CompilerParams(dimension_semantics=("arbitrary", "arbitrary"),
                                             vmem_limit_bytes=VMEM_LIMIT_BYTES),
    )(x2d, norm_g, w_in_sh, *[sh[0] for sh in shards], *casts)


def _in_proj_bwd(dz, w_in, x2d, dx_res, norm_g, tb, reduce, shards, take):
    t = x2d.shape[0]
    nb = t // tb
    parts, wire, steps = reduce
    n = len(parts)
    k = len(shards)
    take_rows, take_width = take

    def body(dz_ref, w_ref, x_ref, dres_ref, g_ref, *refs):
        at = 2 * n + k
        dx_ref, dg_ref = refs[at:at + 2]
        rs_outs, g_outs = refs[at + 2:at + 2 + n], refs[at + 2 + n:at + 2 + n + k]
        cut_ref = refs[at + 2 + n + k]
        scratch = refs[at + 3 + n + k:]
        rs_scr, g_sems, dg_acc, ar_scr, cut_sem = scratch[:-8], scratch[-8:-5], scratch[-5], scratch[-4:-1], scratch[-1]
        rs = _rs_steps(parts, refs[:2 * n], rs_outs, rs_scr)
        for step, when in zip(rs[:-1], steps):
            pl.when(pl.program_id(0) == when)(step)
        gather = _gather_steps(shards, refs[2 * n:at], g_outs, *g_sems)
        for step, when in zip(gather, (0, nb // 2, nb - 1)):
            pl.when(pl.program_id(0) == when)(step)

        @pl.when(pl.program_id(0) == 0)
        def _():
            dg_acc[...] = jnp.zeros_like(dg_acc)

        xn, r = _rms(x_ref[...])
        g = g_ref[...]
        dh = _dot_nt(dz_ref[...], w_ref[...])
        dg_acc[0:1, :] += jnp.sum(dh * xn, axis=0, keepdims=True)
        dx_ref[...] = dres_ref[...] + _rms_bwd(dh * g, xn, r)

        @pl.when(pl.program_id(0) == nb - 1)
        def _():
            x, y, _ = _place()
            mine = pl.ds(pl.multiple_of((2 * x + y) * take_width, take_width), take_width)
            cut = pltpu.make_async_copy(g_outs[0].at[take_rows, mine], cut_ref, cut_sem)
            cut.start()
            _all_reduce_tile(dg_acc, dg_ref, *ar_scr)
            rs[-1]()
            cut.wait()

    row = lambda i: (i, 0)
    fixed = lambda i: (0, 0)
    rs_shape, rs_scratch = _rs_shapes(parts, wire)
    g_shape, g_sems = _gather_shapes(shards)
    any_spec = pl.BlockSpec(memory_space=pl.ANY)
    cut_shape = jax.ShapeDtypeStruct((take_rows.stop - take_rows.start, take_width), F32)
    return pl.pallas_call(
        body, name="in_proj_bwd",
        out_shape=tuple([jax.ShapeDtypeStruct((t, D_MODEL), F32), jax.ShapeDtypeStruct((F32_SUBLANES, D_MODEL), F32)]
                        + rs_shape + g_shape + [cut_shape]),
        grid=(nb,),
        in_specs=[pl.BlockSpec((tb, IN_COLS), row),
                  pl.BlockSpec((D_MODEL, IN_COLS), fixed, pipeline_mode=pl.Buffered(1)),
                  pl.BlockSpec((tb, D_MODEL), row), pl.BlockSpec((tb, D_MODEL), row),
                  pl.BlockSpec((1, D_MODEL), fixed)] + [any_spec] * (2 * n + k),
        out_specs=tuple([pl.BlockSpec((tb, D_MODEL), row), pl.BlockSpec((F32_SUBLANES, D_MODEL), fixed)]
                        + [any_spec] * (n + k + 1)),
        scratch_shapes=rs_scratch + g_sems + [pltpu.VMEM((F32_SUBLANES, D_MODEL), F32)] + _all_reduce_scratch(
            (F32_SUBLANES, D_MODEL)) + [pltpu.SemaphoreType.DMA(())],
        compiler_params=pltpu.CompilerParams(dimension_semantics=("arbitrary",),
                                             vmem_limit_bytes=VMEM_LIMIT_BYTES),
    )(dz, w_in, x2d, dx_res, norm_g, *_rs_operands(parts), *[sh[0] for sh in shards])


def _weight_grad(lhs, rhs, n_chunks, tb, name, reduce=None):
    t, k = lhs.shape
    nc = rhs.shape[1] // n_chunks
    nb = t // tb
    parts, wire, steps = reduce if reduce is not None else ([], F32, ())
    n = len(parts)

    def body(l_ref, r_ref, *refs):
        o_ref, o16_ref = refs[2 * n:2 * n + 2]
        if n:
            at = pl.program_id(0) * nb + pl.program_id(1)
            rs = _rs_steps(parts, refs[:2 * n], refs[2 * n + 2:3 * n + 2], refs[3 * n + 2:])
            for step, when in zip(rs[:-1], steps):
                pl.when(at == when)(step)

        @pl.when(pl.program_id(1) == 0)
        def _():
            o_ref[...] = jnp.zeros_like(o_ref)

        o_ref[...] += _dot_tn(l_ref[...], r_ref[...])

        @pl.when(pl.program_id(1) == nb - 1)
        def _():
            o16_ref[...] = o_ref[...].astype(BF16)

        if n:
            pl.when(at == n_chunks * nb - 1)(rs[-1])

    rs_shape, rs_scratch = _rs_shapes(parts, wire) if n else ([], [])
    any_spec = pl.BlockSpec(memory_space=pl.ANY)
    chunk = pl.BlockSpec((None, k, nc), lambda j, i: (j, 0, 0))
    return pl.pallas_call(
        body, name=name,
        out_shape=tuple([jax.ShapeDtypeStruct((n_chunks, k, nc), F32), jax.ShapeDtypeStruct((n_chunks, k, nc), BF16)]
                        + rs_shape),
        grid=(n_chunks, nb),
        in_specs=[pl.BlockSpec((tb, k), lambda j, i: (i, 0)), pl.BlockSpec((tb, nc), lambda j, i: (i, j))]
        + [any_spec] * (2 * n),
        out_specs=tuple([chunk, chunk] + [any_spec] * n),
        scratch_shapes=rs_scratch,
        compiler_params=pltpu.CompilerParams(dimension_semantics=("arbitrary", "arbitrary"),
                                             vmem_limit_bytes=VMEM_LIMIT_BYTES),
    )(lhs, rhs, *_rs_operands(parts))


def _adam_update(w, g, m, v):
    m_ = ADAM_B1 * m + (1.0 - ADAM_B1) * g
    v_ = ADAM_B2 * v + (1.0 - ADAM_B2) * jnp.square(g)
    m_hat = m_ / (1.0 - ADAM_B1 ** ADAM_STEP)
    v_hat = v_ / (1.0 - ADAM_B2 ** ADAM_STEP)
    return -ADAM_LR * (m_hat / (jnp.sqrt(v_hat) + ADAM_EPS) + ADAM_WD * w), m_, v_


def _adamw_replicated(vec_sum, mat_sum, norm_grad, entries, conv):
    n = len(entries)

    def grad_of(name, shape, vec_ref, mat_ref, norm_ref):
        if name == "norm_g":
            return norm_ref[0:1, :]
        if name in MAT_BAG_AT:
            return mat_ref[MAT_BAG_AT[name]:MAT_BAG_AT[name] + shape[0], :]
        if shape[0] == 1:
            return vec_ref[_bag_row(name), 0:shape[1]]
        return jnp.concatenate([vec_ref[_bag_row(name), h * shape[1]:(h + 1) * shape[1]] for h in range(shape[0])],
                               axis=0)

    def body(vec_ref, mat_ref, norm_ref, *refs):
        ins, outs = refs[:3 * n + 4], refs[3 * n + 4:]
        for k in range(n):
            w_ref, m_ref, v_ref = ins[3 * k:3 * k + 3]
            g = grad_of(entries[k][0], w_ref.shape, vec_ref, mat_ref, norm_ref)
            d, m_, v_ = _adam_update(w_ref[...], g, m_ref[...], v_ref[...])
            for ref, val in zip(outs[4 * k:4 * k + 4], (g, d, m_, v_)):
                ref[...] = val
        w_ref, m_ref, v_ref, g_ref = ins[3 * n:]
        g = g_ref[0:w_ref.shape[0], :]
        for ref, val in zip(outs[4 * n:4 * n + 4], (g,) + _adam_update(w_ref[...], g, m_ref[...], v_ref[...])):
            ref[...] = val
        outs[4 * n + 4][...] = vec_ref[_bag_row("loss"), 0:1]

    arrays = [a for e in entries for a in e[1:]] + list(conv)
    out_shape = [jax.ShapeDtypeStruct(e[1].shape, F32) for e in entries for _ in range(4)]
    out_shape += [jax.ShapeDtypeStruct(conv[0].shape, F32)] * 4 + [jax.ShapeDtypeStruct((1, 1), F32)]
    return pl.pallas_call(
        body, name="adamw_replicated", out_shape=tuple(out_shape),
        compiler_params=pltpu.CompilerParams(vmem_limit_bytes=VMEM_LIMIT_BYTES),
    )(vec_sum, mat_sum, norm_grad, *arrays)


def _adamw(w, g, m, v, rows, name):
    r, c = w.shape

    def body(w_ref, g_ref, m_ref, v_ref, go_ref, d_ref, nm_ref, nv_ref):
        g = g_ref[...]
        go_ref[...] = g
        d_ref[...], nm_ref[...], nv_ref[...] = _adam_update(w_ref[...], g, m_ref[...], v_ref[...])

    spec = pl.BlockSpec((rows, c), lambda i: (i, 0))
    return pl.pallas_call(
        body, name=name, out_shape=tuple(jax.ShapeDtypeStruct((r, c), F32) for _ in range(4)),
        grid=(r // rows,), in_specs=[spec] * 4, out_specs=(spec,) * 4,
        compiler_params=pltpu.CompilerParams(dimension_semantics=("arbitrary",),
                                             vmem_limit_bytes=VMEM_LIMIT_BYTES),
    )(w, g, m, v)


def _adamw_group(items, name):
    n = 4 * len(items)

    def body(*refs):
        ins, outs, bufs = refs[:n], refs[n:2 * n], refs[2 * n:3 * n]
        load_sems, store_sems = refs[3 * n:]
        loads = [pltpu.make_async_copy(ins[j], bufs[j], load_sems.at[j]) for j in range(n)]
        stores = [pltpu.make_async_copy(bufs[j], outs[j], store_sems.at[j]) for j in range(n)]
        for cp in loads:
            cp.start()
        for k in range(len(items)):
            for cp in loads[4 * k:4 * k + 4]:
                cp.wait()
            w_buf, g_buf, m_buf, v_buf = bufs[4 * k:4 * k + 4]
            w_buf[...], m_buf[...], v_buf[...] = _adam_update(w_buf[...], g_buf[...], m_buf[...], v_buf[...])
            for cp in stores[4 * k:4 * k + 4]:
                cp.start()
        for cp in stores:
            cp.wait()

    arrays = [a for item in items for a in item]
    any_spec = pl.BlockSpec(memory_space=pl.ANY)
    flat = pl.pallas_call(
        body, name=name, out_shape=tuple(jax.ShapeDtypeStruct(a.shape, F32) for a in arrays),
        in_specs=[any_spec] * n, out_specs=(any_spec,) * n,
        scratch_shapes=[pltpu.VMEM(a.shape, F32) for a in arrays] + [pltpu.SemaphoreType.DMA((n,))] * 2,
        compiler_params=pltpu.CompilerParams(vmem_limit_bytes=VMEM_LIMIT_BYTES),
    )(*arrays)
    return [(flat[4 * k + 1], flat[4 * k], flat[4 * k + 2], flat[4 * k + 3]) for k in range(len(items))]


def _shift_down(ext, s):
    return pltpu.roll(ext, s, 0)


def _tile_shift(v, s):
    rows, cols = v.shape
    tiles = v.reshape(rows // F32_SUBLANES, F32_SUBLANES, cols)
    return pltpu.roll(tiles, s % F32_SUBLANES, 1).reshape(rows, cols)


def _shift_up(ext, s):
    return pltpu.roll(ext, ext.shape[0] - s, 0)


def _lru_gates(xc, wa_ref, ba, wx_ref, bx, lam):
    pa, px = [], []
    for h in range(LRU_HEADS):
        xh = xc[:, h * HEAD_DIM:(h + 1) * HEAD_DIM].astype(BF16)
        pa.append(_dot(xh, wa_ref[h]))
        px.append(_dot(xh, wx_ref[h]))
    r = _sigmoid(jnp.concatenate(pa, axis=1) + ba)
    ig = _sigmoid(jnp.concatenate(px, axis=1) + bx)
    sp = _softplus(-lam)
    log_a = (-LRU_C * r) * sp
    a = jnp.exp(log_a)
    mult = jnp.sqrt(jnp.tanh(-log_a) * (1.0 + a * a))
    return r, ig, a, mult, sp


def _conv(ext, w_ref, b):
    y = b + _shift_down(ext, 3) * w_ref[0:1, :]
    y = y + _shift_down(ext, 2) * w_ref[1:2, :]
    y = y + _shift_down(ext, 1) * w_ref[2:3, :]
    y = y + ext * w_ref[3:4, :]
    return y[CONV_HIST:, :]


def _pool_diff(ext, pos):
    out = []
    for g, k in enumerate(POOL_WINDOWS):
        col = ext[:, g * POOL_GROUP_DIM:(g + 1) * POOL_GROUP_DIM]
        s = col
        for step in range(g + 1):
            s = s + _shift_down(s, 2 ** step)
        count = jnp.minimum(pos + 1, k).astype(F32)
        out.append(s[POOL_HIST:, :] / count - col[POOL_HIST:, :])
    return out


def _pool_mix(diff, pw_ref):
    return jnp.concatenate([_dot(diff[g].astype(BF16), pw_ref[g]) for g in range(len(POOL_WINDOWS))], axis=1)


def _branch_specs(tb, row_map, fixed):
    fixed3 = lambda i: (0, 0, 0)
    return [pl.BlockSpec((CONV_WIDTH, D_MODEL), fixed), pl.BlockSpec((1, D_MODEL), fixed),
            pl.BlockSpec((LRU_HEADS, HEAD_DIM, HEAD_DIM), fixed3), pl.BlockSpec((1, D_MODEL), fixed),
            pl.BlockSpec((LRU_HEADS, HEAD_DIM, HEAD_DIM), fixed3), pl.BlockSpec((1, D_MODEL), fixed),
            pl.BlockSpec((1, D_MODEL), fixed),
            pl.BlockSpec((len(POOL_WINDOWS), POOL_GROUP_DIM, POOL_GROUP_DIM), fixed3),
            pl.BlockSpec((1, POOL_WIDTH), fixed)]


def _branches_fwd(z, weights, seq, tb, shards):
    t = z.shape[0]
    nb = t // tb
    nbe = seq // tb
    groups = tb // F32_SUBLANES
    n = len(shards)

    def body(xa_ref, ga_ref, xb_ref, gb_ref, cw_ref, cb_ref, wa_ref, ba_ref, wx_ref, bx_ref, lam_ref,
             pw_ref, ps_ref, *refs):
        g_ins = refs[:n]
        ya_ref, yb_ref, hl_ref = refs[n:n + 3]
        g_outs = refs[n + 3:2 * n + 3]
        xa_ext, xb_ext, carry, a_s, u_s, send_sems, recv_sems, local_sems = refs[2 * n + 3:]
        blk = pl.program_id(0) % nbe
        start_gather, relay_gather, finish_gather = _gather_steps(shards, g_ins, g_outs, send_sems, recv_sems,
                                                                  local_sems)
        pl.when(pl.program_id(0) == 0)(start_gather)
        pl.when(pl.program_id(0) == nb // 2)(relay_gather)

        @pl.when(blk == 0)
        def _():
            xa_ext[0:CONV_HIST, :] = jnp.zeros((CONV_HIST, D_MODEL), F32)
            xb_ext[0:POOL_HIST, :] = jnp.zeros((POOL_HIST, POOL_WIDTH), F32)
            carry[...] = jnp.zeros_like(carry)

        xa_ext[CONV_HIST:, :] = xa_ref[...]
        xb_ext[POOL_HIST:, :] = xb_ref[...]
        ea = xa_ext[...]
        eb = xb_ext[...]
        xa_ext[0:CONV_HIST, :] = ea[tb:, :]
        xb_ext[0:POOL_HIST, :] = eb[tb:, :]

        xc = _conv(ea, cw_ref, cb_ref[...])
        _, ig, a, mult, _ = _lru_gates(xc, wa_ref, ba_ref[...], wx_ref, bx_ref[...], lam_ref[...])
        u = mult * (ig * xc)
        row8 = lax.broadcasted_iota(jnp.int32, (tb, D_MODEL), 0) % F32_SUBLANES
        for s in (1, 2, 4):
            m = row8 >= s
            u = jnp.where(m, a * _tile_shift(u, s) + u, u)
            a = jnp.where(m, a * _tile_shift(a, s), a)
        a_s[...] = a
        u_s[...] = u

        def step(g, cr):
            sl = pl.ds(pl.multiple_of(g * F32_SUBLANES, F32_SUBLANES), F32_SUBLANES)
            hb = a_s[sl, :] * cr + u_s[sl, :]
            hl_ref[sl, :] = hb
            return jnp.broadcast_to(hb[F32_SUBLANES - 1:F32_SUBLANES, :], (F32_SUBLANES, D_MODEL))

        carry[...] = lax.fori_loop(0, groups, step, carry[...], unroll=4)
        ga = ga_ref[...]
        ya_ref[...] = (hl_ref[...] * (ga * _sigmoid(ga))).astype(BF16)

        pos = blk * tb + lax.broadcasted_iota(jnp.int32, (tb, POOL_GROUP_DIM), 0)
        ypre = _pool_mix(_pool_diff(eb, pos), pw_ref)
        gb = gb_ref[...]
        yb_ref[...] = ((ypre * ps_ref[...]) * (gb * _sigmoid(gb))).astype(BF16)
        pl.when(pl.program_id(0) == nb - 1)(finish_gather)

    row = lambda i: (i, 0)
    fixed = lambda i: (0, 0)
    any_spec = pl.BlockSpec(memory_space=pl.ANY)
    in_specs = [pl.BlockSpec((tb, D_MODEL), lambda i: (i, 0)), pl.BlockSpec((tb, D_MODEL), lambda i: (i, 1)),
                pl.BlockSpec((tb, POOL_WIDTH), lambda i: (i, 4)), pl.BlockSpec((tb, POOL_WIDTH), lambda i: (i, 5)),
                ] + _branch_specs(tb, row, fixed) + [any_spec] * n
    g_shape, g_sems = _gather_shapes(shards)
    return pl.pallas_call(
        body, name="branches_fwd",
        out_shape=tuple([jax.ShapeDtypeStruct((t, D_MODEL), BF16), jax.ShapeDtypeStruct((t, POOL_WIDTH), BF16),
                         jax.ShapeDtypeStruct((t, D_MODEL), F32)] + g_shape),
        grid=(nb,), in_specs=in_specs,
        out_specs=tuple([pl.BlockSpec((tb, D_MODEL), row), pl.BlockSpec((tb, POOL_WIDTH), row),
                         pl.BlockSpec((tb, D_MODEL), row)] + [any_spec] * n),
        scratch_shapes=[pltpu.VMEM((tb + CONV_HIST, D_MODEL), F32), pltpu.VMEM((tb + POOL_HIST, POOL_WIDTH), F32),
                        pltpu.VMEM((F32_SUBLANES, D_MODEL), F32),
                        pltpu.VMEM((tb, D_MODEL), F32), pltpu.VMEM((tb, D_MODEL), F32)] + g_sems,
        compiler_params=pltpu.CompilerParams(dimension_semantics=("arbitrary",),
                                             vmem_limit_bytes=VMEM_LIMIT_BYTES),
    )(z, z, z, z, *weights, *[sh[0] for sh in shards])


def _branches_bwd(z, hl, dya, dyb, dzm, weights, vec_bag, seq, tb, riders):
    t = z.shape[0]
    nb = t // tb
    nbe = seq // tb
    groups = tb // F32_SUBLANES
    nr = len(riders)

    def body(xa_ref, xap_ref, ga_ref, xb_ref, xbp_ref, gb_ref, hl_ref, hlp_ref, dya_ref, dyb_ref, dzm_ref,
             cw_ref, cb_ref, wa_ref, ba_ref, wx_ref, bx_ref, lam_ref, pw_ref, ps_ref, vec_in_ref, *rest):
        pairs, (dz_ref, vec_ref, mat_ref), grads = rest[:2 * nr], rest[2 * nr:2 * nr + 3], rest[2 * nr + 3:4 * nr + 3]
        xa_ext, xb_ext, hl_ext, a_ext, dxc_ext, dwin_ext, g_carry, b_s, d_s, g_s = rest[4 * nr + 3:]
        i = pl.program_id(0)
        blk = (nb - 1 - i) % nbe

        def mat_rows(name, k):
            at = MAT_BAG_AT[name] + k * HEAD_DIM
            return slice(at, at + HEAD_DIM)

        def rider(k):
            grads[2 * k][...] += _dot_tn(pairs[2 * k][...], pairs[2 * k + 1][...])

        @pl.when(i == 0)
        def _():
            vec_ref[...] = vec_in_ref[...]
            mat_ref[...] = jnp.zeros_like(mat_ref)
            for k in range(nr):
                grads[2 * k][...] = jnp.zeros_like(grads[2 * k])

        @pl.when(blk == nbe - 1)
        def _():
            a_ext[tb:, :] = jnp.zeros((F32_SUBLANES, D_MODEL), F32)
            dxc_ext[tb:, :] = jnp.zeros((CONV_HIST, D_MODEL), F32)
            dwin_ext[tb:, :] = jnp.zeros((POOL_HIST, POOL_WIDTH), F32)
            g_carry[...] = jnp.zeros_like(g_carry)

        live = (blk > 0).astype(F32)
        xa_ext[0:CONV_HIST, :] = xap_ref[...] * live
        xa_ext[CONV_HIST:, :] = xa_ref[...]
        xb_ext[0:POOL_HIST, :] = xbp_ref[...] * live
        xb_ext[POOL_HIST:, :] = xb_ref[...]
        hl_ext[0:F32_SUBLANES, :] = hlp_ref[...] * live
        hl_ext[F32_SUBLANES:, :] = hl_ref[...]
        ea = xa_ext[...]
        eb = xb_ext[...]
        rider(0)

        xc = _conv(ea, cw_ref, cb_ref[...])
        lam = lam_ref[...]
        r, ig, a, mult, sp = _lru_gates(xc, wa_ref, ba_ref[...], wx_ref, bx_ref[...], lam)
        hl = hl_ref[...]
        ga = ga_ref[...]
        sga = _sigmoid(ga)
        dya = dya_ref[...]
        dhl = dya * (ga * sga)
        dz_ref[:, D_MODEL:2 * D_MODEL] = (dya * hl * (sga * (1.0 + ga * (1.0 - sga)))).astype(BF16)

        a_ext[0:tb, :] = a
        b = _shift_up(a_ext[...], 1)[0:tb, :]
        a_ext[tb:, :] = jnp.broadcast_to(a[0:1, :], (F32_SUBLANES, D_MODEL))
        d = dhl
        row8 = lax.broadcasted_iota(jnp.int32, (tb, D_MODEL), 0) % F32_SUBLANES
        for s in (1, 2, 4):
            m = row8 < F32_SUBLANES - s
            d = jnp.where(m, d + b * _tile_shift(d, -s), d)
            b = jnp.where(m, b * _tile_shift(b, -s), b)
        b_s[...] = b
        d_s[...] = d

        def step(k, cr):
            sl = pl.ds(pl.multiple_of((groups - 1 - k) * F32_SUBLANES, F32_SUBLANES), F32_SUBLANES)
            gb_ = d_s[sl, :] + b_s[sl, :] * cr
            g_s[sl, :] = gb_
            return jnp.broadcast_to(gb_[0:1, :], (F32_SUBLANES, D_MODEL))

        g_carry[...] = lax.fori_loop(0, groups, step, g_carry[...], unroll=4)
        rider(1)
        gsc = g_s[...]
        da = gsc * _shift_down(hl_ext[...], 1)[F32_SUBLANES:, :]
        dmult = gsc * (ig * xc)
        dig = gsc * (mult * xc)
        dxc = gsc * (mult * ig)
        dlog_a = da * a - (a * a) * dmult / mult
        dr = dlog_a * (-LRU_C * sp)
        vec_ref[_bag_row("lru_lambda"), :] += jnp.sum(dlog_a * (-LRU_C * r), axis=0, keepdims=True)
        dpa = dr * (r * (1.0 - r))
        dpx = dig * (ig * (1.0 - ig))
        vec_ref[_bag_row("lru_b_a"), :] += jnp.sum(dpa, axis=0, keepdims=True)
        vec_ref[_bag_row("lru_b_x"), :] += jnp.sum(dpx, axis=0, keepdims=True)
        back = []
        for h in range(LRU_HEADS):
            cols = slice(h * HEAD_DIM, (h + 1) * HEAD_DIM)
            xh = xc[:, cols].astype(BF16)
            dpa_h = dpa[:, cols].astype(BF16)
            dpx_h = dpx[:, cols].astype(BF16)
            mat_ref[mat_rows("lru_w_a", h), :] += _dot_tn(xh, dpa_h)
            mat_ref[mat_rows("lru_w_x", h), :] += _dot_tn(xh, dpx_h)
            back.append(_dot_nt(dpa_h, wa_ref[h]) + _dot_nt(dpx_h, wx_ref[h]))
        dxc = dxc + jnp.concatenate(back, axis=1)
        vec_ref[_bag_row("conv_b"), :] += jnp.sum(dxc, axis=0, keepdims=True)
        for k in range(CONV_WIDTH):
            tap = _shift_down(ea, CONV_WIDTH - 1 - k)[CONV_HIST:, :] if k < CONV_WIDTH - 1 else ea[CONV_HIST:, :]
            vec_ref[_bag_row("conv_w", k), :] += jnp.sum(dxc * tap, axis=0, keepdims=True)
        dxc_ext[0:tb, :] = dxc
        ed = dxc_ext[...]
        dxa = ed * cw_ref[3:4, :]
        dxa = dxa + _shift_up(ed, 1) * cw_ref[2:3, :]
        dxa = dxa + _shift_up(ed, 2) * cw_ref[1:2, :]
        dxa = dxa + _shift_up(ed, 3) * cw_ref[0:1, :]
        dz_ref[:, 0:D_MODEL] = dxa[0:tb, :].astype(BF16)
        dxc_ext[tb:, :] = dxc[0:CONV_HIST, :]

        pos = blk * tb + lax.broadcasted_iota(jnp.int32, (tb, POOL_GROUP_DIM), 0)
        diff = _pool_diff(eb, pos)
        rider(2)
        ypre = _pool_mix(diff, pw_ref)
        ps = ps_ref[...]
        gb = gb_ref[...]
        sgb = _sigmoid(gb)
        dyb = dyb_ref[...]
        dyp = dyb * (gb * sgb)
        dz_ref[:, 2 * D_MODEL + POOL_WIDTH:3 * D_MODEL] = (
            dyb * (ypre * ps) * (sgb * (1.0 + gb * (1.0 - sgb)))).astype(BF16)
        vec_ref[_bag_row("pool_scale"), 0:POOL_WIDTH] += jnp.sum(dyp * ypre, axis=0, keepdims=True)
        dypre = dyp * ps
        for g, k in enumerate(POOL_WINDOWS):
            cols = slice(g * POOL_GROUP_DIM, (g + 1) * POOL_GROUP_DIM)
            dyg = dypre[:, cols].astype(BF16)
            mat_ref[mat_rows("pool_w", g), :] += _dot_tn(diff[g].astype(BF16), dyg)
            ddiff = _dot_nt(dyg, pw_ref[g])
            count = jnp.minimum(pos + 1, k).astype(F32)
            dwin = ddiff / count
            dwin_ext[0:tb, cols] = dwin
            s = dwin_ext[:, cols]
            for step_ in range(g + 1):
                s = s + _shift_up(s, 2 ** step_)
            dz_ref[:, 2 * D_MODEL + g * POOL_GROUP_DIM:2 * D_MODEL + (g + 1) * POOL_GROUP_DIM] = (
                s[0:tb, :] - ddiff).astype(BF16)
            dwin_ext[tb:, cols] = dwin[0:POOL_HIST, :]

        dz_ref[:, 3 * D_MODEL:] = dzm_ref[...]

        @pl.when(i == nb - 1)
        def _():
            row = _bag_row("lru_lambda")
            vec_ref[row, :] = vec_ref[row, :] * (-_sigmoid(-lam))
            for k in range(nr):
                grads[2 * k + 1][...] = grads[2 * k][...].astype(BF16)

    rev = lambda i: (nb - 1 - i, 0)
    fixed = lambda i: (0, 0)

    def prev(rows, col):
        per = tb // rows
        return lambda i: (jnp.maximum((nb - 1 - i) * per - 1, 0), col)

    in_specs = [pl.BlockSpec((tb, D_MODEL), lambda i: (nb - 1 - i, 0)),
                pl.BlockSpec((CONV_HIST, D_MODEL), prev(CONV_HIST, 0)),
                pl.BlockSpec((tb, D_MODEL), lambda i: (nb - 1 - i, 1)),
                pl.BlockSpec((tb, POOL_WIDTH), lambda i: (nb - 1 - i, 4)),
                pl.BlockSpec((POOL_HIST, POOL_WIDTH), prev(POOL_HIST, 4)),
                pl.BlockSpec((tb, POOL_WIDTH), lambda i: (nb - 1 - i, 5)),
                pl.BlockSpec((tb, D_MODEL), rev),
                pl.BlockSpec((F32_SUBLANES, D_MODEL), prev(F32_SUBLANES, 0)),
                pl.BlockSpec((tb, D_MODEL), rev), pl.BlockSpec((tb, POOL_WIDTH), rev),
                pl.BlockSpec((tb, 2 * D_MODEL), rev)] + _branch_specs(tb, rev, fixed) + [
                    pl.BlockSpec((VEC_BAG_ROWS, D_MODEL), fixed)]
    vec_at = len(in_specs) - 1
    out_shape = [jax.ShapeDtypeStruct((t, IN_COLS), BF16), jax.ShapeDtypeStruct((VEC_BAG_ROWS, D_MODEL), F32),
                 jax.ShapeDtypeStruct((MAT_BAG_ROWS, HEAD_DIM), F32)]
    out_specs = [pl.BlockSpec((tb, IN_COLS), rev), pl.BlockSpec((VEC_BAG_ROWS, D_MODEL), fixed),
                 pl.BlockSpec((MAT_BAG_ROWS, HEAD_DIM), fixed)]
    for lhs, rhs in riders:
        in_specs += [pl.BlockSpec((tb, lhs.shape[1]), rev), pl.BlockSpec((tb, rhs.shape[1]), rev)]
        grad = (lhs.shape[1], rhs.shape[1])
        out_shape += [jax.ShapeDtypeStruct(grad, F32), jax.ShapeDtypeStruct(grad, BF16)]
        out_specs += [pl.BlockSpec(grad, fixed)] * 2
    scratch = [pltpu.VMEM((tb + CONV_HIST, D_MODEL), F32), pltpu.VMEM((tb + POOL_HIST, POOL_WIDTH), F32),
               pltpu.VMEM((tb + F32_SUBLANES, D_MODEL), F32), pltpu.VMEM((tb + F32_SUBLANES, D_MODEL), F32),
               pltpu.VMEM((tb + CONV_HIST, D_MODEL), F32), pltpu.VMEM((tb + POOL_HIST, POOL_WIDTH), F32),
               pltpu.VMEM((F32_SUBLANES, D_MODEL), F32),
               pltpu.VMEM((tb, D_MODEL), F32), pltpu.VMEM((tb, D_MODEL), F32), pltpu.VMEM((tb, D_MODEL), F32)]
    return pl.pallas_call(
        body, name="branches_bwd", out_shape=tuple(out_shape), grid=(nb,), in_specs=in_specs,
        out_specs=tuple(out_specs), scratch_shapes=scratch, input_output_aliases={vec_at: 1},
        compiler_params=pltpu.CompilerParams(dimension_semantics=("arbitrary",),
                                             vmem_limit_bytes=VMEM_LIMIT_BYTES),
    )(z, z, z, z, z, z, hl, hl, dya, dyb, dzm, *weights, vec_bag, *[a for pair in riders for a in pair])


def _merge_head(x2d, ya, yb, z, p2d, tgt, w_pl, w_pp, w_out, w_pg, w_pe, g2, gf, tb):
    t = x2d.shape[0]
    p_dim = p2d.shape[1]

    def body(x_ref, ya_ref, yb_ref, ma_ref, mb_ref, p_ref, t_ref, wpl_ref, wpp_ref, wout_ref, wpg_ref, wpe_ref,
             g2_ref, gf_ref,
             bag_ref, dxr_ref, dya_ref, dyb_ref, dzm_ref,
             mg_ref, do_ref, hn_ref, dgp_ref, dpe_ref, da_ref, dbm_ref, pbf_ref):
        @pl.when(pl.program_id(0) == 0)
        def _():
            bag_ref[...] = jnp.zeros_like(bag_ref)

        a_ = _dot(ya_ref[...], wpl_ref[...])
        bm = _dot(yb_ref[...], wpp_ref[...])
        sa = _sigmoid(ma_ref[...])
        sb = _sigmoid(mb_ref[...])
        mg = (sa * a_ + sb * bm).astype(BF16)
        mg_ref[...] = mg
        x1 = x_ref[...] + _dot(mg, wout_ref[...])
        xn2, r2 = _rms(x1)
        g2 = g2_ref[...]
        hn = (xn2 * g2).astype(BF16)
        hn_ref[...] = hn
        gate = _sigmoid(_dot(hn, wpg_ref[...]))
        pbf = p_ref[...].astype(BF16)
        pbf_ref[...] = pbf
        pe = _dot(pbf, wpe_ref[...])
        x2 = x1 + gate * pe
        xn3, r3 = _rms(x2)
        gf = gf_ref[...]
        err = xn3 * gf - t_ref[...]
        bag_ref[_bag_rows("loss"), 0:128] += 0.5 * jnp.sum(jnp.mean(err * err, axis=-1))

        dy = err * (1.0 / D_MODEL)
        bag_ref[_bag_row("final_g"), :] += jnp.sum(dy * xn3, axis=0, keepdims=True)
        dx2 = _rms_bwd(dy * gf, xn3, r3)
        dpe_ref[...] = (dx2 * gate).astype(BF16)
        dgp = ((dx2 * pe) * (gate * (1.0 - gate))).astype(BF16)
        dgp_ref[...] = dgp
        dhn = _dot_nt(dgp, wpg_ref[...])
        bag_ref[_bag_row("ple_norm_g"), :] += jnp.sum(dhn * xn2, axis=0, keepdims=True)
        dx1 = dx2 + _rms_bwd(dhn * g2, xn2, r2)
        dxr_ref[...] = dx1
        do = dx1.astype(BF16)
        do_ref[...] = do
        dmg = _dot_nt(do, wout_ref[...])
        da = (dmg * sa).astype(BF16)
        dbm = (dmg * sb).astype(BF16)
        da_ref[...] = da
        dbm_ref[...] = dbm
        dzm_ref[:, 0:D_MODEL] = (dmg * a_ * (sa * (1.0 - sa))).astype(BF16)
        dzm_ref[:, D_MODEL:] = (dmg * bm * (sb * (1.0 - sb))).astype(BF16)
        dya_ref[...] = _dot_nt(da, wpl_ref[...])
        dyb_ref[...] = _dot_nt(dbm, wpp_ref[...])

    row = lambda i: (i, 0)
    fixed = lambda i: (0, 0)

    def resident(shape):
        return pl.BlockSpec(shape, fixed, pipeline_mode=pl.Buffered(1))

    tok = lambda width: pl.BlockSpec((tb, width), row)
    in_specs = [tok(D_MODEL), tok(D_MODEL), tok(POOL_WIDTH),
                pl.BlockSpec((tb, D_MODEL), lambda i: (i, 3)), pl.BlockSpec((tb, D_MODEL), lambda i: (i, 4)),
                tok(p_dim), tok(D_MODEL),
                resident((D_MODEL, D_MODEL)), resident((POOL_WIDTH, D_MODEL)), resident((D_MODEL, D_MODEL)),
                resident((D_MODEL, D_MODEL)), resident((p_dim, D_MODEL)),
                pl.BlockSpec((1, D_MODEL), fixed), pl.BlockSpec((1, D_MODEL), fixed)]
    bf = lambda width: jax.ShapeDtypeStruct((t, width), BF16)
    f32 = lambda width: jax.ShapeDtypeStruct((t, width), F32)
    out_shape = (jax.ShapeDtypeStruct((VEC_BAG_ROWS, D_MODEL), F32),
                 f32(D_MODEL), f32(D_MODEL), f32(POOL_WIDTH), bf(2 * D_MODEL),
                 bf(D_MODEL), bf(D_MODEL), bf(D_MODEL), bf(D_MODEL), bf(D_MODEL), bf(D_MODEL), bf(D_MODEL), bf(p_dim))
    out_specs = (pl.BlockSpec((VEC_BAG_ROWS, D_MODEL), fixed),
                 tok(D_MODEL), tok(D_MODEL), tok(POOL_WIDTH), tok(2 * D_MODEL),
                 tok(D_MODEL), tok(D_MODEL), tok(D_MODEL), tok(D_MODEL), tok(D_MODEL), tok(D_MODEL), tok(D_MODEL),
                 tok(p_dim))
    return pl.pallas_call(
        body, name="merge_head", out_shape=out_shape, grid=(t // tb,), in_specs=in_specs, out_specs=out_specs,
        compiler_params=pltpu.CompilerParams(dimension_semantics=("arbitrary",),
                                             vmem_limit_bytes=VMEM_LIMIT_BYTES),
    )(x2d, ya, yb, z, z, p2d, tgt, w_pl, w_pp, w_out, w_pg, w_pe, g2, gf)


def kernel(x, p, norm_g, w_in, conv_w, conv_b, lru_w_a, lru_b_a, lru_w_x, lru_b_x, lru_lambda, pool_w, pool_scale, w_proj_lru, w_proj_pool, w_out, ple_norm_g, w_ple_gate, w_ple_proj, final_g, loss_target, m_norm_g, m_w_in, m_conv_w, m_conv_b, m_lru_w_a, m_lru_b_a, m_lru_w_x, m_lru_b_x, m_lru_lambda, m_pool_w, m_pool_scale, m_w_proj_lru, m_w_proj_pool, m_w_out, m_ple_norm_g, m_w_ple_gate, m_w_ple_proj, m_final_g, v_norm_g, v_w_in, v_conv_w, v_conv_b, v_lru_w_a, v_lru_b_a, v_lru_w_x, v_lru_b_x, v_lru_lambda, v_pool_w, v_pool_scale, v_w_proj_lru, v_w_proj_pool, v_w_out, v_ple_norm_g, v_w_ple_gate, v_w_ple_proj, v_final_g):
    bsz, seq, _ = x.shape
    t = bsz * seq
    tb_mm = min(1024, seq)
    tb_seq = min(256, seq // 2) if seq >= 512 else seq
    x2d = x.reshape(t, D_MODEL)
    p2d = p.reshape(t, p.shape[-1])
    tgt = loss_target.reshape(t, D_MODEL)

    rest = [(w_proj_lru[0], 0), (w_proj_pool[0], 1), (w_out[0], 0), (w_ple_gate[0], 0), (w_ple_proj[0], 1)]
    z, h_bf, w_in_f, conv_w_f, *narrow = _in_proj_gather(
        x2d, norm_g, w_in[0], [(conv_w[0], 1, False)], tb_mm,
        [w for w, _ in rest] + [lru_w_a[0], lru_w_x[0], pool_w[0]])
    wa_bf, wx_bf, pw_bf = narrow[len(rest):]
    branch_w = (conv_w_f, conv_b, wa_bf, lru_b_a.reshape(1, D_MODEL), wx_bf, lru_b_x.reshape(1, D_MODEL),
                lru_lambda, pw_bf, pool_scale)

    ya, yb, hl, w_pl_f, w_pp_f, w_out_f, w_pg_f, w_pe_f = _branches_fwd(
        z, branch_w, seq, tb_seq, [(w16, axis, True) for w16, (_, axis) in zip(narrow, rest)])
    (vec_bag, dx_res, dya, dyb, dzm, mg_bf, do_bf, hn_bf, dgp_bf, dpe_bf, da_bf, dbm_bf, p_bf) = _merge_head(
        x2d, ya, yb, z, p2d, tgt, w_pl_f, w_pp_f, w_out_f, w_pg_f, w_pe_f, ple_norm_g, final_g.reshape(1, D_MODEL),
        tb_seq)
    dz, vec_bag, mat_bag, g_out, g_out16, g_pp, g_pp16, g_pe, g_pe16 = _branches_bwd(
        z, hl, dya, dyb, dzm, branch_w, vec_bag, seq, tb_seq, [(mg_bf, do_bf), (yb, dbm_bf), (p_bf, dpe_bf)])

    tb_dw = min(1024, seq)
    def row_pieces(g32, g16):
        pieces = (8, g32.shape[0] // 8, g32.shape[1])
        return g32.reshape(pieces), False, g16.reshape(pieces)

    def proj_grad(lhs, rhs, name):
        g32, g16 = _weight_grad(lhs, rhs, 1, tb_dw, name)
        return row_pieces(g32[0], g16[0])

    p_dim = p2d.shape[1]
    proj_parts = [proj_grad(ya, da_bf, "dw_proj_lru"), (g_pp, True, g_pp16), row_pieces(g_out, g_out16),
                  proj_grad(hn_bf, dgp_bf, "dw_ple_gate"), (g_pe, True, g_pe16)]
    nb_dw = t // tb_dw
    g_in, g_in16, r_pl, r_pp, r_out, r_pg, r_pe, vec_mine, mat_mine = _weight_grad(
        h_bf, dz, N_CHIPS, tb_dw, "dw_in",
        reduce=(proj_parts + [(vec_bag.reshape(8, VEC_BAG_ROWS // 8, D_MODEL), False, None),
                              (mat_bag.reshape(8, MAT_BAG_ROWS // 8, HEAD_DIM), False, None)],
                [BF16] * 5 + [F32] * 2,
                (0, 1, 2, 2 * nb_dw - 1, N_CHIPS * nb_dw - 1)))
    pieces = (8, D_MODEL // 2, IN_COLS // N_CHIPS)
    nb_seq = t // tb_seq
    dx, g_g1, r_in, vec_sum, mat_sum, g_cw = _in_proj_bwd(
        dz, w_in_f, x2d, dx_res, norm_g, tb_seq,
        reduce=([(g_in.reshape(pieces), False, g_in16.reshape(pieces))], BF16,
                (0, 1, 2, nb_seq // 2, nb_seq - 1)),
        shards=[(vec_mine.reshape(VEC_BAG_ROWS // N_CHIPS, D_MODEL), 0, True),
                (mat_mine.reshape(MAT_BAG_ROWS // N_CHIPS, HEAD_DIM), 0, True)],
        take=(_bag_rows("conv_w"), D_MODEL // N_CHIPS))

    u_in = tuple(a[None] for a in _adamw(w_in[0], r_in.reshape(D_MODEL, IN_COLS // N_CHIPS), m_w_in[0], v_w_in[0],
                                         D_MODEL // 4, "adamw_w_in"))
    proj = [(w_proj_lru, r_pl, m_w_proj_lru, v_w_proj_lru), (w_proj_pool, r_pp, m_w_proj_pool, v_w_proj_pool),
            (w_out, r_out, m_w_out, v_w_out), (w_ple_gate, r_pg, m_w_ple_gate, v_w_ple_gate),
            (w_ple_proj, r_pe, m_w_ple_proj, v_w_ple_proj)]
    u_pl, u_pp, u_out, u_pg, u_pe = [tuple(a[None] for a in u) for u in _adamw_group(
        [(w[0], g.reshape(w.shape[1:]), m[0], v[0]) for w, g, m, v in proj], "adamw_proj")]

    small = [("norm_g", norm_g, m_norm_g, v_norm_g), ("conv_b", conv_b, m_conv_b, v_conv_b),
             ("lru_w_a", lru_w_a, m_lru_w_a, v_lru_w_a), ("lru_b_a", lru_b_a, m_lru_b_a, v_lru_b_a),
             ("lru_w_x", lru_w_x, m_lru_w_x, v_lru_w_x), ("lru_b_x", lru_b_x, m_lru_b_x, v_lru_b_x),
             ("lru_lambda", lru_lambda, m_lru_lambda, v_lru_lambda), ("pool_w", pool_w, m_pool_w, v_pool_w),
             ("pool_scale", pool_scale, m_pool_scale, v_pool_scale),
             ("ple_norm_g", ple_norm_g, m_ple_norm_g, v_ple_norm_g), ("final_g", final_g, m_final_g, v_final_g)]

    def view(a):
        return a.reshape(-1, a.shape[-1]) if a.ndim != 3 else a[0]

    flat = _adamw_replicated(vec_sum, mat_sum, g_g1, [(name,) + tuple(view(a) for a in arrs) for name, *arrs in small],
                             (conv_w[0], m_conv_w[0], v_conv_w[0], g_cw))
    u_small = {name: tuple(flat[4 * k + pick].reshape(arrs[0].shape) for pick in range(4))
               for k, (name, *arrs) in enumerate(small)}
    u_cw = tuple(a[None] for a in flat[4 * len(small):4 * len(small) + 4])

    loss = flat[-1].reshape(())
    grad_x = dx.reshape(bsz, seq, D_MODEL)

    def ordered(pick):
        s = {name: u[pick] for name, u in u_small.items()}
        return [s["norm_g"], u_in[pick], u_cw[pick], s["conv_b"], s["lru_w_a"], s["lru_b_a"], s["lru_w_x"], s["lru_b_x"],
                s["lru_lambda"], s["pool_w"], s["pool_scale"], u_pl[pick], u_pp[pick], u_out[pick], s["ple_norm_g"],
                u_pg[pick], u_pe[pick], s["final_g"]]

    return (loss, grad_x, *ordered(0), *ordered(1), *ordered(2), *ordered(3))
```

```python
import jax
import jax.numpy as jnp
from jax import lax
from jax.experimental import pallas as pl
from jax.experimental.pallas import tpu as pltpu

F32 = jnp.float32
BF16 = jnp.bfloat16
MESH = pl.DeviceIdType.MESH

D_MODEL = 1024
LRU_HEADS = 8
HEAD_DIM = 128
CONV_WIDTH = 4
LRU_C = 8.0
POOL_WIDTH = 512
POOL_WINDOWS = (2, 4, 8, 16)
POOL_GROUP_DIM = 128
IN_COLS = 5120
N_CHIPS = 4
EPS = 1e-6

ADAM_LR = 0.001
ADAM_B1 = 0.9
ADAM_B2 = 0.999
ADAM_EPS = 1e-08
ADAM_WD = 0.01
ADAM_STEP = 10

F32_SUBLANES = 8
CONV_HIST = 8
POOL_HIST = 16
VMEM_LIMIT_BYTES = 58 * 1024 * 1024
VEC_BAG_SLOTS = ("norm_g", "conv_w", "conv_b", "lru_b_a", "lru_b_x", "lru_lambda", "pool_scale", "ple_norm_g",
                 "final_g", "loss")
VEC_BAG_ROWS = 128
MAT_BAG_AT = {"lru_w_a": 0, "lru_w_x": LRU_HEADS * HEAD_DIM, "pool_w": 2 * LRU_HEADS * HEAD_DIM}
MAT_BAG_ROWS = 2 * LRU_HEADS * HEAD_DIM + len(POOL_WINDOWS) * POOL_GROUP_DIM


def _bag_row(name, k=0):
    at = F32_SUBLANES * VEC_BAG_SLOTS.index(name) + k
    return slice(at, at + 1)


def _bag_rows(name):
    at = F32_SUBLANES * VEC_BAG_SLOTS.index(name)
    return slice(at, at + F32_SUBLANES)


def _dot(a, b):
    return jnp.dot(a, b, preferred_element_type=F32)


def _dot_nt(a, b):
    return lax.dot_general(a, b, (((1,), (1,)), ((), ())), preferred_element_type=F32)


def _dot_tn(a, b):
    return lax.dot_general(a, b, (((0,), (0,)), ((), ())), preferred_element_type=F32)


def _sigmoid(v):
    return jax.nn.sigmoid(v)


def _softplus(v):
    return jnp.maximum(v, 0.0) + jnp.log1p(jnp.exp(-jnp.abs(v)))


def _place():
    return lax.axis_index("x"), lax.axis_index("y"), lax.axis_index("c")


GATHER_SEMS = 6


def _gather_shapes(shards):
    out_shape = []
    for arr, axis, _ in shards:
        r, cols = arr.shape
        out_shape.append(jax.ShapeDtypeStruct((N_CHIPS * r, cols) if axis == 0 else (r, N_CHIPS * cols), arr.dtype))
    n = len(shards)
    sems = [pltpu.SemaphoreType.DMA((n * GATHER_SEMS,)), pltpu.SemaphoreType.DMA((n * GATHER_SEMS,)),
            pltpu.SemaphoreType.DMA((n,))]
    return out_shape, sems


def _gather_steps(shards, ins, outs, send_sems, recv_sems, local_sems):
    n = len(shards)
    x, y, c = _place()
    me, sibling = (x, y, c), (x, y, 1 - c)
    chips = [(x, 1 - y), (1 - x, y), (1 - x, 1 - y)]

    def region(k, cx, cy, hc):
        (r, cols), axis = shards[k][0].shape, shards[k][1]
        j = 2 * cx + cy
        if axis == 0:
            if hc is None:
                return outs[k].at[pl.ds(j * r, r), :]
            return outs[k].at[pl.ds(j * r + hc * (r // 2), r // 2), :]
        if hc is None:
            return outs[k].at[:, pl.ds(j * cols, cols)]
        return outs[k].at[pl.ds(hc * (r // 2), r // 2), pl.ds(j * cols, cols)]

    def remote(k, sem, block, to, src=None):
        dst = region(k, *block)
        return pltpu.make_async_remote_copy(
            src_ref=dst if src is None else src, dst_ref=dst,
            send_sem=send_sems.at[k * GATHER_SEMS + sem], recv_sem=recv_sems.at[k * GATHER_SEMS + sem],
            device_id=to, device_id_type=MESH)

    def first(k, idx):
        r, split = shards[k][0].shape[0], shards[k][2]
        src = ins[k].at[pl.ds(c * (r // 2), r // 2), :] if split else ins[k]
        return remote(k, idx, (x, y, c if split else None), (*chips[idx], c), src=src)

    def relay(k):
        src_chip = (jnp.bitwise_xor(x, 1 - c), jnp.bitwise_xor(y, c))
        dst_chip = (jnp.bitwise_xor(x, c), jnp.bitwise_xor(y, 1 - c))
        return remote(k, 2, (*src_chip, c), (*dst_chip, c))

    def passed(k, idx):
        return remote(k, 3 + idx, (*chips[idx], c), sibling)

    def mine(k):
        return pltpu.make_async_copy(ins[k], region(k, x, y, None), local_sems.at[k])

    def start():
        for k in range(n):
            mine(k).start()
            for idx in range(2 if shards[k][2] else 3):
                first(k, idx).start()

    def relay_on():
        for k in range(n):
            split = shards[k][2]
            for idx in range(2):
                remote(k, idx, (*chips[idx], c if split else None), me).wait_recv()
            if split:
                relay(k).start()
                passed(k, 0).start()
                passed(k, 1).start()

    def finish():
        for k in range(n):
            split = shards[k][2]
            remote(k, 2, (*chips[2], c if split else None), me).wait_recv()
            if split:
                passed(k, 2).start()
        for k in range(n):
            if shards[k][2]:
                for idx in range(3):
                    remote(k, 3 + idx, (*chips[idx], 1 - c), me).wait_recv()
        for k in range(n):
            if shards[k][2]:
                for cp in (first(k, 0), first(k, 1), relay(k), passed(k, 0), passed(k, 1), passed(k, 2)):
                    cp.wait_send()
            else:
                for idx in range(3):
                    first(k, idx).wait_send()
            mine(k).wait()

    return start, relay_on, finish


RS_ADD_ROWS = (64, 32, 16, 8)


N_DEV = 2 * N_CHIPS


def _all_reduce_scratch(shape):
    return [pltpu.VMEM((N_DEV,) + tuple(shape), F32), pltpu.SemaphoreType.DMA((N_DEV - 1,)),
            pltpu.SemaphoreType.DMA((N_DEV - 1,))]


def _all_reduce_tile(v_ref, o_ref, slots, send_sems, recv_sems):
    flips = [(dx, dy, dc) for dx in (0, 1) for dy in (0, 1) for dc in (0, 1)][1:]
    x, y, c = _place()
    mine = 4 * x + 2 * y + c

    def copy(k, to_flip, slot):
        dx, dy, dc = to_flip
        peer = (jnp.bitwise_xor(x, dx), jnp.bitwise_xor(y, dy), jnp.bitwise_xor(c, dc))
        return pltpu.make_async_remote_copy(
            src_ref=v_ref, dst_ref=slots.at[slot], send_sem=send_sems.at[k], recv_sem=recv_sems.at[k],
            device_id=peer, device_id_type=MESH)

    sends = [copy(k, flip, mine) for k, flip in enumerate(flips)]
    for cp in sends:
        cp.start()
    slots[mine] = v_ref[...]
    for k, (dx, dy, dc) in enumerate(flips):
        copy(k, (dx, dy, dc), jnp.bitwise_xor(mine, 4 * dx + 2 * dy + dc)).wait_recv()
    total = slots[0]
    for d in range(1, N_DEV):
        total = total + slots[d]
    o_ref[...] = total
    for cp in sends:
        cp.wait_send()


RS_SEMS = 8
RS_LOCAL_SEMS = 5


def _rs_piece_shape(part):
    arr, cols = part[0], part[1]
    return (arr.shape[0] // 2, arr.shape[1] // N_CHIPS) if cols else tuple(arr.shape[1:])


def _rs_operands(parts):
    return [p[0] for p in parts] + [p[0] if p[2] is None else p[2] for p in parts]


def _rs_wires(parts, wire):
    return list(wire) if isinstance(wire, (list, tuple)) else [wire] * len(parts)


def _rs_shapes(parts, wire):
    n = len(parts)
    shapes = [_rs_piece_shape(p) for p in parts]
    out_shape = [jax.ShapeDtypeStruct((2,) + s, F32) for s in shapes]
    scratch = []
    for lead, kind in ((N_CHIPS, "f32"), (N_CHIPS, "narrow"), (N_CHIPS, "wire"), (None, "f32"), (N_CHIPS, "wire")):
        for s, p, w in zip(shapes, parts, _rs_wires(parts, wire)):
            dtype = {"f32": F32, "narrow": F32 if p[2] is None else p[2].dtype, "wire": w}[kind]
            scratch.append(pltpu.VMEM(s if lead is None else (lead,) + s, dtype))
    scratch += [pltpu.SemaphoreType.DMA((n * RS_SEMS,)), pltpu.SemaphoreType.DMA((n * RS_SEMS,)),
                pltpu.SemaphoreType.DMA((n * RS_LOCAL_SEMS,))]
    return out_shape, scratch


def _rs_steps(parts, ins, outs, scratch):
    n = len(parts)
    own, sib, got, fin, snd = (scratch[k * n:(k + 1) * n] for k in range(5))
    send_sems, recv_sems, local_sems = scratch[5 * n:]
    shapes = [_rs_piece_shape(p) for p in parts]
    x, y, c = _place()
    j_me = 2 * x + y
    me, sibling = (x, y, c), (x, y, 1 - c)

    def piece(a, jj, core, narrow=False):
        ref = ins[n + a] if narrow else ins[a]
        if parts[a][1]:
            r, cl = shapes[a]
            return ref.at[pl.ds(core * r, r), pl.ds(jj * cl, cl)]
        return ref.at[2 * jj + core]

    def remote(a, sem, src, dst, to):
        return pltpu.make_async_remote_copy(
            src_ref=src, dst_ref=dst, send_sem=send_sems.at[a * RS_SEMS + sem],
            recv_sem=recv_sems.at[a * RS_SEMS + sem], device_id=to, device_id_type=MESH)

    def rows_loop(a, fn):
        r = shapes[a][0]
        step = max(s for s in RS_ADD_ROWS if r % s == 0)

        def it(i, carry):
            fn(pl.ds(pl.multiple_of(i * step, step), step))
            return carry

        lax.fori_loop(0, r // step, it, 0)

    def load(a, jj):
        return pltpu.make_async_copy(piece(a, jj, c), own[a].at[jj], local_sems.at[a * RS_LOCAL_SEMS + jj])

    def to_sibling(a, jj):
        return remote(a, jj, piece(a, jj, 1 - c, narrow=True), sib[a].at[jj], sibling)

    near = (jnp.bitwise_xor(x, 1 - c), jnp.bitwise_xor(y, c))
    far = (jnp.bitwise_xor(x, c), jnp.bitwise_xor(y, 1 - c))
    diag = (1 - x, 1 - y)
    FROM_NEAR, FROM_FAR, FEED = 0, 1, 2

    def chip_of(chip):
        return 2 * chip[0] + chip[1]

    def feed(a):
        return remote(a, 4, snd[a].at[chip_of(diag)], got[a].at[FEED], (*near, c))

    def to_near(a):
        return remote(a, 5, snd[a].at[chip_of(near)], got[a].at[FROM_NEAR], (*near, c))

    def to_far(a):
        return remote(a, 6, snd[a].at[chip_of(far)], got[a].at[FROM_FAR], (*far, c))

    def store(a):
        return pltpu.make_async_copy(fin[a], outs[a].at[c], local_sems.at[a * RS_LOCAL_SEMS + 4])

    def result_to_sibling(a):
        return remote(a, 7, fin[a], outs[a].at[c], sibling)

    order = [chip_of(diag), chip_of(near), chip_of(far), j_me]
    sibling_order = [chip_of(diag), chip_of(far), chip_of(near), j_me]

    def exchange():
        for k in range(N_CHIPS):
            for a in range(n):
                load(a, order[k]).start()
                to_sibling(a, sibling_order[k]).start()

    def pair_sum(k):
        jj = order[k]
        for a in range(n):
            load(a, jj).wait()
            remote(a, jj, sib[a].at[jj], sib[a].at[jj], me).wait_recv()

            def add(sl, a=a):
                q = own[a][jj, sl, :] + sib[a][jj, sl, :].astype(F32)
                own[a][jj, sl, :] = q
                snd[a][jj, sl, :] = q.astype(snd[a].dtype)

            rows_loop(a, add)

    def feed_sums():
        pair_sum(0)
        for a in range(n):
            feed(a).start()

    def chip_sums():
        pair_sum(1)
        for a in range(n):
            to_near(a).start()
        pair_sum(2)
        pair_sum(3)

    def relay():
        for a in range(n):
            remote(a, 4, got[a].at[FEED], got[a].at[FEED], me).wait_recv()

            def add(sl, a=a):
                pair = own[a][chip_of(far), sl, :] + got[a][FEED, sl, :].astype(F32)
                snd[a][chip_of(far), sl, :] = pair.astype(snd[a].dtype)

            rows_loop(a, add)
            to_far(a).start()

    def totals():
        for a in range(n):
            remote(a, 5, got[a].at[FROM_NEAR], got[a].at[FROM_NEAR], me).wait_recv()
            remote(a, 6, got[a].at[FROM_FAR], got[a].at[FROM_FAR], me).wait_recv()

            def total(sl, a=a):
                fin[a][sl, :] = (own[a][j_me, sl, :] + got[a][FROM_NEAR, sl, :].astype(F32)) + (
                    got[a][FROM_FAR, sl, :].astype(F32))

            rows_loop(a, total)
            store(a).start()
            result_to_sibling(a).start()

    def finish():
        for a in range(n):
            remote(a, 7, outs[a].at[1 - c], outs[a].at[1 - c], me).wait_recv()
        for a in range(n):
            for jj in range(N_CHIPS):
                to_sibling(a, jj).wait_send()
            for cp in (feed(a), to_near(a), to_far(a), result_to_sibling(a)):
                cp.wait_send()
            store(a).wait()

    return exchange, feed_sums, chip_sums, relay, totals, finish


def _rms(x):
    r = lax.rsqrt(jnp.mean(x * x, axis=-1, keepdims=True) + EPS)
    return x * r, r


def _rms_bwd(dxn, xn, r):
    return r * (dxn - xn * jnp.mean(dxn * xn, axis=-1, keepdims=True))


def _in_proj_gather(x2d, norm_g, w_in_sh, shards, tb, casts):
    t = x2d.shape[0]
    nb = t // tb
    cols = IN_COLS // N_CHIPS
    half = D_MODEL // 2
    n = len(shards)
    nc = len(casts)

    def body(x_ref, g_ref, win_ref, *refs):
        ins, cast_ins = refs[:n], refs[n:n + nc]
        z_ref, h_ref, wfull_ref = refs[n + nc:n + nc + 3]
        outs, cast_outs = refs[n + nc + 3:2 * n + nc + 3], refs[2 * n + nc + 3:2 * (n + nc) + 3]
        scratch = refs[2 * (n + nc) + 3:]
        wv, h_all, send_sems, recv_sems, local_sems, w_send, w_recv, w_local, stage = scratch[:9]
        wide, narrow, cast_sems = scratch[9:9 + nc], scratch[9 + nc:9 + 2 * nc], scratch[9 + 2 * nc]
        s, i = pl.program_id(0), pl.program_id(1)
        x, y, c = _place()
        me, sibling = (x, y, c), (x, y, 1 - c)
        chips = [(x, 1 - y), (1 - x, y), (1 - x, 1 - y)]

        def w_half(cx, cy, hc):
            return wv.at[2 * cx + cy, pl.ds(hc * half, half), :]

        def w_remote(sem, block, to, src=None):
            dst = w_half(*block)
            return pltpu.make_async_remote_copy(
                src_ref=dst if src is None else src, dst_ref=dst, send_sem=w_send.at[sem],
                recv_sem=w_recv.at[sem], device_id=to, device_id_type=MESH)

        def w_first(idx):
            return w_remote(idx, (x, y, c), (*chips[idx], c))

        def w_relay():
            src_chip = (jnp.bitwise_xor(x, 1 - c), jnp.bitwise_xor(y, c))
            dst_chip = (jnp.bitwise_xor(x, c), jnp.bitwise_xor(y, 1 - c))
            return w_remote(2, (*src_chip, c), (*dst_chip, c))

        def w_pass(idx):
            return w_remote(3 + idx, (*chips[idx], c), sibling)

        def w_store(k, cx, cy):
            jj = 2 * cx + cy
            return pltpu.make_async_copy(wv.at[jj], wfull_ref.at[:, pl.ds(jj * cols, cols)], w_local.at[k])

        start_rest, relay_rest, finish_rest = _gather_steps(shards, ins, outs, send_sems, recv_sems, local_sems)

        def own(k, hc):
            return pltpu.make_async_copy(win_ref.at[pl.ds(pl.multiple_of(hc * half, half), half), :], stage.at[k],
                                         w_local.at[4 + 2 * k])

        def round_own(k, hc):
            own(k, hc).wait()
            wv[2 * x + y, pl.ds(pl.multiple_of(hc * half, half), half), :] = stage[k].astype(BF16)

        wide_in = [pltpu.make_async_copy(cast_ins[k], wide[k], cast_sems.at[k]) for k in range(nc)]
        narrow_out = [pltpu.make_async_copy(narrow[k], cast_outs[k], cast_sems.at[nc + k]) for k in range(nc)]

        @pl.when((s == 0) & (i == 0))
        def _():
            own(0, c).start()
            own(1, 1 - c).start()
            for cp in wide_in:
                cp.start()
            round_own(0, c)
            w_first(0).start()
            w_first(1).start()
            start_rest()
            round_own(1, 1 - c)
            w_store(0, x, y).start()

        @pl.when((s == 1) & (i == 0))
        def _():
            for k in range(nc):
                wide_in[k].wait()
                narrow[k][...] = wide[k][...].astype(BF16)
                narrow_out[k].start()
            w_remote(0, (*chips[0], c), me).wait_recv()
            w_remote(1, (*chips[1], c), me).wait_recv()
            w_relay().start()
            w_pass(0).start()
            w_pass(1).start()
            w_remote(3, (*chips[0], 1 - c), me).wait_recv()
            w_store(1, *chips[0]).start()

        @pl.when((s == 2) & (i == 0))
        def _():
            w_remote(4, (*chips[1], 1 - c), me).wait_recv()
            w_store(2, *chips[1]).start()

        @pl.when((s == 3) & (i == 0))
        def _():
            w_remote(2, (*chips[2], c), me).wait_recv()
            w_pass(2).start()
            w_remote(5, (*chips[2], 1 - c), me).wait_recv()
            w_store(3, *chips[2]).start()

        keep_h = pltpu.make_async_copy(h_all.at[i], h_ref.at[pl.ds(pl.multiple_of(i * tb, tb), tb), :], w_local.at[5])

        @pl.when(s == 0)
        def _():
            xn, _ = _rms(x_ref[...])
            h_all[i] = (xn * g_ref[...]).astype(BF16)
            keep_h.start()

        z_ref[...] = _dot(h_all[i], wv[jnp.bitwise_xor(2 * x + y, s)])
        pl.when(s == 0)(keep_h.wait)

        @pl.when((s == N_CHIPS - 1) & (i == nb - 1))
        def _():
            relay_rest()
            finish_rest()
            for cp in (w_first(0), w_first(1), w_relay(), w_pass(0), w_pass(1), w_pass(2)):
                cp.wait_send()
            w_store(0, x, y).wait()
            for idx in range(3):
                w_store(idx + 1, *chips[idx]).wait()
            for cp in narrow_out:
                cp.wait()

    rest_shape, rest_sems = _gather_shapes(shards)
    out_shape = [jax.ShapeDtypeStruct((t, IN_COLS), F32), jax.ShapeDtypeStruct((t, D_MODEL), BF16),
                 jax.ShapeDtypeStruct((D_MODEL, IN_COLS), BF16)] + rest_shape
    out_shape += [jax.ShapeDtypeStruct(a.shape, BF16) for a in casts]
    any_spec = pl.BlockSpec(memory_space=pl.ANY)

    def z_map(s, i):
        return (i, jnp.bitwise_xor(2 * lax.axis_index("x") + lax.axis_index("y"), s))

    return pl.pallas_call(
        body, name="in_proj", out_shape=tuple(out_shape),
        grid=(N_CHIPS, nb),
        in_specs=[pl.BlockSpec((tb, D_MODEL), lambda s, i: (jnp.where(s == 0, i, nb - 1), 0)),
                  pl.BlockSpec((1, D_MODEL), lambda s, i: (0, 0)), any_spec] + [any_spec] * (n + nc),
        out_specs=tuple([pl.BlockSpec((tb, cols), z_map), any_spec, any_spec] + [any_spec] * (n + nc)),
        scratch_shapes=[pltpu.VMEM((N_CHIPS, D_MODEL, cols), BF16), pltpu.VMEM((nb, tb, D_MODEL), BF16)] + rest_sems + [
            pltpu.SemaphoreType.DMA((GATHER_SEMS,)), pltpu.SemaphoreType.DMA((GATHER_SEMS,)),
            pltpu.SemaphoreType.DMA((N_CHIPS + 3,)), pltpu.VMEM((2, half, cols), F32)]
        + [pltpu.VMEM(a.shape, F32) for a in casts] + [pltpu.VMEM(a.shape, BF16) for a in casts]
        + [pltpu.SemaphoreType.DMA((2 * nc,))],
        compiler_params=pltpu.CompilerParams(dimension_semantics=("arbitrary", "arbitrary"),
                                             vmem_limit_bytes=VMEM_LIMIT_BYTES),
    )(x2d, norm_g, w_in_sh, *[sh[0] for sh in shards], *casts)


def _in_proj_bwd(dz, w_in, x2d, dx_res, norm_g, tb, reduce, shards, take):
    t = x2d.shape[0]
    nb = t // tb
    parts, wire, steps = reduce
    n = len(parts)
    k = len(shards)
    take_rows, take_width = take

    def body(dz_ref, w_ref, x_ref, dres_ref, g_ref, *refs):
        at = 2 * n + k
        dx_ref, dg_ref = refs[at:at + 2]
        rs_outs, g_outs = refs[at + 2:at + 2 + n], refs[at + 2 + n:at + 2 + n + k]
        cut_ref = refs[at + 2 + n + k]
        scratch = refs[at + 3 + n + k:]
        rs_scr, g_sems, dg_acc, ar_scr, cut_sem = scratch[:-8], scratch[-8:-5], scratch[-5], scratch[-4:-1], scratch[-1]
        rs = _rs_steps(parts, refs[:2 * n], rs_outs, rs_scr)
        for step, when in zip(rs[:-1], steps):
            pl.when(pl.program_id(0) == when)(step)
        gather = _gather_steps(shards, refs[2 * n:at], g_outs, *g_sems)
        for step, when in zip(gather, (0, nb // 2, nb - 1)):
            pl.when(pl.program_id(0) == when)(step)

        @pl.when(pl.program_id(0) == 0)
        def _():
            dg_acc[...] = jnp.zeros_like(dg_acc)

        xn, r = _rms(x_ref[...])
        g = g_ref[...]
        dh = _dot_nt(dz_ref[...], w_ref[...])
        dg_acc[0:1, :] += jnp.sum(dh * xn, axis=0, keepdims=True)
        dx_ref[...] = dres_ref[...] + _rms_bwd(dh * g, xn, r)

        @pl.when(pl.program_id(0) == nb - 1)
        def _():
            x, y, _ = _place()
            mine = pl.ds(pl.multiple_of((2 * x + y) * take_width, take_width), take_width)
            cut = pltpu.make_async_copy(g_outs[0].at[take_rows, mine], cut_ref, cut_sem)
            cut.start()
            _all_reduce_tile(dg_acc, dg_ref, *ar_scr)
            rs[-1]()
            cut.wait()

    row = lambda i: (i, 0)
    fixed = lambda i: (0, 0)
    rs_shape, rs_scratch = _rs_shapes(parts, wire)
    g_shape, g_sems = _gather_shapes(shards)
    any_spec = pl.BlockSpec(memory_space=pl.ANY)
    cut_shape = jax.ShapeDtypeStruct((take_rows.stop - take_rows.start, take_width), F32)
    return pl.pallas_call(
        body, name="in_proj_bwd",
        out_shape=tuple([jax.ShapeDtypeStruct((t, D_MODEL), F32), jax.ShapeDtypeStruct((F32_SUBLANES, D_MODEL), F32)]
                        + rs_shape + g_shape + [cut_shape]),
        grid=(nb,),
        in_specs=[pl.BlockSpec((tb, IN_COLS), row),
                  pl.BlockSpec((D_MODEL, IN_COLS), fixed, pipeline_mode=pl.Buffered(1)),
                  pl.BlockSpec((tb, D_MODEL), row), pl.BlockSpec((tb, D_MODEL), row),
                  pl.BlockSpec((1, D_MODEL), fixed)] + [any_spec] * (2 * n + k),
        out_specs=tuple([pl.BlockSpec((tb, D_MODEL), row), pl.BlockSpec((F32_SUBLANES, D_MODEL), fixed)]
                        + [any_spec] * (n + k + 1)),
        scratch_shapes=rs_scratch + g_sems + [pltpu.VMEM((F32_SUBLANES, D_MODEL), F32)] + _all_reduce_scratch(
            (F32_SUBLANES, D_MODEL)) + [pltpu.SemaphoreType.DMA(())],
        compiler_params=pltpu.CompilerParams(dimension_semantics=("arbitrary",),
                                             vmem_limit_bytes=VMEM_LIMIT_BYTES),
    )(dz, w_in, x2d, dx_res, norm_g, *_rs_operands(parts), *[sh[0] for sh in shards])


def _weight_grad(lhs, rhs, n_chunks, tb, name, reduce=None):
    t, k = lhs.shape
    nc = rhs.shape[1] // n_chunks
    nb = t // tb
    parts, wire, steps = reduce if reduce is not None else ([], F32, ())
    n = len(parts)

    def body(l_ref, r_ref, *refs):
        o_ref, o16_ref = refs[2 * n:2 * n + 2]
        if n:
            at = pl.program_id(0) * nb + pl.program_id(1)
            rs = _rs_steps(parts, refs[:2 * n], refs[2 * n + 2:3 * n + 2], refs[3 * n + 2:])
            for step, when in zip(rs, steps):
                pl.when(at == when)(step)

        @pl.when(pl.program_id(1) == 0)
        def _():
            o_ref[...] = jnp.zeros_like(o_ref)

        o_ref[...] += _dot_tn(l_ref[...], r_ref[...])

        @pl.when(pl.program_id(1) == nb - 1)
        def _():
            o16_ref[...] = o_ref[...].astype(BF16)

    rs_shape, rs_scratch = _rs_shapes(parts, wire) if n else ([], [])
    any_spec = pl.BlockSpec(memory_space=pl.ANY)
    chunk = pl.BlockSpec((None, k, nc), lambda j, i: (j, 0, 0))
    return pl.pallas_call(
        body, name=name,
        out_shape=tuple([jax.ShapeDtypeStruct((n_chunks, k, nc), F32), jax.ShapeDtypeStruct((n_chunks, k, nc), BF16)]
                        + rs_shape),
        grid=(n_chunks, nb),
        in_specs=[pl.BlockSpec((tb, k), lambda j, i: (i, 0)), pl.BlockSpec((tb, nc), lambda j, i: (i, j))]
        + [any_spec] * (2 * n),
        out_specs=tuple([chunk, chunk] + [any_spec] * n),
        scratch_shapes=rs_scratch,
        compiler_params=pltpu.CompilerParams(dimension_semantics=("arbitrary", "arbitrary"),
                                             vmem_limit_bytes=VMEM_LIMIT_BYTES),
    )(lhs, rhs, *_rs_operands(parts))


def _adam_update(w, g, m, v):
    m_ = ADAM_B1 * m + (1.0 - ADAM_B1) * g
    v_ = ADAM_B2 * v + (1.0 - ADAM_B2) * jnp.square(g)
    m_hat = m_ / (1.0 - ADAM_B1 ** ADAM_STEP)
    v_hat = v_ / (1.0 - ADAM_B2 ** ADAM_STEP)
    return -ADAM_LR * (m_hat / (jnp.sqrt(v_hat) + ADAM_EPS) + ADAM_WD * w), m_, v_


def _adamw_replicated(vec_sum, mat_sum, norm_grad, entries, conv):
    n = len(entries)

    def grad_of(name, shape, vec_ref, mat_ref, norm_ref):
        if name == "norm_g":
            return norm_ref[0:1, :]
        if name in MAT_BAG_AT:
            return mat_ref[MAT_BAG_AT[name]:MAT_BAG_AT[name] + shape[0], :]
        if shape[0] == 1:
            return vec_ref[_bag_row(name), 0:shape[1]]
        return jnp.concatenate([vec_ref[_bag_row(name), h * shape[1]:(h + 1) * shape[1]] for h in range(shape[0])],
                               axis=0)

    def body(vec_ref, mat_ref, norm_ref, *refs):
        ins, outs = refs[:3 * n + 4], refs[3 * n + 4:]
        for k in range(n):
            w_ref, m_ref, v_ref = ins[3 * k:3 * k + 3]
            g = grad_of(entries[k][0], w_ref.shape, vec_ref, mat_ref, norm_ref)
            d, m_, v_ = _adam_update(w_ref[...], g, m_ref[...], v_ref[...])
            for ref, val in zip(outs[4 * k:4 * k + 4], (g, d, m_, v_)):
                ref[...] = val
        w_ref, m_ref, v_ref, g_ref = ins[3 * n:]
        g = g_ref[0:w_ref.shape[0], :]
        for ref, val in zip(outs[4 * n:4 * n + 4], (g,) + _adam_update(w_ref[...], g, m_ref[...], v_ref[...])):
            ref[...] = val
        outs[4 * n + 4][...] = vec_ref[_bag_row("loss"), 0:1]

    arrays = [a for e in entries for a in e[1:]] + list(conv)
    out_shape = [jax.ShapeDtypeStruct(e[1].shape, F32) for e in entries for _ in range(4)]
    out_shape += [jax.ShapeDtypeStruct(conv[0].shape, F32)] * 4 + [jax.ShapeDtypeStruct((1, 1), F32)]
    return pl.pallas_call(
        body, name="adamw_replicated", out_shape=tuple(out_shape),
        compiler_params=pltpu.CompilerParams(vmem_limit_bytes=VMEM_LIMIT_BYTES),
    )(vec_sum, mat_sum, norm_grad, *arrays)


def _adamw(w, g, m, v, rows, name):
    r, c = w.shape

    def body(w_ref, g_ref, m_ref, v_ref, go_ref, d_ref, nm_ref, nv_ref):
        g = g_ref[...]
        go_ref[...] = g
        d_ref[...], nm_ref[...], nv_ref[...] = _adam_update(w_ref[...], g, m_ref[...], v_ref[...])

    spec = pl.BlockSpec((rows, c), lambda i: (i, 0))
    return pl.pallas_call(
        body, name=name, out_shape=tuple(jax.ShapeDtypeStruct((r, c), F32) for _ in range(4)),
        grid=(r // rows,), in_specs=[spec] * 4, out_specs=(spec,) * 4,
        compiler_params=pltpu.CompilerParams(dimension_semantics=("arbitrary",),
                                             vmem_limit_bytes=VMEM_LIMIT_BYTES),
    )(w, g, m, v)


def _adamw_group(items, name):
    n = 4 * len(items)

    def body(*refs):
        ins, outs, bufs = refs[:n], refs[n:2 * n], refs[2 * n:3 * n]
        load_sems, store_sems = refs[3 * n:]
        loads = [pltpu.make_async_copy(ins[j], bufs[j], load_sems.at[j]) for j in range(n)]
        stores = [pltpu.make_async_copy(bufs[j], outs[j], store_sems.at[j]) for j in range(n)]
        for cp in loads:
            cp.start()
        for k in range(len(items)):
            for cp in loads[4 * k:4 * k + 4]:
                cp.wait()
            w_buf, g_buf, m_buf, v_buf = bufs[4 * k:4 * k + 4]
            w_buf[...], m_buf[...], v_buf[...] = _adam_update(w_buf[...], g_buf[...], m_buf[...], v_buf[...])
            for cp in stores[4 * k:4 * k + 4]:
                cp.start()
        for cp in stores:
            cp.wait()

    arrays = [a for item in items for a in item]
    any_spec = pl.BlockSpec(memory_space=pl.ANY)
    flat = pl.pallas_call(
        body, name=name, out_shape=tuple(jax.ShapeDtypeStruct(a.shape, F32) for a in arrays),
        in_specs=[any_spec] * n, out_specs=(any_spec,) * n,
        scratch_shapes=[pltpu.VMEM(a.shape, F32) for a in arrays] + [pltpu.SemaphoreType.DMA((n,))] * 2,
        compiler_params=pltpu.CompilerParams(vmem_limit_bytes=VMEM_LIMIT_BYTES),
    )(*arrays)
    return [(flat[4 * k + 1], flat[4 * k], flat[4 * k + 2], flat[4 * k + 3]) for k in range(len(items))]


def _shift_down(ext, s):
    return pltpu.roll(ext, s, 0)


def _tile_shift(v, s):
    rows, cols = v.shape
    tiles = v.reshape(rows // F32_SUBLANES, F32_SUBLANES, cols)
    return pltpu.roll(tiles, s % F32_SUBLANES, 1).reshape(rows, cols)


def _shift_up(ext, s):
    return pltpu.roll(ext, ext.shape[0] - s, 0)


def _lru_gates(xc, wa_ref, ba, wx_ref, bx, lam):
    pa, px = [], []
    for h in range(LRU_HEADS):
        xh = xc[:, h * HEAD_DIM:(h + 1) * HEAD_DIM].astype(BF16)
        pa.append(_dot(xh, wa_ref[h]))
        px.append(_dot(xh, wx_ref[h]))
    r = _sigmoid(jnp.concatenate(pa, axis=1) + ba)
    ig = _sigmoid(jnp.concatenate(px, axis=1) + bx)
    sp = _softplus(-lam)
    log_a = (-LRU_C * r) * sp
    a = jnp.exp(log_a)
    mult = jnp.sqrt(jnp.tanh(-log_a) * (1.0 + a * a))
    return r, ig, a, mult, sp


def _conv(ext, w_ref, b):
    y = b + _shift_down(ext, 3) * w_ref[0:1, :]
    y = y + _shift_down(ext, 2) * w_ref[1:2, :]
    y = y + _shift_down(ext, 1) * w_ref[2:3, :]
    y = y + ext * w_ref[3:4, :]
    return y[CONV_HIST:, :]


def _pool_diff(ext, pos):
    out = []
    for g, k in enumerate(POOL_WINDOWS):
        col = ext[:, g * POOL_GROUP_DIM:(g + 1) * POOL_GROUP_DIM]
        s = col
        for step in range(g + 1):
            s = s + _shift_down(s, 2 ** step)
        count = jnp.minimum(pos + 1, k).astype(F32)
        out.append(s[POOL_HIST:, :] / count - col[POOL_HIST:, :])
    return out


def _pool_mix(diff, pw_ref):
    return jnp.concatenate([_dot(diff[g].astype(BF16), pw_ref[g]) for g in range(len(POOL_WINDOWS))], axis=1)


def _branch_specs(tb, row_map, fixed):
    fixed3 = lambda i: (0, 0, 0)
    return [pl.BlockSpec((CONV_WIDTH, D_MODEL), fixed), pl.BlockSpec((1, D_MODEL), fixed),
            pl.BlockSpec((LRU_HEADS, HEAD_DIM, HEAD_DIM), fixed3), pl.BlockSpec((1, D_MODEL), fixed),
            pl.BlockSpec((LRU_HEADS, HEAD_DIM, HEAD_DIM), fixed3), pl.BlockSpec((1, D_MODEL), fixed),
            pl.BlockSpec((1, D_MODEL), fixed),
            pl.BlockSpec((len(POOL_WINDOWS), POOL_GROUP_DIM, POOL_GROUP_DIM), fixed3),
            pl.BlockSpec((1, POOL_WIDTH), fixed)]


def _branches_fwd(z, weights, seq, tb, shards):
    t = z.shape[0]
    nb = t // tb
    nbe = seq // tb
    groups = tb // F32_SUBLANES
    n = len(shards)

    def body(xa_ref, ga_ref, xb_ref, gb_ref, cw_ref, cb_ref, wa_ref, ba_ref, wx_ref, bx_ref, lam_ref,
             pw_ref, ps_ref, *refs):
        g_ins = refs[:n]
        ya_ref, yb_ref, hl_ref = refs[n:n + 3]
        g_outs = refs[n + 3:2 * n + 3]
        xa_ext, xb_ext, carry, a_s, u_s, send_sems, recv_sems, local_sems = refs[2 * n + 3:]
        blk = pl.program_id(0) % nbe
        start_gather, relay_gather, finish_gather = _gather_steps(shards, g_ins, g_outs, send_sems, recv_sems,
                                                                  local_sems)
        pl.when(pl.program_id(0) == 0)(start_gather)
        pl.when(pl.program_id(0) == nb // 2)(relay_gather)

        @pl.when(blk == 0)
        def _():
            xa_ext[0:CONV_HIST, :] = jnp.zeros((CONV_HIST, D_MODEL), F32)
            xb_ext[0:POOL_HIST, :] = jnp.zeros((POOL_HIST, POOL_WIDTH), F32)
            carry[...] = jnp.zeros_like(carry)

        xa_ext[CONV_HIST:, :] = xa_ref[...]
        xb_ext[POOL_HIST:, :] = xb_ref[...]
        ea = xa_ext[...]
        eb = xb_ext[...]
        xa_ext[0:CONV_HIST, :] = ea[tb:, :]
        xb_ext[0:POOL_HIST, :] = eb[tb:, :]

        xc = _conv(ea, cw_ref, cb_ref[...])
        _, ig, a, mult, _ = _lru_gates(xc, wa_ref, ba_ref[...], wx_ref, bx_ref[...], lam_ref[...])
        u = mult * (ig * xc)
        row8 = lax.broadcasted_iota(jnp.int32, (tb, D_MODEL), 0) % F32_SUBLANES
        for s in (1, 2, 4):
            m = row8 >= s
            u = jnp.where(m, a * _tile_shift(u, s) + u, u)
            a = jnp.where(m, a * _tile_shift(a, s), a)
        a_s[...] = a
        u_s[...] = u

        def step(g, cr):
            sl = pl.ds(pl.multiple_of(g * F32_SUBLANES, F32_SUBLANES), F32_SUBLANES)
            hb = a_s[sl, :] * cr + u_s[sl, :]
            hl_ref[sl, :] = hb
            return jnp.broadcast_to(hb[F32_SUBLANES - 1:F32_SUBLANES, :], (F32_SUBLANES, D_MODEL))

        carry[...] = lax.fori_loop(0, groups, step, carry[...], unroll=4)
        ga = ga_ref[...]
        ya_ref[...] = (hl_ref[...] * (ga * _sigmoid(ga))).astype(BF16)

        pos = blk * tb + lax.broadcasted_iota(jnp.int32, (tb, POOL_GROUP_DIM), 0)
        ypre = _pool_mix(_pool_diff(eb, pos), pw_ref)
        gb = gb_ref[...]
        yb_ref[...] = ((ypre * ps_ref[...]) * (gb * _sigmoid(gb))).astype(BF16)
        pl.when(pl.program_id(0) == nb - 1)(finish_gather)

    row = lambda i: (i, 0)
    fixed = lambda i: (0, 0)
    any_spec = pl.BlockSpec(memory_space=pl.ANY)
    in_specs = [pl.BlockSpec((tb, D_MODEL), lambda i: (i, 0)), pl.BlockSpec((tb, D_MODEL), lambda i: (i, 1)),
                pl.BlockSpec((tb, POOL_WIDTH), lambda i: (i, 4)), pl.BlockSpec((tb, POOL_WIDTH), lambda i: (i, 5)),
                ] + _branch_specs(tb, row, fixed) + [any_spec] * n
    g_shape, g_sems = _gather_shapes(shards)
    return pl.pallas_call(
        body, name="branches_fwd",
        out_shape=tuple([jax.ShapeDtypeStruct((t, D_MODEL), BF16), jax.ShapeDtypeStruct((t, POOL_WIDTH), BF16),
                         jax.ShapeDtypeStruct((t, D_MODEL), F32)] + g_shape),
        grid=(nb,), in_specs=in_specs,
        out_specs=tuple([pl.BlockSpec((tb, D_MODEL), row), pl.BlockSpec((tb, POOL_WIDTH), row),
                         pl.BlockSpec((tb, D_MODEL), row)] + [any_spec] * n),
        scratch_shapes=[pltpu.VMEM((tb + CONV_HIST, D_MODEL), F32), pltpu.VMEM((tb + POOL_HIST, POOL_WIDTH), F32),
                        pltpu.VMEM((F32_SUBLANES, D_MODEL), F32),
                        pltpu.VMEM((tb, D_MODEL), F32), pltpu.VMEM((tb, D_MODEL), F32)] + g_sems,
        compiler_params=pltpu.CompilerParams(dimension_semantics=("arbitrary",),
                                             vmem_limit_bytes=VMEM_LIMIT_BYTES),
    )(z, z, z, z, *weights, *[sh[0] for sh in shards])


def _branches_bwd(z, hl, dya, dyb, dzm, weights, vec_bag, seq, tb, riders):
    t = z.shape[0]
    nb = t // tb
    nbe = seq // tb
    groups = tb // F32_SUBLANES
    nr = len(riders)

    def body(xa_ref, xap_ref, ga_ref, xb_ref, xbp_ref, gb_ref, hl_ref, hlp_ref, dya_ref, dyb_ref, dzm_ref,
             cw_ref, cb_ref, wa_ref, ba_ref, wx_ref, bx_ref, lam_ref, pw_ref, ps_ref, vec_in_ref, *rest):
        pairs, (dz_ref, vec_ref, mat_ref), grads = rest[:2 * nr], rest[2 * nr:2 * nr + 3], rest[2 * nr + 3:4 * nr + 3]
        xa_ext, xb_ext, hl_ext, a_ext, dxc_ext, dwin_ext, g_carry, b_s, d_s, g_s = rest[4 * nr + 3:]
        i = pl.program_id(0)
        blk = (nb - 1 - i) % nbe

        def mat_rows(name, k):
            at = MAT_BAG_AT[name] + k * HEAD_DIM
            return slice(at, at + HEAD_DIM)

        def rider(k):
            grads[2 * k][...] += _dot_tn(pairs[2 * k][...], pairs[2 * k + 1][...])

        @pl.when(i == 0)
        def _():
            vec_ref[...] = vec_in_ref[...]
            mat_ref[...] = jnp.zeros_like(mat_ref)
            for k in range(nr):
                grads[2 * k][...] = jnp.zeros_like(grads[2 * k])

        @pl.when(blk == nbe - 1)
        def _():
            a_ext[tb:, :] = jnp.zeros((F32_SUBLANES, D_MODEL), F32)
            dxc_ext[tb:, :] = jnp.zeros((CONV_HIST, D_MODEL), F32)
            dwin_ext[tb:, :] = jnp.zeros((POOL_HIST, POOL_WIDTH), F32)
            g_carry[...] = jnp.zeros_like(g_carry)

        live = (blk > 0).astype(F32)
        xa_ext[0:CONV_HIST, :] = xap_ref[...] * live
        xa_ext[CONV_HIST:, :] = xa_ref[...]
        xb_ext[0:POOL_HIST, :] = xbp_ref[...] * live
        xb_ext[POOL_HIST:, :] = xb_ref[...]
        hl_ext[0:F32_SUBLANES, :] = hlp_ref[...] * live
        hl_ext[F32_SUBLANES:, :] = hl_ref[...]
        ea = xa_ext[...]
        eb = xb_ext[...]
        rider(0)

        xc = _conv(ea, cw_ref, cb_ref[...])
        lam = lam_ref[...]
        r, ig, a, mult, sp = _lru_gates(xc, wa_ref, ba_ref[...], wx_ref, bx_ref[...], lam)
        hl = hl_ref[...]
        ga = ga_ref[...]
        sga = _sigmoid(ga)
        dya = dya_ref[...]
        dhl = dya * (ga * sga)
        dz_ref[:, D_MODEL:2 * D_MODEL] = (dya * hl * (sga * (1.0 + ga * (1.0 - sga)))).astype(BF16)

        a_ext[0:tb, :] = a
        b = _shift_up(a_ext[...], 1)[0:tb, :]
        a_ext[tb:, :] = jnp.broadcast_to(a[0:1, :], (F32_SUBLANES, D_MODEL))
        d = dhl
        row8 = lax.broadcasted_iota(jnp.int32, (tb, D_MODEL), 0) % F32_SUBLANES
        for s in (1, 2, 4):
            m = row8 < F32_SUBLANES - s
            d = jnp.where(m, d + b * _tile_shift(d, -s), d)
            b = jnp.where(m, b * _tile_shift(b, -s), b)
        b_s[...] = b
        d_s[...] = d

        def step(k, cr):
            sl = pl.ds(pl.multiple_of((groups - 1 - k) * F32_SUBLANES, F32_SUBLANES), F32_SUBLANES)
            gb_ = d_s[sl, :] + b_s[sl, :] * cr
            g_s[sl, :] = gb_
            return jnp.broadcast_to(gb_[0:1, :], (F32_SUBLANES, D_MODEL))

        g_carry[...] = lax.fori_loop(0, groups, step, g_carry[...], unroll=4)
        rider(1)
        gsc = g_s[...]
        da = gsc * _shift_down(hl_ext[...], 1)[F32_SUBLANES:, :]
        dmult = gsc * (ig * xc)
        dig = gsc * (mult * xc)
        dxc = gsc * (mult * ig)
        dlog_a = da * a - (a * a) * dmult / mult
        dr = dlog_a * (-LRU_C * sp)
        vec_ref[_bag_row("lru_lambda"), :] += jnp.sum(dlog_a * (-LRU_C * r), axis=0, keepdims=True)
        dpa = dr * (r * (1.0 - r))
        dpx = dig * (ig * (1.0 - ig))
        vec_ref[_bag_row("lru_b_a"), :] += jnp.sum(dpa, axis=0, keepdims=True)
        vec_ref[_bag_row("lru_b_x"), :] += jnp.sum(dpx, axis=0, keepdims=True)
        back = []
        for h in range(LRU_HEADS):
            cols = slice(h * HEAD_DIM, (h + 1) * HEAD_DIM)
            xh = xc[:, cols].astype(BF16)
            dpa_h = dpa[:, cols].astype(BF16)
            dpx_h = dpx[:, cols].astype(BF16)
            mat_ref[mat_rows("lru_w_a", h), :] += _dot_tn(xh, dpa_h)
            mat_ref[mat_rows("lru_w_x", h), :] += _dot_tn(xh, dpx_h)
            back.append(_dot_nt(dpa_h, wa_ref[h]) + _dot_nt(dpx_h, wx_ref[h]))
        dxc = dxc + jnp.concatenate(back, axis=1)
        vec_ref[_bag_row("conv_b"), :] += jnp.sum(dxc, axis=0, keepdims=True)
        for k in range(CONV_WIDTH):
            tap = _shift_down(ea, CONV_WIDTH - 1 - k)[CONV_HIST:, :] if k < CONV_WIDTH - 1 else ea[CONV_HIST:, :]
            vec_ref[_bag_row("conv_w", k), :] += jnp.sum(dxc * tap, axis=0, keepdims=True)
        dxc_ext[0:tb, :] = dxc
        ed = dxc_ext[...]
        dxa = ed * cw_ref[3:4, :]
        dxa = dxa + _shift_up(ed, 1) * cw_ref[2:3, :]
        dxa = dxa + _shift_up(ed, 2) * cw_ref[1:2, :]
        dxa = dxa + _shift_up(ed, 3) * cw_ref[0:1, :]
        dz_ref[:, 0:D_MODEL] = dxa[0:tb, :].astype(BF16)
        dxc_ext[tb:, :] = dxc[0:CONV_HIST, :]

        pos = blk * tb + lax.broadcasted_iota(jnp.int32, (tb, POOL_GROUP_DIM), 0)
        diff = _pool_diff(eb, pos)
        rider(2)
        ypre = _pool_mix(diff, pw_ref)
        ps = ps_ref[...]
        gb = gb_ref[...]
        sgb = _sigmoid(gb)
        dyb = dyb_ref[...]
        dyp = dyb * (gb * sgb)
        dz_ref[:, 2 * D_MODEL + POOL_WIDTH:3 * D_MODEL] = (
            dyb * (ypre * ps) * (sgb * (1.0 + gb * (1.0 - sgb)))).astype(BF16)
        vec_ref[_bag_row("pool_scale"), 0:POOL_WIDTH] += jnp.sum(dyp * ypre, axis=0, keepdims=True)
        dypre = dyp * ps
        for g, k in enumerate(POOL_WINDOWS):
            cols = slice(g * POOL_GROUP_DIM, (g + 1) * POOL_GROUP_DIM)
            dyg = dypre[:, cols].astype(BF16)
            mat_ref[mat_rows("pool_w", g), :] += _dot_tn(diff[g].astype(BF16), dyg)
            ddiff = _dot_nt(dyg, pw_ref[g])
            count = jnp.minimum(pos + 1, k).astype(F32)
            dwin = ddiff / count
            dwin_ext[0:tb, cols] = dwin
            s = dwin_ext[:, cols]
            for step_ in range(g + 1):
                s = s + _shift_up(s, 2 ** step_)
            dz_ref[:, 2 * D_MODEL + g * POOL_GROUP_DIM:2 * D_MODEL + (g + 1) * POOL_GROUP_DIM] = (
                s[0:tb, :] - ddiff).astype(BF16)
            dwin_ext[tb:, cols] = dwin[0:POOL_HIST, :]

        dz_ref[:, 3 * D_MODEL:] = dzm_ref[...]

        @pl.when(i == nb - 1)
        def _():
            row = _bag_row("lru_lambda")
            vec_ref[row, :] = vec_ref[row, :] * (-_sigmoid(-lam))
            for k in range(nr):
                grads[2 * k + 1][...] = grads[2 * k][...].astype(BF16)

    rev = lambda i: (nb - 1 - i, 0)
    fixed = lambda i: (0, 0)

    def prev(rows, col):
        per = tb // rows
        return lambda i: (jnp.maximum((nb - 1 - i) * per - 1, 0), col)

    in_specs = [pl.BlockSpec((tb, D_MODEL), lambda i: (nb - 1 - i, 0)),
                pl.BlockSpec((CONV_HIST, D_MODEL), prev(CONV_HIST, 0)),
                pl.BlockSpec((tb, D_MODEL), lambda i: (nb - 1 - i, 1)),
                pl.BlockSpec((tb, POOL_WIDTH), lambda i: (nb - 1 - i, 4)),
                pl.BlockSpec((POOL_HIST, POOL_WIDTH), prev(POOL_HIST, 4)),
                pl.BlockSpec((tb, POOL_WIDTH), lambda i: (nb - 1 - i, 5)),
                pl.BlockSpec((tb, D_MODEL), rev),
                pl.BlockSpec((F32_SUBLANES, D_MODEL), prev(F32_SUBLANES, 0)),
                pl.BlockSpec((tb, D_MODEL), rev), pl.BlockSpec((tb, POOL_WIDTH), rev),
                pl.BlockSpec((tb, 2 * D_MODEL), rev)] + _branch_specs(tb, rev, fixed) + [
                    pl.BlockSpec((VEC_BAG_ROWS, D_MODEL), fixed)]
    vec_at = len(in_specs) - 1
    out_shape = [jax.ShapeDtypeStruct((t, IN_COLS), BF16), jax.ShapeDtypeStruct((VEC_BAG_ROWS, D_MODEL), F32),
                 jax.ShapeDtypeStruct((MAT_BAG_ROWS, HEAD_DIM), F32)]
    out_specs = [pl.BlockSpec((tb, IN_COLS), rev), pl.BlockSpec((VEC_BAG_ROWS, D_MODEL), fixed),
                 pl.BlockSpec((MAT_BAG_ROWS, HEAD_DIM), fixed)]
    for lhs, rhs in riders:
        in_specs += [pl.BlockSpec((tb, lhs.shape[1]), rev), pl.BlockSpec((tb, rhs.shape[1]), rev)]
        grad = (lhs.shape[1], rhs.shape[1])
        out_shape += [jax.ShapeDtypeStruct(grad, F32), jax.ShapeDtypeStruct(grad, BF16)]
        out_specs += [pl.BlockSpec(grad, fixed)] * 2
    scratch = [pltpu.VMEM((tb + CONV_HIST, D_MODEL), F32), pltpu.VMEM((tb + POOL_HIST, POOL_WIDTH), F32),
               pltpu.VMEM((tb + F32_SUBLANES, D_MODEL), F32), pltpu.VMEM((tb + F32_SUBLANES, D_MODEL), F32),
               pltpu.VMEM((tb + CONV_HIST, D_MODEL), F32), pltpu.VMEM((tb + POOL_HIST, POOL_WIDTH), F32),
               pltpu.VMEM((F32_SUBLANES, D_MODEL), F32),
               pltpu.VMEM((tb, D_MODEL), F32), pltpu.VMEM((tb, D_MODEL), F32), pltpu.VMEM((tb, D_MODEL), F32)]
    return pl.pallas_call(
        body, name="branches_bwd", out_shape=tuple(out_shape), grid=(nb,), in_specs=in_specs,
        out_specs=tuple(out_specs), scratch_shapes=scratch, input_output_aliases={vec_at: 1},
        compiler_params=pltpu.CompilerParams(dimension_semantics=("arbitrary",),
                                             vmem_limit_bytes=VMEM_LIMIT_BYTES),
    )(z, z, z, z, z, z, hl, hl, dya, dyb, dzm, *weights, vec_bag, *[a for pair in riders for a in pair])


def _merge_head(x2d, ya, yb, z, p2d, tgt, w_pl, w_pp, w_out, w_pg, w_pe, g2, gf, tb):
    t = x2d.shape[0]
    p_dim = p2d.shape[1]

    def body(x_ref, ya_ref, yb_ref, ma_ref, mb_ref, p_ref, t_ref, wpl_ref, wpp_ref, wout_ref, wpg_ref, wpe_ref,
             g2_ref, gf_ref,
             bag_ref, dxr_ref, dya_ref, dyb_ref, dzm_ref,
             mg_ref, do_ref, hn_ref, dgp_ref, dpe_ref, da_ref, dbm_ref, pbf_ref):
        @pl.when(pl.program_id(0) == 0)
        def _():
            bag_ref[...] = jnp.zeros_like(bag_ref)

        a_ = _dot(ya_ref[...], wpl_ref[...])
        bm = _dot(yb_ref[...], wpp_ref[...])
        sa = _sigmoid(ma_ref[...])
        sb = _sigmoid(mb_ref[...])
        mg = (sa * a_ + sb * bm).astype(BF16)
        mg_ref[...] = mg
        x1 = x_ref[...] + _dot(mg, wout_ref[...])
        xn2, r2 = _rms(x1)
        g2 = g2_ref[...]
        hn = (xn2 * g2).astype(BF16)
        hn_ref[...] = hn
        gate = _sigmoid(_dot(hn, wpg_ref[...]))
        pbf = p_ref[...].astype(BF16)
        pbf_ref[...] = pbf
        pe = _dot(pbf, wpe_ref[...])
        x2 = x1 + gate * pe
        xn3, r3 = _rms(x2)
        gf = gf_ref[...]
        err = xn3 * gf - t_ref[...]
        bag_ref[_bag_rows("loss"), 0:128] += 0.5 * jnp.sum(jnp.mean(err * err, axis=-1))

        dy = err * (1.0 / D_MODEL)
        bag_ref[_bag_row("final_g"), :] += jnp.sum(dy * xn3, axis=0, keepdims=True)
        dx2 = _rms_bwd(dy * gf, xn3, r3)
        dpe_ref[...] = (dx2 * gate).astype(BF16)
        dgp = ((dx2 * pe) * (gate * (1.0 - gate))).astype(BF16)
        dgp_ref[...] = dgp
        dhn = _dot_nt(dgp, wpg_ref[...])
        bag_ref[_bag_row("ple_norm_g"), :] += jnp.sum(dhn * xn2, axis=0, keepdims=True)
        dx1 = dx2 + _rms_bwd(dhn * g2, xn2, r2)
        dxr_ref[...] = dx1
        do = dx1.astype(BF16)
        do_ref[...] = do
        dmg = _dot_nt(do, wout_ref[...])
        da = (dmg * sa).astype(BF16)
        dbm = (dmg * sb).astype(BF16)
        da_ref[...] = da
        dbm_ref[...] = dbm
        dzm_ref[:, 0:D_MODEL] = (dmg * a_ * (sa * (1.0 - sa))).astype(BF16)
        dzm_ref[:, D_MODEL:] = (dmg * bm * (sb * (1.0 - sb))).astype(BF16)
        dya_ref[...] = _dot_nt(da, wpl_ref[...])
        dyb_ref[...] = _dot_nt(dbm, wpp_ref[...])

    row = lambda i: (i, 0)
    fixed = lambda i: (0, 0)

    def resident(shape):
        return pl.BlockSpec(shape, fixed, pipeline_mode=pl.Buffered(1))

    tok = lambda width: pl.BlockSpec((tb, width), row)
    in_specs = [tok(D_MODEL), tok(D_MODEL), tok(POOL_WIDTH),
                pl.BlockSpec((tb, D_MODEL), lambda i: (i, 3)), pl.BlockSpec((tb, D_MODEL), lambda i: (i, 4)),
                tok(p_dim), tok(D_MODEL),
                resident((D_MODEL, D_MODEL)), resident((POOL_WIDTH, D_MODEL)), resident((D_MODEL, D_MODEL)),
                resident((D_MODEL, D_MODEL)), resident((p_dim, D_MODEL)),
                pl.BlockSpec((1, D_MODEL), fixed), pl.BlockSpec((1, D_MODEL), fixed)]
    bf = lambda width: jax.ShapeDtypeStruct((t, width), BF16)
    f32 = lambda width: jax.ShapeDtypeStruct((t, width), F32)
    out_shape = (jax.ShapeDtypeStruct((VEC_BAG_ROWS, D_MODEL), F32),
                 f32(D_MODEL), f32(D_MODEL), f32(POOL_WIDTH), bf(2 * D_MODEL),
                 bf(D_MODEL), bf(D_MODEL), bf(D_MODEL), bf(D_MODEL), bf(D_MODEL), bf(D_MODEL), bf(D_MODEL), bf(p_dim))
    out_specs = (pl.BlockSpec((VEC_BAG_ROWS, D_MODEL), fixed),
                 tok(D_MODEL), tok(D_MODEL), tok(POOL_WIDTH), tok(2 * D_MODEL),
                 tok(D_MODEL), tok(D_MODEL), tok(D_MODEL), tok(D_MODEL), tok(D_MODEL), tok(D_MODEL), tok(D_MODEL),
                 tok(p_dim))
    return pl.pallas_call(
        body, name="merge_head", out_shape=out_shape, grid=(t // tb,), in_specs=in_specs, out_specs=out_specs,
        compiler_params=pltpu.CompilerParams(dimension_semantics=("arbitrary",),
                                             vmem_limit_bytes=VMEM_LIMIT_BYTES),
    )(x2d, ya, yb, z, z, p2d, tgt, w_pl, w_pp, w_out, w_pg, w_pe, g2, gf)


def kernel(x, p, norm_g, w_in, conv_w, conv_b, lru_w_a, lru_b_a, lru_w_x, lru_b_x, lru_lambda, pool_w, pool_scale, w_proj_lru, w_proj_pool, w_out, ple_norm_g, w_ple_gate, w_ple_proj, final_g, loss_target, m_norm_g, m_w_in, m_conv_w, m_conv_b, m_lru_w_a, m_lru_b_a, m_lru_w_x, m_lru_b_x, m_lru_lambda, m_pool_w, m_pool_scale, m_w_proj_lru, m_w_proj_pool, m_w_out, m_ple_norm_g, m_w_ple_gate, m_w_ple_proj, m_final_g, v_norm_g, v_w_in, v_conv_w, v_conv_b, v_lru_w_a, v_lru_b_a, v_lru_w_x, v_lru_b_x, v_lru_lambda, v_pool_w, v_pool_scale, v_w_proj_lru, v_w_proj_pool, v_w_out, v_ple_norm_g, v_w_ple_gate, v_w_ple_proj, v_final_g):
    bsz, seq, _ = x.shape
    t = bsz * seq
    tb_mm = min(1024, seq)
    tb_seq = min(256, seq // 2) if seq >= 512 else seq
    x2d = x.reshape(t, D_MODEL)
    p2d = p.reshape(t, p.shape[-1])
    tgt = loss_target.reshape(t, D_MODEL)

    rest = [(w_proj_lru[0], 0), (w_proj_pool[0], 1), (w_out[0], 0), (w_ple_gate[0], 0), (w_ple_proj[0], 1)]
    z, h_bf, w_in_f, conv_w_f, *narrow = _in_proj_gather(
        x2d, norm_g, w_in[0], [(conv_w[0], 1, False)], tb_mm,
        [w for w, _ in rest] + [lru_w_a[0], lru_w_x[0], pool_w[0]])
    wa_bf, wx_bf, pw_bf = narrow[len(rest):]
    branch_w = (conv_w_f, conv_b, wa_bf, lru_b_a.reshape(1, D_MODEL), wx_bf, lru_b_x.reshape(1, D_MODEL),
                lru_lambda, pw_bf, pool_scale)

    ya, yb, hl, w_pl_f, w_pp_f, w_out_f, w_pg_f, w_pe_f = _branches_fwd(
        z, branch_w, seq, tb_seq, [(w16, axis, True) for w16, (_, axis) in zip(narrow, rest)])
    (vec_bag, dx_res, dya, dyb, dzm, mg_bf, do_bf, hn_bf, dgp_bf, dpe_bf, da_bf, dbm_bf, p_bf) = _merge_head(
        x2d, ya, yb, z, p2d, tgt, w_pl_f, w_pp_f, w_out_f, w_pg_f, w_pe_f, ple_norm_g, final_g.reshape(1, D_MODEL),
        tb_seq)
    dz, vec_bag, mat_bag, g_out, g_out16, g_pp, g_pp16, g_pe, g_pe16 = _branches_bwd(
        z, hl, dya, dyb, dzm, branch_w, vec_bag, seq, tb_seq, [(mg_bf, do_bf), (yb, dbm_bf), (p_bf, dpe_bf)])

    tb_dw = min(1024, seq)
    def row_pieces(g32, g16):
        pieces = (8, g32.shape[0] // 8, g32.shape[1])
        return g32.reshape(pieces), False, g16.reshape(pieces)

    def proj_grad(lhs, rhs, name):
        g32, g16 = _weight_grad(lhs, rhs, 1, tb_dw, name)
        return row_pieces(g32[0], g16[0])

    p_dim = p2d.shape[1]
    proj_parts = [proj_grad(ya, da_bf, "dw_proj_lru"), (g_pp, True, g_pp16), row_pieces(g_out, g_out16),
                  proj_grad(hn_bf, dgp_bf, "dw_ple_gate"), (g_pe, True, g_pe16)]
    nb_dw = t // tb_dw
    g_in, g_in16, r_pl, r_pp, r_out, r_pg, r_pe, vec_mine, mat_mine = _weight_grad(
        h_bf, dz, N_CHIPS, tb_dw, "dw_in",
        reduce=(proj_parts + [(vec_bag.reshape(8, VEC_BAG_ROWS // 8, D_MODEL), False, None),
                              (mat_bag.reshape(8, MAT_BAG_ROWS // 8, HEAD_DIM), False, None)],
                [BF16] * 5 + [F32] * 2,
                (0, 1, 2, nb_dw + nb_dw // 2, 3 * nb_dw + nb_dw // 2, N_CHIPS * nb_dw - 1)))
    pieces = (8, D_MODEL // 2, IN_COLS // N_CHIPS)
    nb_seq = t // tb_seq
    dx, g_g1, r_in, vec_sum, mat_sum, g_cw = _in_proj_bwd(
        dz, w_in_f, x2d, dx_res, norm_g, tb_seq,
        reduce=([(g_in.reshape(pieces), False, g_in16.reshape(pieces))], BF16,
                (0, 1, 2, 3 * nb_seq // 8, nb_seq - 1)),
        shards=[(vec_mine.reshape(VEC_BAG_ROWS // N_CHIPS, D_MODEL), 0, True),
                (mat_mine.reshape(MAT_BAG_ROWS // N_CHIPS, HEAD_DIM), 0, True)],
        take=(_bag_rows("conv_w"), D_MODEL // N_CHIPS))

    u_in = tuple(a[None] for a in _adamw(w_in[0], r_in.reshape(D_MODEL, IN_COLS // N_CHIPS), m_w_in[0], v_w_in[0],
                                         D_MODEL // 4, "adamw_w_in"))
    proj = [(w_proj_lru, r_pl, m_w_proj_lru, v_w_proj_lru), (w_proj_pool, r_pp, m_w_proj_pool, v_w_proj_pool),
            (w_out, r_out, m_w_out, v_w_out), (w_ple_gate, r_pg, m_w_ple_gate, v_w_ple_gate),
            (w_ple_proj, r_pe, m_w_ple_proj, v_w_ple_proj)]
    u_pl, u_pp, u_out, u_pg, u_pe = [tuple(a[None] for a in u) for u in _adamw_group(
        [(w[0], g.reshape(w.shape[1:]), m[0], v[0]) for w, g, m, v in proj], "adamw_proj")]

    small = [("norm_g", norm_g, m_norm_g, v_norm_g), ("conv_b", conv_b, m_conv_b, v_conv_b),
             ("lru_w_a", lru_w_a, m_lru_w_a, v_lru_w_a), ("lru_b_a", lru_b_a, m_lru_b_a, v_lru_b_a),
             ("lru_w_x", lru_w_x, m_lru_w_x, v_lru_w_x), ("lru_b_x", lru_b_x, m_lru_b_x, v_lru_b_x),
             ("lru_lambda", lru_lambda, m_lru_lambda, v_lru_lambda), ("pool_w", pool_w, m_pool_w, v_pool_w),
             ("pool_scale", pool_scale, m_pool_scale, v_pool_scale),
             ("ple_norm_g", ple_norm_g, m_ple_norm_g, v_ple_norm_g), ("final_g", final_g, m_final_g, v_final_g)]

    def view(a):
        return a.reshape(-1, a.shape[-1]) if a.ndim != 3 else a[0]

    flat = _adamw_replicated(vec_sum, mat_sum, g_g1, [(name,) + tuple(view(a) for a in arrs) for name, *arrs in small],
                             (conv_w[0], m_conv_w[0], v_conv_w[0], g_cw))
    u_small = {name: tuple(flat[4 * k + pick].reshape(arrs[0].shape) for pick in range(4))
               for k, (name, *arrs) in enumerate(small)}
    u_cw = tuple(a[None] for a in flat[4 * len(small):4 * len(small) + 4])

    loss = flat[-1].reshape(())
    grad_x = dx.reshape(bsz, seq, D_MODEL)

    def ordered(pick):
        s = {name: u[pick] for name, u in u_small.items()}
        return [s["norm_g"], u_in[pick], u_cw[pick], s["conv_b"], s["lru_w_a"], s["lru_b_a"], s["lru_w_x"], s["lru_b_x"],
                s["lru_lambda"], s["pool_w"], s["pool_scale"], u_pl[pick], u_pp[pick], u_out[pick], s["ple_norm_g"],
                u_pg[pick], u_pe[pick], s["final_g"]]

    return (loss, grad_x, *ordered(0), *ordered(1), *ordered(2), *ordered(3))
```

```python
import jax
import jax.numpy as jnp
from jax import lax
from jax.experimental import pallas as pl
from jax.experimental.pallas import tpu as pltpu

F32 = jnp.float32
BF16 = jnp.bfloat16
MESH = pl.DeviceIdType.MESH

D_MODEL = 1024
LRU_HEADS = 8
HEAD_DIM = 128
CONV_WIDTH = 4
LRU_C = 8.0
POOL_WIDTH = 512
POOL_WINDOWS = (2, 4, 8, 16)
POOL_GROUP_DIM = 128
IN_COLS = 5120
N_CHIPS = 4
EPS = 1e-6

ADAM_LR = 0.001
ADAM_B1 = 0.9
ADAM_B2 = 0.999
ADAM_EPS = 1e-08
ADAM_WD = 0.01
ADAM_STEP = 10

F32_SUBLANES = 8
CONV_HIST = 8
POOL_HIST = 16
VMEM_LIMIT_BYTES = 58 * 1024 * 1024
VEC_BAG_SLOTS = ("norm_g", "conv_w", "conv_b", "lru_b_a", "lru_b_x", "lru_lambda", "pool_scale", "ple_norm_g",
                 "final_g", "loss")
VEC_BAG_ROWS = 128
MAT_BAG_AT = {"lru_w_a": 0, "lru_w_x": LRU_HEADS * HEAD_DIM, "pool_w": 2 * LRU_HEADS * HEAD_DIM}
MAT_BAG_ROWS = 2 * LRU_HEADS * HEAD_DIM + len(POOL_WINDOWS) * POOL_GROUP_DIM


def _bag_row(name, k=0):
    at = F32_SUBLANES * VEC_BAG_SLOTS.index(name) + k
    return slice(at, at + 1)


def _bag_rows(name):
    at = F32_SUBLANES * VEC_BAG_SLOTS.index(name)
    return slice(at, at + F32_SUBLANES)


def _dot(a, b):
    return jnp.dot(a, b, preferred_element_type=F32)


def _dot_nt(a, b):
    return lax.dot_general(a, b, (((1,), (1,)), ((), ())), preferred_element_type=F32)


def _dot_tn(a, b):
    return lax.dot_general(a, b, (((0,), (0,)), ((), ())), preferred_element_type=F32)


def _sigmoid(v):
    return jax.nn.sigmoid(v)


def _softplus(v):
    return jnp.maximum(v, 0.0) + jnp.log1p(jnp.exp(-jnp.abs(v)))


def _place():
    return lax.axis_index("x"), lax.axis_index("y"), lax.axis_index("c")


GATHER_SEMS = 6


def _gather_shapes(shards):
    out_shape = []
    for arr, axis, _ in shards:
        r, cols = arr.shape
        out_shape.append(jax.ShapeDtypeStruct((N_CHIPS * r, cols) if axis == 0 else (r, N_CHIPS * cols), arr.dtype))
    n = len(shards)
    sems = [pltpu.SemaphoreType.DMA((n * GATHER_SEMS,)), pltpu.SemaphoreType.DMA((n * GATHER_SEMS,)),
            pltpu.SemaphoreType.DMA((n,))]
    return out_shape, sems


def _gather_steps(shards, ins, outs, send_sems, recv_sems, local_sems):
    n = len(shards)
    x, y, c = _place()
    me, sibling = (x, y, c), (x, y, 1 - c)
    chips = [(x, 1 - y), (1 - x, y), (1 - x, 1 - y)]

    def region(k, cx, cy, hc):
        (r, cols), axis = shards[k][0].shape, shards[k][1]
        j = 2 * cx + cy
        if axis == 0:
            if hc is None:
                return outs[k].at[pl.ds(j * r, r), :]
            return outs[k].at[pl.ds(j * r + hc * (r // 2), r // 2), :]
        if hc is None:
            return outs[k].at[:, pl.ds(j * cols, cols)]
        return outs[k].at[pl.ds(hc * (r // 2), r // 2), pl.ds(j * cols, cols)]

    def remote(k, sem, block, to, src=None):
        dst = region(k, *block)
        return pltpu.make_async_remote_copy(
            src_ref=dst if src is None else src, dst_ref=dst,
            send_sem=send_sems.at[k * GATHER_SEMS + sem], recv_sem=recv_sems.at[k * GATHER_SEMS + sem],
            device_id=to, device_id_type=MESH)

    def first(k, idx):
        r, split = shards[k][0].shape[0], shards[k][2]
        src = ins[k].at[pl.ds(c * (r // 2), r // 2), :] if split else ins[k]
        return remote(k, idx, (x, y, c if split else None), (*chips[idx], c), src=src)

    def relay(k):
        src_chip = (jnp.bitwise_xor(x, 1 - c), jnp.bitwise_xor(y, c))
        dst_chip = (jnp.bitwise_xor(x, c), jnp.bitwise_xor(y, 1 - c))
        return remote(k, 2, (*src_chip, c), (*dst_chip, c))

    def passed(k, idx):
        return remote(k, 3 + idx, (*chips[idx], c), sibling)

    def mine(k):
        return pltpu.make_async_copy(ins[k], region(k, x, y, None), local_sems.at[k])

    def start():
        for k in range(n):
            mine(k).start()
            for idx in range(2 if shards[k][2] else 3):
                first(k, idx).start()

    def relay_on():
        for k in range(n):
            split = shards[k][2]
            for idx in range(2):
                remote(k, idx, (*chips[idx], c if split else None), me).wait_recv()
            if split:
                relay(k).start()
                passed(k, 0).start()
                passed(k, 1).start()

    def finish():
        for k in range(n):
            split = shards[k][2]
            remote(k, 2, (*chips[2], c if split else None), me).wait_recv()
            if split:
                passed(k, 2).start()
        for k in range(n):
            if shards[k][2]:
                for idx in range(3):
                    remote(k, 3 + idx, (*chips[idx], 1 - c), me).wait_recv()
        for k in range(n):
            if shards[k][2]:
                for cp in (first(k, 0), first(k, 1), relay(k), passed(k, 0), passed(k, 1), passed(k, 2)):
                    cp.wait_send()
            else:
                for idx in range(3):
                    first(k, idx).wait_send()
            mine(k).wait()

    return start, relay_on, finish


RS_ADD_ROWS = (64, 32, 16, 8)


N_DEV = 2 * N_CHIPS


def _all_reduce_scratch(shape):
    return [pltpu.VMEM((N_DEV,) + tuple(shape), F32), pltpu.SemaphoreType.DMA((N_DEV - 1,)),
            pltpu.SemaphoreType.DMA((N_DEV - 1,))]


def _all_reduce_tile(v_ref, o_ref, slots, send_sems, recv_sems):
    flips = [(dx, dy, dc) for dx in (0, 1) for dy in (0, 1) for dc in (0, 1)][1:]
    x, y, c = _place()
    mine = 4 * x + 2 * y + c

    def copy(k, to_flip, slot):
        dx, dy, dc = to_flip
        peer = (jnp.bitwise_xor(x, dx), jnp.bitwise_xor(y, dy), jnp.bitwise_xor(c, dc))
        return pltpu.make_async_remote_copy(
            src_ref=v_ref, dst_ref=slots.at[slot], send_sem=send_sems.at[k], recv_sem=recv_sems.at[k],
            device_id=peer, device_id_type=MESH)

    sends = [copy(k, flip, mine) for k, flip in enumerate(flips)]
    for cp in sends:
        cp.start()
    slots[mine] = v_ref[...]
    for k, (dx, dy, dc) in enumerate(flips):
        copy(k, (dx, dy, dc), jnp.bitwise_xor(mine, 4 * dx + 2 * dy + dc)).wait_recv()
    total = slots[0]
    for d in range(1, N_DEV):
        total = total + slots[d]
    o_ref[...] = total
    for cp in sends:
        cp.wait_send()


RS_SEMS = 8
RS_LOCAL_SEMS = 5


def _rs_piece_shape(part):
    arr, cols = part[0], part[1]
    return (arr.shape[0] // 2, arr.shape[1] // N_CHIPS) if cols else tuple(arr.shape[1:])


def _rs_operands(parts):
    return [p[0] for p in parts] + [p[0] if p[2] is None else p[2] for p in parts]


def _rs_wires(parts, wire):
    return list(wire) if isinstance(wire, (list, tuple)) else [wire] * len(parts)


def _rs_shapes(parts, wire):
    n = len(parts)
    shapes = [_rs_piece_shape(p) for p in parts]
    out_shape = [jax.ShapeDtypeStruct((2,) + s, F32) for s in shapes]
    scratch = []
    for lead, kind in ((N_CHIPS, "f32"), (N_CHIPS, "narrow"), (N_CHIPS, "wire"), (None, "f32"), (N_CHIPS, "wire")):
        for s, p, w in zip(shapes, parts, _rs_wires(parts, wire)):
            dtype = {"f32": F32, "narrow": F32 if p[2] is None else p[2].dtype, "wire": w}[kind]
            scratch.append(pltpu.VMEM(s if lead is None else (lead,) + s, dtype))
    scratch += [pltpu.SemaphoreType.DMA((n * RS_SEMS,)), pltpu.SemaphoreType.DMA((n * RS_SEMS,)),
                pltpu.SemaphoreType.DMA((n * RS_LOCAL_SEMS,))]
    return out_shape, scratch


def _rs_steps(parts, ins, outs, scratch):
    n = len(parts)
    own, sib, got, fin, snd = (scratch[k * n:(k + 1) * n] for k in range(5))
    send_sems, recv_sems, local_sems = scratch[5 * n:]
    shapes = [_rs_piece_shape(p) for p in parts]
    x, y, c = _place()
    j_me = 2 * x + y
    me, sibling = (x, y, c), (x, y, 1 - c)

    def piece(a, jj, core, narrow=False):
        ref = ins[n + a] if narrow else ins[a]
        if parts[a][1]:
            r, cl = shapes[a]
            return ref.at[pl.ds(core * r, r), pl.ds(jj * cl, cl)]
        return ref.at[2 * jj + core]

    def remote(a, sem, src, dst, to):
        return pltpu.make_async_remote_copy(
            src_ref=src, dst_ref=dst, send_sem=send_sems.at[a * RS_SEMS + sem],
            recv_sem=recv_sems.at[a * RS_SEMS + sem], device_id=to, device_id_type=MESH)

    def rows_loop(a, fn):
        r = shapes[a][0]
        step = max(s for s in RS_ADD_ROWS if r % s == 0)

        def it(i, carry):
            fn(pl.ds(pl.multiple_of(i * step, step), step))
            return carry

        lax.fori_loop(0, r // step, it, 0)

    def load(a, jj):
        return pltpu.make_async_copy(piece(a, jj, c), own[a].at[jj], local_sems.at[a * RS_LOCAL_SEMS + jj])

    def to_sibling(a, jj):
        return remote(a, jj, piece(a, jj, 1 - c, narrow=True), sib[a].at[jj], sibling)

    near = (jnp.bitwise_xor(x, 1 - c), jnp.bitwise_xor(y, c))
    far = (jnp.bitwise_xor(x, c), jnp.bitwise_xor(y, 1 - c))
    diag = (1 - x, 1 - y)
    FROM_NEAR, FROM_FAR, FEED = 0, 1, 2

    def chip_of(chip):
        return 2 * chip[0] + chip[1]

    def feed(a):
        return remote(a, 4, snd[a].at[chip_of(diag)], got[a].at[FEED], (*near, c))

    def to_near(a):
        return remote(a, 5, snd[a].at[chip_of(near)], got[a].at[FROM_NEAR], (*near, c))

    def to_far(a):
        return remote(a, 6, snd[a].at[chip_of(far)], got[a].at[FROM_FAR], (*far, c))

    def store(a):
        return pltpu.make_async_copy(fin[a], outs[a].at[c], local_sems.at[a * RS_LOCAL_SEMS + 4])

    def result_to_sibling(a):
        return remote(a, 7, fin[a], outs[a].at[c], sibling)

    def exchange():
        for a in range(n):
            for jj in range(N_CHIPS):
                load(a, jj).start()
                to_sibling(a, jj).start()

    def chip_sums():
        for a in range(n):
            for jj in range(N_CHIPS):
                load(a, jj).wait()
                remote(a, jj, sib[a].at[jj], sib[a].at[jj], me).wait_recv()

                def add(sl, a=a, jj=jj):
                    q = own[a][jj, sl, :] + sib[a][jj, sl, :].astype(F32)
                    own[a][jj, sl, :] = q
                    snd[a][jj, sl, :] = q.astype(snd[a].dtype)

                rows_loop(a, add)
        for a in range(n):
            feed(a).start()
        for a in range(n):
            to_near(a).start()

    def relay():
        for a in range(n):
            remote(a, 4, got[a].at[FEED], got[a].at[FEED], me).wait_recv()

            def add(sl, a=a):
                pair = own[a][chip_of(far), sl, :] + got[a][FEED, sl, :].astype(F32)
                snd[a][chip_of(far), sl, :] = pair.astype(snd[a].dtype)

            rows_loop(a, add)
            to_far(a).start()

    def totals():
        for a in range(n):
            remote(a, 5, got[a].at[FROM_NEAR], got[a].at[FROM_NEAR], me).wait_recv()
            remote(a, 6, got[a].at[FROM_FAR], got[a].at[FROM_FAR], me).wait_recv()

            def total(sl, a=a):
                fin[a][sl, :] = (own[a][j_me, sl, :] + got[a][FROM_NEAR, sl, :].astype(F32)) + (
                    got[a][FROM_FAR, sl, :].astype(F32))

            rows_loop(a, total)
            store(a).start()
            result_to_sibling(a).start()

    def finish():
        for a in range(n):
            remote(a, 7, outs[a].at[1 - c], outs[a].at[1 - c], me).wait_recv()
        for a in range(n):
            for jj in range(N_CHIPS):
                to_sibling(a, jj).wait_send()
            for cp in (feed(a), to_near(a), to_far(a), result_to_sibling(a)):
                cp.wait_send()
            store(a).wait()

    return exchange, chip_sums, relay, totals, finish


def _rms(x):
    r = lax.rsqrt(jnp.mean(x * x, axis=-1, keepdims=True) + EPS)
    return x * r, r


def _rms_bwd(dxn, xn, r):
    return r * (dxn - xn * jnp.mean(dxn * xn, axis=-1, keepdims=True))


def _in_proj_gather(x2d, norm_g, w_in_sh, shards, tb, casts):
    t = x2d.shape[0]
    nb = t // tb
    cols = IN_COLS // N_CHIPS
    half = D_MODEL // 2
    n = len(shards)
    nc = len(casts)

    def body(x_ref, g_ref, win_ref, *refs):
        ins, cast_ins = refs[:n], refs[n:n + nc]
        z_ref, h_ref, wfull_ref = refs[n + nc:n + nc + 3]
        outs, cast_outs = refs[n + nc + 3:2 * n + nc + 3], refs[2 * n + nc + 3:2 * (n + nc) + 3]
        scratch = refs[2 * (n + nc) + 3:]
        wv, h_all, send_sems, recv_sems, local_sems, w_send, w_recv, w_local, stage = scratch[:9]
        wide, narrow, cast_sems = scratch[9:9 + nc], scratch[9 + nc:9 + 2 * nc], scratch[9 + 2 * nc]
        s, i = pl.program_id(0), pl.program_id(1)
        x, y, c = _place()
        me, sibling = (x, y, c), (x, y, 1 - c)
        chips = [(x, 1 - y), (1 - x, y), (1 - x, 1 - y)]

        def w_half(cx, cy, hc):
            return wv.at[2 * cx + cy, pl.ds(hc * half, half), :]

        def w_remote(sem, block, to, src=None):
            dst = w_half(*block)
            return pltpu.make_async_remote_copy(
                src_ref=dst if src is None else src, dst_ref=dst, send_sem=w_send.at[sem],
                recv_sem=w_recv.at[sem], device_id=to, device_id_type=MESH)

        def w_first(idx):
            return w_remote(idx, (x, y, c), (*chips[idx], c))

        def w_relay():
            src_chip = (jnp.bitwise_xor(x, 1 - c), jnp.bitwise_xor(y, c))
            dst_chip = (jnp.bitwise_xor(x, c), jnp.bitwise_xor(y, 1 - c))
            return w_remote(2, (*src_chip, c), (*dst_chip, c))

        def w_pass(idx):
            return w_remote(3 + idx, (*chips[idx], c), sibling)

        def w_store(k, cx, cy):
            jj = 2 * cx + cy
            return pltpu.make_async_copy(wv.at[jj], wfull_ref.at[:, pl.ds(jj * cols, cols)], w_local.at[k])

        start_rest, relay_rest, finish_rest = _gather_steps(shards, ins, outs, send_sems, recv_sems, local_sems)

        def own(k, hc):
            return pltpu.make_async_copy(win_ref.at[pl.ds(pl.multiple_of(hc * half, half), half), :], stage.at[k],
                                         w_local.at[4 + 2 * k])

        def round_own(k, hc):
            own(k, hc).wait()
            wv[2 * x + y, pl.ds(pl.multiple_of(hc * half, half), half), :] = stage[k].astype(BF16)

        wide_in = [pltpu.make_async_copy(cast_ins[k], wide[k], cast_sems.at[k]) for k in range(nc)]
        narrow_out = [pltpu.make_async_copy(narrow[k], cast_outs[k], cast_sems.at[nc + k]) for k in range(nc)]

        @pl.when((s == 0) & (i == 0))
        def _():
            own(0, c).start()
            own(1, 1 - c).start()
            for cp in wide_in:
                cp.start()
            round_own(0, c)
            w_first(0).start()
            w_first(1).start()
            start_rest()
            round_own(1, 1 - c)
            w_store(0, x, y).start()

        @pl.when((s == 1) & (i == 0))
        def _():
            for k in range(nc):
                wide_in[k].wait()
                narrow[k][...] = wide[k][...].astype(BF16)
                narrow_out[k].start()
            w_remote(0, (*chips[0], c), me).wait_recv()
            w_remote(1, (*chips[1], c), me).wait_recv()
            w_relay().start()
            w_pass(0).start()
            w_pass(1).start()
            w_remote(3, (*chips[0], 1 - c), me).wait_recv()
            w_store(1, *chips[0]).start()

        @pl.when((s == 2) & (i == 0))
        def _():
            w_remote(4, (*chips[1], 1 - c), me).wait_recv()
            w_store(2, *chips[1]).start()

        @pl.when((s == 3) & (i == 0))
        def _():
            w_remote(2, (*chips[2], c), me).wait_recv()
            w_pass(2).start()
            w_remote(5, (*chips[2], 1 - c), me).wait_recv()
            w_store(3, *chips[2]).start()

        keep_h = pltpu.make_async_copy(h_all.at[i], h_ref.at[pl.ds(pl.multiple_of(i * tb, tb), tb), :], w_local.at[5])

        @pl.when(s == 0)
        def _():
            xn, _ = _rms(x_ref[...])
            h_all[i] = (xn * g_ref[...]).astype(BF16)
            keep_h.start()

        z_ref[...] = _dot(h_all[i], wv[jnp.bitwise_xor(2 * x + y, s)])
        pl.when(s == 0)(keep_h.wait)

        @pl.when((s == N_CHIPS - 1) & (i == nb - 1))
        def _():
            relay_rest()
            finish_rest()
            for cp in (w_first(0), w_first(1), w_relay(), w_pass(0), w_pass(1), w_pass(2)):
                cp.wait_send()
            w_store(0, x, y).wait()
            for idx in range(3):
                w_store(idx + 1, *chips[idx]).wait()
            for cp in narrow_out:
                cp.wait()

    rest_shape, rest_sems = _gather_shapes(shards)
    out_shape = [jax.ShapeDtypeStruct((t, IN_COLS), F32), jax.ShapeDtypeStruct((t, D_MODEL), BF16),
                 jax.ShapeDtypeStruct((D_MODEL, IN_COLS), BF16)] + rest_shape
    out_shape += [jax.ShapeDtypeStruct(a.shape, BF16) for a in casts]
    any_spec = pl.BlockSpec(memory_space=pl.ANY)

    def z_map(s, i):
        return (i, jnp.bitwise_xor(2 * lax.axis_index("x") + lax.axis_index("y"), s))

    return pl.pallas_call(
        body, name="in_proj", out_shape=tuple(out_shape),
        grid=(N_CHIPS, nb),
        in_specs=[pl.BlockSpec((tb, D_MODEL), lambda s, i: (jnp.where(s == 0, i, nb - 1), 0)),
                  pl.BlockSpec((1, D_MODEL), lambda s, i: (0, 0)), any_spec] + [any_spec] * (n + nc),
        out_specs=tuple([pl.BlockSpec((tb, cols), z_map), any_spec, any_spec] + [any_spec] * (n + nc)),
        scratch_shapes=[pltpu.VMEM((N_CHIPS, D_MODEL, cols), BF16), pltpu.VMEM((nb, tb, D_MODEL), BF16)] + rest_sems + [
            pltpu.SemaphoreType.DMA((GATHER_SEMS,)), pltpu.SemaphoreType.DMA((GATHER_SEMS,)),
            pltpu.SemaphoreType.DMA((N_CHIPS + 3,)), pltpu.VMEM((2, half, cols), F32)]
        + [pltpu.VMEM(a.shape, F32) for a in casts] + [pltpu.VMEM(a.shape, BF16) for a in casts]
        + [pltpu.SemaphoreType.DMA((2 * nc,))],
        compiler_params=pltpu.CompilerParams(dimension_semantics=("arbitrary", "arbitrary"),
                                             vmem_limit_bytes=VMEM_LIMIT_BYTES),
    )(x2d, norm_g, w_in_sh, *[sh[0] for sh in shards], *casts)


def _in_proj_bwd(dz, w_in, x2d, dx_res, norm_g, tb, reduce, shards, take):
    t = x2d.shape[0]
    nb = t // tb
    parts, wire, steps = reduce
    n = len(parts)
    k = len(shards)
    take_rows, take_width = take

    def body(dz_ref, w_ref, x_ref, dres_ref, g_ref, *refs):
        at = 2 * n + k
        dx_ref, dg_ref = refs[at:at + 2]
        rs_outs, g_outs = refs[at + 2:at + 2 + n], refs[at + 2 + n:at + 2 + n + k]
        cut_ref = refs[at + 2 + n + k]
        scratch = refs[at + 3 + n + k:]
        rs_scr, g_sems, dg_acc, ar_scr, cut_sem = scratch[:-8], scratch[-8:-5], scratch[-5], scratch[-4:-1], scratch[-1]
        rs = _rs_steps(parts, refs[:2 * n], rs_outs, rs_scr)
        for step, when in zip(rs[:-1], steps):
            pl.when(pl.program_id(0) == when)(step)
        gather = _gather_steps(shards, refs[2 * n:at], g_outs, *g_sems)
        for step, when in zip(gather, (0, nb // 2, nb - 1)):
            pl.when(pl.program_id(0) == when)(step)

        @pl.when(pl.program_id(0) == 0)
        def _():
            dg_acc[...] = jnp.zeros_like(dg_acc)

        xn, r = _rms(x_ref[...])
        g = g_ref[...]
        dh = _dot_nt(dz_ref[...], w_ref[...])
        dg_acc[0:1, :] += jnp.sum(dh * xn, axis=0, keepdims=True)
        dx_ref[...] = dres_ref[...] + _rms_bwd(dh * g, xn, r)

        @pl.when(pl.program_id(0) == nb - 1)
        def _():
            x, y, _ = _place()
            mine = pl.ds(pl.multiple_of((2 * x + y) * take_width, take_width), take_width)
            cut = pltpu.make_async_copy(g_outs[0].at[take_rows, mine], cut_ref, cut_sem)
            cut.start()
            _all_reduce_tile(dg_acc, dg_ref, *ar_scr)
            rs[-1]()
            cut.wait()

    row = lambda i: (i, 0)
    fixed = lambda i: (0, 0)
    rs_shape, rs_scratch = _rs_shapes(parts, wire)
    g_shape, g_sems = _gather_shapes(shards)
    any_spec = pl.BlockSpec(memory_space=pl.ANY)
    cut_shape = jax.ShapeDtypeStruct((take_rows.stop - take_rows.start, take_width), F32)
    return pl.pallas_call(
        body, name="in_proj_bwd",
        out_shape=tuple([jax.ShapeDtypeStruct((t, D_MODEL), F32), jax.ShapeDtypeStruct((F32_SUBLANES, D_MODEL), F32)]
                        + rs_shape + g_shape + [cut_shape]),
        grid=(nb,),
        in_specs=[pl.BlockSpec((tb, IN_COLS), row),
                  pl.BlockSpec((D_MODEL, IN_COLS), fixed, pipeline_mode=pl.Buffered(1)),
                  pl.BlockSpec((tb, D_MODEL), row), pl.BlockSpec((tb, D_MODEL), row),
                  pl.BlockSpec((1, D_MODEL), fixed)] + [any_spec] * (2 * n + k),
        out_specs=tuple([pl.BlockSpec((tb, D_MODEL), row), pl.BlockSpec((F32_SUBLANES, D_MODEL), fixed)]
                        + [any_spec] * (n + k + 1)),
        scratch_shapes=rs_scratch + g_sems + [pltpu.VMEM((F32_SUBLANES, D_MODEL), F32)] + _all_reduce_scratch(
            (F32_SUBLANES, D_MODEL)) + [pltpu.SemaphoreType.DMA(())],
        compiler_params=pltpu.CompilerParams(dimension_semantics=("arbitrary",),
                                             vmem_limit_bytes=VMEM_LIMIT_BYTES),
    )(dz, w_in, x2d, dx_res, norm_g, *_rs_operands(parts), *[sh[0] for sh in shards])


def _weight_grad(lhs, rhs, n_chunks, tb, name, reduce=None):
    t, k = lhs.shape
    nc = rhs.shape[1] // n_chunks
    nb = t // tb
    parts, wire, steps = reduce if reduce is not None else ([], F32, ())
    n = len(parts)

    def body(l_ref, r_ref, *refs):
        o_ref, o16_ref = refs[2 * n:2 * n + 2]
        if n:
            at = pl.program_id(0) * nb + pl.program_id(1)
            rs = _rs_steps(parts, refs[:2 * n], refs[2 * n + 2:3 * n + 2], refs[3 * n + 2:])
            for step, when in zip(rs, steps):
                pl.when(at == when)(step)

        @pl.when(pl.program_id(1) == 0)
        def _():
            o_ref[...] = jnp.zeros_like(o_ref)

        o_ref[...] += _dot_tn(l_ref[...], r_ref[...])

        @pl.when(pl.program_id(1) == nb - 1)
        def _():
            o16_ref[...] = o_ref[...].astype(BF16)

    rs_shape, rs_scratch = _rs_shapes(parts, wire) if n else ([], [])
    any_spec = pl.BlockSpec(memory_space=pl.ANY)
    chunk = pl.BlockSpec((None, k, nc), lambda j, i: (j, 0, 0))
    return pl.pallas_call(
        body, name=name,
        out_shape=tuple([jax.ShapeDtypeStruct((n_chunks, k, nc), F32), jax.ShapeDtypeStruct((n_chunks, k, nc), BF16)]
                        + rs_shape),
        grid=(n_chunks, nb),
        in_specs=[pl.BlockSpec((tb, k), lambda j, i: (i, 0)), pl.BlockSpec((tb, nc), lambda j, i: (i, j))]
        + [any_spec] * (2 * n),
        out_specs=tuple([chunk, chunk] + [any_spec] * n),
        scratch_shapes=rs_scratch,
        compiler_params=pltpu.CompilerParams(dimension_semantics=("arbitrary", "arbitrary"),
                                             vmem_limit_bytes=VMEM_LIMIT_BYTES),
    )(lhs, rhs, *_rs_operands(parts))


def _adam_update(w, g, m, v):
    m_ = ADAM_B1 * m + (1.0 - ADAM_B1) * g
    v_ = ADAM_B2 * v + (1.0 - ADAM_B2) * jnp.square(g)
    m_hat = m_ / (1.0 - ADAM_B1 ** ADAM_STEP)
    v_hat = v_ / (1.0 - ADAM_B2 ** ADAM_STEP)
    return -ADAM_LR * (m_hat / (jnp.sqrt(v_hat) + ADAM_EPS) + ADAM_WD * w), m_, v_


def _adamw_replicated(vec_sum, mat_sum, norm_grad, entries, conv):
    n = len(entries)

    def grad_of(name, shape, vec_ref, mat_ref, norm_ref):
        if name == "norm_g":
            return norm_ref[0:1, :]
        if name in MAT_BAG_AT:
            return mat_ref[MAT_BAG_AT[name]:MAT_BAG_AT[name] + shape[0], :]
        if shape[0] == 1:
            return vec_ref[_bag_row(name), 0:shape[1]]
        return jnp.concatenate([vec_ref[_bag_row(name), h * shape[1]:(h + 1) * shape[1]] for h in range(shape[0])],
                               axis=0)

    def body(vec_ref, mat_ref, norm_ref, *refs):
        ins, outs = refs[:3 * n + 4], refs[3 * n + 4:]
        for k in range(n):
            w_ref, m_ref, v_ref = ins[3 * k:3 * k + 3]
            g = grad_of(entries[k][0], w_ref.shape, vec_ref, mat_ref, norm_ref)
            d, m_, v_ = _adam_update(w_ref[...], g, m_ref[...], v_ref[...])
            for ref, val in zip(outs[4 * k:4 * k + 4], (g, d, m_, v_)):
                ref[...] = val
        w_ref, m_ref, v_ref, g_ref = ins[3 * n:]
        g = g_ref[0:w_ref.shape[0], :]
        for ref, val in zip(outs[4 * n:4 * n + 4], (g,) + _adam_update(w_ref[...], g, m_ref[...], v_ref[...])):
            ref[...] = val
        outs[4 * n + 4][...] = vec_ref[_bag_row("loss"), 0:1]

    arrays = [a for e in entries for a in e[1:]] + list(conv)
    out_shape = [jax.ShapeDtypeStruct(e[1].shape, F32) for e in entries for _ in range(4)]
    out_shape += [jax.ShapeDtypeStruct(conv[0].shape, F32)] * 4 + [jax.ShapeDtypeStruct((1, 1), F32)]
    return pl.pallas_call(
        body, name="adamw_replicated", out_shape=tuple(out_shape),
        compiler_params=pltpu.CompilerParams(vmem_limit_bytes=VMEM_LIMIT_BYTES),
    )(vec_sum, mat_sum, norm_grad, *arrays)


def _adamw(w, g, m, v, rows, name):
    r, c = w.shape

    def body(w_ref, g_ref, m_ref, v_ref, go_ref, d_ref, nm_ref, nv_ref):
        g = g_ref[...]
        go_ref[...] = g
        d_ref[...], nm_ref[...], nv_ref[...] = _adam_update(w_ref[...], g, m_ref[...], v_ref[...])

    spec = pl.BlockSpec((rows, c), lambda i: (i, 0))
    return pl.pallas_call(
        body, name=name, out_shape=tuple(jax.ShapeDtypeStruct((r, c), F32) for _ in range(4)),
        grid=(r // rows,), in_specs=[spec] * 4, out_specs=(spec,) * 4,
        compiler_params=pltpu.CompilerParams(dimension_semantics=("arbitrary",),
                                             vmem_limit_bytes=VMEM_LIMIT_BYTES),
    )(w, g, m, v)


def _adamw_group(items, name):
    n = 4 * len(items)

    def body(*refs):
        ins, outs, bufs = refs[:n], refs[n:2 * n], refs[2 * n:3 * n]
        load_sems, store_sems = refs[3 * n:]
        loads = [pltpu.make_async_copy(ins[j], bufs[j], load_sems.at[j]) for j in range(n)]
        stores = [pltpu.make_async_copy(bufs[j], outs[j], store_sems.at[j]) for j in range(n)]
        for cp in loads:
            cp.start()
        for k in range(len(items)):
            for cp in loads[4 * k:4 * k + 4]:
                cp.wait()
            w_buf, g_buf, m_buf, v_buf = bufs[4 * k:4 * k + 4]
            w_buf[...], m_buf[...], v_buf[...] = _adam_update(w_buf[...], g_buf[...], m_buf[...], v_buf[...])
            for cp in stores[4 * k:4 * k + 4]:
                cp.start()
        for cp in stores:
            cp.wait()

    arrays = [a for item in items for a in item]
    any_spec = pl.BlockSpec(memory_space=pl.ANY)
    flat = pl.pallas_call(
        body, name=name, out_shape=tuple(jax.ShapeDtypeStruct(a.shape, F32) for a in arrays),
        in_specs=[any_spec] * n, out_specs=(any_spec,) * n,
        scratch_shapes=[pltpu.VMEM(a.shape, F32) for a in arrays] + [pltpu.SemaphoreType.DMA((n,))] * 2,
        compiler_params=pltpu.CompilerParams(vmem_limit_bytes=VMEM_LIMIT_BYTES),
    )(*arrays)
    return [(flat[4 * k + 1], flat[4 * k], flat[4 * k + 2], flat[4 * k + 3]) for k in range(len(items))]


def _shift_down(ext, s):
    return pltpu.roll(ext, s, 0)


def _tile_shift(v, s):
    rows, cols = v.shape
    tiles = v.reshape(rows // F32_SUBLANES, F32_SUBLANES, cols)
    return pltpu.roll(tiles, s % F32_SUBLANES, 1).reshape(rows, cols)


def _shift_up(ext, s):
    return pltpu.roll(ext, ext.shape[0] - s, 0)


def _lru_gates(xc, wa_ref, ba, wx_ref, bx, lam):
    pa, px = [], []
    for h in range(LRU_HEADS):
        xh = xc[:, h * HEAD_DIM:(h + 1) * HEAD_DIM].astype(BF16)
        pa.append(_dot(xh, wa_ref[h]))
        px.append(_dot(xh, wx_ref[h]))
    r = _sigmoid(jnp.concatenate(pa, axis=1) + ba)
    ig = _sigmoid(jnp.concatenate(px, axis=1) + bx)
    sp = _softplus(-lam)
    log_a = (-LRU_C * r) * sp
    a = jnp.exp(log_a)
    mult = jnp.sqrt(jnp.tanh(-log_a) * (1.0 + a * a))
    return r, ig, a, mult, sp


def _conv(ext, w_ref, b):
    y = b + _shift_down(ext, 3) * w_ref[0:1, :]
    y = y + _shift_down(ext, 2) * w_ref[1:2, :]
    y = y + _shift_down(ext, 1) * w_ref[2:3, :]
    y = y + ext * w_ref[3:4, :]
    return y[CONV_HIST:, :]


def _pool_diff(ext, pos):
    out = []
    for g, k in enumerate(POOL_WINDOWS):
        col = ext[:, g * POOL_GROUP_DIM:(g + 1) * POOL_GROUP_DIM]
        s = col
        for step in range(g + 1):
            s = s + _shift_down(s, 2 ** step)
        count = jnp.minimum(pos + 1, k).astype(F32)
        out.append(s[POOL_HIST:, :] / count - col[POOL_HIST:, :])
    return out


def _pool_mix(diff, pw_ref):
    return jnp.concatenate([_dot(diff[g].astype(BF16), pw_ref[g]) for g in range(len(POOL_WINDOWS))], axis=1)


def _branch_specs(tb, row_map, fixed):
    fixed3 = lambda i: (0, 0, 0)
    return [pl.BlockSpec((CONV_WIDTH, D_MODEL), fixed), pl.BlockSpec((1, D_MODEL), fixed),
            pl.BlockSpec((LRU_HEADS, HEAD_DIM, HEAD_DIM), fixed3), pl.BlockSpec((1, D_MODEL), fixed),
            pl.BlockSpec((LRU_HEADS, HEAD_DIM, HEAD_DIM), fixed3), pl.BlockSpec((1, D_MODEL), fixed),
            pl.BlockSpec((1, D_MODEL), fixed),
            pl.BlockSpec((len(POOL_WINDOWS), POOL_GROUP_DIM, POOL_GROUP_DIM), fixed3),
            pl.BlockSpec((1, POOL_WIDTH), fixed)]


def _branches_fwd(z, weights, seq, tb, shards):
    t = z.shape[0]
    nb = t // tb
    nbe = seq // tb
    groups = tb // F32_SUBLANES
    n = len(shards)

    def body(xa_ref, ga_ref, xb_ref, gb_ref, cw_ref, cb_ref, wa_ref, ba_ref, wx_ref, bx_ref, lam_ref,
             pw_ref, ps_ref, *refs):
        g_ins = refs[:n]
        ya_ref, yb_ref, hl_ref = refs[n:n + 3]
        g_outs = refs[n + 3:2 * n + 3]
        xa_ext, xb_ext, carry, a_s, u_s, send_sems, recv_sems, local_sems = refs[2 * n + 3:]
        blk = pl.program_id(0) % nbe
        start_gather, relay_gather, finish_gather = _gather_steps(shards, g_ins, g_outs, send_sems, recv_sems,
                                                                  local_sems)
        pl.when(pl.program_id(0) == 0)(start_gather)
        pl.when(pl.program_id(0) == nb // 2)(relay_gather)

        @pl.when(blk == 0)
        def _():
            xa_ext[0:CONV_HIST, :] = jnp.zeros((CONV_HIST, D_MODEL), F32)
            xb_ext[0:POOL_HIST, :] = jnp.zeros((POOL_HIST, POOL_WIDTH), F32)
            carry[...] = jnp.zeros_like(carry)

        xa_ext[CONV_HIST:, :] = xa_ref[...]
        xb_ext[POOL_HIST:, :] = xb_ref[...]
        ea = xa_ext[...]
        eb = xb_ext[...]
        xa_ext[0:CONV_HIST, :] = ea[tb:, :]
        xb_ext[0:POOL_HIST, :] = eb[tb:, :]

        xc = _conv(ea, cw_ref, cb_ref[...])
        _, ig, a, mult, _ = _lru_gates(xc, wa_ref, ba_ref[...], wx_ref, bx_ref[...], lam_ref[...])
        u = mult * (ig * xc)
        row8 = lax.broadcasted_iota(jnp.int32, (tb, D_MODEL), 0) % F32_SUBLANES
        for s in (1, 2, 4):
            m = row8 >= s
            u = jnp.where(m, a * _tile_shift(u, s) + u, u)
            a = jnp.where(m, a * _tile_shift(a, s), a)
        a_s[...] = a
        u_s[...] = u

        def step(g, cr):
            sl = pl.ds(pl.multiple_of(g * F32_SUBLANES, F32_SUBLANES), F32_SUBLANES)
            hb = a_s[sl, :] * cr + u_s[sl, :]
            hl_ref[sl, :] = hb
            return jnp.broadcast_to(hb[F32_SUBLANES - 1:F32_SUBLANES, :], (F32_SUBLANES, D_MODEL))

        carry[...] = lax.fori_loop(0, groups, step, carry[...], unroll=4)
        ga = ga_ref[...]
        ya_ref[...] = (hl_ref[...] * (ga * _sigmoid(ga))).astype(BF16)

        pos = blk * tb + lax.broadcasted_iota(jnp.int32, (tb, POOL_GROUP_DIM), 0)
        ypre = _pool_mix(_pool_diff(eb, pos), pw_ref)
        gb = gb_ref[...]
        yb_ref[...] = ((ypre * ps_ref[...]) * (gb * _sigmoid(gb))).astype(BF16)
        pl.when(pl.program_id(0) == nb - 1)(finish_gather)

    row = lambda i: (i, 0)
    fixed = lambda i: (0, 0)
    any_spec = pl.BlockSpec(memory_space=pl.ANY)
    in_specs = [pl.BlockSpec((tb, D_MODEL), lambda i: (i, 0)), pl.BlockSpec((tb, D_MODEL), lambda i: (i, 1)),
                pl.BlockSpec((tb, POOL_WIDTH), lambda i: (i, 4)), pl.BlockSpec((tb, POOL_WIDTH), lambda i: (i, 5)),
                ] + _branch_specs(tb, row, fixed) + [any_spec] * n
    g_shape, g_sems = _gather_shapes(shards)
    return pl.pallas_call(
        body, name="branches_fwd",
        out_shape=tuple([jax.ShapeDtypeStruct((t, D_MODEL), BF16), jax.ShapeDtypeStruct((t, POOL_WIDTH), BF16),
                         jax.ShapeDtypeStruct((t, D_MODEL), F32)] + g_shape),
        grid=(nb,), in_specs=in_specs,
        out_specs=tuple([pl.BlockSpec((tb, D_MODEL), row), pl.BlockSpec((tb, POOL_WIDTH), row),
                         pl.BlockSpec((tb, D_MODEL), row)] + [any_spec] * n),
        scratch_shapes=[pltpu.VMEM((tb + CONV_HIST, D_MODEL), F32), pltpu.VMEM((tb + POOL_HIST, POOL_WIDTH), F32),
                        pltpu.VMEM((F32_SUBLANES, D_MODEL), F32),
                        pltpu.VMEM((tb, D_MODEL), F32), pltpu.VMEM((tb, D_MODEL), F32)] + g_sems,
        compiler_params=pltpu.CompilerParams(dimension_semantics=("arbitrary",),
                                             vmem_limit_bytes=VMEM_LIMIT_BYTES),
    )(z, z, z, z, *weights, *[sh[0] for sh in shards])


def _branches_bwd(z, hl, dya, dyb, dzm, weights, vec_bag, seq, tb, riders):
    t = z.shape[0]
    nb = t // tb
    nbe = seq // tb
    groups = tb // F32_SUBLANES
    nr = len(riders)

    def body(xa_ref, xap_ref, ga_ref, xb_ref, xbp_ref, gb_ref, hl_ref, hlp_ref, dya_ref, dyb_ref, dzm_ref,
             cw_ref, cb_ref, wa_ref, ba_ref, wx_ref, bx_ref, lam_ref, pw_ref, ps_ref, vec_in_ref, *rest):
        pairs, (dz_ref, vec_ref, mat_ref), grads = rest[:2 * nr], rest[2 * nr:2 * nr + 3], rest[2 * nr + 3:4 * nr + 3]
        xa_ext, xb_ext, hl_ext, a_ext, dxc_ext, dwin_ext, g_carry, b_s, d_s, g_s = rest[4 * nr + 3:]
        i = pl.program_id(0)
        blk = (nb - 1 - i) % nbe

        def mat_rows(name, k):
            at = MAT_BAG_AT[name] + k * HEAD_DIM
            return slice(at, at + HEAD_DIM)

        def rider(k):
            grads[2 * k][...] += _dot_tn(pairs[2 * k][...], pairs[2 * k + 1][...])

        @pl.when(i == 0)
        def _():
            vec_ref[...] = vec_in_ref[...]
            mat_ref[...] = jnp.zeros_like(mat_ref)
            for k in range(nr):
                grads[2 * k][...] = jnp.zeros_like(grads[2 * k])

        @pl.when(blk == nbe - 1)
        def _():
            a_ext[tb:, :] = jnp.zeros((F32_SUBLANES, D_MODEL), F32)
            dxc_ext[tb:, :] = jnp.zeros((CONV_HIST, D_MODEL), F32)
            dwin_ext[tb:, :] = jnp.zeros((POOL_HIST, POOL_WIDTH), F32)
            g_carry[...] = jnp.zeros_like(g_carry)

        live = (blk > 0).astype(F32)
        xa_ext[0:CONV_HIST, :] = xap_ref[...] * live
        xa_ext[CONV_HIST:, :] = xa_ref[...]
        xb_ext[0:POOL_HIST, :] = xbp_ref[...] * live
        xb_ext[POOL_HIST:, :] = xb_ref[...]
        hl_ext[0:F32_SUBLANES, :] = hlp_ref[...] * live
        hl_ext[F32_SUBLANES:, :] = hl_ref[...]
        ea = xa_ext[...]
        eb = xb_ext[...]
        rider(0)

        xc = _conv(ea, cw_ref, cb_ref[...])
        lam = lam_ref[...]
        r, ig, a, mult, sp = _lru_gates(xc, wa_ref, ba_ref[...], wx_ref, bx_ref[...], lam)
        hl = hl_ref[...]
        ga = ga_ref[...]
        sga = _sigmoid(ga)
        dya = dya_ref[...]
        dhl = dya * (ga * sga)
        dz_ref[:, D_MODEL:2 * D_MODEL] = (dya * hl * (sga * (1.0 + ga * (1.0 - sga)))).astype(BF16)

        a_ext[0:tb, :] = a
        b = _shift_up(a_ext[...], 1)[0:tb, :]
        a_ext[tb:, :] = jnp.broadcast_to(a[0:1, :], (F32_SUBLANES, D_MODEL))
        d = dhl
        row8 = lax.broadcasted_iota(jnp.int32, (tb, D_MODEL), 0) % F32_SUBLANES
        for s in (1, 2, 4):
            m = row8 < F32_SUBLANES - s
            d = jnp.where(m, d + b * _tile_shift(d, -s), d)
            b = jnp.where(m, b * _tile_shift(b, -s), b)
        b_s[...] = b
        d_s[...] = d

        def step(k, cr):
            sl = pl.ds(pl.multiple_of((groups - 1 - k) * F32_SUBLANES, F32_SUBLANES), F32_SUBLANES)
            gb_ = d_s[sl, :] + b_s[sl, :] * cr
            g_s[sl, :] = gb_
            return jnp.broadcast_to(gb_[0:1, :], (F32_SUBLANES, D_MODEL))

        g_carry[...] = lax.fori_loop(0, groups, step, g_carry[...], unroll=4)
        rider(1)
        gsc = g_s[...]
        da = gsc * _shift_down(hl_ext[...], 1)[F32_SUBLANES:, :]
        dmult = gsc * (ig * xc)
        dig = gsc * (mult * xc)
        dxc = gsc * (mult * ig)
        dlog_a = da * a - (a * a) * dmult / mult
        dr = dlog_a * (-LRU_C * sp)
        vec_ref[_bag_row("lru_lambda"), :] += jnp.sum(dlog_a * (-LRU_C * r), axis=0, keepdims=True)
        dpa = dr * (r * (1.0 - r))
        dpx = dig * (ig * (1.0 - ig))
        vec_ref[_bag_row("lru_b_a"), :] += jnp.sum(dpa, axis=0, keepdims=True)
        vec_ref[_bag_row("lru_b_x"), :] += jnp.sum(dpx, axis=0, keepdims=True)
        back = []
        for h in range(LRU_HEADS):
            cols = slice(h * HEAD_DIM, (h + 1) * HEAD_DIM)
            xh = xc[:, cols].astype(BF16)
            dpa_h = dpa[:, cols].astype(BF16)
            dpx_h = dpx[:, cols].astype(BF16)
            mat_ref[mat_rows("lru_w_a", h), :] += _dot_tn(xh, dpa_h)
            mat_ref[mat_rows("lru_w_x", h), :] += _dot_tn(xh, dpx_h)
            back.append(_dot_nt(dpa_h, wa_ref[h]) + _dot_nt(dpx_h, wx_ref[h]))
        dxc = dxc + jnp.concatenate(back, axis=1)
        vec_ref[_bag_row("conv_b"), :] += jnp.sum(dxc, axis=0, keepdims=True)
        for k in range(CONV_WIDTH):
            tap = _shift_down(ea, CONV_WIDTH - 1 - k)[CONV_HIST:, :] if k < CONV_WIDTH - 1 else ea[CONV_HIST:, :]
            vec_ref[_bag_row("conv_w", k), :] += jnp.sum(dxc * tap, axis=0, keepdims=True)
        dxc_ext[0:tb, :] = dxc
        ed = dxc_ext[...]
        dxa = ed * cw_ref[3:4, :]
        dxa = dxa + _shift_up(ed, 1) * cw_ref[2:3, :]
        dxa = dxa + _shift_up(ed, 2) * cw_ref[1:2, :]
        dxa = dxa + _shift_up(ed, 3) * cw_ref[0:1, :]
        dz_ref[:, 0:D_MODEL] = dxa[0:tb, :].astype(BF16)
        dxc_ext[tb:, :] = dxc[0:CONV_HIST, :]

        pos = blk * tb + lax.broadcasted_iota(jnp.int32, (tb, POOL_GROUP_DIM), 0)
        diff = _pool_diff(eb, pos)
        rider(2)
        ypre = _pool_mix(diff, pw_ref)
        ps = ps_ref[...]
        gb = gb_ref[...]
        sgb = _sigmoid(gb)
        dyb = dyb_ref[...]
        dyp = dyb * (gb * sgb)
        dz_ref[:, 2 * D_MODEL + POOL_WIDTH:3 * D_MODEL] = (
            dyb * (ypre * ps) * (sgb * (1.0 + gb * (1.0 - sgb)))).astype(BF16)
        vec_ref[_bag_row("pool_scale"), 0:POOL_WIDTH] += jnp.sum(dyp * ypre, axis=0, keepdims=True)
        dypre = dyp * ps
        for g, k in enumerate(POOL_WINDOWS):
            cols = slice(g * POOL_GROUP_DIM, (g + 1) * POOL_GROUP_DIM)
            dyg = dypre[:, cols].astype(BF16)
            mat_ref[mat_rows("pool_w", g), :] += _dot_tn(diff[g].astype(BF16), dyg)
            ddiff = _dot_nt(dyg, pw_ref[g])
            count = jnp.minimum(pos + 1, k).astype(F32)
            dwin = ddiff / count
            dwin_ext[0:tb, cols] = dwin
            s = dwin_ext[:, cols]
            for step_ in range(g + 1):
                s = s + _shift_up(s, 2 ** step_)
            dz_ref[:, 2 * D_MODEL + g * POOL_GROUP_DIM:2 * D_MODEL + (g + 1) * POOL_GROUP_DIM] = (
                s[0:tb, :] - ddiff).astype(BF16)
            dwin_ext[tb:, cols] = dwin[0:POOL_HIST, :]

        dz_ref[:, 3 * D_MODEL:] = dzm_ref[...]

        @pl.when(i == nb - 1)
        def _():
            row = _bag_row("lru_lambda")
            vec_ref[row, :] = vec_ref[row, :] * (-_sigmoid(-lam))
            for k in range(nr):
                grads[2 * k + 1][...] = grads[2 * k][...].astype(BF16)

    rev = lambda i: (nb - 1 - i, 0)
    fixed = lambda i: (0, 0)

    def prev(rows, col):
        per = tb // rows
        return lambda i: (jnp.maximum((nb - 1 - i) * per - 1, 0), col)

    in_specs = [pl.BlockSpec((tb, D_MODEL), lambda i: (nb - 1 - i, 0)),
                pl.BlockSpec((CONV_HIST, D_MODEL), prev(CONV_HIST, 0)),
                pl.BlockSpec((tb, D_MODEL), lambda i: (nb - 1 - i, 1)),
                pl.BlockSpec((tb, POOL_WIDTH), lambda i: (nb - 1 - i, 4)),
                pl.BlockSpec((POOL_HIST, POOL_WIDTH), prev(POOL_HIST, 4)),
                pl.BlockSpec((tb, POOL_WIDTH), lambda i: (nb - 1 - i, 5)),
                pl.BlockSpec((tb, D_MODEL), rev),
                pl.BlockSpec((F32_SUBLANES, D_MODEL), prev(F32_SUBLANES, 0)),
                pl.BlockSpec((tb, D_MODEL), rev), pl.BlockSpec((tb, POOL_WIDTH), rev),
                pl.BlockSpec((tb, 2 * D_MODEL), rev)] + _branch_specs(tb, rev, fixed) + [
                    pl.BlockSpec((VEC_BAG_ROWS, D_MODEL), fixed)]
    vec_at = len(in_specs) - 1
    out_shape = [jax.ShapeDtypeStruct((t, IN_COLS), BF16), jax.ShapeDtypeStruct((VEC_BAG_ROWS, D_MODEL), F32),
                 jax.ShapeDtypeStruct((MAT_BAG_ROWS, HEAD_DIM), F32)]
    out_specs = [pl.BlockSpec((tb, IN_COLS), rev), pl.BlockSpec((VEC_BAG_ROWS, D_MODEL), fixed),
                 pl.BlockSpec((MAT_BAG_ROWS, HEAD_DIM), fixed)]
    for lhs, rhs in riders:
        in_specs += [pl.BlockSpec((tb, lhs.shape[1]), rev), pl.BlockSpec((tb, rhs.shape[1]), rev)]
        grad = (lhs.shape[1], rhs.shape[1])
        out_shape += [jax.ShapeDtypeStruct(grad, F32), jax.ShapeDtypeStruct(grad, BF16)]
        out_specs += [pl.BlockSpec(grad, fixed)] * 2
    scratch = [pltpu.VMEM((tb + CONV_HIST, D_MODEL), F32), pltpu.VMEM((tb + POOL_HIST, POOL_WIDTH), F32),
               pltpu.VMEM((tb + F32_SUBLANES, D_MODEL), F32), pltpu.VMEM((tb + F32_SUBLANES, D_MODEL), F32),
               pltpu.VMEM((tb + CONV_HIST, D_MODEL), F32), pltpu.VMEM((tb + POOL_HIST, POOL_WIDTH), F32),
               pltpu.VMEM((F32_SUBLANES, D_MODEL), F32),
               pltpu.VMEM((tb, D_MODEL), F32), pltpu.VMEM((tb, D_MODEL), F32), pltpu.VMEM((tb, D_MODEL), F32)]
    return pl.pallas_call(
        body, name="branches_bwd", out_shape=tuple(out_shape), grid=(nb,), in_specs=in_specs,
        out_specs=tuple(out_specs), scratch_shapes=scratch, input_output_aliases={vec_at: 1},
        compiler_params=pltpu.CompilerParams(dimension_semantics=("arbitrary",),
                                             vmem_limit_bytes=VMEM_LIMIT_BYTES),
    )(z, z, z, z, z, z, hl, hl, dya, dyb, dzm, *weights, vec_bag, *[a for pair in riders for a in pair])


def _merge_head(x2d, ya, yb, z, p2d, tgt, w_pl, w_pp, w_out, w_pg, w_pe, g2, gf, tb):
    t = x2d.shape[0]
    p_dim = p2d.shape[1]

    def body(x_ref, ya_ref, yb_ref, ma_ref, mb_ref, p_ref, t_ref, wpl_ref, wpp_ref, wout_ref, wpg_ref, wpe_ref,
             g2_ref, gf_ref,
             bag_ref, dxr_ref, dya_ref, dyb_ref, dzm_ref,
             mg_ref, do_ref, hn_ref, dgp_ref, dpe_ref, da_ref, dbm_ref, pbf_ref):
        @pl.when(pl.program_id(0) == 0)
        def _():
            bag_ref[...] = jnp.zeros_like(bag_ref)

        a_ = _dot(ya_ref[...], wpl_ref[...])
        bm = _dot(yb_ref[...], wpp_ref[...])
        sa = _sigmoid(ma_ref[...])
        sb = _sigmoid(mb_ref[...])
        mg = (sa * a_ + sb * bm).astype(BF16)
        mg_ref[...] = mg
        x1 = x_ref[...] + _dot(mg, wout_ref[...])
        xn2, r2 = _rms(x1)
        g2 = g2_ref[...]
        hn = (xn2 * g2).astype(BF16)
        hn_ref[...] = hn
        gate = _sigmoid(_dot(hn, wpg_ref[...]))
        pbf = p_ref[...].astype(BF16)
        pbf_ref[...] = pbf
        pe = _dot(pbf, wpe_ref[...])
        x2 = x1 + gate * pe
        xn3, r3 = _rms(x2)
        gf = gf_ref[...]
        err = xn3 * gf - t_ref[...]
        bag_ref[_bag_rows("loss"), 0:128] += 0.5 * jnp.sum(jnp.mean(err * err, axis=-1))

        dy = err * (1.0 / D_MODEL)
        bag_ref[_bag_row("final_g"), :] += jnp.sum(dy * xn3, axis=0, keepdims=True)
        dx2 = _rms_bwd(dy * gf, xn3, r3)
        dpe_ref[...] = (dx2 * gate).astype(BF16)
        dgp = ((dx2 * pe) * (gate * (1.0 - gate))).astype(BF16)
        dgp_ref[...] = dgp
        dhn = _dot_nt(dgp, wpg_ref[...])
        bag_ref[_bag_row("ple_norm_g"), :] += jnp.sum(dhn * xn2, axis=0, keepdims=True)
        dx1 = dx2 + _rms_bwd(dhn * g2, xn2, r2)
        dxr_ref[...] = dx1
        do = dx1.astype(BF16)
        do_ref[...] = do
        dmg = _dot_nt(do, wout_ref[...])
        da = (dmg * sa).astype(BF16)
        dbm = (dmg * sb).astype(BF16)
        da_ref[...] = da
        dbm_ref[...] = dbm
        dzm_ref[:, 0:D_MODEL] = (dmg * a_ * (sa * (1.0 - sa))).astype(BF16)
        dzm_ref[:, D_MODEL:] = (dmg * bm * (sb * (1.0 - sb))).astype(BF16)
        dya_ref[...] = _dot_nt(da, wpl_ref[...])
        dyb_ref[...] = _dot_nt(dbm, wpp_ref[...])

    row = lambda i: (i, 0)
    fixed = lambda i: (0, 0)

    def resident(shape):
        return pl.BlockSpec(shape, fixed, pipeline_mode=pl.Buffered(1))

    tok = lambda width: pl.BlockSpec((tb, width), row)
    in_specs = [tok(D_MODEL), tok(D_MODEL), tok(POOL_WIDTH),
                pl.BlockSpec((tb, D_MODEL), lambda i: (i, 3)), pl.BlockSpec((tb, D_MODEL), lambda i: (i, 4)),
                tok(p_dim), tok(D_MODEL),
                resident((D_MODEL, D_MODEL)), resident((POOL_WIDTH, D_MODEL)), resident((D_MODEL, D_MODEL)),
                resident((D_MODEL, D_MODEL)), resident((p_dim, D_MODEL)),
                pl.BlockSpec((1, D_MODEL), fixed), pl.BlockSpec((1, D_MODEL), fixed)]
    bf = lambda width: jax.ShapeDtypeStruct((t, width), BF16)
    f32 = lambda width: jax.ShapeDtypeStruct((t, width), F32)
    out_shape = (jax.ShapeDtypeStruct((VEC_BAG_ROWS, D_MODEL), F32),
                 f32(D_MODEL), f32(D_MODEL), f32(POOL_WIDTH), bf(2 * D_MODEL),
                 bf(D_MODEL), bf(D_MODEL), bf(D_MODEL), bf(D_MODEL), bf(D_MODEL), bf(D_MODEL), bf(D_MODEL), bf(p_dim))
    out_specs = (pl.BlockSpec((VEC_BAG_ROWS, D_MODEL), fixed),
                 tok(D_MODEL), tok(D_MODEL), tok(POOL_WIDTH), tok(2 * D_MODEL),
                 tok(D_MODEL), tok(D_MODEL), tok(D_MODEL), tok(D_MODEL), tok(D_MODEL), tok(D_MODEL), tok(D_MODEL),
                 tok(p_dim))
    return pl.pallas_call(
        body, name="merge_head", out_shape=out_shape, grid=(t // tb,), in_specs=in_specs, out_specs=out_specs,
        compiler_params=pltpu.CompilerParams(dimension_semantics=("arbitrary",),
                                             vmem_limit_bytes=VMEM_LIMIT_BYTES),
    )(x2d, ya, yb, z, z, p2d, tgt, w_pl, w_pp, w_out, w_pg, w_pe, g2, gf)


def kernel(x, p, norm_g, w_in, conv_w, conv_b, lru_w_a, lru_b_a, lru_w_x, lru_b_x, lru_lambda, pool_w, pool_scale, w_proj_lru, w_proj_pool, w_out, ple_norm_g, w_ple_gate, w_ple_proj, final_g, loss_target, m_norm_g, m_w_in, m_conv_w, m_conv_b, m_lru_w_a, m_lru_b_a, m_lru_w_x, m_lru_b_x, m_lru_lambda, m_pool_w, m_pool_scale, m_w_proj_lru, m_w_proj_pool, m_w_out, m_ple_norm_g, m_w_ple_gate, m_w_ple_proj, m_final_g, v_norm_g, v_w_in, v_conv_w, v_conv_b, v_lru_w_a, v_lru_b_a, v_lru_w_x, v_lru_b_x, v_lru_lambda, v_pool_w, v_pool_scale, v_w_proj_lru, v_w_proj_pool, v_w_out, v_ple_norm_g, v_w_ple_gate, v_w_ple_proj, v_final_g):
    bsz, seq, _ = x.shape
    t = bsz * seq
    tb_mm = min(1024, seq)
    tb_seq = min(256, seq // 2) if seq >= 512 else seq
    tb_fwd = min(2 * tb_seq, seq // 2) if seq >= 512 else seq
    x2d = x.reshape(t, D_MODEL)
    p2d = p.reshape(t, p.shape[-1])
    tgt = loss_target.reshape(t, D_MODEL)

    rest = [(w_proj_lru[0], 0), (w_proj_pool[0], 1), (w_out[0], 0), (w_ple_gate[0], 0), (w_ple_proj[0], 1)]
    z, h_bf, w_in_f, conv_w_f, *narrow = _in_proj_gather(
        x2d, norm_g, w_in[0], [(conv_w[0], 1, False)], tb_mm,
        [w for w, _ in rest] + [lru_w_a[0], lru_w_x[0], pool_w[0]])
    wa_bf, wx_bf, pw_bf = narrow[len(rest):]
    branch_w = (conv_w_f, conv_b, wa_bf, lru_b_a.reshape(1, D_MODEL), wx_bf, lru_b_x.reshape(1, D_MODEL),
                lru_lambda, pw_bf, pool_scale)

    ya, yb, hl, w_pl_f, w_pp_f, w_out_f, w_pg_f, w_pe_f = _branches_fwd(
        z, branch_w, seq, tb_fwd, [(w16, axis, True) for w16, (_, axis) in zip(narrow, rest)])
    (vec_bag, dx_res, dya, dyb, dzm, mg_bf, do_bf, hn_bf, dgp_bf, dpe_bf, da_bf, dbm_bf, p_bf) = _merge_head(
        x2d, ya, yb, z, p2d, tgt, w_pl_f, w_pp_f, w_out_f, w_pg_f, w_pe_f, ple_norm_g, final_g.reshape(1, D_MODEL),
        tb_seq)
    dz, vec_bag, mat_bag, g_out, g_out16, g_pp, g_pp16, g_pe, g_pe16 = _branches_bwd(
        z, hl, dya, dyb, dzm, branch_w, vec_bag, seq, tb_seq, [(mg_bf, do_bf), (yb, dbm_bf), (p_bf, dpe_bf)])

    tb_dw = min(1024, seq)
    def row_pieces(g32, g16):
        pieces = (8, g32.shape[0] // 8, g32.shape[1])
        return g32.reshape(pieces), False, g16.reshape(pieces)

    def proj_grad(lhs, rhs, name):
        g32, g16 = _weight_grad(lhs, rhs, 1, tb_dw, name)
        return row_pieces(g32[0], g16[0])

    p_dim = p2d.shape[1]
    proj_parts = [proj_grad(ya, da_bf, "dw_proj_lru"), (g_pp, True, g_pp16), row_pieces(g_out, g_out16),
                  proj_grad(hn_bf, dgp_bf, "dw_ple_gate"), (g_pe, True, g_pe16)]
    nb_dw = t // tb_dw
    g_in, g_in16, r_pl, r_pp, r_out, r_pg, r_pe, vec_mine, mat_mine = _weight_grad(
        h_bf, dz, N_CHIPS, tb_dw, "dw_in",
        reduce=(proj_parts + [(vec_bag.reshape(8, VEC_BAG_ROWS // 8, D_MODEL), False, None),
                              (mat_bag.reshape(8, MAT_BAG_ROWS // 8, HEAD_DIM), False, None)],
                [BF16] * 5 + [F32] * 2,
                (0, nb_dw // 2, 2 * nb_dw - 1, 3 * nb_dw + nb_dw // 2, N_CHIPS * nb_dw - 1)))
    pieces = (8, D_MODEL // 2, IN_COLS // N_CHIPS)
    nb_seq = t // tb_seq
    dx, g_g1, r_in, vec_sum, mat_sum, g_cw = _in_proj_bwd(
        dz, w_in_f, x2d, dx_res, norm_g, tb_seq,
        reduce=([(g_in.reshape(pieces), False, g_in16.reshape(pieces))], BF16,
                (0, nb_seq // 8, nb_seq // 2, nb_seq - 1)),
        shards=[(vec_mine.reshape(VEC_BAG_ROWS // N_CHIPS, D_MODEL), 0, True),
                (mat_mine.reshape(MAT_BAG_ROWS // N_CHIPS, HEAD_DIM), 0, True)],
        take=(_bag_rows("conv_w"), D_MODEL // N_CHIPS))

    u_in = tuple(a[None] for a in _adamw(w_in[0], r_in.reshape(D_MODEL, IN_COLS // N_CHIPS), m_w_in[0], v_w_in[0],
                                         D_MODEL // 4, "adamw_w_in"))
    proj = [(w_proj_lru, r_pl, m_w_proj_lru, v_w_proj_lru), (w_proj_pool, r_pp, m_w_proj_pool, v_w_proj_pool),
            (w_out, r_out, m_w_out, v_w_out), (w_ple_gate, r_pg, m_w_ple_gate, v_w_ple_gate),
            (w_ple_proj, r_pe, m_w_ple_proj, v_w_ple_proj)]
    u_pl, u_pp, u_out, u_pg, u_pe = [tuple(a[None] for a in u) for u in _adamw_group(
        [(w[0], g.reshape(w.shape[1:]), m[0], v[0]) for w, g, m, v in proj], "adamw_proj")]

    small = [("norm_g", norm_g, m_norm_g, v_norm_g), ("conv_b", conv_b, m_conv_b, v_conv_b),
             ("lru_w_a", lru_w_a, m_lru_w_a, v_lru_w_a), ("lru_b_a", lru_b_a, m_lru_b_a, v_lru_b_a),
             ("lru_w_x", lru_w_x, m_lru_w_x, v_lru_w_x), ("lru_b_x", lru_b_x, m_lru_b_x, v_lru_b_x),
             ("lru_lambda", lru_lambda, m_lru_lambda, v_lru_lambda), ("pool_w", pool_w, m_pool_w, v_pool_w),
             ("pool_scale", pool_scale, m_pool_scale, v_pool_scale),
             ("ple_norm_g", ple_norm_g, m_ple_norm_g, v_ple_norm_g), ("final_g", final_g, m_final_g, v_final_g)]

    def view(a):
        return a.reshape(-1, a.shape[-1]) if a.ndim != 3 else a[0]

    flat = _adamw_replicated(vec_sum, mat_sum, g_g1, [(name,) + tuple(view(a) for a in arrs) for name, *arrs in small],
                             (conv_w[0], m_conv_w[0], v_conv_w[0], g_cw))
    u_small = {name: tuple(flat[4 * k + pick].reshape(arrs[0].shape) for pick in range(4))
               for k, (name, *arrs) in enumerate(small)}
    u_cw = tuple(a[None] for a in flat[4 * len(small):4 * len(small) + 4])

    loss = flat[-1].reshape(())
    grad_x = dx.reshape(bsz, seq, D_MODEL)

    def ordered(pick):
        s = {name: u[pick] for name, u in u_small.items()}
        return [s["norm_g"], u_in[pick], u_cw[pick], s["conv_b"], s["lru_w_a"], s["lru_b_a"], s["lru_w_x"], s["lru_b_x"],
                s["lru_lambda"], s["pool_w"], s["pool_scale"], u_pl[pick], u_pp[pick], u_out[pick], s["ple_norm_g"],
                u_pg[pick], u_pe[pick], s["final_g"]]

    return (loss, grad_x, *ordered(0), *ordered(1), *ordered(2), *ordered(3))
```

```python
import jax
import jax.numpy as jnp
from jax import lax
from jax.experimental import pallas as pl
from jax.experimental.pallas import tpu as pltpu

F32 = jnp.float32
BF16 = jnp.bfloat16
MESH = pl.DeviceIdType.MESH

D_MODEL = 1024
LRU_HEADS = 8
HEAD_DIM = 128
CONV_WIDTH = 4
LRU_C = 8.0
POOL_WIDTH = 512
POOL_WINDOWS = (2, 4, 8, 16)
POOL_GROUP_DIM = 128
IN_COLS = 5120
N_CHIPS = 4
EPS = 1e-6

ADAM_LR = 0.001
ADAM_B1 = 0.9
ADAM_B2 = 0.999
ADAM_EPS = 1e-08
ADAM_WD = 0.01
ADAM_STEP = 10

F32_SUBLANES = 8
CONV_HIST = 8
POOL_HIST = 16
VMEM_LIMIT_BYTES = 58 * 1024 * 1024
VEC_BAG_SLOTS = ("norm_g", "conv_w", "conv_b", "lru_b_a", "lru_b_x", "lru_lambda", "pool_scale", "ple_norm_g",
                 "final_g", "loss")
VEC_BAG_ROWS = 128
MAT_BAG_AT = {"lru_w_a": 0, "lru_w_x": LRU_HEADS * HEAD_DIM, "pool_w": 2 * LRU_HEADS * HEAD_DIM}
MAT_BAG_ROWS = 2 * LRU_HEADS * HEAD_DIM + len(POOL_WINDOWS) * POOL_GROUP_DIM


def _bag_row(name, k=0):
    at = F32_SUBLANES * VEC_BAG_SLOTS.index(name) + k
    return slice(at, at + 1)


def _bag_rows(name):
    at = F32_SUBLANES * VEC_BAG_SLOTS.index(name)
    return slice(at, at + F32_SUBLANES)


def _dot(a, b):
    return jnp.dot(a, b, preferred_element_type=F32)


def _dot_nt(a, b):
    return lax.dot_general(a, b, (((1,), (1,)), ((), ())), preferred_element_type=F32)


def _dot_tn(a, b):
    return lax.dot_general(a, b, (((0,), (0,)), ((), ())), preferred_element_type=F32)


def _sigmoid(v):
    return jax.nn.sigmoid(v)


def _softplus(v):
    return jnp.maximum(v, 0.0) + jnp.log1p(jnp.exp(-jnp.abs(v)))


def _place():
    return lax.axis_index("x"), lax.axis_index("y"), lax.axis_index("c")


GATHER_SEMS = 6


def _gather_shapes(shards):
    out_shape = []
    for arr, axis, _ in shards:
        r, cols = arr.shape
        out_shape.append(jax.ShapeDtypeStruct((N_CHIPS * r, cols) if axis == 0 else (r, N_CHIPS * cols), arr.dtype))
    n = len(shards)
    sems = [pltpu.SemaphoreType.DMA((n * GATHER_SEMS,)), pltpu.SemaphoreType.DMA((n * GATHER_SEMS,)),
            pltpu.SemaphoreType.DMA((n,))]
    return out_shape, sems


def _gather_steps(shards, ins, outs, send_sems, recv_sems, local_sems):
    n = len(shards)
    x, y, c = _place()
    me, sibling = (x, y, c), (x, y, 1 - c)
    chips = [(x, 1 - y), (1 - x, y), (1 - x, 1 - y)]

    def region(k, cx, cy, hc):
        (r, cols), axis = shards[k][0].shape, shards[k][1]
        j = 2 * cx + cy
        if axis == 0:
            if hc is None:
                return outs[k].at[pl.ds(j * r, r), :]
            return outs[k].at[pl.ds(j * r + hc * (r // 2), r // 2), :]
        if hc is None:
            return outs[k].at[:, pl.ds(j * cols, cols)]
        return outs[k].at[pl.ds(hc * (r // 2), r // 2), pl.ds(j * cols, cols)]

    def remote(k, sem, block, to, src=None):
        dst = region(k, *block)
        return pltpu.make_async_remote_copy(
            src_ref=dst if src is None else src, dst_ref=dst,
            send_sem=send_sems.at[k * GATHER_SEMS + sem], recv_sem=recv_sems.at[k * GATHER_SEMS + sem],
            device_id=to, device_id_type=MESH)

    def first(k, idx):
        r, split = shards[k][0].shape[0], shards[k][2]
        src = ins[k].at[pl.ds(c * (r // 2), r // 2), :] if split else ins[k]
        return remote(k, idx, (x, y, c if split else None), (*chips[idx], c), src=src)

    def relay(k):
        src_chip = (jnp.bitwise_xor(x, 1 - c), jnp.bitwise_xor(y, c))
        dst_chip = (jnp.bitwise_xor(x, c), jnp.bitwise_xor(y, 1 - c))
        return remote(k, 2, (*src_chip, c), (*dst_chip, c))

    def passed(k, idx):
        return remote(k, 3 + idx, (*chips[idx], c), sibling)

    def mine(k):
        return pltpu.make_async_copy(ins[k], region(k, x, y, None), local_sems.at[k])

    def start():
        for k in range(n):
            mine(k).start()
            for idx in range(2 if shards[k][2] else 3):
                first(k, idx).start()

    def relay_on():
        for k in range(n):
            split = shards[k][2]
            for idx in range(2):
                remote(k, idx, (*chips[idx], c if split else None), me).wait_recv()
            if split:
                relay(k).start()
                passed(k, 0).start()
                passed(k, 1).start()

    def finish():
        for k in range(n):
            split = shards[k][2]
            remote(k, 2, (*chips[2], c if split else None), me).wait_recv()
            if split:
                passed(k, 2).start()
        for k in range(n):
            if shards[k][2]:
                for idx in range(3):
                    remote(k, 3 + idx, (*chips[idx], 1 - c), me).wait_recv()
        for k in range(n):
            if shards[k][2]:
                for cp in (first(k, 0), first(k, 1), relay(k), passed(k, 0), passed(k, 1), passed(k, 2)):
                    cp.wait_send()
            else:
                for idx in range(3):
                    first(k, idx).wait_send()
            mine(k).wait()

    return start, relay_on, finish


RS_ADD_ROWS = (64, 32, 16, 8)


N_DEV = 2 * N_CHIPS


def _all_reduce_scratch(shape):
    return [pltpu.VMEM((N_DEV,) + tuple(shape), F32), pltpu.SemaphoreType.DMA((N_DEV - 1,)),
            pltpu.SemaphoreType.DMA((N_DEV - 1,))]


def _all_reduce_tile(v_ref, o_ref, slots, send_sems, recv_sems):
    flips = [(dx, dy, dc) for dx in (0, 1) for dy in (0, 1) for dc in (0, 1)][1:]
    x, y, c = _place()
    mine = 4 * x + 2 * y + c

    def copy(k, to_flip, slot):
        dx, dy, dc = to_flip
        peer = (jnp.bitwise_xor(x, dx), jnp.bitwise_xor(y, dy), jnp.bitwise_xor(c, dc))
        return pltpu.make_async_remote_copy(
            src_ref=v_ref, dst_ref=slots.at[slot], send_sem=send_sems.at[k], recv_sem=recv_sems.at[k],
            device_id=peer, device_id_type=MESH)

    sends = [copy(k, flip, mine) for k, flip in enumerate(flips)]
    for cp in sends:
        cp.start()
    slots[mine] = v_ref[...]
    for k, (dx, dy, dc) in enumerate(flips):
        copy(k, (dx, dy, dc), jnp.bitwise_xor(mine, 4 * dx + 2 * dy + dc)).wait_recv()
    total = slots[0]
    for d in range(1, N_DEV):
        total = total + slots[d]
    o_ref[...] = total
    for cp in sends:
        cp.wait_send()


RS_SEMS = 8
RS_LOCAL_SEMS = 5


def _rs_piece_shape(part):
    arr, cols = part[0], part[1]
    return (arr.shape[0] // 2, arr.shape[1] // N_CHIPS) if cols else tuple(arr.shape[1:])


def _rs_operands(parts):
    return [p[0] for p in parts] + [p[0] if p[2] is None else p[2] for p in parts]


def _rs_wires(parts, wire):
    return list(wire) if isinstance(wire, (list, tuple)) else [wire] * len(parts)


def _rs_shapes(parts, wire):
    n = len(parts)
    shapes = [_rs_piece_shape(p) for p in parts]
    out_shape = [jax.ShapeDtypeStruct((2,) + s, F32) for s in shapes]
    scratch = []
    for lead, kind in ((N_CHIPS, "f32"), (N_CHIPS, "narrow"), (N_CHIPS, "wire"), (None, "f32"), (N_CHIPS, "wire")):
        for s, p, w in zip(shapes, parts, _rs_wires(parts, wire)):
            dtype = {"f32": F32, "narrow": F32 if p[2] is None else p[2].dtype, "wire": w}[kind]
            scratch.append(pltpu.VMEM(s if lead is None else (lead,) + s, dtype))
    scratch += [pltpu.SemaphoreType.DMA((n * RS_SEMS,)), pltpu.SemaphoreType.DMA((n * RS_SEMS,)),
                pltpu.SemaphoreType.DMA((n * RS_LOCAL_SEMS,))]
    return out_shape, scratch


def _rs_steps(parts, ins, outs, scratch):
    n = len(parts)
    own, sib, got, fin, snd = (scratch[k * n:(k + 1) * n] for k in range(5))
    send_sems, recv_sems, local_sems = scratch[5 * n:]
    shapes = [_rs_piece_shape(p) for p in parts]
    x, y, c = _place()
    j_me = 2 * x + y
    me, sibling = (x, y, c), (x, y, 1 - c)

    def piece(a, jj, core, narrow=False):
        ref = ins[n + a] if narrow else ins[a]
        if parts[a][1]:
            r, cl = shapes[a]
            return ref.at[pl.ds(core * r, r), pl.ds(jj * cl, cl)]
        return ref.at[2 * jj + core]

    def remote(a, sem, src, dst, to):
        return pltpu.make_async_remote_copy(
            src_ref=src, dst_ref=dst, send_sem=send_sems.at[a * RS_SEMS + sem],
            recv_sem=recv_sems.at[a * RS_SEMS + sem], device_id=to, device_id_type=MESH)

    def rows_loop(a, fn):
        r = shapes[a][0]
        step = max(s for s in RS_ADD_ROWS if r % s == 0)

        def it(i, carry):
            fn(pl.ds(pl.multiple_of(i * step, step), step))
            return carry

        lax.fori_loop(0, r // step, it, 0)

    def load(a, jj):
        return pltpu.make_async_copy(piece(a, jj, c), own[a].at[jj], local_sems.at[a * RS_LOCAL_SEMS + jj])

    def to_sibling(a, jj):
        return remote(a, jj, piece(a, jj, 1 - c, narrow=True), sib[a].at[jj], sibling)

    near = (jnp.bitwise_xor(x, 1 - c), jnp.bitwise_xor(y, c))
    far = (jnp.bitwise_xor(x, c), jnp.bitwise_xor(y, 1 - c))
    diag = (1 - x, 1 - y)
    FROM_NEAR, FROM_FAR, FEED = 0, 1, 2

    def chip_of(chip):
        return 2 * chip[0] + chip[1]

    def feed(a):
        return remote(a, 4, snd[a].at[chip_of(diag)], got[a].at[FEED], (*near, c))

    def to_near(a):
        return remote(a, 5, snd[a].at[chip_of(near)], got[a].at[FROM_NEAR], (*near, c))

    def to_far(a):
        return remote(a, 6, snd[a].at[chip_of(far)], got[a].at[FROM_FAR], (*far, c))

    def store(a):
        return pltpu.make_async_copy(fin[a], outs[a].at[c], local_sems.at[a * RS_LOCAL_SEMS + 4])

    def result_to_sibling(a):
        return remote(a, 7, fin[a], outs[a].at[c], sibling)

    def exchange():
        for a in range(n):
            for jj in range(N_CHIPS):
                load(a, jj).start()
                to_sibling(a, jj).start()

    def chip_sums():
        for a in range(n):
            for jj in range(N_CHIPS):
                load(a, jj).wait()
                remote(a, jj, sib[a].at[jj], sib[a].at[jj], me).wait_recv()

                def add(sl, a=a, jj=jj):
                    q = own[a][jj, sl, :] + sib[a][jj, sl, :].astype(F32)
                    own[a][jj, sl, :] = q
                    snd[a][jj, sl, :] = q.astype(snd[a].dtype)

                rows_loop(a, add)
        for a in range(n):
            feed(a).start()
        for a in range(n):
            to_near(a).start()

    def relay():
        for a in range(n):
            remote(a, 4, got[a].at[FEED], got[a].at[FEED], me).wait_recv()

            def add(sl, a=a):
                pair = own[a][chip_of(far), sl, :] + got[a][FEED, sl, :].astype(F32)
                snd[a][chip_of(far), sl, :] = pair.astype(snd[a].dtype)

            rows_loop(a, add)
            to_far(a).start()

    def totals():
        for a in range(n):
            remote(a, 5, got[a].at[FROM_NEAR], got[a].at[FROM_NEAR], me).wait_recv()
            remote(a, 6, got[a].at[FROM_FAR], got[a].at[FROM_FAR], me).wait_recv()

            def total(sl, a=a):
                fin[a][sl, :] = (own[a][j_me, sl, :] + got[a][FROM_NEAR, sl, :].astype(F32)) + (
                    got[a][FROM_FAR, sl, :].astype(F32))

            rows_loop(a, total)
            store(a).start()
            result_to_sibling(a).start()

    def finish():
        for a in range(n):
            remote(a, 7, outs[a].at[1 - c], outs[a].at[1 - c], me).wait_recv()
        for a in range(n):
            for jj in range(N_CHIPS):
                to_sibling(a, jj).wait_send()
            for cp in (feed(a), to_near(a), to_far(a), result_to_sibling(a)):
                cp.wait_send()
            store(a).wait()

    return exchange, chip_sums, relay, totals, finish


def _rms(x):
    r = lax.rsqrt(jnp.mean(x * x, axis=-1, keepdims=True) + EPS)
    return x * r, r


def _rms_bwd(dxn, xn, r):
    return r * (dxn - xn * jnp.mean(dxn * xn, axis=-1, keepdims=True))


def _in_proj_gather(x2d, norm_g, w_in_sh, shards, tb, casts):
    t = x2d.shape[0]
    nb = t // tb
    cols = IN_COLS // N_CHIPS
    half = D_MODEL // 2
    n = len(shards)
    nc = len(casts)

    def body(x_ref, g_ref, win_ref, *refs):
        ins, cast_ins = refs[:n], refs[n:n + nc]
        z_ref, h_ref, wfull_ref = refs[n + nc:n + nc + 3]
        outs, cast_outs = refs[n + nc + 3:2 * n + nc + 3], refs[2 * n + nc + 3:2 * (n + nc) + 3]
        scratch = refs[2 * (n + nc) + 3:]
        wv, h_all, send_sems, recv_sems, local_sems, w_send, w_recv, w_local, stage = scratch[:9]
        wide, narrow, cast_sems = scratch[9:9 + nc], scratch[9 + nc:9 + 2 * nc], scratch[9 + 2 * nc]
        s, i = pl.program_id(0), pl.program_id(1)
        x, y, c = _place()
        me, sibling = (x, y, c), (x, y, 1 - c)
        chips = [(x, 1 - y), (1 - x, y), (1 - x, 1 - y)]

        def w_half(cx, cy, hc):
            return wv.at[2 * cx + cy, pl.ds(hc * half, half), :]

        def w_remote(sem, block, to, src=None):
            dst = w_half(*block)
            return pltpu.make_async_remote_copy(
                src_ref=dst if src is None else src, dst_ref=dst, send_sem=w_send.at[sem],
                recv_sem=w_recv.at[sem], device_id=to, device_id_type=MESH)

        def w_first(idx):
            return w_remote(idx, (x, y, c), (*chips[idx], c))

        def w_relay():
            src_chip = (jnp.bitwise_xor(x, 1 - c), jnp.bitwise_xor(y, c))
            dst_chip = (jnp.bitwise_xor(x, c), jnp.bitwise_xor(y, 1 - c))
            return w_remote(2, (*src_chip, c), (*dst_chip, c))

        def w_pass(idx):
            return w_remote(3 + idx, (*chips[idx], c), sibling)

        def w_store(k, cx, cy):
            jj = 2 * cx + cy
            return pltpu.make_async_copy(wv.at[jj], wfull_ref.at[:, pl.ds(jj * cols, cols)], w_local.at[k])

        start_rest, relay_rest, finish_rest = _gather_steps(shards, ins, outs, send_sems, recv_sems, local_sems)

        def own(k, hc):
            return pltpu.make_async_copy(win_ref.at[pl.ds(pl.multiple_of(hc * half, half), half), :], stage.at[k],
                                         w_local.at[4 + 2 * k])

        def round_own(k, hc):
            own(k, hc).wait()
            wv[2 * x + y, pl.ds(pl.multiple_of(hc * half, half), half), :] = stage[k].astype(BF16)

        wide_in = [pltpu.make_async_copy(cast_ins[k], wide[k], cast_sems.at[k]) for k in range(nc)]
        narrow_out = [pltpu.make_async_copy(narrow[k], cast_outs[k], cast_sems.at[nc + k]) for k in range(nc)]

        @pl.when((s == 0) & (i == 0))
        def _():
            own(0, c).start()
            own(1, 1 - c).start()
            for cp in wide_in:
                cp.start()
            round_own(0, c)
            w_first(0).start()
            w_first(1).start()
            start_rest()
            round_own(1, 1 - c)
            w_store(0, x, y).start()

        @pl.when((s == 1) & (i == 0))
        def _():
            for k in range(nc):
                wide_in[k].wait()
                narrow[k][...] = wide[k][...].astype(BF16)
                narrow_out[k].start()
            w_remote(0, (*chips[0], c), me).wait_recv()
            w_remote(1, (*chips[1], c), me).wait_recv()
            w_relay().start()
            w_pass(0).start()
            w_pass(1).start()
            w_remote(3, (*chips[0], 1 - c), me).wait_recv()
            w_store(1, *chips[0]).start()

        @pl.when((s == 2) & (i == 0))
        def _():
            w_remote(4, (*chips[1], 1 - c), me).wait_recv()
            w_store(2, *chips[1]).start()

        @pl.when((s == 3) & (i == 0))
        def _():
            w_remote(2, (*chips[2], c), me).wait_recv()
            w_pass(2).start()
            w_remote(5, (*chips[2], 1 - c), me).wait_recv()
            w_store(3, *chips[2]).start()

        keep_h = pltpu.make_async_copy(h_all.at[i], h_ref.at[pl.ds(pl.multiple_of(i * tb, tb), tb), :], w_local.at[5])

        @pl.when(s == 0)
        def _():
            xn, _ = _rms(x_ref[...])
            h_all[i] = (xn * g_ref[...]).astype(BF16)
            keep_h.start()

        z_ref[...] = _dot(h_all[i], wv[jnp.bitwise_xor(2 * x + y, s)])
        pl.when(s == 0)(keep_h.wait)

        @pl.when((s == N_CHIPS - 1) & (i == nb - 1))
        def _():
            relay_rest()
            finish_rest()
            for cp in (w_first(0), w_first(1), w_relay(), w_pass(0), w_pass(1), w_pass(2)):
                cp.wait_send()
            w_store(0, x, y).wait()
            for idx in range(3):
                w_store(idx + 1, *chips[idx]).wait()
            for cp in narrow_out:
                cp.wait()

    rest_shape, rest_sems = _gather_shapes(shards)
    out_shape = [jax.ShapeDtypeStruct((t, IN_COLS), F32), jax.ShapeDtypeStruct((t, D_MODEL), BF16),
                 jax.ShapeDtypeStruct((D_MODEL, IN_COLS), BF16)] + rest_shape
    out_shape += [jax.ShapeDtypeStruct(a.shape, BF16) for a in casts]
    any_spec = pl.BlockSpec(memory_space=pl.ANY)

    def z_map(s, i):
        return (i, jnp.bitwise_xor(2 * lax.axis_index("x") + lax.axis_index("y"), s))

    return pl.pallas_call(
        body, name="in_proj", out_shape=tuple(out_shape),
        grid=(N_CHIPS, nb),
        in_specs=[pl.BlockSpec((tb, D_MODEL), lambda s, i: (jnp.where(s == 0, i, nb - 1), 0)),
                  pl.BlockSpec((1, D_MODEL), lambda s, i: (0, 0)), any_spec] + [any_spec] * (n + nc),
        out_specs=tuple([pl.BlockSpec((tb, cols), z_map), any_spec, any_spec] + [any_spec] * (n + nc)),
        scratch_shapes=[pltpu.VMEM((N_CHIPS, D_MODEL, cols), BF16), pltpu.VMEM((nb, tb, D_MODEL), BF16)] + rest_sems + [
            pltpu.SemaphoreType.DMA((GATHER_SEMS,)), pltpu.SemaphoreType.DMA((GATHER_SEMS,)),
            pltpu.SemaphoreType.DMA((N_CHIPS + 3,)), pltpu.VMEM((2, half, cols), F32)]
        + [pltpu.VMEM(a.shape, F32) for a in casts] + [pltpu.VMEM(a.shape, BF16) for a in casts]
        + [pltpu.SemaphoreType.DMA((2 * nc,))],
        compiler_params=pltpu.CompilerParams(dimension_semantics=("arbitrary", "arbitrary"),
                                             vmem_limit_bytes=VMEM_LIMIT_BYTES),
    )(x2d, norm_g, w_in_sh, *[sh[0] for sh in shards], *casts)


def _in_proj_bwd(dz, w_in, x2d, dx_res, norm_g, tb, reduce, shards, take):
    t = x2d.shape[0]
    nb = t // tb
    parts, wire, steps = reduce
    n = len(parts)
    k = len(shards)
    take_rows, take_width = take

    def body(dz_ref, w_ref, x_ref, dres_ref, g_ref, *refs):
        at = 2 * n + k
        dx_ref, dg_ref = refs[at:at + 2]
        rs_outs, g_outs = refs[at + 2:at + 2 + n], refs[at + 2 + n:at + 2 + n + k]
        cut_ref = refs[at + 2 + n + k]
        scratch = refs[at + 3 + n + k:]
        rs_scr, g_sems, dg_acc, ar_scr, cut_sem = scratch[:-8], scratch[-8:-5], scratch[-5], scratch[-4:-1], scratch[-1]
        rs = _rs_steps(parts, refs[:2 * n], rs_outs, rs_scr)
        for step, when in zip(rs[:-1], steps):
            pl.when(pl.program_id(0) == when)(step)
        gather = _gather_steps(shards, refs[2 * n:at], g_outs, *g_sems)
        for step, when in zip(gather, (0, nb // 2, nb - 1)):
            pl.when(pl.program_id(0) == when)(step)

        @pl.when(pl.program_id(0) == 0)
        def _():
            dg_acc[...] = jnp.zeros_like(dg_acc)

        xn, r = _rms(x_ref[...])
        g = g_ref[...]
        dh = _dot_nt(dz_ref[...], w_ref[...])
        dg_acc[0:1, :] += jnp.sum(dh * xn, axis=0, keepdims=True)
        dx_ref[...] = dres_ref[...] + _rms_bwd(dh * g, xn, r)

        @pl.when(pl.program_id(0) == nb - 1)
        def _():
            x, y, _ = _place()
            mine = pl.ds(pl.multiple_of((2 * x + y) * take_width, take_width), take_width)
            cut = pltpu.make_async_copy(g_outs[0].at[take_rows, mine], cut_ref, cut_sem)
            cut.start()
            _all_reduce_tile(dg_acc, dg_ref, *ar_scr)
            rs[-1]()
            cut.wait()

    row = lambda i: (i, 0)
    fixed = lambda i: (0, 0)
    rs_shape, rs_scratch = _rs_shapes(parts, wire)
    g_shape, g_sems = _gather_shapes(shards)
    any_spec = pl.BlockSpec(memory_space=pl.ANY)
    cut_shape = jax.ShapeDtypeStruct((take_rows.stop - take_rows.start, take_width), F32)
    return pl.pallas_call(
        body, name="in_proj_bwd",
        out_shape=tuple([jax.ShapeDtypeStruct((t, D_MODEL), F32), jax.ShapeDtypeStruct((F32_SUBLANES, D_MODEL), F32)]
                        + rs_shape + g_shape + [cut_shape]),
        grid=(nb,),
        in_specs=[pl.BlockSpec((tb, IN_COLS), row),
                  pl.BlockSpec((D_MODEL, IN_COLS), fixed, pipeline_mode=pl.Buffered(1)),
                  pl.BlockSpec((tb, D_MODEL), row), pl.BlockSpec((tb, D_MODEL), row),
                  pl.BlockSpec((1, D_MODEL), fixed)] + [any_spec] * (2 * n + k),
        out_specs=tuple([pl.BlockSpec((tb, D_MODEL), row), pl.BlockSpec((F32_SUBLANES, D_MODEL), fixed)]
                        + [any_spec] * (n + k + 1)),
        scratch_shapes=rs_scratch + g_sems + [pltpu.VMEM((F32_SUBLANES, D_MODEL), F32)] + _all_reduce_scratch(
            (F32_SUBLANES, D_MODEL)) + [pltpu.SemaphoreType.DMA(())],
        compiler_params=pltpu.CompilerParams(dimension_semantics=("arbitrary",),
                                             vmem_limit_bytes=VMEM_LIMIT_BYTES),
    )(dz, w_in, x2d, dx_res, norm_g, *_rs_operands(parts), *[sh[0] for sh in shards])


def _weight_grad(pairs, n_chunks, tb, name, reduce=None):
    t = pairs[0][0].shape[0]
    nb = t // tb
    m = len(pairs)
    parts, wire, steps = reduce if reduce is not None else ([], F32, ())
    n = len(parts)

    def body(*refs):
        lr, refs = refs[:2 * m], refs[2 * m:]
        o_refs = refs[2 * n:2 * n + 2 * m]
        if n:
            at = pl.program_id(0) * nb + pl.program_id(1)
            rs = _rs_steps(parts, refs[:2 * n], refs[2 * n + 2 * m:3 * n + 2 * m], refs[3 * n + 2 * m:])
            for step, when in zip(rs, steps):
                pl.when(at == when)(step)

        @pl.when(pl.program_id(1) == 0)
        def _():
            for q in range(m):
                o_refs[2 * q][...] = jnp.zeros_like(o_refs[2 * q])

        for q in range(m):
            o_refs[2 * q][...] += _dot_tn(lr[2 * q][...], lr[2 * q + 1][...])

        @pl.when(pl.program_id(1) == nb - 1)
        def _():
            for q in range(m):
                o_refs[2 * q + 1][...] = o_refs[2 * q][...].astype(BF16)

    rs_shape, rs_scratch = _rs_shapes(parts, wire) if n else ([], [])
    any_spec = pl.BlockSpec(memory_space=pl.ANY)
    in_specs, out_specs, out_shape = [], [], []
    for lhs, rhs in pairs:
        k, nc = lhs.shape[1], rhs.shape[1] // n_chunks
        in_specs += [pl.BlockSpec((tb, k), lambda j, i: (i, 0)), pl.BlockSpec((tb, nc), lambda j, i: (i, j))]
        out_specs += [pl.BlockSpec((None, k, nc), lambda j, i: (j, 0, 0))] * 2
        out_shape += [jax.ShapeDtypeStruct((n_chunks, k, nc), F32), jax.ShapeDtypeStruct((n_chunks, k, nc), BF16)]
    return pl.pallas_call(
        body, name=name, out_shape=tuple(out_shape + rs_shape),
        grid=(n_chunks, nb),
        in_specs=in_specs + [any_spec] * (2 * n),
        out_specs=tuple(out_specs + [any_spec] * n),
        scratch_shapes=rs_scratch,
        compiler_params=pltpu.CompilerParams(dimension_semantics=("arbitrary", "arbitrary"),
                                             vmem_limit_bytes=VMEM_LIMIT_BYTES),
    )(*[a for pair in pairs for a in pair], *_rs_operands(parts))


def _adam_update(w, g, m, v):
    m_ = ADAM_B1 * m + (1.0 - ADAM_B1) * g
    v_ = ADAM_B2 * v + (1.0 - ADAM_B2) * jnp.square(g)
    m_hat = m_ / (1.0 - ADAM_B1 ** ADAM_STEP)
    v_hat = v_ / (1.0 - ADAM_B2 ** ADAM_STEP)
    return -ADAM_LR * (m_hat / (jnp.sqrt(v_hat) + ADAM_EPS) + ADAM_WD * w), m_, v_


def _adamw_replicated(vec_sum, mat_sum, norm_grad, entries, conv):
    n = len(entries)

    def grad_of(name, shape, vec_ref, mat_ref, norm_ref):
        if name == "norm_g":
            return norm_ref[0:1, :]
        if name in MAT_BAG_AT:
            return mat_ref[MAT_BAG_AT[name]:MAT_BAG_AT[name] + shape[0], :]
        if shape[0] == 1:
            return vec_ref[_bag_row(name), 0:shape[1]]
        return jnp.concatenate([vec_ref[_bag_row(name), h * shape[1]:(h + 1) * shape[1]] for h in range(shape[0])],
                               axis=0)

    def body(vec_ref, mat_ref, norm_ref, *refs):
        ins, outs = refs[:3 * n + 4], refs[3 * n + 4:]
        for k in range(n):
            w_ref, m_ref, v_ref = ins[3 * k:3 * k + 3]
            g = grad_of(entries[k][0], w_ref.shape, vec_ref, mat_ref, norm_ref)
            d, m_, v_ = _adam_update(w_ref[...], g, m_ref[...], v_ref[...])
            for ref, val in zip(outs[4 * k:4 * k + 4], (g, d, m_, v_)):
                ref[...] = val
        w_ref, m_ref, v_ref, g_ref = ins[3 * n:]
        g = g_ref[0:w_ref.shape[0], :]
        for ref, val in zip(outs[4 * n:4 * n + 4], (g,) + _adam_update(w_ref[...], g, m_ref[...], v_ref[...])):
            ref[...] = val
        outs[4 * n + 4][...] = vec_ref[_bag_row("loss"), 0:1]

    arrays = [a for e in entries for a in e[1:]] + list(conv)
    out_shape = [jax.ShapeDtypeStruct(e[1].shape, F32) for e in entries for _ in range(4)]
    out_shape += [jax.ShapeDtypeStruct(conv[0].shape, F32)] * 4 + [jax.ShapeDtypeStruct((1, 1), F32)]
    return pl.pallas_call(
        body, name="adamw_replicated", out_shape=tuple(out_shape),
        compiler_params=pltpu.CompilerParams(vmem_limit_bytes=VMEM_LIMIT_BYTES),
    )(vec_sum, mat_sum, norm_grad, *arrays)


def _adamw(w, g, m, v, rows, name):
    r, c = w.shape

    def body(w_ref, g_ref, m_ref, v_ref, go_ref, d_ref, nm_ref, nv_ref):
        g = g_ref[...]
        go_ref[...] = g
        d_ref[...], nm_ref[...], nv_ref[...] = _adam_update(w_ref[...], g, m_ref[...], v_ref[...])

    spec = pl.BlockSpec((rows, c), lambda i: (i, 0))
    return pl.pallas_call(
        body, name=name, out_shape=tuple(jax.ShapeDtypeStruct((r, c), F32) for _ in range(4)),
        grid=(r // rows,), in_specs=[spec] * 4, out_specs=(spec,) * 4,
        compiler_params=pltpu.CompilerParams(dimension_semantics=("arbitrary",),
                                             vmem_limit_bytes=VMEM_LIMIT_BYTES),
    )(w, g, m, v)


def _adamw_group(items, name):
    n = 4 * len(items)

    def body(*refs):
        ins, outs, bufs = refs[:n], refs[n:2 * n], refs[2 * n:3 * n]
        load_sems, store_sems = refs[3 * n:]
        loads = [pltpu.make_async_copy(ins[j], bufs[j], load_sems.at[j]) for j in range(n)]
        stores = [pltpu.make_async_copy(bufs[j], outs[j], store_sems.at[j]) for j in range(n)]
        for cp in loads:
            cp.start()
        for k in range(len(items)):
            for cp in loads[4 * k:4 * k + 4]:
                cp.wait()
            w_buf, g_buf, m_buf, v_buf = bufs[4 * k:4 * k + 4]
            w_buf[...], m_buf[...], v_buf[...] = _adam_update(w_buf[...], g_buf[...], m_buf[...], v_buf[...])
            for cp in stores[4 * k:4 * k + 4]:
                cp.start()
        for cp in stores:
            cp.wait()

    arrays = [a for item in items for a in item]
    any_spec = pl.BlockSpec(memory_space=pl.ANY)
    flat = pl.pallas_call(
        body, name=name, out_shape=tuple(jax.ShapeDtypeStruct(a.shape, F32) for a in arrays),
        in_specs=[any_spec] * n, out_specs=(any_spec,) * n,
        scratch_shapes=[pltpu.VMEM(a.shape, F32) for a in arrays] + [pltpu.SemaphoreType.DMA((n,))] * 2,
        compiler_params=pltpu.CompilerParams(vmem_limit_bytes=VMEM_LIMIT_BYTES),
    )(*arrays)
    return [(flat[4 * k + 1], flat[4 * k], flat[4 * k + 2], flat[4 * k + 3]) for k in range(len(items))]


def _shift_down(ext, s):
    return pltpu.roll(ext, s, 0)


def _tile_shift(v, s):
    rows, cols = v.shape
    tiles = v.reshape(rows // F32_SUBLANES, F32_SUBLANES, cols)
    return pltpu.roll(tiles, s % F32_SUBLANES, 1).reshape(rows, cols)


def _shift_up(ext, s):
    return pltpu.roll(ext, ext.shape[0] - s, 0)


def _lru_gates(xc, wa_ref, ba, wx_ref, bx, lam):
    pa, px = [], []
    for h in range(LRU_HEADS):
        xh = xc[:, h * HEAD_DIM:(h + 1) * HEAD_DIM].astype(BF16)
        pa.append(_dot(xh, wa_ref[h]))
        px.append(_dot(xh, wx_ref[h]))
    r = _sigmoid(jnp.concatenate(pa, axis=1) + ba)
    ig = _sigmoid(jnp.concatenate(px, axis=1) + bx)
    sp = _softplus(-lam)
    log_a = (-LRU_C * r) * sp
    a = jnp.exp(log_a)
    mult = jnp.sqrt(jnp.tanh(-log_a) * (1.0 + a * a))
    return r, ig, a, mult, sp


def _conv(ext, w_ref, b):
    y = b + _shift_down(ext, 3) * w_ref[0:1, :]
    y = y + _shift_down(ext, 2) * w_ref[1:2, :]
    y = y + _shift_down(ext, 1) * w_ref[2:3, :]
    y = y + ext * w_ref[3:4, :]
    return y[CONV_HIST:, :]


def _pool_diff(ext, pos):
    out = []
    for g, k in enumerate(POOL_WINDOWS):
        col = ext[:, g * POOL_GROUP_DIM:(g + 1) * POOL_GROUP_DIM]
        s = col
        for step in range(g + 1):
            s = s + _shift_down(s, 2 ** step)
        count = jnp.minimum(pos + 1, k).astype(F32)
        out.append(s[POOL_HIST:, :] / count - col[POOL_HIST:, :])
    return out


def _pool_mix(diff, pw_ref):
    return jnp.concatenate([_dot(diff[g].astype(BF16), pw_ref[g]) for g in range(len(POOL_WINDOWS))], axis=1)


def _branch_specs(tb, row_map, fixed):
    fixed3 = lambda i: (0, 0, 0)
    return [pl.BlockSpec((CONV_WIDTH, D_MODEL), fixed), pl.BlockSpec((1, D_MODEL), fixed),
            pl.BlockSpec((LRU_HEADS, HEAD_DIM, HEAD_DIM), fixed3), pl.BlockSpec((1, D_MODEL), fixed),
            pl.BlockSpec((LRU_HEADS, HEAD_DIM, HEAD_DIM), fixed3), pl.BlockSpec((1, D_MODEL), fixed),
            pl.BlockSpec((1, D_MODEL), fixed),
            pl.BlockSpec((len(POOL_WINDOWS), POOL_GROUP_DIM, POOL_GROUP_DIM), fixed3),
            pl.BlockSpec((1, POOL_WIDTH), fixed)]


def _branches_fwd(z, weights, seq, tb, shards):
    t = z.shape[0]
    nb = t // tb
    nbe = seq // tb
    groups = tb // F32_SUBLANES
    n = len(shards)

    def body(xa_ref, ga_ref, xb_ref, gb_ref, cw_ref, cb_ref, wa_ref, ba_ref, wx_ref, bx_ref, lam_ref,
             pw_ref, ps_ref, *refs):
        g_ins = refs[:n]
        ya_ref, yb_ref, hl_ref = refs[n:n + 3]
        g_outs = refs[n + 3:2 * n + 3]
        xa_ext, xb_ext, carry, a_s, u_s, send_sems, recv_sems, local_sems = refs[2 * n + 3:]
        blk = pl.program_id(0) % nbe
        start_gather, relay_gather, finish_gather = _gather_steps(shards, g_ins, g_outs, send_sems, recv_sems,
                                                                  local_sems)
        pl.when(pl.program_id(0) == 0)(start_gather)
        pl.when(pl.program_id(0) == nb // 2)(relay_gather)

        @pl.when(blk == 0)
        def _():
            xa_ext[0:CONV_HIST, :] = jnp.zeros((CONV_HIST, D_MODEL), F32)
            xb_ext[0:POOL_HIST, :] = jnp.zeros((POOL_HIST, POOL_WIDTH), F32)
            carry[...] = jnp.zeros_like(carry)

        xa_ext[CONV_HIST:, :] = xa_ref[...]
        xb_ext[POOL_HIST:, :] = xb_ref[...]
        ea = xa_ext[...]
        eb = xb_ext[...]
        xa_ext[0:CONV_HIST, :] = ea[tb:, :]
        xb_ext[0:POOL_HIST, :] = eb[tb:, :]

        xc = _conv(ea, cw_ref, cb_ref[...])
        _, ig, a, mult, _ = _lru_gates(xc, wa_ref, ba_ref[...], wx_ref, bx_ref[...], lam_ref[...])
        u = mult * (ig * xc)
        row8 = lax.broadcasted_iota(jnp.int32, (tb, D_MODEL), 0) % F32_SUBLANES
        for s in (1, 2, 4):
            m = row8 >= s
            u = jnp.where(m, a * _tile_shift(u, s) + u, u)
            a = jnp.where(m, a * _tile_shift(a, s), a)
        a_s[...] = a
        u_s[...] = u

        def step(g, cr):
            sl = pl.ds(pl.multiple_of(g * F32_SUBLANES, F32_SUBLANES), F32_SUBLANES)
            hb = a_s[sl, :] * cr + u_s[sl, :]
            hl_ref[sl, :] = hb
            return jnp.broadcast_to(hb[F32_SUBLANES - 1:F32_SUBLANES, :], (F32_SUBLANES, D_MODEL))

        carry[...] = lax.fori_loop(0, groups, step, carry[...], unroll=4)
        ga = ga_ref[...]
        ya_ref[...] = (hl_ref[...] * (ga * _sigmoid(ga))).astype(BF16)

        pos = blk * tb + lax.broadcasted_iota(jnp.int32, (tb, POOL_GROUP_DIM), 0)
        ypre = _pool_mix(_pool_diff(eb, pos), pw_ref)
        gb = gb_ref[...]
        yb_ref[...] = ((ypre * ps_ref[...]) * (gb * _sigmoid(gb))).astype(BF16)
        pl.when(pl.program_id(0) == nb - 1)(finish_gather)

    row = lambda i: (i, 0)
    fixed = lambda i: (0, 0)
    any_spec = pl.BlockSpec(memory_space=pl.ANY)
    in_specs = [pl.BlockSpec((tb, D_MODEL), lambda i: (i, 0)), pl.BlockSpec((tb, D_MODEL), lambda i: (i, 1)),
                pl.BlockSpec((tb, POOL_WIDTH), lambda i: (i, 4)), pl.BlockSpec((tb, POOL_WIDTH), lambda i: (i, 5)),
                ] + _branch_specs(tb, row, fixed) + [any_spec] * n
    g_shape, g_sems = _gather_shapes(shards)
    return pl.pallas_call(
        body, name="branches_fwd",
        out_shape=tuple([jax.ShapeDtypeStruct((t, D_MODEL), BF16), jax.ShapeDtypeStruct((t, POOL_WIDTH), BF16),
                         jax.ShapeDtypeStruct((t, D_MODEL), F32)] + g_shape),
        grid=(nb,), in_specs=in_specs,
        out_specs=tuple([pl.BlockSpec((tb, D_MODEL), row), pl.BlockSpec((tb, POOL_WIDTH), row),
                         pl.BlockSpec((tb, D_MODEL), row)] + [any_spec] * n),
        scratch_shapes=[pltpu.VMEM((tb + CONV_HIST, D_MODEL), F32), pltpu.VMEM((tb + POOL_HIST, POOL_WIDTH), F32),
                        pltpu.VMEM((F32_SUBLANES, D_MODEL), F32),
                        pltpu.VMEM((tb, D_MODEL), F32), pltpu.VMEM((tb, D_MODEL), F32)] + g_sems,
        compiler_params=pltpu.CompilerParams(dimension_semantics=("arbitrary",),
                                             vmem_limit_bytes=VMEM_LIMIT_BYTES),
    )(z, z, z, z, *weights, *[sh[0] for sh in shards])


def _branches_bwd(z, hl, dya, dyb, dzm, weights, vec_bag, seq, tb, riders):
    t = z.shape[0]
    nb = t // tb
    nbe = seq // tb
    groups = tb // F32_SUBLANES
    nr = len(riders)

    def body(xa_ref, xap_ref, ga_ref, xb_ref, xbp_ref, gb_ref, hl_ref, hlp_ref, dya_ref, dyb_ref, dzm_ref,
             cw_ref, cb_ref, wa_ref, ba_ref, wx_ref, bx_ref, lam_ref, pw_ref, ps_ref, vec_in_ref, *rest):
        pairs, (dz_ref, vec_ref, mat_ref), grads = rest[:2 * nr], rest[2 * nr:2 * nr + 3], rest[2 * nr + 3:4 * nr + 3]
        xa_ext, xb_ext, hl_ext, a_ext, dxc_ext, dwin_ext, g_carry, b_s, d_s, g_s = rest[4 * nr + 3:]
        i = pl.program_id(0)
        blk = (nb - 1 - i) % nbe

        def mat_rows(name, k):
            at = MAT_BAG_AT[name] + k * HEAD_DIM
            return slice(at, at + HEAD_DIM)

        def rider(k):
            grads[2 * k][...] += _dot_tn(pairs[2 * k][...], pairs[2 * k + 1][...])

        @pl.when(i == 0)
        def _():
            vec_ref[...] = vec_in_ref[...]
            mat_ref[...] = jnp.zeros_like(mat_ref)
            for k in range(nr):
                grads[2 * k][...] = jnp.zeros_like(grads[2 * k])

        @pl.when(blk == nbe - 1)
        def _():
            a_ext[tb:, :] = jnp.zeros((F32_SUBLANES, D_MODEL), F32)
            dxc_ext[tb:, :] = jnp.zeros((CONV_HIST, D_MODEL), F32)
            dwin_ext[tb:, :] = jnp.zeros((POOL_HIST, POOL_WIDTH), F32)
            g_carry[...] = jnp.zeros_like(g_carry)

        live = (blk > 0).astype(F32)
        xa_ext[0:CONV_HIST, :] = xap_ref[...] * live
        xa_ext[CONV_HIST:, :] = xa_ref[...]
        xb_ext[0:POOL_HIST, :] = xbp_ref[...] * live
        xb_ext[POOL_HIST:, :] = xb_ref[...]
        hl_ext[0:F32_SUBLANES, :] = hlp_ref[...] * live
        hl_ext[F32_SUBLANES:, :] = hl_ref[...]
        ea = xa_ext[...]
        eb = xb_ext[...]
        rider(0)

        xc = _conv(ea, cw_ref, cb_ref[...])
        lam = lam_ref[...]
        r, ig, a, mult, sp = _lru_gates(xc, wa_ref, ba_ref[...], wx_ref, bx_ref[...], lam)
        hl = hl_ref[...]
        ga = ga_ref[...]
        sga = _sigmoid(ga)
        dya = dya_ref[...]
        dhl = dya * (ga * sga)
        dz_ref[:, D_MODEL:2 * D_MODEL] = (dya * hl * (sga * (1.0 + ga * (1.0 - sga)))).astype(BF16)

        a_ext[0:tb, :] = a
        b = _shift_up(a_ext[...], 1)[0:tb, :]
        a_ext[tb:, :] = jnp.broadcast_to(a[0:1, :], (F32_SUBLANES, D_MODEL))
        d = dhl
        row8 = lax.broadcasted_iota(jnp.int32, (tb, D_MODEL), 0) % F32_SUBLANES
        for s in (1, 2, 4):
            m = row8 < F32_SUBLANES - s
            d = jnp.where(m, d + b * _tile_shift(d, -s), d)
            b = jnp.where(m, b * _tile_shift(b, -s), b)
        b_s[...] = b
        d_s[...] = d

        def step(k, cr):
            sl = pl.ds(pl.multiple_of((groups - 1 - k) * F32_SUBLANES, F32_SUBLANES), F32_SUBLANES)
            gb_ = d_s[sl, :] + b_s[sl, :] * cr
            g_s[sl, :] = gb_
            return jnp.broadcast_to(gb_[0:1, :], (F32_SUBLANES, D_MODEL))

        g_carry[...] = lax.fori_loop(0, groups, step, g_carry[...], unroll=4)
        rider(1)
        gsc = g_s[...]
        da = gsc * _shift_down(hl_ext[...], 1)[F32_SUBLANES:, :]
        dmult = gsc * (ig * xc)
        dig = gsc * (mult * xc)
        dxc = gsc * (mult * ig)
        dlog_a = da * a - (a * a) * dmult / mult
        dr = dlog_a * (-LRU_C * sp)
        vec_ref[_bag_row("lru_lambda"), :] += jnp.sum(dlog_a * (-LRU_C * r), axis=0, keepdims=True)
        dpa = dr * (r * (1.0 - r))
        dpx = dig * (ig * (1.0 - ig))
        vec_ref[_bag_row("lru_b_a"), :] += jnp.sum(dpa, axis=0, keepdims=True)
        vec_ref[_bag_row("lru_b_x"), :] += jnp.sum(dpx, axis=0, keepdims=True)
        back = []
        for h in range(LRU_HEADS):
            cols = slice(h * HEAD_DIM, (h + 1) * HEAD_DIM)
            xh = xc[:, cols].astype(BF16)
            dpa_h = dpa[:, cols].astype(BF16)
            dpx_h = dpx[:, cols].astype(BF16)
            mat_ref[mat_rows("lru_w_a", h), :] += _dot_tn(xh, dpa_h)
            mat_ref[mat_rows("lru_w_x", h), :] += _dot_tn(xh, dpx_h)
            back.append(_dot_nt(dpa_h, wa_ref[h]) + _dot_nt(dpx_h, wx_ref[h]))
        dxc = dxc + jnp.concatenate(back, axis=1)
        vec_ref[_bag_row("conv_b"), :] += jnp.sum(dxc, axis=0, keepdims=True)
        for k in range(CONV_WIDTH):
            tap = _shift_down(ea, CONV_WIDTH - 1 - k)[CONV_HIST:, :] if k < CONV_WIDTH - 1 else ea[CONV_HIST:, :]
            vec_ref[_bag_row("conv_w", k), :] += jnp.sum(dxc * tap, axis=0, keepdims=True)
        dxc_ext[0:tb, :] = dxc
        ed = dxc_ext[...]
        dxa = ed * cw_ref[3:4, :]
        dxa = dxa + _shift_up(ed, 1) * cw_ref[2:3, :]
        dxa = dxa + _shift_up(ed, 2) * cw_ref[1:2, :]
        dxa = dxa + _shift_up(ed, 3) * cw_ref[0:1, :]
        dz_ref[:, 0:D_MODEL] = dxa[0:tb, :].astype(BF16)
        dxc_ext[tb:, :] = dxc[0:CONV_HIST, :]

        pos = blk * tb + lax.broadcasted_iota(jnp.int32, (tb, POOL_GROUP_DIM), 0)
        diff = _pool_diff(eb, pos)
        rider(2)
        ypre = _pool_mix(diff, pw_ref)
        ps = ps_ref[...]
        gb = gb_ref[...]
        sgb = _sigmoid(gb)
        dyb = dyb_ref[...]
        dyp = dyb * (gb * sgb)
        dz_ref[:, 2 * D_MODEL + POOL_WIDTH:3 * D_MODEL] = (
            dyb * (ypre * ps) * (sgb * (1.0 + gb * (1.0 - sgb)))).astype(BF16)
        vec_ref[_bag_row("pool_scale"), 0:POOL_WIDTH] += jnp.sum(dyp * ypre, axis=0, keepdims=True)
        dypre = dyp * ps
        for g, k in enumerate(POOL_WINDOWS):
            cols = slice(g * POOL_GROUP_DIM, (g + 1) * POOL_GROUP_DIM)
            dyg = dypre[:, cols].astype(BF16)
            mat_ref[mat_rows("pool_w", g), :] += _dot_tn(diff[g].astype(BF16), dyg)
            ddiff = _dot_nt(dyg, pw_ref[g])
            count = jnp.minimum(pos + 1, k).astype(F32)
            dwin = ddiff / count
            dwin_ext[0:tb, cols] = dwin
            s = dwin_ext[:, cols]
            for step_ in range(g + 1):
                s = s + _shift_up(s, 2 ** step_)
            dz_ref[:, 2 * D_MODEL + g * POOL_GROUP_DIM:2 * D_MODEL + (g + 1) * POOL_GROUP_DIM] = (
                s[0:tb, :] - ddiff).astype(BF16)
            dwin_ext[tb:, cols] = dwin[0:POOL_HIST, :]

        dz_ref[:, 3 * D_MODEL:] = dzm_ref[...]

        @pl.when(i == nb - 1)
        def _():
            row = _bag_row("lru_lambda")
            vec_ref[row, :] = vec_ref[row, :] * (-_sigmoid(-lam))
            for k in range(nr):
                grads[2 * k + 1][...] = grads[2 * k][...].astype(BF16)

    rev = lambda i: (nb - 1 - i, 0)
    fixed = lambda i: (0, 0)

    def prev(rows, col):
        per = tb // rows
        return lambda i: (jnp.maximum((nb - 1 - i) * per - 1, 0), col)

    in_specs = [pl.BlockSpec((tb, D_MODEL), lambda i: (nb - 1 - i, 0)),
                pl.BlockSpec((CONV_HIST, D_MODEL), prev(CONV_HIST, 0)),
                pl.BlockSpec((tb, D_MODEL), lambda i: (nb - 1 - i, 1)),
                pl.BlockSpec((tb, POOL_WIDTH), lambda i: (nb - 1 - i, 4)),
                pl.BlockSpec((POOL_HIST, POOL_WIDTH), prev(POOL_HIST, 4)),
                pl.BlockSpec((tb, POOL_WIDTH), lambda i: (nb - 1 - i, 5)),
                pl.BlockSpec((tb, D_MODEL), rev),
                pl.BlockSpec((F32_SUBLANES, D_MODEL), prev(F32_SUBLANES, 0)),
                pl.BlockSpec((tb, D_MODEL), rev), pl.BlockSpec((tb, POOL_WIDTH), rev),
                pl.BlockSpec((tb, 2 * D_MODEL), rev)] + _branch_specs(tb, rev, fixed) + [
                    pl.BlockSpec((VEC_BAG_ROWS, D_MODEL), fixed)]
    vec_at = len(in_specs) - 1
    out_shape = [jax.ShapeDtypeStruct((t, IN_COLS), BF16), jax.ShapeDtypeStruct((VEC_BAG_ROWS, D_MODEL), F32),
                 jax.ShapeDtypeStruct((MAT_BAG_ROWS, HEAD_DIM), F32)]
    out_specs = [pl.BlockSpec((tb, IN_COLS), rev), pl.BlockSpec((VEC_BAG_ROWS, D_MODEL), fixed),
                 pl.BlockSpec((MAT_BAG_ROWS, HEAD_DIM), fixed)]
    for lhs, rhs in riders:
        in_specs += [pl.BlockSpec((tb, lhs.shape[1]), rev), pl.BlockSpec((tb, rhs.shape[1]), rev)]
        grad = (lhs.shape[1], rhs.shape[1])
        out_shape += [jax.ShapeDtypeStruct(grad, F32), jax.ShapeDtypeStruct(grad, BF16)]
        out_specs += [pl.BlockSpec(grad, fixed)] * 2
    scratch = [pltpu.VMEM((tb + CONV_HIST, D_MODEL), F32), pltpu.VMEM((tb + POOL_HIST, POOL_WIDTH), F32),
               pltpu.VMEM((tb + F32_SUBLANES, D_MODEL), F32), pltpu.VMEM((tb + F32_SUBLANES, D_MODEL), F32),
               pltpu.VMEM((tb + CONV_HIST, D_MODEL), F32), pltpu.VMEM((tb + POOL_HIST, POOL_WIDTH), F32),
               pltpu.VMEM((F32_SUBLANES, D_MODEL), F32),
               pltpu.VMEM((tb, D_MODEL), F32), pltpu.VMEM((tb, D_MODEL), F32), pltpu.VMEM((tb, D_MODEL), F32)]
    return pl.pallas_call(
        body, name="branches_bwd", out_shape=tuple(out_shape), grid=(nb,), in_specs=in_specs,
        out_specs=tuple(out_specs), scratch_shapes=scratch, input_output_aliases={vec_at: 1},
        compiler_params=pltpu.CompilerParams(dimension_semantics=("arbitrary",),
                                             vmem_limit_bytes=VMEM_LIMIT_BYTES),
    )(z, z, z, z, z, z, hl, hl, dya, dyb, dzm, *weights, vec_bag, *[a for pair in riders for a in pair])


def _merge_head(x2d, ya, yb, z, p2d, tgt, w_pl, w_pp, w_out, w_pg, w_pe, g2, gf, tb):
    t = x2d.shape[0]
    p_dim = p2d.shape[1]

    def body(x_ref, ya_ref, yb_ref, ma_ref, mb_ref, p_ref, t_ref, wpl_ref, wpp_ref, wout_ref, wpg_ref, wpe_ref,
             g2_ref, gf_ref,
             bag_ref, dxr_ref, dya_ref, dyb_ref, dzm_ref,
             mg_ref, do_ref, hn_ref, dgp_ref, dpe_ref, da_ref, dbm_ref, pbf_ref):
        @pl.when(pl.program_id(0) == 0)
        def _():
            bag_ref[...] = jnp.zeros_like(bag_ref)

        a_ = _dot(ya_ref[...], wpl_ref[...])
        bm = _dot(yb_ref[...], wpp_ref[...])
        sa = _sigmoid(ma_ref[...])
        sb = _sigmoid(mb_ref[...])
        mg = (sa * a_ + sb * bm).astype(BF16)
        mg_ref[...] = mg
        x1 = x_ref[...] + _dot(mg, wout_ref[...])
        xn2, r2 = _rms(x1)
        g2 = g2_ref[...]
        hn = (xn2 * g2).astype(BF16)
        hn_ref[...] = hn
        gate = _sigmoid(_dot(hn, wpg_ref[...]))
        pbf = p_ref[...].astype(BF16)
        pbf_ref[...] = pbf
        pe = _dot(pbf, wpe_ref[...])
        x2 = x1 + gate * pe
        xn3, r3 = _rms(x2)
        gf = gf_ref[...]
        err = xn3 * gf - t_ref[...]
        bag_ref[_bag_rows("loss"), 0:128] += 0.5 * jnp.sum(jnp.mean(err * err, axis=-1))

        dy = err * (1.0 / D_MODEL)
        bag_ref[_bag_row("final_g"), :] += jnp.sum(dy * xn3, axis=0, keepdims=True)
        dx2 = _rms_bwd(dy * gf, xn3, r3)
        dpe_ref[...] = (dx2 * gate).astype(BF16)
        dgp = ((dx2 * pe) * (gate * (1.0 - gate))).astype(BF16)
        dgp_ref[...] = dgp
        dhn = _dot_nt(dgp, wpg_ref[...])
        bag_ref[_bag_row("ple_norm_g"), :] += jnp.sum(dhn * xn2, axis=0, keepdims=True)
        dx1 = dx2 + _rms_bwd(dhn * g2, xn2, r2)
        dxr_ref[...] = dx1
        do = dx1.astype(BF16)
        do_ref[...] = do
        dmg = _dot_nt(do, wout_ref[...])
        da = (dmg * sa).astype(BF16)
        dbm = (dmg * sb).astype(BF16)
        da_ref[...] = da
        dbm_ref[...] = dbm
        dzm_ref[:, 0:D_MODEL] = (dmg * a_ * (sa * (1.0 - sa))).astype(BF16)
        dzm_ref[:, D_MODEL:] = (dmg * bm * (sb * (1.0 - sb))).astype(BF16)
        dya_ref[...] = _dot_nt(da, wpl_ref[...])
        dyb_ref[...] = _dot_nt(dbm, wpp_ref[...])

    row = lambda i: (i, 0)
    fixed = lambda i: (0, 0)

    def resident(shape):
        return pl.BlockSpec(shape, fixed, pipeline_mode=pl.Buffered(1))

    tok = lambda width: pl.BlockSpec((tb, width), row)
    in_specs = [tok(D_MODEL), tok(D_MODEL), tok(POOL_WIDTH),
                pl.BlockSpec((tb, D_MODEL), lambda i: (i, 3)), pl.BlockSpec((tb, D_MODEL), lambda i: (i, 4)),
                tok(p_dim), tok(D_MODEL),
                resident((D_MODEL, D_MODEL)), resident((POOL_WIDTH, D_MODEL)), resident((D_MODEL, D_MODEL)),
                resident((D_MODEL, D_MODEL)), resident((p_dim, D_MODEL)),
                pl.BlockSpec((1, D_MODEL), fixed), pl.BlockSpec((1, D_MODEL), fixed)]
    bf = lambda width: jax.ShapeDtypeStruct((t, width), BF16)
    f32 = lambda width: jax.ShapeDtypeStruct((t, width), F32)
    out_shape = (jax.ShapeDtypeStruct((VEC_BAG_ROWS, D_MODEL), F32),
                 f32(D_MODEL), f32(D_MODEL), f32(POOL_WIDTH), bf(2 * D_MODEL),
                 bf(D_MODEL), bf(D_MODEL), bf(D_MODEL), bf(D_MODEL), bf(D_MODEL), bf(D_MODEL), bf(D_MODEL), bf(p_dim))
    out_specs = (pl.BlockSpec((VEC_BAG_ROWS, D_MODEL), fixed),
                 tok(D_MODEL), tok(D_MODEL), tok(POOL_WIDTH), tok(2 * D_MODEL),
                 tok(D_MODEL), tok(D_MODEL), tok(D_MODEL), tok(D_MODEL), tok(D_MODEL), tok(D_MODEL), tok(D_MODEL),
                 tok(p_dim))
    return pl.pallas_call(
        body, name="merge_head", out_shape=out_shape, grid=(t // tb,), in_specs=in_specs, out_specs=out_specs,
        compiler_params=pltpu.CompilerParams(dimension_semantics=("arbitrary",),
                                             vmem_limit_bytes=VMEM_LIMIT_BYTES),
    )(x2d, ya, yb, z, z, p2d, tgt, w_pl, w_pp, w_out, w_pg, w_pe, g2, gf)


def kernel(x, p, norm_g, w_in, conv_w, conv_b, lru_w_a, lru_b_a, lru_w_x, lru_b_x, lru_lambda, pool_w, pool_scale, w_proj_lru, w_proj_pool, w_out, ple_norm_g, w_ple_gate, w_ple_proj, final_g, loss_target, m_norm_g, m_w_in, m_conv_w, m_conv_b, m_lru_w_a, m_lru_b_a, m_lru_w_x, m_lru_b_x, m_lru_lambda, m_pool_w, m_pool_scale, m_w_proj_lru, m_w_proj_pool, m_w_out, m_ple_norm_g, m_w_ple_gate, m_w_ple_proj, m_final_g, v_norm_g, v_w_in, v_conv_w, v_conv_b, v_lru_w_a, v_lru_b_a, v_lru_w_x, v_lru_b_x, v_lru_lambda, v_pool_w, v_pool_scale, v_w_proj_lru, v_w_proj_pool, v_w_out, v_ple_norm_g, v_w_ple_gate, v_w_ple_proj, v_final_g):
    bsz, seq, _ = x.shape
    t = bsz * seq
    tb_mm = min(1024, seq)
    tb_seq = min(256, seq // 2) if seq >= 512 else seq
    x2d = x.reshape(t, D_MODEL)
    p2d = p.reshape(t, p.shape[-1])
    tgt = loss_target.reshape(t, D_MODEL)

    rest = [(w_proj_lru[0], 0), (w_proj_pool[0], 1), (w_out[0], 0), (w_ple_gate[0], 0), (w_ple_proj[0], 1)]
    z, h_bf, w_in_f, conv_w_f, *narrow = _in_proj_gather(
        x2d, norm_g, w_in[0], [(conv_w[0], 1, False)], tb_mm,
        [w for w, _ in rest] + [lru_w_a[0], lru_w_x[0], pool_w[0]])
    wa_bf, wx_bf, pw_bf = narrow[len(rest):]
    branch_w = (conv_w_f, conv_b, wa_bf, lru_b_a.reshape(1, D_MODEL), wx_bf, lru_b_x.reshape(1, D_MODEL),
                lru_lambda, pw_bf, pool_scale)

    ya, yb, hl, w_pl_f, w_pp_f, w_out_f, w_pg_f, w_pe_f = _branches_fwd(
        z, branch_w, seq, tb_seq, [(w16, axis, True) for w16, (_, axis) in zip(narrow, rest)])
    (vec_bag, dx_res, dya, dyb, dzm, mg_bf, do_bf, hn_bf, dgp_bf, dpe_bf, da_bf, dbm_bf, p_bf) = _merge_head(
        x2d, ya, yb, z, p2d, tgt, w_pl_f, w_pp_f, w_out_f, w_pg_f, w_pe_f, ple_norm_g, final_g.reshape(1, D_MODEL),
        tb_seq)
    dz, vec_bag, mat_bag, g_out, g_out16, g_pp, g_pp16, g_pe, g_pe16 = _branches_bwd(
        z, hl, dya, dyb, dzm, branch_w, vec_bag, seq, tb_seq, [(mg_bf, do_bf), (yb, dbm_bf), (p_bf, dpe_bf)])

    tb_dw = min(1024, seq)
    def row_pieces(g32, g16):
        pieces = (8, g32.shape[0] // 8, g32.shape[1])
        return g32.reshape(pieces), False, g16.reshape(pieces)

    g_pl, g_pl16, g_pg, g_pg16 = _weight_grad([(ya, da_bf), (hn_bf, dgp_bf)], 1, tb_dw, "dw_proj")
    p_dim = p2d.shape[1]
    proj_parts = [row_pieces(g_pl[0], g_pl16[0]), (g_pp, True, g_pp16), row_pieces(g_out, g_out16),
                  row_pieces(g_pg[0], g_pg16[0]), (g_pe, True, g_pe16)]
    nb_dw = t // tb_dw
    g_in, g_in16, r_pl, r_pp, r_out, r_pg, r_pe, vec_mine, mat_mine = _weight_grad(
        [(h_bf, dz)], N_CHIPS, tb_dw, "dw_in",
        reduce=(proj_parts + [(vec_bag.reshape(8, VEC_BAG_ROWS // 8, D_MODEL), False, None),
                              (mat_bag.reshape(8, MAT_BAG_ROWS // 8, HEAD_DIM), False, None)],
                [BF16] * 5 + [F32] * 2,
                (0, nb_dw // 2, 2 * nb_dw - 1, 3 * nb_dw + nb_dw // 2, N_CHIPS * nb_dw - 1)))
    pieces = (8, D_MODEL // 2, IN_COLS // N_CHIPS)
    nb_seq = t // tb_seq
    dx, g_g1, r_in, vec_sum, mat_sum, g_cw = _in_proj_bwd(
        dz, w_in_f, x2d, dx_res, norm_g, tb_seq,
        reduce=([(g_in.reshape(pieces), False, g_in16.reshape(pieces))], BF16,
                (0, nb_seq // 8, nb_seq // 2, nb_seq - 1)),
        shards=[(vec_mine.reshape(VEC_BAG_ROWS // N_CHIPS, D_MODEL), 0, True),
                (mat_mine.reshape(MAT_BAG_ROWS // N_CHIPS, HEAD_DIM), 0, True)],
        take=(_bag_rows("conv_w"), D_MODEL // N_CHIPS))

    u_in = tuple(a[None] for a in _adamw(w_in[0], r_in.reshape(D_MODEL, IN_COLS // N_CHIPS), m_w_in[0], v_w_in[0],
                                         D_MODEL // 4, "adamw_w_in"))
    proj = [(w_proj_lru, r_pl, m_w_proj_lru, v_w_proj_lru), (w_proj_pool, r_pp, m_w_proj_pool, v_w_proj_pool),
            (w_out, r_out, m_w_out, v_w_out), (w_ple_gate, r_pg, m_w_ple_gate, v_w_ple_gate),
            (w_ple_proj, r_pe, m_w_ple_proj, v_w_ple_proj)]
    u_pl, u_pp, u_out, u_pg, u_pe = [tuple(a[None] for a in u) for u in _adamw_group(
        [(w[0], g.reshape(w.shape[1:]), m[0], v[0]) for w, g, m, v in proj], "adamw_proj")]

    small = [("norm_g", norm_g, m_norm_g, v_norm_g), ("conv_b", conv_b, m_conv_b, v_conv_b),
             ("lru_w_a", lru_w_a, m_lru_w_a, v_lru_w_a), ("lru_b_a", lru_b_a, m_lru_b_a, v_lru_b_a),
             ("lru_w_x", lru_w_x, m_lru_w_x, v_lru_w_x), ("lru_b_x", lru_b_x, m_lru_b_x, v_lru_b_x),
             ("lru_lambda", lru_lambda, m_lru_lambda, v_lru_lambda), ("pool_w", pool_w, m_pool_w, v_pool_w),
             ("pool_scale", pool_scale, m_pool_scale, v_pool_scale),
             ("ple_norm_g", ple_norm_g, m_ple_norm_g, v_ple_norm_g), ("final_g", final_g, m_final_g, v_final_g)]

    def view(a):
        return a.reshape(-1, a.shape[-1]) if a.ndim != 3 else a[0]

    flat = _adamw_replicated(vec_sum, mat_sum, g_g1, [(name,) + tuple(view(a) for a in arrs) for name, *arrs in small],
                             (conv_w[0], m_conv_w[0], v_conv_w[0], g_cw))
    u_small = {name: tuple(flat[4 * k + pick].reshape(arrs[0].shape) for pick in range(4))
               for k, (name, *arrs) in enumerate(small)}
    u_cw = tuple(a[None] for a in flat[4 * len(small):4 * len(small) + 4])

    loss = flat[-1].reshape(())
    grad_x = dx.reshape(bsz, seq, D_MODEL)

    def ordered(pick):
        s = {name: u[pick] for name, u in u_small.items()}
        return [s["norm_g"], u_in[pick], u_cw[pick], s["conv_b"], s["lru_w_a"], s["lru_b_a"], s["lru_w_x"], s["lru_b_x"],
                s["lru_lambda"], s["pool_w"], s["pool_scale"], u_pl[pick], u_pp[pick], u_out[pick], s["ple_norm_g"],
                u_pg[pick], u_pe[pick], s["final_g"]]

    return (loss, grad_x, *ordered(0), *ordered(1), *ordered(2), *ordered(3))
```

```python
import jax
import jax.numpy as jnp
from jax import lax
from jax.experimental import pallas as pl
from jax.experimental.pallas import tpu as pltpu

F32 = jnp.float32
BF16 = jnp.bfloat16
MESH = pl.DeviceIdType.MESH

D_MODEL = 1024
LRU_HEADS = 8
HEAD_DIM = 128
CONV_WIDTH = 4
LRU_C = 8.0
POOL_WIDTH = 512
POOL_WINDOWS = (2, 4, 8, 16)
POOL_GROUP_DIM = 128
IN_COLS = 5120
N_CHIPS = 4
EPS = 1e-6

ADAM_LR = 0.001
ADAM_B1 = 0.9
ADAM_B2 = 0.999
ADAM_EPS = 1e-08
ADAM_WD = 0.01
ADAM_STEP = 10

F32_SUBLANES = 8
CONV_HIST = 8
POOL_HIST = 16
VMEM_LIMIT_BYTES = 58 * 1024 * 1024
VEC_BAG_SLOTS = ("norm_g", "conv_w", "conv_b", "lru_b_a", "lru_b_x", "lru_lambda", "pool_scale", "ple_norm_g",
                 "final_g", "loss")
VEC_BAG_ROWS = 128
MAT_BAG_AT = {"lru_w_a": 0, "lru_w_x": LRU_HEADS * HEAD_DIM, "pool_w": 2 * LRU_HEADS * HEAD_DIM}
MAT_BAG_ROWS = 2 * LRU_HEADS * HEAD_DIM + len(POOL_WINDOWS) * POOL_GROUP_DIM


def _bag_row(name, k=0):
    at = F32_SUBLANES * VEC_BAG_SLOTS.index(name) + k
    return slice(at, at + 1)


def _bag_rows(name):
    at = F32_SUBLANES * VEC_BAG_SLOTS.index(name)
    return slice(at, at + F32_SUBLANES)


def _dot(a, b):
    return jnp.dot(a, b, preferred_element_type=F32)


def _dot_nt(a, b):
    return lax.dot_general(a, b, (((1,), (1,)), ((), ())), preferred_element_type=F32)


def _dot_tn(a, b):
    return lax.dot_general(a, b, (((0,), (0,)), ((), ())), preferred_element_type=F32)


def _sigmoid(v):
    return jax.nn.sigmoid(v)


def _softplus(v):
    return jnp.maximum(v, 0.0) + jnp.log1p(jnp.exp(-jnp.abs(v)))


def _place():
    return lax.axis_index("x"), lax.axis_index("y"), lax.axis_index("c")


GATHER_SEMS = 6


def _gather_shapes(shards):
    out_shape = []
    for arr, axis, _ in shards:
        r, cols = arr.shape
        out_shape.append(jax.ShapeDtypeStruct((N_CHIPS * r, cols) if axis == 0 else (r, N_CHIPS * cols), arr.dtype))
    n = len(shards)
    sems = [pltpu.SemaphoreType.DMA((n * GATHER_SEMS,)), pltpu.SemaphoreType.DMA((n * GATHER_SEMS,)),
            pltpu.SemaphoreType.DMA((n,))]
    return out_shape, sems


def _gather_steps(shards, ins, outs, send_sems, recv_sems, local_sems):
    n = len(shards)
    x, y, c = _place()
    me, sibling = (x, y, c), (x, y, 1 - c)
    chips = [(x, 1 - y), (1 - x, y), (1 - x, 1 - y)]

    def region(k, cx, cy, hc):
        (r, cols), axis = shards[k][0].shape, shards[k][1]
        j = 2 * cx + cy
        if axis == 0:
            if hc is None:
                return outs[k].at[pl.ds(j * r, r), :]
            return outs[k].at[pl.ds(j * r + hc * (r // 2), r // 2), :]
        if hc is None:
            return outs[k].at[:, pl.ds(j * cols, cols)]
        return outs[k].at[pl.ds(hc * (r // 2), r // 2), pl.ds(j * cols, cols)]

    def remote(k, sem, block, to, src=None):
        dst = region(k, *block)
        return pltpu.make_async_remote_copy(
            src_ref=dst if src is None else src, dst_ref=dst,
            send_sem=send_sems.at[k * GATHER_SEMS + sem], recv_sem=recv_sems.at[k * GATHER_SEMS + sem],
            device_id=to, device_id_type=MESH)

    def first(k, idx):
        r, split = shards[k][0].shape[0], shards[k][2]
        src = ins[k].at[pl.ds(c * (r // 2), r // 2), :] if split else ins[k]
        return remote(k, idx, (x, y, c if split else None), (*chips[idx], c), src=src)

    def relay(k):
        src_chip = (jnp.bitwise_xor(x, 1 - c), jnp.bitwise_xor(y, c))
        dst_chip = (jnp.bitwise_xor(x, c), jnp.bitwise_xor(y, 1 - c))
        return remote(k, 2, (*src_chip, c), (*dst_chip, c))

    def passed(k, idx):
        return remote(k, 3 + idx, (*chips[idx], c), sibling)

    def mine(k):
        return pltpu.make_async_copy(ins[k], region(k, x, y, None), local_sems.at[k])

    def start():
        for k in range(n):
            mine(k).start()
            for idx in range(2 if shards[k][2] else 3):
                first(k, idx).start()

    def relay_on():
        for k in range(n):
            split = shards[k][2]
            for idx in range(2):
                remote(k, idx, (*chips[idx], c if split else None), me).wait_recv()
            if split:
                relay(k).start()
                passed(k, 0).start()
                passed(k, 1).start()

    def finish():
        for k in range(n):
            split = shards[k][2]
            remote(k, 2, (*chips[2], c if split else None), me).wait_recv()
            if split:
                passed(k, 2).start()
        for k in range(n):
            if shards[k][2]:
                for idx in range(3):
                    remote(k, 3 + idx, (*chips[idx], 1 - c), me).wait_recv()
        for k in range(n):
            if shards[k][2]:
                for cp in (first(k, 0), first(k, 1), relay(k), passed(k, 0), passed(k, 1), passed(k, 2)):
                    cp.wait_send()
            else:
                for idx in range(3):
                    first(k, idx).wait_send()
            mine(k).wait()

    return start, relay_on, finish


RS_ADD_ROWS = (64, 32, 16, 8)


N_DEV = 2 * N_CHIPS


def _all_reduce_scratch(shape):
    return [pltpu.VMEM((N_DEV,) + tuple(shape), F32), pltpu.SemaphoreType.DMA((N_DEV - 1,)),
            pltpu.SemaphoreType.DMA((N_DEV - 1,))]


def _all_reduce_tile(v_ref, o_ref, slots, send_sems, recv_sems):
    flips = [(dx, dy, dc) for dx in (0, 1) for dy in (0, 1) for dc in (0, 1)][1:]
    x, y, c = _place()
    mine = 4 * x + 2 * y + c

    def copy(k, to_flip, slot):
        dx, dy, dc = to_flip
        peer = (jnp.bitwise_xor(x, dx), jnp.bitwise_xor(y, dy), jnp.bitwise_xor(c, dc))
        return pltpu.make_async_remote_copy(
            src_ref=v_ref, dst_ref=slots.at[slot], send_sem=send_sems.at[k], recv_sem=recv_sems.at[k],
            device_id=peer, device_id_type=MESH)

    sends = [copy(k, flip, mine) for k, flip in enumerate(flips)]
    for cp in sends:
        cp.start()
    slots[mine] = v_ref[...]
    for k, (dx, dy, dc) in enumerate(flips):
        copy(k, (dx, dy, dc), jnp.bitwise_xor(mine, 4 * dx + 2 * dy + dc)).wait_recv()
    total = slots[0]
    for d in range(1, N_DEV):
        total = total + slots[d]
    o_ref[...] = total
    for cp in sends:
        cp.wait_send()


RS_SEMS = 8
RS_LOCAL_SEMS = 5


def _rs_piece_shape(part):
    arr, cols = part[0], part[1]
    return (arr.shape[0] // 2, arr.shape[1] // N_CHIPS) if cols else tuple(arr.shape[1:])


def _rs_operands(parts):
    return [p[0] for p in parts] + [p[0] if p[2] is None else p[2] for p in parts]


def _rs_wires(parts, wire):
    return list(wire) if isinstance(wire, (list, tuple)) else [wire] * len(parts)


def _rs_shapes(parts, wire):
    n = len(parts)
    shapes = [_rs_piece_shape(p) for p in parts]
    out_shape = [jax.ShapeDtypeStruct((2,) + s, F32) for s in shapes]
    scratch = []
    for lead, kind in ((N_CHIPS, "f32"), (N_CHIPS, "narrow"), (N_CHIPS, "wire"), (None, "f32"), (N_CHIPS, "wire")):
        for s, p, w in zip(shapes, parts, _rs_wires(parts, wire)):
            dtype = {"f32": F32, "narrow": F32 if p[2] is None else p[2].dtype, "wire": w}[kind]
            scratch.append(pltpu.VMEM(s if lead is None else (lead,) + s, dtype))
    scratch += [pltpu.SemaphoreType.DMA((n * RS_SEMS,)), pltpu.SemaphoreType.DMA((n * RS_SEMS,)),
                pltpu.SemaphoreType.DMA((n * RS_LOCAL_SEMS,))]
    return out_shape, scratch


def _rs_steps(parts, ins, outs, scratch):
    n = len(parts)
    own, sib, got, fin, snd = (scratch[k * n:(k + 1) * n] for k in range(5))
    send_sems, recv_sems, local_sems = scratch[5 * n:]
    shapes = [_rs_piece_shape(p) for p in parts]
    x, y, c = _place()
    j_me = 2 * x + y
    me, sibling = (x, y, c), (x, y, 1 - c)

    def piece(a, jj, core, narrow=False):
        ref = ins[n + a] if narrow else ins[a]
        if parts[a][1]:
            r, cl = shapes[a]
            return ref.at[pl.ds(core * r, r), pl.ds(jj * cl, cl)]
        return ref.at[2 * jj + core]

    def remote(a, sem, src, dst, to):
        return pltpu.make_async_remote_copy(
            src_ref=src, dst_ref=dst, send_sem=send_sems.at[a * RS_SEMS + sem],
            recv_sem=recv_sems.at[a * RS_SEMS + sem], device_id=to, device_id_type=MESH)

    def rows_loop(a, fn):
        r = shapes[a][0]
        step = max(s for s in RS_ADD_ROWS if r % s == 0)

        def it(i, carry):
            fn(pl.ds(pl.multiple_of(i * step, step), step))
            return carry

        lax.fori_loop(0, r // step, it, 0)

    def load(a, jj):
        return pltpu.make_async_copy(piece(a, jj, c), own[a].at[jj], local_sems.at[a * RS_LOCAL_SEMS + jj])

    def to_sibling(a, jj):
        return remote(a, jj, piece(a, jj, 1 - c, narrow=True), sib[a].at[jj], sibling)

    near = (jnp.bitwise_xor(x, 1 - c), jnp.bitwise_xor(y, c))
    far = (jnp.bitwise_xor(x, c), jnp.bitwise_xor(y, 1 - c))
    diag = (1 - x, 1 - y)
    FROM_NEAR, FROM_FAR, FEED = 0, 1, 2

    def chip_of(chip):
        return 2 * chip[0] + chip[1]

    def feed(a):
        return remote(a, 4, snd[a].at[chip_of(diag)], got[a].at[FEED], (*near, c))

    def to_near(a):
        return remote(a, 5, snd[a].at[chip_of(near)], got[a].at[FROM_NEAR], (*near, c))

    def to_far(a):
        return remote(a, 6, snd[a].at[chip_of(far)], got[a].at[FROM_FAR], (*far, c))

    def store(a):
        return pltpu.make_async_copy(fin[a], outs[a].at[c], local_sems.at[a * RS_LOCAL_SEMS + 4])

    def result_to_sibling(a):
        return remote(a, 7, fin[a], outs[a].at[c], sibling)

    def exchange():
        for a in range(n):
            for jj in range(N_CHIPS):
                load(a, jj).start()
                to_sibling(a, jj).start()

    def chip_sums():
        for a in range(n):
            for jj in range(N_CHIPS):
                load(a, jj).wait()
                remote(a, jj, sib[a].at[jj], sib[a].at[jj], me).wait_recv()

                def add(sl, a=a, jj=jj):
                    q = own[a][jj, sl, :] + sib[a][jj, sl, :].astype(F32)
                    own[a][jj, sl, :] = q
                    snd[a][jj, sl, :] = q.astype(snd[a].dtype)

                rows_loop(a, add)
        for a in range(n):
            feed(a).start()
        for a in range(n):
            to_near(a).start()

    def relay():
        for a in range(n):
            remote(a, 4, got[a].at[FEED], got[a].at[FEED], me).wait_recv()

            def add(sl, a=a):
                pair = own[a][chip_of(far), sl, :] + got[a][FEED, sl, :].astype(F32)
                snd[a][chip_of(far), sl, :] = pair.astype(snd[a].dtype)

            rows_loop(a, add)
            to_far(a).start()

    def totals():
        for a in range(n):
            remote(a, 5, got[a].at[FROM_NEAR], got[a].at[FROM_NEAR], me).wait_recv()
            remote(a, 6, got[a].at[FROM_FAR], got[a].at[FROM_FAR], me).wait_recv()

            def total(sl, a=a):
                fin[a][sl, :] = (own[a][j_me, sl, :] + got[a][FROM_NEAR, sl, :].astype(F32)) + (
                    got[a][FROM_FAR, sl, :].astype(F32))

            rows_loop(a, total)
            store(a).start()
            result_to_sibling(a).start()

    def finish():
        for a in range(n):
            remote(a, 7, outs[a].at[1 - c], outs[a].at[1 - c], me).wait_recv()
        for a in range(n):
            for jj in range(N_CHIPS):
                to_sibling(a, jj).wait_send()
            for cp in (feed(a), to_near(a), to_far(a), result_to_sibling(a)):
                cp.wait_send()
            store(a).wait()

    return exchange, chip_sums, relay, totals, finish


def _rms(x):
    r = lax.rsqrt(jnp.mean(x * x, axis=-1, keepdims=True) + EPS)
    return x * r, r


def _rms_bwd(dxn, xn, r):
    return r * (dxn - xn * jnp.mean(dxn * xn, axis=-1, keepdims=True))


def _in_proj_gather(x2d, norm_g, w_in_sh, shards, tb, casts):
    t = x2d.shape[0]
    nb = t // tb
    cols = IN_COLS // N_CHIPS
    half = D_MODEL // 2
    n = len(shards)
    nc = len(casts)

    def body(x_ref, g_ref, win_ref, *refs):
        ins, cast_ins = refs[:n], refs[n:n + nc]
        z_ref, h_ref, wfull_ref = refs[n + nc:n + nc + 3]
        outs, cast_outs = refs[n + nc + 3:2 * n + nc + 3], refs[2 * n + nc + 3:2 * (n + nc) + 3]
        scratch = refs[2 * (n + nc) + 3:]
        wv, h_all, send_sems, recv_sems, local_sems, w_send, w_recv, w_local, stage = scratch[:9]
        wide, narrow, cast_sems = scratch[9:9 + nc], scratch[9 + nc:9 + 2 * nc], scratch[9 + 2 * nc]
        s, i = pl.program_id(0), pl.program_id(1)
        x, y, c = _place()
        me, sibling = (x, y, c), (x, y, 1 - c)
        chips = [(x, 1 - y), (1 - x, y), (1 - x, 1 - y)]

        def w_half(cx, cy, hc):
            return wv.at[2 * cx + cy, pl.ds(hc * half, half), :]

        def w_remote(sem, block, to, src=None):
            dst = w_half(*block)
            return pltpu.make_async_remote_copy(
                src_ref=dst if src is None else src, dst_ref=dst, send_sem=w_send.at[sem],
                recv_sem=w_recv.at[sem], device_id=to, device_id_type=MESH)

        def w_first(idx):
            return w_remote(idx, (x, y, c), (*chips[idx], c))

        def w_relay():
            src_chip = (jnp.bitwise_xor(x, 1 - c), jnp.bitwise_xor(y, c))
            dst_chip = (jnp.bitwise_xor(x, c), jnp.bitwise_xor(y, 1 - c))
            return w_remote(2, (*src_chip, c), (*dst_chip, c))

        def w_pass(idx):
            return w_remote(3 + idx, (*chips[idx], c), sibling)

        def w_store(k, cx, cy):
            jj = 2 * cx + cy
            return pltpu.make_async_copy(wv.at[jj], wfull_ref.at[:, pl.ds(jj * cols, cols)], w_local.at[k])

        start_rest, relay_rest, finish_rest = _gather_steps(shards, ins, outs, send_sems, recv_sems, local_sems)

        def own(k, hc):
            return pltpu.make_async_copy(win_ref.at[pl.ds(pl.multiple_of(hc * half, half), half), :], stage.at[k],
                                         w_local.at[4 + 2 * k])

        def round_own(k, hc):
            own(k, hc).wait()
            wv[2 * x + y, pl.ds(pl.multiple_of(hc * half, half), half), :] = stage[k].astype(BF16)

        wide_in = [pltpu.make_async_copy(cast_ins[k], wide[k], cast_sems.at[k]) for k in range(nc)]
        narrow_out = [pltpu.make_async_copy(narrow[k], cast_outs[k], cast_sems.at[nc + k]) for k in range(nc)]

        @pl.when((s == 0) & (i == 0))
        def _():
            own(0, c).start()
            own(1, 1 - c).start()
            for cp in wide_in:
                cp.start()
            round_own(0, c)
            w_first(0).start()
            w_first(1).start()
            start_rest()
            round_own(1, 1 - c)
            w_store(0, x, y).start()

        @pl.when((s == 1) & (i == 0))
        def _():
            for k in range(nc):
                wide_in[k].wait()
                narrow[k][...] = wide[k][...].astype(BF16)
                narrow_out[k].start()
            w_remote(0, (*chips[0], c), me).wait_recv()
            w_remote(1, (*chips[1], c), me).wait_recv()
            w_relay().start()
            w_pass(0).start()
            w_pass(1).start()
            w_remote(3, (*chips[0], 1 - c), me).wait_recv()
            w_store(1, *chips[0]).start()

        @pl.when((s == 2) & (i == 0))
        def _():
            w_remote(4, (*chips[1], 1 - c), me).wait_recv()
            w_store(2, *chips[1]).start()

        @pl.when((s == 3) & (i == 0))
        def _():
            w_remote(2, (*chips[2], c), me).wait_recv()
            w_pass(2).start()
            w_remote(5, (*chips[2], 1 - c), me).wait_recv()
            w_store(3, *chips[2]).start()

        keep_h = pltpu.make_async_copy(h_all.at[i], h_ref.at[pl.ds(pl.multiple_of(i * tb, tb), tb), :], w_local.at[5])

        @pl.when(s == 0)
        def _():
            xn, _ = _rms(x_ref[...])
            h_all[i] = (xn * g_ref[...]).astype(BF16)
            keep_h.start()

        z_ref[...] = _dot(h_all[i], wv[jnp.bitwise_xor(2 * x + y, s)])
        pl.when(s == 0)(keep_h.wait)

        @pl.when((s == N_CHIPS - 1) & (i == nb - 1))
        def _():
            relay_rest()
            finish_rest()
            for cp in (w_first(0), w_first(1), w_relay(), w_pass(0), w_pass(1), w_pass(2)):
                cp.wait_send()
            w_store(0, x, y).wait()
            for idx in range(3):
                w_store(idx + 1, *chips[idx]).wait()
            for cp in narrow_out:
                cp.wait()

    rest_shape, rest_sems = _gather_shapes(shards)
    out_shape = [jax.ShapeDtypeStruct((t, IN_COLS), F32), jax.ShapeDtypeStruct((t, D_MODEL), BF16),
                 jax.ShapeDtypeStruct((D_MODEL, IN_COLS), BF16)] + rest_shape
    out_shape += [jax.ShapeDtypeStruct(a.shape, BF16) for a in casts]
    any_spec = pl.BlockSpec(memory_space=pl.ANY)

    def z_map(s, i):
        return (i, jnp.bitwise_xor(2 * lax.axis_index("x") + lax.axis_index("y"), s))

    return pl.pallas_call(
        body, name="in_proj", out_shape=tuple(out_shape),
        grid=(N_CHIPS, nb),
        in_specs=[pl.BlockSpec((tb, D_MODEL), lambda s, i: (jnp.where(s == 0, i, nb - 1), 0)),
                  pl.BlockSpec((1, D_MODEL), lambda s, i: (0, 0)), any_spec] + [any_spec] * (n + nc),
        out_specs=tuple([pl.BlockSpec((tb, cols), z_map), any_spec, any_spec] + [any_spec] * (n + nc)),
        scratch_shapes=[pltpu.VMEM((N_CHIPS, D_MODEL, cols), BF16), pltpu.VMEM((nb, tb, D_MODEL), BF16)] + rest_sems + [
            pltpu.SemaphoreType.DMA((GATHER_SEMS,)), pltpu.SemaphoreType.DMA((GATHER_SEMS,)),
            pltpu.SemaphoreType.DMA((N_CHIPS + 3,)), pltpu.VMEM((2, half, cols), F32)]
        + [pltpu.VMEM(a.shape, F32) for a in casts] + [pltpu.VMEM(a.shape, BF16) for a in casts]
        + [pltpu.SemaphoreType.DMA((2 * nc,))],
        compiler_params=pltpu.CompilerParams(dimension_semantics=("arbitrary", "arbitrary"),
                                             vmem_limit_bytes=VMEM_LIMIT_BYTES),
    )(x2d, norm_g, w_in_sh, *[sh[0] for sh in shards], *casts)


def _in_proj_bwd(dz, w_in, x2d, dx_res, norm_g, tb, reduce, shards, take):
    t = x2d.shape[0]
    nb = t // tb
    parts, wire, steps = reduce
    n = len(parts)
    k = len(shards)
    take_rows, take_width = take

    def body(dz_ref, w_ref, x_ref, dres_ref, g_ref, *refs):
        at = 2 * n + k
        dx_ref, dg_ref = refs[at:at + 2]
        rs_outs, g_outs = refs[at + 2:at + 2 + n], refs[at + 2 + n:at + 2 + n + k]
        cut_ref = refs[at + 2 + n + k]
        scratch = refs[at + 3 + n + k:]
        rs_scr, g_sems, dg_acc, ar_scr, cut_sem = scratch[:-8], scratch[-8:-5], scratch[-5], scratch[-4:-1], scratch[-1]
        rs = _rs_steps(parts, refs[:2 * n], rs_outs, rs_scr)
        for step, when in zip(rs[:-1], steps):
            pl.when(pl.program_id(0) == when)(step)
        gather = _gather_steps(shards, refs[2 * n:at], g_outs, *g_sems)
        for step, when in zip(gather, (0, nb // 2, nb - 1)):
            pl.when(pl.program_id(0) == when)(step)

        @pl.when(pl.program_id(0) == 0)
        def _():
            dg_acc[...] = jnp.zeros_like(dg_acc)

        xn, r = _rms(x_ref[...])
        g = g_ref[...]
        dh = _dot_nt(dz_ref[...], w_ref[...])
        dg_acc[0:1, :] += jnp.sum(dh * xn, axis=0, keepdims=True)
        dx_ref[...] = dres_ref[...] + _rms_bwd(dh * g, xn, r)

        @pl.when(pl.program_id(0) == nb - 1)
        def _():
            x, y, _ = _place()
            mine = pl.ds(pl.multiple_of((2 * x + y) * take_width, take_width), take_width)
            cut = pltpu.make_async_copy(g_outs[0].at[take_rows, mine], cut_ref, cut_sem)
            cut.start()
            _all_reduce_tile(dg_acc, dg_ref, *ar_scr)
            rs[-1]()
            cut.wait()

    row = lambda i: (i, 0)
    fixed = lambda i: (0, 0)
    rs_shape, rs_scratch = _rs_shapes(parts, wire)
    g_shape, g_sems = _gather_shapes(shards)
    any_spec = pl.BlockSpec(memory_space=pl.ANY)
    cut_shape = jax.ShapeDtypeStruct((take_rows.stop - take_rows.start, take_width), F32)
    return pl.pallas_call(
        body, name="in_proj_bwd",
        out_shape=tuple([jax.ShapeDtypeStruct((t, D_MODEL), F32), jax.ShapeDtypeStruct((F32_SUBLANES, D_MODEL), F32)]
                        + rs_shape + g_shape + [cut_shape]),
        grid=(nb,),
        in_specs=[pl.BlockSpec((tb, IN_COLS), row),
                  pl.BlockSpec((D_MODEL, IN_COLS), fixed, pipeline_mode=pl.Buffered(1)),
                  pl.BlockSpec((tb, D_MODEL), row), pl.BlockSpec((tb, D_MODEL), row),
                  pl.BlockSpec((1, D_MODEL), fixed)] + [any_spec] * (2 * n + k),
        out_specs=tuple([pl.BlockSpec((tb, D_MODEL), row), pl.BlockSpec((F32_SUBLANES, D_MODEL), fixed)]
                        + [any_spec] * (n + k + 1)),
        scratch_shapes=rs_scratch + g_sems + [pltpu.VMEM((F32_SUBLANES, D_MODEL), F32)] + _all_reduce_scratch(
            (F32_SUBLANES, D_MODEL)) + [pltpu.SemaphoreType.DMA(())],
        compiler_params=pltpu.CompilerParams(dimension_semantics=("arbitrary",),
                                             vmem_limit_bytes=VMEM_LIMIT_BYTES),
    )(dz, w_in, x2d, dx_res, norm_g, *_rs_operands(parts), *[sh[0] for sh in shards])


def _weight_grad(pairs, n_chunks, tb, name, reduce=None):
    t = pairs[0][0].shape[0]
    nb = t // tb
    m = len(pairs)
    parts, wire, steps = reduce if reduce is not None else ([], F32, ())
    n = len(parts)

    def body(*refs):
        lr, refs = refs[:2 * m], refs[2 * m:]
        o_refs = refs[2 * n:2 * n + 2 * m]
        if n:
            at = pl.program_id(0) * nb + pl.program_id(1)
            rs = _rs_steps(parts, refs[:2 * n], refs[2 * n + 2 * m:3 * n + 2 * m], refs[3 * n + 2 * m:])
            for step, when in zip(rs, steps):
                pl.when(at == when)(step)

        @pl.when(pl.program_id(1) == 0)
        def _():
            for q in range(m):
                o_refs[2 * q][...] = jnp.zeros_like(o_refs[2 * q])

        for q in range(m):
            o_refs[2 * q][...] += _dot_tn(lr[2 * q][...], lr[2 * q + 1][...])

        @pl.when(pl.program_id(1) == nb - 1)
        def _():
            for q in range(m):
                o_refs[2 * q + 1][...] = o_refs[2 * q][...].astype(BF16)

    rs_shape, rs_scratch = _rs_shapes(parts, wire) if n else ([], [])
    any_spec = pl.BlockSpec(memory_space=pl.ANY)
    in_specs, out_specs, out_shape = [], [], []
    for lhs, rhs in pairs:
        k, nc = lhs.shape[1], rhs.shape[1] // n_chunks
        in_specs += [pl.BlockSpec((tb, k), lambda j, i: (i, 0)), pl.BlockSpec((tb, nc), lambda j, i: (i, j))]
        out_specs += [pl.BlockSpec((None, k, nc), lambda j, i: (j, 0, 0))] * 2
        out_shape += [jax.ShapeDtypeStruct((n_chunks, k, nc), F32), jax.ShapeDtypeStruct((n_chunks, k, nc), BF16)]
    return pl.pallas_call(
        body, name=name, out_shape=tuple(out_shape + rs_shape),
        grid=(n_chunks, nb),
        in_specs=in_specs + [any_spec] * (2 * n),
        out_specs=tuple(out_specs + [any_spec] * n),
        scratch_shapes=rs_scratch,
        compiler_params=pltpu.CompilerParams(dimension_semantics=("arbitrary", "arbitrary"),
                                             vmem_limit_bytes=VMEM_LIMIT_BYTES),
    )(*[a for pair in pairs for a in pair], *_rs_operands(parts))


def _adam_update(w, g, m, v):
    m_ = ADAM_B1 * m + (1.0 - ADAM_B1) * g
    v_ = ADAM_B2 * v + (1.0 - ADAM_B2) * jnp.square(g)
    m_hat = m_ / (1.0 - ADAM_B1 ** ADAM_STEP)
    v_hat = v_ / (1.0 - ADAM_B2 ** ADAM_STEP)
    return -ADAM_LR * (m_hat / (jnp.sqrt(v_hat) + ADAM_EPS) + ADAM_WD * w), m_, v_


def _adamw_replicated(vec_sum, mat_sum, norm_grad, entries, conv):
    n = len(entries)

    def grad_of(name, shape, vec_ref, mat_ref, norm_ref):
        if name == "norm_g":
            return norm_ref[0:1, :]
        if name in MAT_BAG_AT:
            return mat_ref[MAT_BAG_AT[name]:MAT_BAG_AT[name] + shape[0], :]
        if shape[0] == 1:
            return vec_ref[_bag_row(name), 0:shape[1]]
        return jnp.concatenate([vec_ref[_bag_row(name), h * shape[1]:(h + 1) * shape[1]] for h in range(shape[0])],
                               axis=0)

    def body(vec_ref, mat_ref, norm_ref, *refs):
        ins, outs = refs[:3 * n + 4], refs[3 * n + 4:]
        for k in range(n):
            w_ref, m_ref, v_ref = ins[3 * k:3 * k + 3]
            g = grad_of(entries[k][0], w_ref.shape, vec_ref, mat_ref, norm_ref)
            d, m_, v_ = _adam_update(w_ref[...], g, m_ref[...], v_ref[...])
            for ref, val in zip(outs[4 * k:4 * k + 4], (g, d, m_, v_)):
                ref[...] = val
        w_ref, m_ref, v_ref, g_ref = ins[3 * n:]
        g = g_ref[0:w_ref.shape[0], :]
        for ref, val in zip(outs[4 * n:4 * n + 4], (g,) + _adam_update(w_ref[...], g, m_ref[...], v_ref[...])):
            ref[...] = val
        outs[4 * n + 4][...] = vec_ref[_bag_row("loss"), 0:1]

    arrays = [a for e in entries for a in e[1:]] + list(conv)
    out_shape = [jax.ShapeDtypeStruct(e[1].shape, F32) for e in entries for _ in range(4)]
    out_shape += [jax.ShapeDtypeStruct(conv[0].shape, F32)] * 4 + [jax.ShapeDtypeStruct((1, 1), F32)]
    return pl.pallas_call(
        body, name="adamw_replicated", out_shape=tuple(out_shape),
        compiler_params=pltpu.CompilerParams(vmem_limit_bytes=VMEM_LIMIT_BYTES),
    )(vec_sum, mat_sum, norm_grad, *arrays)


def _adamw_group(items, name):
    arrays = [a for item in items for a in item[:4]]
    n = len(arrays)
    blocks = []
    for k, item in enumerate(items):
        rows = item[0].shape[0] // item[4]
        blocks += [(k, slice(q * rows, (q + 1) * rows)) for q in range(item[4])]

    def body(*refs):
        ins, outs, bufs = refs[:n], refs[n:2 * n], refs[2 * n:3 * n]
        load_sems, store_sems = refs[3 * n:]

        def copies(src, dst, sems):
            return [[pltpu.make_async_copy(src[4 * k + j].at[rows], dst[4 * k + j].at[rows], sems.at[4 * b + j])
                     for j in range(4)] for b, (k, rows) in enumerate(blocks)]

        loads, stores = copies(ins, bufs, load_sems), copies(bufs, outs, store_sems)
        for cp in [cp for block in loads for cp in block]:
            cp.start()
        for b, (k, rows) in enumerate(blocks):
            for cp in loads[b]:
                cp.wait()
            w_buf, g_buf, m_buf, v_buf = bufs[4 * k:4 * k + 4]
            w_buf[rows, :], m_buf[rows, :], v_buf[rows, :] = _adam_update(
                w_buf[rows, :], g_buf[rows, :], m_buf[rows, :], v_buf[rows, :])
            for cp in stores[b]:
                cp.start()
        for cp in [cp for block in stores for cp in block]:
            cp.wait()

    any_spec = pl.BlockSpec(memory_space=pl.ANY)
    flat = pl.pallas_call(
        body, name=name, out_shape=tuple(jax.ShapeDtypeStruct(a.shape, F32) for a in arrays),
        in_specs=[any_spec] * n, out_specs=(any_spec,) * n,
        scratch_shapes=[pltpu.VMEM(a.shape, F32) for a in arrays] + [pltpu.SemaphoreType.DMA((4 * len(blocks),))] * 2,
        compiler_params=pltpu.CompilerParams(vmem_limit_bytes=VMEM_LIMIT_BYTES),
    )(*arrays)
    return [(flat[4 * k + 1], flat[4 * k], flat[4 * k + 2], flat[4 * k + 3]) for k in range(len(items))]


def _shift_down(ext, s):
    return pltpu.roll(ext, s, 0)


def _tile_shift(v, s):
    rows, cols = v.shape
    tiles = v.reshape(rows // F32_SUBLANES, F32_SUBLANES, cols)
    return pltpu.roll(tiles, s % F32_SUBLANES, 1).reshape(rows, cols)


def _shift_up(ext, s):
    return pltpu.roll(ext, ext.shape[0] - s, 0)


def _lru_gates(xc, wa_ref, ba, wx_ref, bx, lam):
    pa, px = [], []
    for h in range(LRU_HEADS):
        xh = xc[:, h * HEAD_DIM:(h + 1) * HEAD_DIM].astype(BF16)
        pa.append(_dot(xh, wa_ref[h]))
        px.append(_dot(xh, wx_ref[h]))
    r = _sigmoid(jnp.concatenate(pa, axis=1) + ba)
    ig = _sigmoid(jnp.concatenate(px, axis=1) + bx)
    sp = _softplus(-lam)
    log_a = (-LRU_C * r) * sp
    a = jnp.exp(log_a)
    mult = jnp.sqrt(jnp.tanh(-log_a) * (1.0 + a * a))
    return r, ig, a, mult, sp


def _conv(ext, w_ref, b):
    y = b + _shift_down(ext, 3) * w_ref[0:1, :]
    y = y + _shift_down(ext, 2) * w_ref[1:2, :]
    y = y + _shift_down(ext, 1) * w_ref[2:3, :]
    y = y + ext * w_ref[3:4, :]
    return y[CONV_HIST:, :]


def _pool_diff(ext, pos):
    out = []
    for g, k in enumerate(POOL_WINDOWS):
        col = ext[:, g * POOL_GROUP_DIM:(g + 1) * POOL_GROUP_DIM]
        s = col
        for step in range(g + 1):
            s = s + _shift_down(s, 2 ** step)
        count = jnp.minimum(pos + 1, k).astype(F32)
        out.append(s[POOL_HIST:, :] / count - col[POOL_HIST:, :])
    return out


def _pool_mix(diff, pw_ref):
    return jnp.concatenate([_dot(diff[g].astype(BF16), pw_ref[g]) for g in range(len(POOL_WINDOWS))], axis=1)


def _branch_specs(tb, row_map, fixed):
    fixed3 = lambda i: (0, 0, 0)
    return [pl.BlockSpec((CONV_WIDTH, D_MODEL), fixed), pl.BlockSpec((1, D_MODEL), fixed),
            pl.BlockSpec((LRU_HEADS, HEAD_DIM, HEAD_DIM), fixed3), pl.BlockSpec((1, D_MODEL), fixed),
            pl.BlockSpec((LRU_HEADS, HEAD_DIM, HEAD_DIM), fixed3), pl.BlockSpec((1, D_MODEL), fixed),
            pl.BlockSpec((1, D_MODEL), fixed),
            pl.BlockSpec((len(POOL_WINDOWS), POOL_GROUP_DIM, POOL_GROUP_DIM), fixed3),
            pl.BlockSpec((1, POOL_WIDTH), fixed)]


def _branches_fwd(z, weights, seq, tb, shards):
    t = z.shape[0]
    nb = t // tb
    nbe = seq // tb
    groups = tb // F32_SUBLANES
    n = len(shards)

    def body(xa_ref, ga_ref, xb_ref, gb_ref, cw_ref, cb_ref, wa_ref, ba_ref, wx_ref, bx_ref, lam_ref,
             pw_ref, ps_ref, *refs):
        g_ins = refs[:n]
        ya_ref, yb_ref, hl_ref = refs[n:n + 3]
        g_outs = refs[n + 3:2 * n + 3]
        xa_ext, xb_ext, carry, a_s, u_s, send_sems, recv_sems, local_sems = refs[2 * n + 3:]
        blk = pl.program_id(0) % nbe
        start_gather, relay_gather, finish_gather = _gather_steps(shards, g_ins, g_outs, send_sems, recv_sems,
                                                                  local_sems)
        pl.when(pl.program_id(0) == 0)(start_gather)
        pl.when(pl.program_id(0) == nb // 2)(relay_gather)

        @pl.when(blk == 0)
        def _():
            xa_ext[0:CONV_HIST, :] = jnp.zeros((CONV_HIST, D_MODEL), F32)
            xb_ext[0:POOL_HIST, :] = jnp.zeros((POOL_HIST, POOL_WIDTH), F32)
            carry[...] = jnp.zeros_like(carry)

        xa_ext[CONV_HIST:, :] = xa_ref[...]
        xb_ext[POOL_HIST:, :] = xb_ref[...]
        ea = xa_ext[...]
        eb = xb_ext[...]
        xa_ext[0:CONV_HIST, :] = ea[tb:, :]
        xb_ext[0:POOL_HIST, :] = eb[tb:, :]

        xc = _conv(ea, cw_ref, cb_ref[...])
        _, ig, a, mult, _ = _lru_gates(xc, wa_ref, ba_ref[...], wx_ref, bx_ref[...], lam_ref[...])
        u = mult * (ig * xc)
        row8 = lax.broadcasted_iota(jnp.int32, (tb, D_MODEL), 0) % F32_SUBLANES
        for s in (1, 2, 4):
            m = row8 >= s
            u = jnp.where(m, a * _tile_shift(u, s) + u, u)
            a = jnp.where(m, a * _tile_shift(a, s), a)
        a_s[...] = a
        u_s[...] = u

        def step(g, cr):
            sl = pl.ds(pl.multiple_of(g * F32_SUBLANES, F32_SUBLANES), F32_SUBLANES)
            hb = a_s[sl, :] * cr + u_s[sl, :]
            hl_ref[sl, :] = hb
            return jnp.broadcast_to(hb[F32_SUBLANES - 1:F32_SUBLANES, :], (F32_SUBLANES, D_MODEL))

        carry[...] = lax.fori_loop(0, groups, step, carry[...], unroll=4)
        ga = ga_ref[...]
        ya_ref[...] = (hl_ref[...] * (ga * _sigmoid(ga))).astype(BF16)

        pos = blk * tb + lax.broadcasted_iota(jnp.int32, (tb, POOL_GROUP_DIM), 0)
        ypre = _pool_mix(_pool_diff(eb, pos), pw_ref)
        gb = gb_ref[...]
        yb_ref[...] = ((ypre * ps_ref[...]) * (gb * _sigmoid(gb))).astype(BF16)
        pl.when(pl.program_id(0) == nb - 1)(finish_gather)

    row = lambda i: (i, 0)
    fixed = lambda i: (0, 0)
    any_spec = pl.BlockSpec(memory_space=pl.ANY)
    in_specs = [pl.BlockSpec((tb, D_MODEL), lambda i: (i, 0)), pl.BlockSpec((tb, D_MODEL), lambda i: (i, 1)),
                pl.BlockSpec((tb, POOL_WIDTH), lambda i: (i, 4)), pl.BlockSpec((tb, POOL_WIDTH), lambda i: (i, 5)),
                ] + _branch_specs(tb, row, fixed) + [any_spec] * n
    g_shape, g_sems = _gather_shapes(shards)
    return pl.pallas_call(
        body, name="branches_fwd",
        out_shape=tuple([jax.ShapeDtypeStruct((t, D_MODEL), BF16), jax.ShapeDtypeStruct((t, POOL_WIDTH), BF16),
                         jax.ShapeDtypeStruct((t, D_MODEL), F32)] + g_shape),
        grid=(nb,), in_specs=in_specs,
        out_specs=tuple([pl.BlockSpec((tb, D_MODEL), row), pl.BlockSpec((tb, POOL_WIDTH), row),
                         pl.BlockSpec((tb, D_MODEL), row)] + [any_spec] * n),
        scratch_shapes=[pltpu.VMEM((tb + CONV_HIST, D_MODEL), F32), pltpu.VMEM((tb + POOL_HIST, POOL_WIDTH), F32),
                        pltpu.VMEM((F32_SUBLANES, D_MODEL), F32),
                        pltpu.VMEM((tb, D_MODEL), F32), pltpu.VMEM((tb, D_MODEL), F32)] + g_sems,
        compiler_params=pltpu.CompilerParams(dimension_semantics=("arbitrary",),
                                             vmem_limit_bytes=VMEM_LIMIT_BYTES),
    )(z, z, z, z, *weights, *[sh[0] for sh in shards])


def _branches_bwd(z, hl, dya, dyb, dzm, weights, vec_bag, seq, tb, riders):
    t = z.shape[0]
    nb = t // tb
    nbe = seq // tb
    groups = tb // F32_SUBLANES
    nr = len(riders)

    def body(xa_ref, xap_ref, ga_ref, xb_ref, xbp_ref, gb_ref, hl_ref, hlp_ref, dya_ref, dyb_ref, dzm_ref,
             cw_ref, cb_ref, wa_ref, ba_ref, wx_ref, bx_ref, lam_ref, pw_ref, ps_ref, vec_in_ref, *rest):
        pairs, (dz_ref, vec_ref, mat_ref), grads = rest[:2 * nr], rest[2 * nr:2 * nr + 3], rest[2 * nr + 3:4 * nr + 3]
        xa_ext, xb_ext, hl_ext, a_ext, dxc_ext, dwin_ext, g_carry, b_s, d_s, g_s = rest[4 * nr + 3:]
        i = pl.program_id(0)
        blk = (nb - 1 - i) % nbe

        def mat_rows(name, k):
            at = MAT_BAG_AT[name] + k * HEAD_DIM
            return slice(at, at + HEAD_DIM)

        def rider(k):
            grads[2 * k][...] += _dot_tn(pairs[2 * k][...], pairs[2 * k + 1][...])

        @pl.when(i == 0)
        def _():
            vec_ref[...] = vec_in_ref[...]
            mat_ref[...] = jnp.zeros_like(mat_ref)
            for k in range(nr):
                grads[2 * k][...] = jnp.zeros_like(grads[2 * k])

        @pl.when(blk == nbe - 1)
        def _():
            a_ext[tb:, :] = jnp.zeros((F32_SUBLANES, D_MODEL), F32)
            dxc_ext[tb:, :] = jnp.zeros((CONV_HIST, D_MODEL), F32)
            dwin_ext[tb:, :] = jnp.zeros((POOL_HIST, POOL_WIDTH), F32)
            g_carry[...] = jnp.zeros_like(g_carry)

        live = (blk > 0).astype(F32)
        xa_ext[0:CONV_HIST, :] = xap_ref[...] * live
        xa_ext[CONV_HIST:, :] = xa_ref[...]
        xb_ext[0:POOL_HIST, :] = xbp_ref[...] * live
        xb_ext[POOL_HIST:, :] = xb_ref[...]
        hl_ext[0:F32_SUBLANES, :] = hlp_ref[...] * live
        hl_ext[F32_SUBLANES:, :] = hl_ref[...]
        ea = xa_ext[...]
        eb = xb_ext[...]
        rider(0)

        xc = _conv(ea, cw_ref, cb_ref[...])
        lam = lam_ref[...]
        r, ig, a, mult, sp = _lru_gates(xc, wa_ref, ba_ref[...], wx_ref, bx_ref[...], lam)
        hl = hl_ref[...]
        ga = ga_ref[...]
        sga = _sigmoid(ga)
        dya = dya_ref[...]
        dhl = dya * (ga * sga)
        dz_ref[:, D_MODEL:2 * D_MODEL] = (dya * hl * (sga * (1.0 + ga * (1.0 - sga)))).astype(BF16)

        a_ext[0:tb, :] = a
        b = _shift_up(a_ext[...], 1)[0:tb, :]
        a_ext[tb:, :] = jnp.broadcast_to(a[0:1, :], (F32_SUBLANES, D_MODEL))
        d = dhl
        row8 = lax.broadcasted_iota(jnp.int32, (tb, D_MODEL), 0) % F32_SUBLANES
        for s in (1, 2, 4):
            m = row8 < F32_SUBLANES - s
            d = jnp.where(m, d + b * _tile_shift(d, -s), d)
            b = jnp.where(m, b * _tile_shift(b, -s), b)
        b_s[...] = b
        d_s[...] = d

        def step(k, cr):
            sl = pl.ds(pl.multiple_of((groups - 1 - k) * F32_SUBLANES, F32_SUBLANES), F32_SUBLANES)
            gb_ = d_s[sl, :] + b_s[sl, :] * cr
            g_s[sl, :] = gb_
            return jnp.broadcast_to(gb_[0:1, :], (F32_SUBLANES, D_MODEL))

        g_carry[...] = lax.fori_loop(0, groups, step, g_carry[...], unroll=4)
        rider(1)
        gsc = g_s[...]
        da = gsc * _shift_down(hl_ext[...], 1)[F32_SUBLANES:, :]
        dmult = gsc * (ig * xc)
        dig = gsc * (mult * xc)
        dxc = gsc * (mult * ig)
        dlog_a = da * a - (a * a) * dmult / mult
        dr = dlog_a * (-LRU_C * sp)
        vec_ref[_bag_row("lru_lambda"), :] += jnp.sum(dlog_a * (-LRU_C * r), axis=0, keepdims=True)
        dpa = dr * (r * (1.0 - r))
        dpx = dig * (ig * (1.0 - ig))
        vec_ref[_bag_row("lru_b_a"), :] += jnp.sum(dpa, axis=0, keepdims=True)
        vec_ref[_bag_row("lru_b_x"), :] += jnp.sum(dpx, axis=0, keepdims=True)
        back = []
        for h in range(LRU_HEADS):
            cols = slice(h * HEAD_DIM, (h + 1) * HEAD_DIM)
            xh = xc[:, cols].astype(BF16)
            dpa_h = dpa[:, cols].astype(BF16)
            dpx_h = dpx[:, cols].astype(BF16)
            mat_ref[mat_rows("lru_w_a", h), :] += _dot_tn(xh, dpa_h)
            mat_ref[mat_rows("lru_w_x", h), :] += _dot_tn(xh, dpx_h)
            back.append(_dot_nt(dpa_h, wa_ref[h]) + _dot_nt(dpx_h, wx_ref[h]))
        dxc = dxc + jnp.concatenate(back, axis=1)
        vec_ref[_bag_row("conv_b"), :] += jnp.sum(dxc, axis=0, keepdims=True)
        for k in range(CONV_WIDTH):
            tap = _shift_down(ea, CONV_WIDTH - 1 - k)[CONV_HIST:, :] if k < CONV_WIDTH - 1 else ea[CONV_HIST:, :]
            vec_ref[_bag_row("conv_w", k), :] += jnp.sum(dxc * tap, axis=0, keepdims=True)
        dxc_ext[0:tb, :] = dxc
        ed = dxc_ext[...]
        dxa = ed * cw_ref[3:4, :]
        dxa = dxa + _shift_up(ed, 1) * cw_ref[2:3, :]
        dxa = dxa + _shift_up(ed, 2) * cw_ref[1:2, :]
        dxa = dxa + _shift_up(ed, 3) * cw_ref[0:1, :]
        dz_ref[:, 0:D_MODEL] = dxa[0:tb, :].astype(BF16)
        dxc_ext[tb:, :] = dxc[0:CONV_HIST, :]

        pos = blk * tb + lax.broadcasted_iota(jnp.int32, (tb, POOL_GROUP_DIM), 0)
        diff = _pool_diff(eb, pos)
        rider(2)
        ypre = _pool_mix(diff, pw_ref)
        ps = ps_ref[...]
        gb = gb_ref[...]
        sgb = _sigmoid(gb)
        dyb = dyb_ref[...]
        dyp = dyb * (gb * sgb)
        dz_ref[:, 2 * D_MODEL + POOL_WIDTH:3 * D_MODEL] = (
            dyb * (ypre * ps) * (sgb * (1.0 + gb * (1.0 - sgb)))).astype(BF16)
        vec_ref[_bag_row("pool_scale"), 0:POOL_WIDTH] += jnp.sum(dyp * ypre, axis=0, keepdims=True)
        dypre = dyp * ps
        for g, k in enumerate(POOL_WINDOWS):
            cols = slice(g * POOL_GROUP_DIM, (g + 1) * POOL_GROUP_DIM)
            dyg = dypre[:, cols].astype(BF16)
            mat_ref[mat_rows("pool_w", g), :] += _dot_tn(diff[g].astype(BF16), dyg)
            ddiff = _dot_nt(dyg, pw_ref[g])
            count = jnp.minimum(pos + 1, k).astype(F32)
            dwin = ddiff / count
            dwin_ext[0:tb, cols] = dwin
            s = dwin_ext[:, cols]
            for step_ in range(g + 1):
                s = s + _shift_up(s, 2 ** step_)
            dz_ref[:, 2 * D_MODEL + g * POOL_GROUP_DIM:2 * D_MODEL + (g + 1) * POOL_GROUP_DIM] = (
                s[0:tb, :] - ddiff).astype(BF16)
            dwin_ext[tb:, cols] = dwin[0:POOL_HIST, :]

        dz_ref[:, 3 * D_MODEL:] = dzm_ref[...]

        @pl.when(i == nb - 1)
        def _():
            row = _bag_row("lru_lambda")
            vec_ref[row, :] = vec_ref[row, :] * (-_sigmoid(-lam))
            for k in range(nr):
                grads[2 * k + 1][...] = grads[2 * k][...].astype(BF16)

    rev = lambda i: (nb - 1 - i, 0)
    fixed = lambda i: (0, 0)

    def prev(rows, col):
        per = tb // rows
        return lambda i: (jnp.maximum((nb - 1 - i) * per - 1, 0), col)

    in_specs = [pl.BlockSpec((tb, D_MODEL), lambda i: (nb - 1 - i, 0)),
                pl.BlockSpec((CONV_HIST, D_MODEL), prev(CONV_HIST, 0)),
                pl.BlockSpec((tb, D_MODEL), lambda i: (nb - 1 - i, 1)),
                pl.BlockSpec((tb, POOL_WIDTH), lambda i: (nb - 1 - i, 4)),
                pl.BlockSpec((POOL_HIST, POOL_WIDTH), prev(POOL_HIST, 4)),
                pl.BlockSpec((tb, POOL_WIDTH), lambda i: (nb - 1 - i, 5)),
                pl.BlockSpec((tb, D_MODEL), rev),
                pl.BlockSpec((F32_SUBLANES, D_MODEL), prev(F32_SUBLANES, 0)),
                pl.BlockSpec((tb, D_MODEL), rev), pl.BlockSpec((tb, POOL_WIDTH), rev),
                pl.BlockSpec((tb, 2 * D_MODEL), rev)] + _branch_specs(tb, rev, fixed) + [
                    pl.BlockSpec((VEC_BAG_ROWS, D_MODEL), fixed)]
    vec_at = len(in_specs) - 1
    out_shape = [jax.ShapeDtypeStruct((t, IN_COLS), BF16), jax.ShapeDtypeStruct((VEC_BAG_ROWS, D_MODEL), F32),
                 jax.ShapeDtypeStruct((MAT_BAG_ROWS, HEAD_DIM), F32)]
    out_specs = [pl.BlockSpec((tb, IN_COLS), rev), pl.BlockSpec((VEC_BAG_ROWS, D_MODEL), fixed),
                 pl.BlockSpec((MAT_BAG_ROWS, HEAD_DIM), fixed)]
    for lhs, rhs in riders:
        in_specs += [pl.BlockSpec((tb, lhs.shape[1]), rev), pl.BlockSpec((tb, rhs.shape[1]), rev)]
        grad = (lhs.shape[1], rhs.shape[1])
        out_shape += [jax.ShapeDtypeStruct(grad, F32), jax.ShapeDtypeStruct(grad, BF16)]
        out_specs += [pl.BlockSpec(grad, fixed)] * 2
    scratch = [pltpu.VMEM((tb + CONV_HIST, D_MODEL), F32), pltpu.VMEM((tb + POOL_HIST, POOL_WIDTH), F32),
               pltpu.VMEM((tb + F32_SUBLANES, D_MODEL), F32), pltpu.VMEM((tb + F32_SUBLANES, D_MODEL), F32),
               pltpu.VMEM((tb + CONV_HIST, D_MODEL), F32), pltpu.VMEM((tb + POOL_HIST, POOL_WIDTH), F32),
               pltpu.VMEM((F32_SUBLANES, D_MODEL), F32),
               pltpu.VMEM((tb, D_MODEL), F32), pltpu.VMEM((tb, D_MODEL), F32), pltpu.VMEM((tb, D_MODEL), F32)]
    return pl.pallas_call(
        body, name="branches_bwd", out_shape=tuple(out_shape), grid=(nb,), in_specs=in_specs,
        out_specs=tuple(out_specs), scratch_shapes=scratch, input_output_aliases={vec_at: 1},
        compiler_params=pltpu.CompilerParams(dimension_semantics=("arbitrary",),
                                             vmem_limit_bytes=VMEM_LIMIT_BYTES),
    )(z, z, z, z, z, z, hl, hl, dya, dyb, dzm, *weights, vec_bag, *[a for pair in riders for a in pair])


def _merge_head(x2d, ya, yb, z, p2d, tgt, w_pl, w_pp, w_out, w_pg, w_pe, g2, gf, tb):
    t = x2d.shape[0]
    p_dim = p2d.shape[1]

    def body(x_ref, ya_ref, yb_ref, ma_ref, mb_ref, p_ref, t_ref, wpl_ref, wpp_ref, wout_ref, wpg_ref, wpe_ref,
             g2_ref, gf_ref,
             bag_ref, dxr_ref, dya_ref, dyb_ref, dzm_ref,
             mg_ref, do_ref, hn_ref, dgp_ref, dpe_ref, da_ref, dbm_ref, pbf_ref):
        @pl.when(pl.program_id(0) == 0)
        def _():
            bag_ref[...] = jnp.zeros_like(bag_ref)

        a_ = _dot(ya_ref[...], wpl_ref[...])
        bm = _dot(yb_ref[...], wpp_ref[...])
        sa = _sigmoid(ma_ref[...])
        sb = _sigmoid(mb_ref[...])
        mg = (sa * a_ + sb * bm).astype(BF16)
        mg_ref[...] = mg
        x1 = x_ref[...] + _dot(mg, wout_ref[...])
        xn2, r2 = _rms(x1)
        g2 = g2_ref[...]
        hn = (xn2 * g2).astype(BF16)
        hn_ref[...] = hn
        gate = _sigmoid(_dot(hn, wpg_ref[...]))
        pbf = p_ref[...].astype(BF16)
        pbf_ref[...] = pbf
        pe = _dot(pbf, wpe_ref[...])
        x2 = x1 + gate * pe
        xn3, r3 = _rms(x2)
        gf = gf_ref[...]
        err = xn3 * gf - t_ref[...]
        bag_ref[_bag_rows("loss"), 0:128] += 0.5 * jnp.sum(jnp.mean(err * err, axis=-1))

        dy = err * (1.0 / D_MODEL)
        bag_ref[_bag_row("final_g"), :] += jnp.sum(dy * xn3, axis=0, keepdims=True)
        dx2 = _rms_bwd(dy * gf, xn3, r3)
        dpe_ref[...] = (dx2 * gate).astype(BF16)
        dgp = ((dx2 * pe) * (gate * (1.0 - gate))).astype(BF16)
        dgp_ref[...] = dgp
        dhn = _dot_nt(dgp, wpg_ref[...])
        bag_ref[_bag_row("ple_norm_g"), :] += jnp.sum(dhn * xn2, axis=0, keepdims=True)
        dx1 = dx2 + _rms_bwd(dhn * g2, xn2, r2)
        dxr_ref[...] = dx1
        do = dx1.astype(BF16)
        do_ref[...] = do
        dmg = _dot_nt(do, wout_ref[...])
        da = (dmg * sa).astype(BF16)
        dbm = (dmg * sb).astype(BF16)
        da_ref[...] = da
        dbm_ref[...] = dbm
        dzm_ref[:, 0:D_MODEL] = (dmg * a_ * (sa * (1.0 - sa))).astype(BF16)
        dzm_ref[:, D_MODEL:] = (dmg * bm * (sb * (1.0 - sb))).astype(BF16)
        dya_ref[...] = _dot_nt(da, wpl_ref[...])
        dyb_ref[...] = _dot_nt(dbm, wpp_ref[...])

    row = lambda i: (i, 0)
    fixed = lambda i: (0, 0)

    def resident(shape):
        return pl.BlockSpec(shape, fixed, pipeline_mode=pl.Buffered(1))

    tok = lambda width: pl.BlockSpec((tb, width), row)
    in_specs = [tok(D_MODEL), tok(D_MODEL), tok(POOL_WIDTH),
                pl.BlockSpec((tb, D_MODEL), lambda i: (i, 3)), pl.BlockSpec((tb, D_MODEL), lambda i: (i, 4)),
                tok(p_dim), tok(D_MODEL),
                resident((D_MODEL, D_MODEL)), resident((POOL_WIDTH, D_MODEL)), resident((D_MODEL, D_MODEL)),
                resident((D_MODEL, D_MODEL)), resident((p_dim, D_MODEL)),
                pl.BlockSpec((1, D_MODEL), fixed), pl.BlockSpec((1, D_MODEL), fixed)]
    bf = lambda width: jax.ShapeDtypeStruct((t, width), BF16)
    f32 = lambda width: jax.ShapeDtypeStruct((t, width), F32)
    out_shape = (jax.ShapeDtypeStruct((VEC_BAG_ROWS, D_MODEL), F32),
                 f32(D_MODEL), f32(D_MODEL), f32(POOL_WIDTH), bf(2 * D_MODEL),
                 bf(D_MODEL), bf(D_MODEL), bf(D_MODEL), bf(D_MODEL), bf(D_MODEL), bf(D_MODEL), bf(D_MODEL), bf(p_dim))
    out_specs = (pl.BlockSpec((VEC_BAG_ROWS, D_MODEL), fixed),
                 tok(D_MODEL), tok(D_MODEL), tok(POOL_WIDTH), tok(2 * D_MODEL),
                 tok(D_MODEL), tok(D_MODEL), tok(D_MODEL), tok(D_MODEL), tok(D_MODEL), tok(D_MODEL), tok(D_MODEL),
                 tok(p_dim))
    return pl.pallas_call(
        body, name="merge_head", out_shape=out_shape, grid=(t // tb,), in_specs=in_specs, out_specs=out_specs,
        compiler_params=pltpu.CompilerParams(dimension_semantics=("arbitrary",),
                                             vmem_limit_bytes=VMEM_LIMIT_BYTES),
    )(x2d, ya, yb, z, z, p2d, tgt, w_pl, w_pp, w_out, w_pg, w_pe, g2, gf)


def kernel(x, p, norm_g, w_in, conv_w, conv_b, lru_w_a, lru_b_a, lru_w_x, lru_b_x, lru_lambda, pool_w, pool_scale, w_proj_lru, w_proj_pool, w_out, ple_norm_g, w_ple_gate, w_ple_proj, final_g, loss_target, m_norm_g, m_w_in, m_conv_w, m_conv_b, m_lru_w_a, m_lru_b_a, m_lru_w_x, m_lru_b_x, m_lru_lambda, m_pool_w, m_pool_scale, m_w_proj_lru, m_w_proj_pool, m_w_out, m_ple_norm_g, m_w_ple_gate, m_w_ple_proj, m_final_g, v_norm_g, v_w_in, v_conv_w, v_conv_b, v_lru_w_a, v_lru_b_a, v_lru_w_x, v_lru_b_x, v_lru_lambda, v_pool_w, v_pool_scale, v_w_proj_lru, v_w_proj_pool, v_w_out, v_ple_norm_g, v_w_ple_gate, v_w_ple_proj, v_final_g):
    bsz, seq, _ = x.shape
    t = bsz * seq
    tb_mm = min(1024, seq)
    tb_seq = min(256, seq // 2) if seq >= 512 else seq
    x2d = x.reshape(t, D_MODEL)
    p2d = p.reshape(t, p.shape[-1])
    tgt = loss_target.reshape(t, D_MODEL)

    rest = [(w_proj_lru[0], 0), (w_proj_pool[0], 1), (w_out[0], 0), (w_ple_gate[0], 0), (w_ple_proj[0], 1)]
    z, h_bf, w_in_f, conv_w_f, *narrow = _in_proj_gather(
        x2d, norm_g, w_in[0], [(conv_w[0], 1, False)], tb_mm,
        [w for w, _ in rest] + [lru_w_a[0], lru_w_x[0], pool_w[0]])
    wa_bf, wx_bf, pw_bf = narrow[len(rest):]
    branch_w = (conv_w_f, conv_b, wa_bf, lru_b_a.reshape(1, D_MODEL), wx_bf, lru_b_x.reshape(1, D_MODEL),
                lru_lambda, pw_bf, pool_scale)

    ya, yb, hl, w_pl_f, w_pp_f, w_out_f, w_pg_f, w_pe_f = _branches_fwd(
        z, branch_w, seq, tb_seq, [(w16, axis, True) for w16, (_, axis) in zip(narrow, rest)])
    (vec_bag, dx_res, dya, dyb, dzm, mg_bf, do_bf, hn_bf, dgp_bf, dpe_bf, da_bf, dbm_bf, p_bf) = _merge_head(
        x2d, ya, yb, z, p2d, tgt, w_pl_f, w_pp_f, w_out_f, w_pg_f, w_pe_f, ple_norm_g, final_g.reshape(1, D_MODEL),
        tb_seq)
    dz, vec_bag, mat_bag, g_out, g_out16, g_pp, g_pp16, g_pe, g_pe16 = _branches_bwd(
        z, hl, dya, dyb, dzm, branch_w, vec_bag, seq, tb_seq, [(mg_bf, do_bf), (yb, dbm_bf), (p_bf, dpe_bf)])

    tb_dw = min(1024, seq)
    def row_pieces(g32, g16):
        pieces = (8, g32.shape[0] // 8, g32.shape[1])
        return g32.reshape(pieces), False, g16.reshape(pieces)

    g_pl, g_pl16, g_pg, g_pg16 = _weight_grad([(ya, da_bf), (hn_bf, dgp_bf)], 1, tb_dw, "dw_proj")
    p_dim = p2d.shape[1]
    proj_parts = [row_pieces(g_pl[0], g_pl16[0]), (g_pp, True, g_pp16), row_pieces(g_out, g_out16),
                  row_pieces(g_pg[0], g_pg16[0]), (g_pe, True, g_pe16)]
    nb_dw = t // tb_dw
    g_in, g_in16, r_pl, r_pp, r_out, r_pg, r_pe, vec_mine, mat_mine = _weight_grad(
        [(h_bf, dz)], N_CHIPS, tb_dw, "dw_in",
        reduce=(proj_parts + [(vec_bag.reshape(8, VEC_BAG_ROWS // 8, D_MODEL), False, None),
                              (mat_bag.reshape(8, MAT_BAG_ROWS // 8, HEAD_DIM), False, None)],
                [BF16] * 5 + [F32] * 2,
                (0, nb_dw // 2, 2 * nb_dw - 1, 3 * nb_dw + nb_dw // 2, N_CHIPS * nb_dw - 1)))
    pieces = (8, D_MODEL // 2, IN_COLS // N_CHIPS)
    nb_seq = t // tb_seq
    dx, g_g1, r_in, vec_sum, mat_sum, g_cw = _in_proj_bwd(
        dz, w_in_f, x2d, dx_res, norm_g, tb_seq,
        reduce=([(g_in.reshape(pieces), False, g_in16.reshape(pieces))], BF16,
                (0, nb_seq // 8, nb_seq // 2, nb_seq - 1)),
        shards=[(vec_mine.reshape(VEC_BAG_ROWS // N_CHIPS, D_MODEL), 0, True),
                (mat_mine.reshape(MAT_BAG_ROWS // N_CHIPS, HEAD_DIM), 0, True)],
        take=(_bag_rows("conv_w"), D_MODEL // N_CHIPS))

    big = [(w_in, r_in, m_w_in, v_w_in, 4), (w_proj_lru, r_pl, m_w_proj_lru, v_w_proj_lru, 1),
           (w_proj_pool, r_pp, m_w_proj_pool, v_w_proj_pool, 1), (w_out, r_out, m_w_out, v_w_out, 1),
           (w_ple_gate, r_pg, m_w_ple_gate, v_w_ple_gate, 1), (w_ple_proj, r_pe, m_w_ple_proj, v_w_ple_proj, 1)]
    u_in, u_pl, u_pp, u_out, u_pg, u_pe = [tuple(a[None] for a in u) for u in _adamw_group(
        [(w[0], g.reshape(w.shape[1:]), m[0], v[0], cuts) for w, g, m, v, cuts in big], "adamw_sharded")]

    small = [("norm_g", norm_g, m_norm_g, v_norm_g), ("conv_b", conv_b, m_conv_b, v_conv_b),
             ("lru_w_a", lru_w_a, m_lru_w_a, v_lru_w_a), ("lru_b_a", lru_b_a, m_lru_b_a, v_lru_b_a),
             ("lru_w_x", lru_w_x, m_lru_w_x, v_lru_w_x), ("lru_b_x", lru_b_x, m_lru_b_x, v_lru_b_x),
             ("lru_lambda", lru_lambda, m_lru_lambda, v_lru_lambda), ("pool_w", pool_w, m_pool_w, v_pool_w),
             ("pool_scale", pool_scale, m_pool_scale, v_pool_scale),
             ("ple_norm_g", ple_norm_g, m_ple_norm_g, v_ple_norm_g), ("final_g", final_g, m_final_g, v_final_g)]

    def view(a):
        return a.reshape(-1, a.shape[-1]) if a.ndim != 3 else a[0]

    flat = _adamw_replicated(vec_sum, mat_sum, g_g1, [(name,) + tuple(view(a) for a in arrs) for name, *arrs in small],
                             (conv_w[0], m_conv_w[0], v_conv_w[0], g_cw))
    u_small = {name: tuple(flat[4 * k + pick].reshape(arrs[0].shape) for pick in range(4))
               for k, (name, *arrs) in enumerate(small)}
    u_cw = tuple(a[None] for a in flat[4 * len(small):4 * len(small) + 4])

    loss = flat[-1].reshape(())
    grad_x = dx.reshape(bsz, seq, D_MODEL)

    def ordered(pick):
        s = {name: u[pick] for name, u in u_small.items()}
        return [s["norm_g"], u_in[pick], u_cw[pick], s["conv_b"], s["lru_w_a"], s["lru_b_a"], s["lru_w_x"], s["lru_b_x"],
                s["lru_lambda"], s["pool_w"], s["pool_scale"], u_pl[pick], u_pp[pick], u_out[pick], s["ple_norm_g"],
                u_pg[pick], u_pe[pick], s["final_g"]]

    return (loss, grad_x, *ordered(0), *ordered(1), *ordered(2), *ordered(3))
```

```python
import jax
import jax.numpy as jnp
from jax import lax
from jax.experimental import pallas as pl
from jax.experimental.pallas import tpu as pltpu

F32 = jnp.float32
BF16 = jnp.bfloat16
MESH = pl.DeviceIdType.MESH

D_MODEL = 1024
LRU_HEADS = 8
HEAD_DIM = 128
CONV_WIDTH = 4
LRU_C = 8.0
POOL_WIDTH = 512
POOL_WINDOWS = (2, 4, 8, 16)
POOL_GROUP_DIM = 128
IN_COLS = 5120
N_CHIPS = 4
EPS = 1e-6

ADAM_LR = 0.001
ADAM_B1 = 0.9
ADAM_B2 = 0.999
ADAM_EPS = 1e-08
ADAM_WD = 0.01
ADAM_STEP = 10

F32_SUBLANES = 8
CONV_HIST = 8
POOL_HIST = 16
VMEM_LIMIT_BYTES = 58 * 1024 * 1024
VEC_BAG_SLOTS = ("norm_g", "conv_w", "conv_b", "lru_b_a", "lru_b_x", "lru_lambda", "pool_scale", "ple_norm_g",
                 "final_g", "loss")
VEC_BAG_ROWS = 128
MAT_BAG_AT = {"lru_w_a": 0, "lru_w_x": LRU_HEADS * HEAD_DIM, "pool_w": 2 * LRU_HEADS * HEAD_DIM}
MAT_BAG_ROWS = 2 * LRU_HEADS * HEAD_DIM + len(POOL_WINDOWS) * POOL_GROUP_DIM


def _bag_row(name, k=0):
    at = F32_SUBLANES * VEC_BAG_SLOTS.index(name) + k
    return slice(at, at + 1)


def _bag_rows(name):
    at = F32_SUBLANES * VEC_BAG_SLOTS.index(name)
    return slice(at, at + F32_SUBLANES)


def _dot(a, b):
    return jnp.dot(a, b, preferred_element_type=F32)


def _dot_nt(a, b):
    return lax.dot_general(a, b, (((1,), (1,)), ((), ())), preferred_element_type=F32)


def _dot_tn(a, b):
    return lax.dot_general(a, b, (((0,), (0,)), ((), ())), preferred_element_type=F32)


def _sigmoid(v):
    return jax.nn.sigmoid(v)


def _softplus(v):
    return jnp.maximum(v, 0.0) + jnp.log1p(jnp.exp(-jnp.abs(v)))


def _place():
    return lax.axis_index("x"), lax.axis_index("y"), lax.axis_index("c")


GATHER_SEMS = 6


def _gather_shapes(shards):
    out_shape = []
    for arr, axis, _ in shards:
        r, cols = arr.shape
        out_shape.append(jax.ShapeDtypeStruct((N_CHIPS * r, cols) if axis == 0 else (r, N_CHIPS * cols), arr.dtype))
    n = len(shards)
    sems = [pltpu.SemaphoreType.DMA((n * GATHER_SEMS,)), pltpu.SemaphoreType.DMA((n * GATHER_SEMS,)),
            pltpu.SemaphoreType.DMA((n,))]
    return out_shape, sems


def _gather_steps(shards, ins, outs, send_sems, recv_sems, local_sems):
    n = len(shards)
    x, y, c = _place()
    me, sibling = (x, y, c), (x, y, 1 - c)
    chips = [(x, 1 - y), (1 - x, y), (1 - x, 1 - y)]

    def region(k, cx, cy, hc):
        (r, cols), axis = shards[k][0].shape, shards[k][1]
        j = 2 * cx + cy
        if axis == 0:
            if hc is None:
                return outs[k].at[pl.ds(j * r, r), :]
            return outs[k].at[pl.ds(j * r + hc * (r // 2), r // 2), :]
        if hc is None:
            return outs[k].at[:, pl.ds(j * cols, cols)]
        return outs[k].at[pl.ds(hc * (r // 2), r // 2), pl.ds(j * cols, cols)]

    def remote(k, sem, block, to, src=None):
        dst = region(k, *block)
        return pltpu.make_async_remote_copy(
            src_ref=dst if src is None else src, dst_ref=dst,
            send_sem=send_sems.at[k * GATHER_SEMS + sem], recv_sem=recv_sems.at[k * GATHER_SEMS + sem],
            device_id=to, device_id_type=MESH)

    def first(k, idx):
        r, split = shards[k][0].shape[0], shards[k][2]
        src = ins[k].at[pl.ds(c * (r // 2), r // 2), :] if split else ins[k]
        return remote(k, idx, (x, y, c if split else None), (*chips[idx], c), src=src)

    def relay(k):
        src_chip = (jnp.bitwise_xor(x, 1 - c), jnp.bitwise_xor(y, c))
        dst_chip = (jnp.bitwise_xor(x, c), jnp.bitwise_xor(y, 1 - c))
        return remote(k, 2, (*src_chip, c), (*dst_chip, c))

    def passed(k, idx):
        return remote(k, 3 + idx, (*chips[idx], c), sibling)

    def mine(k):
        return pltpu.make_async_copy(ins[k], region(k, x, y, None), local_sems.at[k])

    def start():
        for k in range(n):
            mine(k).start()
            for idx in range(2 if shards[k][2] else 3):
                first(k, idx).start()

    def relay_on():
        for k in range(n):
            split = shards[k][2]
            for idx in range(2):
                remote(k, idx, (*chips[idx], c if split else None), me).wait_recv()
            if split:
                relay(k).start()
                passed(k, 0).start()
                passed(k, 1).start()

    def finish():
        for k in range(n):
            split = shards[k][2]
            remote(k, 2, (*chips[2], c if split else None), me).wait_recv()
            if split:
                passed(k, 2).start()
        for k in range(n):
            if shards[k][2]:
                for idx in range(3):
                    remote(k, 3 + idx, (*chips[idx], 1 - c), me).wait_recv()
        for k in range(n):
            if shards[k][2]:
                for cp in (first(k, 0), first(k, 1), relay(k), passed(k, 0), passed(k, 1), passed(k, 2)):
                    cp.wait_send()
            else:
                for idx in range(3):
                    first(k, idx).wait_send()
            mine(k).wait()

    return start, relay_on, finish


RS_ADD_ROWS = (64, 32, 16, 8)


N_DEV = 2 * N_CHIPS


def _all_reduce_scratch(shape):
    return [pltpu.VMEM((N_DEV,) + tuple(shape), F32), pltpu.SemaphoreType.DMA((N_DEV - 1,)),
            pltpu.SemaphoreType.DMA((N_DEV - 1,))]


def _all_reduce_tile(v_ref, o_ref, slots, send_sems, recv_sems):
    flips = [(dx, dy, dc) for dx in (0, 1) for dy in (0, 1) for dc in (0, 1)][1:]
    x, y, c = _place()
    mine = 4 * x + 2 * y + c

    def copy(k, to_flip, slot):
        dx, dy, dc = to_flip
        peer = (jnp.bitwise_xor(x, dx), jnp.bitwise_xor(y, dy), jnp.bitwise_xor(c, dc))
        return pltpu.make_async_remote_copy(
            src_ref=v_ref, dst_ref=slots.at[slot], send_sem=send_sems.at[k], recv_sem=recv_sems.at[k],
            device_id=peer, device_id_type=MESH)

    sends = [copy(k, flip, mine) for k, flip in enumerate(flips)]
    for cp in sends:
        cp.start()
    slots[mine] = v_ref[...]
    for k, (dx, dy, dc) in enumerate(flips):
        copy(k, (dx, dy, dc), jnp.bitwise_xor(mine, 4 * dx + 2 * dy + dc)).wait_recv()
    total = slots[0]
    for d in range(1, N_DEV):
        total = total + slots[d]
    o_ref[...] = total
    for cp in sends:
        cp.wait_send()


RS_SEMS = 8
RS_LOCAL_SEMS = 5


def _rs_piece_shape(part):
    arr, cols = part[0], part[1]
    return (arr.shape[0] // 2, arr.shape[1] // N_CHIPS) if cols else tuple(arr.shape[1:])


def _rs_operands(parts):
    return [p[0] for p in parts] + [p[0] if p[2] is None else p[2] for p in parts]


def _rs_wires(parts, wire):
    return list(wire) if isinstance(wire, (list, tuple)) else [wire] * len(parts)


def _rs_shapes(parts, wire):
    n = len(parts)
    shapes = [_rs_piece_shape(p) for p in parts]
    out_shape = [jax.ShapeDtypeStruct((2,) + s, F32) for s in shapes]
    scratch = []
    for lead, kind in ((N_CHIPS, "f32"), (N_CHIPS, "narrow"), (N_CHIPS, "wire"), (None, "f32"), (N_CHIPS, "wire")):
        for s, p, w in zip(shapes, parts, _rs_wires(parts, wire)):
            dtype = {"f32": F32, "narrow": F32 if p[2] is None else p[2].dtype, "wire": w}[kind]
            scratch.append(pltpu.VMEM(s if lead is None else (lead,) + s, dtype))
    scratch += [pltpu.SemaphoreType.DMA((n * RS_SEMS,)), pltpu.SemaphoreType.DMA((n * RS_SEMS,)),
                pltpu.SemaphoreType.DMA((n * RS_LOCAL_SEMS,))]
    return out_shape, scratch


def _rs_steps(parts, ins, outs, scratch):
    n = len(parts)
    own, sib, got, fin, snd = (scratch[k * n:(k + 1) * n] for k in range(5))
    send_sems, recv_sems, local_sems = scratch[5 * n:]
    shapes = [_rs_piece_shape(p) for p in parts]
    x, y, c = _place()
    j_me = 2 * x + y
    me, sibling = (x, y, c), (x, y, 1 - c)

    def piece(a, jj, core, narrow=False):
        ref = ins[n + a] if narrow else ins[a]
        if parts[a][1]:
            r, cl = shapes[a]
            return ref.at[pl.ds(core * r, r), pl.ds(jj * cl, cl)]
        return ref.at[2 * jj + core]

    def remote(a, sem, src, dst, to):
        return pltpu.make_async_remote_copy(
            src_ref=src, dst_ref=dst, send_sem=send_sems.at[a * RS_SEMS + sem],
            recv_sem=recv_sems.at[a * RS_SEMS + sem], device_id=to, device_id_type=MESH)

    def rows_loop(a, fn):
        r = shapes[a][0]
        step = max(s for s in RS_ADD_ROWS if r % s == 0)

        def it(i, carry):
            fn(pl.ds(pl.multiple_of(i * step, step), step))
            return carry

        lax.fori_loop(0, r // step, it, 0)

    def load(a, jj):
        return pltpu.make_async_copy(piece(a, jj, c), own[a].at[jj], local_sems.at[a * RS_LOCAL_SEMS + jj])

    def to_sibling(a, jj):
        return remote(a, jj, piece(a, jj, 1 - c, narrow=True), sib[a].at[jj], sibling)

    near = (jnp.bitwise_xor(x, 1 - c), jnp.bitwise_xor(y, c))
    far = (jnp.bitwise_xor(x, c), jnp.bitwise_xor(y, 1 - c))
    diag = (1 - x, 1 - y)
    FROM_NEAR, FROM_FAR, FEED = 0, 1, 2

    def chip_of(chip):
        return 2 * chip[0] + chip[1]

    def feed(a):
        return remote(a, 4, snd[a].at[chip_of(diag)], got[a].at[FEED], (*near, c))

    def to_near(a):
        return remote(a, 5, snd[a].at[chip_of(near)], got[a].at[FROM_NEAR], (*near, c))

    def to_far(a):
        return remote(a, 6, snd[a].at[chip_of(far)], got[a].at[FROM_FAR], (*far, c))

    def store(a):
        return pltpu.make_async_copy(fin[a], outs[a].at[c], local_sems.at[a * RS_LOCAL_SEMS + 4])

    def result_to_sibling(a):
        return remote(a, 7, fin[a], outs[a].at[c], sibling)

    def exchange():
        for a in range(n):
            for jj in range(N_CHIPS):
                load(a, jj).start()
                to_sibling(a, jj).start()

    def chip_sums():
        for a in range(n):
            for jj in range(N_CHIPS):
                load(a, jj).wait()
                remote(a, jj, sib[a].at[jj], sib[a].at[jj], me).wait_recv()

                def add(sl, a=a, jj=jj):
                    q = own[a][jj, sl, :] + sib[a][jj, sl, :].astype(F32)
                    own[a][jj, sl, :] = q
                    snd[a][jj, sl, :] = q.astype(snd[a].dtype)

                rows_loop(a, add)
        for a in range(n):
            feed(a).start()
        for a in range(n):
            to_near(a).start()

    def relay():
        for a in range(n):
            remote(a, 4, got[a].at[FEED], got[a].at[FEED], me).wait_recv()

            def add(sl, a=a):
                pair = own[a][chip_of(far), sl, :] + got[a][FEED, sl, :].astype(F32)
                snd[a][chip_of(far), sl, :] = pair.astype(snd[a].dtype)

            rows_loop(a, add)
            to_far(a).start()

    def totals():
        for a in range(n):
            remote(a, 5, got[a].at[FROM_NEAR], got[a].at[FROM_NEAR], me).wait_recv()
            remote(a, 6, got[a].at[FROM_FAR], got[a].at[FROM_FAR], me).wait_recv()

            def total(sl, a=a):
                fin[a][sl, :] = (own[a][j_me, sl, :] + got[a][FROM_NEAR, sl, :].astype(F32)) + (
                    got[a][FROM_FAR, sl, :].astype(F32))

            rows_loop(a, total)
            store(a).start()
            result_to_sibling(a).start()

    def finish():
        for a in range(n):
            remote(a, 7, outs[a].at[1 - c], outs[a].at[1 - c], me).wait_recv()
        for a in range(n):
            for jj in range(N_CHIPS):
                to_sibling(a, jj).wait_send()
            for cp in (feed(a), to_near(a), to_far(a), result_to_sibling(a)):
                cp.wait_send()
            store(a).wait()

    return exchange, chip_sums, relay, totals, finish


def _rms(x):
    r = lax.rsqrt(jnp.mean(x * x, axis=-1, keepdims=True) + EPS)
    return x * r, r


def _rms_bwd(dxn, xn, r):
    return r * (dxn - xn * jnp.mean(dxn * xn, axis=-1, keepdims=True))


def _in_proj_gather(x2d, norm_g, w_in_sh, shards, tb, casts):
    t = x2d.shape[0]
    nb = t // tb
    cols = IN_COLS // N_CHIPS
    half = D_MODEL // 2
    n = len(shards)
    nc = len(casts)

    def body(x_ref, g_ref, win_ref, *refs):
        ins, cast_ins = refs[:n], refs[n:n + nc]
        z_ref, h_ref, wfull_ref = refs[n + nc:n + nc + 3]
        outs, cast_outs = refs[n + nc + 3:2 * n + nc + 3], refs[2 * n + nc + 3:2 * (n + nc) + 3]
        scratch = refs[2 * (n + nc) + 3:]
        wv, h_all, send_sems, recv_sems, local_sems, w_send, w_recv, w_local, stage = scratch[:9]
        wide, narrow, cast_sems = scratch[9:9 + nc], scratch[9 + nc:9 + 2 * nc], scratch[9 + 2 * nc]
        s, i = pl.program_id(0), pl.program_id(1)
        x, y, c = _place()
        me, sibling = (x, y, c), (x, y, 1 - c)
        chips = [(x, 1 - y), (1 - x, y), (1 - x, 1 - y)]

        def w_half(cx, cy, hc):
            return wv.at[2 * cx + cy, pl.ds(hc * half, half), :]

        def w_remote(sem, block, to, src=None):
            dst = w_half(*block)
            return pltpu.make_async_remote_copy(
                src_ref=dst if src is None else src, dst_ref=dst, send_sem=w_send.at[sem],
                recv_sem=w_recv.at[sem], device_id=to, device_id_type=MESH)

        def w_first(idx):
            return w_remote(idx, (x, y, c), (*chips[idx], c))

        def w_relay():
            src_chip = (jnp.bitwise_xor(x, 1 - c), jnp.bitwise_xor(y, c))
            dst_chip = (jnp.bitwise_xor(x, c), jnp.bitwise_xor(y, 1 - c))
            return w_remote(2, (*src_chip, c), (*dst_chip, c))

        def w_pass(idx):
            return w_remote(3 + idx, (*chips[idx], c), sibling)

        def w_store(k, cx, cy):
            jj = 2 * cx + cy
            return pltpu.make_async_copy(wv.at[jj], wfull_ref.at[:, pl.ds(jj * cols, cols)], w_local.at[k])

        start_rest, relay_rest, finish_rest = _gather_steps(shards, ins, outs, send_sems, recv_sems, local_sems)

        def own(k, hc):
            return pltpu.make_async_copy(win_ref.at[pl.ds(pl.multiple_of(hc * half, half), half), :], stage.at[k],
                                         w_local.at[4 + 2 * k])

        def round_own(k, hc):
            own(k, hc).wait()
            wv[2 * x + y, pl.ds(pl.multiple_of(hc * half, half), half), :] = stage[k].astype(BF16)

        wide_in = [pltpu.make_async_copy(cast_ins[k], wide[k], cast_sems.at[k]) for k in range(nc)]
        narrow_out = [pltpu.make_async_copy(narrow[k], cast_outs[k], cast_sems.at[nc + k]) for k in range(nc)]

        @pl.when((s == 0) & (i == 0))
        def _():
            own(0, c).start()
            own(1, 1 - c).start()
            for cp in wide_in:
                cp.start()
            round_own(0, c)
            w_first(0).start()
            w_first(1).start()
            start_rest()
            round_own(1, 1 - c)
            w_store(0, x, y).start()

        @pl.when((s == 1) & (i == 0))
        def _():
            for k in range(nc):
                wide_in[k].wait()
                narrow[k][...] = wide[k][...].astype(BF16)
                narrow_out[k].start()
            w_remote(0, (*chips[0], c), me).wait_recv()
            w_remote(1, (*chips[1], c), me).wait_recv()
            w_relay().start()
            w_pass(0).start()
            w_pass(1).start()
            w_remote(3, (*chips[0], 1 - c), me).wait_recv()
            w_store(1, *chips[0]).start()

        @pl.when((s == 2) & (i == 0))
        def _():
            w_remote(4, (*chips[1], 1 - c), me).wait_recv()
            w_store(2, *chips[1]).start()

        @pl.when((s == 3) & (i == 0))
        def _():
            w_remote(2, (*chips[2], c), me).wait_recv()
            w_pass(2).start()
            w_remote(5, (*chips[2], 1 - c), me).wait_recv()
            w_store(3, *chips[2]).start()

        keep_h = pltpu.make_async_copy(h_all.at[i], h_ref.at[pl.ds(pl.multiple_of(i * tb, tb), tb), :], w_local.at[5])

        @pl.when(s == 0)
        def _():
            xn, _ = _rms(x_ref[...])
            h_all[i] = (xn * g_ref[...]).astype(BF16)
            keep_h.start()

        z_ref[...] = _dot(h_all[i], wv[jnp.bitwise_xor(2 * x + y, s)])
        pl.when(s == 0)(keep_h.wait)

        @pl.when((s == N_CHIPS - 1) & (i == nb - 1))
        def _():
            relay_rest()
            finish_rest()
            for cp in (w_first(0), w_first(1), w_relay(), w_pass(0), w_pass(1), w_pass(2)):
                cp.wait_send()
            w_store(0, x, y).wait()
            for idx in range(3):
                w_store(idx + 1, *chips[idx]).wait()
            for cp in narrow_out:
                cp.wait()

    rest_shape, rest_sems = _gather_shapes(shards)
    out_shape = [jax.ShapeDtypeStruct((t, IN_COLS), F32), jax.ShapeDtypeStruct((t, D_MODEL), BF16),
                 jax.ShapeDtypeStruct((D_MODEL, IN_COLS), BF16)] + rest_shape
    out_shape += [jax.ShapeDtypeStruct(a.shape, BF16) for a in casts]
    any_spec = pl.BlockSpec(memory_space=pl.ANY)

    def z_map(s, i):
        return (i, jnp.bitwise_xor(2 * lax.axis_index("x") + lax.axis_index("y"), s))

    return pl.pallas_call(
        body, name="in_proj", out_shape=tuple(out_shape),
        grid=(N_CHIPS, nb),
        in_specs=[pl.BlockSpec((tb, D_MODEL), lambda s, i: (jnp.where(s == 0, i, nb - 1), 0)),
                  pl.BlockSpec((1, D_MODEL), lambda s, i: (0, 0)), any_spec] + [any_spec] * (n + nc),
        out_specs=tuple([pl.BlockSpec((tb, cols), z_map), any_spec, any_spec] + [any_spec] * (n + nc)),
        scratch_shapes=[pltpu.VMEM((N_CHIPS, D_MODEL, cols), BF16), pltpu.VMEM((nb, tb, D_MODEL), BF16)] + rest_sems + [
            pltpu.SemaphoreType.DMA((GATHER_SEMS,)), pltpu.SemaphoreType.DMA((GATHER_SEMS,)),
            pltpu.SemaphoreType.DMA((N_CHIPS + 3,)), pltpu.VMEM((2, half, cols), F32)]
        + [pltpu.VMEM(a.shape, F32) for a in casts] + [pltpu.VMEM(a.shape, BF16) for a in casts]
        + [pltpu.SemaphoreType.DMA((2 * nc,))],
        compiler_params=pltpu.CompilerParams(dimension_semantics=("arbitrary", "arbitrary"),
                                             vmem_limit_bytes=VMEM_LIMIT_BYTES),
    )(x2d, norm_g, w_in_sh, *[sh[0] for sh in shards], *casts)


def _in_proj_bwd(dz, w_in, x2d, dx_res, norm_g, tb, reduce, shards, take):
    t = x2d.shape[0]
    nb = t // tb
    parts, wire, steps = reduce
    n = len(parts)
    k = len(shards)
    take_rows, take_width = take

    def body(dz_ref, w_ref, x_ref, dres_ref, g_ref, *refs):
        at = 2 * n + k
        dx_ref, dg_ref = refs[at:at + 2]
        rs_outs, g_outs = refs[at + 2:at + 2 + n], refs[at + 2 + n:at + 2 + n + k]
        cut_ref = refs[at + 2 + n + k]
        scratch = refs[at + 3 + n + k:]
        rs_scr, g_sems, dg_acc, ar_scr, cut_sem = scratch[:-8], scratch[-8:-5], scratch[-5], scratch[-4:-1], scratch[-1]
        rs = _rs_steps(parts, refs[:2 * n], rs_outs, rs_scr)
        for step, when in zip(rs[:-1], steps):
            pl.when(pl.program_id(0) == when)(step)
        gather = _gather_steps(shards, refs[2 * n:at], g_outs, *g_sems)
        for step, when in zip(gather, (0, nb // 2, nb - 1)):
            pl.when(pl.program_id(0) == when)(step)

        @pl.when(pl.program_id(0) == 0)
        def _():
            dg_acc[...] = jnp.zeros_like(dg_acc)

        xn, r = _rms(x_ref[...])
        g = g_ref[...]
        dh = _dot_nt(dz_ref[...], w_ref[...])
        dg_acc[0:1, :] += jnp.sum(dh * xn, axis=0, keepdims=True)
        dx_ref[...] = dres_ref[...] + _rms_bwd(dh * g, xn, r)

        @pl.when(pl.program_id(0) == nb - 1)
        def _():
            x, y, _ = _place()
            mine = pl.ds(pl.multiple_of((2 * x + y) * take_width, take_width), take_width)
            cut = pltpu.make_async_copy(g_outs[0].at[take_rows, mine], cut_ref, cut_sem)
            cut.start()
            _all_reduce_tile(dg_acc, dg_ref, *ar_scr)
            rs[-1]()
            cut.wait()

    row = lambda i: (i, 0)
    fixed = lambda i: (0, 0)
    rs_shape, rs_scratch = _rs_shapes(parts, wire)
    g_shape, g_sems = _gather_shapes(shards)
    any_spec = pl.BlockSpec(memory_space=pl.ANY)
    cut_shape = jax.ShapeDtypeStruct((take_rows.stop - take_rows.start, take_width), F32)
    return pl.pallas_call(
        body, name="in_proj_bwd",
        out_shape=tuple([jax.ShapeDtypeStruct((t, D_MODEL), F32), jax.ShapeDtypeStruct((F32_SUBLANES, D_MODEL), F32)]
                        + rs_shape + g_shape + [cut_shape]),
        grid=(nb,),
        in_specs=[pl.BlockSpec((tb, IN_COLS), row),
                  pl.BlockSpec((D_MODEL, IN_COLS), fixed, pipeline_mode=pl.Buffered(1)),
                  pl.BlockSpec((tb, D_MODEL), row), pl.BlockSpec((tb, D_MODEL), row),
                  pl.BlockSpec((1, D_MODEL), fixed)] + [any_spec] * (2 * n + k),
        out_specs=tuple([pl.BlockSpec((tb, D_MODEL), row), pl.BlockSpec((F32_SUBLANES, D_MODEL), fixed)]
                        + [any_spec] * (n + k + 1)),
        scratch_shapes=rs_scratch + g_sems + [pltpu.VMEM((F32_SUBLANES, D_MODEL), F32)] + _all_reduce_scratch(
            (F32_SUBLANES, D_MODEL)) + [pltpu.SemaphoreType.DMA(())],
        compiler_params=pltpu.CompilerParams(dimension_semantics=("arbitrary",),
                                             vmem_limit_bytes=VMEM_LIMIT_BYTES),
    )(dz, w_in, x2d, dx_res, norm_g, *_rs_operands(parts), *[sh[0] for sh in shards])


def _weight_grad(pairs, n_chunks, tb, name, reduce=None):
    t = pairs[0][0].shape[0]
    nb = t // tb
    m = len(pairs)
    parts, wire, steps = reduce if reduce is not None else ([], F32, ())
    n = len(parts)

    def body(*refs):
        lr, refs = refs[:2 * m], refs[2 * m:]
        o_refs = refs[2 * n:2 * n + 2 * m]
        if n:
            at = pl.program_id(0) * nb + pl.program_id(1)
            rs = _rs_steps(parts, refs[:2 * n], refs[2 * n + 2 * m:3 * n + 2 * m], refs[3 * n + 2 * m:])
            for step, when in zip(rs, steps):
                pl.when(at == when)(step)

        @pl.when(pl.program_id(1) == 0)
        def _():
            for q in range(m):
                o_refs[2 * q][...] = jnp.zeros_like(o_refs[2 * q])

        for q in range(m):
            o_refs[2 * q][...] += _dot_tn(lr[2 * q][...], lr[2 * q + 1][...])

        @pl.when(pl.program_id(1) == nb - 1)
        def _():
            for q in range(m):
                o_refs[2 * q + 1][...] = o_refs[2 * q][...].astype(BF16)

    rs_shape, rs_scratch = _rs_shapes(parts, wire) if n else ([], [])
    any_spec = pl.BlockSpec(memory_space=pl.ANY)
    in_specs, out_specs, out_shape = [], [], []
    for lhs, rhs in pairs:
        k, nc = lhs.shape[1], rhs.shape[1] // n_chunks
        in_specs += [pl.BlockSpec((tb, k), lambda j, i: (i, 0)), pl.BlockSpec((tb, nc), lambda j, i: (i, j))]
        out_specs += [pl.BlockSpec((None, k, nc), lambda j, i: (j, 0, 0))] * 2
        out_shape += [jax.ShapeDtypeStruct((n_chunks, k, nc), F32), jax.ShapeDtypeStruct((n_chunks, k, nc), BF16)]
    return pl.pallas_call(
        body, name=name, out_shape=tuple(out_shape + rs_shape),
        grid=(n_chunks, nb),
        in_specs=in_specs + [any_spec] * (2 * n),
        out_specs=tuple(out_specs + [any_spec] * n),
        scratch_shapes=rs_scratch,
        compiler_params=pltpu.CompilerParams(dimension_semantics=("arbitrary", "arbitrary"),
                                             vmem_limit_bytes=VMEM_LIMIT_BYTES),
    )(*[a for pair in pairs for a in pair], *_rs_operands(parts))


def _adam_update(w, g, m, v):
    m_ = ADAM_B1 * m + (1.0 - ADAM_B1) * g
    v_ = ADAM_B2 * v + (1.0 - ADAM_B2) * jnp.square(g)
    m_hat = m_ / (1.0 - ADAM_B1 ** ADAM_STEP)
    v_hat = v_ / (1.0 - ADAM_B2 ** ADAM_STEP)
    return -ADAM_LR * (m_hat / (jnp.sqrt(v_hat) + ADAM_EPS) + ADAM_WD * w), m_, v_


def _adamw_replicated(vec_sum, mat_sum, norm_grad, entries, conv):
    n = len(entries)

    def grad_of(name, shape, vec_ref, mat_ref, norm_ref):
        if name == "norm_g":
            return norm_ref[0:1, :]
        if name in MAT_BAG_AT:
            return mat_ref[MAT_BAG_AT[name]:MAT_BAG_AT[name] + shape[0], :]
        if shape[0] == 1:
            return vec_ref[_bag_row(name), 0:shape[1]]
        return jnp.concatenate([vec_ref[_bag_row(name), h * shape[1]:(h + 1) * shape[1]] for h in range(shape[0])],
                               axis=0)

    def body(vec_ref, mat_ref, norm_ref, *refs):
        ins, outs = refs[:3 * n + 4], refs[3 * n + 4:]
        for k in range(n):
            w_ref, m_ref, v_ref = ins[3 * k:3 * k + 3]
            g = grad_of(entries[k][0], w_ref.shape, vec_ref, mat_ref, norm_ref)
            d, m_, v_ = _adam_update(w_ref[...], g, m_ref[...], v_ref[...])
            for ref, val in zip(outs[4 * k:4 * k + 4], (g, d, m_, v_)):
                ref[...] = val
        w_ref, m_ref, v_ref, g_ref = ins[3 * n:]
        g = g_ref[0:w_ref.shape[0], :]
        for ref, val in zip(outs[4 * n:4 * n + 4], (g,) + _adam_update(w_ref[...], g, m_ref[...], v_ref[...])):
            ref[...] = val
        outs[4 * n + 4][...] = vec_ref[_bag_row("loss"), 0:1]

    arrays = [a for e in entries for a in e[1:]] + list(conv)
    out_shape = [jax.ShapeDtypeStruct(e[1].shape, F32) for e in entries for _ in range(4)]
    out_shape += [jax.ShapeDtypeStruct(conv[0].shape, F32)] * 4 + [jax.ShapeDtypeStruct((1, 1), F32)]
    return pl.pallas_call(
        body, name="adamw_replicated", out_shape=tuple(out_shape),
        compiler_params=pltpu.CompilerParams(vmem_limit_bytes=VMEM_LIMIT_BYTES),
    )(vec_sum, mat_sum, norm_grad, *arrays)


def _adamw_group(items, name):
    arrays = [a for item in items for a in item[:4]]
    n = len(arrays)
    blocks = []
    for k, item in enumerate(items):
        rows = item[0].shape[0] // item[4]
        blocks += [(k, slice(q * rows, (q + 1) * rows)) for q in range(item[4])]

    def body(*refs):
        ins, outs, bufs = refs[:n], refs[n:2 * n], refs[2 * n:3 * n]
        load_sems, store_sems = refs[3 * n:]

        def copies(src, dst, sems):
            return [[pltpu.make_async_copy(src[4 * k + j].at[rows], dst[4 * k + j].at[rows], sems.at[4 * b + j])
                     for j in range(4)] for b, (k, rows) in enumerate(blocks)]

        loads, stores = copies(ins, bufs, load_sems), copies(bufs, outs, store_sems)
        for cp in [cp for block in loads for cp in block]:
            cp.start()
        for b, (k, rows) in enumerate(blocks):
            for cp in loads[b]:
                cp.wait()
            w_buf, g_buf, m_buf, v_buf = bufs[4 * k:4 * k + 4]
            w_buf[rows, :], m_buf[rows, :], v_buf[rows, :] = _adam_update(
                w_buf[rows, :], g_buf[rows, :], m_buf[rows, :], v_buf[rows, :])
            for cp in stores[b]:
                cp.start()
        for cp in [cp for block in stores for cp in block]:
            cp.wait()

    any_spec = pl.BlockSpec(memory_space=pl.ANY)
    flat = pl.pallas_call(
        body, name=name, out_shape=tuple(jax.ShapeDtypeStruct(a.shape, F32) for a in arrays),
        in_specs=[any_spec] * n, out_specs=(any_spec,) * n,
        scratch_shapes=[pltpu.VMEM(a.shape, F32) for a in arrays] + [pltpu.SemaphoreType.DMA((4 * len(blocks),))] * 2,
        compiler_params=pltpu.CompilerParams(vmem_limit_bytes=VMEM_LIMIT_BYTES),
    )(*arrays)
    return [(flat[4 * k + 1], flat[4 * k], flat[4 * k + 2], flat[4 * k + 3]) for k in range(len(items))]


def _shift_down(ext, s):
    return pltpu.roll(ext, s, 0)


def _tile_shift(v, s):
    rows, cols = v.shape
    tiles = v.reshape(rows // F32_SUBLANES, F32_SUBLANES, cols)
    return pltpu.roll(tiles, s % F32_SUBLANES, 1).reshape(rows, cols)


def _shift_up(ext, s):
    return pltpu.roll(ext, ext.shape[0] - s, 0)


def _lru_gates(xc, wa_ref, ba, wx_ref, bx, lam):
    pa, px = [], []
    for h in range(LRU_HEADS):
        xh = xc[:, h * HEAD_DIM:(h + 1) * HEAD_DIM].astype(BF16)
        pa.append(_dot(xh, wa_ref[h]))
        px.append(_dot(xh, wx_ref[h]))
    r = _sigmoid(jnp.concatenate(pa, axis=1) + ba)
    ig = _sigmoid(jnp.concatenate(px, axis=1) + bx)
    sp = _softplus(-lam)
    log_a = (-LRU_C * r) * sp
    a = jnp.exp(log_a)
    mult = jnp.sqrt(jnp.tanh(-log_a) * (1.0 + a * a))
    return r, ig, a, mult, sp


def _conv(ext, w_ref, b):
    y = b + _shift_down(ext, 3) * w_ref[0:1, :]
    y = y + _shift_down(ext, 2) * w_ref[1:2, :]
    y = y + _shift_down(ext, 1) * w_ref[2:3, :]
    y = y + ext * w_ref[3:4, :]
    return y[CONV_HIST:, :]


def _pool_diff(ext, pos):
    out = []
    for g, k in enumerate(POOL_WINDOWS):
        col = ext[:, g * POOL_GROUP_DIM:(g + 1) * POOL_GROUP_DIM]
        s = col
        for step in range(g + 1):
            s = s + _shift_down(s, 2 ** step)
        count = jnp.minimum(pos + 1, k).astype(F32)
        out.append(s[POOL_HIST:, :] / count - col[POOL_HIST:, :])
    return out


def _pool_mix(diff, pw_ref):
    return jnp.concatenate([_dot(diff[g].astype(BF16), pw_ref[g]) for g in range(len(POOL_WINDOWS))], axis=1)


def _branch_specs(tb, row_map, fixed):
    fixed3 = lambda i: (0, 0, 0)
    return [pl.BlockSpec((CONV_WIDTH, D_MODEL), fixed), pl.BlockSpec((1, D_MODEL), fixed),
            pl.BlockSpec((LRU_HEADS, HEAD_DIM, HEAD_DIM), fixed3), pl.BlockSpec((1, D_MODEL), fixed),
            pl.BlockSpec((LRU_HEADS, HEAD_DIM, HEAD_DIM), fixed3), pl.BlockSpec((1, D_MODEL), fixed),
            pl.BlockSpec((1, D_MODEL), fixed),
            pl.BlockSpec((len(POOL_WINDOWS), POOL_GROUP_DIM, POOL_GROUP_DIM), fixed3),
            pl.BlockSpec((1, POOL_WIDTH), fixed)]


def _branches_fwd(z, weights, seq, tb, shards):
    t = z.shape[0]
    nb = t // tb
    nbe = seq // tb
    groups = tb // F32_SUBLANES
    n = len(shards)

    def body(xa_ref, ga_ref, xb_ref, gb_ref, cw_ref, cb_ref, wa_ref, ba_ref, wx_ref, bx_ref, lam_ref,
             pw_ref, ps_ref, *refs):
        g_ins = refs[:n]
        ya_ref, yb_ref, hl_ref = refs[n:n + 3]
        g_outs = refs[n + 3:2 * n + 3]
        xa_ext, xb_ext, carry, a_s, u_s, send_sems, recv_sems, local_sems = refs[2 * n + 3:]
        blk = pl.program_id(0) % nbe
        start_gather, relay_gather, finish_gather = _gather_steps(shards, g_ins, g_outs, send_sems, recv_sems,
                                                                  local_sems)
        pl.when(pl.program_id(0) == 0)(start_gather)
        pl.when(pl.program_id(0) == nb // 2)(relay_gather)

        @pl.when(blk == 0)
        def _():
            xa_ext[0:CONV_HIST, :] = jnp.zeros((CONV_HIST, D_MODEL), F32)
            xb_ext[0:POOL_HIST, :] = jnp.zeros((POOL_HIST, POOL_WIDTH), F32)
            carry[...] = jnp.zeros_like(carry)

        xa_ext[CONV_HIST:, :] = xa_ref[...]
        xb_ext[POOL_HIST:, :] = xb_ref[...]
        ea = xa_ext[...]
        eb = xb_ext[...]
        xa_ext[0:CONV_HIST, :] = ea[tb:, :]
        xb_ext[0:POOL_HIST, :] = eb[tb:, :]

        xc = _conv(ea, cw_ref, cb_ref[...])
        _, ig, a, mult, _ = _lru_gates(xc, wa_ref, ba_ref[...], wx_ref, bx_ref[...], lam_ref[...])
        u = mult * (ig * xc)
        row8 = lax.broadcasted_iota(jnp.int32, (tb, D_MODEL), 0) % F32_SUBLANES
        for s in (1, 2, 4):
            m = row8 >= s
            u = jnp.where(m, a * _tile_shift(u, s) + u, u)
            a = jnp.where(m, a * _tile_shift(a, s), a)
        a_s[...] = a
        u_s[...] = u

        def step(g, cr):
            sl = pl.ds(pl.multiple_of(g * F32_SUBLANES, F32_SUBLANES), F32_SUBLANES)
            hb = a_s[sl, :] * cr + u_s[sl, :]
            hl_ref[sl, :] = hb
            return jnp.broadcast_to(hb[F32_SUBLANES - 1:F32_SUBLANES, :], (F32_SUBLANES, D_MODEL))

        carry[...] = lax.fori_loop(0, groups, step, carry[...], unroll=True)
        ga = ga_ref[...]
        ya_ref[...] = (hl_ref[...] * (ga * _sigmoid(ga))).astype(BF16)

        pos = blk * tb + lax.broadcasted_iota(jnp.int32, (tb, POOL_GROUP_DIM), 0)
        ypre = _pool_mix(_pool_diff(eb, pos), pw_ref)
        gb = gb_ref[...]
        yb_ref[...] = ((ypre * ps_ref[...]) * (gb * _sigmoid(gb))).astype(BF16)
        pl.when(pl.program_id(0) == nb - 1)(finish_gather)

    row = lambda i: (i, 0)
    fixed = lambda i: (0, 0)
    any_spec = pl.BlockSpec(memory_space=pl.ANY)
    in_specs = [pl.BlockSpec((tb, D_MODEL), lambda i: (i, 0)), pl.BlockSpec((tb, D_MODEL), lambda i: (i, 1)),
                pl.BlockSpec((tb, POOL_WIDTH), lambda i: (i, 4)), pl.BlockSpec((tb, POOL_WIDTH), lambda i: (i, 5)),
                ] + _branch_specs(tb, row, fixed) + [any_spec] * n
    g_shape, g_sems = _gather_shapes(shards)
    return pl.pallas_call(
        body, name="branches_fwd",
        out_shape=tuple([jax.ShapeDtypeStruct((t, D_MODEL), BF16), jax.ShapeDtypeStruct((t, POOL_WIDTH), BF16),
                         jax.ShapeDtypeStruct((t, D_MODEL), F32)] + g_shape),
        grid=(nb,), in_specs=in_specs,
        out_specs=tuple([pl.BlockSpec((tb, D_MODEL), row), pl.BlockSpec((tb, POOL_WIDTH), row),
                         pl.BlockSpec((tb, D_MODEL), row)] + [any_spec] * n),
        scratch_shapes=[pltpu.VMEM((tb + CONV_HIST, D_MODEL), F32), pltpu.VMEM((tb + POOL_HIST, POOL_WIDTH), F32),
                        pltpu.VMEM((F32_SUBLANES, D_MODEL), F32),
                        pltpu.VMEM((tb, D_MODEL), F32), pltpu.VMEM((tb, D_MODEL), F32)] + g_sems,
        compiler_params=pltpu.CompilerParams(dimension_semantics=("arbitrary",),
                                             vmem_limit_bytes=VMEM_LIMIT_BYTES),
    )(z, z, z, z, *weights, *[sh[0] for sh in shards])


def _branches_bwd(z, hl, dya, dyb, dzm, weights, vec_bag, seq, tb, riders):
    t = z.shape[0]
    nb = t // tb
    nbe = seq // tb
    groups = tb // F32_SUBLANES
    nr = len(riders)

    def body(xa_ref, xap_ref, ga_ref, xb_ref, xbp_ref, gb_ref, hl_ref, hlp_ref, dya_ref, dyb_ref, dzm_ref,
             cw_ref, cb_ref, wa_ref, ba_ref, wx_ref, bx_ref, lam_ref, pw_ref, ps_ref, vec_in_ref, *rest):
        pairs, (dz_ref, vec_ref, mat_ref), grads = rest[:2 * nr], rest[2 * nr:2 * nr + 3], rest[2 * nr + 3:4 * nr + 3]
        xa_ext, xb_ext, hl_ext, a_ext, dxc_ext, dwin_ext, g_carry, b_s, d_s, g_s = rest[4 * nr + 3:]
        i = pl.program_id(0)
        blk = (nb - 1 - i) % nbe

        def mat_rows(name, k):
            at = MAT_BAG_AT[name] + k * HEAD_DIM
            return slice(at, at + HEAD_DIM)

        def rider(k):
            grads[2 * k][...] += _dot_tn(pairs[2 * k][...], pairs[2 * k + 1][...])

        @pl.when(i == 0)
        def _():
            vec_ref[...] = vec_in_ref[...]
            mat_ref[...] = jnp.zeros_like(mat_ref)
            for k in range(nr):
                grads[2 * k][...] = jnp.zeros_like(grads[2 * k])

        @pl.when(blk == nbe - 1)
        def _():
            a_ext[tb:, :] = jnp.zeros((F32_SUBLANES, D_MODEL), F32)
            dxc_ext[tb:, :] = jnp.zeros((CONV_HIST, D_MODEL), F32)
            dwin_ext[tb:, :] = jnp.zeros((POOL_HIST, POOL_WIDTH), F32)
            g_carry[...] = jnp.zeros_like(g_carry)

        live = (blk > 0).astype(F32)
        xa_ext[0:CONV_HIST, :] = xap_ref[...] * live
        xa_ext[CONV_HIST:, :] = xa_ref[...]
        xb_ext[0:POOL_HIST, :] = xbp_ref[...] * live
        xb_ext[POOL_HIST:, :] = xb_ref[...]
        hl_ext[0:F32_SUBLANES, :] = hlp_ref[...] * live
        hl_ext[F32_SUBLANES:, :] = hl_ref[...]
        ea = xa_ext[...]
        eb = xb_ext[...]
        rider(0)

        xc = _conv(ea, cw_ref, cb_ref[...])
        lam = lam_ref[...]
        r, ig, a, mult, sp = _lru_gates(xc, wa_ref, ba_ref[...], wx_ref, bx_ref[...], lam)
        hl = hl_ref[...]
        ga = ga_ref[...]
        sga = _sigmoid(ga)
        dya = dya_ref[...]
        dhl = dya * (ga * sga)
        dz_ref[:, D_MODEL:2 * D_MODEL] = (dya * hl * (sga * (1.0 + ga * (1.0 - sga)))).astype(BF16)

        a_ext[0:tb, :] = a
        b = _shift_up(a_ext[...], 1)[0:tb, :]
        a_ext[tb:, :] = jnp.broadcast_to(a[0:1, :], (F32_SUBLANES, D_MODEL))
        d = dhl
        row8 = lax.broadcasted_iota(jnp.int32, (tb, D_MODEL), 0) % F32_SUBLANES
        for s in (1, 2, 4):
            m = row8 < F32_SUBLANES - s
            d = jnp.where(m, d + b * _tile_shift(d, -s), d)
            b = jnp.where(m, b * _tile_shift(b, -s), b)
        b_s[...] = b
        d_s[...] = d

        def step(k, cr):
            sl = pl.ds(pl.multiple_of((groups - 1 - k) * F32_SUBLANES, F32_SUBLANES), F32_SUBLANES)
            gb_ = d_s[sl, :] + b_s[sl, :] * cr
            g_s[sl, :] = gb_
            return jnp.broadcast_to(gb_[0:1, :], (F32_SUBLANES, D_MODEL))

        g_carry[...] = lax.fori_loop(0, groups, step, g_carry[...], unroll=4)
        rider(1)
        gsc = g_s[...]
        da = gsc * _shift_down(hl_ext[...], 1)[F32_SUBLANES:, :]
        dmult = gsc * (ig * xc)
        dig = gsc * (mult * xc)
        dxc = gsc * (mult * ig)
        dlog_a = da * a - (a * a) * dmult / mult
        dr = dlog_a * (-LRU_C * sp)
        vec_ref[_bag_row("lru_lambda"), :] += jnp.sum(dlog_a * (-LRU_C * r), axis=0, keepdims=True)
        dpa = dr * (r * (1.0 - r))
        dpx = dig * (ig * (1.0 - ig))
        vec_ref[_bag_row("lru_b_a"), :] += jnp.sum(dpa, axis=0, keepdims=True)
        vec_ref[_bag_row("lru_b_x"), :] += jnp.sum(dpx, axis=0, keepdims=True)
        back = []
        for h in range(LRU_HEADS):
            cols = slice(h * HEAD_DIM, (h + 1) * HEAD_DIM)
            xh = xc[:, cols].astype(BF16)
            dpa_h = dpa[:, cols].astype(BF16)
            dpx_h = dpx[:, cols].astype(BF16)
            mat_ref[mat_rows("lru_w_a", h), :] += _dot_tn(xh, dpa_h)
            mat_ref[mat_rows("lru_w_x", h), :] += _dot_tn(xh, dpx_h)
            back.append(_dot_nt(dpa_h, wa_ref[h]) + _dot_nt(dpx_h, wx_ref[h]))
        dxc = dxc + jnp.concatenate(back, axis=1)
        vec_ref[_bag_row("conv_b"), :] += jnp.sum(dxc, axis=0, keepdims=True)
        for k in range(CONV_WIDTH):
            tap = _shift_down(ea, CONV_WIDTH - 1 - k)[CONV_HIST:, :] if k < CONV_WIDTH - 1 else ea[CONV_HIST:, :]
            vec_ref[_bag_row("conv_w", k), :] += jnp.sum(dxc * tap, axis=0, keepdims=True)
        dxc_ext[0:tb, :] = dxc
        ed = dxc_ext[...]
        dxa = ed * cw_ref[3:4, :]
        dxa = dxa + _shift_up(ed, 1) * cw_ref[2:3, :]
        dxa = dxa + _shift_up(ed, 2) * cw_ref[1:2, :]
        dxa = dxa + _shift_up(ed, 3) * cw_ref[0:1, :]
        dz_ref[:, 0:D_MODEL] = dxa[0:tb, :].astype(BF16)
        dxc_ext[tb:, :] = dxc[0:CONV_HIST, :]

        pos = blk * tb + lax.broadcasted_iota(jnp.int32, (tb, POOL_GROUP_DIM), 0)
        diff = _pool_diff(eb, pos)
        rider(2)
        ypre = _pool_mix(diff, pw_ref)
        ps = ps_ref[...]
        gb = gb_ref[...]
        sgb = _sigmoid(gb)
        dyb = dyb_ref[...]
        dyp = dyb * (gb * sgb)
        dz_ref[:, 2 * D_MODEL + POOL_WIDTH:3 * D_MODEL] = (
            dyb * (ypre * ps) * (sgb * (1.0 + gb * (1.0 - sgb)))).astype(BF16)
        vec_ref[_bag_row("pool_scale"), 0:POOL_WIDTH] += jnp.sum(dyp * ypre, axis=0, keepdims=True)
        dypre = dyp * ps
        for g, k in enumerate(POOL_WINDOWS):
            cols = slice(g * POOL_GROUP_DIM, (g + 1) * POOL_GROUP_DIM)
            dyg = dypre[:, cols].astype(BF16)
            mat_ref[mat_rows("pool_w", g), :] += _dot_tn(diff[g].astype(BF16), dyg)
            ddiff = _dot_nt(dyg, pw_ref[g])
            count = jnp.minimum(pos + 1, k).astype(F32)
            dwin = ddiff / count
            dwin_ext[0:tb, cols] = dwin
            s = dwin_ext[:, cols]
            for step_ in range(g + 1):
                s = s + _shift_up(s, 2 ** step_)
            dz_ref[:, 2 * D_MODEL + g * POOL_GROUP_DIM:2 * D_MODEL + (g + 1) * POOL_GROUP_DIM] = (
                s[0:tb, :] - ddiff).astype(BF16)
            dwin_ext[tb:, cols] = dwin[0:POOL_HIST, :]

        dz_ref[:, 3 * D_MODEL:] = dzm_ref[...]

        @pl.when(i == nb - 1)
        def _():
            row = _bag_row("lru_lambda")
            vec_ref[row, :] = vec_ref[row, :] * (-_sigmoid(-lam))
            for k in range(nr):
                grads[2 * k + 1][...] = grads[2 * k][...].astype(BF16)

    rev = lambda i: (nb - 1 - i, 0)
    fixed = lambda i: (0, 0)

    def prev(rows, col):
        per = tb // rows
        return lambda i: (jnp.maximum((nb - 1 - i) * per - 1, 0), col)

    in_specs = [pl.BlockSpec((tb, D_MODEL), lambda i: (nb - 1 - i, 0)),
                pl.BlockSpec((CONV_HIST, D_MODEL), prev(CONV_HIST, 0)),
                pl.BlockSpec((tb, D_MODEL), lambda i: (nb - 1 - i, 1)),
                pl.BlockSpec((tb, POOL_WIDTH), lambda i: (nb - 1 - i, 4)),
                pl.BlockSpec((POOL_HIST, POOL_WIDTH), prev(POOL_HIST, 4)),
                pl.BlockSpec((tb, POOL_WIDTH), lambda i: (nb - 1 - i, 5)),
                pl.BlockSpec((tb, D_MODEL), rev),
                pl.BlockSpec((F32_SUBLANES, D_MODEL), prev(F32_SUBLANES, 0)),
                pl.BlockSpec((tb, D_MODEL), rev), pl.BlockSpec((tb, POOL_WIDTH), rev),
                pl.BlockSpec((tb, 2 * D_MODEL), rev)] + _branch_specs(tb, rev, fixed) + [
                    pl.BlockSpec((VEC_BAG_ROWS, D_MODEL), fixed)]
    vec_at = len(in_specs) - 1
    out_shape = [jax.ShapeDtypeStruct((t, IN_COLS), BF16), jax.ShapeDtypeStruct((VEC_BAG_ROWS, D_MODEL), F32),
                 jax.ShapeDtypeStruct((MAT_BAG_ROWS, HEAD_DIM), F32)]
    out_specs = [pl.BlockSpec((tb, IN_COLS), rev), pl.BlockSpec((VEC_BAG_ROWS, D_MODEL), fixed),
                 pl.BlockSpec((MAT_BAG_ROWS, HEAD_DIM), fixed)]
    for lhs, rhs in riders:
        in_specs += [pl.BlockSpec((tb, lhs.shape[1]), rev), pl.BlockSpec((tb, rhs.shape[1]), rev)]
        grad = (lhs.shape[1], rhs.shape[1])
        out_shape += [jax.ShapeDtypeStruct(grad, F32), jax.ShapeDtypeStruct(grad, BF16)]
        out_specs += [pl.BlockSpec(grad, fixed)] * 2
    scratch = [pltpu.VMEM((tb + CONV_HIST, D_MODEL), F32), pltpu.VMEM((tb + POOL_HIST, POOL_WIDTH), F32),
               pltpu.VMEM((tb + F32_SUBLANES, D_MODEL), F32), pltpu.VMEM((tb + F32_SUBLANES, D_MODEL), F32),
               pltpu.VMEM((tb + CONV_HIST, D_MODEL), F32), pltpu.VMEM((tb + POOL_HIST, POOL_WIDTH), F32),
               pltpu.VMEM((F32_SUBLANES, D_MODEL), F32),
               pltpu.VMEM((tb, D_MODEL), F32), pltpu.VMEM((tb, D_MODEL), F32), pltpu.VMEM((tb, D_MODEL), F32)]
    return pl.pallas_call(
        body, name="branches_bwd", out_shape=tuple(out_shape), grid=(nb,), in_specs=in_specs,
        out_specs=tuple(out_specs), scratch_shapes=scratch, input_output_aliases={vec_at: 1},
        compiler_params=pltpu.CompilerParams(dimension_semantics=("arbitrary",),
                                             vmem_limit_bytes=VMEM_LIMIT_BYTES),
    )(z, z, z, z, z, z, hl, hl, dya, dyb, dzm, *weights, vec_bag, *[a for pair in riders for a in pair])


def _merge_head(x2d, ya, yb, z, p2d, tgt, w_pl, w_pp, w_out, w_pg, w_pe, g2, gf, tb):
    t = x2d.shape[0]
    p_dim = p2d.shape[1]

    def body(x_ref, ya_ref, yb_ref, ma_ref, mb_ref, p_ref, t_ref, wpl_ref, wpp_ref, wout_ref, wpg_ref, wpe_ref,
             g2_ref, gf_ref,
             bag_ref, dxr_ref, dya_ref, dyb_ref, dzm_ref,
             mg_ref, do_ref, hn_ref, dgp_ref, dpe_ref, da_ref, dbm_ref, pbf_ref):
        @pl.when(pl.program_id(0) == 0)
        def _():
            bag_ref[...] = jnp.zeros_like(bag_ref)

        a_ = _dot(ya_ref[...], wpl_ref[...])
        bm = _dot(yb_ref[...], wpp_ref[...])
        sa = _sigmoid(ma_ref[...])
        sb = _sigmoid(mb_ref[...])
        mg = (sa * a_ + sb * bm).astype(BF16)
        mg_ref[...] = mg
        x1 = x_ref[...] + _dot(mg, wout_ref[...])
        xn2, r2 = _rms(x1)
        g2 = g2_ref[...]
        hn = (xn2 * g2).astype(BF16)
        hn_ref[...] = hn
        gate = _sigmoid(_dot(hn, wpg_ref[...]))
        pbf = p_ref[...].astype(BF16)
        pbf_ref[...] = pbf
        pe = _dot(pbf, wpe_ref[...])
        x2 = x1 + gate * pe
        xn3, r3 = _rms(x2)
        gf = gf_ref[...]
        err = xn3 * gf - t_ref[...]
        bag_ref[_bag_rows("loss"), 0:128] += 0.5 * jnp.sum(jnp.mean(err * err, axis=-1))

        dy = err * (1.0 / D_MODEL)
        bag_ref[_bag_row("final_g"), :] += jnp.sum(dy * xn3, axis=0, keepdims=True)
        dx2 = _rms_bwd(dy * gf, xn3, r3)
        dpe_ref[...] = (dx2 * gate).astype(BF16)
        dgp = ((dx2 * pe) * (gate * (1.0 - gate))).astype(BF16)
        dgp_ref[...] = dgp
        dhn = _dot_nt(dgp, wpg_ref[...])
        bag_ref[_bag_row("ple_norm_g"), :] += jnp.sum(dhn * xn2, axis=0, keepdims=True)
        dx1 = dx2 + _rms_bwd(dhn * g2, xn2, r2)
        dxr_ref[...] = dx1
        do = dx1.astype(BF16)
        do_ref[...] = do
        dmg = _dot_nt(do, wout_ref[...])
        da = (dmg * sa).astype(BF16)
        dbm = (dmg * sb).astype(BF16)
        da_ref[...] = da
        dbm_ref[...] = dbm
        dzm_ref[:, 0:D_MODEL] = (dmg * a_ * (sa * (1.0 - sa))).astype(BF16)
        dzm_ref[:, D_MODEL:] = (dmg * bm * (sb * (1.0 - sb))).astype(BF16)
        dya_ref[...] = _dot_nt(da, wpl_ref[...])
        dyb_ref[...] = _dot_nt(dbm, wpp_ref[...])

    row = lambda i: (i, 0)
    fixed = lambda i: (0, 0)

    def resident(shape):
        return pl.BlockSpec(shape, fixed, pipeline_mode=pl.Buffered(1))

    tok = lambda width: pl.BlockSpec((tb, width), row)
    in_specs = [tok(D_MODEL), tok(D_MODEL), tok(POOL_WIDTH),
                pl.BlockSpec((tb, D_MODEL), lambda i: (i, 3)), pl.BlockSpec((tb, D_MODEL), lambda i: (i, 4)),
                tok(p_dim), tok(D_MODEL),
                resident((D_MODEL, D_MODEL)), resident((POOL_WIDTH, D_MODEL)), resident((D_MODEL, D_MODEL)),
                resident((D_MODEL, D_MODEL)), resident((p_dim, D_MODEL)),
                pl.BlockSpec((1, D_MODEL), fixed), pl.BlockSpec((1, D_MODEL), fixed)]
    bf = lambda width: jax.ShapeDtypeStruct((t, width), BF16)
    f32 = lambda width: jax.ShapeDtypeStruct((t, width), F32)
    out_shape = (jax.ShapeDtypeStruct((VEC_BAG_ROWS, D_MODEL), F32),
                 f32(D_MODEL), f32(D_MODEL), f32(POOL_WIDTH), bf(2 * D_MODEL),
                 bf(D_MODEL), bf(D_MODEL), bf(D_MODEL), bf(D_MODEL), bf(D_MODEL), bf(D_MODEL), bf(D_MODEL), bf(p_dim))
    out_specs = (pl.BlockSpec((VEC_BAG_ROWS, D_MODEL), fixed),
                 tok(D_MODEL), tok(D_MODEL), tok(POOL_WIDTH), tok(2 * D_MODEL),
                 tok(D_MODEL), tok(D_MODEL), tok(D_MODEL), tok(D_MODEL), tok(D_MODEL), tok(D_MODEL), tok(D_MODEL),
                 tok(p_dim))
    return pl.pallas_call(
        body, name="merge_head", out_shape=out_shape, grid=(t // tb,), in_specs=in_specs, out_specs=out_specs,
        compiler_params=pltpu.CompilerParams(dimension_semantics=("arbitrary",),
                                             vmem_limit_bytes=VMEM_LIMIT_BYTES),
    )(x2d, ya, yb, z, z, p2d, tgt, w_pl, w_pp, w_out, w_pg, w_pe, g2, gf)


def kernel(x, p, norm_g, w_in, conv_w, conv_b, lru_w_a, lru_b_a, lru_w_x, lru_b_x, lru_lambda, pool_w, pool_scale, w_proj_lru, w_proj_pool, w_out, ple_norm_g, w_ple_gate, w_ple_proj, final_g, loss_target, m_norm_g, m_w_in, m_conv_w, m_conv_b, m_lru_w_a, m_lru_b_a, m_lru_w_x, m_lru_b_x, m_lru_lambda, m_pool_w, m_pool_scale, m_w_proj_lru, m_w_proj_pool, m_w_out, m_ple_norm_g, m_w_ple_gate, m_w_ple_proj, m_final_g, v_norm_g, v_w_in, v_conv_w, v_conv_b, v_lru_w_a, v_lru_b_a, v_lru_w_x, v_lru_b_x, v_lru_lambda, v_pool_w, v_pool_scale, v_w_proj_lru, v_w_proj_pool, v_w_out, v_ple_norm_g, v_w_ple_gate, v_w_ple_proj, v_final_g):
    bsz, seq, _ = x.shape
    t = bsz * seq
    tb_mm = min(1024, seq)
    tb_seq = min(256, seq // 2) if seq >= 512 else seq
    x2d = x.reshape(t, D_MODEL)
    p2d = p.reshape(t, p.shape[-1])
    tgt = loss_target.reshape(t, D_MODEL)

    rest = [(w_proj_lru[0], 0), (w_proj_pool[0], 1), (w_out[0], 0), (w_ple_gate[0], 0), (w_ple_proj[0], 1)]
    z, h_bf, w_in_f, conv_w_f, *narrow = _in_proj_gather(
        x2d, norm_g, w_in[0], [(conv_w[0], 1, False)], tb_mm,
        [w for w, _ in rest] + [lru_w_a[0], lru_w_x[0], pool_w[0]])
    wa_bf, wx_bf, pw_bf = narrow[len(rest):]
    branch_w = (conv_w_f, conv_b, wa_bf, lru_b_a.reshape(1, D_MODEL), wx_bf, lru_b_x.reshape(1, D_MODEL),
                lru_lambda, pw_bf, pool_scale)

    ya, yb, hl, w_pl_f, w_pp_f, w_out_f, w_pg_f, w_pe_f = _branches_fwd(
        z, branch_w, seq, tb_seq, [(w16, axis, True) for w16, (_, axis) in zip(narrow, rest)])
    (vec_bag, dx_res, dya, dyb, dzm, mg_bf, do_bf, hn_bf, dgp_bf, dpe_bf, da_bf, dbm_bf, p_bf) = _merge_head(
        x2d, ya, yb, z, p2d, tgt, w_pl_f, w_pp_f, w_out_f, w_pg_f, w_pe_f, ple_norm_g, final_g.reshape(1, D_MODEL),
        tb_seq)
    dz, vec_bag, mat_bag, g_out, g_out16, g_pp, g_pp16, g_pe, g_pe16 = _branches_bwd(
        z, hl, dya, dyb, dzm, branch_w, vec_bag, seq, tb_seq, [(mg_bf, do_bf), (yb, dbm_bf), (p_bf, dpe_bf)])

    tb_dw = min(1024, seq)
    def row_pieces(g32, g16):
        pieces = (8, g32.shape[0] // 8, g32.shape[1])
        return g32.reshape(pieces), False, g16.reshape(pieces)

    g_pl, g_pl16, g_pg, g_pg16 = _weight_grad([(ya, da_bf), (hn_bf, dgp_bf)], 1, tb_dw, "dw_proj")
    p_dim = p2d.shape[1]
    proj_parts = [row_pieces(g_pl[0], g_pl16[0]), (g_pp, True, g_pp16), row_pieces(g_out, g_out16),
                  row_pieces(g_pg[0], g_pg16[0]), (g_pe, True, g_pe16)]
    nb_dw = t // tb_dw
    g_in, g_in16, r_pl, r_pp, r_out, r_pg, r_pe, vec_mine, mat_mine = _weight_grad(
        [(h_bf, dz)], N_CHIPS, tb_dw, "dw_in",
        reduce=(proj_parts + [(vec_bag.reshape(8, VEC_BAG_ROWS // 8, D_MODEL), False, None),
                              (mat_bag.reshape(8, MAT_BAG_ROWS // 8, HEAD_DIM), False, None)],
                [BF16] * 5 + [F32] * 2,
                (0, nb_dw // 2, 2 * nb_dw - 1, 3 * nb_dw + nb_dw // 2, N_CHIPS * nb_dw - 1)))
    pieces = (8, D_MODEL // 2, IN_COLS // N_CHIPS)
    nb_seq = t // tb_seq
    dx, g_g1, r_in, vec_sum, mat_sum, g_cw = _in_proj_bwd(
        dz, w_in_f, x2d, dx_res, norm_g, tb_seq,
        reduce=([(g_in.reshape(pieces), False, g_in16.reshape(pieces))], BF16,
                (0, nb_seq // 8, nb_seq // 2, nb_seq - 1)),
        shards=[(vec_mine.reshape(VEC_BAG_ROWS // N_CHIPS, D_MODEL), 0, True),
                (mat_mine.reshape(MAT_BAG_ROWS // N_CHIPS, HEAD_DIM), 0, True)],
        take=(_bag_rows("conv_w"), D_MODEL // N_CHIPS))

    big = [(w_in, r_in, m_w_in, v_w_in, 4), (w_proj_lru, r_pl, m_w_proj_lru, v_w_proj_lru, 1),
           (w_proj_pool, r_pp, m_w_proj_pool, v_w_proj_pool, 1), (w_out, r_out, m_w_out, v_w_out, 1),
           (w_ple_gate, r_pg, m_w_ple_gate, v_w_ple_gate, 1), (w_ple_proj, r_pe, m_w_ple_proj, v_w_ple_proj, 1)]
    u_in, u_pl, u_pp, u_out, u_pg, u_pe = [tuple(a[None] for a in u) for u in _adamw_group(
        [(w[0], g.reshape(w.shape[1:]), m[0], v[0], cuts) for w, g, m, v, cuts in big], "adamw_sharded")]

    small = [("norm_g", norm_g, m_norm_g, v_norm_g), ("conv_b", conv_b, m_conv_b, v_conv_b),
             ("lru_w_a", lru_w_a, m_lru_w_a, v_lru_w_a), ("lru_b_a", lru_b_a, m_lru_b_a, v_lru_b_a),
             ("lru_w_x", lru_w_x, m_lru_w_x, v_lru_w_x), ("lru_b_x", lru_b_x, m_lru_b_x, v_lru_b_x),
             ("lru_lambda", lru_lambda, m_lru_lambda, v_lru_lambda), ("pool_w", pool_w, m_pool_w, v_pool_w),
             ("pool_scale", pool_scale, m_pool_scale, v_pool_scale),
             ("ple_norm_g", ple_norm_g, m_ple_norm_g, v_ple_norm_g), ("final_g", final_g, m_final_g, v_final_g)]

    def view(a):
        return a.reshape(-1, a.shape[-1]) if a.ndim != 3 else a[0]

    flat = _adamw_replicated(vec_sum, mat_sum, g_g1, [(name,) + tuple(view(a) for a in arrs) for name, *arrs in small],
                             (conv_w[0], m_conv_w[0], v_conv_w[0], g_cw))
    u_small = {name: tuple(flat[4 * k + pick].reshape(arrs[0].shape) for pick in range(4))
               for k, (name, *arrs) in enumerate(small)}
    u_cw = tuple(a[None] for a in flat[4 * len(small):4 * len(small) + 4])

    loss = flat[-1].reshape(())
    grad_x = dx.reshape(bsz, seq, D_MODEL)

    def ordered(pick):
        s = {name: u[pick] for name, u in u_small.items()}
        return [s["norm_g"], u_in[pick], u_cw[pick], s["conv_b"], s["lru_w_a"], s["lru_b_a"], s["lru_w_x"], s["lru_b_x"],
                s["lru_lambda"], s["pool_w"], s["pool_scale"], u_pl[pick], u_pp[pick], u_out[pick], s["ple_norm_g"],
                u_pg[pick], u_pe[pick], s["final_g"]]

    return (loss, grad_x, *ordered(0), *ordered(1), *ordered(2), *ordered(3))
```

```python
import jax
import jax.numpy as jnp
from jax import lax
from jax.experimental import pallas as pl
from jax.experimental.pallas import tpu as pltpu

F32 = jnp.float32
BF16 = jnp.bfloat16
MESH = pl.DeviceIdType.MESH

D_MODEL = 1024
LRU_HEADS = 8
HEAD_DIM = 128
CONV_WIDTH = 4
LRU_C = 8.0
POOL_WIDTH = 512
POOL_WINDOWS = (2, 4, 8, 16)
POOL_GROUP_DIM = 128
IN_COLS = 5120
N_CHIPS = 4
EPS = 1e-6

ADAM_LR = 0.001
ADAM_B1 = 0.9
ADAM_B2 = 0.999
ADAM_EPS = 1e-08
ADAM_WD = 0.01
ADAM_STEP = 10

F32_SUBLANES = 8
CONV_HIST = 8
POOL_HIST = 16
VMEM_LIMIT_BYTES = 58 * 1024 * 1024
VEC_BAG_SLOTS = ("norm_g", "conv_w", "conv_b", "lru_b_a", "lru_b_x", "lru_lambda", "pool_scale", "ple_norm_g",
                 "final_g", "loss")
VEC_BAG_ROWS = 128
MAT_BAG_AT = {"lru_w_a": 0, "lru_w_x": LRU_HEADS * HEAD_DIM, "pool_w": 2 * LRU_HEADS * HEAD_DIM}
MAT_BAG_ROWS = 2 * LRU_HEADS * HEAD_DIM + len(POOL_WINDOWS) * POOL_GROUP_DIM


def _bag_row(name, k=0):
    at = F32_SUBLANES * VEC_BAG_SLOTS.index(name) + k
    return slice(at, at + 1)


def _bag_rows(name):
    at = F32_SUBLANES * VEC_BAG_SLOTS.index(name)
    return slice(at, at + F32_SUBLANES)


def _dot(a, b):
    return jnp.dot(a, b, preferred_element_type=F32)


def _dot_nt(a, b):
    return lax.dot_general(a, b, (((1,), (1,)), ((), ())), preferred_element_type=F32)


def _dot_tn(a, b):
    return lax.dot_general(a, b, (((0,), (0,)), ((), ())), preferred_element_type=F32)


def _sigmoid(v):
    return jax.nn.sigmoid(v)


def _softplus(v):
    return jnp.maximum(v, 0.0) + jnp.log1p(jnp.exp(-jnp.abs(v)))


def _place():
    return lax.axis_index("x"), lax.axis_index("y"), lax.axis_index("c")


GATHER_SEMS = 6
Z_RING = 3


def _gather_shapes(shards):
    out_shape = []
    for arr, axis, _ in shards:
        r, cols = arr.shape
        out_shape.append(jax.ShapeDtypeStruct((N_CHIPS * r, cols) if axis == 0 else (r, N_CHIPS * cols), arr.dtype))
    n = len(shards)
    sems = [pltpu.SemaphoreType.DMA((n * GATHER_SEMS,)), pltpu.SemaphoreType.DMA((n * GATHER_SEMS,)),
            pltpu.SemaphoreType.DMA((n,))]
    return out_shape, sems


def _gather_steps(shards, ins, outs, send_sems, recv_sems, local_sems):
    n = len(shards)
    x, y, c = _place()
    me, sibling = (x, y, c), (x, y, 1 - c)
    chips = [(x, 1 - y), (1 - x, y), (1 - x, 1 - y)]

    def region(k, cx, cy, hc):
        (r, cols), axis = shards[k][0].shape, shards[k][1]
        j = 2 * cx + cy
        if axis == 0:
            if hc is None:
                return outs[k].at[pl.ds(j * r, r), :]
            return outs[k].at[pl.ds(j * r + hc * (r // 2), r // 2), :]
        if hc is None:
            return outs[k].at[:, pl.ds(j * cols, cols)]
        return outs[k].at[pl.ds(hc * (r // 2), r // 2), pl.ds(j * cols, cols)]

    def remote(k, sem, block, to, src=None):
        dst = region(k, *block)
        return pltpu.make_async_remote_copy(
            src_ref=dst if src is None else src, dst_ref=dst,
            send_sem=send_sems.at[k * GATHER_SEMS + sem], recv_sem=recv_sems.at[k * GATHER_SEMS + sem],
            device_id=to, device_id_type=MESH)

    def first(k, idx):
        r, split = shards[k][0].shape[0], shards[k][2]
        src = ins[k].at[pl.ds(c * (r // 2), r // 2), :] if split else ins[k]
        return remote(k, idx, (x, y, c if split else None), (*chips[idx], c), src=src)

    def relay(k):
        src_chip = (jnp.bitwise_xor(x, 1 - c), jnp.bitwise_xor(y, c))
        dst_chip = (jnp.bitwise_xor(x, c), jnp.bitwise_xor(y, 1 - c))
        return remote(k, 2, (*src_chip, c), (*dst_chip, c))

    def passed(k, idx):
        return remote(k, 3 + idx, (*chips[idx], c), sibling)

    def mine(k):
        return pltpu.make_async_copy(ins[k], region(k, x, y, None), local_sems.at[k])

    def start():
        for k in range(n):
            mine(k).start()
            for idx in range(2 if shards[k][2] else 3):
                first(k, idx).start()

    def relay_on():
        for k in range(n):
            split = shards[k][2]
            for idx in range(2):
                remote(k, idx, (*chips[idx], c if split else None), me).wait_recv()
            if split:
                relay(k).start()
                passed(k, 0).start()
                passed(k, 1).start()

    def finish():
        for k in range(n):
            split = shards[k][2]
            remote(k, 2, (*chips[2], c if split else None), me).wait_recv()
            if split:
                passed(k, 2).start()
        for k in range(n):
            if shards[k][2]:
                for idx in range(3):
                    remote(k, 3 + idx, (*chips[idx], 1 - c), me).wait_recv()
        for k in range(n):
            if shards[k][2]:
                for cp in (first(k, 0), first(k, 1), relay(k), passed(k, 0), passed(k, 1), passed(k, 2)):
                    cp.wait_send()
            else:
                for idx in range(3):
                    first(k, idx).wait_send()
            mine(k).wait()

    return start, relay_on, finish


RS_ADD_ROWS = (64, 32, 16, 8)


N_DEV = 2 * N_CHIPS


def _all_reduce_scratch(shape):
    return [pltpu.VMEM((N_DEV,) + tuple(shape), F32), pltpu.SemaphoreType.DMA((N_DEV - 1,)),
            pltpu.SemaphoreType.DMA((N_DEV - 1,))]


def _all_reduce_tile(v_ref, o_ref, slots, send_sems, recv_sems):
    flips = [(dx, dy, dc) for dx in (0, 1) for dy in (0, 1) for dc in (0, 1)][1:]
    x, y, c = _place()
    mine = 4 * x + 2 * y + c

    def copy(k, to_flip, slot):
        dx, dy, dc = to_flip
        peer = (jnp.bitwise_xor(x, dx), jnp.bitwise_xor(y, dy), jnp.bitwise_xor(c, dc))
        return pltpu.make_async_remote_copy(
            src_ref=v_ref, dst_ref=slots.at[slot], send_sem=send_sems.at[k], recv_sem=recv_sems.at[k],
            device_id=peer, device_id_type=MESH)

    sends = [copy(k, flip, mine) for k, flip in enumerate(flips)]
    for cp in sends:
        cp.start()
    slots[mine] = v_ref[...]
    for k, (dx, dy, dc) in enumerate(flips):
        copy(k, (dx, dy, dc), jnp.bitwise_xor(mine, 4 * dx + 2 * dy + dc)).wait_recv()
    total = slots[0]
    for d in range(1, N_DEV):
        total = total + slots[d]
    o_ref[...] = total
    for cp in sends:
        cp.wait_send()


RS_SEMS = 8
RS_LOCAL_SEMS = 5


def _rs_piece_shape(part):
    arr, cols = part[0], part[1]
    return (arr.shape[0] // 2, arr.shape[1] // N_CHIPS) if cols else tuple(arr.shape[1:])


def _rs_operands(parts):
    return [p[0] for p in parts] + [p[0] if p[2] is None else p[2] for p in parts]


def _rs_wires(parts, wire):
    return list(wire) if isinstance(wire, (list, tuple)) else [wire] * len(parts)


def _rs_shapes(parts, wire):
    n = len(parts)
    shapes = [_rs_piece_shape(p) for p in parts]
    out_shape = [jax.ShapeDtypeStruct((2,) + s, F32) for s in shapes]
    scratch = []
    for lead, kind in ((N_CHIPS, "f32"), (N_CHIPS, "narrow"), (N_CHIPS, "wire"), (None, "f32"), (N_CHIPS, "wire")):
        for s, p, w in zip(shapes, parts, _rs_wires(parts, wire)):
            dtype = {"f32": F32, "narrow": F32 if p[2] is None else p[2].dtype, "wire": w}[kind]
            scratch.append(pltpu.VMEM(s if lead is None else (lead,) + s, dtype))
    scratch += [pltpu.SemaphoreType.DMA((n * RS_SEMS,)), pltpu.SemaphoreType.DMA((n * RS_SEMS,)),
                pltpu.SemaphoreType.DMA((n * RS_LOCAL_SEMS,))]
    return out_shape, scratch


def _rs_steps(parts, ins, outs, scratch):
    n = len(parts)
    own, sib, got, fin, snd = (scratch[k * n:(k + 1) * n] for k in range(5))
    send_sems, recv_sems, local_sems = scratch[5 * n:]
    shapes = [_rs_piece_shape(p) for p in parts]
    x, y, c = _place()
    j_me = 2 * x + y
    me, sibling = (x, y, c), (x, y, 1 - c)

    def piece(a, jj, core, narrow=False):
        ref = ins[n + a] if narrow else ins[a]
        if parts[a][1]:
            r, cl = shapes[a]
            return ref.at[pl.ds(core * r, r), pl.ds(jj * cl, cl)]
        return ref.at[2 * jj + core]

    def remote(a, sem, src, dst, to):
        return pltpu.make_async_remote_copy(
            src_ref=src, dst_ref=dst, send_sem=send_sems.at[a * RS_SEMS + sem],
            recv_sem=recv_sems.at[a * RS_SEMS + sem], device_id=to, device_id_type=MESH)

    def rows_loop(a, fn):
        r = shapes[a][0]
        step = max(s for s in RS_ADD_ROWS if r % s == 0)

        def it(i, carry):
            fn(pl.ds(pl.multiple_of(i * step, step), step))
            return carry

        lax.fori_loop(0, r // step, it, 0)

    def load(a, jj):
        return pltpu.make_async_copy(piece(a, jj, c), own[a].at[jj], local_sems.at[a * RS_LOCAL_SEMS + jj])

    def to_sibling(a, jj):
        return remote(a, jj, piece(a, jj, 1 - c, narrow=True), sib[a].at[jj], sibling)

    near = (jnp.bitwise_xor(x, 1 - c), jnp.bitwise_xor(y, c))
    far = (jnp.bitwise_xor(x, c), jnp.bitwise_xor(y, 1 - c))
    diag = (1 - x, 1 - y)
    FROM_NEAR, FROM_FAR, FEED = 0, 1, 2

    def chip_of(chip):
        return 2 * chip[0] + chip[1]

    def feed(a):
        return remote(a, 4, snd[a].at[chip_of(diag)], got[a].at[FEED], (*near, c))

    def to_near(a):
        return remote(a, 5, snd[a].at[chip_of(near)], got[a].at[FROM_NEAR], (*near, c))

    def to_far(a):
        return remote(a, 6, snd[a].at[chip_of(far)], got[a].at[FROM_FAR], (*far, c))

    def store(a):
        return pltpu.make_async_copy(fin[a], outs[a].at[c], local_sems.at[a * RS_LOCAL_SEMS + 4])

    def result_to_sibling(a):
        return remote(a, 7, fin[a], outs[a].at[c], sibling)

    def exchange():
        for a in range(n):
            for jj in range(N_CHIPS):
                load(a, jj).start()
                to_sibling(a, jj).start()

    def chip_sums():
        for a in range(n):
            for jj in range(N_CHIPS):
                load(a, jj).wait()
                remote(a, jj, sib[a].at[jj], sib[a].at[jj], me).wait_recv()

                def add(sl, a=a, jj=jj):
                    q = own[a][jj, sl, :] + sib[a][jj, sl, :].astype(F32)
                    own[a][jj, sl, :] = q
                    snd[a][jj, sl, :] = q.astype(snd[a].dtype)

                rows_loop(a, add)
        for a in range(n):
            feed(a).start()
        for a in range(n):
            to_near(a).start()

    def relay():
        for a in range(n):
            remote(a, 4, got[a].at[FEED], got[a].at[FEED], me).wait_recv()

            def add(sl, a=a):
                pair = own[a][chip_of(far), sl, :] + got[a][FEED, sl, :].astype(F32)
                snd[a][chip_of(far), sl, :] = pair.astype(snd[a].dtype)

            rows_loop(a, add)
            to_far(a).start()

    def totals():
        for a in range(n):
            remote(a, 5, got[a].at[FROM_NEAR], got[a].at[FROM_NEAR], me).wait_recv()
            remote(a, 6, got[a].at[FROM_FAR], got[a].at[FROM_FAR], me).wait_recv()

            def total(sl, a=a):
                fin[a][sl, :] = (own[a][j_me, sl, :] + got[a][FROM_NEAR, sl, :].astype(F32)) + (
                    got[a][FROM_FAR, sl, :].astype(F32))

            rows_loop(a, total)
            store(a).start()
            result_to_sibling(a).start()

    def finish():
        for a in range(n):
            remote(a, 7, outs[a].at[1 - c], outs[a].at[1 - c], me).wait_recv()
        for a in range(n):
            for jj in range(N_CHIPS):
                to_sibling(a, jj).wait_send()
            for cp in (feed(a), to_near(a), to_far(a), result_to_sibling(a)):
                cp.wait_send()
            store(a).wait()

    return exchange, chip_sums, relay, totals, finish


def _rms(x):
    r = lax.rsqrt(jnp.mean(x * x, axis=-1, keepdims=True) + EPS)
    return x * r, r


def _rms_bwd(dxn, xn, r):
    return r * (dxn - xn * jnp.mean(dxn * xn, axis=-1, keepdims=True))


def _in_proj_gather(x2d, norm_g, w_in_sh, shards, tb, casts):
    t = x2d.shape[0]
    nb = t // tb
    cols = IN_COLS // N_CHIPS
    half = D_MODEL // 2
    n = len(shards)
    nc = len(casts)

    def body(x_ref, g_ref, win_ref, *refs):
        ins, cast_ins = refs[:n], refs[n:n + nc]
        z_ref, h_ref, wfull_ref = refs[n + nc:n + nc + 3]
        outs, cast_outs = refs[n + nc + 3:2 * n + nc + 3], refs[2 * n + nc + 3:2 * (n + nc) + 3]
        scratch = refs[2 * (n + nc) + 3:]
        wv, h_all, send_sems, recv_sems, local_sems, w_send, w_recv, w_local, stage = scratch[:9]
        wide, narrow, cast_sems = scratch[9:9 + nc], scratch[9 + nc:9 + 2 * nc], scratch[9 + 2 * nc]
        z_ring, z_sems = scratch[10 + 2 * nc:]
        s, i = pl.program_id(0), pl.program_id(1)
        x, y, c = _place()
        me, sibling = (x, y, c), (x, y, 1 - c)
        chips = [(x, 1 - y), (1 - x, y), (1 - x, 1 - y)]

        def w_half(cx, cy, hc):
            return wv.at[2 * cx + cy, pl.ds(hc * half, half), :]

        def w_remote(sem, block, to, src=None):
            dst = w_half(*block)
            return pltpu.make_async_remote_copy(
                src_ref=dst if src is None else src, dst_ref=dst, send_sem=w_send.at[sem],
                recv_sem=w_recv.at[sem], device_id=to, device_id_type=MESH)

        def w_first(idx):
            return w_remote(idx, (x, y, c), (*chips[idx], c))

        def w_relay():
            src_chip = (jnp.bitwise_xor(x, 1 - c), jnp.bitwise_xor(y, c))
            dst_chip = (jnp.bitwise_xor(x, c), jnp.bitwise_xor(y, 1 - c))
            return w_remote(2, (*src_chip, c), (*dst_chip, c))

        def w_pass(idx):
            return w_remote(3 + idx, (*chips[idx], c), sibling)

        def w_store(k, cx, cy):
            jj = 2 * cx + cy
            return pltpu.make_async_copy(wv.at[jj], wfull_ref.at[:, pl.ds(jj * cols, cols)], w_local.at[k])

        start_rest, relay_rest, finish_rest = _gather_steps(shards, ins, outs, send_sems, recv_sems, local_sems)

        def own(k, hc):
            return pltpu.make_async_copy(win_ref.at[pl.ds(pl.multiple_of(hc * half, half), half), :], stage.at[k],
                                         w_local.at[4 + 2 * k])

        def round_own(k, hc):
            own(k, hc).wait()
            wv[2 * x + y, pl.ds(pl.multiple_of(hc * half, half), half), :] = stage[k].astype(BF16)

        wide_in = [pltpu.make_async_copy(cast_ins[k], wide[k], cast_sems.at[k]) for k in range(nc)]
        narrow_out = [pltpu.make_async_copy(narrow[k], cast_outs[k], cast_sems.at[nc + k]) for k in range(nc)]

        @pl.when((s == 0) & (i == 0))
        def _():
            own(0, c).start()
            own(1, 1 - c).start()
            for cp in wide_in:
                cp.start()
            round_own(0, c)
            w_first(0).start()
            w_first(1).start()
            start_rest()
            round_own(1, 1 - c)
            w_store(0, x, y).start()

        @pl.when((s == 1) & (i == 0))
        def _():
            for k in range(nc):
                wide_in[k].wait()
                narrow[k][...] = wide[k][...].astype(BF16)
                narrow_out[k].start()
            w_remote(0, (*chips[0], c), me).wait_recv()
            w_remote(1, (*chips[1], c), me).wait_recv()
            w_relay().start()
            w_pass(0).start()
            w_pass(1).start()
            w_remote(3, (*chips[0], 1 - c), me).wait_recv()
            w_store(1, *chips[0]).start()

        @pl.when((s == 2) & (i == 0))
        def _():
            w_remote(4, (*chips[1], 1 - c), me).wait_recv()
            w_store(2, *chips[1]).start()

        @pl.when((s == 3) & (i == 0))
        def _():
            w_remote(2, (*chips[2], c), me).wait_recv()
            w_pass(2).start()
            w_remote(5, (*chips[2], 1 - c), me).wait_recv()
            w_store(3, *chips[2]).start()

        keep_h = pltpu.make_async_copy(h_all.at[i], h_ref.at[pl.ds(pl.multiple_of(i * tb, tb), tb), :], w_local.at[5])

        @pl.when(s == 0)
        def _():
            xn, _ = _rms(x_ref[...])
            h_all[i] = (xn * g_ref[...]).astype(BF16)
            keep_h.start()

        at = s * nb + i
        slot = at % Z_RING
        shard = jnp.bitwise_xor(2 * x + y, s)
        z_out = pltpu.make_async_copy(
            z_ring.at[slot], z_ref.at[pl.ds(pl.multiple_of(i * tb, tb), tb), pl.ds(pl.multiple_of(shard * cols, cols), cols)],
            z_sems.at[slot])
        pl.when(at >= Z_RING)(z_out.wait)
        z_ring[slot] = _dot(h_all[i], wv[shard])
        z_out.start()
        pl.when(s == 0)(keep_h.wait)

        @pl.when((s == N_CHIPS - 1) & (i == nb - 1))
        def _():
            relay_rest()
            finish_rest()
            for cp in (w_first(0), w_first(1), w_relay(), w_pass(0), w_pass(1), w_pass(2)):
                cp.wait_send()
            w_store(0, x, y).wait()
            for idx in range(3):
                w_store(idx + 1, *chips[idx]).wait()
            for cp in narrow_out:
                cp.wait()
            for k in range(Z_RING):
                pltpu.make_async_copy(z_ring.at[k], z_ref.at[pl.ds(0, tb), pl.ds(0, cols)], z_sems.at[k]).wait()

    assert N_CHIPS * nb >= Z_RING
    rest_shape, rest_sems = _gather_shapes(shards)
    out_shape = [jax.ShapeDtypeStruct((t, IN_COLS), F32), jax.ShapeDtypeStruct((t, D_MODEL), BF16),
                 jax.ShapeDtypeStruct((D_MODEL, IN_COLS), BF16)] + rest_shape
    out_shape += [jax.ShapeDtypeStruct(a.shape, BF16) for a in casts]
    any_spec = pl.BlockSpec(memory_space=pl.ANY)

    return pl.pallas_call(
        body, name="in_proj", out_shape=tuple(out_shape),
        grid=(N_CHIPS, nb),
        in_specs=[pl.BlockSpec((tb, D_MODEL), lambda s, i: (jnp.where(s == 0, i, nb - 1), 0)),
                  pl.BlockSpec((1, D_MODEL), lambda s, i: (0, 0)), any_spec] + [any_spec] * (n + nc),
        out_specs=tuple([any_spec] * (3 + n + nc)),
        scratch_shapes=[pltpu.VMEM((N_CHIPS, D_MODEL, cols), BF16), pltpu.VMEM((nb, tb, D_MODEL), BF16)] + rest_sems + [
            pltpu.SemaphoreType.DMA((GATHER_SEMS,)), pltpu.SemaphoreType.DMA((GATHER_SEMS,)),
            pltpu.SemaphoreType.DMA((N_CHIPS + 3,)), pltpu.VMEM((2, half, cols), F32)]
        + [pltpu.VMEM(a.shape, F32) for a in casts] + [pltpu.VMEM(a.shape, BF16) for a in casts]
        + [pltpu.SemaphoreType.DMA((2 * nc,)), pltpu.VMEM((Z_RING, tb, cols), F32), pltpu.SemaphoreType.DMA((Z_RING,))],
        compiler_params=pltpu.CompilerParams(dimension_semantics=("arbitrary", "arbitrary"),
                                             vmem_limit_bytes=VMEM_LIMIT_BYTES),
    )(x2d, norm_g, w_in_sh, *[sh[0] for sh in shards], *casts)


def _in_proj_bwd(dz, w_in, x2d, dx_res, norm_g, tb, reduce, shards, take):
    t = x2d.shape[0]
    nb = t // tb
    parts, wire, steps = reduce
    n = len(parts)
    k = len(shards)
    take_rows, take_width = take

    def body(dz_ref, w_ref, x_ref, dres_ref, g_ref, *refs):
        at = 2 * n + k
        dx_ref, dg_ref = refs[at:at + 2]
        rs_outs, g_outs = refs[at + 2:at + 2 + n], refs[at + 2 + n:at + 2 + n + k]
        cut_ref = refs[at + 2 + n + k]
        scratch = refs[at + 3 + n + k:]
        rs_scr, g_sems, dg_acc, ar_scr, cut_sem = scratch[:-8], scratch[-8:-5], scratch[-5], scratch[-4:-1], scratch[-1]
        rs = _rs_steps(parts, refs[:2 * n], rs_outs, rs_scr)
        for step, when in zip(rs[:-1], steps):
            pl.when(pl.program_id(0) == when)(step)
        gather = _gather_steps(shards, refs[2 * n:at], g_outs, *g_sems)
        for step, when in zip(gather, (0, nb // 2, nb - 1)):
            pl.when(pl.program_id(0) == when)(step)

        @pl.when(pl.program_id(0) == 0)
        def _():
            dg_acc[...] = jnp.zeros_like(dg_acc)

        xn, r = _rms(x_ref[...])
        g = g_ref[...]
        dh = _dot_nt(dz_ref[...], w_ref[...])
        dg_acc[0:1, :] += jnp.sum(dh * xn, axis=0, keepdims=True)
        dx_ref[...] = dres_ref[...] + _rms_bwd(dh * g, xn, r)

        @pl.when(pl.program_id(0) == nb - 1)
        def _():
            x, y, _ = _place()
            mine = pl.ds(pl.multiple_of((2 * x + y) * take_width, take_width), take_width)
            cut = pltpu.make_async_copy(g_outs[0].at[take_rows, mine], cut_ref, cut_sem)
            cut.start()
            _all_reduce_tile(dg_acc, dg_ref, *ar_scr)
            rs[-1]()
            cut.wait()

    row = lambda i: (i, 0)
    fixed = lambda i: (0, 0)
    rs_shape, rs_scratch = _rs_shapes(parts, wire)
    g_shape, g_sems = _gather_shapes(shards)
    any_spec = pl.BlockSpec(memory_space=pl.ANY)
    cut_shape = jax.ShapeDtypeStruct((take_rows.stop - take_rows.start, take_width), F32)
    return pl.pallas_call(
        body, name="in_proj_bwd",
        out_shape=tuple([jax.ShapeDtypeStruct((t, D_MODEL), F32), jax.ShapeDtypeStruct((F32_SUBLANES, D_MODEL), F32)]
                        + rs_shape + g_shape + [cut_shape]),
        grid=(nb,),
        in_specs=[pl.BlockSpec((tb, IN_COLS), row),
                  pl.BlockSpec((D_MODEL, IN_COLS), fixed, pipeline_mode=pl.Buffered(1)),
                  pl.BlockSpec((tb, D_MODEL), row), pl.BlockSpec((tb, D_MODEL), row),
                  pl.BlockSpec((1, D_MODEL), fixed)] + [any_spec] * (2 * n + k),
        out_specs=tuple([pl.BlockSpec((tb, D_MODEL), row), pl.BlockSpec((F32_SUBLANES, D_MODEL), fixed)]
                        + [any_spec] * (n + k + 1)),
        scratch_shapes=rs_scratch + g_sems + [pltpu.VMEM((F32_SUBLANES, D_MODEL), F32)] + _all_reduce_scratch(
            (F32_SUBLANES, D_MODEL)) + [pltpu.SemaphoreType.DMA(())],
        compiler_params=pltpu.CompilerParams(dimension_semantics=("arbitrary",),
                                             vmem_limit_bytes=VMEM_LIMIT_BYTES),
    )(dz, w_in, x2d, dx_res, norm_g, *_rs_operands(parts), *[sh[0] for sh in shards])


def _weight_grad(pairs, n_chunks, tb, name, reduce=None):
    t = pairs[0][0].shape[0]
    nb = t // tb
    m = len(pairs)
    parts, wire, steps = reduce if reduce is not None else ([], F32, ())
    n = len(parts)

    def body(*refs):
        lr, refs = refs[:2 * m], refs[2 * m:]
        o_refs = refs[2 * n:2 * n + 2 * m]
        if n:
            at = pl.program_id(0) * nb + pl.program_id(1)
            rs = _rs_steps(parts, refs[:2 * n], refs[2 * n + 2 * m:3 * n + 2 * m], refs[3 * n + 2 * m:])
            for step, when in zip(rs, steps):
                pl.when(at == when)(step)

        @pl.when(pl.program_id(1) == 0)
        def _():
            for q in range(m):
                o_refs[2 * q][...] = jnp.zeros_like(o_refs[2 * q])

        for q in range(m):
            o_refs[2 * q][...] += _dot_tn(lr[2 * q][...], lr[2 * q + 1][...])

        @pl.when(pl.program_id(1) == nb - 1)
        def _():
            for q in range(m):
                o_refs[2 * q + 1][...] = o_refs[2 * q][...].astype(BF16)

    rs_shape, rs_scratch = _rs_shapes(parts, wire) if n else ([], [])
    any_spec = pl.BlockSpec(memory_space=pl.ANY)
    in_specs, out_specs, out_shape = [], [], []
    for lhs, rhs in pairs:
        k, nc = lhs.shape[1], rhs.shape[1] // n_chunks
        in_specs += [pl.BlockSpec((tb, k), lambda j, i: (i, 0)), pl.BlockSpec((tb, nc), lambda j, i: (i, j))]
        out_specs += [pl.BlockSpec((None, k, nc), lambda j, i: (j, 0, 0))] * 2
        out_shape += [jax.ShapeDtypeStruct((n_chunks, k, nc), F32), jax.ShapeDtypeStruct((n_chunks, k, nc), BF16)]
    return pl.pallas_call(
        body, name=name, out_shape=tuple(out_shape + rs_shape),
        grid=(n_chunks, nb),
        in_specs=in_specs + [any_spec] * (2 * n),
        out_specs=tuple(out_specs + [any_spec] * n),
        scratch_shapes=rs_scratch,
        compiler_params=pltpu.CompilerParams(dimension_semantics=("arbitrary", "arbitrary"),
                                             vmem_limit_bytes=VMEM_LIMIT_BYTES),
    )(*[a for pair in pairs for a in pair], *_rs_operands(parts))


def _adam_update(w, g, m, v):
    m_ = ADAM_B1 * m + (1.0 - ADAM_B1) * g
    v_ = ADAM_B2 * v + (1.0 - ADAM_B2) * jnp.square(g)
    m_hat = m_ / (1.0 - ADAM_B1 ** ADAM_STEP)
    v_hat = v_ / (1.0 - ADAM_B2 ** ADAM_STEP)
    return -ADAM_LR * (m_hat / (jnp.sqrt(v_hat) + ADAM_EPS) + ADAM_WD * w), m_, v_


def _adamw_replicated(vec_sum, mat_sum, norm_grad, entries, conv):
    n = len(entries)

    def grad_of(name, shape, vec_ref, mat_ref, norm_ref):
        if name == "norm_g":
            return norm_ref[0:1, :]
        if name in MAT_BAG_AT:
            return mat_ref[MAT_BAG_AT[name]:MAT_BAG_AT[name] + shape[0], :]
        if shape[0] == 1:
            return vec_ref[_bag_row(name), 0:shape[1]]
        return jnp.concatenate([vec_ref[_bag_row(name), h * shape[1]:(h + 1) * shape[1]] for h in range(shape[0])],
                               axis=0)

    def body(vec_ref, mat_ref, norm_ref, *refs):
        ins, outs = refs[:3 * n + 4], refs[3 * n + 4:]
        for k in range(n):
            w_ref, m_ref, v_ref = ins[3 * k:3 * k + 3]
            g = grad_of(entries[k][0], w_ref.shape, vec_ref, mat_ref, norm_ref)
            d, m_, v_ = _adam_update(w_ref[...], g, m_ref[...], v_ref[...])
            for ref, val in zip(outs[4 * k:4 * k + 4], (g, d, m_, v_)):
                ref[...] = val
        w_ref, m_ref, v_ref, g_ref = ins[3 * n:]
        g = g_ref[0:w_ref.shape[0], :]
        for ref, val in zip(outs[4 * n:4 * n + 4], (g,) + _adam_update(w_ref[...], g, m_ref[...], v_ref[...])):
            ref[...] = val
        outs[4 * n + 4][...] = vec_ref[_bag_row("loss"), 0:1]

    arrays = [a for e in entries for a in e[1:]] + list(conv)
    out_shape = [jax.ShapeDtypeStruct(e[1].shape, F32) for e in entries for _ in range(4)]
    out_shape += [jax.ShapeDtypeStruct(conv[0].shape, F32)] * 4 + [jax.ShapeDtypeStruct((1, 1), F32)]
    return pl.pallas_call(
        body, name="adamw_replicated", out_shape=tuple(out_shape),
        compiler_params=pltpu.CompilerParams(vmem_limit_bytes=VMEM_LIMIT_BYTES),
    )(vec_sum, mat_sum, norm_grad, *arrays)


def _adamw_group(items, name):
    arrays = [a for item in items for a in item[:4]]
    n = len(arrays)
    blocks = []
    for k, item in enumerate(items):
        rows = item[0].shape[0] // item[4]
        blocks += [(k, slice(q * rows, (q + 1) * rows)) for q in range(item[4])]

    def body(*refs):
        ins, outs, bufs = refs[:n], refs[n:2 * n], refs[2 * n:3 * n]
        load_sems, store_sems = refs[3 * n:]

        def copies(src, dst, sems):
            return [[pltpu.make_async_copy(src[4 * k + j].at[rows], dst[4 * k + j].at[rows], sems.at[4 * b + j])
                     for j in range(4)] for b, (k, rows) in enumerate(blocks)]

        loads, stores = copies(ins, bufs, load_sems), copies(bufs, outs, store_sems)
        for cp in [cp for block in loads for cp in block]:
            cp.start()
        for b, (k, rows) in enumerate(blocks):
            for cp in loads[b]:
                cp.wait()
            w_buf, g_buf, m_buf, v_buf = bufs[4 * k:4 * k + 4]
            w_buf[rows, :], m_buf[rows, :], v_buf[rows, :] = _adam_update(
                w_buf[rows, :], g_buf[rows, :], m_buf[rows, :], v_buf[rows, :])
            for cp in stores[b]:
                cp.start()
        for cp in [cp for block in stores for cp in block]:
            cp.wait()

    any_spec = pl.BlockSpec(memory_space=pl.ANY)
    flat = pl.pallas_call(
        body, name=name, out_shape=tuple(jax.ShapeDtypeStruct(a.shape, F32) for a in arrays),
        in_specs=[any_spec] * n, out_specs=(any_spec,) * n,
        scratch_shapes=[pltpu.VMEM(a.shape, F32) for a in arrays] + [pltpu.SemaphoreType.DMA((4 * len(blocks),))] * 2,
        compiler_params=pltpu.CompilerParams(vmem_limit_bytes=VMEM_LIMIT_BYTES),
    )(*arrays)
    return [(flat[4 * k + 1], flat[4 * k], flat[4 * k + 2], flat[4 * k + 3]) for k in range(len(items))]


def _shift_down(ext, s):
    return pltpu.roll(ext, s, 0)


def _tile_shift(v, s):
    rows, cols = v.shape
    tiles = v.reshape(rows // F32_SUBLANES, F32_SUBLANES, cols)
    return pltpu.roll(tiles, s % F32_SUBLANES, 1).reshape(rows, cols)


def _shift_up(ext, s):
    return pltpu.roll(ext, ext.shape[0] - s, 0)


def _lru_gates(xc, wa_ref, ba, wx_ref, bx, lam):
    pa, px = [], []
    for h in range(LRU_HEADS):
        xh = xc[:, h * HEAD_DIM:(h + 1) * HEAD_DIM].astype(BF16)
        pa.append(_dot(xh, wa_ref[h]))
        px.append(_dot(xh, wx_ref[h]))
    r = _sigmoid(jnp.concatenate(pa, axis=1) + ba)
    ig = _sigmoid(jnp.concatenate(px, axis=1) + bx)
    sp = _softplus(-lam)
    log_a = (-LRU_C * r) * sp
    a = jnp.exp(log_a)
    mult = jnp.sqrt(jnp.tanh(-log_a) * (1.0 + a * a))
    return r, ig, a, mult, sp


def _conv(ext, w_ref, b):
    y = b + _shift_down(ext, 3) * w_ref[0:1, :]
    y = y + _shift_down(ext, 2) * w_ref[1:2, :]
    y = y + _shift_down(ext, 1) * w_ref[2:3, :]
    y = y + ext * w_ref[3:4, :]
    return y[CONV_HIST:, :]


def _pool_diff(ext, pos):
    out = []
    for g, k in enumerate(POOL_WINDOWS):
        col = ext[:, g * POOL_GROUP_DIM:(g + 1) * POOL_GROUP_DIM]
        s = col
        for step in range(g + 1):
            s = s + _shift_down(s, 2 ** step)
        count = jnp.minimum(pos + 1, k).astype(F32)
        out.append(s[POOL_HIST:, :] / count - col[POOL_HIST:, :])
    return out


def _pool_mix(diff, pw_ref):
    return jnp.concatenate([_dot(diff[g].astype(BF16), pw_ref[g]) for g in range(len(POOL_WINDOWS))], axis=1)


def _branch_specs(tb, row_map, fixed):
    fixed3 = lambda i: (0, 0, 0)
    return [pl.BlockSpec((CONV_WIDTH, D_MODEL), fixed), pl.BlockSpec((1, D_MODEL), fixed),
            pl.BlockSpec((LRU_HEADS, HEAD_DIM, HEAD_DIM), fixed3), pl.BlockSpec((1, D_MODEL), fixed),
            pl.BlockSpec((LRU_HEADS, HEAD_DIM, HEAD_DIM), fixed3), pl.BlockSpec((1, D_MODEL), fixed),
            pl.BlockSpec((1, D_MODEL), fixed),
            pl.BlockSpec((len(POOL_WINDOWS), POOL_GROUP_DIM, POOL_GROUP_DIM), fixed3),
            pl.BlockSpec((1, POOL_WIDTH), fixed)]


def _branches_fwd(z, weights, seq, tb, shards):
    t = z.shape[0]
    nb = t // tb
    nbe = seq // tb
    groups = tb // F32_SUBLANES
    n = len(shards)

    def body(xa_ref, ga_ref, xb_ref, gb_ref, cw_ref, cb_ref, wa_ref, ba_ref, wx_ref, bx_ref, lam_ref,
             pw_ref, ps_ref, *refs):
        g_ins = refs[:n]
        ya_ref, yb_ref, hl_ref = refs[n:n + 3]
        g_outs = refs[n + 3:2 * n + 3]
        xa_ext, xb_ext, carry, a_s, u_s, send_sems, recv_sems, local_sems = refs[2 * n + 3:]
        blk = pl.program_id(0) % nbe
        start_gather, relay_gather, finish_gather = _gather_steps(shards, g_ins, g_outs, send_sems, recv_sems,
                                                                  local_sems)
        pl.when(pl.program_id(0) == 0)(start_gather)
        pl.when(pl.program_id(0) == nb // 2)(relay_gather)

        @pl.when(blk == 0)
        def _():
            xa_ext[0:CONV_HIST, :] = jnp.zeros((CONV_HIST, D_MODEL), F32)
            xb_ext[0:POOL_HIST, :] = jnp.zeros((POOL_HIST, POOL_WIDTH), F32)
            carry[...] = jnp.zeros_like(carry)

        xa_ext[CONV_HIST:, :] = xa_ref[...]
        xb_ext[POOL_HIST:, :] = xb_ref[...]
        ea = xa_ext[...]
        eb = xb_ext[...]
        xa_ext[0:CONV_HIST, :] = ea[tb:, :]
        xb_ext[0:POOL_HIST, :] = eb[tb:, :]

        xc = _conv(ea, cw_ref, cb_ref[...])
        _, ig, a, mult, _ = _lru_gates(xc, wa_ref, ba_ref[...], wx_ref, bx_ref[...], lam_ref[...])
        u = mult * (ig * xc)
        row8 = lax.broadcasted_iota(jnp.int32, (tb, D_MODEL), 0) % F32_SUBLANES
        for s in (1, 2, 4):
            m = row8 >= s
            u = jnp.where(m, a * _tile_shift(u, s) + u, u)
            a = jnp.where(m, a * _tile_shift(a, s), a)
        a_s[...] = a
        u_s[...] = u

        def step(g, cr):
            sl = pl.ds(pl.multiple_of(g * F32_SUBLANES, F32_SUBLANES), F32_SUBLANES)
            hb = a_s[sl, :] * cr + u_s[sl, :]
            hl_ref[sl, :] = hb
            return jnp.broadcast_to(hb[F32_SUBLANES - 1:F32_SUBLANES, :], (F32_SUBLANES, D_MODEL))

        carry[...] = lax.fori_loop(0, groups, step, carry[...], unroll=True)
        ga = ga_ref[...]
        ya_ref[...] = (hl_ref[...] * (ga * _sigmoid(ga))).astype(BF16)

        pos = blk * tb + lax.broadcasted_iota(jnp.int32, (tb, POOL_GROUP_DIM), 0)
        ypre = _pool_mix(_pool_diff(eb, pos), pw_ref)
        gb = gb_ref[...]
        yb_ref[...] = ((ypre * ps_ref[...]) * (gb * _sigmoid(gb))).astype(BF16)
        pl.when(pl.program_id(0) == nb - 1)(finish_gather)

    row = lambda i: (i, 0)
    fixed = lambda i: (0, 0)
    any_spec = pl.BlockSpec(memory_space=pl.ANY)
    in_specs = [pl.BlockSpec((tb, D_MODEL), lambda i: (i, 0)), pl.BlockSpec((tb, D_MODEL), lambda i: (i, 1)),
                pl.BlockSpec((tb, POOL_WIDTH), lambda i: (i, 4)), pl.BlockSpec((tb, POOL_WIDTH), lambda i: (i, 5)),
                ] + _branch_specs(tb, row, fixed) + [any_spec] * n
    g_shape, g_sems = _gather_shapes(shards)
    return pl.pallas_call(
        body, name="branches_fwd",
        out_shape=tuple([jax.ShapeDtypeStruct((t, D_MODEL), BF16), jax.ShapeDtypeStruct((t, POOL_WIDTH), BF16),
                         jax.ShapeDtypeStruct((t, D_MODEL), F32)] + g_shape),
        grid=(nb,), in_specs=in_specs,
        out_specs=tuple([pl.BlockSpec((tb, D_MODEL), row), pl.BlockSpec((tb, POOL_WIDTH), row),
                         pl.BlockSpec((tb, D_MODEL), row)] + [any_spec] * n),
        scratch_shapes=[pltpu.VMEM((tb + CONV_HIST, D_MODEL), F32), pltpu.VMEM((tb + POOL_HIST, POOL_WIDTH), F32),
                        pltpu.VMEM((F32_SUBLANES, D_MODEL), F32),
                        pltpu.VMEM((tb, D_MODEL), F32), pltpu.VMEM((tb, D_MODEL), F32)] + g_sems,
        compiler_params=pltpu.CompilerParams(dimension_semantics=("arbitrary",),
                                             vmem_limit_bytes=VMEM_LIMIT_BYTES),
    )(z, z, z, z, *weights, *[sh[0] for sh in shards])


def _branches_bwd(z, hl, dya, dyb, dzm, weights, vec_bag, seq, tb, riders):
    t = z.shape[0]
    nb = t // tb
    nbe = seq // tb
    groups = tb // F32_SUBLANES
    nr = len(riders)

    def body(xa_ref, xap_ref, ga_ref, xb_ref, xbp_ref, gb_ref, hl_ref, hlp_ref, dya_ref, dyb_ref, dzm_ref,
             cw_ref, cb_ref, wa_ref, ba_ref, wx_ref, bx_ref, lam_ref, pw_ref, ps_ref, vec_in_ref, *rest):
        pairs, (dz_ref, vec_ref, mat_ref), grads = rest[:2 * nr], rest[2 * nr:2 * nr + 3], rest[2 * nr + 3:4 * nr + 3]
        xa_ext, xb_ext, hl_ext, a_ext, dxc_ext, dwin_ext, g_carry, b_s, d_s, g_s = rest[4 * nr + 3:]
        i = pl.program_id(0)
        blk = (nb - 1 - i) % nbe

        def mat_rows(name, k):
            at = MAT_BAG_AT[name] + k * HEAD_DIM
            return slice(at, at + HEAD_DIM)

        def rider(k):
            grads[2 * k][...] += _dot_tn(pairs[2 * k][...], pairs[2 * k + 1][...])

        @pl.when(i == 0)
        def _():
            vec_ref[...] = vec_in_ref[...]
            mat_ref[...] = jnp.zeros_like(mat_ref)
            for k in range(nr):
                grads[2 * k][...] = jnp.zeros_like(grads[2 * k])

        @pl.when(blk == nbe - 1)
        def _():
            a_ext[tb:, :] = jnp.zeros((F32_SUBLANES, D_MODEL), F32)
            dxc_ext[tb:, :] = jnp.zeros((CONV_HIST, D_MODEL), F32)
            dwin_ext[tb:, :] = jnp.zeros((POOL_HIST, POOL_WIDTH), F32)
            g_carry[...] = jnp.zeros_like(g_carry)

        live = (blk > 0).astype(F32)
        xa_ext[0:CONV_HIST, :] = xap_ref[...] * live
        xa_ext[CONV_HIST:, :] = xa_ref[...]
        xb_ext[0:POOL_HIST, :] = xbp_ref[...] * live
        xb_ext[POOL_HIST:, :] = xb_ref[...]
        hl_ext[0:F32_SUBLANES, :] = hlp_ref[...] * live
        hl_ext[F32_SUBLANES:, :] = hl_ref[...]
        ea = xa_ext[...]
        eb = xb_ext[...]
        rider(0)

        xc = _conv(ea, cw_ref, cb_ref[...])
        lam = lam_ref[...]
        r, ig, a, mult, sp = _lru_gates(xc, wa_ref, ba_ref[...], wx_ref, bx_ref[...], lam)
        hl = hl_ref[...]
        ga = ga_ref[...]
        sga = _sigmoid(ga)
        dya = dya_ref[...]
        dhl = dya * (ga * sga)
        dz_ref[:, D_MODEL:2 * D_MODEL] = (dya * hl * (sga * (1.0 + ga * (1.0 - sga)))).astype(BF16)

        a_ext[0:tb, :] = a
        b = _shift_up(a_ext[...], 1)[0:tb, :]
        a_ext[tb:, :] = jnp.broadcast_to(a[0:1, :], (F32_SUBLANES, D_MODEL))
        d = dhl
        row8 = lax.broadcasted_iota(jnp.int32, (tb, D_MODEL), 0) % F32_SUBLANES
        for s in (1, 2, 4):
            m = row8 < F32_SUBLANES - s
            d = jnp.where(m, d + b * _tile_shift(d, -s), d)
            b = jnp.where(m, b * _tile_shift(b, -s), b)
        b_s[...] = b
        d_s[...] = d

        def step(k, cr):
            sl = pl.ds(pl.multiple_of((groups - 1 - k) * F32_SUBLANES, F32_SUBLANES), F32_SUBLANES)
            gb_ = d_s[sl, :] + b_s[sl, :] * cr
            g_s[sl, :] = gb_
            return jnp.broadcast_to(gb_[0:1, :], (F32_SUBLANES, D_MODEL))

        g_carry[...] = lax.fori_loop(0, groups, step, g_carry[...], unroll=4)
        rider(1)
        gsc = g_s[...]
        da = gsc * _shift_down(hl_ext[...], 1)[F32_SUBLANES:, :]
        dmult = gsc * (ig * xc)
        dig = gsc * (mult * xc)
        dxc = gsc * (mult * ig)
        dlog_a = da * a - (a * a) * dmult / mult
        dr = dlog_a * (-LRU_C * sp)
        vec_ref[_bag_row("lru_lambda"), :] += jnp.sum(dlog_a * (-LRU_C * r), axis=0, keepdims=True)
        dpa = dr * (r * (1.0 - r))
        dpx = dig * (ig * (1.0 - ig))
        vec_ref[_bag_row("lru_b_a"), :] += jnp.sum(dpa, axis=0, keepdims=True)
        vec_ref[_bag_row("lru_b_x"), :] += jnp.sum(dpx, axis=0, keepdims=True)
        back = []
        for h in range(LRU_HEADS):
            cols = slice(h * HEAD_DIM, (h + 1) * HEAD_DIM)
            xh = xc[:, cols].astype(BF16)
            dpa_h = dpa[:, cols].astype(BF16)
            dpx_h = dpx[:, cols].astype(BF16)
            mat_ref[mat_rows("lru_w_a", h), :] += _dot_tn(xh, dpa_h)
            mat_ref[mat_rows("lru_w_x", h), :] += _dot_tn(xh, dpx_h)
            back.append(_dot_nt(dpa_h, wa_ref[h]) + _dot_nt(dpx_h, wx_ref[h]))
        dxc = dxc + jnp.concatenate(back, axis=1)
        vec_ref[_bag_row("conv_b"), :] += jnp.sum(dxc, axis=0, keepdims=True)
        for k in range(CONV_WIDTH):
            tap = _shift_down(ea, CONV_WIDTH - 1 - k)[CONV_HIST:, :] if k < CONV_WIDTH - 1 else ea[CONV_HIST:, :]
            vec_ref[_bag_row("conv_w", k), :] += jnp.sum(dxc * tap, axis=0, keepdims=True)
        dxc_ext[0:tb, :] = dxc
        ed = dxc_ext[...]
        dxa = ed * cw_ref[3:4, :]
        dxa = dxa + _shift_up(ed, 1) * cw_ref[2:3, :]
        dxa = dxa + _shift_up(ed, 2) * cw_ref[1:2, :]
        dxa = dxa + _shift_up(ed, 3) * cw_ref[0:1, :]
        dz_ref[:, 0:D_MODEL] = dxa[0:tb, :].astype(BF16)
        dxc_ext[tb:, :] = dxc[0:CONV_HIST, :]

        pos = blk * tb + lax.broadcasted_iota(jnp.int32, (tb, POOL_GROUP_DIM), 0)
        diff = _pool_diff(eb, pos)
        rider(2)
        ypre = _pool_mix(diff, pw_ref)
        ps = ps_ref[...]
        gb = gb_ref[...]
        sgb = _sigmoid(gb)
        dyb = dyb_ref[...]
        dyp = dyb * (gb * sgb)
        dz_ref[:, 2 * D_MODEL + POOL_WIDTH:3 * D_MODEL] = (
            dyb * (ypre * ps) * (sgb * (1.0 + gb * (1.0 - sgb)))).astype(BF16)
        vec_ref[_bag_row("pool_scale"), 0:POOL_WIDTH] += jnp.sum(dyp * ypre, axis=0, keepdims=True)
        dypre = dyp * ps
        for g, k in enumerate(POOL_WINDOWS):
            cols = slice(g * POOL_GROUP_DIM, (g + 1) * POOL_GROUP_DIM)
            dyg = dypre[:, cols].astype(BF16)
            mat_ref[mat_rows("pool_w", g), :] += _dot_tn(diff[g].astype(BF16), dyg)
            ddiff = _dot_nt(dyg, pw_ref[g])
            count = jnp.minimum(pos + 1, k).astype(F32)
            dwin = ddiff / count
            dwin_ext[0:tb, cols] = dwin
            s = dwin_ext[:, cols]
            for step_ in range(g + 1):
                s = s + _shift_up(s, 2 ** step_)
            dz_ref[:, 2 * D_MODEL + g * POOL_GROUP_DIM:2 * D_MODEL + (g + 1) * POOL_GROUP_DIM] = (
                s[0:tb, :] - ddiff).astype(BF16)
            dwin_ext[tb:, cols] = dwin[0:POOL_HIST, :]

        dz_ref[:, 3 * D_MODEL:] = dzm_ref[...]

        @pl.when(i == nb - 1)
        def _():
            row = _bag_row("lru_lambda")
            vec_ref[row, :] = vec_ref[row, :] * (-_sigmoid(-lam))
            for k in range(nr):
                grads[2 * k + 1][...] = grads[2 * k][...].astype(BF16)

    rev = lambda i: (nb - 1 - i, 0)
    fixed = lambda i: (0, 0)

    def prev(rows, col):
        per = tb // rows
        return lambda i: (jnp.maximum((nb - 1 - i) * per - 1, 0), col)

    in_specs = [pl.BlockSpec((tb, D_MODEL), lambda i: (nb - 1 - i, 0)),
                pl.BlockSpec((CONV_HIST, D_MODEL), prev(CONV_HIST, 0)),
                pl.BlockSpec((tb, D_MODEL), lambda i: (nb - 1 - i, 1)),
                pl.BlockSpec((tb, POOL_WIDTH), lambda i: (nb - 1 - i, 4)),
                pl.BlockSpec((POOL_HIST, POOL_WIDTH), prev(POOL_HIST, 4)),
                pl.BlockSpec((tb, POOL_WIDTH), lambda i: (nb - 1 - i, 5)),
                pl.BlockSpec((tb, D_MODEL), rev),
                pl.BlockSpec((F32_SUBLANES, D_MODEL), prev(F32_SUBLANES, 0)),
                pl.BlockSpec((tb, D_MODEL), rev), pl.BlockSpec((tb, POOL_WIDTH), rev),
                pl.BlockSpec((tb, 2 * D_MODEL), rev)] + _branch_specs(tb, rev, fixed) + [
                    pl.BlockSpec((VEC_BAG_ROWS, D_MODEL), fixed)]
    vec_at = len(in_specs) - 1
    out_shape = [jax.ShapeDtypeStruct((t, IN_COLS), BF16), jax.ShapeDtypeStruct((VEC_BAG_ROWS, D_MODEL), F32),
                 jax.ShapeDtypeStruct((MAT_BAG_ROWS, HEAD_DIM), F32)]
    out_specs = [pl.BlockSpec((tb, IN_COLS), rev), pl.BlockSpec((VEC_BAG_ROWS, D_MODEL), fixed),
                 pl.BlockSpec((MAT_BAG_ROWS, HEAD_DIM), fixed)]
    for lhs, rhs in riders:
        in_specs += [pl.BlockSpec((tb, lhs.shape[1]), rev), pl.BlockSpec((tb, rhs.shape[1]), rev)]
        grad = (lhs.shape[1], rhs.shape[1])
        out_shape += [jax.ShapeDtypeStruct(grad, F32), jax.ShapeDtypeStruct(grad, BF16)]
        out_specs += [pl.BlockSpec(grad, fixed)] * 2
    scratch = [pltpu.VMEM((tb + CONV_HIST, D_MODEL), F32), pltpu.VMEM((tb + POOL_HIST, POOL_WIDTH), F32),
               pltpu.VMEM((tb + F32_SUBLANES, D_MODEL), F32), pltpu.VMEM((tb + F32_SUBLANES, D_MODEL), F32),
               pltpu.VMEM((tb + CONV_HIST, D_MODEL), F32), pltpu.VMEM((tb + POOL_HIST, POOL_WIDTH), F32),
               pltpu.VMEM((F32_SUBLANES, D_MODEL), F32),
               pltpu.VMEM((tb, D_MODEL), F32), pltpu.VMEM((tb, D_MODEL), F32), pltpu.VMEM((tb, D_MODEL), F32)]
    return pl.pallas_call(
        body, name="branches_bwd", out_shape=tuple(out_shape), grid=(nb,), in_specs=in_specs,
        out_specs=tuple(out_specs), scratch_shapes=scratch, input_output_aliases={vec_at: 1},
        compiler_params=pltpu.CompilerParams(dimension_semantics=("arbitrary",),
                                             vmem_limit_bytes=VMEM_LIMIT_BYTES),
    )(z, z, z, z, z, z, hl, hl, dya, dyb, dzm, *weights, vec_bag, *[a for pair in riders for a in pair])


def _merge_head(x2d, ya, yb, z, p2d, tgt, w_pl, w_pp, w_out, w_pg, w_pe, g2, gf, tb):
    t = x2d.shape[0]
    p_dim = p2d.shape[1]

    def body(x_ref, ya_ref, yb_ref, ma_ref, mb_ref, p_ref, t_ref, wpl_ref, wpp_ref, wout_ref, wpg_ref, wpe_ref,
             g2_ref, gf_ref,
             bag_ref, dxr_ref, dya_ref, dyb_ref, dzm_ref,
             mg_ref, do_ref, hn_ref, dgp_ref, dpe_ref, da_ref, dbm_ref, pbf_ref):
        @pl.when(pl.program_id(0) == 0)
        def _():
            bag_ref[...] = jnp.zeros_like(bag_ref)

        a_ = _dot(ya_ref[...], wpl_ref[...])
        bm = _dot(yb_ref[...], wpp_ref[...])
        sa = _sigmoid(ma_ref[...])
        sb = _sigmoid(mb_ref[...])
        mg = (sa * a_ + sb * bm).astype(BF16)
        mg_ref[...] = mg
        x1 = x_ref[...] + _dot(mg, wout_ref[...])
        xn2, r2 = _rms(x1)
        g2 = g2_ref[...]
        hn = (xn2 * g2).astype(BF16)
        hn_ref[...] = hn
        gate = _sigmoid(_dot(hn, wpg_ref[...]))
        pbf = p_ref[...].astype(BF16)
        pbf_ref[...] = pbf
        pe = _dot(pbf, wpe_ref[...])
        x2 = x1 + gate * pe
        xn3, r3 = _rms(x2)
        gf = gf_ref[...]
        err = xn3 * gf - t_ref[...]
        bag_ref[_bag_rows("loss"), 0:128] += 0.5 * jnp.sum(jnp.mean(err * err, axis=-1))

        dy = err * (1.0 / D_MODEL)
        bag_ref[_bag_row("final_g"), :] += jnp.sum(dy * xn3, axis=0, keepdims=True)
        dx2 = _rms_bwd(dy * gf, xn3, r3)
        dpe_ref[...] = (dx2 * gate).astype(BF16)
        dgp = ((dx2 * pe) * (gate * (1.0 - gate))).astype(BF16)
        dgp_ref[...] = dgp
        dhn = _dot_nt(dgp, wpg_ref[...])
        bag_ref[_bag_row("ple_norm_g"), :] += jnp.sum(dhn * xn2, axis=0, keepdims=True)
        dx1 = dx2 + _rms_bwd(dhn * g2, xn2, r2)
        dxr_ref[...] = dx1
        do = dx1.astype(BF16)
        do_ref[...] = do
        dmg = _dot_nt(do, wout_ref[...])
        da = (dmg * sa).astype(BF16)
        dbm = (dmg * sb).astype(BF16)
        da_ref[...] = da
        dbm_ref[...] = dbm
        dzm_ref[:, 0:D_MODEL] = (dmg * a_ * (sa * (1.0 - sa))).astype(BF16)
        dzm_ref[:, D_MODEL:] = (dmg * bm * (sb * (1.0 - sb))).astype(BF16)
        dya_ref[...] = _dot_nt(da, wpl_ref[...])
        dyb_ref[...] = _dot_nt(dbm, wpp_ref[...])

    row = lambda i: (i, 0)
    fixed = lambda i: (0, 0)

    def resident(shape):
        return pl.BlockSpec(shape, fixed, pipeline_mode=pl.Buffered(1))

    tok = lambda width: pl.BlockSpec((tb, width), row)
    in_specs = [tok(D_MODEL), tok(D_MODEL), tok(POOL_WIDTH),
                pl.BlockSpec((tb, D_MODEL), lambda i: (i, 3)), pl.BlockSpec((tb, D_MODEL), lambda i: (i, 4)),
                tok(p_dim), tok(D_MODEL),
                resident((D_MODEL, D_MODEL)), resident((POOL_WIDTH, D_MODEL)), resident((D_MODEL, D_MODEL)),
                resident((D_MODEL, D_MODEL)), resident((p_dim, D_MODEL)),
                pl.BlockSpec((1, D_MODEL), fixed), pl.BlockSpec((1, D_MODEL), fixed)]
    bf = lambda width: jax.ShapeDtypeStruct((t, width), BF16)
    f32 = lambda width: jax.ShapeDtypeStruct((t, width), F32)
    out_shape = (jax.ShapeDtypeStruct((VEC_BAG_ROWS, D_MODEL), F32),
                 f32(D_MODEL), f32(D_MODEL), f32(POOL_WIDTH), bf(2 * D_MODEL),
                 bf(D_MODEL), bf(D_MODEL), bf(D_MODEL), bf(D_MODEL), bf(D_MODEL), bf(D_MODEL), bf(D_MODEL), bf(p_dim))
    out_specs = (pl.BlockSpec((VEC_BAG_ROWS, D_MODEL), fixed),
                 tok(D_MODEL), tok(D_MODEL), tok(POOL_WIDTH), tok(2 * D_MODEL),
                 tok(D_MODEL), tok(D_MODEL), tok(D_MODEL), tok(D_MODEL), tok(D_MODEL), tok(D_MODEL), tok(D_MODEL),
                 tok(p_dim))
    return pl.pallas_call(
        body, name="merge_head", out_shape=out_shape, grid=(t // tb,), in_specs=in_specs, out_specs=out_specs,
        compiler_params=pltpu.CompilerParams(dimension_semantics=("arbitrary",),
                                             vmem_limit_bytes=VMEM_LIMIT_BYTES),
    )(x2d, ya, yb, z, z, p2d, tgt, w_pl, w_pp, w_out, w_pg, w_pe, g2, gf)


def kernel(x, p, norm_g, w_in, conv_w, conv_b, lru_w_a, lru_b_a, lru_w_x, lru_b_x, lru_lambda, pool_w, pool_scale, w_proj_lru, w_proj_pool, w_out, ple_norm_g, w_ple_gate, w_ple_proj, final_g, loss_target, m_norm_g, m_w_in, m_conv_w, m_conv_b, m_lru_w_a, m_lru_b_a, m_lru_w_x, m_lru_b_x, m_lru_lambda, m_pool_w, m_pool_scale, m_w_proj_lru, m_w_proj_pool, m_w_out, m_ple_norm_g, m_w_ple_gate, m_w_ple_proj, m_final_g, v_norm_g, v_w_in, v_conv_w, v_conv_b, v_lru_w_a, v_lru_b_a, v_lru_w_x, v_lru_b_x, v_lru_lambda, v_pool_w, v_pool_scale, v_w_proj_lru, v_w_proj_pool, v_w_out, v_ple_norm_g, v_w_ple_gate, v_w_ple_proj, v_final_g):
    bsz, seq, _ = x.shape
    t = bsz * seq
    tb_mm = min(1024, seq)
    tb_seq = min(256, seq // 2) if seq >= 512 else seq
    x2d = x.reshape(t, D_MODEL)
    p2d = p.reshape(t, p.shape[-1])
    tgt = loss_target.reshape(t, D_MODEL)

    rest = [(w_proj_lru[0], 0), (w_proj_pool[0], 1), (w_out[0], 0), (w_ple_gate[0], 0), (w_ple_proj[0], 1)]
    z, h_bf, w_in_f, conv_w_f, *narrow = _in_proj_gather(
        x2d, norm_g, w_in[0], [(conv_w[0], 1, False)], tb_mm,
        [w for w, _ in rest] + [lru_w_a[0], lru_w_x[0], pool_w[0]])
    wa_bf, wx_bf, pw_bf = narrow[len(rest):]
    branch_w = (conv_w_f, conv_b, wa_bf, lru_b_a.reshape(1, D_MODEL), wx_bf, lru_b_x.reshape(1, D_MODEL),
                lru_lambda, pw_bf, pool_scale)

    ya, yb, hl, w_pl_f, w_pp_f, w_out_f, w_pg_f, w_pe_f = _branches_fwd(
        z, branch_w, seq, tb_seq, [(w16, axis, True) for w16, (_, axis) in zip(narrow, rest)])
    (vec_bag, dx_res, dya, dyb, dzm, mg_bf, do_bf, hn_bf, dgp_bf, dpe_bf, da_bf, dbm_bf, p_bf) = _merge_head(
        x2d, ya, yb, z, p2d, tgt, w_pl_f, w_pp_f, w_out_f, w_pg_f, w_pe_f, ple_norm_g, final_g.reshape(1, D_MODEL),
        tb_seq)
    dz, vec_bag, mat_bag, g_out, g_out16, g_pp, g_pp16, g_pe, g_pe16 = _branches_bwd(
        z, hl, dya, dyb, dzm, branch_w, vec_bag, seq, tb_seq, [(mg_bf, do_bf), (yb, dbm_bf), (p_bf, dpe_bf)])

    tb_dw = min(1024, seq)
    def row_pieces(g32, g16):
        pieces = (8, g32.shape[0] // 8, g32.shape[1])
        return g32.reshape(pieces), False, g16.reshape(pieces)

    g_pl, g_pl16, g_pg, g_pg16 = _weight_grad([(ya, da_bf), (hn_bf, dgp_bf)], 1, tb_dw, "dw_proj")
    p_dim = p2d.shape[1]
    proj_parts = [row_pieces(g_pl[0], g_pl16[0]), (g_pp, True, g_pp16), row_pieces(g_out, g_out16),
                  row_pieces(g_pg[0], g_pg16[0]), (g_pe, True, g_pe16)]
    nb_dw = t // tb_dw
    g_in, g_in16, r_pl, r_pp, r_out, r_pg, r_pe, vec_mine, mat_mine = _weight_grad(
        [(h_bf, dz)], N_CHIPS, tb_dw, "dw_in",
        reduce=(proj_parts + [(vec_bag.reshape(8, VEC_BAG_ROWS // 8, D_MODEL), False, None),
                              (mat_bag.reshape(8, MAT_BAG_ROWS // 8, HEAD_DIM), False, None)],
                [BF16] * 5 + [F32] * 2,
                (0, nb_dw // 2, 2 * nb_dw - 1, 3 * nb_dw + nb_dw // 2, N_CHIPS * nb_dw - 1)))
    pieces = (8, D_MODEL // 2, IN_COLS // N_CHIPS)
    nb_seq = t // tb_seq
    dx, g_g1, r_in, vec_sum, mat_sum, g_cw = _in_proj_bwd(
        dz, w_in_f, x2d, dx_res, norm_g, tb_seq,
        reduce=([(g_in.reshape(pieces), False, g_in16.reshape(pieces))], BF16,
                (0, nb_seq // 8, nb_seq // 2, nb_seq - 1)),
        shards=[(vec_mine.reshape(VEC_BAG_ROWS // N_CHIPS, D_MODEL), 0, True),
                (mat_mine.reshape(MAT_BAG_ROWS // N_CHIPS, HEAD_DIM), 0, True)],
        take=(_bag_rows("conv_w"), D_MODEL // N_CHIPS))

    big = [(w_in, r_in, m_w_in, v_w_in, 4), (w_proj_lru, r_pl, m_w_proj_lru, v_w_proj_lru, 1),
           (w_proj_pool, r_pp, m_w_proj_pool, v_w_proj_pool, 1), (w_out, r_out, m_w_out, v_w_out, 1),
           (w_ple_gate, r_pg, m_w_ple_gate, v_w_ple_gate, 1), (w_ple_proj, r_pe, m_w_ple_proj, v_w_ple_proj, 1)]
    u_in, u_pl, u_pp, u_out, u_pg, u_pe = [tuple(a[None] for a in u) for u in _adamw_group(
        [(w[0], g.reshape(w.shape[1:]), m[0], v[0], cuts) for w, g, m, v, cuts in big], "adamw_sharded")]

    small = [("norm_g", norm_g, m_norm_g, v_norm_g), ("conv_b", conv_b, m_conv_b, v_conv_b),
             ("lru_w_a", lru_w_a, m_lru_w_a, v_lru_w_a), ("lru_b_a", lru_b_a, m_lru_b_a, v_lru_b_a),
             ("lru_w_x", lru_w_x, m_lru_w_x, v_lru_w_x), ("lru_b_x", lru_b_x, m_lru_b_x, v_lru_b_x),
             ("lru_lambda", lru_lambda, m_lru_lambda, v_lru_lambda), ("pool_w", pool_w, m_pool_w, v_pool_w),
             ("pool_scale", pool_scale, m_pool_scale, v_pool_scale),
             ("ple_norm_g", ple_norm_g, m_ple_norm_g, v_ple_norm_g), ("final_g", final_g, m_final_g, v_final_g)]

    def view(a):
        return a.reshape(-1, a.shape[-1]) if a.ndim != 3 else a[0]

    flat = _adamw_replicated(vec_sum, mat_sum, g_g1, [(name,) + tuple(view(a) for a in arrs) for name, *arrs in small],
                             (conv_w[0], m_conv_w[0], v_conv_w[0], g_cw))
    u_small = {name: tuple(flat[4 * k + pick].reshape(arrs[0].shape) for pick in range(4))
               for k, (name, *arrs) in enumerate(small)}
    u_cw = tuple(a[None] for a in flat[4 * len(small):4 * len(small) + 4])

    loss = flat[-1].reshape(())
    grad_x = dx.reshape(bsz, seq, D_MODEL)

    def ordered(pick):
        s = {name: u[pick] for name, u in u_small.items()}
        return [s["norm_g"], u_in[pick], u_cw[pick], s["conv_b"], s["lru_w_a"], s["lru_b_a"], s["lru_w_x"], s["lru_b_x"],
                s["lru_lambda"], s["pool_w"], s["pool_scale"], u_pl[pick], u_pp[pick], u_out[pick], s["ple_norm_g"],
                u_pg[pick], u_pe[pick], s["final_g"]]

    return (loss, grad_x, *ordered(0), *ordered(1), *ordered(2), *ordered(3))
```

```python
import jax
import jax.numpy as jnp
from jax import lax
from jax.experimental import pallas as pl
from jax.experimental.pallas import tpu as pltpu

F32 = jnp.float32
BF16 = jnp.bfloat16
MESH = pl.DeviceIdType.MESH

D_MODEL = 1024
LRU_HEADS = 8
HEAD_DIM = 128
CONV_WIDTH = 4
LRU_C = 8.0
POOL_WIDTH = 512
POOL_WINDOWS = (2, 4, 8, 16)
POOL_GROUP_DIM = 128
IN_COLS = 5120
N_CHIPS = 4
EPS = 1e-6

ADAM_LR = 0.001
ADAM_B1 = 0.9
ADAM_B2 = 0.999
ADAM_EPS = 1e-08
ADAM_WD = 0.01
ADAM_STEP = 10

F32_SUBLANES = 8
CONV_HIST = 8
POOL_HIST = 16
VMEM_LIMIT_BYTES = 58 * 1024 * 1024
VEC_BAG_SLOTS = ("norm_g", "conv_w", "conv_b", "lru_b_a", "lru_b_x", "lru_lambda", "pool_scale", "ple_norm_g",
                 "final_g", "loss")
VEC_BAG_ROWS = 128
MAT_BAG_AT = {"lru_w_a": 0, "lru_w_x": LRU_HEADS * HEAD_DIM, "pool_w": 2 * LRU_HEADS * HEAD_DIM}
MAT_BAG_ROWS = 2 * LRU_HEADS * HEAD_DIM + len(POOL_WINDOWS) * POOL_GROUP_DIM


def _bag_row(name, k=0):
    at = F32_SUBLANES * VEC_BAG_SLOTS.index(name) + k
    return slice(at, at + 1)


def _bag_rows(name):
    at = F32_SUBLANES * VEC_BAG_SLOTS.index(name)
    return slice(at, at + F32_SUBLANES)


def _dot(a, b):
    return jnp.dot(a, b, preferred_element_type=F32)


def _dot_nt(a, b):
    return lax.dot_general(a, b, (((1,), (1,)), ((), ())), preferred_element_type=F32)


def _dot_tn(a, b):
    return lax.dot_general(a, b, (((0,), (0,)), ((), ())), preferred_element_type=F32)


def _sigmoid(v):
    return jax.nn.sigmoid(v)


def _softplus(v):
    return jnp.maximum(v, 0.0) + jnp.log1p(jnp.exp(-jnp.abs(v)))


def _place():
    return lax.axis_index("x"), lax.axis_index("y"), lax.axis_index("c")


GATHER_SEMS = 6
Z_RING = 3


def _gather_shapes(shards):
    out_shape = []
    for arr, axis, _ in shards:
        r, cols = arr.shape
        out_shape.append(jax.ShapeDtypeStruct((N_CHIPS * r, cols) if axis == 0 else (r, N_CHIPS * cols), arr.dtype))
    n = len(shards)
    sems = [pltpu.SemaphoreType.DMA((n * GATHER_SEMS,)), pltpu.SemaphoreType.DMA((n * GATHER_SEMS,)),
            pltpu.SemaphoreType.DMA((n,))]
    return out_shape, sems


def _gather_steps(shards, ins, outs, send_sems, recv_sems, local_sems):
    n = len(shards)
    x, y, c = _place()
    me, sibling = (x, y, c), (x, y, 1 - c)
    chips = [(x, 1 - y), (1 - x, y), (1 - x, 1 - y)]

    def region(k, cx, cy, hc):
        (r, cols), axis = shards[k][0].shape, shards[k][1]
        j = 2 * cx + cy
        if axis == 0:
            if hc is None:
                return outs[k].at[pl.ds(j * r, r), :]
            return outs[k].at[pl.ds(j * r + hc * (r // 2), r // 2), :]
        if hc is None:
            return outs[k].at[:, pl.ds(j * cols, cols)]
        return outs[k].at[pl.ds(hc * (r // 2), r // 2), pl.ds(j * cols, cols)]

    def remote(k, sem, block, to, src=None):
        dst = region(k, *block)
        return pltpu.make_async_remote_copy(
            src_ref=dst if src is None else src, dst_ref=dst,
            send_sem=send_sems.at[k * GATHER_SEMS + sem], recv_sem=recv_sems.at[k * GATHER_SEMS + sem],
            device_id=to, device_id_type=MESH)

    def first(k, idx):
        r, split = shards[k][0].shape[0], shards[k][2]
        src = ins[k].at[pl.ds(c * (r // 2), r // 2), :] if split else ins[k]
        return remote(k, idx, (x, y, c if split else None), (*chips[idx], c), src=src)

    def relay(k):
        src_chip = (jnp.bitwise_xor(x, 1 - c), jnp.bitwise_xor(y, c))
        dst_chip = (jnp.bitwise_xor(x, c), jnp.bitwise_xor(y, 1 - c))
        return remote(k, 2, (*src_chip, c), (*dst_chip, c))

    def passed(k, idx):
        return remote(k, 3 + idx, (*chips[idx], c), sibling)

    def mine(k):
        return pltpu.make_async_copy(ins[k], region(k, x, y, None), local_sems.at[k])

    def start():
        for k in range(n):
            mine(k).start()
            for idx in range(2 if shards[k][2] else 3):
                first(k, idx).start()

    def relay_on():
        for k in range(n):
            split = shards[k][2]
            for idx in range(2):
                remote(k, idx, (*chips[idx], c if split else None), me).wait_recv()
            if split:
                relay(k).start()
                passed(k, 0).start()
                passed(k, 1).start()

    def finish():
        for k in range(n):
            split = shards[k][2]
            remote(k, 2, (*chips[2], c if split else None), me).wait_recv()
            if split:
                passed(k, 2).start()
        for k in range(n):
            if shards[k][2]:
                for idx in range(3):
                    remote(k, 3 + idx, (*chips[idx], 1 - c), me).wait_recv()
        for k in range(n):
            if shards[k][2]:
                for cp in (first(k, 0), first(k, 1), relay(k), passed(k, 0), passed(k, 1), passed(k, 2)):
                    cp.wait_send()
            else:
                for idx in range(3):
                    first(k, idx).wait_send()
            mine(k).wait()

    return start, relay_on, finish


RS_ADD_ROWS = (64, 32, 16, 8)


N_DEV = 2 * N_CHIPS


def _all_reduce_scratch(shape):
    return [pltpu.VMEM((N_DEV,) + tuple(shape), F32), pltpu.SemaphoreType.DMA((N_DEV - 1,)),
            pltpu.SemaphoreType.DMA((N_DEV - 1,))]


def _all_reduce_tile(v_ref, o_ref, slots, send_sems, recv_sems):
    flips = [(dx, dy, dc) for dx in (0, 1) for dy in (0, 1) for dc in (0, 1)][1:]
    x, y, c = _place()
    mine = 4 * x + 2 * y + c

    def copy(k, to_flip, slot):
        dx, dy, dc = to_flip
        peer = (jnp.bitwise_xor(x, dx), jnp.bitwise_xor(y, dy), jnp.bitwise_xor(c, dc))
        return pltpu.make_async_remote_copy(
            src_ref=v_ref, dst_ref=slots.at[slot], send_sem=send_sems.at[k], recv_sem=recv_sems.at[k],
            device_id=peer, device_id_type=MESH)

    sends = [copy(k, flip, mine) for k, flip in enumerate(flips)]
    for cp in sends:
        cp.start()
    slots[mine] = v_ref[...]
    for k, (dx, dy, dc) in enumerate(flips):
        copy(k, (dx, dy, dc), jnp.bitwise_xor(mine, 4 * dx + 2 * dy + dc)).wait_recv()
    total = slots[0]
    for d in range(1, N_DEV):
        total = total + slots[d]
    o_ref[...] = total
    for cp in sends:
        cp.wait_send()


RS_SEMS = 8
RS_LOCAL_SEMS = 5


def _rs_piece_shape(part):
    arr, cols = part[0], part[1]
    return (arr.shape[0] // 2, arr.shape[1] // N_CHIPS) if cols else tuple(arr.shape[1:])


def _rs_operands(parts):
    return [p[0] for p in parts] + [p[0] if p[2] is None else p[2] for p in parts]


def _rs_wires(parts, wire):
    return list(wire) if isinstance(wire, (list, tuple)) else [wire] * len(parts)


def _rs_shapes(parts, wire):
    n = len(parts)
    shapes = [_rs_piece_shape(p) for p in parts]
    out_shape = [jax.ShapeDtypeStruct((2,) + s, F32) for s in shapes]
    scratch = []
    for lead, kind in ((N_CHIPS, "f32"), (N_CHIPS, "narrow"), (N_CHIPS, "wire"), (None, "f32"), (N_CHIPS, "wire")):
        for s, p, w in zip(shapes, parts, _rs_wires(parts, wire)):
            dtype = {"f32": F32, "narrow": F32 if p[2] is None else p[2].dtype, "wire": w}[kind]
            scratch.append(pltpu.VMEM(s if lead is None else (lead,) + s, dtype))
    scratch += [pltpu.SemaphoreType.DMA((n * RS_SEMS,)), pltpu.SemaphoreType.DMA((n * RS_SEMS,)),
                pltpu.SemaphoreType.DMA((n * RS_LOCAL_SEMS,))]
    return out_shape, scratch


def _rs_steps(parts, ins, outs, scratch):
    n = len(parts)
    own, sib, got, fin, snd = (scratch[k * n:(k + 1) * n] for k in range(5))
    send_sems, recv_sems, local_sems = scratch[5 * n:]
    shapes = [_rs_piece_shape(p) for p in parts]
    x, y, c = _place()
    j_me = 2 * x + y
    me, sibling = (x, y, c), (x, y, 1 - c)

    def piece(a, jj, core, narrow=False):
        ref = ins[n + a] if narrow else ins[a]
        if parts[a][1]:
            r, cl = shapes[a]
            return ref.at[pl.ds(core * r, r), pl.ds(jj * cl, cl)]
        return ref.at[2 * jj + core]

    def remote(a, sem, src, dst, to):
        return pltpu.make_async_remote_copy(
            src_ref=src, dst_ref=dst, send_sem=send_sems.at[a * RS_SEMS + sem],
            recv_sem=recv_sems.at[a * RS_SEMS + sem], device_id=to, device_id_type=MESH)

    def rows_loop(a, fn):
        r = shapes[a][0]
        step = max(s for s in RS_ADD_ROWS if r % s == 0)

        def it(i, carry):
            fn(pl.ds(pl.multiple_of(i * step, step), step))
            return carry

        lax.fori_loop(0, r // step, it, 0)

    def load(a, jj):
        return pltpu.make_async_copy(piece(a, jj, c), own[a].at[jj], local_sems.at[a * RS_LOCAL_SEMS + jj])

    def to_sibling(a, jj):
        return remote(a, jj, piece(a, jj, 1 - c, narrow=True), sib[a].at[jj], sibling)

    near = (jnp.bitwise_xor(x, 1 - c), jnp.bitwise_xor(y, c))
    far = (jnp.bitwise_xor(x, c), jnp.bitwise_xor(y, 1 - c))
    diag = (1 - x, 1 - y)
    FROM_NEAR, FROM_FAR, FEED = 0, 1, 2

    def chip_of(chip):
        return 2 * chip[0] + chip[1]

    def feed(a):
        return remote(a, 4, snd[a].at[chip_of(diag)], got[a].at[FEED], (*near, c))

    def to_near(a):
        return remote(a, 5, snd[a].at[chip_of(near)], got[a].at[FROM_NEAR], (*near, c))

    def to_far(a):
        return remote(a, 6, snd[a].at[chip_of(far)], got[a].at[FROM_FAR], (*far, c))

    def store(a):
        return pltpu.make_async_copy(fin[a], outs[a].at[c], local_sems.at[a * RS_LOCAL_SEMS + 4])

    def result_to_sibling(a):
        return remote(a, 7, fin[a], outs[a].at[c], sibling)

    def exchange():
        for a in range(n):
            for jj in range(N_CHIPS):
                load(a, jj).start()
                to_sibling(a, jj).start()

    def chip_sums():
        for a in range(n):
            for jj in range(N_CHIPS):
                load(a, jj).wait()
                remote(a, jj, sib[a].at[jj], sib[a].at[jj], me).wait_recv()

                def add(sl, a=a, jj=jj):
                    q = own[a][jj, sl, :] + sib[a][jj, sl, :].astype(F32)
                    own[a][jj, sl, :] = q
                    snd[a][jj, sl, :] = q.astype(snd[a].dtype)

                rows_loop(a, add)
        for a in range(n):
            feed(a).start()
        for a in range(n):
            to_near(a).start()

    def relay():
        for a in range(n):
            remote(a, 4, got[a].at[FEED], got[a].at[FEED], me).wait_recv()

            def add(sl, a=a):
                pair = own[a][chip_of(far), sl, :] + got[a][FEED, sl, :].astype(F32)
                snd[a][chip_of(far), sl, :] = pair.astype(snd[a].dtype)

            rows_loop(a, add)
            to_far(a).start()

    def totals():
        for a in range(n):
            remote(a, 5, got[a].at[FROM_NEAR], got[a].at[FROM_NEAR], me).wait_recv()
            remote(a, 6, got[a].at[FROM_FAR], got[a].at[FROM_FAR], me).wait_recv()

            def total(sl, a=a):
                fin[a][sl, :] = (own[a][j_me, sl, :] + got[a][FROM_NEAR, sl, :].astype(F32)) + (
                    got[a][FROM_FAR, sl, :].astype(F32))

            rows_loop(a, total)
            store(a).start()
            result_to_sibling(a).start()

    def finish():
        for a in range(n):
            remote(a, 7, outs[a].at[1 - c], outs[a].at[1 - c], me).wait_recv()
        for a in range(n):
            for jj in range(N_CHIPS):
                to_sibling(a, jj).wait_send()
            for cp in (feed(a), to_near(a), to_far(a), result_to_sibling(a)):
                cp.wait_send()
            store(a).wait()

    return exchange, chip_sums, relay, totals, finish


def _rms(x):
    r = lax.rsqrt(jnp.mean(x * x, axis=-1, keepdims=True) + EPS)
    return x * r, r


def _rms_bwd(dxn, xn, r):
    return r * (dxn - xn * jnp.mean(dxn * xn, axis=-1, keepdims=True))


def _in_proj_gather(x2d, norm_g, w_in_sh, shards, tb, casts):
    t = x2d.shape[0]
    nb = t // tb
    cols = IN_COLS // N_CHIPS
    half = D_MODEL // 2
    n = len(shards)
    nc = len(casts)

    def body(x_ref, g_ref, win_ref, *refs):
        ins, cast_ins = refs[:n], refs[n:n + nc]
        z_ref, h_ref, wfull_ref = refs[n + nc:n + nc + 3]
        outs, cast_outs = refs[n + nc + 3:2 * n + nc + 3], refs[2 * n + nc + 3:2 * (n + nc) + 3]
        scratch = refs[2 * (n + nc) + 3:]
        wv, h_all, send_sems, recv_sems, local_sems, w_send, w_recv, w_local, stage = scratch[:9]
        wide, narrow, cast_sems = scratch[9:9 + nc], scratch[9 + nc:9 + 2 * nc], scratch[9 + 2 * nc]
        z_ring, z_sems = scratch[10 + 2 * nc:]
        s, i = pl.program_id(0), pl.program_id(1)
        x, y, c = _place()
        me, sibling = (x, y, c), (x, y, 1 - c)
        chips = [(x, 1 - y), (1 - x, y), (1 - x, 1 - y)]

        def w_half(cx, cy, hc):
            return wv.at[2 * cx + cy, pl.ds(hc * half, half), :]

        def w_remote(sem, block, to, src=None):
            dst = w_half(*block)
            return pltpu.make_async_remote_copy(
                src_ref=dst if src is None else src, dst_ref=dst, send_sem=w_send.at[sem],
                recv_sem=w_recv.at[sem], device_id=to, device_id_type=MESH)

        def w_first(idx):
            return w_remote(idx, (x, y, c), (*chips[idx], c))

        def w_relay():
            src_chip = (jnp.bitwise_xor(x, 1 - c), jnp.bitwise_xor(y, c))
            dst_chip = (jnp.bitwise_xor(x, c), jnp.bitwise_xor(y, 1 - c))
            return w_remote(2, (*src_chip, c), (*dst_chip, c))

        def w_pass(idx):
            return w_remote(3 + idx, (*chips[idx], c), sibling)

        def w_store(k, cx, cy):
            jj = 2 * cx + cy
            return pltpu.make_async_copy(wv.at[jj], wfull_ref.at[:, pl.ds(jj * cols, cols)], w_local.at[k])

        start_rest, relay_rest, finish_rest = _gather_steps(shards, ins, outs, send_sems, recv_sems, local_sems)

        def own(k, hc):
            return pltpu.make_async_copy(win_ref.at[pl.ds(pl.multiple_of(hc * half, half), half), :], stage.at[k],
                                         w_local.at[4 + 2 * k])

        def round_own(k, hc):
            own(k, hc).wait()
            wv[2 * x + y, pl.ds(pl.multiple_of(hc * half, half), half), :] = stage[k].astype(BF16)

        wide_in = [pltpu.make_async_copy(cast_ins[k], wide[k], cast_sems.at[k]) for k in range(nc)]
        narrow_out = [pltpu.make_async_copy(narrow[k], cast_outs[k], cast_sems.at[nc + k]) for k in range(nc)]

        @pl.when((s == 0) & (i == 0))
        def _():
            own(0, c).start()
            own(1, 1 - c).start()
            for cp in wide_in:
                cp.start()
            round_own(0, c)
            w_first(0).start()
            w_first(1).start()
            start_rest()
            round_own(1, 1 - c)
            w_store(0, x, y).start()

        @pl.when((s == 1) & (i == 0))
        def _():
            for k in range(nc):
                wide_in[k].wait()
                narrow[k][...] = wide[k][...].astype(BF16)
                narrow_out[k].start()
            w_remote(0, (*chips[0], c), me).wait_recv()
            w_remote(1, (*chips[1], c), me).wait_recv()
            w_relay().start()
            w_pass(0).start()
            w_pass(1).start()
            w_remote(3, (*chips[0], 1 - c), me).wait_recv()
            w_store(1, *chips[0]).start()

        @pl.when((s == 2) & (i == 0))
        def _():
            w_remote(4, (*chips[1], 1 - c), me).wait_recv()
            w_store(2, *chips[1]).start()

        @pl.when((s == 3) & (i == 0))
        def _():
            w_remote(2, (*chips[2], c), me).wait_recv()
            w_pass(2).start()
            w_remote(5, (*chips[2], 1 - c), me).wait_recv()
            w_store(3, *chips[2]).start()

        keep_h = pltpu.make_async_copy(h_all.at[i], h_ref.at[pl.ds(pl.multiple_of(i * tb, tb), tb), :], w_local.at[5])

        @pl.when(s == 0)
        def _():
            xn, _ = _rms(x_ref[...])
            h_all[i] = (xn * g_ref[...]).astype(BF16)
            keep_h.start()

        at = s * nb + i
        slot = at % Z_RING
        shard = jnp.bitwise_xor(2 * x + y, s)
        z_out = pltpu.make_async_copy(
            z_ring.at[slot], z_ref.at[pl.ds(pl.multiple_of(i * tb, tb), tb), pl.ds(pl.multiple_of(shard * cols, cols), cols)],
            z_sems.at[slot])
        pl.when(at >= Z_RING)(z_out.wait)
        z_ring[slot] = _dot(h_all[i], wv[shard])
        z_out.start(priority=1)
        pl.when(s == 0)(keep_h.wait)

        @pl.when((s == N_CHIPS - 1) & (i == nb - 1))
        def _():
            relay_rest()
            finish_rest()
            for cp in (w_first(0), w_first(1), w_relay(), w_pass(0), w_pass(1), w_pass(2)):
                cp.wait_send()
            w_store(0, x, y).wait()
            for idx in range(3):
                w_store(idx + 1, *chips[idx]).wait()
            for cp in narrow_out:
                cp.wait()
            for k in range(Z_RING):
                pltpu.make_async_copy(z_ring.at[k], z_ref.at[pl.ds(0, tb), pl.ds(0, cols)], z_sems.at[k]).wait()

    assert N_CHIPS * nb >= Z_RING
    rest_shape, rest_sems = _gather_shapes(shards)
    out_shape = [jax.ShapeDtypeStruct((t, IN_COLS), F32), jax.ShapeDtypeStruct((t, D_MODEL), BF16),
                 jax.ShapeDtypeStruct((D_MODEL, IN_COLS), BF16)] + rest_shape
    out_shape += [jax.ShapeDtypeStruct(a.shape, BF16) for a in casts]
    any_spec = pl.BlockSpec(memory_space=pl.ANY)

    return pl.pallas_call(
        body, name="in_proj", out_shape=tuple(out_shape),
        grid=(N_CHIPS, nb),
        in_specs=[pl.BlockSpec((tb, D_MODEL), lambda s, i: (jnp.where(s == 0, i, nb - 1), 0)),
                  pl.BlockSpec((1, D_MODEL), lambda s, i: (0, 0)), any_spec] + [any_spec] * (n + nc),
        out_specs=tuple([any_spec] * (3 + n + nc)),
        scratch_shapes=[pltpu.VMEM((N_CHIPS, D_MODEL, cols), BF16), pltpu.VMEM((nb, tb, D_MODEL), BF16)] + rest_sems + [
            pltpu.SemaphoreType.DMA((GATHER_SEMS,)), pltpu.SemaphoreType.DMA((GATHER_SEMS,)),
            pltpu.SemaphoreType.DMA((N_CHIPS + 3,)), pltpu.VMEM((2, half, cols), F32)]
        + [pltpu.VMEM(a.shape, F32) for a in casts] + [pltpu.VMEM(a.shape, BF16) for a in casts]
        + [pltpu.SemaphoreType.DMA((2 * nc,)), pltpu.VMEM((Z_RING, tb, cols), F32), pltpu.SemaphoreType.DMA((Z_RING,))],
        compiler_params=pltpu.CompilerParams(dimension_semantics=("arbitrary", "arbitrary"),
                                             vmem_limit_bytes=VMEM_LIMIT_BYTES),
    )(x2d, norm_g, w_in_sh, *[sh[0] for sh in shards], *casts)


def _in_proj_bwd(dz, w_in, x2d, dx_res, norm_g, tb, reduce, shards, take):
    t = x2d.shape[0]
    nb = t // tb
    parts, wire, steps = reduce
    n = len(parts)
    k = len(shards)
    take_rows, take_width = take

    def body(dz_ref, w_ref, x_ref, dres_ref, g_ref, *refs):
        at = 2 * n + k
        dx_ref, dg_ref = refs[at:at + 2]
        rs_outs, g_outs = refs[at + 2:at + 2 + n], refs[at + 2 + n:at + 2 + n + k]
        cut_ref = refs[at + 2 + n + k]
        scratch = refs[at + 3 + n + k:]
        rs_scr, g_sems, dg_acc, ar_scr, cut_sem = scratch[:-8], scratch[-8:-5], scratch[-5], scratch[-4:-1], scratch[-1]
        rs = _rs_steps(parts, refs[:2 * n], rs_outs, rs_scr)
        for step, when in zip(rs[:-1], steps):
            pl.when(pl.program_id(0) == when)(step)
        gather = _gather_steps(shards, refs[2 * n:at], g_outs, *g_sems)
        for step, when in zip(gather, (0, nb // 2, nb - 1)):
            pl.when(pl.program_id(0) == when)(step)

        @pl.when(pl.program_id(0) == 0)
        def _():
            dg_acc[...] = jnp.zeros_like(dg_acc)

        xn, r = _rms(x_ref[...])
        g = g_ref[...]
        dh = _dot_nt(dz_ref[...], w_ref[...])
        dg_acc[0:1, :] += jnp.sum(dh * xn, axis=0, keepdims=True)
        dx_ref[...] = dres_ref[...] + _rms_bwd(dh * g, xn, r)

        @pl.when(pl.program_id(0) == nb - 1)
        def _():
            x, y, _ = _place()
            mine = pl.ds(pl.multiple_of((2 * x + y) * take_width, take_width), take_width)
            cut = pltpu.make_async_copy(g_outs[0].at[take_rows, mine], cut_ref, cut_sem)
            cut.start()
            _all_reduce_tile(dg_acc, dg_ref, *ar_scr)
            rs[-1]()
            cut.wait()

    row = lambda i: (i, 0)
    fixed = lambda i: (0, 0)
    rs_shape, rs_scratch = _rs_shapes(parts, wire)
    g_shape, g_sems = _gather_shapes(shards)
    any_spec = pl.BlockSpec(memory_space=pl.ANY)
    cut_shape = jax.ShapeDtypeStruct((take_rows.stop - take_rows.start, take_width), F32)
    return pl.pallas_call(
        body, name="in_proj_bwd",
        out_shape=tuple([jax.ShapeDtypeStruct((t, D_MODEL), F32), jax.ShapeDtypeStruct((F32_SUBLANES, D_MODEL), F32)]
                        + rs_shape + g_shape + [cut_shape]),
        grid=(nb,),
        in_specs=[pl.BlockSpec((tb, IN_COLS), row),
                  pl.BlockSpec((D_MODEL, IN_COLS), fixed, pipeline_mode=pl.Buffered(1)),
                  pl.BlockSpec((tb, D_MODEL), row), pl.BlockSpec((tb, D_MODEL), row),
                  pl.BlockSpec((1, D_MODEL), fixed)] + [any_spec] * (2 * n + k),
        out_specs=tuple([pl.BlockSpec((tb, D_MODEL), row), pl.BlockSpec((F32_SUBLANES, D_MODEL), fixed)]
                        + [any_spec] * (n + k + 1)),
        scratch_shapes=rs_scratch + g_sems + [pltpu.VMEM((F32_SUBLANES, D_MODEL), F32)] + _all_reduce_scratch(
            (F32_SUBLANES, D_MODEL)) + [pltpu.SemaphoreType.DMA(())],
        compiler_params=pltpu.CompilerParams(dimension_semantics=("arbitrary",),
                                             vmem_limit_bytes=VMEM_LIMIT_BYTES),
    )(dz, w_in, x2d, dx_res, norm_g, *_rs_operands(parts), *[sh[0] for sh in shards])


def _weight_grad(pairs, n_chunks, tb, name, reduce=None):
    t = pairs[0][0].shape[0]
    nb = t // tb
    m = len(pairs)
    parts, wire, steps = reduce if reduce is not None else ([], F32, ())
    n = len(parts)

    def body(*refs):
        lr, refs = refs[:2 * m], refs[2 * m:]
        o_refs = refs[2 * n:2 * n + 2 * m]
        if n:
            at = pl.program_id(0) * nb + pl.program_id(1)
            rs = _rs_steps(parts, refs[:2 * n], refs[2 * n + 2 * m:3 * n + 2 * m], refs[3 * n + 2 * m:])
            for step, when in zip(rs, steps):
                pl.when(at == when)(step)

        @pl.when(pl.program_id(1) == 0)
        def _():
            for q in range(m):
                o_refs[2 * q][...] = jnp.zeros_like(o_refs[2 * q])

        for q in range(m):
            o_refs[2 * q][...] += _dot_tn(lr[2 * q][...], lr[2 * q + 1][...])

        @pl.when(pl.program_id(1) == nb - 1)
        def _():
            for q in range(m):
                o_refs[2 * q + 1][...] = o_refs[2 * q][...].astype(BF16)

    rs_shape, rs_scratch = _rs_shapes(parts, wire) if n else ([], [])
    any_spec = pl.BlockSpec(memory_space=pl.ANY)
    in_specs, out_specs, out_shape = [], [], []
    for lhs, rhs in pairs:
        k, nc = lhs.shape[1], rhs.shape[1] // n_chunks
        in_specs += [pl.BlockSpec((tb, k), lambda j, i: (i, 0)), pl.BlockSpec((tb, nc), lambda j, i: (i, j))]
        out_specs += [pl.BlockSpec((None, k, nc), lambda j, i: (j, 0, 0))] * 2
        out_shape += [jax.ShapeDtypeStruct((n_chunks, k, nc), F32), jax.ShapeDtypeStruct((n_chunks, k, nc), BF16)]
    return pl.pallas_call(
        body, name=name, out_shape=tuple(out_shape + rs_shape),
        grid=(n_chunks, nb),
        in_specs=in_specs + [any_spec] * (2 * n),
        out_specs=tuple(out_specs + [any_spec] * n),
        scratch_shapes=rs_scratch,
        compiler_params=pltpu.CompilerParams(dimension_semantics=("arbitrary", "arbitrary"),
                                             vmem_limit_bytes=VMEM_LIMIT_BYTES),
    )(*[a for pair in pairs for a in pair], *_rs_operands(parts))


def _adam_update(w, g, m, v):
    m_ = ADAM_B1 * m + (1.0 - ADAM_B1) * g
    v_ = ADAM_B2 * v + (1.0 - ADAM_B2) * jnp.square(g)
    m_hat = m_ / (1.0 - ADAM_B1 ** ADAM_STEP)
    v_hat = v_ / (1.0 - ADAM_B2 ** ADAM_STEP)
    return -ADAM_LR * (m_hat / (jnp.sqrt(v_hat) + ADAM_EPS) + ADAM_WD * w), m_, v_


def _adamw_replicated(vec_sum, mat_sum, norm_grad, entries, conv):
    n = len(entries)

    def grad_of(name, shape, vec_ref, mat_ref, norm_ref):
        if name == "norm_g":
            return norm_ref[0:1, :]
        if name in MAT_BAG_AT:
            return mat_ref[MAT_BAG_AT[name]:MAT_BAG_AT[name] + shape[0], :]
        if shape[0] == 1:
            return vec_ref[_bag_row(name), 0:shape[1]]
        return jnp.concatenate([vec_ref[_bag_row(name), h * shape[1]:(h + 1) * shape[1]] for h in range(shape[0])],
                               axis=0)

    def body(vec_ref, mat_ref, norm_ref, *refs):
        ins, outs = refs[:3 * n + 4], refs[3 * n + 4:]
        for k in range(n):
            w_ref, m_ref, v_ref = ins[3 * k:3 * k + 3]
            g = grad_of(entries[k][0], w_ref.shape, vec_ref, mat_ref, norm_ref)
            d, m_, v_ = _adam_update(w_ref[...], g, m_ref[...], v_ref[...])
            for ref, val in zip(outs[4 * k:4 * k + 4], (g, d, m_, v_)):
                ref[...] = val
        w_ref, m_ref, v_ref, g_ref = ins[3 * n:]
        g = g_ref[0:w_ref.shape[0], :]
        for ref, val in zip(outs[4 * n:4 * n + 4], (g,) + _adam_update(w_ref[...], g, m_ref[...], v_ref[...])):
            ref[...] = val
        outs[4 * n + 4][...] = vec_ref[_bag_row("loss"), 0:1]

    arrays = [a for e in entries for a in e[1:]] + list(conv)
    out_shape = [jax.ShapeDtypeStruct(e[1].shape, F32) for e in entries for _ in range(4)]
    out_shape += [jax.ShapeDtypeStruct(conv[0].shape, F32)] * 4 + [jax.ShapeDtypeStruct((1, 1), F32)]
    return pl.pallas_call(
        body, name="adamw_replicated", out_shape=tuple(out_shape),
        compiler_params=pltpu.CompilerParams(vmem_limit_bytes=VMEM_LIMIT_BYTES),
    )(vec_sum, mat_sum, norm_grad, *arrays)


def _adamw_group(items, name):
    arrays = [a for item in items for a in item[:4]]
    n = len(arrays)
    blocks = []
    for k, item in enumerate(items):
        rows = item[0].shape[0] // item[4]
        blocks += [(k, slice(q * rows, (q + 1) * rows)) for q in range(item[4])]

    def body(*refs):
        ins, outs, bufs = refs[:n], refs[n:2 * n], refs[2 * n:3 * n]
        load_sems, store_sems = refs[3 * n:]

        def copies(src, dst, sems):
            return [[pltpu.make_async_copy(src[4 * k + j].at[rows], dst[4 * k + j].at[rows], sems.at[4 * b + j])
                     for j in range(4)] for b, (k, rows) in enumerate(blocks)]

        loads, stores = copies(ins, bufs, load_sems), copies(bufs, outs, store_sems)
        for cp in [cp for block in loads for cp in block]:
            cp.start()
        for b, (k, rows) in enumerate(blocks):
            for cp in loads[b]:
                cp.wait()
            w_buf, g_buf, m_buf, v_buf = bufs[4 * k:4 * k + 4]
            w_buf[rows, :], m_buf[rows, :], v_buf[rows, :] = _adam_update(
                w_buf[rows, :], g_buf[rows, :], m_buf[rows, :], v_buf[rows, :])
            for cp in stores[b]:
                cp.start()
        for cp in [cp for block in stores for cp in block]:
            cp.wait()

    any_spec = pl.BlockSpec(memory_space=pl.ANY)
    flat = pl.pallas_call(
        body, name=name, out_shape=tuple(jax.ShapeDtypeStruct(a.shape, F32) for a in arrays),
        in_specs=[any_spec] * n, out_specs=(any_spec,) * n,
        scratch_shapes=[pltpu.VMEM(a.shape, F32) for a in arrays] + [pltpu.SemaphoreType.DMA((4 * len(blocks),))] * 2,
        compiler_params=pltpu.CompilerParams(vmem_limit_bytes=VMEM_LIMIT_BYTES),
    )(*arrays)
    return [(flat[4 * k + 1], flat[4 * k], flat[4 * k + 2], flat[4 * k + 3]) for k in range(len(items))]


def _shift_down(ext, s):
    return pltpu.roll(ext, s, 0)


def _tile_shift(v, s):
    rows, cols = v.shape
    tiles = v.reshape(rows // F32_SUBLANES, F32_SUBLANES, cols)
    return pltpu.roll(tiles, s % F32_SUBLANES, 1).reshape(rows, cols)


def _shift_up(ext, s):
    return pltpu.roll(ext, ext.shape[0] - s, 0)


def _lru_gates(xc, wa_ref, ba, wx_ref, bx, lam):
    pa, px = [], []
    for h in range(LRU_HEADS):
        xh = xc[:, h * HEAD_DIM:(h + 1) * HEAD_DIM].astype(BF16)
        pa.append(_dot(xh, wa_ref[h]))
        px.append(_dot(xh, wx_ref[h]))
    r = _sigmoid(jnp.concatenate(pa, axis=1) + ba)
    ig = _sigmoid(jnp.concatenate(px, axis=1) + bx)
    sp = _softplus(-lam)
    log_a = (-LRU_C * r) * sp
    a = jnp.exp(log_a)
    mult = jnp.sqrt(jnp.tanh(-log_a) * (1.0 + a * a))
    return r, ig, a, mult, sp


def _conv(ext, w_ref, b):
    y = b + _shift_down(ext, 3) * w_ref[0:1, :]
    y = y + _shift_down(ext, 2) * w_ref[1:2, :]
    y = y + _shift_down(ext, 1) * w_ref[2:3, :]
    y = y + ext * w_ref[3:4, :]
    return y[CONV_HIST:, :]


def _pool_diff(ext, pos):
    out = []
    for g, k in enumerate(POOL_WINDOWS):
        col = ext[:, g * POOL_GROUP_DIM:(g + 1) * POOL_GROUP_DIM]
        s = col
        for step in range(g + 1):
            s = s + _shift_down(s, 2 ** step)
        count = jnp.minimum(pos + 1, k).astype(F32)
        out.append(s[POOL_HIST:, :] / count - col[POOL_HIST:, :])
    return out


def _pool_mix(diff, pw_ref):
    return jnp.concatenate([_dot(diff[g].astype(BF16), pw_ref[g]) for g in range(len(POOL_WINDOWS))], axis=1)


def _branch_specs(tb, row_map, fixed):
    fixed3 = lambda i: (0, 0, 0)
    return [pl.BlockSpec((CONV_WIDTH, D_MODEL), fixed), pl.BlockSpec((1, D_MODEL), fixed),
            pl.BlockSpec((LRU_HEADS, HEAD_DIM, HEAD_DIM), fixed3), pl.BlockSpec((1, D_MODEL), fixed),
            pl.BlockSpec((LRU_HEADS, HEAD_DIM, HEAD_DIM), fixed3), pl.BlockSpec((1, D_MODEL), fixed),
            pl.BlockSpec((1, D_MODEL), fixed),
            pl.BlockSpec((len(POOL_WINDOWS), POOL_GROUP_DIM, POOL_GROUP_DIM), fixed3),
            pl.BlockSpec((1, POOL_WIDTH), fixed)]


def _branches_fwd(z, weights, seq, tb, shards):
    t = z.shape[0]
    nb = t // tb
    nbe = seq // tb
    groups = tb // F32_SUBLANES
    n = len(shards)

    def body(xa_ref, ga_ref, xb_ref, gb_ref, cw_ref, cb_ref, wa_ref, ba_ref, wx_ref, bx_ref, lam_ref,
             pw_ref, ps_ref, *refs):
        g_ins = refs[:n]
        ya_ref, yb_ref, hl_ref = refs[n:n + 3]
        g_outs = refs[n + 3:2 * n + 3]
        xa_ext, xb_ext, carry, a_s, u_s, send_sems, recv_sems, local_sems = refs[2 * n + 3:]
        blk = pl.program_id(0) % nbe
        start_gather, relay_gather, finish_gather = _gather_steps(shards, g_ins, g_outs, send_sems, recv_sems,
                                                                  local_sems)
        pl.when(pl.program_id(0) == 0)(start_gather)
        pl.when(pl.program_id(0) == nb // 2)(relay_gather)

        @pl.when(blk == 0)
        def _():
            xa_ext[0:CONV_HIST, :] = jnp.zeros((CONV_HIST, D_MODEL), F32)
            xb_ext[0:POOL_HIST, :] = jnp.zeros((POOL_HIST, POOL_WIDTH), F32)
            carry[...] = jnp.zeros_like(carry)

        xa_ext[CONV_HIST:, :] = xa_ref[...]
        xb_ext[POOL_HIST:, :] = xb_ref[...]
        ea = xa_ext[...]
        eb = xb_ext[...]
        xa_ext[0:CONV_HIST, :] = ea[tb:, :]
        xb_ext[0:POOL_HIST, :] = eb[tb:, :]

        xc = _conv(ea, cw_ref, cb_ref[...])
        _, ig, a, mult, _ = _lru_gates(xc, wa_ref, ba_ref[...], wx_ref, bx_ref[...], lam_ref[...])
        u = mult * (ig * xc)
        row8 = lax.broadcasted_iota(jnp.int32, (tb, D_MODEL), 0) % F32_SUBLANES
        for s in (1, 2, 4):
            m = row8 >= s
            u = jnp.where(m, a * _tile_shift(u, s) + u, u)
            a = jnp.where(m, a * _tile_shift(a, s), a)
        a_s[...] = a
        u_s[...] = u

        def step(g, cr):
            sl = pl.ds(pl.multiple_of(g * F32_SUBLANES, F32_SUBLANES), F32_SUBLANES)
            hb = a_s[sl, :] * cr + u_s[sl, :]
            hl_ref[sl, :] = hb
            return jnp.broadcast_to(hb[F32_SUBLANES - 1:F32_SUBLANES, :], (F32_SUBLANES, D_MODEL))

        carry[...] = lax.fori_loop(0, groups, step, carry[...], unroll=True)
        ga = ga_ref[...]
        ya_ref[...] = (hl_ref[...] * (ga * _sigmoid(ga))).astype(BF16)

        pos = blk * tb + lax.broadcasted_iota(jnp.int32, (tb, POOL_GROUP_DIM), 0)
        ypre = _pool_mix(_pool_diff(eb, pos), pw_ref)
        gb = gb_ref[...]
        yb_ref[...] = ((ypre * ps_ref[...]) * (gb * _sigmoid(gb))).astype(BF16)
        pl.when(pl.program_id(0) == nb - 1)(finish_gather)

    row = lambda i: (i, 0)
    fixed = lambda i: (0, 0)
    any_spec = pl.BlockSpec(memory_space=pl.ANY)
    in_specs = [pl.BlockSpec((tb, D_MODEL), lambda i: (i, 0)), pl.BlockSpec((tb, D_MODEL), lambda i: (i, 1)),
                pl.BlockSpec((tb, POOL_WIDTH), lambda i: (i, 4)), pl.BlockSpec((tb, POOL_WIDTH), lambda i: (i, 5)),
                ] + _branch_specs(tb, row, fixed) + [any_spec] * n
    g_shape, g_sems = _gather_shapes(shards)
    return pl.pallas_call(
        body, name="branches_fwd",
        out_shape=tuple([jax.ShapeDtypeStruct((t, D_MODEL), BF16), jax.ShapeDtypeStruct((t, POOL_WIDTH), BF16),
                         jax.ShapeDtypeStruct((t, D_MODEL), F32)] + g_shape),
        grid=(nb,), in_specs=in_specs,
        out_specs=tuple([pl.BlockSpec((tb, D_MODEL), row), pl.BlockSpec((tb, POOL_WIDTH), row),
                         pl.BlockSpec((tb, D_MODEL), row)] + [any_spec] * n),
        scratch_shapes=[pltpu.VMEM((tb + CONV_HIST, D_MODEL), F32), pltpu.VMEM((tb + POOL_HIST, POOL_WIDTH), F32),
                        pltpu.VMEM((F32_SUBLANES, D_MODEL), F32),
                        pltpu.VMEM((tb, D_MODEL), F32), pltpu.VMEM((tb, D_MODEL), F32)] + g_sems,
        compiler_params=pltpu.CompilerParams(dimension_semantics=("arbitrary",),
                                             vmem_limit_bytes=VMEM_LIMIT_BYTES),
    )(z, z, z, z, *weights, *[sh[0] for sh in shards])


def _branches_bwd(z, hl, dya, dyb, dzm, weights, vec_bag, seq, tb, riders):
    t = z.shape[0]
    nb = t // tb
    nbe = seq // tb
    groups = tb // F32_SUBLANES
    nr = len(riders)

    def body(xa_ref, xap_ref, ga_ref, xb_ref, xbp_ref, gb_ref, hl_ref, hlp_ref, dya_ref, dyb_ref, dzm_ref,
             cw_ref, cb_ref, wa_ref, ba_ref, wx_ref, bx_ref, lam_ref, pw_ref, ps_ref, vec_in_ref, *rest):
        pairs, (dz_ref, vec_ref, mat_ref), grads = rest[:2 * nr], rest[2 * nr:2 * nr + 3], rest[2 * nr + 3:4 * nr + 3]
        xa_ext, xb_ext, hl_ext, a_ext, dxc_ext, dwin_ext, g_carry, b_s, d_s, g_s = rest[4 * nr + 3:]
        i = pl.program_id(0)
        blk = (nb - 1 - i) % nbe

        def mat_rows(name, k):
            at = MAT_BAG_AT[name] + k * HEAD_DIM
            return slice(at, at + HEAD_DIM)

        def rider(k):
            grads[2 * k][...] += _dot_tn(pairs[2 * k][...], pairs[2 * k + 1][...])

        @pl.when(i == 0)
        def _():
            vec_ref[...] = vec_in_ref[...]
            mat_ref[...] = jnp.zeros_like(mat_ref)
            for k in range(nr):
                grads[2 * k][...] = jnp.zeros_like(grads[2 * k])

        @pl.when(blk == nbe - 1)
        def _():
            a_ext[tb:, :] = jnp.zeros((F32_SUBLANES, D_MODEL), F32)
            dxc_ext[tb:, :] = jnp.zeros((CONV_HIST, D_MODEL), F32)
            dwin_ext[tb:, :] = jnp.zeros((POOL_HIST, POOL_WIDTH), F32)
            g_carry[...] = jnp.zeros_like(g_carry)

        live = (blk > 0).astype(F32)
        xa_ext[0:CONV_HIST, :] = xap_ref[...] * live
        xa_ext[CONV_HIST:, :] = xa_ref[...]
        xb_ext[0:POOL_HIST, :] = xbp_ref[...] * live
        xb_ext[POOL_HIST:, :] = xb_ref[...]
        hl_ext[0:F32_SUBLANES, :] = hlp_ref[...] * live
        hl_ext[F32_SUBLANES:, :] = hl_ref[...]
        ea = xa_ext[...]
        eb = xb_ext[...]
        rider(0)

        xc = _conv(ea, cw_ref, cb_ref[...])
        lam = lam_ref[...]
        r, ig, a, mult, sp = _lru_gates(xc, wa_ref, ba_ref[...], wx_ref, bx_ref[...], lam)
        hl = hl_ref[...]
        ga = ga_ref[...]
        sga = _sigmoid(ga)
        dya = dya_ref[...]
        dhl = dya * (ga * sga)
        dz_ref[:, D_MODEL:2 * D_MODEL] = (dya * hl * (sga * (1.0 + ga * (1.0 - sga)))).astype(BF16)

        a_ext[0:tb, :] = a
        b = _shift_up(a_ext[...], 1)[0:tb, :]
        a_ext[tb:, :] = jnp.broadcast_to(a[0:1, :], (F32_SUBLANES, D_MODEL))
        d = dhl
        row8 = lax.broadcasted_iota(jnp.int32, (tb, D_MODEL), 0) % F32_SUBLANES
        for s in (1, 2, 4):
            m = row8 < F32_SUBLANES - s
            d = jnp.where(m, d + b * _tile_shift(d, -s), d)
            b = jnp.where(m, b * _tile_shift(b, -s), b)
        b_s[...] = b
        d_s[...] = d

        def step(k, cr):
            sl = pl.ds(pl.multiple_of((groups - 1 - k) * F32_SUBLANES, F32_SUBLANES), F32_SUBLANES)
            gb_ = d_s[sl, :] + b_s[sl, :] * cr
            g_s[sl, :] = gb_
            return jnp.broadcast_to(gb_[0:1, :], (F32_SUBLANES, D_MODEL))

        g_carry[...] = lax.fori_loop(0, groups, step, g_carry[...], unroll=4)
        rider(1)
        gsc = g_s[...]
        da = gsc * _shift_down(hl_ext[...], 1)[F32_SUBLANES:, :]
        dmult = gsc * (ig * xc)
        dig = gsc * (mult * xc)
        dxc = gsc * (mult * ig)
        dlog_a = da * a - (a * a) * dmult / mult
        dr = dlog_a * (-LRU_C * sp)
        vec_ref[_bag_row("lru_lambda"), :] += jnp.sum(dlog_a * (-LRU_C * r), axis=0, keepdims=True)
        dpa = dr * (r * (1.0 - r))
        dpx = dig * (ig * (1.0 - ig))
        vec_ref[_bag_row("lru_b_a"), :] += jnp.sum(dpa, axis=0, keepdims=True)
        vec_ref[_bag_row("lru_b_x"), :] += jnp.sum(dpx, axis=0, keepdims=True)
        back = []
        for h in range(LRU_HEADS):
            cols = slice(h * HEAD_DIM, (h + 1) * HEAD_DIM)
            xh = xc[:, cols].astype(BF16)
            dpa_h = dpa[:, cols].astype(BF16)
            dpx_h = dpx[:, cols].astype(BF16)
            mat_ref[mat_rows("lru_w_a", h), :] += _dot_tn(xh, dpa_h)
            mat_ref[mat_rows("lru_w_x", h), :] += _dot_tn(xh, dpx_h)
            back.append(_dot_nt(dpa_h, wa_ref[h]) + _dot_nt(dpx_h, wx_ref[h]))
        dxc = dxc + jnp.concatenate(back, axis=1)
        vec_ref[_bag_row("conv_b"), :] += jnp.sum(dxc, axis=0, keepdims=True)
        for k in range(CONV_WIDTH):
            tap = _shift_down(ea, CONV_WIDTH - 1 - k)[CONV_HIST:, :] if k < CONV_WIDTH - 1 else ea[CONV_HIST:, :]
            vec_ref[_bag_row("conv_w", k), :] += jnp.sum(dxc * tap, axis=0, keepdims=True)
        dxc_ext[0:tb, :] = dxc
        ed = dxc_ext[...]
        dxa = ed * cw_ref[3:4, :]
        dxa = dxa + _shift_up(ed, 1) * cw_ref[2:3, :]
        dxa = dxa + _shift_up(ed, 2) * cw_ref[1:2, :]
        dxa = dxa + _shift_up(ed, 3) * cw_ref[0:1, :]
        dz_ref[:, 0:D_MODEL] = dxa[0:tb, :].astype(BF16)
        dxc_ext[tb:, :] = dxc[0:CONV_HIST, :]

        pos = blk * tb + lax.broadcasted_iota(jnp.int32, (tb, POOL_GROUP_DIM), 0)
        diff = _pool_diff(eb, pos)
        rider(2)
        ypre = _pool_mix(diff, pw_ref)
        ps = ps_ref[...]
        gb = gb_ref[...]
        sgb = _sigmoid(gb)
        dyb = dyb_ref[...]
        dyp = dyb * (gb * sgb)
        dz_ref[:, 2 * D_MODEL + POOL_WIDTH:3 * D_MODEL] = (
            dyb * (ypre * ps) * (sgb * (1.0 + gb * (1.0 - sgb)))).astype(BF16)
        vec_ref[_bag_row("pool_scale"), 0:POOL_WIDTH] += jnp.sum(dyp * ypre, axis=0, keepdims=True)
        dypre = dyp * ps
        for g, k in enumerate(POOL_WINDOWS):
            cols = slice(g * POOL_GROUP_DIM, (g + 1) * POOL_GROUP_DIM)
            dyg = dypre[:, cols].astype(BF16)
            mat_ref[mat_rows("pool_w", g), :] += _dot_tn(diff[g].astype(BF16), dyg)
            ddiff = _dot_nt(dyg, pw_ref[g])
            count = jnp.minimum(pos + 1, k).astype(F32)
            dwin = ddiff / count
            dwin_ext[0:tb, cols] = dwin
            s = dwin_ext[:, cols]
            for step_ in range(g + 1):
                s = s + _shift_up(s, 2 ** step_)
            dz_ref[:, 2 * D_MODEL + g * POOL_GROUP_DIM:2 * D_MODEL + (g + 1) * POOL_GROUP_DIM] = (
                s[0:tb, :] - ddiff).astype(BF16)
            dwin_ext[tb:, cols] = dwin[0:POOL_HIST, :]

        dz_ref[:, 3 * D_MODEL:] = dzm_ref[...]

        @pl.when(i == nb - 1)
        def _():
            row = _bag_row("lru_lambda")
            vec_ref[row, :] = vec_ref[row, :] * (-_sigmoid(-lam))
            for k in range(nr):
                grads[2 * k + 1][...] = grads[2 * k][...].astype(BF16)

    rev = lambda i: (nb - 1 - i, 0)
    fixed = lambda i: (0, 0)

    def prev(rows, col):
        per = tb // rows
        return lambda i: (jnp.maximum((nb - 1 - i) * per - 1, 0), col)

    in_specs = [pl.BlockSpec((tb, D_MODEL), lambda i: (nb - 1 - i, 0)),
                pl.BlockSpec((CONV_HIST, D_MODEL), prev(CONV_HIST, 0)),
                pl.BlockSpec((tb, D_MODEL), lambda i: (nb - 1 - i, 1)),
                pl.BlockSpec((tb, POOL_WIDTH), lambda i: (nb - 1 - i, 4)),
                pl.BlockSpec((POOL_HIST, POOL_WIDTH), prev(POOL_HIST, 4)),
                pl.BlockSpec((tb, POOL_WIDTH), lambda i: (nb - 1 - i, 5)),
                pl.BlockSpec((tb, D_MODEL), rev),
                pl.BlockSpec((F32_SUBLANES, D_MODEL), prev(F32_SUBLANES, 0)),
                pl.BlockSpec((tb, D_MODEL), rev), pl.BlockSpec((tb, POOL_WIDTH), rev),
                pl.BlockSpec((tb, 2 * D_MODEL), rev)] + _branch_specs(tb, rev, fixed) + [
                    pl.BlockSpec((VEC_BAG_ROWS, D_MODEL), fixed)]
    vec_at = len(in_specs) - 1
    out_shape = [jax.ShapeDtypeStruct((t, IN_COLS), BF16), jax.ShapeDtypeStruct((VEC_BAG_ROWS, D_MODEL), F32),
                 jax.ShapeDtypeStruct((MAT_BAG_ROWS, HEAD_DIM), F32)]
    out_specs = [pl.BlockSpec((tb, IN_COLS), rev), pl.BlockSpec((VEC_BAG_ROWS, D_MODEL), fixed),
                 pl.BlockSpec((MAT_BAG_ROWS, HEAD_DIM), fixed)]
    for lhs, rhs in riders:
        in_specs += [pl.BlockSpec((tb, lhs.shape[1]), rev), pl.BlockSpec((tb, rhs.shape[1]), rev)]
        grad = (lhs.shape[1], rhs.shape[1])
        out_shape += [jax.ShapeDtypeStruct(grad, F32), jax.ShapeDtypeStruct(grad, BF16)]
        out_specs += [pl.BlockSpec(grad, fixed)] * 2
    scratch = [pltpu.VMEM((tb + CONV_HIST, D_MODEL), F32), pltpu.VMEM((tb + POOL_HIST, POOL_WIDTH), F32),
               pltpu.VMEM((tb + F32_SUBLANES, D_MODEL), F32), pltpu.VMEM((tb + F32_SUBLANES, D_MODEL), F32),
               pltpu.VMEM((tb + CONV_HIST, D_MODEL), F32), pltpu.VMEM((tb + POOL_HIST, POOL_WIDTH), F32),
               pltpu.VMEM((F32_SUBLANES, D_MODEL), F32),
               pltpu.VMEM((tb, D_MODEL), F32), pltpu.VMEM((tb, D_MODEL), F32), pltpu.VMEM((tb, D_MODEL), F32)]
    return pl.pallas_call(
        body, name="branches_bwd", out_shape=tuple(out_shape), grid=(nb,), in_specs=in_specs,
        out_specs=tuple(out_specs), scratch_shapes=scratch, input_output_aliases={vec_at: 1},
        compiler_params=pltpu.CompilerParams(dimension_semantics=("arbitrary",),
                                             vmem_limit_bytes=VMEM_LIMIT_BYTES),
    )(z, z, z, z, z, z, hl, hl, dya, dyb, dzm, *weights, vec_bag, *[a for pair in riders for a in pair])


def _merge_head(x2d, ya, yb, z, p2d, tgt, w_pl, w_pp, w_out, w_pg, w_pe, g2, gf, tb):
    t = x2d.shape[0]
    p_dim = p2d.shape[1]

    def body(x_ref, ya_ref, yb_ref, ma_ref, mb_ref, p_ref, t_ref, wpl_ref, wpp_ref, wout_ref, wpg_ref, wpe_ref,
             g2_ref, gf_ref,
             bag_ref, dxr_ref, dya_ref, dyb_ref, dzm_ref,
             mg_ref, do_ref, hn_ref, dgp_ref, dpe_ref, da_ref, dbm_ref, pbf_ref):
        @pl.when(pl.program_id(0) == 0)
        def _():
            bag_ref[...] = jnp.zeros_like(bag_ref)

        a_ = _dot(ya_ref[...], wpl_ref[...])
        bm = _dot(yb_ref[...], wpp_ref[...])
        sa = _sigmoid(ma_ref[...])
        sb = _sigmoid(mb_ref[...])
        mg = (sa * a_ + sb * bm).astype(BF16)
        mg_ref[...] = mg
        x1 = x_ref[...] + _dot(mg, wout_ref[...])
        xn2, r2 = _rms(x1)
        g2 = g2_ref[...]
        hn = (xn2 * g2).astype(BF16)
        hn_ref[...] = hn
        gate = _sigmoid(_dot(hn, wpg_ref[...]))
        pbf = p_ref[...].astype(BF16)
        pbf_ref[...] = pbf
        pe = _dot(pbf, wpe_ref[...])
        x2 = x1 + gate * pe
        xn3, r3 = _rms(x2)
        gf = gf_ref[...]
        err = xn3 * gf - t_ref[...]
        bag_ref[_bag_rows("loss"), 0:128] += 0.5 * jnp.sum(jnp.mean(err * err, axis=-1))

        dy = err * (1.0 / D_MODEL)
        bag_ref[_bag_row("final_g"), :] += jnp.sum(dy * xn3, axis=0, keepdims=True)
        dx2 = _rms_bwd(dy * gf, xn3, r3)
        dpe_ref[...] = (dx2 * gate).astype(BF16)
        dgp = ((dx2 * pe) * (gate * (1.0 - gate))).astype(BF16)
        dgp_ref[...] = dgp
        dhn = _dot_nt(dgp, wpg_ref[...])
        bag_ref[_bag_row("ple_norm_g"), :] += jnp.sum(dhn * xn2, axis=0, keepdims=True)
        dx1 = dx2 + _rms_bwd(dhn * g2, xn2, r2)
        dxr_ref[...] = dx1
        do = dx1.astype(BF16)
        do_ref[...] = do
        dmg = _dot_nt(do, wout_ref[...])
        da = (dmg * sa).astype(BF16)
        dbm = (dmg * sb).astype(BF16)
        da_ref[...] = da
        dbm_ref[...] = dbm
        dzm_ref[:, 0:D_MODEL] = (dmg * a_ * (sa * (1.0 - sa))).astype(BF16)
        dzm_ref[:, D_MODEL:] = (dmg * bm * (sb * (1.0 - sb))).astype(BF16)
        dya_ref[...] = _dot_nt(da, wpl_ref[...])
        dyb_ref[...] = _dot_nt(dbm, wpp_ref[...])

    row = lambda i: (i, 0)
    fixed = lambda i: (0, 0)

    def resident(shape):
        return pl.BlockSpec(shape, fixed, pipeline_mode=pl.Buffered(1))

    tok = lambda width: pl.BlockSpec((tb, width), row)
    in_specs = [tok(D_MODEL), tok(D_MODEL), tok(POOL_WIDTH),
                pl.BlockSpec((tb, D_MODEL), lambda i: (i, 3)), pl.BlockSpec((tb, D_MODEL), lambda i: (i, 4)),
                tok(p_dim), tok(D_MODEL),
                resident((D_MODEL, D_MODEL)), resident((POOL_WIDTH, D_MODEL)), resident((D_MODEL, D_MODEL)),
                resident((D_MODEL, D_MODEL)), resident((p_dim, D_MODEL)),
                pl.BlockSpec((1, D_MODEL), fixed), pl.BlockSpec((1, D_MODEL), fixed)]
    bf = lambda width: jax.ShapeDtypeStruct((t, width), BF16)
    f32 = lambda width: jax.ShapeDtypeStruct((t, width), F32)
    out_shape = (jax.ShapeDtypeStruct((VEC_BAG_ROWS, D_MODEL), F32),
                 f32(D_MODEL), f32(D_MODEL), f32(POOL_WIDTH), bf(2 * D_MODEL),
                 bf(D_MODEL), bf(D_MODEL), bf(D_MODEL), bf(D_MODEL), bf(D_MODEL), bf(D_MODEL), bf(D_MODEL), bf(p_dim))
    out_specs = (pl.BlockSpec((VEC_BAG_ROWS, D_MODEL), fixed),
                 tok(D_MODEL), tok(D_MODEL), tok(POOL_WIDTH), tok(2 * D_MODEL),
                 tok(D_MODEL), tok(D_MODEL), tok(D_MODEL), tok(D_MODEL), tok(D_MODEL), tok(D_MODEL), tok(D_MODEL),
                 tok(p_dim))
    return pl.pallas_call(
        body, name="merge_head", out_shape=out_shape, grid=(t // tb,), in_specs=in_specs, out_specs=out_specs,
        compiler_params=pltpu.CompilerParams(dimension_semantics=("arbitrary",),
                                             vmem_limit_bytes=VMEM_LIMIT_BYTES),
    )(x2d, ya, yb, z, z, p2d, tgt, w_pl, w_pp, w_out, w_pg, w_pe, g2, gf)


def kernel(x, p, norm_g, w_in, conv_w, conv_b, lru_w_a, lru_b_a, lru_w_x, lru_b_x, lru_lambda, pool_w, pool_scale, w_proj_lru, w_proj_pool, w_out, ple_norm_g, w_ple_gate, w_ple_proj, final_g, loss_target, m_norm_g, m_w_in, m_conv_w, m_conv_b, m_lru_w_a, m_lru_b_a, m_lru_w_x, m_lru_b_x, m_lru_lambda, m_pool_w, m_pool_scale, m_w_proj_lru, m_w_proj_pool, m_w_out, m_ple_norm_g, m_w_ple_gate, m_w_ple_proj, m_final_g, v_norm_g, v_w_in, v_conv_w, v_conv_b, v_lru_w_a, v_lru_b_a, v_lru_w_x, v_lru_b_x, v_lru_lambda, v_pool_w, v_pool_scale, v_w_proj_lru, v_w_proj_pool, v_w_out, v_ple_norm_g, v_w_ple_gate, v_w_ple_proj, v_final_g):
    bsz, seq, _ = x.shape
    t = bsz * seq
    tb_mm = min(1024, seq)
    tb_seq = min(256, seq // 2) if seq >= 512 else seq
    x2d = x.reshape(t, D_MODEL)
    p2d = p.reshape(t, p.shape[-1])
    tgt = loss_target.reshape(t, D_MODEL)

    rest = [(w_proj_lru[0], 0), (w_proj_pool[0], 1), (w_out[0], 0), (w_ple_gate[0], 0), (w_ple_proj[0], 1)]
    z, h_bf, w_in_f, conv_w_f, *narrow = _in_proj_gather(
        x2d, norm_g, w_in[0], [(conv_w[0], 1, False)], tb_mm,
        [w for w, _ in rest] + [lru_w_a[0], lru_w_x[0], pool_w[0]])
    wa_bf, wx_bf, pw_bf = narrow[len(rest):]
    branch_w = (conv_w_f, conv_b, wa_bf, lru_b_a.reshape(1, D_MODEL), wx_bf, lru_b_x.reshape(1, D_MODEL),
                lru_lambda, pw_bf, pool_scale)

    ya, yb, hl, w_pl_f, w_pp_f, w_out_f, w_pg_f, w_pe_f = _branches_fwd(
        z, branch_w, seq, tb_seq, [(w16, axis, True) for w16, (_, axis) in zip(narrow, rest)])
    (vec_bag, dx_res, dya, dyb, dzm, mg_bf, do_bf, hn_bf, dgp_bf, dpe_bf, da_bf, dbm_bf, p_bf) = _merge_head(
        x2d, ya, yb, z, p2d, tgt, w_pl_f, w_pp_f, w_out_f, w_pg_f, w_pe_f, ple_norm_g, final_g.reshape(1, D_MODEL),
        tb_seq)
    dz, vec_bag, mat_bag, g_out, g_out16, g_pp, g_pp16, g_pe, g_pe16 = _branches_bwd(
        z, hl, dya, dyb, dzm, branch_w, vec_bag, seq, tb_seq, [(mg_bf, do_bf), (yb, dbm_bf), (p_bf, dpe_bf)])

    tb_dw = min(1024, seq)
    def row_pieces(g32, g16):
        pieces = (8, g32.shape[0] // 8, g32.shape[1])
        return g32.reshape(pieces), False, g16.reshape(pieces)

    g_pl, g_pl16, g_pg, g_pg16 = _weight_grad([(ya, da_bf), (hn_bf, dgp_bf)], 1, tb_dw, "dw_proj")
    p_dim = p2d.shape[1]
    proj_parts = [row_pieces(g_pl[0], g_pl16[0]), (g_pp, True, g_pp16), row_pieces(g_out, g_out16),
                  row_pieces(g_pg[0], g_pg16[0]), (g_pe, True, g_pe16)]
    nb_dw = t // tb_dw
    g_in, g_in16, r_pl, r_pp, r_out, r_pg, r_pe, vec_mine, mat_mine = _weight_grad(
        [(h_bf, dz)], N_CHIPS, tb_dw, "dw_in",
        reduce=(proj_parts + [(vec_bag.reshape(8, VEC_BAG_ROWS // 8, D_MODEL), False, None),
                              (mat_bag.reshape(8, MAT_BAG_ROWS // 8, HEAD_DIM), False, None)],
                [BF16] * 5 + [F32] * 2,
                (0, nb_dw // 2, 2 * nb_dw - 1, 3 * nb_dw + nb_dw // 2, N_CHIPS * nb_dw - 1)))
    pieces = (8, D_MODEL // 2, IN_COLS // N_CHIPS)
    nb_seq = t // tb_seq
    dx, g_g1, r_in, vec_sum, mat_sum, g_cw = _in_proj_bwd(
        dz, w_in_f, x2d, dx_res, norm_g, tb_seq,
        reduce=([(g_in.reshape(pieces), False, g_in16.reshape(pieces))], BF16,
                (0, nb_seq // 8, nb_seq // 2, nb_seq - 1)),
        shards=[(vec_mine.reshape(VEC_BAG_ROWS // N_CHIPS, D_MODEL), 0, True),
                (mat_mine.reshape(MAT_BAG_ROWS // N_CHIPS, HEAD_DIM), 0, True)],
        take=(_bag_rows("conv_w"), D_MODEL // N_CHIPS))

    big = [(w_in, r_in, m_w_in, v_w_in, 4), (w_proj_lru, r_pl, m_w_proj_lru, v_w_proj_lru, 1),
           (w_proj_pool, r_pp, m_w_proj_pool, v_w_proj_pool, 1), (w_out, r_out, m_w_out, v_w_out, 1),
           (w_ple_gate, r_pg, m_w_ple_gate, v_w_ple_gate, 1), (w_ple_proj, r_pe, m_w_ple_proj, v_w_ple_proj, 1)]
    u_in, u_pl, u_pp, u_out, u_pg, u_pe = [tuple(a[None] for a in u) for u in _adamw_group(
        [(w[0], g.reshape(w.shape[1:]), m[0], v[0], cuts) for w, g, m, v, cuts in big], "adamw_sharded")]

    small = [("norm_g", norm_g, m_norm_g, v_norm_g), ("conv_b", conv_b, m_conv_b, v_conv_b),
             ("lru_w_a", lru_w_a, m_lru_w_a, v_lru_w_a), ("lru_b_a", lru_b_a, m_lru_b_a, v_lru_b_a),
             ("lru_w_x", lru_w_x, m_lru_w_x, v_lru_w_x), ("lru_b_x", lru_b_x, m_lru_b_x, v_lru_b_x),
             ("lru_lambda", lru_lambda, m_lru_lambda, v_lru_lambda), ("pool_w", pool_w, m_pool_w, v_pool_w),
             ("pool_scale", pool_scale, m_pool_scale, v_pool_scale),
             ("ple_norm_g", ple_norm_g, m_ple_norm_g, v_ple_norm_g), ("final_g", final_g, m_final_g, v_final_g)]

    def view(a):
        return a.reshape(-1, a.shape[-1]) if a.ndim != 3 else a[0]

    flat = _adamw_replicated(vec_sum, mat_sum, g_g1, [(name,) + tuple(view(a) for a in arrs) for name, *arrs in small],
                             (conv_w[0], m_conv_w[0], v_conv_w[0], g_cw))
    u_small = {name: tuple(flat[4 * k + pick].reshape(arrs[0].shape) for pick in range(4))
               for k, (name, *arrs) in enumerate(small)}
    u_cw = tuple(a[None] for a in flat[4 * len(small):4 * len(small) + 4])

    loss = flat[-1].reshape(())
    grad_x = dx.reshape(bsz, seq, D_MODEL)

    def ordered(pick):
        s = {name: u[pick] for name, u in u_small.items()}
        return [s["norm_g"], u_in[pick], u_cw[pick], s["conv_b"], s["lru_w_a"], s["lru_b_a"], s["lru_w_x"], s["lru_b_x"],
                s["lru_lambda"], s["pool_w"], s["pool_scale"], u_pl[pick], u_pp[pick], u_out[pick], s["ple_norm_g"],
                u_pg[pick], u_pe[pick], s["final_g"]]

    return (loss, grad_x, *ordered(0), *ordered(1), *ordered(2), *ordered(3))
```

```python
import jax
import jax.numpy as jnp
from jax import lax
from jax.experimental import pallas as pl
from jax.experimental.pallas import tpu as pltpu

F32 = jnp.float32
BF16 = jnp.bfloat16
MESH = pl.DeviceIdType.MESH

D_MODEL = 1024
LRU_HEADS = 8
HEAD_DIM = 128
CONV_WIDTH = 4
LRU_C = 8.0
POOL_WIDTH = 512
POOL_WINDOWS = (2, 4, 8, 16)
POOL_GROUP_DIM = 128
IN_COLS = 5120
N_CHIPS = 4
EPS = 1e-6

ADAM_LR = 0.001
ADAM_B1 = 0.9
ADAM_B2 = 0.999
ADAM_EPS = 1e-08
ADAM_WD = 0.01
ADAM_STEP = 10

F32_SUBLANES = 8
CONV_HIST = 8
POOL_HIST = 16
VMEM_LIMIT_BYTES = 58 * 1024 * 1024
VEC_BAG_SLOTS = ("norm_g", "conv_w", "conv_b", "lru_b_a", "lru_b_x", "lru_lambda", "pool_scale", "ple_norm_g",
                 "final_g", "loss")
VEC_BAG_ROWS = 128
MAT_BAG_AT = {"lru_w_a": 0, "lru_w_x": LRU_HEADS * HEAD_DIM, "pool_w": 2 * LRU_HEADS * HEAD_DIM}
MAT_BAG_ROWS = 2 * LRU_HEADS * HEAD_DIM + len(POOL_WINDOWS) * POOL_GROUP_DIM


def _bag_row(name, k=0):
    at = F32_SUBLANES * VEC_BAG_SLOTS.index(name) + k
    return slice(at, at + 1)


def _bag_rows(name):
    at = F32_SUBLANES * VEC_BAG_SLOTS.index(name)
    return slice(at, at + F32_SUBLANES)


def _dot(a, b):
    return jnp.dot(a, b, preferred_element_type=F32)


def _dot_nt(a, b):
    return lax.dot_general(a, b, (((1,), (1,)), ((), ())), preferred_element_type=F32)


def _dot_tn(a, b):
    return lax.dot_general(a, b, (((0,), (0,)), ((), ())), preferred_element_type=F32)


def _sigmoid(v):
    return jax.nn.sigmoid(v)


def _softplus(v):
    return jnp.maximum(v, 0.0) + jnp.log1p(jnp.exp(-jnp.abs(v)))


def _place():
    return lax.axis_index("x"), lax.axis_index("y"), lax.axis_index("c")


GATHER_SEMS = 6


def _gather_shapes(shards):
    out_shape = []
    for arr, axis, _ in shards:
        r, cols = arr.shape
        out_shape.append(jax.ShapeDtypeStruct((N_CHIPS * r, cols) if axis == 0 else (r, N_CHIPS * cols), arr.dtype))
    n = len(shards)
    sems = [pltpu.SemaphoreType.DMA((n * GATHER_SEMS,)), pltpu.SemaphoreType.DMA((n * GATHER_SEMS,)),
            pltpu.SemaphoreType.DMA((n,))]
    return out_shape, sems


def _gather_steps(shards, ins, outs, send_sems, recv_sems, local_sems):
    n = len(shards)
    x, y, c = _place()
    me, sibling = (x, y, c), (x, y, 1 - c)
    chips = [(x, 1 - y), (1 - x, y), (1 - x, 1 - y)]

    def region(k, cx, cy, hc):
        (r, cols), axis = shards[k][0].shape, shards[k][1]
        j = 2 * cx + cy
        if axis == 0:
            if hc is None:
                return outs[k].at[pl.ds(j * r, r), :]
            return outs[k].at[pl.ds(j * r + hc * (r // 2), r // 2), :]
        if hc is None:
            return outs[k].at[:, pl.ds(j * cols, cols)]
        return outs[k].at[pl.ds(hc * (r // 2), r // 2), pl.ds(j * cols, cols)]

    def remote(k, sem, block, to, src=None):
        dst = region(k, *block)
        return pltpu.make_async_remote_copy(
            src_ref=dst if src is None else src, dst_ref=dst,
            send_sem=send_sems.at[k * GATHER_SEMS + sem], recv_sem=recv_sems.at[k * GATHER_SEMS + sem],
            device_id=to, device_id_type=MESH)

    def first(k, idx):
        r, split = shards[k][0].shape[0], shards[k][2]
        src = ins[k].at[pl.ds(c * (r // 2), r // 2), :] if split else ins[k]
        return remote(k, idx, (x, y, c if split else None), (*chips[idx], c), src=src)

    def relay(k):
        src_chip = (jnp.bitwise_xor(x, 1 - c), jnp.bitwise_xor(y, c))
        dst_chip = (jnp.bitwise_xor(x, c), jnp.bitwise_xor(y, 1 - c))
        return remote(k, 2, (*src_chip, c), (*dst_chip, c))

    def passed(k, idx):
        return remote(k, 3 + idx, (*chips[idx], c), sibling)

    def mine(k):
        return pltpu.make_async_copy(ins[k], region(k, x, y, None), local_sems.at[k])

    def start():
        for k in range(n):
            mine(k).start()
            for idx in range(2 if shards[k][2] else 3):
                first(k, idx).start()

    def relay_on():
        for k in range(n):
            split = shards[k][2]
            for idx in range(2):
                remote(k, idx, (*chips[idx], c if split else None), me).wait_recv()
            if split:
                relay(k).start()
                passed(k, 0).start()
                passed(k, 1).start()

    def finish():
        for k in range(n):
            split = shards[k][2]
            remote(k, 2, (*chips[2], c if split else None), me).wait_recv()
            if split:
                passed(k, 2).start()
        for k in range(n):
            if shards[k][2]:
                for idx in range(3):
                    remote(k, 3 + idx, (*chips[idx], 1 - c), me).wait_recv()
        for k in range(n):
            if shards[k][2]:
                for cp in (first(k, 0), first(k, 1), relay(k), passed(k, 0), passed(k, 1), passed(k, 2)):
                    cp.wait_send()
            else:
                for idx in range(3):
                    first(k, idx).wait_send()
            mine(k).wait()

    return start, relay_on, finish


RS_ADD_ROWS = (64, 32, 16, 8)


N_DEV = 2 * N_CHIPS


def _all_reduce_scratch(shape):
    return [pltpu.VMEM((N_DEV,) + tuple(shape), F32), pltpu.SemaphoreType.DMA((N_DEV - 1,)),
            pltpu.SemaphoreType.DMA((N_DEV - 1,))]


def _all_reduce_tile(v_ref, o_ref, slots, send_sems, recv_sems):
    flips = [(dx, dy, dc) for dx in (0, 1) for dy in (0, 1) for dc in (0, 1)][1:]
    x, y, c = _place()
    mine = 4 * x + 2 * y + c

    def copy(k, to_flip, slot):
        dx, dy, dc = to_flip
        peer = (jnp.bitwise_xor(x, dx), jnp.bitwise_xor(y, dy), jnp.bitwise_xor(c, dc))
        return pltpu.make_async_remote_copy(
            src_ref=v_ref, dst_ref=slots.at[slot], send_sem=send_sems.at[k], recv_sem=recv_sems.at[k],
            device_id=peer, device_id_type=MESH)

    sends = [copy(k, flip, mine) for k, flip in enumerate(flips)]
    for cp in sends:
        cp.start()
    slots[mine] = v_ref[...]
    for k, (dx, dy, dc) in enumerate(flips):
        copy(k, (dx, dy, dc), jnp.bitwise_xor(mine, 4 * dx + 2 * dy + dc)).wait_recv()
    total = slots[0]
    for d in range(1, N_DEV):
        total = total + slots[d]
    o_ref[...] = total
    for cp in sends:
        cp.wait_send()


RS_SEMS = 8
RS_LOCAL_SEMS = 5


def _rs_piece_shape(part):
    arr, cols = part[0], part[1]
    return (arr.shape[0] // 2, arr.shape[1] // N_CHIPS) if cols else tuple(arr.shape[1:])


def _rs_operands(parts):
    return [p[0] for p in parts] + [p[0] if p[2] is None else p[2] for p in parts]


def _rs_wires(parts, wire):
    return list(wire) if isinstance(wire, (list, tuple)) else [wire] * len(parts)


def _rs_shapes(parts, wire):
    n = len(parts)
    shapes = [_rs_piece_shape(p) for p in parts]
    out_shape = [jax.ShapeDtypeStruct((2,) + s, F32) for s in shapes]
    scratch = []
    for lead, kind in ((N_CHIPS, "f32"), (N_CHIPS, "narrow"), (N_CHIPS, "wire"), (None, "f32"), (N_CHIPS, "wire")):
        for s, p, w in zip(shapes, parts, _rs_wires(parts, wire)):
            dtype = {"f32": F32, "narrow": F32 if p[2] is None else p[2].dtype, "wire": w}[kind]
            scratch.append(pltpu.VMEM(s if lead is None else (lead,) + s, dtype))
    scratch += [pltpu.SemaphoreType.DMA((n * RS_SEMS,)), pltpu.SemaphoreType.DMA((n * RS_SEMS,)),
                pltpu.SemaphoreType.DMA((n * RS_LOCAL_SEMS,))]
    return out_shape, scratch


def _rs_steps(parts, ins, outs, scratch):
    n = len(parts)
    own, sib, got, fin, snd = (scratch[k * n:(k + 1) * n] for k in range(5))
    send_sems, recv_sems, local_sems = scratch[5 * n:]
    shapes = [_rs_piece_shape(p) for p in parts]
    x, y, c = _place()
    j_me = 2 * x + y
    me, sibling = (x, y, c), (x, y, 1 - c)

    def piece(a, jj, core, narrow=False):
        ref = ins[n + a] if narrow else ins[a]
        if parts[a][1]:
            r, cl = shapes[a]
            return ref.at[pl.ds(core * r, r), pl.ds(jj * cl, cl)]
        return ref.at[2 * jj + core]

    def remote(a, sem, src, dst, to):
        return pltpu.make_async_remote_copy(
            src_ref=src, dst_ref=dst, send_sem=send_sems.at[a * RS_SEMS + sem],
            recv_sem=recv_sems.at[a * RS_SEMS + sem], device_id=to, device_id_type=MESH)

    def rows_loop(a, fn):
        r = shapes[a][0]
        step = max(s for s in RS_ADD_ROWS if r % s == 0)

        def it(i, carry):
            fn(pl.ds(pl.multiple_of(i * step, step), step))
            return carry

        lax.fori_loop(0, r // step, it, 0)

    def load(a, jj):
        return pltpu.make_async_copy(piece(a, jj, c), own[a].at[jj], local_sems.at[a * RS_LOCAL_SEMS + jj])

    def to_sibling(a, jj):
        return remote(a, jj, piece(a, jj, 1 - c, narrow=True), sib[a].at[jj], sibling)

    near = (jnp.bitwise_xor(x, 1 - c), jnp.bitwise_xor(y, c))
    far = (jnp.bitwise_xor(x, c), jnp.bitwise_xor(y, 1 - c))
    diag = (1 - x, 1 - y)
    FROM_NEAR, FROM_FAR, FEED = 0, 1, 2

    def chip_of(chip):
        return 2 * chip[0] + chip[1]

    def feed(a):
        return remote(a, 4, snd[a].at[chip_of(diag)], got[a].at[FEED], (*near, c))

    def to_near(a):
        return remote(a, 5, snd[a].at[chip_of(near)], got[a].at[FROM_NEAR], (*near, c))

    def to_far(a):
        return remote(a, 6, snd[a].at[chip_of(far)], got[a].at[FROM_FAR], (*far, c))

    def store(a):
        return pltpu.make_async_copy(fin[a], outs[a].at[c], local_sems.at[a * RS_LOCAL_SEMS + 4])

    def result_to_sibling(a):
        return remote(a, 7, fin[a], outs[a].at[c], sibling)

    def exchange():
        for a in range(n):
            for jj in range(N_CHIPS):
                load(a, jj).start()
                to_sibling(a, jj).start()

    def chip_sums():
        for a in range(n):
            for jj in range(N_CHIPS):
                load(a, jj).wait()
                remote(a, jj, sib[a].at[jj], sib[a].at[jj], me).wait_recv()

                def add(sl, a=a, jj=jj):
                    q = own[a][jj, sl, :] + sib[a][jj, sl, :].astype(F32)
                    own[a][jj, sl, :] = q
                    snd[a][jj, sl, :] = q.astype(snd[a].dtype)

                rows_loop(a, add)
        for a in range(n):
            feed(a).start()
        for a in range(n):
            to_near(a).start()

    def relay():
        for a in range(n):
            remote(a, 4, got[a].at[FEED], got[a].at[FEED], me).wait_recv()

            def add(sl, a=a):
                pair = own[a][chip_of(far), sl, :] + got[a][FEED, sl, :].astype(F32)
                snd[a][chip_of(far), sl, :] = pair.astype(snd[a].dtype)

            rows_loop(a, add)
            to_far(a).start()

    def totals():
        for a in range(n):
            remote(a, 5, got[a].at[FROM_NEAR], got[a].at[FROM_NEAR], me).wait_recv()
            remote(a, 6, got[a].at[FROM_FAR], got[a].at[FROM_FAR], me).wait_recv()

            def total(sl, a=a):
                fin[a][sl, :] = (own[a][j_me, sl, :] + got[a][FROM_NEAR, sl, :].astype(F32)) + (
                    got[a][FROM_FAR, sl, :].astype(F32))

            rows_loop(a, total)
            store(a).start()
            result_to_sibling(a).start()

    def finish():
        for a in range(n):
            remote(a, 7, outs[a].at[1 - c], outs[a].at[1 - c], me).wait_recv()
        for a in range(n):
            for jj in range(N_CHIPS):
                to_sibling(a, jj).wait_send()
            for cp in (feed(a), to_near(a), to_far(a), result_to_sibling(a)):
                cp.wait_send()
            store(a).wait()

    return exchange, chip_sums, relay, totals, finish


def _rms(x):
    r = lax.rsqrt(jnp.mean(x * x, axis=-1, keepdims=True) + EPS)
    return x * r, r


def _rms_bwd(dxn, xn, r):
    return r * (dxn - xn * jnp.mean(dxn * xn, axis=-1, keepdims=True))


def _in_proj_gather(x2d, norm_g, w_in_sh, shards, tb, casts):
    t = x2d.shape[0]
    nb = t // tb
    cols = IN_COLS // N_CHIPS
    half = D_MODEL // 2
    n = len(shards)
    nc = len(casts)

    def body(x_ref, g_ref, win_ref, *refs):
        ins, cast_ins = refs[:n], refs[n:n + nc]
        z_ref, h_ref, wfull_ref = refs[n + nc:n + nc + 3]
        outs, cast_outs = refs[n + nc + 3:2 * n + nc + 3], refs[2 * n + nc + 3:2 * (n + nc) + 3]
        scratch = refs[2 * (n + nc) + 3:]
        wv, h_all, send_sems, recv_sems, local_sems, w_send, w_recv, w_local, stage = scratch[:9]
        wide, narrow, cast_sems = scratch[9:9 + nc], scratch[9 + nc:9 + 2 * nc], scratch[9 + 2 * nc]
        s, i = pl.program_id(0), pl.program_id(1)
        x, y, c = _place()
        me, sibling = (x, y, c), (x, y, 1 - c)
        chips = [(x, 1 - y), (1 - x, y), (1 - x, 1 - y)]

        def w_half(cx, cy, hc):
            return wv.at[2 * cx + cy, pl.ds(hc * half, half), :]

        def w_remote(sem, block, to, src=None):
            dst = w_half(*block)
            return pltpu.make_async_remote_copy(
                src_ref=dst if src is None else src, dst_ref=dst, send_sem=w_send.at[sem],
                recv_sem=w_recv.at[sem], device_id=to, device_id_type=MESH)

        def w_first(idx):
            return w_remote(idx, (x, y, c), (*chips[idx], c))

        def w_relay():
            src_chip = (jnp.bitwise_xor(x, 1 - c), jnp.bitwise_xor(y, c))
            dst_chip = (jnp.bitwise_xor(x, c), jnp.bitwise_xor(y, 1 - c))
            return w_remote(2, (*src_chip, c), (*dst_chip, c))

        def w_pass(idx):
            return w_remote(3 + idx, (*chips[idx], c), sibling)

        def w_store(k, cx, cy):
            jj = 2 * cx + cy
            return pltpu.make_async_copy(wv.at[jj], wfull_ref.at[:, pl.ds(jj * cols, cols)], w_local.at[k])

        start_rest, relay_rest, finish_rest = _gather_steps(shards, ins, outs, send_sems, recv_sems, local_sems)

        def own(k, hc):
            return pltpu.make_async_copy(win_ref.at[pl.ds(pl.multiple_of(hc * half, half), half), :], stage.at[k],
                                         w_local.at[4 + 2 * k])

        def round_own(k, hc):
            own(k, hc).wait()
            wv[2 * x + y, pl.ds(pl.multiple_of(hc * half, half), half), :] = stage[k].astype(BF16)

        wide_in = [pltpu.make_async_copy(cast_ins[k], wide[k], cast_sems.at[k]) for k in range(nc)]
        narrow_out = [pltpu.make_async_copy(narrow[k], cast_outs[k], cast_sems.at[nc + k]) for k in range(nc)]

        @pl.when((s == 0) & (i == 0))
        def _():
            own(0, c).start()
            own(1, 1 - c).start()
            for cp in wide_in:
                cp.start()
            round_own(0, c)
            w_first(0).start()
            w_first(1).start()
            start_rest()
            round_own(1, 1 - c)
            w_store(0, x, y).start()

        @pl.when((s == 1) & (i == 0))
        def _():
            for k in range(nc):
                wide_in[k].wait()
                narrow[k][...] = wide[k][...].astype(BF16)
                narrow_out[k].start()
            w_remote(0, (*chips[0], c), me).wait_recv()
            w_remote(1, (*chips[1], c), me).wait_recv()
            w_relay().start()
            w_pass(0).start()
            w_pass(1).start()
            w_remote(3, (*chips[0], 1 - c), me).wait_recv()
            w_store(1, *chips[0]).start()

        @pl.when((s == 2) & (i == 0))
        def _():
            w_remote(4, (*chips[1], 1 - c), me).wait_recv()
            w_store(2, *chips[1]).start()

        @pl.when((s == 3) & (i == 0))
        def _():
            w_remote(2, (*chips[2], c), me).wait_recv()
            w_pass(2).start()
            w_remote(5, (*chips[2], 1 - c), me).wait_recv()
            w_store(3, *chips[2]).start()

        keep_h = pltpu.make_async_copy(h_all.at[i], h_ref.at[pl.ds(pl.multiple_of(i * tb, tb), tb), :], w_local.at[5])

        @pl.when(s == 0)
        def _():
            xn, _ = _rms(x_ref[...])
            h_all[i] = (xn * g_ref[...]).astype(BF16)
            keep_h.start()

        z_ref[...] = _dot(h_all[i], wv[jnp.bitwise_xor(2 * x + y, s)])
        pl.when(s == 0)(keep_h.wait)

        @pl.when((s == N_CHIPS - 1) & (i == nb - 1))
        def _():
            relay_rest()
            finish_rest()
            for cp in (w_first(0), w_first(1), w_relay(), w_pass(0), w_pass(1), w_pass(2)):
                cp.wait_send()
            w_store(0, x, y).wait()
            for idx in range(3):
                w_store(idx + 1, *chips[idx]).wait()
            for cp in narrow_out:
                cp.wait()

    rest_shape, rest_sems = _gather_shapes(shards)
    out_shape = [jax.ShapeDtypeStruct((t, IN_COLS), F32), jax.ShapeDtypeStruct((t, D_MODEL), BF16),
                 jax.ShapeDtypeStruct((D_MODEL, IN_COLS), BF16)] + rest_shape
    out_shape += [jax.ShapeDtypeStruct(a.shape, BF16) for a in casts]
    any_spec = pl.BlockSpec(memory_space=pl.ANY)

    def z_map(s, i):
        return (i, jnp.bitwise_xor(2 * lax.axis_index("x") + lax.axis_index("y"), s))

    return pl.pallas_call(
        body, name="in_proj", out_shape=tuple(out_shape),
        grid=(N_CHIPS, nb),
        in_specs=[pl.BlockSpec((tb, D_MODEL), lambda s, i: (jnp.where(s == 0, i, nb - 1), 0)),
                  pl.BlockSpec((1, D_MODEL), lambda s, i: (0, 0)), any_spec] + [any_spec] * (n + nc),
        out_specs=tuple([pl.BlockSpec((tb, cols), z_map), any_spec, any_spec] + [any_spec] * (n + nc)),
        scratch_shapes=[pltpu.VMEM((N_CHIPS, D_MODEL, cols), BF16), pltpu.VMEM((nb, tb, D_MODEL), BF16)] + rest_sems + [
            pltpu.SemaphoreType.DMA((GATHER_SEMS,)), pltpu.SemaphoreType.DMA((GATHER_SEMS,)),
            pltpu.SemaphoreType.DMA((N_CHIPS + 3,)), pltpu.VMEM((2, half, cols), F32)]
        + [pltpu.VMEM(a.shape, F32) for a in casts] + [pltpu.VMEM(a.shape, BF16) for a in casts]
        + [pltpu.SemaphoreType.DMA((2 * nc,))],
        compiler_params=pltpu.CompilerParams(dimension_semantics=("arbitrary", "arbitrary"),
                                             vmem_limit_bytes=VMEM_LIMIT_BYTES),
    )(x2d, norm_g, w_in_sh, *[sh[0] for sh in shards], *casts)


def _in_proj_bwd(dz, w_in, x2d, dx_res, norm_g, tb, reduce, shards, take):
    t = x2d.shape[0]
    nb = t // tb
    parts, wire, steps = reduce
    n = len(parts)
    k = len(shards)
    take_rows, take_width = take

    def body(dz_ref, w_ref, x_ref, dres_ref, g_ref, *refs):
        at = 2 * n + k
        dx_ref, dg_ref = refs[at:at + 2]
        rs_outs, g_outs = refs[at + 2:at + 2 + n], refs[at + 2 + n:at + 2 + n + k]
        cut_ref = refs[at + 2 + n + k]
        scratch = refs[at + 3 + n + k:]
        rs_scr, g_sems, dg_acc, ar_scr, cut_sem = scratch[:-8], scratch[-8:-5], scratch[-5], scratch[-4:-1], scratch[-1]
        rs = _rs_steps(parts, refs[:2 * n], rs_outs, rs_scr)
        for step, when in zip(rs[:-1], steps):
            pl.when(pl.program_id(0) == when)(step)
        gather = _gather_steps(shards, refs[2 * n:at], g_outs, *g_sems)
        for step, when in zip(gather, (0, nb // 2, nb - 1)):
            pl.when(pl.program_id(0) == when)(step)

        @pl.when(pl.program_id(0) == 0)
        def _():
            dg_acc[...] = jnp.zeros_like(dg_acc)

        xn, r = _rms(x_ref[...])
        g = g_ref[...]
        dh = _dot_nt(dz_ref[...], w_ref[...])
        dg_acc[0:1, :] += jnp.sum(dh * xn, axis=0, keepdims=True)
        dx_ref[...] = dres_ref[...] + _rms_bwd(dh * g, xn, r)

        @pl.when(pl.program_id(0) == nb - 1)
        def _():
            x, y, _ = _place()
            mine = pl.ds(pl.multiple_of((2 * x + y) * take_width, take_width), take_width)
            cut = pltpu.make_async_copy(g_outs[0].at[take_rows, mine], cut_ref, cut_sem)
            cut.start()
            _all_reduce_tile(dg_acc, dg_ref, *ar_scr)
            rs[-1]()
            cut.wait()

    row = lambda i: (i, 0)
    fixed = lambda i: (0, 0)
    rs_shape, rs_scratch = _rs_shapes(parts, wire)
    g_shape, g_sems = _gather_shapes(shards)
    any_spec = pl.BlockSpec(memory_space=pl.ANY)
    cut_shape = jax.ShapeDtypeStruct((take_rows.stop - take_rows.start, take_width), F32)
    return pl.pallas_call(
        body, name="in_proj_bwd",
        out_shape=tuple([jax.ShapeDtypeStruct((t, D_MODEL), F32), jax.ShapeDtypeStruct((F32_SUBLANES, D_MODEL), F32)]
                        + rs_shape + g_shape + [cut_shape]),
        grid=(nb,),
        in_specs=[pl.BlockSpec((tb, IN_COLS), row),
                  pl.BlockSpec((D_MODEL, IN_COLS), fixed, pipeline_mode=pl.Buffered(1)),
                  pl.BlockSpec((tb, D_MODEL), row), pl.BlockSpec((tb, D_MODEL), row),
                  pl.BlockSpec((1, D_MODEL), fixed)] + [any_spec] * (2 * n + k),
        out_specs=tuple([pl.BlockSpec((tb, D_MODEL), row), pl.BlockSpec((F32_SUBLANES, D_MODEL), fixed)]
                        + [any_spec] * (n + k + 1)),
        scratch_shapes=rs_scratch + g_sems + [pltpu.VMEM((F32_SUBLANES, D_MODEL), F32)] + _all_reduce_scratch(
            (F32_SUBLANES, D_MODEL)) + [pltpu.SemaphoreType.DMA(())],
        compiler_params=pltpu.CompilerParams(dimension_semantics=("arbitrary",),
                                             vmem_limit_bytes=VMEM_LIMIT_BYTES),
    )(dz, w_in, x2d, dx_res, norm_g, *_rs_operands(parts), *[sh[0] for sh in shards])


def _weight_grad(pairs, n_chunks, tb, name, reduce=None):
    t = pairs[0][0].shape[0]
    nb = t // tb
    m = len(pairs)
    parts, wire, steps = reduce if reduce is not None else ([], F32, ())
    n = len(parts)

    def body(*refs):
        lr, refs = refs[:2 * m], refs[2 * m:]
        o_refs = refs[2 * n:2 * n + 2 * m]
        if n:
            at = pl.program_id(0) * nb + pl.program_id(1)
            rs = _rs_steps(parts, refs[:2 * n], refs[2 * n + 2 * m:3 * n + 2 * m], refs[3 * n + 2 * m:])
            for step, when in zip(rs, steps):
                pl.when(at == when)(step)

        @pl.when(pl.program_id(1) == 0)
        def _():
            for q in range(m):
                o_refs[2 * q][...] = jnp.zeros_like(o_refs[2 * q])

        for q in range(m):
            o_refs[2 * q][...] += _dot_tn(lr[2 * q][...], lr[2 * q + 1][...])

        @pl.when(pl.program_id(1) == nb - 1)
        def _():
            for q in range(m):
                o_refs[2 * q + 1][...] = o_refs[2 * q][...].astype(BF16)

    rs_shape, rs_scratch = _rs_shapes(parts, wire) if n else ([], [])
    any_spec = pl.BlockSpec(memory_space=pl.ANY)
    in_specs, out_specs, out_shape = [], [], []
    for lhs, rhs in pairs:
        k, nc = lhs.shape[1], rhs.shape[1] // n_chunks
        in_specs += [pl.BlockSpec((tb, k), lambda j, i: (i, 0)), pl.BlockSpec((tb, nc), lambda j, i: (i, j))]
        out_specs += [pl.BlockSpec((None, k, nc), lambda j, i: (j, 0, 0))] * 2
        out_shape += [jax.ShapeDtypeStruct((n_chunks, k, nc), F32), jax.ShapeDtypeStruct((n_chunks, k, nc), BF16)]
    return pl.pallas_call(
        body, name=name, out_shape=tuple(out_shape + rs_shape),
        grid=(n_chunks, nb),
        in_specs=in_specs + [any_spec] * (2 * n),
        out_specs=tuple(out_specs + [any_spec] * n),
        scratch_shapes=rs_scratch,
        compiler_params=pltpu.CompilerParams(dimension_semantics=("arbitrary", "arbitrary"),
                                             vmem_limit_bytes=VMEM_LIMIT_BYTES),
    )(*[a for pair in pairs for a in pair], *_rs_operands(parts))


def _adam_update(w, g, m, v):
    m_ = ADAM_B1 * m + (1.0 - ADAM_B1) * g
    v_ = ADAM_B2 * v + (1.0 - ADAM_B2) * jnp.square(g)
    m_hat = m_ / (1.0 - ADAM_B1 ** ADAM_STEP)
    v_hat = v_ / (1.0 - ADAM_B2 ** ADAM_STEP)
    return -ADAM_LR * (m_hat / (jnp.sqrt(v_hat) + ADAM_EPS) + ADAM_WD * w), m_, v_


def _adamw_replicated(vec_sum, mat_sum, norm_grad, entries, conv):
    n = len(entries)

    def grad_of(name, shape, vec_ref, mat_ref, norm_ref):
        if name == "norm_g":
            return norm_ref[0:1, :]
        if name in MAT_BAG_AT:
            return mat_ref[MAT_BAG_AT[name]:MAT_BAG_AT[name] + shape[0], :]
        if shape[0] == 1:
            return vec_ref[_bag_row(name), 0:shape[1]]
        return jnp.concatenate([vec_ref[_bag_row(name), h * shape[1]:(h + 1) * shape[1]] for h in range(shape[0])],
                               axis=0)

    def body(vec_ref, mat_ref, norm_ref, *refs):
        ins, outs = refs[:3 * n + 4], refs[3 * n + 4:]
        for k in range(n):
            w_ref, m_ref, v_ref = ins[3 * k:3 * k + 3]
            g = grad_of(entries[k][0], w_ref.shape, vec_ref, mat_ref, norm_ref)
            d, m_, v_ = _adam_update(w_ref[...], g, m_ref[...], v_ref[...])
            for ref, val in zip(outs[4 * k:4 * k + 4], (g, d, m_, v_)):
                ref[...] = val
        w_ref, m_ref, v_ref, g_ref = ins[3 * n:]
        g = g_ref[0:w_ref.shape[0], :]
        for ref, val in zip(outs[4 * n:4 * n + 4], (g,) + _adam_update(w_ref[...], g, m_ref[...], v_ref[...])):
            ref[...] = val
        outs[4 * n + 4][...] = vec_ref[_bag_row("loss"), 0:1]

    arrays = [a for e in entries for a in e[1:]] + list(conv)
    out_shape = [jax.ShapeDtypeStruct(e[1].shape, F32) for e in entries for _ in range(4)]
    out_shape += [jax.ShapeDtypeStruct(conv[0].shape, F32)] * 4 + [jax.ShapeDtypeStruct((1, 1), F32)]
    return pl.pallas_call(
        body, name="adamw_replicated", out_shape=tuple(out_shape),
        compiler_params=pltpu.CompilerParams(vmem_limit_bytes=VMEM_LIMIT_BYTES),
    )(vec_sum, mat_sum, norm_grad, *arrays)


def _adamw_group(items, name):
    arrays = [a for item in items for a in item[:4]]
    n = len(arrays)
    blocks = []
    for k, item in enumerate(items):
        rows = item[0].shape[0] // item[4]
        blocks += [(k, slice(q * rows, (q + 1) * rows)) for q in range(item[4])]

    def body(*refs):
        ins, outs, bufs = refs[:n], refs[n:2 * n], refs[2 * n:3 * n]
        load_sems, store_sems = refs[3 * n:]

        def copies(src, dst, sems):
            return [[pltpu.make_async_copy(src[4 * k + j].at[rows], dst[4 * k + j].at[rows], sems.at[4 * b + j])
                     for j in range(4)] for b, (k, rows) in enumerate(blocks)]

        loads, stores = copies(ins, bufs, load_sems), copies(bufs, outs, store_sems)
        for cp in [cp for block in loads for cp in block]:
            cp.start()
        for b, (k, rows) in enumerate(blocks):
            for cp in loads[b]:
                cp.wait()
            w_buf, g_buf, m_buf, v_buf = bufs[4 * k:4 * k + 4]
            w_buf[rows, :], m_buf[rows, :], v_buf[rows, :] = _adam_update(
                w_buf[rows, :], g_buf[rows, :], m_buf[rows, :], v_buf[rows, :])
            for cp in stores[b]:
                cp.start(priority=1)
        for cp in [cp for block in stores for cp in block]:
            cp.wait()

    any_spec = pl.BlockSpec(memory_space=pl.ANY)
    flat = pl.pallas_call(
        body, name=name, out_shape=tuple(jax.ShapeDtypeStruct(a.shape, F32) for a in arrays),
        in_specs=[any_spec] * n, out_specs=(any_spec,) * n,
        scratch_shapes=[pltpu.VMEM(a.shape, F32) for a in arrays] + [pltpu.SemaphoreType.DMA((4 * len(blocks),))] * 2,
        compiler_params=pltpu.CompilerParams(vmem_limit_bytes=VMEM_LIMIT_BYTES),
    )(*arrays)
    return [(flat[4 * k + 1], flat[4 * k], flat[4 * k + 2], flat[4 * k + 3]) for k in range(len(items))]


def _shift_down(ext, s):
    return pltpu.roll(ext, s, 0)


def _tile_shift(v, s):
    rows, cols = v.shape
    tiles = v.reshape(rows // F32_SUBLANES, F32_SUBLANES, cols)
    return pltpu.roll(tiles, s % F32_SUBLANES, 1).reshape(rows, cols)


def _shift_up(ext, s):
    return pltpu.roll(ext, ext.shape[0] - s, 0)


def _lru_gates(xc, wa_ref, ba, wx_ref, bx, lam):
    pa, px = [], []
    for h in range(LRU_HEADS):
        xh = xc[:, h * HEAD_DIM:(h + 1) * HEAD_DIM].astype(BF16)
        pa.append(_dot(xh, wa_ref[h]))
        px.append(_dot(xh, wx_ref[h]))
    r = _sigmoid(jnp.concatenate(pa, axis=1) + ba)
    ig = _sigmoid(jnp.concatenate(px, axis=1) + bx)
    sp = _softplus(-lam)
    log_a = (-LRU_C * r) * sp
    a = jnp.exp(log_a)
    mult = jnp.sqrt(jnp.tanh(-log_a) * (1.0 + a * a))
    return r, ig, a, mult, sp


def _conv(ext, w_ref, b):
    y = b + _shift_down(ext, 3) * w_ref[0:1, :]
    y = y + _shift_down(ext, 2) * w_ref[1:2, :]
    y = y + _shift_down(ext, 1) * w_ref[2:3, :]
    y = y + ext * w_ref[3:4, :]
    return y[CONV_HIST:, :]


def _pool_diff(ext, pos):
    out = []
    for g, k in enumerate(POOL_WINDOWS):
        col = ext[:, g * POOL_GROUP_DIM:(g + 1) * POOL_GROUP_DIM]
        s = col
        for step in range(g + 1):
            s = s + _shift_down(s, 2 ** step)
        count = jnp.minimum(pos + 1, k).astype(F32)
        out.append(s[POOL_HIST:, :] / count - col[POOL_HIST:, :])
    return out


def _pool_mix(diff, pw_ref):
    return jnp.concatenate([_dot(diff[g].astype(BF16), pw_ref[g]) for g in range(len(POOL_WINDOWS))], axis=1)


def _branch_specs(tb, row_map, fixed):
    fixed3 = lambda i: (0, 0, 0)
    return [pl.BlockSpec((CONV_WIDTH, D_MODEL), fixed), pl.BlockSpec((1, D_MODEL), fixed),
            pl.BlockSpec((LRU_HEADS, HEAD_DIM, HEAD_DIM), fixed3), pl.BlockSpec((1, D_MODEL), fixed),
            pl.BlockSpec((LRU_HEADS, HEAD_DIM, HEAD_DIM), fixed3), pl.BlockSpec((1, D_MODEL), fixed),
            pl.BlockSpec((1, D_MODEL), fixed),
            pl.BlockSpec((len(POOL_WINDOWS), POOL_GROUP_DIM, POOL_GROUP_DIM), fixed3),
            pl.BlockSpec((1, POOL_WIDTH), fixed)]


def _branches_fwd(z, weights, seq, tb, shards):
    t = z.shape[0]
    nb = t // tb
    nbe = seq // tb
    groups = tb // F32_SUBLANES
    n = len(shards)

    def body(xa_ref, ga_ref, xb_ref, gb_ref, cw_ref, cb_ref, wa_ref, ba_ref, wx_ref, bx_ref, lam_ref,
             pw_ref, ps_ref, *refs):
        g_ins = refs[:n]
        ya_ref, yb_ref, hl_ref = refs[n:n + 3]
        g_outs = refs[n + 3:2 * n + 3]
        xa_ext, xb_ext, carry, a_s, u_s, send_sems, recv_sems, local_sems = refs[2 * n + 3:]
        blk = pl.program_id(0) % nbe
        start_gather, relay_gather, finish_gather = _gather_steps(shards, g_ins, g_outs, send_sems, recv_sems,
                                                                  local_sems)
        pl.when(pl.program_id(0) == 0)(start_gather)
        pl.when(pl.program_id(0) == nb // 2)(relay_gather)

        @pl.when(blk == 0)
        def _():
            xa_ext[0:CONV_HIST, :] = jnp.zeros((CONV_HIST, D_MODEL), F32)
            xb_ext[0:POOL_HIST, :] = jnp.zeros((POOL_HIST, POOL_WIDTH), F32)
            carry[...] = jnp.zeros_like(carry)

        xa_ext[CONV_HIST:, :] = xa_ref[...]
        xb_ext[POOL_HIST:, :] = xb_ref[...]
        ea = xa_ext[...]
        eb = xb_ext[...]
        xa_ext[0:CONV_HIST, :] = ea[tb:, :]
        xb_ext[0:POOL_HIST, :] = eb[tb:, :]

        xc = _conv(ea, cw_ref, cb_ref[...])
        _, ig, a, mult, _ = _lru_gates(xc, wa_ref, ba_ref[...], wx_ref, bx_ref[...], lam_ref[...])
        u = mult * (ig * xc)
        row8 = lax.broadcasted_iota(jnp.int32, (tb, D_MODEL), 0) % F32_SUBLANES
        for s in (1, 2, 4):
            m = row8 >= s
            u = jnp.where(m, a * _tile_shift(u, s) + u, u)
            a = jnp.where(m, a * _tile_shift(a, s), a)
        a_s[...] = a
        u_s[...] = u

        def step(g, cr):
            sl = pl.ds(pl.multiple_of(g * F32_SUBLANES, F32_SUBLANES), F32_SUBLANES)
            hb = a_s[sl, :] * cr + u_s[sl, :]
            hl_ref[sl, :] = hb
            return jnp.broadcast_to(hb[F32_SUBLANES - 1:F32_SUBLANES, :], (F32_SUBLANES, D_MODEL))

        carry[...] = lax.fori_loop(0, groups, step, carry[...], unroll=True)
        ga = ga_ref[...]
        ya_ref[...] = (hl_ref[...] * (ga * _sigmoid(ga))).astype(BF16)

        pos = blk * tb + lax.broadcasted_iota(jnp.int32, (tb, POOL_GROUP_DIM), 0)
        ypre = _pool_mix(_pool_diff(eb, pos), pw_ref)
        gb = gb_ref[...]
        yb_ref[...] = ((ypre * ps_ref[...]) * (gb * _sigmoid(gb))).astype(BF16)
        pl.when(pl.program_id(0) == nb - 1)(finish_gather)

    row = lambda i: (i, 0)
    fixed = lambda i: (0, 0)
    any_spec = pl.BlockSpec(memory_space=pl.ANY)
    in_specs = [pl.BlockSpec((tb, D_MODEL), lambda i: (i, 0)), pl.BlockSpec((tb, D_MODEL), lambda i: (i, 1)),
                pl.BlockSpec((tb, POOL_WIDTH), lambda i: (i, 4)), pl.BlockSpec((tb, POOL_WIDTH), lambda i: (i, 5)),
                ] + _branch_specs(tb, row, fixed) + [any_spec] * n
    g_shape, g_sems = _gather_shapes(shards)
    return pl.pallas_call(
        body, name="branches_fwd",
        out_shape=tuple([jax.ShapeDtypeStruct((t, D_MODEL), BF16), jax.ShapeDtypeStruct((t, POOL_WIDTH), BF16),
                         jax.ShapeDtypeStruct((t, D_MODEL), F32)] + g_shape),
        grid=(nb,), in_specs=in_specs,
        out_specs=tuple([pl.BlockSpec((tb, D_MODEL), row), pl.BlockSpec((tb, POOL_WIDTH), row),
                         pl.BlockSpec((tb, D_MODEL), row)] + [any_spec] * n),
        scratch_shapes=[pltpu.VMEM((tb + CONV_HIST, D_MODEL), F32), pltpu.VMEM((tb + POOL_HIST, POOL_WIDTH), F32),
                        pltpu.VMEM((F32_SUBLANES, D_MODEL), F32),
                        pltpu.VMEM((tb, D_MODEL), F32), pltpu.VMEM((tb, D_MODEL), F32)] + g_sems,
        compiler_params=pltpu.CompilerParams(dimension_semantics=("arbitrary",),
                                             vmem_limit_bytes=VMEM_LIMIT_BYTES),
    )(z, z, z, z, *weights, *[sh[0] for sh in shards])


def _branches_bwd(z, hl, dya, dyb, dzm, weights, vec_bag, seq, tb, riders):
    t = z.shape[0]
    nb = t // tb
    nbe = seq // tb
    groups = tb // F32_SUBLANES
    nr = len(riders)

    def body(xa_ref, xap_ref, ga_ref, xb_ref, xbp_ref, gb_ref, hl_ref, hlp_ref, dya_ref, dyb_ref, dzm_ref,
             cw_ref, cb_ref, wa_ref, ba_ref, wx_ref, bx_ref, lam_ref, pw_ref, ps_ref, vec_in_ref, *rest):
        pairs, (dz_ref, vec_ref, mat_ref), grads = rest[:2 * nr], rest[2 * nr:2 * nr + 3], rest[2 * nr + 3:4 * nr + 3]
        xa_ext, xb_ext, hl_ext, a_ext, dxc_ext, dwin_ext, g_carry, b_s, d_s, g_s = rest[4 * nr + 3:]
        i = pl.program_id(0)
        blk = (nb - 1 - i) % nbe

        def mat_rows(name, k):
            at = MAT_BAG_AT[name] + k * HEAD_DIM
            return slice(at, at + HEAD_DIM)

        def rider(k):
            grads[2 * k][...] += _dot_tn(pairs[2 * k][...], pairs[2 * k + 1][...])

        @pl.when(i == 0)
        def _():
            vec_ref[...] = vec_in_ref[...]
            mat_ref[...] = jnp.zeros_like(mat_ref)
            for k in range(nr):
                grads[2 * k][...] = jnp.zeros_like(grads[2 * k])

        @pl.when(blk == nbe - 1)
        def _():
            a_ext[tb:, :] = jnp.zeros((F32_SUBLANES, D_MODEL), F32)
            dxc_ext[tb:, :] = jnp.zeros((CONV_HIST, D_MODEL), F32)
            dwin_ext[tb:, :] = jnp.zeros((POOL_HIST, POOL_WIDTH), F32)
            g_carry[...] = jnp.zeros_like(g_carry)

        live = (blk > 0).astype(F32)
        xa_ext[0:CONV_HIST, :] = xap_ref[...] * live
        xa_ext[CONV_HIST:, :] = xa_ref[...]
        xb_ext[0:POOL_HIST, :] = xbp_ref[...] * live
        xb_ext[POOL_HIST:, :] = xb_ref[...]
        hl_ext[0:F32_SUBLANES, :] = hlp_ref[...] * live
        hl_ext[F32_SUBLANES:, :] = hl_ref[...]
        ea = xa_ext[...]
        eb = xb_ext[...]
        rider(0)

        xc = _conv(ea, cw_ref, cb_ref[...])
        lam = lam_ref[...]
        r, ig, a, mult, sp = _lru_gates(xc, wa_ref, ba_ref[...], wx_ref, bx_ref[...], lam)
        hl = hl_ref[...]
        ga = ga_ref[...]
        sga = _sigmoid(ga)
        dya = dya_ref[...]
        dhl = dya * (ga * sga)
        dz_ref[:, D_MODEL:2 * D_MODEL] = (dya * hl * (sga * (1.0 + ga * (1.0 - sga)))).astype(BF16)

        a_ext[0:tb, :] = a
        b = _shift_up(a_ext[...], 1)[0:tb, :]
        a_ext[tb:, :] = jnp.broadcast_to(a[0:1, :], (F32_SUBLANES, D_MODEL))
        d = dhl
        row8 = lax.broadcasted_iota(jnp.int32, (tb, D_MODEL), 0) % F32_SUBLANES
        for s in (1, 2, 4):
            m = row8 < F32_SUBLANES - s
            d = jnp.where(m, d + b * _tile_shift(d, -s), d)
            b = jnp.where(m, b * _tile_shift(b, -s), b)
        b_s[...] = b
        d_s[...] = d

        def step(k, cr):
            sl = pl.ds(pl.multiple_of((groups - 1 - k) * F32_SUBLANES, F32_SUBLANES), F32_SUBLANES)
            gb_ = d_s[sl, :] + b_s[sl, :] * cr
            g_s[sl, :] = gb_
            return jnp.broadcast_to(gb_[0:1, :], (F32_SUBLANES, D_MODEL))

        g_carry[...] = lax.fori_loop(0, groups, step, g_carry[...], unroll=4)
        rider(1)
        gsc = g_s[...]
        da = gsc * _shift_down(hl_ext[...], 1)[F32_SUBLANES:, :]
        dmult = gsc * (ig * xc)
        dig = gsc * (mult * xc)
        dxc = gsc * (mult * ig)
        dlog_a = da * a - (a * a) * dmult / mult
        dr = dlog_a * (-LRU_C * sp)
        vec_ref[_bag_row("lru_lambda"), :] += jnp.sum(dlog_a * (-LRU_C * r), axis=0, keepdims=True)
        dpa = dr * (r * (1.0 - r))
        dpx = dig * (ig * (1.0 - ig))
        vec_ref[_bag_row("lru_b_a"), :] += jnp.sum(dpa, axis=0, keepdims=True)
        vec_ref[_bag_row("lru_b_x"), :] += jnp.sum(dpx, axis=0, keepdims=True)
        back = []
        for h in range(LRU_HEADS):
            cols = slice(h * HEAD_DIM, (h + 1) * HEAD_DIM)
            xh = xc[:, cols].astype(BF16)
            dpa_h = dpa[:, cols].astype(BF16)
            dpx_h = dpx[:, cols].astype(BF16)
            mat_ref[mat_rows("lru_w_a", h), :] += _dot_tn(xh, dpa_h)
            mat_ref[mat_rows("lru_w_x", h), :] += _dot_tn(xh, dpx_h)
            back.append(_dot_nt(dpa_h, wa_ref[h]) + _dot_nt(dpx_h, wx_ref[h]))
        dxc = dxc + jnp.concatenate(back, axis=1)
        vec_ref[_bag_row("conv_b"), :] += jnp.sum(dxc, axis=0, keepdims=True)
        for k in range(CONV_WIDTH):
            tap = _shift_down(ea, CONV_WIDTH - 1 - k)[CONV_HIST:, :] if k < CONV_WIDTH - 1 else ea[CONV_HIST:, :]
            vec_ref[_bag_row("conv_w", k), :] += jnp.sum(dxc * tap, axis=0, keepdims=True)
        dxc_ext[0:tb, :] = dxc
        ed = dxc_ext[...]
        dxa = ed * cw_ref[3:4, :]
        dxa = dxa + _shift_up(ed, 1) * cw_ref[2:3, :]
        dxa = dxa + _shift_up(ed, 2) * cw_ref[1:2, :]
        dxa = dxa + _shift_up(ed, 3) * cw_ref[0:1, :]
        dz_ref[:, 0:D_MODEL] = dxa[0:tb, :].astype(BF16)
        dxc_ext[tb:, :] = dxc[0:CONV_HIST, :]

        pos = blk * tb + lax.broadcasted_iota(jnp.int32, (tb, POOL_GROUP_DIM), 0)
        diff = _pool_diff(eb, pos)
        rider(2)
        ypre = _pool_mix(diff, pw_ref)
        ps = ps_ref[...]
        gb = gb_ref[...]
        sgb = _sigmoid(gb)
        dyb = dyb_ref[...]
        dyp = dyb * (gb * sgb)
        dz_ref[:, 2 * D_MODEL + POOL_WIDTH:3 * D_MODEL] = (
            dyb * (ypre * ps) * (sgb * (1.0 + gb * (1.0 - sgb)))).astype(BF16)
        vec_ref[_bag_row("pool_scale"), 0:POOL_WIDTH] += jnp.sum(dyp * ypre, axis=0, keepdims=True)
        dypre = dyp * ps
        for g, k in enumerate(POOL_WINDOWS):
            cols = slice(g * POOL_GROUP_DIM, (g + 1) * POOL_GROUP_DIM)
            dyg = dypre[:, cols].astype(BF16)
            mat_ref[mat_rows("pool_w", g), :] += _dot_tn(diff[g].astype(BF16), dyg)
            ddiff = _dot_nt(dyg, pw_ref[g])
            count = jnp.minimum(pos + 1, k).astype(F32)
            dwin = ddiff / count
            dwin_ext[0:tb, cols] = dwin
            s = dwin_ext[:, cols]
            for step_ in range(g + 1):
                s = s + _shift_up(s, 2 ** step_)
            dz_ref[:, 2 * D_MODEL + g * POOL_GROUP_DIM:2 * D_MODEL + (g + 1) * POOL_GROUP_DIM] = (
                s[0:tb, :] - ddiff).astype(BF16)
            dwin_ext[tb:, cols] = dwin[0:POOL_HIST, :]

        dz_ref[:, 3 * D_MODEL:] = dzm_ref[...]

        @pl.when(i == nb - 1)
        def _():
            row = _bag_row("lru_lambda")
            vec_ref[row, :] = vec_ref[row, :] * (-_sigmoid(-lam))
            for k in range(nr):
                grads[2 * k + 1][...] = grads[2 * k][...].astype(BF16)

    rev = lambda i: (nb - 1 - i, 0)
    fixed = lambda i: (0, 0)

    def prev(rows, col):
        per = tb // rows
        return lambda i: (jnp.maximum((nb - 1 - i) * per - 1, 0), col)

    in_specs = [pl.BlockSpec((tb, D_MODEL), lambda i: (nb - 1 - i, 0)),
                pl.BlockSpec((CONV_HIST, D_MODEL), prev(CONV_HIST, 0)),
                pl.BlockSpec((tb, D_MODEL), lambda i: (nb - 1 - i, 1)),
                pl.BlockSpec((tb, POOL_WIDTH), lambda i: (nb - 1 - i, 4)),
                pl.BlockSpec((POOL_HIST, POOL_WIDTH), prev(POOL_HIST, 4)),
                pl.BlockSpec((tb, POOL_WIDTH), lambda i: (nb - 1 - i, 5)),
                pl.BlockSpec((tb, D_MODEL), rev),
                pl.BlockSpec((F32_SUBLANES, D_MODEL), prev(F32_SUBLANES, 0)),
                pl.BlockSpec((tb, D_MODEL), rev), pl.BlockSpec((tb, POOL_WIDTH), rev),
                pl.BlockSpec((tb, 2 * D_MODEL), rev)] + _branch_specs(tb, rev, fixed) + [
                    pl.BlockSpec((VEC_BAG_ROWS, D_MODEL), fixed)]
    vec_at = len(in_specs) - 1
    out_shape = [jax.ShapeDtypeStruct((t, IN_COLS), BF16), jax.ShapeDtypeStruct((VEC_BAG_ROWS, D_MODEL), F32),
                 jax.ShapeDtypeStruct((MAT_BAG_ROWS, HEAD_DIM), F32)]
    out_specs = [pl.BlockSpec((tb, IN_COLS), rev), pl.BlockSpec((VEC_BAG_ROWS, D_MODEL), fixed),
                 pl.BlockSpec((MAT_BAG_ROWS, HEAD_DIM), fixed)]
    for lhs, rhs in riders:
        in_specs += [pl.BlockSpec((tb, lhs.shape[1]), rev), pl.BlockSpec((tb, rhs.shape[1]), rev)]
        grad = (lhs.shape[1], rhs.shape[1])
        out_shape += [jax.ShapeDtypeStruct(grad, F32), jax.ShapeDtypeStruct(grad, BF16)]
        out_specs += [pl.BlockSpec(grad, fixed)] * 2
    scratch = [pltpu.VMEM((tb + CONV_HIST, D_MODEL), F32), pltpu.VMEM((tb + POOL_HIST, POOL_WIDTH), F32),
               pltpu.VMEM((tb + F32_SUBLANES, D_MODEL), F32), pltpu.VMEM((tb + F32_SUBLANES, D_MODEL), F32),
               pltpu.VMEM((tb + CONV_HIST, D_MODEL), F32), pltpu.VMEM((tb + POOL_HIST, POOL_WIDTH), F32),
               pltpu.VMEM((F32_SUBLANES, D_MODEL), F32),
               pltpu.VMEM((tb, D_MODEL), F32), pltpu.VMEM((tb, D_MODEL), F32), pltpu.VMEM((tb, D_MODEL), F32)]
    return pl.pallas_call(
        body, name="branches_bwd", out_shape=tuple(out_shape), grid=(nb,), in_specs=in_specs,
        out_specs=tuple(out_specs), scratch_shapes=scratch, input_output_aliases={vec_at: 1},
        compiler_params=pltpu.CompilerParams(dimension_semantics=("arbitrary",),
                                             vmem_limit_bytes=VMEM_LIMIT_BYTES),
    )(z, z, z, z, z, z, hl, hl, dya, dyb, dzm, *weights, vec_bag, *[a for pair in riders for a in pair])


def _merge_head(x2d, ya, yb, z, p2d, tgt, w_pl, w_pp, w_out, w_pg, w_pe, g2, gf, tb):
    t = x2d.shape[0]
    p_dim = p2d.shape[1]

    def body(x_ref, ya_ref, yb_ref, ma_ref, mb_ref, p_ref, t_ref, wpl_ref, wpp_ref, wout_ref, wpg_ref, wpe_ref,
             g2_ref, gf_ref,
             bag_ref, dxr_ref, dya_ref, dyb_ref, dzm_ref,
             mg_ref, do_ref, hn_ref, dgp_ref, dpe_ref, da_ref, dbm_ref, pbf_ref):
        @pl.when(pl.program_id(0) == 0)
        def _():
            bag_ref[...] = jnp.zeros_like(bag_ref)

        a_ = _dot(ya_ref[...], wpl_ref[...])
        bm = _dot(yb_ref[...], wpp_ref[...])
        sa = _sigmoid(ma_ref[...])
        sb = _sigmoid(mb_ref[...])
        mg = (sa * a_ + sb * bm).astype(BF16)
        mg_ref[...] = mg
        x1 = x_ref[...] + _dot(mg, wout_ref[...])
        xn2, r2 = _rms(x1)
        g2 = g2_ref[...]
        hn = (xn2 * g2).astype(BF16)
        hn_ref[...] = hn
        gate = _sigmoid(_dot(hn, wpg_ref[...]))
        pbf = p_ref[...].astype(BF16)
        pbf_ref[...] = pbf
        pe = _dot(pbf, wpe_ref[...])
        x2 = x1 + gate * pe
        xn3, r3 = _rms(x2)
        gf = gf_ref[...]
        err = xn3 * gf - t_ref[...]
        bag_ref[_bag_rows("loss"), 0:128] += 0.5 * jnp.sum(jnp.mean(err * err, axis=-1))

        dy = err * (1.0 / D_MODEL)
        bag_ref[_bag_row("final_g"), :] += jnp.sum(dy * xn3, axis=0, keepdims=True)
        dx2 = _rms_bwd(dy * gf, xn3, r3)
        dpe_ref[...] = (dx2 * gate).astype(BF16)
        dgp = ((dx2 * pe) * (gate * (1.0 - gate))).astype(BF16)
        dgp_ref[...] = dgp
        dhn = _dot_nt(dgp, wpg_ref[...])
        bag_ref[_bag_row("ple_norm_g"), :] += jnp.sum(dhn * xn2, axis=0, keepdims=True)
        dx1 = dx2 + _rms_bwd(dhn * g2, xn2, r2)
        dxr_ref[...] = dx1
        do = dx1.astype(BF16)
        do_ref[...] = do
        dmg = _dot_nt(do, wout_ref[...])
        da = (dmg * sa).astype(BF16)
        dbm = (dmg * sb).astype(BF16)
        da_ref[...] = da
        dbm_ref[...] = dbm
        dzm_ref[:, 0:D_MODEL] = (dmg * a_ * (sa * (1.0 - sa))).astype(BF16)
        dzm_ref[:, D_MODEL:] = (dmg * bm * (sb * (1.0 - sb))).astype(BF16)
        dya_ref[...] = _dot_nt(da, wpl_ref[...])
        dyb_ref[...] = _dot_nt(dbm, wpp_ref[...])

    row = lambda i: (i, 0)
    fixed = lambda i: (0, 0)

    def resident(shape):
        return pl.BlockSpec(shape, fixed, pipeline_mode=pl.Buffered(1))

    tok = lambda width: pl.BlockSpec((tb, width), row)
    in_specs = [tok(D_MODEL), tok(D_MODEL), tok(POOL_WIDTH),
                pl.BlockSpec((tb, D_MODEL), lambda i: (i, 3)), pl.BlockSpec((tb, D_MODEL), lambda i: (i, 4)),
                tok(p_dim), tok(D_MODEL),
                resident((D_MODEL, D_MODEL)), resident((POOL_WIDTH, D_MODEL)), resident((D_MODEL, D_MODEL)),
                resident((D_MODEL, D_MODEL)), resident((p_dim, D_MODEL)),
                pl.BlockSpec((1, D_MODEL), fixed), pl.BlockSpec((1, D_MODEL), fixed)]
    bf = lambda width: jax.ShapeDtypeStruct((t, width), BF16)
    f32 = lambda width: jax.ShapeDtypeStruct((t, width), F32)
    out_shape = (jax.ShapeDtypeStruct((VEC_BAG_ROWS, D_MODEL), F32),
                 f32(D_MODEL), f32(D_MODEL), f32(POOL_WIDTH), bf(2 * D_MODEL),
                 bf(D_MODEL), bf(D_MODEL), bf(D_MODEL), bf(D_MODEL), bf(D_MODEL), bf(D_MODEL), bf(D_MODEL), bf(p_dim))
    out_specs = (pl.BlockSpec((VEC_BAG_ROWS, D_MODEL), fixed),
                 tok(D_MODEL), tok(D_MODEL), tok(POOL_WIDTH), tok(2 * D_MODEL),
                 tok(D_MODEL), tok(D_MODEL), tok(D_MODEL), tok(D_MODEL), tok(D_MODEL), tok(D_MODEL), tok(D_MODEL),
                 tok(p_dim))
    return pl.pallas_call(
        body, name="merge_head", out_shape=out_shape, grid=(t // tb,), in_specs=in_specs, out_specs=out_specs,
        compiler_params=pltpu.CompilerParams(dimension_semantics=("arbitrary",),
                                             vmem_limit_bytes=VMEM_LIMIT_BYTES),
    )(x2d, ya, yb, z, z, p2d, tgt, w_pl, w_pp, w_out, w_pg, w_pe, g2, gf)


def kernel(x, p, norm_g, w_in, conv_w, conv_b, lru_w_a, lru_b_a, lru_w_x, lru_b_x, lru_lambda, pool_w, pool_scale, w_proj_lru, w_proj_pool, w_out, ple_norm_g, w_ple_gate, w_ple_proj, final_g, loss_target, m_norm_g, m_w_in, m_conv_w, m_conv_b, m_lru_w_a, m_lru_b_a, m_lru_w_x, m_lru_b_x, m_lru_lambda, m_pool_w, m_pool_scale, m_w_proj_lru, m_w_proj_pool, m_w_out, m_ple_norm_g, m_w_ple_gate, m_w_ple_proj, m_final_g, v_norm_g, v_w_in, v_conv_w, v_conv_b, v_lru_w_a, v_lru_b_a, v_lru_w_x, v_lru_b_x, v_lru_lambda, v_pool_w, v_pool_scale, v_w_proj_lru, v_w_proj_pool, v_w_out, v_ple_norm_g, v_w_ple_gate, v_w_ple_proj, v_final_g):
    bsz, seq, _ = x.shape
    t = bsz * seq
    tb_mm = min(1024, seq)
    tb_seq = min(256, seq // 2) if seq >= 512 else seq
    x2d = x.reshape(t, D_MODEL)
    p2d = p.reshape(t, p.shape[-1])
    tgt = loss_target.reshape(t, D_MODEL)

    rest = [(w_proj_lru[0], 0), (w_proj_pool[0], 1), (w_out[0], 0), (w_ple_gate[0], 0), (w_ple_proj[0], 1)]
    z, h_bf, w_in_f, conv_w_f, *narrow = _in_proj_gather(
        x2d, norm_g, w_in[0], [(conv_w[0], 1, False)], tb_mm,
        [w for w, _ in rest] + [lru_w_a[0], lru_w_x[0], pool_w[0]])
    wa_bf, wx_bf, pw_bf = narrow[len(rest):]
    branch_w = (conv_w_f, conv_b, wa_bf, lru_b_a.reshape(1, D_MODEL), wx_bf, lru_b_x.reshape(1, D_MODEL),
                lru_lambda, pw_bf, pool_scale)

    ya, yb, hl, w_pl_f, w_pp_f, w_out_f, w_pg_f, w_pe_f = _branches_fwd(
        z, branch_w, seq, tb_seq, [(w16, axis, True) for w16, (_, axis) in zip(narrow, rest)])
    (vec_bag, dx_res, dya, dyb, dzm, mg_bf, do_bf, hn_bf, dgp_bf, dpe_bf, da_bf, dbm_bf, p_bf) = _merge_head(
        x2d, ya, yb, z, p2d, tgt, w_pl_f, w_pp_f, w_out_f, w_pg_f, w_pe_f, ple_norm_g, final_g.reshape(1, D_MODEL),
        tb_seq)
    dz, vec_bag, mat_bag, g_out, g_out16, g_pp, g_pp16, g_pe, g_pe16 = _branches_bwd(
        z, hl, dya, dyb, dzm, branch_w, vec_bag, seq, tb_seq, [(mg_bf, do_bf), (yb, dbm_bf), (p_bf, dpe_bf)])

    tb_dw = min(1024, seq)
    def row_pieces(g32, g16):
        pieces = (8, g32.shape[0] // 8, g32.shape[1])
        return g32.reshape(pieces), False, g16.reshape(pieces)

    g_pl, g_pl16, g_pg, g_pg16 = _weight_grad([(ya, da_bf), (hn_bf, dgp_bf)], 1, tb_dw, "dw_proj")
    p_dim = p2d.shape[1]
    proj_parts = [row_pieces(g_pl[0], g_pl16[0]), (g_pp, True, g_pp16), row_pieces(g_out, g_out16),
                  row_pieces(g_pg[0], g_pg16[0]), (g_pe, True, g_pe16)]
    nb_dw = t // tb_dw
    g_in, g_in16, r_pl, r_pp, r_out, r_pg, r_pe, vec_mine, mat_mine = _weight_grad(
        [(h_bf, dz)], N_CHIPS, tb_dw, "dw_in",
        reduce=(proj_parts + [(vec_bag.reshape(8, VEC_BAG_ROWS // 8, D_MODEL), False, None),
                              (mat_bag.reshape(8, MAT_BAG_ROWS // 8, HEAD_DIM), False, None)],
                [BF16] * 5 + [F32] * 2,
                (0, nb_dw // 2, 2 * nb_dw - 1, 3 * nb_dw + nb_dw // 2, N_CHIPS * nb_dw - 1)))
    pieces = (8, D_MODEL // 2, IN_COLS // N_CHIPS)
    nb_seq = t // tb_seq
    dx, g_g1, r_in, vec_sum, mat_sum, g_cw = _in_proj_bwd(
        dz, w_in_f, x2d, dx_res, norm_g, tb_seq,
        reduce=([(g_in.reshape(pieces), False, g_in16.reshape(pieces))], BF16,
                (0, nb_seq // 8, nb_seq // 2, nb_seq - 1)),
        shards=[(vec_mine.reshape(VEC_BAG_ROWS // N_CHIPS, D_MODEL), 0, True),
                (mat_mine.reshape(MAT_BAG_ROWS // N_CHIPS, HEAD_DIM), 0, True)],
        take=(_bag_rows("conv_w"), D_MODEL // N_CHIPS))

    big = [(w_in, r_in, m_w_in, v_w_in, 4), (w_proj_lru, r_pl, m_w_proj_lru, v_w_proj_lru, 1),
           (w_proj_pool, r_pp, m_w_proj_pool, v_w_proj_pool, 1), (w_out, r_out, m_w_out, v_w_out, 1),
           (w_ple_gate, r_pg, m_w_ple_gate, v_w_ple_gate, 1), (w_ple_proj, r_pe, m_w_ple_proj, v_w_ple_proj, 1)]
    u_in, u_pl, u_pp, u_out, u_pg, u_pe = [tuple(a[None] for a in u) for u in _adamw_group(
        [(w[0], g.reshape(w.shape[1:]), m[0], v[0], cuts) for w, g, m, v, cuts in big], "adamw_sharded")]

    small = [("norm_g", norm_g, m_norm_g, v_norm_g), ("conv_b", conv_b, m_conv_b, v_conv_b),
             ("lru_w_a", lru_w_a, m_lru_w_a, v_lru_w_a), ("lru_b_a", lru_b_a, m_lru_b_a, v_lru_b_a),
             ("lru_w_x", lru_w_x, m_lru_w_x, v_lru_w_x), ("lru_b_x", lru_b_x, m_lru_b_x, v_lru_b_x),
             ("lru_lambda", lru_lambda, m_lru_lambda, v_lru_lambda), ("pool_w", pool_w, m_pool_w, v_pool_w),
             ("pool_scale", pool_scale, m_pool_scale, v_pool_scale),
             ("ple_norm_g", ple_norm_g, m_ple_norm_g, v_ple_norm_g), ("final_g", final_g, m_final_g, v_final_g)]

    def view(a):
        return a.reshape(-1, a.shape[-1]) if a.ndim != 3 else a[0]

    flat = _adamw_replicated(vec_sum, mat_sum, g_g1, [(name,) + tuple(view(a) for a in arrs) for name, *arrs in small],
                             (conv_w[0], m_conv_w[0], v_conv_w[0], g_cw))
    u_small = {name: tuple(flat[4 * k + pick].reshape(arrs[0].shape) for pick in range(4))
               for k, (name, *arrs) in enumerate(small)}
    u_cw = tuple(a[None] for a in flat[4 * len(small):4 * len(small) + 4])

    loss = flat[-1].reshape(())
    grad_x = dx.reshape(bsz, seq, D_MODEL)

    def ordered(pick):
        s = {name: u[pick] for name, u in u_small.items()}
        return [s["norm_g"], u_in[pick], u_cw[pick], s["conv_b"], s["lru_w_a"], s["lru_b_a"], s["lru_w_x"], s["lru_b_x"],
                s["lru_lambda"], s["pool_w"], s["pool_scale"], u_pl[pick], u_pp[pick], u_out[pick], s["ple_norm_g"],
                u_pg[pick], u_pe[pick], s["final_g"]]

    return (loss, grad_x, *ordered(0), *ordered(1), *ordered(2), *ordered(3))
```
